```python
import math
import jax, jax.numpy as jnp
from jax import lax
import numpy as np

D_MODEL = 1024
BATCH = 16
SEQ = 2048
DEPTH = 1

N_MEM = 256
FOX_WIDTH = D_MODEL // 2
HEAD_DIM = 64
N_FOX_HEADS = FOX_WIDTH // HEAD_DIM
Q_BLOCK = 128
S5_WIDTH = D_MODEL - FOX_WIDTH
S5_GROUP_CH = 16
S5_GROUPS = S5_WIDTH // S5_GROUP_CH
S5_STATE = 64
N_X_HEADS = 4
X_HEAD_DIM = D_MODEL // N_X_HEADS
D_FF = 128 * ((8 * D_MODEL // 3 + 127) // 128)
CONV_W = 3
IN_COLS = 3 * FOX_WIDTH + N_FOX_HEADS + S5_WIDTH
EPS = 1e-6

kernel_name = "fox_s5_parallel_hybrid_layer"


def _rms_norm(x, g):
    xf = x.astype(jnp.float32)
    y = xf * lax.rsqrt(jnp.mean(xf * xf, axis=-1, keepdims=True) + EPS)
    return (y * g.astype(jnp.float32)).astype(x.dtype)


def _fox_attention(q, k, v, log_f):
    L = q.shape[2]
    c = jnp.cumsum(log_f, axis=-1)
    scale = HEAD_DIM ** -0.5
    outs = []
    for blk in range(L // Q_BLOCK):
        qs = blk * Q_BLOCK
        ke = qs + Q_BLOCK
        s = jnp.einsum('bhqd,bhkd->bhqk', q[:, :, qs:ke], k[:, :, :ke]).astype(jnp.float32) * scale
        s = s + c[:, :, qs:ke, None] - c[:, :, None, :ke]
        mask = jnp.arange(ke)[None, :] <= (qs + jnp.arange(Q_BLOCK))[:, None]
        s = jnp.where(mask, s, -jnp.inf)
        p = jax.nn.softmax(s, axis=-1).astype(v.dtype)
        outs.append(jnp.einsum('bhqk,bhkd->bhqd', p, v[:, :, :ke]))
    return jnp.concatenate(outs, axis=2)


def _cdiag_combine(left, right):
    a1r, a1i, b1r, b1i = left
    a2r, a2i, b2r, b2i = right
    ar = a2r * a1r - a2i * a1i
    ai = a2r * a1i + a2i * a1r
    br = a2r * b1r - a2i * b1i + b2r
    bi = a2r * b1i + a2i * b1r + b2i
    return (ar, ai, br, bi)


def _s5(u, a_re, a_im, log_dt, b_re, b_im, c_re, c_im, d):
    Bsz, L, _ = u.shape
    uf = u.astype(jnp.float32).reshape(Bsz, L, S5_GROUPS, S5_GROUP_CH)
    ar = a_re.astype(jnp.float32)
    ai = a_im.astype(jnp.float32)
    dt = jnp.exp(log_dt.astype(jnp.float32))[:, None]
    mag = jnp.exp(ar * dt)
    lb_r = mag * jnp.cos(ai * dt)
    lb_i = mag * jnp.sin(ai * dt)
    den = ar * ar + ai * ai
    nr = lb_r - 1.0
    coef_r = (nr * ar + lb_i * ai) / den
    coef_i = (lb_i * ar - nr * ai) / den
    br = b_re.astype(jnp.float32)
    bi = b_im.astype(jnp.float32)
    bb_r = coef_r[:, :, None] * br - coef_i[:, :, None] * bi
    bb_i = coef_r[:, :, None] * bi + coef_i[:, :, None] * br
    bu_r = jnp.einsum('blgc,gpc->lbgp', uf, bb_r)
    bu_i = jnp.einsum('blgc,gpc->lbgp', uf, bb_i)
    a_r = jnp.broadcast_to(lb_r[None, None], (L, 1, S5_GROUPS, S5_STATE))
    a_i = jnp.broadcast_to(lb_i[None, None], (L, 1, S5_GROUPS, S5_STATE))
    _, _, xr, xi = lax.associative_scan(_cdiag_combine, (a_r, a_i, bu_r, bu_i), axis=0)
    y = (jnp.einsum('lbgp,gcp->blgc', xr, c_re.astype(jnp.float32))
         - jnp.einsum('lbgp,gcp->blgc', xi, c_im.astype(jnp.float32))
         + d.astype(jnp.float32) * uf)
    return y.reshape(Bsz, L, S5_WIDTH)


def _causal_dwconv(a, w, b):
    L = a.shape[1]
    ap = jnp.pad(a, ((0, 0), (CONV_W - 1, 0), (0, 0)))
    out = b
    for i in range(CONV_W):
        out = out + w[i] * ap[:, i:i + L]
    return out


def _fwd_setup_inputs(seed: int = 0) -> dict:
    key = jax.random.key(seed)
    ks = jax.random.split(key, 40)
    f32 = jnp.float32

    def nrm(k, shape, scale):
        return jax.random.normal(k, shape, f32) * scale

    def gain(k, shape):
        return 1.0 + 0.02 * jax.random.normal(k, shape, f32)

    Ld = DEPTH
    n_idx = jnp.arange(S5_STATE, dtype=f32)
    inp = {
        "x": jax.random.normal(ks[0], (BATCH, SEQ, D_MODEL), f32),
        "mem": jax.random.normal(ks[1], (BATCH, N_MEM, D_MODEL), f32),
        "norm_mix": gain(ks[2], (Ld, D_MODEL)),
        "w_in": nrm(ks[3], (Ld, D_MODEL, IN_COLS), D_MODEL ** -0.5),
        "fox_q_norm": gain(ks[4], (Ld, HEAD_DIM)),
        "fox_k_norm": gain(ks[5], (Ld, HEAD_DIM)),
        "fox_f_bias": 3.0 + 0.5 * jax.random.normal(ks[6], (Ld, N_FOX_HEADS), f32),
        "s5_a_re": -0.5 + 0.01 * jax.random.normal(ks[7], (Ld, S5_GROUPS, S5_STATE), f32),
        "s5_a_im": math.pi * n_idx[None, None, :] + 0.01 * jax.random.normal(ks[8], (Ld, S5_GROUPS, S5_STATE), f32),
        "s5_log_dt": jax.random.uniform(ks[9], (Ld, S5_GROUPS), f32, math.log(1e-3), math.log(1e-1)),
        "s5_b_re": nrm(ks[10], (Ld, S5_GROUPS, S5_STATE, S5_GROUP_CH), (2 * S5_GROUP_CH) ** -0.5),
        "s5_b_im": nrm(ks[11], (Ld, S5_GROUPS, S5_STATE, S5_GROUP_CH), (2 * S5_GROUP_CH) ** -0.5),
        "s5_c_re": nrm(ks[12], (Ld, S5_GROUPS, S5_GROUP_CH, S5_STATE), (2 * S5_STATE) ** -0.5),
        "s5_c_im": nrm(ks[13], (Ld, S5_GROUPS, S5_GROUP_CH, S5_STATE), (2 * S5_STATE) ** -0.5),
        "s5_d": nrm(ks[14], (Ld, S5_GROUPS, S5_GROUP_CH), 1.0),
        "s5_w_glu": nrm(ks[15], (Ld, S5_WIDTH, S5_WIDTH), S5_WIDTH ** -0.5),
        "s5_b_glu": nrm(ks[16], (Ld, S5_WIDTH), 0.02),
        "out_norm_fox": gain(ks[17], (Ld, FOX_WIDTH)),
        "out_norm_s5": gain(ks[18], (Ld, S5_WIDTH)),
        "w_out": nrm(ks[19], (Ld, D_MODEL, D_MODEL), D_MODEL ** -0.5),
        "norm_cross": gain(ks[20], (Ld, D_MODEL)),
        "norm_mem": gain(ks[21], (Ld, D_MODEL)),
        "w_xq": nrm(ks[22], (Ld, D_MODEL, D_MODEL), D_MODEL ** -0.5),
        "w_xkv": nrm(ks[23], (Ld, D_MODEL, 2 * D_MODEL), D_MODEL ** -0.5),
        "xq_norm": gain(ks[24], (Ld, X_HEAD_DIM)),
        "xk_norm": gain(ks[25], (Ld, X_HEAD_DIM)),
        "w_xo": nrm(ks[26], (Ld, D_MODEL, D_MODEL), D_MODEL ** -0.5),
        "norm_ffn": gain(ks[27], (Ld, D_MODEL)),
        "w_ffn_up": nrm(ks[28], (Ld, D_MODEL, 2 * D_FF), D_MODEL ** -0.5),
        "ffn_conv_w": nrm(ks[29], (Ld, CONV_W, D_FF), CONV_W ** -0.5),
        "ffn_conv_b": nrm(ks[30], (Ld, D_FF), 0.02),
        "w_ffn_down": nrm(ks[31], (Ld, D_FF, D_MODEL), D_FF ** -0.5),
    }
    return inp


def _fwd_reference(x, mem, norm_mix, w_in, fox_q_norm, fox_k_norm, fox_f_bias,
              s5_a_re, s5_a_im, s5_log_dt, s5_b_re, s5_b_im, s5_c_re, s5_c_im,
              s5_d, s5_w_glu, s5_b_glu, out_norm_fox, out_norm_s5, w_out,
              norm_cross, norm_mem, w_xq, w_xkv, xq_norm, xk_norm, w_xo,
              norm_ffn, w_ffn_up, ffn_conv_w, ffn_conv_b, w_ffn_down):
    Bsz, L, _ = x.shape
    h = x
    for l in range(DEPTH):
        hn = _rms_norm(h, norm_mix[l])
        proj = hn @ w_in[l]
        q, k, v, f_logit, u = jnp.split(
            proj, [FOX_WIDTH, 2 * FOX_WIDTH, 3 * FOX_WIDTH, 3 * FOX_WIDTH + N_FOX_HEADS], axis=-1)
        q = _rms_norm(q.reshape(Bsz, L, N_FOX_HEADS, HEAD_DIM), fox_q_norm[l]).transpose(0, 2, 1, 3)
        k = _rms_norm(k.reshape(Bsz, L, N_FOX_HEADS, HEAD_DIM), fox_k_norm[l]).transpose(0, 2, 1, 3)
        v = v.reshape(Bsz, L, N_FOX_HEADS, HEAD_DIM).transpose(0, 2, 1, 3)
        log_f = jax.nn.log_sigmoid(f_logit.astype(jnp.float32) + fox_f_bias[l].astype(jnp.float32))
        fox = _fox_attention(q, k, v, log_f.transpose(0, 2, 1))
        fox = fox.transpose(0, 2, 1, 3).reshape(Bsz, L, FOX_WIDTH)

        y = _s5(u, s5_a_re[l], s5_a_im[l], s5_log_dt[l], s5_b_re[l], s5_b_im[l],
                s5_c_re[l], s5_c_im[l], s5_d[l])
        y = jax.nn.gelu(y)
        y = y * jax.nn.sigmoid(y @ s5_w_glu[l].astype(jnp.float32) + s5_b_glu[l].astype(jnp.float32))
        y = y.astype(h.dtype)

        mixed = jnp.concatenate([_rms_norm(fox, out_norm_fox[l]), _rms_norm(y, out_norm_s5[l])], axis=-1)
        h = h + mixed @ w_out[l]

        hn = _rms_norm(h, norm_cross[l])
        mn = _rms_norm(mem, norm_mem[l])
        xq = _rms_norm((hn @ w_xq[l]).reshape(Bsz, L, N_X_HEADS, X_HEAD_DIM), xq_norm[l])
        xk, xv = jnp.split(mn @ w_xkv[l], 2, axis=-1)
        xk = _rms_norm(xk.reshape(Bsz, N_MEM, N_X_HEADS, X_HEAD_DIM), xk_norm[l])
        xv = xv.reshape(Bsz, N_MEM, N_X_HEADS, X_HEAD_DIM)
        s = jnp.einsum('bqhd,bmhd->bhqm', xq, xk).astype(jnp.float32) * (X_HEAD_DIM ** -0.5)
        p = jax.nn.softmax(s, axis=-1).astype(xv.dtype)
        xo = jnp.einsum('bhqm,bmhd->bqhd', p, xv).reshape(Bsz, L, D_MODEL)
        h = h + xo @ w_xo[l]

        hn = _rms_norm(h, norm_ffn[l])
        gate, up = jnp.split(hn @ w_ffn_up[l], 2, axis=-1)
        gate = _causal_dwconv(gate, ffn_conv_w[l], ffn_conv_b[l])
        h = h + (jax.nn.silu(gate) * up) @ w_ffn_down[l]
    return h


import jax as _jax
import jax.numpy as _jnp

TWIN_FORMAT = 'train_step'
FWD_PARAMS = ['x', 'mem', 'norm_mix', 'w_in', 'fox_q_norm', 'fox_k_norm', 'fox_f_bias', 's5_a_re', 's5_a_im', 's5_log_dt', 's5_b_re', 's5_b_im', 's5_c_re', 's5_c_im', 's5_d', 's5_w_glu', 's5_b_glu', 'out_norm_fox', 'out_norm_s5', 'w_out', 'norm_cross', 'norm_mem', 'w_xq', 'w_xkv', 'xq_norm', 'xk_norm', 'w_xo', 'norm_ffn', 'w_ffn_up', 'ffn_conv_w', 'ffn_conv_b', 'w_ffn_down']
TWIN_WEIGHTS = ['norm_mix', 'w_in', 'fox_q_norm', 'fox_k_norm', 'fox_f_bias', 's5_a_re', 's5_a_im', 's5_log_dt', 's5_b_re', 's5_b_im', 's5_c_re', 's5_c_im', 's5_d', 's5_w_glu', 's5_b_glu', 'out_norm_fox', 'out_norm_s5', 'w_out', 'norm_cross', 'norm_mem', 'w_xq', 'w_xkv', 'xq_norm', 'xk_norm', 'w_xo', 'norm_ffn', 'w_ffn_up', 'ffn_conv_w', 'ffn_conv_b', 'w_ffn_down']
TWIN_DIFF_INPUT = 'x'
TWIN_INPUTS = ['x', 'mem', 'norm_mix', 'w_in', 'fox_q_norm', 'fox_k_norm', 'fox_f_bias', 's5_a_re', 's5_a_im', 's5_log_dt', 's5_b_re', 's5_b_im', 's5_c_re', 's5_c_im', 's5_d', 's5_w_glu', 's5_b_glu', 'out_norm_fox', 'out_norm_s5', 'w_out', 'norm_cross', 'norm_mem', 'w_xq', 'w_xkv', 'xq_norm', 'xk_norm', 'w_xo', 'norm_ffn', 'w_ffn_up', 'ffn_conv_w', 'ffn_conv_b', 'w_ffn_down', 'loss_target', 'm_norm_mix', 'm_w_in', 'm_fox_q_norm', 'm_fox_k_norm', 'm_fox_f_bias', 'm_s5_a_re', 'm_s5_a_im', 'm_s5_log_dt', 'm_s5_b_re', 'm_s5_b_im', 'm_s5_c_re', 'm_s5_c_im', 'm_s5_d', 'm_s5_w_glu', 'm_s5_b_glu', 'm_out_norm_fox', 'm_out_norm_s5', 'm_w_out', 'm_norm_cross', 'm_norm_mem', 'm_w_xq', 'm_w_xkv', 'm_xq_norm', 'm_xk_norm', 'm_w_xo', 'm_norm_ffn', 'm_w_ffn_up', 'm_ffn_conv_w', 'm_ffn_conv_b', 'm_w_ffn_down', 'v_norm_mix', 'v_w_in', 'v_fox_q_norm', 'v_fox_k_norm', 'v_fox_f_bias', 'v_s5_a_re', 'v_s5_a_im', 'v_s5_log_dt', 'v_s5_b_re', 'v_s5_b_im', 'v_s5_c_re', 'v_s5_c_im', 'v_s5_d', 'v_s5_w_glu', 'v_s5_b_glu', 'v_out_norm_fox', 'v_out_norm_s5', 'v_w_out', 'v_norm_cross', 'v_norm_mem', 'v_w_xq', 'v_w_xkv', 'v_xq_norm', 'v_xk_norm', 'v_w_xo', 'v_norm_ffn', 'v_w_ffn_up', 'v_ffn_conv_w', 'v_ffn_conv_b', 'v_w_ffn_down']
TWIN_OUTPUTS = ['loss', 'grad_x', 'grad_norm_mix', 'grad_w_in', 'grad_fox_q_norm', 'grad_fox_k_norm', 'grad_fox_f_bias', 'grad_s5_a_re', 'grad_s5_a_im', 'grad_s5_log_dt', 'grad_s5_b_re', 'grad_s5_b_im', 'grad_s5_c_re', 'grad_s5_c_im', 'grad_s5_d', 'grad_s5_w_glu', 'grad_s5_b_glu', 'grad_out_norm_fox', 'grad_out_norm_s5', 'grad_w_out', 'grad_norm_cross', 'grad_norm_mem', 'grad_w_xq', 'grad_w_xkv', 'grad_xq_norm', 'grad_xk_norm', 'grad_w_xo', 'grad_norm_ffn', 'grad_w_ffn_up', 'grad_ffn_conv_w', 'grad_ffn_conv_b', 'grad_w_ffn_down', 'delta_norm_mix', 'delta_w_in', 'delta_fox_q_norm', 'delta_fox_k_norm', 'delta_fox_f_bias', 'delta_s5_a_re', 'delta_s5_a_im', 'delta_s5_log_dt', 'delta_s5_b_re', 'delta_s5_b_im', 'delta_s5_c_re', 'delta_s5_c_im', 'delta_s5_d', 'delta_s5_w_glu', 'delta_s5_b_glu', 'delta_out_norm_fox', 'delta_out_norm_s5', 'delta_w_out', 'delta_norm_cross', 'delta_norm_mem', 'delta_w_xq', 'delta_w_xkv', 'delta_xq_norm', 'delta_xk_norm', 'delta_w_xo', 'delta_norm_ffn', 'delta_w_ffn_up', 'delta_ffn_conv_w', 'delta_ffn_conv_b', 'delta_w_ffn_down', 'new_m_norm_mix', 'new_m_w_in', 'new_m_fox_q_norm', 'new_m_fox_k_norm', 'new_m_fox_f_bias', 'new_m_s5_a_re', 'new_m_s5_a_im', 'new_m_s5_log_dt', 'new_m_s5_b_re', 'new_m_s5_b_im', 'new_m_s5_c_re', 'new_m_s5_c_im', 'new_m_s5_d', 'new_m_s5_w_glu', 'new_m_s5_b_glu', 'new_m_out_norm_fox', 'new_m_out_norm_s5', 'new_m_w_out', 'new_m_norm_cross', 'new_m_norm_mem', 'new_m_w_xq', 'new_m_w_xkv', 'new_m_xq_norm', 'new_m_xk_norm', 'new_m_w_xo', 'new_m_norm_ffn', 'new_m_w_ffn_up', 'new_m_ffn_conv_w', 'new_m_ffn_conv_b', 'new_m_w_ffn_down', 'new_v_norm_mix', 'new_v_w_in', 'new_v_fox_q_norm', 'new_v_fox_k_norm', 'new_v_fox_f_bias', 'new_v_s5_a_re', 'new_v_s5_a_im', 'new_v_s5_log_dt', 'new_v_s5_b_re', 'new_v_s5_b_im', 'new_v_s5_c_re', 'new_v_s5_c_im', 'new_v_s5_d', 'new_v_s5_w_glu', 'new_v_s5_b_glu', 'new_v_out_norm_fox', 'new_v_out_norm_s5', 'new_v_w_out', 'new_v_norm_cross', 'new_v_norm_mem', 'new_v_w_xq', 'new_v_w_xkv', 'new_v_xq_norm', 'new_v_xk_norm', 'new_v_w_xo', 'new_v_norm_ffn', 'new_v_w_ffn_up', 'new_v_ffn_conv_w', 'new_v_ffn_conv_b', 'new_v_w_ffn_down']
TWIN_LEAF_KINDS = {'loss': 'loss', 'grad_x': 'grad_x', 'grad_norm_mix': 'grad_w', 'grad_w_in': 'grad_w', 'grad_fox_q_norm': 'grad_w', 'grad_fox_k_norm': 'grad_w', 'grad_fox_f_bias': 'grad_w', 'grad_s5_a_re': 'grad_w', 'grad_s5_a_im': 'grad_w', 'grad_s5_log_dt': 'grad_w', 'grad_s5_b_re': 'grad_w', 'grad_s5_b_im': 'grad_w', 'grad_s5_c_re': 'grad_w', 'grad_s5_c_im': 'grad_w', 'grad_s5_d': 'grad_w', 'grad_s5_w_glu': 'grad_w', 'grad_s5_b_glu': 'grad_w', 'grad_out_norm_fox': 'grad_w', 'grad_out_norm_s5': 'grad_w', 'grad_w_out': 'grad_w', 'grad_norm_cross': 'grad_w', 'grad_norm_mem': 'grad_w', 'grad_w_xq': 'grad_w', 'grad_w_xkv': 'grad_w', 'grad_xq_norm': 'grad_w', 'grad_xk_norm': 'grad_w', 'grad_w_xo': 'grad_w', 'grad_norm_ffn': 'grad_w', 'grad_w_ffn_up': 'grad_w', 'grad_ffn_conv_w': 'grad_w', 'grad_ffn_conv_b': 'grad_w', 'grad_w_ffn_down': 'grad_w', 'delta_norm_mix': 'delta_w', 'delta_w_in': 'delta_w', 'delta_fox_q_norm': 'delta_w', 'delta_fox_k_norm': 'delta_w', 'delta_fox_f_bias': 'delta_w', 'delta_s5_a_re': 'delta_w', 'delta_s5_a_im': 'delta_w', 'delta_s5_log_dt': 'delta_w', 'delta_s5_b_re': 'delta_w', 'delta_s5_b_im': 'delta_w', 'delta_s5_c_re': 'delta_w', 'delta_s5_c_im': 'delta_w', 'delta_s5_d': 'delta_w', 'delta_s5_w_glu': 'delta_w', 'delta_s5_b_glu': 'delta_w', 'delta_out_norm_fox': 'delta_w', 'delta_out_norm_s5': 'delta_w', 'delta_w_out': 'delta_w', 'delta_norm_cross': 'delta_w', 'delta_norm_mem': 'delta_w', 'delta_w_xq': 'delta_w', 'delta_w_xkv': 'delta_w', 'delta_xq_norm': 'delta_w', 'delta_xk_norm': 'delta_w', 'delta_w_xo': 'delta_w', 'delta_norm_ffn': 'delta_w', 'delta_w_ffn_up': 'delta_w', 'delta_ffn_conv_w': 'delta_w', 'delta_ffn_conv_b': 'delta_w', 'delta_w_ffn_down': 'delta_w', 'new_m_norm_mix': 'new_m', 'new_m_w_in': 'new_m', 'new_m_fox_q_norm': 'new_m', 'new_m_fox_k_norm': 'new_m', 'new_m_fox_f_bias': 'new_m', 'new_m_s5_a_re': 'new_m', 'new_m_s5_a_im': 'new_m', 'new_m_s5_log_dt': 'new_m', 'new_m_s5_b_re': 'new_m', 'new_m_s5_b_im': 'new_m', 'new_m_s5_c_re': 'new_m', 'new_m_s5_c_im': 'new_m', 'new_m_s5_d': 'new_m', 'new_m_s5_w_glu': 'new_m', 'new_m_s5_b_glu': 'new_m', 'new_m_out_norm_fox': 'new_m', 'new_m_out_norm_s5': 'new_m', 'new_m_w_out': 'new_m', 'new_m_norm_cross': 'new_m', 'new_m_norm_mem': 'new_m', 'new_m_w_xq': 'new_m', 'new_m_w_xkv': 'new_m', 'new_m_xq_norm': 'new_m', 'new_m_xk_norm': 'new_m', 'new_m_w_xo': 'new_m', 'new_m_norm_ffn': 'new_m', 'new_m_w_ffn_up': 'new_m', 'new_m_ffn_conv_w': 'new_m', 'new_m_ffn_conv_b': 'new_m', 'new_m_w_ffn_down': 'new_m', 'new_v_norm_mix': 'new_v', 'new_v_w_in': 'new_v', 'new_v_fox_q_norm': 'new_v', 'new_v_fox_k_norm': 'new_v', 'new_v_fox_f_bias': 'new_v', 'new_v_s5_a_re': 'new_v', 'new_v_s5_a_im': 'new_v', 'new_v_s5_log_dt': 'new_v', 'new_v_s5_b_re': 'new_v', 'new_v_s5_b_im': 'new_v', 'new_v_s5_c_re': 'new_v', 'new_v_s5_c_im': 'new_v', 'new_v_s5_d': 'new_v', 'new_v_s5_w_glu': 'new_v', 'new_v_s5_b_glu': 'new_v', 'new_v_out_norm_fox': 'new_v', 'new_v_out_norm_s5': 'new_v', 'new_v_w_out': 'new_v', 'new_v_norm_cross': 'new_v', 'new_v_norm_mem': 'new_v', 'new_v_w_xq': 'new_v', 'new_v_w_xkv': 'new_v', 'new_v_xq_norm': 'new_v', 'new_v_xk_norm': 'new_v', 'new_v_w_xo': 'new_v', 'new_v_norm_ffn': 'new_v', 'new_v_w_ffn_up': 'new_v', 'new_v_ffn_conv_w': 'new_v', 'new_v_ffn_conv_b': 'new_v', 'new_v_w_ffn_down': 'new_v'}


def _forward(args):
    return _fwd_reference(*[args[k] for k in FWD_PARAMS])


def _output_shape():
    out = _jax.eval_shape(lambda: _forward(_fwd_setup_inputs(0)))
    return out.shape, out.dtype

N_MICROBATCH = 1
ADAM_LR = 0.001
ADAM_B1 = 0.9
ADAM_B2 = 0.999
ADAM_EPS = 1e-08
ADAM_WD = 0.01
ADAM_STEP = 10
PER_EXAMPLE_BATCH_AXIS = {'x': 0, 'mem': 0, 'loss_target': 0}
SHARED_INPUTS = []
_WEIGHT_DTYPES = {'norm_mix': _jnp.float32, 'w_in': _jnp.float32, 'fox_q_norm': _jnp.float32, 'fox_k_norm': _jnp.float32, 'fox_f_bias': _jnp.float32, 's5_a_re': _jnp.float32, 's5_a_im': _jnp.float32, 's5_log_dt': _jnp.float32, 's5_b_re': _jnp.float32, 's5_b_im': _jnp.float32, 's5_c_re': _jnp.float32, 's5_c_im': _jnp.float32, 's5_d': _jnp.float32, 's5_w_glu': _jnp.float32, 's5_b_glu': _jnp.float32, 'out_norm_fox': _jnp.float32, 'out_norm_s5': _jnp.float32, 'w_out': _jnp.float32, 'norm_cross': _jnp.float32, 'norm_mem': _jnp.float32, 'w_xq': _jnp.float32, 'w_xkv': _jnp.float32, 'xq_norm': _jnp.float32, 'xk_norm': _jnp.float32, 'w_xo': _jnp.float32, 'norm_ffn': _jnp.float32, 'w_ffn_up': _jnp.float32, 'ffn_conv_w': _jnp.float32, 'ffn_conv_b': _jnp.float32, 'w_ffn_down': _jnp.float32}
MOMENT_SCALE = {'norm_mix': 7.682830e-01, 'w_in': 5.289554e-01, 'fox_q_norm': 1.283823e+00, 'fox_k_norm': 1.309370e+00, 'fox_f_bias': 4.947994e+00, 's5_a_re': 2.184387e-02, 's5_a_im': 2.931930e-02, 's5_log_dt': 2.253058e+01, 's5_b_re': 1.914500e-02, 's5_b_im': 1.784169e-02, 's5_c_re': 3.642440e-02, 's5_c_im': 3.707685e-02, 's5_d': 5.958979e+00, 's5_w_glu': 8.155805e-01, 's5_b_glu': 2.656614e+00, 'out_norm_fox': 3.362669e+01, 'out_norm_s5': 5.811641e+01, 'w_out': 4.042397e+00, 'norm_cross': 1.223039e-01, 'norm_mem': 5.635089e-01, 'w_xq': 1.202333e-01, 'w_xkv': 2.690716e-01, 'xq_norm': 1.338237e+00, 'xk_norm': 1.341046e+00, 'w_xo': 3.615489e-01, 'norm_ffn': 2.594417e+01, 'w_ffn_up': 8.770248e-01, 'ffn_conv_w': 3.267463e+00, 'ffn_conv_b': 3.794989e+00, 'w_ffn_down': 6.697693e-01}


def _to_microbatches(a, axis):
    t = _jnp.moveaxis(a, axis, 0)
    t = t.reshape((N_MICROBATCH, t.shape[0] // N_MICROBATCH) + t.shape[1:])
    return _jnp.moveaxis(t, 1, axis + 1)


def setup_inputs(seed: int = 0) -> dict:
    inp = _fwd_setup_inputs(seed)
    key = _jax.random.fold_in(_jax.random.key(seed), 7919)
    shape, _ = _output_shape()
    out = dict(inp)
    out["loss_target"] = _jax.random.normal(_jax.random.fold_in(key, 0), shape, _jnp.float32)
    for i, name in enumerate(TWIN_WEIGHTS):
        w = inp[name].astype(_jnp.float32)
        if MOMENT_SCALE is None:
            s = _jnp.sqrt(_jnp.mean(_jnp.square(w)) + 1e-30)
        else:
            s = MOMENT_SCALE[name]
        km, kv = _jax.random.split(_jax.random.fold_in(key, i + 1))
        out[name] = w
        out["m_" + name] = s * _jax.random.normal(km, w.shape, _jnp.float32)
        out["v_" + name] = (s * s) * _jax.random.uniform(kv, w.shape, _jnp.float32, 0.5, 1.5)
    if N_MICROBATCH > 1:
        for name, axis in PER_EXAMPLE_BATCH_AXIS.items():
            out[name] = _to_microbatches(out[name], axis)
    return {'x': out['x'], 'mem': out['mem'], 'norm_mix': out['norm_mix'], 'w_in': out['w_in'], 'fox_q_norm': out['fox_q_norm'], 'fox_k_norm': out['fox_k_norm'], 'fox_f_bias': out['fox_f_bias'], 's5_a_re': out['s5_a_re'], 's5_a_im': out['s5_a_im'], 's5_log_dt': out['s5_log_dt'], 's5_b_re': out['s5_b_re'], 's5_b_im': out['s5_b_im'], 's5_c_re': out['s5_c_re'], 's5_c_im': out['s5_c_im'], 's5_d': out['s5_d'], 's5_w_glu': out['s5_w_glu'], 's5_b_glu': out['s5_b_glu'], 'out_norm_fox': out['out_norm_fox'], 'out_norm_s5': out['out_norm_s5'], 'w_out': out['w_out'], 'norm_cross': out['norm_cross'], 'norm_mem': out['norm_mem'], 'w_xq': out['w_xq'], 'w_xkv': out['w_xkv'], 'xq_norm': out['xq_norm'], 'xk_norm': out['xk_norm'], 'w_xo': out['w_xo'], 'norm_ffn': out['norm_ffn'], 'w_ffn_up': out['w_ffn_up'], 'ffn_conv_w': out['ffn_conv_w'], 'ffn_conv_b': out['ffn_conv_b'], 'w_ffn_down': out['w_ffn_down'], 'loss_target': out['loss_target'], 'm_norm_mix': out['m_norm_mix'], 'm_w_in': out['m_w_in'], 'm_fox_q_norm': out['m_fox_q_norm'], 'm_fox_k_norm': out['m_fox_k_norm'], 'm_fox_f_bias': out['m_fox_f_bias'], 'm_s5_a_re': out['m_s5_a_re'], 'm_s5_a_im': out['m_s5_a_im'], 'm_s5_log_dt': out['m_s5_log_dt'], 'm_s5_b_re': out['m_s5_b_re'], 'm_s5_b_im': out['m_s5_b_im'], 'm_s5_c_re': out['m_s5_c_re'], 'm_s5_c_im': out['m_s5_c_im'], 'm_s5_d': out['m_s5_d'], 'm_s5_w_glu': out['m_s5_w_glu'], 'm_s5_b_glu': out['m_s5_b_glu'], 'm_out_norm_fox': out['m_out_norm_fox'], 'm_out_norm_s5': out['m_out_norm_s5'], 'm_w_out': out['m_w_out'], 'm_norm_cross': out['m_norm_cross'], 'm_norm_mem': out['m_norm_mem'], 'm_w_xq': out['m_w_xq'], 'm_w_xkv': out['m_w_xkv'], 'm_xq_norm': out['m_xq_norm'], 'm_xk_norm': out['m_xk_norm'], 'm_w_xo': out['m_w_xo'], 'm_norm_ffn': out['m_norm_ffn'], 'm_w_ffn_up': out['m_w_ffn_up'], 'm_ffn_conv_w': out['m_ffn_conv_w'], 'm_ffn_conv_b': out['m_ffn_conv_b'], 'm_w_ffn_down': out['m_w_ffn_down'], 'v_norm_mix': out['v_norm_mix'], 'v_w_in': out['v_w_in'], 'v_fox_q_norm': out['v_fox_q_norm'], 'v_fox_k_norm': out['v_fox_k_norm'], 'v_fox_f_bias': out['v_fox_f_bias'], 'v_s5_a_re': out['v_s5_a_re'], 'v_s5_a_im': out['v_s5_a_im'], 'v_s5_log_dt': out['v_s5_log_dt'], 'v_s5_b_re': out['v_s5_b_re'], 'v_s5_b_im': out['v_s5_b_im'], 'v_s5_c_re': out['v_s5_c_re'], 'v_s5_c_im': out['v_s5_c_im'], 'v_s5_d': out['v_s5_d'], 'v_s5_w_glu': out['v_s5_w_glu'], 'v_s5_b_glu': out['v_s5_b_glu'], 'v_out_norm_fox': out['v_out_norm_fox'], 'v_out_norm_s5': out['v_out_norm_s5'], 'v_w_out': out['v_w_out'], 'v_norm_cross': out['v_norm_cross'], 'v_norm_mem': out['v_norm_mem'], 'v_w_xq': out['v_w_xq'], 'v_w_xkv': out['v_w_xkv'], 'v_xq_norm': out['v_xq_norm'], 'v_xk_norm': out['v_xk_norm'], 'v_w_xo': out['v_w_xo'], 'v_norm_ffn': out['v_norm_ffn'], 'v_w_ffn_up': out['v_w_ffn_up'], 'v_ffn_conv_w': out['v_ffn_conv_w'], 'v_ffn_conv_b': out['v_ffn_conv_b'], 'v_w_ffn_down': out['v_w_ffn_down']}


def _loss(weights, diff, rest, loss_target):
    with _jax.named_scope("forward"):
        args = {**rest, TWIN_DIFF_INPUT: diff, **{k: w.astype(_WEIGHT_DTYPES[k]) for k, w in weights.items()}}
        y = _forward(args)
    with _jax.named_scope("loss_head"):
        err = _jnp.square(y.astype(_jnp.float32) - loss_target)
        return 0.5 * _jnp.sum(_jnp.mean(err, axis=-1)) if err.ndim else 0.5 * err


def _adamw(w, g, m, v):
    m = ADAM_B1 * m + (1.0 - ADAM_B1) * g
    v = ADAM_B2 * v + (1.0 - ADAM_B2) * _jnp.square(g)
    m_hat = m / (1.0 - ADAM_B1 ** ADAM_STEP)
    v_hat = v / (1.0 - ADAM_B2 ** ADAM_STEP)
    delta = -ADAM_LR * (m_hat / (_jnp.sqrt(v_hat) + ADAM_EPS) + ADAM_WD * w)
    return delta, m, v


def reference(x, mem, norm_mix, w_in, fox_q_norm, fox_k_norm, fox_f_bias, s5_a_re, s5_a_im, s5_log_dt, s5_b_re, s5_b_im, s5_c_re, s5_c_im, s5_d, s5_w_glu, s5_b_glu, out_norm_fox, out_norm_s5, w_out, norm_cross, norm_mem, w_xq, w_xkv, xq_norm, xk_norm, w_xo, norm_ffn, w_ffn_up, ffn_conv_w, ffn_conv_b, w_ffn_down, loss_target, m_norm_mix, m_w_in, m_fox_q_norm, m_fox_k_norm, m_fox_f_bias, m_s5_a_re, m_s5_a_im, m_s5_log_dt, m_s5_b_re, m_s5_b_im, m_s5_c_re, m_s5_c_im, m_s5_d, m_s5_w_glu, m_s5_b_glu, m_out_norm_fox, m_out_norm_s5, m_w_out, m_norm_cross, m_norm_mem, m_w_xq, m_w_xkv, m_xq_norm, m_xk_norm, m_w_xo, m_norm_ffn, m_w_ffn_up, m_ffn_conv_w, m_ffn_conv_b, m_w_ffn_down, v_norm_mix, v_w_in, v_fox_q_norm, v_fox_k_norm, v_fox_f_bias, v_s5_a_re, v_s5_a_im, v_s5_log_dt, v_s5_b_re, v_s5_b_im, v_s5_c_re, v_s5_c_im, v_s5_d, v_s5_w_glu, v_s5_b_glu, v_out_norm_fox, v_out_norm_s5, v_w_out, v_norm_cross, v_norm_mem, v_w_xq, v_w_xkv, v_xq_norm, v_xk_norm, v_w_xo, v_norm_ffn, v_w_ffn_up, v_ffn_conv_w, v_ffn_conv_b, v_w_ffn_down):
    given = dict(x=x, mem=mem, norm_mix=norm_mix, w_in=w_in, fox_q_norm=fox_q_norm, fox_k_norm=fox_k_norm, fox_f_bias=fox_f_bias, s5_a_re=s5_a_re, s5_a_im=s5_a_im, s5_log_dt=s5_log_dt, s5_b_re=s5_b_re, s5_b_im=s5_b_im, s5_c_re=s5_c_re, s5_c_im=s5_c_im, s5_d=s5_d, s5_w_glu=s5_w_glu, s5_b_glu=s5_b_glu, out_norm_fox=out_norm_fox, out_norm_s5=out_norm_s5, w_out=w_out, norm_cross=norm_cross, norm_mem=norm_mem, w_xq=w_xq, w_xkv=w_xkv, xq_norm=xq_norm, xk_norm=xk_norm, w_xo=w_xo, norm_ffn=norm_ffn, w_ffn_up=w_ffn_up, ffn_conv_w=ffn_conv_w, ffn_conv_b=ffn_conv_b, w_ffn_down=w_ffn_down, loss_target=loss_target, m_norm_mix=m_norm_mix, m_w_in=m_w_in, m_fox_q_norm=m_fox_q_norm, m_fox_k_norm=m_fox_k_norm, m_fox_f_bias=m_fox_f_bias, m_s5_a_re=m_s5_a_re, m_s5_a_im=m_s5_a_im, m_s5_log_dt=m_s5_log_dt, m_s5_b_re=m_s5_b_re, m_s5_b_im=m_s5_b_im, m_s5_c_re=m_s5_c_re, m_s5_c_im=m_s5_c_im, m_s5_d=m_s5_d, m_s5_w_glu=m_s5_w_glu, m_s5_b_glu=m_s5_b_glu, m_out_norm_fox=m_out_norm_fox, m_out_norm_s5=m_out_norm_s5, m_w_out=m_w_out, m_norm_cross=m_norm_cross, m_norm_mem=m_norm_mem, m_w_xq=m_w_xq, m_w_xkv=m_w_xkv, m_xq_norm=m_xq_norm, m_xk_norm=m_xk_norm, m_w_xo=m_w_xo, m_norm_ffn=m_norm_ffn, m_w_ffn_up=m_w_ffn_up, m_ffn_conv_w=m_ffn_conv_w, m_ffn_conv_b=m_ffn_conv_b, m_w_ffn_down=m_w_ffn_down, v_norm_mix=v_norm_mix, v_w_in=v_w_in, v_fox_q_norm=v_fox_q_norm, v_fox_k_norm=v_fox_k_norm, v_fox_f_bias=v_fox_f_bias, v_s5_a_re=v_s5_a_re, v_s5_a_im=v_s5_a_im, v_s5_log_dt=v_s5_log_dt, v_s5_b_re=v_s5_b_re, v_s5_b_im=v_s5_b_im, v_s5_c_re=v_s5_c_re, v_s5_c_im=v_s5_c_im, v_s5_d=v_s5_d, v_s5_w_glu=v_s5_w_glu, v_s5_b_glu=v_s5_b_glu, v_out_norm_fox=v_out_norm_fox, v_out_norm_s5=v_out_norm_s5, v_w_out=v_w_out, v_norm_cross=v_norm_cross, v_norm_mem=v_norm_mem, v_w_xq=v_w_xq, v_w_xkv=v_w_xkv, v_xq_norm=v_xq_norm, v_xk_norm=v_xk_norm, v_w_xo=v_w_xo, v_norm_ffn=v_norm_ffn, v_w_ffn_up=v_w_ffn_up, v_ffn_conv_w=v_ffn_conv_w, v_ffn_conv_b=v_ffn_conv_b, v_w_ffn_down=v_w_ffn_down)
    weights = {n: given[n] for n in TWIN_WEIGHTS}
    shared = {n: given[n] for n in SHARED_INPUTS}
    per_example = {n: given[n] for n in ['x', 'mem']}
    grad_fn = _jax.value_and_grad(_loss, argnums=(0, 1))

    def one_microbatch(ex, loss_target):
        ex = dict(ex)
        diff = ex.pop(TWIN_DIFF_INPUT)
        return grad_fn(weights, diff, {**shared, **ex}, loss_target)

    if N_MICROBATCH == 1:
        loss, (grad_w, grad_x) = one_microbatch(per_example, given["loss_target"])
    else:
        def body(carry, xs):
            loss_sum, grad_sum = carry
            l_k, (gw_k, gx_k) = one_microbatch(xs[0], xs[1])
            with _jax.named_scope("update"):
                return (loss_sum + l_k, _jax.tree.map(_jnp.add, grad_sum, gw_k)), gx_k

        init = (_jnp.zeros((), _jnp.float32), _jax.tree.map(_jnp.zeros_like, weights))
        (loss, grad_w), grad_x = _jax.lax.scan(body, init, (per_example, given["loss_target"]))
    with _jax.named_scope("update"):
        delta_w, new_m, new_v = {}, {}, {}
        for n in TWIN_WEIGHTS:
            delta_w[n], new_m[n], new_v[n] = _adamw(weights[n], grad_w[n], given["m_" + n], given["v_" + n])
    return (loss, grad_x, *[grad_w[n] for n in TWIN_WEIGHTS], *[delta_w[n] for n in TWIN_WEIGHTS],
            *[new_m[n] for n in TWIN_WEIGHTS], *[new_v[n] for n in TWIN_WEIGHTS])
```

```python
import functools
import math

import jax
import jax.numpy as jnp
from jax import lax
from jax.experimental import pallas as pl
from jax.experimental.pallas import tpu as pltpu

F32 = jnp.float32
BF16 = jnp.bfloat16

D_MODEL = 1024
FOX_WIDTH = 512
HEAD_DIM = 64
N_FOX_HEADS = 8
S5_WIDTH = 512
S5_GROUP_CH = 16
S5_GROUPS = 32
S5_STATE = 64
S5_CH = S5_GROUPS * S5_STATE
N_X_HEADS = 4
X_HEAD_DIM = 256
N_MEM = 256
D_FF = 2816
UF_COLS = 640
EPS = 1e-6
ADAM_LR = 0.001
ADAM_B1 = 0.9
ADAM_B2 = 0.999
ADAM_EPS = 1e-08
ADAM_WD = 0.01
ADAM_STEP = 10

VMEM_LIMIT_BYTES = 56 * 1024 * 1024
MM_BLOCK_BYTES = 6 * 1024 * 1024
MESH = pl.DeviceIdType.MESH

BIG = ("w_in", "s5_w_glu", "w_out", "w_xq", "w_xkv", "w_xo", "w_ffn_up", "w_ffn_down")
COL_SHARDED = ("w_in", "w_xkv", "w_ffn_up")
SMALL = ("norm_mix", "fox_q_norm", "fox_k_norm", "fox_f_bias", "s5_a_re", "s5_a_im", "s5_log_dt",
         "s5_b_re", "s5_b_im", "s5_c_re", "s5_c_im", "s5_d", "s5_b_glu", "out_norm_fox", "out_norm_s5",
         "norm_cross", "norm_mem", "xq_norm", "xk_norm", "norm_ffn", "ffn_conv_b")
WEIGHTS = ("norm_mix", "w_in", "fox_q_norm", "fox_k_norm", "fox_f_bias", "s5_a_re", "s5_a_im", "s5_log_dt",
           "s5_b_re", "s5_b_im", "s5_c_re", "s5_c_im", "s5_d", "s5_w_glu", "s5_b_glu", "out_norm_fox",
           "out_norm_s5", "w_out", "norm_cross", "norm_mem", "w_xq", "w_xkv", "xq_norm", "xk_norm", "w_xo",
           "norm_ffn", "w_ffn_up", "ffn_conv_w", "ffn_conv_b", "w_ffn_down")


def _params(sem=None):
    return pltpu.CompilerParams(dimension_semantics=sem, vmem_limit_bytes=VMEM_LIMIT_BYTES)


def _pick(n, cands):
    for c in cands:
        if n % c == 0:
            return c
    return n


_DIMS = {"nn": (((1,), (0,)), ((), ())), "nt": (((1,), (1,)), ((), ())), "tn": (((0,), (0,)), ((), ()))}


def _mm(a, b, mode, name, out_dtype=F32, res=None):
    if mode == "nn":
        (m, k), (k2, n) = a.shape, b.shape
    elif mode == "nt":
        (m, k), (n, k2) = a.shape, b.shape
    else:
        (k, m), (k2, n) = a.shape, b.shape
    assert k == k2, (name, a.shape, b.shape)

    def fit(dim, itemsize):
        for c in (512, 256, 128):
            if dim % c == 0 and c * k * itemsize <= MM_BLOCK_BYTES:
                return c
        return 128 if dim % 128 == 0 else dim

    tm, tn = fit(m, a.dtype.itemsize), fit(n, b.dtype.itemsize)
    a_spec = pl.BlockSpec((k, tm), lambda i, j: (0, i)) if mode == "tn" else pl.BlockSpec((tm, k), lambda i, j: (i, 0))
    b_spec = pl.BlockSpec((tn, k), lambda i, j: (j, 0)) if mode == "nt" else pl.BlockSpec((k, tn), lambda i, j: (0, j))
    o_spec = pl.BlockSpec((tm, tn), lambda i, j: (i, j))
    dims = _DIMS[mode]
    has_res = res is not None

    def body(*refs):
        a_ref, b_ref = refs[0], refs[1]
        o_ref = refs[-1]
        acc = lax.dot_general(a_ref[...].astype(BF16), b_ref[...].astype(BF16), dims, preferred_element_type=F32)
        if has_res:
            acc = acc + refs[2][...].astype(F32)
        o_ref[...] = acc.astype(o_ref.dtype)

    return pl.pallas_call(
        body, name=name, grid=(m // tm, n // tn),
        in_specs=[a_spec, b_spec] + ([o_spec] if has_res else []),
        out_specs=o_spec, out_shape=jax.ShapeDtypeStruct((m, n), out_dtype),
        compiler_params=_params(("parallel", "parallel")),
    )(*((a, b, res) if has_res else (a, b)))


def _row_spec(tm, bc, off, step):
    return pl.BlockSpec((tm, bc), lambda i, h: (i, off + step * h))


def _rowwise(fn, rows, pars, outs, name, heads=1, tm=256):
    t = rows[0][0].shape[0]
    tm = _pick(t, (tm, 256, 128, 64, 8))
    nr, npar = len(rows), len(pars)

    def body(*refs):
        vals = [r[...].astype(F32) for r in refs[:nr + npar]]
        res = fn(*vals)
        if not isinstance(res, (tuple, list)):
            res = (res,)
        for o_ref, v in zip(refs[nr + npar:], res):
            o_ref[...] = v.astype(o_ref.dtype)

    in_specs = [_row_spec(tm, bc, off, st) for (_, bc, off, st) in rows]
    in_specs += [pl.BlockSpec(p.shape, lambda i, h: (0, 0)) for p in pars]
    out_specs = [_row_spec(tm, bc, 0, st) for (_, bc, st, _) in outs]
    out_shape = [jax.ShapeDtypeStruct((t, c), dt) for (c, _, _, dt) in outs]
    res = pl.pallas_call(
        body, name=name, grid=(t // tm, heads), in_specs=in_specs, out_specs=out_specs, out_shape=out_shape,
        compiler_params=_params(("parallel", "parallel")),
    )(*[r[0] for r in rows], *pars)
    return res[0] if len(res) == 1 else res


def _rowwise_vjp(fn, rows, pars, cts, name, heads=1, adds=None, tm=256, row_dtypes=None):
    t = rows[0][0].shape[0]
    tm = _pick(t, (tm, 256, 128, 64, 8))
    nr, npar, nct = len(rows), len(pars), len(cts)
    adds = adds or [None] * nr
    add_list = [a for a in adds if a is not None]
    row_dtypes = row_dtypes or [F32] * nr

    def body(*refs):
        i, h = pl.program_id(0), pl.program_id(1)
        p = 0
        row_v = [r[...].astype(F32) for r in refs[p:p + nr]]; p += nr
        par_v = [r[...].astype(F32) for r in refs[p:p + npar]]; p += npar
        ct_v = [r[...].astype(F32) for r in refs[p:p + nct]]; p += nct
        add_refs = refs[p:p + len(add_list)]; p += len(add_list)
        drow_refs = refs[p:p + nr]; p += nr
        dpar_refs = refs[p:p + npar]

        def wrapped(*a):
            r = fn(*a)
            return tuple(r) if isinstance(r, (tuple, list)) else (r,)

        _, pull = jax.vjp(wrapped, *row_v, *par_v)
        grads = pull(tuple(ct_v))
        ai = 0
        for k in range(nr):
            g = grads[k]
            if adds[k] is not None:
                g = g + add_refs[ai][...].astype(F32)
                ai += 1
            drow_refs[k][...] = g.astype(drow_refs[k].dtype)

        @pl.when((i == 0) & (h == 0))
        def _():
            for r in dpar_refs:
                r[...] = jnp.zeros(r.shape, r.dtype)

        for k in range(npar):
            dpar_refs[k][...] += grads[nr + k]

    in_specs = [_row_spec(tm, bc, off, st) for (_, bc, off, st) in rows]
    in_specs += [pl.BlockSpec(q.shape, lambda i, h: (0, 0)) for q in pars]
    in_specs += [_row_spec(tm, bc, off, st) for (_, bc, off, st) in cts]
    in_specs += [_row_spec(tm, bc, off, st) for (_, bc, off, st) in add_list]
    out_specs = [_row_spec(tm, bc, 0, st) for (_, bc, _, st) in rows]
    out_specs += [pl.BlockSpec(q.shape, lambda i, h: (0, 0)) for q in pars]
    out_shape = [jax.ShapeDtypeStruct((t, bc * (heads if st else 1)), dt) for (_, bc, _, st), dt in zip(rows, row_dtypes)]
    out_shape += [jax.ShapeDtypeStruct(q.shape, F32) for q in pars]
    res = pl.pallas_call(
        body, name=name, grid=(t // tm, heads), in_specs=in_specs, out_specs=out_specs, out_shape=out_shape,
        compiler_params=_params(("arbitrary", "arbitrary")),
    )(*[r[0] for r in rows], *pars, *[c[0] for c in cts], *[a[0] for a in add_list])
    return list(res[:nr]), list(res[nr:])


def _rms(x, g):
    return x * lax.rsqrt(jnp.mean(x * x, axis=-1, keepdims=True) + EPS) * g


def _gelu(x):
    return 0.5 * x * (1.0 + jnp.tanh(math.sqrt(2.0 / math.pi) * (x + 0.044715 * (x * x * x))))


def _s5_act(ys, u, d):
    return _gelu(ys + d * u)


def _s5_gate(yg, z, b, g):
    return _rms(yg * jax.nn.sigmoid(z + b), g)


def _lane_cumsum(x, reverse):
    n = x.shape[-1]
    lane = lax.broadcasted_iota(jnp.int32, x.shape, 1)
    k = 1
    while k < n:
        if reverse:
            x = x + jnp.where(lane < n - k, pltpu.roll(x, n - k, 1), 0.0)
        else:
            x = x + jnp.where(lane >= k, pltpu.roll(x, k, 1), 0.0)
        k *= 2
    return x


def _log_sigmoid(z):
    return jnp.minimum(z, 0.0) - jnp.log(1.0 + jnp.exp(-jnp.abs(z)))


def _forget_fwd(f, bias):
    def body(f_ref, b_ref, c_ref):
        c_ref[...] = _lane_cumsum(_log_sigmoid(f_ref[...] + b_ref[...]), False)

    return pl.pallas_call(body, name="forget_fwd", out_shape=jax.ShapeDtypeStruct(f.shape, F32),
                          compiler_params=_params())(f, bias)


def _forget_bwd(f, bias, dc):
    def body(f_ref, b_ref, dc_ref, df_ref, db_ref):
        dlog = _lane_cumsum(dc_ref[...], True)
        df = dlog * jax.nn.sigmoid(-(f_ref[...] + b_ref[...]))
        df_ref[...] = df
        db_ref[...] = jnp.sum(df, axis=1, keepdims=True)

    return pl.pallas_call(body, name="forget_bwd",
                          out_shape=(jax.ShapeDtypeStruct(f.shape, F32), jax.ShapeDtypeStruct(bias.shape, F32)),
                          compiler_params=_params())(f, bias, dc)


FOX_BLOCK = 256
_NT = _DIMS["nt"]
_TN = _DIMS["tn"]


def _fox_fwd(q, k, v, c):
    bh, l, dh = q.shape
    tb = min(FOX_BLOCK, l)
    scale = HEAD_DIM ** -0.5

    def body(q_ref, k_ref, v_ref, c_ref, o_ref, lse_ref):
        i = pl.program_id(1)
        qv = q_ref[...]
        rows = i * tb + lax.broadcasted_iota(jnp.int32, (tb, tb), 0)

        def step(j, carry):
            m, s_sum, acc = carry
            off = pl.multiple_of(j * tb, tb)
            kv = k_ref[pl.ds(off, tb), :]
            vv = v_ref[pl.ds(off, tb), :]
            s = lax.dot_general(qv, kv, _NT, preferred_element_type=F32) * scale - c_ref[:, pl.ds(off, tb)]
            cols = off + lax.broadcasted_iota(jnp.int32, (tb, tb), 1)
            s = jnp.where(cols <= rows, s, -jnp.inf)
            m_new = jnp.maximum(m, jnp.max(s, axis=-1, keepdims=True))
            alpha = jnp.exp(m - m_new)
            p = jnp.exp(s - m_new)
            s_sum = alpha * s_sum + jnp.sum(p, axis=-1, keepdims=True)
            acc = alpha * acc + jnp.dot(p.astype(BF16), vv, preferred_element_type=F32)
            return m_new, s_sum, acc

        m, s_sum, acc = lax.fori_loop(
            0, i + 1, step,
            (jnp.full((tb, 1), -jnp.inf, F32), jnp.zeros((tb, 1), F32), jnp.zeros((tb, dh), F32)))
        o_ref[...] = acc / s_sum
        lse_ref[...] = jnp.broadcast_to(m + jnp.log(s_sum), (tb, 128))

    full = pl.BlockSpec((None, l, dh), lambda b, i: (b, 0, 0))
    return pl.pallas_call(
        body, name="fox_fwd", grid=(bh, l // tb),
        in_specs=[pl.BlockSpec((None, tb, dh), lambda b, i: (b, i, 0)), full, full,
                  pl.BlockSpec((None, 1, l), lambda b, i: (b, 0, 0))],
        out_specs=[pl.BlockSpec((None, tb, dh), lambda b, i: (b, i, 0)),
                   pl.BlockSpec((None, tb, 128), lambda b, i: (b, i, 0))],
        out_shape=[jax.ShapeDtypeStruct((bh, l, dh), F32), jax.ShapeDtypeStruct((bh, l, 128), F32)],
        compiler_params=_params(("parallel", "parallel")),
    )(q, k, v, c)


def _fox_bwd(q, k, v, c, o, do, lse):
    bh, l, dh = q.shape
    tb = min(FOX_BLOCK, l)
    nb = l // tb
    scale = HEAD_DIM ** -0.5

    def body(q_ref, k_ref, v_ref, c_ref, o_ref, do_ref, lse_ref, dq_ref, dk_ref, dv_ref, dc_ref, dcq_ref):
        dq_ref[...] = jnp.zeros(dq_ref.shape, F32)
        dcq_ref[...] = jnp.zeros(dcq_ref.shape, F32)

        def kv_block(j, _):
            koff = pl.multiple_of(j * tb, tb)
            kv = k_ref[pl.ds(koff, tb), :]
            vv = v_ref[pl.ds(koff, tb), :]
            cj = c_ref[:, pl.ds(koff, tb)]
            cols = koff + lax.broadcasted_iota(jnp.int32, (tb, tb), 1)

            def q_block(i, carry):
                dk, dv, dc = carry
                qoff = pl.multiple_of(i * tb, tb)
                qv = q_ref[pl.ds(qoff, tb), :]
                dov = do_ref[pl.ds(qoff, tb), :]
                delta = jnp.sum(dov * o_ref[pl.ds(qoff, tb), :], axis=-1, keepdims=True)
                dob = dov.astype(BF16)
                s = lax.dot_general(qv, kv, _NT, preferred_element_type=F32) * scale - cj
                rows = qoff + lax.broadcasted_iota(jnp.int32, (tb, tb), 0)
                p = jnp.where(cols <= rows, jnp.exp(s - lse_ref[pl.ds(qoff, tb), 0:1]), 0.0)
                dp = lax.dot_general(dob, vv, _NT, preferred_element_type=F32)
                ds = p * (dp - delta)
                dsb = ds.astype(BF16)
                dv = dv + lax.dot_general(p.astype(BF16), dob, _TN, preferred_element_type=F32)
                dk = dk + lax.dot_general(dsb, qv, _TN, preferred_element_type=F32) * scale
                dq_ref[pl.ds(qoff, tb), :] += jnp.dot(dsb, kv, preferred_element_type=F32) * scale
                dc = dc - jnp.sum(ds, axis=0, keepdims=True)
                dcq_ref[pl.ds(qoff, tb), :] += jnp.broadcast_to(jnp.sum(ds, axis=-1, keepdims=True), (tb, 128))
                return dk, dv, dc

            dk, dv, dc = lax.fori_loop(
                j, nb, q_block, (jnp.zeros((tb, dh), F32), jnp.zeros((tb, dh), F32), jnp.zeros((1, tb), F32)))
            dk_ref[pl.ds(koff, tb), :] = dk
            dv_ref[pl.ds(koff, tb), :] = dv
            dc_ref[:, pl.ds(koff, tb)] = dc
            return 0

        lax.fori_loop(0, nb, kv_block, 0)

    full = pl.BlockSpec((None, l, dh), lambda b: (b, 0, 0))
    row = pl.BlockSpec((None, 1, l), lambda b: (b, 0, 0))
    wide = pl.BlockSpec((None, l, 128), lambda b: (b, 0, 0))
    return pl.pallas_call(
        body, name="fox_bwd", grid=(bh,),
        in_specs=[full, full, full, row, full, full, wide],
        out_specs=[full, full, full, row, wide],
        out_shape=[jax.ShapeDtypeStruct((bh, l, dh), F32)] * 3 + [jax.ShapeDtypeStruct((bh, 1, l), F32),
                                                                 jax.ShapeDtypeStruct((bh, l, 128), F32)],
        compiler_params=_params(("parallel",)),
    )(q, k, v, c, o, do, lse)


SCAN_ROWS = 256
SCAN_COLS = 1024


def _scan_fwd(bur, bui, ar, ai, seqs):
    t, ch = bur.shape
    l = t // seqs
    tl, cb = min(SCAN_ROWS, l), min(SCAN_COLS, ch)
    nl = l // tl

    def body(br_ref, bi_ref, ar_ref, ai_ref, xr_ref, xi_ref, cr, ci):
        @pl.when(pl.program_id(2) == 0)
        def _():
            cr[...] = jnp.zeros(cr.shape, F32)
            ci[...] = jnp.zeros(ci.shape, F32)

        a_r, a_i = ar_ref[...], ai_ref[...]

        def step(tt, carry):
            xr, xi = carry
            nr = a_r * xr - a_i * xi + br_ref[pl.ds(tt, 1), :]
            ni = a_r * xi + a_i * xr + bi_ref[pl.ds(tt, 1), :]
            xr_ref[pl.ds(tt, 1), :] = nr
            xi_ref[pl.ds(tt, 1), :] = ni
            return nr, ni

        xr, xi = lax.fori_loop(0, tl, step, (cr[...], ci[...]), unroll=8)
        cr[...] = xr
        ci[...] = xi

    blk = pl.BlockSpec((tl, cb), lambda s, j, r: (s * nl + r, j))
    par = pl.BlockSpec((1, cb), lambda s, j, r: (0, j))
    return pl.pallas_call(
        body, name="s5_scan_fwd", grid=(seqs, ch // cb, nl),
        in_specs=[blk, blk, par, par], out_specs=[blk, blk],
        out_shape=[jax.ShapeDtypeStruct((t, ch), F32)] * 2,
        scratch_shapes=[pltpu.VMEM((1, cb), F32), pltpu.VMEM((1, cb), F32)],
        compiler_params=_params(("parallel", "parallel", "arbitrary")),
    )(bur, bui, ar, ai)


def _scan_bwd(gr, gi, xr, xi, ar, ai, seqs):
    t, ch = gr.shape
    l = t // seqs
    tl, cb = min(SCAN_ROWS, l), min(SCAN_COLS, ch)
    nl = l // tl

    def body(gr_ref, gi_ref, xr_ref, xi_ref, ar_ref, ai_ref, lr_ref, li_ref, dar_ref, dai_ref, cr, ci):
        @pl.when(pl.program_id(2) == 0)
        def _():
            cr[...] = jnp.zeros(cr.shape, F32)
            ci[...] = jnp.zeros(ci.shape, F32)
            dar_ref[...] = jnp.zeros(dar_ref.shape, F32)
            dai_ref[...] = jnp.zeros(dai_ref.shape, F32)

        a_r, a_i = ar_ref[...], ai_ref[...]

        def step(k, carry):
            lr, li, dar, dai = carry
            tt = tl - 1 - k
            xr_t = xr_ref[pl.ds(tt, 1), :]
            xi_t = xi_ref[pl.ds(tt, 1), :]
            dar = dar + lr * xr_t + li * xi_t
            dai = dai + li * xr_t - lr * xi_t
            nr = gr_ref[pl.ds(tt, 1), :] + a_r * lr + a_i * li
            ni = gi_ref[pl.ds(tt, 1), :] + a_r * li - a_i * lr
            lr_ref[pl.ds(tt, 1), :] = nr
            li_ref[pl.ds(tt, 1), :] = ni
            return nr, ni, dar, dai

        lr, li, dar, dai = lax.fori_loop(
            0, tl, step, (cr[...], ci[...], jnp.zeros((1, cb), F32), jnp.zeros((1, cb), F32)), unroll=8)
        cr[...] = lr
        ci[...] = li
        dar_ref[...] += dar
        dai_ref[...] += dai

    blk = pl.BlockSpec((tl, cb), lambda s, j, r: (s * nl + nl - 1 - r, j))
    par = pl.BlockSpec((1, cb), lambda s, j, r: (0, j))
    acc = pl.BlockSpec((None, 1, cb), lambda s, j, r: (s, 0, j))
    lr, li, dar, dai = pl.pallas_call(
        body, name="s5_scan_bwd", grid=(seqs, ch // cb, nl),
        in_specs=[blk, blk, blk, blk, par, par], out_specs=[blk, blk, acc, acc],
        out_shape=[jax.ShapeDtypeStruct((t, ch), F32)] * 2 + [jax.ShapeDtypeStruct((seqs, 1, ch), F32)] * 2,
        scratch_shapes=[pltpu.VMEM((1, cb), F32), pltpu.VMEM((1, cb), F32)],
        compiler_params=_params(("parallel", "parallel", "arbitrary")),
    )(gr, gi, xr, xi, ar, ai)
    return lr, li, dar, dai


XATT_BLOCK = 512


def _xatt_probs(qv, kv):
    s = lax.dot_general(qv, kv, _NT, preferred_element_type=F32) * (X_HEAD_DIM ** -0.5)
    e = jnp.exp(s - jnp.max(s, axis=-1, keepdims=True))
    return e / jnp.sum(e, axis=-1, keepdims=True)


def _xatt_fwd(q, k, kv, seqs):
    t = q.shape[0]
    tq = min(XATT_BLOCK, t // seqs)
    nq = t // seqs // tq

    def body(q_ref, k_ref, v_ref, o_ref):
        p = _xatt_probs(q_ref[...], k_ref[...])
        o_ref[...] = jnp.dot(p.astype(BF16), v_ref[...].astype(BF16), preferred_element_type=F32).astype(o_ref.dtype)

    qs = pl.BlockSpec((tq, X_HEAD_DIM), lambda b, h, i: (b * nq + i, h))
    return pl.pallas_call(
        body, name="xatt_fwd", grid=(seqs, N_X_HEADS, nq),
        in_specs=[qs, pl.BlockSpec((N_MEM, X_HEAD_DIM), lambda b, h, i: (b, h)),
                  pl.BlockSpec((N_MEM, X_HEAD_DIM), lambda b, h, i: (b, N_X_HEADS + h))],
        out_specs=qs, out_shape=jax.ShapeDtypeStruct(q.shape, BF16),
        compiler_params=_params(("parallel", "parallel", "parallel")),
    )(q, k, kv)


def _xatt_bwd(q, k, kv, do, seqs):
    t = q.shape[0]
    tq = min(XATT_BLOCK, t // seqs)
    nq = t // seqs // tq
    scale = X_HEAD_DIM ** -0.5

    def body(q_ref, k_ref, v_ref, do_ref, dq_ref, dk_ref, dv_ref):
        @pl.when(pl.program_id(2) == 0)
        def _():
            dk_ref[...] = jnp.zeros(dk_ref.shape, F32)
            dv_ref[...] = jnp.zeros(dv_ref.shape, F32)

        qv, kk = q_ref[...], k_ref[...]
        p = _xatt_probs(qv, kk)
        dob = do_ref[...].astype(BF16)
        dp = lax.dot_general(dob, v_ref[...].astype(BF16), _NT, preferred_element_type=F32)
        ds = p * (dp - jnp.sum(dp * p, axis=-1, keepdims=True))
        dsb = ds.astype(BF16)
        dq_ref[...] = jnp.dot(dsb, kk, preferred_element_type=F32) * scale
        dk_ref[...] += lax.dot_general(dsb, qv, _TN, preferred_element_type=F32) * scale
        dv_ref[...] += lax.dot_general(p.astype(BF16), dob, _TN, preferred_element_type=F32)

    qs = pl.BlockSpec((tq, X_HEAD_DIM), lambda b, h, i: (b * nq + i, h))
    ks = pl.BlockSpec((N_MEM, X_HEAD_DIM), lambda b, h, i: (b, h))
    return pl.pallas_call(
        body, name="xatt_bwd", grid=(seqs, N_X_HEADS, nq),
        in_specs=[qs, ks, pl.BlockSpec((N_MEM, X_HEAD_DIM), lambda b, h, i: (b, N_X_HEADS + h)), qs],
        out_specs=[qs, ks, ks],
        out_shape=[jax.ShapeDtypeStruct(q.shape, F32), jax.ShapeDtypeStruct(k.shape, F32),
                   jax.ShapeDtypeStruct(k.shape, F32)],
        compiler_params=_params(("parallel", "parallel", "arbitrary")),
    )(q, k, kv, do)


CONV_COLS = 256


def _shift_down(x, k, row):
    return jnp.where(row >= k, pltpu.roll(x, k, 0), 0.0)


def _shift_up(x, k, row):
    n = x.shape[0]
    return jnp.where(row < n - k, pltpu.roll(x, n - k, 0), 0.0)


def _conv_pre(g, w, b, row):
    return b + w[0:1, :] * _shift_down(g, 2, row) + w[1:2, :] * _shift_down(g, 1, row) + w[2:3, :] * g


def _convgate_fwd(gu, w, b, seqs):
    t = gu.shape[0]
    l = t // seqs
    nc = D_FF // CONV_COLS

    def body(g_ref, u_ref, w_ref, b_ref, o_ref):
        g = g_ref[...]
        row = lax.broadcasted_iota(jnp.int32, g.shape, 0)
        pre = _conv_pre(g, w_ref[...], b_ref[...], row)
        o_ref[...] = (pre * jax.nn.sigmoid(pre) * u_ref[...]).astype(o_ref.dtype)

    return pl.pallas_call(
        body, name="convgate_fwd", grid=(seqs, nc),
        in_specs=[pl.BlockSpec((l, CONV_COLS), lambda s, j: (s, j)), pl.BlockSpec((l, CONV_COLS), lambda s, j: (s, nc + j)),
                  pl.BlockSpec((3, CONV_COLS), lambda s, j: (0, j)), pl.BlockSpec((1, CONV_COLS), lambda s, j: (0, j))],
        out_specs=pl.BlockSpec((l, CONV_COLS), lambda s, j: (s, j)),
        out_shape=jax.ShapeDtypeStruct((t, D_FF), BF16),
        compiler_params=_params(("parallel", "parallel")),
    )(gu, gu, w, b)


def _convgate_bwd(gu, w, b, dact, seqs):
    t = gu.shape[0]
    l = t // seqs
    nc = D_FF // CONV_COLS

    def body(g_ref, u_ref, w_ref, b_ref, da_ref, dg_ref, du_ref, dw_ref, db_ref):
        @pl.when(pl.program_id(1) == 0)
        def _():
            dw_ref[...] = jnp.zeros(dw_ref.shape, F32)
            db_ref[...] = jnp.zeros(db_ref.shape, F32)

        g, wv, da = g_ref[...], w_ref[...], da_ref[...]
        row = lax.broadcasted_iota(jnp.int32, g.shape, 0)
        g1, g2 = _shift_down(g, 1, row), _shift_down(g, 2, row)
        pre = b_ref[...] + wv[0:1, :] * g2 + wv[1:2, :] * g1 + wv[2:3, :] * g
        sg = jax.nn.sigmoid(pre)
        silu = pre * sg
        du_ref[...] = (da * silu).astype(du_ref.dtype)
        dpre = da * u_ref[...] * (sg * (1.0 + pre * (1.0 - sg)))
        dg = wv[2:3, :] * dpre + wv[1:2, :] * _shift_up(dpre, 1, row) + wv[0:1, :] * _shift_up(dpre, 2, row)
        dg_ref[...] = dg.astype(dg_ref.dtype)
        dw_ref[0:1, :] += jnp.sum(dpre * g2, axis=0, keepdims=True)
        dw_ref[1:2, :] += jnp.sum(dpre * g1, axis=0, keepdims=True)
        dw_ref[2:3, :] += jnp.sum(dpre * g, axis=0, keepdims=True)
        db_ref[...] += jnp.sum(dpre, axis=0, keepdims=True)

    blk = lambda off: pl.BlockSpec((l, CONV_COLS), lambda j, s: (s, off + j))
    return pl.pallas_call(
        body, name="convgate_bwd", grid=(nc, seqs),
        in_specs=[blk(0), blk(nc), pl.BlockSpec((3, CONV_COLS), lambda j, s: (0, j)),
                  pl.BlockSpec((1, CONV_COLS), lambda j, s: (0, j)), blk(0)],
        out_specs=[blk(0), blk(0), pl.BlockSpec((3, CONV_COLS), lambda j, s: (0, j)),
                   pl.BlockSpec((1, CONV_COLS), lambda j, s: (0, j))],
        out_shape=[jax.ShapeDtypeStruct((t, D_FF), BF16), jax.ShapeDtypeStruct((t, D_FF), BF16),
                   jax.ShapeDtypeStruct((3, D_FF), F32), jax.ShapeDtypeStruct((1, D_FF), F32)],
        compiler_params=_params(("parallel", "arbitrary")),
    )(gu, gu, w, b, dact)


def _loss_head(h, target):
    t, d = h.shape
    tm = _pick(t, (256, 128, 8))

    def body(h_ref, t_ref, dh_ref, loss_ref):
        @pl.when(pl.program_id(0) == 0)
        def _():
            loss_ref[...] = jnp.zeros(loss_ref.shape, F32)

        e = h_ref[...] - t_ref[...]
        dh_ref[...] = e * (1.0 / d)
        loss_ref[...] += (0.5 / d) * jnp.sum(jnp.sum(e * e, axis=1, keepdims=True), axis=0, keepdims=True)

    blk = pl.BlockSpec((tm, d), lambda i: (i, 0))
    return pl.pallas_call(
        body, name="loss_head", grid=(t // tm,), in_specs=[blk, blk],
        out_specs=[blk, pl.BlockSpec((1, 1), lambda i: (0, 0))],
        out_shape=[jax.ShapeDtypeStruct((t, d), F32), jax.ShapeDtypeStruct((1, 1), F32)],
        compiler_params=_params(("arbitrary",)),
    )(h, target)


def _s5_discretise(a_re, a_im, log_dt, b_re, b_im):
    dt = jnp.exp(log_dt)[:, None]
    mag = jnp.exp(a_re * dt)
    lb_r = mag * jnp.cos(a_im * dt)
    lb_i = mag * jnp.sin(a_im * dt)
    den = a_re * a_re + a_im * a_im
    nr = lb_r - 1.0
    coef_r = (nr * a_re + lb_i * a_im) / den
    coef_i = (lb_i * a_re - nr * a_im) / den
    bb_r = coef_r[:, :, None] * b_re - coef_i[:, :, None] * b_im
    bb_i = coef_r[:, :, None] * b_im + coef_i[:, :, None] * b_re
    return lb_r, lb_i, bb_r, bb_i


def _blockdiag_in(bb):
    eye = jnp.eye(S5_GROUPS, dtype=bb.dtype)
    dense = jnp.einsum("gpc,gh->gchp", bb, eye).reshape(S5_WIDTH, S5_CH)
    return jnp.pad(dense, ((0, UF_COLS - S5_WIDTH), (0, 0)))


def _blockdiag_in_grad(d):
    eye = jnp.eye(S5_GROUPS, dtype=d.dtype)
    return jnp.einsum("gchp,gh->gpc", d[:S5_WIDTH].reshape(S5_GROUPS, S5_GROUP_CH, S5_GROUPS, S5_STATE), eye)


def _blockdiag_out(c):
    eye = jnp.eye(S5_GROUPS, dtype=c.dtype)
    return jnp.einsum("gcp,gh->gphc", c, eye).reshape(S5_CH, S5_WIDTH)


def _blockdiag_out_grad(d):
    eye = jnp.eye(S5_GROUPS, dtype=d.dtype)
    return jnp.einsum("gphc,gh->gcp", d.reshape(S5_GROUPS, S5_STATE, S5_GROUPS, S5_GROUP_CH), eye)


def _to_heads(a, seqs):
    t = a.shape[0]
    return a.reshape(seqs, t // seqs, N_FOX_HEADS, HEAD_DIM).transpose(0, 2, 1, 3).reshape(
        seqs * N_FOX_HEADS, t // seqs, HEAD_DIM)


def _from_heads(a, seqs):
    bh, l, dh = a.shape
    return a.reshape(seqs, N_FOX_HEADS, l, dh).transpose(0, 2, 1, 3).reshape(seqs * l, N_FOX_HEADS * dh)


def _local_step(x3, mem3, target3, p, wb):
    seqs, l, d = x3.shape
    t = seqs * l
    x = x3.reshape(t, d)
    mem = mem3.reshape(seqs * N_MEM, d)
    target = target3.reshape(t, d)
    full = lambda a: (a, a.shape[1], 0, 0)

    s5_in = (p["s5_a_re"], p["s5_a_im"], p["s5_log_dt"], p["s5_b_re"], p["s5_b_im"])
    (lb_r, lb_i, bb_r, bb_i), s5_pull = jax.vjp(_s5_discretise, *s5_in)
    ar, ai = lb_r.reshape(1, S5_CH), lb_i.reshape(1, S5_CH)
    bbr_d, bbi_d = _blockdiag_in(bb_r).astype(BF16), _blockdiag_in(bb_i).astype(BF16)
    cr_d, ci_d = _blockdiag_out(p["s5_c_re"]).astype(BF16), (-_blockdiag_out(p["s5_c_im"])).astype(BF16)
    d_row = p["s5_d"].reshape(1, S5_WIDTH)

    w_in = wb["w_in"]
    w_qkv = w_in[:, :3 * FOX_WIDTH]
    w_uf = jnp.concatenate(
        [w_in[:, 3 * FOX_WIDTH + N_FOX_HEADS:], w_in[:, 3 * FOX_WIDTH:3 * FOX_WIDTH + N_FOX_HEADS],
         jnp.zeros((d, UF_COLS - S5_WIDTH - N_FOX_HEADS), w_in.dtype)], axis=1)

    hn1 = _rowwise(_rms, [full(x)], [p["norm_mix"]], [(d, d, 0, BF16)], "norm_mix_fwd")
    qkv = _mm(hn1, w_qkv, "nn", "in_qkv")
    uf = _mm(hn1, w_uf, "nn", "in_uf")

    q_raw = _to_heads(qkv[:, :FOX_WIDTH], seqs).reshape(-1, HEAD_DIM)
    k_raw = _to_heads(qkv[:, FOX_WIDTH:2 * FOX_WIDTH], seqs).reshape(-1, HEAD_DIM)
    v_h = _to_heads(qkv[:, 2 * FOX_WIDTH:], seqs).astype(BF16)
    bh = seqs * N_FOX_HEADS
    hd = lambda a: (a, HEAD_DIM, 0, 0)
    qn = _rowwise(_rms, [hd(q_raw)], [p["fox_q_norm"]], [(HEAD_DIM, HEAD_DIM, 0, BF16)], "fox_qnorm_fwd", tm=2048)
    kn = _rowwise(_rms, [hd(k_raw)], [p["fox_k_norm"]], [(HEAD_DIM, HEAD_DIM, 0, BF16)], "fox_knorm_fwd", tm=2048)
    qn, kn = qn.reshape(bh, l, HEAD_DIM), kn.reshape(bh, l, HEAD_DIM)

    f_rows = uf[:, S5_WIDTH:S5_WIDTH + N_FOX_HEADS].reshape(seqs, l, N_FOX_HEADS).transpose(0, 2, 1).reshape(bh, l)
    f_bias = jnp.tile(p["fox_f_bias"].reshape(N_FOX_HEADS, 1), (seqs, 1))
    c_rows = _forget_fwd(f_rows, f_bias).reshape(bh, 1, l)
    o_h, lse = _fox_fwd(qn, kn, v_h, c_rows)
    fox = _from_heads(o_h, seqs)

    bur = _mm(uf, bbr_d, "nn", "s5_bu_re")
    bui = _mm(uf, bbi_d, "nn", "s5_bu_im")
    xr, xi = _scan_fwd(bur, bui, ar, ai, seqs)
    ys = _mm(xi, ci_d, "nn", "s5_y_im", res=_mm(xr, cr_d, "nn", "s5_y_re"))
    u_blk = (uf, S5_WIDTH, 0, 0)
    yg = _rowwise(_s5_act, [full(ys), u_blk], [d_row], [(S5_WIDTH, S5_WIDTH, 0, F32)], "s5_act_fwd")
    z = _mm(yg, wb["s5_w_glu"], "nn", "s5_glu")
    y2n = _rowwise(_s5_gate, [full(yg), full(z)], [p["s5_b_glu"], p["out_norm_s5"]],
                   [(S5_WIDTH, S5_WIDTH, 0, BF16)], "s5_gate_fwd")
    foxn = _rowwise(_rms, [full(fox)], [p["out_norm_fox"]], [(FOX_WIDTH, FOX_WIDTH, 0, BF16)], "fox_outnorm_fwd")
    mixed = jnp.concatenate([foxn, y2n], axis=1)
    h1 = _mm(mixed, wb["w_out"], "nn", "mix_out", res=x)

    hn2 = _rowwise(_rms, [full(h1)], [p["norm_cross"]], [(d, d, 0, BF16)], "norm_cross_fwd")
    mn = _rowwise(_rms, [full(mem)], [p["norm_mem"]], [(d, d, 0, BF16)], "norm_mem_fwd")
    xq_raw = _mm(hn2, wb["w_xq"], "nn", "x_q")
    kv = _mm(mn, wb["w_xkv"], "nn", "x_kv")
    xh = lambda a: (a, X_HEAD_DIM, 0, 1)
    xqn = _rowwise(_rms, [xh(xq_raw)], [p["xq_norm"]], [(d, X_HEAD_DIM, 1, BF16)], "x_qnorm_fwd", heads=N_X_HEADS)
    xkn = _rowwise(_rms, [xh(kv)], [p["xk_norm"]], [(d, X_HEAD_DIM, 1, BF16)], "x_knorm_fwd", heads=N_X_HEADS)
    xo = _xatt_fwd(xqn, xkn, kv, seqs)
    h2 = _mm(xo, wb["w_xo"], "nn", "x_out", res=h1)

    hn3 = _rowwise(_rms, [full(h2)], [p["norm_ffn"]], [(d, d, 0, BF16)], "norm_ffn_fwd")
    gu = _mm(hn3, wb["w_ffn_up"], "nn", "ffn_up")
    act = _convgate_fwd(gu, p["ffn_conv_w"], p["ffn_conv_b"], seqs)
    h3 = _mm(act, wb["w_ffn_down"], "nn", "ffn_down", res=h2)
    dh3, loss = _loss_head(h3, target)

    g = {}
    dact = _mm(dh3, wb["w_ffn_down"], "nt", "ffn_down_dx")
    g["w_ffn_down"] = _mm(act, dh3, "tn", "ffn_down_dw")
    dgate, dup, g["ffn_conv_w"], g["ffn_conv_b"] = _convgate_bwd(gu, p["ffn_conv_w"], p["ffn_conv_b"], dact, seqs)
    dgu = jnp.concatenate([dgate, dup], axis=1)
    dhn3 = _mm(dgu, wb["w_ffn_up"], "nt", "ffn_up_dx")
    g["w_ffn_up"] = _mm(hn3, dgu, "tn", "ffn_up_dw")
    (dh2,), (g["norm_ffn"],) = _rowwise_vjp(_rms, [full(h2)], [p["norm_ffn"]], [full(dhn3)], "norm_ffn_bwd",
                                            adds=[full(dh3)])

    dxo = _mm(dh2, wb["w_xo"], "nt", "x_out_dx")
    g["w_xo"] = _mm(xo, dh2, "tn", "x_out_dw")
    dxqn, dxkn, dxv = _xatt_bwd(xqn, xkn, kv, dxo, seqs)
    (dxq_raw,), (g["xq_norm"],) = _rowwise_vjp(_rms, [xh(xq_raw)], [p["xq_norm"]], [xh(dxqn)], "x_qnorm_bwd",
                                               heads=N_X_HEADS, row_dtypes=[BF16])
    (dxk_raw,), (g["xk_norm"],) = _rowwise_vjp(_rms, [xh(kv)], [p["xk_norm"]], [xh(dxkn)], "x_knorm_bwd",
                                               heads=N_X_HEADS, row_dtypes=[BF16])
    dkv = jnp.concatenate([dxk_raw, dxv.astype(BF16)], axis=1)
    dhn2 = _mm(dxq_raw, wb["w_xq"], "nt", "x_q_dx")
    g["w_xq"] = _mm(hn2, dxq_raw, "tn", "x_q_dw")
    dmn = _mm(dkv, wb["w_xkv"], "nt", "x_kv_dx")
    g["w_xkv"] = _mm(mn, dkv, "tn", "x_kv_dw")
    (dh1,), (g["norm_cross"],) = _rowwise_vjp(_rms, [full(h1)], [p["norm_cross"]], [full(dhn2)], "norm_cross_bwd",
                                              adds=[full(dh2)])
    _, (g["norm_mem"],) = _rowwise_vjp(_rms, [full(mem)], [p["norm_mem"]], [full(dmn)], "norm_mem_bwd",
                                       row_dtypes=[BF16])

    dmixed = _mm(dh1, wb["w_out"], "nt", "mix_out_dx")
    g["w_out"] = _mm(mixed, dh1, "tn", "mix_out_dw")
    (dfox,), (g["out_norm_fox"],) = _rowwise_vjp(_rms, [full(fox)], [p["out_norm_fox"]],
                                                 [(dmixed, FOX_WIDTH, 0, 0)], "fox_outnorm_bwd")
    (dyg_a, dz), (g["s5_b_glu"], g["out_norm_s5"]) = _rowwise_vjp(
        _s5_gate, [full(yg), full(z)], [p["s5_b_glu"], p["out_norm_s5"]], [(dmixed, S5_WIDTH, 1, 0)], "s5_gate_bwd",
        row_dtypes=[F32, BF16])
    dyg = _mm(dz, wb["s5_w_glu"], "nt", "s5_glu_dx", res=dyg_a)
    g["s5_w_glu"] = _mm(yg, dz, "tn", "s5_glu_dw")
    (dys, du_a), (dd_row,) = _rowwise_vjp(_s5_act, [full(ys), u_blk], [d_row], [full(dyg)], "s5_act_bwd",
                                          row_dtypes=[BF16, F32])
    g["s5_d"] = dd_row
    dxr = _mm(dys, cr_d, "nt", "s5_y_re_dx")
    dxi = _mm(dys, ci_d, "nt", "s5_y_im_dx")
    dcr_d = _mm(xr, dys, "tn", "s5_y_re_dw")
    dci_d = _mm(xi, dys, "tn", "s5_y_im_dw")
    lam_r, lam_i, dar, dai = _scan_bwd(dxr, dxi, xr, xi, ar, ai, seqs)
    du_b = _mm(lam_i, bbi_d, "nt", "s5_bu_im_dx", res=_mm(lam_r, bbr_d, "nt", "s5_bu_re_dx"))
    dbbr_d = _mm(uf, lam_r, "tn", "s5_bu_re_dw")
    dbbi_d = _mm(uf, lam_i, "tn", "s5_bu_im_dw")
    d_lb_r = jnp.sum(dar, axis=0).reshape(S5_GROUPS, S5_STATE)
    d_lb_i = jnp.sum(dai, axis=0).reshape(S5_GROUPS, S5_STATE)
    g["s5_a_re"], g["s5_a_im"], g["s5_log_dt"], g["s5_b_re"], g["s5_b_im"] = s5_pull(
        (d_lb_r, d_lb_i, _blockdiag_in_grad(dbbr_d), _blockdiag_in_grad(dbbi_d)))
    g["s5_c_re"] = _blockdiag_out_grad(dcr_d)
    g["s5_c_im"] = -_blockdiag_out_grad(dci_d)

    do_h = _to_heads(dfox, seqs)
    dqn, dkn, dv_h, dc, dcq = _fox_bwd(qn, kn, v_h, c_rows, o_h, do_h, lse)
    (dq_raw,), (g["fox_q_norm"],) = _rowwise_vjp(_rms, [hd(q_raw)], [p["fox_q_norm"]],
                                                 [hd(dqn.reshape(-1, HEAD_DIM))], "fox_qnorm_bwd", tm=2048)
    (dk_raw,), (g["fox_k_norm"],) = _rowwise_vjp(_rms, [hd(k_raw)], [p["fox_k_norm"]],
                                                 [hd(dkn.reshape(-1, HEAD_DIM))], "fox_knorm_bwd", tm=2048)
    df_rows, dfb = _forget_bwd(f_rows, f_bias, dc.reshape(bh, l) + dcq[:, :, 0])
    g["fox_f_bias"] = jnp.sum(dfb.reshape(seqs, N_FOX_HEADS), axis=0)
    df = df_rows.reshape(seqs, N_FOX_HEADS, l).transpose(0, 2, 1).reshape(t, N_FOX_HEADS)
    dqkv = jnp.concatenate([_from_heads(dq_raw.reshape(bh, l, HEAD_DIM), seqs),
                            _from_heads(dk_raw.reshape(bh, l, HEAD_DIM), seqs),
                            _from_heads(dv_h, seqs)], axis=1).astype(BF16)
    duf = jnp.concatenate([du_a + du_b[:, :S5_WIDTH], df,
                           jnp.zeros((t, UF_COLS - S5_WIDTH - N_FOX_HEADS), F32)], axis=1).astype(BF16)
    dhn1 = _mm(duf, w_uf, "nt", "in_uf_dx", res=_mm(dqkv, w_qkv, "nt", "in_qkv_dx"))
    dw_qkv = _mm(hn1, dqkv, "tn", "in_qkv_dw")
    dw_uf = _mm(hn1, duf, "tn", "in_uf_dw")
    g["w_in"] = jnp.concatenate([dw_qkv, dw_uf[:, S5_WIDTH:S5_WIDTH + N_FOX_HEADS], dw_uf[:, :S5_WIDTH]], axis=1)
    (dx,), (g["norm_mix"],) = _rowwise_vjp(_rms, [full(x)], [p["norm_mix"]], [full(dhn1)], "norm_mix_bwd",
                                           adds=[full(dh1)])
    return loss, dx.reshape(seqs, l, d), g


def _place():
    return lax.axis_index("x"), lax.axis_index("y"), lax.axis_index("c")


def _other_chips(x, y):
    return [(1 - x, y), (x, 1 - y), (1 - x, 1 - y)]


ANY = pl.BlockSpec(memory_space=pl.ANY)


def _gather_shards(shards):
    n = len(shards)

    def body(*refs):
        ins, outs = refs[:n], refs[n:2 * n]
        send_sems, recv_sems, local_sems = refs[2 * n:]
        x, y, c = _place()
        mine = 2 * x + y
        copies = []
        for a in range(n):
            loc = pltpu.make_async_copy(ins[a], outs[a].at[mine], local_sems.at[a])
            loc.start()
            copies.append(loc)
            for j, (px, py) in enumerate(_other_chips(x, y)):
                cp = pltpu.make_async_remote_copy(
                    src_ref=ins[a], dst_ref=outs[a].at[mine], send_sem=send_sems.at[3 * a + j],
                    recv_sem=recv_sems.at[3 * a + j], device_id=(px, py, c), device_id_type=MESH)
                cp.start()
                copies.append(cp)
        for cp in copies:
            cp.wait()

    return pl.pallas_call(
        body, name="gather_weights", in_specs=[ANY] * n, out_specs=[ANY] * n,
        out_shape=[jax.ShapeDtypeStruct((4,) + s.shape, s.dtype) for s in shards],
        scratch_shapes=[pltpu.SemaphoreType.DMA((3 * n,)), pltpu.SemaphoreType.DMA((3 * n,)),
                        pltpu.SemaphoreType.DMA((n,))],
        compiler_params=pltpu.CompilerParams(has_side_effects=True),
    )(*shards)


def _pair_exchange_halves(grads):
    n = len(grads)

    def body(*refs):
        ins, outs = refs[:n], refs[n:2 * n]
        send_sems, recv_sems = refs[2 * n:]
        x, y, c = _place()
        copies = []
        for a in range(n):
            hr = ins[a].shape[1] // 2
            src = ins[a].at[:, pl.ds(pl.multiple_of((1 - c) * hr, 8), hr), :]
            cp = pltpu.make_async_remote_copy(
                src_ref=src, dst_ref=outs[a], send_sem=send_sems.at[a], recv_sem=recv_sems.at[a],
                device_id=(x, y, 1 - c), device_id_type=MESH)
            cp.start()
            copies.append(cp)
        for cp in copies:
            cp.wait()

    return pl.pallas_call(
        body, name="reduce_pair_exchange", in_specs=[ANY] * n, out_specs=[ANY] * n,
        out_shape=[jax.ShapeDtypeStruct((4, s.shape[1] // 2, s.shape[2]), s.dtype) for s in grads],
        scratch_shapes=[pltpu.SemaphoreType.DMA((n,)), pltpu.SemaphoreType.DMA((n,))],
        compiler_params=pltpu.CompilerParams(has_side_effects=True),
    )(*grads)


def _chip_exchange(sums):
    n = len(sums)

    def body(*refs):
        ins, outs = refs[:n], refs[n:2 * n]
        send_sems, recv_sems = refs[2 * n:]
        x, y, c = _place()
        copies = []
        for a in range(n):
            for j, (px, py) in enumerate(_other_chips(x, y)):
                cp = pltpu.make_async_remote_copy(
                    src_ref=ins[a].at[2 * px + py], dst_ref=outs[a].at[j], send_sem=send_sems.at[3 * a + j],
                    recv_sem=recv_sems.at[3 * a + j], device_id=(px, py, c), device_id_type=MESH)
                cp.start()
                copies.append(cp)
        for cp in copies:
            cp.wait()

    return pl.pallas_call(
        body, name="reduce_chip_exchange", in_specs=[ANY] * n, out_specs=[ANY] * n,
        out_shape=[jax.ShapeDtypeStruct((3,) + s.shape[1:], s.dtype) for s in sums],
        scratch_shapes=[pltpu.SemaphoreType.DMA((3 * n,)), pltpu.SemaphoreType.DMA((3 * n,))],
        compiler_params=pltpu.CompilerParams(has_side_effects=True),
    )(*sums)


def _pair_gather_halves(halves):
    n = len(halves)

    def body(*refs):
        ins, outs = refs[:n], refs[n:2 * n]
        send_sems, recv_sems, local_sems = refs[2 * n:]
        x, y, c = _place()
        copies = []
        for a in range(n):
            hr = ins[a].shape[0]
            rows = outs[a].at[pl.ds(pl.multiple_of(c * hr, 8), hr), :]
            loc = pltpu.make_async_copy(ins[a], rows, local_sems.at[a])
            loc.start()
            cp = pltpu.make_async_remote_copy(
                src_ref=ins[a], dst_ref=rows, send_sem=send_sems.at[a], recv_sem=recv_sems.at[a],
                device_id=(x, y, 1 - c), device_id_type=MESH)
            cp.start()
            copies += [loc, cp]
        for cp in copies:
            cp.wait()

    return pl.pallas_call(
        body, name="reduce_pair_gather", in_specs=[ANY] * n, out_specs=[ANY] * n,
        out_shape=[jax.ShapeDtypeStruct((2 * s.shape[0], s.shape[1]), s.dtype) for s in halves],
        scratch_shapes=[pltpu.SemaphoreType.DMA((n,)), pltpu.SemaphoreType.DMA((n,)), pltpu.SemaphoreType.DMA((n,))],
        compiler_params=pltpu.CompilerParams(has_side_effects=True),
    )(*halves)


def _sum_selected(name, sel, a, a_lead, others, rows_per_block=None):
    r, c = a.shape[-2:]
    tr = _pick(r, (256, 128, 64, 32, 16, 8))
    specs, args, counts = [], [], []
    if a_lead:
        specs.append(pl.BlockSpec((None, tr, c), lambda i, s: (s[0], i, 0)))
    else:
        specs.append(pl.BlockSpec((tr, c), lambda i, s: (i, 0)))
    args.append(a)
    for o in others:
        if o.ndim == 3:
            for k in range(o.shape[0]):
                specs.append(pl.BlockSpec((None, tr, c), lambda i, s, k=k: (k, i, 0)))
                args.append(o)
        else:
            specs.append(pl.BlockSpec((tr, c), lambda i, s: (i, 0)))
            args.append(o)

    def body(s_ref, *refs):
        acc = refs[0][...]
        for r_ in refs[1:-1]:
            acc = acc + r_[...]
        refs[-1][...] = acc

    return pl.pallas_call(
        body, name=name,
        grid_spec=pltpu.PrefetchScalarGridSpec(
            num_scalar_prefetch=1, grid=(r // tr,), in_specs=specs,
            out_specs=pl.BlockSpec((tr, c), lambda i, s: (i, 0))),
        out_shape=jax.ShapeDtypeStruct((r, c), F32),
        compiler_params=_params(("parallel",)),
    )(sel, *args)


def _pair_sum(name, c_sel, grad, recv):
    _, r, c = grad.shape
    hr = r // 2
    tr = _pick(hr, (256, 128, 64, 32, 16, 8))
    nb = hr // tr

    def body(s_ref, g_ref, r_ref, o_ref):
        o_ref[...] = g_ref[...] + r_ref[...]

    return pl.pallas_call(
        body, name=name,
        grid_spec=pltpu.PrefetchScalarGridSpec(
            num_scalar_prefetch=1, grid=(4, nb),
            in_specs=[pl.BlockSpec((None, tr, c), lambda k, i, s: (k, s[0] * nb + i, 0)),
                      pl.BlockSpec((None, tr, c), lambda k, i, s: (k, i, 0))],
            out_specs=pl.BlockSpec((None, tr, c), lambda k, i, s: (k, i, 0))),
        out_shape=jax.ShapeDtypeStruct((4, hr, c), F32),
        compiler_params=_params(("parallel", "parallel")),
    )(c_sel, grad, recv)


def _allreduce_small(vals):
    n = len(vals)

    def body(*refs):
        ins, outs = refs[:n], refs[n:2 * n]
        recv = [refs[2 * n + k * n: 2 * n + (k + 1) * n] for k in range(3)]
        send_sems, recv_sems = refs[5 * n:]
        x, y, c = _place()
        for a in range(n):
            outs[a][...] = ins[a][...]
        for k, peer in enumerate([(x, y, 1 - c), (1 - x, y, c), (x, 1 - y, c)]):
            copies = []
            for a in range(n):
                cp = pltpu.make_async_remote_copy(
                    src_ref=outs[a], dst_ref=recv[k][a], send_sem=send_sems.at[k * n + a],
                    recv_sem=recv_sems.at[k * n + a], device_id=peer, device_id_type=MESH)
                cp.start()
                copies.append(cp)
            for cp in copies:
                cp.wait()
            for a in range(n):
                outs[a][...] = outs[a][...] + recv[k][a][...]

    vm = pl.BlockSpec(memory_space=pltpu.VMEM)
    return pl.pallas_call(
        body, name="allreduce_small", in_specs=[vm] * n, out_specs=[vm] * n,
        out_shape=[jax.ShapeDtypeStruct(v.shape, F32) for v in vals],
        scratch_shapes=[pltpu.VMEM(v.shape, F32) for _ in range(3) for v in vals]
        + [pltpu.SemaphoreType.DMA((3 * n,)), pltpu.SemaphoreType.DMA((3 * n,))],
        compiler_params=pltpu.CompilerParams(has_side_effects=True, vmem_limit_bytes=VMEM_LIMIT_BYTES),
    )(*vals)


def _adamw_math(w, g, m, v):
    m2 = ADAM_B1 * m + (1.0 - ADAM_B1) * g
    v2 = ADAM_B2 * v + (1.0 - ADAM_B2) * (g * g)
    m_hat = m2 / (1.0 - ADAM_B1 ** ADAM_STEP)
    v_hat = v2 / (1.0 - ADAM_B2 ** ADAM_STEP)
    delta = -ADAM_LR * (m_hat / (jnp.sqrt(v_hat) + ADAM_EPS) + ADAM_WD * w)
    return delta, m2, v2


def _adamw_big(name, w, g, m, v):
    r, c = w.shape
    tr = _pick(r, (256, 128, 64, 32, 16, 8))

    def body(w_ref, g_ref, m_ref, v_ref, go_ref, d_ref, mo_ref, vo_ref):
        gv = g_ref[...]
        d, m2, v2 = _adamw_math(w_ref[...], gv, m_ref[...], v_ref[...])
        go_ref[...] = gv
        d_ref[...] = d
        mo_ref[...] = m2
        vo_ref[...] = v2

    blk = pl.BlockSpec((tr, c), lambda i: (i, 0))
    return pl.pallas_call(
        body, name=name, grid=(r // tr,), in_specs=[blk] * 4, out_specs=[blk] * 4,
        out_shape=[jax.ShapeDtypeStruct((r, c), F32)] * 4, compiler_params=_params(("parallel",)),
    )(w, g, m, v)


def _adamw_small(ws, gs, ms, vs):
    n = len(ws)

    def body(*refs):
        w_r, g_r, m_r, v_r = refs[:n], refs[n:2 * n], refs[2 * n:3 * n], refs[3 * n:4 * n]
        o = refs[4 * n:]
        for a in range(n):
            gv = g_r[a][...]
            d, m2, v2 = _adamw_math(w_r[a][...], gv, m_r[a][...], v_r[a][...])
            o[a][...] = gv
            o[n + a][...] = d
            o[2 * n + a][...] = m2
            o[3 * n + a][...] = v2

    res = pl.pallas_call(
        body, name="adamw_small", out_shape=[jax.ShapeDtypeStruct(w.shape, F32) for _ in range(4) for w in ws],
        compiler_params=_params(),
    )(*ws, *gs, *ms, *vs)
    return res[:n], res[n:2 * n], res[2 * n:3 * n], res[3 * n:]


def _full_from_gathered(name, gathered):
    if name in COL_SHARDED:
        return gathered.transpose(1, 0, 2).reshape(gathered.shape[1], 4 * gathered.shape[2])
    return gathered.reshape(4 * gathered.shape[1], gathered.shape[2])


def _shard_major(name, full):
    if name in COL_SHARDED:
        rows, cols = full.shape
        return full.reshape(rows, 4, cols // 4).transpose(1, 0, 2)
    return full.reshape(4, full.shape[0] // 4, full.shape[1])


def kernel(x, mem, norm_mix, w_in, fox_q_norm, fox_k_norm, fox_f_bias, s5_a_re, s5_a_im, s5_log_dt, s5_b_re, s5_b_im, s5_c_re, s5_c_im, s5_d, s5_w_glu, s5_b_glu, out_norm_fox, out_norm_s5, w_out, norm_cross, norm_mem, w_xq, w_xkv, xq_norm, xk_norm, w_xo, norm_ffn, w_ffn_up, ffn_conv_w, ffn_conv_b, w_ffn_down, loss_target, m_norm_mix, m_w_in, m_fox_q_norm, m_fox_k_norm, m_fox_f_bias, m_s5_a_re, m_s5_a_im, m_s5_log_dt, m_s5_b_re, m_s5_b_im, m_s5_c_re, m_s5_c_im, m_s5_d, m_s5_w_glu, m_s5_b_glu, m_out_norm_fox, m_out_norm_s5, m_w_out, m_norm_cross, m_norm_mem, m_w_xq, m_w_xkv, m_xq_norm, m_xk_norm, m_w_xo, m_norm_ffn, m_w_ffn_up, m_ffn_conv_w, m_ffn_conv_b, m_w_ffn_down, v_norm_mix, v_w_in, v_fox_q_norm, v_fox_k_norm, v_fox_f_bias, v_s5_a_re, v_s5_a_im, v_s5_log_dt, v_s5_b_re, v_s5_b_im, v_s5_c_re, v_s5_c_im, v_s5_d, v_s5_w_glu, v_s5_b_glu, v_out_norm_fox, v_out_norm_s5, v_w_out, v_norm_cross, v_norm_mem, v_w_xq, v_w_xkv, v_xq_norm, v_xk_norm, v_w_xo, v_norm_ffn, v_w_ffn_up, v_ffn_conv_w, v_ffn_conv_b, v_w_ffn_down):
    given = dict(locals())
    w = {n: given[n] for n in WEIGHTS}
    m = {n: given["m_" + n] for n in WEIGHTS}
    v = {n: given["v_" + n] for n in WEIGHTS}
    xi, yi, ci = _place()
    chip = (2 * xi + yi).astype(jnp.int32)

    shards = [w[n][0].astype(BF16) for n in BIG] + [w["ffn_conv_w"][0]]
    gathered = _gather_shards(shards)
    wb = {n: _full_from_gathered(n, gathered[k]) for k, n in enumerate(BIG)}
    conv_w = gathered[-1].transpose(1, 0, 2).reshape(3, D_FF)

    p = {n: w[n][0] for n in SMALL}
    p["ffn_conv_w"] = conv_w
    for n in ("norm_mix", "fox_q_norm", "fox_k_norm", "fox_f_bias", "s5_b_glu", "out_norm_fox", "out_norm_s5",
              "norm_cross", "norm_mem", "xq_norm", "xk_norm", "norm_ffn", "ffn_conv_b"):
        p[n] = p[n].reshape(1, -1)
    loss, grad_x, g = _local_step(x, mem, loss_target, p, wb)

    small_names = list(SMALL) + ["ffn_conv_w"]
    small_vals = [g[n].reshape(w[n].shape if n != "ffn_conv_w" else (1, 3, D_FF)) for n in small_names] + [loss]
    reduced = _allreduce_small(small_vals)
    loss_all = reduced[-1].reshape(())
    conv_w_grad = lax.dynamic_slice_in_dim(reduced[-2], chip * (D_FF // 4), D_FF // 4, axis=2)
    sg, sd, sm, sv = _adamw_small(
        [w[n] for n in small_names], list(reduced[:len(SMALL)]) + [conv_w_grad],
        [m[n] for n in small_names], [v[n] for n in small_names])
    out_g = dict(zip(small_names, sg))
    out_d = dict(zip(small_names, sd))
    out_m = dict(zip(small_names, sm))
    out_v = dict(zip(small_names, sv))

    c_sel = ci.astype(jnp.int32).reshape(1)
    chip_sel = chip.reshape(1)
    grads = [_shard_major(n, g[n]) for n in BIG]
    from_sibling = _pair_exchange_halves(grads)
    pair_sums = [_pair_sum("reduce_pair_sum_" + n, c_sel, gr, rv) for n, gr, rv in zip(BIG, grads, from_sibling)]
    from_chips = _chip_exchange(pair_sums)
    halves = [_sum_selected("reduce_chip_sum_" + n, chip_sel, ps, True, [fc])
              for n, ps, fc in zip(BIG, pair_sums, from_chips)]
    shard_grads = _pair_gather_halves(halves)
    for n, sgr in zip(BIG, shard_grads):
        go, d, m2, v2 = _adamw_big("adamw_" + n, w[n][0], sgr, m[n][0], v[n][0])
        out_g[n], out_d[n], out_m[n], out_v[n] = go[None], d[None], m2[None], v2[None]

    return (loss_all, grad_x, *[out_g[n] for n in WEIGHTS], *[out_d[n] for n in WEIGHTS],
            *[out_m[n] for n in WEIGHTS], *[out_v[n] for n in WEIGHTS])
```

```python
import functools
import math

import jax
import jax.numpy as jnp
from jax import lax
from jax.experimental import pallas as pl
from jax.experimental.pallas import tpu as pltpu

F32 = jnp.float32
BF16 = jnp.bfloat16

D_MODEL = 1024
FOX_WIDTH = 512
HEAD_DIM = 64
N_FOX_HEADS = 8
S5_WIDTH = 512
S5_GROUP_CH = 16
S5_GROUPS = 32
S5_STATE = 64
S5_CH = S5_GROUPS * S5_STATE
N_X_HEADS = 4
X_HEAD_DIM = 256
N_MEM = 256
D_FF = 2816
UF_COLS = 640
EPS = 1e-6
ADAM_LR = 0.001
ADAM_B1 = 0.9
ADAM_B2 = 0.999
ADAM_EPS = 1e-08
ADAM_WD = 0.01
ADAM_STEP = 10

VMEM_LIMIT_BYTES = 56 * 1024 * 1024
MM_BLOCK_BYTES = 6 * 1024 * 1024
MESH = pl.DeviceIdType.MESH

BIG = ("w_in", "s5_w_glu", "w_out", "w_xq", "w_xkv", "w_xo", "w_ffn_up", "w_ffn_down")
COL_KIND = ("w_xkv", "w_ffn_up")
SMALL = ("norm_mix", "fox_q_norm", "fox_k_norm", "fox_f_bias", "s5_a_re", "s5_a_im", "s5_log_dt",
         "s5_b_re", "s5_b_im", "s5_c_re", "s5_c_im", "s5_d", "s5_b_glu", "out_norm_fox", "out_norm_s5",
         "norm_cross", "norm_mem", "xq_norm", "xk_norm", "norm_ffn", "ffn_conv_b")
WEIGHTS = ("norm_mix", "w_in", "fox_q_norm", "fox_k_norm", "fox_f_bias", "s5_a_re", "s5_a_im", "s5_log_dt",
           "s5_b_re", "s5_b_im", "s5_c_re", "s5_c_im", "s5_d", "s5_w_glu", "s5_b_glu", "out_norm_fox",
           "out_norm_s5", "w_out", "norm_cross", "norm_mem", "w_xq", "w_xkv", "xq_norm", "xk_norm", "w_xo",
           "norm_ffn", "w_ffn_up", "ffn_conv_w", "ffn_conv_b", "w_ffn_down")


def _params(sem=None):
    return pltpu.CompilerParams(dimension_semantics=sem, vmem_limit_bytes=VMEM_LIMIT_BYTES)


def _pick(n, cands):
    for c in cands:
        if n % c == 0:
            return c
    return n


_DIMS = {"nn": (((1,), (0,)), ((), ())), "nt": (((1,), (1,)), ((), ())), "tn": (((0,), (0,)), ((), ()))}


def _mm(a, b, mode, name, out_dtype=F32, res=None):
    if mode == "nn":
        (m, k), (k2, n) = a.shape, b.shape
    elif mode == "nt":
        (m, k), (n, k2) = a.shape, b.shape
    else:
        (k, m), (k2, n) = a.shape, b.shape
    assert k == k2, (name, a.shape, b.shape)

    def fit(dim, itemsize):
        for c in (512, 256, 128):
            if dim % c == 0 and c * k * itemsize <= MM_BLOCK_BYTES:
                return c
        return 128 if dim % 128 == 0 else dim

    tm, tn = fit(m, a.dtype.itemsize), fit(n, b.dtype.itemsize)
    a_spec = pl.BlockSpec((k, tm), lambda i, j: (0, i)) if mode == "tn" else pl.BlockSpec((tm, k), lambda i, j: (i, 0))
    b_spec = pl.BlockSpec((tn, k), lambda i, j: (j, 0)) if mode == "nt" else pl.BlockSpec((k, tn), lambda i, j: (0, j))
    o_spec = pl.BlockSpec((tm, tn), lambda i, j: (i, j))
    dims = _DIMS[mode]
    has_res = res is not None

    def body(*refs):
        a_ref, b_ref = refs[0], refs[1]
        o_ref = refs[-1]
        acc = lax.dot_general(a_ref[...].astype(BF16), b_ref[...].astype(BF16), dims, preferred_element_type=F32)
        if has_res:
            acc = acc + refs[2][...].astype(F32)
        o_ref[...] = acc.astype(o_ref.dtype)

    return pl.pallas_call(
        body, name=name, grid=(m // tm, n // tn),
        in_specs=[a_spec, b_spec] + ([o_spec] if has_res else []),
        out_specs=o_spec, out_shape=jax.ShapeDtypeStruct((m, n), out_dtype),
        compiler_params=_params(("parallel", "parallel")),
    )(*((a, b, res) if has_res else (a, b)))


def _gmm_tn(a, b, m, n, groups, name):
    t = a.shape[0]

    def body(a_ref, b_ref, o_ref):
        o_ref[...] = lax.dot_general(a_ref[...].astype(BF16), b_ref[...].astype(BF16), _DIMS["tn"],
                                     preferred_element_type=F32)

    return pl.pallas_call(
        body, name=name, grid=(groups,),
        in_specs=[pl.BlockSpec((t, m), lambda j: (0, j)), pl.BlockSpec((t, n), lambda j: (0, j))],
        out_specs=pl.BlockSpec((None, m, n), lambda j: (j, 0, 0)),
        out_shape=jax.ShapeDtypeStruct((groups, m, n), F32), compiler_params=_params(("parallel",)),
    )(a, b)


def _gmm(a, b, mode, name, out_dtype=F32, res=None):
    g = b.shape[0]
    dims = _DIMS[mode]
    has_res = res is not None

    def body(*refs):
        acc = lax.dot_general(refs[0][...].astype(BF16), refs[1][...].astype(BF16), dims, preferred_element_type=F32)
        if has_res:
            acc = acc + refs[2][...].astype(F32)
        refs[-1][...] = acc.astype(refs[-1].dtype)

    k, n = (b.shape[1], b.shape[2]) if mode == "nn" else (b.shape[2], b.shape[1])
    m = a.shape[0]
    tm = _pick(m, (512, 256, 128))
    o_spec = pl.BlockSpec((tm, n), lambda i, j: (i, j))
    return pl.pallas_call(
        body, name=name, grid=(m // tm, g),
        in_specs=[pl.BlockSpec((tm, k), lambda i, j: (i, j)), pl.BlockSpec((None,) + b.shape[1:], lambda i, j: (j, 0, 0))]
        + ([o_spec] if has_res else []),
        out_specs=o_spec, out_shape=jax.ShapeDtypeStruct((m, g * n), out_dtype),
        compiler_params=_params(("parallel", "parallel")),
    )(*((a, b, res) if has_res else (a, b)))


def _row_spec(tm, bc, off, step):
    return pl.BlockSpec((tm, bc), lambda i, h: (i, off + step * h))


def _rowwise(fn, rows, pars, outs, name, heads=1, tm=256):
    t = rows[0][0].shape[0]
    tm = _pick(t, (tm, 256, 128, 64, 8))
    nr, npar = len(rows), len(pars)

    def body(*refs):
        vals = [r[...].astype(F32) for r in refs[:nr + npar]]
        res = fn(*vals)
        if not isinstance(res, (tuple, list)):
            res = (res,)
        for o_ref, v in zip(refs[nr + npar:], res):
            o_ref[...] = v.astype(o_ref.dtype)

    in_specs = [_row_spec(tm, bc, off, st) for (_, bc, off, st) in rows]
    in_specs += [pl.BlockSpec(p.shape, lambda i, h: (0, 0)) for p in pars]
    out_specs = [_row_spec(tm, bc, 0, st) for (_, bc, st, _) in outs]
    out_shape = [jax.ShapeDtypeStruct((t, c), dt) for (c, _, _, dt) in outs]
    res = pl.pallas_call(
        body, name=name, grid=(t // tm, heads), in_specs=in_specs, out_specs=out_specs, out_shape=out_shape,
        compiler_params=_params(("parallel", "parallel")),
    )(*[r[0] for r in rows], *pars)
    return res[0] if len(res) == 1 else res


def _rowwise_vjp(fn, rows, pars, cts, name, heads=1, adds=None, tm=256, row_dtypes=None):
    t = rows[0][0].shape[0]
    tm = _pick(t, (tm, 256, 128, 64, 8))
    nr, npar, nct = len(rows), len(pars), len(cts)
    adds = adds or [None] * nr
    add_list = [a for a in adds if a is not None]
    row_dtypes = row_dtypes or [F32] * nr

    def body(*refs):
        i, h = pl.program_id(0), pl.program_id(1)
        p = 0
        row_v = [r[...].astype(F32) for r in refs[p:p + nr]]; p += nr
        par_v = [r[...].astype(F32) for r in refs[p:p + npar]]; p += npar
        ct_v = [r[...].astype(F32) for r in refs[p:p + nct]]; p += nct
        add_refs = refs[p:p + len(add_list)]; p += len(add_list)
        drow_refs = refs[p:p + nr]; p += nr
        dpar_refs = refs[p:p + npar]

        def wrapped(*a):
            r = fn(*a)
            return tuple(r) if isinstance(r, (tuple, list)) else (r,)

        _, pull = jax.vjp(wrapped, *row_v, *par_v)
        grads = pull(tuple(ct_v))
        ai = 0
        for k in range(nr):
            g = grads[k]
            if adds[k] is not None:
                g = g + add_refs[ai][...].astype(F32)
                ai += 1
            drow_refs[k][...] = g.astype(drow_refs[k].dtype)

        @pl.when((i == 0) & (h == 0))
        def _():
            for r in dpar_refs:
                r[...] = jnp.zeros(r.shape, r.dtype)

        for k in range(npar):
            dpar_refs[k][...] += grads[nr + k]

    in_specs = [_row_spec(tm, bc, off, st) for (_, bc, off, st) in rows]
    in_specs += [pl.BlockSpec(q.shape, lambda i, h: (0, 0)) for q in pars]
    in_specs += [_row_spec(tm, bc, off, st) for (_, bc, off, st) in cts]
    in_specs += [_row_spec(tm, bc, off, st) for (_, bc, off, st) in add_list]
    out_specs = [_row_spec(tm, bc, 0, st) for (_, bc, _, st) in rows]
    out_specs += [pl.BlockSpec(q.shape, lambda i, h: (0, 0)) for q in pars]
    out_shape = [jax.ShapeDtypeStruct((t, bc * (heads if st else 1)), dt) for (_, bc, _, st), dt in zip(rows, row_dtypes)]
    out_shape += [jax.ShapeDtypeStruct(q.shape, F32) for q in pars]
    res = pl.pallas_call(
        body, name=name, grid=(t // tm, heads), in_specs=in_specs, out_specs=out_specs, out_shape=out_shape,
        compiler_params=_params(("arbitrary", "arbitrary")),
    )(*[r[0] for r in rows], *pars, *[c[0] for c in cts], *[a[0] for a in add_list])
    return list(res[:nr]), list(res[nr:])


def _rms(x, g):
    return x * lax.rsqrt(jnp.mean(x * x, axis=-1, keepdims=True) + EPS) * g


def _rms_pair(x, g):
    left = lax.broadcasted_iota(jnp.int32, x.shape, 1) < HEAD_DIM
    x2 = x * x
    ms_a = jnp.sum(jnp.where(left, x2, 0.0), axis=-1, keepdims=True) * (1.0 / HEAD_DIM)
    ms_b = jnp.sum(jnp.where(left, 0.0, x2), axis=-1, keepdims=True) * (1.0 / HEAD_DIM)
    return x * lax.rsqrt(jnp.where(left, ms_a, ms_b) + EPS) * g


def _gelu(x):
    return 0.5 * x * (1.0 + jnp.tanh(math.sqrt(2.0 / math.pi) * (x + 0.044715 * (x * x * x))))


def _s5_act(ys, u, d):
    return _gelu(ys + d * u)


def _s5_gate(yg, z, b, g):
    return _rms(yg * jax.nn.sigmoid(z + b), g)


def _lane_cumsum(x, reverse):
    n = x.shape[-1]
    lane = lax.broadcasted_iota(jnp.int32, x.shape, 1)
    k = 1
    while k < n:
        if reverse:
            x = x + jnp.where(lane < n - k, pltpu.roll(x, n - k, 1), 0.0)
        else:
            x = x + jnp.where(lane >= k, pltpu.roll(x, k, 1), 0.0)
        k *= 2
    return x


def _log_sigmoid(z):
    return jnp.minimum(z, 0.0) - jnp.log(1.0 + jnp.exp(-jnp.abs(z)))


def _forget_fwd(f, bias):
    def body(f_ref, b_ref, c_ref):
        c_ref[...] = _lane_cumsum(_log_sigmoid(f_ref[...] + b_ref[...]), False)

    return pl.pallas_call(body, name="forget_fwd", out_shape=jax.ShapeDtypeStruct(f.shape, F32),
                          compiler_params=_params())(f, bias)


def _forget_bwd(f, bias, dc):
    def body(f_ref, b_ref, dc_ref, df_ref, db_ref):
        dlog = _lane_cumsum(dc_ref[...], True)
        df = dlog * jax.nn.sigmoid(-(f_ref[...] + b_ref[...]))
        df_ref[...] = df
        db_ref[...] = jnp.sum(df, axis=1, keepdims=True)

    return pl.pallas_call(body, name="forget_bwd",
                          out_shape=(jax.ShapeDtypeStruct(f.shape, F32), jax.ShapeDtypeStruct(bias.shape, F32)),
                          compiler_params=_params())(f, bias, dc)


FOX_BLOCK = 256
_NT = _DIMS["nt"]
_TN = _DIMS["tn"]


N_PAIRS = N_FOX_HEADS // 2
V_BLOCK0 = 2 * N_PAIRS


def _left_lanes(shape):
    return lax.broadcasted_iota(jnp.int32, shape, 1) < HEAD_DIM


def _fox_fwd(qn, kn, qkv, c2, seqs):
    t = qn.shape[0]
    l = t // seqs
    tb = min(FOX_BLOCK, l)
    nb = l // tb
    scale = HEAD_DIM ** -0.5

    def body(q_ref, k_ref, v_ref, c_ref, o_ref, lse_ref):
        i = pl.program_id(2)
        left = _left_lanes((tb, 128))
        q2 = (q_ref[...].astype(F32) * scale).astype(BF16)
        zero = jnp.zeros_like(q2)
        qs = (jnp.where(left, q2, zero), jnp.where(left, zero, q2))
        causal = lax.broadcasted_iota(jnp.int32, (tb, tb), 1) <= lax.broadcasted_iota(jnp.int32, (tb, tb), 0)

        def tile(j, carry, masked):
            off = pl.multiple_of(j * tb, tb)
            k2 = k_ref[pl.ds(off, tb), :]
            v2 = v_ref[pl.ds(off, tb), :].astype(BF16)
            vs = (jnp.where(left, v2, zero), jnp.where(left, zero, v2))
            (ma, sa), (mb, sb), acc = carry
            new, alphas, pv = [], [], []
            for h, (m, s_sum) in enumerate(((ma, sa), (mb, sb))):
                s = lax.dot_general(qs[h], k2, _NT, preferred_element_type=F32) - c_ref[h:h + 1, pl.ds(off, tb)]
                if masked:
                    s = jnp.where(causal, s, -jnp.inf)
                m_new = jnp.maximum(m, jnp.max(s, axis=-1, keepdims=True))
                alpha = jnp.exp(m - m_new)
                p = jnp.exp(s - m_new)
                new.append((m_new, alpha * s_sum + jnp.sum(p, axis=-1, keepdims=True)))
                alphas.append(alpha)
                pv.append(jnp.dot(p.astype(BF16), vs[h], preferred_element_type=F32))
            acc = jnp.where(left, alphas[0], alphas[1]) * acc + pv[0] + pv[1]
            return new[0], new[1], acc

        stat = (jnp.full((tb, 1), -jnp.inf, F32), jnp.zeros((tb, 1), F32))
        carry = lax.fori_loop(0, i, lambda j, c: tile(j, c, False), (stat, stat, jnp.zeros((tb, 128), F32)))
        (ma, sa), (mb, sb), acc = tile(i, carry, True)
        o_ref[...] = acc / jnp.where(left, sa, sb)
        lse_ref[...] = jnp.where(left, ma + jnp.log(sa), mb + jnp.log(sb))

    qblk = pl.BlockSpec((tb, 128), lambda b, hp, i: (b * nb + i, hp))
    return pl.pallas_call(
        body, name="fox_fwd", grid=(seqs, N_PAIRS, nb),
        in_specs=[qblk, pl.BlockSpec((l, 128), lambda b, hp, i: (b, hp)),
                  pl.BlockSpec((l, 128), lambda b, hp, i: (b, V_BLOCK0 + hp)),
                  pl.BlockSpec((None, 2, l), lambda b, hp, i: (b * N_PAIRS + hp, 0, 0))],
        out_specs=[qblk, qblk],
        out_shape=[jax.ShapeDtypeStruct((t, FOX_WIDTH), F32), jax.ShapeDtypeStruct((t, FOX_WIDTH), F32)],
        compiler_params=_params(("parallel", "parallel", "parallel")),
    )(qn, kn, qkv, c2)


def _fox_bwd(qn, kn, qkv, c2, o, do, lse, seqs):
    t = qn.shape[0]
    l = t // seqs
    tb = min(FOX_BLOCK, l)
    nb = l // tb
    scale = HEAD_DIM ** -0.5

    def body(q_ref, k_ref, v_ref, c_ref, o_ref, do_ref, lse_ref, dq_ref, dk_ref, dv_ref, dc_ref, dcq_ref):
        dq_ref[...] = jnp.zeros(dq_ref.shape, F32)
        dcq_ref[...] = jnp.zeros(dcq_ref.shape, F32)
        left = _left_lanes((tb, 128))
        zero = jnp.zeros((tb, 128), BF16)
        split = lambda a: (jnp.where(left, a, zero), jnp.where(left, zero, a))
        causal = lax.broadcasted_iota(jnp.int32, (tb, tb), 1) <= lax.broadcasted_iota(jnp.int32, (tb, tb), 0)

        def kv_block(j, _):
            koff = pl.multiple_of(j * tb, tb)
            k2 = k_ref[pl.ds(koff, tb), :]
            ks = split(k2)
            v2 = v_ref[pl.ds(koff, tb), :].astype(BF16)

            def q_block(i, carry, masked):
                dk, dv, dca, dcb = carry
                qoff = pl.multiple_of(i * tb, tb)
                qs = split((q_ref[pl.ds(qoff, tb), :].astype(F32) * scale).astype(BF16))
                dov = do_ref[pl.ds(qoff, tb), :]
                prod = dov * o_ref[pl.ds(qoff, tb), :]
                dos = split(dov.astype(BF16))
                dq_new, rowsums, dcs = [], [], []
                for h in (0, 1):
                    mine = left if h == 0 else jnp.logical_not(left)
                    delta = jnp.sum(jnp.where(mine, prod, 0.0), axis=-1, keepdims=True)
                    s = lax.dot_general(qs[h], k2, _NT, preferred_element_type=F32) - c_ref[h:h + 1, pl.ds(koff, tb)]
                    p = jnp.exp(s - lse_ref[pl.ds(qoff, tb), h * HEAD_DIM:h * HEAD_DIM + 1])
                    if masked:
                        p = jnp.where(causal, p, 0.0)
                    dp = lax.dot_general(dos[h], v2, _NT, preferred_element_type=F32)
                    ds = p * (dp - delta)
                    dsb = ds.astype(BF16)
                    dv = dv + lax.dot_general(p.astype(BF16), dos[h], _TN, preferred_element_type=F32)
                    dk = dk + lax.dot_general(dsb, qs[h], _TN, preferred_element_type=F32)
                    dq_new.append(jnp.dot(dsb, ks[h], preferred_element_type=F32))
                    rowsums.append(jnp.sum(ds, axis=-1, keepdims=True))
                    dcs.append(jnp.sum(ds, axis=0, keepdims=True))
                dq_ref[pl.ds(qoff, tb), :] += (dq_new[0] + dq_new[1]) * scale
                dcq_ref[pl.ds(qoff, tb), :] += jnp.where(left, rowsums[0], rowsums[1])
                return dk, dv, dca - dcs[0], dcb - dcs[1]

            init = (jnp.zeros((tb, 128), F32), jnp.zeros((tb, 128), F32), jnp.zeros((1, tb), F32),
                    jnp.zeros((1, tb), F32))
            carry = q_block(j, init, True)
            dk, dv, dca, dcb = lax.fori_loop(j + 1, nb, lambda i, c: q_block(i, c, False), carry)
            dk_ref[pl.ds(koff, tb), :] = dk
            dv_ref[pl.ds(koff, tb), :] = dv
            dc_ref[0:1, pl.ds(koff, tb)] = dca
            dc_ref[1:2, pl.ds(koff, tb)] = dcb
            return 0

        lax.fori_loop(0, nb, kv_block, 0)

    blk = pl.BlockSpec((l, 128), lambda b, hp: (b, hp))
    crow = pl.BlockSpec((None, 2, l), lambda b, hp: (b * N_PAIRS + hp, 0, 0))
    wide = jax.ShapeDtypeStruct((t, FOX_WIDTH), F32)
    return pl.pallas_call(
        body, name="fox_bwd", grid=(seqs, N_PAIRS),
        in_specs=[blk, blk, pl.BlockSpec((l, 128), lambda b, hp: (b, V_BLOCK0 + hp)), crow, blk, blk, blk],
        out_specs=[blk, blk, blk, crow, blk],
        out_shape=[wide, wide, wide, jax.ShapeDtypeStruct((seqs * N_PAIRS, 2, l), F32), wide],
        compiler_params=_params(("parallel", "parallel")),
    )(qn, kn, qkv, c2, o, do, lse)


SCAN_ROWS = 256
SCAN_COLS = 1024


def _scan_fwd(bur, bui, ar, ai, seqs):
    t, ch = bur.shape
    l = t // seqs
    tl, cb = min(SCAN_ROWS, l), min(SCAN_COLS, ch)
    nl = l // tl

    def body(br_ref, bi_ref, ar_ref, ai_ref, xr_ref, xi_ref, cr, ci):
        @pl.when(pl.program_id(2) == 0)
        def _():
            cr[...] = jnp.zeros(cr.shape, F32)
            ci[...] = jnp.zeros(ci.shape, F32)

        a_r, a_i = ar_ref[...], ai_ref[...]

        def step(tt, carry):
            xr, xi = carry
            nr = a_r * xr - a_i * xi + br_ref[pl.ds(tt, 1), :]
            ni = a_r * xi + a_i * xr + bi_ref[pl.ds(tt, 1), :]
            xr_ref[pl.ds(tt, 1), :] = nr
            xi_ref[pl.ds(tt, 1), :] = ni
            return nr, ni

        xr, xi = lax.fori_loop(0, tl, step, (cr[...], ci[...]), unroll=8)
        cr[...] = xr
        ci[...] = xi

    blk = pl.BlockSpec((tl, cb), lambda s, j, r: (s * nl + r, j))
    par = pl.BlockSpec((1, cb), lambda s, j, r: (0, j))
    return pl.pallas_call(
        body, name="s5_scan_fwd", grid=(seqs, ch // cb, nl),
        in_specs=[blk, blk, par, par], out_specs=[blk, blk],
        out_shape=[jax.ShapeDtypeStruct((t, ch), F32)] * 2,
        scratch_shapes=[pltpu.VMEM((1, cb), F32), pltpu.VMEM((1, cb), F32)],
        compiler_params=_params(("parallel", "parallel", "arbitrary")),
    )(bur, bui, ar, ai)


def _scan_bwd(gr, gi, xr, xi, ar, ai, seqs):
    t, ch = gr.shape
    l = t // seqs
    tl, cb = min(SCAN_ROWS, l), min(SCAN_COLS, ch)
    nl = l // tl

    def body(gr_ref, gi_ref, xr_ref, xi_ref, ar_ref, ai_ref, lr_ref, li_ref, dar_ref, dai_ref, cr, ci):
        @pl.when(pl.program_id(2) == 0)
        def _():
            cr[...] = jnp.zeros(cr.shape, F32)
            ci[...] = jnp.zeros(ci.shape, F32)
            dar_ref[...] = jnp.zeros(dar_ref.shape, F32)
            dai_ref[...] = jnp.zeros(dai_ref.shape, F32)

        a_r, a_i = ar_ref[...], ai_ref[...]

        def step(k, carry):
            lr, li, dar, dai = carry
            tt = tl - 1 - k
            xr_t = xr_ref[pl.ds(tt, 1), :]
            xi_t = xi_ref[pl.ds(tt, 1), :]
            dar = dar + lr * xr_t + li * xi_t
            dai = dai + li * xr_t - lr * xi_t
            nr = gr_ref[pl.ds(tt, 1), :] + a_r * lr + a_i * li
            ni = gi_ref[pl.ds(tt, 1), :] + a_r * li - a_i * lr
            lr_ref[pl.ds(tt, 1), :] = nr
            li_ref[pl.ds(tt, 1), :] = ni
            return nr, ni, dar, dai

        lr, li, dar, dai = lax.fori_loop(
            0, tl, step, (cr[...], ci[...], jnp.zeros((1, cb), F32), jnp.zeros((1, cb), F32)), unroll=8)
        cr[...] = lr
        ci[...] = li
        dar_ref[...] += dar
        dai_ref[...] += dai

    blk = pl.BlockSpec((tl, cb), lambda s, j, r: (s * nl + nl - 1 - r, j))
    par = pl.BlockSpec((1, cb), lambda s, j, r: (0, j))
    acc = pl.BlockSpec((None, 1, cb), lambda s, j, r: (s, 0, j))
    lr, li, dar, dai = pl.pallas_call(
        body, name="s5_scan_bwd", grid=(seqs, ch // cb, nl),
        in_specs=[blk, blk, blk, blk, par, par], out_specs=[blk, blk, acc, acc],
        out_shape=[jax.ShapeDtypeStruct((t, ch), F32)] * 2 + [jax.ShapeDtypeStruct((seqs, 1, ch), F32)] * 2,
        scratch_shapes=[pltpu.VMEM((1, cb), F32), pltpu.VMEM((1, cb), F32)],
        compiler_params=_params(("parallel", "parallel", "arbitrary")),
    )(gr, gi, xr, xi, ar, ai)
    return lr, li, dar, dai


XATT_BLOCK = 512


def _xatt_probs(qv, kv):
    s = lax.dot_general(qv, kv, _NT, preferred_element_type=F32) * (X_HEAD_DIM ** -0.5)
    e = jnp.exp(s - jnp.max(s, axis=-1, keepdims=True))
    return e / jnp.sum(e, axis=-1, keepdims=True)


def _xatt_fwd(q, k, kv, seqs):
    t = q.shape[0]
    tq = min(XATT_BLOCK, t // seqs)
    nq = t // seqs // tq

    def body(q_ref, k_ref, v_ref, o_ref):
        p = _xatt_probs(q_ref[...], k_ref[...])
        o_ref[...] = jnp.dot(p.astype(BF16), v_ref[...].astype(BF16), preferred_element_type=F32).astype(o_ref.dtype)

    qs = pl.BlockSpec((tq, X_HEAD_DIM), lambda b, h, i: (b * nq + i, h))
    return pl.pallas_call(
        body, name="xatt_fwd", grid=(seqs, N_X_HEADS, nq),
        in_specs=[qs, pl.BlockSpec((N_MEM, X_HEAD_DIM), lambda b, h, i: (b, h)),
                  pl.BlockSpec((N_MEM, X_HEAD_DIM), lambda b, h, i: (b, N_X_HEADS + h))],
        out_specs=qs, out_shape=jax.ShapeDtypeStruct(q.shape, BF16),
        compiler_params=_params(("parallel", "parallel", "parallel")),
    )(q, k, kv)


def _xatt_bwd(q, k, kv, do, seqs):
    t = q.shape[0]
    tq = min(XATT_BLOCK, t // seqs)
    nq = t // seqs // tq
    scale = X_HEAD_DIM ** -0.5

    def body(q_ref, k_ref, v_ref, do_ref, dq_ref, dk_ref, dv_ref):
        @pl.when(pl.program_id(2) == 0)
        def _():
            dk_ref[...] = jnp.zeros(dk_ref.shape, F32)
            dv_ref[...] = jnp.zeros(dv_ref.shape, F32)

        qv, kk = q_ref[...], k_ref[...]
        p = _xatt_probs(qv, kk)
        dob = do_ref[...].astype(BF16)
        dp = lax.dot_general(dob, v_ref[...].astype(BF16), _NT, preferred_element_type=F32)
        ds = p * (dp - jnp.sum(dp * p, axis=-1, keepdims=True))
        dsb = ds.astype(BF16)
        dq_ref[...] = jnp.dot(dsb, kk, preferred_element_type=F32) * scale
        dk_ref[...] += lax.dot_general(dsb, qv, _TN, preferred_element_type=F32) * scale
        dv_ref[...] += lax.dot_general(p.astype(BF16), dob, _TN, preferred_element_type=F32)

    qs = pl.BlockSpec((tq, X_HEAD_DIM), lambda b, h, i: (b * nq + i, h))
    ks = pl.BlockSpec((N_MEM, X_HEAD_DIM), lambda b, h, i: (b, h))
    return pl.pallas_call(
        body, name="xatt_bwd", grid=(seqs, N_X_HEADS, nq),
        in_specs=[qs, ks, pl.BlockSpec((N_MEM, X_HEAD_DIM), lambda b, h, i: (b, N_X_HEADS + h)), qs],
        out_specs=[qs, ks, ks],
        out_shape=[jax.ShapeDtypeStruct(q.shape, F32), jax.ShapeDtypeStruct(k.shape, F32),
                   jax.ShapeDtypeStruct(k.shape, F32)],
        compiler_params=_params(("parallel", "parallel", "arbitrary")),
    )(q, k, kv, do)


CONV_COLS = 256


def _shift_down(x, k, row):
    return jnp.where(row >= k, pltpu.roll(x, k, 0), 0.0)


def _shift_up(x, k, row):
    n = x.shape[0]
    return jnp.where(row < n - k, pltpu.roll(x, n - k, 0), 0.0)


def _conv_pre(g, w, b, row):
    return b + w[0:1, :] * _shift_down(g, 2, row) + w[1:2, :] * _shift_down(g, 1, row) + w[2:3, :] * g


def _convgate_fwd(gu, w, b, seqs):
    t = gu.shape[0]
    l = t // seqs
    nc = D_FF // CONV_COLS

    def body(g_ref, u_ref, w_ref, b_ref, o_ref):
        g = g_ref[...]
        row = lax.broadcasted_iota(jnp.int32, g.shape, 0)
        pre = _conv_pre(g, w_ref[...], b_ref[...], row)
        o_ref[...] = (pre * jax.nn.sigmoid(pre) * u_ref[...]).astype(o_ref.dtype)

    return pl.pallas_call(
        body, name="convgate_fwd", grid=(seqs, nc),
        in_specs=[pl.BlockSpec((l, CONV_COLS), lambda s, j: (s, j)), pl.BlockSpec((l, CONV_COLS), lambda s, j: (s, nc + j)),
                  pl.BlockSpec((3, CONV_COLS), lambda s, j: (0, j)), pl.BlockSpec((1, CONV_COLS), lambda s, j: (0, j))],
        out_specs=pl.BlockSpec((l, CONV_COLS), lambda s, j: (s, j)),
        out_shape=jax.ShapeDtypeStruct((t, D_FF), BF16),
        compiler_params=_params(("parallel", "parallel")),
    )(gu, gu, w, b)


def _convgate_bwd(gu, w, b, dact, seqs):
    t = gu.shape[0]
    l = t // seqs
    nc = D_FF // CONV_COLS

    def body(g_ref, u_ref, w_ref, b_ref, da_ref, dg_ref, du_ref, dw_ref, db_ref):
        @pl.when(pl.program_id(1) == 0)
        def _():
            dw_ref[...] = jnp.zeros(dw_ref.shape, F32)
            db_ref[...] = jnp.zeros(db_ref.shape, F32)

        g, wv, da = g_ref[...], w_ref[...], da_ref[...]
        row = lax.broadcasted_iota(jnp.int32, g.shape, 0)
        g1, g2 = _shift_down(g, 1, row), _shift_down(g, 2, row)
        pre = b_ref[...] + wv[0:1, :] * g2 + wv[1:2, :] * g1 + wv[2:3, :] * g
        sg = jax.nn.sigmoid(pre)
        silu = pre * sg
        du_ref[...] = (da * silu).astype(du_ref.dtype)
        dpre = da * u_ref[...] * (sg * (1.0 + pre * (1.0 - sg)))
        dg = wv[2:3, :] * dpre + wv[1:2, :] * _shift_up(dpre, 1, row) + wv[0:1, :] * _shift_up(dpre, 2, row)
        dg_ref[...] = dg.astype(dg_ref.dtype)
        dw_ref[0:1, :] += jnp.sum(dpre * g2, axis=0, keepdims=True)
        dw_ref[1:2, :] += jnp.sum(dpre * g1, axis=0, keepdims=True)
        dw_ref[2:3, :] += jnp.sum(dpre * g, axis=0, keepdims=True)
        db_ref[...] += jnp.sum(dpre, axis=0, keepdims=True)

    blk = lambda off: pl.BlockSpec((l, CONV_COLS), lambda j, s: (s, off + j))
    return pl.pallas_call(
        body, name="convgate_bwd", grid=(nc, seqs),
        in_specs=[blk(0), blk(nc), pl.BlockSpec((3, CONV_COLS), lambda j, s: (0, j)),
                  pl.BlockSpec((1, CONV_COLS), lambda j, s: (0, j)), blk(0)],
        out_specs=[blk(0), blk(0), pl.BlockSpec((3, CONV_COLS), lambda j, s: (0, j)),
                   pl.BlockSpec((1, CONV_COLS), lambda j, s: (0, j))],
        out_shape=[jax.ShapeDtypeStruct((t, D_FF), BF16), jax.ShapeDtypeStruct((t, D_FF), BF16),
                   jax.ShapeDtypeStruct((3, D_FF), F32), jax.ShapeDtypeStruct((1, D_FF), F32)],
        compiler_params=_params(("parallel", "arbitrary")),
    )(gu, gu, w, b, dact)


def _loss_head(h, target):
    t, d = h.shape
    tm = _pick(t, (256, 128, 8))

    def body(h_ref, t_ref, dh_ref, loss_ref):
        @pl.when(pl.program_id(0) == 0)
        def _():
            loss_ref[...] = jnp.zeros(loss_ref.shape, F32)

        e = h_ref[...] - t_ref[...]
        dh_ref[...] = e * (1.0 / d)
        loss_ref[...] += (0.5 / d) * jnp.sum(jnp.sum(e * e, axis=1, keepdims=True), axis=0, keepdims=True)

    blk = pl.BlockSpec((tm, d), lambda i: (i, 0))
    return pl.pallas_call(
        body, name="loss_head", grid=(t // tm,), in_specs=[blk, blk],
        out_specs=[blk, pl.BlockSpec((1, 1), lambda i: (0, 0))],
        out_shape=[jax.ShapeDtypeStruct((t, d), F32), jax.ShapeDtypeStruct((1, 1), F32)],
        compiler_params=_params(("arbitrary",)),
    )(h, target)


def _s5_discretise(a_re, a_im, log_dt, b_re, b_im):
    dt = jnp.exp(log_dt)[:, None]
    mag = jnp.exp(a_re * dt)
    lb_r = mag * jnp.cos(a_im * dt)
    lb_i = mag * jnp.sin(a_im * dt)
    den = a_re * a_re + a_im * a_im
    nr = lb_r - 1.0
    coef_r = (nr * a_re + lb_i * a_im) / den
    coef_i = (lb_i * a_re - nr * a_im) / den
    bb_r = coef_r[:, :, None] * b_re - coef_i[:, :, None] * b_im
    bb_i = coef_r[:, :, None] * b_im + coef_i[:, :, None] * b_re
    return lb_r, lb_i, bb_r, bb_i


S5_CHUNKS = 4
S5_PER = S5_GROUPS // S5_CHUNKS


def _blockdiag_in(bb):
    eye = jnp.eye(S5_PER, dtype=bb.dtype)
    return jnp.einsum("jgpc,gh->jgchp", bb.reshape(S5_CHUNKS, S5_PER, S5_STATE, S5_GROUP_CH), eye).reshape(
        S5_CHUNKS, S5_PER * S5_GROUP_CH, S5_PER * S5_STATE)


def _blockdiag_in_grad(d):
    eye = jnp.eye(S5_PER, dtype=d.dtype)
    return jnp.einsum("jgchp,gh->jgpc", d.reshape(S5_CHUNKS, S5_PER, S5_GROUP_CH, S5_PER, S5_STATE), eye).reshape(
        S5_GROUPS, S5_STATE, S5_GROUP_CH)


def _blockdiag_out(c):
    eye = jnp.eye(S5_PER, dtype=c.dtype)
    return jnp.einsum("jgcp,gh->jgphc", c.reshape(S5_CHUNKS, S5_PER, S5_GROUP_CH, S5_STATE), eye).reshape(
        S5_CHUNKS, S5_PER * S5_STATE, S5_PER * S5_GROUP_CH)


def _blockdiag_out_grad(d):
    eye = jnp.eye(S5_PER, dtype=d.dtype)
    return jnp.einsum("jgphc,gh->jgcp", d.reshape(S5_CHUNKS, S5_PER, S5_STATE, S5_PER, S5_GROUP_CH), eye).reshape(
        S5_GROUPS, S5_GROUP_CH, S5_STATE)


def _local_step(x3, mem3, target3, p, wb):
    seqs, l, d = x3.shape
    t = seqs * l
    x = x3.reshape(t, d)
    mem = mem3.reshape(seqs * N_MEM, d)
    target = target3.reshape(t, d)
    full = lambda a: (a, a.shape[1], 0, 0)

    s5_in = (p["s5_a_re"], p["s5_a_im"], p["s5_log_dt"], p["s5_b_re"], p["s5_b_im"])
    (lb_r, lb_i, bb_r, bb_i), s5_pull = jax.vjp(_s5_discretise, *s5_in)
    ar, ai = lb_r.reshape(1, S5_CH), lb_i.reshape(1, S5_CH)
    bbr_d, bbi_d = _blockdiag_in(bb_r).astype(BF16), _blockdiag_in(bb_i).astype(BF16)
    cr_d, ci_d = _blockdiag_out(p["s5_c_re"]).astype(BF16), (-_blockdiag_out(p["s5_c_im"])).astype(BF16)
    d_row = p["s5_d"].reshape(1, S5_WIDTH)

    w_in = wb["w_in"]
    w_qkv = w_in[:, :3 * FOX_WIDTH]
    w_uf = jnp.concatenate(
        [w_in[:, 3 * FOX_WIDTH + N_FOX_HEADS:], w_in[:, 3 * FOX_WIDTH:3 * FOX_WIDTH + N_FOX_HEADS],
         jnp.zeros((d, UF_COLS - S5_WIDTH - N_FOX_HEADS), w_in.dtype)], axis=1)

    hn1 = _rowwise(_rms, [full(x)], [p["norm_mix"]], [(d, d, 0, BF16)], "norm_mix_fwd")
    qkv = _mm(hn1, w_qkv, "nn", "in_qkv")
    uf = _mm(hn1, w_uf, "nn", "in_uf")

    bh = seqs * N_FOX_HEADS
    q_pair = (qkv, 128, 0, 1)
    k_pair = (qkv, 128, N_PAIRS, 1)
    gq2, gk2 = jnp.tile(p["fox_q_norm"], (1, 2)), jnp.tile(p["fox_k_norm"], (1, 2))
    pair_out = [(FOX_WIDTH, 128, 1, BF16)]
    qn = _rowwise(_rms_pair, [q_pair], [gq2], pair_out, "fox_qnorm_fwd", heads=N_PAIRS, tm=512)
    kn = _rowwise(_rms_pair, [k_pair], [gk2], pair_out, "fox_knorm_fwd", heads=N_PAIRS, tm=512)

    f_rows = uf[:, S5_WIDTH:S5_WIDTH + N_FOX_HEADS].reshape(seqs, l, N_FOX_HEADS).transpose(0, 2, 1).reshape(bh, l)
    f_bias = jnp.tile(p["fox_f_bias"].reshape(N_FOX_HEADS, 1), (seqs, 1))
    c2 = _forget_fwd(f_rows, f_bias).reshape(seqs * N_PAIRS, 2, l)
    fox, lse = _fox_fwd(qn, kn, qkv, c2, seqs)

    bur = _gmm(uf, bbr_d, "nn", "s5_bu_re")
    bui = _gmm(uf, bbi_d, "nn", "s5_bu_im")
    xr, xi = _scan_fwd(bur, bui, ar, ai, seqs)
    ys = _gmm(xi, ci_d, "nn", "s5_y_im", res=_gmm(xr, cr_d, "nn", "s5_y_re"))
    u_blk = (uf, S5_WIDTH, 0, 0)
    yg = _rowwise(_s5_act, [full(ys), u_blk], [d_row], [(S5_WIDTH, S5_WIDTH, 0, F32)], "s5_act_fwd")
    z = _mm(yg, wb["s5_w_glu"], "nn", "s5_glu")
    y2n = _rowwise(_s5_gate, [full(yg), full(z)], [p["s5_b_glu"], p["out_norm_s5"]],
                   [(S5_WIDTH, S5_WIDTH, 0, BF16)], "s5_gate_fwd")
    foxn = _rowwise(_rms, [full(fox)], [p["out_norm_fox"]], [(FOX_WIDTH, FOX_WIDTH, 0, BF16)], "fox_outnorm_fwd")
    mixed = jnp.concatenate([foxn, y2n], axis=1)
    h1 = _mm(mixed, wb["w_out"], "nn", "mix_out", res=x)

    hn2 = _rowwise(_rms, [full(h1)], [p["norm_cross"]], [(d, d, 0, BF16)], "norm_cross_fwd")
    mn = _rowwise(_rms, [full(mem)], [p["norm_mem"]], [(d, d, 0, BF16)], "norm_mem_fwd")
    xq_raw = _mm(hn2, wb["w_xq"], "nn", "x_q")
    kv = _mm(mn, wb["w_xkv"], "nn", "x_kv")
    xh = lambda a: (a, X_HEAD_DIM, 0, 1)
    xqn = _rowwise(_rms, [xh(xq_raw)], [p["xq_norm"]], [(d, X_HEAD_DIM, 1, BF16)], "x_qnorm_fwd", heads=N_X_HEADS)
    xkn = _rowwise(_rms, [xh(kv)], [p["xk_norm"]], [(d, X_HEAD_DIM, 1, BF16)], "x_knorm_fwd", heads=N_X_HEADS)
    xo = _xatt_fwd(xqn, xkn, kv, seqs)
    h2 = _mm(xo, wb["w_xo"], "nn", "x_out", res=h1)

    hn3 = _rowwise(_rms, [full(h2)], [p["norm_ffn"]], [(d, d, 0, BF16)], "norm_ffn_fwd")
    gu = _mm(hn3, wb["w_ffn_up"], "nn", "ffn_up")
    act = _convgate_fwd(gu, p["ffn_conv_w"], p["ffn_conv_b"], seqs)
    h3 = _mm(act, wb["w_ffn_down"], "nn", "ffn_down", res=h2)
    dh3, loss = _loss_head(h3, target)

    g = {}
    dact = _mm(dh3, wb["w_ffn_down"], "nt", "ffn_down_dx")
    g["w_ffn_down"] = _mm(act, dh3, "tn", "ffn_down_dw")
    dgate, dup, g["ffn_conv_w"], g["ffn_conv_b"] = _convgate_bwd(gu, p["ffn_conv_w"], p["ffn_conv_b"], dact, seqs)
    dgu = jnp.concatenate([dgate, dup], axis=1)
    dhn3 = _mm(dgu, wb["w_ffn_up"], "nt", "ffn_up_dx")
    g["w_ffn_up"] = _mm(hn3, dgu, "tn", "ffn_up_dw")
    (dh2,), (g["norm_ffn"],) = _rowwise_vjp(_rms, [full(h2)], [p["norm_ffn"]], [full(dhn3)], "norm_ffn_bwd",
                                            adds=[full(dh3)])

    dxo = _mm(dh2, wb["w_xo"], "nt", "x_out_dx")
    g["w_xo"] = _mm(xo, dh2, "tn", "x_out_dw")
    dxqn, dxkn, dxv = _xatt_bwd(xqn, xkn, kv, dxo, seqs)
    (dxq_raw,), (g["xq_norm"],) = _rowwise_vjp(_rms, [xh(xq_raw)], [p["xq_norm"]], [xh(dxqn)], "x_qnorm_bwd",
                                               heads=N_X_HEADS, row_dtypes=[BF16])
    (dxk_raw,), (g["xk_norm"],) = _rowwise_vjp(_rms, [xh(kv)], [p["xk_norm"]], [xh(dxkn)], "x_knorm_bwd",
                                               heads=N_X_HEADS, row_dtypes=[BF16])
    dkv = jnp.concatenate([dxk_raw, dxv.astype(BF16)], axis=1)
    dhn2 = _mm(dxq_raw, wb["w_xq"], "nt", "x_q_dx")
    g["w_xq"] = _mm(hn2, dxq_raw, "tn", "x_q_dw")
    dmn = _mm(dkv, wb["w_xkv"], "nt", "x_kv_dx")
    g["w_xkv"] = _mm(mn, dkv, "tn", "x_kv_dw")
    (dh1,), (g["norm_cross"],) = _rowwise_vjp(_rms, [full(h1)], [p["norm_cross"]], [full(dhn2)], "norm_cross_bwd",
                                              adds=[full(dh2)])
    _, (g["norm_mem"],) = _rowwise_vjp(_rms, [full(mem)], [p["norm_mem"]], [full(dmn)], "norm_mem_bwd",
                                       row_dtypes=[BF16])

    dmixed = _mm(dh1, wb["w_out"], "nt", "mix_out_dx")
    g["w_out"] = _mm(mixed, dh1, "tn", "mix_out_dw")
    (dfox,), (g["out_norm_fox"],) = _rowwise_vjp(_rms, [full(fox)], [p["out_norm_fox"]],
                                                 [(dmixed, FOX_WIDTH, 0, 0)], "fox_outnorm_bwd")
    (dyg_a, dz), (g["s5_b_glu"], g["out_norm_s5"]) = _rowwise_vjp(
        _s5_gate, [full(yg), full(z)], [p["s5_b_glu"], p["out_norm_s5"]], [(dmixed, S5_WIDTH, 1, 0)], "s5_gate_bwd",
        row_dtypes=[F32, BF16])
    dyg = _mm(dz, wb["s5_w_glu"], "nt", "s5_glu_dx", res=dyg_a)
    g["s5_w_glu"] = _mm(yg, dz, "tn", "s5_glu_dw")
    (dys, du_a), (dd_row,) = _rowwise_vjp(_s5_act, [full(ys), u_blk], [d_row], [full(dyg)], "s5_act_bwd",
                                          row_dtypes=[BF16, F32])
    g["s5_d"] = dd_row
    cin, cst = S5_PER * S5_GROUP_CH, S5_PER * S5_STATE
    dxr = _gmm(dys, cr_d, "nt", "s5_y_re_dx")
    dxi = _gmm(dys, ci_d, "nt", "s5_y_im_dx")
    dcr_d = _gmm_tn(xr, dys, cst, cin, S5_CHUNKS, "s5_y_re_dw")
    dci_d = _gmm_tn(xi, dys, cst, cin, S5_CHUNKS, "s5_y_im_dw")
    lam_r, lam_i, dar, dai = _scan_bwd(dxr, dxi, xr, xi, ar, ai, seqs)
    du_b = _gmm(lam_i, bbi_d, "nt", "s5_bu_im_dx", res=_gmm(lam_r, bbr_d, "nt", "s5_bu_re_dx"))
    dbbr_d = _gmm_tn(uf, lam_r, cin, cst, S5_CHUNKS, "s5_bu_re_dw")
    dbbi_d = _gmm_tn(uf, lam_i, cin, cst, S5_CHUNKS, "s5_bu_im_dw")
    d_lb_r = jnp.sum(dar, axis=0).reshape(S5_GROUPS, S5_STATE)
    d_lb_i = jnp.sum(dai, axis=0).reshape(S5_GROUPS, S5_STATE)
    g["s5_a_re"], g["s5_a_im"], g["s5_log_dt"], g["s5_b_re"], g["s5_b_im"] = s5_pull(
        (d_lb_r, d_lb_i, _blockdiag_in_grad(dbbr_d), _blockdiag_in_grad(dbbi_d)))
    g["s5_c_re"] = _blockdiag_out_grad(dcr_d)
    g["s5_c_im"] = -_blockdiag_out_grad(dci_d)

    dqn, dkn, dv, dc, dcq = _fox_bwd(qn, kn, qkv, c2, fox, dfox, lse, seqs)
    pair = lambda a: (a, 128, 0, 1)
    (dq_raw,), (dgq2,) = _rowwise_vjp(_rms_pair, [q_pair], [gq2], [pair(dqn)], "fox_qnorm_bwd", heads=N_PAIRS,
                                      tm=512, row_dtypes=[BF16])
    (dk_raw,), (dgk2,) = _rowwise_vjp(_rms_pair, [k_pair], [gk2], [pair(dkn)], "fox_knorm_bwd", heads=N_PAIRS,
                                      tm=512, row_dtypes=[BF16])
    g["fox_q_norm"] = dgq2[:, :HEAD_DIM] + dgq2[:, HEAD_DIM:]
    g["fox_k_norm"] = dgk2[:, :HEAD_DIM] + dgk2[:, HEAD_DIM:]
    dcq_rows = dcq.reshape(seqs, l, N_FOX_HEADS, HEAD_DIM)[:, :, :, 0].transpose(0, 2, 1).reshape(bh, l)
    df_rows, dfb = _forget_bwd(f_rows, f_bias, dc.reshape(bh, l) + dcq_rows)
    g["fox_f_bias"] = jnp.sum(dfb.reshape(seqs, N_FOX_HEADS), axis=0)
    df = df_rows.reshape(seqs, N_FOX_HEADS, l).transpose(0, 2, 1).reshape(t, N_FOX_HEADS)
    dqkv = jnp.concatenate([dq_raw, dk_raw, dv.astype(BF16)], axis=1)
    duf = jnp.concatenate([du_a + du_b, df, jnp.zeros((t, UF_COLS - S5_WIDTH - N_FOX_HEADS), F32)],
                          axis=1).astype(BF16)
    dhn1 = _mm(duf, w_uf, "nt", "in_uf_dx", res=_mm(dqkv, w_qkv, "nt", "in_qkv_dx"))
    dw_qkv = _mm(hn1, dqkv, "tn", "in_qkv_dw")
    dw_uf = _mm(hn1, duf, "tn", "in_uf_dw")
    g["w_in"] = jnp.concatenate([dw_qkv, dw_uf[:, S5_WIDTH:S5_WIDTH + N_FOX_HEADS], dw_uf[:, :S5_WIDTH]], axis=1)
    (dx,), (g["norm_mix"],) = _rowwise_vjp(_rms, [full(x)], [p["norm_mix"]], [full(dhn1)], "norm_mix_bwd",
                                           adds=[full(dh1)])
    return loss, dx.reshape(seqs, l, d), g


def _place():
    return lax.axis_index("x"), lax.axis_index("y"), lax.axis_index("c")


def _other_chips(x, y):
    return [(1 - x, y), (x, 1 - y), (1 - x, 1 - y)]


ANY = pl.BlockSpec(memory_space=pl.ANY)


def _gather_weights(shards, col_kind, taps):
    n = len(shards)

    def body(*refs):
        ins, tap_in, outs, tap_out = refs[:n], refs[n], refs[n + 1:2 * n + 1], refs[2 * n + 1]
        ici_send, ici_recv, d2d_send, d2d_recv, local_sems = refs[2 * n + 2:]
        x, y, c = _place()
        mine = 2 * x + y
        chips = _other_chips(x, y)

        def piece(a, s, h):
            r, cs = ins[a].shape
            hr = r // 2
            if col_kind[a]:
                return outs[a].at[pl.ds(pl.multiple_of(h * hr, 16), hr), pl.ds(pl.multiple_of(s * cs, 128), cs)]
            return outs[a].at[pl.ds(pl.multiple_of(s * r + h * hr, 16), hr), :]

        def own_half(a, h):
            hr = ins[a].shape[0] // 2
            return ins[a].at[pl.ds(pl.multiple_of(h * hr, 16), hr), :]

        started = []
        for a in range(n):
            for h in (0, 1):
                loc = pltpu.make_async_copy(own_half(a, h), piece(a, mine, h), local_sems.at[2 * a + h])
                loc.start()
                started.append(loc)
        tap_local = pltpu.make_async_copy(tap_in, tap_out.at[mine], local_sems.at[2 * n])
        tap_local.start()
        started.append(tap_local)
        sends = []
        for a in range(n):
            for j, (px, py) in enumerate(chips):
                cp = pltpu.make_async_remote_copy(
                    src_ref=own_half(a, c), dst_ref=piece(a, mine, c), send_sem=ici_send.at[3 * a + j],
                    recv_sem=ici_recv.at[3 * a + j], device_id=(px, py, c), device_id_type=MESH)
                cp.start()
                sends.append(cp)
        for j, (px, py) in enumerate(chips):
            cp = pltpu.make_async_remote_copy(
                src_ref=tap_in, dst_ref=tap_out.at[mine], send_sem=ici_send.at[3 * n + j],
                recv_sem=ici_recv.at[3 * n + j], device_id=(px, py, c), device_id_type=MESH)
            cp.start()
            sends.append(cp)
        for a in range(n):
            for j, (px, py) in enumerate(chips):
                got = piece(a, 2 * px + py, c)
                pltpu.make_async_remote_copy(
                    src_ref=got, dst_ref=got, send_sem=ici_send.at[3 * a + j], recv_sem=ici_recv.at[3 * a + j],
                    device_id=(px, py, c), device_id_type=MESH).wait_recv()
                fwd = pltpu.make_async_remote_copy(
                    src_ref=got, dst_ref=got, send_sem=d2d_send.at[3 * a + j], recv_sem=d2d_recv.at[3 * a + j],
                    device_id=(x, y, 1 - c), device_id_type=MESH)
                fwd.start()
                sends.append(fwd)
        for a in range(n):
            for j, (px, py) in enumerate(chips):
                other = piece(a, 2 * px + py, 1 - c)
                pltpu.make_async_remote_copy(
                    src_ref=other, dst_ref=other, send_sem=d2d_send.at[3 * a + j], recv_sem=d2d_recv.at[3 * a + j],
                    device_id=(x, y, 1 - c), device_id_type=MESH).wait_recv()
        for j, (px, py) in enumerate(chips):
            pltpu.make_async_remote_copy(
                src_ref=tap_in, dst_ref=tap_out.at[2 * px + py], send_sem=ici_send.at[3 * n + j],
                recv_sem=ici_recv.at[3 * n + j], device_id=(px, py, c), device_id_type=MESH).wait_recv()
        for cp in sends:
            cp.wait_send()
        for cp in started:
            cp.wait()

    def full_shape(a):
        r, cs = shards[a].shape
        return (r, 4 * cs) if col_kind[a] else (4 * r, cs)

    res = pl.pallas_call(
        body, name="gather_weights", in_specs=[ANY] * (n + 1), out_specs=[ANY] * (n + 1),
        out_shape=[jax.ShapeDtypeStruct(full_shape(a), shards[a].dtype) for a in range(n)]
        + [jax.ShapeDtypeStruct((4,) + taps.shape, taps.dtype)],
        scratch_shapes=[pltpu.SemaphoreType.DMA((3 * n + 3,)), pltpu.SemaphoreType.DMA((3 * n + 3,)),
                        pltpu.SemaphoreType.DMA((3 * n,)), pltpu.SemaphoreType.DMA((3 * n,)),
                        pltpu.SemaphoreType.DMA((2 * n + 1,))],
        compiler_params=pltpu.CompilerParams(has_side_effects=True),
    )(*shards, taps)
    return res[:n], res[n]


def _pair_exchange_halves(grads, col_kind):
    n = len(grads)

    def body(*refs):
        ins, outs = refs[:n], refs[n:2 * n]
        send_sems, recv_sems = refs[2 * n:]
        x, y, c = _place()
        copies = []
        for a in range(n):
            if col_kind[a]:
                hr = ins[a].shape[0] // 2
                src = ins[a].at[pl.ds(pl.multiple_of((1 - c) * hr, 8), hr), :]
            else:
                hr = ins[a].shape[1] // 2
                src = ins[a].at[:, pl.ds(pl.multiple_of((1 - c) * hr, 8), hr), :]
            cp = pltpu.make_async_remote_copy(
                src_ref=src, dst_ref=outs[a], send_sem=send_sems.at[a], recv_sem=recv_sems.at[a],
                device_id=(x, y, 1 - c), device_id_type=MESH)
            cp.start()
            copies.append(cp)
        for cp in copies:
            cp.wait()

    def half_shape(a):
        s = grads[a].shape
        return (s[0] // 2, s[1]) if col_kind[a] else (4, s[1] // 2, s[2])

    return pl.pallas_call(
        body, name="reduce_pair_exchange", in_specs=[ANY] * n, out_specs=[ANY] * n,
        out_shape=[jax.ShapeDtypeStruct(half_shape(a), grads[a].dtype) for a in range(n)],
        scratch_shapes=[pltpu.SemaphoreType.DMA((n,)), pltpu.SemaphoreType.DMA((n,))],
        compiler_params=pltpu.CompilerParams(has_side_effects=True),
    )(*grads)


def _chip_exchange(sums, col_kind):
    n = len(sums)

    def piece_shape(a):
        s = sums[a].shape
        return (s[0], s[1] // 4) if col_kind[a] else (s[1], s[2])

    def body(*refs):
        ins, outs = refs[:n], refs[n:2 * n]
        send_sems, recv_sems = refs[2 * n:]
        x, y, c = _place()
        copies = []
        for a in range(n):
            for j, (px, py) in enumerate(_other_chips(x, y)):
                if col_kind[a]:
                    cs = piece_shape(a)[1]
                    src = ins[a].at[:, pl.ds(pl.multiple_of((2 * px + py) * cs, 128), cs)]
                else:
                    src = ins[a].at[2 * px + py]
                cp = pltpu.make_async_remote_copy(
                    src_ref=src, dst_ref=outs[a].at[j], send_sem=send_sems.at[3 * a + j],
                    recv_sem=recv_sems.at[3 * a + j], device_id=(px, py, c), device_id_type=MESH)
                cp.start()
                copies.append(cp)
        for cp in copies:
            cp.wait()

    return pl.pallas_call(
        body, name="reduce_chip_exchange", in_specs=[ANY] * n, out_specs=[ANY] * n,
        out_shape=[jax.ShapeDtypeStruct((3,) + piece_shape(a), sums[a].dtype) for a in range(n)],
        scratch_shapes=[pltpu.SemaphoreType.DMA((3 * n,)), pltpu.SemaphoreType.DMA((3 * n,))],
        compiler_params=pltpu.CompilerParams(has_side_effects=True),
    )(*sums)


def _pair_gather_halves(halves):
    n = len(halves)

    def body(*refs):
        ins, outs = refs[:n], refs[n:2 * n]
        send_sems, recv_sems, local_sems = refs[2 * n:]
        x, y, c = _place()
        copies = []
        for a in range(n):
            hr = ins[a].shape[0]
            rows = outs[a].at[pl.ds(pl.multiple_of(c * hr, 8), hr), :]
            loc = pltpu.make_async_copy(ins[a], rows, local_sems.at[a])
            loc.start()
            cp = pltpu.make_async_remote_copy(
                src_ref=ins[a], dst_ref=rows, send_sem=send_sems.at[a], recv_sem=recv_sems.at[a],
                device_id=(x, y, 1 - c), device_id_type=MESH)
            cp.start()
            copies += [loc, cp]
        for cp in copies:
            cp.wait()

    return pl.pallas_call(
        body, name="reduce_pair_gather", in_specs=[ANY] * n, out_specs=[ANY] * n,
        out_shape=[jax.ShapeDtypeStruct((2 * s.shape[0], s.shape[1]), s.dtype) for s in halves],
        scratch_shapes=[pltpu.SemaphoreType.DMA((n,)), pltpu.SemaphoreType.DMA((n,)), pltpu.SemaphoreType.DMA((n,))],
        compiler_params=pltpu.CompilerParams(has_side_effects=True),
    )(*halves)


def _chip_sum(name, chip_sel, own, col, others):
    _, r, c = others.shape
    tr = _pick(r, (256, 128, 64, 32, 16))
    if col:
        own_spec = pl.BlockSpec((tr, c), lambda i, s: (i, s[0]))
    else:
        own_spec = pl.BlockSpec((None, tr, c), lambda i, s: (s[0], i, 0))
    specs = [own_spec] + [pl.BlockSpec((None, tr, c), lambda i, s, k=k: (k, i, 0)) for k in range(3)]

    def body(s_ref, own_ref, r0, r1, r2, o_ref):
        o_ref[...] = ((own_ref[...].astype(F32) + r0[...].astype(F32)) + r1[...].astype(F32)) + r2[...].astype(F32)

    return pl.pallas_call(
        body, name=name,
        grid_spec=pltpu.PrefetchScalarGridSpec(
            num_scalar_prefetch=1, grid=(r // tr,), in_specs=specs,
            out_specs=pl.BlockSpec((tr, c), lambda i, s: (i, 0))),
        out_shape=jax.ShapeDtypeStruct((r, c), F32),
        compiler_params=_params(("parallel",)),
    )(chip_sel, own, others, others, others)


def _pair_sum(name, c_sel, grad, recv, col):
    if col:
        r, c4 = grad.shape
        hr, c = r // 2, c4 // 4
    else:
        _, r, c = grad.shape
        hr = r // 2
    tr = _pick(hr, (256, 128, 64, 32, 16))
    nb = hr // tr

    def body(s_ref, g_ref, r_ref, o_ref):
        o_ref[...] = (g_ref[...] + r_ref[...]).astype(o_ref.dtype)

    if col:
        in_specs = [pl.BlockSpec((tr, c), lambda k, i, s: (s[0] * nb + i, k)), pl.BlockSpec((tr, c), lambda k, i, s: (i, k))]
        out_spec = pl.BlockSpec((tr, c), lambda k, i, s: (i, k))
    else:
        in_specs = [pl.BlockSpec((None, tr, c), lambda k, i, s: (k, s[0] * nb + i, 0)),
                    pl.BlockSpec((None, tr, c), lambda k, i, s: (k, i, 0))]
        out_spec = pl.BlockSpec((None, tr, c), lambda k, i, s: (k, i, 0))
    return pl.pallas_call(
        body, name=name,
        grid_spec=pltpu.PrefetchScalarGridSpec(num_scalar_prefetch=1, grid=(4, nb), in_specs=in_specs,
                                               out_specs=out_spec),
        out_shape=jax.ShapeDtypeStruct(recv.shape, BF16),
        compiler_params=_params(("parallel", "parallel")),
    )(c_sel, grad, recv)


def _allreduce_small(vals):
    n = len(vals)

    def body(*refs):
        ins, outs = refs[:n], refs[n:2 * n]
        recv = [refs[2 * n + k * n: 2 * n + (k + 1) * n] for k in range(3)]
        send_sems, recv_sems = refs[5 * n:]
        x, y, c = _place()
        for a in range(n):
            outs[a][...] = ins[a][...]
        for k, peer in enumerate([(x, y, 1 - c), (1 - x, y, c), (x, 1 - y, c)]):
            copies = []
            for a in range(n):
                cp = pltpu.make_async_remote_copy(
                    src_ref=outs[a], dst_ref=recv[k][a], send_sem=send_sems.at[k * n + a],
                    recv_sem=recv_sems.at[k * n + a], device_id=peer, device_id_type=MESH)
                cp.start()
                copies.append(cp)
            for cp in copies:
                cp.wait()
            for a in range(n):
                outs[a][...] = outs[a][...] + recv[k][a][...]

    vm = pl.BlockSpec(memory_space=pltpu.VMEM)
    return pl.pallas_call(
        body, name="allreduce_small", in_specs=[vm] * n, out_specs=[vm] * n,
        out_shape=[jax.ShapeDtypeStruct(v.shape, F32) for v in vals],
        scratch_shapes=[pltpu.VMEM(v.shape, F32) for _ in range(3) for v in vals]
        + [pltpu.SemaphoreType.DMA((3 * n,)), pltpu.SemaphoreType.DMA((3 * n,))],
        compiler_params=pltpu.CompilerParams(has_side_effects=True, vmem_limit_bytes=VMEM_LIMIT_BYTES),
    )(*vals)


def _adamw_math(w, g, m, v):
    m2 = ADAM_B1 * m + (1.0 - ADAM_B1) * g
    v2 = ADAM_B2 * v + (1.0 - ADAM_B2) * (g * g)
    m_hat = m2 / (1.0 - ADAM_B1 ** ADAM_STEP)
    v_hat = v2 / (1.0 - ADAM_B2 ** ADAM_STEP)
    delta = -ADAM_LR * (m_hat / (jnp.sqrt(v_hat) + ADAM_EPS) + ADAM_WD * w)
    return delta, m2, v2


def _adamw_big(name, w, g, m, v):
    r, c = w.shape
    tr = _pick(r, (256, 128, 64, 32, 16, 8))

    def body(w_ref, g_ref, m_ref, v_ref, go_ref, d_ref, mo_ref, vo_ref):
        gv = g_ref[...]
        d, m2, v2 = _adamw_math(w_ref[...], gv, m_ref[...], v_ref[...])
        go_ref[...] = gv
        d_ref[...] = d
        mo_ref[...] = m2
        vo_ref[...] = v2

    blk = pl.BlockSpec((tr, c), lambda i: (i, 0))
    return pl.pallas_call(
        body, name=name, grid=(r // tr,), in_specs=[blk] * 4, out_specs=[blk] * 4,
        out_shape=[jax.ShapeDtypeStruct((r, c), F32)] * 4, compiler_params=_params(("parallel",)),
    )(w, g, m, v)


def _adamw_small(ws, gs, ms, vs):
    n = len(ws)

    def body(*refs):
        w_r, g_r, m_r, v_r = refs[:n], refs[n:2 * n], refs[2 * n:3 * n], refs[3 * n:4 * n]
        o = refs[4 * n:]
        for a in range(n):
            gv = g_r[a][...]
            d, m2, v2 = _adamw_math(w_r[a][...], gv, m_r[a][...], v_r[a][...])
            o[a][...] = gv
            o[n + a][...] = d
            o[2 * n + a][...] = m2
            o[3 * n + a][...] = v2

    res = pl.pallas_call(
        body, name="adamw_small", out_shape=[jax.ShapeDtypeStruct(w.shape, F32) for _ in range(4) for w in ws],
        compiler_params=_params(),
    )(*ws, *gs, *ms, *vs)
    return res[:n], res[n:2 * n], res[2 * n:3 * n], res[3 * n:]


def _full_from_gathered(name, gathered):
    if name == "w_in":
        rows = gathered.shape[0] // 4
        return gathered.reshape(4, rows, gathered.shape[1]).transpose(1, 0, 2).reshape(rows, 4 * gathered.shape[1])
    return gathered


def _reduce_layout(name, full):
    if name in COL_KIND:
        return full
    if name == "w_in":
        rows, cols = full.shape
        return full.reshape(rows, 4, cols // 4).transpose(1, 0, 2)
    return full.reshape(4, full.shape[0] // 4, full.shape[1])


def kernel(x, mem, norm_mix, w_in, fox_q_norm, fox_k_norm, fox_f_bias, s5_a_re, s5_a_im, s5_log_dt, s5_b_re, s5_b_im, s5_c_re, s5_c_im, s5_d, s5_w_glu, s5_b_glu, out_norm_fox, out_norm_s5, w_out, norm_cross, norm_mem, w_xq, w_xkv, xq_norm, xk_norm, w_xo, norm_ffn, w_ffn_up, ffn_conv_w, ffn_conv_b, w_ffn_down, loss_target, m_norm_mix, m_w_in, m_fox_q_norm, m_fox_k_norm, m_fox_f_bias, m_s5_a_re, m_s5_a_im, m_s5_log_dt, m_s5_b_re, m_s5_b_im, m_s5_c_re, m_s5_c_im, m_s5_d, m_s5_w_glu, m_s5_b_glu, m_out_norm_fox, m_out_norm_s5, m_w_out, m_norm_cross, m_norm_mem, m_w_xq, m_w_xkv, m_xq_norm, m_xk_norm, m_w_xo, m_norm_ffn, m_w_ffn_up, m_ffn_conv_w, m_ffn_conv_b, m_w_ffn_down, v_norm_mix, v_w_in, v_fox_q_norm, v_fox_k_norm, v_fox_f_bias, v_s5_a_re, v_s5_a_im, v_s5_log_dt, v_s5_b_re, v_s5_b_im, v_s5_c_re, v_s5_c_im, v_s5_d, v_s5_w_glu, v_s5_b_glu, v_out_norm_fox, v_out_norm_s5, v_w_out, v_norm_cross, v_norm_mem, v_w_xq, v_w_xkv, v_xq_norm, v_xk_norm, v_w_xo, v_norm_ffn, v_w_ffn_up, v_ffn_conv_w, v_ffn_conv_b, v_w_ffn_down):
    given = dict(locals())
    w = {n: given[n] for n in WEIGHTS}
    m = {n: given["m_" + n] for n in WEIGHTS}
    v = {n: given["v_" + n] for n in WEIGHTS}
    xi, yi, ci = _place()
    chip = (2 * xi + yi).astype(jnp.int32)

    col_kind = [n in COL_KIND for n in BIG]
    gathered, taps = _gather_weights([w[n][0].astype(BF16) for n in BIG], col_kind, w["ffn_conv_w"][0])
    wb = {n: _full_from_gathered(n, gathered[k]) for k, n in enumerate(BIG)}
    conv_w = taps.transpose(1, 0, 2).reshape(3, D_FF)

    p = {n: w[n][0] for n in SMALL}
    p["ffn_conv_w"] = conv_w
    for n in ("norm_mix", "fox_q_norm", "fox_k_norm", "fox_f_bias", "s5_b_glu", "out_norm_fox", "out_norm_s5",
              "norm_cross", "norm_mem", "xq_norm", "xk_norm", "norm_ffn", "ffn_conv_b"):
        p[n] = p[n].reshape(1, -1)
    loss, grad_x, g = _local_step(x, mem, loss_target, p, wb)

    small_names = list(SMALL) + ["ffn_conv_w"]
    small_vals = [g[n].reshape(w[n].shape if n != "ffn_conv_w" else (1, 3, D_FF)) for n in small_names] + [loss]
    reduced = _allreduce_small(small_vals)
    loss_all = reduced[-1].reshape(())
    conv_w_grad = lax.dynamic_slice_in_dim(reduced[-2], chip * (D_FF // 4), D_FF // 4, axis=2)
    sg, sd, sm, sv = _adamw_small(
        [w[n] for n in small_names], list(reduced[:len(SMALL)]) + [conv_w_grad],
        [m[n] for n in small_names], [v[n] for n in small_names])
    out_g = dict(zip(small_names, sg))
    out_d = dict(zip(small_names, sd))
    out_m = dict(zip(small_names, sm))
    out_v = dict(zip(small_names, sv))

    c_sel = ci.astype(jnp.int32).reshape(1)
    chip_sel = chip.reshape(1)
    grads = [_reduce_layout(n, g[n]) for n in BIG]
    from_sibling = _pair_exchange_halves(grads, col_kind)
    pair_sums = [_pair_sum("reduce_pair_sum_" + n, c_sel, gr, rv, ck)
                 for n, gr, rv, ck in zip(BIG, grads, from_sibling, col_kind)]
    from_chips = _chip_exchange(pair_sums, col_kind)
    halves = [_chip_sum("reduce_chip_sum_" + n, chip_sel, ps, ck, fc)
              for n, ps, fc, ck in zip(BIG, pair_sums, from_chips, col_kind)]
    shard_grads = _pair_gather_halves(halves)
    for n, sgr in zip(BIG, shard_grads):
        go, d, m2, v2 = _adamw_big("adamw_" + n, w[n][0], sgr, m[n][0], v[n][0])
        out_g[n], out_d[n], out_m[n], out_v[n] = go[None], d[None], m2[None], v2[None]

    return (loss_all, grad_x, *[out_g[n] for n in WEIGHTS], *[out_d[n] for n in WEIGHTS],
            *[out_m[n] for n in WEIGHTS], *[out_v[n] for n in WEIGHTS])
```

```python
import functools
import math

import jax
import jax.numpy as jnp
from jax import lax
from jax.experimental import pallas as pl
from jax.experimental.pallas import tpu as pltpu

F32 = jnp.float32
BF16 = jnp.bfloat16

D_MODEL = 1024
FOX_WIDTH = 512
HEAD_DIM = 64
N_FOX_HEADS = 8
S5_WIDTH = 512
S5_GROUP_CH = 16
S5_GROUPS = 32
S5_STATE = 64
S5_CH = S5_GROUPS * S5_STATE
N_X_HEADS = 4
X_HEAD_DIM = 256
N_MEM = 256
D_FF = 2816
UF_COLS = 640
EPS = 1e-6
ADAM_LR = 0.001
ADAM_B1 = 0.9
ADAM_B2 = 0.999
ADAM_EPS = 1e-08
ADAM_WD = 0.01
ADAM_STEP = 10

VMEM_LIMIT_BYTES = 56 * 1024 * 1024
MM_BLOCK_BYTES = 6 * 1024 * 1024
MESH = pl.DeviceIdType.MESH

BIG = ("w_in", "s5_w_glu", "w_out", "w_xq", "w_xkv", "w_xo", "w_ffn_up", "w_ffn_down")
COL_KIND = ("w_xkv", "w_ffn_up")
SMALL = ("norm_mix", "fox_q_norm", "fox_k_norm", "fox_f_bias", "s5_a_re", "s5_a_im", "s5_log_dt",
         "s5_b_re", "s5_b_im", "s5_c_re", "s5_c_im", "s5_d", "s5_b_glu", "out_norm_fox", "out_norm_s5",
         "norm_cross", "norm_mem", "xq_norm", "xk_norm", "norm_ffn", "ffn_conv_b")
WEIGHTS = ("norm_mix", "w_in", "fox_q_norm", "fox_k_norm", "fox_f_bias", "s5_a_re", "s5_a_im", "s5_log_dt",
           "s5_b_re", "s5_b_im", "s5_c_re", "s5_c_im", "s5_d", "s5_w_glu", "s5_b_glu", "out_norm_fox",
           "out_norm_s5", "w_out", "norm_cross", "norm_mem", "w_xq", "w_xkv", "xq_norm", "xk_norm", "w_xo",
           "norm_ffn", "w_ffn_up", "ffn_conv_w", "ffn_conv_b", "w_ffn_down")


def _params(sem=None):
    return pltpu.CompilerParams(dimension_semantics=sem, vmem_limit_bytes=VMEM_LIMIT_BYTES)


def _pick(n, cands):
    for c in cands:
        if n % c == 0:
            return c
    return n


_DIMS = {"nn": (((1,), (0,)), ((), ())), "nt": (((1,), (1,)), ((), ())), "tn": (((0,), (0,)), ((), ()))}


def _mm(a, b, mode, name, out_dtype=F32, res=None):
    if mode == "nn":
        (m, k), (k2, n) = a.shape, b.shape
    elif mode == "nt":
        (m, k), (n, k2) = a.shape, b.shape
    else:
        (k, m), (k2, n) = a.shape, b.shape
    assert k == k2, (name, a.shape, b.shape)

    def fit(dim, itemsize):
        for c in (512, 256, 128):
            if dim % c == 0 and c * k * itemsize <= MM_BLOCK_BYTES:
                return c
        return 128 if dim % 128 == 0 else dim

    tm, tn = fit(m, a.dtype.itemsize), fit(n, b.dtype.itemsize)
    a_spec = pl.BlockSpec((k, tm), lambda i, j: (0, i)) if mode == "tn" else pl.BlockSpec((tm, k), lambda i, j: (i, 0))
    b_spec = pl.BlockSpec((tn, k), lambda i, j: (j, 0)) if mode == "nt" else pl.BlockSpec((k, tn), lambda i, j: (0, j))
    o_spec = pl.BlockSpec((tm, tn), lambda i, j: (i, j))
    dims = _DIMS[mode]
    has_res = res is not None

    def body(*refs):
        a_ref, b_ref = refs[0], refs[1]
        o_ref = refs[-1]
        acc = lax.dot_general(a_ref[...].astype(BF16), b_ref[...].astype(BF16), dims, preferred_element_type=F32)
        if has_res:
            acc = acc + refs[2][...].astype(F32)
        o_ref[...] = acc.astype(o_ref.dtype)

    return pl.pallas_call(
        body, name=name, grid=(m // tm, n // tn),
        in_specs=[a_spec, b_spec] + ([o_spec] if has_res else []),
        out_specs=o_spec, out_shape=jax.ShapeDtypeStruct((m, n), out_dtype),
        compiler_params=_params(("parallel", "parallel")),
    )(*((a, b, res) if has_res else (a, b)))


def _gmm_tn(a, b, m, n, groups, name):
    t = a.shape[0]

    def body(a_ref, b_ref, o_ref):
        o_ref[...] = lax.dot_general(a_ref[...].astype(BF16), b_ref[...].astype(BF16), _DIMS["tn"],
                                     preferred_element_type=F32)

    return pl.pallas_call(
        body, name=name, grid=(groups,),
        in_specs=[pl.BlockSpec((t, m), lambda j: (0, j)), pl.BlockSpec((t, n), lambda j: (0, j))],
        out_specs=pl.BlockSpec((None, m, n), lambda j: (j, 0, 0)),
        out_shape=jax.ShapeDtypeStruct((groups, m, n), F32), compiler_params=_params(("parallel",)),
    )(a, b)


def _gmm(a, b, mode, name, out_dtype=F32, res=None):
    g = b.shape[0]
    dims = _DIMS[mode]
    has_res = res is not None

    def body(*refs):
        acc = lax.dot_general(refs[0][...].astype(BF16), refs[1][...].astype(BF16), dims, preferred_element_type=F32)
        if has_res:
            acc = acc + refs[2][...].astype(F32)
        refs[-1][...] = acc.astype(refs[-1].dtype)

    k, n = (b.shape[1], b.shape[2]) if mode == "nn" else (b.shape[2], b.shape[1])
    m = a.shape[0]
    tm = _pick(m, (512, 256, 128))
    o_spec = pl.BlockSpec((tm, n), lambda i, j: (i, j))
    return pl.pallas_call(
        body, name=name, grid=(m // tm, g),
        in_specs=[pl.BlockSpec((tm, k), lambda i, j: (i, j)), pl.BlockSpec((None,) + b.shape[1:], lambda i, j: (j, 0, 0))]
        + ([o_spec] if has_res else []),
        out_specs=o_spec, out_shape=jax.ShapeDtypeStruct((m, g * n), out_dtype),
        compiler_params=_params(("parallel", "parallel")),
    )(*((a, b, res) if has_res else (a, b)))


def _row_spec(tm, bc, off, step):
    return pl.BlockSpec((tm, bc), lambda i, h: (i, off + step * h))


def _rowwise(fn, rows, pars, outs, name, heads=1, tm=256):
    t = rows[0][0].shape[0]
    tm = _pick(t, (tm, 256, 128, 64, 8))
    nr, npar = len(rows), len(pars)

    def body(*refs):
        vals = [r[...].astype(F32) for r in refs[:nr + npar]]
        res = fn(*vals)
        if not isinstance(res, (tuple, list)):
            res = (res,)
        for o_ref, v in zip(refs[nr + npar:], res):
            o_ref[...] = v.astype(o_ref.dtype)

    in_specs = [_row_spec(tm, bc, off, st) for (_, bc, off, st) in rows]
    in_specs += [pl.BlockSpec(p.shape, lambda i, h: (0, 0)) for p in pars]
    out_specs = [_row_spec(tm, bc, 0, st) for (_, bc, st, _) in outs]
    out_shape = [jax.ShapeDtypeStruct((t, c), dt) for (c, _, _, dt) in outs]
    res = pl.pallas_call(
        body, name=name, grid=(t // tm, heads), in_specs=in_specs, out_specs=out_specs, out_shape=out_shape,
        compiler_params=_params(("parallel", "parallel")),
    )(*[r[0] for r in rows], *pars)
    return res[0] if len(res) == 1 else res


def _rowwise_vjp(fn, rows, pars, cts, name, heads=1, adds=None, tm=256, row_dtypes=None):
    t = rows[0][0].shape[0]
    tm = _pick(t, (tm, 256, 128, 64, 8))
    nr, npar, nct = len(rows), len(pars), len(cts)
    adds = adds or [None] * nr
    add_list = [a for a in adds if a is not None]
    row_dtypes = row_dtypes or [F32] * nr

    def body(*refs):
        i, h = pl.program_id(0), pl.program_id(1)
        p = 0
        row_v = [r[...].astype(F32) for r in refs[p:p + nr]]; p += nr
        par_v = [r[...].astype(F32) for r in refs[p:p + npar]]; p += npar
        ct_v = [r[...].astype(F32) for r in refs[p:p + nct]]; p += nct
        add_refs = refs[p:p + len(add_list)]; p += len(add_list)
        drow_refs = refs[p:p + nr]; p += nr
        dpar_refs = refs[p:p + npar]

        def wrapped(*a):
            r = fn(*a)
            return tuple(r) if isinstance(r, (tuple, list)) else (r,)

        _, pull = jax.vjp(wrapped, *row_v, *par_v)
        grads = pull(tuple(ct_v))
        ai = 0
        for k in range(nr):
            g = grads[k]
            if adds[k] is not None:
                g = g + add_refs[ai][...].astype(F32)
                ai += 1
            drow_refs[k][...] = g.astype(drow_refs[k].dtype)

        @pl.when((i == 0) & (h == 0))
        def _():
            for r in dpar_refs:
                r[...] = jnp.zeros(r.shape, r.dtype)

        for k in range(npar):
            dpar_refs[k][...] += grads[nr + k]

    in_specs = [_row_spec(tm, bc, off, st) for (_, bc, off, st) in rows]
    in_specs += [pl.BlockSpec(q.shape, lambda i, h: (0, 0)) for q in pars]
    in_specs += [_row_spec(tm, bc, off, st) for (_, bc, off, st) in cts]
    in_specs += [_row_spec(tm, bc, off, st) for (_, bc, off, st) in add_list]
    out_specs = [_row_spec(tm, bc, 0, st) for (_, bc, _, st) in rows]
    out_specs += [pl.BlockSpec(q.shape, lambda i, h: (0, 0)) for q in pars]
    out_shape = [jax.ShapeDtypeStruct((t, bc * (heads if st else 1)), dt) for (_, bc, _, st), dt in zip(rows, row_dtypes)]
    out_shape += [jax.ShapeDtypeStruct(q.shape, F32) for q in pars]
    res = pl.pallas_call(
        body, name=name, grid=(t // tm, heads), in_specs=in_specs, out_specs=out_specs, out_shape=out_shape,
        compiler_params=_params(("arbitrary", "arbitrary")),
    )(*[r[0] for r in rows], *pars, *[c[0] for c in cts], *[a[0] for a in add_list])
    return list(res[:nr]), list(res[nr:])


def _rms(x, g):
    return x * lax.rsqrt(jnp.mean(x * x, axis=-1, keepdims=True) + EPS) * g


def _rms_pair(x, g):
    left = lax.broadcasted_iota(jnp.int32, x.shape, 1) < HEAD_DIM
    x2 = x * x
    ms_a = jnp.sum(jnp.where(left, x2, 0.0), axis=-1, keepdims=True) * (1.0 / HEAD_DIM)
    ms_b = jnp.sum(jnp.where(left, 0.0, x2), axis=-1, keepdims=True) * (1.0 / HEAD_DIM)
    return x * lax.rsqrt(jnp.where(left, ms_a, ms_b) + EPS) * g


def _gelu(x):
    return 0.5 * x * (1.0 + jnp.tanh(math.sqrt(2.0 / math.pi) * (x + 0.044715 * (x * x * x))))


def _s5_act(ys, u, d):
    return _gelu(ys + d * u)


def _s5_gate(yg, z, b, g):
    return _rms(yg * jax.nn.sigmoid(z + b), g)


def _lane_cumsum(x, reverse):
    n = x.shape[-1]
    lane = lax.broadcasted_iota(jnp.int32, x.shape, 1)
    k = 1
    while k < n:
        if reverse:
            x = x + jnp.where(lane < n - k, pltpu.roll(x, n - k, 1), 0.0)
        else:
            x = x + jnp.where(lane >= k, pltpu.roll(x, k, 1), 0.0)
        k *= 2
    return x


def _log_sigmoid(z):
    return jnp.minimum(z, 0.0) - jnp.log(1.0 + jnp.exp(-jnp.abs(z)))


def _forget_fwd(f, bias):
    def body(f_ref, b_ref, c_ref):
        c_ref[...] = _lane_cumsum(_log_sigmoid(f_ref[...] + b_ref[...]), False)

    return pl.pallas_call(body, name="forget_fwd", out_shape=jax.ShapeDtypeStruct(f.shape, F32),
                          compiler_params=_params())(f, bias)


def _forget_bwd(f, bias, dc):
    def body(f_ref, b_ref, dc_ref, df_ref, db_ref):
        dlog = _lane_cumsum(dc_ref[...], True)
        df = dlog * jax.nn.sigmoid(-(f_ref[...] + b_ref[...]))
        df_ref[...] = df
        db_ref[...] = jnp.sum(df, axis=1, keepdims=True)

    return pl.pallas_call(body, name="forget_bwd",
                          out_shape=(jax.ShapeDtypeStruct(f.shape, F32), jax.ShapeDtypeStruct(bias.shape, F32)),
                          compiler_params=_params())(f, bias, dc)


FOX_BLOCK = 256
_NT = _DIMS["nt"]
_TN = _DIMS["tn"]


N_PAIRS = N_FOX_HEADS // 2
V_BLOCK0 = 2 * N_PAIRS


def _left_lanes(shape):
    return lax.broadcasted_iota(jnp.int32, shape, 1) < HEAD_DIM


def _fox_fwd(qn, kn, qkv, c2, seqs):
    t = qn.shape[0]
    l = t // seqs
    tb = min(FOX_BLOCK, l)
    nb = l // tb
    scale = HEAD_DIM ** -0.5

    def body(q_ref, k_ref, v_ref, c_ref, o_ref, lse_ref):
        i = pl.program_id(2)
        left = _left_lanes((tb, 128))
        q2 = (q_ref[...].astype(F32) * scale).astype(BF16)
        zero = jnp.zeros_like(q2)
        qs = (jnp.where(left, q2, zero), jnp.where(left, zero, q2))
        causal = lax.broadcasted_iota(jnp.int32, (tb, tb), 1) <= lax.broadcasted_iota(jnp.int32, (tb, tb), 0)

        def tile(j, carry, masked):
            off = pl.multiple_of(j * tb, tb)
            k2 = k_ref[pl.ds(off, tb), :]
            v2 = v_ref[pl.ds(off, tb), :].astype(BF16)
            vs = (jnp.where(left, v2, zero), jnp.where(left, zero, v2))
            (ma, sa), (mb, sb), acc = carry
            new, alphas, pv = [], [], []
            for h, (m, s_sum) in enumerate(((ma, sa), (mb, sb))):
                s = lax.dot_general(qs[h], k2, _NT, preferred_element_type=F32) - c_ref[h:h + 1, pl.ds(off, tb)]
                if masked:
                    s = jnp.where(causal, s, -jnp.inf)
                m_new = jnp.maximum(m, jnp.max(s, axis=-1, keepdims=True))
                alpha = jnp.exp(m - m_new)
                p = jnp.exp(s - m_new)
                new.append((m_new, alpha * s_sum + jnp.sum(p, axis=-1, keepdims=True)))
                alphas.append(alpha)
                pv.append(jnp.dot(p.astype(BF16), vs[h], preferred_element_type=F32))
            acc = jnp.where(left, alphas[0], alphas[1]) * acc + pv[0] + pv[1]
            return new[0], new[1], acc

        stat = (jnp.full((tb, 1), -jnp.inf, F32), jnp.zeros((tb, 1), F32))
        carry = lax.fori_loop(0, i, lambda j, c: tile(j, c, False), (stat, stat, jnp.zeros((tb, 128), F32)))
        (ma, sa), (mb, sb), acc = tile(i, carry, True)
        o_ref[...] = acc / jnp.where(left, sa, sb)
        lse_ref[...] = jnp.where(left, ma + jnp.log(sa), mb + jnp.log(sb))

    qblk = pl.BlockSpec((tb, 128), lambda b, hp, i: (b * nb + i, hp))
    return pl.pallas_call(
        body, name="fox_fwd", grid=(seqs, N_PAIRS, nb),
        in_specs=[qblk, pl.BlockSpec((l, 128), lambda b, hp, i: (b, hp)),
                  pl.BlockSpec((l, 128), lambda b, hp, i: (b, V_BLOCK0 + hp)),
                  pl.BlockSpec((None, 2, l), lambda b, hp, i: (b * N_PAIRS + hp, 0, 0))],
        out_specs=[qblk, qblk],
        out_shape=[jax.ShapeDtypeStruct((t, FOX_WIDTH), F32), jax.ShapeDtypeStruct((t, FOX_WIDTH), F32)],
        compiler_params=_params(("parallel", "parallel", "parallel")),
    )(qn, kn, qkv, c2)


def _fox_bwd(qn, kn, qkv, c2, o, do, lse, seqs):
    t = qn.shape[0]
    l = t // seqs
    tb = min(FOX_BLOCK, l)
    nb = l // tb
    scale = HEAD_DIM ** -0.5

    def body(q_ref, k_ref, v_ref, c_ref, o_ref, do_ref, lse_ref, dq_ref, dk_ref, dv_ref, dc_ref, dcq_ref):
        dq_ref[...] = jnp.zeros(dq_ref.shape, F32)
        dcq_ref[...] = jnp.zeros(dcq_ref.shape, F32)
        left = _left_lanes((tb, 128))
        zero = jnp.zeros((tb, 128), BF16)
        split = lambda a: (jnp.where(left, a, zero), jnp.where(left, zero, a))
        causal = lax.broadcasted_iota(jnp.int32, (tb, tb), 1) <= lax.broadcasted_iota(jnp.int32, (tb, tb), 0)

        def kv_block(j, _):
            koff = pl.multiple_of(j * tb, tb)
            k2 = k_ref[pl.ds(koff, tb), :]
            ks = split(k2)
            v2 = v_ref[pl.ds(koff, tb), :].astype(BF16)

            def q_block(i, carry, masked):
                dk, dv, dca, dcb = carry
                qoff = pl.multiple_of(i * tb, tb)
                qs = split((q_ref[pl.ds(qoff, tb), :].astype(F32) * scale).astype(BF16))
                dov = do_ref[pl.ds(qoff, tb), :]
                prod = dov * o_ref[pl.ds(qoff, tb), :]
                dos = split(dov.astype(BF16))
                dq_new, rowsums, dcs = [], [], []
                for h in (0, 1):
                    mine = left if h == 0 else jnp.logical_not(left)
                    delta = jnp.sum(jnp.where(mine, prod, 0.0), axis=-1, keepdims=True)
                    s = lax.dot_general(qs[h], k2, _NT, preferred_element_type=F32) - c_ref[h:h + 1, pl.ds(koff, tb)]
                    p = jnp.exp(s - lse_ref[pl.ds(qoff, tb), h * HEAD_DIM:h * HEAD_DIM + 1])
                    if masked:
                        p = jnp.where(causal, p, 0.0)
                    dp = lax.dot_general(dos[h], v2, _NT, preferred_element_type=F32)
                    ds = p * (dp - delta)
                    dsb = ds.astype(BF16)
                    dv = dv + lax.dot_general(p.astype(BF16), dos[h], _TN, preferred_element_type=F32)
                    dk = dk + lax.dot_general(dsb, qs[h], _TN, preferred_element_type=F32)
                    dq_new.append(jnp.dot(dsb, ks[h], preferred_element_type=F32))
                    rowsums.append(jnp.sum(ds, axis=-1, keepdims=True))
                    dcs.append(jnp.sum(ds, axis=0, keepdims=True))
                dq_ref[pl.ds(qoff, tb), :] += (dq_new[0] + dq_new[1]) * scale
                dcq_ref[pl.ds(qoff, tb), :] += jnp.where(left, rowsums[0], rowsums[1])
                return dk, dv, dca - dcs[0], dcb - dcs[1]

            init = (jnp.zeros((tb, 128), F32), jnp.zeros((tb, 128), F32), jnp.zeros((1, tb), F32),
                    jnp.zeros((1, tb), F32))
            carry = q_block(j, init, True)
            dk, dv, dca, dcb = lax.fori_loop(j + 1, nb, lambda i, c: q_block(i, c, False), carry)
            dk_ref[pl.ds(koff, tb), :] = dk
            dv_ref[pl.ds(koff, tb), :] = dv
            dc_ref[0:1, pl.ds(koff, tb)] = dca
            dc_ref[1:2, pl.ds(koff, tb)] = dcb
            return 0

        lax.fori_loop(0, nb, kv_block, 0)

    blk = pl.BlockSpec((l, 128), lambda b, hp: (b, hp))
    crow = pl.BlockSpec((None, 2, l), lambda b, hp: (b * N_PAIRS + hp, 0, 0))
    wide = jax.ShapeDtypeStruct((t, FOX_WIDTH), F32)
    return pl.pallas_call(
        body, name="fox_bwd", grid=(seqs, N_PAIRS),
        in_specs=[blk, blk, pl.BlockSpec((l, 128), lambda b, hp: (b, V_BLOCK0 + hp)), crow, blk, blk, blk],
        out_specs=[blk, blk, blk, crow, blk],
        out_shape=[wide, wide, wide, jax.ShapeDtypeStruct((seqs * N_PAIRS, 2, l), F32), wide],
        compiler_params=_params(("parallel", "parallel")),
    )(qn, kn, qkv, c2, o, do, lse)


SCAN_ROWS = 256
SCAN_COLS = 1024


def _scan_fwd(bur, bui, ar, ai, seqs):
    t, ch = bur.shape
    l = t // seqs
    tl, cb = min(SCAN_ROWS, l), min(SCAN_COLS, ch)
    nl = l // tl

    def body(br_ref, bi_ref, ar_ref, ai_ref, xr_ref, xi_ref, cr, ci):
        @pl.when(pl.program_id(2) == 0)
        def _():
            cr[...] = jnp.zeros(cr.shape, F32)
            ci[...] = jnp.zeros(ci.shape, F32)

        a_r, a_i = ar_ref[...], ai_ref[...]

        def step(tt, carry):
            xr, xi = carry
            nr = a_r * xr - a_i * xi + br_ref[pl.ds(tt, 1), :]
            ni = a_r * xi + a_i * xr + bi_ref[pl.ds(tt, 1), :]
            xr_ref[pl.ds(tt, 1), :] = nr
            xi_ref[pl.ds(tt, 1), :] = ni
            return nr, ni

        xr, xi = lax.fori_loop(0, tl, step, (cr[...], ci[...]), unroll=8)
        cr[...] = xr
        ci[...] = xi

    blk = pl.BlockSpec((tl, cb), lambda s, j, r: (s * nl + r, j))
    par = pl.BlockSpec((1, cb), lambda s, j, r: (0, j))
    return pl.pallas_call(
        body, name="s5_scan_fwd", grid=(seqs, ch // cb, nl),
        in_specs=[blk, blk, par, par], out_specs=[blk, blk],
        out_shape=[jax.ShapeDtypeStruct((t, ch), F32)] * 2,
        scratch_shapes=[pltpu.VMEM((1, cb), F32), pltpu.VMEM((1, cb), F32)],
        compiler_params=_params(("parallel", "parallel", "arbitrary")),
    )(bur, bui, ar, ai)


def _scan_bwd(gr, gi, xr, xi, ar, ai, seqs):
    t, ch = gr.shape
    l = t // seqs
    tl, cb = min(SCAN_ROWS, l), min(SCAN_COLS, ch)
    nl = l // tl

    def body(gr_ref, gi_ref, xr_ref, xi_ref, ar_ref, ai_ref, lr_ref, li_ref, dar_ref, dai_ref, cr, ci):
        @pl.when(pl.program_id(2) == 0)
        def _():
            cr[...] = jnp.zeros(cr.shape, F32)
            ci[...] = jnp.zeros(ci.shape, F32)
            dar_ref[...] = jnp.zeros(dar_ref.shape, F32)
            dai_ref[...] = jnp.zeros(dai_ref.shape, F32)

        a_r, a_i = ar_ref[...], ai_ref[...]

        def step(k, carry):
            lr, li, dar, dai = carry
            tt = tl - 1 - k
            xr_t = xr_ref[pl.ds(tt, 1), :]
            xi_t = xi_ref[pl.ds(tt, 1), :]
            dar = dar + lr * xr_t + li * xi_t
            dai = dai + li * xr_t - lr * xi_t
            nr = gr_ref[pl.ds(tt, 1), :] + a_r * lr + a_i * li
            ni = gi_ref[pl.ds(tt, 1), :] + a_r * li - a_i * lr
            lr_ref[pl.ds(tt, 1), :] = nr
            li_ref[pl.ds(tt, 1), :] = ni
            return nr, ni, dar, dai

        lr, li, dar, dai = lax.fori_loop(
            0, tl, step, (cr[...], ci[...], jnp.zeros((1, cb), F32), jnp.zeros((1, cb), F32)), unroll=8)
        cr[...] = lr
        ci[...] = li
        dar_ref[...] += dar
        dai_ref[...] += dai

    blk = pl.BlockSpec((tl, cb), lambda s, j, r: (s * nl + nl - 1 - r, j))
    par = pl.BlockSpec((1, cb), lambda s, j, r: (0, j))
    acc = pl.BlockSpec((None, 1, cb), lambda s, j, r: (s, 0, j))
    lr, li, dar, dai = pl.pallas_call(
        body, name="s5_scan_bwd", grid=(seqs, ch // cb, nl),
        in_specs=[blk, blk, blk, blk, par, par], out_specs=[blk, blk, acc, acc],
        out_shape=[jax.ShapeDtypeStruct((t, ch), F32)] * 2 + [jax.ShapeDtypeStruct((seqs, 1, ch), F32)] * 2,
        scratch_shapes=[pltpu.VMEM((1, cb), F32), pltpu.VMEM((1, cb), F32)],
        compiler_params=_params(("parallel", "parallel", "arbitrary")),
    )(gr, gi, xr, xi, ar, ai)
    return lr, li, dar, dai


XATT_BLOCK = 512


def _xatt_probs(qv, kv):
    s = lax.dot_general(qv, kv, _NT, preferred_element_type=F32) * (X_HEAD_DIM ** -0.5)
    e = jnp.exp(s - jnp.max(s, axis=-1, keepdims=True))
    return e / jnp.sum(e, axis=-1, keepdims=True)


def _xatt_fwd(q, k, kv, seqs):
    t = q.shape[0]
    tq = min(XATT_BLOCK, t // seqs)
    nq = t // seqs // tq

    def body(q_ref, k_ref, v_ref, o_ref):
        p = _xatt_probs(q_ref[...], k_ref[...])
        o_ref[...] = jnp.dot(p.astype(BF16), v_ref[...].astype(BF16), preferred_element_type=F32).astype(o_ref.dtype)

    qs = pl.BlockSpec((tq, X_HEAD_DIM), lambda b, h, i: (b * nq + i, h))
    return pl.pallas_call(
        body, name="xatt_fwd", grid=(seqs, N_X_HEADS, nq),
        in_specs=[qs, pl.BlockSpec((N_MEM, X_HEAD_DIM), lambda b, h, i: (b, h)),
                  pl.BlockSpec((N_MEM, X_HEAD_DIM), lambda b, h, i: (b, N_X_HEADS + h))],
        out_specs=qs, out_shape=jax.ShapeDtypeStruct(q.shape, BF16),
        compiler_params=_params(("parallel", "parallel", "parallel")),
    )(q, k, kv)


def _xatt_bwd(q, k, kv, do, seqs):
    t = q.shape[0]
    tq = min(XATT_BLOCK, t // seqs)
    nq = t // seqs // tq
    scale = X_HEAD_DIM ** -0.5

    def body(q_ref, k_ref, v_ref, do_ref, dq_ref, dk_ref, dv_ref):
        @pl.when(pl.program_id(2) == 0)
        def _():
            dk_ref[...] = jnp.zeros(dk_ref.shape, F32)
            dv_ref[...] = jnp.zeros(dv_ref.shape, F32)

        qv, kk = q_ref[...], k_ref[...]
        p = _xatt_probs(qv, kk)
        dob = do_ref[...].astype(BF16)
        dp = lax.dot_general(dob, v_ref[...].astype(BF16), _NT, preferred_element_type=F32)
        ds = p * (dp - jnp.sum(dp * p, axis=-1, keepdims=True))
        dsb = ds.astype(BF16)
        dq_ref[...] = jnp.dot(dsb, kk, preferred_element_type=F32) * scale
        dk_ref[...] += lax.dot_general(dsb, qv, _TN, preferred_element_type=F32) * scale
        dv_ref[...] += lax.dot_general(p.astype(BF16), dob, _TN, preferred_element_type=F32)

    qs = pl.BlockSpec((tq, X_HEAD_DIM), lambda b, h, i: (b * nq + i, h))
    ks = pl.BlockSpec((N_MEM, X_HEAD_DIM), lambda b, h, i: (b, h))
    return pl.pallas_call(
        body, name="xatt_bwd", grid=(seqs, N_X_HEADS, nq),
        in_specs=[qs, ks, pl.BlockSpec((N_MEM, X_HEAD_DIM), lambda b, h, i: (b, N_X_HEADS + h)), qs],
        out_specs=[qs, ks, ks],
        out_shape=[jax.ShapeDtypeStruct(q.shape, F32), jax.ShapeDtypeStruct(k.shape, F32),
                   jax.ShapeDtypeStruct(k.shape, F32)],
        compiler_params=_params(("parallel", "parallel", "arbitrary")),
    )(q, k, kv, do)


CONV_COLS = 256


def _shift_down(x, k, row):
    return jnp.where(row >= k, pltpu.roll(x, k, 0), 0.0)


def _shift_up(x, k, row):
    n = x.shape[0]
    return jnp.where(row < n - k, pltpu.roll(x, n - k, 0), 0.0)


def _conv_pre(g, w, b, row):
    return b + w[0:1, :] * _shift_down(g, 2, row) + w[1:2, :] * _shift_down(g, 1, row) + w[2:3, :] * g


def _convgate_fwd(gu, w, b, seqs):
    t = gu.shape[0]
    l = t // seqs
    nc = D_FF // CONV_COLS

    def body(g_ref, u_ref, w_ref, b_ref, o_ref):
        g = g_ref[...]
        row = lax.broadcasted_iota(jnp.int32, g.shape, 0)
        pre = _conv_pre(g, w_ref[...], b_ref[...], row)
        o_ref[...] = (pre * jax.nn.sigmoid(pre) * u_ref[...]).astype(o_ref.dtype)

    return pl.pallas_call(
        body, name="convgate_fwd", grid=(seqs, nc),
        in_specs=[pl.BlockSpec((l, CONV_COLS), lambda s, j: (s, j)), pl.BlockSpec((l, CONV_COLS), lambda s, j: (s, nc + j)),
                  pl.BlockSpec((3, CONV_COLS), lambda s, j: (0, j)), pl.BlockSpec((1, CONV_COLS), lambda s, j: (0, j))],
        out_specs=pl.BlockSpec((l, CONV_COLS), lambda s, j: (s, j)),
        out_shape=jax.ShapeDtypeStruct((t, D_FF), BF16),
        compiler_params=_params(("parallel", "parallel")),
    )(gu, gu, w, b)


def _convgate_bwd(gu, w, b, dact, seqs):
    t = gu.shape[0]
    l = t // seqs
    nc = D_FF // CONV_COLS

    def body(g_ref, u_ref, w_ref, b_ref, da_ref, dg_ref, du_ref, dw_ref, db_ref):
        @pl.when(pl.program_id(1) == 0)
        def _():
            dw_ref[...] = jnp.zeros(dw_ref.shape, F32)
            db_ref[...] = jnp.zeros(db_ref.shape, F32)

        g, wv, da = g_ref[...], w_ref[...], da_ref[...]
        row = lax.broadcasted_iota(jnp.int32, g.shape, 0)
        g1, g2 = _shift_down(g, 1, row), _shift_down(g, 2, row)
        pre = b_ref[...] + wv[0:1, :] * g2 + wv[1:2, :] * g1 + wv[2:3, :] * g
        sg = jax.nn.sigmoid(pre)
        silu = pre * sg
        du_ref[...] = (da * silu).astype(du_ref.dtype)
        dpre = da * u_ref[...] * (sg * (1.0 + pre * (1.0 - sg)))
        dg = wv[2:3, :] * dpre + wv[1:2, :] * _shift_up(dpre, 1, row) + wv[0:1, :] * _shift_up(dpre, 2, row)
        dg_ref[...] = dg.astype(dg_ref.dtype)
        dw_ref[0:1, :] += jnp.sum(dpre * g2, axis=0, keepdims=True)
        dw_ref[1:2, :] += jnp.sum(dpre * g1, axis=0, keepdims=True)
        dw_ref[2:3, :] += jnp.sum(dpre * g, axis=0, keepdims=True)
        db_ref[...] += jnp.sum(dpre, axis=0, keepdims=True)

    blk = lambda off: pl.BlockSpec((l, CONV_COLS), lambda j, s: (s, off + j))
    return pl.pallas_call(
        body, name="convgate_bwd", grid=(nc, seqs),
        in_specs=[blk(0), blk(nc), pl.BlockSpec((3, CONV_COLS), lambda j, s: (0, j)),
                  pl.BlockSpec((1, CONV_COLS), lambda j, s: (0, j)), blk(0)],
        out_specs=[blk(0), blk(0), pl.BlockSpec((3, CONV_COLS), lambda j, s: (0, j)),
                   pl.BlockSpec((1, CONV_COLS), lambda j, s: (0, j))],
        out_shape=[jax.ShapeDtypeStruct((t, D_FF), BF16), jax.ShapeDtypeStruct((t, D_FF), BF16),
                   jax.ShapeDtypeStruct((3, D_FF), F32), jax.ShapeDtypeStruct((1, D_FF), F32)],
        compiler_params=_params(("parallel", "arbitrary")),
    )(gu, gu, w, b, dact)


def _loss_head(h, target):
    t, d = h.shape
    tm = _pick(t, (256, 128, 8))

    def body(h_ref, t_ref, dh_ref, loss_ref):
        @pl.when(pl.program_id(0) == 0)
        def _():
            loss_ref[...] = jnp.zeros(loss_ref.shape, F32)

        e = h_ref[...] - t_ref[...]
        dh_ref[...] = e * (1.0 / d)
        loss_ref[...] += (0.5 / d) * jnp.sum(jnp.sum(e * e, axis=1, keepdims=True), axis=0, keepdims=True)

    blk = pl.BlockSpec((tm, d), lambda i: (i, 0))
    return pl.pallas_call(
        body, name="loss_head", grid=(t // tm,), in_specs=[blk, blk],
        out_specs=[blk, pl.BlockSpec((1, 1), lambda i: (0, 0))],
        out_shape=[jax.ShapeDtypeStruct((t, d), F32), jax.ShapeDtypeStruct((1, 1), F32)],
        compiler_params=_params(("arbitrary",)),
    )(h, target)


def _s5_discretise(a_re, a_im, log_dt, b_re, b_im):
    dt = jnp.exp(log_dt)[:, None]
    mag = jnp.exp(a_re * dt)
    lb_r = mag * jnp.cos(a_im * dt)
    lb_i = mag * jnp.sin(a_im * dt)
    den = a_re * a_re + a_im * a_im
    nr = lb_r - 1.0
    coef_r = (nr * a_re + lb_i * a_im) / den
    coef_i = (lb_i * a_re - nr * a_im) / den
    bb_r = coef_r[:, :, None] * b_re - coef_i[:, :, None] * b_im
    bb_i = coef_r[:, :, None] * b_im + coef_i[:, :, None] * b_re
    return lb_r, lb_i, bb_r, bb_i


S5_CHUNKS = 4
S5_PER = S5_GROUPS // S5_CHUNKS


def _blockdiag_in(bb):
    eye = jnp.eye(S5_PER, dtype=bb.dtype)
    return jnp.einsum("jgpc,gh->jgchp", bb.reshape(S5_CHUNKS, S5_PER, S5_STATE, S5_GROUP_CH), eye).reshape(
        S5_CHUNKS, S5_PER * S5_GROUP_CH, S5_PER * S5_STATE)


def _blockdiag_in_grad(d):
    eye = jnp.eye(S5_PER, dtype=d.dtype)
    return jnp.einsum("jgchp,gh->jgpc", d.reshape(S5_CHUNKS, S5_PER, S5_GROUP_CH, S5_PER, S5_STATE), eye).reshape(
        S5_GROUPS, S5_STATE, S5_GROUP_CH)


def _blockdiag_out(c):
    eye = jnp.eye(S5_PER, dtype=c.dtype)
    return jnp.einsum("jgcp,gh->jgphc", c.reshape(S5_CHUNKS, S5_PER, S5_GROUP_CH, S5_STATE), eye).reshape(
        S5_CHUNKS, S5_PER * S5_STATE, S5_PER * S5_GROUP_CH)


def _blockdiag_out_grad(d):
    eye = jnp.eye(S5_PER, dtype=d.dtype)
    return jnp.einsum("jgphc,gh->jgcp", d.reshape(S5_CHUNKS, S5_PER, S5_STATE, S5_PER, S5_GROUP_CH), eye).reshape(
        S5_GROUPS, S5_GROUP_CH, S5_STATE)


def _local_step(x3, mem3, target3, p, wb):
    seqs, l, d = x3.shape
    t = seqs * l
    x = x3.reshape(t, d)
    mem = mem3.reshape(seqs * N_MEM, d)
    target = target3.reshape(t, d)
    full = lambda a: (a, a.shape[1], 0, 0)

    s5_in = (p["s5_a_re"], p["s5_a_im"], p["s5_log_dt"], p["s5_b_re"], p["s5_b_im"])
    (lb_r, lb_i, bb_r, bb_i), s5_pull = jax.vjp(_s5_discretise, *s5_in)
    ar, ai = lb_r.reshape(1, S5_CH), lb_i.reshape(1, S5_CH)
    bbr_d, bbi_d = _blockdiag_in(bb_r).astype(BF16), _blockdiag_in(bb_i).astype(BF16)
    cr_d, ci_d = _blockdiag_out(p["s5_c_re"]).astype(BF16), (-_blockdiag_out(p["s5_c_im"])).astype(BF16)
    d_row = p["s5_d"].reshape(1, S5_WIDTH)

    w_in = wb["w_in"]
    w_qkv = w_in[:, :3 * FOX_WIDTH]
    w_uf = jnp.concatenate(
        [w_in[:, 3 * FOX_WIDTH + N_FOX_HEADS:], w_in[:, 3 * FOX_WIDTH:3 * FOX_WIDTH + N_FOX_HEADS],
         jnp.zeros((d, UF_COLS - S5_WIDTH - N_FOX_HEADS), w_in.dtype)], axis=1)

    hn1 = _rowwise(_rms, [full(x)], [p["norm_mix"]], [(d, d, 0, BF16)], "norm_mix_fwd")
    qkv = _mm(hn1, w_qkv, "nn", "in_qkv")
    uf = _mm(hn1, w_uf, "nn", "in_uf")

    bh = seqs * N_FOX_HEADS
    q_pair = (qkv, 128, 0, 1)
    k_pair = (qkv, 128, N_PAIRS, 1)
    gq2, gk2 = jnp.tile(p["fox_q_norm"], (1, 2)), jnp.tile(p["fox_k_norm"], (1, 2))
    pair_out = [(FOX_WIDTH, 128, 1, BF16)]
    qn = _rowwise(_rms_pair, [q_pair], [gq2], pair_out, "fox_qnorm_fwd", heads=N_PAIRS, tm=512)
    kn = _rowwise(_rms_pair, [k_pair], [gk2], pair_out, "fox_knorm_fwd", heads=N_PAIRS, tm=512)

    f_rows = uf[:, S5_WIDTH:S5_WIDTH + N_FOX_HEADS].reshape(seqs, l, N_FOX_HEADS).transpose(0, 2, 1).reshape(bh, l)
    f_bias = jnp.tile(p["fox_f_bias"].reshape(N_FOX_HEADS, 1), (seqs, 1))
    c2 = _forget_fwd(f_rows, f_bias).reshape(seqs * N_PAIRS, 2, l)
    fox, lse = _fox_fwd(qn, kn, qkv, c2, seqs)

    bur = _gmm(uf, bbr_d, "nn", "s5_bu_re")
    bui = _gmm(uf, bbi_d, "nn", "s5_bu_im")
    xr, xi = _scan_fwd(bur, bui, ar, ai, seqs)
    ys = _gmm(xi, ci_d, "nn", "s5_y_im", res=_gmm(xr, cr_d, "nn", "s5_y_re"))
    u_blk = (uf, S5_WIDTH, 0, 0)
    yg = _rowwise(_s5_act, [full(ys), u_blk], [d_row], [(S5_WIDTH, S5_WIDTH, 0, F32)], "s5_act_fwd")
    z = _mm(yg, wb["s5_w_glu"], "nn", "s5_glu")
    y2n = _rowwise(_s5_gate, [full(yg), full(z)], [p["s5_b_glu"], p["out_norm_s5"]],
                   [(S5_WIDTH, S5_WIDTH, 0, BF16)], "s5_gate_fwd")
    foxn = _rowwise(_rms, [full(fox)], [p["out_norm_fox"]], [(FOX_WIDTH, FOX_WIDTH, 0, BF16)], "fox_outnorm_fwd")
    mixed = jnp.concatenate([foxn, y2n], axis=1)
    h1 = _mm(mixed, wb["w_out"], "nn", "mix_out", res=x)

    hn2 = _rowwise(_rms, [full(h1)], [p["norm_cross"]], [(d, d, 0, BF16)], "norm_cross_fwd")
    mn = _rowwise(_rms, [full(mem)], [p["norm_mem"]], [(d, d, 0, BF16)], "norm_mem_fwd")
    xq_raw = _mm(hn2, wb["w_xq"], "nn", "x_q")
    kv = _mm(mn, wb["w_xkv"], "nn", "x_kv")
    xh = lambda a: (a, X_HEAD_DIM, 0, 1)
    xqn = _rowwise(_rms, [xh(xq_raw)], [p["xq_norm"]], [(d, X_HEAD_DIM, 1, BF16)], "x_qnorm_fwd", heads=N_X_HEADS)
    xkn = _rowwise(_rms, [xh(kv)], [p["xk_norm"]], [(d, X_HEAD_DIM, 1, BF16)], "x_knorm_fwd", heads=N_X_HEADS)
    xo = _xatt_fwd(xqn, xkn, kv, seqs)
    h2 = _mm(xo, wb["w_xo"], "nn", "x_out", res=h1)

    hn3 = _rowwise(_rms, [full(h2)], [p["norm_ffn"]], [(d, d, 0, BF16)], "norm_ffn_fwd")
    gu = _mm(hn3, wb["w_ffn_up"], "nn", "ffn_up")
    act = _convgate_fwd(gu, p["ffn_conv_w"], p["ffn_conv_b"], seqs)
    h3 = _mm(act, wb["w_ffn_down"], "nn", "ffn_down", res=h2)
    dh3, loss = _loss_head(h3, target)

    g = {}
    dact = _mm(dh3, wb["w_ffn_down"], "nt", "ffn_down_dx")
    g["w_ffn_down"] = _mm(act, dh3, "tn", "ffn_down_dw")
    dgate, dup, g["ffn_conv_w"], g["ffn_conv_b"] = _convgate_bwd(gu, p["ffn_conv_w"], p["ffn_conv_b"], dact, seqs)
    dgu = jnp.concatenate([dgate, dup], axis=1)
    dhn3 = _mm(dgu, wb["w_ffn_up"], "nt", "ffn_up_dx")
    g["w_ffn_up"] = _mm(hn3, dgu, "tn", "ffn_up_dw")
    (dh2,), (g["norm_ffn"],) = _rowwise_vjp(_rms, [full(h2)], [p["norm_ffn"]], [full(dhn3)], "norm_ffn_bwd",
                                            adds=[full(dh3)])

    dxo = _mm(dh2, wb["w_xo"], "nt", "x_out_dx")
    g["w_xo"] = _mm(xo, dh2, "tn", "x_out_dw")
    dxqn, dxkn, dxv = _xatt_bwd(xqn, xkn, kv, dxo, seqs)
    (dxq_raw,), (g["xq_norm"],) = _rowwise_vjp(_rms, [xh(xq_raw)], [p["xq_norm"]], [xh(dxqn)], "x_qnorm_bwd",
                                               heads=N_X_HEADS, row_dtypes=[BF16])
    (dxk_raw,), (g["xk_norm"],) = _rowwise_vjp(_rms, [xh(kv)], [p["xk_norm"]], [xh(dxkn)], "x_knorm_bwd",
                                               heads=N_X_HEADS, row_dtypes=[BF16])
    dkv = jnp.concatenate([dxk_raw, dxv.astype(BF16)], axis=1)
    dhn2 = _mm(dxq_raw, wb["w_xq"], "nt", "x_q_dx")
    g["w_xq"] = _mm(hn2, dxq_raw, "tn", "x_q_dw")
    dmn = _mm(dkv, wb["w_xkv"], "nt", "x_kv_dx")
    g["w_xkv"] = _mm(mn, dkv, "tn", "x_kv_dw")
    (dh1,), (g["norm_cross"],) = _rowwise_vjp(_rms, [full(h1)], [p["norm_cross"]], [full(dhn2)], "norm_cross_bwd",
                                              adds=[full(dh2)])
    _, (g["norm_mem"],) = _rowwise_vjp(_rms, [full(mem)], [p["norm_mem"]], [full(dmn)], "norm_mem_bwd",
                                       row_dtypes=[BF16])

    dmixed = _mm(dh1, wb["w_out"], "nt", "mix_out_dx")
    g["w_out"] = _mm(mixed, dh1, "tn", "mix_out_dw")
    (dfox,), (g["out_norm_fox"],) = _rowwise_vjp(_rms, [full(fox)], [p["out_norm_fox"]],
                                                 [(dmixed, FOX_WIDTH, 0, 0)], "fox_outnorm_bwd")
    (dyg_a, dz), (g["s5_b_glu"], g["out_norm_s5"]) = _rowwise_vjp(
        _s5_gate, [full(yg), full(z)], [p["s5_b_glu"], p["out_norm_s5"]], [(dmixed, S5_WIDTH, 1, 0)], "s5_gate_bwd",
        row_dtypes=[F32, BF16])
    dyg = _mm(dz, wb["s5_w_glu"], "nt", "s5_glu_dx", res=dyg_a)
    g["s5_w_glu"] = _mm(yg, dz, "tn", "s5_glu_dw")
    (dys, du_a), (dd_row,) = _rowwise_vjp(_s5_act, [full(ys), u_blk], [d_row], [full(dyg)], "s5_act_bwd",
                                          row_dtypes=[BF16, F32])
    g["s5_d"] = dd_row
    cin, cst = S5_PER * S5_GROUP_CH, S5_PER * S5_STATE
    dxr = _gmm(dys, cr_d, "nt", "s5_y_re_dx")
    dxi = _gmm(dys, ci_d, "nt", "s5_y_im_dx")
    dcr_d = _gmm_tn(xr, dys, cst, cin, S5_CHUNKS, "s5_y_re_dw")
    dci_d = _gmm_tn(xi, dys, cst, cin, S5_CHUNKS, "s5_y_im_dw")
    lam_r, lam_i, dar, dai = _scan_bwd(dxr, dxi, xr, xi, ar, ai, seqs)
    du_b = _gmm(lam_i, bbi_d, "nt", "s5_bu_im_dx", res=_gmm(lam_r, bbr_d, "nt", "s5_bu_re_dx"))
    dbbr_d = _gmm_tn(uf, lam_r, cin, cst, S5_CHUNKS, "s5_bu_re_dw")
    dbbi_d = _gmm_tn(uf, lam_i, cin, cst, S5_CHUNKS, "s5_bu_im_dw")
    d_lb_r = jnp.sum(dar, axis=0).reshape(S5_GROUPS, S5_STATE)
    d_lb_i = jnp.sum(dai, axis=0).reshape(S5_GROUPS, S5_STATE)
    g["s5_a_re"], g["s5_a_im"], g["s5_log_dt"], g["s5_b_re"], g["s5_b_im"] = s5_pull(
        (d_lb_r, d_lb_i, _blockdiag_in_grad(dbbr_d), _blockdiag_in_grad(dbbi_d)))
    g["s5_c_re"] = _blockdiag_out_grad(dcr_d)
    g["s5_c_im"] = -_blockdiag_out_grad(dci_d)

    dqn, dkn, dv, dc, dcq = _fox_bwd(qn, kn, qkv, c2, fox, dfox, lse, seqs)
    pair = lambda a: (a, 128, 0, 1)
    (dq_raw,), (dgq2,) = _rowwise_vjp(_rms_pair, [q_pair], [gq2], [pair(dqn)], "fox_qnorm_bwd", heads=N_PAIRS,
                                      tm=512, row_dtypes=[BF16])
    (dk_raw,), (dgk2,) = _rowwise_vjp(_rms_pair, [k_pair], [gk2], [pair(dkn)], "fox_knorm_bwd", heads=N_PAIRS,
                                      tm=512, row_dtypes=[BF16])
    g["fox_q_norm"] = dgq2[:, :HEAD_DIM] + dgq2[:, HEAD_DIM:]
    g["fox_k_norm"] = dgk2[:, :HEAD_DIM] + dgk2[:, HEAD_DIM:]
    dcq_rows = dcq.reshape(seqs, l, N_FOX_HEADS, HEAD_DIM)[:, :, :, 0].transpose(0, 2, 1).reshape(bh, l)
    df_rows, dfb = _forget_bwd(f_rows, f_bias, dc.reshape(bh, l) + dcq_rows)
    g["fox_f_bias"] = jnp.sum(dfb.reshape(seqs, N_FOX_HEADS), axis=0)
    df = df_rows.reshape(seqs, N_FOX_HEADS, l).transpose(0, 2, 1).reshape(t, N_FOX_HEADS)
    dqkv = jnp.concatenate([dq_raw, dk_raw, dv.astype(BF16)], axis=1)
    duf = jnp.concatenate([du_a + du_b, df, jnp.zeros((t, UF_COLS - S5_WIDTH - N_FOX_HEADS), F32)],
                          axis=1).astype(BF16)
    dhn1 = _mm(duf, w_uf, "nt", "in_uf_dx", res=_mm(dqkv, w_qkv, "nt", "in_qkv_dx"))
    dw_qkv = _mm(hn1, dqkv, "tn", "in_qkv_dw")
    dw_uf = _mm(hn1, duf, "tn", "in_uf_dw")
    g["w_in"] = jnp.concatenate([dw_qkv, dw_uf[:, S5_WIDTH:S5_WIDTH + N_FOX_HEADS], dw_uf[:, :S5_WIDTH]], axis=1)
    (dx,), (g["norm_mix"],) = _rowwise_vjp(_rms, [full(x)], [p["norm_mix"]], [full(dhn1)], "norm_mix_bwd",
                                           adds=[full(dh1)])
    return loss, dx.reshape(seqs, l, d), g


def _place():
    return lax.axis_index("x"), lax.axis_index("y"), lax.axis_index("c")


def _other_chips(x, y):
    return [(1 - x, y), (x, 1 - y), (1 - x, 1 - y)]


ANY = pl.BlockSpec(memory_space=pl.ANY)


def _gather_weights(shards, col_kind, taps):
    n = len(shards)

    def body(*refs):
        ins, tap_in, outs, tap_out = refs[:n], refs[n], refs[n + 1:2 * n + 1], refs[2 * n + 1]
        ici_send, ici_recv, d2d_send, d2d_recv, own_send, own_recv = refs[2 * n + 2:]
        x, y, c = _place()
        mine = 2 * x + y
        chips = _other_chips(x, y)
        sibling = (x, y, 1 - c)

        def piece(a, s, h):
            r, cs = ins[a].shape
            hr = r // 2
            if col_kind[a]:
                return outs[a].at[pl.ds(pl.multiple_of(h * hr, 16), hr), pl.ds(pl.multiple_of(s * cs, 128), cs)]
            return outs[a].at[pl.ds(pl.multiple_of(s * r + h * hr, 16), hr), :]

        def slab(a, s):
            r, cs = ins[a].shape
            if col_kind[a]:
                return outs[a].at[:, pl.ds(pl.multiple_of(s * cs, 128), cs)]
            return outs[a].at[pl.ds(pl.multiple_of(s * r, 16), r), :]

        def own_half(a, h):
            hr = ins[a].shape[0] // 2
            return ins[a].at[pl.ds(pl.multiple_of(h * hr, 16), hr), :]

        sends = []
        for a in range(n):
            cp = pltpu.make_async_remote_copy(
                src_ref=ins[a], dst_ref=slab(a, mine), send_sem=own_send.at[a], recv_sem=own_recv.at[a],
                device_id=sibling, device_id_type=MESH)
            cp.start()
            sends.append(cp)
        cp = pltpu.make_async_remote_copy(
            src_ref=tap_in, dst_ref=tap_out.at[mine], send_sem=own_send.at[n], recv_sem=own_recv.at[n],
            device_id=sibling, device_id_type=MESH)
        cp.start()
        sends.append(cp)
        for a in range(n):
            for j, (px, py) in enumerate(chips):
                cp = pltpu.make_async_remote_copy(
                    src_ref=own_half(a, c), dst_ref=piece(a, mine, c), send_sem=ici_send.at[3 * a + j],
                    recv_sem=ici_recv.at[3 * a + j], device_id=(px, py, c), device_id_type=MESH)
                cp.start()
                sends.append(cp)
        for j, (px, py) in enumerate(chips):
            cp = pltpu.make_async_remote_copy(
                src_ref=tap_in, dst_ref=tap_out.at[mine], send_sem=ici_send.at[3 * n + j],
                recv_sem=ici_recv.at[3 * n + j], device_id=(px, py, c), device_id_type=MESH)
            cp.start()
            sends.append(cp)
        for a in range(n):
            for j, (px, py) in enumerate(chips):
                got = piece(a, 2 * px + py, c)
                pltpu.make_async_remote_copy(
                    src_ref=got, dst_ref=got, send_sem=ici_send.at[3 * a + j], recv_sem=ici_recv.at[3 * a + j],
                    device_id=(px, py, c), device_id_type=MESH).wait_recv()
                fwd = pltpu.make_async_remote_copy(
                    src_ref=got, dst_ref=got, send_sem=d2d_send.at[3 * a + j], recv_sem=d2d_recv.at[3 * a + j],
                    device_id=(x, y, 1 - c), device_id_type=MESH)
                fwd.start()
                sends.append(fwd)
        for a in range(n):
            for j, (px, py) in enumerate(chips):
                other = piece(a, 2 * px + py, 1 - c)
                pltpu.make_async_remote_copy(
                    src_ref=other, dst_ref=other, send_sem=d2d_send.at[3 * a + j], recv_sem=d2d_recv.at[3 * a + j],
                    device_id=(x, y, 1 - c), device_id_type=MESH).wait_recv()
        for j, (px, py) in enumerate(chips):
            pltpu.make_async_remote_copy(
                src_ref=tap_in, dst_ref=tap_out.at[2 * px + py], send_sem=ici_send.at[3 * n + j],
                recv_sem=ici_recv.at[3 * n + j], device_id=(px, py, c), device_id_type=MESH).wait_recv()
        for a in range(n):
            pltpu.make_async_remote_copy(
                src_ref=ins[a], dst_ref=slab(a, mine), send_sem=own_send.at[a], recv_sem=own_recv.at[a],
                device_id=sibling, device_id_type=MESH).wait_recv()
        pltpu.make_async_remote_copy(
            src_ref=tap_in, dst_ref=tap_out.at[mine], send_sem=own_send.at[n], recv_sem=own_recv.at[n],
            device_id=sibling, device_id_type=MESH).wait_recv()
        for cp in sends:
            cp.wait_send()

    def full_shape(a):
        r, cs = shards[a].shape
        return (r, 4 * cs) if col_kind[a] else (4 * r, cs)

    res = pl.pallas_call(
        body, name="gather_weights", in_specs=[ANY] * (n + 1), out_specs=[ANY] * (n + 1),
        out_shape=[jax.ShapeDtypeStruct(full_shape(a), shards[a].dtype) for a in range(n)]
        + [jax.ShapeDtypeStruct((4,) + taps.shape, taps.dtype)],
        scratch_shapes=[pltpu.SemaphoreType.DMA((3 * n + 3,)), pltpu.SemaphoreType.DMA((3 * n + 3,)),
                        pltpu.SemaphoreType.DMA((3 * n,)), pltpu.SemaphoreType.DMA((3 * n,)),
                        pltpu.SemaphoreType.DMA((n + 1,)), pltpu.SemaphoreType.DMA((n + 1,))],
        compiler_params=pltpu.CompilerParams(has_side_effects=True),
    )(*shards, taps)
    return res[:n], res[n]


def _pair_exchange_halves(grads, col_kind):
    n = len(grads)

    def body(*refs):
        ins, outs = refs[:n], refs[n:2 * n]
        send_sems, recv_sems = refs[2 * n:]
        x, y, c = _place()
        copies = []
        for a in range(n):
            if col_kind[a]:
                hr = ins[a].shape[0] // 2
                src = ins[a].at[pl.ds(pl.multiple_of((1 - c) * hr, 8), hr), :]
            else:
                hr = ins[a].shape[1] // 2
                src = ins[a].at[:, pl.ds(pl.multiple_of((1 - c) * hr, 8), hr), :]
            cp = pltpu.make_async_remote_copy(
                src_ref=src, dst_ref=outs[a], send_sem=send_sems.at[a], recv_sem=recv_sems.at[a],
                device_id=(x, y, 1 - c), device_id_type=MESH)
            cp.start()
            copies.append(cp)
        for cp in copies:
            cp.wait()

    def half_shape(a):
        s = grads[a].shape
        return (s[0] // 2, s[1]) if col_kind[a] else (4, s[1] // 2, s[2])

    return pl.pallas_call(
        body, name="reduce_pair_exchange", in_specs=[ANY] * n, out_specs=[ANY] * n,
        out_shape=[jax.ShapeDtypeStruct(half_shape(a), grads[a].dtype) for a in range(n)],
        scratch_shapes=[pltpu.SemaphoreType.DMA((n,)), pltpu.SemaphoreType.DMA((n,))],
        compiler_params=pltpu.CompilerParams(has_side_effects=True),
    )(*grads)


def _chip_exchange(sums, col_kind):
    n = len(sums)

    def piece_shape(a):
        s = sums[a].shape
        return (s[0], s[1] // 4) if col_kind[a] else (s[1], s[2])

    def body(*refs):
        ins, outs = refs[:n], refs[n:2 * n]
        send_sems, recv_sems = refs[2 * n:]
        x, y, c = _place()
        copies = []
        for a in range(n):
            for j, (px, py) in enumerate(_other_chips(x, y)):
                if col_kind[a]:
                    cs = piece_shape(a)[1]
                    src = ins[a].at[:, pl.ds(pl.multiple_of((2 * px + py) * cs, 128), cs)]
                else:
                    src = ins[a].at[2 * px + py]
                cp = pltpu.make_async_remote_copy(
                    src_ref=src, dst_ref=outs[a].at[j], send_sem=send_sems.at[3 * a + j],
                    recv_sem=recv_sems.at[3 * a + j], device_id=(px, py, c), device_id_type=MESH)
                cp.start()
                copies.append(cp)
        for cp in copies:
            cp.wait()

    return pl.pallas_call(
        body, name="reduce_chip_exchange", in_specs=[ANY] * n, out_specs=[ANY] * n,
        out_shape=[jax.ShapeDtypeStruct((3,) + piece_shape(a), sums[a].dtype) for a in range(n)],
        scratch_shapes=[pltpu.SemaphoreType.DMA((3 * n,)), pltpu.SemaphoreType.DMA((3 * n,))],
        compiler_params=pltpu.CompilerParams(has_side_effects=True),
    )(*sums)


def _pair_swap_halves(halves):
    n = len(halves)

    def body(*refs):
        ins, outs = refs[:n], refs[n:2 * n]
        send_sems, recv_sems = refs[2 * n:]
        x, y, c = _place()
        copies = []
        for a in range(n):
            cp = pltpu.make_async_remote_copy(
                src_ref=ins[a], dst_ref=outs[a], send_sem=send_sems.at[a], recv_sem=recv_sems.at[a],
                device_id=(x, y, 1 - c), device_id_type=MESH)
            cp.start()
            copies.append(cp)
        for cp in copies:
            cp.wait()

    return pl.pallas_call(
        body, name="reduce_pair_swap", in_specs=[ANY] * n, out_specs=[ANY] * n,
        out_shape=[jax.ShapeDtypeStruct(s.shape, s.dtype) for s in halves],
        scratch_shapes=[pltpu.SemaphoreType.DMA((n,)), pltpu.SemaphoreType.DMA((n,))],
        compiler_params=pltpu.CompilerParams(has_side_effects=True),
    )(*halves)


def _chip_sum(name, chip_sel, own, col, others):
    _, r, c = others.shape
    tr = _pick(r, (256, 128, 64, 32, 16))
    if col:
        own_spec = pl.BlockSpec((tr, c), lambda i, s: (i, s[0]))
    else:
        own_spec = pl.BlockSpec((None, tr, c), lambda i, s: (s[0], i, 0))
    specs = [own_spec] + [pl.BlockSpec((None, tr, c), lambda i, s, k=k: (k, i, 0)) for k in range(3)]

    def body(s_ref, own_ref, r0, r1, r2, o_ref):
        o_ref[...] = ((own_ref[...].astype(F32) + r0[...].astype(F32)) + r1[...].astype(F32)) + r2[...].astype(F32)

    return pl.pallas_call(
        body, name=name,
        grid_spec=pltpu.PrefetchScalarGridSpec(
            num_scalar_prefetch=1, grid=(r // tr,), in_specs=specs,
            out_specs=pl.BlockSpec((tr, c), lambda i, s: (i, 0))),
        out_shape=jax.ShapeDtypeStruct((r, c), F32),
        compiler_params=_params(("parallel",)),
    )(chip_sel, own, others, others, others)


def _pair_sum(name, c_sel, grad, recv, col):
    if col:
        r, c4 = grad.shape
        hr, c = r // 2, c4 // 4
    else:
        _, r, c = grad.shape
        hr = r // 2
    tr = _pick(hr, (256, 128, 64, 32, 16))
    nb = hr // tr

    def body(s_ref, g_ref, r_ref, o_ref):
        o_ref[...] = (g_ref[...] + r_ref[...]).astype(o_ref.dtype)

    if col:
        in_specs = [pl.BlockSpec((tr, c), lambda k, i, s: (s[0] * nb + i, k)), pl.BlockSpec((tr, c), lambda k, i, s: (i, k))]
        out_spec = pl.BlockSpec((tr, c), lambda k, i, s: (i, k))
    else:
        in_specs = [pl.BlockSpec((None, tr, c), lambda k, i, s: (k, s[0] * nb + i, 0)),
                    pl.BlockSpec((None, tr, c), lambda k, i, s: (k, i, 0))]
        out_spec = pl.BlockSpec((None, tr, c), lambda k, i, s: (k, i, 0))
    return pl.pallas_call(
        body, name=name,
        grid_spec=pltpu.PrefetchScalarGridSpec(num_scalar_prefetch=1, grid=(4, nb), in_specs=in_specs,
                                               out_specs=out_spec),
        out_shape=jax.ShapeDtypeStruct(recv.shape, BF16),
        compiler_params=_params(("parallel", "parallel")),
    )(c_sel, grad, recv)


def _allreduce_small(vals):
    sizes = [int(math.prod(v.shape)) for v in vals]
    padded = [-(-s // 128) * 128 for s in sizes]
    total = -(-sum(padded) // 1024) * 1024
    flat = [jnp.pad(v.reshape(-1), (0, p - s)) for v, s, p in zip(vals, sizes, padded)]
    flat.append(jnp.zeros((total - sum(padded),), F32))
    packed = jnp.concatenate(flat).reshape(total // 128, 128)

    def body(in_ref, out_ref, r0, r1, r2, send_sems, recv_sems):
        x, y, c = _place()
        out_ref[...] = in_ref[...]
        for k, (peer, land) in enumerate(zip([(x, y, 1 - c), (1 - x, y, c), (x, 1 - y, c)], (r0, r1, r2))):
            cp = pltpu.make_async_remote_copy(
                src_ref=out_ref, dst_ref=land, send_sem=send_sems.at[k], recv_sem=recv_sems.at[k],
                device_id=peer, device_id_type=MESH)
            cp.start()
            cp.wait()
            out_ref[...] = out_ref[...] + land[...]

    vm = pl.BlockSpec(memory_space=pltpu.VMEM)
    summed = pl.pallas_call(
        body, name="allreduce_small", in_specs=[vm], out_specs=vm,
        out_shape=jax.ShapeDtypeStruct(packed.shape, F32),
        scratch_shapes=[pltpu.VMEM(packed.shape, F32)] * 3
        + [pltpu.SemaphoreType.DMA((3,)), pltpu.SemaphoreType.DMA((3,))],
        compiler_params=pltpu.CompilerParams(has_side_effects=True, vmem_limit_bytes=VMEM_LIMIT_BYTES),
    )(packed).reshape(-1)
    outs, off = [], 0
    for v, s, p in zip(vals, sizes, padded):
        outs.append(summed[off:off + s].reshape(v.shape))
        off += p
    return outs


def _adamw_math(w, g, m, v):
    m2 = ADAM_B1 * m + (1.0 - ADAM_B1) * g
    v2 = ADAM_B2 * v + (1.0 - ADAM_B2) * (g * g)
    m_hat = m2 / (1.0 - ADAM_B1 ** ADAM_STEP)
    v_hat = v2 / (1.0 - ADAM_B2 ** ADAM_STEP)
    delta = -ADAM_LR * (m_hat / (jnp.sqrt(v_hat) + ADAM_EPS) + ADAM_WD * w)
    return delta, m2, v2


def _adamw_big(name, c_sel, w, g_mine, g_sibling, m, v):
    r, c = w.shape
    hr = r // 2
    tr = _pick(hr, (256, 128, 64, 32, 16, 8))
    nb = hr // tr

    def body(s_ref, w_ref, ga_ref, gb_ref, m_ref, v_ref, go_ref, d_ref, mo_ref, vo_ref):
        gv = jnp.where(pl.program_id(0) == s_ref[0], ga_ref[...], gb_ref[...])
        d, m2, v2 = _adamw_math(w_ref[...], gv, m_ref[...], v_ref[...])
        go_ref[...] = gv
        d_ref[...] = d
        mo_ref[...] = m2
        vo_ref[...] = v2

    blk = pl.BlockSpec((tr, c), lambda h, i, s: (h * nb + i, 0))
    half = pl.BlockSpec((tr, c), lambda h, i, s: (i, 0))
    return pl.pallas_call(
        body, name=name,
        grid_spec=pltpu.PrefetchScalarGridSpec(
            num_scalar_prefetch=1, grid=(2, nb), in_specs=[blk, half, half, blk, blk], out_specs=[blk] * 4),
        out_shape=[jax.ShapeDtypeStruct((r, c), F32)] * 4, compiler_params=_params(("parallel", "parallel")),
    )(c_sel, w, g_mine, g_sibling, m, v)


def _adamw_small(ws, gs, ms, vs):
    n = len(ws)

    def body(*refs):
        w_r, g_r, m_r, v_r = refs[:n], refs[n:2 * n], refs[2 * n:3 * n], refs[3 * n:4 * n]
        o = refs[4 * n:]
        for a in range(n):
            gv = g_r[a][...]
            d, m2, v2 = _adamw_math(w_r[a][...], gv, m_r[a][...], v_r[a][...])
            o[a][...] = gv
            o[n + a][...] = d
            o[2 * n + a][...] = m2
            o[3 * n + a][...] = v2

    res = pl.pallas_call(
        body, name="adamw_small", out_shape=[jax.ShapeDtypeStruct(w.shape, F32) for _ in range(4) for w in ws],
        compiler_params=_params(),
    )(*ws, *gs, *ms, *vs)
    return res[:n], res[n:2 * n], res[2 * n:3 * n], res[3 * n:]


def _full_from_gathered(name, gathered):
    if name == "w_in":
        rows = gathered.shape[0] // 4
        return gathered.reshape(4, rows, gathered.shape[1]).transpose(1, 0, 2).reshape(rows, 4 * gathered.shape[1])
    return gathered


def _reduce_layout(name, full):
    if name in COL_KIND:
        return full
    if name == "w_in":
        rows, cols = full.shape
        return full.reshape(rows, 4, cols // 4).transpose(1, 0, 2)
    return full.reshape(4, full.shape[0] // 4, full.shape[1])


def kernel(x, mem, norm_mix, w_in, fox_q_norm, fox_k_norm, fox_f_bias, s5_a_re, s5_a_im, s5_log_dt, s5_b_re, s5_b_im, s5_c_re, s5_c_im, s5_d, s5_w_glu, s5_b_glu, out_norm_fox, out_norm_s5, w_out, norm_cross, norm_mem, w_xq, w_xkv, xq_norm, xk_norm, w_xo, norm_ffn, w_ffn_up, ffn_conv_w, ffn_conv_b, w_ffn_down, loss_target, m_norm_mix, m_w_in, m_fox_q_norm, m_fox_k_norm, m_fox_f_bias, m_s5_a_re, m_s5_a_im, m_s5_log_dt, m_s5_b_re, m_s5_b_im, m_s5_c_re, m_s5_c_im, m_s5_d, m_s5_w_glu, m_s5_b_glu, m_out_norm_fox, m_out_norm_s5, m_w_out, m_norm_cross, m_norm_mem, m_w_xq, m_w_xkv, m_xq_norm, m_xk_norm, m_w_xo, m_norm_ffn, m_w_ffn_up, m_ffn_conv_w, m_ffn_conv_b, m_w_ffn_down, v_norm_mix, v_w_in, v_fox_q_norm, v_fox_k_norm, v_fox_f_bias, v_s5_a_re, v_s5_a_im, v_s5_log_dt, v_s5_b_re, v_s5_b_im, v_s5_c_re, v_s5_c_im, v_s5_d, v_s5_w_glu, v_s5_b_glu, v_out_norm_fox, v_out_norm_s5, v_w_out, v_norm_cross, v_norm_mem, v_w_xq, v_w_xkv, v_xq_norm, v_xk_norm, v_w_xo, v_norm_ffn, v_w_ffn_up, v_ffn_conv_w, v_ffn_conv_b, v_w_ffn_down):
    given = dict(locals())
    w = {n: given[n] for n in WEIGHTS}
    m = {n: given["m_" + n] for n in WEIGHTS}
    v = {n: given["v_" + n] for n in WEIGHTS}
    xi, yi, ci = _place()
    chip = (2 * xi + yi).astype(jnp.int32)

    col_kind = [n in COL_KIND for n in BIG]
    gathered, taps = _gather_weights([w[n][0].astype(BF16) for n in BIG], col_kind, w["ffn_conv_w"][0])
    wb = {n: _full_from_gathered(n, gathered[k]) for k, n in enumerate(BIG)}
    conv_w = taps.transpose(1, 0, 2).reshape(3, D_FF)

    p = {n: w[n][0] for n in SMALL}
    p["ffn_conv_w"] = conv_w
    for n in ("norm_mix", "fox_q_norm", "fox_k_norm", "fox_f_bias", "s5_b_glu", "out_norm_fox", "out_norm_s5",
              "norm_cross", "norm_mem", "xq_norm", "xk_norm", "norm_ffn", "ffn_conv_b"):
        p[n] = p[n].reshape(1, -1)
    loss, grad_x, g = _local_step(x, mem, loss_target, p, wb)

    small_names = list(SMALL) + ["ffn_conv_w"]
    small_vals = [g[n].reshape(w[n].shape if n != "ffn_conv_w" else (1, 3, D_FF)) for n in small_names] + [loss]
    reduced = _allreduce_small(small_vals)
    loss_all = reduced[-1].reshape(())
    conv_w_grad = lax.dynamic_slice_in_dim(reduced[-2], chip * (D_FF // 4), D_FF // 4, axis=2)
    sg, sd, sm, sv = _adamw_small(
        [w[n] for n in small_names], list(reduced[:len(SMALL)]) + [conv_w_grad],
        [m[n] for n in small_names], [v[n] for n in small_names])
    out_g = dict(zip(small_names, sg))
    out_d = dict(zip(small_names, sd))
    out_m = dict(zip(small_names, sm))
    out_v = dict(zip(small_names, sv))

    c_sel = ci.astype(jnp.int32).reshape(1)
    chip_sel = chip.reshape(1)
    grads = [_reduce_layout(n, g[n]) for n in BIG]
    from_sibling = _pair_exchange_halves(grads, col_kind)
    pair_sums = [_pair_sum("reduce_pair_sum_" + n, c_sel, gr, rv, ck)
                 for n, gr, rv, ck in zip(BIG, grads, from_sibling, col_kind)]
    from_chips = _chip_exchange(pair_sums, col_kind)
    halves = [_chip_sum("reduce_chip_sum_" + n, chip_sel, ps, ck, fc)
              for n, ps, fc, ck in zip(BIG, pair_sums, from_chips, col_kind)]
    sibling_halves = _pair_swap_halves(halves)
    for n, mine, theirs in zip(BIG, halves, sibling_halves):
        go, d, m2, v2 = _adamw_big("adamw_" + n, c_sel, w[n][0], mine, theirs, m[n][0], v[n][0])
        out_g[n], out_d[n], out_m[n], out_v[n] = go[None], d[None], m2[None], v2[None]

    return (loss_all, grad_x, *[out_g[n] for n in WEIGHTS], *[out_d[n] for n in WEIGHTS],
            *[out_m[n] for n in WEIGHTS], *[out_v[n] for n in WEIGHTS])
```

```python
import functools
import math

import jax
import jax.numpy as jnp
from jax import lax
from jax.experimental import pallas as pl
from jax.experimental.pallas import tpu as pltpu

F32 = jnp.float32
BF16 = jnp.bfloat16

D_MODEL = 1024
FOX_WIDTH = 512
HEAD_DIM = 64
N_FOX_HEADS = 8
S5_WIDTH = 512
S5_GROUP_CH = 16
S5_GROUPS = 32
S5_STATE = 64
S5_CH = S5_GROUPS * S5_STATE
N_X_HEADS = 4
X_HEAD_DIM = 256
N_MEM = 256
D_FF = 2816
UF_COLS = 640
EPS = 1e-6
ADAM_LR = 0.001
ADAM_B1 = 0.9
ADAM_B2 = 0.999
ADAM_EPS = 1e-08
ADAM_WD = 0.01
ADAM_STEP = 10

VMEM_LIMIT_BYTES = 56 * 1024 * 1024
MM_BLOCK_BYTES = 6 * 1024 * 1024
MESH = pl.DeviceIdType.MESH

BIG = ("w_in", "s5_w_glu", "w_out", "w_xq", "w_xkv", "w_xo", "w_ffn_up", "w_ffn_down")
COL_KIND = ("w_xkv", "w_ffn_up")
SMALL = ("norm_mix", "fox_q_norm", "fox_k_norm", "fox_f_bias", "s5_a_re", "s5_a_im", "s5_log_dt",
         "s5_b_re", "s5_b_im", "s5_c_re", "s5_c_im", "s5_d", "s5_b_glu", "out_norm_fox", "out_norm_s5",
         "norm_cross", "norm_mem", "xq_norm", "xk_norm", "norm_ffn", "ffn_conv_b")
WEIGHTS = ("norm_mix", "w_in", "fox_q_norm", "fox_k_norm", "fox_f_bias", "s5_a_re", "s5_a_im", "s5_log_dt",
           "s5_b_re", "s5_b_im", "s5_c_re", "s5_c_im", "s5_d", "s5_w_glu", "s5_b_glu", "out_norm_fox",
           "out_norm_s5", "w_out", "norm_cross", "norm_mem", "w_xq", "w_xkv", "xq_norm", "xk_norm", "w_xo",
           "norm_ffn", "w_ffn_up", "ffn_conv_w", "ffn_conv_b", "w_ffn_down")


def _params(sem=None):
    return pltpu.CompilerParams(dimension_semantics=sem, vmem_limit_bytes=VMEM_LIMIT_BYTES)


def _pick(n, cands):
    for c in cands:
        if n % c == 0:
            return c
    return n


_DIMS = {"nn": (((1,), (0,)), ((), ())), "nt": (((1,), (1,)), ((), ())), "tn": (((0,), (0,)), ((), ()))}


def _mm(a, b, mode, name, out_dtype=F32, res=None):
    if mode == "nn":
        (m, k), (k2, n) = a.shape, b.shape
    elif mode == "nt":
        (m, k), (n, k2) = a.shape, b.shape
    else:
        (k, m), (k2, n) = a.shape, b.shape
    assert k == k2, (name, a.shape, b.shape)

    def fit(dim, itemsize):
        for c in (512, 256, 128):
            if dim % c == 0 and c * k * itemsize <= MM_BLOCK_BYTES:
                return c
        return 128 if dim % 128 == 0 else dim

    tm, tn = fit(m, a.dtype.itemsize), fit(n, b.dtype.itemsize)
    a_spec = pl.BlockSpec((k, tm), lambda i, j: (0, i)) if mode == "tn" else pl.BlockSpec((tm, k), lambda i, j: (i, 0))
    b_spec = pl.BlockSpec((tn, k), lambda i, j: (j, 0)) if mode == "nt" else pl.BlockSpec((k, tn), lambda i, j: (0, j))
    o_spec = pl.BlockSpec((tm, tn), lambda i, j: (i, j))
    dims = _DIMS[mode]
    has_res = res is not None

    def body(*refs):
        a_ref, b_ref = refs[0], refs[1]
        o_ref = refs[-1]
        acc = lax.dot_general(a_ref[...].astype(BF16), b_ref[...].astype(BF16), dims, preferred_element_type=F32)
        if has_res:
            acc = acc + refs[2][...].astype(F32)
        o_ref[...] = acc.astype(o_ref.dtype)

    return pl.pallas_call(
        body, name=name, grid=(m // tm, n // tn),
        in_specs=[a_spec, b_spec] + ([o_spec] if has_res else []),
        out_specs=o_spec, out_shape=jax.ShapeDtypeStruct((m, n), out_dtype),
        compiler_params=_params(("parallel", "parallel")),
    )(*((a, b, res) if has_res else (a, b)))


def _gmm_tn(a, b, m, n, groups, name):
    t = a.shape[0]

    def body(a_ref, b_ref, o_ref):
        o_ref[...] = lax.dot_general(a_ref[...].astype(BF16), b_ref[...].astype(BF16), _DIMS["tn"],
                                     preferred_element_type=F32)

    return pl.pallas_call(
        body, name=name, grid=(groups,),
        in_specs=[pl.BlockSpec((t, m), lambda j: (0, j)), pl.BlockSpec((t, n), lambda j: (0, j))],
        out_specs=pl.BlockSpec((None, m, n), lambda j: (j, 0, 0)),
        out_shape=jax.ShapeDtypeStruct((groups, m, n), F32), compiler_params=_params(("parallel",)),
    )(a, b)


def _gmm(a, b, mode, name, out_dtype=F32, res=None):
    g = b.shape[0]
    dims = _DIMS[mode]
    has_res = res is not None

    def body(*refs):
        acc = lax.dot_general(refs[0][...].astype(BF16), refs[1][...].astype(BF16), dims, preferred_element_type=F32)
        if has_res:
            acc = acc + refs[2][...].astype(F32)
        refs[-1][...] = acc.astype(refs[-1].dtype)

    k, n = (b.shape[1], b.shape[2]) if mode == "nn" else (b.shape[2], b.shape[1])
    m = a.shape[0]
    tm = _pick(m, (512, 256, 128))
    o_spec = pl.BlockSpec((tm, n), lambda i, j: (i, j))
    return pl.pallas_call(
        body, name=name, grid=(m // tm, g),
        in_specs=[pl.BlockSpec((tm, k), lambda i, j: (i, j)), pl.BlockSpec((None,) + b.shape[1:], lambda i, j: (j, 0, 0))]
        + ([o_spec] if has_res else []),
        out_specs=o_spec, out_shape=jax.ShapeDtypeStruct((m, g * n), out_dtype),
        compiler_params=_params(("parallel", "parallel")),
    )(*((a, b, res) if has_res else (a, b)))


def _row_spec(tm, bc, off, step):
    return pl.BlockSpec((tm, bc), lambda i, h: (i, off + step * h))


ROW_TILE_ELEMS = 512 * 1024


def _row_tile(t, rows):
    widest = max(bc for (_, bc, _, _) in rows)
    return _pick(t, (min(t, ROW_TILE_ELEMS // widest), 512, 256, 128, 64, 8))


def _rowwise(fn, rows, pars, outs, name, heads=1):
    t = rows[0][0].shape[0]
    tm = _row_tile(t, rows)
    nr, npar = len(rows), len(pars)

    def body(*refs):
        vals = [r[...].astype(F32) for r in refs[:nr + npar]]
        res = fn(*vals)
        if not isinstance(res, (tuple, list)):
            res = (res,)
        for o_ref, v in zip(refs[nr + npar:], res):
            o_ref[...] = v.astype(o_ref.dtype)

    in_specs = [_row_spec(tm, bc, off, st) for (_, bc, off, st) in rows]
    in_specs += [pl.BlockSpec(p.shape, lambda i, h: (0, 0)) for p in pars]
    out_specs = [_row_spec(tm, bc, 0, st) for (_, bc, st, _) in outs]
    out_shape = [jax.ShapeDtypeStruct((t, c), dt) for (c, _, _, dt) in outs]
    res = pl.pallas_call(
        body, name=name, grid=(t // tm, heads), in_specs=in_specs, out_specs=out_specs, out_shape=out_shape,
        compiler_params=_params(("parallel", "parallel")),
    )(*[r[0] for r in rows], *pars)
    return res[0] if len(res) == 1 else res


def _rowwise_vjp(fn, rows, pars, cts, name, heads=1, adds=None, row_dtypes=None):
    t = rows[0][0].shape[0]
    tm = _row_tile(t, rows)
    nr, npar, nct = len(rows), len(pars), len(cts)
    adds = adds or [None] * nr
    add_list = [a for a in adds if a is not None]
    row_dtypes = row_dtypes or [F32] * nr

    def body(*refs):
        i, h = pl.program_id(0), pl.program_id(1)
        p = 0
        row_v = [r[...].astype(F32) for r in refs[p:p + nr]]; p += nr
        par_v = [r[...].astype(F32) for r in refs[p:p + npar]]; p += npar
        ct_v = [r[...].astype(F32) for r in refs[p:p + nct]]; p += nct
        add_refs = refs[p:p + len(add_list)]; p += len(add_list)
        drow_refs = refs[p:p + nr]; p += nr
        dpar_refs = refs[p:p + npar]

        def wrapped(*a):
            r = fn(*a)
            return tuple(r) if isinstance(r, (tuple, list)) else (r,)

        _, pull = jax.vjp(wrapped, *row_v, *par_v)
        grads = pull(tuple(ct_v))
        ai = 0
        for k in range(nr):
            g = grads[k]
            if adds[k] is not None:
                g = g + add_refs[ai][...].astype(F32)
                ai += 1
            drow_refs[k][...] = g.astype(drow_refs[k].dtype)

        @pl.when((i == 0) & (h == 0))
        def _():
            for r in dpar_refs:
                r[...] = jnp.zeros(r.shape, r.dtype)

        for k in range(npar):
            dpar_refs[k][...] += grads[nr + k]

    in_specs = [_row_spec(tm, bc, off, st) for (_, bc, off, st) in rows]
    in_specs += [pl.BlockSpec(q.shape, lambda i, h: (0, 0)) for q in pars]
    in_specs += [_row_spec(tm, bc, off, st) for (_, bc, off, st) in cts]
    in_specs += [_row_spec(tm, bc, off, st) for (_, bc, off, st) in add_list]
    out_specs = [_row_spec(tm, bc, 0, st) for (_, bc, _, st) in rows]
    out_specs += [pl.BlockSpec(q.shape, lambda i, h: (0, 0)) for q in pars]
    out_shape = [jax.ShapeDtypeStruct((t, bc * (heads if st else 1)), dt) for (_, bc, _, st), dt in zip(rows, row_dtypes)]
    out_shape += [jax.ShapeDtypeStruct(q.shape, F32) for q in pars]
    res = pl.pallas_call(
        body, name=name, grid=(t // tm, heads), in_specs=in_specs, out_specs=out_specs, out_shape=out_shape,
        compiler_params=_params(("arbitrary", "arbitrary")),
    )(*[r[0] for r in rows], *pars, *[c[0] for c in cts], *[a[0] for a in add_list])
    return list(res[:nr]), list(res[nr:])


def _rms(x, g):
    return x * lax.rsqrt(jnp.mean(x * x, axis=-1, keepdims=True) + EPS) * g


def _rms_pair(x, g):
    left = lax.broadcasted_iota(jnp.int32, x.shape, 1) < HEAD_DIM
    x2 = x * x
    ms_a = jnp.sum(jnp.where(left, x2, 0.0), axis=-1, keepdims=True) * (1.0 / HEAD_DIM)
    ms_b = jnp.sum(jnp.where(left, 0.0, x2), axis=-1, keepdims=True) * (1.0 / HEAD_DIM)
    return x * lax.rsqrt(jnp.where(left, ms_a, ms_b) + EPS) * g


def _gelu(x):
    return 0.5 * x * (1.0 + jnp.tanh(math.sqrt(2.0 / math.pi) * (x + 0.044715 * (x * x * x))))


def _s5_act(ys, u, d):
    return _gelu(ys + d * u)


def _s5_gate(yg, z, b, g):
    return _rms(yg * jax.nn.sigmoid(z + b), g)


def _lane_cumsum(x, reverse):
    n = x.shape[-1]
    lane = lax.broadcasted_iota(jnp.int32, x.shape, 1)
    k = 1
    while k < n:
        if reverse:
            x = x + jnp.where(lane < n - k, pltpu.roll(x, n - k, 1), 0.0)
        else:
            x = x + jnp.where(lane >= k, pltpu.roll(x, k, 1), 0.0)
        k *= 2
    return x


def _log_sigmoid(z):
    return jnp.minimum(z, 0.0) - jnp.log(1.0 + jnp.exp(-jnp.abs(z)))


def _forget_fwd(f, bias):
    def body(f_ref, b_ref, c_ref):
        c_ref[...] = _lane_cumsum(_log_sigmoid(f_ref[...] + b_ref[...]), False)

    return pl.pallas_call(body, name="forget_fwd", out_shape=jax.ShapeDtypeStruct(f.shape, F32),
                          compiler_params=_params())(f, bias)


def _forget_bwd(f, bias, dc):
    def body(f_ref, b_ref, dc_ref, df_ref, db_ref):
        dlog = _lane_cumsum(dc_ref[...], True)
        df = dlog * jax.nn.sigmoid(-(f_ref[...] + b_ref[...]))
        df_ref[...] = df
        db_ref[...] = jnp.sum(df, axis=1, keepdims=True)

    return pl.pallas_call(body, name="forget_bwd",
                          out_shape=(jax.ShapeDtypeStruct(f.shape, F32), jax.ShapeDtypeStruct(bias.shape, F32)),
                          compiler_params=_params())(f, bias, dc)


FOX_BLOCK = 256
_NT = _DIMS["nt"]
_TN = _DIMS["tn"]


N_PAIRS = N_FOX_HEADS // 2
V_BLOCK0 = 2 * N_PAIRS


def _left_lanes(shape):
    return lax.broadcasted_iota(jnp.int32, shape, 1) < HEAD_DIM


def _top_rows(shape):
    return lax.broadcasted_iota(jnp.int32, shape, 0) < HEAD_DIM


def _wide(c_tile, n):
    return c_tile if n == 128 else jnp.concatenate([c_tile] * (n // 128), axis=1)


def _fox_fwd(qn, kn, qkv, c_wide, seqs):
    t = qn.shape[0]
    l = t // seqs
    tb = min(FOX_BLOCK, l)
    nb = l // tb
    scale = HEAD_DIM ** -0.5

    def body(q_ref, k_ref, v_ref, ca_ref, cb_ref, o_ref, lse_ref, vt_ref):
        i = pl.program_id(2)
        top = _top_rows((128, tb))

        @pl.when(i == 0)
        def _():
            vt_ref[...] = v_ref[...].T.astype(BF16)

        qt = (q_ref[...].astype(F32) * scale).T.astype(BF16)
        zero = jnp.zeros_like(qt)
        qts = (jnp.where(top, qt, zero), jnp.where(top, zero, qt))
        causal = lax.broadcasted_iota(jnp.int32, (tb, tb), 0) <= lax.broadcasted_iota(jnp.int32, (tb, tb), 1)
        c_refs = (ca_ref, cb_ref)

        def tile(j, carry, masked):
            off = pl.multiple_of(j * tb, tb)
            k2 = k_ref[pl.ds(off, tb), :]
            vt = vt_ref[:, pl.ds(off, tb)]
            vts = (jnp.where(top, vt, zero), jnp.where(top, zero, vt))
            (ma, sa), (mb, sb), acc = carry
            new, alphas, pv = [], [], []
            for h, (m, s_sum) in enumerate(((ma, sa), (mb, sb))):
                st = jnp.dot(k2, qts[h], preferred_element_type=F32) - _wide(c_refs[h][pl.ds(off, tb), :], tb)
                if masked:
                    st = jnp.where(causal, st, -jnp.inf)
                m_new = jnp.maximum(m, jnp.max(st, axis=0, keepdims=True))
                alpha = jnp.exp(m - m_new)
                p = jnp.exp(st - m_new)
                new.append((m_new, alpha * s_sum + jnp.sum(p, axis=0, keepdims=True)))
                alphas.append(alpha)
                pv.append(jnp.dot(vts[h], p.astype(BF16), preferred_element_type=F32))
            acc = jnp.where(top, alphas[0], alphas[1]) * acc + pv[0] + pv[1]
            return new[0], new[1], acc

        stat = (jnp.full((1, tb), -jnp.inf, F32), jnp.zeros((1, tb), F32))
        carry = lax.fori_loop(0, i, lambda j, c: tile(j, c, False), (stat, stat, jnp.zeros((128, tb), F32)))
        (ma, sa), (mb, sb), acc = tile(i, carry, True)
        o_ref[...] = (acc / jnp.where(top, sa, sb)).T
        lse_ref[0:1, :] = ma + jnp.log(sa)
        lse_ref[1:2, :] = mb + jnp.log(sb)

    qblk = pl.BlockSpec((tb, 128), lambda b, hp, i: (b * nb + i, hp))
    return pl.pallas_call(
        body, name="fox_fwd", grid=(seqs, N_PAIRS, nb),
        in_specs=[qblk, pl.BlockSpec((l, 128), lambda b, hp, i: (b, hp)),
                  pl.BlockSpec((l, 128), lambda b, hp, i: (b, V_BLOCK0 + hp)),
                  pl.BlockSpec((None, l, 128), lambda b, hp, i: (b * N_FOX_HEADS + 2 * hp, 0, 0)),
                  pl.BlockSpec((None, l, 128), lambda b, hp, i: (b * N_FOX_HEADS + 2 * hp + 1, 0, 0))],
        out_specs=[qblk, pl.BlockSpec((None, 2, tb), lambda b, hp, i: (b * N_PAIRS + hp, 0, i))],
        out_shape=[jax.ShapeDtypeStruct((t, FOX_WIDTH), F32), jax.ShapeDtypeStruct((seqs * N_PAIRS, 2, l), F32)],
        scratch_shapes=[pltpu.VMEM((128, l), BF16)],
        compiler_params=_params(("parallel", "parallel", "arbitrary")),
    )(qn, kn, qkv, c_wide, c_wide)


def _fox_bwd(qn, kn, qkv, c_wide, o, do, lse, seqs):
    t = qn.shape[0]
    l = t // seqs
    tb = min(FOX_BLOCK, l)
    nb = l // tb
    scale = HEAD_DIM ** -0.5

    def body(q_ref, k_ref, v_ref, ca_ref, cb_ref, o_ref, do_ref, lse_ref, dq_ref, dk_ref, dv_ref, dc_ref, dcq_ref,
             qt_ref, kt_ref, dot_ref, delta_ref, dqt_ref):
        top_l = _top_rows((128, l))
        top = _top_rows((128, tb))
        left = _left_lanes((tb, 128))
        zero_t = jnp.zeros((128, tb), BF16)
        zero_l = jnp.zeros((tb, 128), BF16)
        rows = lambda a: (jnp.where(top, a, zero_t), jnp.where(top, zero_t, a))
        lanes = lambda a: (jnp.where(left, a, zero_l), jnp.where(left, zero_l, a))
        causal = lax.broadcasted_iota(jnp.int32, (tb, tb), 0) <= lax.broadcasted_iota(jnp.int32, (tb, tb), 1)
        ones_q = jnp.ones((tb, 128), BF16)
        ones_k = jnp.ones((8, tb), BF16)
        c_refs = (ca_ref, cb_ref)

        qt_ref[...] = (q_ref[...].astype(F32) * scale).T.astype(BF16)
        kt_ref[...] = k_ref[...].astype(F32).T.astype(BF16)
        do_t = do_ref[...].T
        dot_ref[...] = do_t.astype(BF16)
        prod_t = do_t * o_ref[...].T
        delta_ref[0:1, :] = jnp.sum(jnp.where(top_l, prod_t, 0.0), axis=0, keepdims=True)
        delta_ref[1:2, :] = jnp.sum(jnp.where(top_l, 0.0, prod_t), axis=0, keepdims=True)
        dqt_ref[...] = jnp.zeros(dqt_ref.shape, F32)
        dcq_ref[...] = jnp.zeros(dcq_ref.shape, F32)

        def kv_block(j, _):
            koff = pl.multiple_of(j * tb, tb)
            k2 = k_ref[pl.ds(koff, tb), :]
            v2 = v_ref[pl.ds(koff, tb), :].astype(BF16)
            kts = rows(kt_ref[:, pl.ds(koff, tb)])
            cw = tuple(_wide(c_refs[h][pl.ds(koff, tb), :], tb) for h in (0, 1))

            def q_block(i, carry, masked):
                dk, dv, dca, dcb = carry
                qoff = pl.multiple_of(i * tb, tb)
                qs = lanes((q_ref[pl.ds(qoff, tb), :].astype(F32) * scale).astype(BF16))
                dos = lanes(do_ref[pl.ds(qoff, tb), :].astype(BF16))
                qts = rows(qt_ref[:, pl.ds(qoff, tb)])
                dots = rows(dot_ref[:, pl.ds(qoff, tb)])
                dq_t, dcs = 0.0, []
                for h in (0, 1):
                    st = jnp.dot(k2, qts[h], preferred_element_type=F32) - cw[h]
                    p = jnp.exp(st - lse_ref[h:h + 1, pl.ds(qoff, tb)])
                    if masked:
                        p = jnp.where(causal, p, 0.0)
                    dp = jnp.dot(v2, dots[h], preferred_element_type=F32)
                    dsb = (p * (dp - delta_ref[h:h + 1, pl.ds(qoff, tb)])).astype(BF16)
                    dv = dv + jnp.dot(p.astype(BF16), dos[h], preferred_element_type=F32)
                    dk = dk + jnp.dot(dsb, qs[h], preferred_element_type=F32)
                    dq_t = dq_t + jnp.dot(kts[h], dsb, preferred_element_type=F32)
                    dcs.append(jnp.dot(dsb, ones_q, preferred_element_type=F32))
                    dcq_ref[h:h + 1, pl.ds(qoff, tb)] += jnp.dot(ones_k, dsb, preferred_element_type=F32)[0:1, :]
                dqt_ref[:, pl.ds(qoff, tb)] += dq_t
                return dk, dv, dca - dcs[0], dcb - dcs[1]

            z = jnp.zeros((tb, 128), F32)
            carry = q_block(j, (z, z, z, z), True)
            dk, dv, dca, dcb = lax.fori_loop(j + 1, nb, lambda i, c: q_block(i, c, False), carry)
            dk_ref[pl.ds(koff, tb), :] = dk
            dv_ref[pl.ds(koff, tb), :] = dv
            dc_ref[pl.ds(koff, tb), 0:128] = dca
            dc_ref[pl.ds(koff, tb), 128:256] = dcb
            return 0

        lax.fori_loop(0, nb, kv_block, 0)
        dq_ref[...] = (dqt_ref[...] * scale).T

    blk = pl.BlockSpec((l, 128), lambda b, hp: (b, hp))
    cspec = lambda k: pl.BlockSpec((None, l, 128), lambda b, hp: (b * N_FOX_HEADS + 2 * hp + k, 0, 0))
    rows2 = pl.BlockSpec((None, 2, l), lambda b, hp: (b * N_PAIRS + hp, 0, 0))
    wide = jax.ShapeDtypeStruct((t, FOX_WIDTH), F32)
    return pl.pallas_call(
        body, name="fox_bwd", grid=(seqs, N_PAIRS),
        in_specs=[blk, blk, pl.BlockSpec((l, 128), lambda b, hp: (b, V_BLOCK0 + hp)), cspec(0), cspec(1), blk, blk, rows2],
        out_specs=[blk, blk, blk, pl.BlockSpec((None, l, 256), lambda b, hp: (b * N_PAIRS + hp, 0, 0)), rows2],
        out_shape=[wide, wide, wide, jax.ShapeDtypeStruct((seqs * N_PAIRS, l, 256), F32),
                   jax.ShapeDtypeStruct((seqs * N_PAIRS, 2, l), F32)],
        scratch_shapes=[pltpu.VMEM((128, l), BF16), pltpu.VMEM((128, l), BF16), pltpu.VMEM((128, l), BF16),
                        pltpu.VMEM((2, l), F32), pltpu.VMEM((128, l), F32)],
        compiler_params=_params(("parallel", "parallel")),
    )(qn, kn, qkv, c_wide, c_wide, o, do, lse)


SCAN_ROWS = 256
SCAN_COLS = 1024


def _scan_fwd(bur, bui, ar, ai, seqs):
    t, ch = bur.shape
    l = t // seqs
    tl, cb = min(SCAN_ROWS, l), min(SCAN_COLS, ch)
    nl = l // tl

    def body(br_ref, bi_ref, ar_ref, ai_ref, xr_ref, xi_ref, cr, ci):
        @pl.when(pl.program_id(2) == 0)
        def _():
            cr[...] = jnp.zeros(cr.shape, F32)
            ci[...] = jnp.zeros(ci.shape, F32)

        a_r, a_i = ar_ref[...], ai_ref[...]

        def step(tt, carry):
            xr, xi = carry
            nr = a_r * xr - a_i * xi + br_ref[pl.ds(tt, 1), :]
            ni = a_r * xi + a_i * xr + bi_ref[pl.ds(tt, 1), :]
            xr_ref[pl.ds(tt, 1), :] = nr
            xi_ref[pl.ds(tt, 1), :] = ni
            return nr, ni

        xr, xi = lax.fori_loop(0, tl, step, (cr[...], ci[...]), unroll=8)
        cr[...] = xr
        ci[...] = xi

    blk = pl.BlockSpec((tl, cb), lambda s, j, r: (s * nl + r, j))
    par = pl.BlockSpec((1, cb), lambda s, j, r: (0, j))
    return pl.pallas_call(
        body, name="s5_scan_fwd", grid=(seqs, ch // cb, nl),
        in_specs=[blk, blk, par, par], out_specs=[blk, blk],
        out_shape=[jax.ShapeDtypeStruct((t, ch), F32)] * 2,
        scratch_shapes=[pltpu.VMEM((1, cb), F32), pltpu.VMEM((1, cb), F32)],
        compiler_params=_params(("parallel", "parallel", "arbitrary")),
    )(bur, bui, ar, ai)


def _scan_bwd(gr, gi, xr, xi, ar, ai, seqs):
    t, ch = gr.shape
    l = t // seqs
    tl, cb = min(SCAN_ROWS, l), min(SCAN_COLS, ch)
    nl = l // tl

    def body(gr_ref, gi_ref, xr_ref, xi_ref, ar_ref, ai_ref, lr_ref, li_ref, dar_ref, dai_ref, cr, ci):
        @pl.when(pl.program_id(2) == 0)
        def _():
            cr[...] = jnp.zeros(cr.shape, F32)
            ci[...] = jnp.zeros(ci.shape, F32)
            dar_ref[...] = jnp.zeros(dar_ref.shape, F32)
            dai_ref[...] = jnp.zeros(dai_ref.shape, F32)

        a_r, a_i = ar_ref[...], ai_ref[...]

        def step(k, carry):
            lr, li, dar, dai = carry
            tt = tl - 1 - k
            xr_t = xr_ref[pl.ds(tt, 1), :]
            xi_t = xi_ref[pl.ds(tt, 1), :]
            dar = dar + lr * xr_t + li * xi_t
            dai = dai + li * xr_t - lr * xi_t
            nr = gr_ref[pl.ds(tt, 1), :] + a_r * lr + a_i * li
            ni = gi_ref[pl.ds(tt, 1), :] + a_r * li - a_i * lr
            lr_ref[pl.ds(tt, 1), :] = nr
            li_ref[pl.ds(tt, 1), :] = ni
            return nr, ni, dar, dai

        lr, li, dar, dai = lax.fori_loop(
            0, tl, step, (cr[...], ci[...], jnp.zeros((1, cb), F32), jnp.zeros((1, cb), F32)), unroll=8)
        cr[...] = lr
        ci[...] = li
        dar_ref[...] += dar
        dai_ref[...] += dai

    blk = pl.BlockSpec((tl, cb), lambda s, j, r: (s * nl + nl - 1 - r, j))
    par = pl.BlockSpec((1, cb), lambda s, j, r: (0, j))
    acc = pl.BlockSpec((None, 1, cb), lambda s, j, r: (s, 0, j))
    lr, li, dar, dai = pl.pallas_call(
        body, name="s5_scan_bwd", grid=(seqs, ch // cb, nl),
        in_specs=[blk, blk, blk, blk, par, par], out_specs=[blk, blk, acc, acc],
        out_shape=[jax.ShapeDtypeStruct((t, ch), F32)] * 2 + [jax.ShapeDtypeStruct((seqs, 1, ch), F32)] * 2,
        scratch_shapes=[pltpu.VMEM((1, cb), F32), pltpu.VMEM((1, cb), F32)],
        compiler_params=_params(("parallel", "parallel", "arbitrary")),
    )(gr, gi, xr, xi, ar, ai)
    return lr, li, dar, dai


XATT_BLOCK = 512


def _xatt_probs(qv, kv):
    s = lax.dot_general(qv, kv, _NT, preferred_element_type=F32) * (X_HEAD_DIM ** -0.5)
    e = jnp.exp(s - jnp.max(s, axis=-1, keepdims=True))
    return e / jnp.sum(e, axis=-1, keepdims=True)


def _xatt_fwd(q, k, kv, seqs):
    t = q.shape[0]
    tq = min(XATT_BLOCK, t // seqs)
    nq = t // seqs // tq

    def body(q_ref, k_ref, v_ref, o_ref):
        p = _xatt_probs(q_ref[...], k_ref[...])
        o_ref[...] = jnp.dot(p.astype(BF16), v_ref[...].astype(BF16), preferred_element_type=F32).astype(o_ref.dtype)

    qs = pl.BlockSpec((tq, X_HEAD_DIM), lambda b, h, i: (b * nq + i, h))
    return pl.pallas_call(
        body, name="xatt_fwd", grid=(seqs, N_X_HEADS, nq),
        in_specs=[qs, pl.BlockSpec((N_MEM, X_HEAD_DIM), lambda b, h, i: (b, h)),
                  pl.BlockSpec((N_MEM, X_HEAD_DIM), lambda b, h, i: (b, N_X_HEADS + h))],
        out_specs=qs, out_shape=jax.ShapeDtypeStruct(q.shape, BF16),
        compiler_params=_params(("parallel", "parallel", "parallel")),
    )(q, k, kv)


def _xatt_bwd(q, k, kv, do, seqs):
    t = q.shape[0]
    tq = min(XATT_BLOCK, t // seqs)
    nq = t // seqs // tq
    scale = X_HEAD_DIM ** -0.5

    def body(q_ref, k_ref, v_ref, do_ref, dq_ref, dk_ref, dv_ref):
        @pl.when(pl.program_id(2) == 0)
        def _():
            dk_ref[...] = jnp.zeros(dk_ref.shape, F32)
            dv_ref[...] = jnp.zeros(dv_ref.shape, F32)

        qv, kk = q_ref[...], k_ref[...]
        p = _xatt_probs(qv, kk)
        dob = do_ref[...].astype(BF16)
        dp = lax.dot_general(dob, v_ref[...].astype(BF16), _NT, preferred_element_type=F32)
        ds = p * (dp - jnp.sum(dp * p, axis=-1, keepdims=True))
        dsb = ds.astype(BF16)
        dq_ref[...] = jnp.dot(dsb, kk, preferred_element_type=F32) * scale
        dk_ref[...] += lax.dot_general(dsb, qv, _TN, preferred_element_type=F32) * scale
        dv_ref[...] += lax.dot_general(p.astype(BF16), dob, _TN, preferred_element_type=F32)

    qs = pl.BlockSpec((tq, X_HEAD_DIM), lambda b, h, i: (b * nq + i, h))
    ks = pl.BlockSpec((N_MEM, X_HEAD_DIM), lambda b, h, i: (b, h))
    return pl.pallas_call(
        body, name="xatt_bwd", grid=(seqs, N_X_HEADS, nq),
        in_specs=[qs, ks, pl.BlockSpec((N_MEM, X_HEAD_DIM), lambda b, h, i: (b, N_X_HEADS + h)), qs],
        out_specs=[qs, ks, ks],
        out_shape=[jax.ShapeDtypeStruct(q.shape, F32), jax.ShapeDtypeStruct(k.shape, F32),
                   jax.ShapeDtypeStruct(k.shape, F32)],
        compiler_params=_params(("parallel", "parallel", "arbitrary")),
    )(q, k, kv, do)


CONV_COLS = 256


def _shift_down(x, k, row):
    return jnp.where(row >= k, pltpu.roll(x, k, 0), 0.0)


def _shift_up(x, k, row):
    n = x.shape[0]
    return jnp.where(row < n - k, pltpu.roll(x, n - k, 0), 0.0)


def _conv_pre(g, w, b, row):
    return b + w[0:1, :] * _shift_down(g, 2, row) + w[1:2, :] * _shift_down(g, 1, row) + w[2:3, :] * g


def _convgate_fwd(gu, w, b, seqs):
    t = gu.shape[0]
    l = t // seqs
    nc = D_FF // CONV_COLS

    def body(g_ref, u_ref, w_ref, b_ref, o_ref):
        g = g_ref[...]
        row = lax.broadcasted_iota(jnp.int32, g.shape, 0)
        pre = _conv_pre(g, w_ref[...], b_ref[...], row)
        o_ref[...] = (pre * jax.nn.sigmoid(pre) * u_ref[...]).astype(o_ref.dtype)

    return pl.pallas_call(
        body, name="convgate_fwd", grid=(seqs, nc),
        in_specs=[pl.BlockSpec((l, CONV_COLS), lambda s, j: (s, j)), pl.BlockSpec((l, CONV_COLS), lambda s, j: (s, nc + j)),
                  pl.BlockSpec((3, CONV_COLS), lambda s, j: (0, j)), pl.BlockSpec((1, CONV_COLS), lambda s, j: (0, j))],
        out_specs=pl.BlockSpec((l, CONV_COLS), lambda s, j: (s, j)),
        out_shape=jax.ShapeDtypeStruct((t, D_FF), BF16),
        compiler_params=_params(("parallel", "parallel")),
    )(gu, gu, w, b)


def _convgate_bwd(gu, w, b, dact, seqs):
    t = gu.shape[0]
    l = t // seqs
    nc = D_FF // CONV_COLS

    def body(g_ref, u_ref, w_ref, b_ref, da_ref, dg_ref, du_ref, dw_ref, db_ref):
        @pl.when(pl.program_id(1) == 0)
        def _():
            dw_ref[...] = jnp.zeros(dw_ref.shape, F32)
            db_ref[...] = jnp.zeros(db_ref.shape, F32)

        g, wv, da = g_ref[...], w_ref[...], da_ref[...]
        row = lax.broadcasted_iota(jnp.int32, g.shape, 0)
        g1, g2 = _shift_down(g, 1, row), _shift_down(g, 2, row)
        pre = b_ref[...] + wv[0:1, :] * g2 + wv[1:2, :] * g1 + wv[2:3, :] * g
        sg = jax.nn.sigmoid(pre)
        silu = pre * sg
        du_ref[...] = (da * silu).astype(du_ref.dtype)
        dpre = da * u_ref[...] * (sg * (1.0 + pre * (1.0 - sg)))
        dg = wv[2:3, :] * dpre + wv[1:2, :] * _shift_up(dpre, 1, row) + wv[0:1, :] * _shift_up(dpre, 2, row)
        dg_ref[...] = dg.astype(dg_ref.dtype)
        dw_ref[0:1, :] += jnp.sum(dpre * g2, axis=0, keepdims=True)
        dw_ref[1:2, :] += jnp.sum(dpre * g1, axis=0, keepdims=True)
        dw_ref[2:3, :] += jnp.sum(dpre * g, axis=0, keepdims=True)
        db_ref[...] += jnp.sum(dpre, axis=0, keepdims=True)

    blk = lambda off: pl.BlockSpec((l, CONV_COLS), lambda j, s: (s, off + j))
    return pl.pallas_call(
        body, name="convgate_bwd", grid=(nc, seqs),
        in_specs=[blk(0), blk(nc), pl.BlockSpec((3, CONV_COLS), lambda j, s: (0, j)),
                  pl.BlockSpec((1, CONV_COLS), lambda j, s: (0, j)), blk(0)],
        out_specs=[blk(0), blk(0), pl.BlockSpec((3, CONV_COLS), lambda j, s: (0, j)),
                   pl.BlockSpec((1, CONV_COLS), lambda j, s: (0, j))],
        out_shape=[jax.ShapeDtypeStruct((t, D_FF), BF16), jax.ShapeDtypeStruct((t, D_FF), BF16),
                   jax.ShapeDtypeStruct((3, D_FF), F32), jax.ShapeDtypeStruct((1, D_FF), F32)],
        compiler_params=_params(("parallel", "arbitrary")),
    )(gu, gu, w, b, dact)


def _loss_head(h, target):
    t, d = h.shape
    tm = _pick(t, (256, 128, 8))

    def body(h_ref, t_ref, dh_ref, loss_ref):
        @pl.when(pl.program_id(0) == 0)
        def _():
            loss_ref[...] = jnp.zeros(loss_ref.shape, F32)

        e = h_ref[...] - t_ref[...]
        dh_ref[...] = e * (1.0 / d)
        loss_ref[...] += (0.5 / d) * jnp.sum(jnp.sum(e * e, axis=1, keepdims=True), axis=0, keepdims=True)

    blk = pl.BlockSpec((tm, d), lambda i: (i, 0))
    return pl.pallas_call(
        body, name="loss_head", grid=(t // tm,), in_specs=[blk, blk],
        out_specs=[blk, pl.BlockSpec((1, 1), lambda i: (0, 0))],
        out_shape=[jax.ShapeDtypeStruct((t, d), F32), jax.ShapeDtypeStruct((1, 1), F32)],
        compiler_params=_params(("arbitrary",)),
    )(h, target)


def _s5_discretise(a_re, a_im, log_dt, b_re, b_im):
    dt = jnp.exp(log_dt)[:, None]
    mag = jnp.exp(a_re * dt)
    lb_r = mag * jnp.cos(a_im * dt)
    lb_i = mag * jnp.sin(a_im * dt)
    den = a_re * a_re + a_im * a_im
    nr = lb_r - 1.0
    coef_r = (nr * a_re + lb_i * a_im) / den
    coef_i = (lb_i * a_re - nr * a_im) / den
    bb_r = coef_r[:, :, None] * b_re - coef_i[:, :, None] * b_im
    bb_i = coef_r[:, :, None] * b_im + coef_i[:, :, None] * b_re
    return lb_r, lb_i, bb_r, bb_i


S5_CHUNKS = 4
S5_PER = S5_GROUPS // S5_CHUNKS


def _blockdiag_in(bb):
    eye = jnp.eye(S5_PER, dtype=bb.dtype)
    return jnp.einsum("jgpc,gh->jgchp", bb.reshape(S5_CHUNKS, S5_PER, S5_STATE, S5_GROUP_CH), eye).reshape(
        S5_CHUNKS, S5_PER * S5_GROUP_CH, S5_PER * S5_STATE)


def _blockdiag_in_grad(d):
    eye = jnp.eye(S5_PER, dtype=d.dtype)
    return jnp.einsum("jgchp,gh->jgpc", d.reshape(S5_CHUNKS, S5_PER, S5_GROUP_CH, S5_PER, S5_STATE), eye).reshape(
        S5_GROUPS, S5_STATE, S5_GROUP_CH)


def _blockdiag_out(c):
    eye = jnp.eye(S5_PER, dtype=c.dtype)
    return jnp.einsum("jgcp,gh->jgphc", c.reshape(S5_CHUNKS, S5_PER, S5_GROUP_CH, S5_STATE), eye).reshape(
        S5_CHUNKS, S5_PER * S5_STATE, S5_PER * S5_GROUP_CH)


def _blockdiag_out_grad(d):
    eye = jnp.eye(S5_PER, dtype=d.dtype)
    return jnp.einsum("jgphc,gh->jgcp", d.reshape(S5_CHUNKS, S5_PER, S5_STATE, S5_PER, S5_GROUP_CH), eye).reshape(
        S5_GROUPS, S5_GROUP_CH, S5_STATE)


def _local_step(x3, mem3, target3, p, wb):
    seqs, l, d = x3.shape
    t = seqs * l
    x = x3.reshape(t, d)
    mem = mem3.reshape(seqs * N_MEM, d)
    target = target3.reshape(t, d)
    full = lambda a: (a, a.shape[1], 0, 0)

    s5_in = (p["s5_a_re"], p["s5_a_im"], p["s5_log_dt"], p["s5_b_re"], p["s5_b_im"])
    (lb_r, lb_i, bb_r, bb_i), s5_pull = jax.vjp(_s5_discretise, *s5_in)
    ar, ai = lb_r.reshape(1, S5_CH), lb_i.reshape(1, S5_CH)
    bbr_d, bbi_d = _blockdiag_in(bb_r).astype(BF16), _blockdiag_in(bb_i).astype(BF16)
    cr_d, ci_d = _blockdiag_out(p["s5_c_re"]).astype(BF16), (-_blockdiag_out(p["s5_c_im"])).astype(BF16)
    d_row = p["s5_d"].reshape(1, S5_WIDTH)

    w_in = wb["w_in"]
    w_qkv = w_in[:, :3 * FOX_WIDTH]
    w_uf = jnp.concatenate(
        [w_in[:, 3 * FOX_WIDTH + N_FOX_HEADS:], w_in[:, 3 * FOX_WIDTH:3 * FOX_WIDTH + N_FOX_HEADS],
         jnp.zeros((d, UF_COLS - S5_WIDTH - N_FOX_HEADS), w_in.dtype)], axis=1)

    hn1 = _rowwise(_rms, [full(x)], [p["norm_mix"]], [(d, d, 0, BF16)], "norm_mix_fwd")
    qkv = _mm(hn1, w_qkv, "nn", "in_qkv")
    uf = _mm(hn1, w_uf, "nn", "in_uf")

    bh = seqs * N_FOX_HEADS
    q_pair = (qkv, 128, 0, 1)
    k_pair = (qkv, 128, N_PAIRS, 1)
    gq2, gk2 = jnp.tile(p["fox_q_norm"], (1, 2)), jnp.tile(p["fox_k_norm"], (1, 2))
    pair_out = [(FOX_WIDTH, 128, 1, BF16)]
    qn = _rowwise(_rms_pair, [q_pair], [gq2], pair_out, "fox_qnorm_fwd", heads=N_PAIRS)
    kn = _rowwise(_rms_pair, [k_pair], [gk2], pair_out, "fox_knorm_fwd", heads=N_PAIRS)

    f_rows = uf[:, S5_WIDTH:S5_WIDTH + N_FOX_HEADS].reshape(seqs, l, N_FOX_HEADS).transpose(0, 2, 1).reshape(bh, l)
    f_bias = jnp.tile(p["fox_f_bias"].reshape(N_FOX_HEADS, 1), (seqs, 1))
    c_wide = jnp.broadcast_to(_forget_fwd(f_rows, f_bias)[:, :, None], (bh, l, 128))
    fox, lse = _fox_fwd(qn, kn, qkv, c_wide, seqs)

    bur = _gmm(uf, bbr_d, "nn", "s5_bu_re")
    bui = _gmm(uf, bbi_d, "nn", "s5_bu_im")
    xr, xi = _scan_fwd(bur, bui, ar, ai, seqs)
    ys = _gmm(xi, ci_d, "nn", "s5_y_im", res=_gmm(xr, cr_d, "nn", "s5_y_re"))
    u_blk = (uf, S5_WIDTH, 0, 0)
    yg = _rowwise(_s5_act, [full(ys), u_blk], [d_row], [(S5_WIDTH, S5_WIDTH, 0, F32)], "s5_act_fwd")
    z = _mm(yg, wb["s5_w_glu"], "nn", "s5_glu")
    y2n = _rowwise(_s5_gate, [full(yg), full(z)], [p["s5_b_glu"], p["out_norm_s5"]],
                   [(S5_WIDTH, S5_WIDTH, 0, BF16)], "s5_gate_fwd")
    foxn = _rowwise(_rms, [full(fox)], [p["out_norm_fox"]], [(FOX_WIDTH, FOX_WIDTH, 0, BF16)], "fox_outnorm_fwd")
    mixed = jnp.concatenate([foxn, y2n], axis=1)
    h1 = _mm(mixed, wb["w_out"], "nn", "mix_out", res=x)

    hn2 = _rowwise(_rms, [full(h1)], [p["norm_cross"]], [(d, d, 0, BF16)], "norm_cross_fwd")
    mn = _rowwise(_rms, [full(mem)], [p["norm_mem"]], [(d, d, 0, BF16)], "norm_mem_fwd")
    xq_raw = _mm(hn2, wb["w_xq"], "nn", "x_q")
    kv = _mm(mn, wb["w_xkv"], "nn", "x_kv")
    xh = lambda a: (a, X_HEAD_DIM, 0, 1)
    xqn = _rowwise(_rms, [xh(xq_raw)], [p["xq_norm"]], [(d, X_HEAD_DIM, 1, BF16)], "x_qnorm_fwd", heads=N_X_HEADS)
    xkn = _rowwise(_rms, [xh(kv)], [p["xk_norm"]], [(d, X_HEAD_DIM, 1, BF16)], "x_knorm_fwd", heads=N_X_HEADS)
    xo = _xatt_fwd(xqn, xkn, kv, seqs)
    h2 = _mm(xo, wb["w_xo"], "nn", "x_out", res=h1)

    hn3 = _rowwise(_rms, [full(h2)], [p["norm_ffn"]], [(d, d, 0, BF16)], "norm_ffn_fwd")
    gu = _mm(hn3, wb["w_ffn_up"], "nn", "ffn_up")
    act = _convgate_fwd(gu, p["ffn_conv_w"], p["ffn_conv_b"], seqs)
    h3 = _mm(act, wb["w_ffn_down"], "nn", "ffn_down", res=h2)
    dh3, loss = _loss_head(h3, target)

    g = {}
    dact = _mm(dh3, wb["w_ffn_down"], "nt", "ffn_down_dx")
    g["w_ffn_down"] = _mm(act, dh3, "tn", "ffn_down_dw")
    dgate, dup, g["ffn_conv_w"], g["ffn_conv_b"] = _convgate_bwd(gu, p["ffn_conv_w"], p["ffn_conv_b"], dact, seqs)
    dgu = jnp.concatenate([dgate, dup], axis=1)
    dhn3 = _mm(dgu, wb["w_ffn_up"], "nt", "ffn_up_dx")
    g["w_ffn_up"] = _mm(hn3, dgu, "tn", "ffn_up_dw")
    (dh2,), (g["norm_ffn"],) = _rowwise_vjp(_rms, [full(h2)], [p["norm_ffn"]], [full(dhn3)], "norm_ffn_bwd",
                                            adds=[full(dh3)])

    dxo = _mm(dh2, wb["w_xo"], "nt", "x_out_dx")
    g["w_xo"] = _mm(xo, dh2, "tn", "x_out_dw")
    dxqn, dxkn, dxv = _xatt_bwd(xqn, xkn, kv, dxo, seqs)
    (dxq_raw,), (g["xq_norm"],) = _rowwise_vjp(_rms, [xh(xq_raw)], [p["xq_norm"]], [xh(dxqn)], "x_qnorm_bwd",
                                               heads=N_X_HEADS, row_dtypes=[BF16])
    (dxk_raw,), (g["xk_norm"],) = _rowwise_vjp(_rms, [xh(kv)], [p["xk_norm"]], [xh(dxkn)], "x_knorm_bwd",
                                               heads=N_X_HEADS, row_dtypes=[BF16])
    dkv = jnp.concatenate([dxk_raw, dxv.astype(BF16)], axis=1)
    dhn2 = _mm(dxq_raw, wb["w_xq"], "nt", "x_q_dx")
    g["w_xq"] = _mm(hn2, dxq_raw, "tn", "x_q_dw")
    dmn = _mm(dkv, wb["w_xkv"], "nt", "x_kv_dx")
    g["w_xkv"] = _mm(mn, dkv, "tn", "x_kv_dw")
    (dh1,), (g["norm_cross"],) = _rowwise_vjp(_rms, [full(h1)], [p["norm_cross"]], [full(dhn2)], "norm_cross_bwd",
                                              adds=[full(dh2)])
    _, (g["norm_mem"],) = _rowwise_vjp(_rms, [full(mem)], [p["norm_mem"]], [full(dmn)], "norm_mem_bwd",
                                       row_dtypes=[BF16])

    dmixed = _mm(dh1, wb["w_out"], "nt", "mix_out_dx")
    g["w_out"] = _mm(mixed, dh1, "tn", "mix_out_dw")
    (dfox,), (g["out_norm_fox"],) = _rowwise_vjp(_rms, [full(fox)], [p["out_norm_fox"]],
                                                 [(dmixed, FOX_WIDTH, 0, 0)], "fox_outnorm_bwd")
    (dyg_a, dz), (g["s5_b_glu"], g["out_norm_s5"]) = _rowwise_vjp(
        _s5_gate, [full(yg), full(z)], [p["s5_b_glu"], p["out_norm_s5"]], [(dmixed, S5_WIDTH, 1, 0)], "s5_gate_bwd",
        row_dtypes=[F32, BF16])
    dyg = _mm(dz, wb["s5_w_glu"], "nt", "s5_glu_dx", res=dyg_a)
    g["s5_w_glu"] = _mm(yg, dz, "tn", "s5_glu_dw")
    (dys, du_a), (dd_row,) = _rowwise_vjp(_s5_act, [full(ys), u_blk], [d_row], [full(dyg)], "s5_act_bwd",
                                          row_dtypes=[BF16, F32])
    g["s5_d"] = dd_row
    cin, cst = S5_PER * S5_GROUP_CH, S5_PER * S5_STATE
    dxr = _gmm(dys, cr_d, "nt", "s5_y_re_dx")
    dxi = _gmm(dys, ci_d, "nt", "s5_y_im_dx")
    dcr_d = _gmm_tn(xr, dys, cst, cin, S5_CHUNKS, "s5_y_re_dw")
    dci_d = _gmm_tn(xi, dys, cst, cin, S5_CHUNKS, "s5_y_im_dw")
    lam_r, lam_i, dar, dai = _scan_bwd(dxr, dxi, xr, xi, ar, ai, seqs)
    du_b = _gmm(lam_i, bbi_d, "nt", "s5_bu_im_dx", res=_gmm(lam_r, bbr_d, "nt", "s5_bu_re_dx"))
    dbbr_d = _gmm_tn(uf, lam_r, cin, cst, S5_CHUNKS, "s5_bu_re_dw")
    dbbi_d = _gmm_tn(uf, lam_i, cin, cst, S5_CHUNKS, "s5_bu_im_dw")
    d_lb_r = jnp.sum(dar, axis=0).reshape(S5_GROUPS, S5_STATE)
    d_lb_i = jnp.sum(dai, axis=0).reshape(S5_GROUPS, S5_STATE)
    g["s5_a_re"], g["s5_a_im"], g["s5_log_dt"], g["s5_b_re"], g["s5_b_im"] = s5_pull(
        (d_lb_r, d_lb_i, _blockdiag_in_grad(dbbr_d), _blockdiag_in_grad(dbbi_d)))
    g["s5_c_re"] = _blockdiag_out_grad(dcr_d)
    g["s5_c_im"] = -_blockdiag_out_grad(dci_d)

    dqn, dkn, dv, dc, dcq = _fox_bwd(qn, kn, qkv, c_wide, fox, dfox, lse, seqs)
    pair = lambda a: (a, 128, 0, 1)
    (dq_raw,), (dgq2,) = _rowwise_vjp(_rms_pair, [q_pair], [gq2], [pair(dqn)], "fox_qnorm_bwd", heads=N_PAIRS,
                                      row_dtypes=[BF16])
    (dk_raw,), (dgk2,) = _rowwise_vjp(_rms_pair, [k_pair], [gk2], [pair(dkn)], "fox_knorm_bwd", heads=N_PAIRS,
                                      row_dtypes=[BF16])
    g["fox_q_norm"] = dgq2[:, :HEAD_DIM] + dgq2[:, HEAD_DIM:]
    g["fox_k_norm"] = dgk2[:, :HEAD_DIM] + dgk2[:, HEAD_DIM:]
    dc_rows = jnp.stack([dc[:, :, 0], dc[:, :, 128]], axis=1).reshape(bh, l)
    df_rows, dfb = _forget_bwd(f_rows, f_bias, dc_rows + dcq.reshape(bh, l))
    g["fox_f_bias"] = jnp.sum(dfb.reshape(seqs, N_FOX_HEADS), axis=0)
    df = df_rows.reshape(seqs, N_FOX_HEADS, l).transpose(0, 2, 1).reshape(t, N_FOX_HEADS)
    dqkv = jnp.concatenate([dq_raw, dk_raw, dv.astype(BF16)], axis=1)
    duf = jnp.concatenate([du_a + du_b, df, jnp.zeros((t, UF_COLS - S5_WIDTH - N_FOX_HEADS), F32)],
                          axis=1).astype(BF16)
    dhn1 = _mm(duf, w_uf, "nt", "in_uf_dx", res=_mm(dqkv, w_qkv, "nt", "in_qkv_dx"))
    dw_qkv = _mm(hn1, dqkv, "tn", "in_qkv_dw")
    dw_uf = _mm(hn1, duf, "tn", "in_uf_dw")
    g["w_in"] = jnp.concatenate([dw_qkv, dw_uf[:, S5_WIDTH:S5_WIDTH + N_FOX_HEADS], dw_uf[:, :S5_WIDTH]], axis=1)
    (dx,), (g["norm_mix"],) = _rowwise_vjp(_rms, [full(x)], [p["norm_mix"]], [full(dhn1)], "norm_mix_bwd",
                                           adds=[full(dh1)])
    return loss, dx.reshape(seqs, l, d), g


def _place():
    return lax.axis_index("x"), lax.axis_index("y"), lax.axis_index("c")


def _other_chips(x, y):
    return [(1 - x, y), (x, 1 - y), (1 - x, 1 - y)]


ANY = pl.BlockSpec(memory_space=pl.ANY)


def _gather_weights(shards, col_kind, taps):
    n = len(shards)

    def body(*refs):
        ins, tap_in, outs, tap_out = refs[:n], refs[n], refs[n + 1:2 * n + 1], refs[2 * n + 1]
        ici_send, ici_recv, d2d_send, d2d_recv, own_send, own_recv = refs[2 * n + 2:]
        x, y, c = _place()
        mine = 2 * x + y
        chips = _other_chips(x, y)
        sibling = (x, y, 1 - c)

        def piece(a, s, h):
            r, cs = ins[a].shape
            hr = r // 2
            if col_kind[a]:
                return outs[a].at[pl.ds(pl.multiple_of(h * hr, 16), hr), pl.ds(pl.multiple_of(s * cs, 128), cs)]
            return outs[a].at[pl.ds(pl.multiple_of(s * r + h * hr, 16), hr), :]

        def slab(a, s):
            r, cs = ins[a].shape
            if col_kind[a]:
                return outs[a].at[:, pl.ds(pl.multiple_of(s * cs, 128), cs)]
            return outs[a].at[pl.ds(pl.multiple_of(s * r, 16), r), :]

        def own_half(a, h):
            hr = ins[a].shape[0] // 2
            return ins[a].at[pl.ds(pl.multiple_of(h * hr, 16), hr), :]

        sends = []
        for a in range(n):
            cp = pltpu.make_async_remote_copy(
                src_ref=ins[a], dst_ref=slab(a, mine), send_sem=own_send.at[a], recv_sem=own_recv.at[a],
                device_id=sibling, device_id_type=MESH)
            cp.start()
            sends.append(cp)
        cp = pltpu.make_async_remote_copy(
            src_ref=tap_in, dst_ref=tap_out.at[mine], send_sem=own_send.at[n], recv_sem=own_recv.at[n],
            device_id=sibling, device_id_type=MESH)
        cp.start()
        sends.append(cp)
        for a in range(n):
            for j, (px, py) in enumerate(chips):
                cp = pltpu.make_async_remote_copy(
                    src_ref=own_half(a, c), dst_ref=piece(a, mine, c), send_sem=ici_send.at[3 * a + j],
                    recv_sem=ici_recv.at[3 * a + j], device_id=(px, py, c), device_id_type=MESH)
                cp.start()
                sends.append(cp)
        for j, (px, py) in enumerate(chips):
            cp = pltpu.make_async_remote_copy(
                src_ref=tap_in, dst_ref=tap_out.at[mine], send_sem=ici_send.at[3 * n + j],
                recv_sem=ici_recv.at[3 * n + j], device_id=(px, py, c), device_id_type=MESH)
            cp.start()
            sends.append(cp)
        for a in range(n):
            for j, (px, py) in enumerate(chips):
                got = piece(a, 2 * px + py, c)
                pltpu.make_async_remote_copy(
                    src_ref=got, dst_ref=got, send_sem=ici_send.at[3 * a + j], recv_sem=ici_recv.at[3 * a + j],
                    device_id=(px, py, c), device_id_type=MESH).wait_recv()
                fwd = pltpu.make_async_remote_copy(
                    src_ref=got, dst_ref=got, send_sem=d2d_send.at[3 * a + j], recv_sem=d2d_recv.at[3 * a + j],
                    device_id=(x, y, 1 - c), device_id_type=MESH)
                fwd.start()
                sends.append(fwd)
        for a in range(n):
            for j, (px, py) in enumerate(chips):
                other = piece(a, 2 * px + py, 1 - c)
                pltpu.make_async_remote_copy(
                    src_ref=other, dst_ref=other, send_sem=d2d_send.at[3 * a + j], recv_sem=d2d_recv.at[3 * a + j],
                    device_id=(x, y, 1 - c), device_id_type=MESH).wait_recv()
        for j, (px, py) in enumerate(chips):
            pltpu.make_async_remote_copy(
                src_ref=tap_in, dst_ref=tap_out.at[2 * px + py], send_sem=ici_send.at[3 * n + j],
                recv_sem=ici_recv.at[3 * n + j], device_id=(px, py, c), device_id_type=MESH).wait_recv()
        for a in range(n):
            pltpu.make_async_remote_copy(
                src_ref=ins[a], dst_ref=slab(a, mine), send_sem=own_send.at[a], recv_sem=own_recv.at[a],
                device_id=sibling, device_id_type=MESH).wait_recv()
        pltpu.make_async_remote_copy(
            src_ref=tap_in, dst_ref=tap_out.at[mine], send_sem=own_send.at[n], recv_sem=own_recv.at[n],
            device_id=sibling, device_id_type=MESH).wait_recv()
        for cp in sends:
            cp.wait_send()

    def full_shape(a):
        r, cs = shards[a].shape
        return (r, 4 * cs) if col_kind[a] else (4 * r, cs)

    res = pl.pallas_call(
        body, name="gather_weights", in_specs=[ANY] * (n + 1), out_specs=[ANY] * (n + 1),
        out_shape=[jax.ShapeDtypeStruct(full_shape(a), shards[a].dtype) for a in range(n)]
        + [jax.ShapeDtypeStruct((4,) + taps.shape, taps.dtype)],
        scratch_shapes=[pltpu.SemaphoreType.DMA((3 * n + 3,)), pltpu.SemaphoreType.DMA((3 * n + 3,)),
                        pltpu.SemaphoreType.DMA((3 * n,)), pltpu.SemaphoreType.DMA((3 * n,)),
                        pltpu.SemaphoreType.DMA((n + 1,)), pltpu.SemaphoreType.DMA((n + 1,))],
        compiler_params=pltpu.CompilerParams(has_side_effects=True),
    )(*shards, taps)
    return res[:n], res[n]


def _pair_exchange_halves(grads, col_kind):
    n = len(grads)

    def body(*refs):
        ins, outs = refs[:n], refs[n:2 * n]
        send_sems, recv_sems = refs[2 * n:]
        x, y, c = _place()
        copies = []
        for a in range(n):
            if col_kind[a]:
                hr = ins[a].shape[0] // 2
                src = ins[a].at[pl.ds(pl.multiple_of((1 - c) * hr, 8), hr), :]
            else:
                hr = ins[a].shape[1] // 2
                src = ins[a].at[:, pl.ds(pl.multiple_of((1 - c) * hr, 8), hr), :]
            cp = pltpu.make_async_remote_copy(
                src_ref=src, dst_ref=outs[a], send_sem=send_sems.at[a], recv_sem=recv_sems.at[a],
                device_id=(x, y, 1 - c), device_id_type=MESH)
            cp.start()
            copies.append(cp)
        for cp in copies:
            cp.wait()

    def half_shape(a):
        s = grads[a].shape
        return (s[0] // 2, s[1]) if col_kind[a] else (4, s[1] // 2, s[2])

    return pl.pallas_call(
        body, name="reduce_pair_exchange", in_specs=[ANY] * n, out_specs=[ANY] * n,
        out_shape=[jax.ShapeDtypeStruct(half_shape(a), grads[a].dtype) for a in range(n)],
        scratch_shapes=[pltpu.SemaphoreType.DMA((n,)), pltpu.SemaphoreType.DMA((n,))],
        compiler_params=pltpu.CompilerParams(has_side_effects=True),
    )(*grads)


def _chip_exchange(sums, col_kind):
    n = len(sums)

    def piece_shape(a):
        s = sums[a].shape
        return (s[0], s[1] // 4) if col_kind[a] else (s[1], s[2])

    def body(*refs):
        ins, outs = refs[:n], refs[n:2 * n]
        send_sems, recv_sems = refs[2 * n:]
        x, y, c = _place()
        copies = []
        for a in range(n):
            for j, (px, py) in enumerate(_other_chips(x, y)):
                if col_kind[a]:
                    cs = piece_shape(a)[1]
                    src = ins[a].at[:, pl.ds(pl.multiple_of((2 * px + py) * cs, 128), cs)]
                else:
                    src = ins[a].at[2 * px + py]
                cp = pltpu.make_async_remote_copy(
                    src_ref=src, dst_ref=outs[a].at[j], send_sem=send_sems.at[3 * a + j],
                    recv_sem=recv_sems.at[3 * a + j], device_id=(px, py, c), device_id_type=MESH)
                cp.start()
                copies.append(cp)
        for cp in copies:
            cp.wait()

    return pl.pallas_call(
        body, name="reduce_chip_exchange", in_specs=[ANY] * n, out_specs=[ANY] * n,
        out_shape=[jax.ShapeDtypeStruct((3,) + piece_shape(a), sums[a].dtype) for a in range(n)],
        scratch_shapes=[pltpu.SemaphoreType.DMA((3 * n,)), pltpu.SemaphoreType.DMA((3 * n,))],
        compiler_params=pltpu.CompilerParams(has_side_effects=True),
    )(*sums)


def _pair_swap_halves(halves):
    n = len(halves)

    def body(*refs):
        ins, outs = refs[:n], refs[n:2 * n]
        send_sems, recv_sems = refs[2 * n:]
        x, y, c = _place()
        copies = []
        for a in range(n):
            cp = pltpu.make_async_remote_copy(
                src_ref=ins[a], dst_ref=outs[a], send_sem=send_sems.at[a], recv_sem=recv_sems.at[a],
                device_id=(x, y, 1 - c), device_id_type=MESH)
            cp.start()
            copies.append(cp)
        for cp in copies:
            cp.wait()

    return pl.pallas_call(
        body, name="reduce_pair_swap", in_specs=[ANY] * n, out_specs=[ANY] * n,
        out_shape=[jax.ShapeDtypeStruct(s.shape, s.dtype) for s in halves],
        scratch_shapes=[pltpu.SemaphoreType.DMA((n,)), pltpu.SemaphoreType.DMA((n,))],
        compiler_params=pltpu.CompilerParams(has_side_effects=True),
    )(*halves)


def _chip_sum(name, chip_sel, own, col, others):
    _, r, c = others.shape
    tr = _pick(r, (256, 128, 64, 32, 16))
    if col:
        own_spec = pl.BlockSpec((tr, c), lambda i, s: (i, s[0]))
    else:
        own_spec = pl.BlockSpec((None, tr, c), lambda i, s: (s[0], i, 0))
    specs = [own_spec] + [pl.BlockSpec((None, tr, c), lambda i, s, k=k: (k, i, 0)) for k in range(3)]

    def body(s_ref, own_ref, r0, r1, r2, o_ref):
        o_ref[...] = ((own_ref[...].astype(F32) + r0[...].astype(F32)) + r1[...].astype(F32)) + r2[...].astype(F32)

    return pl.pallas_call(
        body, name=name,
        grid_spec=pltpu.PrefetchScalarGridSpec(
            num_scalar_prefetch=1, grid=(r // tr,), in_specs=specs,
            out_specs=pl.BlockSpec((tr, c), lambda i, s: (i, 0))),
        out_shape=jax.ShapeDtypeStruct((r, c), F32),
        compiler_params=_params(("parallel",)),
    )(chip_sel, own, others, others, others)


def _pair_sum(name, c_sel, grad, recv, col):
    if col:
        r, c4 = grad.shape
        hr, c = r // 2, c4 // 4
    else:
        _, r, c = grad.shape
        hr = r // 2
    tr = _pick(hr, (256, 128, 64, 32, 16))
    nb = hr // tr

    def body(s_ref, g_ref, r_ref, o_ref):
        o_ref[...] = (g_ref[...] + r_ref[...]).astype(o_ref.dtype)

    if col:
        in_specs = [pl.BlockSpec((tr, c), lambda k, i, s: (s[0] * nb + i, k)), pl.BlockSpec((tr, c), lambda k, i, s: (i, k))]
        out_spec = pl.BlockSpec((tr, c), lambda k, i, s: (i, k))
    else:
        in_specs = [pl.BlockSpec((None, tr, c), lambda k, i, s: (k, s[0] * nb + i, 0)),
                    pl.BlockSpec((None, tr, c), lambda k, i, s: (k, i, 0))]
        out_spec = pl.BlockSpec((None, tr, c), lambda k, i, s: (k, i, 0))
    return pl.pallas_call(
        body, name=name,
        grid_spec=pltpu.PrefetchScalarGridSpec(num_scalar_prefetch=1, grid=(4, nb), in_specs=in_specs,
                                               out_specs=out_spec),
        out_shape=jax.ShapeDtypeStruct(recv.shape, BF16),
        compiler_params=_params(("parallel", "parallel")),
    )(c_sel, grad, recv)


def _allreduce_small(vals):
    sizes = [int(math.prod(v.shape)) for v in vals]
    padded = [-(-s // 128) * 128 for s in sizes]
    total = -(-sum(padded) // 1024) * 1024
    flat = [jnp.pad(v.reshape(-1), (0, p - s)) for v, s, p in zip(vals, sizes, padded)]
    flat.append(jnp.zeros((total - sum(padded),), F32))
    packed = jnp.concatenate(flat).reshape(total // 128, 128)

    def body(in_ref, out_ref, r0, r1, r2, send_sems, recv_sems):
        x, y, c = _place()
        out_ref[...] = in_ref[...]
        for k, (peer, land) in enumerate(zip([(x, y, 1 - c), (1 - x, y, c), (x, 1 - y, c)], (r0, r1, r2))):
            cp = pltpu.make_async_remote_copy(
                src_ref=out_ref, dst_ref=land, send_sem=send_sems.at[k], recv_sem=recv_sems.at[k],
                device_id=peer, device_id_type=MESH)
            cp.start()
            cp.wait()
            out_ref[...] = out_ref[...] + land[...]

    vm = pl.BlockSpec(memory_space=pltpu.VMEM)
    summed = pl.pallas_call(
        body, name="allreduce_small", in_specs=[vm], out_specs=vm,
        out_shape=jax.ShapeDtypeStruct(packed.shape, F32),
        scratch_shapes=[pltpu.VMEM(packed.shape, F32)] * 3
        + [pltpu.SemaphoreType.DMA((3,)), pltpu.SemaphoreType.DMA((3,))],
        compiler_params=pltpu.CompilerParams(has_side_effects=True, vmem_limit_bytes=VMEM_LIMIT_BYTES),
    )(packed).reshape(-1)
    outs, off = [], 0
    for v, s, p in zip(vals, sizes, padded):
        outs.append(summed[off:off + s].reshape(v.shape))
        off += p
    return outs


def _adamw_math(w, g, m, v):
    m2 = ADAM_B1 * m + (1.0 - ADAM_B1) * g
    v2 = ADAM_B2 * v + (1.0 - ADAM_B2) * (g * g)
    m_hat = m2 / (1.0 - ADAM_B1 ** ADAM_STEP)
    v_hat = v2 / (1.0 - ADAM_B2 ** ADAM_STEP)
    delta = -ADAM_LR * (m_hat / (jnp.sqrt(v_hat) + ADAM_EPS) + ADAM_WD * w)
    return delta, m2, v2


def _adamw_big(name, c_sel, w, g_mine, g_sibling, m, v):
    r, c = w.shape
    hr = r // 2
    tr = _pick(hr, (256, 128, 64, 32, 16, 8))
    nb = hr // tr

    def body(s_ref, w_ref, ga_ref, gb_ref, m_ref, v_ref, go_ref, d_ref, mo_ref, vo_ref):
        gv = jnp.where(pl.program_id(0) == s_ref[0], ga_ref[...], gb_ref[...])
        d, m2, v2 = _adamw_math(w_ref[...], gv, m_ref[...], v_ref[...])
        go_ref[...] = gv
        d_ref[...] = d
        mo_ref[...] = m2
        vo_ref[...] = v2

    blk = pl.BlockSpec((tr, c), lambda h, i, s: (h * nb + i, 0))
    half = pl.BlockSpec((tr, c), lambda h, i, s: (i, 0))
    return pl.pallas_call(
        body, name=name,
        grid_spec=pltpu.PrefetchScalarGridSpec(
            num_scalar_prefetch=1, grid=(2, nb), in_specs=[blk, half, half, blk, blk], out_specs=[blk] * 4),
        out_shape=[jax.ShapeDtypeStruct((r, c), F32)] * 4, compiler_params=_params(("parallel", "parallel")),
    )(c_sel, w, g_mine, g_sibling, m, v)


def _adamw_small(ws, gs, ms, vs):
    n = len(ws)

    def body(*refs):
        w_r, g_r, m_r, v_r = refs[:n], refs[n:2 * n], refs[2 * n:3 * n], refs[3 * n:4 * n]
        o = refs[4 * n:]
        for a in range(n):
            gv = g_r[a][...]
            d, m2, v2 = _adamw_math(w_r[a][...], gv, m_r[a][...], v_r[a][...])
            o[a][...] = gv
            o[n + a][...] = d
            o[2 * n + a][...] = m2
            o[3 * n + a][...] = v2

    res = pl.pallas_call(
        body, name="adamw_small", out_shape=[jax.ShapeDtypeStruct(w.shape, F32) for _ in range(4) for w in ws],
        compiler_params=_params(),
    )(*ws, *gs, *ms, *vs)
    return res[:n], res[n:2 * n], res[2 * n:3 * n], res[3 * n:]


def _full_from_gathered(name, gathered):
    if name == "w_in":
        rows = gathered.shape[0] // 4
        return gathered.reshape(4, rows, gathered.shape[1]).transpose(1, 0, 2).reshape(rows, 4 * gathered.shape[1])
    return gathered


def _reduce_layout(name, full):
    if name in COL_KIND:
        return full
    if name == "w_in":
        rows, cols = full.shape
        return full.reshape(rows, 4, cols // 4).transpose(1, 0, 2)
    return full.reshape(4, full.shape[0] // 4, full.shape[1])


def kernel(x, mem, norm_mix, w_in, fox_q_norm, fox_k_norm, fox_f_bias, s5_a_re, s5_a_im, s5_log_dt, s5_b_re, s5_b_im, s5_c_re, s5_c_im, s5_d, s5_w_glu, s5_b_glu, out_norm_fox, out_norm_s5, w_out, norm_cross, norm_mem, w_xq, w_xkv, xq_norm, xk_norm, w_xo, norm_ffn, w_ffn_up, ffn_conv_w, ffn_conv_b, w_ffn_down, loss_target, m_norm_mix, m_w_in, m_fox_q_norm, m_fox_k_norm, m_fox_f_bias, m_s5_a_re, m_s5_a_im, m_s5_log_dt, m_s5_b_re, m_s5_b_im, m_s5_c_re, m_s5_c_im, m_s5_d, m_s5_w_glu, m_s5_b_glu, m_out_norm_fox, m_out_norm_s5, m_w_out, m_norm_cross, m_norm_mem, m_w_xq, m_w_xkv, m_xq_norm, m_xk_norm, m_w_xo, m_norm_ffn, m_w_ffn_up, m_ffn_conv_w, m_ffn_conv_b, m_w_ffn_down, v_norm_mix, v_w_in, v_fox_q_norm, v_fox_k_norm, v_fox_f_bias, v_s5_a_re, v_s5_a_im, v_s5_log_dt, v_s5_b_re, v_s5_b_im, v_s5_c_re, v_s5_c_im, v_s5_d, v_s5_w_glu, v_s5_b_glu, v_out_norm_fox, v_out_norm_s5, v_w_out, v_norm_cross, v_norm_mem, v_w_xq, v_w_xkv, v_xq_norm, v_xk_norm, v_w_xo, v_norm_ffn, v_w_ffn_up, v_ffn_conv_w, v_ffn_conv_b, v_w_ffn_down):
    given = dict(locals())
    w = {n: given[n] for n in WEIGHTS}
    m = {n: given["m_" + n] for n in WEIGHTS}
    v = {n: given["v_" + n] for n in WEIGHTS}
    xi, yi, ci = _place()
    chip = (2 * xi + yi).astype(jnp.int32)

    col_kind = [n in COL_KIND for n in BIG]
    gathered, taps = _gather_weights([w[n][0].astype(BF16) for n in BIG], col_kind, w["ffn_conv_w"][0])
    wb = {n: _full_from_gathered(n, gathered[k]) for k, n in enumerate(BIG)}
    conv_w = taps.transpose(1, 0, 2).reshape(3, D_FF)

    p = {n: w[n][0] for n in SMALL}
    p["ffn_conv_w"] = conv_w
    for n in ("norm_mix", "fox_q_norm", "fox_k_norm", "fox_f_bias", "s5_b_glu", "out_norm_fox", "out_norm_s5",
              "norm_cross", "norm_mem", "xq_norm", "xk_norm", "norm_ffn", "ffn_conv_b"):
        p[n] = p[n].reshape(1, -1)
    loss, grad_x, g = _local_step(x, mem, loss_target, p, wb)

    small_names = list(SMALL) + ["ffn_conv_w"]
    small_vals = [g[n].reshape(w[n].shape if n != "ffn_conv_w" else (1, 3, D_FF)) for n in small_names] + [loss]
    reduced = _allreduce_small(small_vals)
    loss_all = reduced[-1].reshape(())
    conv_w_grad = lax.dynamic_slice_in_dim(reduced[-2], chip * (D_FF // 4), D_FF // 4, axis=2)
    sg, sd, sm, sv = _adamw_small(
        [w[n] for n in small_names], list(reduced[:len(SMALL)]) + [conv_w_grad],
        [m[n] for n in small_names], [v[n] for n in small_names])
    out_g = dict(zip(small_names, sg))
    out_d = dict(zip(small_names, sd))
    out_m = dict(zip(small_names, sm))
    out_v = dict(zip(small_names, sv))

    c_sel = ci.astype(jnp.int32).reshape(1)
    chip_sel = chip.reshape(1)
    grads = [_reduce_layout(n, g[n]) for n in BIG]
    from_sibling = _pair_exchange_halves(grads, col_kind)
    pair_sums = [_pair_sum("reduce_pair_sum_" + n, c_sel, gr, rv, ck)
                 for n, gr, rv, ck in zip(BIG, grads, from_sibling, col_kind)]
    from_chips = _chip_exchange(pair_sums, col_kind)
    halves = [_chip_sum("reduce_chip_sum_" + n, chip_sel, ps, ck, fc)
              for n, ps, fc, ck in zip(BIG, pair_sums, from_chips, col_kind)]
    sibling_halves = _pair_swap_halves(halves)
    for n, mine, theirs in zip(BIG, halves, sibling_halves):
        go, d, m2, v2 = _adamw_big("adamw_" + n, c_sel, w[n][0], mine, theirs, m[n][0], v[n][0])
        out_g[n], out_d[n], out_m[n], out_v[n] = go[None], d[None], m2[None], v2[None]

    return (loss_all, grad_x, *[out_g[n] for n in WEIGHTS], *[out_d[n] for n in WEIGHTS],
            *[out_m[n] for n in WEIGHTS], *[out_v[n] for n in WEIGHTS])
```

```python
import functools
import math

import jax
import jax.numpy as jnp
from jax import lax
from jax.experimental import pallas as pl
from jax.experimental.pallas import tpu as pltpu

F32 = jnp.float32
BF16 = jnp.bfloat16

D_MODEL = 1024
FOX_WIDTH = 512
HEAD_DIM = 64
N_FOX_HEADS = 8
S5_WIDTH = 512
S5_GROUP_CH = 16
S5_GROUPS = 32
S5_STATE = 64
S5_CH = S5_GROUPS * S5_STATE
N_X_HEADS = 4
X_HEAD_DIM = 256
N_MEM = 256
D_FF = 2816
UF_COLS = 640
EPS = 1e-6
ADAM_LR = 0.001
ADAM_B1 = 0.9
ADAM_B2 = 0.999
ADAM_EPS = 1e-08
ADAM_WD = 0.01
ADAM_STEP = 10

VMEM_LIMIT_BYTES = 56 * 1024 * 1024
MM_BLOCK_BYTES = 6 * 1024 * 1024
MESH = pl.DeviceIdType.MESH

EARLY_WEIGHTS = ("w_in", "s5_w_glu", "w_out")
LATE_WEIGHTS = ("w_xq", "w_xkv", "w_xo", "w_ffn_up", "w_ffn_down")
BIG = EARLY_WEIGHTS + LATE_WEIGHTS
COL_KIND = ("w_xkv", "w_ffn_up")
SMALL = ("norm_mix", "fox_q_norm", "fox_k_norm", "fox_f_bias", "s5_a_re", "s5_a_im", "s5_log_dt",
         "s5_b_re", "s5_b_im", "s5_c_re", "s5_c_im", "s5_d", "s5_b_glu", "out_norm_fox", "out_norm_s5",
         "norm_cross", "norm_mem", "xq_norm", "xk_norm", "norm_ffn", "ffn_conv_b")
WEIGHTS = ("norm_mix", "w_in", "fox_q_norm", "fox_k_norm", "fox_f_bias", "s5_a_re", "s5_a_im", "s5_log_dt",
           "s5_b_re", "s5_b_im", "s5_c_re", "s5_c_im", "s5_d", "s5_w_glu", "s5_b_glu", "out_norm_fox",
           "out_norm_s5", "w_out", "norm_cross", "norm_mem", "w_xq", "w_xkv", "xq_norm", "xk_norm", "w_xo",
           "norm_ffn", "w_ffn_up", "ffn_conv_w", "ffn_conv_b", "w_ffn_down")


def _params(sem=None):
    return pltpu.CompilerParams(dimension_semantics=sem, vmem_limit_bytes=VMEM_LIMIT_BYTES)


def _pick(n, cands):
    for c in cands:
        if n % c == 0:
            return c
    return n


_DIMS = {"nn": (((1,), (0,)), ((), ())), "nt": (((1,), (1,)), ((), ())), "tn": (((0,), (0,)), ((), ()))}


def _mm(a, b, mode, name, out_dtype=F32, res=None):
    if mode == "nn":
        (m, k), (k2, n) = a.shape, b.shape
    elif mode == "nt":
        (m, k), (n, k2) = a.shape, b.shape
    else:
        (k, m), (k2, n) = a.shape, b.shape
    assert k == k2, (name, a.shape, b.shape)

    def fit(dim, itemsize):
        for c in (512, 256, 128):
            if dim % c == 0 and c * k * itemsize <= MM_BLOCK_BYTES:
                return c
        return 128 if dim % 128 == 0 else dim

    tm, tn = fit(m, a.dtype.itemsize), fit(n, b.dtype.itemsize)
    a_spec = pl.BlockSpec((k, tm), lambda i, j: (0, i)) if mode == "tn" else pl.BlockSpec((tm, k), lambda i, j: (i, 0))
    b_spec = pl.BlockSpec((tn, k), lambda i, j: (j, 0)) if mode == "nt" else pl.BlockSpec((k, tn), lambda i, j: (0, j))
    o_spec = pl.BlockSpec((tm, tn), lambda i, j: (i, j))
    dims = _DIMS[mode]
    has_res = res is not None

    def body(*refs):
        a_ref, b_ref = refs[0], refs[1]
        o_ref = refs[-1]
        acc = lax.dot_general(a_ref[...].astype(BF16), b_ref[...].astype(BF16), dims, preferred_element_type=F32)
        if has_res:
            acc = acc + refs[2][...].astype(F32)
        o_ref[...] = acc.astype(o_ref.dtype)

    return pl.pallas_call(
        body, name=name, grid=(m // tm, n // tn),
        in_specs=[a_spec, b_spec] + ([o_spec] if has_res else []),
        out_specs=o_spec, out_shape=jax.ShapeDtypeStruct((m, n), out_dtype),
        compiler_params=_params(("parallel", "parallel")),
    )(*((a, b, res) if has_res else (a, b)))


def _gmm_tn(a, b, m, n, groups, name):
    t = a.shape[0]

    def body(a_ref, b_ref, o_ref):
        o_ref[...] = lax.dot_general(a_ref[...].astype(BF16), b_ref[...].astype(BF16), _DIMS["tn"],
                                     preferred_element_type=F32)

    return pl.pallas_call(
        body, name=name, grid=(groups,),
        in_specs=[pl.BlockSpec((t, m), lambda j: (0, j)), pl.BlockSpec((t, n), lambda j: (0, j))],
        out_specs=pl.BlockSpec((None, m, n), lambda j: (j, 0, 0)),
        out_shape=jax.ShapeDtypeStruct((groups, m, n), F32), compiler_params=_params(("parallel",)),
    )(a, b)


def _gmm(a, b, mode, name, out_dtype=F32, res=None):
    g = b.shape[0]
    dims = _DIMS[mode]
    has_res = res is not None

    def body(*refs):
        acc = lax.dot_general(refs[0][...].astype(BF16), refs[1][...].astype(BF16), dims, preferred_element_type=F32)
        if has_res:
            acc = acc + refs[2][...].astype(F32)
        refs[-1][...] = acc.astype(refs[-1].dtype)

    k, n = (b.shape[1], b.shape[2]) if mode == "nn" else (b.shape[2], b.shape[1])
    m = a.shape[0]
    tm = _pick(m, (512, 256, 128))
    o_spec = pl.BlockSpec((tm, n), lambda i, j: (i, j))
    return pl.pallas_call(
        body, name=name, grid=(m // tm, g),
        in_specs=[pl.BlockSpec((tm, k), lambda i, j: (i, j)), pl.BlockSpec((None,) + b.shape[1:], lambda i, j: (j, 0, 0))]
        + ([o_spec] if has_res else []),
        out_specs=o_spec, out_shape=jax.ShapeDtypeStruct((m, g * n), out_dtype),
        compiler_params=_params(("parallel", "parallel")),
    )(*((a, b, res) if has_res else (a, b)))


def _row_spec(tm, bc, off, step):
    return pl.BlockSpec((tm, bc), lambda i, h: (i, off + step * h))


ROW_TILE_ELEMS = 512 * 1024


def _row_tile(t, rows):
    widest = max(bc for (_, bc, _, _) in rows)
    return _pick(t, (min(t, ROW_TILE_ELEMS // widest), 512, 256, 128, 64, 8))


def _rowwise(fn, rows, pars, outs, name, heads=1):
    t = rows[0][0].shape[0]
    tm = _row_tile(t, rows)
    nr, npar = len(rows), len(pars)

    def body(*refs):
        vals = [r[...].astype(F32) for r in refs[:nr + npar]]
        res = fn(*vals)
        if not isinstance(res, (tuple, list)):
            res = (res,)
        for o_ref, v in zip(refs[nr + npar:], res):
            o_ref[...] = v.astype(o_ref.dtype)

    in_specs = [_row_spec(tm, bc, off, st) for (_, bc, off, st) in rows]
    in_specs += [pl.BlockSpec(p.shape, lambda i, h: (0, 0)) for p in pars]
    out_specs = [_row_spec(tm, bc, 0, st) for (_, bc, st, _) in outs]
    out_shape = [jax.ShapeDtypeStruct((t, c), dt) for (c, _, _, dt) in outs]
    res = pl.pallas_call(
        body, name=name, grid=(t // tm, heads), in_specs=in_specs, out_specs=out_specs, out_shape=out_shape,
        compiler_params=_params(("parallel", "parallel")),
    )(*[r[0] for r in rows], *pars)
    return res[0] if len(res) == 1 else res


def _rowwise_vjp(fn, rows, pars, cts, name, heads=1, adds=None, row_dtypes=None):
    t = rows[0][0].shape[0]
    tm = _row_tile(t, rows)
    nr, npar, nct = len(rows), len(pars), len(cts)
    adds = adds or [None] * nr
    add_list = [a for a in adds if a is not None]
    row_dtypes = row_dtypes or [F32] * nr

    def body(*refs):
        i, h = pl.program_id(0), pl.program_id(1)
        p = 0
        row_v = [r[...].astype(F32) for r in refs[p:p + nr]]; p += nr
        par_v = [r[...].astype(F32) for r in refs[p:p + npar]]; p += npar
        ct_v = [r[...].astype(F32) for r in refs[p:p + nct]]; p += nct
        add_refs = refs[p:p + len(add_list)]; p += len(add_list)
        drow_refs = refs[p:p + nr]; p += nr
        dpar_refs = refs[p:p + npar]

        def wrapped(*a):
            r = fn(*a)
            return tuple(r) if isinstance(r, (tuple, list)) else (r,)

        _, pull = jax.vjp(wrapped, *row_v, *par_v)
        grads = pull(tuple(ct_v))
        ai = 0
        for k in range(nr):
            g = grads[k]
            if adds[k] is not None:
                g = g + add_refs[ai][...].astype(F32)
                ai += 1
            drow_refs[k][...] = g.astype(drow_refs[k].dtype)

        @pl.when((i == 0) & (h == 0))
        def _():
            for r in dpar_refs:
                r[...] = jnp.zeros(r.shape, r.dtype)

        for k in range(npar):
            dpar_refs[k][...] += grads[nr + k]

    in_specs = [_row_spec(tm, bc, off, st) for (_, bc, off, st) in rows]
    in_specs += [pl.BlockSpec(q.shape, lambda i, h: (0, 0)) for q in pars]
    in_specs += [_row_spec(tm, bc, off, st) for (_, bc, off, st) in cts]
    in_specs += [_row_spec(tm, bc, off, st) for (_, bc, off, st) in add_list]
    out_specs = [_row_spec(tm, bc, 0, st) for (_, bc, _, st) in rows]
    out_specs += [pl.BlockSpec(q.shape, lambda i, h: (0, 0)) for q in pars]
    out_shape = [jax.ShapeDtypeStruct((t, bc * (heads if st else 1)), dt) for (_, bc, _, st), dt in zip(rows, row_dtypes)]
    out_shape += [jax.ShapeDtypeStruct(q.shape, F32) for q in pars]
    res = pl.pallas_call(
        body, name=name, grid=(t // tm, heads), in_specs=in_specs, out_specs=out_specs, out_shape=out_shape,
        compiler_params=_params(("arbitrary", "arbitrary")),
    )(*[r[0] for r in rows], *pars, *[c[0] for c in cts], *[a[0] for a in add_list])
    return list(res[:nr]), list(res[nr:])


def _rms(x, g):
    return x * lax.rsqrt(jnp.mean(x * x, axis=-1, keepdims=True) + EPS) * g


def _rms_pair(x, g):
    left = lax.broadcasted_iota(jnp.int32, x.shape, 1) < HEAD_DIM
    x2 = x * x
    ms_a = jnp.sum(jnp.where(left, x2, 0.0), axis=-1, keepdims=True) * (1.0 / HEAD_DIM)
    ms_b = jnp.sum(jnp.where(left, 0.0, x2), axis=-1, keepdims=True) * (1.0 / HEAD_DIM)
    return x * lax.rsqrt(jnp.where(left, ms_a, ms_b) + EPS) * g


def _gelu(x):
    return 0.5 * x * (1.0 + jnp.tanh(math.sqrt(2.0 / math.pi) * (x + 0.044715 * (x * x * x))))


def _s5_act(ys, u, d):
    return _gelu(ys + d * u)


def _s5_gate(yg, z, b, g):
    return _rms(yg * jax.nn.sigmoid(z + b), g)


def _lane_cumsum(x, reverse):
    n = x.shape[-1]
    lane = lax.broadcasted_iota(jnp.int32, x.shape, 1)
    k = 1
    while k < n:
        if reverse:
            x = x + jnp.where(lane < n - k, pltpu.roll(x, n - k, 1), 0.0)
        else:
            x = x + jnp.where(lane >= k, pltpu.roll(x, k, 1), 0.0)
        k *= 2
    return x


def _log_sigmoid(z):
    return jnp.minimum(z, 0.0) - jnp.log(1.0 + jnp.exp(-jnp.abs(z)))


def _forget_fwd(f, bias):
    def body(f_ref, b_ref, c_ref):
        c_ref[...] = _lane_cumsum(_log_sigmoid(f_ref[...] + b_ref[...]), False)

    return pl.pallas_call(body, name="forget_fwd", out_shape=jax.ShapeDtypeStruct(f.shape, F32),
                          compiler_params=_params())(f, bias)


def _forget_bwd(f, bias, dc):
    def body(f_ref, b_ref, dc_ref, df_ref, db_ref):
        dlog = _lane_cumsum(dc_ref[...], True)
        df = dlog * jax.nn.sigmoid(-(f_ref[...] + b_ref[...]))
        df_ref[...] = df
        db_ref[...] = jnp.sum(df, axis=1, keepdims=True)

    return pl.pallas_call(body, name="forget_bwd",
                          out_shape=(jax.ShapeDtypeStruct(f.shape, F32), jax.ShapeDtypeStruct(bias.shape, F32)),
                          compiler_params=_params())(f, bias, dc)


FOX_BLOCK = 256
_NT = _DIMS["nt"]
_TN = _DIMS["tn"]


N_PAIRS = N_FOX_HEADS // 2
V_BLOCK0 = 2 * N_PAIRS


def _left_lanes(shape):
    return lax.broadcasted_iota(jnp.int32, shape, 1) < HEAD_DIM


def _top_rows(shape):
    return lax.broadcasted_iota(jnp.int32, shape, 0) < HEAD_DIM


def _wide(c_tile, n):
    return c_tile if n == 128 else jnp.concatenate([c_tile] * (n // 128), axis=1)


def _fox_fwd(qn, kn, qkv, c_wide, seqs):
    t = qn.shape[0]
    l = t // seqs
    tb = min(FOX_BLOCK, l)
    nb = l // tb
    scale = HEAD_DIM ** -0.5

    def body(q_ref, k_ref, v_ref, ca_ref, cb_ref, o_ref, lse_ref, vt_ref):
        i = pl.program_id(2)
        top = _top_rows((128, tb))

        @pl.when(i == 0)
        def _():
            vt_ref[...] = v_ref[...].T.astype(BF16)

        qt = (q_ref[...].astype(F32) * scale).T.astype(BF16)
        zero = jnp.zeros_like(qt)
        qts = (jnp.where(top, qt, zero), jnp.where(top, zero, qt))
        causal = lax.broadcasted_iota(jnp.int32, (tb, tb), 0) <= lax.broadcasted_iota(jnp.int32, (tb, tb), 1)
        c_refs = (ca_ref, cb_ref)

        def tile(j, carry, masked):
            off = pl.multiple_of(j * tb, tb)
            k2 = k_ref[pl.ds(off, tb), :]
            vt = vt_ref[:, pl.ds(off, tb)]
            vts = (jnp.where(top, vt, zero), jnp.where(top, zero, vt))
            (ma, sa), (mb, sb), acc = carry
            new, alphas, pv = [], [], []
            for h, (m, s_sum) in enumerate(((ma, sa), (mb, sb))):
                st = jnp.dot(k2, qts[h], preferred_element_type=F32) - _wide(c_refs[h][pl.ds(off, tb), :], tb)
                if masked:
                    st = jnp.where(causal, st, -jnp.inf)
                m_new = jnp.maximum(m, jnp.max(st, axis=0, keepdims=True))
                alpha = jnp.exp(m - m_new)
                p = jnp.exp(st - m_new)
                new.append((m_new, alpha * s_sum + jnp.sum(p, axis=0, keepdims=True)))
                alphas.append(alpha)
                pv.append(jnp.dot(vts[h], p.astype(BF16), preferred_element_type=F32))
            acc = jnp.where(top, alphas[0], alphas[1]) * acc + pv[0] + pv[1]
            return new[0], new[1], acc

        stat = (jnp.full((1, tb), -jnp.inf, F32), jnp.zeros((1, tb), F32))
        carry = lax.fori_loop(0, i, lambda j, c: tile(j, c, False), (stat, stat, jnp.zeros((128, tb), F32)))
        (ma, sa), (mb, sb), acc = tile(i, carry, True)
        o_ref[...] = (acc / jnp.where(top, sa, sb)).T
        lse_ref[0:1, :] = ma + jnp.log(sa)
        lse_ref[1:2, :] = mb + jnp.log(sb)

    qblk = pl.BlockSpec((tb, 128), lambda b, hp, i: (b * nb + i, hp))
    return pl.pallas_call(
        body, name="fox_fwd", grid=(seqs, N_PAIRS, nb),
        in_specs=[qblk, pl.BlockSpec((l, 128), lambda b, hp, i: (b, hp)),
                  pl.BlockSpec((l, 128), lambda b, hp, i: (b, V_BLOCK0 + hp)),
                  pl.BlockSpec((None, l, 128), lambda b, hp, i: (b * N_FOX_HEADS + 2 * hp, 0, 0)),
                  pl.BlockSpec((None, l, 128), lambda b, hp, i: (b * N_FOX_HEADS + 2 * hp + 1, 0, 0))],
        out_specs=[qblk, pl.BlockSpec((None, 2, tb), lambda b, hp, i: (b * N_PAIRS + hp, 0, i))],
        out_shape=[jax.ShapeDtypeStruct((t, FOX_WIDTH), F32), jax.ShapeDtypeStruct((seqs * N_PAIRS, 2, l), F32)],
        scratch_shapes=[pltpu.VMEM((128, l), BF16)],
        compiler_params=_params(("parallel", "parallel", "arbitrary")),
    )(qn, kn, qkv, c_wide, c_wide)


def _fox_bwd(qn, kn, qkv, c_wide, o, do, lse, seqs):
    t = qn.shape[0]
    l = t // seqs
    tb = min(FOX_BLOCK, l)
    nb = l // tb
    scale = HEAD_DIM ** -0.5

    def body(q_ref, k_ref, v_ref, ca_ref, cb_ref, o_ref, do_ref, lse_ref, dq_ref, dk_ref, dv_ref, dc_ref, dcq_ref,
             qt_ref, kt_ref, dot_ref, delta_ref, dqt_ref):
        top_l = _top_rows((128, l))
        top = _top_rows((128, tb))
        left = _left_lanes((tb, 128))
        zero_t = jnp.zeros((128, tb), BF16)
        zero_l = jnp.zeros((tb, 128), BF16)
        rows = lambda a: (jnp.where(top, a, zero_t), jnp.where(top, zero_t, a))
        lanes = lambda a: (jnp.where(left, a, zero_l), jnp.where(left, zero_l, a))
        causal = lax.broadcasted_iota(jnp.int32, (tb, tb), 0) <= lax.broadcasted_iota(jnp.int32, (tb, tb), 1)
        ones_q = jnp.ones((tb, 128), BF16)
        ones_k = jnp.ones((8, tb), BF16)
        c_refs = (ca_ref, cb_ref)

        qt_ref[...] = (q_ref[...].astype(F32) * scale).T.astype(BF16)
        kt_ref[...] = k_ref[...].astype(F32).T.astype(BF16)
        do_t = do_ref[...].T
        dot_ref[...] = do_t.astype(BF16)
        prod_t = do_t * o_ref[...].T
        delta_ref[0:1, :] = jnp.sum(jnp.where(top_l, prod_t, 0.0), axis=0, keepdims=True)
        delta_ref[1:2, :] = jnp.sum(jnp.where(top_l, 0.0, prod_t), axis=0, keepdims=True)
        dqt_ref[...] = jnp.zeros(dqt_ref.shape, F32)
        dcq_ref[...] = jnp.zeros(dcq_ref.shape, F32)

        def kv_block(j, _):
            koff = pl.multiple_of(j * tb, tb)
            k2 = k_ref[pl.ds(koff, tb), :]
            v2 = v_ref[pl.ds(koff, tb), :].astype(BF16)
            kts = rows(kt_ref[:, pl.ds(koff, tb)])
            cw = tuple(_wide(c_refs[h][pl.ds(koff, tb), :], tb) for h in (0, 1))

            def q_block(i, carry, masked):
                dk, dv, dca, dcb = carry
                qoff = pl.multiple_of(i * tb, tb)
                qs = lanes((q_ref[pl.ds(qoff, tb), :].astype(F32) * scale).astype(BF16))
                dos = lanes(do_ref[pl.ds(qoff, tb), :].astype(BF16))
                qts = rows(qt_ref[:, pl.ds(qoff, tb)])
                dots = rows(dot_ref[:, pl.ds(qoff, tb)])
                dq_t, dcs = 0.0, []
                for h in (0, 1):
                    st = jnp.dot(k2, qts[h], preferred_element_type=F32) - cw[h]
                    p = jnp.exp(st - lse_ref[h:h + 1, pl.ds(qoff, tb)])
                    if masked:
                        p = jnp.where(causal, p, 0.0)
                    dp = jnp.dot(v2, dots[h], preferred_element_type=F32)
                    dsb = (p * (dp - delta_ref[h:h + 1, pl.ds(qoff, tb)])).astype(BF16)
                    dv = dv + jnp.dot(p.astype(BF16), dos[h], preferred_element_type=F32)
                    dk = dk + jnp.dot(dsb, qs[h], preferred_element_type=F32)
                    dq_t = dq_t + jnp.dot(kts[h], dsb, preferred_element_type=F32)
                    dcs.append(jnp.dot(dsb, ones_q, preferred_element_type=F32))
                    dcq_ref[h:h + 1, pl.ds(qoff, tb)] += jnp.dot(ones_k, dsb, preferred_element_type=F32)[0:1, :]
                dqt_ref[:, pl.ds(qoff, tb)] += dq_t
                return dk, dv, dca - dcs[0], dcb - dcs[1]

            z = jnp.zeros((tb, 128), F32)
            carry = q_block(j, (z, z, z, z), True)
            dk, dv, dca, dcb = lax.fori_loop(j + 1, nb, lambda i, c: q_block(i, c, False), carry)
            dk_ref[pl.ds(koff, tb), :] = dk
            dv_ref[pl.ds(koff, tb), :] = dv
            dc_ref[pl.ds(koff, tb), 0:128] = dca
            dc_ref[pl.ds(koff, tb), 128:256] = dcb
            return 0

        lax.fori_loop(0, nb, kv_block, 0)
        dq_ref[...] = (dqt_ref[...] * scale).T

    blk = pl.BlockSpec((l, 128), lambda b, hp: (b, hp))
    cspec = lambda k: pl.BlockSpec((None, l, 128), lambda b, hp: (b * N_FOX_HEADS + 2 * hp + k, 0, 0))
    rows2 = pl.BlockSpec((None, 2, l), lambda b, hp: (b * N_PAIRS + hp, 0, 0))
    wide = jax.ShapeDtypeStruct((t, FOX_WIDTH), F32)
    return pl.pallas_call(
        body, name="fox_bwd", grid=(seqs, N_PAIRS),
        in_specs=[blk, blk, pl.BlockSpec((l, 128), lambda b, hp: (b, V_BLOCK0 + hp)), cspec(0), cspec(1), blk, blk, rows2],
        out_specs=[blk, blk, blk, pl.BlockSpec((None, l, 256), lambda b, hp: (b * N_PAIRS + hp, 0, 0)), rows2],
        out_shape=[wide, wide, wide, jax.ShapeDtypeStruct((seqs * N_PAIRS, l, 256), F32),
                   jax.ShapeDtypeStruct((seqs * N_PAIRS, 2, l), F32)],
        scratch_shapes=[pltpu.VMEM((128, l), BF16), pltpu.VMEM((128, l), BF16), pltpu.VMEM((128, l), BF16),
                        pltpu.VMEM((2, l), F32), pltpu.VMEM((128, l), F32)],
        compiler_params=_params(("parallel", "parallel")),
    )(qn, kn, qkv, c_wide, c_wide, o, do, lse)


SCAN_ROWS = 256
SCAN_COLS = 1024


def _scan_fwd(bur, bui, ar, ai, seqs):
    t, ch = bur.shape
    l = t // seqs
    tl, cb = min(SCAN_ROWS, l), min(SCAN_COLS, ch)
    nl = l // tl

    def body(br_ref, bi_ref, ar_ref, ai_ref, xr_ref, xi_ref, cr, ci):
        @pl.when(pl.program_id(2) == 0)
        def _():
            cr[...] = jnp.zeros(cr.shape, F32)
            ci[...] = jnp.zeros(ci.shape, F32)

        a_r, a_i = ar_ref[...], ai_ref[...]

        def step(tt, carry):
            xr, xi = carry
            nr = a_r * xr - a_i * xi + br_ref[pl.ds(tt, 1), :]
            ni = a_r * xi + a_i * xr + bi_ref[pl.ds(tt, 1), :]
            xr_ref[pl.ds(tt, 1), :] = nr
            xi_ref[pl.ds(tt, 1), :] = ni
            return nr, ni

        xr, xi = lax.fori_loop(0, tl, step, (cr[...], ci[...]), unroll=8)
        cr[...] = xr
        ci[...] = xi

    blk = pl.BlockSpec((tl, cb), lambda s, j, r: (s * nl + r, j))
    par = pl.BlockSpec((1, cb), lambda s, j, r: (0, j))
    return pl.pallas_call(
        body, name="s5_scan_fwd", grid=(seqs, ch // cb, nl),
        in_specs=[blk, blk, par, par], out_specs=[blk, blk],
        out_shape=[jax.ShapeDtypeStruct((t, ch), F32)] * 2,
        scratch_shapes=[pltpu.VMEM((1, cb), F32), pltpu.VMEM((1, cb), F32)],
        compiler_params=_params(("parallel", "parallel", "arbitrary")),
    )(bur, bui, ar, ai)


def _scan_bwd(gr, gi, xr, xi, ar, ai, seqs):
    t, ch = gr.shape
    l = t // seqs
    tl, cb = min(SCAN_ROWS, l), min(SCAN_COLS, ch)
    nl = l // tl

    def body(gr_ref, gi_ref, xr_ref, xi_ref, ar_ref, ai_ref, lr_ref, li_ref, dar_ref, dai_ref, cr, ci):
        @pl.when(pl.program_id(2) == 0)
        def _():
            cr[...] = jnp.zeros(cr.shape, F32)
            ci[...] = jnp.zeros(ci.shape, F32)
            dar_ref[...] = jnp.zeros(dar_ref.shape, F32)
            dai_ref[...] = jnp.zeros(dai_ref.shape, F32)

        a_r, a_i = ar_ref[...], ai_ref[...]

        def step(k, carry):
            lr, li, dar, dai = carry
            tt = tl - 1 - k
            xr_t = xr_ref[pl.ds(tt, 1), :]
            xi_t = xi_ref[pl.ds(tt, 1), :]
            dar = dar + lr * xr_t + li * xi_t
            dai = dai + li * xr_t - lr * xi_t
            nr = gr_ref[pl.ds(tt, 1), :] + a_r * lr + a_i * li
            ni = gi_ref[pl.ds(tt, 1), :] + a_r * li - a_i * lr
            lr_ref[pl.ds(tt, 1), :] = nr
            li_ref[pl.ds(tt, 1), :] = ni
            return nr, ni, dar, dai

        lr, li, dar, dai = lax.fori_loop(
            0, tl, step, (cr[...], ci[...], jnp.zeros((1, cb), F32), jnp.zeros((1, cb), F32)), unroll=8)
        cr[...] = lr
        ci[...] = li
        dar_ref[...] += dar
        dai_ref[...] += dai

    blk = pl.BlockSpec((tl, cb), lambda s, j, r: (s * nl + nl - 1 - r, j))
    par = pl.BlockSpec((1, cb), lambda s, j, r: (0, j))
    acc = pl.BlockSpec((None, 1, cb), lambda s, j, r: (s, 0, j))
    lr, li, dar, dai = pl.pallas_call(
        body, name="s5_scan_bwd", grid=(seqs, ch // cb, nl),
        in_specs=[blk, blk, blk, blk, par, par], out_specs=[blk, blk, acc, acc],
        out_shape=[jax.ShapeDtypeStruct((t, ch), F32)] * 2 + [jax.ShapeDtypeStruct((seqs, 1, ch), F32)] * 2,
        scratch_shapes=[pltpu.VMEM((1, cb), F32), pltpu.VMEM((1, cb), F32)],
        compiler_params=_params(("parallel", "parallel", "arbitrary")),
    )(gr, gi, xr, xi, ar, ai)
    return lr, li, dar, dai


XATT_BLOCK = 512


def _xatt_probs(qv, kv):
    s = lax.dot_general(qv, kv, _NT, preferred_element_type=F32) * (X_HEAD_DIM ** -0.5)
    e = jnp.exp(s - jnp.max(s, axis=-1, keepdims=True))
    return e / jnp.sum(e, axis=-1, keepdims=True)


def _xatt_fwd(q, k, kv, seqs):
    t = q.shape[0]
    tq = min(XATT_BLOCK, t // seqs)
    nq = t // seqs // tq

    def body(q_ref, k_ref, v_ref, o_ref):
        p = _xatt_probs(q_ref[...], k_ref[...])
        o_ref[...] = jnp.dot(p.astype(BF16), v_ref[...].astype(BF16), preferred_element_type=F32).astype(o_ref.dtype)

    qs = pl.BlockSpec((tq, X_HEAD_DIM), lambda b, h, i: (b * nq + i, h))
    return pl.pallas_call(
        body, name="xatt_fwd", grid=(seqs, N_X_HEADS, nq),
        in_specs=[qs, pl.BlockSpec((N_MEM, X_HEAD_DIM), lambda b, h, i: (b, h)),
                  pl.BlockSpec((N_MEM, X_HEAD_DIM), lambda b, h, i: (b, N_X_HEADS + h))],
        out_specs=qs, out_shape=jax.ShapeDtypeStruct(q.shape, BF16),
        compiler_params=_params(("parallel", "parallel", "parallel")),
    )(q, k, kv)


def _xatt_bwd(q, k, kv, do, seqs):
    t = q.shape[0]
    tq = min(XATT_BLOCK, t // seqs)
    nq = t // seqs // tq
    scale = X_HEAD_DIM ** -0.5

    def body(q_ref, k_ref, v_ref, do_ref, dq_ref, dk_ref, dv_ref):
        @pl.when(pl.program_id(2) == 0)
        def _():
            dk_ref[...] = jnp.zeros(dk_ref.shape, F32)
            dv_ref[...] = jnp.zeros(dv_ref.shape, F32)

        qv, kk = q_ref[...], k_ref[...]
        p = _xatt_probs(qv, kk)
        dob = do_ref[...].astype(BF16)
        dp = lax.dot_general(dob, v_ref[...].astype(BF16), _NT, preferred_element_type=F32)
        ds = p * (dp - jnp.sum(dp * p, axis=-1, keepdims=True))
        dsb = ds.astype(BF16)
        dq_ref[...] = jnp.dot(dsb, kk, preferred_element_type=F32) * scale
        dk_ref[...] += lax.dot_general(dsb, qv, _TN, preferred_element_type=F32) * scale
        dv_ref[...] += lax.dot_general(p.astype(BF16), dob, _TN, preferred_element_type=F32)

    qs = pl.BlockSpec((tq, X_HEAD_DIM), lambda b, h, i: (b * nq + i, h))
    ks = pl.BlockSpec((N_MEM, X_HEAD_DIM), lambda b, h, i: (b, h))
    return pl.pallas_call(
        body, name="xatt_bwd", grid=(seqs, N_X_HEADS, nq),
        in_specs=[qs, ks, pl.BlockSpec((N_MEM, X_HEAD_DIM), lambda b, h, i: (b, N_X_HEADS + h)), qs],
        out_specs=[qs, ks, ks],
        out_shape=[jax.ShapeDtypeStruct(q.shape, F32), jax.ShapeDtypeStruct(k.shape, F32),
                   jax.ShapeDtypeStruct(k.shape, F32)],
        compiler_params=_params(("parallel", "parallel", "arbitrary")),
    )(q, k, kv, do)


CONV_COLS = 256


def _shift_down(x, k, row):
    return jnp.where(row >= k, pltpu.roll(x, k, 0), 0.0)


def _shift_up(x, k, row):
    n = x.shape[0]
    return jnp.where(row < n - k, pltpu.roll(x, n - k, 0), 0.0)


def _conv_pre(g, w, b, row):
    return b + w[0:1, :] * _shift_down(g, 2, row) + w[1:2, :] * _shift_down(g, 1, row) + w[2:3, :] * g


def _convgate_fwd(gu, w, b, seqs):
    t = gu.shape[0]
    l = t // seqs
    nc = D_FF // CONV_COLS

    def body(g_ref, u_ref, w_ref, b_ref, o_ref):
        g = g_ref[...]
        row = lax.broadcasted_iota(jnp.int32, g.shape, 0)
        pre = _conv_pre(g, w_ref[...], b_ref[...], row)
        o_ref[...] = (pre * jax.nn.sigmoid(pre) * u_ref[...]).astype(o_ref.dtype)

    return pl.pallas_call(
        body, name="convgate_fwd", grid=(seqs, nc),
        in_specs=[pl.BlockSpec((l, CONV_COLS), lambda s, j: (s, j)), pl.BlockSpec((l, CONV_COLS), lambda s, j: (s, nc + j)),
                  pl.BlockSpec((3, CONV_COLS), lambda s, j: (0, j)), pl.BlockSpec((1, CONV_COLS), lambda s, j: (0, j))],
        out_specs=pl.BlockSpec((l, CONV_COLS), lambda s, j: (s, j)),
        out_shape=jax.ShapeDtypeStruct((t, D_FF), BF16),
        compiler_params=_params(("parallel", "parallel")),
    )(gu, gu, w, b)


def _convgate_bwd(gu, w, b, dact, seqs):
    t = gu.shape[0]
    l = t // seqs
    nc = D_FF // CONV_COLS

    def body(g_ref, u_ref, w_ref, b_ref, da_ref, dg_ref, du_ref, dw_ref, db_ref):
        @pl.when(pl.program_id(1) == 0)
        def _():
            dw_ref[...] = jnp.zeros(dw_ref.shape, F32)
            db_ref[...] = jnp.zeros(db_ref.shape, F32)

        g, wv, da = g_ref[...], w_ref[...], da_ref[...]
        row = lax.broadcasted_iota(jnp.int32, g.shape, 0)
        g1, g2 = _shift_down(g, 1, row), _shift_down(g, 2, row)
        pre = b_ref[...] + wv[0:1, :] * g2 + wv[1:2, :] * g1 + wv[2:3, :] * g
        sg = jax.nn.sigmoid(pre)
        silu = pre * sg
        du_ref[...] = (da * silu).astype(du_ref.dtype)
        dpre = da * u_ref[...] * (sg * (1.0 + pre * (1.0 - sg)))
        dg = wv[2:3, :] * dpre + wv[1:2, :] * _shift_up(dpre, 1, row) + wv[0:1, :] * _shift_up(dpre, 2, row)
        dg_ref[...] = dg.astype(dg_ref.dtype)
        dw_ref[0:1, :] += jnp.sum(dpre * g2, axis=0, keepdims=True)
        dw_ref[1:2, :] += jnp.sum(dpre * g1, axis=0, keepdims=True)
        dw_ref[2:3, :] += jnp.sum(dpre * g, axis=0, keepdims=True)
        db_ref[...] += jnp.sum(dpre, axis=0, keepdims=True)

    blk = lambda off: pl.BlockSpec((l, CONV_COLS), lambda j, s: (s, off + j))
    return pl.pallas_call(
        body, name="convgate_bwd", grid=(nc, seqs),
        in_specs=[blk(0), blk(nc), pl.BlockSpec((3, CONV_COLS), lambda j, s: (0, j)),
                  pl.BlockSpec((1, CONV_COLS), lambda j, s: (0, j)), blk(0)],
        out_specs=[blk(0), blk(0), pl.BlockSpec((3, CONV_COLS), lambda j, s: (0, j)),
                   pl.BlockSpec((1, CONV_COLS), lambda j, s: (0, j))],
        out_shape=[jax.ShapeDtypeStruct((t, D_FF), BF16), jax.ShapeDtypeStruct((t, D_FF), BF16),
                   jax.ShapeDtypeStruct((3, D_FF), F32), jax.ShapeDtypeStruct((1, D_FF), F32)],
        compiler_params=_params(("parallel", "arbitrary")),
    )(gu, gu, w, b, dact)


def _loss_head(h, target):
    t, d = h.shape
    tm = _pick(t, (256, 128, 8))

    def body(h_ref, t_ref, dh_ref, loss_ref):
        @pl.when(pl.program_id(0) == 0)
        def _():
            loss_ref[...] = jnp.zeros(loss_ref.shape, F32)

        e = h_ref[...] - t_ref[...]
        dh_ref[...] = e * (1.0 / d)
        loss_ref[...] += (0.5 / d) * jnp.sum(jnp.sum(e * e, axis=1, keepdims=True), axis=0, keepdims=True)

    blk = pl.BlockSpec((tm, d), lambda i: (i, 0))
    return pl.pallas_call(
        body, name="loss_head", grid=(t // tm,), in_specs=[blk, blk],
        out_specs=[blk, pl.BlockSpec((1, 1), lambda i: (0, 0))],
        out_shape=[jax.ShapeDtypeStruct((t, d), F32), jax.ShapeDtypeStruct((1, 1), F32)],
        compiler_params=_params(("arbitrary",)),
    )(h, target)


def _s5_discretise(a_re, a_im, log_dt, b_re, b_im):
    dt = jnp.exp(log_dt)[:, None]
    mag = jnp.exp(a_re * dt)
    lb_r = mag * jnp.cos(a_im * dt)
    lb_i = mag * jnp.sin(a_im * dt)
    den = a_re * a_re + a_im * a_im
    nr = lb_r - 1.0
    coef_r = (nr * a_re + lb_i * a_im) / den
    coef_i = (lb_i * a_re - nr * a_im) / den
    bb_r = coef_r[:, :, None] * b_re - coef_i[:, :, None] * b_im
    bb_i = coef_r[:, :, None] * b_im + coef_i[:, :, None] * b_re
    return lb_r, lb_i, bb_r, bb_i


S5_CHUNKS = 4
S5_PER = S5_GROUPS // S5_CHUNKS


def _blockdiag_in(bb):
    eye = jnp.eye(S5_PER, dtype=bb.dtype)
    return jnp.einsum("jgpc,gh->jgchp", bb.reshape(S5_CHUNKS, S5_PER, S5_STATE, S5_GROUP_CH), eye).reshape(
        S5_CHUNKS, S5_PER * S5_GROUP_CH, S5_PER * S5_STATE)


def _blockdiag_in_grad(d):
    eye = jnp.eye(S5_PER, dtype=d.dtype)
    return jnp.einsum("jgchp,gh->jgpc", d.reshape(S5_CHUNKS, S5_PER, S5_GROUP_CH, S5_PER, S5_STATE), eye).reshape(
        S5_GROUPS, S5_STATE, S5_GROUP_CH)


def _blockdiag_out(c):
    eye = jnp.eye(S5_PER, dtype=c.dtype)
    return jnp.einsum("jgcp,gh->jgphc", c.reshape(S5_CHUNKS, S5_PER, S5_GROUP_CH, S5_STATE), eye).reshape(
        S5_CHUNKS, S5_PER * S5_STATE, S5_PER * S5_GROUP_CH)


def _blockdiag_out_grad(d):
    eye = jnp.eye(S5_PER, dtype=d.dtype)
    return jnp.einsum("jgphc,gh->jgcp", d.reshape(S5_CHUNKS, S5_PER, S5_STATE, S5_PER, S5_GROUP_CH), eye).reshape(
        S5_GROUPS, S5_GROUP_CH, S5_STATE)


def _local_step(x3, mem3, target3, p, wb, late_weights=None, early_grads=None):
    seqs, l, d = x3.shape
    t = seqs * l
    x = x3.reshape(t, d)
    mem = mem3.reshape(seqs * N_MEM, d)
    target = target3.reshape(t, d)
    full = lambda a: (a, a.shape[1], 0, 0)

    s5_in = (p["s5_a_re"], p["s5_a_im"], p["s5_log_dt"], p["s5_b_re"], p["s5_b_im"])
    (lb_r, lb_i, bb_r, bb_i), s5_pull = jax.vjp(_s5_discretise, *s5_in)
    ar, ai = lb_r.reshape(1, S5_CH), lb_i.reshape(1, S5_CH)
    bbr_d, bbi_d = _blockdiag_in(bb_r).astype(BF16), _blockdiag_in(bb_i).astype(BF16)
    cr_d, ci_d = _blockdiag_out(p["s5_c_re"]).astype(BF16), (-_blockdiag_out(p["s5_c_im"])).astype(BF16)
    d_row = p["s5_d"].reshape(1, S5_WIDTH)

    w_in = wb["w_in"]
    w_qkv = w_in[:, :3 * FOX_WIDTH]
    w_uf = jnp.concatenate(
        [w_in[:, 3 * FOX_WIDTH + N_FOX_HEADS:], w_in[:, 3 * FOX_WIDTH:3 * FOX_WIDTH + N_FOX_HEADS],
         jnp.zeros((d, UF_COLS - S5_WIDTH - N_FOX_HEADS), w_in.dtype)], axis=1)

    hn1 = _rowwise(_rms, [full(x)], [p["norm_mix"]], [(d, d, 0, BF16)], "norm_mix_fwd")
    qkv = _mm(hn1, w_qkv, "nn", "in_qkv")
    uf = _mm(hn1, w_uf, "nn", "in_uf")

    bh = seqs * N_FOX_HEADS
    q_pair = (qkv, 128, 0, 1)
    k_pair = (qkv, 128, N_PAIRS, 1)
    gq2, gk2 = jnp.tile(p["fox_q_norm"], (1, 2)), jnp.tile(p["fox_k_norm"], (1, 2))
    pair_out = [(FOX_WIDTH, 128, 1, BF16)]
    qn = _rowwise(_rms_pair, [q_pair], [gq2], pair_out, "fox_qnorm_fwd", heads=N_PAIRS)
    kn = _rowwise(_rms_pair, [k_pair], [gk2], pair_out, "fox_knorm_fwd", heads=N_PAIRS)

    f_rows = uf[:, S5_WIDTH:S5_WIDTH + N_FOX_HEADS].reshape(seqs, l, N_FOX_HEADS).transpose(0, 2, 1).reshape(bh, l)
    f_bias = jnp.tile(p["fox_f_bias"].reshape(N_FOX_HEADS, 1), (seqs, 1))
    c_wide = jnp.broadcast_to(_forget_fwd(f_rows, f_bias)[:, :, None], (bh, l, 128))
    fox, lse = _fox_fwd(qn, kn, qkv, c_wide, seqs)

    bur = _gmm(uf, bbr_d, "nn", "s5_bu_re")
    bui = _gmm(uf, bbi_d, "nn", "s5_bu_im")
    xr, xi = _scan_fwd(bur, bui, ar, ai, seqs)
    ys = _gmm(xi, ci_d, "nn", "s5_y_im", res=_gmm(xr, cr_d, "nn", "s5_y_re"))
    u_blk = (uf, S5_WIDTH, 0, 0)
    yg = _rowwise(_s5_act, [full(ys), u_blk], [d_row], [(S5_WIDTH, S5_WIDTH, 0, F32)], "s5_act_fwd")
    z = _mm(yg, wb["s5_w_glu"], "nn", "s5_glu")
    y2n = _rowwise(_s5_gate, [full(yg), full(z)], [p["s5_b_glu"], p["out_norm_s5"]],
                   [(S5_WIDTH, S5_WIDTH, 0, BF16)], "s5_gate_fwd")
    foxn = _rowwise(_rms, [full(fox)], [p["out_norm_fox"]], [(FOX_WIDTH, FOX_WIDTH, 0, BF16)], "fox_outnorm_fwd")
    mixed = jnp.concatenate([foxn, y2n], axis=1)
    h1 = _mm(mixed, wb["w_out"], "nn", "mix_out", res=x)
    if late_weights is not None:
        wb = dict(wb, **late_weights(h1))

    hn2 = _rowwise(_rms, [full(h1)], [p["norm_cross"]], [(d, d, 0, BF16)], "norm_cross_fwd")
    mn = _rowwise(_rms, [full(mem)], [p["norm_mem"]], [(d, d, 0, BF16)], "norm_mem_fwd")
    xq_raw = _mm(hn2, wb["w_xq"], "nn", "x_q")
    kv = _mm(mn, wb["w_xkv"], "nn", "x_kv")
    xh = lambda a: (a, X_HEAD_DIM, 0, 1)
    xqn = _rowwise(_rms, [xh(xq_raw)], [p["xq_norm"]], [(d, X_HEAD_DIM, 1, BF16)], "x_qnorm_fwd", heads=N_X_HEADS)
    xkn = _rowwise(_rms, [xh(kv)], [p["xk_norm"]], [(d, X_HEAD_DIM, 1, BF16)], "x_knorm_fwd", heads=N_X_HEADS)
    xo = _xatt_fwd(xqn, xkn, kv, seqs)
    h2 = _mm(xo, wb["w_xo"], "nn", "x_out", res=h1)

    hn3 = _rowwise(_rms, [full(h2)], [p["norm_ffn"]], [(d, d, 0, BF16)], "norm_ffn_fwd")
    gu = _mm(hn3, wb["w_ffn_up"], "nn", "ffn_up")
    act = _convgate_fwd(gu, p["ffn_conv_w"], p["ffn_conv_b"], seqs)
    h3 = _mm(act, wb["w_ffn_down"], "nn", "ffn_down", res=h2)
    dh3, loss = _loss_head(h3, target)

    g = {}
    dact = _mm(dh3, wb["w_ffn_down"], "nt", "ffn_down_dx")
    g["w_ffn_down"] = _mm(act, dh3, "tn", "ffn_down_dw")
    dgate, dup, g["ffn_conv_w"], g["ffn_conv_b"] = _convgate_bwd(gu, p["ffn_conv_w"], p["ffn_conv_b"], dact, seqs)
    dgu = jnp.concatenate([dgate, dup], axis=1)
    dhn3 = _mm(dgu, wb["w_ffn_up"], "nt", "ffn_up_dx")
    g["w_ffn_up"] = _mm(hn3, dgu, "tn", "ffn_up_dw")
    (dh2,), (g["norm_ffn"],) = _rowwise_vjp(_rms, [full(h2)], [p["norm_ffn"]], [full(dhn3)], "norm_ffn_bwd",
                                            adds=[full(dh3)])

    dxo = _mm(dh2, wb["w_xo"], "nt", "x_out_dx")
    g["w_xo"] = _mm(xo, dh2, "tn", "x_out_dw")
    dxqn, dxkn, dxv = _xatt_bwd(xqn, xkn, kv, dxo, seqs)
    (dxq_raw,), (g["xq_norm"],) = _rowwise_vjp(_rms, [xh(xq_raw)], [p["xq_norm"]], [xh(dxqn)], "x_qnorm_bwd",
                                               heads=N_X_HEADS, row_dtypes=[BF16])
    (dxk_raw,), (g["xk_norm"],) = _rowwise_vjp(_rms, [xh(kv)], [p["xk_norm"]], [xh(dxkn)], "x_knorm_bwd",
                                               heads=N_X_HEADS, row_dtypes=[BF16])
    dkv = jnp.concatenate([dxk_raw, dxv.astype(BF16)], axis=1)
    dhn2 = _mm(dxq_raw, wb["w_xq"], "nt", "x_q_dx")
    g["w_xq"] = _mm(hn2, dxq_raw, "tn", "x_q_dw")
    dmn = _mm(dkv, wb["w_xkv"], "nt", "x_kv_dx")
    g["w_xkv"] = _mm(mn, dkv, "tn", "x_kv_dw")
    norm_cross = p["norm_cross"]
    if early_grads is not None:
        norm_cross = norm_cross + early_grads({n: g[n] for n in LATE_WEIGHTS})
    (dh1,), (g["norm_cross"],) = _rowwise_vjp(_rms, [full(h1)], [norm_cross], [full(dhn2)], "norm_cross_bwd",
                                              adds=[full(dh2)])
    _, (g["norm_mem"],) = _rowwise_vjp(_rms, [full(mem)], [p["norm_mem"]], [full(dmn)], "norm_mem_bwd",
                                       row_dtypes=[BF16])

    dmixed = _mm(dh1, wb["w_out"], "nt", "mix_out_dx")
    g["w_out"] = _mm(mixed, dh1, "tn", "mix_out_dw")
    (dfox,), (g["out_norm_fox"],) = _rowwise_vjp(_rms, [full(fox)], [p["out_norm_fox"]],
                                                 [(dmixed, FOX_WIDTH, 0, 0)], "fox_outnorm_bwd")
    (dyg_a, dz), (g["s5_b_glu"], g["out_norm_s5"]) = _rowwise_vjp(
        _s5_gate, [full(yg), full(z)], [p["s5_b_glu"], p["out_norm_s5"]], [(dmixed, S5_WIDTH, 1, 0)], "s5_gate_bwd",
        row_dtypes=[F32, BF16])
    dyg = _mm(dz, wb["s5_w_glu"], "nt", "s5_glu_dx", res=dyg_a)
    g["s5_w_glu"] = _mm(yg, dz, "tn", "s5_glu_dw")
    (dys, du_a), (dd_row,) = _rowwise_vjp(_s5_act, [full(ys), u_blk], [d_row], [full(dyg)], "s5_act_bwd",
                                          row_dtypes=[BF16, F32])
    g["s5_d"] = dd_row
    cin, cst = S5_PER * S5_GROUP_CH, S5_PER * S5_STATE
    dxr = _gmm(dys, cr_d, "nt", "s5_y_re_dx")
    dxi = _gmm(dys, ci_d, "nt", "s5_y_im_dx")
    dcr_d = _gmm_tn(xr, dys, cst, cin, S5_CHUNKS, "s5_y_re_dw")
    dci_d = _gmm_tn(xi, dys, cst, cin, S5_CHUNKS, "s5_y_im_dw")
    lam_r, lam_i, dar, dai = _scan_bwd(dxr, dxi, xr, xi, ar, ai, seqs)
    du_b = _gmm(lam_i, bbi_d, "nt", "s5_bu_im_dx", res=_gmm(lam_r, bbr_d, "nt", "s5_bu_re_dx"))
    dbbr_d = _gmm_tn(uf, lam_r, cin, cst, S5_CHUNKS, "s5_bu_re_dw")
    dbbi_d = _gmm_tn(uf, lam_i, cin, cst, S5_CHUNKS, "s5_bu_im_dw")
    d_lb_r = jnp.sum(dar, axis=0).reshape(S5_GROUPS, S5_STATE)
    d_lb_i = jnp.sum(dai, axis=0).reshape(S5_GROUPS, S5_STATE)
    g["s5_a_re"], g["s5_a_im"], g["s5_log_dt"], g["s5_b_re"], g["s5_b_im"] = s5_pull(
        (d_lb_r, d_lb_i, _blockdiag_in_grad(dbbr_d), _blockdiag_in_grad(dbbi_d)))
    g["s5_c_re"] = _blockdiag_out_grad(dcr_d)
    g["s5_c_im"] = -_blockdiag_out_grad(dci_d)

    dqn, dkn, dv, dc, dcq = _fox_bwd(qn, kn, qkv, c_wide, fox, dfox, lse, seqs)
    pair = lambda a: (a, 128, 0, 1)
    (dq_raw,), (dgq2,) = _rowwise_vjp(_rms_pair, [q_pair], [gq2], [pair(dqn)], "fox_qnorm_bwd", heads=N_PAIRS,
                                      row_dtypes=[BF16])
    (dk_raw,), (dgk2,) = _rowwise_vjp(_rms_pair, [k_pair], [gk2], [pair(dkn)], "fox_knorm_bwd", heads=N_PAIRS,
                                      row_dtypes=[BF16])
    g["fox_q_norm"] = dgq2[:, :HEAD_DIM] + dgq2[:, HEAD_DIM:]
    g["fox_k_norm"] = dgk2[:, :HEAD_DIM] + dgk2[:, HEAD_DIM:]
    dc_rows = jnp.stack([dc[:, :, 0], dc[:, :, 128]], axis=1).reshape(bh, l)
    df_rows, dfb = _forget_bwd(f_rows, f_bias, dc_rows + dcq.reshape(bh, l))
    g["fox_f_bias"] = jnp.sum(dfb.reshape(seqs, N_FOX_HEADS), axis=0)
    df = df_rows.reshape(seqs, N_FOX_HEADS, l).transpose(0, 2, 1).reshape(t, N_FOX_HEADS)
    dqkv = jnp.concatenate([dq_raw, dk_raw, dv.astype(BF16)], axis=1)
    duf = jnp.concatenate([du_a + du_b, df, jnp.zeros((t, UF_COLS - S5_WIDTH - N_FOX_HEADS), F32)],
                          axis=1).astype(BF16)
    dhn1 = _mm(duf, w_uf, "nt", "in_uf_dx", res=_mm(dqkv, w_qkv, "nt", "in_qkv_dx"))
    dw_qkv = _mm(hn1, dqkv, "tn", "in_qkv_dw")
    dw_uf = _mm(hn1, duf, "tn", "in_uf_dw")
    g["w_in"] = jnp.concatenate([dw_qkv, dw_uf[:, S5_WIDTH:S5_WIDTH + N_FOX_HEADS], dw_uf[:, :S5_WIDTH]], axis=1)
    (dx,), (g["norm_mix"],) = _rowwise_vjp(_rms, [full(x)], [p["norm_mix"]], [full(dhn1)], "norm_mix_bwd",
                                           adds=[full(dh1)])
    return loss, dx.reshape(seqs, l, d), g


def _place():
    return lax.axis_index("x"), lax.axis_index("y"), lax.axis_index("c")


def _other_chips(x, y):
    return [(1 - x, y), (x, 1 - y), (1 - x, 1 - y)]


ANY = pl.BlockSpec(memory_space=pl.ANY)


def _gather_weights(shards, col_kind, taps):
    n = len(shards)

    def body(*refs):
        ins, tap_in, outs, tap_out = refs[:n], refs[n], refs[n + 1:2 * n + 1], refs[2 * n + 1]
        ici_send, ici_recv, d2d_send, d2d_recv, own_send, own_recv = refs[2 * n + 2:]
        x, y, c = _place()
        mine = 2 * x + y
        chips = _other_chips(x, y)
        sibling = (x, y, 1 - c)

        def piece(a, s, h):
            r, cs = ins[a].shape
            hr = r // 2
            if col_kind[a]:
                return outs[a].at[pl.ds(pl.multiple_of(h * hr, 16), hr), pl.ds(pl.multiple_of(s * cs, 128), cs)]
            return outs[a].at[pl.ds(pl.multiple_of(s * r + h * hr, 16), hr), :]

        def slab(a, s):
            r, cs = ins[a].shape
            if col_kind[a]:
                return outs[a].at[:, pl.ds(pl.multiple_of(s * cs, 128), cs)]
            return outs[a].at[pl.ds(pl.multiple_of(s * r, 16), r), :]

        def own_half(a, h):
            hr = ins[a].shape[0] // 2
            return ins[a].at[pl.ds(pl.multiple_of(h * hr, 16), hr), :]

        sends = []
        for a in range(n):
            cp = pltpu.make_async_remote_copy(
                src_ref=ins[a], dst_ref=slab(a, mine), send_sem=own_send.at[a], recv_sem=own_recv.at[a],
                device_id=sibling, device_id_type=MESH)
            cp.start()
            sends.append(cp)
        cp = pltpu.make_async_remote_copy(
            src_ref=tap_in, dst_ref=tap_out.at[mine], send_sem=own_send.at[n], recv_sem=own_recv.at[n],
            device_id=sibling, device_id_type=MESH)
        cp.start()
        sends.append(cp)
        for a in range(n):
            for j, (px, py) in enumerate(chips):
                cp = pltpu.make_async_remote_copy(
                    src_ref=own_half(a, c), dst_ref=piece(a, mine, c), send_sem=ici_send.at[3 * a + j],
                    recv_sem=ici_recv.at[3 * a + j], device_id=(px, py, c), device_id_type=MESH)
                cp.start()
                sends.append(cp)
        for j, (px, py) in enumerate(chips):
            cp = pltpu.make_async_remote_copy(
                src_ref=tap_in, dst_ref=tap_out.at[mine], send_sem=ici_send.at[3 * n + j],
                recv_sem=ici_recv.at[3 * n + j], device_id=(px, py, c), device_id_type=MESH)
            cp.start()
            sends.append(cp)
        for a in range(n):
            for j, (px, py) in enumerate(chips):
                got = piece(a, 2 * px + py, c)
                pltpu.make_async_remote_copy(
                    src_ref=got, dst_ref=got, send_sem=ici_send.at[3 * a + j], recv_sem=ici_recv.at[3 * a + j],
                    device_id=(px, py, c), device_id_type=MESH).wait_recv()
                fwd = pltpu.make_async_remote_copy(
                    src_ref=got, dst_ref=got, send_sem=d2d_send.at[3 * a + j], recv_sem=d2d_recv.at[3 * a + j],
                    device_id=(x, y, 1 - c), device_id_type=MESH)
                fwd.start()
                sends.append(fwd)
        for a in range(n):
            for j, (px, py) in enumerate(chips):
                other = piece(a, 2 * px + py, 1 - c)
                pltpu.make_async_remote_copy(
                    src_ref=other, dst_ref=other, send_sem=d2d_send.at[3 * a + j], recv_sem=d2d_recv.at[3 * a + j],
                    device_id=(x, y, 1 - c), device_id_type=MESH).wait_recv()
        for j, (px, py) in enumerate(chips):
            pltpu.make_async_remote_copy(
                src_ref=tap_in, dst_ref=tap_out.at[2 * px + py], send_sem=ici_send.at[3 * n + j],
                recv_sem=ici_recv.at[3 * n + j], device_id=(px, py, c), device_id_type=MESH).wait_recv()
        for a in range(n):
            pltpu.make_async_remote_copy(
                src_ref=ins[a], dst_ref=slab(a, mine), send_sem=own_send.at[a], recv_sem=own_recv.at[a],
                device_id=sibling, device_id_type=MESH).wait_recv()
        pltpu.make_async_remote_copy(
            src_ref=tap_in, dst_ref=tap_out.at[mine], send_sem=own_send.at[n], recv_sem=own_recv.at[n],
            device_id=sibling, device_id_type=MESH).wait_recv()
        for cp in sends:
            cp.wait_send()

    def full_shape(a):
        r, cs = shards[a].shape
        return (r, 4 * cs) if col_kind[a] else (4 * r, cs)

    res = pl.pallas_call(
        body, name="gather_weights", in_specs=[ANY] * (n + 1), out_specs=[ANY] * (n + 1),
        out_shape=[jax.ShapeDtypeStruct(full_shape(a), shards[a].dtype) for a in range(n)]
        + [jax.ShapeDtypeStruct((4,) + taps.shape, taps.dtype)],
        scratch_shapes=[pltpu.SemaphoreType.DMA((3 * n + 3,)), pltpu.SemaphoreType.DMA((3 * n + 3,)),
                        pltpu.SemaphoreType.DMA((3 * n,)), pltpu.SemaphoreType.DMA((3 * n,)),
                        pltpu.SemaphoreType.DMA((n + 1,)), pltpu.SemaphoreType.DMA((n + 1,))],
        compiler_params=pltpu.CompilerParams(has_side_effects=True),
    )(*shards, taps)
    return res[:n], res[n]


HBM = pl.BlockSpec(memory_space=pltpu.HBM)
SEM = pl.BlockSpec(memory_space=pltpu.SEMAPHORE)
DATAFLOW = pltpu.SideEffectType.DATAFLOW_SIDE_EFFECTING


def _in_hbm(a):
    return pltpu.with_memory_space_constraint(a, pltpu.HBM)


def _split_start(name, srcs, lands, n_copies, plan):
    n = len(srcs)

    def body(*refs):
        src_refs, land_refs = refs[:n], refs[n:2 * n]
        send_sems, recv_sems = refs[2 * n], refs[2 * n + 1]
        for i, (src, dst, dev) in enumerate(plan(src_refs, land_refs)):
            pltpu.make_async_remote_copy(src_ref=src, dst_ref=dst, send_sem=send_sems.at[i], recv_sem=recv_sems.at[i],
                                         device_id=dev, device_id_type=MESH).start()
        refs[-1][...] = jnp.zeros((8, 128), F32)

    res = pl.pallas_call(
        body, name=name, in_specs=[HBM] * (2 * n),
        out_specs=[SEM, SEM] + [HBM] * (2 * n) + [pl.BlockSpec(memory_space=pltpu.VMEM)],
        out_shape=[pltpu.SemaphoreType.DMA((n_copies,)), pltpu.SemaphoreType.DMA((n_copies,))]
        + [pltpu.HBM(a.shape, a.dtype) for a in list(srcs) + list(lands)] + [jax.ShapeDtypeStruct((8, 128), F32)],
        input_output_aliases={i: 2 + i for i in range(2 * n)},
        compiler_params=pltpu.CompilerParams(has_side_effects=DATAFLOW),
    )(*[_in_hbm(a) for a in list(srcs) + list(lands)])
    return res[0], res[1], list(res[2:2 + n]), list(res[2 + n:2 + 2 * n]), res[-1]


def _split_wait(name, send_sems, recv_sems, srcs, lands, after, plan):
    n = len(srcs)

    def body(*refs):
        src_refs, land_refs = refs[:n], refs[n:2 * n]
        send_ref, recv_ref = refs[2 * n], refs[2 * n + 1]
        for i, (src, dst, dev) in enumerate(plan(src_refs, land_refs)):
            cp = pltpu.make_async_remote_copy(src_ref=src, dst_ref=dst, send_sem=send_ref.at[i], recv_sem=recv_ref.at[i],
                                              device_id=dev, device_id_type=MESH)
            cp.wait_send()
            cp.wait_recv()

    res = pl.pallas_call(
        body, name=name, in_specs=[HBM] * (2 * n) + [SEM, SEM, ANY], out_specs=[HBM] * (2 * n),
        out_shape=[pltpu.HBM(a.shape, a.dtype) for a in list(srcs) + list(lands)],
        input_output_aliases={i: i for i in range(2 * n)},
        compiler_params=pltpu.CompilerParams(has_side_effects=DATAFLOW),
    )(*srcs, *lands, send_sems, recv_sems, after)
    return list(res[:n]), list(res[n:])


def _late_gather_plan(col_kind):
    def plan(src_refs, land_refs):
        x, y, c = _place()
        mine = 2 * x + y
        copies = []
        for a, (src, land) in enumerate(zip(src_refs, land_refs)):
            r, cs = src.shape
            if col_kind[a]:
                dst = land.at[:, pl.ds(pl.multiple_of(mine * cs, 128), cs)]
            else:
                dst = land.at[pl.ds(pl.multiple_of(mine * r, 16), r), :]
            copies.append((src, dst, (x, y, 1 - c)))
            copies += [(src, dst, (px, py, c)) for (px, py) in _other_chips(x, y)]
        return copies
    return plan


def _late_reduce_plan(col_kind):
    def plan(src_refs, land_refs):
        x, y, c = _place()
        copies = []
        for a, (src, land) in enumerate(zip(src_refs, land_refs)):
            for j, (px, py) in enumerate(_other_chips(x, y)):
                if col_kind[a]:
                    cs = land.shape[2]
                    piece = src.at[:, pl.ds(pl.multiple_of((2 * px + py) * cs, 128), cs)]
                else:
                    piece = src.at[2 * px + py]
                copies.append((piece, land.at[j], (px, py, c)))
        return copies
    return plan


def _pair_exchange_halves(name, grads, col_kind):
    n = len(grads)

    def body(*refs):
        ins, outs = refs[:n], refs[n:2 * n]
        send_sems, recv_sems = refs[2 * n:]
        x, y, c = _place()
        copies = []
        for a in range(n):
            if col_kind[a]:
                hr = ins[a].shape[0] // 2
                src = ins[a].at[pl.ds(pl.multiple_of((1 - c) * hr, 8), hr), :]
            else:
                hr = ins[a].shape[1] // 2
                src = ins[a].at[:, pl.ds(pl.multiple_of((1 - c) * hr, 8), hr), :]
            cp = pltpu.make_async_remote_copy(
                src_ref=src, dst_ref=outs[a], send_sem=send_sems.at[a], recv_sem=recv_sems.at[a],
                device_id=(x, y, 1 - c), device_id_type=MESH)
            cp.start()
            copies.append(cp)
        for cp in copies:
            cp.wait()

    def half_shape(a):
        s = grads[a].shape
        return (s[0] // 2, s[1]) if col_kind[a] else (4, s[1] // 2, s[2])

    return pl.pallas_call(
        body, name=name, in_specs=[ANY] * n, out_specs=[ANY] * n,
        out_shape=[jax.ShapeDtypeStruct(half_shape(a), grads[a].dtype) for a in range(n)],
        scratch_shapes=[pltpu.SemaphoreType.DMA((n,)), pltpu.SemaphoreType.DMA((n,))],
        compiler_params=pltpu.CompilerParams(has_side_effects=True),
    )(*grads)


def _chip_exchange(name, sums, col_kind):
    n = len(sums)

    def piece_shape(a):
        s = sums[a].shape
        return (s[0], s[1] // 4) if col_kind[a] else (s[1], s[2])

    def body(*refs):
        ins, outs = refs[:n], refs[n:2 * n]
        send_sems, recv_sems = refs[2 * n:]
        x, y, c = _place()
        copies = []
        for a in range(n):
            for j, (px, py) in enumerate(_other_chips(x, y)):
                if col_kind[a]:
                    cs = piece_shape(a)[1]
                    src = ins[a].at[:, pl.ds(pl.multiple_of((2 * px + py) * cs, 128), cs)]
                else:
                    src = ins[a].at[2 * px + py]
                cp = pltpu.make_async_remote_copy(
                    src_ref=src, dst_ref=outs[a].at[j], send_sem=send_sems.at[3 * a + j],
                    recv_sem=recv_sems.at[3 * a + j], device_id=(px, py, c), device_id_type=MESH)
                cp.start()
                copies.append(cp)
        for cp in copies:
            cp.wait()

    return pl.pallas_call(
        body, name=name, in_specs=[ANY] * n, out_specs=[ANY] * n,
        out_shape=[jax.ShapeDtypeStruct((3,) + piece_shape(a), sums[a].dtype) for a in range(n)],
        scratch_shapes=[pltpu.SemaphoreType.DMA((3 * n,)), pltpu.SemaphoreType.DMA((3 * n,))],
        compiler_params=pltpu.CompilerParams(has_side_effects=True),
    )(*sums)


def _pair_swap_halves(halves):
    n = len(halves)

    def body(*refs):
        ins, outs = refs[:n], refs[n:2 * n]
        send_sems, recv_sems = refs[2 * n:]
        x, y, c = _place()
        copies = []
        for a in range(n):
            cp = pltpu.make_async_remote_copy(
                src_ref=ins[a], dst_ref=outs[a], send_sem=send_sems.at[a], recv_sem=recv_sems.at[a],
                device_id=(x, y, 1 - c), device_id_type=MESH)
            cp.start()
            copies.append(cp)
        for cp in copies:
            cp.wait()

    return pl.pallas_call(
        body, name="reduce_pair_swap", in_specs=[ANY] * n, out_specs=[ANY] * n,
        out_shape=[jax.ShapeDtypeStruct(s.shape, s.dtype) for s in halves],
        scratch_shapes=[pltpu.SemaphoreType.DMA((n,)), pltpu.SemaphoreType.DMA((n,))],
        compiler_params=pltpu.CompilerParams(has_side_effects=True),
    )(*halves)


def _chip_sum(name, chip_sel, own, col, others):
    _, r, c = others.shape
    tr = _pick(r, (256, 128, 64, 32, 16))
    if col:
        own_spec = pl.BlockSpec((tr, c), lambda i, s: (i, s[0]))
    else:
        own_spec = pl.BlockSpec((None, tr, c), lambda i, s: (s[0], i, 0))
    specs = [own_spec] + [pl.BlockSpec((None, tr, c), lambda i, s, k=k: (k, i, 0)) for k in range(3)]

    def body(s_ref, own_ref, r0, r1, r2, o_ref):
        o_ref[...] = ((own_ref[...].astype(F32) + r0[...].astype(F32)) + r1[...].astype(F32)) + r2[...].astype(F32)

    return pl.pallas_call(
        body, name=name,
        grid_spec=pltpu.PrefetchScalarGridSpec(
            num_scalar_prefetch=1, grid=(r // tr,), in_specs=specs,
            out_specs=pl.BlockSpec((tr, c), lambda i, s: (i, 0))),
        out_shape=jax.ShapeDtypeStruct((r, c), F32),
        compiler_params=_params(("parallel",)),
    )(chip_sel, own, others, others, others)


def _pair_sum(name, c_sel, grad, recv, col):
    if col:
        r, c4 = grad.shape
        hr, c = r // 2, c4 // 4
    else:
        _, r, c = grad.shape
        hr = r // 2
    tr = _pick(hr, (256, 128, 64, 32, 16))
    nb = hr // tr

    def body(s_ref, g_ref, r_ref, o_ref):
        o_ref[...] = (g_ref[...] + r_ref[...]).astype(o_ref.dtype)

    if col:
        in_specs = [pl.BlockSpec((tr, c), lambda k, i, s: (s[0] * nb + i, k)), pl.BlockSpec((tr, c), lambda k, i, s: (i, k))]
        out_spec = pl.BlockSpec((tr, c), lambda k, i, s: (i, k))
    else:
        in_specs = [pl.BlockSpec((None, tr, c), lambda k, i, s: (k, s[0] * nb + i, 0)),
                    pl.BlockSpec((None, tr, c), lambda k, i, s: (k, i, 0))]
        out_spec = pl.BlockSpec((None, tr, c), lambda k, i, s: (k, i, 0))
    return pl.pallas_call(
        body, name=name,
        grid_spec=pltpu.PrefetchScalarGridSpec(num_scalar_prefetch=1, grid=(4, nb), in_specs=in_specs,
                                               out_specs=out_spec),
        out_shape=jax.ShapeDtypeStruct(recv.shape, BF16),
        compiler_params=_params(("parallel", "parallel")),
    )(c_sel, grad, recv)


def _allreduce_small(vals):
    sizes = [int(math.prod(v.shape)) for v in vals]
    padded = [-(-s // 128) * 128 for s in sizes]
    total = -(-sum(padded) // 1024) * 1024
    flat = [jnp.pad(v.reshape(-1), (0, p - s)) for v, s, p in zip(vals, sizes, padded)]
    flat.append(jnp.zeros((total - sum(padded),), F32))
    packed = jnp.concatenate(flat).reshape(total // 128, 128)

    def body(in_ref, out_ref, r0, r1, r2, send_sems, recv_sems):
        x, y, c = _place()
        out_ref[...] = in_ref[...]
        for k, (peer, land) in enumerate(zip([(x, y, 1 - c), (1 - x, y, c), (x, 1 - y, c)], (r0, r1, r2))):
            cp = pltpu.make_async_remote_copy(
                src_ref=out_ref, dst_ref=land, send_sem=send_sems.at[k], recv_sem=recv_sems.at[k],
                device_id=peer, device_id_type=MESH)
            cp.start()
            cp.wait()
            out_ref[...] = out_ref[...] + land[...]

    vm = pl.BlockSpec(memory_space=pltpu.VMEM)
    summed = pl.pallas_call(
        body, name="allreduce_small", in_specs=[vm], out_specs=vm,
        out_shape=jax.ShapeDtypeStruct(packed.shape, F32),
        scratch_shapes=[pltpu.VMEM(packed.shape, F32)] * 3
        + [pltpu.SemaphoreType.DMA((3,)), pltpu.SemaphoreType.DMA((3,))],
        compiler_params=pltpu.CompilerParams(has_side_effects=True, vmem_limit_bytes=VMEM_LIMIT_BYTES),
    )(packed).reshape(-1)
    outs, off = [], 0
    for v, s, p in zip(vals, sizes, padded):
        outs.append(summed[off:off + s].reshape(v.shape))
        off += p
    return outs


def _adamw_math(w, g, m, v):
    m2 = ADAM_B1 * m + (1.0 - ADAM_B1) * g
    v2 = ADAM_B2 * v + (1.0 - ADAM_B2) * (g * g)
    m_hat = m2 / (1.0 - ADAM_B1 ** ADAM_STEP)
    v_hat = v2 / (1.0 - ADAM_B2 ** ADAM_STEP)
    delta = -ADAM_LR * (m_hat / (jnp.sqrt(v_hat) + ADAM_EPS) + ADAM_WD * w)
    return delta, m2, v2


def _adamw_big(name, c_sel, w, g_mine, g_sibling, m, v):
    r, c = w.shape
    hr = r // 2
    tr = _pick(hr, (256, 128, 64, 32, 16, 8))
    nb = hr // tr

    def body(s_ref, w_ref, ga_ref, gb_ref, m_ref, v_ref, go_ref, d_ref, mo_ref, vo_ref):
        gv = jnp.where(pl.program_id(0) == s_ref[0], ga_ref[...], gb_ref[...])
        d, m2, v2 = _adamw_math(w_ref[...], gv, m_ref[...], v_ref[...])
        go_ref[...] = gv
        d_ref[...] = d
        mo_ref[...] = m2
        vo_ref[...] = v2

    blk = pl.BlockSpec((tr, c), lambda h, i, s: (h * nb + i, 0))
    half = pl.BlockSpec((tr, c), lambda h, i, s: (i, 0))
    return pl.pallas_call(
        body, name=name,
        grid_spec=pltpu.PrefetchScalarGridSpec(
            num_scalar_prefetch=1, grid=(2, nb), in_specs=[blk, half, half, blk, blk], out_specs=[blk] * 4),
        out_shape=[jax.ShapeDtypeStruct((r, c), F32)] * 4, compiler_params=_params(("parallel", "parallel")),
    )(c_sel, w, g_mine, g_sibling, m, v)


def _adamw_small(ws, gs, ms, vs):
    n = len(ws)

    def body(*refs):
        w_r, g_r, m_r, v_r = refs[:n], refs[n:2 * n], refs[2 * n:3 * n], refs[3 * n:4 * n]
        o = refs[4 * n:]
        for a in range(n):
            gv = g_r[a][...]
            d, m2, v2 = _adamw_math(w_r[a][...], gv, m_r[a][...], v_r[a][...])
            o[a][...] = gv
            o[n + a][...] = d
            o[2 * n + a][...] = m2
            o[3 * n + a][...] = v2

    res = pl.pallas_call(
        body, name="adamw_small", out_shape=[jax.ShapeDtypeStruct(w.shape, F32) for _ in range(4) for w in ws],
        compiler_params=_params(),
    )(*ws, *gs, *ms, *vs)
    return res[:n], res[n:2 * n], res[2 * n:3 * n], res[3 * n:]


def _full_from_gathered(name, gathered):
    if name == "w_in":
        rows = gathered.shape[0] // 4
        return gathered.reshape(4, rows, gathered.shape[1]).transpose(1, 0, 2).reshape(rows, 4 * gathered.shape[1])
    return gathered


def _reduce_layout(name, full):
    if name in COL_KIND:
        return full
    if name == "w_in":
        rows, cols = full.shape
        return full.reshape(rows, 4, cols // 4).transpose(1, 0, 2)
    return full.reshape(4, full.shape[0] // 4, full.shape[1])


def kernel(x, mem, norm_mix, w_in, fox_q_norm, fox_k_norm, fox_f_bias, s5_a_re, s5_a_im, s5_log_dt, s5_b_re, s5_b_im, s5_c_re, s5_c_im, s5_d, s5_w_glu, s5_b_glu, out_norm_fox, out_norm_s5, w_out, norm_cross, norm_mem, w_xq, w_xkv, xq_norm, xk_norm, w_xo, norm_ffn, w_ffn_up, ffn_conv_w, ffn_conv_b, w_ffn_down, loss_target, m_norm_mix, m_w_in, m_fox_q_norm, m_fox_k_norm, m_fox_f_bias, m_s5_a_re, m_s5_a_im, m_s5_log_dt, m_s5_b_re, m_s5_b_im, m_s5_c_re, m_s5_c_im, m_s5_d, m_s5_w_glu, m_s5_b_glu, m_out_norm_fox, m_out_norm_s5, m_w_out, m_norm_cross, m_norm_mem, m_w_xq, m_w_xkv, m_xq_norm, m_xk_norm, m_w_xo, m_norm_ffn, m_w_ffn_up, m_ffn_conv_w, m_ffn_conv_b, m_w_ffn_down, v_norm_mix, v_w_in, v_fox_q_norm, v_fox_k_norm, v_fox_f_bias, v_s5_a_re, v_s5_a_im, v_s5_log_dt, v_s5_b_re, v_s5_b_im, v_s5_c_re, v_s5_c_im, v_s5_d, v_s5_w_glu, v_s5_b_glu, v_out_norm_fox, v_out_norm_s5, v_w_out, v_norm_cross, v_norm_mem, v_w_xq, v_w_xkv, v_xq_norm, v_xk_norm, v_w_xo, v_norm_ffn, v_w_ffn_up, v_ffn_conv_w, v_ffn_conv_b, v_w_ffn_down):
    given = dict(locals())
    w = {n: given[n] for n in WEIGHTS}
    m = {n: given["m_" + n] for n in WEIGHTS}
    v = {n: given["v_" + n] for n in WEIGHTS}
    xi, yi, ci = _place()
    chip = (2 * xi + yi).astype(jnp.int32)

    c_sel = ci.astype(jnp.int32).reshape(1)
    chip_sel = chip.reshape(1)
    early_kind = [n in COL_KIND for n in EARLY_WEIGHTS]
    late_kind = [n in COL_KIND for n in LATE_WEIGHTS]

    gathered, taps = _gather_weights([w[n][0].astype(BF16) for n in EARLY_WEIGHTS], early_kind, w["ffn_conv_w"][0])
    wb = {n: _full_from_gathered(n, gathered[k]) for k, n in enumerate(EARLY_WEIGHTS)}
    conv_w = taps.transpose(1, 0, 2).reshape(3, D_FF)
    late_shards = [w[n][0].astype(BF16) for n in LATE_WEIGHTS]
    late_full = [lax.empty((s.shape[0], 4 * s.shape[1]) if ck else (4 * s.shape[0], s.shape[1]), BF16)
                 for s, ck in zip(late_shards, late_kind)]
    gather_plan = _late_gather_plan(late_kind)
    g_send, g_recv, g_srcs, g_lands, g_started = _split_start(
        "gather_late_start", late_shards, late_full, 4 * len(LATE_WEIGHTS), gather_plan)

    def late_weights(after):
        _, full = _split_wait("gather_late_wait", g_send, g_recv, g_srcs, g_lands, after, gather_plan)
        return dict(zip(LATE_WEIGHTS, full))

    reduce_plan = _late_reduce_plan(late_kind)
    late_reduce = {}

    def early_grads(late_g):
        grads = [_reduce_layout(n, late_g[n]) for n in LATE_WEIGHTS]
        from_sibling = _pair_exchange_halves("reduce_pair_exchange_late", grads, late_kind)
        sums = [_pair_sum("reduce_pair_sum_" + n, c_sel, gr, rv, ck)
                for n, gr, rv, ck in zip(LATE_WEIGHTS, grads, from_sibling, late_kind)]
        lands = [lax.empty((3, s.shape[0], s.shape[1] // 4) if ck else (3,) + s.shape[1:], BF16)
                 for s, ck in zip(sums, late_kind)]
        late_reduce["sems"] = _split_start("reduce_late_start", sums, lands, 3 * len(LATE_WEIGHTS), reduce_plan)
        return late_reduce["sems"][4][0:1, 0:1]

    p = {n: w[n][0] for n in SMALL}
    p["ffn_conv_w"] = conv_w
    for n in ("norm_mix", "fox_q_norm", "fox_k_norm", "fox_f_bias", "s5_b_glu", "out_norm_fox", "out_norm_s5",
              "norm_cross", "norm_mem", "xq_norm", "xk_norm", "norm_ffn", "ffn_conv_b"):
        p[n] = p[n].reshape(1, -1)
    p["norm_mix"] = p["norm_mix"] + g_started[0:1, 0:1]
    loss, grad_x, g = _local_step(x, mem, loss_target, p, wb, late_weights, early_grads)

    small_names = list(SMALL) + ["ffn_conv_w"]
    small_vals = [g[n].reshape(w[n].shape if n != "ffn_conv_w" else (1, 3, D_FF)) for n in small_names] + [loss]
    reduced = _allreduce_small(small_vals)
    loss_all = reduced[-1].reshape(())
    conv_w_grad = lax.dynamic_slice_in_dim(reduced[-2], chip * (D_FF // 4), D_FF // 4, axis=2)
    sg, sd, sm, sv = _adamw_small(
        [w[n] for n in small_names], list(reduced[:len(SMALL)]) + [conv_w_grad],
        [m[n] for n in small_names], [v[n] for n in small_names])
    out_g = dict(zip(small_names, sg))
    out_d = dict(zip(small_names, sd))
    out_m = dict(zip(small_names, sm))
    out_v = dict(zip(small_names, sv))

    grads = [_reduce_layout(n, g[n]) for n in EARLY_WEIGHTS]
    from_sibling = _pair_exchange_halves("reduce_pair_exchange_early", grads, early_kind)
    pair_sums = [_pair_sum("reduce_pair_sum_" + n, c_sel, gr, rv, ck)
                 for n, gr, rv, ck in zip(EARLY_WEIGHTS, grads, from_sibling, early_kind)]
    from_chips = _chip_exchange("reduce_chip_exchange_early", pair_sums, early_kind)
    r_send, r_recv, r_srcs, r_lands, _ = late_reduce["sems"]
    late_sums, late_from_chips = _split_wait("reduce_late_wait", r_send, r_recv, r_srcs, r_lands, from_chips[0],
                                             reduce_plan)
    col_kind = early_kind + late_kind
    halves = [_chip_sum("reduce_chip_sum_" + n, chip_sel, ps, ck, fc)
              for n, ps, fc, ck in zip(BIG, pair_sums + late_sums, list(from_chips) + late_from_chips, col_kind)]
    sibling_halves = _pair_swap_halves(halves)
    for n, mine, theirs in zip(BIG, halves, sibling_halves):
        go, d, m2, v2 = _adamw_big("adamw_" + n, c_sel, w[n][0], mine, theirs, m[n][0], v[n][0])
        out_g[n], out_d[n], out_m[n], out_v[n] = go[None], d[None], m2[None], v2[None]

    return (loss_all, grad_x, *[out_g[n] for n in WEIGHTS], *[out_d[n] for n in WEIGHTS],
            *[out_m[n] for n in WEIGHTS], *[out_v[n] for n in WEIGHTS])
```

```python
import functools
import math

import jax
import jax.numpy as jnp
from jax import lax
from jax.experimental import pallas as pl
from jax.experimental.pallas import tpu as pltpu

F32 = jnp.float32
BF16 = jnp.bfloat16

D_MODEL = 1024
FOX_WIDTH = 512
HEAD_DIM = 64
N_FOX_HEADS = 8
S5_WIDTH = 512
S5_GROUP_CH = 16
S5_GROUPS = 32
S5_STATE = 64
S5_CH = S5_GROUPS * S5_STATE
N_X_HEADS = 4
X_HEAD_DIM = 256
N_MEM = 256
D_FF = 2816
UF_COLS = 640
EPS = 1e-6
ADAM_LR = 0.001
ADAM_B1 = 0.9
ADAM_B2 = 0.999
ADAM_EPS = 1e-08
ADAM_WD = 0.01
ADAM_STEP = 10

VMEM_LIMIT_BYTES = 56 * 1024 * 1024
MM_BLOCK_BYTES = 6 * 1024 * 1024
MM_VMEM_BYTES = 40 * 1024 * 1024
MESH = pl.DeviceIdType.MESH

EARLY_WEIGHTS = ("w_in", "s5_w_glu", "w_out")
LATE_WEIGHTS = ("w_xq", "w_xkv", "w_xo", "w_ffn_up", "w_ffn_down")
BIG = EARLY_WEIGHTS + LATE_WEIGHTS
COL_KIND = ("w_xkv", "w_ffn_up")
SMALL = ("norm_mix", "fox_q_norm", "fox_k_norm", "fox_f_bias", "s5_a_re", "s5_a_im", "s5_log_dt",
         "s5_b_re", "s5_b_im", "s5_c_re", "s5_c_im", "s5_d", "s5_b_glu", "out_norm_fox", "out_norm_s5",
         "norm_cross", "norm_mem", "xq_norm", "xk_norm", "norm_ffn", "ffn_conv_b")
WEIGHTS = ("norm_mix", "w_in", "fox_q_norm", "fox_k_norm", "fox_f_bias", "s5_a_re", "s5_a_im", "s5_log_dt",
           "s5_b_re", "s5_b_im", "s5_c_re", "s5_c_im", "s5_d", "s5_w_glu", "s5_b_glu", "out_norm_fox",
           "out_norm_s5", "w_out", "norm_cross", "norm_mem", "w_xq", "w_xkv", "xq_norm", "xk_norm", "w_xo",
           "norm_ffn", "w_ffn_up", "ffn_conv_w", "ffn_conv_b", "w_ffn_down")


def _params(sem=None):
    return pltpu.CompilerParams(dimension_semantics=sem, vmem_limit_bytes=VMEM_LIMIT_BYTES)


def _pick(n, cands):
    for c in cands:
        if n % c == 0:
            return c
    return n


_DIMS = {"nn": (((1,), (0,)), ((), ())), "nt": (((1,), (1,)), ((), ())), "tn": (((0,), (0,)), ((), ()))}


def _mm(a, b, mode, name, out_dtype=F32, res=None):
    if mode == "nn":
        (m, k), (k2, n) = a.shape, b.shape
    elif mode == "nt":
        (m, k), (n, k2) = a.shape, b.shape
    else:
        (k, m), (k2, n) = a.shape, b.shape
    assert k == k2, (name, a.shape, b.shape)

    has_res = res is not None
    a_size, b_size = a.dtype.itemsize, b.dtype.itemsize
    o_size = jnp.dtype(out_dtype).itemsize + (res.dtype.itemsize if has_res else 0)

    def tiles(dim):
        return [c for c in (1024, 512, 256, 128) if dim % c == 0] or [dim]

    best = None
    for tm in tiles(m):
        for tn in tiles(n):
            a_blk, b_blk = tm * k * a_size, tn * k * b_size
            if max(a_blk, b_blk) > MM_BLOCK_BYTES or 2 * (a_blk + b_blk + tm * tn * o_size) > MM_VMEM_BYTES:
                continue
            for rows_outer in (True, False):
                moved = (m * k * a_size + (m // tm) * n * k * b_size) if rows_outer else \
                        (n * k * b_size + (n // tn) * m * k * a_size)
                key = (moved, -(tm * tn))
                if best is None or key < best[0]:
                    best = (key, tm, tn, rows_outer)
    assert best is not None, (name, a.shape, b.shape)
    _, tm, tn, rows_outer = best
    ij = (lambda g0, g1: (g0, g1)) if rows_outer else (lambda g0, g1: (g1, g0))
    if mode == "tn":
        a_spec = pl.BlockSpec((k, tm), lambda g0, g1: (0, ij(g0, g1)[0]))
    else:
        a_spec = pl.BlockSpec((tm, k), lambda g0, g1: (ij(g0, g1)[0], 0))
    if mode == "nt":
        b_spec = pl.BlockSpec((tn, k), lambda g0, g1: (ij(g0, g1)[1], 0))
    else:
        b_spec = pl.BlockSpec((k, tn), lambda g0, g1: (0, ij(g0, g1)[1]))
    o_spec = pl.BlockSpec((tm, tn), lambda g0, g1: ij(g0, g1))
    grid = (m // tm, n // tn) if rows_outer else (n // tn, m // tm)
    dims = _DIMS[mode]

    def body(*refs):
        a_ref, b_ref = refs[0], refs[1]
        o_ref = refs[-1]
        acc = lax.dot_general(a_ref[...].astype(BF16), b_ref[...].astype(BF16), dims, preferred_element_type=F32)
        if has_res:
            acc = acc + refs[2][...].astype(F32)
        o_ref[...] = acc.astype(o_ref.dtype)

    return pl.pallas_call(
        body, name=name, grid=grid,
        in_specs=[a_spec, b_spec] + ([o_spec] if has_res else []),
        out_specs=o_spec, out_shape=jax.ShapeDtypeStruct((m, n), out_dtype),
        compiler_params=_params(("parallel", "parallel")),
    )(*((a, b, res) if has_res else (a, b)))


def _row_spec(tm, bc, off, step):
    return pl.BlockSpec((tm, bc), lambda i, h: (i, off + step * h))


ROW_TILE_ELEMS = 512 * 1024


def _row_tile(t, rows):
    widest = max(bc for (_, bc, _, _) in rows)
    return _pick(t, (min(t, ROW_TILE_ELEMS // widest), 512, 256, 128, 64, 8))


def _rowwise(fn, rows, pars, outs, name, heads=1):
    t = rows[0][0].shape[0]
    tm = _row_tile(t, rows)
    nr, npar = len(rows), len(pars)

    def body(*refs):
        vals = [r[...].astype(F32) for r in refs[:nr + npar]]
        res = fn(*vals)
        if not isinstance(res, (tuple, list)):
            res = (res,)
        for o_ref, v in zip(refs[nr + npar:], res):
            o_ref[...] = v.astype(o_ref.dtype)

    in_specs = [_row_spec(tm, bc, off, st) for (_, bc, off, st) in rows]
    in_specs += [pl.BlockSpec(p.shape, lambda i, h: (0, 0)) for p in pars]
    out_specs = [_row_spec(tm, bc, 0, st) for (_, bc, st, _) in outs]
    out_shape = [jax.ShapeDtypeStruct((t, c), dt) for (c, _, _, dt) in outs]
    res = pl.pallas_call(
        body, name=name, grid=(t // tm, heads), in_specs=in_specs, out_specs=out_specs, out_shape=out_shape,
        compiler_params=_params(("parallel", "parallel")),
    )(*[r[0] for r in rows], *pars)
    return res[0] if len(res) == 1 else res


def _rowwise_vjp(fn, rows, pars, cts, name, heads=1, adds=None, row_dtypes=None):
    t = rows[0][0].shape[0]
    tm = _row_tile(t, rows)
    nr, npar, nct = len(rows), len(pars), len(cts)
    adds = adds or [None] * nr
    add_list = [a for a in adds if a is not None]
    row_dtypes = row_dtypes or [F32] * nr

    def body(*refs):
        i, h = pl.program_id(0), pl.program_id(1)
        p = 0
        row_v = [r[...].astype(F32) for r in refs[p:p + nr]]; p += nr
        par_v = [r[...].astype(F32) for r in refs[p:p + npar]]; p += npar
        ct_v = [r[...].astype(F32) for r in refs[p:p + nct]]; p += nct
        add_refs = refs[p:p + len(add_list)]; p += len(add_list)
        drow_refs = refs[p:p + nr]; p += nr
        dpar_refs = refs[p:p + npar]

        def wrapped(*a):
            r = fn(*a)
            return tuple(r) if isinstance(r, (tuple, list)) else (r,)

        _, pull = jax.vjp(wrapped, *row_v, *par_v)
        grads = pull(tuple(ct_v))
        ai = 0
        for k in range(nr):
            g = grads[k]
            if adds[k] is not None:
                g = g + add_refs[ai][...].astype(F32)
                ai += 1
            drow_refs[k][...] = g.astype(drow_refs[k].dtype)

        @pl.when((i == 0) & (h == 0))
        def _():
            for r in dpar_refs:
                r[...] = jnp.zeros(r.shape, r.dtype)

        for k in range(npar):
            dpar_refs[k][...] += grads[nr + k]

    in_specs = [_row_spec(tm, bc, off, st) for (_, bc, off, st) in rows]
    in_specs += [pl.BlockSpec(q.shape, lambda i, h: (0, 0)) for q in pars]
    in_specs += [_row_spec(tm, bc, off, st) for (_, bc, off, st) in cts]
    in_specs += [_row_spec(tm, bc, off, st) for (_, bc, off, st) in add_list]
    out_specs = [_row_spec(tm, bc, 0, st) for (_, bc, _, st) in rows]
    out_specs += [pl.BlockSpec(q.shape, lambda i, h: (0, 0)) for q in pars]
    out_shape = [jax.ShapeDtypeStruct((t, bc * (heads if st else 1)), dt) for (_, bc, _, st), dt in zip(rows, row_dtypes)]
    out_shape += [jax.ShapeDtypeStruct(q.shape, F32) for q in pars]
    res = pl.pallas_call(
        body, name=name, grid=(t // tm, heads), in_specs=in_specs, out_specs=out_specs, out_shape=out_shape,
        compiler_params=_params(("arbitrary", "arbitrary")),
    )(*[r[0] for r in rows], *pars, *[c[0] for c in cts], *[a[0] for a in add_list])
    return list(res[:nr]), list(res[nr:])


def _rms(x, g):
    return x * lax.rsqrt(jnp.mean(x * x, axis=-1, keepdims=True) + EPS) * g


def _rms_pair(x, g):
    left = lax.broadcasted_iota(jnp.int32, x.shape, 1) < HEAD_DIM
    x2 = x * x
    ms_a = jnp.sum(jnp.where(left, x2, 0.0), axis=-1, keepdims=True) * (1.0 / HEAD_DIM)
    ms_b = jnp.sum(jnp.where(left, 0.0, x2), axis=-1, keepdims=True) * (1.0 / HEAD_DIM)
    return x * lax.rsqrt(jnp.where(left, ms_a, ms_b) + EPS) * g


def _gelu(x):
    return 0.5 * x * (1.0 + jnp.tanh(math.sqrt(2.0 / math.pi) * (x + 0.044715 * (x * x * x))))


def _s5_act(ys, u, d):
    return _gelu(ys + d * u)


def _s5_gate(yg, z, b, g):
    return _rms(yg * jax.nn.sigmoid(z + b), g)


def _lane_cumsum(x, reverse):
    n = x.shape[-1]
    lane = lax.broadcasted_iota(jnp.int32, x.shape, 1)
    k = 1
    while k < n:
        if reverse:
            x = x + jnp.where(lane < n - k, pltpu.roll(x, n - k, 1), 0.0)
        else:
            x = x + jnp.where(lane >= k, pltpu.roll(x, k, 1), 0.0)
        k *= 2
    return x


def _log_sigmoid(z):
    return jnp.minimum(z, 0.0) - jnp.log(1.0 + jnp.exp(-jnp.abs(z)))


def _forget_fwd(f, bias):
    def body(f_ref, b_ref, c_ref):
        c_ref[...] = _lane_cumsum(_log_sigmoid(f_ref[...] + b_ref[...]), False)

    return pl.pallas_call(body, name="forget_fwd", out_shape=jax.ShapeDtypeStruct(f.shape, F32),
                          compiler_params=_params())(f, bias)


def _forget_bwd(f, bias, dc):
    def body(f_ref, b_ref, dc_ref, df_ref, db_ref):
        dlog = _lane_cumsum(dc_ref[...], True)
        df = dlog * jax.nn.sigmoid(-(f_ref[...] + b_ref[...]))
        df_ref[...] = df
        db_ref[...] = jnp.sum(df, axis=1, keepdims=True)

    return pl.pallas_call(body, name="forget_bwd",
                          out_shape=(jax.ShapeDtypeStruct(f.shape, F32), jax.ShapeDtypeStruct(bias.shape, F32)),
                          compiler_params=_params())(f, bias, dc)


FOX_BLOCK = 256
_NT = _DIMS["nt"]
_TN = _DIMS["tn"]


N_PAIRS = N_FOX_HEADS // 2
V_BLOCK0 = 2 * N_PAIRS


def _left_lanes(shape):
    return lax.broadcasted_iota(jnp.int32, shape, 1) < HEAD_DIM


def _top_rows(shape):
    return lax.broadcasted_iota(jnp.int32, shape, 0) < HEAD_DIM


def _wide(c_tile, n):
    return c_tile if n == 128 else jnp.concatenate([c_tile] * (n // 128), axis=1)


def _fox_fwd(qn, kn, qkv, c_wide, seqs):
    t = qn.shape[0]
    l = t // seqs
    tb = min(FOX_BLOCK, l)
    nb = l // tb
    scale = HEAD_DIM ** -0.5

    def body(q_ref, k_ref, v_ref, ca_ref, cb_ref, o_ref, lse_ref, vt_ref):
        i = pl.program_id(2)
        top = _top_rows((128, tb))

        @pl.when(i == 0)
        def _():
            vt_ref[...] = v_ref[...].T.astype(BF16)

        qt = (q_ref[...].astype(F32) * scale).T.astype(BF16)
        zero = jnp.zeros_like(qt)
        qts = (jnp.where(top, qt, zero), jnp.where(top, zero, qt))
        causal = lax.broadcasted_iota(jnp.int32, (tb, tb), 0) <= lax.broadcasted_iota(jnp.int32, (tb, tb), 1)
        c_refs = (ca_ref, cb_ref)

        def tile(j, carry, masked):
            off = pl.multiple_of(j * tb, tb)
            k2 = k_ref[pl.ds(off, tb), :]
            vt = vt_ref[:, pl.ds(off, tb)]
            vts = (jnp.where(top, vt, zero), jnp.where(top, zero, vt))
            (ma, sa), (mb, sb), acc = carry
            new, alphas, pv = [], [], []
            for h, (m, s_sum) in enumerate(((ma, sa), (mb, sb))):
                st = jnp.dot(k2, qts[h], preferred_element_type=F32) - _wide(c_refs[h][pl.ds(off, tb), :], tb)
                if masked:
                    st = jnp.where(causal, st, -jnp.inf)
                m_new = jnp.maximum(m, jnp.max(st, axis=0, keepdims=True))
                alpha = jnp.exp(m - m_new)
                p = jnp.exp(st - m_new)
                new.append((m_new, alpha * s_sum + jnp.sum(p, axis=0, keepdims=True)))
                alphas.append(alpha)
                pv.append(jnp.dot(vts[h], p.astype(BF16), preferred_element_type=F32))
            acc = jnp.where(top, alphas[0], alphas[1]) * acc + pv[0] + pv[1]
            return new[0], new[1], acc

        stat = (jnp.full((1, tb), -jnp.inf, F32), jnp.zeros((1, tb), F32))
        carry = lax.fori_loop(0, i, lambda j, c: tile(j, c, False), (stat, stat, jnp.zeros((128, tb), F32)))
        (ma, sa), (mb, sb), acc = tile(i, carry, True)
        o_ref[...] = (acc / jnp.where(top, sa, sb)).T
        lse_ref[0:1, :] = ma + jnp.log(sa)
        lse_ref[1:2, :] = mb + jnp.log(sb)

    qblk = pl.BlockSpec((tb, 128), lambda b, hp, i: (b * nb + i, hp))
    return pl.pallas_call(
        body, name="fox_fwd", grid=(seqs, N_PAIRS, nb),
        in_specs=[qblk, pl.BlockSpec((l, 128), lambda b, hp, i: (b, hp)),
                  pl.BlockSpec((l, 128), lambda b, hp, i: (b, V_BLOCK0 + hp)),
                  pl.BlockSpec((None, l, 128), lambda b, hp, i: (b * N_FOX_HEADS + 2 * hp, 0, 0)),
                  pl.BlockSpec((None, l, 128), lambda b, hp, i: (b * N_FOX_HEADS + 2 * hp + 1, 0, 0))],
        out_specs=[qblk, pl.BlockSpec((None, 2, tb), lambda b, hp, i: (b * N_PAIRS + hp, 0, i))],
        out_shape=[jax.ShapeDtypeStruct((t, FOX_WIDTH), F32), jax.ShapeDtypeStruct((seqs * N_PAIRS, 2, l), F32)],
        scratch_shapes=[pltpu.VMEM((128, l), BF16)],
        compiler_params=_params(("parallel", "parallel", "arbitrary")),
    )(qn, kn, qkv, c_wide, c_wide)


def _fox_bwd(qn, kn, qkv, c_wide, o, do, lse, seqs):
    t = qn.shape[0]
    l = t // seqs
    tb = min(FOX_BLOCK, l)
    nb = l // tb
    scale = HEAD_DIM ** -0.5

    def body(q_ref, k_ref, v_ref, ca_ref, cb_ref, o_ref, do_ref, lse_ref, dq_ref, dk_ref, dv_ref, dc_ref, dcq_ref,
             qt_ref, kt_ref, dot_ref, delta_ref, dqt_ref):
        top_l = _top_rows((128, l))
        top = _top_rows((128, tb))
        left = _left_lanes((tb, 128))
        zero_t = jnp.zeros((128, tb), BF16)
        zero_l = jnp.zeros((tb, 128), BF16)
        rows = lambda a: (jnp.where(top, a, zero_t), jnp.where(top, zero_t, a))
        lanes = lambda a: (jnp.where(left, a, zero_l), jnp.where(left, zero_l, a))
        causal = lax.broadcasted_iota(jnp.int32, (tb, tb), 0) <= lax.broadcasted_iota(jnp.int32, (tb, tb), 1)
        ones_q = jnp.ones((tb, 128), BF16)
        ones_k = jnp.ones((8, tb), BF16)
        c_refs = (ca_ref, cb_ref)

        qt_ref[...] = (q_ref[...].astype(F32) * scale).T.astype(BF16)
        kt_ref[...] = k_ref[...].astype(F32).T.astype(BF16)
        do_t = do_ref[...].T
        dot_ref[...] = do_t.astype(BF16)
        prod_t = do_t * o_ref[...].T
        delta_ref[0:1, :] = jnp.sum(jnp.where(top_l, prod_t, 0.0), axis=0, keepdims=True)
        delta_ref[1:2, :] = jnp.sum(jnp.where(top_l, 0.0, prod_t), axis=0, keepdims=True)
        dqt_ref[...] = jnp.zeros(dqt_ref.shape, F32)
        dcq_ref[...] = jnp.zeros(dcq_ref.shape, F32)

        def kv_block(j, _):
            koff = pl.multiple_of(j * tb, tb)
            k2 = k_ref[pl.ds(koff, tb), :]
            v2 = v_ref[pl.ds(koff, tb), :].astype(BF16)
            kts = rows(kt_ref[:, pl.ds(koff, tb)])
            cw = tuple(_wide(c_refs[h][pl.ds(koff, tb), :], tb) for h in (0, 1))

            def q_block(i, carry, masked):
                dk, dv, dca, dcb = carry
                qoff = pl.multiple_of(i * tb, tb)
                qs = lanes((q_ref[pl.ds(qoff, tb), :].astype(F32) * scale).astype(BF16))
                dos = lanes(do_ref[pl.ds(qoff, tb), :].astype(BF16))
                qts = rows(qt_ref[:, pl.ds(qoff, tb)])
                dots = rows(dot_ref[:, pl.ds(qoff, tb)])
                dq_t, dcs = 0.0, []
                for h in (0, 1):
                    st = jnp.dot(k2, qts[h], preferred_element_type=F32) - cw[h]
                    p = jnp.exp(st - lse_ref[h:h + 1, pl.ds(qoff, tb)])
                    if masked:
                        p = jnp.where(causal, p, 0.0)
                    dp = jnp.dot(v2, dots[h], preferred_element_type=F32)
                    dsb = (p * (dp - delta_ref[h:h + 1, pl.ds(qoff, tb)])).astype(BF16)
                    dv = dv + jnp.dot(p.astype(BF16), dos[h], preferred_element_type=F32)
                    dk = dk + jnp.dot(dsb, qs[h], preferred_element_type=F32)
                    dq_t = dq_t + jnp.dot(kts[h], dsb, preferred_element_type=F32)
                    dcs.append(jnp.dot(dsb, ones_q, preferred_element_type=F32))
                    dcq_ref[h:h + 1, pl.ds(qoff, tb)] += jnp.dot(ones_k, dsb, preferred_element_type=F32)[0:1, :]
                dqt_ref[:, pl.ds(qoff, tb)] += dq_t
                return dk, dv, dca - dcs[0], dcb - dcs[1]

            z = jnp.zeros((tb, 128), F32)
            carry = q_block(j, (z, z, z, z), True)
            dk, dv, dca, dcb = lax.fori_loop(j + 1, nb, lambda i, c: q_block(i, c, False), carry)
            dk_ref[pl.ds(koff, tb), :] = dk
            dv_ref[pl.ds(koff, tb), :] = dv
            dc_ref[pl.ds(koff, tb), 0:128] = dca
            dc_ref[pl.ds(koff, tb), 128:256] = dcb
            return 0

        lax.fori_loop(0, nb, kv_block, 0)
        dq_ref[...] = (dqt_ref[...] * scale).T

    blk = pl.BlockSpec((l, 128), lambda b, hp: (b, hp))
    cspec = lambda k: pl.BlockSpec((None, l, 128), lambda b, hp: (b * N_FOX_HEADS + 2 * hp + k, 0, 0))
    rows2 = pl.BlockSpec((None, 2, l), lambda b, hp: (b * N_PAIRS + hp, 0, 0))
    wide = jax.ShapeDtypeStruct((t, FOX_WIDTH), F32)
    return pl.pallas_call(
        body, name="fox_bwd", grid=(seqs, N_PAIRS),
        in_specs=[blk, blk, pl.BlockSpec((l, 128), lambda b, hp: (b, V_BLOCK0 + hp)), cspec(0), cspec(1), blk, blk, rows2],
        out_specs=[blk, blk, blk, pl.BlockSpec((None, l, 256), lambda b, hp: (b * N_PAIRS + hp, 0, 0)), rows2],
        out_shape=[wide, wide, wide, jax.ShapeDtypeStruct((seqs * N_PAIRS, l, 256), F32),
                   jax.ShapeDtypeStruct((seqs * N_PAIRS, 2, l), F32)],
        scratch_shapes=[pltpu.VMEM((128, l), BF16), pltpu.VMEM((128, l), BF16), pltpu.VMEM((128, l), BF16),
                        pltpu.VMEM((2, l), F32), pltpu.VMEM((128, l), F32)],
        compiler_params=_params(("parallel", "parallel")),
    )(qn, kn, qkv, c_wide, c_wide, o, do, lse)


SCAN_ROWS = 256
SCAN_COLS = 1024


S5_IN = 128
S5_ST = 512
SCAN_CHUNKS = SCAN_COLS // S5_ST


def _s5_fwd(uf, bbr, bbi, cr, ci, ar, ai, seqs):
    t = uf.shape[0]
    l = t // seqs
    tl = min(SCAN_ROWS, l)
    nl = l // tl
    cb, nq = SCAN_COLS, SCAN_CHUNKS

    def body(u_ref, bbr_ref, bbi_ref, cr_ref, ci_ref, ar_ref, ai_ref, xr_ref, xi_ref, ys_ref, car_r, car_i, bu_r, bu_i):
        @pl.when(pl.program_id(2) == 0)
        def _():
            car_r[...] = jnp.zeros(car_r.shape, F32)
            car_i[...] = jnp.zeros(car_i.shape, F32)

        u = u_ref[...].astype(BF16)
        for q in range(nq):
            uq = u[:, q * S5_IN:(q + 1) * S5_IN]
            bu_r[:, q * S5_ST:(q + 1) * S5_ST] = jnp.dot(uq, bbr_ref[q], preferred_element_type=F32)
            bu_i[:, q * S5_ST:(q + 1) * S5_ST] = jnp.dot(uq, bbi_ref[q], preferred_element_type=F32)
        a_r, a_i = ar_ref[...], ai_ref[...]

        def step(tt, carry):
            xr, xi = carry
            nr = a_r * xr - a_i * xi + bu_r[pl.ds(tt, 1), :]
            ni = a_r * xi + a_i * xr + bu_i[pl.ds(tt, 1), :]
            xr_ref[pl.ds(tt, 1), :] = nr
            xi_ref[pl.ds(tt, 1), :] = ni
            return nr, ni

        xr, xi = lax.fori_loop(0, tl, step, (car_r[...], car_i[...]), unroll=8)
        car_r[...] = xr
        car_i[...] = xi
        for q in range(nq):
            xq_r = xr_ref[:, q * S5_ST:(q + 1) * S5_ST].astype(BF16)
            xq_i = xi_ref[:, q * S5_ST:(q + 1) * S5_ST].astype(BF16)
            ys_ref[:, q * S5_IN:(q + 1) * S5_IN] = (jnp.dot(xq_r, cr_ref[q], preferred_element_type=F32)
                                                    + jnp.dot(xq_i, ci_ref[q], preferred_element_type=F32))

    rows = lambda w: pl.BlockSpec((tl, w), lambda s, j, r: (s * nl + r, j))
    chunk = lambda a: pl.BlockSpec((nq,) + a.shape[1:], lambda s, j, r: (j, 0, 0))
    par = pl.BlockSpec((1, cb), lambda s, j, r: (0, j))
    return pl.pallas_call(
        body, name="s5_fwd", grid=(seqs, S5_CH // cb, nl),
        in_specs=[rows(nq * S5_IN), chunk(bbr), chunk(bbi), chunk(cr), chunk(ci), par, par],
        out_specs=[rows(cb), rows(cb), rows(nq * S5_IN)],
        out_shape=[jax.ShapeDtypeStruct((t, S5_CH), F32)] * 2 + [jax.ShapeDtypeStruct((t, S5_WIDTH), F32)],
        scratch_shapes=[pltpu.VMEM((1, cb), F32), pltpu.VMEM((1, cb), F32), pltpu.VMEM((tl, cb), F32),
                        pltpu.VMEM((tl, cb), F32)],
        compiler_params=_params(("parallel", "parallel", "arbitrary")),
    )(uf, bbr, bbi, cr, ci, ar, ai)


def _s5_bwd(dys, uf, xr, xi, bbr, bbi, cr, ci, ar, ai, seqs):
    t = dys.shape[0]
    l = t // seqs
    tl = min(SCAN_ROWS, l)
    nl = l // tl
    cb, nq = SCAN_COLS, SCAN_CHUNKS

    def body(dy_ref, u_ref, xr_ref, xi_ref, bbr_ref, bbi_ref, cr_ref, ci_ref, ar_ref, ai_ref,
             du_ref, dbbr_ref, dbbi_ref, dcr_ref, dci_ref, dar_ref, dai_ref, car_r, car_i, lam_r, lam_i):
        @pl.when(pl.program_id(2) == 0)
        def _():
            car_r[...] = jnp.zeros(car_r.shape, F32)
            car_i[...] = jnp.zeros(car_i.shape, F32)
            for acc_ref in (dbbr_ref, dbbi_ref, dcr_ref, dci_ref, dar_ref, dai_ref):
                acc_ref[...] = jnp.zeros(acc_ref.shape, F32)

        dy = dy_ref[...]
        for q in range(nq):
            dyq = dy[:, q * S5_IN:(q + 1) * S5_IN]
            lam_r[:, q * S5_ST:(q + 1) * S5_ST] = lax.dot_general(dyq, cr_ref[q], _NT, preferred_element_type=F32)
            lam_i[:, q * S5_ST:(q + 1) * S5_ST] = lax.dot_general(dyq, ci_ref[q], _NT, preferred_element_type=F32)
        a_r, a_i = ar_ref[...], ai_ref[...]

        def step(k, carry):
            lr, li, dar, dai = carry
            tt = tl - 1 - k
            xr_t = xr_ref[pl.ds(tt, 1), :]
            xi_t = xi_ref[pl.ds(tt, 1), :]
            dar = dar + lr * xr_t + li * xi_t
            dai = dai + li * xr_t - lr * xi_t
            nr = lam_r[pl.ds(tt, 1), :] + a_r * lr + a_i * li
            ni = lam_i[pl.ds(tt, 1), :] + a_r * li - a_i * lr
            lam_r[pl.ds(tt, 1), :] = nr
            lam_i[pl.ds(tt, 1), :] = ni
            return nr, ni, dar, dai

        lr, li, dar, dai = lax.fori_loop(
            0, tl, step, (car_r[...], car_i[...], jnp.zeros((1, cb), F32), jnp.zeros((1, cb), F32)), unroll=8)
        car_r[...] = lr
        car_i[...] = li
        dar_ref[...] += dar
        dai_ref[...] += dai
        u = u_ref[...].astype(BF16)
        for q in range(nq):
            st = slice(q * S5_ST, (q + 1) * S5_ST)
            io = slice(q * S5_IN, (q + 1) * S5_IN)
            lq_r, lq_i = lam_r[:, st].astype(BF16), lam_i[:, st].astype(BF16)
            du_ref[:, io] = (lax.dot_general(lq_r, bbr_ref[q], _NT, preferred_element_type=F32)
                             + lax.dot_general(lq_i, bbi_ref[q], _NT, preferred_element_type=F32))
            dbbr_ref[q] += lax.dot_general(u[:, io], lq_r, _TN, preferred_element_type=F32)
            dbbi_ref[q] += lax.dot_general(u[:, io], lq_i, _TN, preferred_element_type=F32)
            dcr_ref[q] += lax.dot_general(xr_ref[:, st].astype(BF16), dy[:, io], _TN, preferred_element_type=F32)
            dci_ref[q] += lax.dot_general(xi_ref[:, st].astype(BF16), dy[:, io], _TN, preferred_element_type=F32)

    rows = lambda w: pl.BlockSpec((tl, w), lambda s, j, r: (s * nl + nl - 1 - r, j))
    chunk = lambda a: pl.BlockSpec((nq,) + a.shape[1:], lambda s, j, r: (j, 0, 0))
    acc = lambda a: pl.BlockSpec((None, nq) + a.shape[1:], lambda s, j, r: (s, j, 0, 0))
    par = pl.BlockSpec((1, cb), lambda s, j, r: (0, j))
    par_acc = pl.BlockSpec((None, 1, cb), lambda s, j, r: (s, 0, j))
    per_seq = lambda a: jax.ShapeDtypeStruct((seqs,) + a.shape, F32)
    return pl.pallas_call(
        body, name="s5_bwd", grid=(seqs, S5_CH // cb, nl),
        in_specs=[rows(nq * S5_IN), rows(nq * S5_IN), rows(cb), rows(cb), chunk(bbr), chunk(bbi), chunk(cr), chunk(ci),
                  par, par],
        out_specs=[rows(nq * S5_IN), acc(bbr), acc(bbi), acc(cr), acc(ci), par_acc, par_acc],
        out_shape=[jax.ShapeDtypeStruct((t, S5_WIDTH), F32), per_seq(bbr), per_seq(bbi), per_seq(cr), per_seq(ci),
                   jax.ShapeDtypeStruct((seqs, 1, S5_CH), F32), jax.ShapeDtypeStruct((seqs, 1, S5_CH), F32)],
        scratch_shapes=[pltpu.VMEM((1, cb), F32), pltpu.VMEM((1, cb), F32), pltpu.VMEM((tl, cb), F32),
                        pltpu.VMEM((tl, cb), F32)],
        compiler_params=_params(("parallel", "parallel", "arbitrary")),
    )(dys, uf, xr, xi, bbr, bbi, cr, ci, ar, ai)


XATT_BLOCK = 512


def _xatt_probs(qv, kv):
    s = lax.dot_general(qv, kv, _NT, preferred_element_type=F32) * (X_HEAD_DIM ** -0.5)
    e = jnp.exp(s - jnp.max(s, axis=-1, keepdims=True))
    return e / jnp.sum(e, axis=-1, keepdims=True)


def _xatt_fwd(q, k, kv, seqs):
    t = q.shape[0]
    tq = min(XATT_BLOCK, t // seqs)
    nq = t // seqs // tq

    def body(q_ref, k_ref, v_ref, o_ref):
        p = _xatt_probs(q_ref[...], k_ref[...])
        o_ref[...] = jnp.dot(p.astype(BF16), v_ref[...].astype(BF16), preferred_element_type=F32).astype(o_ref.dtype)

    qs = pl.BlockSpec((tq, X_HEAD_DIM), lambda b, h, i: (b * nq + i, h))
    return pl.pallas_call(
        body, name="xatt_fwd", grid=(seqs, N_X_HEADS, nq),
        in_specs=[qs, pl.BlockSpec((N_MEM, X_HEAD_DIM), lambda b, h, i: (b, h)),
                  pl.BlockSpec((N_MEM, X_HEAD_DIM), lambda b, h, i: (b, N_X_HEADS + h))],
        out_specs=qs, out_shape=jax.ShapeDtypeStruct(q.shape, BF16),
        compiler_params=_params(("parallel", "parallel", "parallel")),
    )(q, k, kv)


def _xatt_bwd(q, k, kv, do, seqs):
    t = q.shape[0]
    tq = min(XATT_BLOCK, t // seqs)
    nq = t // seqs // tq
    scale = X_HEAD_DIM ** -0.5

    def body(q_ref, k_ref, v_ref, do_ref, dq_ref, dk_ref, dv_ref):
        @pl.when(pl.program_id(2) == 0)
        def _():
            dk_ref[...] = jnp.zeros(dk_ref.shape, F32)
            dv_ref[...] = jnp.zeros(dv_ref.shape, F32)

        qv, kk = q_ref[...], k_ref[...]
        p = _xatt_probs(qv, kk)
        dob = do_ref[...].astype(BF16)
        dp = lax.dot_general(dob, v_ref[...].astype(BF16), _NT, preferred_element_type=F32)
        ds = p * (dp - jnp.sum(dp * p, axis=-1, keepdims=True))
        dsb = ds.astype(BF16)
        dq_ref[...] = jnp.dot(dsb, kk, preferred_element_type=F32) * scale
        dk_ref[...] += lax.dot_general(dsb, qv, _TN, preferred_element_type=F32) * scale
        dv_ref[...] += lax.dot_general(p.astype(BF16), dob, _TN, preferred_element_type=F32)

    qs = pl.BlockSpec((tq, X_HEAD_DIM), lambda b, h, i: (b * nq + i, h))
    ks = pl.BlockSpec((N_MEM, X_HEAD_DIM), lambda b, h, i: (b, h))
    return pl.pallas_call(
        body, name="xatt_bwd", grid=(seqs, N_X_HEADS, nq),
        in_specs=[qs, ks, pl.BlockSpec((N_MEM, X_HEAD_DIM), lambda b, h, i: (b, N_X_HEADS + h)), qs],
        out_specs=[qs, ks, ks],
        out_shape=[jax.ShapeDtypeStruct(q.shape, F32), jax.ShapeDtypeStruct(k.shape, F32),
                   jax.ShapeDtypeStruct(k.shape, F32)],
        compiler_params=_params(("parallel", "parallel", "arbitrary")),
    )(q, k, kv, do)


CONV_COLS = 256


def _shift_down(x, k, row):
    return jnp.where(row >= k, pltpu.roll(x, k, 0), 0.0)


def _shift_up(x, k, row):
    n = x.shape[0]
    return jnp.where(row < n - k, pltpu.roll(x, n - k, 0), 0.0)


def _conv_pre(g, w, b, row):
    return b + w[0:1, :] * _shift_down(g, 2, row) + w[1:2, :] * _shift_down(g, 1, row) + w[2:3, :] * g


def _convgate_fwd(gu, w, b, seqs):
    t = gu.shape[0]
    l = t // seqs
    nc = D_FF // CONV_COLS

    def body(g_ref, u_ref, w_ref, b_ref, o_ref):
        g = g_ref[...].astype(F32)
        row = lax.broadcasted_iota(jnp.int32, g.shape, 0)
        pre = _conv_pre(g, w_ref[...], b_ref[...], row)
        o_ref[...] = (pre * jax.nn.sigmoid(pre) * u_ref[...].astype(F32)).astype(o_ref.dtype)

    return pl.pallas_call(
        body, name="convgate_fwd", grid=(seqs, nc),
        in_specs=[pl.BlockSpec((l, CONV_COLS), lambda s, j: (s, j)), pl.BlockSpec((l, CONV_COLS), lambda s, j: (s, nc + j)),
                  pl.BlockSpec((3, CONV_COLS), lambda s, j: (0, j)), pl.BlockSpec((1, CONV_COLS), lambda s, j: (0, j))],
        out_specs=pl.BlockSpec((l, CONV_COLS), lambda s, j: (s, j)),
        out_shape=jax.ShapeDtypeStruct((t, D_FF), BF16),
        compiler_params=_params(("parallel", "parallel")),
    )(gu, gu, w, b)


def _convgate_bwd(gu, w, b, dact, seqs):
    t = gu.shape[0]
    l = t // seqs
    nc = D_FF // CONV_COLS

    def body(g_ref, u_ref, w_ref, b_ref, da_ref, dg_ref, du_ref, dw_ref, db_ref):
        @pl.when(pl.program_id(1) == 0)
        def _():
            dw_ref[...] = jnp.zeros(dw_ref.shape, F32)
            db_ref[...] = jnp.zeros(db_ref.shape, F32)

        g, wv, da = g_ref[...].astype(F32), w_ref[...], da_ref[...].astype(F32)
        row = lax.broadcasted_iota(jnp.int32, g.shape, 0)
        g1, g2 = _shift_down(g, 1, row), _shift_down(g, 2, row)
        pre = b_ref[...] + wv[0:1, :] * g2 + wv[1:2, :] * g1 + wv[2:3, :] * g
        sg = jax.nn.sigmoid(pre)
        silu = pre * sg
        du_ref[...] = (da * silu).astype(du_ref.dtype)
        dpre = da * u_ref[...].astype(F32) * (sg * (1.0 + pre * (1.0 - sg)))
        dg = wv[2:3, :] * dpre + wv[1:2, :] * _shift_up(dpre, 1, row) + wv[0:1, :] * _shift_up(dpre, 2, row)
        dg_ref[...] = dg.astype(dg_ref.dtype)
        dw_ref[0:1, :] += jnp.sum(dpre * g2, axis=0, keepdims=True)
        dw_ref[1:2, :] += jnp.sum(dpre * g1, axis=0, keepdims=True)
        dw_ref[2:3, :] += jnp.sum(dpre * g, axis=0, keepdims=True)
        db_ref[...] += jnp.sum(dpre, axis=0, keepdims=True)

    blk = lambda off: pl.BlockSpec((l, CONV_COLS), lambda j, s: (s, off + j))
    return pl.pallas_call(
        body, name="convgate_bwd", grid=(nc, seqs),
        in_specs=[blk(0), blk(nc), pl.BlockSpec((3, CONV_COLS), lambda j, s: (0, j)),
                  pl.BlockSpec((1, CONV_COLS), lambda j, s: (0, j)), blk(0)],
        out_specs=[blk(0), blk(0), pl.BlockSpec((3, CONV_COLS), lambda j, s: (0, j)),
                   pl.BlockSpec((1, CONV_COLS), lambda j, s: (0, j))],
        out_shape=[jax.ShapeDtypeStruct((t, D_FF), BF16), jax.ShapeDtypeStruct((t, D_FF), BF16),
                   jax.ShapeDtypeStruct((3, D_FF), F32), jax.ShapeDtypeStruct((1, D_FF), F32)],
        compiler_params=_params(("parallel", "arbitrary")),
    )(gu, gu, w, b, dact)


def _loss_head(h, target):
    t, d = h.shape
    tm = _pick(t, (256, 128, 8))

    def body(h_ref, t_ref, dh_ref, dhb_ref, loss_ref):
        @pl.when(pl.program_id(0) == 0)
        def _():
            loss_ref[...] = jnp.zeros(loss_ref.shape, F32)

        e = h_ref[...] - t_ref[...]
        dh = e * (1.0 / d)
        dh_ref[...] = dh
        dhb_ref[...] = dh.astype(BF16)
        loss_ref[...] += (0.5 / d) * jnp.sum(jnp.sum(e * e, axis=1, keepdims=True), axis=0, keepdims=True)

    blk = pl.BlockSpec((tm, d), lambda i: (i, 0))
    return pl.pallas_call(
        body, name="loss_head", grid=(t // tm,), in_specs=[blk, blk],
        out_specs=[blk, blk, pl.BlockSpec((1, 1), lambda i: (0, 0))],
        out_shape=[jax.ShapeDtypeStruct((t, d), F32), jax.ShapeDtypeStruct((t, d), BF16),
                   jax.ShapeDtypeStruct((1, 1), F32)],
        compiler_params=_params(("arbitrary",)),
    )(h, target)


def _s5_discretise(a_re, a_im, log_dt, b_re, b_im):
    dt = jnp.exp(log_dt)[:, None]
    mag = jnp.exp(a_re * dt)
    lb_r = mag * jnp.cos(a_im * dt)
    lb_i = mag * jnp.sin(a_im * dt)
    den = a_re * a_re + a_im * a_im
    nr = lb_r - 1.0
    coef_r = (nr * a_re + lb_i * a_im) / den
    coef_i = (lb_i * a_re - nr * a_im) / den
    bb_r = coef_r[:, :, None] * b_re - coef_i[:, :, None] * b_im
    bb_i = coef_r[:, :, None] * b_im + coef_i[:, :, None] * b_re
    return lb_r, lb_i, bb_r, bb_i


S5_CHUNKS = 4
S5_PER = S5_GROUPS // S5_CHUNKS


def _blockdiag_in(bb):
    eye = jnp.eye(S5_PER, dtype=bb.dtype)
    return jnp.einsum("jgpc,gh->jgchp", bb.reshape(S5_CHUNKS, S5_PER, S5_STATE, S5_GROUP_CH), eye).reshape(
        S5_CHUNKS, S5_PER * S5_GROUP_CH, S5_PER * S5_STATE)


def _blockdiag_in_grad(d):
    eye = jnp.eye(S5_PER, dtype=d.dtype)
    return jnp.einsum("jgchp,gh->jgpc", d.reshape(S5_CHUNKS, S5_PER, S5_GROUP_CH, S5_PER, S5_STATE), eye).reshape(
        S5_GROUPS, S5_STATE, S5_GROUP_CH)


def _blockdiag_out(c):
    eye = jnp.eye(S5_PER, dtype=c.dtype)
    return jnp.einsum("jgcp,gh->jgphc", c.reshape(S5_CHUNKS, S5_PER, S5_GROUP_CH, S5_STATE), eye).reshape(
        S5_CHUNKS, S5_PER * S5_STATE, S5_PER * S5_GROUP_CH)


def _blockdiag_out_grad(d):
    eye = jnp.eye(S5_PER, dtype=d.dtype)
    return jnp.einsum("jgphc,gh->jgcp", d.reshape(S5_CHUNKS, S5_PER, S5_STATE, S5_PER, S5_GROUP_CH), eye).reshape(
        S5_GROUPS, S5_GROUP_CH, S5_STATE)


def _local_step(x3, mem3, target3, p, wb, late_weights=None, early_grads=None):
    seqs, l, d = x3.shape
    t = seqs * l
    x = x3.reshape(t, d)
    mem = mem3.reshape(seqs * N_MEM, d)
    target = target3.reshape(t, d)
    full = lambda a: (a, a.shape[1], 0, 0)

    s5_in = (p["s5_a_re"], p["s5_a_im"], p["s5_log_dt"], p["s5_b_re"], p["s5_b_im"])
    (lb_r, lb_i, bb_r, bb_i), s5_pull = jax.vjp(_s5_discretise, *s5_in)
    ar, ai = lb_r.reshape(1, S5_CH), lb_i.reshape(1, S5_CH)
    bbr_d, bbi_d = _blockdiag_in(bb_r).astype(BF16), _blockdiag_in(bb_i).astype(BF16)
    cr_d, ci_d = _blockdiag_out(p["s5_c_re"]).astype(BF16), (-_blockdiag_out(p["s5_c_im"])).astype(BF16)
    d_row = p["s5_d"].reshape(1, S5_WIDTH)

    w_in = wb["w_in"]
    w_qkv = w_in[:, :3 * FOX_WIDTH]
    w_uf = jnp.concatenate(
        [w_in[:, 3 * FOX_WIDTH + N_FOX_HEADS:], w_in[:, 3 * FOX_WIDTH:3 * FOX_WIDTH + N_FOX_HEADS],
         jnp.zeros((d, UF_COLS - S5_WIDTH - N_FOX_HEADS), w_in.dtype)], axis=1)

    hn1 = _rowwise(_rms, [full(x)], [p["norm_mix"]], [(d, d, 0, BF16)], "norm_mix_fwd")
    qkv = _mm(hn1, w_qkv, "nn", "in_qkv")
    uf = _mm(hn1, w_uf, "nn", "in_uf")

    bh = seqs * N_FOX_HEADS
    q_pair = (qkv, 128, 0, 1)
    k_pair = (qkv, 128, N_PAIRS, 1)
    gq2, gk2 = jnp.tile(p["fox_q_norm"], (1, 2)), jnp.tile(p["fox_k_norm"], (1, 2))
    pair_out = [(FOX_WIDTH, 128, 1, BF16)]
    qn = _rowwise(_rms_pair, [q_pair], [gq2], pair_out, "fox_qnorm_fwd", heads=N_PAIRS)
    kn = _rowwise(_rms_pair, [k_pair], [gk2], pair_out, "fox_knorm_fwd", heads=N_PAIRS)

    f_rows = uf[:, S5_WIDTH:S5_WIDTH + N_FOX_HEADS].reshape(seqs, l, N_FOX_HEADS).transpose(0, 2, 1).reshape(bh, l)
    f_bias = jnp.tile(p["fox_f_bias"].reshape(N_FOX_HEADS, 1), (seqs, 1))
    c_wide = jnp.broadcast_to(_forget_fwd(f_rows, f_bias)[:, :, None], (bh, l, 128))
    fox, lse = _fox_fwd(qn, kn, qkv, c_wide, seqs)

    xr, xi, ys = _s5_fwd(uf, bbr_d, bbi_d, cr_d, ci_d, ar, ai, seqs)
    u_blk = (uf, S5_WIDTH, 0, 0)
    yg = _rowwise(_s5_act, [full(ys), u_blk], [d_row], [(S5_WIDTH, S5_WIDTH, 0, F32)], "s5_act_fwd")
    z = _mm(yg, wb["s5_w_glu"], "nn", "s5_glu")
    y2n = _rowwise(_s5_gate, [full(yg), full(z)], [p["s5_b_glu"], p["out_norm_s5"]],
                   [(S5_WIDTH, S5_WIDTH, 0, BF16)], "s5_gate_fwd")
    foxn = _rowwise(_rms, [full(fox)], [p["out_norm_fox"]], [(FOX_WIDTH, FOX_WIDTH, 0, BF16)], "fox_outnorm_fwd")
    mixed = jnp.concatenate([foxn, y2n], axis=1)
    h1 = _mm(mixed, wb["w_out"], "nn", "mix_out", res=x)
    if late_weights is not None:
        wb = dict(wb, **late_weights(h1))

    hn2 = _rowwise(_rms, [full(h1)], [p["norm_cross"]], [(d, d, 0, BF16)], "norm_cross_fwd")
    mn = _rowwise(_rms, [full(mem)], [p["norm_mem"]], [(d, d, 0, BF16)], "norm_mem_fwd")
    xq_raw = _mm(hn2, wb["w_xq"], "nn", "x_q")
    kv = _mm(mn, wb["w_xkv"], "nn", "x_kv")
    xh = lambda a: (a, X_HEAD_DIM, 0, 1)
    xqn = _rowwise(_rms, [xh(xq_raw)], [p["xq_norm"]], [(d, X_HEAD_DIM, 1, BF16)], "x_qnorm_fwd", heads=N_X_HEADS)
    xkn = _rowwise(_rms, [xh(kv)], [p["xk_norm"]], [(d, X_HEAD_DIM, 1, BF16)], "x_knorm_fwd", heads=N_X_HEADS)
    xo = _xatt_fwd(xqn, xkn, kv, seqs)
    h2 = _mm(xo, wb["w_xo"], "nn", "x_out", res=h1)

    hn3 = _rowwise(_rms, [full(h2)], [p["norm_ffn"]], [(d, d, 0, BF16)], "norm_ffn_fwd")
    gu = _mm(hn3, wb["w_ffn_up"], "nn", "ffn_up", out_dtype=BF16)
    act = _convgate_fwd(gu, p["ffn_conv_w"], p["ffn_conv_b"], seqs)
    h3 = _mm(act, wb["w_ffn_down"], "nn", "ffn_down", res=h2)
    dh3, dh3_b, loss = _loss_head(h3, target)

    g = {}
    dact = _mm(dh3_b, wb["w_ffn_down"], "nt", "ffn_down_dx", out_dtype=BF16)
    g["w_ffn_down"] = _mm(act, dh3_b, "tn", "ffn_down_dw")
    dgate, dup, g["ffn_conv_w"], g["ffn_conv_b"] = _convgate_bwd(gu, p["ffn_conv_w"], p["ffn_conv_b"], dact, seqs)
    dgu = jnp.concatenate([dgate, dup], axis=1)
    dhn3 = _mm(dgu, wb["w_ffn_up"], "nt", "ffn_up_dx")
    g["w_ffn_up"] = _mm(hn3, dgu, "tn", "ffn_up_dw")
    (dh2,), (g["norm_ffn"],) = _rowwise_vjp(_rms, [full(h2)], [p["norm_ffn"]], [full(dhn3)], "norm_ffn_bwd",
                                            adds=[full(dh3)])

    dxo = _mm(dh2, wb["w_xo"], "nt", "x_out_dx")
    g["w_xo"] = _mm(xo, dh2, "tn", "x_out_dw")
    dxqn, dxkn, dxv = _xatt_bwd(xqn, xkn, kv, dxo, seqs)
    (dxq_raw,), (g["xq_norm"],) = _rowwise_vjp(_rms, [xh(xq_raw)], [p["xq_norm"]], [xh(dxqn)], "x_qnorm_bwd",
                                               heads=N_X_HEADS, row_dtypes=[BF16])
    (dxk_raw,), (g["xk_norm"],) = _rowwise_vjp(_rms, [xh(kv)], [p["xk_norm"]], [xh(dxkn)], "x_knorm_bwd",
                                               heads=N_X_HEADS, row_dtypes=[BF16])
    dkv = jnp.concatenate([dxk_raw, dxv.astype(BF16)], axis=1)
    dhn2 = _mm(dxq_raw, wb["w_xq"], "nt", "x_q_dx")
    g["w_xq"] = _mm(hn2, dxq_raw, "tn", "x_q_dw")
    dmn = _mm(dkv, wb["w_xkv"], "nt", "x_kv_dx")
    g["w_xkv"] = _mm(mn, dkv, "tn", "x_kv_dw")
    norm_cross = p["norm_cross"]
    if early_grads is not None:
        norm_cross = norm_cross + early_grads({n: g[n] for n in LATE_WEIGHTS})
    (dh1,), (g["norm_cross"],) = _rowwise_vjp(_rms, [full(h1)], [norm_cross], [full(dhn2)], "norm_cross_bwd",
                                              adds=[full(dh2)])
    _, (g["norm_mem"],) = _rowwise_vjp(_rms, [full(mem)], [p["norm_mem"]], [full(dmn)], "norm_mem_bwd",
                                       row_dtypes=[BF16])

    dmixed = _mm(dh1, wb["w_out"], "nt", "mix_out_dx")
    g["w_out"] = _mm(mixed, dh1, "tn", "mix_out_dw")
    (dfox,), (g["out_norm_fox"],) = _rowwise_vjp(_rms, [full(fox)], [p["out_norm_fox"]],
                                                 [(dmixed, FOX_WIDTH, 0, 0)], "fox_outnorm_bwd")
    (dyg_a, dz), (g["s5_b_glu"], g["out_norm_s5"]) = _rowwise_vjp(
        _s5_gate, [full(yg), full(z)], [p["s5_b_glu"], p["out_norm_s5"]], [(dmixed, S5_WIDTH, 1, 0)], "s5_gate_bwd",
        row_dtypes=[F32, BF16])
    dyg = _mm(dz, wb["s5_w_glu"], "nt", "s5_glu_dx", res=dyg_a)
    g["s5_w_glu"] = _mm(yg, dz, "tn", "s5_glu_dw")
    (dys, du_a), (dd_row,) = _rowwise_vjp(_s5_act, [full(ys), u_blk], [d_row], [full(dyg)], "s5_act_bwd",
                                          row_dtypes=[BF16, F32])
    g["s5_d"] = dd_row
    du_b, dbbr_d, dbbi_d, dcr_d, dci_d, dar, dai = _s5_bwd(dys, uf, xr, xi, bbr_d, bbi_d, cr_d, ci_d, ar, ai, seqs)
    dbbr_d, dbbi_d, dcr_d, dci_d = (jnp.sum(a, axis=0) for a in (dbbr_d, dbbi_d, dcr_d, dci_d))
    d_lb_r = jnp.sum(dar, axis=0).reshape(S5_GROUPS, S5_STATE)
    d_lb_i = jnp.sum(dai, axis=0).reshape(S5_GROUPS, S5_STATE)
    g["s5_a_re"], g["s5_a_im"], g["s5_log_dt"], g["s5_b_re"], g["s5_b_im"] = s5_pull(
        (d_lb_r, d_lb_i, _blockdiag_in_grad(dbbr_d), _blockdiag_in_grad(dbbi_d)))
    g["s5_c_re"] = _blockdiag_out_grad(dcr_d)
    g["s5_c_im"] = -_blockdiag_out_grad(dci_d)

    dqn, dkn, dv, dc, dcq = _fox_bwd(qn, kn, qkv, c_wide, fox, dfox, lse, seqs)
    pair = lambda a: (a, 128, 0, 1)
    (dq_raw,), (dgq2,) = _rowwise_vjp(_rms_pair, [q_pair], [gq2], [pair(dqn)], "fox_qnorm_bwd", heads=N_PAIRS,
                                      row_dtypes=[BF16])
    (dk_raw,), (dgk2,) = _rowwise_vjp(_rms_pair, [k_pair], [gk2], [pair(dkn)], "fox_knorm_bwd", heads=N_PAIRS,
                                      row_dtypes=[BF16])
    g["fox_q_norm"] = dgq2[:, :HEAD_DIM] + dgq2[:, HEAD_DIM:]
    g["fox_k_norm"] = dgk2[:, :HEAD_DIM] + dgk2[:, HEAD_DIM:]
    dc_rows = jnp.stack([dc[:, :, 0], dc[:, :, 128]], axis=1).reshape(bh, l)
    df_rows, dfb = _forget_bwd(f_rows, f_bias, dc_rows + dcq.reshape(bh, l))
    g["fox_f_bias"] = jnp.sum(dfb.reshape(seqs, N_FOX_HEADS), axis=0)
    df = df_rows.reshape(seqs, N_FOX_HEADS, l).transpose(0, 2, 1).reshape(t, N_FOX_HEADS)
    dqkv = jnp.concatenate([dq_raw, dk_raw, dv.astype(BF16)], axis=1)
    duf = jnp.concatenate([du_a + du_b, df, jnp.zeros((t, UF_COLS - S5_WIDTH - N_FOX_HEADS), F32)],
                          axis=1).astype(BF16)
    dhn1 = _mm(duf, w_uf, "nt", "in_uf_dx", res=_mm(dqkv, w_qkv, "nt", "in_qkv_dx"))
    dw_qkv = _mm(hn1, dqkv, "tn", "in_qkv_dw")
    dw_uf = _mm(hn1, duf, "tn", "in_uf_dw")
    g["w_in"] = jnp.concatenate([dw_qkv, dw_uf[:, S5_WIDTH:S5_WIDTH + N_FOX_HEADS], dw_uf[:, :S5_WIDTH]], axis=1)
    (dx,), (g["norm_mix"],) = _rowwise_vjp(_rms, [full(x)], [p["norm_mix"]], [full(dhn1)], "norm_mix_bwd",
                                           adds=[full(dh1)])
    return loss, dx.reshape(seqs, l, d), g


def _place():
    return lax.axis_index("x"), lax.axis_index("y"), lax.axis_index("c")


def _other_chips(x, y):
    return [(1 - x, y), (x, 1 - y), (1 - x, 1 - y)]


ANY = pl.BlockSpec(memory_space=pl.ANY)


def _gather_weights(shards, col_kind, taps):
    n = len(shards)

    def body(*refs):
        ins, tap_in, outs, tap_out = refs[:n], refs[n], refs[n + 1:2 * n + 1], refs[2 * n + 1]
        ici_send, ici_recv, d2d_send, d2d_recv, own_send, own_recv = refs[2 * n + 2:]
        x, y, c = _place()
        mine = 2 * x + y
        chips = _other_chips(x, y)
        sibling = (x, y, 1 - c)

        def piece(a, s, h):
            r, cs = ins[a].shape
            hr = r // 2
            if col_kind[a]:
                return outs[a].at[pl.ds(pl.multiple_of(h * hr, 16), hr), pl.ds(pl.multiple_of(s * cs, 128), cs)]
            return outs[a].at[pl.ds(pl.multiple_of(s * r + h * hr, 16), hr), :]

        def slab(a, s):
            r, cs = ins[a].shape
            if col_kind[a]:
                return outs[a].at[:, pl.ds(pl.multiple_of(s * cs, 128), cs)]
            return outs[a].at[pl.ds(pl.multiple_of(s * r, 16), r), :]

        def own_half(a, h):
            hr = ins[a].shape[0] // 2
            return ins[a].at[pl.ds(pl.multiple_of(h * hr, 16), hr), :]

        sends = []
        for a in range(n):
            cp = pltpu.make_async_remote_copy(
                src_ref=ins[a], dst_ref=slab(a, mine), send_sem=own_send.at[a], recv_sem=own_recv.at[a],
                device_id=sibling, device_id_type=MESH)
            cp.start()
            sends.append(cp)
        cp = pltpu.make_async_remote_copy(
            src_ref=tap_in, dst_ref=tap_out.at[mine], send_sem=own_send.at[n], recv_sem=own_recv.at[n],
            device_id=sibling, device_id_type=MESH)
        cp.start()
        sends.append(cp)
        for a in range(n):
            for j, (px, py) in enumerate(chips):
                cp = pltpu.make_async_remote_copy(
                    src_ref=own_half(a, c), dst_ref=piece(a, mine, c), send_sem=ici_send.at[3 * a + j],
                    recv_sem=ici_recv.at[3 * a + j], device_id=(px, py, c), device_id_type=MESH)
                cp.start()
                sends.append(cp)
        for j, (px, py) in enumerate(chips):
            cp = pltpu.make_async_remote_copy(
                src_ref=tap_in, dst_ref=tap_out.at[mine], send_sem=ici_send.at[3 * n + j],
                recv_sem=ici_recv.at[3 * n + j], device_id=(px, py, c), device_id_type=MESH)
            cp.start()
            sends.append(cp)
        for a in range(n):
            for j, (px, py) in enumerate(chips):
                got = piece(a, 2 * px + py, c)
                pltpu.make_async_remote_copy(
                    src_ref=got, dst_ref=got, send_sem=ici_send.at[3 * a + j], recv_sem=ici_recv.at[3 * a + j],
                    device_id=(px, py, c), device_id_type=MESH).wait_recv()
                fwd = pltpu.make_async_remote_copy(
                    src_ref=got, dst_ref=got, send_sem=d2d_send.at[3 * a + j], recv_sem=d2d_recv.at[3 * a + j],
                    device_id=(x, y, 1 - c), device_id_type=MESH)
                fwd.start()
                sends.append(fwd)
        for a in range(n):
            for j, (px, py) in enumerate(chips):
                other = piece(a, 2 * px + py, 1 - c)
                pltpu.make_async_remote_copy(
                    src_ref=other, dst_ref=other, send_sem=d2d_send.at[3 * a + j], recv_sem=d2d_recv.at[3 * a + j],
                    device_id=(x, y, 1 - c), device_id_type=MESH).wait_recv()
        for j, (px, py) in enumerate(chips):
            pltpu.make_async_remote_copy(
                src_ref=tap_in, dst_ref=tap_out.at[2 * px + py], send_sem=ici_send.at[3 * n + j],
                recv_sem=ici_recv.at[3 * n + j], device_id=(px, py, c), device_id_type=MESH).wait_recv()
        for a in range(n):
            pltpu.make_async_remote_copy(
                src_ref=ins[a], dst_ref=slab(a, mine), send_sem=own_send.at[a], recv_sem=own_recv.at[a],
                device_id=sibling, device_id_type=MESH).wait_recv()
        pltpu.make_async_remote_copy(
            src_ref=tap_in, dst_ref=tap_out.at[mine], send_sem=own_send.at[n], recv_sem=own_recv.at[n],
            device_id=sibling, device_id_type=MESH).wait_recv()
        for cp in sends:
            cp.wait_send()

    def full_shape(a):
        r, cs = shards[a].shape
        return (r, 4 * cs) if col_kind[a] else (4 * r, cs)

    res = pl.pallas_call(
        body, name="gather_weights", in_specs=[ANY] * (n + 1), out_specs=[ANY] * (n + 1),
        out_shape=[jax.ShapeDtypeStruct(full_shape(a), shards[a].dtype) for a in range(n)]
        + [jax.ShapeDtypeStruct((4,) + taps.shape, taps.dtype)],
        scratch_shapes=[pltpu.SemaphoreType.DMA((3 * n + 3,)), pltpu.SemaphoreType.DMA((3 * n + 3,)),
                        pltpu.SemaphoreType.DMA((3 * n,)), pltpu.SemaphoreType.DMA((3 * n,)),
                        pltpu.SemaphoreType.DMA((n + 1,)), pltpu.SemaphoreType.DMA((n + 1,))],
        compiler_params=pltpu.CompilerParams(has_side_effects=True),
    )(*shards, taps)
    return res[:n], res[n]


HBM = pl.BlockSpec(memory_space=pltpu.HBM)
SEM = pl.BlockSpec(memory_space=pltpu.SEMAPHORE)
DATAFLOW = pltpu.SideEffectType.DATAFLOW_SIDE_EFFECTING


def _in_hbm(a):
    return pltpu.with_memory_space_constraint(a, pltpu.HBM)


def _split_start(name, srcs, lands, n_copies, plan):
    n = len(srcs)

    def body(*refs):
        src_refs, land_refs = refs[:n], refs[n:2 * n]
        send_sems, recv_sems = refs[2 * n], refs[2 * n + 1]
        for i, (src, dst, dev) in enumerate(plan(src_refs, land_refs)):
            pltpu.make_async_remote_copy(src_ref=src, dst_ref=dst, send_sem=send_sems.at[i], recv_sem=recv_sems.at[i],
                                         device_id=dev, device_id_type=MESH).start()
        refs[-1][...] = jnp.zeros((8, 128), F32)

    res = pl.pallas_call(
        body, name=name, in_specs=[HBM] * (2 * n),
        out_specs=[SEM, SEM] + [HBM] * (2 * n) + [pl.BlockSpec(memory_space=pltpu.VMEM)],
        out_shape=[pltpu.SemaphoreType.DMA((n_copies,)), pltpu.SemaphoreType.DMA((n_copies,))]
        + [pltpu.HBM(a.shape, a.dtype) for a in list(srcs) + list(lands)] + [jax.ShapeDtypeStruct((8, 128), F32)],
        input_output_aliases={i: 2 + i for i in range(2 * n)},
        compiler_params=pltpu.CompilerParams(has_side_effects=DATAFLOW),
    )(*[_in_hbm(a) for a in list(srcs) + list(lands)])
    return res[0], res[1], list(res[2:2 + n]), list(res[2 + n:2 + 2 * n]), res[-1]


def _split_wait(name, send_sems, recv_sems, srcs, lands, after, plan):
    n = len(srcs)

    def body(*refs):
        src_refs, land_refs = refs[:n], refs[n:2 * n]
        send_ref, recv_ref = refs[2 * n], refs[2 * n + 1]
        for i, (src, dst, dev) in enumerate(plan(src_refs, land_refs)):
            cp = pltpu.make_async_remote_copy(src_ref=src, dst_ref=dst, send_sem=send_ref.at[i], recv_sem=recv_ref.at[i],
                                              device_id=dev, device_id_type=MESH)
            cp.wait_send()
            cp.wait_recv()

    res = pl.pallas_call(
        body, name=name, in_specs=[HBM] * (2 * n) + [SEM, SEM, ANY], out_specs=[HBM] * (2 * n),
        out_shape=[pltpu.HBM(a.shape, a.dtype) for a in list(srcs) + list(lands)],
        input_output_aliases={i: i for i in range(2 * n)},
        compiler_params=pltpu.CompilerParams(has_side_effects=DATAFLOW),
    )(*srcs, *lands, send_sems, recv_sems, after)
    return list(res[:n]), list(res[n:])


def _late_gather_plan(col_kind):
    def plan(src_refs, land_refs):
        x, y, c = _place()
        mine = 2 * x + y
        copies = []
        for a, (src, land) in enumerate(zip(src_refs, land_refs)):
            r, cs = src.shape
            if col_kind[a]:
                dst = land.at[:, pl.ds(pl.multiple_of(mine * cs, 128), cs)]
            else:
                dst = land.at[pl.ds(pl.multiple_of(mine * r, 16), r), :]
            copies.append((src, dst, (x, y, 1 - c)))
            copies += [(src, dst, (px, py, c)) for (px, py) in _other_chips(x, y)]
        return copies
    return plan


def _late_reduce_plan(col_kind):
    def plan(src_refs, land_refs):
        x, y, c = _place()
        copies = []
        for a, (src, land) in enumerate(zip(src_refs, land_refs)):
            for j, (px, py) in enumerate(_other_chips(x, y)):
                if col_kind[a]:
                    cs = land.shape[2]
                    piece = src.at[:, pl.ds(pl.multiple_of((2 * px + py) * cs, 128), cs)]
                else:
                    piece = src.at[2 * px + py]
                copies.append((piece, land.at[j], (px, py, c)))
        return copies
    return plan


def _pair_exchange_halves(name, grads, col_kind):
    n = len(grads)

    def body(*refs):
        ins, outs = refs[:n], refs[n:2 * n]
        send_sems, recv_sems = refs[2 * n:]
        x, y, c = _place()
        copies = []
        for a in range(n):
            if col_kind[a]:
                hr = ins[a].shape[0] // 2
                src = ins[a].at[pl.ds(pl.multiple_of((1 - c) * hr, 8), hr), :]
            else:
                hr = ins[a].shape[1] // 2
                src = ins[a].at[:, pl.ds(pl.multiple_of((1 - c) * hr, 8), hr), :]
            cp = pltpu.make_async_remote_copy(
                src_ref=src, dst_ref=outs[a], send_sem=send_sems.at[a], recv_sem=recv_sems.at[a],
                device_id=(x, y, 1 - c), device_id_type=MESH)
            cp.start()
            copies.append(cp)
        for cp in copies:
            cp.wait()

    def half_shape(a):
        s = grads[a].shape
        return (s[0] // 2, s[1]) if col_kind[a] else (4, s[1] // 2, s[2])

    return pl.pallas_call(
        body, name=name, in_specs=[ANY] * n, out_specs=[ANY] * n,
        out_shape=[jax.ShapeDtypeStruct(half_shape(a), grads[a].dtype) for a in range(n)],
        scratch_shapes=[pltpu.SemaphoreType.DMA((n,)), pltpu.SemaphoreType.DMA((n,))],
        compiler_params=pltpu.CompilerParams(has_side_effects=True),
    )(*grads)


def _chip_exchange(name, sums, col_kind):
    n = len(sums)

    def piece_shape(a):
        s = sums[a].shape
        return (s[0], s[1] // 4) if col_kind[a] else (s[1], s[2])

    def body(*refs):
        ins, outs = refs[:n], refs[n:2 * n]
        send_sems, recv_sems = refs[2 * n:]
        x, y, c = _place()
        copies = []
        for a in range(n):
            for j, (px, py) in enumerate(_other_chips(x, y)):
                if col_kind[a]:
                    cs = piece_shape(a)[1]
                    src = ins[a].at[:, pl.ds(pl.multiple_of((2 * px + py) * cs, 128), cs)]
                else:
                    src = ins[a].at[2 * px + py]
                cp = pltpu.make_async_remote_copy(
                    src_ref=src, dst_ref=outs[a].at[j], send_sem=send_sems.at[3 * a + j],
                    recv_sem=recv_sems.at[3 * a + j], device_id=(px, py, c), device_id_type=MESH)
                cp.start()
                copies.append(cp)
        for cp in copies:
            cp.wait()

    return pl.pallas_call(
        body, name=name, in_specs=[ANY] * n, out_specs=[ANY] * n,
        out_shape=[jax.ShapeDtypeStruct((3,) + piece_shape(a), sums[a].dtype) for a in range(n)],
        scratch_shapes=[pltpu.SemaphoreType.DMA((3 * n,)), pltpu.SemaphoreType.DMA((3 * n,))],
        compiler_params=pltpu.CompilerParams(has_side_effects=True),
    )(*sums)


def _pair_swap_halves(halves):
    n = len(halves)

    def body(*refs):
        ins, outs = refs[:n], refs[n:2 * n]
        send_sems, recv_sems = refs[2 * n:]
        x, y, c = _place()
        copies = []
        for a in range(n):
            cp = pltpu.make_async_remote_copy(
                src_ref=ins[a], dst_ref=outs[a], send_sem=send_sems.at[a], recv_sem=recv_sems.at[a],
                device_id=(x, y, 1 - c), device_id_type=MESH)
            cp.start()
            copies.append(cp)
        for cp in copies:
            cp.wait()

    return pl.pallas_call(
        body, name="reduce_pair_swap", in_specs=[ANY] * n, out_specs=[ANY] * n,
        out_shape=[jax.ShapeDtypeStruct(s.shape, s.dtype) for s in halves],
        scratch_shapes=[pltpu.SemaphoreType.DMA((n,)), pltpu.SemaphoreType.DMA((n,))],
        compiler_params=pltpu.CompilerParams(has_side_effects=True),
    )(*halves)


def _chip_sum(name, chip_sel, own, col, others):
    _, r, c = others.shape
    tr = _pick(r, (256, 128, 64, 32, 16))
    if col:
        own_spec = pl.BlockSpec((tr, c), lambda i, s: (i, s[0]))
    else:
        own_spec = pl.BlockSpec((None, tr, c), lambda i, s: (s[0], i, 0))
    specs = [own_spec] + [pl.BlockSpec((None, tr, c), lambda i, s, k=k: (k, i, 0)) for k in range(3)]

    def body(s_ref, own_ref, r0, r1, r2, o_ref):
        o_ref[...] = ((own_ref[...].astype(F32) + r0[...].astype(F32)) + r1[...].astype(F32)) + r2[...].astype(F32)

    return pl.pallas_call(
        body, name=name,
        grid_spec=pltpu.PrefetchScalarGridSpec(
            num_scalar_prefetch=1, grid=(r // tr,), in_specs=specs,
            out_specs=pl.BlockSpec((tr, c), lambda i, s: (i, 0))),
        out_shape=jax.ShapeDtypeStruct((r, c), F32),
        compiler_params=_params(("parallel",)),
    )(chip_sel, own, others, others, others)


def _pair_sum(name, c_sel, grad, recv, col):
    if col:
        r, c4 = grad.shape
        hr, c = r // 2, c4 // 4
    else:
        _, r, c = grad.shape
        hr = r // 2
    tr = _pick(hr, (256, 128, 64, 32, 16))
    nb = hr // tr

    def body(s_ref, g_ref, r_ref, o_ref):
        o_ref[...] = (g_ref[...] + r_ref[...]).astype(o_ref.dtype)

    if col:
        in_specs = [pl.BlockSpec((tr, c), lambda k, i, s: (s[0] * nb + i, k)), pl.BlockSpec((tr, c), lambda k, i, s: (i, k))]
        out_spec = pl.BlockSpec((tr, c), lambda k, i, s: (i, k))
    else:
        in_specs = [pl.BlockSpec((None, tr, c), lambda k, i, s: (k, s[0] * nb + i, 0)),
                    pl.BlockSpec((None, tr, c), lambda k, i, s: (k, i, 0))]
        out_spec = pl.BlockSpec((None, tr, c), lambda k, i, s: (k, i, 0))
    return pl.pallas_call(
        body, name=name,
        grid_spec=pltpu.PrefetchScalarGridSpec(num_scalar_prefetch=1, grid=(4, nb), in_specs=in_specs,
                                               out_specs=out_spec),
        out_shape=jax.ShapeDtypeStruct(recv.shape, BF16),
        compiler_params=_params(("parallel", "parallel")),
    )(c_sel, grad, recv)


def _allreduce_small(vals):
    sizes = [int(math.prod(v.shape)) for v in vals]
    padded = [-(-s // 128) * 128 for s in sizes]
    total = -(-sum(padded) // 1024) * 1024
    flat = [jnp.pad(v.reshape(-1), (0, p - s)) for v, s, p in zip(vals, sizes, padded)]
    flat.append(jnp.zeros((total - sum(padded),), F32))
    packed = jnp.concatenate(flat).reshape(total // 128, 128)

    def body(in_ref, out_ref, r0, r1, r2, send_sems, recv_sems):
        x, y, c = _place()
        out_ref[...] = in_ref[...]
        for k, (peer, land) in enumerate(zip([(x, y, 1 - c), (1 - x, y, c), (x, 1 - y, c)], (r0, r1, r2))):
            cp = pltpu.make_async_remote_copy(
                src_ref=out_ref, dst_ref=land, send_sem=send_sems.at[k], recv_sem=recv_sems.at[k],
                device_id=peer, device_id_type=MESH)
            cp.start()
            cp.wait()
            out_ref[...] = out_ref[...] + land[...]

    vm = pl.BlockSpec(memory_space=pltpu.VMEM)
    summed = pl.pallas_call(
        body, name="allreduce_small", in_specs=[vm], out_specs=vm,
        out_shape=jax.ShapeDtypeStruct(packed.shape, F32),
        scratch_shapes=[pltpu.VMEM(packed.shape, F32)] * 3
        + [pltpu.SemaphoreType.DMA((3,)), pltpu.SemaphoreType.DMA((3,))],
        compiler_params=pltpu.CompilerParams(has_side_effects=True, vmem_limit_bytes=VMEM_LIMIT_BYTES),
    )(packed).reshape(-1)
    outs, off = [], 0
    for v, s, p in zip(vals, sizes, padded):
        outs.append(summed[off:off + s].reshape(v.shape))
        off += p
    return outs


def _adamw_math(w, g, m, v):
    m2 = ADAM_B1 * m + (1.0 - ADAM_B1) * g
    v2 = ADAM_B2 * v + (1.0 - ADAM_B2) * (g * g)
    m_hat = m2 / (1.0 - ADAM_B1 ** ADAM_STEP)
    v_hat = v2 / (1.0 - ADAM_B2 ** ADAM_STEP)
    delta = -ADAM_LR * (m_hat / (jnp.sqrt(v_hat) + ADAM_EPS) + ADAM_WD * w)
    return delta, m2, v2


def _adamw_big(name, c_sel, w, g_mine, g_sibling, m, v):
    r, c = w.shape
    hr = r // 2
    tr = _pick(hr, (256, 128, 64, 32, 16, 8))
    nb = hr // tr

    def body(s_ref, w_ref, ga_ref, gb_ref, m_ref, v_ref, go_ref, d_ref, mo_ref, vo_ref):
        gv = jnp.where(pl.program_id(0) == s_ref[0], ga_ref[...], gb_ref[...])
        d, m2, v2 = _adamw_math(w_ref[...], gv, m_ref[...], v_ref[...])
        go_ref[...] = gv
        d_ref[...] = d
        mo_ref[...] = m2
        vo_ref[...] = v2

    blk = pl.BlockSpec((tr, c), lambda h, i, s: (h * nb + i, 0))
    half = pl.BlockSpec((tr, c), lambda h, i, s: (i, 0))
    return pl.pallas_call(
        body, name=name,
        grid_spec=pltpu.PrefetchScalarGridSpec(
            num_scalar_prefetch=1, grid=(2, nb), in_specs=[blk, half, half, blk, blk], out_specs=[blk] * 4),
        out_shape=[jax.ShapeDtypeStruct((r, c), F32)] * 4, compiler_params=_params(("parallel", "parallel")),
    )(c_sel, w, g_mine, g_sibling, m, v)


def _adamw_small(ws, gs, ms, vs):
    n = len(ws)

    def body(*refs):
        w_r, g_r, m_r, v_r = refs[:n], refs[n:2 * n], refs[2 * n:3 * n], refs[3 * n:4 * n]
        o = refs[4 * n:]
        for a in range(n):
            gv = g_r[a][...]
            d, m2, v2 = _adamw_math(w_r[a][...], gv, m_r[a][...], v_r[a][...])
            o[a][...] = gv
            o[n + a][...] = d
            o[2 * n + a][...] = m2
            o[3 * n + a][...] = v2

    res = pl.pallas_call(
        body, name="adamw_small", out_shape=[jax.ShapeDtypeStruct(w.shape, F32) for _ in range(4) for w in ws],
        compiler_params=_params(),
    )(*ws, *gs, *ms, *vs)
    return res[:n], res[n:2 * n], res[2 * n:3 * n], res[3 * n:]


def _full_from_gathered(name, gathered):
    if name == "w_in":
        rows = gathered.shape[0] // 4
        return gathered.reshape(4, rows, gathered.shape[1]).transpose(1, 0, 2).reshape(rows, 4 * gathered.shape[1])
    return gathered


def _reduce_layout(name, full):
    if name in COL_KIND:
        return full
    if name == "w_in":
        rows, cols = full.shape
        return full.reshape(rows, 4, cols // 4).transpose(1, 0, 2)
    return full.reshape(4, full.shape[0] // 4, full.shape[1])


def kernel(x, mem, norm_mix, w_in, fox_q_norm, fox_k_norm, fox_f_bias, s5_a_re, s5_a_im, s5_log_dt, s5_b_re, s5_b_im, s5_c_re, s5_c_im, s5_d, s5_w_glu, s5_b_glu, out_norm_fox, out_norm_s5, w_out, norm_cross, norm_mem, w_xq, w_xkv, xq_norm, xk_norm, w_xo, norm_ffn, w_ffn_up, ffn_conv_w, ffn_conv_b, w_ffn_down, loss_target, m_norm_mix, m_w_in, m_fox_q_norm, m_fox_k_norm, m_fox_f_bias, m_s5_a_re, m_s5_a_im, m_s5_log_dt, m_s5_b_re, m_s5_b_im, m_s5_c_re, m_s5_c_im, m_s5_d, m_s5_w_glu, m_s5_b_glu, m_out_norm_fox, m_out_norm_s5, m_w_out, m_norm_cross, m_norm_mem, m_w_xq, m_w_xkv, m_xq_norm, m_xk_norm, m_w_xo, m_norm_ffn, m_w_ffn_up, m_ffn_conv_w, m_ffn_conv_b, m_w_ffn_down, v_norm_mix, v_w_in, v_fox_q_norm, v_fox_k_norm, v_fox_f_bias, v_s5_a_re, v_s5_a_im, v_s5_log_dt, v_s5_b_re, v_s5_b_im, v_s5_c_re, v_s5_c_im, v_s5_d, v_s5_w_glu, v_s5_b_glu, v_out_norm_fox, v_out_norm_s5, v_w_out, v_norm_cross, v_norm_mem, v_w_xq, v_w_xkv, v_xq_norm, v_xk_norm, v_w_xo, v_norm_ffn, v_w_ffn_up, v_ffn_conv_w, v_ffn_conv_b, v_w_ffn_down):
    given = dict(locals())
    w = {n: given[n] for n in WEIGHTS}
    m = {n: given["m_" + n] for n in WEIGHTS}
    v = {n: given["v_" + n] for n in WEIGHTS}
    xi, yi, ci = _place()
    chip = (2 * xi + yi).astype(jnp.int32)

    c_sel = ci.astype(jnp.int32).reshape(1)
    chip_sel = chip.reshape(1)
    early_kind = [n in COL_KIND for n in EARLY_WEIGHTS]
    late_kind = [n in COL_KIND for n in LATE_WEIGHTS]

    gathered, taps = _gather_weights([w[n][0].astype(BF16) for n in EARLY_WEIGHTS], early_kind, w["ffn_conv_w"][0])
    wb = {n: _full_from_gathered(n, gathered[k]) for k, n in enumerate(EARLY_WEIGHTS)}
    conv_w = taps.transpose(1, 0, 2).reshape(3, D_FF)
    late_shards = [w[n][0].astype(BF16) for n in LATE_WEIGHTS]
    late_full = [lax.empty((s.shape[0], 4 * s.shape[1]) if ck else (4 * s.shape[0], s.shape[1]), BF16)
                 for s, ck in zip(late_shards, late_kind)]
    gather_plan = _late_gather_plan(late_kind)
    g_send, g_recv, g_srcs, g_lands, g_started = _split_start(
        "gather_late_start", late_shards, late_full, 4 * len(LATE_WEIGHTS), gather_plan)

    def late_weights(after):
        _, full = _split_wait("gather_late_wait", g_send, g_recv, g_srcs, g_lands, after, gather_plan)
        return dict(zip(LATE_WEIGHTS, full))

    reduce_plan = _late_reduce_plan(late_kind)
    late_reduce = {}

    def early_grads(late_g):
        grads = [_reduce_layout(n, late_g[n]) for n in LATE_WEIGHTS]
        from_sibling = _pair_exchange_halves("reduce_pair_exchange_late", grads, late_kind)
        sums = [_pair_sum("reduce_pair_sum_" + n, c_sel, gr, rv, ck)
                for n, gr, rv, ck in zip(LATE_WEIGHTS, grads, from_sibling, late_kind)]
        lands = [lax.empty((3, s.shape[0], s.shape[1] // 4) if ck else (3,) + s.shape[1:], BF16)
                 for s, ck in zip(sums, late_kind)]
        late_reduce["sems"] = _split_start("reduce_late_start", sums, lands, 3 * len(LATE_WEIGHTS), reduce_plan)
        return late_reduce["sems"][4][0:1, 0:1]

    p = {n: w[n][0] for n in SMALL}
    p["ffn_conv_w"] = conv_w
    for n in ("norm_mix", "fox_q_norm", "fox_k_norm", "fox_f_bias", "s5_b_glu", "out_norm_fox", "out_norm_s5",
              "norm_cross", "norm_mem", "xq_norm", "xk_norm", "norm_ffn", "ffn_conv_b"):
        p[n] = p[n].reshape(1, -1)
    p["norm_mix"] = p["norm_mix"] + g_started[0:1, 0:1]
    loss, grad_x, g = _local_step(x, mem, loss_target, p, wb, late_weights, early_grads)

    small_names = list(SMALL) + ["ffn_conv_w"]
    small_vals = [g[n].reshape(w[n].shape if n != "ffn_conv_w" else (1, 3, D_FF)) for n in small_names] + [loss]
    reduced = _allreduce_small(small_vals)
    loss_all = reduced[-1].reshape(())
    conv_w_grad = lax.dynamic_slice_in_dim(reduced[-2], chip * (D_FF // 4), D_FF // 4, axis=2)
    sg, sd, sm, sv = _adamw_small(
        [w[n] for n in small_names], list(reduced[:len(SMALL)]) + [conv_w_grad],
        [m[n] for n in small_names], [v[n] for n in small_names])
    out_g = dict(zip(small_names, sg))
    out_d = dict(zip(small_names, sd))
    out_m = dict(zip(small_names, sm))
    out_v = dict(zip(small_names, sv))

    grads = [_reduce_layout(n, g[n]) for n in EARLY_WEIGHTS]
    from_sibling = _pair_exchange_halves("reduce_pair_exchange_early", grads, early_kind)
    pair_sums = [_pair_sum("reduce_pair_sum_" + n, c_sel, gr, rv, ck)
                 for n, gr, rv, ck in zip(EARLY_WEIGHTS, grads, from_sibling, early_kind)]
    from_chips = _chip_exchange("reduce_chip_exchange_early", pair_sums, early_kind)
    r_send, r_recv, r_srcs, r_lands, _ = late_reduce["sems"]
    late_sums, late_from_chips = _split_wait("reduce_late_wait", r_send, r_recv, r_srcs, r_lands, from_chips[0],
                                             reduce_plan)
    col_kind = early_kind + late_kind
    halves = [_chip_sum("reduce_chip_sum_" + n, chip_sel, ps, ck, fc)
              for n, ps, fc, ck in zip(BIG, pair_sums + late_sums, list(from_chips) + late_from_chips, col_kind)]
    sibling_halves = _pair_swap_halves(halves)
    for n, mine, theirs in zip(BIG, halves, sibling_halves):
        go, d, m2, v2 = _adamw_big("adamw_" + n, c_sel, w[n][0], mine, theirs, m[n][0], v[n][0])
        out_g[n], out_d[n], out_m[n], out_v[n] = go[None], d[None], m2[None], v2[None]

    return (loss_all, grad_x, *[out_g[n] for n in WEIGHTS], *[out_d[n] for n in WEIGHTS],
            *[out_m[n] for n in WEIGHTS], *[out_v[n] for n in WEIGHTS])
```

```python
import functools
import math

import jax
import jax.numpy as jnp
from jax import lax
from jax.experimental import pallas as pl
from jax.experimental.pallas import tpu as pltpu

F32 = jnp.float32
BF16 = jnp.bfloat16

D_MODEL = 1024
FOX_WIDTH = 512
HEAD_DIM = 64
N_FOX_HEADS = 8
S5_WIDTH = 512
S5_GROUP_CH = 16
S5_GROUPS = 32
S5_STATE = 64
S5_CH = S5_GROUPS * S5_STATE
N_X_HEADS = 4
X_HEAD_DIM = 256
N_MEM = 256
D_FF = 2816
UF_COLS = 640
EPS = 1e-6
ADAM_LR = 0.001
ADAM_B1 = 0.9
ADAM_B2 = 0.999
ADAM_EPS = 1e-08
ADAM_WD = 0.01
ADAM_STEP = 10

VMEM_LIMIT_BYTES = 56 * 1024 * 1024
MM_BLOCK_BYTES = 6 * 1024 * 1024
MM_VMEM_BYTES = 40 * 1024 * 1024
MESH = pl.DeviceIdType.MESH

EARLY_WEIGHTS = ("w_in", "s5_w_glu", "w_out")
LATE_WEIGHTS = ("w_xq", "w_xkv", "w_xo", "w_ffn_up", "w_ffn_down")
BIG = EARLY_WEIGHTS + LATE_WEIGHTS
COL_KIND = ("w_xkv", "w_ffn_up")
SMALL = ("norm_mix", "fox_q_norm", "fox_k_norm", "fox_f_bias", "s5_a_re", "s5_a_im", "s5_log_dt",
         "s5_b_re", "s5_b_im", "s5_c_re", "s5_c_im", "s5_d", "s5_b_glu", "out_norm_fox", "out_norm_s5",
         "norm_cross", "norm_mem", "xq_norm", "xk_norm", "norm_ffn", "ffn_conv_b")
WEIGHTS = ("norm_mix", "w_in", "fox_q_norm", "fox_k_norm", "fox_f_bias", "s5_a_re", "s5_a_im", "s5_log_dt",
           "s5_b_re", "s5_b_im", "s5_c_re", "s5_c_im", "s5_d", "s5_w_glu", "s5_b_glu", "out_norm_fox",
           "out_norm_s5", "w_out", "norm_cross", "norm_mem", "w_xq", "w_xkv", "xq_norm", "xk_norm", "w_xo",
           "norm_ffn", "w_ffn_up", "ffn_conv_w", "ffn_conv_b", "w_ffn_down")


def _params(sem=None):
    return pltpu.CompilerParams(dimension_semantics=sem, vmem_limit_bytes=VMEM_LIMIT_BYTES)


def _pick(n, cands):
    for c in cands:
        if n % c == 0:
            return c
    return n


_DIMS = {"nn": (((1,), (0,)), ((), ())), "nt": (((1,), (1,)), ((), ())), "tn": (((0,), (0,)), ((), ()))}


def _mm(a, b, mode, name, out_dtype=F32, res=None):
    if mode == "nn":
        (m, k), (k2, n) = a.shape, b.shape
    elif mode == "nt":
        (m, k), (n, k2) = a.shape, b.shape
    else:
        (k, m), (k2, n) = a.shape, b.shape
    assert k == k2, (name, a.shape, b.shape)

    has_res = res is not None
    a_size, b_size = a.dtype.itemsize, b.dtype.itemsize
    o_size = jnp.dtype(out_dtype).itemsize + (res.dtype.itemsize if has_res else 0)

    def tiles(dim):
        return [c for c in (1024, 512, 256, 128) if dim % c == 0] or [dim]

    best = None
    for tm in tiles(m):
        for tn in tiles(n):
            a_blk, b_blk = tm * k * a_size, tn * k * b_size
            if max(a_blk, b_blk) > MM_BLOCK_BYTES or 2 * (a_blk + b_blk + tm * tn * o_size) > MM_VMEM_BYTES:
                continue
            for rows_outer in (True, False):
                moved = (m * k * a_size + (m // tm) * n * k * b_size) if rows_outer else \
                        (n * k * b_size + (n // tn) * m * k * a_size)
                key = (moved, -(tm * tn))
                if best is None or key < best[0]:
                    best = (key, tm, tn, rows_outer)
    assert best is not None, (name, a.shape, b.shape)
    _, tm, tn, rows_outer = best
    ij = (lambda g0, g1: (g0, g1)) if rows_outer else (lambda g0, g1: (g1, g0))
    if mode == "tn":
        a_spec = pl.BlockSpec((k, tm), lambda g0, g1: (0, ij(g0, g1)[0]))
    else:
        a_spec = pl.BlockSpec((tm, k), lambda g0, g1: (ij(g0, g1)[0], 0))
    if mode == "nt":
        b_spec = pl.BlockSpec((tn, k), lambda g0, g1: (ij(g0, g1)[1], 0))
    else:
        b_spec = pl.BlockSpec((k, tn), lambda g0, g1: (0, ij(g0, g1)[1]))
    o_spec = pl.BlockSpec((tm, tn), lambda g0, g1: ij(g0, g1))
    grid = (m // tm, n // tn) if rows_outer else (n // tn, m // tm)
    dims = _DIMS[mode]

    def body(*refs):
        a_ref, b_ref = refs[0], refs[1]
        o_ref = refs[-1]
        acc = lax.dot_general(a_ref[...].astype(BF16), b_ref[...].astype(BF16), dims, preferred_element_type=F32)
        if has_res:
            acc = acc + refs[2][...].astype(F32)
        o_ref[...] = acc.astype(o_ref.dtype)

    return pl.pallas_call(
        body, name=name, grid=grid,
        in_specs=[a_spec, b_spec] + ([o_spec] if has_res else []),
        out_specs=o_spec, out_shape=jax.ShapeDtypeStruct((m, n), out_dtype),
        compiler_params=_params(("parallel", "parallel")),
    )(*((a, b, res) if has_res else (a, b)))


def _row_spec(tm, bc, off, step):
    return pl.BlockSpec((tm, bc), lambda i, h: (i, off + step * h))


ROW_TILE_ELEMS = 512 * 1024


def _row_tile(t, rows):
    widest = max(bc for (_, bc, _, _) in rows)
    return _pick(t, (min(t, ROW_TILE_ELEMS // widest), 512, 256, 128, 64, 8))


def _rowwise(fn, rows, pars, outs, name, heads=1):
    t = rows[0][0].shape[0]
    tm = _row_tile(t, rows)
    nr, npar = len(rows), len(pars)

    def body(*refs):
        vals = [r[...].astype(F32) for r in refs[:nr + npar]]
        res = fn(*vals)
        if not isinstance(res, (tuple, list)):
            res = (res,)
        for o_ref, v in zip(refs[nr + npar:], res):
            o_ref[...] = v.astype(o_ref.dtype)

    in_specs = [_row_spec(tm, bc, off, st) for (_, bc, off, st) in rows]
    in_specs += [pl.BlockSpec(p.shape, lambda i, h: (0, 0)) for p in pars]
    out_specs = [_row_spec(tm, bc, 0, st) for (_, bc, st, _) in outs]
    out_shape = [jax.ShapeDtypeStruct((t, c), dt) for (c, _, _, dt) in outs]
    res = pl.pallas_call(
        body, name=name, grid=(t // tm, heads), in_specs=in_specs, out_specs=out_specs, out_shape=out_shape,
        compiler_params=_params(("parallel", "parallel")),
    )(*[r[0] for r in rows], *pars)
    return res[0] if len(res) == 1 else res


def _rowwise_vjp(fn, rows, pars, cts, name, heads=1, adds=None, row_dtypes=None):
    t = rows[0][0].shape[0]
    tm = _row_tile(t, rows)
    nr, npar, nct = len(rows), len(pars), len(cts)
    adds = adds or [None] * nr
    add_list = [a for a in adds if a is not None]
    row_dtypes = row_dtypes or [F32] * nr

    def body(*refs):
        i, h = pl.program_id(0), pl.program_id(1)
        p = 0
        row_v = [r[...].astype(F32) for r in refs[p:p + nr]]; p += nr
        par_v = [r[...].astype(F32) for r in refs[p:p + npar]]; p += npar
        ct_v = [r[...].astype(F32) for r in refs[p:p + nct]]; p += nct
        add_refs = refs[p:p + len(add_list)]; p += len(add_list)
        drow_refs = refs[p:p + nr]; p += nr
        dpar_refs = refs[p:p + npar]

        def wrapped(*a):
            r = fn(*a)
            return tuple(r) if isinstance(r, (tuple, list)) else (r,)

        _, pull = jax.vjp(wrapped, *row_v, *par_v)
        grads = pull(tuple(ct_v))
        ai = 0
        for k in range(nr):
            g = grads[k]
            if adds[k] is not None:
                g = g + add_refs[ai][...].astype(F32)
                ai += 1
            drow_refs[k][...] = g.astype(drow_refs[k].dtype)

        @pl.when((i == 0) & (h == 0))
        def _():
            for r in dpar_refs:
                r[...] = jnp.zeros(r.shape, r.dtype)

        for k in range(npar):
            dpar_refs[k][...] += grads[nr + k]

    in_specs = [_row_spec(tm, bc, off, st) for (_, bc, off, st) in rows]
    in_specs += [pl.BlockSpec(q.shape, lambda i, h: (0, 0)) for q in pars]
    in_specs += [_row_spec(tm, bc, off, st) for (_, bc, off, st) in cts]
    in_specs += [_row_spec(tm, bc, off, st) for (_, bc, off, st) in add_list]
    out_specs = [_row_spec(tm, bc, 0, st) for (_, bc, _, st) in rows]
    out_specs += [pl.BlockSpec(q.shape, lambda i, h: (0, 0)) for q in pars]
    out_shape = [jax.ShapeDtypeStruct((t, bc * (heads if st else 1)), dt) for (_, bc, _, st), dt in zip(rows, row_dtypes)]
    out_shape += [jax.ShapeDtypeStruct(q.shape, F32) for q in pars]
    res = pl.pallas_call(
        body, name=name, grid=(t // tm, heads), in_specs=in_specs, out_specs=out_specs, out_shape=out_shape,
        compiler_params=_params(("arbitrary", "arbitrary")),
    )(*[r[0] for r in rows], *pars, *[c[0] for c in cts], *[a[0] for a in add_list])
    return list(res[:nr]), list(res[nr:])


def _rms(x, g):
    return x * lax.rsqrt(jnp.mean(x * x, axis=-1, keepdims=True) + EPS) * g


def _rms_pair(x, g):
    left = lax.broadcasted_iota(jnp.int32, x.shape, 1) < HEAD_DIM
    x2 = x * x
    ms_a = jnp.sum(jnp.where(left, x2, 0.0), axis=-1, keepdims=True) * (1.0 / HEAD_DIM)
    ms_b = jnp.sum(jnp.where(left, 0.0, x2), axis=-1, keepdims=True) * (1.0 / HEAD_DIM)
    return x * lax.rsqrt(jnp.where(left, ms_a, ms_b) + EPS) * g


def _gelu(x):
    return 0.5 * x * (1.0 + jnp.tanh(math.sqrt(2.0 / math.pi) * (x + 0.044715 * (x * x * x))))


def _s5_act(ys, u, d):
    return _gelu(ys + d * u)


def _s5_gate(yg, z, b, g):
    return _rms(yg * jax.nn.sigmoid(z + b), g)


def _lane_cumsum(x, reverse):
    n = x.shape[-1]
    lane = lax.broadcasted_iota(jnp.int32, x.shape, 1)
    k = 1
    while k < n:
        if reverse:
            x = x + jnp.where(lane < n - k, pltpu.roll(x, n - k, 1), 0.0)
        else:
            x = x + jnp.where(lane >= k, pltpu.roll(x, k, 1), 0.0)
        k *= 2
    return x


def _log_sigmoid(z):
    return jnp.minimum(z, 0.0) - jnp.log(1.0 + jnp.exp(-jnp.abs(z)))


def _forget_fwd(f, bias):
    def body(f_ref, b_ref, c_ref):
        c_ref[...] = _lane_cumsum(_log_sigmoid(f_ref[...] + b_ref[...]), False)

    return pl.pallas_call(body, name="forget_fwd", out_shape=jax.ShapeDtypeStruct(f.shape, F32),
                          compiler_params=_params())(f, bias)


def _forget_bwd(f, bias, dc):
    def body(f_ref, b_ref, dc_ref, df_ref, db_ref):
        dlog = _lane_cumsum(dc_ref[...], True)
        df = dlog * jax.nn.sigmoid(-(f_ref[...] + b_ref[...]))
        df_ref[...] = df
        db_ref[...] = jnp.sum(df, axis=1, keepdims=True)

    return pl.pallas_call(body, name="forget_bwd",
                          out_shape=(jax.ShapeDtypeStruct(f.shape, F32), jax.ShapeDtypeStruct(bias.shape, F32)),
                          compiler_params=_params())(f, bias, dc)


FOX_BLOCK = 256
_NT = _DIMS["nt"]
_TN = _DIMS["tn"]


N_PAIRS = N_FOX_HEADS // 2
V_BLOCK0 = 2 * N_PAIRS


def _left_lanes(shape):
    return lax.broadcasted_iota(jnp.int32, shape, 1) < HEAD_DIM


def _top_rows(shape):
    return lax.broadcasted_iota(jnp.int32, shape, 0) < HEAD_DIM


def _wide(c_tile, n):
    return c_tile if n == 128 else jnp.concatenate([c_tile] * (n // 128), axis=1)


def _fox_fwd(qn, kn, qkv, c_wide, seqs):
    t = qn.shape[0]
    l = t // seqs
    tb = min(FOX_BLOCK, l)
    nb = l // tb
    scale = HEAD_DIM ** -0.5

    def body(q_ref, k_ref, v_ref, ca_ref, cb_ref, o_ref, lse_ref, vt_ref):
        i = pl.program_id(2)
        top = _top_rows((128, tb))

        @pl.when(i == 0)
        def _():
            vt_ref[...] = v_ref[...].T.astype(BF16)

        qt = (q_ref[...].astype(F32) * scale).T.astype(BF16)
        zero = jnp.zeros_like(qt)
        qts = (jnp.where(top, qt, zero), jnp.where(top, zero, qt))
        causal = lax.broadcasted_iota(jnp.int32, (tb, tb), 0) <= lax.broadcasted_iota(jnp.int32, (tb, tb), 1)
        c_refs = (ca_ref, cb_ref)

        def tile(j, carry, masked):
            off = pl.multiple_of(j * tb, tb)
            k2 = k_ref[pl.ds(off, tb), :]
            vt = vt_ref[:, pl.ds(off, tb)]
            vts = (jnp.where(top, vt, zero), jnp.where(top, zero, vt))
            (ma, sa), (mb, sb), acc = carry
            new, alphas, pv = [], [], []
            for h, (m, s_sum) in enumerate(((ma, sa), (mb, sb))):
                st = jnp.dot(k2, qts[h], preferred_element_type=F32) - _wide(c_refs[h][pl.ds(off, tb), :], tb)
                if masked:
                    st = jnp.where(causal, st, -jnp.inf)
                m_new = jnp.maximum(m, jnp.max(st, axis=0, keepdims=True))
                alpha = jnp.exp(m - m_new)
                p = jnp.exp(st - m_new)
                new.append((m_new, alpha * s_sum + jnp.sum(p, axis=0, keepdims=True)))
                alphas.append(alpha)
                pv.append(jnp.dot(vts[h], p.astype(BF16), preferred_element_type=F32))
            acc = jnp.where(top, alphas[0], alphas[1]) * acc + pv[0] + pv[1]
            return new[0], new[1], acc

        stat = (jnp.full((1, tb), -jnp.inf, F32), jnp.zeros((1, tb), F32))
        carry = lax.fori_loop(0, i, lambda j, c: tile(j, c, False), (stat, stat, jnp.zeros((128, tb), F32)))
        (ma, sa), (mb, sb), acc = tile(i, carry, True)
        o_ref[...] = (acc / jnp.where(top, sa, sb)).T
        lse_ref[0:1, :] = ma + jnp.log(sa)
        lse_ref[1:2, :] = mb + jnp.log(sb)

    qblk = pl.BlockSpec((tb, 128), lambda b, hp, i: (b * nb + i, hp))
    return pl.pallas_call(
        body, name="fox_fwd", grid=(seqs, N_PAIRS, nb),
        in_specs=[qblk, pl.BlockSpec((l, 128), lambda b, hp, i: (b, hp)),
                  pl.BlockSpec((l, 128), lambda b, hp, i: (b, V_BLOCK0 + hp)),
                  pl.BlockSpec((None, l, 128), lambda b, hp, i: (b * N_FOX_HEADS + 2 * hp, 0, 0)),
                  pl.BlockSpec((None, l, 128), lambda b, hp, i: (b * N_FOX_HEADS + 2 * hp + 1, 0, 0))],
        out_specs=[qblk, pl.BlockSpec((None, 2, tb), lambda b, hp, i: (b * N_PAIRS + hp, 0, i))],
        out_shape=[jax.ShapeDtypeStruct((t, FOX_WIDTH), F32), jax.ShapeDtypeStruct((seqs * N_PAIRS, 2, l), F32)],
        scratch_shapes=[pltpu.VMEM((128, l), BF16)],
        compiler_params=_params(("parallel", "parallel", "arbitrary")),
    )(qn, kn, qkv, c_wide, c_wide)


def _fox_bwd(qn, kn, qkv, c_wide, o, do, lse, seqs):
    t = qn.shape[0]
    l = t // seqs
    tb = min(FOX_BLOCK, l)
    nb = l // tb
    scale = HEAD_DIM ** -0.5
    one_at = (HEAD_DIM, 0)

    def body(q_ref, k_ref, v_ref, ca_ref, cb_ref, o_ref, do_ref, lse_ref, dq_ref, dk_ref, dv_ref, dc_ref, dcq_ref,
             qt_ref, kt_ref, dot_ref, delta_ref, dqa_ref, dqb_ref):
        top_l = _top_rows((128, l))
        top = _top_rows((128, tb))
        left = _left_lanes((tb, 128))
        row_id = lax.broadcasted_iota(jnp.int32, (128, tb), 0)
        lane_id = lax.broadcasted_iota(jnp.int32, (tb, 128), 1)
        zero_t = jnp.zeros((128, tb), BF16)
        zero_l = jnp.zeros((tb, 128), BF16)
        rows = lambda a: (jnp.where(top, a, zero_t), jnp.where(top, zero_t, a))
        lanes = lambda a: (jnp.where(left, a, zero_l), jnp.where(left, zero_l, a))
        with_one_row = lambda pair: tuple(jnp.where(row_id == one_at[h], 1.0, pair[h]).astype(BF16) for h in (0, 1))
        with_one_lane = lambda pair: tuple(jnp.where(lane_id == one_at[h], 1.0, pair[h]).astype(BF16) for h in (0, 1))
        causal = lax.broadcasted_iota(jnp.int32, (tb, tb), 0) <= lax.broadcasted_iota(jnp.int32, (tb, tb), 1)
        c_refs = (ca_ref, cb_ref)
        dq_refs = (dqa_ref, dqb_ref)

        qt_ref[...] = (q_ref[...].astype(F32) * scale).T.astype(BF16)
        kt_ref[...] = k_ref[...].astype(F32).T.astype(BF16)
        do_t = do_ref[...].T
        dot_ref[...] = do_t.astype(BF16)
        prod_t = do_t * o_ref[...].T
        delta_ref[0:1, :] = jnp.sum(jnp.where(top_l, prod_t, 0.0), axis=0, keepdims=True)
        delta_ref[1:2, :] = jnp.sum(jnp.where(top_l, 0.0, prod_t), axis=0, keepdims=True)
        dqa_ref[...] = jnp.zeros(dqa_ref.shape, F32)
        dqb_ref[...] = jnp.zeros(dqb_ref.shape, F32)

        def kv_block(j, _):
            koff = pl.multiple_of(j * tb, tb)
            k2 = k_ref[pl.ds(koff, tb), :]
            v2 = v_ref[pl.ds(koff, tb), :].astype(BF16)
            kts = with_one_row(rows(kt_ref[:, pl.ds(koff, tb)]))
            cw = tuple(_wide(c_refs[h][pl.ds(koff, tb), :], tb) for h in (0, 1))

            def q_block(i, carry, masked):
                dks, dv = list(carry[:2]), carry[2]
                qoff = pl.multiple_of(i * tb, tb)
                qs = lanes((q_ref[pl.ds(qoff, tb), :].astype(F32) * scale).astype(BF16))
                qs_one = with_one_lane(qs)
                dos = lanes(do_ref[pl.ds(qoff, tb), :].astype(BF16))
                qts = rows(qt_ref[:, pl.ds(qoff, tb)])
                dots = rows(dot_ref[:, pl.ds(qoff, tb)])
                for h in (0, 1):
                    st = jnp.dot(k2, qts[h], preferred_element_type=F32) - cw[h]
                    p = jnp.exp(st - lse_ref[h:h + 1, pl.ds(qoff, tb)])
                    if masked:
                        p = jnp.where(causal, p, 0.0)
                    dp = jnp.dot(v2, dots[h], preferred_element_type=F32)
                    dsb = (p * (dp - delta_ref[h:h + 1, pl.ds(qoff, tb)])).astype(BF16)
                    dv = dv + jnp.dot(p.astype(BF16), dos[h], preferred_element_type=F32)
                    dks[h] = dks[h] + jnp.dot(dsb, qs_one[h], preferred_element_type=F32)
                    dq_refs[h][:, pl.ds(qoff, tb)] += jnp.dot(kts[h], dsb, preferred_element_type=F32)
                return dks[0], dks[1], dv

            z = jnp.zeros((tb, 128), F32)
            carry = q_block(j, (z, z, z), True)
            dka, dkb, dv = lax.fori_loop(j + 1, nb, lambda i, c: q_block(i, c, False), carry)
            dk_ref[pl.ds(koff, tb), :] = jnp.where(left, dka, dkb)
            dv_ref[pl.ds(koff, tb), :] = dv
            dc_ref[0:1, pl.ds(koff, tb)] = -dka.T[one_at[0]:one_at[0] + 1, :]
            dc_ref[1:2, pl.ds(koff, tb)] = -dkb.T[one_at[1]:one_at[1] + 1, :]
            return 0

        lax.fori_loop(0, nb, kv_block, 0)
        dq_ref[...] = (jnp.where(top_l, dqa_ref[...], dqb_ref[...]) * scale).T
        dcq_ref[0:1, :] = dqa_ref[one_at[0]:one_at[0] + 1, :]
        dcq_ref[1:2, :] = dqb_ref[one_at[1]:one_at[1] + 1, :]

    blk = pl.BlockSpec((l, 128), lambda b, hp: (b, hp))
    cspec = lambda k: pl.BlockSpec((None, l, 128), lambda b, hp: (b * N_FOX_HEADS + 2 * hp + k, 0, 0))
    rows2 = pl.BlockSpec((None, 2, l), lambda b, hp: (b * N_PAIRS + hp, 0, 0))
    wide = jax.ShapeDtypeStruct((t, FOX_WIDTH), F32)
    pair_rows = jax.ShapeDtypeStruct((seqs * N_PAIRS, 2, l), F32)
    return pl.pallas_call(
        body, name="fox_bwd", grid=(seqs, N_PAIRS),
        in_specs=[blk, blk, pl.BlockSpec((l, 128), lambda b, hp: (b, V_BLOCK0 + hp)), cspec(0), cspec(1), blk, blk, rows2],
        out_specs=[blk, blk, blk, rows2, rows2],
        out_shape=[wide, wide, wide, pair_rows, pair_rows],
        scratch_shapes=[pltpu.VMEM((128, l), BF16), pltpu.VMEM((128, l), BF16), pltpu.VMEM((128, l), BF16),
                        pltpu.VMEM((2, l), F32), pltpu.VMEM((128, l), F32), pltpu.VMEM((128, l), F32)],
        compiler_params=_params(("parallel", "parallel")),
    )(qn, kn, qkv, c_wide, c_wide, o, do, lse)


SCAN_ROWS = 256
SCAN_COLS = 1024


S5_IN = 128
S5_ST = 512
SCAN_CHUNKS = SCAN_COLS // S5_ST


def _s5_fwd(uf, bbr, bbi, cr, ci, ar, ai, seqs):
    t = uf.shape[0]
    l = t // seqs
    tl = min(SCAN_ROWS, l)
    nl = l // tl
    cb, nq = SCAN_COLS, SCAN_CHUNKS

    def body(u_ref, bbr_ref, bbi_ref, cr_ref, ci_ref, ar_ref, ai_ref, xr_ref, xi_ref, ys_ref, car_r, car_i, bu_r, bu_i):
        @pl.when(pl.program_id(2) == 0)
        def _():
            car_r[...] = jnp.zeros(car_r.shape, F32)
            car_i[...] = jnp.zeros(car_i.shape, F32)

        u = u_ref[...].astype(BF16)
        for q in range(nq):
            uq = u[:, q * S5_IN:(q + 1) * S5_IN]
            bu_r[:, q * S5_ST:(q + 1) * S5_ST] = jnp.dot(uq, bbr_ref[q], preferred_element_type=F32)
            bu_i[:, q * S5_ST:(q + 1) * S5_ST] = jnp.dot(uq, bbi_ref[q], preferred_element_type=F32)
        a_r, a_i = ar_ref[...], ai_ref[...]

        def step(tt, carry):
            xr, xi = carry
            nr = a_r * xr - a_i * xi + bu_r[pl.ds(tt, 1), :]
            ni = a_r * xi + a_i * xr + bu_i[pl.ds(tt, 1), :]
            xr_ref[pl.ds(tt, 1), :] = nr
            xi_ref[pl.ds(tt, 1), :] = ni
            return nr, ni

        xr, xi = lax.fori_loop(0, tl, step, (car_r[...], car_i[...]), unroll=8)
        car_r[...] = xr
        car_i[...] = xi
        for q in range(nq):
            xq_r = xr_ref[:, q * S5_ST:(q + 1) * S5_ST].astype(BF16)
            xq_i = xi_ref[:, q * S5_ST:(q + 1) * S5_ST].astype(BF16)
            ys_ref[:, q * S5_IN:(q + 1) * S5_IN] = (jnp.dot(xq_r, cr_ref[q], preferred_element_type=F32)
                                                    + jnp.dot(xq_i, ci_ref[q], preferred_element_type=F32))

    rows = lambda w: pl.BlockSpec((tl, w), lambda s, j, r: (s * nl + r, j))
    chunk = lambda a: pl.BlockSpec((nq,) + a.shape[1:], lambda s, j, r: (j, 0, 0))
    par = pl.BlockSpec((1, cb), lambda s, j, r: (0, j))
    return pl.pallas_call(
        body, name="s5_fwd", grid=(seqs, S5_CH // cb, nl),
        in_specs=[rows(nq * S5_IN), chunk(bbr), chunk(bbi), chunk(cr), chunk(ci), par, par],
        out_specs=[rows(cb), rows(cb), rows(nq * S5_IN)],
        out_shape=[jax.ShapeDtypeStruct((t, S5_CH), F32)] * 2 + [jax.ShapeDtypeStruct((t, S5_WIDTH), F32)],
        scratch_shapes=[pltpu.VMEM((1, cb), F32), pltpu.VMEM((1, cb), F32), pltpu.VMEM((tl, cb), F32),
                        pltpu.VMEM((tl, cb), F32)],
        compiler_params=_params(("parallel", "parallel", "arbitrary")),
    )(uf, bbr, bbi, cr, ci, ar, ai)


def _s5_bwd(dys, uf, xr, xi, bbr, bbi, cr, ci, ar, ai, seqs):
    t = dys.shape[0]
    l = t // seqs
    tl = min(SCAN_ROWS, l)
    nl = l // tl
    cb, nq = SCAN_COLS, SCAN_CHUNKS

    def body(dy_ref, u_ref, xr_ref, xi_ref, bbr_ref, bbi_ref, cr_ref, ci_ref, ar_ref, ai_ref,
             du_ref, dbbr_ref, dbbi_ref, dcr_ref, dci_ref, dar_ref, dai_ref, car_r, car_i, lam_r, lam_i):
        @pl.when(pl.program_id(2) == 0)
        def _():
            car_r[...] = jnp.zeros(car_r.shape, F32)
            car_i[...] = jnp.zeros(car_i.shape, F32)
            for acc_ref in (dbbr_ref, dbbi_ref, dcr_ref, dci_ref, dar_ref, dai_ref):
                acc_ref[...] = jnp.zeros(acc_ref.shape, F32)

        dy = dy_ref[...]
        for q in range(nq):
            dyq = dy[:, q * S5_IN:(q + 1) * S5_IN]
            lam_r[:, q * S5_ST:(q + 1) * S5_ST] = lax.dot_general(dyq, cr_ref[q], _NT, preferred_element_type=F32)
            lam_i[:, q * S5_ST:(q + 1) * S5_ST] = lax.dot_general(dyq, ci_ref[q], _NT, preferred_element_type=F32)
        a_r, a_i = ar_ref[...], ai_ref[...]

        def step(k, carry):
            lr, li, dar, dai = carry
            tt = tl - 1 - k
            xr_t = xr_ref[pl.ds(tt, 1), :]
            xi_t = xi_ref[pl.ds(tt, 1), :]
            dar = dar + lr * xr_t + li * xi_t
            dai = dai + li * xr_t - lr * xi_t
            nr = lam_r[pl.ds(tt, 1), :] + a_r * lr + a_i * li
            ni = lam_i[pl.ds(tt, 1), :] + a_r * li - a_i * lr
            lam_r[pl.ds(tt, 1), :] = nr
            lam_i[pl.ds(tt, 1), :] = ni
            return nr, ni, dar, dai

        lr, li, dar, dai = lax.fori_loop(
            0, tl, step, (car_r[...], car_i[...], jnp.zeros((1, cb), F32), jnp.zeros((1, cb), F32)), unroll=8)
        car_r[...] = lr
        car_i[...] = li
        dar_ref[...] += dar
        dai_ref[...] += dai
        u = u_ref[...].astype(BF16)
        for q in range(nq):
            st = slice(q * S5_ST, (q + 1) * S5_ST)
            io = slice(q * S5_IN, (q + 1) * S5_IN)
            lq_r, lq_i = lam_r[:, st].astype(BF16), lam_i[:, st].astype(BF16)
            du_ref[:, io] = (lax.dot_general(lq_r, bbr_ref[q], _NT, preferred_element_type=F32)
                             + lax.dot_general(lq_i, bbi_ref[q], _NT, preferred_element_type=F32))
            dbbr_ref[q] += lax.dot_general(u[:, io], lq_r, _TN, preferred_element_type=F32)
            dbbi_ref[q] += lax.dot_general(u[:, io], lq_i, _TN, preferred_element_type=F32)
            dcr_ref[q] += lax.dot_general(xr_ref[:, st].astype(BF16), dy[:, io], _TN, preferred_element_type=F32)
            dci_ref[q] += lax.dot_general(xi_ref[:, st].astype(BF16), dy[:, io], _TN, preferred_element_type=F32)

    rows = lambda w: pl.BlockSpec((tl, w), lambda s, j, r: (s * nl + nl - 1 - r, j))
    chunk = lambda a: pl.BlockSpec((nq,) + a.shape[1:], lambda s, j, r: (j, 0, 0))
    acc = lambda a: pl.BlockSpec((None, nq) + a.shape[1:], lambda s, j, r: (s, j, 0, 0))
    par = pl.BlockSpec((1, cb), lambda s, j, r: (0, j))
    par_acc = pl.BlockSpec((None, 1, cb), lambda s, j, r: (s, 0, j))
    per_seq = lambda a: jax.ShapeDtypeStruct((seqs,) + a.shape, F32)
    return pl.pallas_call(
        body, name="s5_bwd", grid=(seqs, S5_CH // cb, nl),
        in_specs=[rows(nq * S5_IN), rows(nq * S5_IN), rows(cb), rows(cb), chunk(bbr), chunk(bbi), chunk(cr), chunk(ci),
                  par, par],
        out_specs=[rows(nq * S5_IN), acc(bbr), acc(bbi), acc(cr), acc(ci), par_acc, par_acc],
        out_shape=[jax.ShapeDtypeStruct((t, S5_WIDTH), F32), per_seq(bbr), per_seq(bbi), per_seq(cr), per_seq(ci),
                   jax.ShapeDtypeStruct((seqs, 1, S5_CH), F32), jax.ShapeDtypeStruct((seqs, 1, S5_CH), F32)],
        scratch_shapes=[pltpu.VMEM((1, cb), F32), pltpu.VMEM((1, cb), F32), pltpu.VMEM((tl, cb), F32),
                        pltpu.VMEM((tl, cb), F32)],
        compiler_params=_params(("parallel", "parallel", "arbitrary")),
    )(dys, uf, xr, xi, bbr, bbi, cr, ci, ar, ai)


XATT_BLOCK = 512


def _xatt_probs(qv, kv):
    s = lax.dot_general(qv, kv, _NT, preferred_element_type=F32) * (X_HEAD_DIM ** -0.5)
    e = jnp.exp(s - jnp.max(s, axis=-1, keepdims=True))
    return e / jnp.sum(e, axis=-1, keepdims=True)


def _xatt_fwd(q, k, kv, seqs):
    t = q.shape[0]
    tq = min(XATT_BLOCK, t // seqs)
    nq = t // seqs // tq

    def body(q_ref, k_ref, v_ref, o_ref):
        p = _xatt_probs(q_ref[...], k_ref[...])
        o_ref[...] = jnp.dot(p.astype(BF16), v_ref[...].astype(BF16), preferred_element_type=F32).astype(o_ref.dtype)

    qs = pl.BlockSpec((tq, X_HEAD_DIM), lambda b, h, i: (b * nq + i, h))
    return pl.pallas_call(
        body, name="xatt_fwd", grid=(seqs, N_X_HEADS, nq),
        in_specs=[qs, pl.BlockSpec((N_MEM, X_HEAD_DIM), lambda b, h, i: (b, h)),
                  pl.BlockSpec((N_MEM, X_HEAD_DIM), lambda b, h, i: (b, N_X_HEADS + h))],
        out_specs=qs, out_shape=jax.ShapeDtypeStruct(q.shape, BF16),
        compiler_params=_params(("parallel", "parallel", "parallel")),
    )(q, k, kv)


def _xatt_bwd(q, k, kv, do, seqs):
    t = q.shape[0]
    tq = min(XATT_BLOCK, t // seqs)
    nq = t // seqs // tq
    scale = X_HEAD_DIM ** -0.5

    def body(q_ref, k_ref, v_ref, do_ref, dq_ref, dk_ref, dv_ref):
        @pl.when(pl.program_id(2) == 0)
        def _():
            dk_ref[...] = jnp.zeros(dk_ref.shape, F32)
            dv_ref[...] = jnp.zeros(dv_ref.shape, F32)

        qv, kk = q_ref[...], k_ref[...]
        p = _xatt_probs(qv, kk)
        dob = do_ref[...].astype(BF16)
        dp = lax.dot_general(dob, v_ref[...].astype(BF16), _NT, preferred_element_type=F32)
        ds = p * (dp - jnp.sum(dp * p, axis=-1, keepdims=True))
        dsb = ds.astype(BF16)
        dq_ref[...] = jnp.dot(dsb, kk, preferred_element_type=F32) * scale
        dk_ref[...] += lax.dot_general(dsb, qv, _TN, preferred_element_type=F32) * scale
        dv_ref[...] += lax.dot_general(p.astype(BF16), dob, _TN, preferred_element_type=F32)

    qs = pl.BlockSpec((tq, X_HEAD_DIM), lambda b, h, i: (b * nq + i, h))
    ks = pl.BlockSpec((N_MEM, X_HEAD_DIM), lambda b, h, i: (b, h))
    return pl.pallas_call(
        body, name="xatt_bwd", grid=(seqs, N_X_HEADS, nq),
        in_specs=[qs, ks, pl.BlockSpec((N_MEM, X_HEAD_DIM), lambda b, h, i: (b, N_X_HEADS + h)), qs],
        out_specs=[qs, ks, ks],
        out_shape=[jax.ShapeDtypeStruct(q.shape, F32), jax.ShapeDtypeStruct(k.shape, F32),
                   jax.ShapeDtypeStruct(k.shape, F32)],
        compiler_params=_params(("parallel", "parallel", "arbitrary")),
    )(q, k, kv, do)


CONV_COLS = 256


def _shift_down(x, k, row):
    return jnp.where(row >= k, pltpu.roll(x, k, 0), 0.0)


def _shift_up(x, k, row):
    n = x.shape[0]
    return jnp.where(row < n - k, pltpu.roll(x, n - k, 0), 0.0)


def _conv_pre(g, w, b, row):
    return b + w[0:1, :] * _shift_down(g, 2, row) + w[1:2, :] * _shift_down(g, 1, row) + w[2:3, :] * g


def _convgate_fwd(gu, w, b, seqs):
    t = gu.shape[0]
    l = t // seqs
    nc = D_FF // CONV_COLS

    def body(g_ref, u_ref, w_ref, b_ref, o_ref):
        g = g_ref[...].astype(F32)
        row = lax.broadcasted_iota(jnp.int32, g.shape, 0)
        pre = _conv_pre(g, w_ref[...], b_ref[...], row)
        o_ref[...] = (pre * jax.nn.sigmoid(pre) * u_ref[...].astype(F32)).astype(o_ref.dtype)

    return pl.pallas_call(
        body, name="convgate_fwd", grid=(seqs, nc),
        in_specs=[pl.BlockSpec((l, CONV_COLS), lambda s, j: (s, j)), pl.BlockSpec((l, CONV_COLS), lambda s, j: (s, nc + j)),
                  pl.BlockSpec((3, CONV_COLS), lambda s, j: (0, j)), pl.BlockSpec((1, CONV_COLS), lambda s, j: (0, j))],
        out_specs=pl.BlockSpec((l, CONV_COLS), lambda s, j: (s, j)),
        out_shape=jax.ShapeDtypeStruct((t, D_FF), BF16),
        compiler_params=_params(("parallel", "parallel")),
    )(gu, gu, w, b)


def _convgate_bwd(gu, w, b, dact, seqs):
    t = gu.shape[0]
    l = t // seqs
    nc = D_FF // CONV_COLS
    steps = nc * seqs

    def body(g_ref, u_ref, w_ref, b_ref, da_ref, dgu_ref, dw_ref, db_ref, stage, sems):
        j, s = pl.program_id(0), pl.program_id(1)
        n = j * seqs + s
        slot = n % 2

        def copies(slot_, j_, s_):
            rows = pl.ds(pl.multiple_of(s_ * l, 16), l)
            return [pltpu.make_async_copy(
                stage.at[slot_, half],
                dgu_ref.at[rows, pl.ds(pl.multiple_of((half * nc + j_) * CONV_COLS, 128), CONV_COLS)],
                sems.at[slot_, half]) for half in (0, 1)]

        @pl.when(s == 0)
        def _():
            dw_ref[...] = jnp.zeros(dw_ref.shape, F32)
            db_ref[...] = jnp.zeros(db_ref.shape, F32)

        @pl.when(n >= 2)
        def _():
            for cp in copies(slot, j, s):
                cp.wait()

        g, wv, da = g_ref[...].astype(F32), w_ref[...], da_ref[...].astype(F32)
        row = lax.broadcasted_iota(jnp.int32, g.shape, 0)
        g1, g2 = _shift_down(g, 1, row), _shift_down(g, 2, row)
        pre = b_ref[...] + wv[0:1, :] * g2 + wv[1:2, :] * g1 + wv[2:3, :] * g
        sg = jax.nn.sigmoid(pre)
        silu = pre * sg
        stage[slot, 1] = (da * silu).astype(stage.dtype)
        dpre = da * u_ref[...].astype(F32) * (sg * (1.0 + pre * (1.0 - sg)))
        dg = wv[2:3, :] * dpre + wv[1:2, :] * _shift_up(dpre, 1, row) + wv[0:1, :] * _shift_up(dpre, 2, row)
        stage[slot, 0] = dg.astype(stage.dtype)
        for cp in copies(slot, j, s):
            cp.start()
        dw_ref[0:1, :] += jnp.sum(dpre * g2, axis=0, keepdims=True)
        dw_ref[1:2, :] += jnp.sum(dpre * g1, axis=0, keepdims=True)
        dw_ref[2:3, :] += jnp.sum(dpre * g, axis=0, keepdims=True)
        db_ref[...] += jnp.sum(dpre, axis=0, keepdims=True)

        @pl.when(n == steps - 1)
        def _():
            for cp in copies(slot, j, s) + (copies(1 - slot, j, s) if steps > 1 else []):
                cp.wait()

    blk = lambda off: pl.BlockSpec((l, CONV_COLS), lambda j, s: (s, off + j))
    return pl.pallas_call(
        body, name="convgate_bwd", grid=(nc, seqs),
        in_specs=[blk(0), blk(nc), pl.BlockSpec((3, CONV_COLS), lambda j, s: (0, j)),
                  pl.BlockSpec((1, CONV_COLS), lambda j, s: (0, j)), blk(0)],
        out_specs=[ANY, pl.BlockSpec((3, CONV_COLS), lambda j, s: (0, j)),
                   pl.BlockSpec((1, CONV_COLS), lambda j, s: (0, j))],
        out_shape=[jax.ShapeDtypeStruct((t, 2 * D_FF), BF16), jax.ShapeDtypeStruct((3, D_FF), F32),
                   jax.ShapeDtypeStruct((1, D_FF), F32)],
        scratch_shapes=[pltpu.VMEM((2, 2, l, CONV_COLS), BF16), pltpu.SemaphoreType.DMA((2, 2))],
        compiler_params=_params(("arbitrary", "arbitrary")),
    )(gu, gu, w, b, dact)


def _loss_head(h, target):
    t, d = h.shape
    tm = _pick(t, (256, 128, 8))

    def body(h_ref, t_ref, dh_ref, dhb_ref, loss_ref):
        @pl.when(pl.program_id(0) == 0)
        def _():
            loss_ref[...] = jnp.zeros(loss_ref.shape, F32)

        e = h_ref[...] - t_ref[...]
        dh = e * (1.0 / d)
        dh_ref[...] = dh
        dhb_ref[...] = dh.astype(BF16)
        loss_ref[...] += (0.5 / d) * jnp.sum(jnp.sum(e * e, axis=1, keepdims=True), axis=0, keepdims=True)

    blk = pl.BlockSpec((tm, d), lambda i: (i, 0))
    return pl.pallas_call(
        body, name="loss_head", grid=(t // tm,), in_specs=[blk, blk],
        out_specs=[blk, blk, pl.BlockSpec((1, 1), lambda i: (0, 0))],
        out_shape=[jax.ShapeDtypeStruct((t, d), F32), jax.ShapeDtypeStruct((t, d), BF16),
                   jax.ShapeDtypeStruct((1, 1), F32)],
        compiler_params=_params(("arbitrary",)),
    )(h, target)


def _s5_discretise(a_re, a_im, log_dt, b_re, b_im):
    dt = jnp.exp(log_dt)[:, None]
    mag = jnp.exp(a_re * dt)
    lb_r = mag * jnp.cos(a_im * dt)
    lb_i = mag * jnp.sin(a_im * dt)
    den = a_re * a_re + a_im * a_im
    nr = lb_r - 1.0
    coef_r = (nr * a_re + lb_i * a_im) / den
    coef_i = (lb_i * a_re - nr * a_im) / den
    bb_r = coef_r[:, :, None] * b_re - coef_i[:, :, None] * b_im
    bb_i = coef_r[:, :, None] * b_im + coef_i[:, :, None] * b_re
    return lb_r, lb_i, bb_r, bb_i


S5_CHUNKS = 4
S5_PER = S5_GROUPS // S5_CHUNKS


def _blockdiag_in(bb):
    eye = jnp.eye(S5_PER, dtype=bb.dtype)
    return jnp.einsum("jgpc,gh->jgchp", bb.reshape(S5_CHUNKS, S5_PER, S5_STATE, S5_GROUP_CH), eye).reshape(
        S5_CHUNKS, S5_PER * S5_GROUP_CH, S5_PER * S5_STATE)


def _blockdiag_in_grad(d):
    eye = jnp.eye(S5_PER, dtype=d.dtype)
    return jnp.einsum("jgchp,gh->jgpc", d.reshape(S5_CHUNKS, S5_PER, S5_GROUP_CH, S5_PER, S5_STATE), eye).reshape(
        S5_GROUPS, S5_STATE, S5_GROUP_CH)


def _blockdiag_out(c):
    eye = jnp.eye(S5_PER, dtype=c.dtype)
    return jnp.einsum("jgcp,gh->jgphc", c.reshape(S5_CHUNKS, S5_PER, S5_GROUP_CH, S5_STATE), eye).reshape(
        S5_CHUNKS, S5_PER * S5_STATE, S5_PER * S5_GROUP_CH)


def _blockdiag_out_grad(d):
    eye = jnp.eye(S5_PER, dtype=d.dtype)
    return jnp.einsum("jgphc,gh->jgcp", d.reshape(S5_CHUNKS, S5_PER, S5_STATE, S5_PER, S5_GROUP_CH), eye).reshape(
        S5_GROUPS, S5_GROUP_CH, S5_STATE)


def _local_step(x3, mem3, target3, p, wb, late_weights=None, early_grads=None):
    seqs, l, d = x3.shape
    t = seqs * l
    x = x3.reshape(t, d)
    mem = mem3.reshape(seqs * N_MEM, d)
    target = target3.reshape(t, d)
    full = lambda a: (a, a.shape[1], 0, 0)

    s5_in = (p["s5_a_re"], p["s5_a_im"], p["s5_log_dt"], p["s5_b_re"], p["s5_b_im"])
    (lb_r, lb_i, bb_r, bb_i), s5_pull = jax.vjp(_s5_discretise, *s5_in)
    ar, ai = lb_r.reshape(1, S5_CH), lb_i.reshape(1, S5_CH)
    bbr_d, bbi_d = _blockdiag_in(bb_r).astype(BF16), _blockdiag_in(bb_i).astype(BF16)
    cr_d, ci_d = _blockdiag_out(p["s5_c_re"]).astype(BF16), (-_blockdiag_out(p["s5_c_im"])).astype(BF16)
    d_row = p["s5_d"].reshape(1, S5_WIDTH)

    w_in = wb["w_in"]
    w_qkv = w_in[:, :3 * FOX_WIDTH]
    w_uf = jnp.concatenate(
        [w_in[:, 3 * FOX_WIDTH + N_FOX_HEADS:], w_in[:, 3 * FOX_WIDTH:3 * FOX_WIDTH + N_FOX_HEADS],
         jnp.zeros((d, UF_COLS - S5_WIDTH - N_FOX_HEADS), w_in.dtype)], axis=1)

    hn1 = _rowwise(_rms, [full(x)], [p["norm_mix"]], [(d, d, 0, BF16)], "norm_mix_fwd")
    qkv = _mm(hn1, w_qkv, "nn", "in_qkv")
    uf = _mm(hn1, w_uf, "nn", "in_uf")

    bh = seqs * N_FOX_HEADS
    q_pair = (qkv, 128, 0, 1)
    k_pair = (qkv, 128, N_PAIRS, 1)
    gq2, gk2 = jnp.tile(p["fox_q_norm"], (1, 2)), jnp.tile(p["fox_k_norm"], (1, 2))
    pair_out = [(FOX_WIDTH, 128, 1, BF16)]
    qn = _rowwise(_rms_pair, [q_pair], [gq2], pair_out, "fox_qnorm_fwd", heads=N_PAIRS)
    kn = _rowwise(_rms_pair, [k_pair], [gk2], pair_out, "fox_knorm_fwd", heads=N_PAIRS)

    f_rows = uf[:, S5_WIDTH:S5_WIDTH + N_FOX_HEADS].reshape(seqs, l, N_FOX_HEADS).transpose(0, 2, 1).reshape(bh, l)
    f_bias = jnp.tile(p["fox_f_bias"].reshape(N_FOX_HEADS, 1), (seqs, 1))
    c_wide = jnp.broadcast_to(_forget_fwd(f_rows, f_bias)[:, :, None], (bh, l, 128))
    fox, lse = _fox_fwd(qn, kn, qkv, c_wide, seqs)

    xr, xi, ys = _s5_fwd(uf, bbr_d, bbi_d, cr_d, ci_d, ar, ai, seqs)
    u_blk = (uf, S5_WIDTH, 0, 0)
    yg = _rowwise(_s5_act, [full(ys), u_blk], [d_row], [(S5_WIDTH, S5_WIDTH, 0, F32)], "s5_act_fwd")
    z = _mm(yg, wb["s5_w_glu"], "nn", "s5_glu")
    y2n = _rowwise(_s5_gate, [full(yg), full(z)], [p["s5_b_glu"], p["out_norm_s5"]],
                   [(S5_WIDTH, S5_WIDTH, 0, BF16)], "s5_gate_fwd")
    foxn = _rowwise(_rms, [full(fox)], [p["out_norm_fox"]], [(FOX_WIDTH, FOX_WIDTH, 0, BF16)], "fox_outnorm_fwd")
    mixed = jnp.concatenate([foxn, y2n], axis=1)
    h1 = _mm(mixed, wb["w_out"], "nn", "mix_out", res=x)
    if late_weights is not None:
        wb = dict(wb, **late_weights(h1))

    hn2 = _rowwise(_rms, [full(h1)], [p["norm_cross"]], [(d, d, 0, BF16)], "norm_cross_fwd")
    mn = _rowwise(_rms, [full(mem)], [p["norm_mem"]], [(d, d, 0, BF16)], "norm_mem_fwd")
    xq_raw = _mm(hn2, wb["w_xq"], "nn", "x_q")
    kv = _mm(mn, wb["w_xkv"], "nn", "x_kv")
    xh = lambda a: (a, X_HEAD_DIM, 0, 1)
    xqn = _rowwise(_rms, [xh(xq_raw)], [p["xq_norm"]], [(d, X_HEAD_DIM, 1, BF16)], "x_qnorm_fwd", heads=N_X_HEADS)
    xkn = _rowwise(_rms, [xh(kv)], [p["xk_norm"]], [(d, X_HEAD_DIM, 1, BF16)], "x_knorm_fwd", heads=N_X_HEADS)
    xo = _xatt_fwd(xqn, xkn, kv, seqs)
    h2 = _mm(xo, wb["w_xo"], "nn", "x_out", res=h1)

    hn3 = _rowwise(_rms, [full(h2)], [p["norm_ffn"]], [(d, d, 0, BF16)], "norm_ffn_fwd")
    gu = _mm(hn3, wb["w_ffn_up"], "nn", "ffn_up", out_dtype=BF16)
    act = _convgate_fwd(gu, p["ffn_conv_w"], p["ffn_conv_b"], seqs)
    h3 = _mm(act, wb["w_ffn_down"], "nn", "ffn_down", res=h2)
    dh3, dh3_b, loss = _loss_head(h3, target)

    g = {}
    dact = _mm(dh3_b, wb["w_ffn_down"], "nt", "ffn_down_dx", out_dtype=BF16)
    g["w_ffn_down"] = _mm(act, dh3_b, "tn", "ffn_down_dw")
    dgu, g["ffn_conv_w"], g["ffn_conv_b"] = _convgate_bwd(gu, p["ffn_conv_w"], p["ffn_conv_b"], dact, seqs)
    dhn3 = _mm(dgu, wb["w_ffn_up"], "nt", "ffn_up_dx")
    g["w_ffn_up"] = _mm(hn3, dgu, "tn", "ffn_up_dw")
    (dh2,), (g["norm_ffn"],) = _rowwise_vjp(_rms, [full(h2)], [p["norm_ffn"]], [full(dhn3)], "norm_ffn_bwd",
                                            adds=[full(dh3)])

    dxo = _mm(dh2, wb["w_xo"], "nt", "x_out_dx")
    g["w_xo"] = _mm(xo, dh2, "tn", "x_out_dw")
    dxqn, dxkn, dxv = _xatt_bwd(xqn, xkn, kv, dxo, seqs)
    (dxq_raw,), (g["xq_norm"],) = _rowwise_vjp(_rms, [xh(xq_raw)], [p["xq_norm"]], [xh(dxqn)], "x_qnorm_bwd",
                                               heads=N_X_HEADS, row_dtypes=[BF16])
    (dxk_raw,), (g["xk_norm"],) = _rowwise_vjp(_rms, [xh(kv)], [p["xk_norm"]], [xh(dxkn)], "x_knorm_bwd",
                                               heads=N_X_HEADS, row_dtypes=[BF16])
    dkv = jnp.concatenate([dxk_raw, dxv.astype(BF16)], axis=1)
    dhn2 = _mm(dxq_raw, wb["w_xq"], "nt", "x_q_dx")
    g["w_xq"] = _mm(hn2, dxq_raw, "tn", "x_q_dw")
    dmn = _mm(dkv, wb["w_xkv"], "nt", "x_kv_dx")
    g["w_xkv"] = _mm(mn, dkv, "tn", "x_kv_dw")
    norm_cross = p["norm_cross"]
    if early_grads is not None:
        norm_cross = norm_cross + early_grads({n: g[n] for n in LATE_WEIGHTS})
    (dh1,), (g["norm_cross"],) = _rowwise_vjp(_rms, [full(h1)], [norm_cross], [full(dhn2)], "norm_cross_bwd",
                                              adds=[full(dh2)])
    _, (g["norm_mem"],) = _rowwise_vjp(_rms, [full(mem)], [p["norm_mem"]], [full(dmn)], "norm_mem_bwd",
                                       row_dtypes=[BF16])

    dmixed = _mm(dh1, wb["w_out"], "nt", "mix_out_dx")
    g["w_out"] = _mm(mixed, dh1, "tn", "mix_out_dw")
    (dfox,), (g["out_norm_fox"],) = _rowwise_vjp(_rms, [full(fox)], [p["out_norm_fox"]],
                                                 [(dmixed, FOX_WIDTH, 0, 0)], "fox_outnorm_bwd")
    (dyg_a, dz), (g["s5_b_glu"], g["out_norm_s5"]) = _rowwise_vjp(
        _s5_gate, [full(yg), full(z)], [p["s5_b_glu"], p["out_norm_s5"]], [(dmixed, S5_WIDTH, 1, 0)], "s5_gate_bwd",
        row_dtypes=[F32, BF16])
    dyg = _mm(dz, wb["s5_w_glu"], "nt", "s5_glu_dx", res=dyg_a)
    g["s5_w_glu"] = _mm(yg, dz, "tn", "s5_glu_dw")
    (dys, du_a), (dd_row,) = _rowwise_vjp(_s5_act, [full(ys), u_blk], [d_row], [full(dyg)], "s5_act_bwd",
                                          row_dtypes=[BF16, F32])
    g["s5_d"] = dd_row
    du_b, dbbr_d, dbbi_d, dcr_d, dci_d, dar, dai = _s5_bwd(dys, uf, xr, xi, bbr_d, bbi_d, cr_d, ci_d, ar, ai, seqs)
    dbbr_d, dbbi_d, dcr_d, dci_d = (jnp.sum(a, axis=0) for a in (dbbr_d, dbbi_d, dcr_d, dci_d))
    d_lb_r = jnp.sum(dar, axis=0).reshape(S5_GROUPS, S5_STATE)
    d_lb_i = jnp.sum(dai, axis=0).reshape(S5_GROUPS, S5_STATE)
    g["s5_a_re"], g["s5_a_im"], g["s5_log_dt"], g["s5_b_re"], g["s5_b_im"] = s5_pull(
        (d_lb_r, d_lb_i, _blockdiag_in_grad(dbbr_d), _blockdiag_in_grad(dbbi_d)))
    g["s5_c_re"] = _blockdiag_out_grad(dcr_d)
    g["s5_c_im"] = -_blockdiag_out_grad(dci_d)

    dqn, dkn, dv, dc, dcq = _fox_bwd(qn, kn, qkv, c_wide, fox, dfox, lse, seqs)
    pair = lambda a: (a, 128, 0, 1)
    (dq_raw,), (dgq2,) = _rowwise_vjp(_rms_pair, [q_pair], [gq2], [pair(dqn)], "fox_qnorm_bwd", heads=N_PAIRS,
                                      row_dtypes=[BF16])
    (dk_raw,), (dgk2,) = _rowwise_vjp(_rms_pair, [k_pair], [gk2], [pair(dkn)], "fox_knorm_bwd", heads=N_PAIRS,
                                      row_dtypes=[BF16])
    g["fox_q_norm"] = dgq2[:, :HEAD_DIM] + dgq2[:, HEAD_DIM:]
    g["fox_k_norm"] = dgk2[:, :HEAD_DIM] + dgk2[:, HEAD_DIM:]
    df_rows, dfb = _forget_bwd(f_rows, f_bias, (dc + dcq).reshape(bh, l))
    g["fox_f_bias"] = jnp.sum(dfb.reshape(seqs, N_FOX_HEADS), axis=0)
    df = df_rows.reshape(seqs, N_FOX_HEADS, l).transpose(0, 2, 1).reshape(t, N_FOX_HEADS)
    dqkv = jnp.concatenate([dq_raw, dk_raw, dv.astype(BF16)], axis=1)
    duf = jnp.concatenate([du_a + du_b, df, jnp.zeros((t, UF_COLS - S5_WIDTH - N_FOX_HEADS), F32)],
                          axis=1).astype(BF16)
    dhn1 = _mm(duf, w_uf, "nt", "in_uf_dx", res=_mm(dqkv, w_qkv, "nt", "in_qkv_dx"))
    dw_qkv = _mm(hn1, dqkv, "tn", "in_qkv_dw")
    dw_uf = _mm(hn1, duf, "tn", "in_uf_dw")
    g["w_in"] = jnp.concatenate([dw_qkv, dw_uf[:, S5_WIDTH:S5_WIDTH + N_FOX_HEADS], dw_uf[:, :S5_WIDTH]], axis=1)
    (dx,), (g["norm_mix"],) = _rowwise_vjp(_rms, [full(x)], [p["norm_mix"]], [full(dhn1)], "norm_mix_bwd",
                                           adds=[full(dh1)])
    return loss, dx.reshape(seqs, l, d), g


def _place():
    return lax.axis_index("x"), lax.axis_index("y"), lax.axis_index("c")


def _other_chips(x, y):
    return [(1 - x, y), (x, 1 - y), (1 - x, 1 - y)]


ANY = pl.BlockSpec(memory_space=pl.ANY)


def _gather_weights(shards, col_kind, taps):
    n = len(shards)

    def body(*refs):
        ins, tap_in, outs, tap_out = refs[:n], refs[n], refs[n + 1:2 * n + 1], refs[2 * n + 1]
        ici_send, ici_recv, d2d_send, d2d_recv, own_send, own_recv = refs[2 * n + 2:]
        x, y, c = _place()
        mine = 2 * x + y
        chips = _other_chips(x, y)
        sibling = (x, y, 1 - c)

        def piece(a, s, h):
            r, cs = ins[a].shape
            hr = r // 2
            if col_kind[a]:
                return outs[a].at[pl.ds(pl.multiple_of(h * hr, 16), hr), pl.ds(pl.multiple_of(s * cs, 128), cs)]
            return outs[a].at[pl.ds(pl.multiple_of(s * r + h * hr, 16), hr), :]

        def slab(a, s):
            r, cs = ins[a].shape
            if col_kind[a]:
                return outs[a].at[:, pl.ds(pl.multiple_of(s * cs, 128), cs)]
            return outs[a].at[pl.ds(pl.multiple_of(s * r, 16), r), :]

        def own_half(a, h):
            hr = ins[a].shape[0] // 2
            return ins[a].at[pl.ds(pl.multiple_of(h * hr, 16), hr), :]

        sends = []
        for a in range(n):
            cp = pltpu.make_async_remote_copy(
                src_ref=ins[a], dst_ref=slab(a, mine), send_sem=own_send.at[a], recv_sem=own_recv.at[a],
                device_id=sibling, device_id_type=MESH)
            cp.start()
            sends.append(cp)
        cp = pltpu.make_async_remote_copy(
            src_ref=tap_in, dst_ref=tap_out.at[mine], send_sem=own_send.at[n], recv_sem=own_recv.at[n],
            device_id=sibling, device_id_type=MESH)
        cp.start()
        sends.append(cp)
        for a in range(n):
            for j, (px, py) in enumerate(chips):
                cp = pltpu.make_async_remote_copy(
                    src_ref=own_half(a, c), dst_ref=piece(a, mine, c), send_sem=ici_send.at[3 * a + j],
                    recv_sem=ici_recv.at[3 * a + j], device_id=(px, py, c), device_id_type=MESH)
                cp.start()
                sends.append(cp)
        for j, (px, py) in enumerate(chips):
            cp = pltpu.make_async_remote_copy(
                src_ref=tap_in, dst_ref=tap_out.at[mine], send_sem=ici_send.at[3 * n + j],
                recv_sem=ici_recv.at[3 * n + j], device_id=(px, py, c), device_id_type=MESH)
            cp.start()
            sends.append(cp)
        for a in range(n):
            for j, (px, py) in enumerate(chips):
                got = piece(a, 2 * px + py, c)
                pltpu.make_async_remote_copy(
                    src_ref=got, dst_ref=got, send_sem=ici_send.at[3 * a + j], recv_sem=ici_recv.at[3 * a + j],
                    device_id=(px, py, c), device_id_type=MESH).wait_recv()
                fwd = pltpu.make_async_remote_copy(
                    src_ref=got, dst_ref=got, send_sem=d2d_send.at[3 * a + j], recv_sem=d2d_recv.at[3 * a + j],
                    device_id=(x, y, 1 - c), device_id_type=MESH)
                fwd.start()
                sends.append(fwd)
        for a in range(n):
            for j, (px, py) in enumerate(chips):
                other = piece(a, 2 * px + py, 1 - c)
                pltpu.make_async_remote_copy(
                    src_ref=other, dst_ref=other, send_sem=d2d_send.at[3 * a + j], recv_sem=d2d_recv.at[3 * a + j],
                    device_id=(x, y, 1 - c), device_id_type=MESH).wait_recv()
        for j, (px, py) in enumerate(chips):
            pltpu.make_async_remote_copy(
                src_ref=tap_in, dst_ref=tap_out.at[2 * px + py], send_sem=ici_send.at[3 * n + j],
                recv_sem=ici_recv.at[3 * n + j], device_id=(px, py, c), device_id_type=MESH).wait_recv()
        for a in range(n):
            pltpu.make_async_remote_copy(
                src_ref=ins[a], dst_ref=slab(a, mine), send_sem=own_send.at[a], recv_sem=own_recv.at[a],
                device_id=sibling, device_id_type=MESH).wait_recv()
        pltpu.make_async_remote_copy(
            src_ref=tap_in, dst_ref=tap_out.at[mine], send_sem=own_send.at[n], recv_sem=own_recv.at[n],
            device_id=sibling, device_id_type=MESH).wait_recv()
        for cp in sends:
            cp.wait_send()

    def full_shape(a):
        r, cs = shards[a].shape
        return (r, 4 * cs) if col_kind[a] else (4 * r, cs)

    res = pl.pallas_call(
        body, name="gather_weights", in_specs=[ANY] * (n + 1), out_specs=[ANY] * (n + 1),
        out_shape=[jax.ShapeDtypeStruct(full_shape(a), shards[a].dtype) for a in range(n)]
        + [jax.ShapeDtypeStruct((4,) + taps.shape, taps.dtype)],
        scratch_shapes=[pltpu.SemaphoreType.DMA((3 * n + 3,)), pltpu.SemaphoreType.DMA((3 * n + 3,)),
                        pltpu.SemaphoreType.DMA((3 * n,)), pltpu.SemaphoreType.DMA((3 * n,)),
                        pltpu.SemaphoreType.DMA((n + 1,)), pltpu.SemaphoreType.DMA((n + 1,))],
        compiler_params=pltpu.CompilerParams(has_side_effects=True),
    )(*shards, taps)
    return res[:n], res[n]


HBM = pl.BlockSpec(memory_space=pltpu.HBM)
SEM = pl.BlockSpec(memory_space=pltpu.SEMAPHORE)
DATAFLOW = pltpu.SideEffectType.DATAFLOW_SIDE_EFFECTING


def _in_hbm(a):
    return pltpu.with_memory_space_constraint(a, pltpu.HBM)


def _split_start(name, srcs, lands, n_copies, plan):
    n = len(srcs)

    def body(*refs):
        src_refs, land_refs = refs[:n], refs[n:2 * n]
        send_sems, recv_sems = refs[2 * n], refs[2 * n + 1]
        for i, (src, dst, dev) in enumerate(plan(src_refs, land_refs)):
            pltpu.make_async_remote_copy(src_ref=src, dst_ref=dst, send_sem=send_sems.at[i], recv_sem=recv_sems.at[i],
                                         device_id=dev, device_id_type=MESH).start()
        refs[-1][...] = jnp.zeros((8, 128), F32)

    res = pl.pallas_call(
        body, name=name, in_specs=[HBM] * (2 * n),
        out_specs=[SEM, SEM] + [HBM] * (2 * n) + [pl.BlockSpec(memory_space=pltpu.VMEM)],
        out_shape=[pltpu.SemaphoreType.DMA((n_copies,)), pltpu.SemaphoreType.DMA((n_copies,))]
        + [pltpu.HBM(a.shape, a.dtype) for a in list(srcs) + list(lands)] + [jax.ShapeDtypeStruct((8, 128), F32)],
        input_output_aliases={i: 2 + i for i in range(2 * n)},
        compiler_params=pltpu.CompilerParams(has_side_effects=DATAFLOW),
    )(*[_in_hbm(a) for a in list(srcs) + list(lands)])
    return res[0], res[1], list(res[2:2 + n]), list(res[2 + n:2 + 2 * n]), res[-1]


def _split_wait(name, send_sems, recv_sems, srcs, lands, after, plan):
    n = len(srcs)

    def body(*refs):
        src_refs, land_refs = refs[:n], refs[n:2 * n]
        send_ref, recv_ref = refs[2 * n], refs[2 * n + 1]
        for i, (src, dst, dev) in enumerate(plan(src_refs, land_refs)):
            cp = pltpu.make_async_remote_copy(src_ref=src, dst_ref=dst, send_sem=send_ref.at[i], recv_sem=recv_ref.at[i],
                                              device_id=dev, device_id_type=MESH)
            cp.wait_send()
            cp.wait_recv()

    res = pl.pallas_call(
        body, name=name, in_specs=[HBM] * (2 * n) + [SEM, SEM, ANY], out_specs=[HBM] * (2 * n),
        out_shape=[pltpu.HBM(a.shape, a.dtype) for a in list(srcs) + list(lands)],
        input_output_aliases={i: i for i in range(2 * n)},
        compiler_params=pltpu.CompilerParams(has_side_effects=DATAFLOW),
    )(*srcs, *lands, send_sems, recv_sems, after)
    return list(res[:n]), list(res[n:])


def _late_gather_plan(col_kind):
    def plan(src_refs, land_refs):
        x, y, c = _place()
        mine = 2 * x + y
        copies = []
        for a, (src, land) in enumerate(zip(src_refs, land_refs)):
            r, cs = src.shape
            if col_kind[a]:
                dst = land.at[:, pl.ds(pl.multiple_of(mine * cs, 128), cs)]
            else:
                dst = land.at[pl.ds(pl.multiple_of(mine * r, 16), r), :]
            copies.append((src, dst, (x, y, 1 - c)))
            copies += [(src, dst, (px, py, c)) for (px, py) in _other_chips(x, y)]
        return copies
    return plan


def _late_reduce_plan(col_kind):
    def plan(src_refs, land_refs):
        x, y, c = _place()
        copies = []
        for a, (src, land) in enumerate(zip(src_refs, land_refs)):
            for j, (px, py) in enumerate(_other_chips(x, y)):
                if col_kind[a]:
                    cs = land.shape[2]
                    piece = src.at[:, pl.ds(pl.multiple_of((2 * px + py) * cs, 128), cs)]
                else:
                    piece = src.at[2 * px + py]
                copies.append((piece, land.at[j], (px, py, c)))
        return copies
    return plan


def _pair_exchange_halves(name, grads, col_kind):
    n = len(grads)

    def body(*refs):
        ins, outs = refs[:n], refs[n:2 * n]
        send_sems, recv_sems = refs[2 * n:]
        x, y, c = _place()
        copies = []
        for a in range(n):
            if col_kind[a]:
                hr = ins[a].shape[0] // 2
                src = ins[a].at[pl.ds(pl.multiple_of((1 - c) * hr, 8), hr), :]
            else:
                hr = ins[a].shape[1] // 2
                src = ins[a].at[:, pl.ds(pl.multiple_of((1 - c) * hr, 8), hr), :]
            cp = pltpu.make_async_remote_copy(
                src_ref=src, dst_ref=outs[a], send_sem=send_sems.at[a], recv_sem=recv_sems.at[a],
                device_id=(x, y, 1 - c), device_id_type=MESH)
            cp.start()
            copies.append(cp)
        for cp in copies:
            cp.wait()

    def half_shape(a):
        s = grads[a].shape
        return (s[0] // 2, s[1]) if col_kind[a] else (4, s[1] // 2, s[2])

    return pl.pallas_call(
        body, name=name, in_specs=[ANY] * n, out_specs=[ANY] * n,
        out_shape=[jax.ShapeDtypeStruct(half_shape(a), grads[a].dtype) for a in range(n)],
        scratch_shapes=[pltpu.SemaphoreType.DMA((n,)), pltpu.SemaphoreType.DMA((n,))],
        compiler_params=pltpu.CompilerParams(has_side_effects=True),
    )(*grads)


def _pair_swap_halves(name, halves):
    n = len(halves)

    def body(*refs):
        ins, outs = refs[:n], refs[n:2 * n]
        send_sems, recv_sems = refs[2 * n:]
        x, y, c = _place()
        copies = []
        for a in range(n):
            cp = pltpu.make_async_remote_copy(
                src_ref=ins[a], dst_ref=outs[a], send_sem=send_sems.at[a], recv_sem=recv_sems.at[a],
                device_id=(x, y, 1 - c), device_id_type=MESH)
            cp.start()
            copies.append(cp)
        for cp in copies:
            cp.wait()

    return pl.pallas_call(
        body, name=name, in_specs=[ANY] * n, out_specs=[ANY] * n,
        out_shape=[jax.ShapeDtypeStruct(s.shape, s.dtype) for s in halves],
        scratch_shapes=[pltpu.SemaphoreType.DMA((n,)), pltpu.SemaphoreType.DMA((n,))],
        compiler_params=pltpu.CompilerParams(has_side_effects=True),
    )(*halves)


def _chip_sum(name, chip_sel, own, col, others):
    _, r, c = others.shape
    tr = _pick(r, (256, 128, 64, 32, 16))
    if col:
        own_spec = pl.BlockSpec((tr, c), lambda i, s: (i, s[0]))
    else:
        own_spec = pl.BlockSpec((None, tr, c), lambda i, s: (s[0], i, 0))
    specs = [own_spec] + [pl.BlockSpec((None, tr, c), lambda i, s, k=k: (k, i, 0)) for k in range(3)]

    def body(s_ref, own_ref, r0, r1, r2, o_ref):
        o_ref[...] = ((own_ref[...].astype(F32) + r0[...].astype(F32)) + r1[...].astype(F32)) + r2[...].astype(F32)

    return pl.pallas_call(
        body, name=name,
        grid_spec=pltpu.PrefetchScalarGridSpec(
            num_scalar_prefetch=1, grid=(r // tr,), in_specs=specs,
            out_specs=pl.BlockSpec((tr, c), lambda i, s: (i, 0))),
        out_shape=jax.ShapeDtypeStruct((r, c), F32),
        compiler_params=_params(("parallel",)),
    )(chip_sel, own, others, others, others)


def _pair_sum(name, c_sel, grad, recv, col):
    if col:
        r, c4 = grad.shape
        hr, c = r // 2, c4 // 4
    else:
        _, r, c = grad.shape
        hr = r // 2
    tr = _pick(hr, (256, 128, 64, 32, 16))
    nb = hr // tr

    def body(s_ref, g_ref, r_ref, o_ref):
        o_ref[...] = (g_ref[...] + r_ref[...]).astype(o_ref.dtype)

    if col:
        in_specs = [pl.BlockSpec((tr, c), lambda k, i, s: (s[0] * nb + i, k)), pl.BlockSpec((tr, c), lambda k, i, s: (i, k))]
        out_spec = pl.BlockSpec((tr, c), lambda k, i, s: (i, k))
    else:
        in_specs = [pl.BlockSpec((None, tr, c), lambda k, i, s: (k, s[0] * nb + i, 0)),
                    pl.BlockSpec((None, tr, c), lambda k, i, s: (k, i, 0))]
        out_spec = pl.BlockSpec((None, tr, c), lambda k, i, s: (k, i, 0))
    return pl.pallas_call(
        body, name=name,
        grid_spec=pltpu.PrefetchScalarGridSpec(num_scalar_prefetch=1, grid=(4, nb), in_specs=in_specs,
                                               out_specs=out_spec),
        out_shape=jax.ShapeDtypeStruct(recv.shape, BF16),
        compiler_params=_params(("parallel", "parallel")),
    )(c_sel, grad, recv)


def _allreduce_small(vals):
    sizes = [int(math.prod(v.shape)) for v in vals]
    padded = [-(-s // 128) * 128 for s in sizes]
    total = -(-sum(padded) // 1024) * 1024
    flat = [jnp.pad(v.reshape(-1), (0, p - s)) for v, s, p in zip(vals, sizes, padded)]
    flat.append(jnp.zeros((total - sum(padded),), F32))
    packed = jnp.concatenate(flat).reshape(total // 128, 128)

    def body(in_ref, out_ref, r0, r1, r2, send_sems, recv_sems):
        x, y, c = _place()
        out_ref[...] = in_ref[...]
        for k, (peer, land) in enumerate(zip([(x, y, 1 - c), (1 - x, y, c), (x, 1 - y, c)], (r0, r1, r2))):
            cp = pltpu.make_async_remote_copy(
                src_ref=out_ref, dst_ref=land, send_sem=send_sems.at[k], recv_sem=recv_sems.at[k],
                device_id=peer, device_id_type=MESH)
            cp.start()
            cp.wait()
            out_ref[...] = out_ref[...] + land[...]

    vm = pl.BlockSpec(memory_space=pltpu.VMEM)
    summed = pl.pallas_call(
        body, name="allreduce_small", in_specs=[vm], out_specs=vm,
        out_shape=jax.ShapeDtypeStruct(packed.shape, F32),
        scratch_shapes=[pltpu.VMEM(packed.shape, F32)] * 3
        + [pltpu.SemaphoreType.DMA((3,)), pltpu.SemaphoreType.DMA((3,))],
        compiler_params=pltpu.CompilerParams(has_side_effects=True, vmem_limit_bytes=VMEM_LIMIT_BYTES),
    )(packed).reshape(-1)
    outs, off = [], 0
    for v, s, p in zip(vals, sizes, padded):
        outs.append(summed[off:off + s].reshape(v.shape))
        off += p
    return outs


def _adamw_math(w, g, m, v):
    m2 = ADAM_B1 * m + (1.0 - ADAM_B1) * g
    v2 = ADAM_B2 * v + (1.0 - ADAM_B2) * (g * g)
    m_hat = m2 / (1.0 - ADAM_B1 ** ADAM_STEP)
    v_hat = v2 / (1.0 - ADAM_B2 ** ADAM_STEP)
    delta = -ADAM_LR * (m_hat / (jnp.sqrt(v_hat) + ADAM_EPS) + ADAM_WD * w)
    return delta, m2, v2


def _adamw_big(name, c_sel, w, g_mine, g_sibling, m, v):
    _, r, c = w.shape
    hr = r // 2
    tr = _pick(hr, (256, 128, 64, 32, 16, 8))
    nb = hr // tr

    def body(s_ref, w_ref, ga_ref, gb_ref, m_ref, v_ref, go_ref, d_ref, mo_ref, vo_ref):
        gv = jnp.where(pl.program_id(0) == s_ref[0], ga_ref[...], gb_ref[...])
        d, m2, v2 = _adamw_math(w_ref[...], gv, m_ref[...], v_ref[...])
        go_ref[...] = gv
        d_ref[...] = d
        mo_ref[...] = m2
        vo_ref[...] = v2

    blk = pl.BlockSpec((None, tr, c), lambda h, i, s: (0, h * nb + i, 0))
    half = pl.BlockSpec((tr, c), lambda h, i, s: (i, 0))
    return pl.pallas_call(
        body, name=name,
        grid_spec=pltpu.PrefetchScalarGridSpec(
            num_scalar_prefetch=1, grid=(2, nb), in_specs=[blk, half, half, blk, blk], out_specs=[blk] * 4),
        out_shape=[jax.ShapeDtypeStruct((1, r, c), F32)] * 4, compiler_params=_params(("parallel", "parallel")),
    )(c_sel, w, g_mine, g_sibling, m, v)


def _adamw_small(ws, gs, ms, vs):
    n = len(ws)

    def body(*refs):
        w_r, g_r, m_r, v_r = refs[:n], refs[n:2 * n], refs[2 * n:3 * n], refs[3 * n:4 * n]
        o = refs[4 * n:]
        for a in range(n):
            gv = g_r[a][...]
            d, m2, v2 = _adamw_math(w_r[a][...], gv, m_r[a][...], v_r[a][...])
            o[a][...] = gv
            o[n + a][...] = d
            o[2 * n + a][...] = m2
            o[3 * n + a][...] = v2

    res = pl.pallas_call(
        body, name="adamw_small", out_shape=[jax.ShapeDtypeStruct(w.shape, F32) for _ in range(4) for w in ws],
        compiler_params=_params(),
    )(*ws, *gs, *ms, *vs)
    return res[:n], res[n:2 * n], res[2 * n:3 * n], res[3 * n:]


def _full_from_gathered(name, gathered):
    if name == "w_in":
        rows = gathered.shape[0] // 4
        return gathered.reshape(4, rows, gathered.shape[1]).transpose(1, 0, 2).reshape(rows, 4 * gathered.shape[1])
    return gathered


def _reduce_layout(name, full):
    if name in COL_KIND:
        return full
    if name == "w_in":
        rows, cols = full.shape
        return full.reshape(rows, 4, cols // 4).transpose(1, 0, 2)
    return full.reshape(4, full.shape[0] // 4, full.shape[1])


def kernel(x, mem, norm_mix, w_in, fox_q_norm, fox_k_norm, fox_f_bias, s5_a_re, s5_a_im, s5_log_dt, s5_b_re, s5_b_im, s5_c_re, s5_c_im, s5_d, s5_w_glu, s5_b_glu, out_norm_fox, out_norm_s5, w_out, norm_cross, norm_mem, w_xq, w_xkv, xq_norm, xk_norm, w_xo, norm_ffn, w_ffn_up, ffn_conv_w, ffn_conv_b, w_ffn_down, loss_target, m_norm_mix, m_w_in, m_fox_q_norm, m_fox_k_norm, m_fox_f_bias, m_s5_a_re, m_s5_a_im, m_s5_log_dt, m_s5_b_re, m_s5_b_im, m_s5_c_re, m_s5_c_im, m_s5_d, m_s5_w_glu, m_s5_b_glu, m_out_norm_fox, m_out_norm_s5, m_w_out, m_norm_cross, m_norm_mem, m_w_xq, m_w_xkv, m_xq_norm, m_xk_norm, m_w_xo, m_norm_ffn, m_w_ffn_up, m_ffn_conv_w, m_ffn_conv_b, m_w_ffn_down, v_norm_mix, v_w_in, v_fox_q_norm, v_fox_k_norm, v_fox_f_bias, v_s5_a_re, v_s5_a_im, v_s5_log_dt, v_s5_b_re, v_s5_b_im, v_s5_c_re, v_s5_c_im, v_s5_d, v_s5_w_glu, v_s5_b_glu, v_out_norm_fox, v_out_norm_s5, v_w_out, v_norm_cross, v_norm_mem, v_w_xq, v_w_xkv, v_xq_norm, v_xk_norm, v_w_xo, v_norm_ffn, v_w_ffn_up, v_ffn_conv_w, v_ffn_conv_b, v_w_ffn_down):
    given = dict(locals())
    w = {n: given[n] for n in WEIGHTS}
    m = {n: given["m_" + n] for n in WEIGHTS}
    v = {n: given["v_" + n] for n in WEIGHTS}
    xi, yi, ci = _place()
    chip = (2 * xi + yi).astype(jnp.int32)

    c_sel = ci.astype(jnp.int32).reshape(1)
    chip_sel = chip.reshape(1)
    early_kind = [n in COL_KIND for n in EARLY_WEIGHTS]
    late_kind = [n in COL_KIND for n in LATE_WEIGHTS]

    gathered, taps = _gather_weights([w[n][0].astype(BF16) for n in EARLY_WEIGHTS], early_kind, w["ffn_conv_w"][0])
    wb = {n: _full_from_gathered(n, gathered[k]) for k, n in enumerate(EARLY_WEIGHTS)}
    conv_w = taps.transpose(1, 0, 2).reshape(3, D_FF)
    late_shards = [w[n][0].astype(BF16) for n in LATE_WEIGHTS]
    late_full = [lax.empty((s.shape[0], 4 * s.shape[1]) if ck else (4 * s.shape[0], s.shape[1]), BF16)
                 for s, ck in zip(late_shards, late_kind)]
    gather_plan = _late_gather_plan(late_kind)
    g_send, g_recv, g_srcs, g_lands, g_started = _split_start(
        "gather_late_start", late_shards, late_full, 4 * len(LATE_WEIGHTS), gather_plan)

    def late_weights(after):
        _, full = _split_wait("gather_late_wait", g_send, g_recv, g_srcs, g_lands, after, gather_plan)
        return dict(zip(LATE_WEIGHTS, full))

    reduce_plan = _late_reduce_plan(late_kind)
    late_reduce = {}

    def early_grads(late_g):
        grads = [_reduce_layout(n, late_g[n]) for n in LATE_WEIGHTS]
        from_sibling = _pair_exchange_halves("reduce_pair_exchange_late", grads, late_kind)
        sums = [_pair_sum("reduce_pair_sum_" + n, c_sel, gr, rv, ck)
                for n, gr, rv, ck in zip(LATE_WEIGHTS, grads, from_sibling, late_kind)]
        lands = [lax.empty((3, s.shape[0], s.shape[1] // 4) if ck else (3,) + s.shape[1:], BF16)
                 for s, ck in zip(sums, late_kind)]
        late_reduce["sems"] = _split_start("reduce_late_start", sums, lands, 3 * len(LATE_WEIGHTS), reduce_plan)
        return late_reduce["sems"][4][0:1, 0:1]

    p = {n: w[n][0] for n in SMALL}
    p["ffn_conv_w"] = conv_w
    for n in ("norm_mix", "fox_q_norm", "fox_k_norm", "fox_f_bias", "s5_b_glu", "out_norm_fox", "out_norm_s5",
              "norm_cross", "norm_mem", "xq_norm", "xk_norm", "norm_ffn", "ffn_conv_b"):
        p[n] = p[n].reshape(1, -1)
    p["norm_mix"] = p["norm_mix"] + g_started[0:1, 0:1]
    loss, grad_x, g = _local_step(x, mem, loss_target, p, wb, late_weights, early_grads)

    grads = [_reduce_layout(n, g[n]) for n in EARLY_WEIGHTS]
    from_sibling = _pair_exchange_halves("reduce_pair_exchange_early", grads, early_kind)
    pair_sums = [_pair_sum("reduce_pair_sum_" + n, c_sel, gr, rv, ck)
                 for n, gr, rv, ck in zip(EARLY_WEIGHTS, grads, from_sibling, early_kind)]
    early_lands = [lax.empty((3, s.shape[0], s.shape[1] // 4) if ck else (3,) + s.shape[1:], BF16)
                   for s, ck in zip(pair_sums, early_kind)]
    early_plan = _late_reduce_plan(early_kind)
    e_send, e_recv, e_srcs, e_lands, e_started = _split_start(
        "reduce_early_start", pair_sums, early_lands, 3 * len(EARLY_WEIGHTS), early_plan)

    small_names = list(SMALL) + ["ffn_conv_w"]
    small_vals = [g[n].reshape(w[n].shape if n != "ffn_conv_w" else (1, 3, D_FF)) for n in small_names]
    reduced = _allreduce_small(small_vals + [loss + e_started[0:1, 0:1]])
    loss_all = reduced[-1].reshape(())
    conv_w_grad = lax.dynamic_slice_in_dim(reduced[-2], chip * (D_FF // 4), D_FF // 4, axis=2)
    sg, sd, sm, sv = _adamw_small(
        [w[n] for n in small_names], list(reduced[:len(SMALL)]) + [conv_w_grad],
        [m[n] for n in small_names], [v[n] for n in small_names])
    out_g = dict(zip(small_names, sg))
    out_d = dict(zip(small_names, sd))
    out_m = dict(zip(small_names, sm))
    out_v = dict(zip(small_names, sv))

    def finish(names, kinds, sums, from_chips, tag):
        halves = [_chip_sum("reduce_chip_sum_" + n, chip_sel, ps, ck, fc)
                  for n, ps, fc, ck in zip(names, sums, from_chips, kinds)]
        sibling_halves = _pair_swap_halves("reduce_pair_swap_" + tag, halves)
        for n, mine, theirs in zip(names, halves, sibling_halves):
            out_g[n], out_d[n], out_m[n], out_v[n] = _adamw_big("adamw_" + n, c_sel, w[n], mine, theirs, m[n], v[n])

    r_send, r_recv, r_srcs, r_lands, _ = late_reduce["sems"]
    late_sums, late_from_chips = _split_wait("reduce_late_wait", r_send, r_recv, r_srcs, r_lands, reduced[0],
                                             reduce_plan)
    finish(LATE_WEIGHTS, late_kind, late_sums, late_from_chips, "late")
    early_sums, early_from_chips = _split_wait("reduce_early_wait", e_send, e_recv, e_srcs, e_lands,
                                               out_v[LATE_WEIGHTS[-1]], early_plan)
    finish(EARLY_WEIGHTS, early_kind, early_sums, early_from_chips, "early")

    return (loss_all, grad_x, *[out_g[n] for n in WEIGHTS], *[out_d[n] for n in WEIGHTS],
            *[out_m[n] for n in WEIGHTS], *[out_v[n] for n in WEIGHTS])
```

```python
import functools
import math

import jax
import jax.numpy as jnp
from jax import lax
from jax.experimental import pallas as pl
from jax.experimental.pallas import tpu as pltpu

F32 = jnp.float32
BF16 = jnp.bfloat16

D_MODEL = 1024
FOX_WIDTH = 512
HEAD_DIM = 64
N_FOX_HEADS = 8
S5_WIDTH = 512
S5_GROUP_CH = 16
S5_GROUPS = 32
S5_STATE = 64
S5_CH = S5_GROUPS * S5_STATE
N_X_HEADS = 4
X_HEAD_DIM = 256
N_MEM = 256
D_FF = 2816
UF_COLS = 640
EPS = 1e-6
ADAM_LR = 0.001
ADAM_B1 = 0.9
ADAM_B2 = 0.999
ADAM_EPS = 1e-08
ADAM_WD = 0.01
ADAM_STEP = 10

VMEM_LIMIT_BYTES = 56 * 1024 * 1024
MM_BLOCK_BYTES = 6 * 1024 * 1024
MM_VMEM_BYTES = 40 * 1024 * 1024
MESH = pl.DeviceIdType.MESH

EARLY_WEIGHTS = ("w_in", "s5_w_glu", "w_out")
LATE_WEIGHTS = ("w_xq", "w_xkv", "w_xo", "w_ffn_up", "w_ffn_down")
BIG = EARLY_WEIGHTS + LATE_WEIGHTS
COL_KIND = ("w_xkv", "w_ffn_up")
SMALL = ("norm_mix", "fox_q_norm", "fox_k_norm", "fox_f_bias", "s5_a_re", "s5_a_im", "s5_log_dt",
         "s5_b_re", "s5_b_im", "s5_c_re", "s5_c_im", "s5_d", "s5_b_glu", "out_norm_fox", "out_norm_s5",
         "norm_cross", "norm_mem", "xq_norm", "xk_norm", "norm_ffn", "ffn_conv_b")
WEIGHTS = ("norm_mix", "w_in", "fox_q_norm", "fox_k_norm", "fox_f_bias", "s5_a_re", "s5_a_im", "s5_log_dt",
           "s5_b_re", "s5_b_im", "s5_c_re", "s5_c_im", "s5_d", "s5_w_glu", "s5_b_glu", "out_norm_fox",
           "out_norm_s5", "w_out", "norm_cross", "norm_mem", "w_xq", "w_xkv", "xq_norm", "xk_norm", "w_xo",
           "norm_ffn", "w_ffn_up", "ffn_conv_w", "ffn_conv_b", "w_ffn_down")


def _params(sem=None):
    return pltpu.CompilerParams(dimension_semantics=sem, vmem_limit_bytes=VMEM_LIMIT_BYTES)


def _pick(n, cands):
    for c in cands:
        if n % c == 0:
            return c
    return n


_DIMS = {"nn": (((1,), (0,)), ((), ())), "nt": (((1,), (1,)), ((), ())), "tn": (((0,), (0,)), ((), ()))}


def _mm(a, b, mode, name, out_dtype=F32, res=None):
    if mode == "nn":
        (m, k), (k2, n) = a.shape, b.shape
    elif mode == "nt":
        (m, k), (n, k2) = a.shape, b.shape
    else:
        (k, m), (k2, n) = a.shape, b.shape
    assert k == k2, (name, a.shape, b.shape)

    has_res = res is not None
    a_size, b_size = a.dtype.itemsize, b.dtype.itemsize
    o_size = jnp.dtype(out_dtype).itemsize + (res.dtype.itemsize if has_res else 0)

    def tiles(dim):
        return [c for c in (1024, 512, 256, 128) if dim % c == 0] or [dim]

    best = None
    for tm in tiles(m):
        for tn in tiles(n):
            a_blk, b_blk = tm * k * a_size, tn * k * b_size
            if max(a_blk, b_blk) > MM_BLOCK_BYTES or 2 * (a_blk + b_blk + tm * tn * o_size) > MM_VMEM_BYTES:
                continue
            for rows_outer in (True, False):
                moved = (m * k * a_size + (m // tm) * n * k * b_size) if rows_outer else \
                        (n * k * b_size + (n // tn) * m * k * a_size)
                key = (moved, -(tm * tn))
                if best is None or key < best[0]:
                    best = (key, tm, tn, rows_outer)
    assert best is not None, (name, a.shape, b.shape)
    _, tm, tn, rows_outer = best
    ij = (lambda g0, g1: (g0, g1)) if rows_outer else (lambda g0, g1: (g1, g0))
    if mode == "tn":
        a_spec = pl.BlockSpec((k, tm), lambda g0, g1: (0, ij(g0, g1)[0]))
    else:
        a_spec = pl.BlockSpec((tm, k), lambda g0, g1: (ij(g0, g1)[0], 0))
    if mode == "nt":
        b_spec = pl.BlockSpec((tn, k), lambda g0, g1: (ij(g0, g1)[1], 0))
    else:
        b_spec = pl.BlockSpec((k, tn), lambda g0, g1: (0, ij(g0, g1)[1]))
    o_spec = pl.BlockSpec((tm, tn), lambda g0, g1: ij(g0, g1))
    grid = (m // tm, n // tn) if rows_outer else (n // tn, m // tm)
    dims = _DIMS[mode]

    def body(*refs):
        a_ref, b_ref = refs[0], refs[1]
        o_ref = refs[-1]
        acc = lax.dot_general(a_ref[...].astype(BF16), b_ref[...].astype(BF16), dims, preferred_element_type=F32)
        if has_res:
            acc = acc + refs[2][...].astype(F32)
        o_ref[...] = acc.astype(o_ref.dtype)

    return pl.pallas_call(
        body, name=name, grid=grid,
        in_specs=[a_spec, b_spec] + ([o_spec] if has_res else []),
        out_specs=o_spec, out_shape=jax.ShapeDtypeStruct((m, n), out_dtype),
        compiler_params=_params(("parallel", "parallel")),
    )(*((a, b, res) if has_res else (a, b)))


def _row_spec(tm, bc, off, step):
    return pl.BlockSpec((tm, bc), lambda i, h: (i, off + step * h))


ROW_TILE_ELEMS = 512 * 1024


def _row_tile(t, rows):
    widest = max(bc for (_, bc, _, _) in rows)
    return _pick(t, (min(t, ROW_TILE_ELEMS // widest), 512, 256, 128, 64, 8))


def _rowwise(fn, rows, pars, outs, name, heads=1):
    t = rows[0][0].shape[0]
    tm = _row_tile(t, rows)
    nr, npar = len(rows), len(pars)

    def body(*refs):
        vals = [r[...].astype(F32) for r in refs[:nr + npar]]
        res = fn(*vals)
        if not isinstance(res, (tuple, list)):
            res = (res,)
        for o_ref, v in zip(refs[nr + npar:], res):
            o_ref[...] = v.astype(o_ref.dtype)

    in_specs = [_row_spec(tm, bc, off, st) for (_, bc, off, st) in rows]
    in_specs += [pl.BlockSpec(p.shape, lambda i, h: (0, 0)) for p in pars]
    out_specs = [_row_spec(tm, bc, 0, st) for (_, bc, st, _) in outs]
    out_shape = [jax.ShapeDtypeStruct((t, c), dt) for (c, _, _, dt) in outs]
    res = pl.pallas_call(
        body, name=name, grid=(t // tm, heads), in_specs=in_specs, out_specs=out_specs, out_shape=out_shape,
        compiler_params=_params(("parallel", "parallel")),
    )(*[r[0] for r in rows], *pars)
    return res[0] if len(res) == 1 else res


def _rowwise_vjp(fn, rows, pars, cts, name, heads=1, adds=None, row_dtypes=None):
    t = rows[0][0].shape[0]
    tm = _row_tile(t, rows)
    nr, npar, nct = len(rows), len(pars), len(cts)
    adds = adds or [None] * nr
    add_list = [a for a in adds if a is not None]
    row_dtypes = row_dtypes or [F32] * nr

    def body(*refs):
        i, h = pl.program_id(0), pl.program_id(1)
        p = 0
        row_v = [r[...].astype(F32) for r in refs[p:p + nr]]; p += nr
        par_v = [r[...].astype(F32) for r in refs[p:p + npar]]; p += npar
        ct_v = [r[...].astype(F32) for r in refs[p:p + nct]]; p += nct
        add_refs = refs[p:p + len(add_list)]; p += len(add_list)
        drow_refs = refs[p:p + nr]; p += nr
        dpar_refs = refs[p:p + npar]

        def wrapped(*a):
            r = fn(*a)
            return tuple(r) if isinstance(r, (tuple, list)) else (r,)

        _, pull = jax.vjp(wrapped, *row_v, *par_v)
        grads = pull(tuple(ct_v))
        ai = 0
        for k in range(nr):
            g = grads[k]
            if adds[k] is not None:
                g = g + add_refs[ai][...].astype(F32)
                ai += 1
            drow_refs[k][...] = g.astype(drow_refs[k].dtype)

        @pl.when((i == 0) & (h == 0))
        def _():
            for r in dpar_refs:
                r[...] = jnp.zeros(r.shape, r.dtype)

        for k in range(npar):
            dpar_refs[k][...] += grads[nr + k]

    in_specs = [_row_spec(tm, bc, off, st) for (_, bc, off, st) in rows]
    in_specs += [pl.BlockSpec(q.shape, lambda i, h: (0, 0)) for q in pars]
    in_specs += [_row_spec(tm, bc, off, st) for (_, bc, off, st) in cts]
    in_specs += [_row_spec(tm, bc, off, st) for (_, bc, off, st) in add_list]
    out_specs = [_row_spec(tm, bc, 0, st) for (_, bc, _, st) in rows]
    out_specs += [pl.BlockSpec(q.shape, lambda i, h: (0, 0)) for q in pars]
    out_shape = [jax.ShapeDtypeStruct((t, bc * (heads if st else 1)), dt) for (_, bc, _, st), dt in zip(rows, row_dtypes)]
    out_shape += [jax.ShapeDtypeStruct(q.shape, F32) for q in pars]
    res = pl.pallas_call(
        body, name=name, grid=(t // tm, heads), in_specs=in_specs, out_specs=out_specs, out_shape=out_shape,
        compiler_params=_params(("arbitrary", "arbitrary")),
    )(*[r[0] for r in rows], *pars, *[c[0] for c in cts], *[a[0] for a in add_list])
    return list(res[:nr]), list(res[nr:])


def _rms(x, g):
    return x * lax.rsqrt(jnp.mean(x * x, axis=-1, keepdims=True) + EPS) * g


def _rms_pair(x, g):
    left = lax.broadcasted_iota(jnp.int32, x.shape, 1) < HEAD_DIM
    x2 = x * x
    ms_a = jnp.sum(jnp.where(left, x2, 0.0), axis=-1, keepdims=True) * (1.0 / HEAD_DIM)
    ms_b = jnp.sum(jnp.where(left, 0.0, x2), axis=-1, keepdims=True) * (1.0 / HEAD_DIM)
    return x * lax.rsqrt(jnp.where(left, ms_a, ms_b) + EPS) * g


def _gelu(x):
    return 0.5 * x * (1.0 + jnp.tanh(math.sqrt(2.0 / math.pi) * (x + 0.044715 * (x * x * x))))


def _s5_act(ys, u, d):
    return _gelu(ys + d * u)


def _s5_gate(yg, z, b, g):
    return _rms(yg * jax.nn.sigmoid(z + b), g)


def _lane_cumsum(x, reverse):
    n = x.shape[-1]
    lane = lax.broadcasted_iota(jnp.int32, x.shape, 1)
    k = 1
    while k < n:
        if reverse:
            x = x + jnp.where(lane < n - k, pltpu.roll(x, n - k, 1), 0.0)
        else:
            x = x + jnp.where(lane >= k, pltpu.roll(x, k, 1), 0.0)
        k *= 2
    return x


def _log_sigmoid(z):
    return jnp.minimum(z, 0.0) - jnp.log(1.0 + jnp.exp(-jnp.abs(z)))


def _forget_fwd(f, bias):
    def body(f_ref, b_ref, c_ref):
        c_ref[...] = _lane_cumsum(_log_sigmoid(f_ref[...] + b_ref[...]), False)

    return pl.pallas_call(body, name="forget_fwd", out_shape=jax.ShapeDtypeStruct(f.shape, F32),
                          compiler_params=_params())(f, bias)


def _forget_bwd(f, bias, dc):
    def body(f_ref, b_ref, dc_ref, df_ref, db_ref):
        dlog = _lane_cumsum(dc_ref[...], True)
        df = dlog * jax.nn.sigmoid(-(f_ref[...] + b_ref[...]))
        df_ref[...] = df
        db_ref[...] = jnp.sum(df, axis=1, keepdims=True)

    return pl.pallas_call(body, name="forget_bwd",
                          out_shape=(jax.ShapeDtypeStruct(f.shape, F32), jax.ShapeDtypeStruct(bias.shape, F32)),
                          compiler_params=_params())(f, bias, dc)


FOX_BLOCK = 256
FOX_KEYS = 256
_NT = _DIMS["nt"]
_TN = _DIMS["tn"]


N_PAIRS = N_FOX_HEADS // 2
V_BLOCK0 = 2 * N_PAIRS


def _left_lanes(shape):
    return lax.broadcasted_iota(jnp.int32, shape, 1) < HEAD_DIM


def _top_rows(shape):
    return lax.broadcasted_iota(jnp.int32, shape, 0) < HEAD_DIM


def _wide(c_tile, n):
    return c_tile if n == 128 else jnp.concatenate([c_tile] * (n // 128), axis=1)


def _fox_fwd(qn, kn, qkv, c_wide, seqs):
    t = qn.shape[0]
    l = t // seqs
    tb = min(FOX_BLOCK, l)
    tk = min(FOX_KEYS, tb)
    ratio = tb // tk
    nb = l // tb
    scale = HEAD_DIM ** -0.5

    def body(q_ref, k_ref, v_ref, ca_ref, cb_ref, o_ref, lse_ref, vt_ref):
        i = pl.program_id(2)
        top = _top_rows((128, tb))

        @pl.when(i == 0)
        def _():
            vt_ref[...] = v_ref[...].T.astype(BF16)

        qt = (q_ref[...].astype(F32) * scale).T.astype(BF16)
        zero = jnp.zeros_like(qt)
        qts = (jnp.where(top, qt, zero), jnp.where(top, zero, qt))
        top_k = _top_rows((128, tk))
        zero_k = jnp.zeros((128, tk), BF16)
        key_pos = lax.broadcasted_iota(jnp.int32, (tk, tb), 0)
        query_pos = lax.broadcasted_iota(jnp.int32, (tk, tb), 1)
        c_refs = (ca_ref, cb_ref)

        def scores(j):
            off = pl.multiple_of(j * tk, tk)
            k2 = k_ref[pl.ds(off, tk), :]
            return tuple(jnp.dot(k2, qts[h], preferred_element_type=F32) - _wide(c_refs[h][pl.ds(off, tk), :], tb)
                         for h in (0, 1))

        def values_times(ps, j):
            vt = vt_ref[:, pl.ds(pl.multiple_of(j * tk, tk), tk)]
            return (jnp.dot(jnp.where(top_k, vt, zero_k), ps[0], preferred_element_type=F32)
                    + jnp.dot(jnp.where(top_k, zero_k, vt), ps[1], preferred_element_type=F32))

        def softmax_step(sts, stats, first_key):
            ps, new, alphas = [], [], []
            for st, (m, s_sum) in zip(sts, stats):
                if first_key is not None:
                    st = jnp.where(key_pos + first_key <= query_pos, st, -jnp.inf)
                m_new = jnp.maximum(m, jnp.max(st, axis=0, keepdims=True))
                alpha = jnp.exp(m - m_new)
                p = jnp.exp(st - m_new)
                new.append((m_new, alpha * s_sum + jnp.sum(p, axis=0, keepdims=True)))
                alphas.append(alpha)
                ps.append(p.astype(BF16))
            return tuple(ps), tuple(new), jnp.where(top, alphas[0], alphas[1])

        def step(j, carry):
            sts, ps_prev, stats, acc = carry
            sts_next = scores(j + 1)
            acc = acc + values_times(ps_prev, jnp.maximum(j - 1, 0))
            ps, stats, alpha = softmax_step(sts, stats, None)
            return sts_next, ps, stats, alpha * acc

        stat = (jnp.full((1, tb), -jnp.inf, F32), jnp.zeros((1, tb), F32))
        no_p = jnp.zeros((tk, tb), BF16)
        below = i * ratio
        sts, ps_prev, stats, acc = lax.fori_loop(
            0, below, step, (scores(0), (no_p, no_p), (stat, stat), jnp.zeros((128, tb), F32)))
        for r in range(ratio):
            sts_next = scores(below + r + 1) if r + 1 < ratio else None
            acc = acc + values_times(ps_prev, jnp.maximum(below + r - 1, 0))
            ps_prev, stats, alpha = softmax_step(sts, stats, r * tk)
            acc = alpha * acc
            sts = sts_next
        acc = acc + values_times(ps_prev, below + ratio - 1)
        (ma, sa), (mb, sb) = stats
        o_ref[...] = (acc / jnp.where(top, sa, sb)).T
        lse_ref[0:1, :] = ma + jnp.log(sa)
        lse_ref[1:2, :] = mb + jnp.log(sb)

    qblk = pl.BlockSpec((tb, 128), lambda b, hp, i: (b * nb + i, hp))
    return pl.pallas_call(
        body, name="fox_fwd", grid=(seqs, N_PAIRS, nb),
        in_specs=[qblk, pl.BlockSpec((l, 128), lambda b, hp, i: (b, hp)),
                  pl.BlockSpec((l, 128), lambda b, hp, i: (b, V_BLOCK0 + hp)),
                  pl.BlockSpec((None, l, 128), lambda b, hp, i: (b * N_FOX_HEADS + 2 * hp, 0, 0)),
                  pl.BlockSpec((None, l, 128), lambda b, hp, i: (b * N_FOX_HEADS + 2 * hp + 1, 0, 0))],
        out_specs=[qblk, pl.BlockSpec((None, 2, tb), lambda b, hp, i: (b * N_PAIRS + hp, 0, i))],
        out_shape=[jax.ShapeDtypeStruct((t, FOX_WIDTH), F32), jax.ShapeDtypeStruct((seqs * N_PAIRS, 2, l), F32)],
        scratch_shapes=[pltpu.VMEM((128, l), BF16)],
        compiler_params=_params(("parallel", "parallel", "arbitrary")),
    )(qn, kn, qkv, c_wide, c_wide)


def _fox_bwd(qn, kn, qkv, c_wide, o, do, lse, seqs):
    t = qn.shape[0]
    l = t // seqs
    tb = min(FOX_BLOCK, l)
    nb = l // tb
    scale = HEAD_DIM ** -0.5
    one_at = (HEAD_DIM, 0)

    def body(q_ref, k_ref, v_ref, ca_ref, cb_ref, o_ref, do_ref, lse_ref, dq_ref, dk_ref, dv_ref, dc_ref, dcq_ref,
             qt_ref, kt_ref, dot_ref, delta_ref, dqa_ref, dqb_ref):
        top_l = _top_rows((128, l))
        top = _top_rows((128, tb))
        left = _left_lanes((tb, 128))
        row_id = lax.broadcasted_iota(jnp.int32, (128, tb), 0)
        lane_id = lax.broadcasted_iota(jnp.int32, (tb, 128), 1)
        zero_t = jnp.zeros((128, tb), BF16)
        zero_l = jnp.zeros((tb, 128), BF16)
        rows = lambda a: (jnp.where(top, a, zero_t), jnp.where(top, zero_t, a))
        lanes = lambda a: (jnp.where(left, a, zero_l), jnp.where(left, zero_l, a))
        with_one_row = lambda pair: tuple(jnp.where(row_id == one_at[h], 1.0, pair[h]).astype(BF16) for h in (0, 1))
        with_one_lane = lambda pair: tuple(jnp.where(lane_id == one_at[h], 1.0, pair[h]).astype(BF16) for h in (0, 1))
        causal = lax.broadcasted_iota(jnp.int32, (tb, tb), 0) <= lax.broadcasted_iota(jnp.int32, (tb, tb), 1)
        c_refs = (ca_ref, cb_ref)
        dq_refs = (dqa_ref, dqb_ref)

        qt_ref[...] = (q_ref[...].astype(F32) * scale).T.astype(BF16)
        kt_ref[...] = k_ref[...].astype(F32).T.astype(BF16)
        do_t = do_ref[...].T
        dot_ref[...] = do_t.astype(BF16)
        prod_t = do_t * o_ref[...].T
        delta_ref[0:1, :] = jnp.sum(jnp.where(top_l, prod_t, 0.0), axis=0, keepdims=True)
        delta_ref[1:2, :] = jnp.sum(jnp.where(top_l, 0.0, prod_t), axis=0, keepdims=True)
        dqa_ref[...] = jnp.zeros(dqa_ref.shape, F32)
        dqb_ref[...] = jnp.zeros(dqb_ref.shape, F32)

        def kv_block(j, _):
            koff = pl.multiple_of(j * tb, tb)
            k2 = k_ref[pl.ds(koff, tb), :]
            v2 = v_ref[pl.ds(koff, tb), :].astype(BF16)
            kts = with_one_row(rows(kt_ref[:, pl.ds(koff, tb)]))
            cw = tuple(_wide(c_refs[h][pl.ds(koff, tb), :], tb) for h in (0, 1))

            def q_block(i, carry, masked):
                dks, dv = list(carry[:2]), carry[2]
                qoff = pl.multiple_of(i * tb, tb)
                qs = lanes((q_ref[pl.ds(qoff, tb), :].astype(F32) * scale).astype(BF16))
                qs_one = with_one_lane(qs)
                dos = lanes(do_ref[pl.ds(qoff, tb), :].astype(BF16))
                qts = rows(qt_ref[:, pl.ds(qoff, tb)])
                dots = rows(dot_ref[:, pl.ds(qoff, tb)])
                for h in (0, 1):
                    st = jnp.dot(k2, qts[h], preferred_element_type=F32) - cw[h]
                    p = jnp.exp(st - lse_ref[h:h + 1, pl.ds(qoff, tb)])
                    if masked:
                        p = jnp.where(causal, p, 0.0)
                    dp = jnp.dot(v2, dots[h], preferred_element_type=F32)
                    dsb = (p * (dp - delta_ref[h:h + 1, pl.ds(qoff, tb)])).astype(BF16)
                    dv = dv + jnp.dot(p.astype(BF16), dos[h], preferred_element_type=F32)
                    dks[h] = dks[h] + jnp.dot(dsb, qs_one[h], preferred_element_type=F32)
                    dq_refs[h][:, pl.ds(qoff, tb)] += jnp.dot(kts[h], dsb, preferred_element_type=F32)
                return dks[0], dks[1], dv

            z = jnp.zeros((tb, 128), F32)
            carry = q_block(j, (z, z, z), True)
            dka, dkb, dv = lax.fori_loop(j + 1, nb, lambda i, c: q_block(i, c, False), carry)
            dk_ref[pl.ds(koff, tb), :] = jnp.where(left, dka, dkb)
            dv_ref[pl.ds(koff, tb), :] = dv
            dc_ref[0:1, pl.ds(koff, tb)] = -dka.T[one_at[0]:one_at[0] + 1, :]
            dc_ref[1:2, pl.ds(koff, tb)] = -dkb.T[one_at[1]:one_at[1] + 1, :]
            return 0

        lax.fori_loop(0, nb, kv_block, 0)
        dq_ref[...] = (jnp.where(top_l, dqa_ref[...], dqb_ref[...]) * scale).T
        dcq_ref[0:1, :] = dqa_ref[one_at[0]:one_at[0] + 1, :]
        dcq_ref[1:2, :] = dqb_ref[one_at[1]:one_at[1] + 1, :]

    blk = pl.BlockSpec((l, 128), lambda b, hp: (b, hp))
    cspec = lambda k: pl.BlockSpec((None, l, 128), lambda b, hp: (b * N_FOX_HEADS + 2 * hp + k, 0, 0))
    rows2 = pl.BlockSpec((None, 2, l), lambda b, hp: (b * N_PAIRS + hp, 0, 0))
    wide = jax.ShapeDtypeStruct((t, FOX_WIDTH), F32)
    pair_rows = jax.ShapeDtypeStruct((seqs * N_PAIRS, 2, l), F32)
    return pl.pallas_call(
        body, name="fox_bwd", grid=(seqs, N_PAIRS),
        in_specs=[blk, blk, pl.BlockSpec((l, 128), lambda b, hp: (b, V_BLOCK0 + hp)), cspec(0), cspec(1), blk, blk, rows2],
        out_specs=[blk, blk, blk, rows2, rows2],
        out_shape=[wide, wide, wide, pair_rows, pair_rows],
        scratch_shapes=[pltpu.VMEM((128, l), BF16), pltpu.VMEM((128, l), BF16), pltpu.VMEM((128, l), BF16),
                        pltpu.VMEM((2, l), F32), pltpu.VMEM((128, l), F32), pltpu.VMEM((128, l), F32)],
        compiler_params=_params(("parallel", "parallel")),
    )(qn, kn, qkv, c_wide, c_wide, o, do, lse)


SCAN_ROWS = 256
SCAN_COLS = 1024


S5_IN = 128
S5_ST = 512
SCAN_CHUNKS = SCAN_COLS // S5_ST


def _s5_fwd(uf, bbr, bbi, cr, ci, ar, ai, seqs):
    t = uf.shape[0]
    l = t // seqs
    tl = min(SCAN_ROWS, l)
    nl = l // tl
    cb, nq = SCAN_COLS, SCAN_CHUNKS

    def body(u_ref, bbr_ref, bbi_ref, cr_ref, ci_ref, ar_ref, ai_ref, xr_ref, xi_ref, ys_ref, car_r, car_i, bu_r, bu_i):
        @pl.when(pl.program_id(2) == 0)
        def _():
            car_r[...] = jnp.zeros(car_r.shape, F32)
            car_i[...] = jnp.zeros(car_i.shape, F32)

        u = u_ref[...].astype(BF16)
        for q in range(nq):
            uq = u[:, q * S5_IN:(q + 1) * S5_IN]
            bu_r[:, q * S5_ST:(q + 1) * S5_ST] = jnp.dot(uq, bbr_ref[q], preferred_element_type=F32)
            bu_i[:, q * S5_ST:(q + 1) * S5_ST] = jnp.dot(uq, bbi_ref[q], preferred_element_type=F32)
        a_r, a_i = ar_ref[...], ai_ref[...]

        def step(tt, carry):
            xr, xi = carry
            nr = a_r * xr - a_i * xi + bu_r[pl.ds(tt, 1), :]
            ni = a_r * xi + a_i * xr + bu_i[pl.ds(tt, 1), :]
            xr_ref[pl.ds(tt, 1), :] = nr
            xi_ref[pl.ds(tt, 1), :] = ni
            return nr, ni

        xr, xi = lax.fori_loop(0, tl, step, (car_r[...], car_i[...]), unroll=8)
        car_r[...] = xr
        car_i[...] = xi
        for q in range(nq):
            xq_r = xr_ref[:, q * S5_ST:(q + 1) * S5_ST].astype(BF16)
            xq_i = xi_ref[:, q * S5_ST:(q + 1) * S5_ST].astype(BF16)
            ys_ref[:, q * S5_IN:(q + 1) * S5_IN] = (jnp.dot(xq_r, cr_ref[q], preferred_element_type=F32)
                                                    + jnp.dot(xq_i, ci_ref[q], preferred_element_type=F32))

    rows = lambda w: pl.BlockSpec((tl, w), lambda s, j, r: (s * nl + r, j))
    chunk = lambda a: pl.BlockSpec((nq,) + a.shape[1:], lambda s, j, r: (j, 0, 0))
    par = pl.BlockSpec((1, cb), lambda s, j, r: (0, j))
    return pl.pallas_call(
        body, name="s5_fwd", grid=(seqs, S5_CH // cb, nl),
        in_specs=[rows(nq * S5_IN), chunk(bbr), chunk(bbi), chunk(cr), chunk(ci), par, par],
        out_specs=[rows(cb), rows(cb), rows(nq * S5_IN)],
        out_shape=[jax.ShapeDtypeStruct((t, S5_CH), F32)] * 2 + [jax.ShapeDtypeStruct((t, S5_WIDTH), F32)],
        scratch_shapes=[pltpu.VMEM((1, cb), F32), pltpu.VMEM((1, cb), F32), pltpu.VMEM((tl, cb), F32),
                        pltpu.VMEM((tl, cb), F32)],
        compiler_params=_params(("parallel", "parallel", "arbitrary")),
    )(uf, bbr, bbi, cr, ci, ar, ai)


def _s5_bwd(dys, uf, xr, xi, bbr, bbi, cr, ci, ar, ai, seqs):
    t = dys.shape[0]
    l = t // seqs
    tl = min(SCAN_ROWS, l)
    nl = l // tl
    cb, nq = SCAN_COLS, SCAN_CHUNKS

    def body(dy_ref, u_ref, xr_ref, xi_ref, bbr_ref, bbi_ref, cr_ref, ci_ref, ar_ref, ai_ref,
             du_ref, dbbr_ref, dbbi_ref, dcr_ref, dci_ref, dar_ref, dai_ref, car_r, car_i, lam_r, lam_i):
        @pl.when(pl.program_id(2) == 0)
        def _():
            car_r[...] = jnp.zeros(car_r.shape, F32)
            car_i[...] = jnp.zeros(car_i.shape, F32)
            for acc_ref in (dbbr_ref, dbbi_ref, dcr_ref, dci_ref, dar_ref, dai_ref):
                acc_ref[...] = jnp.zeros(acc_ref.shape, F32)

        dy = dy_ref[...]
        for q in range(nq):
            dyq = dy[:, q * S5_IN:(q + 1) * S5_IN]
            lam_r[:, q * S5_ST:(q + 1) * S5_ST] = lax.dot_general(dyq, cr_ref[q], _NT, preferred_element_type=F32)
            lam_i[:, q * S5_ST:(q + 1) * S5_ST] = lax.dot_general(dyq, ci_ref[q], _NT, preferred_element_type=F32)
        a_r, a_i = ar_ref[...], ai_ref[...]

        def step(k, carry):
            lr, li, dar, dai = carry
            tt = tl - 1 - k
            xr_t = xr_ref[pl.ds(tt, 1), :]
            xi_t = xi_ref[pl.ds(tt, 1), :]
            dar = dar + lr * xr_t + li * xi_t
            dai = dai + li * xr_t - lr * xi_t
            nr = lam_r[pl.ds(tt, 1), :] + a_r * lr + a_i * li
            ni = lam_i[pl.ds(tt, 1), :] + a_r * li - a_i * lr
            lam_r[pl.ds(tt, 1), :] = nr
            lam_i[pl.ds(tt, 1), :] = ni
            return nr, ni, dar, dai

        lr, li, dar, dai = lax.fori_loop(
            0, tl, step, (car_r[...], car_i[...], jnp.zeros((1, cb), F32), jnp.zeros((1, cb), F32)), unroll=8)
        car_r[...] = lr
        car_i[...] = li
        dar_ref[...] += dar
        dai_ref[...] += dai
        u = u_ref[...].astype(BF16)
        for q in range(nq):
            st = slice(q * S5_ST, (q + 1) * S5_ST)
            io = slice(q * S5_IN, (q + 1) * S5_IN)
            lq_r, lq_i = lam_r[:, st].astype(BF16), lam_i[:, st].astype(BF16)
            du_ref[:, io] = (lax.dot_general(lq_r, bbr_ref[q], _NT, preferred_element_type=F32)
                             + lax.dot_general(lq_i, bbi_ref[q], _NT, preferred_element_type=F32))
            dbbr_ref[q] += lax.dot_general(u[:, io], lq_r, _TN, preferred_element_type=F32)
            dbbi_ref[q] += lax.dot_general(u[:, io], lq_i, _TN, preferred_element_type=F32)
            dcr_ref[q] += lax.dot_general(xr_ref[:, st].astype(BF16), dy[:, io], _TN, preferred_element_type=F32)
            dci_ref[q] += lax.dot_general(xi_ref[:, st].astype(BF16), dy[:, io], _TN, preferred_element_type=F32)

    rows = lambda w: pl.BlockSpec((tl, w), lambda s, j, r: (s * nl + nl - 1 - r, j))
    chunk = lambda a: pl.BlockSpec((nq,) + a.shape[1:], lambda s, j, r: (j, 0, 0))
    acc = lambda a: pl.BlockSpec((None, nq) + a.shape[1:], lambda s, j, r: (s, j, 0, 0))
    par = pl.BlockSpec((1, cb), lambda s, j, r: (0, j))
    par_acc = pl.BlockSpec((None, 1, cb), lambda s, j, r: (s, 0, j))
    per_seq = lambda a: jax.ShapeDtypeStruct((seqs,) + a.shape, F32)
    return pl.pallas_call(
        body, name="s5_bwd", grid=(seqs, S5_CH // cb, nl),
        in_specs=[rows(nq * S5_IN), rows(nq * S5_IN), rows(cb), rows(cb), chunk(bbr), chunk(bbi), chunk(cr), chunk(ci),
                  par, par],
        out_specs=[rows(nq * S5_IN), acc(bbr), acc(bbi), acc(cr), acc(ci), par_acc, par_acc],
        out_shape=[jax.ShapeDtypeStruct((t, S5_WIDTH), F32), per_seq(bbr), per_seq(bbi), per_seq(cr), per_seq(ci),
                   jax.ShapeDtypeStruct((seqs, 1, S5_CH), F32), jax.ShapeDtypeStruct((seqs, 1, S5_CH), F32)],
        scratch_shapes=[pltpu.VMEM((1, cb), F32), pltpu.VMEM((1, cb), F32), pltpu.VMEM((tl, cb), F32),
                        pltpu.VMEM((tl, cb), F32)],
        compiler_params=_params(("parallel", "parallel", "arbitrary")),
    )(dys, uf, xr, xi, bbr, bbi, cr, ci, ar, ai)


XATT_BLOCK = 512


def _xatt_probs(qv, kv):
    s = lax.dot_general(qv, kv, _NT, preferred_element_type=F32) * (X_HEAD_DIM ** -0.5)
    e = jnp.exp(s - jnp.max(s, axis=-1, keepdims=True))
    return e / jnp.sum(e, axis=-1, keepdims=True)


def _xatt_fwd(q, k, kv, seqs):
    t = q.shape[0]
    tq = min(XATT_BLOCK, t // seqs)
    nq = t // seqs // tq

    def body(q_ref, k_ref, v_ref, o_ref):
        p = _xatt_probs(q_ref[...], k_ref[...])
        o_ref[...] = jnp.dot(p.astype(BF16), v_ref[...].astype(BF16), preferred_element_type=F32).astype(o_ref.dtype)

    qs = pl.BlockSpec((tq, X_HEAD_DIM), lambda b, h, i: (b * nq + i, h))
    return pl.pallas_call(
        body, name="xatt_fwd", grid=(seqs, N_X_HEADS, nq),
        in_specs=[qs, pl.BlockSpec((N_MEM, X_HEAD_DIM), lambda b, h, i: (b, h)),
                  pl.BlockSpec((N_MEM, X_HEAD_DIM), lambda b, h, i: (b, N_X_HEADS + h))],
        out_specs=qs, out_shape=jax.ShapeDtypeStruct(q.shape, BF16),
        compiler_params=_params(("parallel", "parallel", "parallel")),
    )(q, k, kv)


def _xatt_bwd(q, k, kv, do, seqs):
    t = q.shape[0]
    tq = min(XATT_BLOCK, t // seqs)
    nq = t // seqs // tq
    scale = X_HEAD_DIM ** -0.5

    def body(q_ref, k_ref, v_ref, do_ref, dq_ref, dk_ref, dv_ref):
        @pl.when(pl.program_id(2) == 0)
        def _():
            dk_ref[...] = jnp.zeros(dk_ref.shape, F32)
            dv_ref[...] = jnp.zeros(dv_ref.shape, F32)

        qv, kk = q_ref[...], k_ref[...]
        p = _xatt_probs(qv, kk)
        dob = do_ref[...].astype(BF16)
        dp = lax.dot_general(dob, v_ref[...].astype(BF16), _NT, preferred_element_type=F32)
        ds = p * (dp - jnp.sum(dp * p, axis=-1, keepdims=True))
        dsb = ds.astype(BF16)
        dq_ref[...] = jnp.dot(dsb, kk, preferred_element_type=F32) * scale
        dk_ref[...] += lax.dot_general(dsb, qv, _TN, preferred_element_type=F32) * scale
        dv_ref[...] += lax.dot_general(p.astype(BF16), dob, _TN, preferred_element_type=F32)

    qs = pl.BlockSpec((tq, X_HEAD_DIM), lambda b, h, i: (b * nq + i, h))
    ks = pl.BlockSpec((N_MEM, X_HEAD_DIM), lambda b, h, i: (b, h))
    return pl.pallas_call(
        body, name="xatt_bwd", grid=(seqs, N_X_HEADS, nq),
        in_specs=[qs, ks, pl.BlockSpec((N_MEM, X_HEAD_DIM), lambda b, h, i: (b, N_X_HEADS + h)), qs],
        out_specs=[qs, ks, ks],
        out_shape=[jax.ShapeDtypeStruct(q.shape, F32), jax.ShapeDtypeStruct(k.shape, F32),
                   jax.ShapeDtypeStruct(k.shape, F32)],
        compiler_params=_params(("parallel", "parallel", "arbitrary")),
    )(q, k, kv, do)


CONV_COLS = 256


def _shift_down(x, k, row):
    return jnp.where(row >= k, pltpu.roll(x, k, 0), 0.0)


def _shift_up(x, k, row):
    n = x.shape[0]
    return jnp.where(row < n - k, pltpu.roll(x, n - k, 0), 0.0)


def _conv_pre(g, w, b, row):
    return b + w[0:1, :] * _shift_down(g, 2, row) + w[1:2, :] * _shift_down(g, 1, row) + w[2:3, :] * g


def _convgate_fwd(gu, w, b, seqs):
    t = gu.shape[0]
    l = t // seqs
    nc = D_FF // CONV_COLS

    def body(g_ref, u_ref, w_ref, b_ref, o_ref):
        g = g_ref[...].astype(F32)
        row = lax.broadcasted_iota(jnp.int32, g.shape, 0)
        pre = _conv_pre(g, w_ref[...], b_ref[...], row)
        o_ref[...] = (pre * jax.nn.sigmoid(pre) * u_ref[...].astype(F32)).astype(o_ref.dtype)

    return pl.pallas_call(
        body, name="convgate_fwd", grid=(seqs, nc),
        in_specs=[pl.BlockSpec((l, CONV_COLS), lambda s, j: (s, j)), pl.BlockSpec((l, CONV_COLS), lambda s, j: (s, nc + j)),
                  pl.BlockSpec((3, CONV_COLS), lambda s, j: (0, j)), pl.BlockSpec((1, CONV_COLS), lambda s, j: (0, j))],
        out_specs=pl.BlockSpec((l, CONV_COLS), lambda s, j: (s, j)),
        out_shape=jax.ShapeDtypeStruct((t, D_FF), BF16),
        compiler_params=_params(("parallel", "parallel")),
    )(gu, gu, w, b)


def _convgate_bwd(gu, w, b, dact, seqs):
    t = gu.shape[0]
    l = t // seqs
    nc = D_FF // CONV_COLS
    steps = nc * seqs

    def body(g_ref, u_ref, w_ref, b_ref, da_ref, dgu_ref, dw_ref, db_ref, stage, sems):
        j, s = pl.program_id(0), pl.program_id(1)
        n = j * seqs + s
        slot = n % 2

        def copies(slot_, j_, s_):
            rows = pl.ds(pl.multiple_of(s_ * l, 16), l)
            return [pltpu.make_async_copy(
                stage.at[slot_, half],
                dgu_ref.at[rows, pl.ds(pl.multiple_of((half * nc + j_) * CONV_COLS, 128), CONV_COLS)],
                sems.at[slot_, half]) for half in (0, 1)]

        @pl.when(s == 0)
        def _():
            dw_ref[...] = jnp.zeros(dw_ref.shape, F32)
            db_ref[...] = jnp.zeros(db_ref.shape, F32)

        @pl.when(n >= 2)
        def _():
            for cp in copies(slot, j, s):
                cp.wait()

        g, wv, da = g_ref[...].astype(F32), w_ref[...], da_ref[...].astype(F32)
        row = lax.broadcasted_iota(jnp.int32, g.shape, 0)
        g1, g2 = _shift_down(g, 1, row), _shift_down(g, 2, row)
        pre = b_ref[...] + wv[0:1, :] * g2 + wv[1:2, :] * g1 + wv[2:3, :] * g
        sg = jax.nn.sigmoid(pre)
        silu = pre * sg
        stage[slot, 1] = (da * silu).astype(stage.dtype)
        dpre = da * u_ref[...].astype(F32) * (sg * (1.0 + pre * (1.0 - sg)))
        dg = wv[2:3, :] * dpre + wv[1:2, :] * _shift_up(dpre, 1, row) + wv[0:1, :] * _shift_up(dpre, 2, row)
        stage[slot, 0] = dg.astype(stage.dtype)
        for cp in copies(slot, j, s):
            cp.start()
        dw_ref[0:1, :] += jnp.sum(dpre * g2, axis=0, keepdims=True)
        dw_ref[1:2, :] += jnp.sum(dpre * g1, axis=0, keepdims=True)
        dw_ref[2:3, :] += jnp.sum(dpre * g, axis=0, keepdims=True)
        db_ref[...] += jnp.sum(dpre, axis=0, keepdims=True)

        @pl.when(n == steps - 1)
        def _():
            for cp in copies(slot, j, s) + (copies(1 - slot, j, s) if steps > 1 else []):
                cp.wait()

    blk = lambda off: pl.BlockSpec((l, CONV_COLS), lambda j, s: (s, off + j))
    return pl.pallas_call(
        body, name="convgate_bwd", grid=(nc, seqs),
        in_specs=[blk(0), blk(nc), pl.BlockSpec((3, CONV_COLS), lambda j, s: (0, j)),
                  pl.BlockSpec((1, CONV_COLS), lambda j, s: (0, j)), blk(0)],
        out_specs=[ANY, pl.BlockSpec((3, CONV_COLS), lambda j, s: (0, j)),
                   pl.BlockSpec((1, CONV_COLS), lambda j, s: (0, j))],
        out_shape=[jax.ShapeDtypeStruct((t, 2 * D_FF), BF16), jax.ShapeDtypeStruct((3, D_FF), F32),
                   jax.ShapeDtypeStruct((1, D_FF), F32)],
        scratch_shapes=[pltpu.VMEM((2, 2, l, CONV_COLS), BF16), pltpu.SemaphoreType.DMA((2, 2))],
        compiler_params=_params(("arbitrary", "arbitrary")),
    )(gu, gu, w, b, dact)


def _loss_head(h, target):
    t, d = h.shape
    tm = _pick(t, (256, 128, 8))

    def body(h_ref, t_ref, dh_ref, dhb_ref, loss_ref):
        @pl.when(pl.program_id(0) == 0)
        def _():
            loss_ref[...] = jnp.zeros(loss_ref.shape, F32)

        e = h_ref[...] - t_ref[...]
        dh = e * (1.0 / d)
        dh_ref[...] = dh
        dhb_ref[...] = dh.astype(BF16)
        loss_ref[...] += (0.5 / d) * jnp.sum(jnp.sum(e * e, axis=1, keepdims=True), axis=0, keepdims=True)

    blk = pl.BlockSpec((tm, d), lambda i: (i, 0))
    return pl.pallas_call(
        body, name="loss_head", grid=(t // tm,), in_specs=[blk, blk],
        out_specs=[blk, blk, pl.BlockSpec((1, 1), lambda i: (0, 0))],
        out_shape=[jax.ShapeDtypeStruct((t, d), F32), jax.ShapeDtypeStruct((t, d), BF16),
                   jax.ShapeDtypeStruct((1, 1), F32)],
        compiler_params=_params(("arbitrary",)),
    )(h, target)


def _s5_discretise(a_re, a_im, log_dt, b_re, b_im):
    dt = jnp.exp(log_dt)[:, None]
    mag = jnp.exp(a_re * dt)
    lb_r = mag * jnp.cos(a_im * dt)
    lb_i = mag * jnp.sin(a_im * dt)
    den = a_re * a_re + a_im * a_im
    nr = lb_r - 1.0
    coef_r = (nr * a_re + lb_i * a_im) / den
    coef_i = (lb_i * a_re - nr * a_im) / den
    bb_r = coef_r[:, :, None] * b_re - coef_i[:, :, None] * b_im
    bb_i = coef_r[:, :, None] * b_im + coef_i[:, :, None] * b_re
    return lb_r, lb_i, bb_r, bb_i


S5_CHUNKS = 4
S5_PER = S5_GROUPS // S5_CHUNKS


def _blockdiag_in(bb):
    eye = jnp.eye(S5_PER, dtype=bb.dtype)
    return jnp.einsum("jgpc,gh->jgchp", bb.reshape(S5_CHUNKS, S5_PER, S5_STATE, S5_GROUP_CH), eye).reshape(
        S5_CHUNKS, S5_PER * S5_GROUP_CH, S5_PER * S5_STATE)


def _blockdiag_in_grad(d):
    eye = jnp.eye(S5_PER, dtype=d.dtype)
    return jnp.einsum("jgchp,gh->jgpc", d.reshape(S5_CHUNKS, S5_PER, S5_GROUP_CH, S5_PER, S5_STATE), eye).reshape(
        S5_GROUPS, S5_STATE, S5_GROUP_CH)


def _blockdiag_out(c):
    eye = jnp.eye(S5_PER, dtype=c.dtype)
    return jnp.einsum("jgcp,gh->jgphc", c.reshape(S5_CHUNKS, S5_PER, S5_GROUP_CH, S5_STATE), eye).reshape(
        S5_CHUNKS, S5_PER * S5_STATE, S5_PER * S5_GROUP_CH)


def _blockdiag_out_grad(d):
    eye = jnp.eye(S5_PER, dtype=d.dtype)
    return jnp.einsum("jgphc,gh->jgcp", d.reshape(S5_CHUNKS, S5_PER, S5_STATE, S5_PER, S5_GROUP_CH), eye).reshape(
        S5_GROUPS, S5_GROUP_CH, S5_STATE)


def _local_step(x3, mem3, target3, p, wb, late_weights=None, early_grads=None):
    seqs, l, d = x3.shape
    t = seqs * l
    x = x3.reshape(t, d)
    mem = mem3.reshape(seqs * N_MEM, d)
    target = target3.reshape(t, d)
    full = lambda a: (a, a.shape[1], 0, 0)

    s5_in = (p["s5_a_re"], p["s5_a_im"], p["s5_log_dt"], p["s5_b_re"], p["s5_b_im"])
    (lb_r, lb_i, bb_r, bb_i), s5_pull = jax.vjp(_s5_discretise, *s5_in)
    ar, ai = lb_r.reshape(1, S5_CH), lb_i.reshape(1, S5_CH)
    bbr_d, bbi_d = _blockdiag_in(bb_r).astype(BF16), _blockdiag_in(bb_i).astype(BF16)
    cr_d, ci_d = _blockdiag_out(p["s5_c_re"]).astype(BF16), (-_blockdiag_out(p["s5_c_im"])).astype(BF16)
    d_row = p["s5_d"].reshape(1, S5_WIDTH)

    w_in = wb["w_in"]
    w_qkv = w_in[:, :3 * FOX_WIDTH]
    w_uf = jnp.concatenate(
        [w_in[:, 3 * FOX_WIDTH + N_FOX_HEADS:], w_in[:, 3 * FOX_WIDTH:3 * FOX_WIDTH + N_FOX_HEADS],
         jnp.zeros((d, UF_COLS - S5_WIDTH - N_FOX_HEADS), w_in.dtype)], axis=1)

    hn1 = _rowwise(_rms, [full(x)], [p["norm_mix"]], [(d, d, 0, BF16)], "norm_mix_fwd")
    qkv = _mm(hn1, w_qkv, "nn", "in_qkv")
    uf = _mm(hn1, w_uf, "nn", "in_uf")

    bh = seqs * N_FOX_HEADS
    q_pair = (qkv, 128, 0, 1)
    k_pair = (qkv, 128, N_PAIRS, 1)
    gq2, gk2 = jnp.tile(p["fox_q_norm"], (1, 2)), jnp.tile(p["fox_k_norm"], (1, 2))
    pair_out = [(FOX_WIDTH, 128, 1, BF16)]
    qn = _rowwise(_rms_pair, [q_pair], [gq2], pair_out, "fox_qnorm_fwd", heads=N_PAIRS)
    kn = _rowwise(_rms_pair, [k_pair], [gk2], pair_out, "fox_knorm_fwd", heads=N_PAIRS)

    f_rows = uf[:, S5_WIDTH:S5_WIDTH + N_FOX_HEADS].reshape(seqs, l, N_FOX_HEADS).transpose(0, 2, 1).reshape(bh, l)
    f_bias = jnp.tile(p["fox_f_bias"].reshape(N_FOX_HEADS, 1), (seqs, 1))
    c_wide = jnp.broadcast_to(_forget_fwd(f_rows, f_bias)[:, :, None], (bh, l, 128))
    fox, lse = _fox_fwd(qn, kn, qkv, c_wide, seqs)

    xr, xi, ys = _s5_fwd(uf, bbr_d, bbi_d, cr_d, ci_d, ar, ai, seqs)
    u_blk = (uf, S5_WIDTH, 0, 0)
    yg = _rowwise(_s5_act, [full(ys), u_blk], [d_row], [(S5_WIDTH, S5_WIDTH, 0, F32)], "s5_act_fwd")
    z = _mm(yg, wb["s5_w_glu"], "nn", "s5_glu")
    y2n = _rowwise(_s5_gate, [full(yg), full(z)], [p["s5_b_glu"], p["out_norm_s5"]],
                   [(S5_WIDTH, S5_WIDTH, 0, BF16)], "s5_gate_fwd")
    foxn = _rowwise(_rms, [full(fox)], [p["out_norm_fox"]], [(FOX_WIDTH, FOX_WIDTH, 0, BF16)], "fox_outnorm_fwd")
    mixed = jnp.concatenate([foxn, y2n], axis=1)
    h1 = _mm(mixed, wb["w_out"], "nn", "mix_out", res=x)
    if late_weights is not None:
        wb = dict(wb, **late_weights(h1))

    hn2 = _rowwise(_rms, [full(h1)], [p["norm_cross"]], [(d, d, 0, BF16)], "norm_cross_fwd")
    mn = _rowwise(_rms, [full(mem)], [p["norm_mem"]], [(d, d, 0, BF16)], "norm_mem_fwd")
    xq_raw = _mm(hn2, wb["w_xq"], "nn", "x_q")
    kv = _mm(mn, wb["w_xkv"], "nn", "x_kv")
    xh = lambda a: (a, X_HEAD_DIM, 0, 1)
    xqn = _rowwise(_rms, [xh(xq_raw)], [p["xq_norm"]], [(d, X_HEAD_DIM, 1, BF16)], "x_qnorm_fwd", heads=N_X_HEADS)
    xkn = _rowwise(_rms, [xh(kv)], [p["xk_norm"]], [(d, X_HEAD_DIM, 1, BF16)], "x_knorm_fwd", heads=N_X_HEADS)
    xo = _xatt_fwd(xqn, xkn, kv, seqs)
    h2 = _mm(xo, wb["w_xo"], "nn", "x_out", res=h1)

    hn3 = _rowwise(_rms, [full(h2)], [p["norm_ffn"]], [(d, d, 0, BF16)], "norm_ffn_fwd")
    gu = _mm(hn3, wb["w_ffn_up"], "nn", "ffn_up", out_dtype=BF16)
    act = _convgate_fwd(gu, p["ffn_conv_w"], p["ffn_conv_b"], seqs)
    h3 = _mm(act, wb["w_ffn_down"], "nn", "ffn_down", res=h2)
    dh3, dh3_b, loss = _loss_head(h3, target)

    g = {}
    dact = _mm(dh3_b, wb["w_ffn_down"], "nt", "ffn_down_dx", out_dtype=BF16)
    late_dt = BF16 if early_grads is not None else F32
    g["w_ffn_down"] = _mm(act, dh3_b, "tn", "ffn_down_dw", out_dtype=late_dt)
    dgu, g["ffn_conv_w"], g["ffn_conv_b"] = _convgate_bwd(gu, p["ffn_conv_w"], p["ffn_conv_b"], dact, seqs)
    dhn3 = _mm(dgu, wb["w_ffn_up"], "nt", "ffn_up_dx")
    g["w_ffn_up"] = _mm(hn3, dgu, "tn", "ffn_up_dw", out_dtype=late_dt)
    (dh2,), (g["norm_ffn"],) = _rowwise_vjp(_rms, [full(h2)], [p["norm_ffn"]], [full(dhn3)], "norm_ffn_bwd",
                                            adds=[full(dh3)])

    dxo = _mm(dh2, wb["w_xo"], "nt", "x_out_dx")
    g["w_xo"] = _mm(xo, dh2, "tn", "x_out_dw", out_dtype=late_dt)
    dxqn, dxkn, dxv = _xatt_bwd(xqn, xkn, kv, dxo, seqs)
    (dxq_raw,), (g["xq_norm"],) = _rowwise_vjp(_rms, [xh(xq_raw)], [p["xq_norm"]], [xh(dxqn)], "x_qnorm_bwd",
                                               heads=N_X_HEADS, row_dtypes=[BF16])
    (dxk_raw,), (g["xk_norm"],) = _rowwise_vjp(_rms, [xh(kv)], [p["xk_norm"]], [xh(dxkn)], "x_knorm_bwd",
                                               heads=N_X_HEADS, row_dtypes=[BF16])
    dkv = jnp.concatenate([dxk_raw, dxv.astype(BF16)], axis=1)
    dhn2 = _mm(dxq_raw, wb["w_xq"], "nt", "x_q_dx")
    g["w_xq"] = _mm(hn2, dxq_raw, "tn", "x_q_dw", out_dtype=late_dt)
    dmn = _mm(dkv, wb["w_xkv"], "nt", "x_kv_dx")
    g["w_xkv"] = _mm(mn, dkv, "tn", "x_kv_dw", out_dtype=late_dt)
    norm_cross = p["norm_cross"]
    if early_grads is not None:
        norm_cross = norm_cross + early_grads({n: g[n] for n in LATE_WEIGHTS})
    (dh1,), (g["norm_cross"],) = _rowwise_vjp(_rms, [full(h1)], [norm_cross], [full(dhn2)], "norm_cross_bwd",
                                              adds=[full(dh2)])
    _, (g["norm_mem"],) = _rowwise_vjp(_rms, [full(mem)], [p["norm_mem"]], [full(dmn)], "norm_mem_bwd",
                                       row_dtypes=[BF16])

    dmixed = _mm(dh1, wb["w_out"], "nt", "mix_out_dx")
    g["w_out"] = _mm(mixed, dh1, "tn", "mix_out_dw")
    (dfox,), (g["out_norm_fox"],) = _rowwise_vjp(_rms, [full(fox)], [p["out_norm_fox"]],
                                                 [(dmixed, FOX_WIDTH, 0, 0)], "fox_outnorm_bwd")
    (dyg_a, dz), (g["s5_b_glu"], g["out_norm_s5"]) = _rowwise_vjp(
        _s5_gate, [full(yg), full(z)], [p["s5_b_glu"], p["out_norm_s5"]], [(dmixed, S5_WIDTH, 1, 0)], "s5_gate_bwd",
        row_dtypes=[F32, BF16])
    dyg = _mm(dz, wb["s5_w_glu"], "nt", "s5_glu_dx", res=dyg_a)
    g["s5_w_glu"] = _mm(yg, dz, "tn", "s5_glu_dw")
    (dys, du_a), (dd_row,) = _rowwise_vjp(_s5_act, [full(ys), u_blk], [d_row], [full(dyg)], "s5_act_bwd",
                                          row_dtypes=[BF16, F32])
    g["s5_d"] = dd_row
    du_b, dbbr_d, dbbi_d, dcr_d, dci_d, dar, dai = _s5_bwd(dys, uf, xr, xi, bbr_d, bbi_d, cr_d, ci_d, ar, ai, seqs)
    dbbr_d, dbbi_d, dcr_d, dci_d = (jnp.sum(a, axis=0) for a in (dbbr_d, dbbi_d, dcr_d, dci_d))
    d_lb_r = jnp.sum(dar, axis=0).reshape(S5_GROUPS, S5_STATE)
    d_lb_i = jnp.sum(dai, axis=0).reshape(S5_GROUPS, S5_STATE)
    g["s5_a_re"], g["s5_a_im"], g["s5_log_dt"], g["s5_b_re"], g["s5_b_im"] = s5_pull(
        (d_lb_r, d_lb_i, _blockdiag_in_grad(dbbr_d), _blockdiag_in_grad(dbbi_d)))
    g["s5_c_re"] = _blockdiag_out_grad(dcr_d)
    g["s5_c_im"] = -_blockdiag_out_grad(dci_d)

    dqn, dkn, dv, dc, dcq = _fox_bwd(qn, kn, qkv, c_wide, fox, dfox, lse, seqs)
    pair = lambda a: (a, 128, 0, 1)
    (dq_raw,), (dgq2,) = _rowwise_vjp(_rms_pair, [q_pair], [gq2], [pair(dqn)], "fox_qnorm_bwd", heads=N_PAIRS,
                                      row_dtypes=[BF16])
    (dk_raw,), (dgk2,) = _rowwise_vjp(_rms_pair, [k_pair], [gk2], [pair(dkn)], "fox_knorm_bwd", heads=N_PAIRS,
                                      row_dtypes=[BF16])
    g["fox_q_norm"] = dgq2[:, :HEAD_DIM] + dgq2[:, HEAD_DIM:]
    g["fox_k_norm"] = dgk2[:, :HEAD_DIM] + dgk2[:, HEAD_DIM:]
    df_rows, dfb = _forget_bwd(f_rows, f_bias, (dc + dcq).reshape(bh, l))
    g["fox_f_bias"] = jnp.sum(dfb.reshape(seqs, N_FOX_HEADS), axis=0)
    df = df_rows.reshape(seqs, N_FOX_HEADS, l).transpose(0, 2, 1).reshape(t, N_FOX_HEADS)
    dqkv = jnp.concatenate([dq_raw, dk_raw, dv.astype(BF16)], axis=1)
    duf = jnp.concatenate([du_a + du_b, df, jnp.zeros((t, UF_COLS - S5_WIDTH - N_FOX_HEADS), F32)],
                          axis=1).astype(BF16)
    dhn1 = _mm(duf, w_uf, "nt", "in_uf_dx", res=_mm(dqkv, w_qkv, "nt", "in_qkv_dx"))
    dw_qkv = _mm(hn1, dqkv, "tn", "in_qkv_dw")
    dw_uf = _mm(hn1, duf, "tn", "in_uf_dw")
    g["w_in"] = jnp.concatenate([dw_qkv, dw_uf[:, S5_WIDTH:S5_WIDTH + N_FOX_HEADS], dw_uf[:, :S5_WIDTH]], axis=1)
    (dx,), (g["norm_mix"],) = _rowwise_vjp(_rms, [full(x)], [p["norm_mix"]], [full(dhn1)], "norm_mix_bwd",
                                           adds=[full(dh1)])
    return loss, dx.reshape(seqs, l, d), g


def _place():
    return lax.axis_index("x"), lax.axis_index("y"), lax.axis_index("c")


def _other_chips(x, y):
    return [(1 - x, y), (x, 1 - y), (1 - x, 1 - y)]


ANY = pl.BlockSpec(memory_space=pl.ANY)


def _gather_weights(shards, col_kind, taps):
    n = len(shards)

    def body(*refs):
        ins, tap_in, outs, tap_out = refs[:n], refs[n], refs[n + 1:2 * n + 1], refs[2 * n + 1]
        ici_send, ici_recv, d2d_send, d2d_recv, own_send, own_recv = refs[2 * n + 2:]
        x, y, c = _place()
        mine = 2 * x + y
        chips = _other_chips(x, y)
        sibling = (x, y, 1 - c)

        def piece(a, s, h):
            r, cs = ins[a].shape
            hr = r // 2
            if col_kind[a]:
                return outs[a].at[pl.ds(pl.multiple_of(h * hr, 16), hr), pl.ds(pl.multiple_of(s * cs, 128), cs)]
            return outs[a].at[pl.ds(pl.multiple_of(s * r + h * hr, 16), hr), :]

        def slab(a, s):
            r, cs = ins[a].shape
            if col_kind[a]:
                return outs[a].at[:, pl.ds(pl.multiple_of(s * cs, 128), cs)]
            return outs[a].at[pl.ds(pl.multiple_of(s * r, 16), r), :]

        def own_half(a, h):
            hr = ins[a].shape[0] // 2
            return ins[a].at[pl.ds(pl.multiple_of(h * hr, 16), hr), :]

        sends = []
        for a in range(n):
            cp = pltpu.make_async_remote_copy(
                src_ref=ins[a], dst_ref=slab(a, mine), send_sem=own_send.at[a], recv_sem=own_recv.at[a],
                device_id=sibling, device_id_type=MESH)
            cp.start()
            sends.append(cp)
        cp = pltpu.make_async_remote_copy(
            src_ref=tap_in, dst_ref=tap_out.at[mine], send_sem=own_send.at[n], recv_sem=own_recv.at[n],
            device_id=sibling, device_id_type=MESH)
        cp.start()
        sends.append(cp)
        for a in range(n):
            for j, (px, py) in enumerate(chips):
                cp = pltpu.make_async_remote_copy(
                    src_ref=own_half(a, c), dst_ref=piece(a, mine, c), send_sem=ici_send.at[3 * a + j],
                    recv_sem=ici_recv.at[3 * a + j], device_id=(px, py, c), device_id_type=MESH)
                cp.start()
                sends.append(cp)
        for j, (px, py) in enumerate(chips):
            cp = pltpu.make_async_remote_copy(
                src_ref=tap_in, dst_ref=tap_out.at[mine], send_sem=ici_send.at[3 * n + j],
                recv_sem=ici_recv.at[3 * n + j], device_id=(px, py, c), device_id_type=MESH)
            cp.start()
            sends.append(cp)
        for a in range(n):
            for j, (px, py) in enumerate(chips):
                got = piece(a, 2 * px + py, c)
                pltpu.make_async_remote_copy(
                    src_ref=got, dst_ref=got, send_sem=ici_send.at[3 * a + j], recv_sem=ici_recv.at[3 * a + j],
                    device_id=(px, py, c), device_id_type=MESH).wait_recv()
                fwd = pltpu.make_async_remote_copy(
                    src_ref=got, dst_ref=got, send_sem=d2d_send.at[3 * a + j], recv_sem=d2d_recv.at[3 * a + j],
                    device_id=(x, y, 1 - c), device_id_type=MESH)
                fwd.start()
                sends.append(fwd)
        for a in range(n):
            for j, (px, py) in enumerate(chips):
                other = piece(a, 2 * px + py, 1 - c)
                pltpu.make_async_remote_copy(
                    src_ref=other, dst_ref=other, send_sem=d2d_send.at[3 * a + j], recv_sem=d2d_recv.at[3 * a + j],
                    device_id=(x, y, 1 - c), device_id_type=MESH).wait_recv()
        for j, (px, py) in enumerate(chips):
            pltpu.make_async_remote_copy(
                src_ref=tap_in, dst_ref=tap_out.at[2 * px + py], send_sem=ici_send.at[3 * n + j],
                recv_sem=ici_recv.at[3 * n + j], device_id=(px, py, c), device_id_type=MESH).wait_recv()
        for a in range(n):
            pltpu.make_async_remote_copy(
                src_ref=ins[a], dst_ref=slab(a, mine), send_sem=own_send.at[a], recv_sem=own_recv.at[a],
                device_id=sibling, device_id_type=MESH).wait_recv()
        pltpu.make_async_remote_copy(
            src_ref=tap_in, dst_ref=tap_out.at[mine], send_sem=own_send.at[n], recv_sem=own_recv.at[n],
            device_id=sibling, device_id_type=MESH).wait_recv()
        for cp in sends:
            cp.wait_send()

    def full_shape(a):
        r, cs = shards[a].shape
        return (r, 4 * cs) if col_kind[a] else (4 * r, cs)

    res = pl.pallas_call(
        body, name="gather_weights", in_specs=[ANY] * (n + 1), out_specs=[ANY] * (n + 1),
        out_shape=[jax.ShapeDtypeStruct(full_shape(a), shards[a].dtype) for a in range(n)]
        + [jax.ShapeDtypeStruct((4,) + taps.shape, taps.dtype)],
        scratch_shapes=[pltpu.SemaphoreType.DMA((3 * n + 3,)), pltpu.SemaphoreType.DMA((3 * n + 3,)),
                        pltpu.SemaphoreType.DMA((3 * n,)), pltpu.SemaphoreType.DMA((3 * n,)),
                        pltpu.SemaphoreType.DMA((n + 1,)), pltpu.SemaphoreType.DMA((n + 1,))],
        compiler_params=pltpu.CompilerParams(has_side_effects=True),
    )(*shards, taps)
    return res[:n], res[n]


HBM = pl.BlockSpec(memory_space=pltpu.HBM)
SEM = pl.BlockSpec(memory_space=pltpu.SEMAPHORE)
DATAFLOW = pltpu.SideEffectType.DATAFLOW_SIDE_EFFECTING


def _in_hbm(a):
    return pltpu.with_memory_space_constraint(a, pltpu.HBM)


def _split_start(name, srcs, lands, n_copies, plan):
    n = len(srcs)

    def body(*refs):
        src_refs, land_refs = refs[:n], refs[n:2 * n]
        send_sems, recv_sems = refs[2 * n], refs[2 * n + 1]
        for i, (src, dst, dev) in enumerate(plan(src_refs, land_refs)):
            pltpu.make_async_remote_copy(src_ref=src, dst_ref=dst, send_sem=send_sems.at[i], recv_sem=recv_sems.at[i],
                                         device_id=dev, device_id_type=MESH).start()
        refs[-1][...] = jnp.zeros((8, 128), F32)

    res = pl.pallas_call(
        body, name=name, in_specs=[HBM] * (2 * n),
        out_specs=[SEM, SEM] + [HBM] * (2 * n) + [pl.BlockSpec(memory_space=pltpu.VMEM)],
        out_shape=[pltpu.SemaphoreType.DMA((n_copies,)), pltpu.SemaphoreType.DMA((n_copies,))]
        + [pltpu.HBM(a.shape, a.dtype) for a in list(srcs) + list(lands)] + [jax.ShapeDtypeStruct((8, 128), F32)],
        input_output_aliases={i: 2 + i for i in range(2 * n)},
        compiler_params=pltpu.CompilerParams(has_side_effects=DATAFLOW),
    )(*[_in_hbm(a) for a in list(srcs) + list(lands)])
    return res[0], res[1], list(res[2:2 + n]), list(res[2 + n:2 + 2 * n]), res[-1]


def _split_wait(name, send_sems, recv_sems, srcs, lands, after, plan):
    n = len(srcs)

    def body(*refs):
        src_refs, land_refs = refs[:n], refs[n:2 * n]
        send_ref, recv_ref = refs[2 * n], refs[2 * n + 1]
        for i, (src, dst, dev) in enumerate(plan(src_refs, land_refs)):
            cp = pltpu.make_async_remote_copy(src_ref=src, dst_ref=dst, send_sem=send_ref.at[i], recv_sem=recv_ref.at[i],
                                              device_id=dev, device_id_type=MESH)
            cp.wait_send()
            cp.wait_recv()

    res = pl.pallas_call(
        body, name=name, in_specs=[HBM] * (2 * n) + [SEM, SEM, ANY], out_specs=[HBM] * (2 * n),
        out_shape=[pltpu.HBM(a.shape, a.dtype) for a in list(srcs) + list(lands)],
        input_output_aliases={i: i for i in range(2 * n)},
        compiler_params=pltpu.CompilerParams(has_side_effects=DATAFLOW),
    )(*srcs, *lands, send_sems, recv_sems, after)
    return list(res[:n]), list(res[n:])


def _late_gather_plan(col_kind):
    def plan(src_refs, land_refs):
        x, y, c = _place()
        mine = 2 * x + y
        copies = []
        for a, (src, land) in enumerate(zip(src_refs, land_refs)):
            r, cs = src.shape
            if col_kind[a]:
                dst = land.at[:, pl.ds(pl.multiple_of(mine * cs, 128), cs)]
            else:
                dst = land.at[pl.ds(pl.multiple_of(mine * r, 16), r), :]
            copies.append((src, dst, (x, y, 1 - c)))
            copies += [(src, dst, (px, py, c)) for (px, py) in _other_chips(x, y)]
        return copies
    return plan


def _late_reduce_plan(col_kind):
    def plan(src_refs, land_refs):
        x, y, c = _place()
        copies = []
        for a, (src, land) in enumerate(zip(src_refs, land_refs)):
            for j, (px, py) in enumerate(_other_chips(x, y)):
                if col_kind[a]:
                    cs = land.shape[2]
                    piece = src.at[:, pl.ds(pl.multiple_of((2 * px + py) * cs, 128), cs)]
                else:
                    piece = src.at[2 * px + py]
                copies.append((piece, land.at[j], (px, py, c)))
        return copies
    return plan


def _pair_exchange_halves(name, grads, col_kind):
    n = len(grads)

    def body(*refs):
        ins, outs = refs[:n], refs[n:2 * n]
        send_sems, recv_sems = refs[2 * n:]
        x, y, c = _place()
        copies = []
        for a in range(n):
            if col_kind[a]:
                hr = ins[a].shape[0] // 2
                src = ins[a].at[pl.ds(pl.multiple_of((1 - c) * hr, 8), hr), :]
            else:
                hr = ins[a].shape[1] // 2
                src = ins[a].at[:, pl.ds(pl.multiple_of((1 - c) * hr, 8), hr), :]
            cp = pltpu.make_async_remote_copy(
                src_ref=src, dst_ref=outs[a], send_sem=send_sems.at[a], recv_sem=recv_sems.at[a],
                device_id=(x, y, 1 - c), device_id_type=MESH)
            cp.start()
            copies.append(cp)
        for cp in copies:
            cp.wait()

    def half_shape(a):
        s = grads[a].shape
        return (s[0] // 2, s[1]) if col_kind[a] else (4, s[1] // 2, s[2])

    return pl.pallas_call(
        body, name=name, in_specs=[ANY] * n, out_specs=[ANY] * n,
        out_shape=[jax.ShapeDtypeStruct(half_shape(a), grads[a].dtype) for a in range(n)],
        scratch_shapes=[pltpu.SemaphoreType.DMA((n,)), pltpu.SemaphoreType.DMA((n,))],
        compiler_params=pltpu.CompilerParams(has_side_effects=True),
    )(*grads)


def _pair_swap_halves(name, halves):
    n = len(halves)

    def body(*refs):
        ins, outs = refs[:n], refs[n:2 * n]
        send_sems, recv_sems = refs[2 * n:]
        x, y, c = _place()
        copies = []
        for a in range(n):
            cp = pltpu.make_async_remote_copy(
                src_ref=ins[a], dst_ref=outs[a], send_sem=send_sems.at[a], recv_sem=recv_sems.at[a],
                device_id=(x, y, 1 - c), device_id_type=MESH)
            cp.start()
            copies.append(cp)
        for cp in copies:
            cp.wait()

    return pl.pallas_call(
        body, name=name, in_specs=[ANY] * n, out_specs=[ANY] * n,
        out_shape=[jax.ShapeDtypeStruct(s.shape, s.dtype) for s in halves],
        scratch_shapes=[pltpu.SemaphoreType.DMA((n,)), pltpu.SemaphoreType.DMA((n,))],
        compiler_params=pltpu.CompilerParams(has_side_effects=True),
    )(*halves)


def _chip_sum(name, chip_sel, own, col, others):
    _, r, c = others.shape
    tr = _pick(r, (256, 128, 64, 32, 16))
    if col:
        own_spec = pl.BlockSpec((tr, c), lambda i, s: (i, s[0]))
    else:
        own_spec = pl.BlockSpec((None, tr, c), lambda i, s: (s[0], i, 0))
    specs = [own_spec] + [pl.BlockSpec((None, tr, c), lambda i, s, k=k: (k, i, 0)) for k in range(3)]

    def body(s_ref, own_ref, r0, r1, r2, o_ref):
        o_ref[...] = ((own_ref[...].astype(F32) + r0[...].astype(F32)) + r1[...].astype(F32)) + r2[...].astype(F32)

    return pl.pallas_call(
        body, name=name,
        grid_spec=pltpu.PrefetchScalarGridSpec(
            num_scalar_prefetch=1, grid=(r // tr,), in_specs=specs,
            out_specs=pl.BlockSpec((tr, c), lambda i, s: (i, 0))),
        out_shape=jax.ShapeDtypeStruct((r, c), F32),
        compiler_params=_params(("parallel",)),
    )(chip_sel, own, others, others, others)


def _pair_sum(name, c_sel, grad, recv, col):
    if col:
        r, c4 = grad.shape
        hr, c = r // 2, c4 // 4
    else:
        _, r, c = grad.shape
        hr = r // 2
    tr = _pick(hr, (256, 128, 64, 32, 16))
    nb = hr // tr

    def body(s_ref, g_ref, r_ref, o_ref):
        o_ref[...] = (g_ref[...] + r_ref[...]).astype(o_ref.dtype)

    if col:
        in_specs = [pl.BlockSpec((tr, c), lambda k, i, s: (s[0] * nb + i, k)), pl.BlockSpec((tr, c), lambda k, i, s: (i, k))]
        out_spec = pl.BlockSpec((tr, c), lambda k, i, s: (i, k))
    else:
        in_specs = [pl.BlockSpec((None, tr, c), lambda k, i, s: (k, s[0] * nb + i, 0)),
                    pl.BlockSpec((None, tr, c), lambda k, i, s: (k, i, 0))]
        out_spec = pl.BlockSpec((None, tr, c), lambda k, i, s: (k, i, 0))
    return pl.pallas_call(
        body, name=name,
        grid_spec=pltpu.PrefetchScalarGridSpec(num_scalar_prefetch=1, grid=(4, nb), in_specs=in_specs,
                                               out_specs=out_spec),
        out_shape=jax.ShapeDtypeStruct(recv.shape, BF16),
        compiler_params=_params(("parallel", "parallel")),
    )(c_sel, grad, recv)


def _allreduce_small(vals):
    sizes = [int(math.prod(v.shape)) for v in vals]
    padded = [-(-s // 128) * 128 for s in sizes]
    total = -(-sum(padded) // 1024) * 1024
    flat = [jnp.pad(v.reshape(-1), (0, p - s)) for v, s, p in zip(vals, sizes, padded)]
    flat.append(jnp.zeros((total - sum(padded),), F32))
    packed = jnp.concatenate(flat).reshape(total // 128, 128)

    def body(in_ref, out_ref, r0, r1, r2, send_sems, recv_sems):
        x, y, c = _place()
        out_ref[...] = in_ref[...]
        for k, (peer, land) in enumerate(zip([(x, y, 1 - c), (1 - x, y, c), (x, 1 - y, c)], (r0, r1, r2))):
            cp = pltpu.make_async_remote_copy(
                src_ref=out_ref, dst_ref=land, send_sem=send_sems.at[k], recv_sem=recv_sems.at[k],
                device_id=peer, device_id_type=MESH)
            cp.start()
            cp.wait()
            out_ref[...] = out_ref[...] + land[...]

    vm = pl.BlockSpec(memory_space=pltpu.VMEM)
    summed = pl.pallas_call(
        body, name="allreduce_small", in_specs=[vm], out_specs=vm,
        out_shape=jax.ShapeDtypeStruct(packed.shape, F32),
        scratch_shapes=[pltpu.VMEM(packed.shape, F32)] * 3
        + [pltpu.SemaphoreType.DMA((3,)), pltpu.SemaphoreType.DMA((3,))],
        compiler_params=pltpu.CompilerParams(has_side_effects=True, vmem_limit_bytes=VMEM_LIMIT_BYTES),
    )(packed).reshape(-1)
    outs, off = [], 0
    for v, s, p in zip(vals, sizes, padded):
        outs.append(summed[off:off + s].reshape(v.shape))
        off += p
    return outs


def _adamw_math(w, g, m, v):
    m2 = ADAM_B1 * m + (1.0 - ADAM_B1) * g
    v2 = ADAM_B2 * v + (1.0 - ADAM_B2) * (g * g)
    m_hat = m2 / (1.0 - ADAM_B1 ** ADAM_STEP)
    v_hat = v2 / (1.0 - ADAM_B2 ** ADAM_STEP)
    delta = -ADAM_LR * (m_hat / (jnp.sqrt(v_hat) + ADAM_EPS) + ADAM_WD * w)
    return delta, m2, v2


def _adamw_big(name, c_sel, w, g_mine, g_sibling, m, v, halves):
    _, r, c = w.shape
    hr = r // 2
    tr = _pick(hr, (256, 128, 64, 32, 16, 8))
    nb = hr // tr

    def body(s_ref, w_ref, ga_ref, gb_ref, m_ref, v_ref, go_ref, d_ref, mo_ref, vo_ref):
        if halves:
            gv = jnp.where(pl.program_id(0) == s_ref[0], ga_ref[...], gb_ref[...])
        else:
            gv = ga_ref[...] + gb_ref[...]
        d, m2, v2 = _adamw_math(w_ref[...], gv, m_ref[...], v_ref[...])
        go_ref[...] = gv
        d_ref[...] = d
        mo_ref[...] = m2
        vo_ref[...] = v2

    blk = pl.BlockSpec((None, tr, c), lambda h, i, s: (0, h * nb + i, 0))
    half = pl.BlockSpec((tr, c), (lambda h, i, s: (i, 0)) if halves else (lambda h, i, s: (h * nb + i, 0)))
    return pl.pallas_call(
        body, name=name,
        grid_spec=pltpu.PrefetchScalarGridSpec(
            num_scalar_prefetch=1, grid=(2, nb), in_specs=[blk, half, half, blk, blk], out_specs=[blk] * 4),
        out_shape=[jax.ShapeDtypeStruct((1, r, c), F32)] * 4, compiler_params=_params(("parallel", "parallel")),
    )(c_sel, w, g_mine, g_sibling, m, v)


def _adamw_small(ws, gs, ms, vs):
    n = len(ws)

    def body(*refs):
        w_r, g_r, m_r, v_r = refs[:n], refs[n:2 * n], refs[2 * n:3 * n], refs[3 * n:4 * n]
        o = refs[4 * n:]
        for a in range(n):
            gv = g_r[a][...]
            d, m2, v2 = _adamw_math(w_r[a][...], gv, m_r[a][...], v_r[a][...])
            o[a][...] = gv
            o[n + a][...] = d
            o[2 * n + a][...] = m2
            o[3 * n + a][...] = v2

    res = pl.pallas_call(
        body, name="adamw_small", out_shape=[jax.ShapeDtypeStruct(w.shape, F32) for _ in range(4) for w in ws],
        compiler_params=_params(),
    )(*ws, *gs, *ms, *vs)
    return res[:n], res[n:2 * n], res[2 * n:3 * n], res[3 * n:]


def _full_from_gathered(name, gathered):
    if name == "w_in":
        rows = gathered.shape[0] // 4
        return gathered.reshape(4, rows, gathered.shape[1]).transpose(1, 0, 2).reshape(rows, 4 * gathered.shape[1])
    return gathered


def _reduce_layout(name, full):
    if name in COL_KIND:
        return full
    if name == "w_in":
        rows, cols = full.shape
        return full.reshape(rows, 4, cols // 4).transpose(1, 0, 2)
    return full.reshape(4, full.shape[0] // 4, full.shape[1])


def kernel(x, mem, norm_mix, w_in, fox_q_norm, fox_k_norm, fox_f_bias, s5_a_re, s5_a_im, s5_log_dt, s5_b_re, s5_b_im, s5_c_re, s5_c_im, s5_d, s5_w_glu, s5_b_glu, out_norm_fox, out_norm_s5, w_out, norm_cross, norm_mem, w_xq, w_xkv, xq_norm, xk_norm, w_xo, norm_ffn, w_ffn_up, ffn_conv_w, ffn_conv_b, w_ffn_down, loss_target, m_norm_mix, m_w_in, m_fox_q_norm, m_fox_k_norm, m_fox_f_bias, m_s5_a_re, m_s5_a_im, m_s5_log_dt, m_s5_b_re, m_s5_b_im, m_s5_c_re, m_s5_c_im, m_s5_d, m_s5_w_glu, m_s5_b_glu, m_out_norm_fox, m_out_norm_s5, m_w_out, m_norm_cross, m_norm_mem, m_w_xq, m_w_xkv, m_xq_norm, m_xk_norm, m_w_xo, m_norm_ffn, m_w_ffn_up, m_ffn_conv_w, m_ffn_conv_b, m_w_ffn_down, v_norm_mix, v_w_in, v_fox_q_norm, v_fox_k_norm, v_fox_f_bias, v_s5_a_re, v_s5_a_im, v_s5_log_dt, v_s5_b_re, v_s5_b_im, v_s5_c_re, v_s5_c_im, v_s5_d, v_s5_w_glu, v_s5_b_glu, v_out_norm_fox, v_out_norm_s5, v_w_out, v_norm_cross, v_norm_mem, v_w_xq, v_w_xkv, v_xq_norm, v_xk_norm, v_w_xo, v_norm_ffn, v_w_ffn_up, v_ffn_conv_w, v_ffn_conv_b, v_w_ffn_down):
    given = dict(locals())
    w = {n: given[n] for n in WEIGHTS}
    m = {n: given["m_" + n] for n in WEIGHTS}
    v = {n: given["v_" + n] for n in WEIGHTS}
    xi, yi, ci = _place()
    chip = (2 * xi + yi).astype(jnp.int32)

    c_sel = ci.astype(jnp.int32).reshape(1)
    chip_sel = chip.reshape(1)
    early_kind = [n in COL_KIND for n in EARLY_WEIGHTS]
    late_kind = [n in COL_KIND for n in LATE_WEIGHTS]

    gathered, taps = _gather_weights([w[n][0].astype(BF16) for n in EARLY_WEIGHTS], early_kind, w["ffn_conv_w"][0])
    wb = {n: _full_from_gathered(n, gathered[k]) for k, n in enumerate(EARLY_WEIGHTS)}
    conv_w = taps.transpose(1, 0, 2).reshape(3, D_FF)
    late_shards = [w[n][0].astype(BF16) for n in LATE_WEIGHTS]
    late_full = [lax.empty((s.shape[0], 4 * s.shape[1]) if ck else (4 * s.shape[0], s.shape[1]), BF16)
                 for s, ck in zip(late_shards, late_kind)]
    gather_plan = _late_gather_plan(late_kind)
    g_send, g_recv, g_srcs, g_lands, g_started = _split_start(
        "gather_late_start", late_shards, late_full, 4 * len(LATE_WEIGHTS), gather_plan)

    def late_weights(after):
        _, full = _split_wait("gather_late_wait", g_send, g_recv, g_srcs, g_lands, after, gather_plan)
        return dict(zip(LATE_WEIGHTS, full))

    reduce_plan = _late_reduce_plan(late_kind)
    late_reduce = {}

    def early_grads(late_g):
        grads = [_reduce_layout(n, late_g[n]) for n in LATE_WEIGHTS]
        lands = [lax.empty((3, s.shape[0], s.shape[1] // 4) if ck else (3,) + s.shape[1:], BF16)
                 for s, ck in zip(grads, late_kind)]
        late_reduce["sems"] = _split_start("reduce_late_start", grads, lands, 3 * len(LATE_WEIGHTS), reduce_plan)
        return late_reduce["sems"][4][0:1, 0:1]

    p = {n: w[n][0] for n in SMALL}
    p["ffn_conv_w"] = conv_w
    for n in ("norm_mix", "fox_q_norm", "fox_k_norm", "fox_f_bias", "s5_b_glu", "out_norm_fox", "out_norm_s5",
              "norm_cross", "norm_mem", "xq_norm", "xk_norm", "norm_ffn", "ffn_conv_b"):
        p[n] = p[n].reshape(1, -1)
    p["norm_mix"] = p["norm_mix"] + g_started[0:1, 0:1]
    loss, grad_x, g = _local_step(x, mem, loss_target, p, wb, late_weights, early_grads)

    grads = [_reduce_layout(n, g[n]) for n in EARLY_WEIGHTS]
    from_sibling = _pair_exchange_halves("reduce_pair_exchange_early", grads, early_kind)
    pair_sums = [_pair_sum("reduce_pair_sum_" + n, c_sel, gr, rv, ck)
                 for n, gr, rv, ck in zip(EARLY_WEIGHTS, grads, from_sibling, early_kind)]
    early_lands = [lax.empty((3, s.shape[0], s.shape[1] // 4) if ck else (3,) + s.shape[1:], BF16)
                   for s, ck in zip(pair_sums, early_kind)]
    early_plan = _late_reduce_plan(early_kind)
    e_send, e_recv, e_srcs, e_lands, e_started = _split_start(
        "reduce_early_start", pair_sums, early_lands, 3 * len(EARLY_WEIGHTS), early_plan)

    out_g, out_d, out_m, out_v = {}, {}, {}, {}

    def finish(names, kinds, sums, from_chips, tag, halves):
        mine = [_chip_sum("reduce_chip_sum_" + n, chip_sel, ps, ck, fc)
                for n, ps, fc, ck in zip(names, sums, from_chips, kinds)]
        theirs = _pair_swap_halves("reduce_pair_swap_" + tag, mine)
        for n, a, b in zip(names, mine, theirs):
            out_g[n], out_d[n], out_m[n], out_v[n] = _adamw_big("adamw_" + n, c_sel, w[n], a, b, m[n], v[n], halves)

    r_send, r_recv, r_srcs, r_lands, _ = late_reduce["sems"]
    late_sums, late_from_chips = _split_wait("reduce_late_wait", r_send, r_recv, r_srcs, r_lands, e_started,
                                             reduce_plan)
    finish(LATE_WEIGHTS, late_kind, late_sums, late_from_chips, "late", False)

    small_names = list(SMALL) + ["ffn_conv_w"]
    small_vals = [g[n].reshape(w[n].shape if n != "ffn_conv_w" else (1, 3, D_FF)) for n in small_names]
    reduced = _allreduce_small(small_vals + [loss])
    loss_all = reduced[-1].reshape(())
    conv_w_grad = lax.dynamic_slice_in_dim(reduced[-2], chip * (D_FF // 4), D_FF // 4, axis=2)
    sg, sd, sm, sv = _adamw_small(
        [w[n] for n in small_names], list(reduced[:len(SMALL)]) + [conv_w_grad],
        [m[n] for n in small_names], [v[n] for n in small_names])
    out_g.update(zip(small_names, sg))
    out_d.update(zip(small_names, sd))
    out_m.update(zip(small_names, sm))
    out_v.update(zip(small_names, sv))

    early_sums, early_from_chips = _split_wait("reduce_early_wait", e_send, e_recv, e_srcs, e_lands, reduced[0],
                                               early_plan)
    finish(EARLY_WEIGHTS, early_kind, early_sums, early_from_chips, "early", True)

    return (loss_all, grad_x, *[out_g[n] for n in WEIGHTS], *[out_d[n] for n in WEIGHTS],
            *[out_m[n] for n in WEIGHTS], *[out_v[n] for n in WEIGHTS])
```

```python
import functools
import math

import jax
import jax.numpy as jnp
from jax import lax
from jax.experimental import pallas as pl
from jax.experimental.pallas import tpu as pltpu

F32 = jnp.float32
BF16 = jnp.bfloat16

D_MODEL = 1024
FOX_WIDTH = 512
HEAD_DIM = 64
N_FOX_HEADS = 8
S5_WIDTH = 512
S5_GROUP_CH = 16
S5_GROUPS = 32
S5_STATE = 64
S5_CH = S5_GROUPS * S5_STATE
N_X_HEADS = 4
X_HEAD_DIM = 256
N_MEM = 256
D_FF = 2816
UF_COLS = 640
EPS = 1e-6
ADAM_LR = 0.001
ADAM_B1 = 0.9
ADAM_B2 = 0.999
ADAM_EPS = 1e-08
ADAM_WD = 0.01
ADAM_STEP = 10

VMEM_LIMIT_BYTES = 56 * 1024 * 1024
MM_BLOCK_BYTES = 6 * 1024 * 1024
MM_VMEM_BYTES = 40 * 1024 * 1024
MESH = pl.DeviceIdType.MESH

EARLY_WEIGHTS = ("w_in", "s5_w_glu", "w_out")
LATE_WEIGHTS = ("w_xq", "w_xkv", "w_xo", "w_ffn_up", "w_ffn_down")
BIG = EARLY_WEIGHTS + LATE_WEIGHTS
COL_KIND = ("w_xkv", "w_ffn_up")
SMALL = ("norm_mix", "fox_q_norm", "fox_k_norm", "fox_f_bias", "s5_a_re", "s5_a_im", "s5_log_dt",
         "s5_b_re", "s5_b_im", "s5_c_re", "s5_c_im", "s5_d", "s5_b_glu", "out_norm_fox", "out_norm_s5",
         "norm_cross", "norm_mem", "xq_norm", "xk_norm", "norm_ffn", "ffn_conv_b")
WEIGHTS = ("norm_mix", "w_in", "fox_q_norm", "fox_k_norm", "fox_f_bias", "s5_a_re", "s5_a_im", "s5_log_dt",
           "s5_b_re", "s5_b_im", "s5_c_re", "s5_c_im", "s5_d", "s5_w_glu", "s5_b_glu", "out_norm_fox",
           "out_norm_s5", "w_out", "norm_cross", "norm_mem", "w_xq", "w_xkv", "xq_norm", "xk_norm", "w_xo",
           "norm_ffn", "w_ffn_up", "ffn_conv_w", "ffn_conv_b", "w_ffn_down")


def _params(sem=None):
    return pltpu.CompilerParams(dimension_semantics=sem, vmem_limit_bytes=VMEM_LIMIT_BYTES)


def _pick(n, cands):
    for c in cands:
        if n % c == 0:
            return c
    return n


_DIMS = {"nn": (((1,), (0,)), ((), ())), "nt": (((1,), (1,)), ((), ())), "tn": (((0,), (0,)), ((), ()))}


def _mm(a, b, mode, name, out_dtype=F32, res=None):
    if mode == "nn":
        (m, k), (k2, n) = a.shape, b.shape
    elif mode == "nt":
        (m, k), (n, k2) = a.shape, b.shape
    else:
        (k, m), (k2, n) = a.shape, b.shape
    assert k == k2, (name, a.shape, b.shape)

    has_res = res is not None
    a_size, b_size = a.dtype.itemsize, b.dtype.itemsize
    o_size = jnp.dtype(out_dtype).itemsize + (res.dtype.itemsize if has_res else 0)

    def tiles(dim):
        return [c for c in (1024, 512, 256, 128) if dim % c == 0] or [dim]

    best = None
    for tm in tiles(m):
        for tn in tiles(n):
            a_blk, b_blk = tm * k * a_size, tn * k * b_size
            if max(a_blk, b_blk) > MM_BLOCK_BYTES or 2 * (a_blk + b_blk + tm * tn * o_size) > MM_VMEM_BYTES:
                continue
            for rows_outer in (True, False):
                moved = (m * k * a_size + (m // tm) * n * k * b_size) if rows_outer else \
                        (n * k * b_size + (n // tn) * m * k * a_size)
                key = (moved, -(tm * tn))
                if best is None or key < best[0]:
                    best = (key, tm, tn, rows_outer)
    assert best is not None, (name, a.shape, b.shape)
    _, tm, tn, rows_outer = best
    ij = (lambda g0, g1: (g0, g1)) if rows_outer else (lambda g0, g1: (g1, g0))
    if mode == "tn":
        a_spec = pl.BlockSpec((k, tm), lambda g0, g1: (0, ij(g0, g1)[0]))
    else:
        a_spec = pl.BlockSpec((tm, k), lambda g0, g1: (ij(g0, g1)[0], 0))
    if mode == "nt":
        b_spec = pl.BlockSpec((tn, k), lambda g0, g1: (ij(g0, g1)[1], 0))
    else:
        b_spec = pl.BlockSpec((k, tn), lambda g0, g1: (0, ij(g0, g1)[1]))
    o_spec = pl.BlockSpec((tm, tn), lambda g0, g1: ij(g0, g1))
    grid = (m // tm, n // tn) if rows_outer else (n // tn, m // tm)
    dims = _DIMS[mode]

    def body(*refs):
        a_ref, b_ref = refs[0], refs[1]
        o_ref = refs[-1]
        acc = lax.dot_general(a_ref[...].astype(BF16), b_ref[...].astype(BF16), dims, preferred_element_type=F32)
        if has_res:
            acc = acc + refs[2][...].astype(F32)
        o_ref[...] = acc.astype(o_ref.dtype)

    return pl.pallas_call(
        body, name=name, grid=grid,
        in_specs=[a_spec, b_spec] + ([o_spec] if has_res else []),
        out_specs=o_spec, out_shape=jax.ShapeDtypeStruct((m, n), out_dtype),
        compiler_params=_params(("parallel", "parallel")),
    )(*((a, b, res) if has_res else (a, b)))


def _row_spec(tm, bc, off, step):
    return pl.BlockSpec((tm, bc), lambda i, h: (i, off + step * h))


ROW_TILE_ELEMS = 512 * 1024


def _row_tile(t, rows):
    widest = max(bc for (_, bc, _, _) in rows)
    return _pick(t, (min(t, ROW_TILE_ELEMS // widest), 512, 256, 128, 64, 8))


def _rowwise(fn, rows, pars, outs, name, heads=1):
    t = rows[0][0].shape[0]
    tm = _row_tile(t, rows)
    nr, npar = len(rows), len(pars)

    def body(*refs):
        vals = [r[...].astype(F32) for r in refs[:nr + npar]]
        res = fn(*vals)
        if not isinstance(res, (tuple, list)):
            res = (res,)
        for o_ref, v in zip(refs[nr + npar:], res):
            o_ref[...] = v.astype(o_ref.dtype)

    in_specs = [_row_spec(tm, bc, off, st) for (_, bc, off, st) in rows]
    in_specs += [pl.BlockSpec(p.shape, lambda i, h: (0, 0)) for p in pars]
    out_specs = [_row_spec(tm, bc, 0, st) for (_, bc, st, _) in outs]
    out_shape = [jax.ShapeDtypeStruct((t, c), dt) for (c, _, _, dt) in outs]
    res = pl.pallas_call(
        body, name=name, grid=(t // tm, heads), in_specs=in_specs, out_specs=out_specs, out_shape=out_shape,
        compiler_params=_params(("parallel", "parallel")),
    )(*[r[0] for r in rows], *pars)
    return res[0] if len(res) == 1 else res


def _rowwise_vjp(fn, rows, pars, cts, name, heads=1, adds=None, row_dtypes=None):
    t = rows[0][0].shape[0]
    tm = _row_tile(t, rows)
    nr, npar, nct = len(rows), len(pars), len(cts)
    adds = adds or [None] * nr
    add_list = [a for a in adds if a is not None]
    row_dtypes = row_dtypes or [F32] * nr

    def body(*refs):
        i, h = pl.program_id(0), pl.program_id(1)
        p = 0
        row_v = [r[...].astype(F32) for r in refs[p:p + nr]]; p += nr
        par_v = [r[...].astype(F32) for r in refs[p:p + npar]]; p += npar
        ct_v = [r[...].astype(F32) for r in refs[p:p + nct]]; p += nct
        add_refs = refs[p:p + len(add_list)]; p += len(add_list)
        drow_refs = refs[p:p + nr]; p += nr
        dpar_refs = refs[p:p + npar]

        def wrapped(*a):
            r = fn(*a)
            return tuple(r) if isinstance(r, (tuple, list)) else (r,)

        _, pull = jax.vjp(wrapped, *row_v, *par_v)
        grads = pull(tuple(ct_v))
        ai = 0
        for k in range(nr):
            g = grads[k]
            if adds[k] is not None:
                g = g + add_refs[ai][...].astype(F32)
                ai += 1
            drow_refs[k][...] = g.astype(drow_refs[k].dtype)

        @pl.when((i == 0) & (h == 0))
        def _():
            for r in dpar_refs:
                r[...] = jnp.zeros(r.shape, r.dtype)

        for k in range(npar):
            dpar_refs[k][...] += grads[nr + k]

    in_specs = [_row_spec(tm, bc, off, st) for (_, bc, off, st) in rows]
    in_specs += [pl.BlockSpec(q.shape, lambda i, h: (0, 0)) for q in pars]
    in_specs += [_row_spec(tm, bc, off, st) for (_, bc, off, st) in cts]
    in_specs += [_row_spec(tm, bc, off, st) for (_, bc, off, st) in add_list]
    out_specs = [_row_spec(tm, bc, 0, st) for (_, bc, _, st) in rows]
    out_specs += [pl.BlockSpec(q.shape, lambda i, h: (0, 0)) for q in pars]
    out_shape = [jax.ShapeDtypeStruct((t, bc * (heads if st else 1)), dt) for (_, bc, _, st), dt in zip(rows, row_dtypes)]
    out_shape += [jax.ShapeDtypeStruct(q.shape, F32) for q in pars]
    res = pl.pallas_call(
        body, name=name, grid=(t // tm, heads), in_specs=in_specs, out_specs=out_specs, out_shape=out_shape,
        compiler_params=_params(("arbitrary", "arbitrary")),
    )(*[r[0] for r in rows], *pars, *[c[0] for c in cts], *[a[0] for a in add_list])
    return list(res[:nr]), list(res[nr:])


def _rms(x, g):
    return x * lax.rsqrt(jnp.mean(x * x, axis=-1, keepdims=True) + EPS) * g


def _rms_pair(x, g):
    left = lax.broadcasted_iota(jnp.int32, x.shape, 1) < HEAD_DIM
    x2 = x * x
    ms_a = jnp.sum(jnp.where(left, x2, 0.0), axis=-1, keepdims=True) * (1.0 / HEAD_DIM)
    ms_b = jnp.sum(jnp.where(left, 0.0, x2), axis=-1, keepdims=True) * (1.0 / HEAD_DIM)
    return x * lax.rsqrt(jnp.where(left, ms_a, ms_b) + EPS) * g


def _gelu(x):
    return 0.5 * x * (1.0 + jnp.tanh(math.sqrt(2.0 / math.pi) * (x + 0.044715 * (x * x * x))))


def _s5_act(ys, u, d):
    return _gelu(ys + d * u)


def _s5_gate(yg, z, b, g):
    return _rms(yg * jax.nn.sigmoid(z + b), g)


def _lane_cumsum(x, reverse):
    n = x.shape[-1]
    lane = lax.broadcasted_iota(jnp.int32, x.shape, 1)
    k = 1
    while k < n:
        if reverse:
            x = x + jnp.where(lane < n - k, pltpu.roll(x, n - k, 1), 0.0)
        else:
            x = x + jnp.where(lane >= k, pltpu.roll(x, k, 1), 0.0)
        k *= 2
    return x


def _log_sigmoid(z):
    return jnp.minimum(z, 0.0) - jnp.log(1.0 + jnp.exp(-jnp.abs(z)))


def _forget_fwd(f, bias):
    def body(f_ref, b_ref, c_ref):
        c_ref[...] = _lane_cumsum(_log_sigmoid(f_ref[...] + b_ref[...]), False)

    return pl.pallas_call(body, name="forget_fwd", out_shape=jax.ShapeDtypeStruct(f.shape, F32),
                          compiler_params=_params())(f, bias)


def _forget_bwd(f, bias, dc):
    def body(f_ref, b_ref, dc_ref, df_ref, db_ref):
        dlog = _lane_cumsum(dc_ref[...], True)
        df = dlog * jax.nn.sigmoid(-(f_ref[...] + b_ref[...]))
        df_ref[...] = df
        db_ref[...] = jnp.sum(df, axis=1, keepdims=True)

    return pl.pallas_call(body, name="forget_bwd",
                          out_shape=(jax.ShapeDtypeStruct(f.shape, F32), jax.ShapeDtypeStruct(bias.shape, F32)),
                          compiler_params=_params())(f, bias, dc)


FOX_BLOCK = 256
FOX_KEYS = 256
_NT = _DIMS["nt"]
_TN = _DIMS["tn"]


N_PAIRS = N_FOX_HEADS // 2
V_BLOCK0 = 2 * N_PAIRS


def _left_lanes(shape):
    return lax.broadcasted_iota(jnp.int32, shape, 1) < HEAD_DIM


def _top_rows(shape):
    return lax.broadcasted_iota(jnp.int32, shape, 0) < HEAD_DIM


def _wide(c_tile, n):
    return c_tile if n == 128 else jnp.concatenate([c_tile] * (n // 128), axis=1)


def _fox_fwd(qn, kn, qkv, c_wide, seqs):
    t = qn.shape[0]
    l = t // seqs
    tb = min(FOX_BLOCK, l)
    tk = min(FOX_KEYS, tb)
    ratio = tb // tk
    nb = l // tb
    scale = HEAD_DIM ** -0.5

    def body(q_ref, k_ref, v_ref, ca_ref, cb_ref, o_ref, lse_ref, vt_ref):
        i = pl.program_id(2)
        top = _top_rows((128, tb))

        @pl.when(i == 0)
        def _():
            vt_ref[...] = v_ref[...].T.astype(BF16)

        qt = (q_ref[...].astype(F32) * scale).T.astype(BF16)
        zero = jnp.zeros_like(qt)
        qts = (jnp.where(top, qt, zero), jnp.where(top, zero, qt))
        top_k = _top_rows((128, tk))
        zero_k = jnp.zeros((128, tk), BF16)
        key_pos = lax.broadcasted_iota(jnp.int32, (tk, tb), 0)
        query_pos = lax.broadcasted_iota(jnp.int32, (tk, tb), 1)
        c_refs = (ca_ref, cb_ref)

        def scores(j):
            off = pl.multiple_of(j * tk, tk)
            k2 = k_ref[pl.ds(off, tk), :]
            return tuple(jnp.dot(k2, qts[h], preferred_element_type=F32) - _wide(c_refs[h][pl.ds(off, tk), :], tb)
                         for h in (0, 1))

        def values_times(ps, j):
            vt = vt_ref[:, pl.ds(pl.multiple_of(j * tk, tk), tk)]
            return (jnp.dot(jnp.where(top_k, vt, zero_k), ps[0], preferred_element_type=F32)
                    + jnp.dot(jnp.where(top_k, zero_k, vt), ps[1], preferred_element_type=F32))

        def softmax_step(sts, stats, first_key):
            ps, new, alphas = [], [], []
            for st, (m, s_sum) in zip(sts, stats):
                if first_key is not None:
                    st = jnp.where(key_pos + first_key <= query_pos, st, -jnp.inf)
                m_new = jnp.maximum(m, jnp.max(st, axis=0, keepdims=True))
                alpha = jnp.exp(m - m_new)
                p = jnp.exp(st - m_new)
                new.append((m_new, alpha * s_sum + jnp.sum(p, axis=0, keepdims=True)))
                alphas.append(alpha)
                ps.append(p.astype(BF16))
            return tuple(ps), tuple(new), jnp.where(top, alphas[0], alphas[1])

        def step(j, carry):
            sts, ps_prev, stats, acc = carry
            sts_next = scores(j + 1)
            acc = acc + values_times(ps_prev, jnp.maximum(j - 1, 0))
            ps, stats, alpha = softmax_step(sts, stats, None)
            return sts_next, ps, stats, alpha * acc

        stat = (jnp.full((1, tb), -jnp.inf, F32), jnp.zeros((1, tb), F32))
        no_p = jnp.zeros((tk, tb), BF16)
        below = i * ratio
        sts, ps_prev, stats, acc = lax.fori_loop(
            0, below, step, (scores(0), (no_p, no_p), (stat, stat), jnp.zeros((128, tb), F32)))
        for r in range(ratio):
            sts_next = scores(below + r + 1) if r + 1 < ratio else None
            acc = acc + values_times(ps_prev, jnp.maximum(below + r - 1, 0))
            ps_prev, stats, alpha = softmax_step(sts, stats, r * tk)
            acc = alpha * acc
            sts = sts_next
        acc = acc + values_times(ps_prev, below + ratio - 1)
        (ma, sa), (mb, sb) = stats
        o_ref[...] = (acc / jnp.where(top, sa, sb)).T
        lse_ref[0:1, :] = ma + jnp.log(sa)
        lse_ref[1:2, :] = mb + jnp.log(sb)

    qblk = pl.BlockSpec((tb, 128), lambda b, hp, i: (b * nb + i, hp))
    return pl.pallas_call(
        body, name="fox_fwd", grid=(seqs, N_PAIRS, nb),
        in_specs=[qblk, pl.BlockSpec((l, 128), lambda b, hp, i: (b, hp)),
                  pl.BlockSpec((l, 128), lambda b, hp, i: (b, V_BLOCK0 + hp)),
                  pl.BlockSpec((None, l, 128), lambda b, hp, i: (b * N_FOX_HEADS + 2 * hp, 0, 0)),
                  pl.BlockSpec((None, l, 128), lambda b, hp, i: (b * N_FOX_HEADS + 2 * hp + 1, 0, 0))],
        out_specs=[qblk, pl.BlockSpec((None, 2, tb), lambda b, hp, i: (b * N_PAIRS + hp, 0, i))],
        out_shape=[jax.ShapeDtypeStruct((t, FOX_WIDTH), F32), jax.ShapeDtypeStruct((seqs * N_PAIRS, 2, l), F32)],
        scratch_shapes=[pltpu.VMEM((128, l), BF16)],
        compiler_params=_params(("parallel", "parallel", "arbitrary")),
    )(qn, kn, qkv, c_wide, c_wide)


def _fox_bwd(qn, kn, qkv, c_wide, o, do, lse, seqs):
    t = qn.shape[0]
    l = t // seqs
    tb = min(FOX_BLOCK, l)
    nb = l // tb
    scale = HEAD_DIM ** -0.5
    one_at = (HEAD_DIM, 0)

    def body(q_ref, k_ref, v_ref, ca_ref, cb_ref, o_ref, do_ref, lse_ref, dq_ref, dk_ref, dv_ref, dc_ref, dcq_ref,
             qt_ref, kt_ref, dot_ref, delta_ref, dqa_ref, dqb_ref):
        top_l = _top_rows((128, l))
        top = _top_rows((128, tb))
        left = _left_lanes((tb, 128))
        row_id = lax.broadcasted_iota(jnp.int32, (128, tb), 0)
        lane_id = lax.broadcasted_iota(jnp.int32, (tb, 128), 1)
        zero_t = jnp.zeros((128, tb), BF16)
        zero_l = jnp.zeros((tb, 128), BF16)
        rows = lambda a: (jnp.where(top, a, zero_t), jnp.where(top, zero_t, a))
        lanes = lambda a: (jnp.where(left, a, zero_l), jnp.where(left, zero_l, a))
        with_one_row = lambda pair: tuple(jnp.where(row_id == one_at[h], 1.0, pair[h]).astype(BF16) for h in (0, 1))
        with_one_lane = lambda pair: tuple(jnp.where(lane_id == one_at[h], 1.0, pair[h]).astype(BF16) for h in (0, 1))
        causal = lax.broadcasted_iota(jnp.int32, (tb, tb), 0) <= lax.broadcasted_iota(jnp.int32, (tb, tb), 1)
        c_refs = (ca_ref, cb_ref)
        dq_refs = (dqa_ref, dqb_ref)

        qt_ref[...] = (q_ref[...].astype(F32) * scale).T.astype(BF16)
        kt_ref[...] = k_ref[...].astype(F32).T.astype(BF16)
        do_t = do_ref[...].T
        dot_ref[...] = do_t.astype(BF16)
        prod_t = do_t * o_ref[...].T
        delta_ref[0:1, :] = jnp.sum(jnp.where(top_l, prod_t, 0.0), axis=0, keepdims=True)
        delta_ref[1:2, :] = jnp.sum(jnp.where(top_l, 0.0, prod_t), axis=0, keepdims=True)
        dqa_ref[...] = jnp.zeros(dqa_ref.shape, F32)
        dqb_ref[...] = jnp.zeros(dqb_ref.shape, F32)

        def kv_block(j, _):
            koff = pl.multiple_of(j * tb, tb)
            k2 = k_ref[pl.ds(koff, tb), :]
            v2 = v_ref[pl.ds(koff, tb), :].astype(BF16)
            kts = with_one_row(rows(kt_ref[:, pl.ds(koff, tb)]))
            cw = tuple(_wide(c_refs[h][pl.ds(koff, tb), :], tb) for h in (0, 1))

            def q_block(i, carry, masked):
                dks, dv = list(carry[:2]), carry[2]
                qoff = pl.multiple_of(i * tb, tb)
                qs = lanes((q_ref[pl.ds(qoff, tb), :].astype(F32) * scale).astype(BF16))
                qs_one = with_one_lane(qs)
                dos = lanes(do_ref[pl.ds(qoff, tb), :].astype(BF16))
                qts = rows(qt_ref[:, pl.ds(qoff, tb)])
                dots = rows(dot_ref[:, pl.ds(qoff, tb)])
                for h in (0, 1):
                    st = jnp.dot(k2, qts[h], preferred_element_type=F32) - cw[h]
                    p = jnp.exp(st - lse_ref[h:h + 1, pl.ds(qoff, tb)])
                    if masked:
                        p = jnp.where(causal, p, 0.0)
                    dp = jnp.dot(v2, dots[h], preferred_element_type=F32)
                    dsb = (p * (dp - delta_ref[h:h + 1, pl.ds(qoff, tb)])).astype(BF16)
                    dv = dv + jnp.dot(p.astype(BF16), dos[h], preferred_element_type=F32)
                    dks[h] = dks[h] + jnp.dot(dsb, qs_one[h], preferred_element_type=F32)
                    dq_refs[h][:, pl.ds(qoff, tb)] += jnp.dot(kts[h], dsb, preferred_element_type=F32)
                return dks[0], dks[1], dv

            z = jnp.zeros((tb, 128), F32)
            carry = q_block(j, (z, z, z), True)
            dka, dkb, dv = lax.fori_loop(j + 1, nb, lambda i, c: q_block(i, c, False), carry)
            dk_ref[pl.ds(koff, tb), :] = jnp.where(left, dka, dkb)
            dv_ref[pl.ds(koff, tb), :] = dv
            dc_ref[0:1, pl.ds(koff, tb)] = -dka.T[one_at[0]:one_at[0] + 1, :]
            dc_ref[1:2, pl.ds(koff, tb)] = -dkb.T[one_at[1]:one_at[1] + 1, :]
            return 0

        lax.fori_loop(0, nb, kv_block, 0)
        dq_ref[...] = (jnp.where(top_l, dqa_ref[...], dqb_ref[...]) * scale).T
        dcq_ref[0:1, :] = dqa_ref[one_at[0]:one_at[0] + 1, :]
        dcq_ref[1:2, :] = dqb_ref[one_at[1]:one_at[1] + 1, :]

    blk = pl.BlockSpec((l, 128), lambda b, hp: (b, hp))
    cspec = lambda k: pl.BlockSpec((None, l, 128), lambda b, hp: (b * N_FOX_HEADS + 2 * hp + k, 0, 0))
    rows2 = pl.BlockSpec((None, 2, l), lambda b, hp: (b * N_PAIRS + hp, 0, 0))
    wide = jax.ShapeDtypeStruct((t, FOX_WIDTH), F32)
    pair_rows = jax.ShapeDtypeStruct((seqs * N_PAIRS, 2, l), F32)
    return pl.pallas_call(
        body, name="fox_bwd", grid=(seqs, N_PAIRS),
        in_specs=[blk, blk, pl.BlockSpec((l, 128), lambda b, hp: (b, V_BLOCK0 + hp)), cspec(0), cspec(1), blk, blk, rows2],
        out_specs=[blk, blk, blk, rows2, rows2],
        out_shape=[wide, wide, wide, pair_rows, pair_rows],
        scratch_shapes=[pltpu.VMEM((128, l), BF16), pltpu.VMEM((128, l), BF16), pltpu.VMEM((128, l), BF16),
                        pltpu.VMEM((2, l), F32), pltpu.VMEM((128, l), F32), pltpu.VMEM((128, l), F32)],
        compiler_params=_params(("parallel", "parallel")),
    )(qn, kn, qkv, c_wide, c_wide, o, do, lse)


SCAN_ROWS = 256
SCAN_COLS = 1024


S5_IN = 128
S5_ST = 512
SCAN_CHUNKS = SCAN_COLS // S5_ST
SCAN_SEGS = 8
LANES = 128


def _cmul(ar, ai, br, bi):
    return ar * br - ai * bi, ar * bi + ai * br


def _powers_into(pw_r, pw_i, a_r, a_i, seg):
    pw_r[0:1, :] = a_r
    pw_i[0:1, :] = a_i
    for k in range(1, seg):
        pr, pi = _cmul(pw_r[k - 1:k, :], pw_i[k - 1:k, :], a_r, a_i)
        pw_r[k:k + 1, :] = pr
        pw_i[k:k + 1, :] = pi


def _interleave(dst, src, seg):
    for h in range(src.shape[0]):
        for j in range(seg):
            dst[h, j * SCAN_SEGS:(j + 1) * SCAN_SEGS, :] = src[h, pl.ds(j, SCAN_SEGS, stride=seg), :]


def _deinterleave(dst, src, seg):
    for h in range(src.shape[0]):
        for j in range(seg):
            dst[h, pl.ds(j, SCAN_SEGS, stride=seg), :] = src[h, j * SCAN_SEGS:(j + 1) * SCAN_SEGS, :]


def _interleaved(ref, tmp_a, tmp_b, seg):
    n = ref.shape[1] // LANES
    for h in range(n):
        tmp_a[h] = ref[:, h * LANES:(h + 1) * LANES].astype(F32)
    _interleave(tmp_b, tmp_a, seg)
    return jnp.concatenate([tmp_b[h] for h in range(n)], axis=1)


def _store_deinterleaved(ref, val, tmp_a, tmp_b, seg):
    n = ref.shape[1] // LANES
    for h in range(n):
        tmp_a[h] = val[:, h * LANES:(h + 1) * LANES]
    _deinterleave(tmp_b, tmp_a, seg)
    for h in range(n):
        ref[:, h * LANES:(h + 1) * LANES] = tmp_b[h]


def _segment_scan(b_r, b_i, x_r, x_i, pw_r, pw_i, car_r, car_i, seg, sign, reverse, visit=None):
    nc = b_r.shape[0]
    sub = lax.broadcasted_iota(jnp.int32, (SCAN_SEGS, LANES), 0)
    lanes = lambda c: slice(c * LANES, (c + 1) * LANES)
    rows = lambda j: pl.ds(pl.multiple_of(((seg - 1 - j) if reverse else j) * SCAN_SEGS, SCAN_SEGS), SCAN_SEGS)
    a1 = [(pw_r[0:1, lanes(c)], sign * pw_i[0:1, lanes(c)]) for c in range(nc)]

    def local(j, xs):
        out = []
        for c in range(nc):
            xr, xi = xs[2 * c], xs[2 * c + 1]
            nr = a1[c][0] * xr - a1[c][1] * xi + b_r[c, rows(j), :]
            ni = a1[c][0] * xi + a1[c][1] * xr + b_i[c, rows(j), :]
            x_r[c, rows(j), :] = nr
            x_i[c, rows(j), :] = ni
            out += [nr, ni]
        return tuple(out)

    zero = jnp.zeros((SCAN_SEGS, LANES), F32)
    ends = lax.fori_loop(0, seg, local, (zero,) * (2 * nc))

    if reverse:
        first = sub == SCAN_SEGS - 1
        neighbour = lambda v: pltpu.roll(v, SCAN_SEGS - 1, 0)
        shift = lambda v, d: jnp.where(sub < SCAN_SEGS - d, pltpu.roll(v, SCAN_SEGS - d, 0), 0.0)
    else:
        first = sub == 0
        neighbour = lambda v: pltpu.roll(v, 1, 0)
        shift = lambda v, d: jnp.where(sub >= d, pltpu.roll(v, d, 0), 0.0)
    last = 0 if reverse else SCAN_SEGS - 1
    entries = []
    for c in range(nc):
        er, ei = ends[2 * c], ends[2 * c + 1]
        pr, pi = pw_r[seg - 1:seg, lanes(c)], sign * pw_i[seg - 1:seg, lanes(c)]
        yr = jnp.where(first, car_r[:, lanes(c)], neighbour(er))
        yi = jnp.where(first, car_i[:, lanes(c)], neighbour(ei))
        qr, qi = pr, pi
        for d in (1, 2, 4):
            mr, mi = _cmul(qr, qi, shift(yr, d), shift(yi, d))
            yr, yi = yr + mr, yi + mi
            qr, qi = _cmul(qr, qi, qr, qi)
        lr, li = _cmul(pr, pi, yr, yi)
        car_r[:, lanes(c)] = (er + lr)[last:last + 1, :]
        car_i[:, lanes(c)] = (ei + li)[last:last + 1, :]
        entries += [yr, yi]

    def correct(j, prev):
        out = []
        row_r, row_i = pw_r[pl.ds(j, 1), :], sign * pw_i[pl.ds(j, 1), :]
        for c in range(nc):
            mr, mi = _cmul(row_r[:, lanes(c)], row_i[:, lanes(c)], entries[2 * c], entries[2 * c + 1])
            nr = x_r[c, rows(j), :] + mr
            ni = x_i[c, rows(j), :] + mi
            x_r[c, rows(j), :] = nr
            x_i[c, rows(j), :] = ni
            if visit is not None:
                visit(c, rows(j), prev[2 * c], prev[2 * c + 1])
            out += [nr, ni]
        return tuple(out)

    lax.fori_loop(0, seg, correct, tuple(entries))


def _s5_fwd(uf, bbr, bbi, cr, ci, ar, ai, seqs):
    t = uf.shape[0]
    l = t // seqs
    tl = min(SCAN_ROWS, l)
    nl = l // tl
    seg = tl // SCAN_SEGS
    cb, nq = SCAN_COLS, SCAN_CHUNKS
    nc = cb // LANES
    per = S5_ST // LANES

    def body(u_ref, bbr_ref, bbi_ref, cr_ref, ci_ref, ar_ref, ai_ref, xr_ref, xi_ref, ys_ref,
             car_r, car_i, pw_r, pw_i, b_r, b_i, x_r, x_i, tmp_a, tmp_b):
        @pl.when(pl.program_id(2) == 0)
        def _():
            car_r[...] = jnp.zeros(car_r.shape, F32)
            car_i[...] = jnp.zeros(car_i.shape, F32)
            _powers_into(pw_r, pw_i, ar_ref[...], ai_ref[...], seg)

        u = _interleaved(u_ref, tmp_a, tmp_b, seg).astype(BF16)
        for q in range(nq):
            uq = u[:, q * S5_IN:(q + 1) * S5_IN]
            br = jnp.dot(uq, bbr_ref[q], preferred_element_type=F32)
            bi = jnp.dot(uq, bbi_ref[q], preferred_element_type=F32)
            for s in range(per):
                b_r[q * per + s] = br[:, s * LANES:(s + 1) * LANES]
                b_i[q * per + s] = bi[:, s * LANES:(s + 1) * LANES]
        _segment_scan(b_r, b_i, x_r, x_i, pw_r, pw_i, car_r, car_i, seg, 1.0, False)
        for c in range(nc):
            xr_ref[:, c * LANES:(c + 1) * LANES] = x_r[c]
            xi_ref[:, c * LANES:(c + 1) * LANES] = x_i[c]
        ys = []
        for q in range(nq):
            xq_r = xr_ref[:, q * S5_ST:(q + 1) * S5_ST].astype(BF16)
            xq_i = xi_ref[:, q * S5_ST:(q + 1) * S5_ST].astype(BF16)
            ys.append(jnp.dot(xq_r, cr_ref[q], preferred_element_type=F32)
                      + jnp.dot(xq_i, ci_ref[q], preferred_element_type=F32))
        _store_deinterleaved(ys_ref, jnp.concatenate(ys, axis=1), tmp_a, tmp_b, seg)

    rows = lambda w: pl.BlockSpec((tl, w), lambda s, j, r: (s * nl + r, j))
    chunk = lambda a: pl.BlockSpec((nq,) + a.shape[1:], lambda s, j, r: (j, 0, 0))
    par = pl.BlockSpec((1, cb), lambda s, j, r: (0, j))
    return pl.pallas_call(
        body, name="s5_fwd", grid=(seqs, S5_CH // cb, nl),
        in_specs=[rows(nq * S5_IN), chunk(bbr), chunk(bbi), chunk(cr), chunk(ci), par, par],
        out_specs=[rows(cb), rows(cb), rows(nq * S5_IN)],
        out_shape=[jax.ShapeDtypeStruct((t, S5_CH), F32)] * 2 + [jax.ShapeDtypeStruct((t, S5_WIDTH), F32)],
        scratch_shapes=[pltpu.VMEM((1, cb), F32), pltpu.VMEM((1, cb), F32), pltpu.VMEM((seg, cb), F32),
                        pltpu.VMEM((seg, cb), F32)] + [pltpu.VMEM((nc, tl, LANES), F32)] * 4
        + [pltpu.VMEM((nq * S5_IN // LANES, tl, LANES), F32)] * 2,
        compiler_params=_params(("parallel", "parallel", "arbitrary")),
    )(uf, bbr, bbi, cr, ci, ar, ai)


def _s5_bwd(dys, uf, xr, xi, bbr, bbi, cr, ci, ar, ai, seqs):
    t = dys.shape[0]
    l = t // seqs
    tl = min(SCAN_ROWS, l)
    nl = l // tl
    seg = tl // SCAN_SEGS
    cb, nq = SCAN_COLS, SCAN_CHUNKS
    nc = cb // LANES
    per = S5_ST // LANES

    def body(dy_ref, u_ref, xr_ref, xi_ref, bbr_ref, bbi_ref, cr_ref, ci_ref, ar_ref, ai_ref,
             du_ref, dbbr_ref, dbbi_ref, dcr_ref, dci_ref, dar_ref, dai_ref,
             car_r, car_i, pw_r, pw_i, g_r, g_i, lam_r, lam_i, x_r, x_i, acc_r, acc_i, tmp_a, tmp_b):
        @pl.when(pl.program_id(2) == 0)
        def _():
            car_r[...] = jnp.zeros(car_r.shape, F32)
            car_i[...] = jnp.zeros(car_i.shape, F32)
            _powers_into(pw_r, pw_i, ar_ref[...], ai_ref[...], seg)
            for acc_ref in (dbbr_ref, dbbi_ref, dcr_ref, dci_ref, dar_ref, dai_ref):
                acc_ref[...] = jnp.zeros(acc_ref.shape, F32)

        dy = _interleaved(dy_ref, tmp_a, tmp_b, seg).astype(BF16)
        for q in range(nq):
            dyq = dy[:, q * S5_IN:(q + 1) * S5_IN]
            gr = lax.dot_general(dyq, cr_ref[q], _NT, preferred_element_type=F32)
            gi = lax.dot_general(dyq, ci_ref[q], _NT, preferred_element_type=F32)
            for s in range(per):
                g_r[q * per + s] = gr[:, s * LANES:(s + 1) * LANES]
                g_i[q * per + s] = gi[:, s * LANES:(s + 1) * LANES]
        for c in range(nc):
            x_r[c] = xr_ref[:, c * LANES:(c + 1) * LANES]
            x_i[c] = xi_ref[:, c * LANES:(c + 1) * LANES]
        acc_r[...] = jnp.zeros(acc_r.shape, F32)
        acc_i[...] = jnp.zeros(acc_i.shape, F32)

        def visit(c, rws, lr, li):
            xr_t, xi_t = x_r[c, rws, :], x_i[c, rws, :]
            acc_r[c] += lr * xr_t + li * xi_t
            acc_i[c] += li * xr_t - lr * xi_t

        _segment_scan(g_r, g_i, lam_r, lam_i, pw_r, pw_i, car_r, car_i, seg, -1.0, True, visit)
        for c in range(nc):
            dar_ref[:, c * LANES:(c + 1) * LANES] += jnp.sum(acc_r[c], axis=0, keepdims=True)
            dai_ref[:, c * LANES:(c + 1) * LANES] += jnp.sum(acc_i[c], axis=0, keepdims=True)
        u = _interleaved(u_ref, tmp_a, tmp_b, seg).astype(BF16)
        du = []
        for q in range(nq):
            st = slice(q * S5_ST, (q + 1) * S5_ST)
            io = slice(q * S5_IN, (q + 1) * S5_IN)
            lq_r = jnp.concatenate([lam_r[q * per + s] for s in range(per)], axis=1).astype(BF16)
            lq_i = jnp.concatenate([lam_i[q * per + s] for s in range(per)], axis=1).astype(BF16)
            du.append(lax.dot_general(lq_r, bbr_ref[q], _NT, preferred_element_type=F32)
                      + lax.dot_general(lq_i, bbi_ref[q], _NT, preferred_element_type=F32))
            dbbr_ref[q] += lax.dot_general(u[:, io], lq_r, _TN, preferred_element_type=F32)
            dbbi_ref[q] += lax.dot_general(u[:, io], lq_i, _TN, preferred_element_type=F32)
            dcr_ref[q] += lax.dot_general(xr_ref[:, st].astype(BF16), dy[:, io], _TN, preferred_element_type=F32)
            dci_ref[q] += lax.dot_general(xi_ref[:, st].astype(BF16), dy[:, io], _TN, preferred_element_type=F32)
        _store_deinterleaved(du_ref, jnp.concatenate(du, axis=1), tmp_a, tmp_b, seg)

    rows = lambda w: pl.BlockSpec((tl, w), lambda s, j, r: (s * nl + nl - 1 - r, j))
    chunk = lambda a: pl.BlockSpec((nq,) + a.shape[1:], lambda s, j, r: (j, 0, 0))
    acc = lambda a: pl.BlockSpec((None, nq) + a.shape[1:], lambda s, j, r: (s, j, 0, 0))
    par = pl.BlockSpec((1, cb), lambda s, j, r: (0, j))
    par_acc = pl.BlockSpec((None, 1, cb), lambda s, j, r: (s, 0, j))
    per_seq = lambda a: jax.ShapeDtypeStruct((seqs,) + a.shape, F32)
    return pl.pallas_call(
        body, name="s5_bwd", grid=(seqs, S5_CH // cb, nl),
        in_specs=[rows(nq * S5_IN), rows(nq * S5_IN), rows(cb), rows(cb), chunk(bbr), chunk(bbi), chunk(cr), chunk(ci),
                  par, par],
        out_specs=[rows(nq * S5_IN), acc(bbr), acc(bbi), acc(cr), acc(ci), par_acc, par_acc],
        out_shape=[jax.ShapeDtypeStruct((t, S5_WIDTH), F32), per_seq(bbr), per_seq(bbi), per_seq(cr), per_seq(ci),
                   jax.ShapeDtypeStruct((seqs, 1, S5_CH), F32), jax.ShapeDtypeStruct((seqs, 1, S5_CH), F32)],
        scratch_shapes=[pltpu.VMEM((1, cb), F32), pltpu.VMEM((1, cb), F32), pltpu.VMEM((seg, cb), F32),
                        pltpu.VMEM((seg, cb), F32)] + [pltpu.VMEM((nc, tl, LANES), F32)] * 6
        + [pltpu.VMEM((nc, SCAN_SEGS, LANES), F32)] * 2 + [pltpu.VMEM((nq * S5_IN // LANES, tl, LANES), F32)] * 2,
        compiler_params=_params(("parallel", "parallel", "arbitrary")),
    )(dys, uf, xr, xi, bbr, bbi, cr, ci, ar, ai)


XATT_BLOCK = 512


def _xatt_probs(qv, kv):
    s = lax.dot_general(qv, kv, _NT, preferred_element_type=F32) * (X_HEAD_DIM ** -0.5)
    e = jnp.exp(s - jnp.max(s, axis=-1, keepdims=True))
    return e / jnp.sum(e, axis=-1, keepdims=True)


def _xatt_fwd(q, k, kv, seqs):
    t = q.shape[0]
    tq = min(XATT_BLOCK, t // seqs)
    nq = t // seqs // tq

    def body(q_ref, k_ref, v_ref, o_ref):
        p = _xatt_probs(q_ref[...], k_ref[...])
        o_ref[...] = jnp.dot(p.astype(BF16), v_ref[...].astype(BF16), preferred_element_type=F32).astype(o_ref.dtype)

    qs = pl.BlockSpec((tq, X_HEAD_DIM), lambda b, h, i: (b * nq + i, h))
    return pl.pallas_call(
        body, name="xatt_fwd", grid=(seqs, N_X_HEADS, nq),
        in_specs=[qs, pl.BlockSpec((N_MEM, X_HEAD_DIM), lambda b, h, i: (b, h)),
                  pl.BlockSpec((N_MEM, X_HEAD_DIM), lambda b, h, i: (b, N_X_HEADS + h))],
        out_specs=qs, out_shape=jax.ShapeDtypeStruct(q.shape, BF16),
        compiler_params=_params(("parallel", "parallel", "parallel")),
    )(q, k, kv)


def _xatt_bwd(q, k, kv, do, seqs):
    t = q.shape[0]
    tq = min(XATT_BLOCK, t // seqs)
    nq = t // seqs // tq
    scale = X_HEAD_DIM ** -0.5

    def body(q_ref, k_ref, v_ref, do_ref, dq_ref, dk_ref, dv_ref):
        @pl.when(pl.program_id(2) == 0)
        def _():
            dk_ref[...] = jnp.zeros(dk_ref.shape, F32)
            dv_ref[...] = jnp.zeros(dv_ref.shape, F32)

        qv, kk = q_ref[...], k_ref[...]
        p = _xatt_probs(qv, kk)
        dob = do_ref[...].astype(BF16)
        dp = lax.dot_general(dob, v_ref[...].astype(BF16), _NT, preferred_element_type=F32)
        ds = p * (dp - jnp.sum(dp * p, axis=-1, keepdims=True))
        dsb = ds.astype(BF16)
        dq_ref[...] = jnp.dot(dsb, kk, preferred_element_type=F32) * scale
        dk_ref[...] += lax.dot_general(dsb, qv, _TN, preferred_element_type=F32) * scale
        dv_ref[...] += lax.dot_general(p.astype(BF16), dob, _TN, preferred_element_type=F32)

    qs = pl.BlockSpec((tq, X_HEAD_DIM), lambda b, h, i: (b * nq + i, h))
    ks = pl.BlockSpec((N_MEM, X_HEAD_DIM), lambda b, h, i: (b, h))
    return pl.pallas_call(
        body, name="xatt_bwd", grid=(seqs, N_X_HEADS, nq),
        in_specs=[qs, ks, pl.BlockSpec((N_MEM, X_HEAD_DIM), lambda b, h, i: (b, N_X_HEADS + h)), qs],
        out_specs=[qs, ks, ks],
        out_shape=[jax.ShapeDtypeStruct(q.shape, F32), jax.ShapeDtypeStruct(k.shape, F32),
                   jax.ShapeDtypeStruct(k.shape, F32)],
        compiler_params=_params(("parallel", "parallel", "arbitrary")),
    )(q, k, kv, do)


CONV_COLS = 256


def _shift_down(x, k, row):
    return jnp.where(row >= k, pltpu.roll(x, k, 0), 0.0)


def _shift_up(x, k, row):
    n = x.shape[0]
    return jnp.where(row < n - k, pltpu.roll(x, n - k, 0), 0.0)


def _conv_pre(g, w, b, row):
    return b + w[0:1, :] * _shift_down(g, 2, row) + w[1:2, :] * _shift_down(g, 1, row) + w[2:3, :] * g


def _convgate_fwd(gu, w, b, seqs):
    t = gu.shape[0]
    l = t // seqs
    nc = D_FF // CONV_COLS

    def body(g_ref, u_ref, w_ref, b_ref, o_ref):
        g = g_ref[...].astype(F32)
        row = lax.broadcasted_iota(jnp.int32, g.shape, 0)
        pre = _conv_pre(g, w_ref[...], b_ref[...], row)
        o_ref[...] = (pre * jax.nn.sigmoid(pre) * u_ref[...].astype(F32)).astype(o_ref.dtype)

    return pl.pallas_call(
        body, name="convgate_fwd", grid=(seqs, nc),
        in_specs=[pl.BlockSpec((l, CONV_COLS), lambda s, j: (s, j)), pl.BlockSpec((l, CONV_COLS), lambda s, j: (s, nc + j)),
                  pl.BlockSpec((3, CONV_COLS), lambda s, j: (0, j)), pl.BlockSpec((1, CONV_COLS), lambda s, j: (0, j))],
        out_specs=pl.BlockSpec((l, CONV_COLS), lambda s, j: (s, j)),
        out_shape=jax.ShapeDtypeStruct((t, D_FF), BF16),
        compiler_params=_params(("parallel", "parallel")),
    )(gu, gu, w, b)


def _convgate_bwd(gu, w, b, dact, seqs):
    t = gu.shape[0]
    l = t // seqs
    nc = D_FF // CONV_COLS
    steps = nc * seqs

    def body(g_ref, u_ref, w_ref, b_ref, da_ref, dgu_ref, dw_ref, db_ref, stage, sems):
        j, s = pl.program_id(0), pl.program_id(1)
        n = j * seqs + s
        slot = n % 2

        def copies(slot_, j_, s_):
            rows = pl.ds(pl.multiple_of(s_ * l, 16), l)
            return [pltpu.make_async_copy(
                stage.at[slot_, half],
                dgu_ref.at[rows, pl.ds(pl.multiple_of((half * nc + j_) * CONV_COLS, 128), CONV_COLS)],
                sems.at[slot_, half]) for half in (0, 1)]

        @pl.when(s == 0)
        def _():
            dw_ref[...] = jnp.zeros(dw_ref.shape, F32)
            db_ref[...] = jnp.zeros(db_ref.shape, F32)

        @pl.when(n >= 2)
        def _():
            for cp in copies(slot, j, s):
                cp.wait()

        g, wv, da = g_ref[...].astype(F32), w_ref[...], da_ref[...].astype(F32)
        row = lax.broadcasted_iota(jnp.int32, g.shape, 0)
        g1, g2 = _shift_down(g, 1, row), _shift_down(g, 2, row)
        pre = b_ref[...] + wv[0:1, :] * g2 + wv[1:2, :] * g1 + wv[2:3, :] * g
        sg = jax.nn.sigmoid(pre)
        silu = pre * sg
        stage[slot, 1] = (da * silu).astype(stage.dtype)
        dpre = da * u_ref[...].astype(F32) * (sg * (1.0 + pre * (1.0 - sg)))
        dg = wv[2:3, :] * dpre + wv[1:2, :] * _shift_up(dpre, 1, row) + wv[0:1, :] * _shift_up(dpre, 2, row)
        stage[slot, 0] = dg.astype(stage.dtype)
        for cp in copies(slot, j, s):
            cp.start()
        dw_ref[0:1, :] += jnp.sum(dpre * g2, axis=0, keepdims=True)
        dw_ref[1:2, :] += jnp.sum(dpre * g1, axis=0, keepdims=True)
        dw_ref[2:3, :] += jnp.sum(dpre * g, axis=0, keepdims=True)
        db_ref[...] += jnp.sum(dpre, axis=0, keepdims=True)

        @pl.when(n == steps - 1)
        def _():
            for cp in copies(slot, j, s) + (copies(1 - slot, j, s) if steps > 1 else []):
                cp.wait()

    blk = lambda off: pl.BlockSpec((l, CONV_COLS), lambda j, s: (s, off + j))
    return pl.pallas_call(
        body, name="convgate_bwd", grid=(nc, seqs),
        in_specs=[blk(0), blk(nc), pl.BlockSpec((3, CONV_COLS), lambda j, s: (0, j)),
                  pl.BlockSpec((1, CONV_COLS), lambda j, s: (0, j)), blk(0)],
        out_specs=[ANY, pl.BlockSpec((3, CONV_COLS), lambda j, s: (0, j)),
                   pl.BlockSpec((1, CONV_COLS), lambda j, s: (0, j))],
        out_shape=[jax.ShapeDtypeStruct((t, 2 * D_FF), BF16), jax.ShapeDtypeStruct((3, D_FF), F32),
                   jax.ShapeDtypeStruct((1, D_FF), F32)],
        scratch_shapes=[pltpu.VMEM((2, 2, l, CONV_COLS), BF16), pltpu.SemaphoreType.DMA((2, 2))],
        compiler_params=_params(("arbitrary", "arbitrary")),
    )(gu, gu, w, b, dact)


def _loss_head(h, target):
    t, d = h.shape
    tm = _pick(t, (256, 128, 8))

    def body(h_ref, t_ref, dh_ref, dhb_ref, loss_ref):
        @pl.when(pl.program_id(0) == 0)
        def _():
            loss_ref[...] = jnp.zeros(loss_ref.shape, F32)

        e = h_ref[...] - t_ref[...]
        dh = e * (1.0 / d)
        dh_ref[...] = dh
        dhb_ref[...] = dh.astype(BF16)
        loss_ref[...] += (0.5 / d) * jnp.sum(jnp.sum(e * e, axis=1, keepdims=True), axis=0, keepdims=True)

    blk = pl.BlockSpec((tm, d), lambda i: (i, 0))
    return pl.pallas_call(
        body, name="loss_head", grid=(t // tm,), in_specs=[blk, blk],
        out_specs=[blk, blk, pl.BlockSpec((1, 1), lambda i: (0, 0))],
        out_shape=[jax.ShapeDtypeStruct((t, d), F32), jax.ShapeDtypeStruct((t, d), BF16),
                   jax.ShapeDtypeStruct((1, 1), F32)],
        compiler_params=_params(("arbitrary",)),
    )(h, target)


def _s5_discretise(a_re, a_im, log_dt, b_re, b_im):
    dt = jnp.exp(log_dt)[:, None]
    mag = jnp.exp(a_re * dt)
    lb_r = mag * jnp.cos(a_im * dt)
    lb_i = mag * jnp.sin(a_im * dt)
    den = a_re * a_re + a_im * a_im
    nr = lb_r - 1.0
    coef_r = (nr * a_re + lb_i * a_im) / den
    coef_i = (lb_i * a_re - nr * a_im) / den
    bb_r = coef_r[:, :, None] * b_re - coef_i[:, :, None] * b_im
    bb_i = coef_r[:, :, None] * b_im + coef_i[:, :, None] * b_re
    return lb_r, lb_i, bb_r, bb_i


S5_CHUNKS = 4
S5_PER = S5_GROUPS // S5_CHUNKS


def _blockdiag_in(bb):
    eye = jnp.eye(S5_PER, dtype=bb.dtype)
    return jnp.einsum("jgpc,gh->jgchp", bb.reshape(S5_CHUNKS, S5_PER, S5_STATE, S5_GROUP_CH), eye).reshape(
        S5_CHUNKS, S5_PER * S5_GROUP_CH, S5_PER * S5_STATE)


def _blockdiag_in_grad(d):
    eye = jnp.eye(S5_PER, dtype=d.dtype)
    return jnp.einsum("jgchp,gh->jgpc", d.reshape(S5_CHUNKS, S5_PER, S5_GROUP_CH, S5_PER, S5_STATE), eye).reshape(
        S5_GROUPS, S5_STATE, S5_GROUP_CH)


def _blockdiag_out(c):
    eye = jnp.eye(S5_PER, dtype=c.dtype)
    return jnp.einsum("jgcp,gh->jgphc", c.reshape(S5_CHUNKS, S5_PER, S5_GROUP_CH, S5_STATE), eye).reshape(
        S5_CHUNKS, S5_PER * S5_STATE, S5_PER * S5_GROUP_CH)


def _blockdiag_out_grad(d):
    eye = jnp.eye(S5_PER, dtype=d.dtype)
    return jnp.einsum("jgphc,gh->jgcp", d.reshape(S5_CHUNKS, S5_PER, S5_STATE, S5_PER, S5_GROUP_CH), eye).reshape(
        S5_GROUPS, S5_GROUP_CH, S5_STATE)


def _local_step(x3, mem3, target3, p, wb, late_weights=None, early_grads=None):
    seqs, l, d = x3.shape
    t = seqs * l
    x = x3.reshape(t, d)
    mem = mem3.reshape(seqs * N_MEM, d)
    target = target3.reshape(t, d)
    full = lambda a: (a, a.shape[1], 0, 0)

    s5_in = (p["s5_a_re"], p["s5_a_im"], p["s5_log_dt"], p["s5_b_re"], p["s5_b_im"])
    (lb_r, lb_i, bb_r, bb_i), s5_pull = jax.vjp(_s5_discretise, *s5_in)
    ar, ai = lb_r.reshape(1, S5_CH), lb_i.reshape(1, S5_CH)
    bbr_d, bbi_d = _blockdiag_in(bb_r).astype(BF16), _blockdiag_in(bb_i).astype(BF16)
    cr_d, ci_d = _blockdiag_out(p["s5_c_re"]).astype(BF16), (-_blockdiag_out(p["s5_c_im"])).astype(BF16)
    d_row = p["s5_d"].reshape(1, S5_WIDTH)

    w_in = wb["w_in"]
    w_qkv = w_in[:, :3 * FOX_WIDTH]
    w_uf = jnp.concatenate(
        [w_in[:, 3 * FOX_WIDTH + N_FOX_HEADS:], w_in[:, 3 * FOX_WIDTH:3 * FOX_WIDTH + N_FOX_HEADS],
         jnp.zeros((d, UF_COLS - S5_WIDTH - N_FOX_HEADS), w_in.dtype)], axis=1)

    hn1 = _rowwise(_rms, [full(x)], [p["norm_mix"]], [(d, d, 0, BF16)], "norm_mix_fwd")
    qkv = _mm(hn1, w_qkv, "nn", "in_qkv")
    uf = _mm(hn1, w_uf, "nn", "in_uf")

    bh = seqs * N_FOX_HEADS
    q_pair = (qkv, 128, 0, 1)
    k_pair = (qkv, 128, N_PAIRS, 1)
    gq2, gk2 = jnp.tile(p["fox_q_norm"], (1, 2)), jnp.tile(p["fox_k_norm"], (1, 2))
    pair_out = [(FOX_WIDTH, 128, 1, BF16)]
    qn = _rowwise(_rms_pair, [q_pair], [gq2], pair_out, "fox_qnorm_fwd", heads=N_PAIRS)
    kn = _rowwise(_rms_pair, [k_pair], [gk2], pair_out, "fox_knorm_fwd", heads=N_PAIRS)

    f_rows = uf[:, S5_WIDTH:S5_WIDTH + N_FOX_HEADS].reshape(seqs, l, N_FOX_HEADS).transpose(0, 2, 1).reshape(bh, l)
    f_bias = jnp.tile(p["fox_f_bias"].reshape(N_FOX_HEADS, 1), (seqs, 1))
    c_wide = jnp.broadcast_to(_forget_fwd(f_rows, f_bias)[:, :, None], (bh, l, 128))
    fox, lse = _fox_fwd(qn, kn, qkv, c_wide, seqs)

    xr, xi, ys = _s5_fwd(uf, bbr_d, bbi_d, cr_d, ci_d, ar, ai, seqs)
    u_blk = (uf, S5_WIDTH, 0, 0)
    yg = _rowwise(_s5_act, [full(ys), u_blk], [d_row], [(S5_WIDTH, S5_WIDTH, 0, F32)], "s5_act_fwd")
    z = _mm(yg, wb["s5_w_glu"], "nn", "s5_glu")
    y2n = _rowwise(_s5_gate, [full(yg), full(z)], [p["s5_b_glu"], p["out_norm_s5"]],
                   [(S5_WIDTH, S5_WIDTH, 0, BF16)], "s5_gate_fwd")
    foxn = _rowwise(_rms, [full(fox)], [p["out_norm_fox"]], [(FOX_WIDTH, FOX_WIDTH, 0, BF16)], "fox_outnorm_fwd")
    mixed = jnp.concatenate([foxn, y2n], axis=1)
    h1 = _mm(mixed, wb["w_out"], "nn", "mix_out", res=x)
    if late_weights is not None:
        wb = dict(wb, **late_weights(h1))

    hn2 = _rowwise(_rms, [full(h1)], [p["norm_cross"]], [(d, d, 0, BF16)], "norm_cross_fwd")
    mn = _rowwise(_rms, [full(mem)], [p["norm_mem"]], [(d, d, 0, BF16)], "norm_mem_fwd")
    xq_raw = _mm(hn2, wb["w_xq"], "nn", "x_q")
    kv = _mm(mn, wb["w_xkv"], "nn", "x_kv")
    xh = lambda a: (a, X_HEAD_DIM, 0, 1)
    xqn = _rowwise(_rms, [xh(xq_raw)], [p["xq_norm"]], [(d, X_HEAD_DIM, 1, BF16)], "x_qnorm_fwd", heads=N_X_HEADS)
    xkn = _rowwise(_rms, [xh(kv)], [p["xk_norm"]], [(d, X_HEAD_DIM, 1, BF16)], "x_knorm_fwd", heads=N_X_HEADS)
    xo = _xatt_fwd(xqn, xkn, kv, seqs)
    h2 = _mm(xo, wb["w_xo"], "nn", "x_out", res=h1)

    hn3 = _rowwise(_rms, [full(h2)], [p["norm_ffn"]], [(d, d, 0, BF16)], "norm_ffn_fwd")
    gu = _mm(hn3, wb["w_ffn_up"], "nn", "ffn_up", out_dtype=BF16)
    act = _convgate_fwd(gu, p["ffn_conv_w"], p["ffn_conv_b"], seqs)
    h3 = _mm(act, wb["w_ffn_down"], "nn", "ffn_down", res=h2)
    dh3, dh3_b, loss = _loss_head(h3, target)

    g = {}
    dact = _mm(dh3_b, wb["w_ffn_down"], "nt", "ffn_down_dx", out_dtype=BF16)
    late_dt = BF16 if early_grads is not None else F32
    g["w_ffn_down"] = _mm(act, dh3_b, "tn", "ffn_down_dw", out_dtype=late_dt)
    dgu, g["ffn_conv_w"], g["ffn_conv_b"] = _convgate_bwd(gu, p["ffn_conv_w"], p["ffn_conv_b"], dact, seqs)
    dhn3 = _mm(dgu, wb["w_ffn_up"], "nt", "ffn_up_dx")
    g["w_ffn_up"] = _mm(hn3, dgu, "tn", "ffn_up_dw", out_dtype=late_dt)
    (dh2,), (g["norm_ffn"],) = _rowwise_vjp(_rms, [full(h2)], [p["norm_ffn"]], [full(dhn3)], "norm_ffn_bwd",
                                            adds=[full(dh3)])

    dxo = _mm(dh2, wb["w_xo"], "nt", "x_out_dx")
    g["w_xo"] = _mm(xo, dh2, "tn", "x_out_dw", out_dtype=late_dt)
    dxqn, dxkn, dxv = _xatt_bwd(xqn, xkn, kv, dxo, seqs)
    (dxq_raw,), (g["xq_norm"],) = _rowwise_vjp(_rms, [xh(xq_raw)], [p["xq_norm"]], [xh(dxqn)], "x_qnorm_bwd",
                                               heads=N_X_HEADS, row_dtypes=[BF16])
    (dxk_raw,), (g["xk_norm"],) = _rowwise_vjp(_rms, [xh(kv)], [p["xk_norm"]], [xh(dxkn)], "x_knorm_bwd",
                                               heads=N_X_HEADS, row_dtypes=[BF16])
    dkv = jnp.concatenate([dxk_raw, dxv.astype(BF16)], axis=1)
    dhn2 = _mm(dxq_raw, wb["w_xq"], "nt", "x_q_dx")
    g["w_xq"] = _mm(hn2, dxq_raw, "tn", "x_q_dw", out_dtype=late_dt)
    dmn = _mm(dkv, wb["w_xkv"], "nt", "x_kv_dx")
    g["w_xkv"] = _mm(mn, dkv, "tn", "x_kv_dw", out_dtype=late_dt)
    norm_cross = p["norm_cross"]
    if early_grads is not None:
        norm_cross = norm_cross + early_grads({n: g[n] for n in LATE_WEIGHTS})
    (dh1,), (g["norm_cross"],) = _rowwise_vjp(_rms, [full(h1)], [norm_cross], [full(dhn2)], "norm_cross_bwd",
                                              adds=[full(dh2)])
    _, (g["norm_mem"],) = _rowwise_vjp(_rms, [full(mem)], [p["norm_mem"]], [full(dmn)], "norm_mem_bwd",
                                       row_dtypes=[BF16])

    dmixed = _mm(dh1, wb["w_out"], "nt", "mix_out_dx")
    g["w_out"] = _mm(mixed, dh1, "tn", "mix_out_dw")
    (dfox,), (g["out_norm_fox"],) = _rowwise_vjp(_rms, [full(fox)], [p["out_norm_fox"]],
                                                 [(dmixed, FOX_WIDTH, 0, 0)], "fox_outnorm_bwd")
    (dyg_a, dz), (g["s5_b_glu"], g["out_norm_s5"]) = _rowwise_vjp(
        _s5_gate, [full(yg), full(z)], [p["s5_b_glu"], p["out_norm_s5"]], [(dmixed, S5_WIDTH, 1, 0)], "s5_gate_bwd",
        row_dtypes=[F32, BF16])
    dyg = _mm(dz, wb["s5_w_glu"], "nt", "s5_glu_dx", res=dyg_a)
    g["s5_w_glu"] = _mm(yg, dz, "tn", "s5_glu_dw")
    (dys, du_a), (dd_row,) = _rowwise_vjp(_s5_act, [full(ys), u_blk], [d_row], [full(dyg)], "s5_act_bwd",
                                          row_dtypes=[BF16, F32])
    g["s5_d"] = dd_row
    du_b, dbbr_d, dbbi_d, dcr_d, dci_d, dar, dai = _s5_bwd(dys, uf, xr, xi, bbr_d, bbi_d, cr_d, ci_d, ar, ai, seqs)
    dbbr_d, dbbi_d, dcr_d, dci_d = (jnp.sum(a, axis=0) for a in (dbbr_d, dbbi_d, dcr_d, dci_d))
    d_lb_r = jnp.sum(dar, axis=0).reshape(S5_GROUPS, S5_STATE)
    d_lb_i = jnp.sum(dai, axis=0).reshape(S5_GROUPS, S5_STATE)
    g["s5_a_re"], g["s5_a_im"], g["s5_log_dt"], g["s5_b_re"], g["s5_b_im"] = s5_pull(
        (d_lb_r, d_lb_i, _blockdiag_in_grad(dbbr_d), _blockdiag_in_grad(dbbi_d)))
    g["s5_c_re"] = _blockdiag_out_grad(dcr_d)
    g["s5_c_im"] = -_blockdiag_out_grad(dci_d)

    dqn, dkn, dv, dc, dcq = _fox_bwd(qn, kn, qkv, c_wide, fox, dfox, lse, seqs)
    pair = lambda a: (a, 128, 0, 1)
    (dq_raw,), (dgq2,) = _rowwise_vjp(_rms_pair, [q_pair], [gq2], [pair(dqn)], "fox_qnorm_bwd", heads=N_PAIRS,
                                      row_dtypes=[BF16])
    (dk_raw,), (dgk2,) = _rowwise_vjp(_rms_pair, [k_pair], [gk2], [pair(dkn)], "fox_knorm_bwd", heads=N_PAIRS,
                                      row_dtypes=[BF16])
    g["fox_q_norm"] = dgq2[:, :HEAD_DIM] + dgq2[:, HEAD_DIM:]
    g["fox_k_norm"] = dgk2[:, :HEAD_DIM] + dgk2[:, HEAD_DIM:]
    df_rows, dfb = _forget_bwd(f_rows, f_bias, (dc + dcq).reshape(bh, l))
    g["fox_f_bias"] = jnp.sum(dfb.reshape(seqs, N_FOX_HEADS), axis=0)
    df = df_rows.reshape(seqs, N_FOX_HEADS, l).transpose(0, 2, 1).reshape(t, N_FOX_HEADS)
    dqkv = jnp.concatenate([dq_raw, dk_raw, dv.astype(BF16)], axis=1)
    duf = jnp.concatenate([du_a + du_b, df, jnp.zeros((t, UF_COLS - S5_WIDTH - N_FOX_HEADS), F32)],
                          axis=1).astype(BF16)
    dhn1 = _mm(duf, w_uf, "nt", "in_uf_dx", res=_mm(dqkv, w_qkv, "nt", "in_qkv_dx"))
    dw_qkv = _mm(hn1, dqkv, "tn", "in_qkv_dw")
    dw_uf = _mm(hn1, duf, "tn", "in_uf_dw")
    g["w_in"] = jnp.concatenate([dw_qkv, dw_uf[:, S5_WIDTH:S5_WIDTH + N_FOX_HEADS], dw_uf[:, :S5_WIDTH]], axis=1)
    (dx,), (g["norm_mix"],) = _rowwise_vjp(_rms, [full(x)], [p["norm_mix"]], [full(dhn1)], "norm_mix_bwd",
                                           adds=[full(dh1)])
    return loss, dx.reshape(seqs, l, d), g


def _place():
    return lax.axis_index("x"), lax.axis_index("y"), lax.axis_index("c")


def _other_chips(x, y):
    return [(1 - x, y), (x, 1 - y), (1 - x, 1 - y)]


ANY = pl.BlockSpec(memory_space=pl.ANY)


def _gather_weights(shards, col_kind, taps):
    n = len(shards)

    def body(*refs):
        ins, tap_in, outs, tap_out = refs[:n], refs[n], refs[n + 1:2 * n + 1], refs[2 * n + 1]
        ici_send, ici_recv, d2d_send, d2d_recv, own_send, own_recv = refs[2 * n + 2:]
        x, y, c = _place()
        mine = 2 * x + y
        chips = _other_chips(x, y)
        sibling = (x, y, 1 - c)

        def piece(a, s, h):
            r, cs = ins[a].shape
            hr = r // 2
            if col_kind[a]:
                return outs[a].at[pl.ds(pl.multiple_of(h * hr, 16), hr), pl.ds(pl.multiple_of(s * cs, 128), cs)]
            return outs[a].at[pl.ds(pl.multiple_of(s * r + h * hr, 16), hr), :]

        def slab(a, s):
            r, cs = ins[a].shape
            if col_kind[a]:
                return outs[a].at[:, pl.ds(pl.multiple_of(s * cs, 128), cs)]
            return outs[a].at[pl.ds(pl.multiple_of(s * r, 16), r), :]

        def own_half(a, h):
            hr = ins[a].shape[0] // 2
            return ins[a].at[pl.ds(pl.multiple_of(h * hr, 16), hr), :]

        sends = []
        for a in range(n):
            cp = pltpu.make_async_remote_copy(
                src_ref=ins[a], dst_ref=slab(a, mine), send_sem=own_send.at[a], recv_sem=own_recv.at[a],
                device_id=sibling, device_id_type=MESH)
            cp.start()
            sends.append(cp)
        cp = pltpu.make_async_remote_copy(
            src_ref=tap_in, dst_ref=tap_out.at[mine], send_sem=own_send.at[n], recv_sem=own_recv.at[n],
            device_id=sibling, device_id_type=MESH)
        cp.start()
        sends.append(cp)
        for a in range(n):
            for j, (px, py) in enumerate(chips):
                cp = pltpu.make_async_remote_copy(
                    src_ref=own_half(a, c), dst_ref=piece(a, mine, c), send_sem=ici_send.at[3 * a + j],
                    recv_sem=ici_recv.at[3 * a + j], device_id=(px, py, c), device_id_type=MESH)
                cp.start()
                sends.append(cp)
        for j, (px, py) in enumerate(chips):
            cp = pltpu.make_async_remote_copy(
                src_ref=tap_in, dst_ref=tap_out.at[mine], send_sem=ici_send.at[3 * n + j],
                recv_sem=ici_recv.at[3 * n + j], device_id=(px, py, c), device_id_type=MESH)
            cp.start()
            sends.append(cp)
        for a in range(n):
            for j, (px, py) in enumerate(chips):
                got = piece(a, 2 * px + py, c)
                pltpu.make_async_remote_copy(
                    src_ref=got, dst_ref=got, send_sem=ici_send.at[3 * a + j], recv_sem=ici_recv.at[3 * a + j],
                    device_id=(px, py, c), device_id_type=MESH).wait_recv()
                fwd = pltpu.make_async_remote_copy(
                    src_ref=got, dst_ref=got, send_sem=d2d_send.at[3 * a + j], recv_sem=d2d_recv.at[3 * a + j],
                    device_id=(x, y, 1 - c), device_id_type=MESH)
                fwd.start()
                sends.append(fwd)
        for a in range(n):
            for j, (px, py) in enumerate(chips):
                other = piece(a, 2 * px + py, 1 - c)
                pltpu.make_async_remote_copy(
                    src_ref=other, dst_ref=other, send_sem=d2d_send.at[3 * a + j], recv_sem=d2d_recv.at[3 * a + j],
                    device_id=(x, y, 1 - c), device_id_type=MESH).wait_recv()
        for j, (px, py) in enumerate(chips):
            pltpu.make_async_remote_copy(
                src_ref=tap_in, dst_ref=tap_out.at[2 * px + py], send_sem=ici_send.at[3 * n + j],
                recv_sem=ici_recv.at[3 * n + j], device_id=(px, py, c), device_id_type=MESH).wait_recv()
        for a in range(n):
            pltpu.make_async_remote_copy(
                src_ref=ins[a], dst_ref=slab(a, mine), send_sem=own_send.at[a], recv_sem=own_recv.at[a],
                device_id=sibling, device_id_type=MESH).wait_recv()
        pltpu.make_async_remote_copy(
            src_ref=tap_in, dst_ref=tap_out.at[mine], send_sem=own_send.at[n], recv_sem=own_recv.at[n],
            device_id=sibling, device_id_type=MESH).wait_recv()
        for cp in sends:
            cp.wait_send()

    def full_shape(a):
        r, cs = shards[a].shape
        return (r, 4 * cs) if col_kind[a] else (4 * r, cs)

    res = pl.pallas_call(
        body, name="gather_weights", in_specs=[ANY] * (n + 1), out_specs=[ANY] * (n + 1),
        out_shape=[jax.ShapeDtypeStruct(full_shape(a), shards[a].dtype) for a in range(n)]
        + [jax.ShapeDtypeStruct((4,) + taps.shape, taps.dtype)],
        scratch_shapes=[pltpu.SemaphoreType.DMA((3 * n + 3,)), pltpu.SemaphoreType.DMA((3 * n + 3,)),
                        pltpu.SemaphoreType.DMA((3 * n,)), pltpu.SemaphoreType.DMA((3 * n,)),
                        pltpu.SemaphoreType.DMA((n + 1,)), pltpu.SemaphoreType.DMA((n + 1,))],
        compiler_params=pltpu.CompilerParams(has_side_effects=True),
    )(*shards, taps)
    return res[:n], res[n]


HBM = pl.BlockSpec(memory_space=pltpu.HBM)
SEM = pl.BlockSpec(memory_space=pltpu.SEMAPHORE)
DATAFLOW = pltpu.SideEffectType.DATAFLOW_SIDE_EFFECTING


def _in_hbm(a):
    return pltpu.with_memory_space_constraint(a, pltpu.HBM)


def _split_start(name, srcs, lands, n_copies, plan):
    n = len(srcs)

    def body(*refs):
        src_refs, land_refs = refs[:n], refs[n:2 * n]
        send_sems, recv_sems = refs[2 * n], refs[2 * n + 1]
        for i, (src, dst, dev) in enumerate(plan(src_refs, land_refs)):
            pltpu.make_async_remote_copy(src_ref=src, dst_ref=dst, send_sem=send_sems.at[i], recv_sem=recv_sems.at[i],
                                         device_id=dev, device_id_type=MESH).start()
        refs[-1][...] = jnp.zeros((8, 128), F32)

    res = pl.pallas_call(
        body, name=name, in_specs=[HBM] * (2 * n),
        out_specs=[SEM, SEM] + [HBM] * (2 * n) + [pl.BlockSpec(memory_space=pltpu.VMEM)],
        out_shape=[pltpu.SemaphoreType.DMA((n_copies,)), pltpu.SemaphoreType.DMA((n_copies,))]
        + [pltpu.HBM(a.shape, a.dtype) for a in list(srcs) + list(lands)] + [jax.ShapeDtypeStruct((8, 128), F32)],
        input_output_aliases={i: 2 + i for i in range(2 * n)},
        compiler_params=pltpu.CompilerParams(has_side_effects=DATAFLOW),
    )(*[_in_hbm(a) for a in list(srcs) + list(lands)])
    return res[0], res[1], list(res[2:2 + n]), list(res[2 + n:2 + 2 * n]), res[-1]


def _split_wait(name, send_sems, recv_sems, srcs, lands, after, plan):
    n = len(srcs)

    def body(*refs):
        src_refs, land_refs = refs[:n], refs[n:2 * n]
        send_ref, recv_ref = refs[2 * n], refs[2 * n + 1]
        for i, (src, dst, dev) in enumerate(plan(src_refs, land_refs)):
            cp = pltpu.make_async_remote_copy(src_ref=src, dst_ref=dst, send_sem=send_ref.at[i], recv_sem=recv_ref.at[i],
                                              device_id=dev, device_id_type=MESH)
            cp.wait_send()
            cp.wait_recv()

    res = pl.pallas_call(
        body, name=name, in_specs=[HBM] * (2 * n) + [SEM, SEM, ANY], out_specs=[HBM] * (2 * n),
        out_shape=[pltpu.HBM(a.shape, a.dtype) for a in list(srcs) + list(lands)],
        input_output_aliases={i: i for i in range(2 * n)},
        compiler_params=pltpu.CompilerParams(has_side_effects=DATAFLOW),
    )(*srcs, *lands, send_sems, recv_sems, after)
    return list(res[:n]), list(res[n:])


def _late_gather_plan(col_kind):
    def plan(src_refs, land_refs):
        x, y, c = _place()
        mine = 2 * x + y
        copies = []
        for a, (src, land) in enumerate(zip(src_refs, land_refs)):
            r, cs = src.shape
            if col_kind[a]:
                dst = land.at[:, pl.ds(pl.multiple_of(mine * cs, 128), cs)]
            else:
                dst = land.at[pl.ds(pl.multiple_of(mine * r, 16), r), :]
            copies.append((src, dst, (x, y, 1 - c)))
            copies += [(src, dst, (px, py, c)) for (px, py) in _other_chips(x, y)]
        return copies
    return plan


def _late_reduce_plan(col_kind):
    def plan(src_refs, land_refs):
        x, y, c = _place()
        copies = []
        for a, (src, land) in enumerate(zip(src_refs, land_refs)):
            for j, (px, py) in enumerate(_other_chips(x, y)):
                if col_kind[a]:
                    cs = land.shape[2]
                    piece = src.at[:, pl.ds(pl.multiple_of((2 * px + py) * cs, 128), cs)]
                else:
                    piece = src.at[2 * px + py]
                copies.append((piece, land.at[j], (px, py, c)))
        return copies
    return plan


def _pair_exchange_halves(name, grads, col_kind):
    n = len(grads)

    def body(*refs):
        ins, outs = refs[:n], refs[n:2 * n]
        send_sems, recv_sems = refs[2 * n:]
        x, y, c = _place()
        copies = []
        for a in range(n):
            if col_kind[a]:
                hr = ins[a].shape[0] // 2
                src = ins[a].at[pl.ds(pl.multiple_of((1 - c) * hr, 8), hr), :]
            else:
                hr = ins[a].shape[1] // 2
                src = ins[a].at[:, pl.ds(pl.multiple_of((1 - c) * hr, 8), hr), :]
            cp = pltpu.make_async_remote_copy(
                src_ref=src, dst_ref=outs[a], send_sem=send_sems.at[a], recv_sem=recv_sems.at[a],
                device_id=(x, y, 1 - c), device_id_type=MESH)
            cp.start()
            copies.append(cp)
        for cp in copies:
            cp.wait()

    def half_shape(a):
        s = grads[a].shape
        return (s[0] // 2, s[1]) if col_kind[a] else (4, s[1] // 2, s[2])

    return pl.pallas_call(
        body, name=name, in_specs=[ANY] * n, out_specs=[ANY] * n,
        out_shape=[jax.ShapeDtypeStruct(half_shape(a), grads[a].dtype) for a in range(n)],
        scratch_shapes=[pltpu.SemaphoreType.DMA((n,)), pltpu.SemaphoreType.DMA((n,))],
        compiler_params=pltpu.CompilerParams(has_side_effects=True),
    )(*grads)


def _pair_swap_halves(name, halves):
    n = len(halves)

    def body(*refs):
        ins, outs = refs[:n], refs[n:2 * n]
        send_sems, recv_sems = refs[2 * n:]
        x, y, c = _place()
        copies = []
        for a in range(n):
            cp = pltpu.make_async_remote_copy(
                src_ref=ins[a], dst_ref=outs[a], send_sem=send_sems.at[a], recv_sem=recv_sems.at[a],
                device_id=(x, y, 1 - c), device_id_type=MESH)
            cp.start()
            copies.append(cp)
        for cp in copies:
            cp.wait()

    return pl.pallas_call(
        body, name=name, in_specs=[ANY] * n, out_specs=[ANY] * n,
        out_shape=[jax.ShapeDtypeStruct(s.shape, s.dtype) for s in halves],
        scratch_shapes=[pltpu.SemaphoreType.DMA((n,)), pltpu.SemaphoreType.DMA((n,))],
        compiler_params=pltpu.CompilerParams(has_side_effects=True),
    )(*halves)


def _chip_sum(name, chip_sel, own, col, others):
    _, r, c = others.shape
    tr = _pick(r, (256, 128, 64, 32, 16))
    if col:
        own_spec = pl.BlockSpec((tr, c), lambda i, s: (i, s[0]))
    else:
        own_spec = pl.BlockSpec((None, tr, c), lambda i, s: (s[0], i, 0))
    specs = [own_spec] + [pl.BlockSpec((None, tr, c), lambda i, s, k=k: (k, i, 0)) for k in range(3)]

    def body(s_ref, own_ref, r0, r1, r2, o_ref):
        o_ref[...] = ((own_ref[...].astype(F32) + r0[...].astype(F32)) + r1[...].astype(F32)) + r2[...].astype(F32)

    return pl.pallas_call(
        body, name=name,
        grid_spec=pltpu.PrefetchScalarGridSpec(
            num_scalar_prefetch=1, grid=(r // tr,), in_specs=specs,
            out_specs=pl.BlockSpec((tr, c), lambda i, s: (i, 0))),
        out_shape=jax.ShapeDtypeStruct((r, c), F32),
        compiler_params=_params(("parallel",)),
    )(chip_sel, own, others, others, others)


def _pair_sum(name, c_sel, grad, recv, col):
    if col:
        r, c4 = grad.shape
        hr, c = r // 2, c4 // 4
    else:
        _, r, c = grad.shape
        hr = r // 2
    tr = _pick(hr, (256, 128, 64, 32, 16))
    nb = hr // tr

    def body(s_ref, g_ref, r_ref, o_ref):
        o_ref[...] = (g_ref[...] + r_ref[...]).astype(o_ref.dtype)

    if col:
        in_specs = [pl.BlockSpec((tr, c), lambda k, i, s: (s[0] * nb + i, k)), pl.BlockSpec((tr, c), lambda k, i, s: (i, k))]
        out_spec = pl.BlockSpec((tr, c), lambda k, i, s: (i, k))
    else:
        in_specs = [pl.BlockSpec((None, tr, c), lambda k, i, s: (k, s[0] * nb + i, 0)),
                    pl.BlockSpec((None, tr, c), lambda k, i, s: (k, i, 0))]
        out_spec = pl.BlockSpec((None, tr, c), lambda k, i, s: (k, i, 0))
    return pl.pallas_call(
        body, name=name,
        grid_spec=pltpu.PrefetchScalarGridSpec(num_scalar_prefetch=1, grid=(4, nb), in_specs=in_specs,
                                               out_specs=out_spec),
        out_shape=jax.ShapeDtypeStruct(recv.shape, BF16),
        compiler_params=_params(("parallel", "parallel")),
    )(c_sel, grad, recv)


def _allreduce_small(vals):
    sizes = [int(math.prod(v.shape)) for v in vals]
    padded = [-(-s // 128) * 128 for s in sizes]
    total = -(-sum(padded) // 1024) * 1024
    flat = [jnp.pad(v.reshape(-1), (0, p - s)) for v, s, p in zip(vals, sizes, padded)]
    flat.append(jnp.zeros((total - sum(padded),), F32))
    packed = jnp.concatenate(flat).reshape(total // 128, 128)

    def body(in_ref, out_ref, r0, r1, r2, send_sems, recv_sems):
        x, y, c = _place()
        out_ref[...] = in_ref[...]
        for k, (peer, land) in enumerate(zip([(x, y, 1 - c), (1 - x, y, c), (x, 1 - y, c)], (r0, r1, r2))):
            cp = pltpu.make_async_remote_copy(
                src_ref=out_ref, dst_ref=land, send_sem=send_sems.at[k], recv_sem=recv_sems.at[k],
                device_id=peer, device_id_type=MESH)
            cp.start()
            cp.wait()
            out_ref[...] = out_ref[...] + land[...]

    vm = pl.BlockSpec(memory_space=pltpu.VMEM)
    summed = pl.pallas_call(
        body, name="allreduce_small", in_specs=[vm], out_specs=vm,
        out_shape=jax.ShapeDtypeStruct(packed.shape, F32),
        scratch_shapes=[pltpu.VMEM(packed.shape, F32)] * 3
        + [pltpu.SemaphoreType.DMA((3,)), pltpu.SemaphoreType.DMA((3,))],
        compiler_params=pltpu.CompilerParams(has_side_effects=True, vmem_limit_bytes=VMEM_LIMIT_BYTES),
    )(packed).reshape(-1)
    outs, off = [], 0
    for v, s, p in zip(vals, sizes, padded):
        outs.append(summed[off:off + s].reshape(v.shape))
        off += p
    return outs


def _adamw_math(w, g, m, v):
    m2 = ADAM_B1 * m + (1.0 - ADAM_B1) * g
    v2 = ADAM_B2 * v + (1.0 - ADAM_B2) * (g * g)
    m_hat = m2 / (1.0 - ADAM_B1 ** ADAM_STEP)
    v_hat = v2 / (1.0 - ADAM_B2 ** ADAM_STEP)
    delta = -ADAM_LR * (m_hat / (jnp.sqrt(v_hat) + ADAM_EPS) + ADAM_WD * w)
    return delta, m2, v2


def _adamw_big(name, c_sel, w, g_mine, g_sibling, m, v, halves):
    _, r, c = w.shape
    hr = r // 2
    tr = _pick(hr, (256, 128, 64, 32, 16, 8))
    nb = hr // tr

    def body(s_ref, w_ref, ga_ref, gb_ref, m_ref, v_ref, go_ref, d_ref, mo_ref, vo_ref):
        if halves:
            gv = jnp.where(pl.program_id(0) == s_ref[0], ga_ref[...], gb_ref[...])
        else:
            gv = ga_ref[...] + gb_ref[...]
        d, m2, v2 = _adamw_math(w_ref[...], gv, m_ref[...], v_ref[...])
        go_ref[...] = gv
        d_ref[...] = d
        mo_ref[...] = m2
        vo_ref[...] = v2

    blk = pl.BlockSpec((None, tr, c), lambda h, i, s: (0, h * nb + i, 0))
    half = pl.BlockSpec((tr, c), (lambda h, i, s: (i, 0)) if halves else (lambda h, i, s: (h * nb + i, 0)))
    return pl.pallas_call(
        body, name=name,
        grid_spec=pltpu.PrefetchScalarGridSpec(
            num_scalar_prefetch=1, grid=(2, nb), in_specs=[blk, half, half, blk, blk], out_specs=[blk] * 4),
        out_shape=[jax.ShapeDtypeStruct((1, r, c), F32)] * 4, compiler_params=_params(("parallel", "parallel")),
    )(c_sel, w, g_mine, g_sibling, m, v)


def _adamw_small(ws, gs, ms, vs):
    n = len(ws)

    def body(*refs):
        w_r, g_r, m_r, v_r = refs[:n], refs[n:2 * n], refs[2 * n:3 * n], refs[3 * n:4 * n]
        o = refs[4 * n:]
        for a in range(n):
            gv = g_r[a][...]
            d, m2, v2 = _adamw_math(w_r[a][...], gv, m_r[a][...], v_r[a][...])
            o[a][...] = gv
            o[n + a][...] = d
            o[2 * n + a][...] = m2
            o[3 * n + a][...] = v2

    res = pl.pallas_call(
        body, name="adamw_small", out_shape=[jax.ShapeDtypeStruct(w.shape, F32) for _ in range(4) for w in ws],
        compiler_params=_params(),
    )(*ws, *gs, *ms, *vs)
    return res[:n], res[n:2 * n], res[2 * n:3 * n], res[3 * n:]


def _full_from_gathered(name, gathered):
    if name == "w_in":
        rows = gathered.shape[0] // 4
        return gathered.reshape(4, rows, gathered.shape[1]).transpose(1, 0, 2).reshape(rows, 4 * gathered.shape[1])
    return gathered


def _reduce_layout(name, full):
    if name in COL_KIND:
        return full
    if name == "w_in":
        rows, cols = full.shape
        return full.reshape(rows, 4, cols // 4).transpose(1, 0, 2)
    return full.reshape(4, full.shape[0] // 4, full.shape[1])


def kernel(x, mem, norm_mix, w_in, fox_q_norm, fox_k_norm, fox_f_bias, s5_a_re, s5_a_im, s5_log_dt, s5_b_re, s5_b_im, s5_c_re, s5_c_im, s5_d, s5_w_glu, s5_b_glu, out_norm_fox, out_norm_s5, w_out, norm_cross, norm_mem, w_xq, w_xkv, xq_norm, xk_norm, w_xo, norm_ffn, w_ffn_up, ffn_conv_w, ffn_conv_b, w_ffn_down, loss_target, m_norm_mix, m_w_in, m_fox_q_norm, m_fox_k_norm, m_fox_f_bias, m_s5_a_re, m_s5_a_im, m_s5_log_dt, m_s5_b_re, m_s5_b_im, m_s5_c_re, m_s5_c_im, m_s5_d, m_s5_w_glu, m_s5_b_glu, m_out_norm_fox, m_out_norm_s5, m_w_out, m_norm_cross, m_norm_mem, m_w_xq, m_w_xkv, m_xq_norm, m_xk_norm, m_w_xo, m_norm_ffn, m_w_ffn_up, m_ffn_conv_w, m_ffn_conv_b, m_w_ffn_down, v_norm_mix, v_w_in, v_fox_q_norm, v_fox_k_norm, v_fox_f_bias, v_s5_a_re, v_s5_a_im, v_s5_log_dt, v_s5_b_re, v_s5_b_im, v_s5_c_re, v_s5_c_im, v_s5_d, v_s5_w_glu, v_s5_b_glu, v_out_norm_fox, v_out_norm_s5, v_w_out, v_norm_cross, v_norm_mem, v_w_xq, v_w_xkv, v_xq_norm, v_xk_norm, v_w_xo, v_norm_ffn, v_w_ffn_up, v_ffn_conv_w, v_ffn_conv_b, v_w_ffn_down):
    given = dict(locals())
    w = {n: given[n] for n in WEIGHTS}
    m = {n: given["m_" + n] for n in WEIGHTS}
    v = {n: given["v_" + n] for n in WEIGHTS}
    xi, yi, ci = _place()
    chip = (2 * xi + yi).astype(jnp.int32)

    c_sel = ci.astype(jnp.int32).reshape(1)
    chip_sel = chip.reshape(1)
    early_kind = [n in COL_KIND for n in EARLY_WEIGHTS]
    late_kind = [n in COL_KIND for n in LATE_WEIGHTS]

    gathered, taps = _gather_weights([w[n][0].astype(BF16) for n in EARLY_WEIGHTS], early_kind, w["ffn_conv_w"][0])
    wb = {n: _full_from_gathered(n, gathered[k]) for k, n in enumerate(EARLY_WEIGHTS)}
    conv_w = taps.transpose(1, 0, 2).reshape(3, D_FF)
    late_shards = [w[n][0].astype(BF16) for n in LATE_WEIGHTS]
    late_full = [lax.empty((s.shape[0], 4 * s.shape[1]) if ck else (4 * s.shape[0], s.shape[1]), BF16)
                 for s, ck in zip(late_shards, late_kind)]
    gather_plan = _late_gather_plan(late_kind)
    g_send, g_recv, g_srcs, g_lands, g_started = _split_start(
        "gather_late_start", late_shards, late_full, 4 * len(LATE_WEIGHTS), gather_plan)

    def late_weights(after):
        _, full = _split_wait("gather_late_wait", g_send, g_recv, g_srcs, g_lands, after, gather_plan)
        return dict(zip(LATE_WEIGHTS, full))

    reduce_plan = _late_reduce_plan(late_kind)
    late_reduce = {}

    def early_grads(late_g):
        grads = [_reduce_layout(n, late_g[n]) for n in LATE_WEIGHTS]
        lands = [lax.empty((3, s.shape[0], s.shape[1] // 4) if ck else (3,) + s.shape[1:], BF16)
                 for s, ck in zip(grads, late_kind)]
        late_reduce["sems"] = _split_start("reduce_late_start", grads, lands, 3 * len(LATE_WEIGHTS), reduce_plan)
        return late_reduce["sems"][4][0:1, 0:1]

    p = {n: w[n][0] for n in SMALL}
    p["ffn_conv_w"] = conv_w
    for n in ("norm_mix", "fox_q_norm", "fox_k_norm", "fox_f_bias", "s5_b_glu", "out_norm_fox", "out_norm_s5",
              "norm_cross", "norm_mem", "xq_norm", "xk_norm", "norm_ffn", "ffn_conv_b"):
        p[n] = p[n].reshape(1, -1)
    p["norm_mix"] = p["norm_mix"] + g_started[0:1, 0:1]
    loss, grad_x, g = _local_step(x, mem, loss_target, p, wb, late_weights, early_grads)

    grads = [_reduce_layout(n, g[n]) for n in EARLY_WEIGHTS]
    from_sibling = _pair_exchange_halves("reduce_pair_exchange_early", grads, early_kind)
    pair_sums = [_pair_sum("reduce_pair_sum_" + n, c_sel, gr, rv, ck)
                 for n, gr, rv, ck in zip(EARLY_WEIGHTS, grads, from_sibling, early_kind)]
    early_lands = [lax.empty((3, s.shape[0], s.shape[1] // 4) if ck else (3,) + s.shape[1:], BF16)
                   for s, ck in zip(pair_sums, early_kind)]
    early_plan = _late_reduce_plan(early_kind)
    e_send, e_recv, e_srcs, e_lands, e_started = _split_start(
        "reduce_early_start", pair_sums, early_lands, 3 * len(EARLY_WEIGHTS), early_plan)

    out_g, out_d, out_m, out_v = {}, {}, {}, {}

    def finish(names, kinds, sums, from_chips, tag, halves):
        mine = [_chip_sum("reduce_chip_sum_" + n, chip_sel, ps, ck, fc)
                for n, ps, fc, ck in zip(names, sums, from_chips, kinds)]
        theirs = _pair_swap_halves("reduce_pair_swap_" + tag, mine)
        for n, a, b in zip(names, mine, theirs):
            out_g[n], out_d[n], out_m[n], out_v[n] = _adamw_big("adamw_" + n, c_sel, w[n], a, b, m[n], v[n], halves)

    r_send, r_recv, r_srcs, r_lands, _ = late_reduce["sems"]
    late_sums, late_from_chips = _split_wait("reduce_late_wait", r_send, r_recv, r_srcs, r_lands, e_started,
                                             reduce_plan)
    finish(LATE_WEIGHTS, late_kind, late_sums, late_from_chips, "late", False)

    small_names = list(SMALL) + ["ffn_conv_w"]
    small_vals = [g[n].reshape(w[n].shape if n != "ffn_conv_w" else (1, 3, D_FF)) for n in small_names]
    last = LATE_WEIGHTS[-1]
    loss, out_v[last] = lax.optimization_barrier((loss, out_v[last]))
    reduced = _allreduce_small(small_vals + [loss])
    loss_all = reduced[-1].reshape(())
    conv_w_grad = lax.dynamic_slice_in_dim(reduced[-2], chip * (D_FF // 4), D_FF // 4, axis=2)
    sg, sd, sm, sv = _adamw_small(
        [w[n] for n in small_names], list(reduced[:len(SMALL)]) + [conv_w_grad],
        [m[n] for n in small_names], [v[n] for n in small_names])
    out_g.update(zip(small_names, sg))
    out_d.update(zip(small_names, sd))
    out_m.update(zip(small_names, sm))
    out_v.update(zip(small_names, sv))

    early_sums, early_from_chips = _split_wait("reduce_early_wait", e_send, e_recv, e_srcs, e_lands, reduced[0],
                                               early_plan)
    finish(EARLY_WEIGHTS, early_kind, early_sums, early_from_chips, "early", True)

    return (loss_all, grad_x, *[out_g[n] for n in WEIGHTS], *[out_d[n] for n in WEIGHTS],
            *[out_m[n] for n in WEIGHTS], *[out_v[n] for n in WEIGHTS])
```

```python
import functools
import math

import jax
import jax.numpy as jnp
from jax import lax
from jax.experimental import pallas as pl
from jax.experimental.pallas import tpu as pltpu

F32 = jnp.float32
BF16 = jnp.bfloat16

D_MODEL = 1024
FOX_WIDTH = 512
HEAD_DIM = 64
N_FOX_HEADS = 8
S5_WIDTH = 512
S5_GROUP_CH = 16
S5_GROUPS = 32
S5_STATE = 64
S5_CH = S5_GROUPS * S5_STATE
N_X_HEADS = 4
X_HEAD_DIM = 256
N_MEM = 256
D_FF = 2816
UF_COLS = 640
EPS = 1e-6
ADAM_LR = 0.001
ADAM_B1 = 0.9
ADAM_B2 = 0.999
ADAM_EPS = 1e-08
ADAM_WD = 0.01
ADAM_STEP = 10

VMEM_LIMIT_BYTES = 56 * 1024 * 1024
MM_BLOCK_BYTES = 6 * 1024 * 1024
MM_VMEM_BYTES = 40 * 1024 * 1024
MESH = pl.DeviceIdType.MESH

EARLY_WEIGHTS = ("w_in", "s5_w_glu", "w_out")
LATE_WEIGHTS = ("w_xq", "w_xkv", "w_xo", "w_ffn_up", "w_ffn_down")
BIG = EARLY_WEIGHTS + LATE_WEIGHTS
COL_KIND = ("w_xkv", "w_ffn_up")
SMALL = ("norm_mix", "fox_q_norm", "fox_k_norm", "fox_f_bias", "s5_a_re", "s5_a_im", "s5_log_dt",
         "s5_b_re", "s5_b_im", "s5_c_re", "s5_c_im", "s5_d", "s5_b_glu", "out_norm_fox", "out_norm_s5",
         "norm_cross", "norm_mem", "xq_norm", "xk_norm", "norm_ffn", "ffn_conv_b")
WEIGHTS = ("norm_mix", "w_in", "fox_q_norm", "fox_k_norm", "fox_f_bias", "s5_a_re", "s5_a_im", "s5_log_dt",
           "s5_b_re", "s5_b_im", "s5_c_re", "s5_c_im", "s5_d", "s5_w_glu", "s5_b_glu", "out_norm_fox",
           "out_norm_s5", "w_out", "norm_cross", "norm_mem", "w_xq", "w_xkv", "xq_norm", "xk_norm", "w_xo",
           "norm_ffn", "w_ffn_up", "ffn_conv_w", "ffn_conv_b", "w_ffn_down")


def _params(sem=None):
    return pltpu.CompilerParams(dimension_semantics=sem, vmem_limit_bytes=VMEM_LIMIT_BYTES)


def _pick(n, cands):
    for c in cands:
        if n % c == 0:
            return c
    return n


_DIMS = {"nn": (((1,), (0,)), ((), ())), "nt": (((1,), (1,)), ((), ())), "tn": (((0,), (0,)), ((), ()))}


def _mm(a, b, mode, name, out_dtype=F32, res=None):
    if mode == "nn":
        (m, k), (k2, n) = a.shape, b.shape
    elif mode == "nt":
        (m, k), (n, k2) = a.shape, b.shape
    else:
        (k, m), (k2, n) = a.shape, b.shape
    assert k == k2, (name, a.shape, b.shape)

    has_res = res is not None
    a_size, b_size = a.dtype.itemsize, b.dtype.itemsize
    o_size = jnp.dtype(out_dtype).itemsize + (res.dtype.itemsize if has_res else 0)

    def tiles(dim):
        return [c for c in (1024, 512, 256, 128) if dim % c == 0] or [dim]

    best = None
    for tm in tiles(m):
        for tn in tiles(n):
            a_blk, b_blk = tm * k * a_size, tn * k * b_size
            if max(a_blk, b_blk) > MM_BLOCK_BYTES or 2 * (a_blk + b_blk + tm * tn * o_size) > MM_VMEM_BYTES:
                continue
            for rows_outer in (True, False):
                moved = (m * k * a_size + (m // tm) * n * k * b_size) if rows_outer else \
                        (n * k * b_size + (n // tn) * m * k * a_size)
                key = (moved, -(tm * tn))
                if best is None or key < best[0]:
                    best = (key, tm, tn, rows_outer)
    assert best is not None, (name, a.shape, b.shape)
    _, tm, tn, rows_outer = best
    ij = (lambda g0, g1: (g0, g1)) if rows_outer else (lambda g0, g1: (g1, g0))
    if mode == "tn":
        a_spec = pl.BlockSpec((k, tm), lambda g0, g1: (0, ij(g0, g1)[0]))
    else:
        a_spec = pl.BlockSpec((tm, k), lambda g0, g1: (ij(g0, g1)[0], 0))
    if mode == "nt":
        b_spec = pl.BlockSpec((tn, k), lambda g0, g1: (ij(g0, g1)[1], 0))
    else:
        b_spec = pl.BlockSpec((k, tn), lambda g0, g1: (0, ij(g0, g1)[1]))
    o_spec = pl.BlockSpec((tm, tn), lambda g0, g1: ij(g0, g1))
    grid = (m // tm, n // tn) if rows_outer else (n // tn, m // tm)
    dims = _DIMS[mode]

    def body(*refs):
        a_ref, b_ref = refs[0], refs[1]
        o_ref = refs[-1]
        acc = lax.dot_general(a_ref[...].astype(BF16), b_ref[...].astype(BF16), dims, preferred_element_type=F32)
        if has_res:
            acc = acc + refs[2][...].astype(F32)
        o_ref[...] = acc.astype(o_ref.dtype)

    return pl.pallas_call(
        body, name=name, grid=grid,
        in_specs=[a_spec, b_spec] + ([o_spec] if has_res else []),
        out_specs=o_spec, out_shape=jax.ShapeDtypeStruct((m, n), out_dtype),
        compiler_params=_params(("parallel", "parallel")),
    )(*((a, b, res) if has_res else (a, b)))


def _row_spec(tm, bc, off, step):
    return pl.BlockSpec((tm, bc), lambda i, h: (i, off + step * h))


ROW_TILE_ELEMS = 512 * 1024


def _row_tile(t, rows):
    widest = max(bc for (_, bc, _, _) in rows)
    return _pick(t, (min(t, ROW_TILE_ELEMS // widest), 512, 256, 128, 64, 8))


def _rowwise(fn, rows, pars, outs, name, heads=1):
    t = rows[0][0].shape[0]
    tm = _row_tile(t, rows)
    nr, npar = len(rows), len(pars)

    def body(*refs):
        vals = [r[...].astype(F32) for r in refs[:nr + npar]]
        res = fn(*vals)
        if not isinstance(res, (tuple, list)):
            res = (res,)
        for o_ref, v in zip(refs[nr + npar:], res):
            o_ref[...] = v.astype(o_ref.dtype)

    in_specs = [_row_spec(tm, bc, off, st) for (_, bc, off, st) in rows]
    in_specs += [pl.BlockSpec(p.shape, lambda i, h: (0, 0)) for p in pars]
    out_specs = [_row_spec(tm, bc, 0, st) for (_, bc, st, _) in outs]
    out_shape = [jax.ShapeDtypeStruct((t, c), dt) for (c, _, _, dt) in outs]
    res = pl.pallas_call(
        body, name=name, grid=(t // tm, heads), in_specs=in_specs, out_specs=out_specs, out_shape=out_shape,
        compiler_params=_params(("parallel", "parallel")),
    )(*[r[0] for r in rows], *pars)
    return res[0] if len(res) == 1 else res


def _rowwise_vjp(fn, rows, pars, cts, name, heads=1, adds=None, row_dtypes=None):
    t = rows[0][0].shape[0]
    tm = _row_tile(t, rows)
    nr, npar, nct = len(rows), len(pars), len(cts)
    adds = adds or [None] * nr
    add_list = [a for a in adds if a is not None]
    row_dtypes = row_dtypes or [F32] * nr

    def body(*refs):
        i, h = pl.program_id(0), pl.program_id(1)
        p = 0
        row_v = [r[...].astype(F32) for r in refs[p:p + nr]]; p += nr
        par_v = [r[...].astype(F32) for r in refs[p:p + npar]]; p += npar
        ct_v = [r[...].astype(F32) for r in refs[p:p + nct]]; p += nct
        add_refs = refs[p:p + len(add_list)]; p += len(add_list)
        drow_refs = refs[p:p + nr]; p += nr
        dpar_refs = refs[p:p + npar]

        def wrapped(*a):
            r = fn(*a)
            return tuple(r) if isinstance(r, (tuple, list)) else (r,)

        _, pull = jax.vjp(wrapped, *row_v, *par_v)
        grads = pull(tuple(ct_v))
        ai = 0
        for k in range(nr):
            g = grads[k]
            if adds[k] is not None:
                g = g + add_refs[ai][...].astype(F32)
                ai += 1
            drow_refs[k][...] = g.astype(drow_refs[k].dtype)

        @pl.when((i == 0) & (h == 0))
        def _():
            for r in dpar_refs:
                r[...] = jnp.zeros(r.shape, r.dtype)

        for k in range(npar):
            dpar_refs[k][...] += grads[nr + k]

    in_specs = [_row_spec(tm, bc, off, st) for (_, bc, off, st) in rows]
    in_specs += [pl.BlockSpec(q.shape, lambda i, h: (0, 0)) for q in pars]
    in_specs += [_row_spec(tm, bc, off, st) for (_, bc, off, st) in cts]
    in_specs += [_row_spec(tm, bc, off, st) for (_, bc, off, st) in add_list]
    out_specs = [_row_spec(tm, bc, 0, st) for (_, bc, _, st) in rows]
    out_specs += [pl.BlockSpec(q.shape, lambda i, h: (0, 0)) for q in pars]
    out_shape = [jax.ShapeDtypeStruct((t, bc * (heads if st else 1)), dt) for (_, bc, _, st), dt in zip(rows, row_dtypes)]
    out_shape += [jax.ShapeDtypeStruct(q.shape, F32) for q in pars]
    res = pl.pallas_call(
        body, name=name, grid=(t // tm, heads), in_specs=in_specs, out_specs=out_specs, out_shape=out_shape,
        compiler_params=_params(("arbitrary", "arbitrary")),
    )(*[r[0] for r in rows], *pars, *[c[0] for c in cts], *[a[0] for a in add_list])
    return list(res[:nr]), list(res[nr:])


def _rms(x, g):
    return x * lax.rsqrt(jnp.mean(x * x, axis=-1, keepdims=True) + EPS) * g


def _rms_pair(x, g):
    left = lax.broadcasted_iota(jnp.int32, x.shape, 1) < HEAD_DIM
    x2 = x * x
    ms_a = jnp.sum(jnp.where(left, x2, 0.0), axis=-1, keepdims=True) * (1.0 / HEAD_DIM)
    ms_b = jnp.sum(jnp.where(left, 0.0, x2), axis=-1, keepdims=True) * (1.0 / HEAD_DIM)
    return x * lax.rsqrt(jnp.where(left, ms_a, ms_b) + EPS) * g


def _gelu(x):
    return 0.5 * x * (1.0 + jnp.tanh(math.sqrt(2.0 / math.pi) * (x + 0.044715 * (x * x * x))))


def _s5_act(ys, u, d):
    return _gelu(ys + d * u)


def _s5_gate(yg, z, b, g):
    return _rms(yg * jax.nn.sigmoid(z + b), g)


def _lane_cumsum(x, reverse):
    n = x.shape[-1]
    lane = lax.broadcasted_iota(jnp.int32, x.shape, 1)
    k = 1
    while k < n:
        if reverse:
            x = x + jnp.where(lane < n - k, pltpu.roll(x, n - k, 1), 0.0)
        else:
            x = x + jnp.where(lane >= k, pltpu.roll(x, k, 1), 0.0)
        k *= 2
    return x


def _log_sigmoid(z):
    return jnp.minimum(z, 0.0) - jnp.log(1.0 + jnp.exp(-jnp.abs(z)))


def _forget_fwd(f, bias):
    def body(f_ref, b_ref, c_ref):
        c_ref[...] = _lane_cumsum(_log_sigmoid(f_ref[...] + b_ref[...]), False)

    return pl.pallas_call(body, name="forget_fwd", out_shape=jax.ShapeDtypeStruct(f.shape, F32),
                          compiler_params=_params())(f, bias)


def _forget_bwd(f, bias, dc):
    def body(f_ref, b_ref, dc_ref, df_ref, db_ref):
        dlog = _lane_cumsum(dc_ref[...], True)
        df = dlog * jax.nn.sigmoid(-(f_ref[...] + b_ref[...]))
        df_ref[...] = df
        db_ref[...] = jnp.sum(df, axis=1, keepdims=True)

    return pl.pallas_call(body, name="forget_bwd",
                          out_shape=(jax.ShapeDtypeStruct(f.shape, F32), jax.ShapeDtypeStruct(bias.shape, F32)),
                          compiler_params=_params())(f, bias, dc)


FOX_BLOCK = 256
FOX_KEYS = 256
_NT = _DIMS["nt"]
_TN = _DIMS["tn"]


N_PAIRS = N_FOX_HEADS // 2
V_BLOCK0 = 2 * N_PAIRS


def _left_lanes(shape):
    return lax.broadcasted_iota(jnp.int32, shape, 1) < HEAD_DIM


def _top_rows(shape):
    return lax.broadcasted_iota(jnp.int32, shape, 0) < HEAD_DIM


def _wide(c_tile, n):
    return c_tile if n == 128 else jnp.concatenate([c_tile] * (n // 128), axis=1)


def _fox_fwd(qn, kn, qkv, c_wide, seqs):
    t = qn.shape[0]
    l = t // seqs
    tb = min(FOX_BLOCK, l)
    tk = min(FOX_KEYS, tb)
    ratio = tb // tk
    nb = l // tb
    scale = HEAD_DIM ** -0.5

    def body(q_ref, k_ref, v_ref, ca_ref, cb_ref, o_ref, lse_ref, vt_ref):
        i = pl.program_id(2)
        top = _top_rows((128, tb))

        @pl.when(i == 0)
        def _():
            vt_ref[...] = v_ref[...].T.astype(BF16)

        qt = (q_ref[...].astype(F32) * scale).T.astype(BF16)
        zero = jnp.zeros_like(qt)
        qts = (jnp.where(top, qt, zero), jnp.where(top, zero, qt))
        top_k = _top_rows((128, tk))
        zero_k = jnp.zeros((128, tk), BF16)
        key_pos = lax.broadcasted_iota(jnp.int32, (tk, tb), 0)
        query_pos = lax.broadcasted_iota(jnp.int32, (tk, tb), 1)
        c_refs = (ca_ref, cb_ref)

        def scores(j):
            off = pl.multiple_of(j * tk, tk)
            k2 = k_ref[pl.ds(off, tk), :]
            return tuple(jnp.dot(k2, qts[h], preferred_element_type=F32) - _wide(c_refs[h][pl.ds(off, tk), :], tb)
                         for h in (0, 1))

        def values_times(ps, j):
            vt = vt_ref[:, pl.ds(pl.multiple_of(j * tk, tk), tk)]
            return (jnp.dot(jnp.where(top_k, vt, zero_k), ps[0], preferred_element_type=F32)
                    + jnp.dot(jnp.where(top_k, zero_k, vt), ps[1], preferred_element_type=F32))

        def softmax_step(sts, stats, first_key):
            ps, new, alphas = [], [], []
            for st, (m, s_sum) in zip(sts, stats):
                if first_key is not None:
                    st = jnp.where(key_pos + first_key <= query_pos, st, -jnp.inf)
                m_new = jnp.maximum(m, jnp.max(st, axis=0, keepdims=True))
                alpha = jnp.exp(m - m_new)
                p = jnp.exp(st - m_new)
                new.append((m_new, alpha * s_sum + jnp.sum(p, axis=0, keepdims=True)))
                alphas.append(alpha)
                ps.append(p.astype(BF16))
            return tuple(ps), tuple(new), jnp.where(top, alphas[0], alphas[1])

        def step(j, carry):
            sts, ps_prev, stats, acc = carry
            sts_next = scores(j + 1)
            acc = acc + values_times(ps_prev, jnp.maximum(j - 1, 0))
            ps, stats, alpha = softmax_step(sts, stats, None)
            return sts_next, ps, stats, alpha * acc

        stat = (jnp.full((1, tb), -jnp.inf, F32), jnp.zeros((1, tb), F32))
        no_p = jnp.zeros((tk, tb), BF16)
        below = i * ratio
        sts, ps_prev, stats, acc = lax.fori_loop(
            0, below, step, (scores(0), (no_p, no_p), (stat, stat), jnp.zeros((128, tb), F32)))
        for r in range(ratio):
            sts_next = scores(below + r + 1) if r + 1 < ratio else None
            acc = acc + values_times(ps_prev, jnp.maximum(below + r - 1, 0))
            ps_prev, stats, alpha = softmax_step(sts, stats, r * tk)
            acc = alpha * acc
            sts = sts_next
        acc = acc + values_times(ps_prev, below + ratio - 1)
        (ma, sa), (mb, sb) = stats
        o_ref[...] = (acc / jnp.where(top, sa, sb)).T
        lse_ref[0:1, :] = ma + jnp.log(sa)
        lse_ref[1:2, :] = mb + jnp.log(sb)

    qblk = pl.BlockSpec((tb, 128), lambda b, hp, i: (b * nb + i, hp))
    return pl.pallas_call(
        body, name="fox_fwd", grid=(seqs, N_PAIRS, nb),
        in_specs=[qblk, pl.BlockSpec((l, 128), lambda b, hp, i: (b, hp)),
                  pl.BlockSpec((l, 128), lambda b, hp, i: (b, V_BLOCK0 + hp)),
                  pl.BlockSpec((None, l, 128), lambda b, hp, i: (b * N_FOX_HEADS + 2 * hp, 0, 0)),
                  pl.BlockSpec((None, l, 128), lambda b, hp, i: (b * N_FOX_HEADS + 2 * hp + 1, 0, 0))],
        out_specs=[qblk, pl.BlockSpec((None, 2, tb), lambda b, hp, i: (b * N_PAIRS + hp, 0, i))],
        out_shape=[jax.ShapeDtypeStruct((t, FOX_WIDTH), F32), jax.ShapeDtypeStruct((seqs * N_PAIRS, 2, l), F32)],
        scratch_shapes=[pltpu.VMEM((128, l), BF16)],
        compiler_params=_params(("parallel", "parallel", "arbitrary")),
    )(qn, kn, qkv, c_wide, c_wide)


def _fox_bwd(qn, kn, qkv, c_wide, o, do, lse, seqs):
    t = qn.shape[0]
    l = t // seqs
    tb = min(FOX_BLOCK, l)
    nb = l // tb
    scale = HEAD_DIM ** -0.5
    one_at = (HEAD_DIM, 0)

    def body(q_ref, k_ref, v_ref, ca_ref, cb_ref, o_ref, do_ref, lse_ref, dq_ref, dk_ref, dv_ref, dc_ref, dcq_ref,
             qt_ref, kt_ref, dot_ref, delta_ref, dqa_ref, dqb_ref):
        top_l = _top_rows((128, l))
        top = _top_rows((128, tb))
        left = _left_lanes((tb, 128))
        row_id = lax.broadcasted_iota(jnp.int32, (128, tb), 0)
        lane_id = lax.broadcasted_iota(jnp.int32, (tb, 128), 1)
        zero_t = jnp.zeros((128, tb), BF16)
        zero_l = jnp.zeros((tb, 128), BF16)
        rows = lambda a: (jnp.where(top, a, zero_t), jnp.where(top, zero_t, a))
        lanes = lambda a: (jnp.where(left, a, zero_l), jnp.where(left, zero_l, a))
        with_one_row = lambda pair: tuple(jnp.where(row_id == one_at[h], 1.0, pair[h]).astype(BF16) for h in (0, 1))
        with_one_lane = lambda pair: tuple(jnp.where(lane_id == one_at[h], 1.0, pair[h]).astype(BF16) for h in (0, 1))
        causal = lax.broadcasted_iota(jnp.int32, (tb, tb), 0) <= lax.broadcasted_iota(jnp.int32, (tb, tb), 1)
        c_refs = (ca_ref, cb_ref)
        dq_refs = (dqa_ref, dqb_ref)

        qt_ref[...] = (q_ref[...].astype(F32) * scale).T.astype(BF16)
        kt_ref[...] = k_ref[...].astype(F32).T.astype(BF16)
        do_t = do_ref[...].T
        dot_ref[...] = do_t.astype(BF16)
        prod_t = do_t * o_ref[...].T
        delta_ref[0:1, :] = jnp.sum(jnp.where(top_l, prod_t, 0.0), axis=0, keepdims=True)
        delta_ref[1:2, :] = jnp.sum(jnp.where(top_l, 0.0, prod_t), axis=0, keepdims=True)
        dqa_ref[...] = jnp.zeros(dqa_ref.shape, F32)
        dqb_ref[...] = jnp.zeros(dqb_ref.shape, F32)

        def kv_block(j, _):
            koff = pl.multiple_of(j * tb, tb)
            k2 = k_ref[pl.ds(koff, tb), :]
            v2 = v_ref[pl.ds(koff, tb), :].astype(BF16)
            kts = with_one_row(rows(kt_ref[:, pl.ds(koff, tb)]))
            cw = tuple(_wide(c_refs[h][pl.ds(koff, tb), :], tb) for h in (0, 1))

            def q_block(i, carry, masked):
                dks, dv = list(carry[:2]), carry[2]
                qoff = pl.multiple_of(i * tb, tb)
                qs = lanes((q_ref[pl.ds(qoff, tb), :].astype(F32) * scale).astype(BF16))
                qs_one = with_one_lane(qs)
                dos = lanes(do_ref[pl.ds(qoff, tb), :].astype(BF16))
                qts = rows(qt_ref[:, pl.ds(qoff, tb)])
                dots = rows(dot_ref[:, pl.ds(qoff, tb)])
                for h in (0, 1):
                    st = jnp.dot(k2, qts[h], preferred_element_type=F32) - cw[h]
                    p = jnp.exp(st - lse_ref[h:h + 1, pl.ds(qoff, tb)])
                    if masked:
                        p = jnp.where(causal, p, 0.0)
                    dp = jnp.dot(v2, dots[h], preferred_element_type=F32)
                    dsb = (p * (dp - delta_ref[h:h + 1, pl.ds(qoff, tb)])).astype(BF16)
                    dv = dv + jnp.dot(p.astype(BF16), dos[h], preferred_element_type=F32)
                    dks[h] = dks[h] + jnp.dot(dsb, qs_one[h], preferred_element_type=F32)
                    dq_refs[h][:, pl.ds(qoff, tb)] += jnp.dot(kts[h], dsb, preferred_element_type=F32)
                return dks[0], dks[1], dv

            z = jnp.zeros((tb, 128), F32)
            carry = q_block(j, (z, z, z), True)
            rest = nb - 1 - j
            carry = lax.fori_loop(
                0, rest // 2, lambda n, c: q_block(j + 2 + 2 * n, q_block(j + 1 + 2 * n, c, False), False), carry)
            dka, dkb, dv = lax.cond(rest % 2 == 1, lambda c: q_block(nb - 1, c, False), lambda c: c, carry)
            dk_ref[pl.ds(koff, tb), :] = jnp.where(left, dka, dkb)
            dv_ref[pl.ds(koff, tb), :] = dv
            dc_ref[0:1, pl.ds(koff, tb)] = -dka.T[one_at[0]:one_at[0] + 1, :]
            dc_ref[1:2, pl.ds(koff, tb)] = -dkb.T[one_at[1]:one_at[1] + 1, :]
            return 0

        lax.fori_loop(0, nb, kv_block, 0)
        dq_ref[...] = (jnp.where(top_l, dqa_ref[...], dqb_ref[...]) * scale).T
        dcq_ref[0:1, :] = dqa_ref[one_at[0]:one_at[0] + 1, :]
        dcq_ref[1:2, :] = dqb_ref[one_at[1]:one_at[1] + 1, :]

    blk = pl.BlockSpec((l, 128), lambda b, hp: (b, hp))
    cspec = lambda k: pl.BlockSpec((None, l, 128), lambda b, hp: (b * N_FOX_HEADS + 2 * hp + k, 0, 0))
    rows2 = pl.BlockSpec((None, 2, l), lambda b, hp: (b * N_PAIRS + hp, 0, 0))
    wide = jax.ShapeDtypeStruct((t, FOX_WIDTH), F32)
    pair_rows = jax.ShapeDtypeStruct((seqs * N_PAIRS, 2, l), F32)
    return pl.pallas_call(
        body, name="fox_bwd", grid=(seqs, N_PAIRS),
        in_specs=[blk, blk, pl.BlockSpec((l, 128), lambda b, hp: (b, V_BLOCK0 + hp)), cspec(0), cspec(1), blk, blk, rows2],
        out_specs=[blk, blk, blk, rows2, rows2],
        out_shape=[wide, wide, wide, pair_rows, pair_rows],
        scratch_shapes=[pltpu.VMEM((128, l), BF16), pltpu.VMEM((128, l), BF16), pltpu.VMEM((128, l), BF16),
                        pltpu.VMEM((2, l), F32), pltpu.VMEM((128, l), F32), pltpu.VMEM((128, l), F32)],
        compiler_params=_params(("parallel", "parallel")),
    )(qn, kn, qkv, c_wide, c_wide, o, do, lse)


SCAN_ROWS = 256
SCAN_COLS = 1024


S5_IN = 128
S5_ST = 512
SCAN_CHUNKS = SCAN_COLS // S5_ST
SCAN_SEGS = 8
LANES = 128


def _cmul(ar, ai, br, bi):
    return ar * br - ai * bi, ar * bi + ai * br


def _powers_into(pw_r, pw_i, a_r, a_i, seg):
    pw_r[0:1, :] = a_r
    pw_i[0:1, :] = a_i
    for k in range(1, seg):
        pr, pi = _cmul(pw_r[k - 1:k, :], pw_i[k - 1:k, :], a_r, a_i)
        pw_r[k:k + 1, :] = pr
        pw_i[k:k + 1, :] = pi


def _interleave(dst, src, seg):
    for h in range(src.shape[0]):
        for j in range(seg):
            dst[h, j * SCAN_SEGS:(j + 1) * SCAN_SEGS, :] = src[h, pl.ds(j, SCAN_SEGS, stride=seg), :]


def _deinterleave(dst, src, seg):
    for h in range(src.shape[0]):
        for j in range(seg):
            dst[h, pl.ds(j, SCAN_SEGS, stride=seg), :] = src[h, j * SCAN_SEGS:(j + 1) * SCAN_SEGS, :]


def _interleaved(ref, tmp_a, tmp_b, seg):
    n = ref.shape[1] // LANES
    for h in range(n):
        tmp_a[h] = ref[:, h * LANES:(h + 1) * LANES].astype(F32)
    _interleave(tmp_b, tmp_a, seg)
    return jnp.concatenate([tmp_b[h] for h in range(n)], axis=1)


def _store_deinterleaved(ref, val, tmp_a, tmp_b, seg):
    n = ref.shape[1] // LANES
    for h in range(n):
        tmp_a[h] = val[:, h * LANES:(h + 1) * LANES]
    _deinterleave(tmp_b, tmp_a, seg)
    for h in range(n):
        ref[:, h * LANES:(h + 1) * LANES] = tmp_b[h]


def _segment_scan(b_r, b_i, x_r, x_i, pw_r, pw_i, car_r, car_i, seg, sign, reverse, visit=None):
    nc = b_r.shape[0]
    sub = lax.broadcasted_iota(jnp.int32, (SCAN_SEGS, LANES), 0)
    lanes = lambda c: slice(c * LANES, (c + 1) * LANES)
    rows = lambda j: pl.ds(pl.multiple_of(((seg - 1 - j) if reverse else j) * SCAN_SEGS, SCAN_SEGS), SCAN_SEGS)
    a1 = [(pw_r[0:1, lanes(c)], sign * pw_i[0:1, lanes(c)]) for c in range(nc)]

    def local(j, xs):
        out = []
        for c in range(nc):
            xr, xi = xs[2 * c], xs[2 * c + 1]
            nr = a1[c][0] * xr - a1[c][1] * xi + b_r[c, rows(j), :]
            ni = a1[c][0] * xi + a1[c][1] * xr + b_i[c, rows(j), :]
            x_r[c, rows(j), :] = nr
            x_i[c, rows(j), :] = ni
            out += [nr, ni]
        return tuple(out)

    zero = jnp.zeros((SCAN_SEGS, LANES), F32)
    ends = lax.fori_loop(0, seg, local, (zero,) * (2 * nc))

    if reverse:
        first = sub == SCAN_SEGS - 1
        neighbour = lambda v: pltpu.roll(v, SCAN_SEGS - 1, 0)
        shift = lambda v, d: jnp.where(sub < SCAN_SEGS - d, pltpu.roll(v, SCAN_SEGS - d, 0), 0.0)
    else:
        first = sub == 0
        neighbour = lambda v: pltpu.roll(v, 1, 0)
        shift = lambda v, d: jnp.where(sub >= d, pltpu.roll(v, d, 0), 0.0)
    last = 0 if reverse else SCAN_SEGS - 1
    entries = []
    for c in range(nc):
        er, ei = ends[2 * c], ends[2 * c + 1]
        pr, pi = pw_r[seg - 1:seg, lanes(c)], sign * pw_i[seg - 1:seg, lanes(c)]
        yr = jnp.where(first, car_r[:, lanes(c)], neighbour(er))
        yi = jnp.where(first, car_i[:, lanes(c)], neighbour(ei))
        qr, qi = pr, pi
        for d in (1, 2, 4):
            mr, mi = _cmul(qr, qi, shift(yr, d), shift(yi, d))
            yr, yi = yr + mr, yi + mi
            qr, qi = _cmul(qr, qi, qr, qi)
        lr, li = _cmul(pr, pi, yr, yi)
        car_r[:, lanes(c)] = (er + lr)[last:last + 1, :]
        car_i[:, lanes(c)] = (ei + li)[last:last + 1, :]
        entries += [yr, yi]

    def correct(j, prev):
        out = []
        row_r, row_i = pw_r[pl.ds(j, 1), :], sign * pw_i[pl.ds(j, 1), :]
        for c in range(nc):
            mr, mi = _cmul(row_r[:, lanes(c)], row_i[:, lanes(c)], entries[2 * c], entries[2 * c + 1])
            nr = x_r[c, rows(j), :] + mr
            ni = x_i[c, rows(j), :] + mi
            x_r[c, rows(j), :] = nr
            x_i[c, rows(j), :] = ni
            if visit is not None:
                visit(c, rows(j), prev[2 * c], prev[2 * c + 1])
            out += [nr, ni]
        return tuple(out)

    lax.fori_loop(0, seg, correct, tuple(entries))


def _s5_fwd(uf, bbr, bbi, cr, ci, ar, ai, seqs):
    t = uf.shape[0]
    l = t // seqs
    tl = min(SCAN_ROWS, l)
    nl = l // tl
    seg = tl // SCAN_SEGS
    cb, nq = SCAN_COLS, SCAN_CHUNKS
    nc = cb // LANES
    per = S5_ST // LANES

    def body(u_ref, bbr_ref, bbi_ref, cr_ref, ci_ref, ar_ref, ai_ref, xr_ref, xi_ref, ys_ref,
             car_r, car_i, pw_r, pw_i, b_r, b_i, x_r, x_i, tmp_a, tmp_b):
        @pl.when(pl.program_id(2) == 0)
        def _():
            car_r[...] = jnp.zeros(car_r.shape, F32)
            car_i[...] = jnp.zeros(car_i.shape, F32)
            _powers_into(pw_r, pw_i, ar_ref[...], ai_ref[...], seg)

        u = _interleaved(u_ref, tmp_a, tmp_b, seg).astype(BF16)
        for q in range(nq):
            uq = u[:, q * S5_IN:(q + 1) * S5_IN]
            br = jnp.dot(uq, bbr_ref[q], preferred_element_type=F32)
            bi = jnp.dot(uq, bbi_ref[q], preferred_element_type=F32)
            for s in range(per):
                b_r[q * per + s] = br[:, s * LANES:(s + 1) * LANES]
                b_i[q * per + s] = bi[:, s * LANES:(s + 1) * LANES]
        _segment_scan(b_r, b_i, x_r, x_i, pw_r, pw_i, car_r, car_i, seg, 1.0, False)
        for c in range(nc):
            xr_ref[:, c * LANES:(c + 1) * LANES] = x_r[c]
            xi_ref[:, c * LANES:(c + 1) * LANES] = x_i[c]
        ys = []
        for q in range(nq):
            xq_r = xr_ref[:, q * S5_ST:(q + 1) * S5_ST].astype(BF16)
            xq_i = xi_ref[:, q * S5_ST:(q + 1) * S5_ST].astype(BF16)
            ys.append(jnp.dot(xq_r, cr_ref[q], preferred_element_type=F32)
                      + jnp.dot(xq_i, ci_ref[q], preferred_element_type=F32))
        _store_deinterleaved(ys_ref, jnp.concatenate(ys, axis=1), tmp_a, tmp_b, seg)

    rows = lambda w: pl.BlockSpec((tl, w), lambda s, j, r: (s * nl + r, j))
    chunk = lambda a: pl.BlockSpec((nq,) + a.shape[1:], lambda s, j, r: (j, 0, 0))
    par = pl.BlockSpec((1, cb), lambda s, j, r: (0, j))
    return pl.pallas_call(
        body, name="s5_fwd", grid=(seqs, S5_CH // cb, nl),
        in_specs=[rows(nq * S5_IN), chunk(bbr), chunk(bbi), chunk(cr), chunk(ci), par, par],
        out_specs=[rows(cb), rows(cb), rows(nq * S5_IN)],
        out_shape=[jax.ShapeDtypeStruct((t, S5_CH), F32)] * 2 + [jax.ShapeDtypeStruct((t, S5_WIDTH), F32)],
        scratch_shapes=[pltpu.VMEM((1, cb), F32), pltpu.VMEM((1, cb), F32), pltpu.VMEM((seg, cb), F32),
                        pltpu.VMEM((seg, cb), F32)] + [pltpu.VMEM((nc, tl, LANES), F32)] * 4
        + [pltpu.VMEM((nq * S5_IN // LANES, tl, LANES), F32)] * 2,
        compiler_params=_params(("parallel", "parallel", "arbitrary")),
    )(uf, bbr, bbi, cr, ci, ar, ai)


def _s5_bwd(dys, uf, xr, xi, bbr, bbi, cr, ci, ar, ai, seqs):
    t = dys.shape[0]
    l = t // seqs
    tl = min(SCAN_ROWS, l)
    nl = l // tl
    seg = tl // SCAN_SEGS
    cb, nq = SCAN_COLS, SCAN_CHUNKS
    nc = cb // LANES
    per = S5_ST // LANES

    def body(dy_ref, u_ref, xr_ref, xi_ref, bbr_ref, bbi_ref, cr_ref, ci_ref, ar_ref, ai_ref,
             du_ref, dbbr_ref, dbbi_ref, dcr_ref, dci_ref, dar_ref, dai_ref,
             car_r, car_i, pw_r, pw_i, g_r, g_i, lam_r, lam_i, x_r, x_i, acc_r, acc_i, tmp_a, tmp_b):
        @pl.when(pl.program_id(2) == 0)
        def _():
            car_r[...] = jnp.zeros(car_r.shape, F32)
            car_i[...] = jnp.zeros(car_i.shape, F32)
            _powers_into(pw_r, pw_i, ar_ref[...], ai_ref[...], seg)
            for acc_ref in (dbbr_ref, dbbi_ref, dcr_ref, dci_ref, dar_ref, dai_ref):
                acc_ref[...] = jnp.zeros(acc_ref.shape, F32)

        dy = _interleaved(dy_ref, tmp_a, tmp_b, seg).astype(BF16)
        for q in range(nq):
            dyq = dy[:, q * S5_IN:(q + 1) * S5_IN]
            gr = lax.dot_general(dyq, cr_ref[q], _NT, preferred_element_type=F32)
            gi = lax.dot_general(dyq, ci_ref[q], _NT, preferred_element_type=F32)
            for s in range(per):
                g_r[q * per + s] = gr[:, s * LANES:(s + 1) * LANES]
                g_i[q * per + s] = gi[:, s * LANES:(s + 1) * LANES]
        for c in range(nc):
            x_r[c] = xr_ref[:, c * LANES:(c + 1) * LANES]
            x_i[c] = xi_ref[:, c * LANES:(c + 1) * LANES]
        acc_r[...] = jnp.zeros(acc_r.shape, F32)
        acc_i[...] = jnp.zeros(acc_i.shape, F32)

        def visit(c, rws, lr, li):
            xr_t, xi_t = x_r[c, rws, :], x_i[c, rws, :]
            acc_r[c] += lr * xr_t + li * xi_t
            acc_i[c] += li * xr_t - lr * xi_t

        _segment_scan(g_r, g_i, lam_r, lam_i, pw_r, pw_i, car_r, car_i, seg, -1.0, True, visit)
        for c in range(nc):
            dar_ref[:, c * LANES:(c + 1) * LANES] += jnp.sum(acc_r[c], axis=0, keepdims=True)
            dai_ref[:, c * LANES:(c + 1) * LANES] += jnp.sum(acc_i[c], axis=0, keepdims=True)
        u = _interleaved(u_ref, tmp_a, tmp_b, seg).astype(BF16)
        du = []
        for q in range(nq):
            st = slice(q * S5_ST, (q + 1) * S5_ST)
            io = slice(q * S5_IN, (q + 1) * S5_IN)
            lq_r = jnp.concatenate([lam_r[q * per + s] for s in range(per)], axis=1).astype(BF16)
            lq_i = jnp.concatenate([lam_i[q * per + s] for s in range(per)], axis=1).astype(BF16)
            du.append(lax.dot_general(lq_r, bbr_ref[q], _NT, preferred_element_type=F32)
                      + lax.dot_general(lq_i, bbi_ref[q], _NT, preferred_element_type=F32))
            dbbr_ref[q] += lax.dot_general(u[:, io], lq_r, _TN, preferred_element_type=F32)
            dbbi_ref[q] += lax.dot_general(u[:, io], lq_i, _TN, preferred_element_type=F32)
            dcr_ref[q] += lax.dot_general(xr_ref[:, st].astype(BF16), dy[:, io], _TN, preferred_element_type=F32)
            dci_ref[q] += lax.dot_general(xi_ref[:, st].astype(BF16), dy[:, io], _TN, preferred_element_type=F32)
        _store_deinterleaved(du_ref, jnp.concatenate(du, axis=1), tmp_a, tmp_b, seg)

    rows = lambda w: pl.BlockSpec((tl, w), lambda s, j, r: (s * nl + nl - 1 - r, j))
    chunk = lambda a: pl.BlockSpec((nq,) + a.shape[1:], lambda s, j, r: (j, 0, 0))
    acc = lambda a: pl.BlockSpec((None, nq) + a.shape[1:], lambda s, j, r: (s, j, 0, 0))
    par = pl.BlockSpec((1, cb), lambda s, j, r: (0, j))
    par_acc = pl.BlockSpec((None, 1, cb), lambda s, j, r: (s, 0, j))
    per_seq = lambda a: jax.ShapeDtypeStruct((seqs,) + a.shape, F32)
    return pl.pallas_call(
        body, name="s5_bwd", grid=(seqs, S5_CH // cb, nl),
        in_specs=[rows(nq * S5_IN), rows(nq * S5_IN), rows(cb), rows(cb), chunk(bbr), chunk(bbi), chunk(cr), chunk(ci),
                  par, par],
        out_specs=[rows(nq * S5_IN), acc(bbr), acc(bbi), acc(cr), acc(ci), par_acc, par_acc],
        out_shape=[jax.ShapeDtypeStruct((t, S5_WIDTH), F32), per_seq(bbr), per_seq(bbi), per_seq(cr), per_seq(ci),
                   jax.ShapeDtypeStruct((seqs, 1, S5_CH), F32), jax.ShapeDtypeStruct((seqs, 1, S5_CH), F32)],
        scratch_shapes=[pltpu.VMEM((1, cb), F32), pltpu.VMEM((1, cb), F32), pltpu.VMEM((seg, cb), F32),
                        pltpu.VMEM((seg, cb), F32)] + [pltpu.VMEM((nc, tl, LANES), F32)] * 6
        + [pltpu.VMEM((nc, SCAN_SEGS, LANES), F32)] * 2 + [pltpu.VMEM((nq * S5_IN // LANES, tl, LANES), F32)] * 2,
        compiler_params=_params(("parallel", "parallel", "arbitrary")),
    )(dys, uf, xr, xi, bbr, bbi, cr, ci, ar, ai)


XATT_BLOCK = 2048


def _xatt_probs(qv, kv):
    s = lax.dot_general(qv, kv, _NT, preferred_element_type=F32) * (X_HEAD_DIM ** -0.5)
    e = jnp.exp(s - jnp.max(s, axis=-1, keepdims=True))
    return e / jnp.sum(e, axis=-1, keepdims=True)


def _xatt_fwd(q, k, kv, seqs):
    t = q.shape[0]
    tq = min(XATT_BLOCK, t // seqs)
    nq = t // seqs // tq

    def body(q_ref, k_ref, v_ref, o_ref):
        p = _xatt_probs(q_ref[...], k_ref[...])
        o_ref[...] = jnp.dot(p.astype(BF16), v_ref[...].astype(BF16), preferred_element_type=F32).astype(o_ref.dtype)

    qs = pl.BlockSpec((tq, X_HEAD_DIM), lambda b, h, i: (b * nq + i, h))
    return pl.pallas_call(
        body, name="xatt_fwd", grid=(seqs, N_X_HEADS, nq),
        in_specs=[qs, pl.BlockSpec((N_MEM, X_HEAD_DIM), lambda b, h, i: (b, h)),
                  pl.BlockSpec((N_MEM, X_HEAD_DIM), lambda b, h, i: (b, N_X_HEADS + h))],
        out_specs=qs, out_shape=jax.ShapeDtypeStruct(q.shape, BF16),
        compiler_params=_params(("parallel", "parallel", "parallel")),
    )(q, k, kv)


def _xatt_bwd(q, k, kv, do, seqs):
    t = q.shape[0]
    tq = min(XATT_BLOCK, t // seqs)
    nq = t // seqs // tq
    scale = X_HEAD_DIM ** -0.5

    def body(q_ref, k_ref, v_ref, do_ref, dq_ref, dk_ref, dv_ref):
        @pl.when(pl.program_id(2) == 0)
        def _():
            dk_ref[...] = jnp.zeros(dk_ref.shape, F32)
            dv_ref[...] = jnp.zeros(dv_ref.shape, F32)

        qv, kk = q_ref[...], k_ref[...]
        p = _xatt_probs(qv, kk)
        dob = do_ref[...].astype(BF16)
        dp = lax.dot_general(dob, v_ref[...].astype(BF16), _NT, preferred_element_type=F32)
        ds = p * (dp - jnp.sum(dp * p, axis=-1, keepdims=True))
        dsb = ds.astype(BF16)
        dq_ref[...] = jnp.dot(dsb, kk, preferred_element_type=F32) * scale
        dk_ref[...] += lax.dot_general(dsb, qv, _TN, preferred_element_type=F32) * scale
        dv_ref[...] += lax.dot_general(p.astype(BF16), dob, _TN, preferred_element_type=F32)

    qs = pl.BlockSpec((tq, X_HEAD_DIM), lambda b, h, i: (b * nq + i, h))
    ks = pl.BlockSpec((N_MEM, X_HEAD_DIM), lambda b, h, i: (b, h))
    return pl.pallas_call(
        body, name="xatt_bwd", grid=(seqs, N_X_HEADS, nq),
        in_specs=[qs, ks, pl.BlockSpec((N_MEM, X_HEAD_DIM), lambda b, h, i: (b, N_X_HEADS + h)), qs],
        out_specs=[qs, ks, ks],
        out_shape=[jax.ShapeDtypeStruct(q.shape, F32), jax.ShapeDtypeStruct(k.shape, F32),
                   jax.ShapeDtypeStruct(k.shape, F32)],
        compiler_params=_params(("parallel", "parallel", "arbitrary")),
    )(q, k, kv, do)


CONV_COLS = 256


def _shift_down(x, k, row):
    return jnp.where(row >= k, pltpu.roll(x, k, 0), 0.0)


def _shift_up(x, k, row):
    n = x.shape[0]
    return jnp.where(row < n - k, pltpu.roll(x, n - k, 0), 0.0)


def _conv_pre(g, w, b, row):
    return b + w[0:1, :] * _shift_down(g, 2, row) + w[1:2, :] * _shift_down(g, 1, row) + w[2:3, :] * g


def _convgate_fwd(gu, w, b, seqs):
    t = gu.shape[0]
    l = t // seqs
    nc = D_FF // CONV_COLS

    def body(g_ref, u_ref, w_ref, b_ref, o_ref):
        g = g_ref[...].astype(F32)
        row = lax.broadcasted_iota(jnp.int32, g.shape, 0)
        pre = _conv_pre(g, w_ref[...], b_ref[...], row)
        o_ref[...] = (pre * jax.nn.sigmoid(pre) * u_ref[...].astype(F32)).astype(o_ref.dtype)

    return pl.pallas_call(
        body, name="convgate_fwd", grid=(seqs, nc),
        in_specs=[pl.BlockSpec((l, CONV_COLS), lambda s, j: (s, j)), pl.BlockSpec((l, CONV_COLS), lambda s, j: (s, nc + j)),
                  pl.BlockSpec((3, CONV_COLS), lambda s, j: (0, j)), pl.BlockSpec((1, CONV_COLS), lambda s, j: (0, j))],
        out_specs=pl.BlockSpec((l, CONV_COLS), lambda s, j: (s, j)),
        out_shape=jax.ShapeDtypeStruct((t, D_FF), BF16),
        compiler_params=_params(("parallel", "parallel")),
    )(gu, gu, w, b)


def _convgate_bwd(gu, w, b, dact, seqs):
    t = gu.shape[0]
    l = t // seqs
    nc = D_FF // CONV_COLS
    steps = nc * seqs

    def body(g_ref, u_ref, w_ref, b_ref, da_ref, dgu_ref, dw_ref, db_ref, stage, sems):
        j, s = pl.program_id(0), pl.program_id(1)
        n = j * seqs + s
        slot = n % 2

        def copies(slot_, j_, s_):
            rows = pl.ds(pl.multiple_of(s_ * l, 16), l)
            return [pltpu.make_async_copy(
                stage.at[slot_, half],
                dgu_ref.at[rows, pl.ds(pl.multiple_of((half * nc + j_) * CONV_COLS, 128), CONV_COLS)],
                sems.at[slot_, half]) for half in (0, 1)]

        @pl.when(s == 0)
        def _():
            dw_ref[...] = jnp.zeros(dw_ref.shape, F32)
            db_ref[...] = jnp.zeros(db_ref.shape, F32)

        @pl.when(n >= 2)
        def _():
            for cp in copies(slot, j, s):
                cp.wait()

        g, wv, da = g_ref[...].astype(F32), w_ref[...], da_ref[...].astype(F32)
        row = lax.broadcasted_iota(jnp.int32, g.shape, 0)
        g1, g2 = _shift_down(g, 1, row), _shift_down(g, 2, row)
        pre = b_ref[...] + wv[0:1, :] * g2 + wv[1:2, :] * g1 + wv[2:3, :] * g
        sg = jax.nn.sigmoid(pre)
        silu = pre * sg
        stage[slot, 1] = (da * silu).astype(stage.dtype)
        dpre = da * u_ref[...].astype(F32) * (sg * (1.0 + pre * (1.0 - sg)))
        dg = wv[2:3, :] * dpre + wv[1:2, :] * _shift_up(dpre, 1, row) + wv[0:1, :] * _shift_up(dpre, 2, row)
        stage[slot, 0] = dg.astype(stage.dtype)
        for cp in copies(slot, j, s):
            cp.start()
        dw_ref[0:1, :] += jnp.sum(dpre * g2, axis=0, keepdims=True)
        dw_ref[1:2, :] += jnp.sum(dpre * g1, axis=0, keepdims=True)
        dw_ref[2:3, :] += jnp.sum(dpre * g, axis=0, keepdims=True)
        db_ref[...] += jnp.sum(dpre, axis=0, keepdims=True)

        @pl.when(n == steps - 1)
        def _():
            for cp in copies(slot, j, s) + (copies(1 - slot, j, s) if steps > 1 else []):
                cp.wait()

    blk = lambda off: pl.BlockSpec((l, CONV_COLS), lambda j, s: (s, off + j))
    return pl.pallas_call(
        body, name="convgate_bwd", grid=(nc, seqs),
        in_specs=[blk(0), blk(nc), pl.BlockSpec((3, CONV_COLS), lambda j, s: (0, j)),
                  pl.BlockSpec((1, CONV_COLS), lambda j, s: (0, j)), blk(0)],
        out_specs=[ANY, pl.BlockSpec((3, CONV_COLS), lambda j, s: (0, j)),
                   pl.BlockSpec((1, CONV_COLS), lambda j, s: (0, j))],
        out_shape=[jax.ShapeDtypeStruct((t, 2 * D_FF), BF16), jax.ShapeDtypeStruct((3, D_FF), F32),
                   jax.ShapeDtypeStruct((1, D_FF), F32)],
        scratch_shapes=[pltpu.VMEM((2, 2, l, CONV_COLS), BF16), pltpu.SemaphoreType.DMA((2, 2))],
        compiler_params=_params(("arbitrary", "arbitrary")),
    )(gu, gu, w, b, dact)


def _loss_head(h, target):
    t, d = h.shape
    tm = _pick(t, (256, 128, 8))

    def body(h_ref, t_ref, dh_ref, dhb_ref, loss_ref):
        @pl.when(pl.program_id(0) == 0)
        def _():
            loss_ref[...] = jnp.zeros(loss_ref.shape, F32)

        e = h_ref[...] - t_ref[...]
        dh = e * (1.0 / d)
        dh_ref[...] = dh
        dhb_ref[...] = dh.astype(BF16)
        loss_ref[...] += (0.5 / d) * jnp.sum(jnp.sum(e * e, axis=1, keepdims=True), axis=0, keepdims=True)

    blk = pl.BlockSpec((tm, d), lambda i: (i, 0))
    return pl.pallas_call(
        body, name="loss_head", grid=(t // tm,), in_specs=[blk, blk],
        out_specs=[blk, blk, pl.BlockSpec((1, 1), lambda i: (0, 0))],
        out_shape=[jax.ShapeDtypeStruct((t, d), F32), jax.ShapeDtypeStruct((t, d), BF16),
                   jax.ShapeDtypeStruct((1, 1), F32)],
        compiler_params=_params(("arbitrary",)),
    )(h, target)


def _s5_discretise(a_re, a_im, log_dt, b_re, b_im):
    dt = jnp.exp(log_dt)[:, None]
    mag = jnp.exp(a_re * dt)
    lb_r = mag * jnp.cos(a_im * dt)
    lb_i = mag * jnp.sin(a_im * dt)
    den = a_re * a_re + a_im * a_im
    nr = lb_r - 1.0
    coef_r = (nr * a_re + lb_i * a_im) / den
    coef_i = (lb_i * a_re - nr * a_im) / den
    bb_r = coef_r[:, :, None] * b_re - coef_i[:, :, None] * b_im
    bb_i = coef_r[:, :, None] * b_im + coef_i[:, :, None] * b_re
    return lb_r, lb_i, bb_r, bb_i


S5_CHUNKS = 4
S5_PER = S5_GROUPS // S5_CHUNKS


def _blockdiag_in(bb):
    eye = jnp.eye(S5_PER, dtype=bb.dtype)
    return jnp.einsum("jgpc,gh->jgchp", bb.reshape(S5_CHUNKS, S5_PER, S5_STATE, S5_GROUP_CH), eye).reshape(
        S5_CHUNKS, S5_PER * S5_GROUP_CH, S5_PER * S5_STATE)


def _blockdiag_in_grad(d):
    eye = jnp.eye(S5_PER, dtype=d.dtype)
    return jnp.einsum("jgchp,gh->jgpc", d.reshape(S5_CHUNKS, S5_PER, S5_GROUP_CH, S5_PER, S5_STATE), eye).reshape(
        S5_GROUPS, S5_STATE, S5_GROUP_CH)


def _blockdiag_out(c):
    eye = jnp.eye(S5_PER, dtype=c.dtype)
    return jnp.einsum("jgcp,gh->jgphc", c.reshape(S5_CHUNKS, S5_PER, S5_GROUP_CH, S5_STATE), eye).reshape(
        S5_CHUNKS, S5_PER * S5_STATE, S5_PER * S5_GROUP_CH)


def _blockdiag_out_grad(d):
    eye = jnp.eye(S5_PER, dtype=d.dtype)
    return jnp.einsum("jgphc,gh->jgcp", d.reshape(S5_CHUNKS, S5_PER, S5_STATE, S5_PER, S5_GROUP_CH), eye).reshape(
        S5_GROUPS, S5_GROUP_CH, S5_STATE)


def _local_step(x3, mem3, target3, p, wb, late_weights=None, early_grads=None):
    seqs, l, d = x3.shape
    t = seqs * l
    x = x3.reshape(t, d)
    mem = mem3.reshape(seqs * N_MEM, d)
    target = target3.reshape(t, d)
    full = lambda a: (a, a.shape[1], 0, 0)

    s5_in = (p["s5_a_re"], p["s5_a_im"], p["s5_log_dt"], p["s5_b_re"], p["s5_b_im"])
    (lb_r, lb_i, bb_r, bb_i), s5_pull = jax.vjp(_s5_discretise, *s5_in)
    ar, ai = lb_r.reshape(1, S5_CH), lb_i.reshape(1, S5_CH)
    bbr_d, bbi_d = _blockdiag_in(bb_r).astype(BF16), _blockdiag_in(bb_i).astype(BF16)
    cr_d, ci_d = _blockdiag_out(p["s5_c_re"]).astype(BF16), (-_blockdiag_out(p["s5_c_im"])).astype(BF16)
    d_row = p["s5_d"].reshape(1, S5_WIDTH)

    w_in = wb["w_in"]
    w_qkv = w_in[:, :3 * FOX_WIDTH]
    w_uf = jnp.concatenate(
        [w_in[:, 3 * FOX_WIDTH + N_FOX_HEADS:], w_in[:, 3 * FOX_WIDTH:3 * FOX_WIDTH + N_FOX_HEADS],
         jnp.zeros((d, UF_COLS - S5_WIDTH - N_FOX_HEADS), w_in.dtype)], axis=1)

    hn1 = _rowwise(_rms, [full(x)], [p["norm_mix"]], [(d, d, 0, BF16)], "norm_mix_fwd")
    qkv = _mm(hn1, w_qkv, "nn", "in_qkv")
    uf = _mm(hn1, w_uf, "nn", "in_uf")

    bh = seqs * N_FOX_HEADS
    q_pair = (qkv, 128, 0, 1)
    k_pair = (qkv, 128, N_PAIRS, 1)
    gq2, gk2 = jnp.tile(p["fox_q_norm"], (1, 2)), jnp.tile(p["fox_k_norm"], (1, 2))
    pair_out = [(FOX_WIDTH, 128, 1, BF16)]
    qn = _rowwise(_rms_pair, [q_pair], [gq2], pair_out, "fox_qnorm_fwd", heads=N_PAIRS)
    kn = _rowwise(_rms_pair, [k_pair], [gk2], pair_out, "fox_knorm_fwd", heads=N_PAIRS)

    f_rows = uf[:, S5_WIDTH:S5_WIDTH + N_FOX_HEADS].reshape(seqs, l, N_FOX_HEADS).transpose(0, 2, 1).reshape(bh, l)
    f_bias = jnp.tile(p["fox_f_bias"].reshape(N_FOX_HEADS, 1), (seqs, 1))
    c_wide = jnp.broadcast_to(_forget_fwd(f_rows, f_bias)[:, :, None], (bh, l, 128))
    fox, lse = _fox_fwd(qn, kn, qkv, c_wide, seqs)

    xr, xi, ys = _s5_fwd(uf, bbr_d, bbi_d, cr_d, ci_d, ar, ai, seqs)
    u_blk = (uf, S5_WIDTH, 0, 0)
    yg = _rowwise(_s5_act, [full(ys), u_blk], [d_row], [(S5_WIDTH, S5_WIDTH, 0, F32)], "s5_act_fwd")
    z = _mm(yg, wb["s5_w_glu"], "nn", "s5_glu")
    y2n = _rowwise(_s5_gate, [full(yg), full(z)], [p["s5_b_glu"], p["out_norm_s5"]],
                   [(S5_WIDTH, S5_WIDTH, 0, BF16)], "s5_gate_fwd")
    foxn = _rowwise(_rms, [full(fox)], [p["out_norm_fox"]], [(FOX_WIDTH, FOX_WIDTH, 0, BF16)], "fox_outnorm_fwd")
    mixed = jnp.concatenate([foxn, y2n], axis=1)
    h1 = _mm(mixed, wb["w_out"], "nn", "mix_out", res=x)
    if late_weights is not None:
        wb = dict(wb, **late_weights(h1))

    hn2 = _rowwise(_rms, [full(h1)], [p["norm_cross"]], [(d, d, 0, BF16)], "norm_cross_fwd")
    mn = _rowwise(_rms, [full(mem)], [p["norm_mem"]], [(d, d, 0, BF16)], "norm_mem_fwd")
    xq_raw = _mm(hn2, wb["w_xq"], "nn", "x_q")
    kv = _mm(mn, wb["w_xkv"], "nn", "x_kv")
    xh = lambda a: (a, X_HEAD_DIM, 0, 1)
    xqn = _rowwise(_rms, [xh(xq_raw)], [p["xq_norm"]], [(d, X_HEAD_DIM, 1, BF16)], "x_qnorm_fwd", heads=N_X_HEADS)
    xkn = _rowwise(_rms, [xh(kv)], [p["xk_norm"]], [(d, X_HEAD_DIM, 1, BF16)], "x_knorm_fwd", heads=N_X_HEADS)
    xo = _xatt_fwd(xqn, xkn, kv, seqs)
    h2 = _mm(xo, wb["w_xo"], "nn", "x_out", res=h1)

    hn3 = _rowwise(_rms, [full(h2)], [p["norm_ffn"]], [(d, d, 0, BF16)], "norm_ffn_fwd")
    gu = _mm(hn3, wb["w_ffn_up"], "nn", "ffn_up", out_dtype=BF16)
    act = _convgate_fwd(gu, p["ffn_conv_w"], p["ffn_conv_b"], seqs)
    h3 = _mm(act, wb["w_ffn_down"], "nn", "ffn_down", res=h2)
    dh3, dh3_b, loss = _loss_head(h3, target)

    g = {}
    dact = _mm(dh3_b, wb["w_ffn_down"], "nt", "ffn_down_dx", out_dtype=BF16)
    late_dt = BF16 if early_grads is not None else F32
    g["w_ffn_down"] = _mm(act, dh3_b, "tn", "ffn_down_dw", out_dtype=late_dt)
    dgu, g["ffn_conv_w"], g["ffn_conv_b"] = _convgate_bwd(gu, p["ffn_conv_w"], p["ffn_conv_b"], dact, seqs)
    dhn3 = _mm(dgu, wb["w_ffn_up"], "nt", "ffn_up_dx")
    g["w_ffn_up"] = _mm(hn3, dgu, "tn", "ffn_up_dw", out_dtype=late_dt)
    (dh2,), (g["norm_ffn"],) = _rowwise_vjp(_rms, [full(h2)], [p["norm_ffn"]], [full(dhn3)], "norm_ffn_bwd",
                                            adds=[full(dh3)])

    dxo = _mm(dh2, wb["w_xo"], "nt", "x_out_dx")
    g["w_xo"] = _mm(xo, dh2, "tn", "x_out_dw", out_dtype=late_dt)
    dxqn, dxkn, dxv = _xatt_bwd(xqn, xkn, kv, dxo, seqs)
    (dxq_raw,), (g["xq_norm"],) = _rowwise_vjp(_rms, [xh(xq_raw)], [p["xq_norm"]], [xh(dxqn)], "x_qnorm_bwd",
                                               heads=N_X_HEADS, row_dtypes=[BF16])
    (dxk_raw,), (g["xk_norm"],) = _rowwise_vjp(_rms, [xh(kv)], [p["xk_norm"]], [xh(dxkn)], "x_knorm_bwd",
                                               heads=N_X_HEADS, row_dtypes=[BF16])
    dkv = jnp.concatenate([dxk_raw, dxv.astype(BF16)], axis=1)
    dhn2 = _mm(dxq_raw, wb["w_xq"], "nt", "x_q_dx")
    g["w_xq"] = _mm(hn2, dxq_raw, "tn", "x_q_dw", out_dtype=late_dt)
    dmn = _mm(dkv, wb["w_xkv"], "nt", "x_kv_dx")
    g["w_xkv"] = _mm(mn, dkv, "tn", "x_kv_dw", out_dtype=late_dt)
    norm_cross = p["norm_cross"]
    if early_grads is not None:
        norm_cross = norm_cross + early_grads({n: g[n] for n in LATE_WEIGHTS})
    (dh1,), (g["norm_cross"],) = _rowwise_vjp(_rms, [full(h1)], [norm_cross], [full(dhn2)], "norm_cross_bwd",
                                              adds=[full(dh2)])
    _, (g["norm_mem"],) = _rowwise_vjp(_rms, [full(mem)], [p["norm_mem"]], [full(dmn)], "norm_mem_bwd",
                                       row_dtypes=[BF16])

    dmixed = _mm(dh1, wb["w_out"], "nt", "mix_out_dx")
    g["w_out"] = _mm(mixed, dh1, "tn", "mix_out_dw")
    (dfox,), (g["out_norm_fox"],) = _rowwise_vjp(_rms, [full(fox)], [p["out_norm_fox"]],
                                                 [(dmixed, FOX_WIDTH, 0, 0)], "fox_outnorm_bwd")
    (dyg_a, dz), (g["s5_b_glu"], g["out_norm_s5"]) = _rowwise_vjp(
        _s5_gate, [full(yg), full(z)], [p["s5_b_glu"], p["out_norm_s5"]], [(dmixed, S5_WIDTH, 1, 0)], "s5_gate_bwd",
        row_dtypes=[F32, BF16])
    dyg = _mm(dz, wb["s5_w_glu"], "nt", "s5_glu_dx", res=dyg_a)
    g["s5_w_glu"] = _mm(yg, dz, "tn", "s5_glu_dw")
    (dys, du_a), (dd_row,) = _rowwise_vjp(_s5_act, [full(ys), u_blk], [d_row], [full(dyg)], "s5_act_bwd",
                                          row_dtypes=[BF16, F32])
    g["s5_d"] = dd_row
    du_b, dbbr_d, dbbi_d, dcr_d, dci_d, dar, dai = _s5_bwd(dys, uf, xr, xi, bbr_d, bbi_d, cr_d, ci_d, ar, ai, seqs)
    dbbr_d, dbbi_d, dcr_d, dci_d = (jnp.sum(a, axis=0) for a in (dbbr_d, dbbi_d, dcr_d, dci_d))
    d_lb_r = jnp.sum(dar, axis=0).reshape(S5_GROUPS, S5_STATE)
    d_lb_i = jnp.sum(dai, axis=0).reshape(S5_GROUPS, S5_STATE)
    g["s5_a_re"], g["s5_a_im"], g["s5_log_dt"], g["s5_b_re"], g["s5_b_im"] = s5_pull(
        (d_lb_r, d_lb_i, _blockdiag_in_grad(dbbr_d), _blockdiag_in_grad(dbbi_d)))
    g["s5_c_re"] = _blockdiag_out_grad(dcr_d)
    g["s5_c_im"] = -_blockdiag_out_grad(dci_d)

    dqn, dkn, dv, dc, dcq = _fox_bwd(qn, kn, qkv, c_wide, fox, dfox, lse, seqs)
    pair = lambda a: (a, 128, 0, 1)
    (dq_raw,), (dgq2,) = _rowwise_vjp(_rms_pair, [q_pair], [gq2], [pair(dqn)], "fox_qnorm_bwd", heads=N_PAIRS,
                                      row_dtypes=[BF16])
    (dk_raw,), (dgk2,) = _rowwise_vjp(_rms_pair, [k_pair], [gk2], [pair(dkn)], "fox_knorm_bwd", heads=N_PAIRS,
                                      row_dtypes=[BF16])
    g["fox_q_norm"] = dgq2[:, :HEAD_DIM] + dgq2[:, HEAD_DIM:]
    g["fox_k_norm"] = dgk2[:, :HEAD_DIM] + dgk2[:, HEAD_DIM:]
    df_rows, dfb = _forget_bwd(f_rows, f_bias, (dc + dcq).reshape(bh, l))
    g["fox_f_bias"] = jnp.sum(dfb.reshape(seqs, N_FOX_HEADS), axis=0)
    df = df_rows.reshape(seqs, N_FOX_HEADS, l).transpose(0, 2, 1).reshape(t, N_FOX_HEADS)
    dqkv = jnp.concatenate([dq_raw, dk_raw, dv.astype(BF16)], axis=1)
    duf = jnp.concatenate([du_a + du_b, df, jnp.zeros((t, UF_COLS - S5_WIDTH - N_FOX_HEADS), F32)],
                          axis=1).astype(BF16)
    dhn1 = _mm(duf, w_uf, "nt", "in_uf_dx", res=_mm(dqkv, w_qkv, "nt", "in_qkv_dx"))
    dw_qkv = _mm(hn1, dqkv, "tn", "in_qkv_dw")
    dw_uf = _mm(hn1, duf, "tn", "in_uf_dw")
    g["w_in"] = jnp.concatenate([dw_qkv, dw_uf[:, S5_WIDTH:S5_WIDTH + N_FOX_HEADS], dw_uf[:, :S5_WIDTH]], axis=1)
    (dx,), (g["norm_mix"],) = _rowwise_vjp(_rms, [full(x)], [p["norm_mix"]], [full(dhn1)], "norm_mix_bwd",
                                           adds=[full(dh1)])
    return loss, dx.reshape(seqs, l, d), g


def _place():
    return lax.axis_index("x"), lax.axis_index("y"), lax.axis_index("c")


def _other_chips(x, y):
    return [(1 - x, y), (x, 1 - y), (1 - x, 1 - y)]


ANY = pl.BlockSpec(memory_space=pl.ANY)


def _gather_weights(shards, col_kind, taps):
    n = len(shards)

    def body(*refs):
        ins, tap_in, outs, tap_out = refs[:n], refs[n], refs[n + 1:2 * n + 1], refs[2 * n + 1]
        ici_send, ici_recv, d2d_send, d2d_recv, own_send, own_recv = refs[2 * n + 2:]
        x, y, c = _place()
        mine = 2 * x + y
        chips = _other_chips(x, y)
        sibling = (x, y, 1 - c)

        def piece(a, s, h):
            r, cs = ins[a].shape
            hr = r // 2
            if col_kind[a]:
                return outs[a].at[pl.ds(pl.multiple_of(h * hr, 16), hr), pl.ds(pl.multiple_of(s * cs, 128), cs)]
            return outs[a].at[pl.ds(pl.multiple_of(s * r + h * hr, 16), hr), :]

        def slab(a, s):
            r, cs = ins[a].shape
            if col_kind[a]:
                return outs[a].at[:, pl.ds(pl.multiple_of(s * cs, 128), cs)]
            return outs[a].at[pl.ds(pl.multiple_of(s * r, 16), r), :]

        def own_half(a, h):
            hr = ins[a].shape[0] // 2
            return ins[a].at[pl.ds(pl.multiple_of(h * hr, 16), hr), :]

        sends = []
        for a in range(n):
            cp = pltpu.make_async_remote_copy(
                src_ref=ins[a], dst_ref=slab(a, mine), send_sem=own_send.at[a], recv_sem=own_recv.at[a],
                device_id=sibling, device_id_type=MESH)
            cp.start()
            sends.append(cp)
        cp = pltpu.make_async_remote_copy(
            src_ref=tap_in, dst_ref=tap_out.at[mine], send_sem=own_send.at[n], recv_sem=own_recv.at[n],
            device_id=sibling, device_id_type=MESH)
        cp.start()
        sends.append(cp)
        for a in range(n):
            for j, (px, py) in enumerate(chips):
                cp = pltpu.make_async_remote_copy(
                    src_ref=own_half(a, c), dst_ref=piece(a, mine, c), send_sem=ici_send.at[3 * a + j],
                    recv_sem=ici_recv.at[3 * a + j], device_id=(px, py, c), device_id_type=MESH)
                cp.start()
                sends.append(cp)
        for j, (px, py) in enumerate(chips):
            cp = pltpu.make_async_remote_copy(
                src_ref=tap_in, dst_ref=tap_out.at[mine], send_sem=ici_send.at[3 * n + j],
                recv_sem=ici_recv.at[3 * n + j], device_id=(px, py, c), device_id_type=MESH)
            cp.start()
            sends.append(cp)
        for a in range(n):
            for j, (px, py) in enumerate(chips):
                got = piece(a, 2 * px + py, c)
                pltpu.make_async_remote_copy(
                    src_ref=got, dst_ref=got, send_sem=ici_send.at[3 * a + j], recv_sem=ici_recv.at[3 * a + j],
                    device_id=(px, py, c), device_id_type=MESH).wait_recv()
                fwd = pltpu.make_async_remote_copy(
                    src_ref=got, dst_ref=got, send_sem=d2d_send.at[3 * a + j], recv_sem=d2d_recv.at[3 * a + j],
                    device_id=(x, y, 1 - c), device_id_type=MESH)
                fwd.start()
                sends.append(fwd)
        for a in range(n):
            for j, (px, py) in enumerate(chips):
                other = piece(a, 2 * px + py, 1 - c)
                pltpu.make_async_remote_copy(
                    src_ref=other, dst_ref=other, send_sem=d2d_send.at[3 * a + j], recv_sem=d2d_recv.at[3 * a + j],
                    device_id=(x, y, 1 - c), device_id_type=MESH).wait_recv()
        for j, (px, py) in enumerate(chips):
            pltpu.make_async_remote_copy(
                src_ref=tap_in, dst_ref=tap_out.at[2 * px + py], send_sem=ici_send.at[3 * n + j],
                recv_sem=ici_recv.at[3 * n + j], device_id=(px, py, c), device_id_type=MESH).wait_recv()
        for a in range(n):
            pltpu.make_async_remote_copy(
                src_ref=ins[a], dst_ref=slab(a, mine), send_sem=own_send.at[a], recv_sem=own_recv.at[a],
                device_id=sibling, device_id_type=MESH).wait_recv()
        pltpu.make_async_remote_copy(
            src_ref=tap_in, dst_ref=tap_out.at[mine], send_sem=own_send.at[n], recv_sem=own_recv.at[n],
            device_id=sibling, device_id_type=MESH).wait_recv()
        for cp in sends:
            cp.wait_send()

    def full_shape(a):
        r, cs = shards[a].shape
        return (r, 4 * cs) if col_kind[a] else (4 * r, cs)

    res = pl.pallas_call(
        body, name="gather_weights", in_specs=[ANY] * (n + 1), out_specs=[ANY] * (n + 1),
        out_shape=[jax.ShapeDtypeStruct(full_shape(a), shards[a].dtype) for a in range(n)]
        + [jax.ShapeDtypeStruct((4,) + taps.shape, taps.dtype)],
        scratch_shapes=[pltpu.SemaphoreType.DMA((3 * n + 3,)), pltpu.SemaphoreType.DMA((3 * n + 3,)),
                        pltpu.SemaphoreType.DMA((3 * n,)), pltpu.SemaphoreType.DMA((3 * n,)),
                        pltpu.SemaphoreType.DMA((n + 1,)), pltpu.SemaphoreType.DMA((n + 1,))],
        compiler_params=pltpu.CompilerParams(has_side_effects=True),
    )(*shards, taps)
    return res[:n], res[n]


HBM = pl.BlockSpec(memory_space=pltpu.HBM)
SEM = pl.BlockSpec(memory_space=pltpu.SEMAPHORE)
DATAFLOW = pltpu.SideEffectType.DATAFLOW_SIDE_EFFECTING


def _in_hbm(a):
    return pltpu.with_memory_space_constraint(a, pltpu.HBM)


def _split_start(name, srcs, lands, n_copies, plan):
    n = len(srcs)

    def body(*refs):
        src_refs, land_refs = refs[:n], refs[n:2 * n]
        send_sems, recv_sems = refs[2 * n], refs[2 * n + 1]
        for i, (src, dst, dev) in enumerate(plan(src_refs, land_refs)):
            pltpu.make_async_remote_copy(src_ref=src, dst_ref=dst, send_sem=send_sems.at[i], recv_sem=recv_sems.at[i],
                                         device_id=dev, device_id_type=MESH).start()
        refs[-1][...] = jnp.zeros((8, 128), F32)

    res = pl.pallas_call(
        body, name=name, in_specs=[HBM] * (2 * n),
        out_specs=[SEM, SEM] + [HBM] * (2 * n) + [pl.BlockSpec(memory_space=pltpu.VMEM)],
        out_shape=[pltpu.SemaphoreType.DMA((n_copies,)), pltpu.SemaphoreType.DMA((n_copies,))]
        + [pltpu.HBM(a.shape, a.dtype) for a in list(srcs) + list(lands)] + [jax.ShapeDtypeStruct((8, 128), F32)],
        input_output_aliases={i: 2 + i for i in range(2 * n)},
        compiler_params=pltpu.CompilerParams(has_side_effects=DATAFLOW),
    )(*[_in_hbm(a) for a in list(srcs) + list(lands)])
    return res[0], res[1], list(res[2:2 + n]), list(res[2 + n:2 + 2 * n]), res[-1]


def _split_wait(name, send_sems, recv_sems, srcs, lands, after, plan):
    n = len(srcs)

    def body(*refs):
        src_refs, land_refs = refs[:n], refs[n:2 * n]
        send_ref, recv_ref = refs[2 * n], refs[2 * n + 1]
        for i, (src, dst, dev) in enumerate(plan(src_refs, land_refs)):
            cp = pltpu.make_async_remote_copy(src_ref=src, dst_ref=dst, send_sem=send_ref.at[i], recv_sem=recv_ref.at[i],
                                              device_id=dev, device_id_type=MESH)
            cp.wait_send()
            cp.wait_recv()

    res = pl.pallas_call(
        body, name=name, in_specs=[HBM] * (2 * n) + [SEM, SEM, ANY], out_specs=[HBM] * (2 * n),
        out_shape=[pltpu.HBM(a.shape, a.dtype) for a in list(srcs) + list(lands)],
        input_output_aliases={i: i for i in range(2 * n)},
        compiler_params=pltpu.CompilerParams(has_side_effects=DATAFLOW),
    )(*srcs, *lands, send_sems, recv_sems, after)
    return list(res[:n]), list(res[n:])


def _late_gather_plan(col_kind):
    def plan(src_refs, land_refs):
        x, y, c = _place()
        mine = 2 * x + y
        copies = []
        for a, (src, land) in enumerate(zip(src_refs, land_refs)):
            r, cs = src.shape
            if col_kind[a]:
                dst = land.at[:, pl.ds(pl.multiple_of(mine * cs, 128), cs)]
            else:
                dst = land.at[pl.ds(pl.multiple_of(mine * r, 16), r), :]
            copies.append((src, dst, (x, y, 1 - c)))
            copies += [(src, dst, (px, py, c)) for (px, py) in _other_chips(x, y)]
        return copies
    return plan


def _late_reduce_plan(col_kind):
    def plan(src_refs, land_refs):
        x, y, c = _place()
        copies = []
        for a, (src, land) in enumerate(zip(src_refs, land_refs)):
            for j, (px, py) in enumerate(_other_chips(x, y)):
                if col_kind[a]:
                    cs = land.shape[2]
                    piece = src.at[:, pl.ds(pl.multiple_of((2 * px + py) * cs, 128), cs)]
                else:
                    piece = src.at[2 * px + py]
                copies.append((piece, land.at[j], (px, py, c)))
        return copies
    return plan


def _pair_exchange_halves(name, grads, col_kind):
    n = len(grads)

    def body(*refs):
        ins, outs = refs[:n], refs[n:2 * n]
        send_sems, recv_sems = refs[2 * n:]
        x, y, c = _place()
        copies = []
        for a in range(n):
            if col_kind[a]:
                hr = ins[a].shape[0] // 2
                src = ins[a].at[pl.ds(pl.multiple_of((1 - c) * hr, 8), hr), :]
            else:
                hr = ins[a].shape[1] // 2
                src = ins[a].at[:, pl.ds(pl.multiple_of((1 - c) * hr, 8), hr), :]
            cp = pltpu.make_async_remote_copy(
                src_ref=src, dst_ref=outs[a], send_sem=send_sems.at[a], recv_sem=recv_sems.at[a],
                device_id=(x, y, 1 - c), device_id_type=MESH)
            cp.start()
            copies.append(cp)
        for cp in copies:
            cp.wait()

    def half_shape(a):
        s = grads[a].shape
        return (s[0] // 2, s[1]) if col_kind[a] else (4, s[1] // 2, s[2])

    return pl.pallas_call(
        body, name=name, in_specs=[ANY] * n, out_specs=[ANY] * n,
        out_shape=[jax.ShapeDtypeStruct(half_shape(a), grads[a].dtype) for a in range(n)],
        scratch_shapes=[pltpu.SemaphoreType.DMA((n,)), pltpu.SemaphoreType.DMA((n,))],
        compiler_params=pltpu.CompilerParams(has_side_effects=True),
    )(*grads)


def _pair_swap_halves(name, halves):
    n = len(halves)

    def body(*refs):
        ins, outs = refs[:n], refs[n:2 * n]
        send_sems, recv_sems = refs[2 * n:]
        x, y, c = _place()
        copies = []
        for a in range(n):
            cp = pltpu.make_async_remote_copy(
                src_ref=ins[a], dst_ref=outs[a], send_sem=send_sems.at[a], recv_sem=recv_sems.at[a],
                device_id=(x, y, 1 - c), device_id_type=MESH)
            cp.start()
            copies.append(cp)
        for cp in copies:
            cp.wait()

    return pl.pallas_call(
        body, name=name, in_specs=[ANY] * n, out_specs=[ANY] * n,
        out_shape=[jax.ShapeDtypeStruct(s.shape, s.dtype) for s in halves],
        scratch_shapes=[pltpu.SemaphoreType.DMA((n,)), pltpu.SemaphoreType.DMA((n,))],
        compiler_params=pltpu.CompilerParams(has_side_effects=True),
    )(*halves)


def _chip_sum(name, chip_sel, own, col, others):
    _, r, c = others.shape
    tr = _pick(r, (256, 128, 64, 32, 16))
    if col:
        own_spec = pl.BlockSpec((tr, c), lambda i, s: (i, s[0]))
    else:
        own_spec = pl.BlockSpec((None, tr, c), lambda i, s: (s[0], i, 0))
    specs = [own_spec] + [pl.BlockSpec((None, tr, c), lambda i, s, k=k: (k, i, 0)) for k in range(3)]

    def body(s_ref, own_ref, r0, r1, r2, o_ref):
        o_ref[...] = ((own_ref[...].astype(F32) + r0[...].astype(F32)) + r1[...].astype(F32)) + r2[...].astype(F32)

    return pl.pallas_call(
        body, name=name,
        grid_spec=pltpu.PrefetchScalarGridSpec(
            num_scalar_prefetch=1, grid=(r // tr,), in_specs=specs,
            out_specs=pl.BlockSpec((tr, c), lambda i, s: (i, 0))),
        out_shape=jax.ShapeDtypeStruct((r, c), F32),
        compiler_params=_params(("parallel",)),
    )(chip_sel, own, others, others, others)


def _pair_sum(name, c_sel, grad, recv, col):
    if col:
        r, c4 = grad.shape
        hr, c = r // 2, c4 // 4
    else:
        _, r, c = grad.shape
        hr = r // 2
    tr = _pick(hr, (256, 128, 64, 32, 16))
    nb = hr // tr

    def body(s_ref, g_ref, r_ref, o_ref):
        o_ref[...] = (g_ref[...] + r_ref[...]).astype(o_ref.dtype)

    if col:
        in_specs = [pl.BlockSpec((tr, c), lambda k, i, s: (s[0] * nb + i, k)), pl.BlockSpec((tr, c), lambda k, i, s: (i, k))]
        out_spec = pl.BlockSpec((tr, c), lambda k, i, s: (i, k))
    else:
        in_specs = [pl.BlockSpec((None, tr, c), lambda k, i, s: (k, s[0] * nb + i, 0)),
                    pl.BlockSpec((None, tr, c), lambda k, i, s: (k, i, 0))]
        out_spec = pl.BlockSpec((None, tr, c), lambda k, i, s: (k, i, 0))
    return pl.pallas_call(
        body, name=name,
        grid_spec=pltpu.PrefetchScalarGridSpec(num_scalar_prefetch=1, grid=(4, nb), in_specs=in_specs,
                                               out_specs=out_spec),
        out_shape=jax.ShapeDtypeStruct(recv.shape, BF16),
        compiler_params=_params(("parallel", "parallel")),
    )(c_sel, grad, recv)


def _allreduce_small(vals):
    sizes = [int(math.prod(v.shape)) for v in vals]
    padded = [-(-s // 128) * 128 for s in sizes]
    total = -(-sum(padded) // 1024) * 1024
    flat = [jnp.pad(v.reshape(-1), (0, p - s)) for v, s, p in zip(vals, sizes, padded)]
    flat.append(jnp.zeros((total - sum(padded),), F32))
    packed = jnp.concatenate(flat).reshape(total // 128, 128)

    def body(in_ref, out_ref, r0, r1, r2, send_sems, recv_sems):
        x, y, c = _place()
        out_ref[...] = in_ref[...]
        for k, (peer, land) in enumerate(zip([(x, y, 1 - c), (1 - x, y, c), (x, 1 - y, c)], (r0, r1, r2))):
            cp = pltpu.make_async_remote_copy(
                src_ref=out_ref, dst_ref=land, send_sem=send_sems.at[k], recv_sem=recv_sems.at[k],
                device_id=peer, device_id_type=MESH)
            cp.start()
            cp.wait()
            out_ref[...] = out_ref[...] + land[...]

    vm = pl.BlockSpec(memory_space=pltpu.VMEM)
    summed = pl.pallas_call(
        body, name="allreduce_small", in_specs=[vm], out_specs=vm,
        out_shape=jax.ShapeDtypeStruct(packed.shape, F32),
        scratch_shapes=[pltpu.VMEM(packed.shape, F32)] * 3
        + [pltpu.SemaphoreType.DMA((3,)), pltpu.SemaphoreType.DMA((3,))],
        compiler_params=pltpu.CompilerParams(has_side_effects=True, vmem_limit_bytes=VMEM_LIMIT_BYTES),
    )(packed).reshape(-1)
    outs, off = [], 0
    for v, s, p in zip(vals, sizes, padded):
        outs.append(summed[off:off + s].reshape(v.shape))
        off += p
    return outs


def _adamw_math(w, g, m, v):
    m2 = ADAM_B1 * m + (1.0 - ADAM_B1) * g
    v2 = ADAM_B2 * v + (1.0 - ADAM_B2) * (g * g)
    m_hat = m2 / (1.0 - ADAM_B1 ** ADAM_STEP)
    v_hat = v2 / (1.0 - ADAM_B2 ** ADAM_STEP)
    delta = -ADAM_LR * (m_hat / (jnp.sqrt(v_hat) + ADAM_EPS) + ADAM_WD * w)
    return delta, m2, v2


def _adamw_big(name, c_sel, w, g_mine, g_sibling, m, v, halves):
    _, r, c = w.shape

    def body(s_ref, w_ref, ga_ref, gb_ref, m_ref, v_ref, go_ref, d_ref, mo_ref, vo_ref):
        if halves:
            gv = jnp.where(pl.program_id(0) == s_ref[0], ga_ref[...], gb_ref[...])
        else:
            gv = ga_ref[...] + gb_ref[...]
        d, m2, v2 = _adamw_math(w_ref[...], gv, m_ref[...], v_ref[...])
        go_ref[...] = gv
        d_ref[...] = d
        mo_ref[...] = m2
        vo_ref[...] = v2

    if halves == "cols":
        hc = c // 2
        nb = 1
        blk = pl.BlockSpec((None, r, hc), lambda h, i, s: (0, 0, h))
        half = pl.BlockSpec((r, hc), lambda h, i, s: (0, 0))
    else:
        hr = r // 2
        tr = _pick(hr, (256, 128, 64, 32, 16, 8))
        nb = hr // tr
        blk = pl.BlockSpec((None, tr, c), lambda h, i, s: (0, h * nb + i, 0))
        half = pl.BlockSpec((tr, c), (lambda h, i, s: (i, 0)) if halves else (lambda h, i, s: (h * nb + i, 0)))
    return pl.pallas_call(
        body, name=name,
        grid_spec=pltpu.PrefetchScalarGridSpec(
            num_scalar_prefetch=1, grid=(2, nb), in_specs=[blk, half, half, blk, blk], out_specs=[blk] * 4),
        out_shape=[jax.ShapeDtypeStruct((1, r, c), F32)] * 4, compiler_params=_params(("parallel", "parallel")),
    )(c_sel, w, g_mine, g_sibling, m, v)


def _adamw_small(ws, gs, ms, vs):
    n = len(ws)

    def body(*refs):
        w_r, g_r, m_r, v_r = refs[:n], refs[n:2 * n], refs[2 * n:3 * n], refs[3 * n:4 * n]
        o = refs[4 * n:]
        for a in range(n):
            gv = g_r[a][...]
            d, m2, v2 = _adamw_math(w_r[a][...], gv, m_r[a][...], v_r[a][...])
            o[a][...] = gv
            o[n + a][...] = d
            o[2 * n + a][...] = m2
            o[3 * n + a][...] = v2

    res = pl.pallas_call(
        body, name="adamw_small", out_shape=[jax.ShapeDtypeStruct(w.shape, F32) for _ in range(4) for w in ws],
        compiler_params=_params(),
    )(*ws, *gs, *ms, *vs)
    return res[:n], res[n:2 * n], res[2 * n:3 * n], res[3 * n:]


def _full_from_gathered(name, gathered):
    if name == "w_in":
        rows = gathered.shape[0] // 4
        return gathered.reshape(4, rows, gathered.shape[1]).transpose(1, 0, 2).reshape(rows, 4 * gathered.shape[1])
    return gathered


def _reduce_layout(name, full):
    if name in COL_KIND:
        return full
    if name == "w_in":
        rows, cols = full.shape
        return full.reshape(rows, 4, cols // 4).transpose(1, 0, 2)
    return full.reshape(4, full.shape[0] // 4, full.shape[1])


def kernel(x, mem, norm_mix, w_in, fox_q_norm, fox_k_norm, fox_f_bias, s5_a_re, s5_a_im, s5_log_dt, s5_b_re, s5_b_im, s5_c_re, s5_c_im, s5_d, s5_w_glu, s5_b_glu, out_norm_fox, out_norm_s5, w_out, norm_cross, norm_mem, w_xq, w_xkv, xq_norm, xk_norm, w_xo, norm_ffn, w_ffn_up, ffn_conv_w, ffn_conv_b, w_ffn_down, loss_target, m_norm_mix, m_w_in, m_fox_q_norm, m_fox_k_norm, m_fox_f_bias, m_s5_a_re, m_s5_a_im, m_s5_log_dt, m_s5_b_re, m_s5_b_im, m_s5_c_re, m_s5_c_im, m_s5_d, m_s5_w_glu, m_s5_b_glu, m_out_norm_fox, m_out_norm_s5, m_w_out, m_norm_cross, m_norm_mem, m_w_xq, m_w_xkv, m_xq_norm, m_xk_norm, m_w_xo, m_norm_ffn, m_w_ffn_up, m_ffn_conv_w, m_ffn_conv_b, m_w_ffn_down, v_norm_mix, v_w_in, v_fox_q_norm, v_fox_k_norm, v_fox_f_bias, v_s5_a_re, v_s5_a_im, v_s5_log_dt, v_s5_b_re, v_s5_b_im, v_s5_c_re, v_s5_c_im, v_s5_d, v_s5_w_glu, v_s5_b_glu, v_out_norm_fox, v_out_norm_s5, v_w_out, v_norm_cross, v_norm_mem, v_w_xq, v_w_xkv, v_xq_norm, v_xk_norm, v_w_xo, v_norm_ffn, v_w_ffn_up, v_ffn_conv_w, v_ffn_conv_b, v_w_ffn_down):
    given = dict(locals())
    w = {n: given[n] for n in WEIGHTS}
    m = {n: given["m_" + n] for n in WEIGHTS}
    v = {n: given["v_" + n] for n in WEIGHTS}
    xi, yi, ci = _place()
    chip = (2 * xi + yi).astype(jnp.int32)

    c_sel = ci.astype(jnp.int32).reshape(1)
    chip_sel = chip.reshape(1)
    early_kind = [n in COL_KIND for n in EARLY_WEIGHTS]
    late_kind = [n in COL_KIND for n in LATE_WEIGHTS]

    gathered, taps = _gather_weights([w[n][0].astype(BF16) for n in EARLY_WEIGHTS], early_kind, w["ffn_conv_w"][0])
    wb = {n: _full_from_gathered(n, gathered[k]) for k, n in enumerate(EARLY_WEIGHTS)}
    conv_w = taps.transpose(1, 0, 2).reshape(3, D_FF)
    late_shards = [w[n][0].astype(BF16) for n in LATE_WEIGHTS]
    late_full = [lax.empty((s.shape[0], 4 * s.shape[1]) if ck else (4 * s.shape[0], s.shape[1]), BF16)
                 for s, ck in zip(late_shards, late_kind)]
    gather_plan = _late_gather_plan(late_kind)
    g_send, g_recv, g_srcs, g_lands, g_started = _split_start(
        "gather_late_start", late_shards, late_full, 4 * len(LATE_WEIGHTS), gather_plan)

    def late_weights(after):
        _, full = _split_wait("gather_late_wait", g_send, g_recv, g_srcs, g_lands, after, gather_plan)
        return dict(zip(LATE_WEIGHTS, full))

    reduce_plan = _late_reduce_plan(late_kind)
    late_reduce = {}

    def early_grads(late_g):
        grads = [_reduce_layout(n, late_g[n]) for n in LATE_WEIGHTS]
        lands = [lax.empty((3, s.shape[0], s.shape[1] // 4) if ck else (3,) + s.shape[1:], BF16)
                 for s, ck in zip(grads, late_kind)]
        late_reduce["sems"] = _split_start("reduce_late_start", grads, lands, 3 * len(LATE_WEIGHTS), reduce_plan)
        return late_reduce["sems"][4][0:1, 0:1]

    p = {n: w[n][0] for n in SMALL}
    p["ffn_conv_w"] = conv_w
    for n in ("norm_mix", "fox_q_norm", "fox_k_norm", "fox_f_bias", "s5_b_glu", "out_norm_fox", "out_norm_s5",
              "norm_cross", "norm_mem", "xq_norm", "xk_norm", "norm_ffn", "ffn_conv_b"):
        p[n] = p[n].reshape(1, -1)
    p["norm_mix"] = p["norm_mix"] + g_started[0:1, 0:1]
    loss, grad_x, g = _local_step(x, mem, loss_target, p, wb, late_weights, early_grads)

    grads = [_reduce_layout(n, g[n]) for n in EARLY_WEIGHTS]
    from_sibling = _pair_exchange_halves("reduce_pair_exchange_early", grads, early_kind)
    pair_sums = [_pair_sum("reduce_pair_sum_" + n, c_sel, gr, rv, ck)
                 for n, gr, rv, ck in zip(EARLY_WEIGHTS, grads, from_sibling, early_kind)]
    early_lands = [lax.empty((3, s.shape[0], s.shape[1] // 4) if ck else (3,) + s.shape[1:], BF16)
                   for s, ck in zip(pair_sums, early_kind)]
    early_plan = _late_reduce_plan(early_kind)
    e_send, e_recv, e_srcs, e_lands, e_started = _split_start(
        "reduce_early_start", pair_sums, early_lands, 3 * len(EARLY_WEIGHTS), early_plan)

    out_g, out_d, out_m, out_v = {}, {}, {}, {}

    def finish(names, kinds, sums, from_chips, tag, halves):
        mine = [_chip_sum("reduce_chip_sum_" + n, chip_sel, ps, ck, fc)
                for n, ps, fc, ck in zip(names, sums, from_chips, kinds)]
        theirs = _pair_swap_halves("reduce_pair_swap_" + tag, mine)
        for n, a, b in zip(names, mine, theirs):
            if n == "w_in":
                flip = lambda t: jnp.swapaxes(t, -1, -2)
                res = _adamw_big("adamw_" + n, c_sel, flip(w[n]), flip(a), flip(b), flip(m[n]), flip(v[n]), "cols")
                out_g[n], out_d[n], out_m[n], out_v[n] = (flip(t) for t in res)
                continue
            out_g[n], out_d[n], out_m[n], out_v[n] = _adamw_big("adamw_" + n, c_sel, w[n], a, b, m[n], v[n], halves)

    r_send, r_recv, r_srcs, r_lands, _ = late_reduce["sems"]
    late_sums, late_from_chips = _split_wait("reduce_late_wait", r_send, r_recv, r_srcs, r_lands, e_started,
                                             reduce_plan)
    finish(LATE_WEIGHTS, late_kind, late_sums, late_from_chips, "late", False)

    small_names = list(SMALL) + ["ffn_conv_w"]
    small_vals = [g[n].reshape(w[n].shape if n != "ffn_conv_w" else (1, 3, D_FF)) for n in small_names]
    last = LATE_WEIGHTS[-1]
    loss, out_v[last] = lax.optimization_barrier((loss, out_v[last]))
    reduced = _allreduce_small(small_vals + [loss])
    loss_all = reduced[-1].reshape(())
    conv_w_grad = lax.dynamic_slice_in_dim(reduced[-2], chip * (D_FF // 4), D_FF // 4, axis=2)
    sg, sd, sm, sv = _adamw_small(
        [w[n] for n in small_names], list(reduced[:len(SMALL)]) + [conv_w_grad],
        [m[n] for n in small_names], [v[n] for n in small_names])
    out_g.update(zip(small_names, sg))
    out_d.update(zip(small_names, sd))
    out_m.update(zip(small_names, sm))
    out_v.update(zip(small_names, sv))

    early_sums, early_from_chips = _split_wait("reduce_early_wait", e_send, e_recv, e_srcs, e_lands, reduced[0],
                                               early_plan)
    finish(EARLY_WEIGHTS, early_kind, early_sums, early_from_chips, "early", True)

    return (loss_all, grad_x, *[out_g[n] for n in WEIGHTS], *[out_d[n] for n in WEIGHTS],
            *[out_m[n] for n in WEIGHTS], *[out_v[n] for n in WEIGHTS])
```

```python
import functools
import math

import jax
import jax.numpy as jnp
from jax import lax
from jax.experimental import pallas as pl
from jax.experimental.pallas import tpu as pltpu

F32 = jnp.float32
BF16 = jnp.bfloat16

D_MODEL = 1024
FOX_WIDTH = 512
HEAD_DIM = 64
N_FOX_HEADS = 8
S5_WIDTH = 512
S5_GROUP_CH = 16
S5_GROUPS = 32
S5_STATE = 64
S5_CH = S5_GROUPS * S5_STATE
N_X_HEADS = 4
X_HEAD_DIM = 256
N_MEM = 256
D_FF = 2816
UF_COLS = 640
EPS = 1e-6
ADAM_LR = 0.001
ADAM_B1 = 0.9
ADAM_B2 = 0.999
ADAM_EPS = 1e-08
ADAM_WD = 0.01
ADAM_STEP = 10

VMEM_LIMIT_BYTES = 56 * 1024 * 1024
MM_BLOCK_BYTES = 6 * 1024 * 1024
MM_VMEM_BYTES = 40 * 1024 * 1024
MM_TILE_MAX = 1536
MESH = pl.DeviceIdType.MESH

FIRST_WEIGHT = "w_in"
MID_WEIGHTS = ("s5_w_glu", "w_out")
EARLY_WEIGHTS = (FIRST_WEIGHT,) + MID_WEIGHTS
LATE_WEIGHTS = ("w_xq", "w_xkv", "w_xo", "w_ffn_up", "w_ffn_down")
BIG = EARLY_WEIGHTS + LATE_WEIGHTS
COL_KIND = ("w_xkv", "w_ffn_up")
SMALL = ("norm_mix", "fox_q_norm", "fox_k_norm", "fox_f_bias", "s5_a_re", "s5_a_im", "s5_log_dt",
         "s5_b_re", "s5_b_im", "s5_c_re", "s5_c_im", "s5_d", "s5_b_glu", "out_norm_fox", "out_norm_s5",
         "norm_cross", "norm_mem", "xq_norm", "xk_norm", "norm_ffn", "ffn_conv_b")
WEIGHTS = ("norm_mix", "w_in", "fox_q_norm", "fox_k_norm", "fox_f_bias", "s5_a_re", "s5_a_im", "s5_log_dt",
           "s5_b_re", "s5_b_im", "s5_c_re", "s5_c_im", "s5_d", "s5_w_glu", "s5_b_glu", "out_norm_fox",
           "out_norm_s5", "w_out", "norm_cross", "norm_mem", "w_xq", "w_xkv", "xq_norm", "xk_norm", "w_xo",
           "norm_ffn", "w_ffn_up", "ffn_conv_w", "ffn_conv_b", "w_ffn_down")


def _params(sem=None):
    return pltpu.CompilerParams(dimension_semantics=sem, vmem_limit_bytes=VMEM_LIMIT_BYTES)


def _pick(n, cands):
    for c in cands:
        if n % c == 0:
            return c
    return n


_DIMS = {"nn": (((1,), (0,)), ((), ())), "nt": (((1,), (1,)), ((), ())), "tn": (((0,), (0,)), ((), ()))}


def _mm(a, b, mode, name, out_dtype=F32, res=None):
    if mode == "nn":
        (m, k), (k2, n) = a.shape, b.shape
    elif mode == "nt":
        (m, k), (n, k2) = a.shape, b.shape
    else:
        (k, m), (k2, n) = a.shape, b.shape
    assert k == k2, (name, a.shape, b.shape)

    has_res = res is not None
    a_size, b_size = a.dtype.itemsize, b.dtype.itemsize
    o_size = jnp.dtype(out_dtype).itemsize + (res.dtype.itemsize if has_res else 0)

    def tiles(dim):
        return [c for c in range(MM_TILE_MAX, 0, -128) if dim % c == 0] or [dim]

    best = None
    for tm in tiles(m):
        for tn in tiles(n):
            a_blk, b_blk = tm * k * a_size, tn * k * b_size
            if max(a_blk, b_blk) > MM_BLOCK_BYTES or 2 * (a_blk + b_blk + tm * tn * o_size) > MM_VMEM_BYTES:
                continue
            for rows_outer in (True, False):
                moved = (m * k * a_size + (m // tm) * n * k * b_size) if rows_outer else \
                        (n * k * b_size + (n // tn) * m * k * a_size)
                key = (moved, -(tm * tn))
                if best is None or key < best[0]:
                    best = (key, tm, tn, rows_outer)
    assert best is not None, (name, a.shape, b.shape)
    _, tm, tn, rows_outer = best
    ij = (lambda g0, g1: (g0, g1)) if rows_outer else (lambda g0, g1: (g1, g0))
    if mode == "tn":
        a_spec = pl.BlockSpec((k, tm), lambda g0, g1: (0, ij(g0, g1)[0]))
    else:
        a_spec = pl.BlockSpec((tm, k), lambda g0, g1: (ij(g0, g1)[0], 0))
    if mode == "nt":
        b_spec = pl.BlockSpec((tn, k), lambda g0, g1: (ij(g0, g1)[1], 0))
    else:
        b_spec = pl.BlockSpec((k, tn), lambda g0, g1: (0, ij(g0, g1)[1]))
    o_spec = pl.BlockSpec((tm, tn), lambda g0, g1: ij(g0, g1))
    grid = (m // tm, n // tn) if rows_outer else (n // tn, m // tm)
    dims = _DIMS[mode]

    def body(*refs):
        a_ref, b_ref = refs[0], refs[1]
        o_ref = refs[-1]
        acc = lax.dot_general(a_ref[...].astype(BF16), b_ref[...].astype(BF16), dims, preferred_element_type=F32)
        if has_res:
            acc = acc + refs[2][...].astype(F32)
        o_ref[...] = acc.astype(o_ref.dtype)

    return pl.pallas_call(
        body, name=name, grid=grid,
        in_specs=[a_spec, b_spec] + ([o_spec] if has_res else []),
        out_specs=o_spec, out_shape=jax.ShapeDtypeStruct((m, n), out_dtype),
        compiler_params=_params(("parallel", "parallel")),
    )(*((a, b, res) if has_res else (a, b)))


def _row_spec(tm, bc, off, step):
    return pl.BlockSpec((tm, bc), lambda i, h: (i, off + step * h))


ROW_TILE_ELEMS = 512 * 1024


def _row_tile(t, rows):
    widest = max(bc for (_, bc, _, _) in rows)
    return _pick(t, (min(t, ROW_TILE_ELEMS // widest), 512, 256, 128, 64, 8))


def _rowwise(fn, rows, pars, outs, name, heads=1):
    t = rows[0][0].shape[0]
    tm = _row_tile(t, rows)
    nr, npar = len(rows), len(pars)

    def body(*refs):
        vals = [r[...].astype(F32) for r in refs[:nr + npar]]
        res = fn(*vals)
        if not isinstance(res, (tuple, list)):
            res = (res,)
        for o_ref, v in zip(refs[nr + npar:], res):
            o_ref[...] = v.astype(o_ref.dtype)

    in_specs = [_row_spec(tm, bc, off, st) for (_, bc, off, st) in rows]
    in_specs += [pl.BlockSpec(p.shape, lambda i, h: (0, 0)) for p in pars]
    out_specs = [_row_spec(tm, bc, 0, st) for (_, bc, st, _) in outs]
    out_shape = [jax.ShapeDtypeStruct((t, c), dt) for (c, _, _, dt) in outs]
    res = pl.pallas_call(
        body, name=name, grid=(t // tm, heads), in_specs=in_specs, out_specs=out_specs, out_shape=out_shape,
        compiler_params=_params(("parallel", "parallel")),
    )(*[r[0] for r in rows], *pars)
    return res[0] if len(res) == 1 else res


def _rowwise_vjp(fn, rows, pars, cts, name, heads=1, adds=None, row_dtypes=None):
    t = rows[0][0].shape[0]
    tm = _row_tile(t, rows)
    nr, npar, nct = len(rows), len(pars), len(cts)
    adds = adds or [None] * nr
    add_list = [a for a in adds if a is not None]
    row_dtypes = row_dtypes or [F32] * nr

    def body(*refs):
        i, h = pl.program_id(0), pl.program_id(1)
        p = 0
        row_v = [r[...].astype(F32) for r in refs[p:p + nr]]; p += nr
        par_v = [r[...].astype(F32) for r in refs[p:p + npar]]; p += npar
        ct_v = [r[...].astype(F32) for r in refs[p:p + nct]]; p += nct
        add_refs = refs[p:p + len(add_list)]; p += len(add_list)
        drow_refs = refs[p:p + nr]; p += nr
        dpar_refs = refs[p:p + npar]

        def wrapped(*a):
            r = fn(*a)
            return tuple(r) if isinstance(r, (tuple, list)) else (r,)

        _, pull = jax.vjp(wrapped, *row_v, *par_v)
        grads = pull(tuple(ct_v))
        ai = 0
        for k in range(nr):
            g = grads[k]
            if adds[k] is not None:
                g = g + add_refs[ai][...].astype(F32)
                ai += 1
            drow_refs[k][...] = g.astype(drow_refs[k].dtype)

        @pl.when((i == 0) & (h == 0))
        def _():
            for r in dpar_refs:
                r[...] = jnp.zeros(r.shape, r.dtype)

        for k in range(npar):
            dpar_refs[k][...] += grads[nr + k]

    in_specs = [_row_spec(tm, bc, off, st) for (_, bc, off, st) in rows]
    in_specs += [pl.BlockSpec(q.shape, lambda i, h: (0, 0)) for q in pars]
    in_specs += [_row_spec(tm, bc, off, st) for (_, bc, off, st) in cts]
    in_specs += [_row_spec(tm, bc, off, st) for (_, bc, off, st) in add_list]
    out_specs = [_row_spec(tm, bc, 0, st) for (_, bc, _, st) in rows]
    out_specs += [pl.BlockSpec(q.shape, lambda i, h: (0, 0)) for q in pars]
    out_shape = [jax.ShapeDtypeStruct((t, bc * (heads if st else 1)), dt) for (_, bc, _, st), dt in zip(rows, row_dtypes)]
    out_shape += [jax.ShapeDtypeStruct(q.shape, F32) for q in pars]
    res = pl.pallas_call(
        body, name=name, grid=(t // tm, heads), in_specs=in_specs, out_specs=out_specs, out_shape=out_shape,
        compiler_params=_params(("arbitrary", "arbitrary")),
    )(*[r[0] for r in rows], *pars, *[c[0] for c in cts], *[a[0] for a in add_list])
    return list(res[:nr]), list(res[nr:])


def _rms(x, g):
    return x * lax.rsqrt(jnp.mean(x * x, axis=-1, keepdims=True) + EPS) * g


def _rms_pair(x, g):
    left = lax.broadcasted_iota(jnp.int32, x.shape, 1) < HEAD_DIM
    x2 = x * x
    ms_a = jnp.sum(jnp.where(left, x2, 0.0), axis=-1, keepdims=True) * (1.0 / HEAD_DIM)
    ms_b = jnp.sum(jnp.where(left, 0.0, x2), axis=-1, keepdims=True) * (1.0 / HEAD_DIM)
    return x * lax.rsqrt(jnp.where(left, ms_a, ms_b) + EPS) * g


def _gelu(x):
    return 0.5 * x * (1.0 + jnp.tanh(math.sqrt(2.0 / math.pi) * (x + 0.044715 * (x * x * x))))


def _s5_act(ys, u, d):
    return _gelu(ys + d * u)


def _s5_gate(yg, z, b, g):
    return _rms(yg * jax.nn.sigmoid(z + b), g)


def _lane_cumsum(x, reverse):
    n = x.shape[-1]
    lane = lax.broadcasted_iota(jnp.int32, x.shape, 1)
    k = 1
    while k < n:
        if reverse:
            x = x + jnp.where(lane < n - k, pltpu.roll(x, n - k, 1), 0.0)
        else:
            x = x + jnp.where(lane >= k, pltpu.roll(x, k, 1), 0.0)
        k *= 2
    return x


def _log_sigmoid(z):
    return jnp.minimum(z, 0.0) - jnp.log(1.0 + jnp.exp(-jnp.abs(z)))


def _forget_fwd(f, bias):
    def body(f_ref, b_ref, c_ref):
        c_ref[...] = _lane_cumsum(_log_sigmoid(f_ref[...] + b_ref[...]), False)

    return pl.pallas_call(body, name="forget_fwd", out_shape=jax.ShapeDtypeStruct(f.shape, F32),
                          compiler_params=_params())(f, bias)


def _forget_bwd(f, bias, dc):
    def body(f_ref, b_ref, dc_ref, df_ref, db_ref):
        dlog = _lane_cumsum(dc_ref[...], True)
        df = dlog * jax.nn.sigmoid(-(f_ref[...] + b_ref[...]))
        df_ref[...] = df
        db_ref[...] = jnp.sum(df, axis=1, keepdims=True)

    return pl.pallas_call(body, name="forget_bwd",
                          out_shape=(jax.ShapeDtypeStruct(f.shape, F32), jax.ShapeDtypeStruct(bias.shape, F32)),
                          compiler_params=_params())(f, bias, dc)


FOX_BLOCK = 256
FOX_KEYS = 256
_NT = _DIMS["nt"]
_TN = _DIMS["tn"]


N_PAIRS = N_FOX_HEADS // 2
V_BLOCK0 = 2 * N_PAIRS


def _left_lanes(shape):
    return lax.broadcasted_iota(jnp.int32, shape, 1) < HEAD_DIM


def _top_rows(shape):
    return lax.broadcasted_iota(jnp.int32, shape, 0) < HEAD_DIM


def _wide(c_tile, n):
    return c_tile if n == 128 else jnp.concatenate([c_tile] * (n // 128), axis=1)


def _fox_fwd(qn, kn, qkv, c_wide, seqs):
    t = qn.shape[0]
    l = t // seqs
    tb = min(FOX_BLOCK, l)
    tk = min(FOX_KEYS, tb)
    ratio = tb // tk
    nb = l // tb
    scale = HEAD_DIM ** -0.5

    def body(q_ref, k_ref, v_ref, ca_ref, cb_ref, o_ref, lse_ref, vt_ref):
        i = pl.program_id(2)
        top = _top_rows((128, tb))

        @pl.when(i == 0)
        def _():
            vt_ref[...] = v_ref[...].T.astype(BF16)

        qt = (q_ref[...].astype(F32) * scale).T.astype(BF16)
        zero = jnp.zeros_like(qt)
        qts = (jnp.where(top, qt, zero), jnp.where(top, zero, qt))
        top_k = _top_rows((128, tk))
        zero_k = jnp.zeros((128, tk), BF16)
        key_pos = lax.broadcasted_iota(jnp.int32, (tk, tb), 0)
        query_pos = lax.broadcasted_iota(jnp.int32, (tk, tb), 1)
        c_refs = (ca_ref, cb_ref)

        def scores(j):
            off = pl.multiple_of(j * tk, tk)
            k2 = k_ref[pl.ds(off, tk), :]
            return tuple(jnp.dot(k2, qts[h], preferred_element_type=F32) - _wide(c_refs[h][pl.ds(off, tk), :], tb)
                         for h in (0, 1))

        def values_times(ps, j):
            vt = vt_ref[:, pl.ds(pl.multiple_of(j * tk, tk), tk)]
            return (jnp.dot(jnp.where(top_k, vt, zero_k), ps[0], preferred_element_type=F32)
                    + jnp.dot(jnp.where(top_k, zero_k, vt), ps[1], preferred_element_type=F32))

        def softmax_step(sts, stats, first_key):
            ps, new, alphas = [], [], []
            for st, (m, s_sum) in zip(sts, stats):
                if first_key is not None:
                    st = jnp.where(key_pos + first_key <= query_pos, st, -jnp.inf)
                m_new = jnp.maximum(m, jnp.max(st, axis=0, keepdims=True))
                alpha = jnp.exp(m - m_new)
                p = jnp.exp(st - m_new)
                new.append((m_new, alpha * s_sum + jnp.sum(p, axis=0, keepdims=True)))
                alphas.append(alpha)
                ps.append(p.astype(BF16))
            return tuple(ps), tuple(new), jnp.where(top, alphas[0], alphas[1])

        def step(j, carry):
            sts, ps_prev, stats, acc = carry
            sts_next = scores(j + 1)
            acc = acc + values_times(ps_prev, jnp.maximum(j - 1, 0))
            ps, stats, alpha = softmax_step(sts, stats, None)
            return sts_next, ps, stats, alpha * acc

        stat = (jnp.full((1, tb), -jnp.inf, F32), jnp.zeros((1, tb), F32))
        no_p = jnp.zeros((tk, tb), BF16)
        below = i * ratio
        sts, ps_prev, stats, acc = lax.fori_loop(
            0, below, step, (scores(0), (no_p, no_p), (stat, stat), jnp.zeros((128, tb), F32)))
        for r in range(ratio):
            sts_next = scores(below + r + 1) if r + 1 < ratio else None
            acc = acc + values_times(ps_prev, jnp.maximum(below + r - 1, 0))
            ps_prev, stats, alpha = softmax_step(sts, stats, r * tk)
            acc = alpha * acc
            sts = sts_next
        acc = acc + values_times(ps_prev, below + ratio - 1)
        (ma, sa), (mb, sb) = stats
        o_ref[...] = (acc / jnp.where(top, sa, sb)).T
        lse_ref[0:1, :] = ma + jnp.log(sa)
        lse_ref[1:2, :] = mb + jnp.log(sb)

    qblk = pl.BlockSpec((tb, 128), lambda b, hp, i: (b * nb + i, hp))
    return pl.pallas_call(
        body, name="fox_fwd", grid=(seqs, N_PAIRS, nb),
        in_specs=[qblk, pl.BlockSpec((l, 128), lambda b, hp, i: (b, hp)),
                  pl.BlockSpec((l, 128), lambda b, hp, i: (b, V_BLOCK0 + hp)),
                  pl.BlockSpec((None, l, 128), lambda b, hp, i: (b * N_FOX_HEADS + 2 * hp, 0, 0)),
                  pl.BlockSpec((None, l, 128), lambda b, hp, i: (b * N_FOX_HEADS + 2 * hp + 1, 0, 0))],
        out_specs=[qblk, pl.BlockSpec((None, 2, tb), lambda b, hp, i: (b * N_PAIRS + hp, 0, i))],
        out_shape=[jax.ShapeDtypeStruct((t, FOX_WIDTH), F32), jax.ShapeDtypeStruct((seqs * N_PAIRS, 2, l), F32)],
        scratch_shapes=[pltpu.VMEM((128, l), BF16)],
        compiler_params=_params(("parallel", "parallel", "arbitrary")),
    )(qn, kn, qkv, c_wide, c_wide)


def _fox_bwd(qn, kn, qkv, c_wide, o, do, lse, seqs):
    t = qn.shape[0]
    l = t // seqs
    tb = min(FOX_BLOCK, l)
    nb = l // tb
    scale = HEAD_DIM ** -0.5
    one_at = (HEAD_DIM, 0)

    def body(q_ref, k_ref, v_ref, ca_ref, cb_ref, o_ref, do_ref, lse_ref, dq_ref, dk_ref, dv_ref, dc_ref, dcq_ref,
             qt_ref, kt_ref, dot_ref, delta_ref, dqa_ref, dqb_ref):
        top_l = _top_rows((128, l))
        top = _top_rows((128, tb))
        left = _left_lanes((tb, 128))
        row_id = lax.broadcasted_iota(jnp.int32, (128, tb), 0)
        lane_id = lax.broadcasted_iota(jnp.int32, (tb, 128), 1)
        zero_t = jnp.zeros((128, tb), BF16)
        zero_l = jnp.zeros((tb, 128), BF16)
        rows = lambda a: (jnp.where(top, a, zero_t), jnp.where(top, zero_t, a))
        lanes = lambda a: (jnp.where(left, a, zero_l), jnp.where(left, zero_l, a))
        with_one_row = lambda pair: tuple(jnp.where(row_id == one_at[h], 1.0, pair[h]).astype(BF16) for h in (0, 1))
        with_one_lane = lambda pair: tuple(jnp.where(lane_id == one_at[h], 1.0, pair[h]).astype(BF16) for h in (0, 1))
        causal = lax.broadcasted_iota(jnp.int32, (tb, tb), 0) <= lax.broadcasted_iota(jnp.int32, (tb, tb), 1)
        c_refs = (ca_ref, cb_ref)
        dq_refs = (dqa_ref, dqb_ref)

        qt_ref[...] = (q_ref[...].astype(F32) * scale).T.astype(BF16)
        kt_ref[...] = k_ref[...].astype(F32).T.astype(BF16)
        do_t = do_ref[...].T
        dot_ref[...] = do_t.astype(BF16)
        prod_t = do_t * o_ref[...].T
        delta_ref[0:1, :] = jnp.sum(jnp.where(top_l, prod_t, 0.0), axis=0, keepdims=True)
        delta_ref[1:2, :] = jnp.sum(jnp.where(top_l, 0.0, prod_t), axis=0, keepdims=True)
        dqa_ref[...] = jnp.zeros(dqa_ref.shape, F32)
        dqb_ref[...] = jnp.zeros(dqb_ref.shape, F32)

        def kv_block(j, _):
            koff = pl.multiple_of(j * tb, tb)
            k2 = k_ref[pl.ds(koff, tb), :]
            v2 = v_ref[pl.ds(koff, tb), :].astype(BF16)
            kts = with_one_row(rows(kt_ref[:, pl.ds(koff, tb)]))
            cw = tuple(_wide(c_refs[h][pl.ds(koff, tb), :], tb) for h in (0, 1))

            def q_block(i, carry, masked):
                dks, dv = list(carry[:2]), carry[2]
                qoff = pl.multiple_of(i * tb, tb)
                qs = lanes((q_ref[pl.ds(qoff, tb), :].astype(F32) * scale).astype(BF16))
                qs_one = with_one_lane(qs)
                dos = lanes(do_ref[pl.ds(qoff, tb), :].astype(BF16))
                qts = rows(qt_ref[:, pl.ds(qoff, tb)])
                dots = rows(dot_ref[:, pl.ds(qoff, tb)])
                for h in (0, 1):
                    st = jnp.dot(k2, qts[h], preferred_element_type=F32) - cw[h]
                    p = jnp.exp(st - lse_ref[h:h + 1, pl.ds(qoff, tb)])
                    if masked:
                        p = jnp.where(causal, p, 0.0)
                    dp = jnp.dot(v2, dots[h], preferred_element_type=F32)
                    dsb = (p * (dp - delta_ref[h:h + 1, pl.ds(qoff, tb)])).astype(BF16)
                    dv = dv + jnp.dot(p.astype(BF16), dos[h], preferred_element_type=F32)
                    dks[h] = dks[h] + jnp.dot(dsb, qs_one[h], preferred_element_type=F32)
                    dq_refs[h][:, pl.ds(qoff, tb)] += jnp.dot(kts[h], dsb, preferred_element_type=F32)
                return dks[0], dks[1], dv

            z = jnp.zeros((tb, 128), F32)
            carry = q_block(j, (z, z, z), True)
            rest = nb - 1 - j
            carry = lax.fori_loop(
                0, rest // 2, lambda n, c: q_block(j + 2 + 2 * n, q_block(j + 1 + 2 * n, c, False), False), carry)
            dka, dkb, dv = lax.cond(rest % 2 == 1, lambda c: q_block(nb - 1, c, False), lambda c: c, carry)
            dk_ref[pl.ds(koff, tb), :] = jnp.where(left, dka, dkb)
            dv_ref[pl.ds(koff, tb), :] = dv
            dc_ref[0:1, pl.ds(koff, tb)] = -dka.T[one_at[0]:one_at[0] + 1, :]
            dc_ref[1:2, pl.ds(koff, tb)] = -dkb.T[one_at[1]:one_at[1] + 1, :]
            return 0

        lax.fori_loop(0, nb, kv_block, 0)
        dq_ref[...] = (jnp.where(top_l, dqa_ref[...], dqb_ref[...]) * scale).T
        dcq_ref[0:1, :] = dqa_ref[one_at[0]:one_at[0] + 1, :]
        dcq_ref[1:2, :] = dqb_ref[one_at[1]:one_at[1] + 1, :]

    blk = pl.BlockSpec((l, 128), lambda b, hp: (b, hp))
    cspec = lambda k: pl.BlockSpec((None, l, 128), lambda b, hp: (b * N_FOX_HEADS + 2 * hp + k, 0, 0))
    rows2 = pl.BlockSpec((None, 2, l), lambda b, hp: (b * N_PAIRS + hp, 0, 0))
    wide = jax.ShapeDtypeStruct((t, FOX_WIDTH), F32)
    pair_rows = jax.ShapeDtypeStruct((seqs * N_PAIRS, 2, l), F32)
    return pl.pallas_call(
        body, name="fox_bwd", grid=(seqs, N_PAIRS),
        in_specs=[blk, blk, pl.BlockSpec((l, 128), lambda b, hp: (b, V_BLOCK0 + hp)), cspec(0), cspec(1), blk, blk, rows2],
        out_specs=[blk, blk, blk, rows2, rows2],
        out_shape=[wide, wide, wide, pair_rows, pair_rows],
        scratch_shapes=[pltpu.VMEM((128, l), BF16), pltpu.VMEM((128, l), BF16), pltpu.VMEM((128, l), BF16),
                        pltpu.VMEM((2, l), F32), pltpu.VMEM((128, l), F32), pltpu.VMEM((128, l), F32)],
        compiler_params=_params(("parallel", "parallel")),
    )(qn, kn, qkv, c_wide, c_wide, o, do, lse)


SCAN_ROWS = 256
SCAN_COLS = 1024


S5_IN = 128
S5_ST = 512
SCAN_CHUNKS = SCAN_COLS // S5_ST
SCAN_SEGS = 8
LANES = 128


def _cmul(ar, ai, br, bi):
    return ar * br - ai * bi, ar * bi + ai * br


def _powers_into(pw_r, pw_i, a_r, a_i, seg):
    pw_r[0:1, :] = a_r
    pw_i[0:1, :] = a_i
    for k in range(1, seg):
        pr, pi = _cmul(pw_r[k - 1:k, :], pw_i[k - 1:k, :], a_r, a_i)
        pw_r[k:k + 1, :] = pr
        pw_i[k:k + 1, :] = pi


def _interleave(dst, src, seg):
    for h in range(src.shape[0]):
        for j in range(seg):
            dst[h, j * SCAN_SEGS:(j + 1) * SCAN_SEGS, :] = src[h, pl.ds(j, SCAN_SEGS, stride=seg), :]


def _deinterleave(dst, src, seg):
    for h in range(src.shape[0]):
        for j in range(seg):
            dst[h, pl.ds(j, SCAN_SEGS, stride=seg), :] = src[h, j * SCAN_SEGS:(j + 1) * SCAN_SEGS, :]


def _interleaved(ref, tmp_a, tmp_b, seg):
    n = ref.shape[1] // LANES
    for h in range(n):
        tmp_a[h] = ref[:, h * LANES:(h + 1) * LANES].astype(F32)
    _interleave(tmp_b, tmp_a, seg)
    return jnp.concatenate([tmp_b[h] for h in range(n)], axis=1)


def _store_deinterleaved(ref, val, tmp_a, tmp_b, seg):
    n = ref.shape[1] // LANES
    for h in range(n):
        tmp_a[h] = val[:, h * LANES:(h + 1) * LANES]
    _deinterleave(tmp_b, tmp_a, seg)
    for h in range(n):
        ref[:, h * LANES:(h + 1) * LANES] = tmp_b[h]


def _segment_scan(b_r, b_i, x_r, x_i, pw_r, pw_i, car_r, car_i, seg, sign, reverse, visit=None):
    nc = b_r.shape[0]
    sub = lax.broadcasted_iota(jnp.int32, (SCAN_SEGS, LANES), 0)
    lanes = lambda c: slice(c * LANES, (c + 1) * LANES)
    rows = lambda j: pl.ds(pl.multiple_of(((seg - 1 - j) if reverse else j) * SCAN_SEGS, SCAN_SEGS), SCAN_SEGS)
    a1 = [(pw_r[0:1, lanes(c)], sign * pw_i[0:1, lanes(c)]) for c in range(nc)]

    def local(j, xs):
        out = []
        for c in range(nc):
            xr, xi = xs[2 * c], xs[2 * c + 1]
            nr = a1[c][0] * xr - a1[c][1] * xi + b_r[c, rows(j), :]
            ni = a1[c][0] * xi + a1[c][1] * xr + b_i[c, rows(j), :]
            x_r[c, rows(j), :] = nr
            x_i[c, rows(j), :] = ni
            out += [nr, ni]
        return tuple(out)

    zero = jnp.zeros((SCAN_SEGS, LANES), F32)
    ends = lax.fori_loop(0, seg, local, (zero,) * (2 * nc))

    if reverse:
        first = sub == SCAN_SEGS - 1
        neighbour = lambda v: pltpu.roll(v, SCAN_SEGS - 1, 0)
        shift = lambda v, d: jnp.where(sub < SCAN_SEGS - d, pltpu.roll(v, SCAN_SEGS - d, 0), 0.0)
    else:
        first = sub == 0
        neighbour = lambda v: pltpu.roll(v, 1, 0)
        shift = lambda v, d: jnp.where(sub >= d, pltpu.roll(v, d, 0), 0.0)
    last = 0 if reverse else SCAN_SEGS - 1
    entries = []
    for c in range(nc):
        er, ei = ends[2 * c], ends[2 * c + 1]
        pr, pi = pw_r[seg - 1:seg, lanes(c)], sign * pw_i[seg - 1:seg, lanes(c)]
        yr = jnp.where(first, car_r[:, lanes(c)], neighbour(er))
        yi = jnp.where(first, car_i[:, lanes(c)], neighbour(ei))
        qr, qi = pr, pi
        for d in (1, 2, 4):
            mr, mi = _cmul(qr, qi, shift(yr, d), shift(yi, d))
            yr, yi = yr + mr, yi + mi
            qr, qi = _cmul(qr, qi, qr, qi)
        lr, li = _cmul(pr, pi, yr, yi)
        car_r[:, lanes(c)] = (er + lr)[last:last + 1, :]
        car_i[:, lanes(c)] = (ei + li)[last:last + 1, :]
        entries += [yr, yi]

    def correct(j, prev):
        out = []
        row_r, row_i = pw_r[pl.ds(j, 1), :], sign * pw_i[pl.ds(j, 1), :]
        for c in range(nc):
            mr, mi = _cmul(row_r[:, lanes(c)], row_i[:, lanes(c)], entries[2 * c], entries[2 * c + 1])
            nr = x_r[c, rows(j), :] + mr
            ni = x_i[c, rows(j), :] + mi
            x_r[c, rows(j), :] = nr
            x_i[c, rows(j), :] = ni
            if visit is not None:
                visit(c, rows(j), prev[2 * c], prev[2 * c + 1])
            out += [nr, ni]
        return tuple(out)

    lax.fori_loop(0, seg, correct, tuple(entries))


def _s5_fwd(uf, bbr, bbi, cr, ci, ar, ai, seqs):
    t = uf.shape[0]
    l = t // seqs
    tl = min(SCAN_ROWS, l)
    nl = l // tl
    seg = tl // SCAN_SEGS
    cb, nq = SCAN_COLS, SCAN_CHUNKS
    nc = cb // LANES
    per = S5_ST // LANES

    def body(u_ref, bbr_ref, bbi_ref, cr_ref, ci_ref, ar_ref, ai_ref, xr_ref, xi_ref, ys_ref,
             car_r, car_i, pw_r, pw_i, b_r, b_i, x_r, x_i, tmp_a, tmp_b):
        @pl.when(pl.program_id(2) == 0)
        def _():
            car_r[...] = jnp.zeros(car_r.shape, F32)
            car_i[...] = jnp.zeros(car_i.shape, F32)
            _powers_into(pw_r, pw_i, ar_ref[...], ai_ref[...], seg)

        u = _interleaved(u_ref, tmp_a, tmp_b, seg).astype(BF16)
        for q in range(nq):
            uq = u[:, q * S5_IN:(q + 1) * S5_IN]
            br = jnp.dot(uq, bbr_ref[q], preferred_element_type=F32)
            bi = jnp.dot(uq, bbi_ref[q], preferred_element_type=F32)
            for s in range(per):
                b_r[q * per + s] = br[:, s * LANES:(s + 1) * LANES]
                b_i[q * per + s] = bi[:, s * LANES:(s + 1) * LANES]
        _segment_scan(b_r, b_i, x_r, x_i, pw_r, pw_i, car_r, car_i, seg, 1.0, False)
        for c in range(nc):
            xr_ref[:, c * LANES:(c + 1) * LANES] = x_r[c]
            xi_ref[:, c * LANES:(c + 1) * LANES] = x_i[c]
        ys = []
        for q in range(nq):
            xq_r = xr_ref[:, q * S5_ST:(q + 1) * S5_ST].astype(BF16)
            xq_i = xi_ref[:, q * S5_ST:(q + 1) * S5_ST].astype(BF16)
            ys.append(jnp.dot(xq_r, cr_ref[q], preferred_element_type=F32)
                      + jnp.dot(xq_i, ci_ref[q], preferred_element_type=F32))
        _store_deinterleaved(ys_ref, jnp.concatenate(ys, axis=1), tmp_a, tmp_b, seg)

    rows = lambda w: pl.BlockSpec((tl, w), lambda s, j, r: (s * nl + r, j))
    chunk = lambda a: pl.BlockSpec((nq,) + a.shape[1:], lambda s, j, r: (j, 0, 0))
    par = pl.BlockSpec((1, cb), lambda s, j, r: (0, j))
    return pl.pallas_call(
        body, name="s5_fwd", grid=(seqs, S5_CH // cb, nl),
        in_specs=[rows(nq * S5_IN), chunk(bbr), chunk(bbi), chunk(cr), chunk(ci), par, par],
        out_specs=[rows(cb), rows(cb), rows(nq * S5_IN)],
        out_shape=[jax.ShapeDtypeStruct((t, S5_CH), F32)] * 2 + [jax.ShapeDtypeStruct((t, S5_WIDTH), F32)],
        scratch_shapes=[pltpu.VMEM((1, cb), F32), pltpu.VMEM((1, cb), F32), pltpu.VMEM((seg, cb), F32),
                        pltpu.VMEM((seg, cb), F32)] + [pltpu.VMEM((nc, tl, LANES), F32)] * 4
        + [pltpu.VMEM((nq * S5_IN // LANES, tl, LANES), F32)] * 2,
        compiler_params=_params(("parallel", "parallel", "arbitrary")),
    )(uf, bbr, bbi, cr, ci, ar, ai)


def _s5_bwd(dys, uf, xr, xi, bbr, bbi, cr, ci, ar, ai, seqs):
    t = dys.shape[0]
    l = t // seqs
    tl = min(SCAN_ROWS, l)
    nl = l // tl
    seg = tl // SCAN_SEGS
    cb, nq = SCAN_COLS, SCAN_CHUNKS
    nc = cb // LANES
    per = S5_ST // LANES

    def body(dy_ref, u_ref, xr_ref, xi_ref, bbr_ref, bbi_ref, cr_ref, ci_ref, ar_ref, ai_ref,
             du_ref, dbbr_ref, dbbi_ref, dcr_ref, dci_ref, dar_ref, dai_ref,
             car_r, car_i, pw_r, pw_i, g_r, g_i, lam_r, lam_i, x_r, x_i, acc_r, acc_i, tmp_a, tmp_b):
        @pl.when(pl.program_id(2) == 0)
        def _():
            car_r[...] = jnp.zeros(car_r.shape, F32)
            car_i[...] = jnp.zeros(car_i.shape, F32)
            _powers_into(pw_r, pw_i, ar_ref[...], ai_ref[...], seg)
            for acc_ref in (dbbr_ref, dbbi_ref, dcr_ref, dci_ref, dar_ref, dai_ref):
                acc_ref[...] = jnp.zeros(acc_ref.shape, F32)

        dy = _interleaved(dy_ref, tmp_a, tmp_b, seg).astype(BF16)
        for q in range(nq):
            dyq = dy[:, q * S5_IN:(q + 1) * S5_IN]
            gr = lax.dot_general(dyq, cr_ref[q], _NT, preferred_element_type=F32)
            gi = lax.dot_general(dyq, ci_ref[q], _NT, preferred_element_type=F32)
            for s in range(per):
                g_r[q * per + s] = gr[:, s * LANES:(s + 1) * LANES]
                g_i[q * per + s] = gi[:, s * LANES:(s + 1) * LANES]
        for c in range(nc):
            x_r[c] = xr_ref[:, c * LANES:(c + 1) * LANES]
            x_i[c] = xi_ref[:, c * LANES:(c + 1) * LANES]
        acc_r[...] = jnp.zeros(acc_r.shape, F32)
        acc_i[...] = jnp.zeros(acc_i.shape, F32)

        def visit(c, rws, lr, li):
            xr_t, xi_t = x_r[c, rws, :], x_i[c, rws, :]
            acc_r[c] += lr * xr_t + li * xi_t
            acc_i[c] += li * xr_t - lr * xi_t

        _segment_scan(g_r, g_i, lam_r, lam_i, pw_r, pw_i, car_r, car_i, seg, -1.0, True, visit)
        for c in range(nc):
            dar_ref[:, c * LANES:(c + 1) * LANES] += jnp.sum(acc_r[c], axis=0, keepdims=True)
            dai_ref[:, c * LANES:(c + 1) * LANES] += jnp.sum(acc_i[c], axis=0, keepdims=True)
        u = _interleaved(u_ref, tmp_a, tmp_b, seg).astype(BF16)
        du = []
        for q in range(nq):
            st = slice(q * S5_ST, (q + 1) * S5_ST)
            io = slice(q * S5_IN, (q + 1) * S5_IN)
            lq_r = jnp.concatenate([lam_r[q * per + s] for s in range(per)], axis=1).astype(BF16)
            lq_i = jnp.concatenate([lam_i[q * per + s] for s in range(per)], axis=1).astype(BF16)
            du.append(lax.dot_general(lq_r, bbr_ref[q], _NT, preferred_element_type=F32)
                      + lax.dot_general(lq_i, bbi_ref[q], _NT, preferred_element_type=F32))
            dbbr_ref[q] += lax.dot_general(u[:, io], lq_r, _TN, preferred_element_type=F32)
            dbbi_ref[q] += lax.dot_general(u[:, io], lq_i, _TN, preferred_element_type=F32)
            dcr_ref[q] += lax.dot_general(xr_ref[:, st].astype(BF16), dy[:, io], _TN, preferred_element_type=F32)
            dci_ref[q] += lax.dot_general(xi_ref[:, st].astype(BF16), dy[:, io], _TN, preferred_element_type=F32)
        _store_deinterleaved(du_ref, jnp.concatenate(du, axis=1), tmp_a, tmp_b, seg)

    rows = lambda w: pl.BlockSpec((tl, w), lambda s, j, r: (s * nl + nl - 1 - r, j))
    chunk = lambda a: pl.BlockSpec((nq,) + a.shape[1:], lambda s, j, r: (j, 0, 0))
    acc = lambda a: pl.BlockSpec((None, nq) + a.shape[1:], lambda s, j, r: (s, j, 0, 0))
    par = pl.BlockSpec((1, cb), lambda s, j, r: (0, j))
    par_acc = pl.BlockSpec((None, 1, cb), lambda s, j, r: (s, 0, j))
    per_seq = lambda a: jax.ShapeDtypeStruct((seqs,) + a.shape, F32)
    return pl.pallas_call(
        body, name="s5_bwd", grid=(seqs, S5_CH // cb, nl),
        in_specs=[rows(nq * S5_IN), rows(nq * S5_IN), rows(cb), rows(cb), chunk(bbr), chunk(bbi), chunk(cr), chunk(ci),
                  par, par],
        out_specs=[rows(nq * S5_IN), acc(bbr), acc(bbi), acc(cr), acc(ci), par_acc, par_acc],
        out_shape=[jax.ShapeDtypeStruct((t, S5_WIDTH), F32), per_seq(bbr), per_seq(bbi), per_seq(cr), per_seq(ci),
                   jax.ShapeDtypeStruct((seqs, 1, S5_CH), F32), jax.ShapeDtypeStruct((seqs, 1, S5_CH), F32)],
        scratch_shapes=[pltpu.VMEM((1, cb), F32), pltpu.VMEM((1, cb), F32), pltpu.VMEM((seg, cb), F32),
                        pltpu.VMEM((seg, cb), F32)] + [pltpu.VMEM((nc, tl, LANES), F32)] * 6
        + [pltpu.VMEM((nc, SCAN_SEGS, LANES), F32)] * 2 + [pltpu.VMEM((nq * S5_IN // LANES, tl, LANES), F32)] * 2,
        compiler_params=_params(("parallel", "parallel", "arbitrary")),
    )(dys, uf, xr, xi, bbr, bbi, cr, ci, ar, ai)


XATT_BLOCK = 2048


def _xatt_probs(qv, kv):
    s = lax.dot_general(qv, kv, _NT, preferred_element_type=F32) * (X_HEAD_DIM ** -0.5)
    e = jnp.exp(s - jnp.max(s, axis=-1, keepdims=True))
    return e / jnp.sum(e, axis=-1, keepdims=True)


def _xatt_fwd(q, k, kv, seqs):
    t = q.shape[0]
    tq = min(XATT_BLOCK, t // seqs)
    nq = t // seqs // tq

    def body(q_ref, k_ref, v_ref, o_ref):
        p = _xatt_probs(q_ref[...], k_ref[...])
        o_ref[...] = jnp.dot(p.astype(BF16), v_ref[...].astype(BF16), preferred_element_type=F32).astype(o_ref.dtype)

    qs = pl.BlockSpec((tq, X_HEAD_DIM), lambda b, h, i: (b * nq + i, h))
    return pl.pallas_call(
        body, name="xatt_fwd", grid=(seqs, N_X_HEADS, nq),
        in_specs=[qs, pl.BlockSpec((N_MEM, X_HEAD_DIM), lambda b, h, i: (b, h)),
                  pl.BlockSpec((N_MEM, X_HEAD_DIM), lambda b, h, i: (b, N_X_HEADS + h))],
        out_specs=qs, out_shape=jax.ShapeDtypeStruct(q.shape, BF16),
        compiler_params=_params(("parallel", "parallel", "parallel")),
    )(q, k, kv)


def _xatt_bwd(q, k, kv, do, seqs):
    t = q.shape[0]
    tq = min(XATT_BLOCK, t // seqs)
    nq = t // seqs // tq
    scale = X_HEAD_DIM ** -0.5

    def body(q_ref, k_ref, v_ref, do_ref, dq_ref, dk_ref, dv_ref):
        @pl.when(pl.program_id(2) == 0)
        def _():
            dk_ref[...] = jnp.zeros(dk_ref.shape, F32)
            dv_ref[...] = jnp.zeros(dv_ref.shape, F32)

        qv, kk = q_ref[...], k_ref[...]
        p = _xatt_probs(qv, kk)
        dob = do_ref[...].astype(BF16)
        dp = lax.dot_general(dob, v_ref[...].astype(BF16), _NT, preferred_element_type=F32)
        ds = p * (dp - jnp.sum(dp * p, axis=-1, keepdims=True))
        dsb = ds.astype(BF16)
        dq_ref[...] = jnp.dot(dsb, kk, preferred_element_type=F32) * scale
        dk_ref[...] += lax.dot_general(dsb, qv, _TN, preferred_element_type=F32) * scale
        dv_ref[...] += lax.dot_general(p.astype(BF16), dob, _TN, preferred_element_type=F32)

    qs = pl.BlockSpec((tq, X_HEAD_DIM), lambda b, h, i: (b * nq + i, h))
    ks = pl.BlockSpec((N_MEM, X_HEAD_DIM), lambda b, h, i: (b, h))
    return pl.pallas_call(
        body, name="xatt_bwd", grid=(seqs, N_X_HEADS, nq),
        in_specs=[qs, ks, pl.BlockSpec((N_MEM, X_HEAD_DIM), lambda b, h, i: (b, N_X_HEADS + h)), qs],
        out_specs=[qs, ks, ks],
        out_shape=[jax.ShapeDtypeStruct(q.shape, F32), jax.ShapeDtypeStruct(k.shape, F32),
                   jax.ShapeDtypeStruct(k.shape, F32)],
        compiler_params=_params(("parallel", "parallel", "arbitrary")),
    )(q, k, kv, do)


CONV_COLS = 256


def _shift_down(x, k, row):
    return jnp.where(row >= k, pltpu.roll(x, k, 0), 0.0)


def _shift_up(x, k, row):
    n = x.shape[0]
    return jnp.where(row < n - k, pltpu.roll(x, n - k, 0), 0.0)


def _conv_pre(g, w, b, row):
    return b + w[0:1, :] * _shift_down(g, 2, row) + w[1:2, :] * _shift_down(g, 1, row) + w[2:3, :] * g


def _convgate_fwd(gu, w, b, seqs):
    t = gu.shape[0]
    l = t // seqs
    nc = D_FF // CONV_COLS

    def body(g_ref, u_ref, w_ref, b_ref, o_ref):
        g = g_ref[...].astype(F32)
        row = lax.broadcasted_iota(jnp.int32, g.shape, 0)
        pre = _conv_pre(g, w_ref[...], b_ref[...], row)
        o_ref[...] = (pre * jax.nn.sigmoid(pre) * u_ref[...].astype(F32)).astype(o_ref.dtype)

    return pl.pallas_call(
        body, name="convgate_fwd", grid=(seqs, nc),
        in_specs=[pl.BlockSpec((l, CONV_COLS), lambda s, j: (s, j)), pl.BlockSpec((l, CONV_COLS), lambda s, j: (s, nc + j)),
                  pl.BlockSpec((3, CONV_COLS), lambda s, j: (0, j)), pl.BlockSpec((1, CONV_COLS), lambda s, j: (0, j))],
        out_specs=pl.BlockSpec((l, CONV_COLS), lambda s, j: (s, j)),
        out_shape=jax.ShapeDtypeStruct((t, D_FF), BF16),
        compiler_params=_params(("parallel", "parallel")),
    )(gu, gu, w, b)


def _convgate_bwd(gu, w, b, dact, seqs):
    t = gu.shape[0]
    l = t // seqs
    nc = D_FF // CONV_COLS
    steps = nc * seqs

    def body(g_ref, u_ref, w_ref, b_ref, da_ref, dgu_ref, dw_ref, db_ref, stage, sems):
        j, s = pl.program_id(0), pl.program_id(1)
        n = j * seqs + s
        slot = n % 2

        def copies(slot_, j_, s_):
            rows = pl.ds(pl.multiple_of(s_ * l, 16), l)
            return [pltpu.make_async_copy(
                stage.at[slot_, half],
                dgu_ref.at[rows, pl.ds(pl.multiple_of((half * nc + j_) * CONV_COLS, 128), CONV_COLS)],
                sems.at[slot_, half]) for half in (0, 1)]

        @pl.when(s == 0)
        def _():
            dw_ref[...] = jnp.zeros(dw_ref.shape, F32)
            db_ref[...] = jnp.zeros(db_ref.shape, F32)

        @pl.when(n >= 2)
        def _():
            for cp in copies(slot, j, s):
                cp.wait()

        g, wv, da = g_ref[...].astype(F32), w_ref[...], da_ref[...].astype(F32)
        row = lax.broadcasted_iota(jnp.int32, g.shape, 0)
        g1, g2 = _shift_down(g, 1, row), _shift_down(g, 2, row)
        pre = b_ref[...] + wv[0:1, :] * g2 + wv[1:2, :] * g1 + wv[2:3, :] * g
        sg = jax.nn.sigmoid(pre)
        silu = pre * sg
        stage[slot, 1] = (da * silu).astype(stage.dtype)
        dpre = da * u_ref[...].astype(F32) * (sg * (1.0 + pre * (1.0 - sg)))
        dg = wv[2:3, :] * dpre + wv[1:2, :] * _shift_up(dpre, 1, row) + wv[0:1, :] * _shift_up(dpre, 2, row)
        stage[slot, 0] = dg.astype(stage.dtype)
        for cp in copies(slot, j, s):
            cp.start()
        dw_ref[0:1, :] += jnp.sum(dpre * g2, axis=0, keepdims=True)
        dw_ref[1:2, :] += jnp.sum(dpre * g1, axis=0, keepdims=True)
        dw_ref[2:3, :] += jnp.sum(dpre * g, axis=0, keepdims=True)
        db_ref[...] += jnp.sum(dpre, axis=0, keepdims=True)

        @pl.when(n == steps - 1)
        def _():
            for cp in copies(slot, j, s) + (copies(1 - slot, j, s) if steps > 1 else []):
                cp.wait()

    blk = lambda off: pl.BlockSpec((l, CONV_COLS), lambda j, s: (s, off + j))
    return pl.pallas_call(
        body, name="convgate_bwd", grid=(nc, seqs),
        in_specs=[blk(0), blk(nc), pl.BlockSpec((3, CONV_COLS), lambda j, s: (0, j)),
                  pl.BlockSpec((1, CONV_COLS), lambda j, s: (0, j)), blk(0)],
        out_specs=[ANY, pl.BlockSpec((3, CONV_COLS), lambda j, s: (0, j)),
                   pl.BlockSpec((1, CONV_COLS), lambda j, s: (0, j))],
        out_shape=[jax.ShapeDtypeStruct((t, 2 * D_FF), BF16), jax.ShapeDtypeStruct((3, D_FF), F32),
                   jax.ShapeDtypeStruct((1, D_FF), F32)],
        scratch_shapes=[pltpu.VMEM((2, 2, l, CONV_COLS), BF16), pltpu.SemaphoreType.DMA((2, 2))],
        compiler_params=_params(("arbitrary", "arbitrary")),
    )(gu, gu, w, b, dact)


def _loss_head(h, target):
    t, d = h.shape
    tm = _pick(t, (256, 128, 8))

    def body(h_ref, t_ref, dh_ref, dhb_ref, loss_ref):
        @pl.when(pl.program_id(0) == 0)
        def _():
            loss_ref[...] = jnp.zeros(loss_ref.shape, F32)

        e = h_ref[...] - t_ref[...]
        dh = e * (1.0 / d)
        dh_ref[...] = dh
        dhb_ref[...] = dh.astype(BF16)
        loss_ref[...] += (0.5 / d) * jnp.sum(jnp.sum(e * e, axis=1, keepdims=True), axis=0, keepdims=True)

    blk = pl.BlockSpec((tm, d), lambda i: (i, 0))
    return pl.pallas_call(
        body, name="loss_head", grid=(t // tm,), in_specs=[blk, blk],
        out_specs=[blk, blk, pl.BlockSpec((1, 1), lambda i: (0, 0))],
        out_shape=[jax.ShapeDtypeStruct((t, d), F32), jax.ShapeDtypeStruct((t, d), BF16),
                   jax.ShapeDtypeStruct((1, 1), F32)],
        compiler_params=_params(("arbitrary",)),
    )(h, target)


def _s5_discretise(a_re, a_im, log_dt, b_re, b_im):
    dt = jnp.exp(log_dt)[:, None]
    mag = jnp.exp(a_re * dt)
    lb_r = mag * jnp.cos(a_im * dt)
    lb_i = mag * jnp.sin(a_im * dt)
    den = a_re * a_re + a_im * a_im
    nr = lb_r - 1.0
    coef_r = (nr * a_re + lb_i * a_im) / den
    coef_i = (lb_i * a_re - nr * a_im) / den
    bb_r = coef_r[:, :, None] * b_re - coef_i[:, :, None] * b_im
    bb_i = coef_r[:, :, None] * b_im + coef_i[:, :, None] * b_re
    return lb_r, lb_i, bb_r, bb_i


S5_CHUNKS = 4
S5_PER = S5_GROUPS // S5_CHUNKS


def _blockdiag_in(bb):
    eye = jnp.eye(S5_PER, dtype=bb.dtype)
    return jnp.einsum("jgpc,gh->jgchp", bb.reshape(S5_CHUNKS, S5_PER, S5_STATE, S5_GROUP_CH), eye).reshape(
        S5_CHUNKS, S5_PER * S5_GROUP_CH, S5_PER * S5_STATE)


def _blockdiag_in_grad(d):
    eye = jnp.eye(S5_PER, dtype=d.dtype)
    return jnp.einsum("jgchp,gh->jgpc", d.reshape(S5_CHUNKS, S5_PER, S5_GROUP_CH, S5_PER, S5_STATE), eye).reshape(
        S5_GROUPS, S5_STATE, S5_GROUP_CH)


def _blockdiag_out(c):
    eye = jnp.eye(S5_PER, dtype=c.dtype)
    return jnp.einsum("jgcp,gh->jgphc", c.reshape(S5_CHUNKS, S5_PER, S5_GROUP_CH, S5_STATE), eye).reshape(
        S5_CHUNKS, S5_PER * S5_STATE, S5_PER * S5_GROUP_CH)


def _blockdiag_out_grad(d):
    eye = jnp.eye(S5_PER, dtype=d.dtype)
    return jnp.einsum("jgphc,gh->jgcp", d.reshape(S5_CHUNKS, S5_PER, S5_STATE, S5_PER, S5_GROUP_CH), eye).reshape(
        S5_GROUPS, S5_GROUP_CH, S5_STATE)


def _local_step(x3, mem3, target3, p, wb, late_weights=None, early_grads=None):
    seqs, l, d = x3.shape
    t = seqs * l
    x = x3.reshape(t, d)
    mem = mem3.reshape(seqs * N_MEM, d)
    target = target3.reshape(t, d)
    full = lambda a: (a, a.shape[1], 0, 0)

    s5_in = (p["s5_a_re"], p["s5_a_im"], p["s5_log_dt"], p["s5_b_re"], p["s5_b_im"])
    (lb_r, lb_i, bb_r, bb_i), s5_pull = jax.vjp(_s5_discretise, *s5_in)
    ar, ai = lb_r.reshape(1, S5_CH), lb_i.reshape(1, S5_CH)
    bbr_d, bbi_d = _blockdiag_in(bb_r).astype(BF16), _blockdiag_in(bb_i).astype(BF16)
    cr_d, ci_d = _blockdiag_out(p["s5_c_re"]).astype(BF16), (-_blockdiag_out(p["s5_c_im"])).astype(BF16)
    d_row = p["s5_d"].reshape(1, S5_WIDTH)

    w_in = wb["w_in"]
    w_qkv = w_in[:, :3 * FOX_WIDTH]
    w_uf = jnp.concatenate(
        [w_in[:, 3 * FOX_WIDTH + N_FOX_HEADS:], w_in[:, 3 * FOX_WIDTH:3 * FOX_WIDTH + N_FOX_HEADS],
         jnp.zeros((d, UF_COLS - S5_WIDTH - N_FOX_HEADS), w_in.dtype)], axis=1)

    hn1 = _rowwise(_rms, [full(x)], [p["norm_mix"]], [(d, d, 0, BF16)], "norm_mix_fwd")
    qkv = _mm(hn1, w_qkv, "nn", "in_qkv")
    uf = _mm(hn1, w_uf, "nn", "in_uf")

    bh = seqs * N_FOX_HEADS
    q_pair = (qkv, 128, 0, 1)
    k_pair = (qkv, 128, N_PAIRS, 1)
    gq2, gk2 = jnp.tile(p["fox_q_norm"], (1, 2)), jnp.tile(p["fox_k_norm"], (1, 2))
    pair_out = [(FOX_WIDTH, 128, 1, BF16)]
    qn = _rowwise(_rms_pair, [q_pair], [gq2], pair_out, "fox_qnorm_fwd", heads=N_PAIRS)
    kn = _rowwise(_rms_pair, [k_pair], [gk2], pair_out, "fox_knorm_fwd", heads=N_PAIRS)

    f_rows = uf[:, S5_WIDTH:S5_WIDTH + N_FOX_HEADS].reshape(seqs, l, N_FOX_HEADS).transpose(0, 2, 1).reshape(bh, l)
    f_bias = jnp.tile(p["fox_f_bias"].reshape(N_FOX_HEADS, 1), (seqs, 1))
    c_wide = jnp.broadcast_to(_forget_fwd(f_rows, f_bias)[:, :, None], (bh, l, 128))
    fox, lse = _fox_fwd(qn, kn, qkv, c_wide, seqs)

    xr, xi, ys = _s5_fwd(uf, bbr_d, bbi_d, cr_d, ci_d, ar, ai, seqs)
    u_blk = (uf, S5_WIDTH, 0, 0)
    yg = _rowwise(_s5_act, [full(ys), u_blk], [d_row], [(S5_WIDTH, S5_WIDTH, 0, F32)], "s5_act_fwd")
    if late_weights is not None:
        wb = dict(wb, **late_weights("mid", yg))
    z = _mm(yg, wb["s5_w_glu"], "nn", "s5_glu")
    y2n = _rowwise(_s5_gate, [full(yg), full(z)], [p["s5_b_glu"], p["out_norm_s5"]],
                   [(S5_WIDTH, S5_WIDTH, 0, BF16)], "s5_gate_fwd")
    foxn = _rowwise(_rms, [full(fox)], [p["out_norm_fox"]], [(FOX_WIDTH, FOX_WIDTH, 0, BF16)], "fox_outnorm_fwd")
    mixed = jnp.concatenate([foxn, y2n], axis=1)
    h1 = _mm(mixed, wb["w_out"], "nn", "mix_out", res=x)
    if late_weights is not None:
        wb = dict(wb, **late_weights("late", h1))

    hn2 = _rowwise(_rms, [full(h1)], [p["norm_cross"]], [(d, d, 0, BF16)], "norm_cross_fwd")
    mn = _rowwise(_rms, [full(mem)], [p["norm_mem"]], [(d, d, 0, BF16)], "norm_mem_fwd")
    xq_raw = _mm(hn2, wb["w_xq"], "nn", "x_q")
    kv = _mm(mn, wb["w_xkv"], "nn", "x_kv")
    xh = lambda a: (a, X_HEAD_DIM, 0, 1)
    xqn = _rowwise(_rms, [xh(xq_raw)], [p["xq_norm"]], [(d, X_HEAD_DIM, 1, BF16)], "x_qnorm_fwd", heads=N_X_HEADS)
    xkn = _rowwise(_rms, [xh(kv)], [p["xk_norm"]], [(d, X_HEAD_DIM, 1, BF16)], "x_knorm_fwd", heads=N_X_HEADS)
    xo = _xatt_fwd(xqn, xkn, kv, seqs)
    h2 = _mm(xo, wb["w_xo"], "nn", "x_out", res=h1)

    hn3 = _rowwise(_rms, [full(h2)], [p["norm_ffn"]], [(d, d, 0, BF16)], "norm_ffn_fwd")
    gu = _mm(hn3, wb["w_ffn_up"], "nn", "ffn_up", out_dtype=BF16)
    act = _convgate_fwd(gu, p["ffn_conv_w"], p["ffn_conv_b"], seqs)
    h3 = _mm(act, wb["w_ffn_down"], "nn", "ffn_down", res=h2)
    dh3, dh3_b, loss = _loss_head(h3, target)

    g = {}
    dact = _mm(dh3_b, wb["w_ffn_down"], "nt", "ffn_down_dx", out_dtype=BF16)
    late_dt = BF16 if early_grads is not None else F32
    g["w_ffn_down"] = _mm(act, dh3_b, "tn", "ffn_down_dw", out_dtype=late_dt)
    dgu, g["ffn_conv_w"], g["ffn_conv_b"] = _convgate_bwd(gu, p["ffn_conv_w"], p["ffn_conv_b"], dact, seqs)
    dhn3 = _mm(dgu, wb["w_ffn_up"], "nt", "ffn_up_dx")
    g["w_ffn_up"] = _mm(hn3, dgu, "tn", "ffn_up_dw", out_dtype=late_dt)
    (dh2,), (g["norm_ffn"],) = _rowwise_vjp(_rms, [full(h2)], [p["norm_ffn"]], [full(dhn3)], "norm_ffn_bwd",
                                            adds=[full(dh3)])

    dxo = _mm(dh2, wb["w_xo"], "nt", "x_out_dx")
    g["w_xo"] = _mm(xo, dh2, "tn", "x_out_dw", out_dtype=late_dt)
    dxqn, dxkn, dxv = _xatt_bwd(xqn, xkn, kv, dxo, seqs)
    (dxq_raw,), (g["xq_norm"],) = _rowwise_vjp(_rms, [xh(xq_raw)], [p["xq_norm"]], [xh(dxqn)], "x_qnorm_bwd",
                                               heads=N_X_HEADS, row_dtypes=[BF16])
    (dxk_raw,), (g["xk_norm"],) = _rowwise_vjp(_rms, [xh(kv)], [p["xk_norm"]], [xh(dxkn)], "x_knorm_bwd",
                                               heads=N_X_HEADS, row_dtypes=[BF16])
    dkv = jnp.concatenate([dxk_raw, dxv.astype(BF16)], axis=1)
    dhn2 = _mm(dxq_raw, wb["w_xq"], "nt", "x_q_dx")
    g["w_xq"] = _mm(hn2, dxq_raw, "tn", "x_q_dw", out_dtype=late_dt)
    dmn = _mm(dkv, wb["w_xkv"], "nt", "x_kv_dx")
    g["w_xkv"] = _mm(mn, dkv, "tn", "x_kv_dw", out_dtype=late_dt)
    norm_cross = p["norm_cross"]
    if early_grads is not None:
        norm_cross = norm_cross + early_grads({n: g[n] for n in LATE_WEIGHTS})
    (dh1,), (g["norm_cross"],) = _rowwise_vjp(_rms, [full(h1)], [norm_cross], [full(dhn2)], "norm_cross_bwd",
                                              adds=[full(dh2)])
    _, (g["norm_mem"],) = _rowwise_vjp(_rms, [full(mem)], [p["norm_mem"]], [full(dmn)], "norm_mem_bwd",
                                       row_dtypes=[BF16])

    dmixed = _mm(dh1, wb["w_out"], "nt", "mix_out_dx")
    g["w_out"] = _mm(mixed, dh1, "tn", "mix_out_dw")
    (dfox,), (g["out_norm_fox"],) = _rowwise_vjp(_rms, [full(fox)], [p["out_norm_fox"]],
                                                 [(dmixed, FOX_WIDTH, 0, 0)], "fox_outnorm_bwd")
    (dyg_a, dz), (g["s5_b_glu"], g["out_norm_s5"]) = _rowwise_vjp(
        _s5_gate, [full(yg), full(z)], [p["s5_b_glu"], p["out_norm_s5"]], [(dmixed, S5_WIDTH, 1, 0)], "s5_gate_bwd",
        row_dtypes=[F32, BF16])
    dyg = _mm(dz, wb["s5_w_glu"], "nt", "s5_glu_dx", res=dyg_a)
    g["s5_w_glu"] = _mm(yg, dz, "tn", "s5_glu_dw")
    (dys, du_a), (dd_row,) = _rowwise_vjp(_s5_act, [full(ys), u_blk], [d_row], [full(dyg)], "s5_act_bwd",
                                          row_dtypes=[BF16, F32])
    g["s5_d"] = dd_row
    du_b, dbbr_d, dbbi_d, dcr_d, dci_d, dar, dai = _s5_bwd(dys, uf, xr, xi, bbr_d, bbi_d, cr_d, ci_d, ar, ai, seqs)
    dbbr_d, dbbi_d, dcr_d, dci_d = (jnp.sum(a, axis=0) for a in (dbbr_d, dbbi_d, dcr_d, dci_d))
    d_lb_r = jnp.sum(dar, axis=0).reshape(S5_GROUPS, S5_STATE)
    d_lb_i = jnp.sum(dai, axis=0).reshape(S5_GROUPS, S5_STATE)
    g["s5_a_re"], g["s5_a_im"], g["s5_log_dt"], g["s5_b_re"], g["s5_b_im"] = s5_pull(
        (d_lb_r, d_lb_i, _blockdiag_in_grad(dbbr_d), _blockdiag_in_grad(dbbi_d)))
    g["s5_c_re"] = _blockdiag_out_grad(dcr_d)
    g["s5_c_im"] = -_blockdiag_out_grad(dci_d)

    dqn, dkn, dv, dc, dcq = _fox_bwd(qn, kn, qkv, c_wide, fox, dfox, lse, seqs)
    pair = lambda a: (a, 128, 0, 1)
    (dq_raw,), (dgq2,) = _rowwise_vjp(_rms_pair, [q_pair], [gq2], [pair(dqn)], "fox_qnorm_bwd", heads=N_PAIRS,
                                      row_dtypes=[BF16])
    (dk_raw,), (dgk2,) = _rowwise_vjp(_rms_pair, [k_pair], [gk2], [pair(dkn)], "fox_knorm_bwd", heads=N_PAIRS,
                                      row_dtypes=[BF16])
    g["fox_q_norm"] = dgq2[:, :HEAD_DIM] + dgq2[:, HEAD_DIM:]
    g["fox_k_norm"] = dgk2[:, :HEAD_DIM] + dgk2[:, HEAD_DIM:]
    df_rows, dfb = _forget_bwd(f_rows, f_bias, (dc + dcq).reshape(bh, l))
    g["fox_f_bias"] = jnp.sum(dfb.reshape(seqs, N_FOX_HEADS), axis=0)
    df = df_rows.reshape(seqs, N_FOX_HEADS, l).transpose(0, 2, 1).reshape(t, N_FOX_HEADS)
    dqkv = jnp.concatenate([dq_raw, dk_raw, dv.astype(BF16)], axis=1)
    duf = jnp.concatenate([du_a + du_b, df, jnp.zeros((t, UF_COLS - S5_WIDTH - N_FOX_HEADS), F32)],
                          axis=1).astype(BF16)
    dhn1 = _mm(duf, w_uf, "nt", "in_uf_dx", res=_mm(dqkv, w_qkv, "nt", "in_qkv_dx"))
    dw_qkv = _mm(hn1, dqkv, "tn", "in_qkv_dw")
    dw_uf = _mm(hn1, duf, "tn", "in_uf_dw")
    g["w_in"] = jnp.concatenate([dw_qkv, dw_uf[:, S5_WIDTH:S5_WIDTH + N_FOX_HEADS], dw_uf[:, :S5_WIDTH]], axis=1)
    (dx,), (g["norm_mix"],) = _rowwise_vjp(_rms, [full(x)], [p["norm_mix"]], [full(dhn1)], "norm_mix_bwd",
                                           adds=[full(dh1)])
    return loss, dx.reshape(seqs, l, d), g


def _place():
    return lax.axis_index("x"), lax.axis_index("y"), lax.axis_index("c")


def _other_chips(x, y):
    return [(1 - x, y), (x, 1 - y), (1 - x, 1 - y)]


ANY = pl.BlockSpec(memory_space=pl.ANY)


def _gather_weights(shards, col_kind, taps):
    n = len(shards)

    def body(*refs):
        ins, tap_in, outs, tap_out = refs[:n], refs[n], refs[n + 1:2 * n + 1], refs[2 * n + 1]
        ici_send, ici_recv, d2d_send, d2d_recv, own_send, own_recv = refs[2 * n + 2:]
        x, y, c = _place()
        mine = 2 * x + y
        chips = _other_chips(x, y)
        sibling = (x, y, 1 - c)

        def piece(a, s, h):
            r, cs = ins[a].shape
            hr = r // 2
            if col_kind[a]:
                return outs[a].at[pl.ds(pl.multiple_of(h * hr, 16), hr), pl.ds(pl.multiple_of(s * cs, 128), cs)]
            return outs[a].at[pl.ds(pl.multiple_of(s * r + h * hr, 16), hr), :]

        def slab(a, s):
            r, cs = ins[a].shape
            if col_kind[a]:
                return outs[a].at[:, pl.ds(pl.multiple_of(s * cs, 128), cs)]
            return outs[a].at[pl.ds(pl.multiple_of(s * r, 16), r), :]

        def own_half(a, h):
            hr = ins[a].shape[0] // 2
            return ins[a].at[pl.ds(pl.multiple_of(h * hr, 16), hr), :]

        sends = []
        for a in range(n):
            cp = pltpu.make_async_remote_copy(
                src_ref=ins[a], dst_ref=slab(a, mine), send_sem=own_send.at[a], recv_sem=own_recv.at[a],
                device_id=sibling, device_id_type=MESH)
            cp.start()
            sends.append(cp)
        cp = pltpu.make_async_remote_copy(
            src_ref=tap_in, dst_ref=tap_out.at[mine], send_sem=own_send.at[n], recv_sem=own_recv.at[n],
            device_id=sibling, device_id_type=MESH)
        cp.start()
        sends.append(cp)
        for a in range(n):
            for j, (px, py) in enumerate(chips):
                cp = pltpu.make_async_remote_copy(
                    src_ref=own_half(a, c), dst_ref=piece(a, mine, c), send_sem=ici_send.at[3 * a + j],
                    recv_sem=ici_recv.at[3 * a + j], device_id=(px, py, c), device_id_type=MESH)
                cp.start()
                sends.append(cp)
        for j, (px, py) in enumerate(chips):
            cp = pltpu.make_async_remote_copy(
                src_ref=tap_in, dst_ref=tap_out.at[mine], send_sem=ici_send.at[3 * n + j],
                recv_sem=ici_recv.at[3 * n + j], device_id=(px, py, c), device_id_type=MESH)
            cp.start()
            sends.append(cp)
        for a in range(n):
            for j, (px, py) in enumerate(chips):
                got = piece(a, 2 * px + py, c)
                pltpu.make_async_remote_copy(
                    src_ref=got, dst_ref=got, send_sem=ici_send.at[3 * a + j], recv_sem=ici_recv.at[3 * a + j],
                    device_id=(px, py, c), device_id_type=MESH).wait_recv()
                fwd = pltpu.make_async_remote_copy(
                    src_ref=got, dst_ref=got, send_sem=d2d_send.at[3 * a + j], recv_sem=d2d_recv.at[3 * a + j],
                    device_id=(x, y, 1 - c), device_id_type=MESH)
                fwd.start()
                sends.append(fwd)
        for a in range(n):
            for j, (px, py) in enumerate(chips):
                other = piece(a, 2 * px + py, 1 - c)
                pltpu.make_async_remote_copy(
                    src_ref=other, dst_ref=other, send_sem=d2d_send.at[3 * a + j], recv_sem=d2d_recv.at[3 * a + j],
                    device_id=(x, y, 1 - c), device_id_type=MESH).wait_recv()
        for j, (px, py) in enumerate(chips):
            pltpu.make_async_remote_copy(
                src_ref=tap_in, dst_ref=tap_out.at[2 * px + py], send_sem=ici_send.at[3 * n + j],
                recv_sem=ici_recv.at[3 * n + j], device_id=(px, py, c), device_id_type=MESH).wait_recv()
        for a in range(n):
            pltpu.make_async_remote_copy(
                src_ref=ins[a], dst_ref=slab(a, mine), send_sem=own_send.at[a], recv_sem=own_recv.at[a],
                device_id=sibling, device_id_type=MESH).wait_recv()
        pltpu.make_async_remote_copy(
            src_ref=tap_in, dst_ref=tap_out.at[mine], send_sem=own_send.at[n], recv_sem=own_recv.at[n],
            device_id=sibling, device_id_type=MESH).wait_recv()
        for cp in sends:
            cp.wait_send()

    def full_shape(a):
        r, cs = shards[a].shape
        return (r, 4 * cs) if col_kind[a] else (4 * r, cs)

    res = pl.pallas_call(
        body, name="gather_weights", in_specs=[ANY] * (n + 1), out_specs=[ANY] * (n + 1),
        out_shape=[jax.ShapeDtypeStruct(full_shape(a), shards[a].dtype) for a in range(n)]
        + [jax.ShapeDtypeStruct((4,) + taps.shape, taps.dtype)],
        scratch_shapes=[pltpu.SemaphoreType.DMA((3 * n + 3,)), pltpu.SemaphoreType.DMA((3 * n + 3,)),
                        pltpu.SemaphoreType.DMA((3 * n,)), pltpu.SemaphoreType.DMA((3 * n,)),
                        pltpu.SemaphoreType.DMA((n + 1,)), pltpu.SemaphoreType.DMA((n + 1,))],
        compiler_params=pltpu.CompilerParams(has_side_effects=True),
    )(*shards, taps)
    return res[:n], res[n]


HBM = pl.BlockSpec(memory_space=pltpu.HBM)
SEM = pl.BlockSpec(memory_space=pltpu.SEMAPHORE)
DATAFLOW = pltpu.SideEffectType.DATAFLOW_SIDE_EFFECTING


def _in_hbm(a):
    return pltpu.with_memory_space_constraint(a, pltpu.HBM)


def _split_start(name, srcs, lands, n_copies, plan):
    n = len(srcs)

    def body(*refs):
        src_refs, land_refs = refs[:n], refs[n:2 * n]
        send_sems, recv_sems = refs[2 * n], refs[2 * n + 1]
        for i, (src, dst, dev) in enumerate(plan(src_refs, land_refs)):
            pltpu.make_async_remote_copy(src_ref=src, dst_ref=dst, send_sem=send_sems.at[i], recv_sem=recv_sems.at[i],
                                         device_id=dev, device_id_type=MESH).start()
        refs[-1][...] = jnp.zeros((8, 128), F32)

    res = pl.pallas_call(
        body, name=name, in_specs=[HBM] * (2 * n),
        out_specs=[SEM, SEM] + [HBM] * (2 * n) + [pl.BlockSpec(memory_space=pltpu.VMEM)],
        out_shape=[pltpu.SemaphoreType.DMA((n_copies,)), pltpu.SemaphoreType.DMA((n_copies,))]
        + [pltpu.HBM(a.shape, a.dtype) for a in list(srcs) + list(lands)] + [jax.ShapeDtypeStruct((8, 128), F32)],
        input_output_aliases={i: 2 + i for i in range(2 * n)},
        compiler_params=pltpu.CompilerParams(has_side_effects=DATAFLOW),
    )(*[_in_hbm(a) for a in list(srcs) + list(lands)])
    return res[0], res[1], list(res[2:2 + n]), list(res[2 + n:2 + 2 * n]), res[-1]


def _split_wait(name, send_sems, recv_sems, srcs, lands, after, plan):
    n = len(srcs)

    def body(*refs):
        src_refs, land_refs = refs[:n], refs[n:2 * n]
        send_ref, recv_ref = refs[2 * n], refs[2 * n + 1]
        for i, (src, dst, dev) in enumerate(plan(src_refs, land_refs)):
            cp = pltpu.make_async_remote_copy(src_ref=src, dst_ref=dst, send_sem=send_ref.at[i], recv_sem=recv_ref.at[i],
                                              device_id=dev, device_id_type=MESH)
            cp.wait_send()
            cp.wait_recv()

    res = pl.pallas_call(
        body, name=name, in_specs=[HBM] * (2 * n) + [SEM, SEM, ANY], out_specs=[HBM] * (2 * n),
        out_shape=[pltpu.HBM(a.shape, a.dtype) for a in list(srcs) + list(lands)],
        input_output_aliases={i: i for i in range(2 * n)},
        compiler_params=pltpu.CompilerParams(has_side_effects=DATAFLOW),
    )(*srcs, *lands, send_sems, recv_sems, after)
    return list(res[:n]), list(res[n:])


def _late_gather_plan(col_kind):
    def plan(src_refs, land_refs):
        x, y, c = _place()
        mine = 2 * x + y
        copies = []
        for a, (src, land) in enumerate(zip(src_refs, land_refs)):
            r, cs = src.shape
            if col_kind[a]:
                dst = land.at[:, pl.ds(pl.multiple_of(mine * cs, 128), cs)]
            else:
                dst = land.at[pl.ds(pl.multiple_of(mine * r, 16), r), :]
            copies.append((src, dst, (x, y, 1 - c)))
            copies += [(src, dst, (px, py, c)) for (px, py) in _other_chips(x, y)]
        return copies
    return plan


def _late_reduce_plan(col_kind):
    def plan(src_refs, land_refs):
        x, y, c = _place()
        copies = []
        for a, (src, land) in enumerate(zip(src_refs, land_refs)):
            for j, (px, py) in enumerate(_other_chips(x, y)):
                if col_kind[a]:
                    cs = land.shape[2]
                    piece = src.at[:, pl.ds(pl.multiple_of((2 * px + py) * cs, 128), cs)]
                else:
                    piece = src.at[2 * px + py]
                copies.append((piece, land.at[j], (px, py, c)))
        return copies
    return plan


def _pair_exchange_halves(name, grads, col_kind):
    n = len(grads)

    def body(*refs):
        ins, outs = refs[:n], refs[n:2 * n]
        send_sems, recv_sems = refs[2 * n:]
        x, y, c = _place()
        copies = []
        for a in range(n):
            if col_kind[a]:
                hr = ins[a].shape[0] // 2
                src = ins[a].at[pl.ds(pl.multiple_of((1 - c) * hr, 8), hr), :]
            else:
                hr = ins[a].shape[1] // 2
                src = ins[a].at[:, pl.ds(pl.multiple_of((1 - c) * hr, 8), hr), :]
            cp = pltpu.make_async_remote_copy(
                src_ref=src, dst_ref=outs[a], send_sem=send_sems.at[a], recv_sem=recv_sems.at[a],
                device_id=(x, y, 1 - c), device_id_type=MESH)
            cp.start()
            copies.append(cp)
        for cp in copies:
            cp.wait()

    def half_shape(a):
        s = grads[a].shape
        return (s[0] // 2, s[1]) if col_kind[a] else (4, s[1] // 2, s[2])

    return pl.pallas_call(
        body, name=name, in_specs=[ANY] * n, out_specs=[ANY] * n,
        out_shape=[jax.ShapeDtypeStruct(half_shape(a), grads[a].dtype) for a in range(n)],
        scratch_shapes=[pltpu.SemaphoreType.DMA((n,)), pltpu.SemaphoreType.DMA((n,))],
        compiler_params=pltpu.CompilerParams(has_side_effects=True),
    )(*grads)


def _pair_swap_halves(name, halves):
    n = len(halves)

    def body(*refs):
        ins, outs = refs[:n], refs[n:2 * n]
        send_sems, recv_sems = refs[2 * n:]
        x, y, c = _place()
        copies = []
        for a in range(n):
            cp = pltpu.make_async_remote_copy(
                src_ref=ins[a], dst_ref=outs[a], send_sem=send_sems.at[a], recv_sem=recv_sems.at[a],
                device_id=(x, y, 1 - c), device_id_type=MESH)
            cp.start()
            copies.append(cp)
        for cp in copies:
            cp.wait()

    return pl.pallas_call(
        body, name=name, in_specs=[ANY] * n, out_specs=[ANY] * n,
        out_shape=[jax.ShapeDtypeStruct(s.shape, s.dtype) for s in halves],
        scratch_shapes=[pltpu.SemaphoreType.DMA((n,)), pltpu.SemaphoreType.DMA((n,))],
        compiler_params=pltpu.CompilerParams(has_side_effects=True),
    )(*halves)


def _chip_sum(name, chip_sel, own, col, others):
    _, r, c = others.shape
    tr = _pick(r, (256, 128, 64, 32, 16))
    if col:
        own_spec = pl.BlockSpec((tr, c), lambda i, s: (i, s[0]))
    else:
        own_spec = pl.BlockSpec((None, tr, c), lambda i, s: (s[0], i, 0))
    specs = [own_spec] + [pl.BlockSpec((None, tr, c), lambda i, s, k=k: (k, i, 0)) for k in range(3)]

    def body(s_ref, own_ref, r0, r1, r2, o_ref):
        o_ref[...] = ((own_ref[...].astype(F32) + r0[...].astype(F32)) + r1[...].astype(F32)) + r2[...].astype(F32)

    return pl.pallas_call(
        body, name=name,
        grid_spec=pltpu.PrefetchScalarGridSpec(
            num_scalar_prefetch=1, grid=(r // tr,), in_specs=specs,
            out_specs=pl.BlockSpec((tr, c), lambda i, s: (i, 0))),
        out_shape=jax.ShapeDtypeStruct((r, c), F32),
        compiler_params=_params(("parallel",)),
    )(chip_sel, own, others, others, others)


def _pair_sum(name, c_sel, grad, recv, col):
    if col:
        r, c4 = grad.shape
        hr, c = r // 2, c4 // 4
    else:
        _, r, c = grad.shape
        hr = r // 2
    tr = _pick(hr, (256, 128, 64, 32, 16))
    nb = hr // tr

    def body(s_ref, g_ref, r_ref, o_ref):
        o_ref[...] = (g_ref[...] + r_ref[...]).astype(o_ref.dtype)

    if col:
        in_specs = [pl.BlockSpec((tr, c), lambda k, i, s: (s[0] * nb + i, k)), pl.BlockSpec((tr, c), lambda k, i, s: (i, k))]
        out_spec = pl.BlockSpec((tr, c), lambda k, i, s: (i, k))
    else:
        in_specs = [pl.BlockSpec((None, tr, c), lambda k, i, s: (k, s[0] * nb + i, 0)),
                    pl.BlockSpec((None, tr, c), lambda k, i, s: (k, i, 0))]
        out_spec = pl.BlockSpec((None, tr, c), lambda k, i, s: (k, i, 0))
    return pl.pallas_call(
        body, name=name,
        grid_spec=pltpu.PrefetchScalarGridSpec(num_scalar_prefetch=1, grid=(4, nb), in_specs=in_specs,
                                               out_specs=out_spec),
        out_shape=jax.ShapeDtypeStruct(recv.shape, BF16),
        compiler_params=_params(("parallel", "parallel")),
    )(c_sel, grad, recv)


def _allreduce_small(vals):
    sizes = [int(math.prod(v.shape)) for v in vals]
    padded = [-(-s // 128) * 128 for s in sizes]
    total = -(-sum(padded) // 1024) * 1024
    flat = [jnp.pad(v.reshape(-1), (0, p - s)) for v, s, p in zip(vals, sizes, padded)]
    flat.append(jnp.zeros((total - sum(padded),), F32))
    packed = jnp.concatenate(flat).reshape(total // 128, 128)

    def body(in_ref, out_ref, r0, r1, r2, send_sems, recv_sems):
        x, y, c = _place()
        out_ref[...] = in_ref[...]
        for k, (peer, land) in enumerate(zip([(x, y, 1 - c), (1 - x, y, c), (x, 1 - y, c)], (r0, r1, r2))):
            cp = pltpu.make_async_remote_copy(
                src_ref=out_ref, dst_ref=land, send_sem=send_sems.at[k], recv_sem=recv_sems.at[k],
                device_id=peer, device_id_type=MESH)
            cp.start()
            cp.wait()
            out_ref[...] = out_ref[...] + land[...]

    vm = pl.BlockSpec(memory_space=pltpu.VMEM)
    summed = pl.pallas_call(
        body, name="allreduce_small", in_specs=[vm], out_specs=vm,
        out_shape=jax.ShapeDtypeStruct(packed.shape, F32),
        scratch_shapes=[pltpu.VMEM(packed.shape, F32)] * 3
        + [pltpu.SemaphoreType.DMA((3,)), pltpu.SemaphoreType.DMA((3,))],
        compiler_params=pltpu.CompilerParams(has_side_effects=True, vmem_limit_bytes=VMEM_LIMIT_BYTES),
    )(packed).reshape(-1)
    outs, off = [], 0
    for v, s, p in zip(vals, sizes, padded):
        outs.append(summed[off:off + s].reshape(v.shape))
        off += p
    return outs


def _adamw_math(w, g, m, v):
    m2 = ADAM_B1 * m + (1.0 - ADAM_B1) * g
    v2 = ADAM_B2 * v + (1.0 - ADAM_B2) * (g * g)
    m_hat = m2 / (1.0 - ADAM_B1 ** ADAM_STEP)
    v_hat = v2 / (1.0 - ADAM_B2 ** ADAM_STEP)
    delta = -ADAM_LR * (m_hat / (jnp.sqrt(v_hat) + ADAM_EPS) + ADAM_WD * w)
    return delta, m2, v2


def _adamw_big(name, c_sel, w, g_mine, g_sibling, m, v, halves):
    _, r, c = w.shape

    def body(s_ref, w_ref, ga_ref, gb_ref, m_ref, v_ref, go_ref, d_ref, mo_ref, vo_ref):
        if halves:
            gv = jnp.where(pl.program_id(0) == s_ref[0], ga_ref[...], gb_ref[...])
        else:
            gv = ga_ref[...] + gb_ref[...]
        d, m2, v2 = _adamw_math(w_ref[...], gv, m_ref[...], v_ref[...])
        go_ref[...] = gv
        d_ref[...] = d
        mo_ref[...] = m2
        vo_ref[...] = v2

    if halves == "cols":
        hc = c // 2
        nb = 1
        blk = pl.BlockSpec((None, r, hc), lambda h, i, s: (0, 0, h))
        half = pl.BlockSpec((r, hc), lambda h, i, s: (0, 0))
    else:
        hr = r // 2
        tr = _pick(hr, (256, 128, 64, 32, 16, 8))
        nb = hr // tr
        blk = pl.BlockSpec((None, tr, c), lambda h, i, s: (0, h * nb + i, 0))
        half = pl.BlockSpec((tr, c), (lambda h, i, s: (i, 0)) if halves else (lambda h, i, s: (h * nb + i, 0)))
    return pl.pallas_call(
        body, name=name,
        grid_spec=pltpu.PrefetchScalarGridSpec(
            num_scalar_prefetch=1, grid=(2, nb), in_specs=[blk, half, half, blk, blk], out_specs=[blk] * 4),
        out_shape=[jax.ShapeDtypeStruct((1, r, c), F32)] * 4, compiler_params=_params(("parallel", "parallel")),
    )(c_sel, w, g_mine, g_sibling, m, v)


def _adamw_small(ws, gs, ms, vs):
    n = len(ws)

    def body(*refs):
        w_r, g_r, m_r, v_r = refs[:n], refs[n:2 * n], refs[2 * n:3 * n], refs[3 * n:4 * n]
        o = refs[4 * n:]
        for a in range(n):
            gv = g_r[a][...]
            d, m2, v2 = _adamw_math(w_r[a][...], gv, m_r[a][...], v_r[a][...])
            o[a][...] = gv
            o[n + a][...] = d
            o[2 * n + a][...] = m2
            o[3 * n + a][...] = v2

    res = pl.pallas_call(
        body, name="adamw_small", out_shape=[jax.ShapeDtypeStruct(w.shape, F32) for _ in range(4) for w in ws],
        compiler_params=_params(),
    )(*ws, *gs, *ms, *vs)
    return res[:n], res[n:2 * n], res[2 * n:3 * n], res[3 * n:]


def _full_from_gathered(name, gathered):
    if name == "w_in":
        rows = gathered.shape[0] // 4
        return gathered.reshape(4, rows, gathered.shape[1]).transpose(1, 0, 2).reshape(rows, 4 * gathered.shape[1])
    return gathered


def _reduce_layout(name, full):
    if name in COL_KIND:
        return full
    if name == "w_in":
        rows, cols = full.shape
        return full.reshape(rows, 4, cols // 4).transpose(1, 0, 2)
    return full.reshape(4, full.shape[0] // 4, full.shape[1])


def kernel(x, mem, norm_mix, w_in, fox_q_norm, fox_k_norm, fox_f_bias, s5_a_re, s5_a_im, s5_log_dt, s5_b_re, s5_b_im, s5_c_re, s5_c_im, s5_d, s5_w_glu, s5_b_glu, out_norm_fox, out_norm_s5, w_out, norm_cross, norm_mem, w_xq, w_xkv, xq_norm, xk_norm, w_xo, norm_ffn, w_ffn_up, ffn_conv_w, ffn_conv_b, w_ffn_down, loss_target, m_norm_mix, m_w_in, m_fox_q_norm, m_fox_k_norm, m_fox_f_bias, m_s5_a_re, m_s5_a_im, m_s5_log_dt, m_s5_b_re, m_s5_b_im, m_s5_c_re, m_s5_c_im, m_s5_d, m_s5_w_glu, m_s5_b_glu, m_out_norm_fox, m_out_norm_s5, m_w_out, m_norm_cross, m_norm_mem, m_w_xq, m_w_xkv, m_xq_norm, m_xk_norm, m_w_xo, m_norm_ffn, m_w_ffn_up, m_ffn_conv_w, m_ffn_conv_b, m_w_ffn_down, v_norm_mix, v_w_in, v_fox_q_norm, v_fox_k_norm, v_fox_f_bias, v_s5_a_re, v_s5_a_im, v_s5_log_dt, v_s5_b_re, v_s5_b_im, v_s5_c_re, v_s5_c_im, v_s5_d, v_s5_w_glu, v_s5_b_glu, v_out_norm_fox, v_out_norm_s5, v_w_out, v_norm_cross, v_norm_mem, v_w_xq, v_w_xkv, v_xq_norm, v_xk_norm, v_w_xo, v_norm_ffn, v_w_ffn_up, v_ffn_conv_w, v_ffn_conv_b, v_w_ffn_down):
    given = dict(locals())
    w = {n: given[n] for n in WEIGHTS}
    m = {n: given["m_" + n] for n in WEIGHTS}
    v = {n: given["v_" + n] for n in WEIGHTS}
    xi, yi, ci = _place()
    chip = (2 * xi + yi).astype(jnp.int32)

    c_sel = ci.astype(jnp.int32).reshape(1)
    chip_sel = chip.reshape(1)
    early_kind = [n in COL_KIND for n in EARLY_WEIGHTS]
    late_kind = [n in COL_KIND for n in LATE_WEIGHTS]

    gathered, taps = _gather_weights([w[FIRST_WEIGHT][0].astype(BF16)], [False], w["ffn_conv_w"][0])
    wb = {FIRST_WEIGHT: _full_from_gathered(FIRST_WEIGHT, gathered[0])}
    conv_w = taps.transpose(1, 0, 2).reshape(3, D_FF)
    pending = {}
    g_started = None
    for stage, names in (("mid", MID_WEIGHTS), ("late", LATE_WEIGHTS)):
        kinds = [n in COL_KIND for n in names]
        shards = [w[n][0].astype(BF16) for n in names]
        if g_started is not None:
            shards[0] = shards[0] + g_started[0:1, 0:1].astype(BF16)
        full = [lax.empty((s.shape[0], 4 * s.shape[1]) if ck else (4 * s.shape[0], s.shape[1]), BF16)
                for s, ck in zip(shards, kinds)]
        plan = _late_gather_plan(kinds)
        send, recv, srcs, lands, g_started = _split_start(
            "gather_" + stage + "_start", shards, full, 4 * len(names), plan)
        pending[stage] = (names, plan, send, recv, srcs, lands)

    def late_weights(stage, after):
        names, plan, send, recv, srcs, lands = pending[stage]
        _, full = _split_wait("gather_" + stage + "_wait", send, recv, srcs, lands, after, plan)
        return dict(zip(names, full))

    reduce_plan = _late_reduce_plan(late_kind)
    late_reduce = {}

    def early_grads(late_g):
        grads = [_reduce_layout(n, late_g[n]) for n in LATE_WEIGHTS]
        lands = [lax.empty((3, s.shape[0], s.shape[1] // 4) if ck else (3,) + s.shape[1:], BF16)
                 for s, ck in zip(grads, late_kind)]
        late_reduce["sems"] = _split_start("reduce_late_start", grads, lands, 3 * len(LATE_WEIGHTS), reduce_plan)
        return late_reduce["sems"][4][0:1, 0:1]

    p = {n: w[n][0] for n in SMALL}
    p["ffn_conv_w"] = conv_w
    for n in ("norm_mix", "fox_q_norm", "fox_k_norm", "fox_f_bias", "s5_b_glu", "out_norm_fox", "out_norm_s5",
              "norm_cross", "norm_mem", "xq_norm", "xk_norm", "norm_ffn", "ffn_conv_b"):
        p[n] = p[n].reshape(1, -1)
    p["norm_mix"] = p["norm_mix"] + g_started[0:1, 0:1]
    loss, grad_x, g = _local_step(x, mem, loss_target, p, wb, late_weights, early_grads)

    grads = [_reduce_layout(n, g[n]) for n in EARLY_WEIGHTS]
    from_sibling = _pair_exchange_halves("reduce_pair_exchange_early", grads, early_kind)
    pair_sums = [_pair_sum("reduce_pair_sum_" + n, c_sel, gr, rv, ck)
                 for n, gr, rv, ck in zip(EARLY_WEIGHTS, grads, from_sibling, early_kind)]
    early_lands = [lax.empty((3, s.shape[0], s.shape[1] // 4) if ck else (3,) + s.shape[1:], BF16)
                   for s, ck in zip(pair_sums, early_kind)]
    early_plan = _late_reduce_plan(early_kind)
    e_send, e_recv, e_srcs, e_lands, e_started = _split_start(
        "reduce_early_start", pair_sums, early_lands, 3 * len(EARLY_WEIGHTS), early_plan)

    out_g, out_d, out_m, out_v = {}, {}, {}, {}

    def finish(names, kinds, sums, from_chips, tag, halves):
        mine = [_chip_sum("reduce_chip_sum_" + n, chip_sel, ps, ck, fc)
                for n, ps, fc, ck in zip(names, sums, from_chips, kinds)]
        theirs = _pair_swap_halves("reduce_pair_swap_" + tag, mine)
        for n, a, b in zip(names, mine, theirs):
            if n == "w_in":
                flip = lambda t: jnp.swapaxes(t, -1, -2)
                res = _adamw_big("adamw_" + n, c_sel, flip(w[n]), flip(a), flip(b), flip(m[n]), flip(v[n]), "cols")
                out_g[n], out_d[n], out_m[n], out_v[n] = (flip(t) for t in res)
                continue
            out_g[n], out_d[n], out_m[n], out_v[n] = _adamw_big("adamw_" + n, c_sel, w[n], a, b, m[n], v[n], halves)

    r_send, r_recv, r_srcs, r_lands, _ = late_reduce["sems"]
    late_sums, late_from_chips = _split_wait("reduce_late_wait", r_send, r_recv, r_srcs, r_lands, e_started,
                                             reduce_plan)
    finish(LATE_WEIGHTS, late_kind, late_sums, late_from_chips, "late", False)

    small_names = list(SMALL) + ["ffn_conv_w"]
    small_vals = [g[n].reshape(w[n].shape if n != "ffn_conv_w" else (1, 3, D_FF)) for n in small_names]
    last = LATE_WEIGHTS[-1]
    loss, out_v[last] = lax.optimization_barrier((loss, out_v[last]))
    reduced = _allreduce_small(small_vals + [loss])
    loss_all = reduced[-1].reshape(())
    conv_w_grad = lax.dynamic_slice_in_dim(reduced[-2], chip * (D_FF // 4), D_FF // 4, axis=2)
    sg, sd, sm, sv = _adamw_small(
        [w[n] for n in small_names], list(reduced[:len(SMALL)]) + [conv_w_grad],
        [m[n] for n in small_names], [v[n] for n in small_names])
    out_g.update(zip(small_names, sg))
    out_d.update(zip(small_names, sd))
    out_m.update(zip(small_names, sm))
    out_v.update(zip(small_names, sv))

    early_sums, early_from_chips = _split_wait("reduce_early_wait", e_send, e_recv, e_srcs, e_lands, reduced[0],
                                               early_plan)
    finish(EARLY_WEIGHTS, early_kind, early_sums, early_from_chips, "early", True)

    return (loss_all, grad_x, *[out_g[n] for n in WEIGHTS], *[out_d[n] for n in WEIGHTS],
            *[out_m[n] for n in WEIGHTS], *[out_v[n] for n in WEIGHTS])
```

```python
import functools
import math

import jax
import jax.numpy as jnp
from jax import lax
from jax.experimental import pallas as pl
from jax.experimental.pallas import tpu as pltpu

F32 = jnp.float32
BF16 = jnp.bfloat16

D_MODEL = 1024
FOX_WIDTH = 512
HEAD_DIM = 64
N_FOX_HEADS = 8
S5_WIDTH = 512
S5_GROUP_CH = 16
S5_GROUPS = 32
S5_STATE = 64
S5_CH = S5_GROUPS * S5_STATE
N_X_HEADS = 4
X_HEAD_DIM = 256
N_MEM = 256
D_FF = 2816
UF_COLS = 640
EPS = 1e-6
ADAM_LR = 0.001
ADAM_B1 = 0.9
ADAM_B2 = 0.999
ADAM_EPS = 1e-08
ADAM_WD = 0.01
ADAM_STEP = 10

VMEM_LIMIT_BYTES = 56 * 1024 * 1024
MM_BLOCK_BYTES = 6 * 1024 * 1024
MM_VMEM_BYTES = 40 * 1024 * 1024
MM_TILE_MAX = 1536
MESH = pl.DeviceIdType.MESH

FIRST_WEIGHT = "w_in"
MID_WEIGHTS = ("s5_w_glu", "w_out")
EARLY_WEIGHTS = (FIRST_WEIGHT,) + MID_WEIGHTS
LATE_WEIGHTS = ("w_xq", "w_xkv", "w_xo", "w_ffn_up", "w_ffn_down")
BIG = EARLY_WEIGHTS + LATE_WEIGHTS
COL_KIND = ("w_xkv", "w_ffn_up")
SMALL = ("norm_mix", "fox_q_norm", "fox_k_norm", "fox_f_bias", "s5_a_re", "s5_a_im", "s5_log_dt",
         "s5_b_re", "s5_b_im", "s5_c_re", "s5_c_im", "s5_d", "s5_b_glu", "out_norm_fox", "out_norm_s5",
         "norm_cross", "norm_mem", "xq_norm", "xk_norm", "norm_ffn", "ffn_conv_b")
WEIGHTS = ("norm_mix", "w_in", "fox_q_norm", "fox_k_norm", "fox_f_bias", "s5_a_re", "s5_a_im", "s5_log_dt",
           "s5_b_re", "s5_b_im", "s5_c_re", "s5_c_im", "s5_d", "s5_w_glu", "s5_b_glu", "out_norm_fox",
           "out_norm_s5", "w_out", "norm_cross", "norm_mem", "w_xq", "w_xkv", "xq_norm", "xk_norm", "w_xo",
           "norm_ffn", "w_ffn_up", "ffn_conv_w", "ffn_conv_b", "w_ffn_down")


def _params(sem=None):
    return pltpu.CompilerParams(dimension_semantics=sem, vmem_limit_bytes=VMEM_LIMIT_BYTES)


def _pick(n, cands):
    for c in cands:
        if n % c == 0:
            return c
    return n


_DIMS = {"nn": (((1,), (0,)), ((), ())), "nt": (((1,), (1,)), ((), ())), "tn": (((0,), (0,)), ((), ()))}


def _mm(a, b, mode, name, out_dtype=F32, res=None):
    if mode == "nn":
        (m, k), (k2, n) = a.shape, b.shape
    elif mode == "nt":
        (m, k), (n, k2) = a.shape, b.shape
    else:
        (k, m), (k2, n) = a.shape, b.shape
    assert k == k2, (name, a.shape, b.shape)

    has_res = res is not None
    a_size, b_size = a.dtype.itemsize, b.dtype.itemsize
    o_size = jnp.dtype(out_dtype).itemsize + (res.dtype.itemsize if has_res else 0)

    def tiles(dim):
        return [c for c in range(MM_TILE_MAX, 0, -128) if dim % c == 0] or [dim]

    best = None
    for tm in tiles(m):
        for tn in tiles(n):
            a_blk, b_blk = tm * k * a_size, tn * k * b_size
            if max(a_blk, b_blk) > MM_BLOCK_BYTES or 2 * (a_blk + b_blk + tm * tn * o_size) > MM_VMEM_BYTES:
                continue
            for rows_outer in (True, False):
                moved = (m * k * a_size + (m // tm) * n * k * b_size) if rows_outer else \
                        (n * k * b_size + (n // tn) * m * k * a_size)
                key = (moved, -(tm * tn))
                if best is None or key < best[0]:
                    best = (key, tm, tn, rows_outer)
    assert best is not None, (name, a.shape, b.shape)
    _, tm, tn, rows_outer = best
    ij = (lambda g0, g1: (g0, g1)) if rows_outer else (lambda g0, g1: (g1, g0))
    if mode == "tn":
        a_spec = pl.BlockSpec((k, tm), lambda g0, g1: (0, ij(g0, g1)[0]))
    else:
        a_spec = pl.BlockSpec((tm, k), lambda g0, g1: (ij(g0, g1)[0], 0))
    if mode == "nt":
        b_spec = pl.BlockSpec((tn, k), lambda g0, g1: (ij(g0, g1)[1], 0))
    else:
        b_spec = pl.BlockSpec((k, tn), lambda g0, g1: (0, ij(g0, g1)[1]))
    o_spec = pl.BlockSpec((tm, tn), lambda g0, g1: ij(g0, g1))
    grid = (m // tm, n // tn) if rows_outer else (n // tn, m // tm)
    dims = _DIMS[mode]

    def body(*refs):
        a_ref, b_ref = refs[0], refs[1]
        o_ref = refs[-1]
        acc = lax.dot_general(a_ref[...].astype(BF16), b_ref[...].astype(BF16), dims, preferred_element_type=F32)
        if has_res:
            acc = acc + refs[2][...].astype(F32)
        o_ref[...] = acc.astype(o_ref.dtype)

    return pl.pallas_call(
        body, name=name, grid=grid,
        in_specs=[a_spec, b_spec] + ([o_spec] if has_res else []),
        out_specs=o_spec, out_shape=jax.ShapeDtypeStruct((m, n), out_dtype),
        compiler_params=_params(("parallel", "parallel")),
    )(*((a, b, res) if has_res else (a, b)))


def _row_spec(tm, bc, off, step):
    return pl.BlockSpec((tm, bc), lambda i, h: (i, off + step * h))


ROW_TILE_ELEMS = 512 * 1024


def _row_tile(t, rows):
    widest = max(bc for (_, bc, _, _) in rows)
    return _pick(t, (min(t, ROW_TILE_ELEMS // widest), 512, 256, 128, 64, 8))


def _rowwise(fn, rows, pars, outs, name, heads=1):
    t = rows[0][0].shape[0]
    tm = _row_tile(t, rows)
    nr, npar = len(rows), len(pars)

    def body(*refs):
        vals = [r[...].astype(F32) for r in refs[:nr + npar]]
        res = fn(*vals)
        if not isinstance(res, (tuple, list)):
            res = (res,)
        for o_ref, v in zip(refs[nr + npar:], res):
            o_ref[...] = v.astype(o_ref.dtype)

    in_specs = [_row_spec(tm, bc, off, st) for (_, bc, off, st) in rows]
    in_specs += [pl.BlockSpec(p.shape, lambda i, h: (0, 0)) for p in pars]
    out_specs = [_row_spec(tm, bc, 0, st) for (_, bc, st, _) in outs]
    out_shape = [jax.ShapeDtypeStruct((t, c), dt) for (c, _, _, dt) in outs]
    res = pl.pallas_call(
        body, name=name, grid=(t // tm, heads), in_specs=in_specs, out_specs=out_specs, out_shape=out_shape,
        compiler_params=_params(("parallel", "parallel")),
    )(*[r[0] for r in rows], *pars)
    return res[0] if len(res) == 1 else res


def _rowwise_vjp(fn, rows, pars, cts, name, heads=1, adds=None, row_dtypes=None):
    t = rows[0][0].shape[0]
    tm = _row_tile(t, rows)
    nr, npar, nct = len(rows), len(pars), len(cts)
    adds = adds or [None] * nr
    add_list = [a for a in adds if a is not None]
    row_dtypes = row_dtypes or [F32] * nr

    def body(*refs):
        i, h = pl.program_id(0), pl.program_id(1)
        p = 0
        row_v = [r[...].astype(F32) for r in refs[p:p + nr]]; p += nr
        par_v = [r[...].astype(F32) for r in refs[p:p + npar]]; p += npar
        ct_v = [r[...].astype(F32) for r in refs[p:p + nct]]; p += nct
        add_refs = refs[p:p + len(add_list)]; p += len(add_list)
        drow_refs = refs[p:p + nr]; p += nr
        dpar_refs = refs[p:p + npar]

        def wrapped(*a):
            r = fn(*a)
            return tuple(r) if isinstance(r, (tuple, list)) else (r,)

        _, pull = jax.vjp(wrapped, *row_v, *par_v)
        grads = pull(tuple(ct_v))
        ai = 0
        for k in range(nr):
            g = grads[k]
            if adds[k] is not None:
                g = g + add_refs[ai][...].astype(F32)
                ai += 1
            drow_refs[k][...] = g.astype(drow_refs[k].dtype)

        @pl.when((i == 0) & (h == 0))
        def _():
            for r in dpar_refs:
                r[...] = jnp.zeros(r.shape, r.dtype)

        for k in range(npar):
            dpar_refs[k][...] += grads[nr + k]

    in_specs = [_row_spec(tm, bc, off, st) for (_, bc, off, st) in rows]
    in_specs += [pl.BlockSpec(q.shape, lambda i, h: (0, 0)) for q in pars]
    in_specs += [_row_spec(tm, bc, off, st) for (_, bc, off, st) in cts]
    in_specs += [_row_spec(tm, bc, off, st) for (_, bc, off, st) in add_list]
    out_specs = [_row_spec(tm, bc, 0, st) for (_, bc, _, st) in rows]
    out_specs += [pl.BlockSpec(q.shape, lambda i, h: (0, 0)) for q in pars]
    out_shape = [jax.ShapeDtypeStruct((t, bc * (heads if st else 1)), dt) for (_, bc, _, st), dt in zip(rows, row_dtypes)]
    out_shape += [jax.ShapeDtypeStruct(q.shape, F32) for q in pars]
    res = pl.pallas_call(
        body, name=name, grid=(t // tm, heads), in_specs=in_specs, out_specs=out_specs, out_shape=out_shape,
        compiler_params=_params(("arbitrary", "arbitrary")),
    )(*[r[0] for r in rows], *pars, *[c[0] for c in cts], *[a[0] for a in add_list])
    return list(res[:nr]), list(res[nr:])


def _rms(x, g):
    return x * lax.rsqrt(jnp.mean(x * x, axis=-1, keepdims=True) + EPS) * g


def _rms_pair(x, g):
    left = lax.broadcasted_iota(jnp.int32, x.shape, 1) < HEAD_DIM
    x2 = x * x
    ms_a = jnp.sum(jnp.where(left, x2, 0.0), axis=-1, keepdims=True) * (1.0 / HEAD_DIM)
    ms_b = jnp.sum(jnp.where(left, 0.0, x2), axis=-1, keepdims=True) * (1.0 / HEAD_DIM)
    return x * lax.rsqrt(jnp.where(left, ms_a, ms_b) + EPS) * g


def _gelu(x):
    return 0.5 * x * (1.0 + jnp.tanh(math.sqrt(2.0 / math.pi) * (x + 0.044715 * (x * x * x))))


def _s5_act(ys, u, d):
    return _gelu(ys + d * u)


def _s5_gate(yg, z, b, g):
    return _rms(yg * jax.nn.sigmoid(z + b), g)


def _lane_cumsum(x, reverse):
    n = x.shape[-1]
    lane = lax.broadcasted_iota(jnp.int32, x.shape, 1)
    k = 1
    while k < n:
        if reverse:
            x = x + jnp.where(lane < n - k, pltpu.roll(x, n - k, 1), 0.0)
        else:
            x = x + jnp.where(lane >= k, pltpu.roll(x, k, 1), 0.0)
        k *= 2
    return x


def _log_sigmoid(z):
    return jnp.minimum(z, 0.0) - jnp.log(1.0 + jnp.exp(-jnp.abs(z)))


def _forget_fwd(f, bias):
    def body(f_ref, b_ref, c_ref):
        c_ref[...] = _lane_cumsum(_log_sigmoid(f_ref[...] + b_ref[...]), False)

    return pl.pallas_call(body, name="forget_fwd", out_shape=jax.ShapeDtypeStruct(f.shape, F32),
                          compiler_params=_params())(f, bias)


def _forget_bwd(f, bias, dc):
    def body(f_ref, b_ref, dc_ref, df_ref, db_ref):
        dlog = _lane_cumsum(dc_ref[...], True)
        df = dlog * jax.nn.sigmoid(-(f_ref[...] + b_ref[...]))
        df_ref[...] = df
        db_ref[...] = jnp.sum(df, axis=1, keepdims=True)

    return pl.pallas_call(body, name="forget_bwd",
                          out_shape=(jax.ShapeDtypeStruct(f.shape, F32), jax.ShapeDtypeStruct(bias.shape, F32)),
                          compiler_params=_params())(f, bias, dc)


FOX_BLOCK = 256
FOX_KEYS = 256
_NT = _DIMS["nt"]
_TN = _DIMS["tn"]


N_PAIRS = N_FOX_HEADS // 2
V_BLOCK0 = 2 * N_PAIRS


def _left_lanes(shape):
    return lax.broadcasted_iota(jnp.int32, shape, 1) < HEAD_DIM


def _top_rows(shape):
    return lax.broadcasted_iota(jnp.int32, shape, 0) < HEAD_DIM


def _wide(c_tile, n):
    return c_tile if n == 128 else jnp.concatenate([c_tile] * (n // 128), axis=1)


def _fox_fwd(qn, kn, qkv, c_wide, seqs):
    t = qn.shape[0]
    l = t // seqs
    tb = min(FOX_BLOCK, l)
    tk = min(FOX_KEYS, tb)
    ratio = tb // tk
    nb = l // tb
    scale = HEAD_DIM ** -0.5

    def body(q_ref, k_ref, v_ref, ca_ref, cb_ref, o_ref, lse_ref, vt_ref):
        i = pl.program_id(2)
        top = _top_rows((128, tb))

        @pl.when(i == 0)
        def _():
            vt_ref[...] = v_ref[...].T.astype(BF16)

        qt = (q_ref[...].astype(F32) * scale).T.astype(BF16)
        zero = jnp.zeros_like(qt)
        qts = (jnp.where(top, qt, zero), jnp.where(top, zero, qt))
        top_k = _top_rows((128, tk))
        zero_k = jnp.zeros((128, tk), BF16)
        key_pos = lax.broadcasted_iota(jnp.int32, (tk, tb), 0)
        query_pos = lax.broadcasted_iota(jnp.int32, (tk, tb), 1)
        c_refs = (ca_ref, cb_ref)

        def scores(j):
            off = pl.multiple_of(j * tk, tk)
            k2 = k_ref[pl.ds(off, tk), :]
            return tuple(jnp.dot(k2, qts[h], preferred_element_type=F32) - _wide(c_refs[h][pl.ds(off, tk), :], tb)
                         for h in (0, 1))

        def values_times(ps, j):
            vt = vt_ref[:, pl.ds(pl.multiple_of(j * tk, tk), tk)]
            return (jnp.dot(jnp.where(top_k, vt, zero_k), ps[0], preferred_element_type=F32)
                    + jnp.dot(jnp.where(top_k, zero_k, vt), ps[1], preferred_element_type=F32))

        def softmax_step(sts, stats, first_key):
            ps, new, alphas = [], [], []
            for st, (m, s_sum) in zip(sts, stats):
                if first_key is not None:
                    st = jnp.where(key_pos + first_key <= query_pos, st, -jnp.inf)
                m_new = jnp.maximum(m, jnp.max(st, axis=0, keepdims=True))
                alpha = jnp.exp(m - m_new)
                p = jnp.exp(st - m_new)
                new.append((m_new, alpha * s_sum + jnp.sum(p, axis=0, keepdims=True)))
                alphas.append(alpha)
                ps.append(p.astype(BF16))
            return tuple(ps), tuple(new), jnp.where(top, alphas[0], alphas[1])

        def step(j, carry):
            sts, ps_prev, stats, acc = carry
            sts_next = scores(j + 1)
            acc = acc + values_times(ps_prev, jnp.maximum(j - 1, 0))
            ps, stats, alpha = softmax_step(sts, stats, None)
            return sts_next, ps, stats, alpha * acc

        stat = (jnp.full((1, tb), -jnp.inf, F32), jnp.zeros((1, tb), F32))
        no_p = jnp.zeros((tk, tb), BF16)
        below = i * ratio
        sts, ps_prev, stats, acc = lax.fori_loop(
            0, below, step, (scores(0), (no_p, no_p), (stat, stat), jnp.zeros((128, tb), F32)))
        for r in range(ratio):
            sts_next = scores(below + r + 1) if r + 1 < ratio else None
            acc = acc + values_times(ps_prev, jnp.maximum(below + r - 1, 0))
            ps_prev, stats, alpha = softmax_step(sts, stats, r * tk)
            acc = alpha * acc
            sts = sts_next
        acc = acc + values_times(ps_prev, below + ratio - 1)
        (ma, sa), (mb, sb) = stats
        o_ref[...] = (acc / jnp.where(top, sa, sb)).T
        lse_ref[0:1, :] = ma + jnp.log(sa)
        lse_ref[1:2, :] = mb + jnp.log(sb)

    qblk = pl.BlockSpec((tb, 128), lambda b, hp, i: (b * nb + i, hp))
    return pl.pallas_call(
        body, name="fox_fwd", grid=(seqs, N_PAIRS, nb),
        in_specs=[qblk, pl.BlockSpec((l, 128), lambda b, hp, i: (b, hp)),
                  pl.BlockSpec((l, 128), lambda b, hp, i: (b, V_BLOCK0 + hp)),
                  pl.BlockSpec((None, l, 128), lambda b, hp, i: (b * N_FOX_HEADS + 2 * hp, 0, 0)),
                  pl.BlockSpec((None, l, 128), lambda b, hp, i: (b * N_FOX_HEADS + 2 * hp + 1, 0, 0))],
        out_specs=[qblk, pl.BlockSpec((None, 2, tb), lambda b, hp, i: (b * N_PAIRS + hp, 0, i))],
        out_shape=[jax.ShapeDtypeStruct((t, FOX_WIDTH), F32), jax.ShapeDtypeStruct((seqs * N_PAIRS, 2, l), F32)],
        scratch_shapes=[pltpu.VMEM((128, l), BF16)],
        compiler_params=_params(("parallel", "parallel", "arbitrary")),
    )(qn, kn, qkv, c_wide, c_wide)


def _fox_bwd(qn, kn, qkv, c_wide, o, do, lse, seqs):
    t = qn.shape[0]
    l = t // seqs
    tb = min(FOX_BLOCK, l)
    nb = l // tb
    scale = HEAD_DIM ** -0.5
    one_at = (HEAD_DIM, 0)

    def body(q_ref, k_ref, v_ref, ca_ref, cb_ref, o_ref, do_ref, lse_ref, dq_ref, dk_ref, dv_ref, dc_ref, dcq_ref,
             qt_ref, kt_ref, dot_ref, delta_ref, dqa_ref, dqb_ref):
        top_l = _top_rows((128, l))
        top = _top_rows((128, tb))
        left = _left_lanes((tb, 128))
        row_id = lax.broadcasted_iota(jnp.int32, (128, tb), 0)
        lane_id = lax.broadcasted_iota(jnp.int32, (tb, 128), 1)
        zero_t = jnp.zeros((128, tb), BF16)
        zero_l = jnp.zeros((tb, 128), BF16)
        rows = lambda a: (jnp.where(top, a, zero_t), jnp.where(top, zero_t, a))
        lanes = lambda a: (jnp.where(left, a, zero_l), jnp.where(left, zero_l, a))
        with_one_row = lambda pair: tuple(jnp.where(row_id == one_at[h], 1.0, pair[h]).astype(BF16) for h in (0, 1))
        with_one_lane = lambda pair: tuple(jnp.where(lane_id == one_at[h], 1.0, pair[h]).astype(BF16) for h in (0, 1))
        causal = lax.broadcasted_iota(jnp.int32, (tb, tb), 0) <= lax.broadcasted_iota(jnp.int32, (tb, tb), 1)
        c_refs = (ca_ref, cb_ref)
        dq_refs = (dqa_ref, dqb_ref)

        qt_ref[...] = (q_ref[...].astype(F32) * scale).T.astype(BF16)
        kt_ref[...] = k_ref[...].astype(F32).T.astype(BF16)
        do_t = do_ref[...].T
        dot_ref[...] = do_t.astype(BF16)
        prod_t = do_t * o_ref[...].T
        delta_ref[0:1, :] = jnp.sum(jnp.where(top_l, prod_t, 0.0), axis=0, keepdims=True)
        delta_ref[1:2, :] = jnp.sum(jnp.where(top_l, 0.0, prod_t), axis=0, keepdims=True)
        dqa_ref[...] = jnp.zeros(dqa_ref.shape, F32)
        dqb_ref[...] = jnp.zeros(dqb_ref.shape, F32)

        def kv_block(j, _):
            koff = pl.multiple_of(j * tb, tb)
            k2 = k_ref[pl.ds(koff, tb), :]
            v2 = v_ref[pl.ds(koff, tb), :].astype(BF16)
            kts = with_one_row(rows(kt_ref[:, pl.ds(koff, tb)]))
            cw = tuple(_wide(c_refs[h][pl.ds(koff, tb), :], tb) for h in (0, 1))

            def q_block(i, carry, masked):
                dks, dv = list(carry[:2]), carry[2]
                qoff = pl.multiple_of(i * tb, tb)
                qs = lanes((q_ref[pl.ds(qoff, tb), :].astype(F32) * scale).astype(BF16))
                qs_one = with_one_lane(qs)
                dos = lanes(do_ref[pl.ds(qoff, tb), :].astype(BF16))
                qts = rows(qt_ref[:, pl.ds(qoff, tb)])
                dots = rows(dot_ref[:, pl.ds(qoff, tb)])
                for h in (0, 1):
                    st = jnp.dot(k2, qts[h], preferred_element_type=F32) - cw[h]
                    p = jnp.exp(st - lse_ref[h:h + 1, pl.ds(qoff, tb)])
                    if masked:
                        p = jnp.where(causal, p, 0.0)
                    dp = jnp.dot(v2, dots[h], preferred_element_type=F32)
                    dsb = (p * (dp - delta_ref[h:h + 1, pl.ds(qoff, tb)])).astype(BF16)
                    dv = dv + jnp.dot(p.astype(BF16), dos[h], preferred_element_type=F32)
                    dks[h] = dks[h] + jnp.dot(dsb, qs_one[h], preferred_element_type=F32)
                    dq_refs[h][:, pl.ds(qoff, tb)] += jnp.dot(kts[h], dsb, preferred_element_type=F32)
                return dks[0], dks[1], dv

            z = jnp.zeros((tb, 128), F32)
            carry = q_block(j, (z, z, z), True)
            rest = nb - 1 - j
            carry = lax.fori_loop(
                0, rest // 2, lambda n, c: q_block(j + 2 + 2 * n, q_block(j + 1 + 2 * n, c, False), False), carry)
            dka, dkb, dv = lax.cond(rest % 2 == 1, lambda c: q_block(nb - 1, c, False), lambda c: c, carry)
            dk_ref[pl.ds(koff, tb), :] = jnp.where(left, dka, dkb)
            dv_ref[pl.ds(koff, tb), :] = dv
            dc_ref[0:1, pl.ds(koff, tb)] = -dka.T[one_at[0]:one_at[0] + 1, :]
            dc_ref[1:2, pl.ds(koff, tb)] = -dkb.T[one_at[1]:one_at[1] + 1, :]
            return 0

        lax.fori_loop(0, nb, kv_block, 0)
        dq_ref[...] = (jnp.where(top_l, dqa_ref[...], dqb_ref[...]) * scale).T
        dcq_ref[0:1, :] = dqa_ref[one_at[0]:one_at[0] + 1, :]
        dcq_ref[1:2, :] = dqb_ref[one_at[1]:one_at[1] + 1, :]

    blk = pl.BlockSpec((l, 128), lambda b, hp: (b, hp))
    cspec = lambda k: pl.BlockSpec((None, l, 128), lambda b, hp: (b * N_FOX_HEADS + 2 * hp + k, 0, 0))
    rows2 = pl.BlockSpec((None, 2, l), lambda b, hp: (b * N_PAIRS + hp, 0, 0))
    wide = jax.ShapeDtypeStruct((t, FOX_WIDTH), F32)
    pair_rows = jax.ShapeDtypeStruct((seqs * N_PAIRS, 2, l), F32)
    return pl.pallas_call(
        body, name="fox_bwd", grid=(seqs, N_PAIRS),
        in_specs=[blk, blk, pl.BlockSpec((l, 128), lambda b, hp: (b, V_BLOCK0 + hp)), cspec(0), cspec(1), blk, blk, rows2],
        out_specs=[blk, blk, blk, rows2, rows2],
        out_shape=[wide, wide, wide, pair_rows, pair_rows],
        scratch_shapes=[pltpu.VMEM((128, l), BF16), pltpu.VMEM((128, l), BF16), pltpu.VMEM((128, l), BF16),
                        pltpu.VMEM((2, l), F32), pltpu.VMEM((128, l), F32), pltpu.VMEM((128, l), F32)],
        compiler_params=_params(("parallel", "parallel")),
    )(qn, kn, qkv, c_wide, c_wide, o, do, lse)


SCAN_ROWS = 256
SCAN_COLS = 1024


S5_IN = 128
S5_ST = 512
SCAN_CHUNKS = SCAN_COLS // S5_ST
SCAN_SEGS = 8
LANES = 128


def _cmul(ar, ai, br, bi):
    return ar * br - ai * bi, ar * bi + ai * br


def _powers_into(pw_r, pw_i, a_r, a_i, seg):
    pw_r[0:1, :] = a_r
    pw_i[0:1, :] = a_i
    for k in range(1, seg):
        pr, pi = _cmul(pw_r[k - 1:k, :], pw_i[k - 1:k, :], a_r, a_i)
        pw_r[k:k + 1, :] = pr
        pw_i[k:k + 1, :] = pi


def _interleave(dst, src, seg):
    for h in range(src.shape[0]):
        for j in range(seg):
            dst[h, j * SCAN_SEGS:(j + 1) * SCAN_SEGS, :] = src[h, pl.ds(j, SCAN_SEGS, stride=seg), :]


def _deinterleave(dst, src, seg):
    for h in range(src.shape[0]):
        for j in range(seg):
            dst[h, pl.ds(j, SCAN_SEGS, stride=seg), :] = src[h, j * SCAN_SEGS:(j + 1) * SCAN_SEGS, :]


def _interleaved(ref, tmp_a, tmp_b, seg):
    n = ref.shape[1] // LANES
    for h in range(n):
        tmp_a[h] = ref[:, h * LANES:(h + 1) * LANES].astype(F32)
    _interleave(tmp_b, tmp_a, seg)
    return jnp.concatenate([tmp_b[h] for h in range(n)], axis=1)


def _store_deinterleaved(ref, val, tmp_a, tmp_b, seg):
    n = ref.shape[1] // LANES
    for h in range(n):
        tmp_a[h] = val[:, h * LANES:(h + 1) * LANES]
    _deinterleave(tmp_b, tmp_a, seg)
    for h in range(n):
        ref[:, h * LANES:(h + 1) * LANES] = tmp_b[h]


def _segment_scan(b_r, b_i, x_r, x_i, pw_r, pw_i, car_r, car_i, seg, sign, reverse, visit=None):
    nc = b_r.shape[0]
    sub = lax.broadcasted_iota(jnp.int32, (SCAN_SEGS, LANES), 0)
    lanes = lambda c: slice(c * LANES, (c + 1) * LANES)
    rows = lambda j: pl.ds(pl.multiple_of(((seg - 1 - j) if reverse else j) * SCAN_SEGS, SCAN_SEGS), SCAN_SEGS)
    a1 = [(pw_r[0:1, lanes(c)], sign * pw_i[0:1, lanes(c)]) for c in range(nc)]

    def local(j, xs):
        out = []
        for c in range(nc):
            xr, xi = xs[2 * c], xs[2 * c + 1]
            nr = a1[c][0] * xr - a1[c][1] * xi + b_r[c, rows(j), :]
            ni = a1[c][0] * xi + a1[c][1] * xr + b_i[c, rows(j), :]
            x_r[c, rows(j), :] = nr
            x_i[c, rows(j), :] = ni
            out += [nr, ni]
        return tuple(out)

    zero = jnp.zeros((SCAN_SEGS, LANES), F32)
    ends = lax.fori_loop(0, seg, local, (zero,) * (2 * nc))

    if reverse:
        first = sub == SCAN_SEGS - 1
        neighbour = lambda v: pltpu.roll(v, SCAN_SEGS - 1, 0)
        shift = lambda v, d: jnp.where(sub < SCAN_SEGS - d, pltpu.roll(v, SCAN_SEGS - d, 0), 0.0)
    else:
        first = sub == 0
        neighbour = lambda v: pltpu.roll(v, 1, 0)
        shift = lambda v, d: jnp.where(sub >= d, pltpu.roll(v, d, 0), 0.0)
    last = 0 if reverse else SCAN_SEGS - 1
    entries = []
    for c in range(nc):
        er, ei = ends[2 * c], ends[2 * c + 1]
        pr, pi = pw_r[seg - 1:seg, lanes(c)], sign * pw_i[seg - 1:seg, lanes(c)]
        yr = jnp.where(first, car_r[:, lanes(c)], neighbour(er))
        yi = jnp.where(first, car_i[:, lanes(c)], neighbour(ei))
        qr, qi = pr, pi
        for d in (1, 2, 4):
            mr, mi = _cmul(qr, qi, shift(yr, d), shift(yi, d))
            yr, yi = yr + mr, yi + mi
            qr, qi = _cmul(qr, qi, qr, qi)
        lr, li = _cmul(pr, pi, yr, yi)
        car_r[:, lanes(c)] = (er + lr)[last:last + 1, :]
        car_i[:, lanes(c)] = (ei + li)[last:last + 1, :]
        entries += [yr, yi]

    def correct(j, prev):
        out = []
        row_r, row_i = pw_r[pl.ds(j, 1), :], sign * pw_i[pl.ds(j, 1), :]
        for c in range(nc):
            mr, mi = _cmul(row_r[:, lanes(c)], row_i[:, lanes(c)], entries[2 * c], entries[2 * c + 1])
            nr = x_r[c, rows(j), :] + mr
            ni = x_i[c, rows(j), :] + mi
            x_r[c, rows(j), :] = nr
            x_i[c, rows(j), :] = ni
            if visit is not None:
                visit(c, rows(j), prev[2 * c], prev[2 * c + 1])
            out += [nr, ni]
        return tuple(out)

    lax.fori_loop(0, seg, correct, tuple(entries))


def _s5_fwd(uf, bbr, bbi, cr, ci, ar, ai, seqs):
    t = uf.shape[0]
    l = t // seqs
    tl = min(SCAN_ROWS, l)
    nl = l // tl
    seg = tl // SCAN_SEGS
    cb, nq = SCAN_COLS, SCAN_CHUNKS
    nc = cb // LANES
    per = S5_ST // LANES

    def body(u_ref, bbr_ref, bbi_ref, cr_ref, ci_ref, ar_ref, ai_ref, xr_ref, xi_ref, ys_ref,
             car_r, car_i, pw_r, pw_i, b_r, b_i, x_r, x_i, tmp_a, tmp_b):
        @pl.when(pl.program_id(2) == 0)
        def _():
            car_r[...] = jnp.zeros(car_r.shape, F32)
            car_i[...] = jnp.zeros(car_i.shape, F32)
            _powers_into(pw_r, pw_i, ar_ref[...], ai_ref[...], seg)

        u = _interleaved(u_ref, tmp_a, tmp_b, seg).astype(BF16)
        for q in range(nq):
            uq = u[:, q * S5_IN:(q + 1) * S5_IN]
            br = jnp.dot(uq, bbr_ref[q], preferred_element_type=F32)
            bi = jnp.dot(uq, bbi_ref[q], preferred_element_type=F32)
            for s in range(per):
                b_r[q * per + s] = br[:, s * LANES:(s + 1) * LANES]
                b_i[q * per + s] = bi[:, s * LANES:(s + 1) * LANES]
        _segment_scan(b_r, b_i, x_r, x_i, pw_r, pw_i, car_r, car_i, seg, 1.0, False)
        for c in range(nc):
            xr_ref[:, c * LANES:(c + 1) * LANES] = x_r[c]
            xi_ref[:, c * LANES:(c + 1) * LANES] = x_i[c]
        ys = []
        for q in range(nq):
            xq_r = xr_ref[:, q * S5_ST:(q + 1) * S5_ST].astype(BF16)
            xq_i = xi_ref[:, q * S5_ST:(q + 1) * S5_ST].astype(BF16)
            ys.append(jnp.dot(xq_r, cr_ref[q], preferred_element_type=F32)
                      + jnp.dot(xq_i, ci_ref[q], preferred_element_type=F32))
        _store_deinterleaved(ys_ref, jnp.concatenate(ys, axis=1), tmp_a, tmp_b, seg)

    rows = lambda w: pl.BlockSpec((tl, w), lambda s, j, r: (s * nl + r, j))
    chunk = lambda a: pl.BlockSpec((nq,) + a.shape[1:], lambda s, j, r: (j, 0, 0))
    par = pl.BlockSpec((1, cb), lambda s, j, r: (0, j))
    return pl.pallas_call(
        body, name="s5_fwd", grid=(seqs, S5_CH // cb, nl),
        in_specs=[rows(nq * S5_IN), chunk(bbr), chunk(bbi), chunk(cr), chunk(ci), par, par],
        out_specs=[rows(cb), rows(cb), rows(nq * S5_IN)],
        out_shape=[jax.ShapeDtypeStruct((t, S5_CH), F32)] * 2 + [jax.ShapeDtypeStruct((t, S5_WIDTH), F32)],
        scratch_shapes=[pltpu.VMEM((1, cb), F32), pltpu.VMEM((1, cb), F32), pltpu.VMEM((seg, cb), F32),
                        pltpu.VMEM((seg, cb), F32)] + [pltpu.VMEM((nc, tl, LANES), F32)] * 4
        + [pltpu.VMEM((nq * S5_IN // LANES, tl, LANES), F32)] * 2,
        compiler_params=_params(("parallel", "parallel", "arbitrary")),
    )(uf, bbr, bbi, cr, ci, ar, ai)


def _s5_bwd(dys, uf, xr, xi, bbr, bbi, cr, ci, ar, ai, seqs):
    t = dys.shape[0]
    l = t // seqs
    tl = min(SCAN_ROWS, l)
    nl = l // tl
    seg = tl // SCAN_SEGS
    cb, nq = SCAN_COLS, SCAN_CHUNKS
    nc = cb // LANES
    per = S5_ST // LANES

    def body(dy_ref, u_ref, xr_ref, xi_ref, bbr_ref, bbi_ref, cr_ref, ci_ref, ar_ref, ai_ref,
             du_ref, dbbr_ref, dbbi_ref, dcr_ref, dci_ref, dar_ref, dai_ref,
             car_r, car_i, pw_r, pw_i, g_r, g_i, lam_r, lam_i, x_r, x_i, acc_r, acc_i, tmp_a, tmp_b):
        @pl.when(pl.program_id(2) == 0)
        def _():
            car_r[...] = jnp.zeros(car_r.shape, F32)
            car_i[...] = jnp.zeros(car_i.shape, F32)
            _powers_into(pw_r, pw_i, ar_ref[...], ai_ref[...], seg)
            for acc_ref in (dbbr_ref, dbbi_ref, dcr_ref, dci_ref, dar_ref, dai_ref):
                acc_ref[...] = jnp.zeros(acc_ref.shape, F32)

        dy = _interleaved(dy_ref, tmp_a, tmp_b, seg).astype(BF16)
        for q in range(nq):
            dyq = dy[:, q * S5_IN:(q + 1) * S5_IN]
            gr = lax.dot_general(dyq, cr_ref[q], _NT, preferred_element_type=F32)
            gi = lax.dot_general(dyq, ci_ref[q], _NT, preferred_element_type=F32)
            for s in range(per):
                g_r[q * per + s] = gr[:, s * LANES:(s + 1) * LANES]
                g_i[q * per + s] = gi[:, s * LANES:(s + 1) * LANES]
        for c in range(nc):
            x_r[c] = xr_ref[:, c * LANES:(c + 1) * LANES]
            x_i[c] = xi_ref[:, c * LANES:(c + 1) * LANES]
        acc_r[...] = jnp.zeros(acc_r.shape, F32)
        acc_i[...] = jnp.zeros(acc_i.shape, F32)

        def visit(c, rws, lr, li):
            xr_t, xi_t = x_r[c, rws, :], x_i[c, rws, :]
            acc_r[c] += lr * xr_t + li * xi_t
            acc_i[c] += li * xr_t - lr * xi_t

        _segment_scan(g_r, g_i, lam_r, lam_i, pw_r, pw_i, car_r, car_i, seg, -1.0, True, visit)
        for c in range(nc):
            dar_ref[:, c * LANES:(c + 1) * LANES] += jnp.sum(acc_r[c], axis=0, keepdims=True)
            dai_ref[:, c * LANES:(c + 1) * LANES] += jnp.sum(acc_i[c], axis=0, keepdims=True)
        u = _interleaved(u_ref, tmp_a, tmp_b, seg).astype(BF16)
        du = []
        for q in range(nq):
            st = slice(q * S5_ST, (q + 1) * S5_ST)
            io = slice(q * S5_IN, (q + 1) * S5_IN)
            lq_r = jnp.concatenate([lam_r[q * per + s] for s in range(per)], axis=1).astype(BF16)
            lq_i = jnp.concatenate([lam_i[q * per + s] for s in range(per)], axis=1).astype(BF16)
            du.append(lax.dot_general(lq_r, bbr_ref[q], _NT, preferred_element_type=F32)
                      + lax.dot_general(lq_i, bbi_ref[q], _NT, preferred_element_type=F32))
            dbbr_ref[q] += lax.dot_general(u[:, io], lq_r, _TN, preferred_element_type=F32)
            dbbi_ref[q] += lax.dot_general(u[:, io], lq_i, _TN, preferred_element_type=F32)
            dcr_ref[q] += lax.dot_general(xr_ref[:, st].astype(BF16), dy[:, io], _TN, preferred_element_type=F32)
            dci_ref[q] += lax.dot_general(xi_ref[:, st].astype(BF16), dy[:, io], _TN, preferred_element_type=F32)
        _store_deinterleaved(du_ref, jnp.concatenate(du, axis=1), tmp_a, tmp_b, seg)

    rows = lambda w: pl.BlockSpec((tl, w), lambda s, j, r: (s * nl + nl - 1 - r, j))
    chunk = lambda a: pl.BlockSpec((nq,) + a.shape[1:], lambda s, j, r: (j, 0, 0))
    acc = lambda a: pl.BlockSpec((None, nq) + a.shape[1:], lambda s, j, r: (s, j, 0, 0))
    par = pl.BlockSpec((1, cb), lambda s, j, r: (0, j))
    par_acc = pl.BlockSpec((None, 1, cb), lambda s, j, r: (s, 0, j))
    per_seq = lambda a: jax.ShapeDtypeStruct((seqs,) + a.shape, F32)
    return pl.pallas_call(
        body, name="s5_bwd", grid=(seqs, S5_CH // cb, nl),
        in_specs=[rows(nq * S5_IN), rows(nq * S5_IN), rows(cb), rows(cb), chunk(bbr), chunk(bbi), chunk(cr), chunk(ci),
                  par, par],
        out_specs=[rows(nq * S5_IN), acc(bbr), acc(bbi), acc(cr), acc(ci), par_acc, par_acc],
        out_shape=[jax.ShapeDtypeStruct((t, S5_WIDTH), F32), per_seq(bbr), per_seq(bbi), per_seq(cr), per_seq(ci),
                   jax.ShapeDtypeStruct((seqs, 1, S5_CH), F32), jax.ShapeDtypeStruct((seqs, 1, S5_CH), F32)],
        scratch_shapes=[pltpu.VMEM((1, cb), F32), pltpu.VMEM((1, cb), F32), pltpu.VMEM((seg, cb), F32),
                        pltpu.VMEM((seg, cb), F32)] + [pltpu.VMEM((nc, tl, LANES), F32)] * 6
        + [pltpu.VMEM((nc, SCAN_SEGS, LANES), F32)] * 2 + [pltpu.VMEM((nq * S5_IN // LANES, tl, LANES), F32)] * 2,
        compiler_params=_params(("parallel", "parallel", "arbitrary")),
    )(dys, uf, xr, xi, bbr, bbi, cr, ci, ar, ai)


XATT_BLOCK = 2048


def _xatt_probs(qv, kv):
    s = lax.dot_general(qv, kv, _NT, preferred_element_type=F32) * (X_HEAD_DIM ** -0.5)
    e = jnp.exp(s - jnp.max(s, axis=-1, keepdims=True))
    return e / jnp.sum(e, axis=-1, keepdims=True)


def _xatt_fwd(q, k, kv, seqs):
    t = q.shape[0]
    tq = min(XATT_BLOCK, t // seqs)
    nq = t // seqs // tq

    def body(q_ref, k_ref, v_ref, o_ref):
        p = _xatt_probs(q_ref[...], k_ref[...])
        o_ref[...] = jnp.dot(p.astype(BF16), v_ref[...].astype(BF16), preferred_element_type=F32).astype(o_ref.dtype)

    qs = pl.BlockSpec((tq, X_HEAD_DIM), lambda b, h, i: (b * nq + i, h))
    return pl.pallas_call(
        body, name="xatt_fwd", grid=(seqs, N_X_HEADS, nq),
        in_specs=[qs, pl.BlockSpec((N_MEM, X_HEAD_DIM), lambda b, h, i: (b, h)),
                  pl.BlockSpec((N_MEM, X_HEAD_DIM), lambda b, h, i: (b, N_X_HEADS + h))],
        out_specs=qs, out_shape=jax.ShapeDtypeStruct(q.shape, BF16),
        compiler_params=_params(("parallel", "parallel", "parallel")),
    )(q, k, kv)


def _xatt_bwd(q, k, kv, do, seqs):
    t = q.shape[0]
    tq = min(XATT_BLOCK, t // seqs)
    nq = t // seqs // tq
    scale = X_HEAD_DIM ** -0.5

    def body(q_ref, k_ref, v_ref, do_ref, dq_ref, dk_ref, dv_ref):
        @pl.when(pl.program_id(2) == 0)
        def _():
            dk_ref[...] = jnp.zeros(dk_ref.shape, F32)
            dv_ref[...] = jnp.zeros(dv_ref.shape, F32)

        qv, kk = q_ref[...], k_ref[...]
        p = _xatt_probs(qv, kk)
        dob = do_ref[...].astype(BF16)
        dp = lax.dot_general(dob, v_ref[...].astype(BF16), _NT, preferred_element_type=F32)
        ds = p * (dp - jnp.sum(dp * p, axis=-1, keepdims=True))
        dsb = ds.astype(BF16)
        dq_ref[...] = jnp.dot(dsb, kk, preferred_element_type=F32) * scale
        dk_ref[...] += lax.dot_general(dsb, qv, _TN, preferred_element_type=F32) * scale
        dv_ref[...] += lax.dot_general(p.astype(BF16), dob, _TN, preferred_element_type=F32)

    qs = pl.BlockSpec((tq, X_HEAD_DIM), lambda b, h, i: (b * nq + i, h))
    ks = pl.BlockSpec((N_MEM, X_HEAD_DIM), lambda b, h, i: (b, h))
    return pl.pallas_call(
        body, name="xatt_bwd", grid=(seqs, N_X_HEADS, nq),
        in_specs=[qs, ks, pl.BlockSpec((N_MEM, X_HEAD_DIM), lambda b, h, i: (b, N_X_HEADS + h)), qs],
        out_specs=[qs, ks, ks],
        out_shape=[jax.ShapeDtypeStruct(q.shape, F32), jax.ShapeDtypeStruct(k.shape, F32),
                   jax.ShapeDtypeStruct(k.shape, F32)],
        compiler_params=_params(("parallel", "parallel", "arbitrary")),
    )(q, k, kv, do)


CONV_COLS = 256


def _shift_down(x, k, row):
    return jnp.where(row >= k, pltpu.roll(x, k, 0), 0.0)


def _shift_up(x, k, row):
    n = x.shape[0]
    return jnp.where(row < n - k, pltpu.roll(x, n - k, 0), 0.0)


def _conv_pre(g, w, b, row):
    return b + w[0:1, :] * _shift_down(g, 2, row) + w[1:2, :] * _shift_down(g, 1, row) + w[2:3, :] * g


def _convgate_fwd(gu, w, b, seqs):
    t = gu.shape[0]
    l = t // seqs
    nc = D_FF // CONV_COLS

    def body(g_ref, u_ref, w_ref, b_ref, o_ref):
        g = g_ref[...].astype(F32)
        row = lax.broadcasted_iota(jnp.int32, g.shape, 0)
        pre = _conv_pre(g, w_ref[...], b_ref[...], row)
        o_ref[...] = (pre * jax.nn.sigmoid(pre) * u_ref[...].astype(F32)).astype(o_ref.dtype)

    return pl.pallas_call(
        body, name="convgate_fwd", grid=(seqs, nc),
        in_specs=[pl.BlockSpec((l, CONV_COLS), lambda s, j: (s, j)), pl.BlockSpec((l, CONV_COLS), lambda s, j: (s, nc + j)),
                  pl.BlockSpec((3, CONV_COLS), lambda s, j: (0, j)), pl.BlockSpec((1, CONV_COLS), lambda s, j: (0, j))],
        out_specs=pl.BlockSpec((l, CONV_COLS), lambda s, j: (s, j)),
        out_shape=jax.ShapeDtypeStruct((t, D_FF), BF16),
        compiler_params=_params(("parallel", "parallel")),
    )(gu, gu, w, b)


def _convgate_bwd(gu, w, b, dact, seqs):
    t = gu.shape[0]
    l = t // seqs
    nc = D_FF // CONV_COLS
    steps = nc * seqs

    def body(g_ref, u_ref, w_ref, b_ref, da_ref, dgu_ref, dw_ref, db_ref, stage, sems):
        j, s = pl.program_id(0), pl.program_id(1)
        n = j * seqs + s
        slot = n % 2

        def copies(slot_, j_, s_):
            rows = pl.ds(pl.multiple_of(s_ * l, 16), l)
            return [pltpu.make_async_copy(
                stage.at[slot_, half],
                dgu_ref.at[rows, pl.ds(pl.multiple_of((half * nc + j_) * CONV_COLS, 128), CONV_COLS)],
                sems.at[slot_, half]) for half in (0, 1)]

        @pl.when(s == 0)
        def _():
            dw_ref[...] = jnp.zeros(dw_ref.shape, F32)
            db_ref[...] = jnp.zeros(db_ref.shape, F32)

        @pl.when(n >= 2)
        def _():
            for cp in copies(slot, j, s):
                cp.wait()

        g, wv, da = g_ref[...].astype(F32), w_ref[...], da_ref[...].astype(F32)
        row = lax.broadcasted_iota(jnp.int32, g.shape, 0)
        g1, g2 = _shift_down(g, 1, row), _shift_down(g, 2, row)
        pre = b_ref[...] + wv[0:1, :] * g2 + wv[1:2, :] * g1 + wv[2:3, :] * g
        sg = jax.nn.sigmoid(pre)
        silu = pre * sg
        stage[slot, 1] = (da * silu).astype(stage.dtype)
        dpre = da * u_ref[...].astype(F32) * (sg * (1.0 + pre * (1.0 - sg)))
        dg = wv[2:3, :] * dpre + wv[1:2, :] * _shift_up(dpre, 1, row) + wv[0:1, :] * _shift_up(dpre, 2, row)
        stage[slot, 0] = dg.astype(stage.dtype)
        for cp in copies(slot, j, s):
            cp.start()
        dw_ref[0:1, :] += jnp.sum(dpre * g2, axis=0, keepdims=True)
        dw_ref[1:2, :] += jnp.sum(dpre * g1, axis=0, keepdims=True)
        dw_ref[2:3, :] += jnp.sum(dpre * g, axis=0, keepdims=True)
        db_ref[...] += jnp.sum(dpre, axis=0, keepdims=True)

        @pl.when(n == steps - 1)
        def _():
            for cp in copies(slot, j, s) + (copies(1 - slot, j, s) if steps > 1 else []):
                cp.wait()

    blk = lambda off: pl.BlockSpec((l, CONV_COLS), lambda j, s: (s, off + j))
    return pl.pallas_call(
        body, name="convgate_bwd", grid=(nc, seqs),
        in_specs=[blk(0), blk(nc), pl.BlockSpec((3, CONV_COLS), lambda j, s: (0, j)),
                  pl.BlockSpec((1, CONV_COLS), lambda j, s: (0, j)), blk(0)],
        out_specs=[ANY, pl.BlockSpec((3, CONV_COLS), lambda j, s: (0, j)),
                   pl.BlockSpec((1, CONV_COLS), lambda j, s: (0, j))],
        out_shape=[jax.ShapeDtypeStruct((t, 2 * D_FF), BF16), jax.ShapeDtypeStruct((3, D_FF), F32),
                   jax.ShapeDtypeStruct((1, D_FF), F32)],
        scratch_shapes=[pltpu.VMEM((2, 2, l, CONV_COLS), BF16), pltpu.SemaphoreType.DMA((2, 2))],
        compiler_params=_params(("arbitrary", "arbitrary")),
    )(gu, gu, w, b, dact)


def _loss_head(h, target):
    t, d = h.shape
    tm = _pick(t, (256, 128, 8))

    def body(h_ref, t_ref, dh_ref, dhb_ref, loss_ref):
        @pl.when(pl.program_id(0) == 0)
        def _():
            loss_ref[...] = jnp.zeros(loss_ref.shape, F32)

        e = h_ref[...] - t_ref[...]
        dh = e * (1.0 / d)
        dh_ref[...] = dh
        dhb_ref[...] = dh.astype(BF16)
        loss_ref[...] += (0.5 / d) * jnp.sum(jnp.sum(e * e, axis=1, keepdims=True), axis=0, keepdims=True)

    blk = pl.BlockSpec((tm, d), lambda i: (i, 0))
    return pl.pallas_call(
        body, name="loss_head", grid=(t // tm,), in_specs=[blk, blk],
        out_specs=[blk, blk, pl.BlockSpec((1, 1), lambda i: (0, 0))],
        out_shape=[jax.ShapeDtypeStruct((t, d), F32), jax.ShapeDtypeStruct((t, d), BF16),
                   jax.ShapeDtypeStruct((1, 1), F32)],
        compiler_params=_params(("arbitrary",)),
    )(h, target)


def _s5_discretise(a_re, a_im, log_dt, b_re, b_im):
    dt = jnp.exp(log_dt)[:, None]
    mag = jnp.exp(a_re * dt)
    lb_r = mag * jnp.cos(a_im * dt)
    lb_i = mag * jnp.sin(a_im * dt)
    den = a_re * a_re + a_im * a_im
    nr = lb_r - 1.0
    coef_r = (nr * a_re + lb_i * a_im) / den
    coef_i = (lb_i * a_re - nr * a_im) / den
    bb_r = coef_r[:, :, None] * b_re - coef_i[:, :, None] * b_im
    bb_i = coef_r[:, :, None] * b_im + coef_i[:, :, None] * b_re
    return lb_r, lb_i, bb_r, bb_i


S5_CHUNKS = 4
S5_PER = S5_GROUPS // S5_CHUNKS


def _blockdiag_in(bb):
    eye = jnp.eye(S5_PER, dtype=bb.dtype)
    return jnp.einsum("jgpc,gh->jgchp", bb.reshape(S5_CHUNKS, S5_PER, S5_STATE, S5_GROUP_CH), eye).reshape(
        S5_CHUNKS, S5_PER * S5_GROUP_CH, S5_PER * S5_STATE)


def _blockdiag_in_grad(d):
    eye = jnp.eye(S5_PER, dtype=d.dtype)
    return jnp.einsum("jgchp,gh->jgpc", d.reshape(S5_CHUNKS, S5_PER, S5_GROUP_CH, S5_PER, S5_STATE), eye).reshape(
        S5_GROUPS, S5_STATE, S5_GROUP_CH)


def _blockdiag_out(c):
    eye = jnp.eye(S5_PER, dtype=c.dtype)
    return jnp.einsum("jgcp,gh->jgphc", c.reshape(S5_CHUNKS, S5_PER, S5_GROUP_CH, S5_STATE), eye).reshape(
        S5_CHUNKS, S5_PER * S5_STATE, S5_PER * S5_GROUP_CH)


def _blockdiag_out_grad(d):
    eye = jnp.eye(S5_PER, dtype=d.dtype)
    return jnp.einsum("jgphc,gh->jgcp", d.reshape(S5_CHUNKS, S5_PER, S5_STATE, S5_PER, S5_GROUP_CH), eye).reshape(
        S5_GROUPS, S5_GROUP_CH, S5_STATE)


def _local_step(x3, mem3, target3, p, wb, late_weights=None, early_grads=None):
    seqs, l, d = x3.shape
    t = seqs * l
    x = x3.reshape(t, d)
    mem = mem3.reshape(seqs * N_MEM, d)
    target = target3.reshape(t, d)
    full = lambda a: (a, a.shape[1], 0, 0)

    s5_in = (p["s5_a_re"], p["s5_a_im"], p["s5_log_dt"], p["s5_b_re"], p["s5_b_im"])
    (lb_r, lb_i, bb_r, bb_i), s5_pull = jax.vjp(_s5_discretise, *s5_in)
    ar, ai = lb_r.reshape(1, S5_CH), lb_i.reshape(1, S5_CH)
    bbr_d, bbi_d = _blockdiag_in(bb_r).astype(BF16), _blockdiag_in(bb_i).astype(BF16)
    cr_d, ci_d = _blockdiag_out(p["s5_c_re"]).astype(BF16), (-_blockdiag_out(p["s5_c_im"])).astype(BF16)
    d_row = p["s5_d"].reshape(1, S5_WIDTH)

    w_in = wb["w_in"]
    w_qkv = w_in[:, :3 * FOX_WIDTH]
    w_uf = jnp.concatenate(
        [w_in[:, 3 * FOX_WIDTH + N_FOX_HEADS:], w_in[:, 3 * FOX_WIDTH:3 * FOX_WIDTH + N_FOX_HEADS],
         jnp.zeros((d, UF_COLS - S5_WIDTH - N_FOX_HEADS), w_in.dtype)], axis=1)

    hn1 = _rowwise(_rms, [full(x)], [p["norm_mix"]], [(d, d, 0, BF16)], "norm_mix_fwd")
    qkv = _mm(hn1, w_qkv, "nn", "in_qkv")
    uf = _mm(hn1, w_uf, "nn", "in_uf")

    bh = seqs * N_FOX_HEADS
    q_pair = (qkv, 128, 0, 1)
    k_pair = (qkv, 128, N_PAIRS, 1)
    gq2, gk2 = jnp.tile(p["fox_q_norm"], (1, 2)), jnp.tile(p["fox_k_norm"], (1, 2))
    pair_out = [(FOX_WIDTH, 128, 1, BF16)]
    qn = _rowwise(_rms_pair, [q_pair], [gq2], pair_out, "fox_qnorm_fwd", heads=N_PAIRS)
    kn = _rowwise(_rms_pair, [k_pair], [gk2], pair_out, "fox_knorm_fwd", heads=N_PAIRS)

    f_rows = uf[:, S5_WIDTH:S5_WIDTH + N_FOX_HEADS].reshape(seqs, l, N_FOX_HEADS).transpose(0, 2, 1).reshape(bh, l)
    f_bias = jnp.tile(p["fox_f_bias"].reshape(N_FOX_HEADS, 1), (seqs, 1))
    c_wide = jnp.broadcast_to(_forget_fwd(f_rows, f_bias)[:, :, None], (bh, l, 128))
    fox, lse = _fox_fwd(qn, kn, qkv, c_wide, seqs)

    xr, xi, ys = _s5_fwd(uf, bbr_d, bbi_d, cr_d, ci_d, ar, ai, seqs)
    u_blk = (uf, S5_WIDTH, 0, 0)
    yg = _rowwise(_s5_act, [full(ys), u_blk], [d_row], [(S5_WIDTH, S5_WIDTH, 0, F32)], "s5_act_fwd")
    if late_weights is not None:
        wb = dict(wb, **late_weights("mid", yg))
    z = _mm(yg, wb["s5_w_glu"], "nn", "s5_glu")
    y2n = _rowwise(_s5_gate, [full(yg), full(z)], [p["s5_b_glu"], p["out_norm_s5"]],
                   [(S5_WIDTH, S5_WIDTH, 0, BF16)], "s5_gate_fwd")
    foxn = _rowwise(_rms, [full(fox)], [p["out_norm_fox"]], [(FOX_WIDTH, FOX_WIDTH, 0, BF16)], "fox_outnorm_fwd")
    mixed = jnp.concatenate([foxn, y2n], axis=1)
    h1 = _mm(mixed, wb["w_out"], "nn", "mix_out", res=x)
    if late_weights is not None:
        wb = dict(wb, **late_weights("late", h1))

    hn2 = _rowwise(_rms, [full(h1)], [p["norm_cross"]], [(d, d, 0, BF16)], "norm_cross_fwd")
    mn = _rowwise(_rms, [full(mem)], [p["norm_mem"]], [(d, d, 0, BF16)], "norm_mem_fwd")
    xq_raw = _mm(hn2, wb["w_xq"], "nn", "x_q")
    kv = _mm(mn, wb["w_xkv"], "nn", "x_kv")
    xh = lambda a: (a, X_HEAD_DIM, 0, 1)
    xqn = _rowwise(_rms, [xh(xq_raw)], [p["xq_norm"]], [(d, X_HEAD_DIM, 1, BF16)], "x_qnorm_fwd", heads=N_X_HEADS)
    xkn = _rowwise(_rms, [xh(kv)], [p["xk_norm"]], [(d, X_HEAD_DIM, 1, BF16)], "x_knorm_fwd", heads=N_X_HEADS)
    xo = _xatt_fwd(xqn, xkn, kv, seqs)
    h2 = _mm(xo, wb["w_xo"], "nn", "x_out", res=h1)

    hn3 = _rowwise(_rms, [full(h2)], [p["norm_ffn"]], [(d, d, 0, BF16)], "norm_ffn_fwd")
    gu = _mm(hn3, wb["w_ffn_up"], "nn", "ffn_up", out_dtype=BF16)
    act = _convgate_fwd(gu, p["ffn_conv_w"], p["ffn_conv_b"], seqs)
    h3 = _mm(act, wb["w_ffn_down"], "nn", "ffn_down", res=h2)
    dh3, dh3_b, loss = _loss_head(h3, target)

    g = {}
    dact = _mm(dh3_b, wb["w_ffn_down"], "nt", "ffn_down_dx", out_dtype=BF16)
    late_dt = BF16 if early_grads is not None else F32
    g["w_ffn_down"] = _mm(act, dh3_b, "tn", "ffn_down_dw", out_dtype=late_dt)
    dgu, g["ffn_conv_w"], g["ffn_conv_b"] = _convgate_bwd(gu, p["ffn_conv_w"], p["ffn_conv_b"], dact, seqs)
    dhn3 = _mm(dgu, wb["w_ffn_up"], "nt", "ffn_up_dx")
    g["w_ffn_up"] = _mm(hn3, dgu, "tn", "ffn_up_dw", out_dtype=late_dt)
    (dh2,), (g["norm_ffn"],) = _rowwise_vjp(_rms, [full(h2)], [p["norm_ffn"]], [full(dhn3)], "norm_ffn_bwd",
                                            adds=[full(dh3)])

    dxo = _mm(dh2, wb["w_xo"], "nt", "x_out_dx")
    g["w_xo"] = _mm(xo, dh2, "tn", "x_out_dw", out_dtype=late_dt)
    dxqn, dxkn, dxv = _xatt_bwd(xqn, xkn, kv, dxo, seqs)
    (dxq_raw,), (g["xq_norm"],) = _rowwise_vjp(_rms, [xh(xq_raw)], [p["xq_norm"]], [xh(dxqn)], "x_qnorm_bwd",
                                               heads=N_X_HEADS, row_dtypes=[BF16])
    (dxk_raw,), (g["xk_norm"],) = _rowwise_vjp(_rms, [xh(kv)], [p["xk_norm"]], [xh(dxkn)], "x_knorm_bwd",
                                               heads=N_X_HEADS, row_dtypes=[BF16])
    dkv = jnp.concatenate([dxk_raw, dxv.astype(BF16)], axis=1)
    dhn2 = _mm(dxq_raw, wb["w_xq"], "nt", "x_q_dx")
    g["w_xq"] = _mm(hn2, dxq_raw, "tn", "x_q_dw", out_dtype=late_dt)
    dmn = _mm(dkv, wb["w_xkv"], "nt", "x_kv_dx")
    g["w_xkv"] = _mm(mn, dkv, "tn", "x_kv_dw", out_dtype=late_dt)
    norm_cross = p["norm_cross"]
    if early_grads is not None:
        norm_cross = norm_cross + early_grads({n: g[n] for n in LATE_WEIGHTS})
    (dh1,), (g["norm_cross"],) = _rowwise_vjp(_rms, [full(h1)], [norm_cross], [full(dhn2)], "norm_cross_bwd",
                                              adds=[full(dh2)])
    _, (g["norm_mem"],) = _rowwise_vjp(_rms, [full(mem)], [p["norm_mem"]], [full(dmn)], "norm_mem_bwd",
                                       row_dtypes=[BF16])

    dmixed = _mm(dh1, wb["w_out"], "nt", "mix_out_dx")
    g["w_out"] = _mm(mixed, dh1, "tn", "mix_out_dw", out_dtype=late_dt)
    (dfox,), (g["out_norm_fox"],) = _rowwise_vjp(_rms, [full(fox)], [p["out_norm_fox"]],
                                                 [(dmixed, FOX_WIDTH, 0, 0)], "fox_outnorm_bwd")
    (dyg_a, dz), (g["s5_b_glu"], g["out_norm_s5"]) = _rowwise_vjp(
        _s5_gate, [full(yg), full(z)], [p["s5_b_glu"], p["out_norm_s5"]], [(dmixed, S5_WIDTH, 1, 0)], "s5_gate_bwd",
        row_dtypes=[F32, BF16])
    dyg = _mm(dz, wb["s5_w_glu"], "nt", "s5_glu_dx", res=dyg_a)
    g["s5_w_glu"] = _mm(yg, dz, "tn", "s5_glu_dw", out_dtype=late_dt)
    (dys, du_a), (dd_row,) = _rowwise_vjp(_s5_act, [full(ys), u_blk], [d_row], [full(dyg)], "s5_act_bwd",
                                          row_dtypes=[BF16, F32])
    g["s5_d"] = dd_row
    du_b, dbbr_d, dbbi_d, dcr_d, dci_d, dar, dai = _s5_bwd(dys, uf, xr, xi, bbr_d, bbi_d, cr_d, ci_d, ar, ai, seqs)
    dbbr_d, dbbi_d, dcr_d, dci_d = (jnp.sum(a, axis=0) for a in (dbbr_d, dbbi_d, dcr_d, dci_d))
    d_lb_r = jnp.sum(dar, axis=0).reshape(S5_GROUPS, S5_STATE)
    d_lb_i = jnp.sum(dai, axis=0).reshape(S5_GROUPS, S5_STATE)
    g["s5_a_re"], g["s5_a_im"], g["s5_log_dt"], g["s5_b_re"], g["s5_b_im"] = s5_pull(
        (d_lb_r, d_lb_i, _blockdiag_in_grad(dbbr_d), _blockdiag_in_grad(dbbi_d)))
    g["s5_c_re"] = _blockdiag_out_grad(dcr_d)
    g["s5_c_im"] = -_blockdiag_out_grad(dci_d)

    dqn, dkn, dv, dc, dcq = _fox_bwd(qn, kn, qkv, c_wide, fox, dfox, lse, seqs)
    pair = lambda a: (a, 128, 0, 1)
    (dq_raw,), (dgq2,) = _rowwise_vjp(_rms_pair, [q_pair], [gq2], [pair(dqn)], "fox_qnorm_bwd", heads=N_PAIRS,
                                      row_dtypes=[BF16])
    (dk_raw,), (dgk2,) = _rowwise_vjp(_rms_pair, [k_pair], [gk2], [pair(dkn)], "fox_knorm_bwd", heads=N_PAIRS,
                                      row_dtypes=[BF16])
    g["fox_q_norm"] = dgq2[:, :HEAD_DIM] + dgq2[:, HEAD_DIM:]
    g["fox_k_norm"] = dgk2[:, :HEAD_DIM] + dgk2[:, HEAD_DIM:]
    df_rows, dfb = _forget_bwd(f_rows, f_bias, (dc + dcq).reshape(bh, l))
    g["fox_f_bias"] = jnp.sum(dfb.reshape(seqs, N_FOX_HEADS), axis=0)
    df = df_rows.reshape(seqs, N_FOX_HEADS, l).transpose(0, 2, 1).reshape(t, N_FOX_HEADS)
    dqkv = jnp.concatenate([dq_raw, dk_raw, dv.astype(BF16)], axis=1)
    duf = jnp.concatenate([du_a + du_b, df, jnp.zeros((t, UF_COLS - S5_WIDTH - N_FOX_HEADS), F32)],
                          axis=1).astype(BF16)
    dhn1 = _mm(duf, w_uf, "nt", "in_uf_dx", res=_mm(dqkv, w_qkv, "nt", "in_qkv_dx"))
    dw_qkv = _mm(hn1, dqkv, "tn", "in_qkv_dw")
    dw_uf = _mm(hn1, duf, "tn", "in_uf_dw")
    g["w_in"] = jnp.concatenate([dw_qkv, dw_uf[:, S5_WIDTH:S5_WIDTH + N_FOX_HEADS], dw_uf[:, :S5_WIDTH]], axis=1)
    (dx,), (g["norm_mix"],) = _rowwise_vjp(_rms, [full(x)], [p["norm_mix"]], [full(dhn1)], "norm_mix_bwd",
                                           adds=[full(dh1)])
    return loss, dx.reshape(seqs, l, d), g


def _place():
    return lax.axis_index("x"), lax.axis_index("y"), lax.axis_index("c")


def _other_chips(x, y):
    return [(1 - x, y), (x, 1 - y), (1 - x, 1 - y)]


ANY = pl.BlockSpec(memory_space=pl.ANY)


def _gather_weights(shards, col_kind, taps):
    n = len(shards)

    def body(*refs):
        ins, tap_in, outs, tap_out = refs[:n], refs[n], refs[n + 1:2 * n + 1], refs[2 * n + 1]
        ici_send, ici_recv, d2d_send, d2d_recv, own_send, own_recv = refs[2 * n + 2:]
        x, y, c = _place()
        mine = 2 * x + y
        chips = _other_chips(x, y)
        sibling = (x, y, 1 - c)

        def piece(a, s, h):
            r, cs = ins[a].shape
            hr = r // 2
            if col_kind[a]:
                return outs[a].at[pl.ds(pl.multiple_of(h * hr, 16), hr), pl.ds(pl.multiple_of(s * cs, 128), cs)]
            return outs[a].at[pl.ds(pl.multiple_of(s * r + h * hr, 16), hr), :]

        def slab(a, s):
            r, cs = ins[a].shape
            if col_kind[a]:
                return outs[a].at[:, pl.ds(pl.multiple_of(s * cs, 128), cs)]
            return outs[a].at[pl.ds(pl.multiple_of(s * r, 16), r), :]

        def own_half(a, h):
            hr = ins[a].shape[0] // 2
            return ins[a].at[pl.ds(pl.multiple_of(h * hr, 16), hr), :]

        sends = []
        for a in range(n):
            cp = pltpu.make_async_remote_copy(
                src_ref=ins[a], dst_ref=slab(a, mine), send_sem=own_send.at[a], recv_sem=own_recv.at[a],
                device_id=sibling, device_id_type=MESH)
            cp.start()
            sends.append(cp)
        cp = pltpu.make_async_remote_copy(
            src_ref=tap_in, dst_ref=tap_out.at[mine], send_sem=own_send.at[n], recv_sem=own_recv.at[n],
            device_id=sibling, device_id_type=MESH)
        cp.start()
        sends.append(cp)
        for a in range(n):
            for j, (px, py) in enumerate(chips):
                cp = pltpu.make_async_remote_copy(
                    src_ref=own_half(a, c), dst_ref=piece(a, mine, c), send_sem=ici_send.at[3 * a + j],
                    recv_sem=ici_recv.at[3 * a + j], device_id=(px, py, c), device_id_type=MESH)
                cp.start()
                sends.append(cp)
        for j, (px, py) in enumerate(chips):
            cp = pltpu.make_async_remote_copy(
                src_ref=tap_in, dst_ref=tap_out.at[mine], send_sem=ici_send.at[3 * n + j],
                recv_sem=ici_recv.at[3 * n + j], device_id=(px, py, c), device_id_type=MESH)
            cp.start()
            sends.append(cp)
        for a in range(n):
            for j, (px, py) in enumerate(chips):
                got = piece(a, 2 * px + py, c)
                pltpu.make_async_remote_copy(
                    src_ref=got, dst_ref=got, send_sem=ici_send.at[3 * a + j], recv_sem=ici_recv.at[3 * a + j],
                    device_id=(px, py, c), device_id_type=MESH).wait_recv()
                fwd = pltpu.make_async_remote_copy(
                    src_ref=got, dst_ref=got, send_sem=d2d_send.at[3 * a + j], recv_sem=d2d_recv.at[3 * a + j],
                    device_id=(x, y, 1 - c), device_id_type=MESH)
                fwd.start()
                sends.append(fwd)
        for a in range(n):
            for j, (px, py) in enumerate(chips):
                other = piece(a, 2 * px + py, 1 - c)
                pltpu.make_async_remote_copy(
                    src_ref=other, dst_ref=other, send_sem=d2d_send.at[3 * a + j], recv_sem=d2d_recv.at[3 * a + j],
                    device_id=(x, y, 1 - c), device_id_type=MESH).wait_recv()
        for j, (px, py) in enumerate(chips):
            pltpu.make_async_remote_copy(
                src_ref=tap_in, dst_ref=tap_out.at[2 * px + py], send_sem=ici_send.at[3 * n + j],
                recv_sem=ici_recv.at[3 * n + j], device_id=(px, py, c), device_id_type=MESH).wait_recv()
        for a in range(n):
            pltpu.make_async_remote_copy(
                src_ref=ins[a], dst_ref=slab(a, mine), send_sem=own_send.at[a], recv_sem=own_recv.at[a],
                device_id=sibling, device_id_type=MESH).wait_recv()
        pltpu.make_async_remote_copy(
            src_ref=tap_in, dst_ref=tap_out.at[mine], send_sem=own_send.at[n], recv_sem=own_recv.at[n],
            device_id=sibling, device_id_type=MESH).wait_recv()
        for cp in sends:
            cp.wait_send()

    def full_shape(a):
        r, cs = shards[a].shape
        return (r, 4 * cs) if col_kind[a] else (4 * r, cs)

    res = pl.pallas_call(
        body, name="gather_weights", in_specs=[ANY] * (n + 1), out_specs=[ANY] * (n + 1),
        out_shape=[jax.ShapeDtypeStruct(full_shape(a), shards[a].dtype) for a in range(n)]
        + [jax.ShapeDtypeStruct((4,) + taps.shape, taps.dtype)],
        scratch_shapes=[pltpu.SemaphoreType.DMA((3 * n + 3,)), pltpu.SemaphoreType.DMA((3 * n + 3,)),
                        pltpu.SemaphoreType.DMA((3 * n,)), pltpu.SemaphoreType.DMA((3 * n,)),
                        pltpu.SemaphoreType.DMA((n + 1,)), pltpu.SemaphoreType.DMA((n + 1,))],
        compiler_params=pltpu.CompilerParams(has_side_effects=True),
    )(*shards, taps)
    return res[:n], res[n]


HBM = pl.BlockSpec(memory_space=pltpu.HBM)
SEM = pl.BlockSpec(memory_space=pltpu.SEMAPHORE)
DATAFLOW = pltpu.SideEffectType.DATAFLOW_SIDE_EFFECTING


def _in_hbm(a):
    return pltpu.with_memory_space_constraint(a, pltpu.HBM)


def _split_start(name, srcs, lands, n_copies, plan):
    n = len(srcs)

    def body(*refs):
        src_refs, land_refs = refs[:n], refs[n:2 * n]
        send_sems, recv_sems = refs[2 * n], refs[2 * n + 1]
        for i, (src, dst, dev) in enumerate(plan(src_refs, land_refs)):
            pltpu.make_async_remote_copy(src_ref=src, dst_ref=dst, send_sem=send_sems.at[i], recv_sem=recv_sems.at[i],
                                         device_id=dev, device_id_type=MESH).start()
        refs[-1][...] = jnp.zeros((8, 128), F32)

    res = pl.pallas_call(
        body, name=name, in_specs=[HBM] * (2 * n),
        out_specs=[SEM, SEM] + [HBM] * (2 * n) + [pl.BlockSpec(memory_space=pltpu.VMEM)],
        out_shape=[pltpu.SemaphoreType.DMA((n_copies,)), pltpu.SemaphoreType.DMA((n_copies,))]
        + [pltpu.HBM(a.shape, a.dtype) for a in list(srcs) + list(lands)] + [jax.ShapeDtypeStruct((8, 128), F32)],
        input_output_aliases={i: 2 + i for i in range(2 * n)},
        compiler_params=pltpu.CompilerParams(has_side_effects=DATAFLOW),
    )(*[_in_hbm(a) for a in list(srcs) + list(lands)])
    return res[0], res[1], list(res[2:2 + n]), list(res[2 + n:2 + 2 * n]), res[-1]


def _split_wait(name, send_sems, recv_sems, srcs, lands, after, plan):
    n = len(srcs)

    def body(*refs):
        src_refs, land_refs = refs[:n], refs[n:2 * n]
        send_ref, recv_ref = refs[2 * n], refs[2 * n + 1]
        for i, (src, dst, dev) in enumerate(plan(src_refs, land_refs)):
            cp = pltpu.make_async_remote_copy(src_ref=src, dst_ref=dst, send_sem=send_ref.at[i], recv_sem=recv_ref.at[i],
                                              device_id=dev, device_id_type=MESH)
            cp.wait_send()
            cp.wait_recv()

    res = pl.pallas_call(
        body, name=name, in_specs=[HBM] * (2 * n) + [SEM, SEM, ANY], out_specs=[HBM] * (2 * n),
        out_shape=[pltpu.HBM(a.shape, a.dtype) for a in list(srcs) + list(lands)],
        input_output_aliases={i: i for i in range(2 * n)},
        compiler_params=pltpu.CompilerParams(has_side_effects=DATAFLOW),
    )(*srcs, *lands, send_sems, recv_sems, after)
    return list(res[:n]), list(res[n:])


def _late_gather_plan(col_kind):
    def plan(src_refs, land_refs):
        x, y, c = _place()
        mine = 2 * x + y
        copies = []
        for a, (src, land) in enumerate(zip(src_refs, land_refs)):
            r, cs = src.shape
            if col_kind[a]:
                dst = land.at[:, pl.ds(pl.multiple_of(mine * cs, 128), cs)]
            else:
                dst = land.at[pl.ds(pl.multiple_of(mine * r, 16), r), :]
            copies.append((src, dst, (x, y, 1 - c)))
            copies += [(src, dst, (px, py, c)) for (px, py) in _other_chips(x, y)]
        return copies
    return plan


def _late_reduce_plan(col_kind):
    def plan(src_refs, land_refs):
        x, y, c = _place()
        copies = []
        for a, (src, land) in enumerate(zip(src_refs, land_refs)):
            for j, (px, py) in enumerate(_other_chips(x, y)):
                if col_kind[a]:
                    cs = land.shape[2]
                    piece = src.at[:, pl.ds(pl.multiple_of((2 * px + py) * cs, 128), cs)]
                else:
                    piece = src.at[2 * px + py]
                copies.append((piece, land.at[j], (px, py, c)))
        return copies
    return plan


def _pair_swap(name, halves):
    n = len(halves)

    def body(*refs):
        ins, outs = refs[:n], refs[n:2 * n]
        send_sems, recv_sems = refs[2 * n:]
        x, y, c = _place()
        copies = []
        for a in range(n):
            cp = pltpu.make_async_remote_copy(
                src_ref=ins[a], dst_ref=outs[a], send_sem=send_sems.at[a], recv_sem=recv_sems.at[a],
                device_id=(x, y, 1 - c), device_id_type=MESH)
            cp.start()
            copies.append(cp)
        for cp in copies:
            cp.wait()

    return pl.pallas_call(
        body, name=name, in_specs=[ANY] * n, out_specs=[ANY] * n,
        out_shape=[jax.ShapeDtypeStruct(s.shape, s.dtype) for s in halves],
        scratch_shapes=[pltpu.SemaphoreType.DMA((n,)), pltpu.SemaphoreType.DMA((n,))],
        compiler_params=pltpu.CompilerParams(has_side_effects=True),
    )(*halves)


def _chip_sum(name, chip_sel, own, col, others):
    _, r, c = others.shape
    tr = _pick(r, (256, 128, 64, 32, 16))
    if col:
        own_spec = pl.BlockSpec((tr, c), lambda i, s: (i, s[0]))
    else:
        own_spec = pl.BlockSpec((None, tr, c), lambda i, s: (s[0], i, 0))
    specs = [own_spec] + [pl.BlockSpec((None, tr, c), lambda i, s, k=k: (k, i, 0)) for k in range(3)]

    def body(s_ref, own_ref, r0, r1, r2, o_ref):
        o_ref[...] = ((own_ref[...].astype(F32) + r0[...].astype(F32)) + r1[...].astype(F32)) + r2[...].astype(F32)

    return pl.pallas_call(
        body, name=name,
        grid_spec=pltpu.PrefetchScalarGridSpec(
            num_scalar_prefetch=1, grid=(r // tr,), in_specs=specs,
            out_specs=pl.BlockSpec((tr, c), lambda i, s: (i, 0))),
        out_shape=jax.ShapeDtypeStruct((r, c), F32),
        compiler_params=_params(("parallel",)),
    )(chip_sel, own, others, others, others)


def _allreduce_small(vals):
    sizes = [int(math.prod(v.shape)) for v in vals]
    padded = [-(-s // 128) * 128 for s in sizes]
    total = -(-sum(padded) // 1024) * 1024
    flat = [jnp.pad(v.reshape(-1), (0, p - s)) for v, s, p in zip(vals, sizes, padded)]
    flat.append(jnp.zeros((total - sum(padded),), F32))
    packed = jnp.concatenate(flat).reshape(total // 128, 128)

    def body(in_ref, out_ref, r0, r1, r2, send_sems, recv_sems):
        x, y, c = _place()
        out_ref[...] = in_ref[...]
        for k, (peer, land) in enumerate(zip([(x, y, 1 - c), (1 - x, y, c), (x, 1 - y, c)], (r0, r1, r2))):
            cp = pltpu.make_async_remote_copy(
                src_ref=out_ref, dst_ref=land, send_sem=send_sems.at[k], recv_sem=recv_sems.at[k],
                device_id=peer, device_id_type=MESH)
            cp.start()
            cp.wait()
            out_ref[...] = out_ref[...] + land[...]

    vm = pl.BlockSpec(memory_space=pltpu.VMEM)
    summed = pl.pallas_call(
        body, name="allreduce_small", in_specs=[vm], out_specs=vm,
        out_shape=jax.ShapeDtypeStruct(packed.shape, F32),
        scratch_shapes=[pltpu.VMEM(packed.shape, F32)] * 3
        + [pltpu.SemaphoreType.DMA((3,)), pltpu.SemaphoreType.DMA((3,))],
        compiler_params=pltpu.CompilerParams(has_side_effects=True, vmem_limit_bytes=VMEM_LIMIT_BYTES),
    )(packed).reshape(-1)
    outs, off = [], 0
    for v, s, p in zip(vals, sizes, padded):
        outs.append(summed[off:off + s].reshape(v.shape))
        off += p
    return outs


def _adamw_math(w, g, m, v):
    m2 = ADAM_B1 * m + (1.0 - ADAM_B1) * g
    v2 = ADAM_B2 * v + (1.0 - ADAM_B2) * (g * g)
    m_hat = m2 / (1.0 - ADAM_B1 ** ADAM_STEP)
    v_hat = v2 / (1.0 - ADAM_B2 ** ADAM_STEP)
    delta = -ADAM_LR * (m_hat / (jnp.sqrt(v_hat) + ADAM_EPS) + ADAM_WD * w)
    return delta, m2, v2


def _adamw_big(name, w, g_mine, g_sibling, m, v):
    _, r, c = w.shape

    def body(w_ref, ga_ref, gb_ref, m_ref, v_ref, go_ref, d_ref, mo_ref, vo_ref):
        gv = ga_ref[...] + gb_ref[...]
        d, m2, v2 = _adamw_math(w_ref[...], gv, m_ref[...], v_ref[...])
        go_ref[...] = gv
        d_ref[...] = d
        mo_ref[...] = m2
        vo_ref[...] = v2

    tr = _pick(r, (256, 128, 64, 32, 16, 8))
    if r % tr == 0 and tr % 8 == 0:
        grid = (r // tr,)
        blk = pl.BlockSpec((None, tr, c), lambda i: (0, i, 0))
        part = pl.BlockSpec((tr, c), lambda i: (i, 0))
    else:
        grid = (c // 512,)
        blk = pl.BlockSpec((None, r, 512), lambda i: (0, 0, i))
        part = pl.BlockSpec((r, 512), lambda i: (0, i))
    return pl.pallas_call(
        body, name=name, grid=grid, in_specs=[blk, part, part, blk, blk], out_specs=[blk] * 4,
        out_shape=[jax.ShapeDtypeStruct((1, r, c), F32)] * 4, compiler_params=_params(("parallel",)),
    )(w, g_mine, g_sibling, m, v)


def _adamw_small(ws, gs, ms, vs):
    n = len(ws)

    def body(*refs):
        w_r, g_r, m_r, v_r = refs[:n], refs[n:2 * n], refs[2 * n:3 * n], refs[3 * n:4 * n]
        o = refs[4 * n:]
        for a in range(n):
            gv = g_r[a][...]
            d, m2, v2 = _adamw_math(w_r[a][...], gv, m_r[a][...], v_r[a][...])
            o[a][...] = gv
            o[n + a][...] = d
            o[2 * n + a][...] = m2
            o[3 * n + a][...] = v2

    res = pl.pallas_call(
        body, name="adamw_small", out_shape=[jax.ShapeDtypeStruct(w.shape, F32) for _ in range(4) for w in ws],
        compiler_params=_params(),
    )(*ws, *gs, *ms, *vs)
    return res[:n], res[n:2 * n], res[2 * n:3 * n], res[3 * n:]


def _full_from_gathered(name, gathered):
    if name == "w_in":
        rows = gathered.shape[0] // 4
        return gathered.reshape(4, rows, gathered.shape[1]).transpose(1, 0, 2).reshape(rows, 4 * gathered.shape[1])
    return gathered


def _reduce_layout(name, full):
    if name in COL_KIND:
        return full
    if name == "w_in":
        rows, cols = full.shape
        return full.reshape(rows, 4, cols // 4).transpose(1, 0, 2)
    return full.reshape(4, full.shape[0] // 4, full.shape[1])


def kernel(x, mem, norm_mix, w_in, fox_q_norm, fox_k_norm, fox_f_bias, s5_a_re, s5_a_im, s5_log_dt, s5_b_re, s5_b_im, s5_c_re, s5_c_im, s5_d, s5_w_glu, s5_b_glu, out_norm_fox, out_norm_s5, w_out, norm_cross, norm_mem, w_xq, w_xkv, xq_norm, xk_norm, w_xo, norm_ffn, w_ffn_up, ffn_conv_w, ffn_conv_b, w_ffn_down, loss_target, m_norm_mix, m_w_in, m_fox_q_norm, m_fox_k_norm, m_fox_f_bias, m_s5_a_re, m_s5_a_im, m_s5_log_dt, m_s5_b_re, m_s5_b_im, m_s5_c_re, m_s5_c_im, m_s5_d, m_s5_w_glu, m_s5_b_glu, m_out_norm_fox, m_out_norm_s5, m_w_out, m_norm_cross, m_norm_mem, m_w_xq, m_w_xkv, m_xq_norm, m_xk_norm, m_w_xo, m_norm_ffn, m_w_ffn_up, m_ffn_conv_w, m_ffn_conv_b, m_w_ffn_down, v_norm_mix, v_w_in, v_fox_q_norm, v_fox_k_norm, v_fox_f_bias, v_s5_a_re, v_s5_a_im, v_s5_log_dt, v_s5_b_re, v_s5_b_im, v_s5_c_re, v_s5_c_im, v_s5_d, v_s5_w_glu, v_s5_b_glu, v_out_norm_fox, v_out_norm_s5, v_w_out, v_norm_cross, v_norm_mem, v_w_xq, v_w_xkv, v_xq_norm, v_xk_norm, v_w_xo, v_norm_ffn, v_w_ffn_up, v_ffn_conv_w, v_ffn_conv_b, v_w_ffn_down):
    given = dict(locals())
    w = {n: given[n] for n in WEIGHTS}
    m = {n: given["m_" + n] for n in WEIGHTS}
    v = {n: given["v_" + n] for n in WEIGHTS}
    xi, yi, _ = _place()
    chip = (2 * xi + yi).astype(jnp.int32)
    chip_sel = chip.reshape(1)
    early_kind = [n in COL_KIND for n in EARLY_WEIGHTS]
    late_kind = [n in COL_KIND for n in LATE_WEIGHTS]

    gathered, taps = _gather_weights([w[FIRST_WEIGHT][0].astype(BF16)], [False], w["ffn_conv_w"][0])
    wb = {FIRST_WEIGHT: _full_from_gathered(FIRST_WEIGHT, gathered[0])}
    conv_w = taps.transpose(1, 0, 2).reshape(3, D_FF)
    pending = {}
    g_started = None
    for stage, names in (("mid", MID_WEIGHTS), ("late", LATE_WEIGHTS)):
        kinds = [n in COL_KIND for n in names]
        shards = [w[n][0].astype(BF16) for n in names]
        if g_started is not None:
            shards[0] = shards[0] + g_started[0:1, 0:1].astype(BF16)
        full = [lax.empty((s.shape[0], 4 * s.shape[1]) if ck else (4 * s.shape[0], s.shape[1]), BF16)
                for s, ck in zip(shards, kinds)]
        plan = _late_gather_plan(kinds)
        send, recv, srcs, lands, g_started = _split_start(
            "gather_" + stage + "_start", shards, full, 4 * len(names), plan)
        pending[stage] = (names, plan, send, recv, srcs, lands)

    def late_weights(stage, after):
        names, plan, send, recv, srcs, lands = pending[stage]
        _, full = _split_wait("gather_" + stage + "_wait", send, recv, srcs, lands, after, plan)
        return dict(zip(names, full))

    reduce_plan = _late_reduce_plan(late_kind)
    late_reduce = {}

    def early_grads(late_g):
        grads = [_reduce_layout(n, late_g[n]) for n in LATE_WEIGHTS]
        lands = [lax.empty((3, s.shape[0], s.shape[1] // 4) if ck else (3,) + s.shape[1:], BF16)
                 for s, ck in zip(grads, late_kind)]
        late_reduce["sems"] = _split_start("reduce_late_start", grads, lands, 3 * len(LATE_WEIGHTS), reduce_plan)
        return late_reduce["sems"][4][0:1, 0:1]

    p = {n: w[n][0] for n in SMALL}
    p["ffn_conv_w"] = conv_w
    for n in ("norm_mix", "fox_q_norm", "fox_k_norm", "fox_f_bias", "s5_b_glu", "out_norm_fox", "out_norm_s5",
              "norm_cross", "norm_mem", "xq_norm", "xk_norm", "norm_ffn", "ffn_conv_b"):
        p[n] = p[n].reshape(1, -1)
    p["norm_mix"] = p["norm_mix"] + g_started[0:1, 0:1]
    loss, grad_x, g = _local_step(x, mem, loss_target, p, wb, late_weights, early_grads)

    grads = [_reduce_layout(n, g[n].astype(BF16)) for n in EARLY_WEIGHTS]
    early_lands = [lax.empty((3, s.shape[0], s.shape[1] // 4) if ck else (3,) + s.shape[1:], BF16)
                   for s, ck in zip(grads, early_kind)]
    early_plan = _late_reduce_plan(early_kind)
    e_send, e_recv, e_srcs, e_lands, e_started = _split_start(
        "reduce_early_start", grads, early_lands, 3 * len(EARLY_WEIGHTS), early_plan)

    out_g, out_d, out_m, out_v = {}, {}, {}, {}

    def finish(names, kinds, sums, from_chips, tag):
        mine = [_chip_sum("reduce_chip_sum_" + n, chip_sel, ps, ck, fc)
                for n, ps, fc, ck in zip(names, sums, from_chips, kinds)]
        theirs = _pair_swap("reduce_pair_swap_" + tag, mine)
        for n, a, b in zip(names, mine, theirs):
            if n == "w_in":
                flip = lambda t: jnp.swapaxes(t, -1, -2)
                res = _adamw_big("adamw_" + n, flip(w[n]), flip(a), flip(b), flip(m[n]), flip(v[n]))
                out_g[n], out_d[n], out_m[n], out_v[n] = (flip(t) for t in res)
                continue
            out_g[n], out_d[n], out_m[n], out_v[n] = _adamw_big("adamw_" + n, w[n], a, b, m[n], v[n])

    r_send, r_recv, r_srcs, r_lands, _ = late_reduce["sems"]
    late_sums, late_from_chips = _split_wait("reduce_late_wait", r_send, r_recv, r_srcs, r_lands, e_started,
                                             reduce_plan)
    finish(LATE_WEIGHTS, late_kind, late_sums, late_from_chips, "late")

    small_names = list(SMALL) + ["ffn_conv_w"]
    small_vals = [g[n].reshape(w[n].shape if n != "ffn_conv_w" else (1, 3, D_FF)) for n in small_names]
    last = LATE_WEIGHTS[-1]
    loss, out_v[last] = lax.optimization_barrier((loss, out_v[last]))
    reduced = _allreduce_small(small_vals + [loss])
    loss_all = reduced[-1].reshape(())
    conv_w_grad = lax.dynamic_slice_in_dim(reduced[-2], chip * (D_FF // 4), D_FF // 4, axis=2)
    sg, sd, sm, sv = _adamw_small(
        [w[n] for n in small_names], list(reduced[:len(SMALL)]) + [conv_w_grad],
        [m[n] for n in small_names], [v[n] for n in small_names])
    out_g.update(zip(small_names, sg))
    out_d.update(zip(small_names, sd))
    out_m.update(zip(small_names, sm))
    out_v.update(zip(small_names, sv))

    early_sums, early_from_chips = _split_wait("reduce_early_wait", e_send, e_recv, e_srcs, e_lands, reduced[0],
                                               early_plan)
    finish(EARLY_WEIGHTS, early_kind, early_sums, early_from_chips, "early")

    return (loss_all, grad_x, *[out_g[n] for n in WEIGHTS], *[out_d[n] for n in WEIGHTS],
            *[out_m[n] for n in WEIGHTS], *[out_v[n] for n in WEIGHTS])
```

```python
import functools
import math

import jax
import jax.numpy as jnp
from jax import lax
from jax.experimental import pallas as pl
from jax.experimental.pallas import tpu as pltpu

F32 = jnp.float32
BF16 = jnp.bfloat16

D_MODEL = 1024
FOX_WIDTH = 512
HEAD_DIM = 64
N_FOX_HEADS = 8
S5_WIDTH = 512
S5_GROUP_CH = 16
S5_GROUPS = 32
S5_STATE = 64
S5_CH = S5_GROUPS * S5_STATE
N_X_HEADS = 4
X_HEAD_DIM = 256
N_MEM = 256
D_FF = 2816
UF_COLS = 640
EPS = 1e-6
ADAM_LR = 0.001
ADAM_B1 = 0.9
ADAM_B2 = 0.999
ADAM_EPS = 1e-08
ADAM_WD = 0.01
ADAM_STEP = 10

VMEM_LIMIT_BYTES = 56 * 1024 * 1024
MM_BLOCK_BYTES = 6 * 1024 * 1024
MM_VMEM_BYTES = 40 * 1024 * 1024
MM_TILE_MAX = 1536
MESH = pl.DeviceIdType.MESH

FIRST_WEIGHT = "w_in"
MID_WEIGHTS = ("s5_w_glu", "w_out")
EARLY_WEIGHTS = (FIRST_WEIGHT,) + MID_WEIGHTS
LATE_WEIGHTS = ("w_xq", "w_xkv", "w_xo", "w_ffn_up", "w_ffn_down")
BIG = EARLY_WEIGHTS + LATE_WEIGHTS
COL_KIND = ("w_xkv", "w_ffn_up")
SMALL = ("norm_mix", "fox_q_norm", "fox_k_norm", "fox_f_bias", "s5_a_re", "s5_a_im", "s5_log_dt",
         "s5_b_re", "s5_b_im", "s5_c_re", "s5_c_im", "s5_d", "s5_b_glu", "out_norm_fox", "out_norm_s5",
         "norm_cross", "norm_mem", "xq_norm", "xk_norm", "norm_ffn", "ffn_conv_b")
WEIGHTS = ("norm_mix", "w_in", "fox_q_norm", "fox_k_norm", "fox_f_bias", "s5_a_re", "s5_a_im", "s5_log_dt",
           "s5_b_re", "s5_b_im", "s5_c_re", "s5_c_im", "s5_d", "s5_w_glu", "s5_b_glu", "out_norm_fox",
           "out_norm_s5", "w_out", "norm_cross", "norm_mem", "w_xq", "w_xkv", "xq_norm", "xk_norm", "w_xo",
           "norm_ffn", "w_ffn_up", "ffn_conv_w", "ffn_conv_b", "w_ffn_down")


def _params(sem=None):
    return pltpu.CompilerParams(dimension_semantics=sem, vmem_limit_bytes=VMEM_LIMIT_BYTES)


def _pick(n, cands):
    for c in cands:
        if n % c == 0:
            return c
    return n


_DIMS = {"nn": (((1,), (0,)), ((), ())), "nt": (((1,), (1,)), ((), ())), "tn": (((0,), (0,)), ((), ()))}


def _mm(a, b, mode, name, out_dtype=F32, res=None):
    if mode == "nn":
        (m, k), (k2, n) = a.shape, b.shape
    elif mode == "nt":
        (m, k), (n, k2) = a.shape, b.shape
    else:
        (k, m), (k2, n) = a.shape, b.shape
    assert k == k2, (name, a.shape, b.shape)

    has_res = res is not None
    a_size, b_size = a.dtype.itemsize, b.dtype.itemsize
    o_size = jnp.dtype(out_dtype).itemsize + (res.dtype.itemsize if has_res else 0)

    def tiles(dim):
        return [c for c in range(MM_TILE_MAX, 0, -128) if dim % c == 0] or [dim]

    best = None
    for tm in tiles(m):
        for tn in tiles(n):
            a_blk, b_blk = tm * k * a_size, tn * k * b_size
            if max(a_blk, b_blk) > MM_BLOCK_BYTES or 2 * (a_blk + b_blk + tm * tn * o_size) > MM_VMEM_BYTES:
                continue
            for rows_outer in (True, False):
                moved = (m * k * a_size + (m // tm) * n * k * b_size) if rows_outer else \
                        (n * k * b_size + (n // tn) * m * k * a_size)
                key = (moved, -(tm * tn))
                if best is None or key < best[0]:
                    best = (key, tm, tn, rows_outer)
    assert best is not None, (name, a.shape, b.shape)
    _, tm, tn, rows_outer = best
    ij = (lambda g0, g1: (g0, g1)) if rows_outer else (lambda g0, g1: (g1, g0))
    if mode == "tn":
        a_spec = pl.BlockSpec((k, tm), lambda g0, g1: (0, ij(g0, g1)[0]))
    else:
        a_spec = pl.BlockSpec((tm, k), lambda g0, g1: (ij(g0, g1)[0], 0))
    if mode == "nt":
        b_spec = pl.BlockSpec((tn, k), lambda g0, g1: (ij(g0, g1)[1], 0))
    else:
        b_spec = pl.BlockSpec((k, tn), lambda g0, g1: (0, ij(g0, g1)[1]))
    o_spec = pl.BlockSpec((tm, tn), lambda g0, g1: ij(g0, g1))
    grid = (m // tm, n // tn) if rows_outer else (n // tn, m // tm)
    dims = _DIMS[mode]

    def body(*refs):
        a_ref, b_ref = refs[0], refs[1]
        o_ref = refs[-1]
        acc = lax.dot_general(a_ref[...].astype(BF16), b_ref[...].astype(BF16), dims, preferred_element_type=F32)
        if has_res:
            acc = acc + refs[2][...].astype(F32)
        o_ref[...] = acc.astype(o_ref.dtype)

    return pl.pallas_call(
        body, name=name, grid=grid,
        in_specs=[a_spec, b_spec] + ([o_spec] if has_res else []),
        out_specs=o_spec, out_shape=jax.ShapeDtypeStruct((m, n), out_dtype),
        compiler_params=_params(("parallel", "parallel")),
    )(*((a, b, res) if has_res else (a, b)))


def _row_spec(tm, bc, off, step):
    return pl.BlockSpec((tm, bc), lambda i, h: (i, off + step * h))


ROW_TILE_ELEMS = 512 * 1024


def _row_tile(t, rows):
    widest = max(bc for (_, bc, _, _) in rows)
    return _pick(t, (min(t, ROW_TILE_ELEMS // widest), 512, 256, 128, 64, 8))


def _rowwise(fn, rows, pars, outs, name, heads=1):
    t = rows[0][0].shape[0]
    tm = _row_tile(t, rows)
    nr, npar = len(rows), len(pars)

    def body(*refs):
        vals = [r[...].astype(F32) for r in refs[:nr + npar]]
        res = fn(*vals)
        if not isinstance(res, (tuple, list)):
            res = (res,)
        for o_ref, v in zip(refs[nr + npar:], res):
            o_ref[...] = v.astype(o_ref.dtype)

    in_specs = [_row_spec(tm, bc, off, st) for (_, bc, off, st) in rows]
    in_specs += [pl.BlockSpec(p.shape, lambda i, h: (0, 0)) for p in pars]
    out_specs = [_row_spec(tm, bc, 0, st) for (_, bc, st, _) in outs]
    out_shape = [jax.ShapeDtypeStruct((t, c), dt) for (c, _, _, dt) in outs]
    res = pl.pallas_call(
        body, name=name, grid=(t // tm, heads), in_specs=in_specs, out_specs=out_specs, out_shape=out_shape,
        compiler_params=_params(("parallel", "parallel")),
    )(*[r[0] for r in rows], *pars)
    return res[0] if len(res) == 1 else res


def _rowwise_vjp(fn, rows, pars, cts, name, heads=1, adds=None, row_dtypes=None):
    t = rows[0][0].shape[0]
    tm = _row_tile(t, rows)
    nr, npar, nct = len(rows), len(pars), len(cts)
    adds = adds or [None] * nr
    add_list = [a for a in adds if a is not None]
    row_dtypes = row_dtypes or [F32] * nr

    def body(*refs):
        i, h = pl.program_id(0), pl.program_id(1)
        p = 0
        row_v = [r[...].astype(F32) for r in refs[p:p + nr]]; p += nr
        par_v = [r[...].astype(F32) for r in refs[p:p + npar]]; p += npar
        ct_v = [r[...].astype(F32) for r in refs[p:p + nct]]; p += nct
        add_refs = refs[p:p + len(add_list)]; p += len(add_list)
        drow_refs = refs[p:p + nr]; p += nr
        dpar_refs = refs[p:p + npar]

        def wrapped(*a):
            r = fn(*a)
            return tuple(r) if isinstance(r, (tuple, list)) else (r,)

        _, pull = jax.vjp(wrapped, *row_v, *par_v)
        grads = pull(tuple(ct_v))
        ai = 0
        for k in range(nr):
            g = grads[k]
            if adds[k] is not None:
                g = g + add_refs[ai][...].astype(F32)
                ai += 1
            drow_refs[k][...] = g.astype(drow_refs[k].dtype)

        @pl.when((i == 0) & (h == 0))
        def _():
            for r in dpar_refs:
                r[...] = jnp.zeros(r.shape, r.dtype)

        for k in range(npar):
            dpar_refs[k][...] += grads[nr + k]

    in_specs = [_row_spec(tm, bc, off, st) for (_, bc, off, st) in rows]
    in_specs += [pl.BlockSpec(q.shape, lambda i, h: (0, 0)) for q in pars]
    in_specs += [_row_spec(tm, bc, off, st) for (_, bc, off, st) in cts]
    in_specs += [_row_spec(tm, bc, off, st) for (_, bc, off, st) in add_list]
    out_specs = [_row_spec(tm, bc, 0, st) for (_, bc, _, st) in rows]
    out_specs += [pl.BlockSpec(q.shape, lambda i, h: (0, 0)) for q in pars]
    out_shape = [jax.ShapeDtypeStruct((t, bc * (heads if st else 1)), dt) for (_, bc, _, st), dt in zip(rows, row_dtypes)]
    out_shape += [jax.ShapeDtypeStruct(q.shape, F32) for q in pars]
    res = pl.pallas_call(
        body, name=name, grid=(t // tm, heads), in_specs=in_specs, out_specs=out_specs, out_shape=out_shape,
        compiler_params=_params(("arbitrary", "arbitrary")),
    )(*[r[0] for r in rows], *pars, *[c[0] for c in cts], *[a[0] for a in add_list])
    return list(res[:nr]), list(res[nr:])


def _rms(x, g):
    return x * lax.rsqrt(jnp.mean(x * x, axis=-1, keepdims=True) + EPS) * g


def _rms_pair(x, g):
    left = lax.broadcasted_iota(jnp.int32, x.shape, 1) < HEAD_DIM
    x2 = x * x
    ms_a = jnp.sum(jnp.where(left, x2, 0.0), axis=-1, keepdims=True) * (1.0 / HEAD_DIM)
    ms_b = jnp.sum(jnp.where(left, 0.0, x2), axis=-1, keepdims=True) * (1.0 / HEAD_DIM)
    return x * lax.rsqrt(jnp.where(left, ms_a, ms_b) + EPS) * g


def _gelu(x):
    return 0.5 * x * (1.0 + jnp.tanh(math.sqrt(2.0 / math.pi) * (x + 0.044715 * (x * x * x))))


def _s5_act(ys, u, d):
    return _gelu(ys + d * u)


def _s5_gate(yg, z, b, g):
    return _rms(yg * jax.nn.sigmoid(z + b), g)


def _lane_cumsum(x, reverse):
    n = x.shape[-1]
    lane = lax.broadcasted_iota(jnp.int32, x.shape, 1)
    k = 1
    while k < n:
        if reverse:
            x = x + jnp.where(lane < n - k, pltpu.roll(x, n - k, 1), 0.0)
        else:
            x = x + jnp.where(lane >= k, pltpu.roll(x, k, 1), 0.0)
        k *= 2
    return x


def _log_sigmoid(z):
    return jnp.minimum(z, 0.0) - jnp.log(1.0 + jnp.exp(-jnp.abs(z)))


def _forget_fwd(f, bias):
    def body(f_ref, b_ref, c_ref):
        c_ref[...] = _lane_cumsum(_log_sigmoid(f_ref[...] + b_ref[...]), False)

    return pl.pallas_call(body, name="forget_fwd", out_shape=jax.ShapeDtypeStruct(f.shape, F32),
                          compiler_params=_params())(f, bias)


def _forget_bwd(f, bias, dc):
    def body(f_ref, b_ref, dc_ref, df_ref, db_ref):
        dlog = _lane_cumsum(dc_ref[...], True)
        df = dlog * jax.nn.sigmoid(-(f_ref[...] + b_ref[...]))
        df_ref[...] = df
        db_ref[...] = jnp.sum(df, axis=1, keepdims=True)

    return pl.pallas_call(body, name="forget_bwd",
                          out_shape=(jax.ShapeDtypeStruct(f.shape, F32), jax.ShapeDtypeStruct(bias.shape, F32)),
                          compiler_params=_params())(f, bias, dc)


FOX_BLOCK = 256
FOX_KEYS = 256
_NT = _DIMS["nt"]
_TN = _DIMS["tn"]


N_PAIRS = N_FOX_HEADS // 2
V_BLOCK0 = 2 * N_PAIRS


def _left_lanes(shape):
    return lax.broadcasted_iota(jnp.int32, shape, 1) < HEAD_DIM


def _top_rows(shape):
    return lax.broadcasted_iota(jnp.int32, shape, 0) < HEAD_DIM


def _wide(c_tile, n):
    return c_tile if n == 128 else jnp.concatenate([c_tile] * (n // 128), axis=1)


def _fox_fwd(qn, kn, qkv, c_wide, seqs):
    t = qn.shape[0]
    l = t // seqs
    tb = min(FOX_BLOCK, l)
    tk = min(FOX_KEYS, tb)
    ratio = tb // tk
    nb = l // tb
    scale = HEAD_DIM ** -0.5

    def body(q_ref, k_ref, v_ref, ca_ref, cb_ref, o_ref, lse_ref, vt_ref):
        i = pl.program_id(2)
        top = _top_rows((128, tb))

        @pl.when(i == 0)
        def _():
            vt_ref[...] = v_ref[...].T.astype(BF16)

        qt = (q_ref[...].astype(F32) * scale).T.astype(BF16)
        zero = jnp.zeros_like(qt)
        qts = (jnp.where(top, qt, zero), jnp.where(top, zero, qt))
        top_k = _top_rows((128, tk))
        zero_k = jnp.zeros((128, tk), BF16)
        key_pos = lax.broadcasted_iota(jnp.int32, (tk, tb), 0)
        query_pos = lax.broadcasted_iota(jnp.int32, (tk, tb), 1)
        c_refs = (ca_ref, cb_ref)

        def scores(j):
            off = pl.multiple_of(j * tk, tk)
            k2 = k_ref[pl.ds(off, tk), :]
            return tuple(jnp.dot(k2, qts[h], preferred_element_type=F32) - _wide(c_refs[h][pl.ds(off, tk), :], tb)
                         for h in (0, 1))

        def values_times(ps, j):
            vt = vt_ref[:, pl.ds(pl.multiple_of(j * tk, tk), tk)]
            return (jnp.dot(jnp.where(top_k, vt, zero_k), ps[0], preferred_element_type=F32)
                    + jnp.dot(jnp.where(top_k, zero_k, vt), ps[1], preferred_element_type=F32))

        def softmax_step(sts, stats, first_key):
            ps, new, alphas = [], [], []
            for st, (m, s_sum) in zip(sts, stats):
                if first_key is not None:
                    st = jnp.where(key_pos + first_key <= query_pos, st, -jnp.inf)
                m_new = jnp.maximum(m, jnp.max(st, axis=0, keepdims=True))
                alpha = jnp.exp(m - m_new)
                p = jnp.exp(st - m_new)
                new.append((m_new, alpha * s_sum + jnp.sum(p, axis=0, keepdims=True)))
                alphas.append(alpha)
                ps.append(p.astype(BF16))
            return tuple(ps), tuple(new), jnp.where(top, alphas[0], alphas[1])

        def step(j, carry):
            sts, ps_prev, stats, acc = carry
            sts_next = scores(j + 1)
            acc = acc + values_times(ps_prev, jnp.maximum(j - 1, 0))
            ps, stats, alpha = softmax_step(sts, stats, None)
            return sts_next, ps, stats, alpha * acc

        stat = (jnp.full((1, tb), -jnp.inf, F32), jnp.zeros((1, tb), F32))
        no_p = jnp.zeros((tk, tb), BF16)
        below = i * ratio
        sts, ps_prev, stats, acc = lax.fori_loop(
            0, below, step, (scores(0), (no_p, no_p), (stat, stat), jnp.zeros((128, tb), F32)))
        for r in range(ratio):
            sts_next = scores(below + r + 1) if r + 1 < ratio else None
            acc = acc + values_times(ps_prev, jnp.maximum(below + r - 1, 0))
            ps_prev, stats, alpha = softmax_step(sts, stats, r * tk)
            acc = alpha * acc
            sts = sts_next
        acc = acc + values_times(ps_prev, below + ratio - 1)
        (ma, sa), (mb, sb) = stats
        o_ref[...] = (acc / jnp.where(top, sa, sb)).T
        lse_ref[0:1, :] = ma + jnp.log(sa)
        lse_ref[1:2, :] = mb + jnp.log(sb)

    qblk = pl.BlockSpec((tb, 128), lambda b, hp, i: (b * nb + i, hp))
    return pl.pallas_call(
        body, name="fox_fwd", grid=(seqs, N_PAIRS, nb),
        in_specs=[qblk, pl.BlockSpec((l, 128), lambda b, hp, i: (b, hp)),
                  pl.BlockSpec((l, 128), lambda b, hp, i: (b, V_BLOCK0 + hp)),
                  pl.BlockSpec((None, l, 128), lambda b, hp, i: (b * N_FOX_HEADS + 2 * hp, 0, 0)),
                  pl.BlockSpec((None, l, 128), lambda b, hp, i: (b * N_FOX_HEADS + 2 * hp + 1, 0, 0))],
        out_specs=[qblk, pl.BlockSpec((None, 2, tb), lambda b, hp, i: (b * N_PAIRS + hp, 0, i))],
        out_shape=[jax.ShapeDtypeStruct((t, FOX_WIDTH), F32), jax.ShapeDtypeStruct((seqs * N_PAIRS, 2, l), F32)],
        scratch_shapes=[pltpu.VMEM((128, l), BF16)],
        compiler_params=_params(("parallel", "parallel", "arbitrary")),
    )(qn, kn, qkv, c_wide, c_wide)


def _fox_bwd(qn, kn, qkv, c_wide, o, do, lse, seqs):
    t = qn.shape[0]
    l = t // seqs
    tb = min(FOX_BLOCK, l)
    nb = l // tb
    scale = HEAD_DIM ** -0.5
    one_at = (HEAD_DIM, 0)

    def body(q_ref, k_ref, v_ref, ca_ref, cb_ref, o_ref, do_ref, lse_ref, dq_ref, dk_ref, dv_ref, dc_ref, dcq_ref,
             qt_ref, kt_ref, dot_ref, delta_ref, dqa_ref, dqb_ref):
        top_l = _top_rows((128, l))
        top = _top_rows((128, tb))
        left = _left_lanes((tb, 128))
        row_id = lax.broadcasted_iota(jnp.int32, (128, tb), 0)
        lane_id = lax.broadcasted_iota(jnp.int32, (tb, 128), 1)
        zero_t = jnp.zeros((128, tb), BF16)
        zero_l = jnp.zeros((tb, 128), BF16)
        rows = lambda a: (jnp.where(top, a, zero_t), jnp.where(top, zero_t, a))
        lanes = lambda a: (jnp.where(left, a, zero_l), jnp.where(left, zero_l, a))
        with_one_row = lambda pair: tuple(jnp.where(row_id == one_at[h], 1.0, pair[h]).astype(BF16) for h in (0, 1))
        with_one_lane = lambda pair: tuple(jnp.where(lane_id == one_at[h], 1.0, pair[h]).astype(BF16) for h in (0, 1))
        causal = lax.broadcasted_iota(jnp.int32, (tb, tb), 0) <= lax.broadcasted_iota(jnp.int32, (tb, tb), 1)
        c_refs = (ca_ref, cb_ref)
        dq_refs = (dqa_ref, dqb_ref)

        qt_ref[...] = (q_ref[...].astype(F32) * scale).T.astype(BF16)
        kt_ref[...] = k_ref[...].astype(F32).T.astype(BF16)
        do_t = do_ref[...].T
        dot_ref[...] = do_t.astype(BF16)
        prod_t = do_t * o_ref[...].T
        delta_ref[0:1, :] = jnp.sum(jnp.where(top_l, prod_t, 0.0), axis=0, keepdims=True)
        delta_ref[1:2, :] = jnp.sum(jnp.where(top_l, 0.0, prod_t), axis=0, keepdims=True)
        dqa_ref[...] = jnp.zeros(dqa_ref.shape, F32)
        dqb_ref[...] = jnp.zeros(dqb_ref.shape, F32)

        def kv_block(j, _):
            koff = pl.multiple_of(j * tb, tb)
            k2 = k_ref[pl.ds(koff, tb), :]
            v2 = v_ref[pl.ds(koff, tb), :].astype(BF16)
            kts = with_one_row(rows(kt_ref[:, pl.ds(koff, tb)]))
            cw = tuple(_wide(c_refs[h][pl.ds(koff, tb), :], tb) for h in (0, 1))

            def q_block(i, carry, masked):
                dks, dv = list(carry[:2]), carry[2]
                qoff = pl.multiple_of(i * tb, tb)
                qs = lanes((q_ref[pl.ds(qoff, tb), :].astype(F32) * scale).astype(BF16))
                qs_one = with_one_lane(qs)
                dos = lanes(do_ref[pl.ds(qoff, tb), :].astype(BF16))
                qts = rows(qt_ref[:, pl.ds(qoff, tb)])
                dots = rows(dot_ref[:, pl.ds(qoff, tb)])
                for h in (0, 1):
                    st = jnp.dot(k2, qts[h], preferred_element_type=F32) - cw[h]
                    p = jnp.exp(st - lse_ref[h:h + 1, pl.ds(qoff, tb)])
                    if masked:
                        p = jnp.where(causal, p, 0.0)
                    dp = jnp.dot(v2, dots[h], preferred_element_type=F32)
                    dsb = (p * (dp - delta_ref[h:h + 1, pl.ds(qoff, tb)])).astype(BF16)
                    dv = dv + jnp.dot(p.astype(BF16), dos[h], preferred_element_type=F32)
                    dks[h] = dks[h] + jnp.dot(dsb, qs_one[h], preferred_element_type=F32)
                    dq_refs[h][:, pl.ds(qoff, tb)] += jnp.dot(kts[h], dsb, preferred_element_type=F32)
                return dks[0], dks[1], dv

            z = jnp.zeros((tb, 128), F32)
            carry = q_block(j, (z, z, z), True)
            rest = nb - 1 - j
            carry = lax.fori_loop(
                0, rest // 2, lambda n, c: q_block(j + 2 + 2 * n, q_block(j + 1 + 2 * n, c, False), False), carry)
            dka, dkb, dv = lax.cond(rest % 2 == 1, lambda c: q_block(nb - 1, c, False), lambda c: c, carry)
            dk_ref[pl.ds(koff, tb), :] = jnp.where(left, dka, dkb)
            dv_ref[pl.ds(koff, tb), :] = dv
            dc_ref[0:1, pl.ds(koff, tb)] = -dka.T[one_at[0]:one_at[0] + 1, :]
            dc_ref[1:2, pl.ds(koff, tb)] = -dkb.T[one_at[1]:one_at[1] + 1, :]
            return 0

        lax.fori_loop(0, nb, kv_block, 0)
        dq_ref[...] = (jnp.where(top_l, dqa_ref[...], dqb_ref[...]) * scale).T
        dcq_ref[0:1, :] = dqa_ref[one_at[0]:one_at[0] + 1, :]
        dcq_ref[1:2, :] = dqb_ref[one_at[1]:one_at[1] + 1, :]

    blk = pl.BlockSpec((l, 128), lambda b, hp: (b, hp))
    cspec = lambda k: pl.BlockSpec((None, l, 128), lambda b, hp: (b * N_FOX_HEADS + 2 * hp + k, 0, 0))
    rows2 = pl.BlockSpec((None, 2, l), lambda b, hp: (b * N_PAIRS + hp, 0, 0))
    wide = jax.ShapeDtypeStruct((t, FOX_WIDTH), F32)
    pair_rows = jax.ShapeDtypeStruct((seqs * N_PAIRS, 2, l), F32)
    return pl.pallas_call(
        body, name="fox_bwd", grid=(seqs, N_PAIRS),
        in_specs=[blk, blk, pl.BlockSpec((l, 128), lambda b, hp: (b, V_BLOCK0 + hp)), cspec(0), cspec(1), blk, blk, rows2],
        out_specs=[blk, blk, blk, rows2, rows2],
        out_shape=[wide, wide, wide, pair_rows, pair_rows],
        scratch_shapes=[pltpu.VMEM((128, l), BF16), pltpu.VMEM((128, l), BF16), pltpu.VMEM((128, l), BF16),
                        pltpu.VMEM((2, l), F32), pltpu.VMEM((128, l), F32), pltpu.VMEM((128, l), F32)],
        compiler_params=_params(("parallel", "parallel")),
    )(qn, kn, qkv, c_wide, c_wide, o, do, lse)


SCAN_ROWS = 256
SCAN_COLS = 1024


S5_IN = 128
S5_ST = 512
SCAN_CHUNKS = SCAN_COLS // S5_ST
SCAN_SEGS = 8
LANES = 128


def _cmul(ar, ai, br, bi):
    return ar * br - ai * bi, ar * bi + ai * br


def _powers_into(pw_r, pw_i, a_r, a_i, seg):
    pw_r[0:1, :] = a_r
    pw_i[0:1, :] = a_i
    for k in range(1, seg):
        pr, pi = _cmul(pw_r[k - 1:k, :], pw_i[k - 1:k, :], a_r, a_i)
        pw_r[k:k + 1, :] = pr
        pw_i[k:k + 1, :] = pi


def _interleave(dst, src, seg):
    for h in range(src.shape[0]):
        for j in range(seg):
            dst[h, j * SCAN_SEGS:(j + 1) * SCAN_SEGS, :] = src[h, pl.ds(j, SCAN_SEGS, stride=seg), :]


def _deinterleave(dst, src, seg):
    for h in range(src.shape[0]):
        for j in range(seg):
            dst[h, pl.ds(j, SCAN_SEGS, stride=seg), :] = src[h, j * SCAN_SEGS:(j + 1) * SCAN_SEGS, :]


def _interleaved(ref, tmp_a, tmp_b, seg):
    n = ref.shape[1] // LANES
    for h in range(n):
        tmp_a[h] = ref[:, h * LANES:(h + 1) * LANES].astype(F32)
    _interleave(tmp_b, tmp_a, seg)
    return jnp.concatenate([tmp_b[h] for h in range(n)], axis=1)


def _store_deinterleaved(ref, val, tmp_a, tmp_b, seg):
    n = ref.shape[1] // LANES
    for h in range(n):
        tmp_a[h] = val[:, h * LANES:(h + 1) * LANES]
    _deinterleave(tmp_b, tmp_a, seg)
    for h in range(n):
        ref[:, h * LANES:(h + 1) * LANES] = tmp_b[h]


def _segment_scan(b_r, b_i, x_r, x_i, pw_r, pw_i, car_r, car_i, seg, sign, reverse, visit=None):
    nc = b_r.shape[0]
    sub = lax.broadcasted_iota(jnp.int32, (SCAN_SEGS, LANES), 0)
    lanes = lambda c: slice(c * LANES, (c + 1) * LANES)
    rows = lambda j: pl.ds(pl.multiple_of(((seg - 1 - j) if reverse else j) * SCAN_SEGS, SCAN_SEGS), SCAN_SEGS)
    a1 = [(pw_r[0:1, lanes(c)], sign * pw_i[0:1, lanes(c)]) for c in range(nc)]

    def local(j, xs):
        out = []
        for c in range(nc):
            xr, xi = xs[2 * c], xs[2 * c + 1]
            nr = a1[c][0] * xr - a1[c][1] * xi + b_r[c, rows(j), :]
            ni = a1[c][0] * xi + a1[c][1] * xr + b_i[c, rows(j), :]
            x_r[c, rows(j), :] = nr
            x_i[c, rows(j), :] = ni
            out += [nr, ni]
        return tuple(out)

    zero = jnp.zeros((SCAN_SEGS, LANES), F32)
    ends = lax.fori_loop(0, seg, local, (zero,) * (2 * nc))

    if reverse:
        first = sub == SCAN_SEGS - 1
        neighbour = lambda v: pltpu.roll(v, SCAN_SEGS - 1, 0)
        shift = lambda v, d: jnp.where(sub < SCAN_SEGS - d, pltpu.roll(v, SCAN_SEGS - d, 0), 0.0)
    else:
        first = sub == 0
        neighbour = lambda v: pltpu.roll(v, 1, 0)
        shift = lambda v, d: jnp.where(sub >= d, pltpu.roll(v, d, 0), 0.0)
    last = 0 if reverse else SCAN_SEGS - 1
    entries = []
    for c in range(nc):
        er, ei = ends[2 * c], ends[2 * c + 1]
        pr, pi = pw_r[seg - 1:seg, lanes(c)], sign * pw_i[seg - 1:seg, lanes(c)]
        yr = jnp.where(first, car_r[:, lanes(c)], neighbour(er))
        yi = jnp.where(first, car_i[:, lanes(c)], neighbour(ei))
        qr, qi = pr, pi
        for d in (1, 2, 4):
            mr, mi = _cmul(qr, qi, shift(yr, d), shift(yi, d))
            yr, yi = yr + mr, yi + mi
            qr, qi = _cmul(qr, qi, qr, qi)
        lr, li = _cmul(pr, pi, yr, yi)
        car_r[:, lanes(c)] = (er + lr)[last:last + 1, :]
        car_i[:, lanes(c)] = (ei + li)[last:last + 1, :]
        entries += [yr, yi]

    def correct(j, prev):
        out = []
        row_r, row_i = pw_r[pl.ds(j, 1), :], sign * pw_i[pl.ds(j, 1), :]
        for c in range(nc):
            mr, mi = _cmul(row_r[:, lanes(c)], row_i[:, lanes(c)], entries[2 * c], entries[2 * c + 1])
            nr = x_r[c, rows(j), :] + mr
            ni = x_i[c, rows(j), :] + mi
            x_r[c, rows(j), :] = nr
            x_i[c, rows(j), :] = ni
            if visit is not None:
                visit(c, rows(j), prev[2 * c], prev[2 * c + 1])
            out += [nr, ni]
        return tuple(out)

    lax.fori_loop(0, seg, correct, tuple(entries))


def _s5_fwd(uf, bbr, bbi, cr, ci, ar, ai, seqs):
    t = uf.shape[0]
    l = t // seqs
    tl = min(SCAN_ROWS, l)
    nl = l // tl
    seg = tl // SCAN_SEGS
    cb, nq = SCAN_COLS, SCAN_CHUNKS
    nc = cb // LANES
    per = S5_ST // LANES

    def body(u_ref, bbr_ref, bbi_ref, cr_ref, ci_ref, ar_ref, ai_ref, xr_ref, xi_ref, ys_ref,
             car_r, car_i, pw_r, pw_i, b_r, b_i, x_r, x_i, tmp_a, tmp_b):
        @pl.when(pl.program_id(2) == 0)
        def _():
            car_r[...] = jnp.zeros(car_r.shape, F32)
            car_i[...] = jnp.zeros(car_i.shape, F32)
            _powers_into(pw_r, pw_i, ar_ref[...], ai_ref[...], seg)

        u = _interleaved(u_ref, tmp_a, tmp_b, seg).astype(BF16)
        for q in range(nq):
            uq = u[:, q * S5_IN:(q + 1) * S5_IN]
            br = jnp.dot(uq, bbr_ref[q], preferred_element_type=F32)
            bi = jnp.dot(uq, bbi_ref[q], preferred_element_type=F32)
            for s in range(per):
                b_r[q * per + s] = br[:, s * LANES:(s + 1) * LANES]
                b_i[q * per + s] = bi[:, s * LANES:(s + 1) * LANES]
        _segment_scan(b_r, b_i, x_r, x_i, pw_r, pw_i, car_r, car_i, seg, 1.0, False)
        for c in range(nc):
            xr_ref[:, c * LANES:(c + 1) * LANES] = x_r[c]
            xi_ref[:, c * LANES:(c + 1) * LANES] = x_i[c]
        ys = []
        for q in range(nq):
            xq_r = xr_ref[:, q * S5_ST:(q + 1) * S5_ST].astype(BF16)
            xq_i = xi_ref[:, q * S5_ST:(q + 1) * S5_ST].astype(BF16)
            ys.append(jnp.dot(xq_r, cr_ref[q], preferred_element_type=F32)
                      + jnp.dot(xq_i, ci_ref[q], preferred_element_type=F32))
        _store_deinterleaved(ys_ref, jnp.concatenate(ys, axis=1), tmp_a, tmp_b, seg)

    rows = lambda w: pl.BlockSpec((tl, w), lambda s, j, r: (s * nl + r, j))
    chunk = lambda a: pl.BlockSpec((nq,) + a.shape[1:], lambda s, j, r: (j, 0, 0))
    par = pl.BlockSpec((1, cb), lambda s, j, r: (0, j))
    return pl.pallas_call(
        body, name="s5_fwd", grid=(seqs, S5_CH // cb, nl),
        in_specs=[rows(nq * S5_IN), chunk(bbr), chunk(bbi), chunk(cr), chunk(ci), par, par],
        out_specs=[rows(cb), rows(cb), rows(nq * S5_IN)],
        out_shape=[jax.ShapeDtypeStruct((t, S5_CH), F32)] * 2 + [jax.ShapeDtypeStruct((t, S5_WIDTH), F32)],
        scratch_shapes=[pltpu.VMEM((1, cb), F32), pltpu.VMEM((1, cb), F32), pltpu.VMEM((seg, cb), F32),
                        pltpu.VMEM((seg, cb), F32)] + [pltpu.VMEM((nc, tl, LANES), F32)] * 4
        + [pltpu.VMEM((nq * S5_IN // LANES, tl, LANES), F32)] * 2,
        compiler_params=_params(("parallel", "parallel", "arbitrary")),
    )(uf, bbr, bbi, cr, ci, ar, ai)


def _s5_bwd(dys, uf, xr, xi, bbr, bbi, cr, ci, ar, ai, seqs):
    t = dys.shape[0]
    l = t // seqs
    tl = min(SCAN_ROWS, l)
    nl = l // tl
    seg = tl // SCAN_SEGS
    cb, nq = SCAN_COLS, SCAN_CHUNKS
    nc = cb // LANES
    per = S5_ST // LANES

    def body(dy_ref, u_ref, xr_ref, xi_ref, bbr_ref, bbi_ref, cr_ref, ci_ref, ar_ref, ai_ref,
             du_ref, dbbr_ref, dbbi_ref, dcr_ref, dci_ref, dar_ref, dai_ref,
             car_r, car_i, pw_r, pw_i, g_r, g_i, lam_r, lam_i, x_r, x_i, acc_r, acc_i, tmp_a, tmp_b):
        @pl.when(pl.program_id(2) == 0)
        def _():
            car_r[...] = jnp.zeros(car_r.shape, F32)
            car_i[...] = jnp.zeros(car_i.shape, F32)
            _powers_into(pw_r, pw_i, ar_ref[...], ai_ref[...], seg)
            for acc_ref in (dbbr_ref, dbbi_ref, dcr_ref, dci_ref, dar_ref, dai_ref):
                acc_ref[...] = jnp.zeros(acc_ref.shape, F32)

        dy = _interleaved(dy_ref, tmp_a, tmp_b, seg).astype(BF16)
        for q in range(nq):
            dyq = dy[:, q * S5_IN:(q + 1) * S5_IN]
            gr = lax.dot_general(dyq, cr_ref[q], _NT, preferred_element_type=F32)
            gi = lax.dot_general(dyq, ci_ref[q], _NT, preferred_element_type=F32)
            for s in range(per):
                g_r[q * per + s] = gr[:, s * LANES:(s + 1) * LANES]
                g_i[q * per + s] = gi[:, s * LANES:(s + 1) * LANES]
        for c in range(nc):
            x_r[c] = xr_ref[:, c * LANES:(c + 1) * LANES]
            x_i[c] = xi_ref[:, c * LANES:(c + 1) * LANES]
        acc_r[...] = jnp.zeros(acc_r.shape, F32)
        acc_i[...] = jnp.zeros(acc_i.shape, F32)

        def visit(c, rws, lr, li):
            xr_t, xi_t = x_r[c, rws, :], x_i[c, rws, :]
            acc_r[c] += lr * xr_t + li * xi_t
            acc_i[c] += li * xr_t - lr * xi_t

        _segment_scan(g_r, g_i, lam_r, lam_i, pw_r, pw_i, car_r, car_i, seg, -1.0, True, visit)
        for c in range(nc):
            dar_ref[:, c * LANES:(c + 1) * LANES] += jnp.sum(acc_r[c], axis=0, keepdims=True)
            dai_ref[:, c * LANES:(c + 1) * LANES] += jnp.sum(acc_i[c], axis=0, keepdims=True)
        u = _interleaved(u_ref, tmp_a, tmp_b, seg).astype(BF16)
        du = []
        for q in range(nq):
            st = slice(q * S5_ST, (q + 1) * S5_ST)
            io = slice(q * S5_IN, (q + 1) * S5_IN)
            lq_r = jnp.concatenate([lam_r[q * per + s] for s in range(per)], axis=1).astype(BF16)
            lq_i = jnp.concatenate([lam_i[q * per + s] for s in range(per)], axis=1).astype(BF16)
            du.append(lax.dot_general(lq_r, bbr_ref[q], _NT, preferred_element_type=F32)
                      + lax.dot_general(lq_i, bbi_ref[q], _NT, preferred_element_type=F32))
            dbbr_ref[q] += lax.dot_general(u[:, io], lq_r, _TN, preferred_element_type=F32)
            dbbi_ref[q] += lax.dot_general(u[:, io], lq_i, _TN, preferred_element_type=F32)
            dcr_ref[q] += lax.dot_general(xr_ref[:, st].astype(BF16), dy[:, io], _TN, preferred_element_type=F32)
            dci_ref[q] += lax.dot_general(xi_ref[:, st].astype(BF16), dy[:, io], _TN, preferred_element_type=F32)
        _store_deinterleaved(du_ref, jnp.concatenate(du, axis=1), tmp_a, tmp_b, seg)

    rows = lambda w: pl.BlockSpec((tl, w), lambda s, j, r: (s * nl + nl - 1 - r, j))
    chunk = lambda a: pl.BlockSpec((nq,) + a.shape[1:], lambda s, j, r: (j, 0, 0))
    acc = lambda a: pl.BlockSpec((None, nq) + a.shape[1:], lambda s, j, r: (s, j, 0, 0))
    par = pl.BlockSpec((1, cb), lambda s, j, r: (0, j))
    par_acc = pl.BlockSpec((None, 1, cb), lambda s, j, r: (s, 0, j))
    per_seq = lambda a: jax.ShapeDtypeStruct((seqs,) + a.shape, F32)
    return pl.pallas_call(
        body, name="s5_bwd", grid=(seqs, S5_CH // cb, nl),
        in_specs=[rows(nq * S5_IN), rows(nq * S5_IN), rows(cb), rows(cb), chunk(bbr), chunk(bbi), chunk(cr), chunk(ci),
                  par, par],
        out_specs=[rows(nq * S5_IN), acc(bbr), acc(bbi), acc(cr), acc(ci), par_acc, par_acc],
        out_shape=[jax.ShapeDtypeStruct((t, S5_WIDTH), F32), per_seq(bbr), per_seq(bbi), per_seq(cr), per_seq(ci),
                   jax.ShapeDtypeStruct((seqs, 1, S5_CH), F32), jax.ShapeDtypeStruct((seqs, 1, S5_CH), F32)],
        scratch_shapes=[pltpu.VMEM((1, cb), F32), pltpu.VMEM((1, cb), F32), pltpu.VMEM((seg, cb), F32),
                        pltpu.VMEM((seg, cb), F32)] + [pltpu.VMEM((nc, tl, LANES), F32)] * 6
        + [pltpu.VMEM((nc, SCAN_SEGS, LANES), F32)] * 2 + [pltpu.VMEM((nq * S5_IN // LANES, tl, LANES), F32)] * 2,
        compiler_params=_params(("parallel", "parallel", "arbitrary")),
    )(dys, uf, xr, xi, bbr, bbi, cr, ci, ar, ai)


XATT_BLOCK = 2048


def _xatt_probs(qv, kv):
    s = lax.dot_general(qv, kv, _NT, preferred_element_type=F32) * (X_HEAD_DIM ** -0.5)
    e = jnp.exp(s - jnp.max(s, axis=-1, keepdims=True))
    return e / jnp.sum(e, axis=-1, keepdims=True)


def _xatt_fwd(q, k, kv, seqs):
    t = q.shape[0]
    tq = min(XATT_BLOCK, t // seqs)
    nq = t // seqs // tq

    def body(q_ref, k_ref, v_ref, o_ref):
        p = _xatt_probs(q_ref[...], k_ref[...])
        o_ref[...] = jnp.dot(p.astype(BF16), v_ref[...].astype(BF16), preferred_element_type=F32).astype(o_ref.dtype)

    qs = pl.BlockSpec((tq, X_HEAD_DIM), lambda b, h, i: (b * nq + i, h))
    return pl.pallas_call(
        body, name="xatt_fwd", grid=(seqs, N_X_HEADS, nq),
        in_specs=[qs, pl.BlockSpec((N_MEM, X_HEAD_DIM), lambda b, h, i: (b, h)),
                  pl.BlockSpec((N_MEM, X_HEAD_DIM), lambda b, h, i: (b, N_X_HEADS + h))],
        out_specs=qs, out_shape=jax.ShapeDtypeStruct(q.shape, BF16),
        compiler_params=_params(("parallel", "parallel", "parallel")),
    )(q, k, kv)


def _xatt_bwd(q, k, kv, do, seqs):
    t = q.shape[0]
    tq = min(XATT_BLOCK, t // seqs)
    nq = t // seqs // tq
    scale = X_HEAD_DIM ** -0.5

    def body(q_ref, k_ref, v_ref, do_ref, dq_ref, dk_ref, dv_ref):
        @pl.when(pl.program_id(2) == 0)
        def _():
            dk_ref[...] = jnp.zeros(dk_ref.shape, F32)
            dv_ref[...] = jnp.zeros(dv_ref.shape, F32)

        qv, kk = q_ref[...], k_ref[...]
        p = _xatt_probs(qv, kk)
        dob = do_ref[...].astype(BF16)
        dp = lax.dot_general(dob, v_ref[...].astype(BF16), _NT, preferred_element_type=F32)
        ds = p * (dp - jnp.sum(dp * p, axis=-1, keepdims=True))
        dsb = ds.astype(BF16)
        dq_ref[...] = jnp.dot(dsb, kk, preferred_element_type=F32) * scale
        dk_ref[...] += lax.dot_general(dsb, qv, _TN, preferred_element_type=F32) * scale
        dv_ref[...] += lax.dot_general(p.astype(BF16), dob, _TN, preferred_element_type=F32)

    qs = pl.BlockSpec((tq, X_HEAD_DIM), lambda b, h, i: (b * nq + i, h))
    ks = pl.BlockSpec((N_MEM, X_HEAD_DIM), lambda b, h, i: (b, h))
    return pl.pallas_call(
        body, name="xatt_bwd", grid=(seqs, N_X_HEADS, nq),
        in_specs=[qs, ks, pl.BlockSpec((N_MEM, X_HEAD_DIM), lambda b, h, i: (b, N_X_HEADS + h)), qs],
        out_specs=[qs, ks, ks],
        out_shape=[jax.ShapeDtypeStruct(q.shape, F32), jax.ShapeDtypeStruct(k.shape, F32),
                   jax.ShapeDtypeStruct(k.shape, F32)],
        compiler_params=_params(("parallel", "parallel", "arbitrary")),
    )(q, k, kv, do)


CONV_COLS = 256


def _shift_down(x, k, row):
    return jnp.where(row >= k, pltpu.roll(x, k, 0), 0.0)


def _shift_up(x, k, row):
    n = x.shape[0]
    return jnp.where(row < n - k, pltpu.roll(x, n - k, 0), 0.0)


def _conv_pre(g, w, b, row):
    return b + w[0:1, :] * _shift_down(g, 2, row) + w[1:2, :] * _shift_down(g, 1, row) + w[2:3, :] * g


def _convgate_fwd(gu, w, b, seqs):
    t = gu.shape[0]
    l = t // seqs
    nc = D_FF // CONV_COLS

    def body(g_ref, u_ref, w_ref, b_ref, o_ref):
        g = g_ref[...].astype(F32)
        row = lax.broadcasted_iota(jnp.int32, g.shape, 0)
        pre = _conv_pre(g, w_ref[...], b_ref[...], row)
        o_ref[...] = (pre * jax.nn.sigmoid(pre) * u_ref[...].astype(F32)).astype(o_ref.dtype)

    return pl.pallas_call(
        body, name="convgate_fwd", grid=(seqs, nc),
        in_specs=[pl.BlockSpec((l, CONV_COLS), lambda s, j: (s, j)), pl.BlockSpec((l, CONV_COLS), lambda s, j: (s, nc + j)),
                  pl.BlockSpec((3, CONV_COLS), lambda s, j: (0, j)), pl.BlockSpec((1, CONV_COLS), lambda s, j: (0, j))],
        out_specs=pl.BlockSpec((l, CONV_COLS), lambda s, j: (s, j)),
        out_shape=jax.ShapeDtypeStruct((t, D_FF), BF16),
        compiler_params=_params(("parallel", "parallel")),
    )(gu, gu, w, b)


def _convgate_bwd(gu, w, b, dact, seqs):
    t = gu.shape[0]
    l = t // seqs
    nc = D_FF // CONV_COLS
    steps = nc * seqs

    def body(g_ref, u_ref, w_ref, b_ref, da_ref, dgu_ref, dw_ref, db_ref, stage, sems):
        j, s = pl.program_id(0), pl.program_id(1)
        n = j * seqs + s
        slot = n % 2

        def copies(slot_, j_, s_):
            rows = pl.ds(pl.multiple_of(s_ * l, 16), l)
            return [pltpu.make_async_copy(
                stage.at[slot_, half],
                dgu_ref.at[rows, pl.ds(pl.multiple_of((half * nc + j_) * CONV_COLS, 128), CONV_COLS)],
                sems.at[slot_, half]) for half in (0, 1)]

        @pl.when(s == 0)
        def _():
            dw_ref[...] = jnp.zeros(dw_ref.shape, F32)
            db_ref[...] = jnp.zeros(db_ref.shape, F32)

        @pl.when(n >= 2)
        def _():
            for cp in copies(slot, j, s):
                cp.wait()

        g, wv, da = g_ref[...].astype(F32), w_ref[...], da_ref[...].astype(F32)
        row = lax.broadcasted_iota(jnp.int32, g.shape, 0)
        g1, g2 = _shift_down(g, 1, row), _shift_down(g, 2, row)
        pre = b_ref[...] + wv[0:1, :] * g2 + wv[1:2, :] * g1 + wv[2:3, :] * g
        sg = jax.nn.sigmoid(pre)
        silu = pre * sg
        stage[slot, 1] = (da * silu).astype(stage.dtype)
        dpre = da * u_ref[...].astype(F32) * (sg * (1.0 + pre * (1.0 - sg)))
        dg = wv[2:3, :] * dpre + wv[1:2, :] * _shift_up(dpre, 1, row) + wv[0:1, :] * _shift_up(dpre, 2, row)
        stage[slot, 0] = dg.astype(stage.dtype)
        for cp in copies(slot, j, s):
            cp.start()
        dw_ref[0:1, :] += jnp.sum(dpre * g2, axis=0, keepdims=True)
        dw_ref[1:2, :] += jnp.sum(dpre * g1, axis=0, keepdims=True)
        dw_ref[2:3, :] += jnp.sum(dpre * g, axis=0, keepdims=True)
        db_ref[...] += jnp.sum(dpre, axis=0, keepdims=True)

        @pl.when(n == steps - 1)
        def _():
            for cp in copies(slot, j, s) + (copies(1 - slot, j, s) if steps > 1 else []):
                cp.wait()

    blk = lambda off: pl.BlockSpec((l, CONV_COLS), lambda j, s: (s, off + j))
    return pl.pallas_call(
        body, name="convgate_bwd", grid=(nc, seqs),
        in_specs=[blk(0), blk(nc), pl.BlockSpec((3, CONV_COLS), lambda j, s: (0, j)),
                  pl.BlockSpec((1, CONV_COLS), lambda j, s: (0, j)), blk(0)],
        out_specs=[ANY, pl.BlockSpec((3, CONV_COLS), lambda j, s: (0, j)),
                   pl.BlockSpec((1, CONV_COLS), lambda j, s: (0, j))],
        out_shape=[jax.ShapeDtypeStruct((t, 2 * D_FF), BF16), jax.ShapeDtypeStruct((3, D_FF), F32),
                   jax.ShapeDtypeStruct((1, D_FF), F32)],
        scratch_shapes=[pltpu.VMEM((2, 2, l, CONV_COLS), BF16), pltpu.SemaphoreType.DMA((2, 2))],
        compiler_params=_params(("arbitrary", "arbitrary")),
    )(gu, gu, w, b, dact)


def _loss_head(h, target):
    t, d = h.shape
    tm = _pick(t, (256, 128, 8))

    def body(h_ref, t_ref, dh_ref, dhb_ref, loss_ref):
        @pl.when(pl.program_id(0) == 0)
        def _():
            loss_ref[...] = jnp.zeros(loss_ref.shape, F32)

        e = h_ref[...] - t_ref[...]
        dh = e * (1.0 / d)
        dh_ref[...] = dh
        dhb_ref[...] = dh.astype(BF16)
        loss_ref[...] += (0.5 / d) * jnp.sum(jnp.sum(e * e, axis=1, keepdims=True), axis=0, keepdims=True)

    blk = pl.BlockSpec((tm, d), lambda i: (i, 0))
    return pl.pallas_call(
        body, name="loss_head", grid=(t // tm,), in_specs=[blk, blk],
        out_specs=[blk, blk, pl.BlockSpec((1, 1), lambda i: (0, 0))],
        out_shape=[jax.ShapeDtypeStruct((t, d), F32), jax.ShapeDtypeStruct((t, d), BF16),
                   jax.ShapeDtypeStruct((1, 1), F32)],
        compiler_params=_params(("arbitrary",)),
    )(h, target)


def _s5_discretise(a_re, a_im, log_dt, b_re, b_im):
    dt = jnp.exp(log_dt)[:, None]
    mag = jnp.exp(a_re * dt)
    lb_r = mag * jnp.cos(a_im * dt)
    lb_i = mag * jnp.sin(a_im * dt)
    den = a_re * a_re + a_im * a_im
    nr = lb_r - 1.0
    coef_r = (nr * a_re + lb_i * a_im) / den
    coef_i = (lb_i * a_re - nr * a_im) / den
    bb_r = coef_r[:, :, None] * b_re - coef_i[:, :, None] * b_im
    bb_i = coef_r[:, :, None] * b_im + coef_i[:, :, None] * b_re
    return lb_r, lb_i, bb_r, bb_i


S5_CHUNKS = 4
S5_PER = S5_GROUPS // S5_CHUNKS


def _blockdiag_in(bb):
    eye = jnp.eye(S5_PER, dtype=bb.dtype)
    return jnp.einsum("jgpc,gh->jgchp", bb.reshape(S5_CHUNKS, S5_PER, S5_STATE, S5_GROUP_CH), eye).reshape(
        S5_CHUNKS, S5_PER * S5_GROUP_CH, S5_PER * S5_STATE)


def _blockdiag_in_grad(d):
    eye = jnp.eye(S5_PER, dtype=d.dtype)
    return jnp.einsum("jgchp,gh->jgpc", d.reshape(S5_CHUNKS, S5_PER, S5_GROUP_CH, S5_PER, S5_STATE), eye).reshape(
        S5_GROUPS, S5_STATE, S5_GROUP_CH)


def _blockdiag_out(c):
    eye = jnp.eye(S5_PER, dtype=c.dtype)
    return jnp.einsum("jgcp,gh->jgphc", c.reshape(S5_CHUNKS, S5_PER, S5_GROUP_CH, S5_STATE), eye).reshape(
        S5_CHUNKS, S5_PER * S5_STATE, S5_PER * S5_GROUP_CH)


def _blockdiag_out_grad(d):
    eye = jnp.eye(S5_PER, dtype=d.dtype)
    return jnp.einsum("jgphc,gh->jgcp", d.reshape(S5_CHUNKS, S5_PER, S5_STATE, S5_PER, S5_GROUP_CH), eye).reshape(
        S5_GROUPS, S5_GROUP_CH, S5_STATE)


def _local_step(x3, mem3, target3, p, wb, late_weights=None, early_grads=None):
    seqs, l, d = x3.shape
    t = seqs * l
    x = x3.reshape(t, d)
    mem = mem3.reshape(seqs * N_MEM, d)
    target = target3.reshape(t, d)
    full = lambda a: (a, a.shape[1], 0, 0)

    s5_in = (p["s5_a_re"], p["s5_a_im"], p["s5_log_dt"], p["s5_b_re"], p["s5_b_im"])
    (lb_r, lb_i, bb_r, bb_i), s5_pull = jax.vjp(_s5_discretise, *s5_in)
    ar, ai = lb_r.reshape(1, S5_CH), lb_i.reshape(1, S5_CH)
    bbr_d, bbi_d = _blockdiag_in(bb_r).astype(BF16), _blockdiag_in(bb_i).astype(BF16)
    cr_d, ci_d = _blockdiag_out(p["s5_c_re"]).astype(BF16), (-_blockdiag_out(p["s5_c_im"])).astype(BF16)
    d_row = p["s5_d"].reshape(1, S5_WIDTH)

    w_in = wb["w_in"]
    w_qkv = w_in[:, :3 * FOX_WIDTH]
    w_uf = jnp.concatenate(
        [w_in[:, 3 * FOX_WIDTH + N_FOX_HEADS:], w_in[:, 3 * FOX_WIDTH:3 * FOX_WIDTH + N_FOX_HEADS],
         jnp.zeros((d, UF_COLS - S5_WIDTH - N_FOX_HEADS), w_in.dtype)], axis=1)

    hn1 = _rowwise(_rms, [full(x)], [p["norm_mix"]], [(d, d, 0, BF16)], "norm_mix_fwd")
    qkv = _mm(hn1, w_qkv, "nn", "in_qkv")
    uf = _mm(hn1, w_uf, "nn", "in_uf")

    bh = seqs * N_FOX_HEADS
    q_pair = (qkv, 128, 0, 1)
    k_pair = (qkv, 128, N_PAIRS, 1)
    gq2, gk2 = jnp.tile(p["fox_q_norm"], (1, 2)), jnp.tile(p["fox_k_norm"], (1, 2))
    pair_out = [(FOX_WIDTH, 128, 1, BF16)]
    qn = _rowwise(_rms_pair, [q_pair], [gq2], pair_out, "fox_qnorm_fwd", heads=N_PAIRS)
    kn = _rowwise(_rms_pair, [k_pair], [gk2], pair_out, "fox_knorm_fwd", heads=N_PAIRS)

    f_rows = uf[:, S5_WIDTH:S5_WIDTH + N_FOX_HEADS].reshape(seqs, l, N_FOX_HEADS).transpose(0, 2, 1).reshape(bh, l)
    f_bias = jnp.tile(p["fox_f_bias"].reshape(N_FOX_HEADS, 1), (seqs, 1))
    c_wide = jnp.broadcast_to(_forget_fwd(f_rows, f_bias)[:, :, None], (bh, l, 128))
    fox, lse = _fox_fwd(qn, kn, qkv, c_wide, seqs)

    xr, xi, ys = _s5_fwd(uf, bbr_d, bbi_d, cr_d, ci_d, ar, ai, seqs)
    u_blk = (uf, S5_WIDTH, 0, 0)
    yg = _rowwise(_s5_act, [full(ys), u_blk], [d_row], [(S5_WIDTH, S5_WIDTH, 0, F32)], "s5_act_fwd")
    if late_weights is not None:
        wb = dict(wb, **late_weights("mid", yg))
    z = _mm(yg, wb["s5_w_glu"], "nn", "s5_glu")
    y2n = _rowwise(_s5_gate, [full(yg), full(z)], [p["s5_b_glu"], p["out_norm_s5"]],
                   [(S5_WIDTH, S5_WIDTH, 0, BF16)], "s5_gate_fwd")
    foxn = _rowwise(_rms, [full(fox)], [p["out_norm_fox"]], [(FOX_WIDTH, FOX_WIDTH, 0, BF16)], "fox_outnorm_fwd")
    mixed = jnp.concatenate([foxn, y2n], axis=1)
    h1 = _mm(mixed, wb["w_out"], "nn", "mix_out", res=x)
    if late_weights is not None:
        wb = dict(wb, **late_weights("late", h1))

    hn2 = _rowwise(_rms, [full(h1)], [p["norm_cross"]], [(d, d, 0, BF16)], "norm_cross_fwd")
    mn = _rowwise(_rms, [full(mem)], [p["norm_mem"]], [(d, d, 0, BF16)], "norm_mem_fwd")
    xq_raw = _mm(hn2, wb["w_xq"], "nn", "x_q")
    kv = _mm(mn, wb["w_xkv"], "nn", "x_kv")
    xh = lambda a: (a, X_HEAD_DIM, 0, 1)
    xqn = _rowwise(_rms, [xh(xq_raw)], [p["xq_norm"]], [(d, X_HEAD_DIM, 1, BF16)], "x_qnorm_fwd", heads=N_X_HEADS)
    xkn = _rowwise(_rms, [xh(kv)], [p["xk_norm"]], [(d, X_HEAD_DIM, 1, BF16)], "x_knorm_fwd", heads=N_X_HEADS)
    xo = _xatt_fwd(xqn, xkn, kv, seqs)
    h2 = _mm(xo, wb["w_xo"], "nn", "x_out", res=h1)

    hn3 = _rowwise(_rms, [full(h2)], [p["norm_ffn"]], [(d, d, 0, BF16)], "norm_ffn_fwd")
    gu = _mm(hn3, wb["w_ffn_up"], "nn", "ffn_up", out_dtype=BF16)
    act = _convgate_fwd(gu, p["ffn_conv_w"], p["ffn_conv_b"], seqs)
    h3 = _mm(act, wb["w_ffn_down"], "nn", "ffn_down", res=h2)
    dh3, dh3_b, loss = _loss_head(h3, target)

    g = {}
    dact = _mm(dh3_b, wb["w_ffn_down"], "nt", "ffn_down_dx", out_dtype=BF16)
    late_dt = BF16 if early_grads is not None else F32
    g["w_ffn_down"] = _mm(act, dh3_b, "tn", "ffn_down_dw", out_dtype=late_dt)
    dgu, g["ffn_conv_w"], g["ffn_conv_b"] = _convgate_bwd(gu, p["ffn_conv_w"], p["ffn_conv_b"], dact, seqs)
    dhn3 = _mm(dgu, wb["w_ffn_up"], "nt", "ffn_up_dx")
    g["w_ffn_up"] = _mm(hn3, dgu, "tn", "ffn_up_dw", out_dtype=late_dt)
    (dh2,), (g["norm_ffn"],) = _rowwise_vjp(_rms, [full(h2)], [p["norm_ffn"]], [full(dhn3)], "norm_ffn_bwd",
                                            adds=[full(dh3)])

    dxo = _mm(dh2, wb["w_xo"], "nt", "x_out_dx")
    g["w_xo"] = _mm(xo, dh2, "tn", "x_out_dw", out_dtype=late_dt)
    dxqn, dxkn, dxv = _xatt_bwd(xqn, xkn, kv, dxo, seqs)
    (dxq_raw,), (g["xq_norm"],) = _rowwise_vjp(_rms, [xh(xq_raw)], [p["xq_norm"]], [xh(dxqn)], "x_qnorm_bwd",
                                               heads=N_X_HEADS, row_dtypes=[BF16])
    (dxk_raw,), (g["xk_norm"],) = _rowwise_vjp(_rms, [xh(kv)], [p["xk_norm"]], [xh(dxkn)], "x_knorm_bwd",
                                               heads=N_X_HEADS, row_dtypes=[BF16])
    dkv = jnp.concatenate([dxk_raw, dxv.astype(BF16)], axis=1)
    dhn2 = _mm(dxq_raw, wb["w_xq"], "nt", "x_q_dx")
    g["w_xq"] = _mm(hn2, dxq_raw, "tn", "x_q_dw", out_dtype=late_dt)
    dmn = _mm(dkv, wb["w_xkv"], "nt", "x_kv_dx")
    g["w_xkv"] = _mm(mn, dkv, "tn", "x_kv_dw", out_dtype=late_dt)
    norm_cross = p["norm_cross"]
    if early_grads is not None:
        norm_cross = norm_cross + early_grads({n: g[n] for n in LATE_WEIGHTS})
    (dh1,), (g["norm_cross"],) = _rowwise_vjp(_rms, [full(h1)], [norm_cross], [full(dhn2)], "norm_cross_bwd",
                                              adds=[full(dh2)])
    _, (g["norm_mem"],) = _rowwise_vjp(_rms, [full(mem)], [p["norm_mem"]], [full(dmn)], "norm_mem_bwd",
                                       row_dtypes=[BF16])

    dmixed = _mm(dh1, wb["w_out"], "nt", "mix_out_dx")
    g["w_out"] = _mm(mixed, dh1, "tn", "mix_out_dw", out_dtype=late_dt)
    (dfox,), (g["out_norm_fox"],) = _rowwise_vjp(_rms, [full(fox)], [p["out_norm_fox"]],
                                                 [(dmixed, FOX_WIDTH, 0, 0)], "fox_outnorm_bwd")
    (dyg_a, dz), (g["s5_b_glu"], g["out_norm_s5"]) = _rowwise_vjp(
        _s5_gate, [full(yg), full(z)], [p["s5_b_glu"], p["out_norm_s5"]], [(dmixed, S5_WIDTH, 1, 0)], "s5_gate_bwd",
        row_dtypes=[F32, BF16])
    dyg = _mm(dz, wb["s5_w_glu"], "nt", "s5_glu_dx", res=dyg_a)
    g["s5_w_glu"] = _mm(yg, dz, "tn", "s5_glu_dw", out_dtype=late_dt)
    (dys, du_a), (dd_row,) = _rowwise_vjp(_s5_act, [full(ys), u_blk], [d_row], [full(dyg)], "s5_act_bwd",
                                          row_dtypes=[BF16, F32])
    g["s5_d"] = dd_row
    du_b, dbbr_d, dbbi_d, dcr_d, dci_d, dar, dai = _s5_bwd(dys, uf, xr, xi, bbr_d, bbi_d, cr_d, ci_d, ar, ai, seqs)
    dbbr_d, dbbi_d, dcr_d, dci_d = (jnp.sum(a, axis=0) for a in (dbbr_d, dbbi_d, dcr_d, dci_d))
    d_lb_r = jnp.sum(dar, axis=0).reshape(S5_GROUPS, S5_STATE)
    d_lb_i = jnp.sum(dai, axis=0).reshape(S5_GROUPS, S5_STATE)
    g["s5_a_re"], g["s5_a_im"], g["s5_log_dt"], g["s5_b_re"], g["s5_b_im"] = s5_pull(
        (d_lb_r, d_lb_i, _blockdiag_in_grad(dbbr_d), _blockdiag_in_grad(dbbi_d)))
    g["s5_c_re"] = _blockdiag_out_grad(dcr_d)
    g["s5_c_im"] = -_blockdiag_out_grad(dci_d)

    dqn, dkn, dv, dc, dcq = _fox_bwd(qn, kn, qkv, c_wide, fox, dfox, lse, seqs)
    pair = lambda a: (a, 128, 0, 1)
    (dq_raw,), (dgq2,) = _rowwise_vjp(_rms_pair, [q_pair], [gq2], [pair(dqn)], "fox_qnorm_bwd", heads=N_PAIRS,
                                      row_dtypes=[BF16])
    (dk_raw,), (dgk2,) = _rowwise_vjp(_rms_pair, [k_pair], [gk2], [pair(dkn)], "fox_knorm_bwd", heads=N_PAIRS,
                                      row_dtypes=[BF16])
    g["fox_q_norm"] = dgq2[:, :HEAD_DIM] + dgq2[:, HEAD_DIM:]
    g["fox_k_norm"] = dgk2[:, :HEAD_DIM] + dgk2[:, HEAD_DIM:]
    df_rows, dfb = _forget_bwd(f_rows, f_bias, (dc + dcq).reshape(bh, l))
    g["fox_f_bias"] = jnp.sum(dfb.reshape(seqs, N_FOX_HEADS), axis=0)
    df = df_rows.reshape(seqs, N_FOX_HEADS, l).transpose(0, 2, 1).reshape(t, N_FOX_HEADS)
    dqkv = jnp.concatenate([dq_raw, dk_raw, dv.astype(BF16)], axis=1)
    duf = jnp.concatenate([du_a + du_b, df, jnp.zeros((t, UF_COLS - S5_WIDTH - N_FOX_HEADS), F32)],
                          axis=1).astype(BF16)
    dhn1 = _mm(duf, w_uf, "nt", "in_uf_dx", res=_mm(dqkv, w_qkv, "nt", "in_qkv_dx"))
    dw_qkv = _mm(hn1, dqkv, "tn", "in_qkv_dw")
    dw_uf = _mm(hn1, duf, "tn", "in_uf_dw")
    g["w_in"] = jnp.concatenate([dw_qkv, dw_uf[:, S5_WIDTH:S5_WIDTH + N_FOX_HEADS], dw_uf[:, :S5_WIDTH]], axis=1)
    (dx,), (g["norm_mix"],) = _rowwise_vjp(_rms, [full(x)], [p["norm_mix"]], [full(dhn1)], "norm_mix_bwd",
                                           adds=[full(dh1)])
    return loss, dx.reshape(seqs, l, d), g


def _place():
    return lax.axis_index("x"), lax.axis_index("y"), lax.axis_index("c")


def _other_chips(x, y):
    return [(1 - x, y), (x, 1 - y), (1 - x, 1 - y)]


ANY = pl.BlockSpec(memory_space=pl.ANY)


def _gather_weights(shards, col_kind, taps):
    n = len(shards)

    def body(*refs):
        ins, tap_in, outs, tap_out = refs[:n], refs[n], refs[n + 1:2 * n + 1], refs[2 * n + 1]
        ici_send, ici_recv, d2d_send, d2d_recv, own_send, own_recv = refs[2 * n + 2:]
        x, y, c = _place()
        mine = 2 * x + y
        chips = _other_chips(x, y)
        sibling = (x, y, 1 - c)

        def piece(a, s, h):
            r, cs = ins[a].shape
            hr = r // 2
            if col_kind[a]:
                return outs[a].at[pl.ds(pl.multiple_of(h * hr, 16), hr), pl.ds(pl.multiple_of(s * cs, 128), cs)]
            return outs[a].at[pl.ds(pl.multiple_of(s * r + h * hr, 16), hr), :]

        def slab(a, s):
            r, cs = ins[a].shape
            if col_kind[a]:
                return outs[a].at[:, pl.ds(pl.multiple_of(s * cs, 128), cs)]
            return outs[a].at[pl.ds(pl.multiple_of(s * r, 16), r), :]

        def own_half(a, h):
            hr = ins[a].shape[0] // 2
            return ins[a].at[pl.ds(pl.multiple_of(h * hr, 16), hr), :]

        sends = []
        for a in range(n):
            cp = pltpu.make_async_remote_copy(
                src_ref=ins[a], dst_ref=slab(a, mine), send_sem=own_send.at[a], recv_sem=own_recv.at[a],
                device_id=sibling, device_id_type=MESH)
            cp.start()
            sends.append(cp)
        cp = pltpu.make_async_remote_copy(
            src_ref=tap_in, dst_ref=tap_out.at[mine], send_sem=own_send.at[n], recv_sem=own_recv.at[n],
            device_id=sibling, device_id_type=MESH)
        cp.start()
        sends.append(cp)
        for a in range(n):
            for j, (px, py) in enumerate(chips):
                cp = pltpu.make_async_remote_copy(
                    src_ref=own_half(a, c), dst_ref=piece(a, mine, c), send_sem=ici_send.at[3 * a + j],
                    recv_sem=ici_recv.at[3 * a + j], device_id=(px, py, c), device_id_type=MESH)
                cp.start()
                sends.append(cp)
        for j, (px, py) in enumerate(chips):
            cp = pltpu.make_async_remote_copy(
                src_ref=tap_in, dst_ref=tap_out.at[mine], send_sem=ici_send.at[3 * n + j],
                recv_sem=ici_recv.at[3 * n + j], device_id=(px, py, c), device_id_type=MESH)
            cp.start()
            sends.append(cp)
        for a in range(n):
            for j, (px, py) in enumerate(chips):
                got = piece(a, 2 * px + py, c)
                pltpu.make_async_remote_copy(
                    src_ref=got, dst_ref=got, send_sem=ici_send.at[3 * a + j], recv_sem=ici_recv.at[3 * a + j],
                    device_id=(px, py, c), device_id_type=MESH).wait_recv()
                fwd = pltpu.make_async_remote_copy(
                    src_ref=got, dst_ref=got, send_sem=d2d_send.at[3 * a + j], recv_sem=d2d_recv.at[3 * a + j],
                    device_id=(x, y, 1 - c), device_id_type=MESH)
                fwd.start()
                sends.append(fwd)
        for a in range(n):
            for j, (px, py) in enumerate(chips):
                other = piece(a, 2 * px + py, 1 - c)
                pltpu.make_async_remote_copy(
                    src_ref=other, dst_ref=other, send_sem=d2d_send.at[3 * a + j], recv_sem=d2d_recv.at[3 * a + j],
                    device_id=(x, y, 1 - c), device_id_type=MESH).wait_recv()
        for j, (px, py) in enumerate(chips):
            pltpu.make_async_remote_copy(
                src_ref=tap_in, dst_ref=tap_out.at[2 * px + py], send_sem=ici_send.at[3 * n + j],
                recv_sem=ici_recv.at[3 * n + j], device_id=(px, py, c), device_id_type=MESH).wait_recv()
        for a in range(n):
            pltpu.make_async_remote_copy(
                src_ref=ins[a], dst_ref=slab(a, mine), send_sem=own_send.at[a], recv_sem=own_recv.at[a],
                device_id=sibling, device_id_type=MESH).wait_recv()
        pltpu.make_async_remote_copy(
            src_ref=tap_in, dst_ref=tap_out.at[mine], send_sem=own_send.at[n], recv_sem=own_recv.at[n],
            device_id=sibling, device_id_type=MESH).wait_recv()
        for cp in sends:
            cp.wait_send()

    def full_shape(a):
        r, cs = shards[a].shape
        return (r, 4 * cs) if col_kind[a] else (4 * r, cs)

    res = pl.pallas_call(
        body, name="gather_weights", in_specs=[ANY] * (n + 1), out_specs=[ANY] * (n + 1),
        out_shape=[jax.ShapeDtypeStruct(full_shape(a), shards[a].dtype) for a in range(n)]
        + [jax.ShapeDtypeStruct((4,) + taps.shape, taps.dtype)],
        scratch_shapes=[pltpu.SemaphoreType.DMA((3 * n + 3,)), pltpu.SemaphoreType.DMA((3 * n + 3,)),
                        pltpu.SemaphoreType.DMA((3 * n,)), pltpu.SemaphoreType.DMA((3 * n,)),
                        pltpu.SemaphoreType.DMA((n + 1,)), pltpu.SemaphoreType.DMA((n + 1,))],
        compiler_params=pltpu.CompilerParams(has_side_effects=True),
    )(*shards, taps)
    return res[:n], res[n]


HBM = pl.BlockSpec(memory_space=pltpu.HBM)
SEM = pl.BlockSpec(memory_space=pltpu.SEMAPHORE)
DATAFLOW = pltpu.SideEffectType.DATAFLOW_SIDE_EFFECTING


def _in_hbm(a):
    return pltpu.with_memory_space_constraint(a, pltpu.HBM)


def _split_start(name, srcs, lands, n_copies, plan):
    n = len(srcs)

    def body(*refs):
        src_refs, land_refs = refs[:n], refs[n:2 * n]
        send_sems, recv_sems = refs[2 * n], refs[2 * n + 1]
        for i, (src, dst, dev) in enumerate(plan(src_refs, land_refs)):
            pltpu.make_async_remote_copy(src_ref=src, dst_ref=dst, send_sem=send_sems.at[i], recv_sem=recv_sems.at[i],
                                         device_id=dev, device_id_type=MESH).start()
        refs[-1][...] = jnp.zeros((8, 128), F32)

    res = pl.pallas_call(
        body, name=name, in_specs=[HBM] * (2 * n),
        out_specs=[SEM, SEM] + [HBM] * (2 * n) + [pl.BlockSpec(memory_space=pltpu.VMEM)],
        out_shape=[pltpu.SemaphoreType.DMA((n_copies,)), pltpu.SemaphoreType.DMA((n_copies,))]
        + [pltpu.HBM(a.shape, a.dtype) for a in list(srcs) + list(lands)] + [jax.ShapeDtypeStruct((8, 128), F32)],
        input_output_aliases={i: 2 + i for i in range(2 * n)},
        compiler_params=pltpu.CompilerParams(has_side_effects=DATAFLOW),
    )(*[_in_hbm(a) for a in list(srcs) + list(lands)])
    return res[0], res[1], list(res[2:2 + n]), list(res[2 + n:2 + 2 * n]), res[-1]


def _split_wait(name, send_sems, recv_sems, srcs, lands, after, plan):
    n = len(srcs)

    def body(*refs):
        src_refs, land_refs = refs[:n], refs[n:2 * n]
        send_ref, recv_ref = refs[2 * n], refs[2 * n + 1]
        for i, (src, dst, dev) in enumerate(plan(src_refs, land_refs)):
            cp = pltpu.make_async_remote_copy(src_ref=src, dst_ref=dst, send_sem=send_ref.at[i], recv_sem=recv_ref.at[i],
                                              device_id=dev, device_id_type=MESH)
            cp.wait_send()
            cp.wait_recv()

    res = pl.pallas_call(
        body, name=name, in_specs=[HBM] * (2 * n) + [SEM, SEM, ANY], out_specs=[HBM] * (2 * n),
        out_shape=[pltpu.HBM(a.shape, a.dtype) for a in list(srcs) + list(lands)],
        input_output_aliases={i: i for i in range(2 * n)},
        compiler_params=pltpu.CompilerParams(has_side_effects=DATAFLOW),
    )(*srcs, *lands, send_sems, recv_sems, after)
    return list(res[:n]), list(res[n:])


def _late_gather_plan(col_kind):
    def plan(src_refs, land_refs):
        x, y, c = _place()
        mine = 2 * x + y
        copies = []
        for a, (src, land) in enumerate(zip(src_refs, land_refs)):
            r, cs = src.shape
            if col_kind[a]:
                dst = land.at[:, pl.ds(pl.multiple_of(mine * cs, 128), cs)]
            else:
                dst = land.at[pl.ds(pl.multiple_of(mine * r, 16), r), :]
            copies.append((src, dst, (x, y, 1 - c)))
            copies += [(src, dst, (px, py, c)) for (px, py) in _other_chips(x, y)]
        return copies
    return plan


def _late_reduce_plan(col_kind):
    def plan(src_refs, land_refs):
        x, y, c = _place()
        copies = []
        for a, (src, land) in enumerate(zip(src_refs, land_refs)):
            for j, (px, py) in enumerate(_other_chips(x, y)):
                if col_kind[a]:
                    cs = land.shape[2]
                    piece = src.at[:, pl.ds(pl.multiple_of((2 * px + py) * cs, 128), cs)]
                else:
                    piece = src.at[2 * px + py]
                copies.append((piece, land.at[j], (px, py, c)))
        return copies
    return plan


def _pair_swap(name, halves):
    n = len(halves)

    def body(*refs):
        ins, outs = refs[:n], refs[n:2 * n]
        send_sems, recv_sems = refs[2 * n:]
        x, y, c = _place()
        copies = []
        for a in range(n):
            cp = pltpu.make_async_remote_copy(
                src_ref=ins[a], dst_ref=outs[a], send_sem=send_sems.at[a], recv_sem=recv_sems.at[a],
                device_id=(x, y, 1 - c), device_id_type=MESH)
            cp.start()
            copies.append(cp)
        for cp in copies:
            cp.wait()

    return pl.pallas_call(
        body, name=name, in_specs=[ANY] * n, out_specs=[ANY] * n,
        out_shape=[jax.ShapeDtypeStruct(s.shape, s.dtype) for s in halves],
        scratch_shapes=[pltpu.SemaphoreType.DMA((n,)), pltpu.SemaphoreType.DMA((n,))],
        compiler_params=pltpu.CompilerParams(has_side_effects=True),
    )(*halves)


def _chip_sum(name, chip_sel, own, col, others):
    _, r, c = others.shape
    tr = _pick(r, (256, 128, 64, 32, 16))
    if col:
        own_spec = pl.BlockSpec((tr, c), lambda i, s: (i, s[0]))
    else:
        own_spec = pl.BlockSpec((None, tr, c), lambda i, s: (s[0], i, 0))
    specs = [own_spec] + [pl.BlockSpec((None, tr, c), lambda i, s, k=k: (k, i, 0)) for k in range(3)]

    def body(s_ref, own_ref, r0, r1, r2, o_ref):
        o_ref[...] = ((own_ref[...].astype(F32) + r0[...].astype(F32)) + r1[...].astype(F32)) + r2[...].astype(F32)

    return pl.pallas_call(
        body, name=name,
        grid_spec=pltpu.PrefetchScalarGridSpec(
            num_scalar_prefetch=1, grid=(r // tr,), in_specs=specs,
            out_specs=pl.BlockSpec((tr, c), lambda i, s: (i, 0))),
        out_shape=jax.ShapeDtypeStruct((r, c), F32),
        compiler_params=_params(("parallel",)),
    )(chip_sel, own, others, others, others)


def _allreduce_small(vals):
    sizes = [int(math.prod(v.shape)) for v in vals]
    padded = [-(-s // 128) * 128 for s in sizes]
    total = -(-sum(padded) // 1024) * 1024
    flat = [jnp.pad(v.reshape(-1), (0, p - s)) for v, s, p in zip(vals, sizes, padded)]
    flat.append(jnp.zeros((total - sum(padded),), F32))
    packed = jnp.concatenate(flat).reshape(total // 128, 128)

    def body(in_ref, out_ref, r0, r1, r2, send_sems, recv_sems):
        x, y, c = _place()
        out_ref[...] = in_ref[...]
        for k, (peer, land) in enumerate(zip([(x, y, 1 - c), (1 - x, y, c), (x, 1 - y, c)], (r0, r1, r2))):
            cp = pltpu.make_async_remote_copy(
                src_ref=out_ref, dst_ref=land, send_sem=send_sems.at[k], recv_sem=recv_sems.at[k],
                device_id=peer, device_id_type=MESH)
            cp.start()
            cp.wait()
            out_ref[...] = out_ref[...] + land[...]

    vm = pl.BlockSpec(memory_space=pltpu.VMEM)
    summed = pl.pallas_call(
        body, name="allreduce_small", in_specs=[vm], out_specs=vm,
        out_shape=jax.ShapeDtypeStruct(packed.shape, F32),
        scratch_shapes=[pltpu.VMEM(packed.shape, F32)] * 3
        + [pltpu.SemaphoreType.DMA((3,)), pltpu.SemaphoreType.DMA((3,))],
        compiler_params=pltpu.CompilerParams(has_side_effects=True, vmem_limit_bytes=VMEM_LIMIT_BYTES),
    )(packed).reshape(-1)
    outs, off = [], 0
    for v, s, p in zip(vals, sizes, padded):
        outs.append(summed[off:off + s].reshape(v.shape))
        off += p
    return outs


def _adamw_math(w, g, m, v):
    m2 = ADAM_B1 * m + (1.0 - ADAM_B1) * g
    v2 = ADAM_B2 * v + (1.0 - ADAM_B2) * (g * g)
    m_hat = m2 / (1.0 - ADAM_B1 ** ADAM_STEP)
    v_hat = v2 / (1.0 - ADAM_B2 ** ADAM_STEP)
    delta = -ADAM_LR * (m_hat / (jnp.sqrt(v_hat) + ADAM_EPS) + ADAM_WD * w)
    return delta, m2, v2


def _adamw_big(name, w, g_mine, g_sibling, m, v):
    _, r, c = w.shape

    def body(w_ref, ga_ref, gb_ref, m_ref, v_ref, go_ref, d_ref, mo_ref, vo_ref):
        gv = ga_ref[...] + gb_ref[...]
        d, m2, v2 = _adamw_math(w_ref[...], gv, m_ref[...], v_ref[...])
        go_ref[...] = gv
        d_ref[...] = d
        mo_ref[...] = m2
        vo_ref[...] = v2

    tr = _pick(r, (256, 128, 64, 32, 16, 8))
    if r % tr == 0 and tr % 8 == 0:
        grid = (r // tr,)
        blk = pl.BlockSpec((None, tr, c), lambda i: (0, i, 0))
        part = pl.BlockSpec((tr, c), lambda i: (i, 0))
    else:
        grid = (c // 512,)
        blk = pl.BlockSpec((None, r, 512), lambda i: (0, 0, i))
        part = pl.BlockSpec((r, 512), lambda i: (0, i))
    return pl.pallas_call(
        body, name=name, grid=grid, in_specs=[blk, part, part, blk, blk], out_specs=[blk] * 4,
        out_shape=[jax.ShapeDtypeStruct((1, r, c), F32)] * 4, compiler_params=_params(("parallel",)),
    )(w, g_mine, g_sibling, m, v)


def _adamw_small(ws, gs, ms, vs):
    n = len(ws)

    def body(*refs):
        w_r, g_r, m_r, v_r = refs[:n], refs[n:2 * n], refs[2 * n:3 * n], refs[3 * n:4 * n]
        o = refs[4 * n:]
        for a in range(n):
            gv = g_r[a][...]
            d, m2, v2 = _adamw_math(w_r[a][...], gv, m_r[a][...], v_r[a][...])
            o[a][...] = gv
            o[n + a][...] = d
            o[2 * n + a][...] = m2
            o[3 * n + a][...] = v2

    res = pl.pallas_call(
        body, name="adamw_small", out_shape=[jax.ShapeDtypeStruct(w.shape, F32) for _ in range(4) for w in ws],
        compiler_params=_params(),
    )(*ws, *gs, *ms, *vs)
    return res[:n], res[n:2 * n], res[2 * n:3 * n], res[3 * n:]


def _full_from_gathered(name, gathered):
    if name == "w_in":
        rows = gathered.shape[0] // 4
        return gathered.reshape(4, rows, gathered.shape[1]).transpose(1, 0, 2).reshape(rows, 4 * gathered.shape[1])
    return gathered


def _reduce_layout(name, full):
    if name in COL_KIND:
        return full
    if name == "w_in":
        rows, cols = full.shape
        return full.reshape(rows, 4, cols // 4).transpose(1, 0, 2)
    return full.reshape(4, full.shape[0] // 4, full.shape[1])


def kernel(x, mem, norm_mix, w_in, fox_q_norm, fox_k_norm, fox_f_bias, s5_a_re, s5_a_im, s5_log_dt, s5_b_re, s5_b_im, s5_c_re, s5_c_im, s5_d, s5_w_glu, s5_b_glu, out_norm_fox, out_norm_s5, w_out, norm_cross, norm_mem, w_xq, w_xkv, xq_norm, xk_norm, w_xo, norm_ffn, w_ffn_up, ffn_conv_w, ffn_conv_b, w_ffn_down, loss_target, m_norm_mix, m_w_in, m_fox_q_norm, m_fox_k_norm, m_fox_f_bias, m_s5_a_re, m_s5_a_im, m_s5_log_dt, m_s5_b_re, m_s5_b_im, m_s5_c_re, m_s5_c_im, m_s5_d, m_s5_w_glu, m_s5_b_glu, m_out_norm_fox, m_out_norm_s5, m_w_out, m_norm_cross, m_norm_mem, m_w_xq, m_w_xkv, m_xq_norm, m_xk_norm, m_w_xo, m_norm_ffn, m_w_ffn_up, m_ffn_conv_w, m_ffn_conv_b, m_w_ffn_down, v_norm_mix, v_w_in, v_fox_q_norm, v_fox_k_norm, v_fox_f_bias, v_s5_a_re, v_s5_a_im, v_s5_log_dt, v_s5_b_re, v_s5_b_im, v_s5_c_re, v_s5_c_im, v_s5_d, v_s5_w_glu, v_s5_b_glu, v_out_norm_fox, v_out_norm_s5, v_w_out, v_norm_cross, v_norm_mem, v_w_xq, v_w_xkv, v_xq_norm, v_xk_norm, v_w_xo, v_norm_ffn, v_w_ffn_up, v_ffn_conv_w, v_ffn_conv_b, v_w_ffn_down):
    given = dict(locals())
    w = {n: given[n] for n in WEIGHTS}
    m = {n: given["m_" + n] for n in WEIGHTS}
    v = {n: given["v_" + n] for n in WEIGHTS}
    xi, yi, _ = _place()
    chip = (2 * xi + yi).astype(jnp.int32)
    chip_sel = chip.reshape(1)
    early_kind = [n in COL_KIND for n in EARLY_WEIGHTS]
    late_kind = [n in COL_KIND for n in LATE_WEIGHTS]

    gathered, taps = _gather_weights([w[FIRST_WEIGHT][0].astype(BF16)], [False], w["ffn_conv_w"][0])
    first_full = gathered[0]
    conv_w = taps.transpose(1, 0, 2).reshape(3, D_FF)
    pending = {}
    g_started = None
    for stage, names in (("mid", MID_WEIGHTS), ("late", LATE_WEIGHTS)):
        kinds = [n in COL_KIND for n in names]
        shards = [w[n][0].astype(BF16) for n in names]
        if g_started is None:
            first_full, shards[0] = lax.optimization_barrier((first_full, shards[0]))
        else:
            shards[0] = shards[0] + g_started[0:1, 0:1].astype(BF16)
        full = [lax.empty((s.shape[0], 4 * s.shape[1]) if ck else (4 * s.shape[0], s.shape[1]), BF16)
                for s, ck in zip(shards, kinds)]
        plan = _late_gather_plan(kinds)
        send, recv, srcs, lands, g_started = _split_start(
            "gather_" + stage + "_start", shards, full, 4 * len(names), plan)
        pending[stage] = (names, plan, send, recv, srcs, lands)
    wb = {FIRST_WEIGHT: _full_from_gathered(FIRST_WEIGHT, first_full)}

    def late_weights(stage, after):
        names, plan, send, recv, srcs, lands = pending[stage]
        _, full = _split_wait("gather_" + stage + "_wait", send, recv, srcs, lands, after, plan)
        return dict(zip(names, full))

    reduce_plan = _late_reduce_plan(late_kind)
    late_reduce = {}

    def early_grads(late_g):
        grads = [_reduce_layout(n, late_g[n]) for n in LATE_WEIGHTS]
        lands = [lax.empty((3, s.shape[0], s.shape[1] // 4) if ck else (3,) + s.shape[1:], BF16)
                 for s, ck in zip(grads, late_kind)]
        late_reduce["sems"] = _split_start("reduce_late_start", grads, lands, 3 * len(LATE_WEIGHTS), reduce_plan)
        return late_reduce["sems"][4][0:1, 0:1]

    p = {n: w[n][0] for n in SMALL}
    p["ffn_conv_w"] = conv_w
    for n in ("norm_mix", "fox_q_norm", "fox_k_norm", "fox_f_bias", "s5_b_glu", "out_norm_fox", "out_norm_s5",
              "norm_cross", "norm_mem", "xq_norm", "xk_norm", "norm_ffn", "ffn_conv_b"):
        p[n] = p[n].reshape(1, -1)
    p["norm_mix"] = p["norm_mix"] + g_started[0:1, 0:1]
    loss, grad_x, g = _local_step(x, mem, loss_target, p, wb, late_weights, early_grads)

    grads = [_reduce_layout(n, g[n].astype(BF16)) for n in EARLY_WEIGHTS]
    early_lands = [lax.empty((3, s.shape[0], s.shape[1] // 4) if ck else (3,) + s.shape[1:], BF16)
                   for s, ck in zip(grads, early_kind)]
    early_plan = _late_reduce_plan(early_kind)
    e_send, e_recv, e_srcs, e_lands, e_started = _split_start(
        "reduce_early_start", grads, early_lands, 3 * len(EARLY_WEIGHTS), early_plan)

    out_g, out_d, out_m, out_v = {}, {}, {}, {}

    def finish(names, kinds, sums, from_chips, tag):
        mine = [_chip_sum("reduce_chip_sum_" + n, chip_sel, ps, ck, fc)
                for n, ps, fc, ck in zip(names, sums, from_chips, kinds)]
        theirs = _pair_swap("reduce_pair_swap_" + tag, mine)
        for n, a, b in zip(names, mine, theirs):
            if n == "w_in":
                flip = lambda t: jnp.swapaxes(t, -1, -2)
                res = _adamw_big("adamw_" + n, flip(w[n]), flip(a), flip(b), flip(m[n]), flip(v[n]))
                out_g[n], out_d[n], out_m[n], out_v[n] = (flip(t) for t in res)
                continue
            out_g[n], out_d[n], out_m[n], out_v[n] = _adamw_big("adamw_" + n, w[n], a, b, m[n], v[n])

    r_send, r_recv, r_srcs, r_lands, _ = late_reduce["sems"]
    late_sums, late_from_chips = _split_wait("reduce_late_wait", r_send, r_recv, r_srcs, r_lands, e_started,
                                             reduce_plan)
    finish(LATE_WEIGHTS, late_kind, late_sums, late_from_chips, "late")

    small_names = list(SMALL) + ["ffn_conv_w"]
    small_vals = [g[n].reshape(w[n].shape if n != "ffn_conv_w" else (1, 3, D_FF)) for n in small_names]
    last = LATE_WEIGHTS[-1]
    loss, out_v[last] = lax.optimization_barrier((loss, out_v[last]))
    reduced = _allreduce_small(small_vals + [loss])
    loss_all = reduced[-1].reshape(())
    conv_w_grad = lax.dynamic_slice_in_dim(reduced[-2], chip * (D_FF // 4), D_FF // 4, axis=2)
    sg, sd, sm, sv = _adamw_small(
        [w[n] for n in small_names], list(reduced[:len(SMALL)]) + [conv_w_grad],
        [m[n] for n in small_names], [v[n] for n in small_names])
    out_g.update(zip(small_names, sg))
    out_d.update(zip(small_names, sd))
    out_m.update(zip(small_names, sm))
    out_v.update(zip(small_names, sv))

    early_sums, early_from_chips = _split_wait("reduce_early_wait", e_send, e_recv, e_srcs, e_lands, reduced[0],
                                               early_plan)
    finish(EARLY_WEIGHTS, early_kind, early_sums, early_from_chips, "early")

    return (loss_all, grad_x, *[out_g[n] for n in WEIGHTS], *[out_d[n] for n in WEIGHTS],
            *[out_m[n] for n in WEIGHTS], *[out_v[n] for n in WEIGHTS])
```

```python
import math

import jax
import jax.numpy as jnp
from jax import lax
from jax.experimental import pallas as pl
from jax.experimental.pallas import tpu as pltpu

F32 = jnp.float32
BF16 = jnp.bfloat16

D_MODEL = 1024
FOX_WIDTH = 512
HEAD_DIM = 64
N_FOX_HEADS = 8
S5_WIDTH = 512
S5_GROUP_CH = 16
S5_GROUPS = 32
S5_STATE = 64
S5_CH = S5_GROUPS * S5_STATE
N_X_HEADS = 4
X_HEAD_DIM = 256
N_MEM = 256
D_FF = 2816
UF_COLS = 640
EPS = 1e-6
ADAM_LR = 0.001
ADAM_B1 = 0.9
ADAM_B2 = 0.999
ADAM_EPS = 1e-08
ADAM_WD = 0.01
ADAM_STEP = 10

VMEM_LIMIT_BYTES = 56 * 1024 * 1024
MM_BLOCK_BYTES = 6 * 1024 * 1024
MM_VMEM_BYTES = 40 * 1024 * 1024
MM_TILE_MAX = 1536
MESH = pl.DeviceIdType.MESH

FIRST_WEIGHT = "w_in"
MID_WEIGHTS = ("s5_w_glu", "w_out")
EARLY_WEIGHTS = (FIRST_WEIGHT,) + MID_WEIGHTS
LATE_WEIGHTS = ("w_xq", "w_xkv", "w_xo", "w_ffn_up", "w_ffn_down")
BIG = EARLY_WEIGHTS + LATE_WEIGHTS
COL_KIND = ("w_xkv", "w_ffn_up")
SMALL = ("norm_mix", "fox_q_norm", "fox_k_norm", "fox_f_bias", "s5_a_re", "s5_a_im", "s5_log_dt",
         "s5_b_re", "s5_b_im", "s5_c_re", "s5_c_im", "s5_d", "s5_b_glu", "out_norm_fox", "out_norm_s5",
         "norm_cross", "norm_mem", "xq_norm", "xk_norm", "norm_ffn", "ffn_conv_b")
WEIGHTS = ("norm_mix", "w_in", "fox_q_norm", "fox_k_norm", "fox_f_bias", "s5_a_re", "s5_a_im", "s5_log_dt",
           "s5_b_re", "s5_b_im", "s5_c_re", "s5_c_im", "s5_d", "s5_w_glu", "s5_b_glu", "out_norm_fox",
           "out_norm_s5", "w_out", "norm_cross", "norm_mem", "w_xq", "w_xkv", "xq_norm", "xk_norm", "w_xo",
           "norm_ffn", "w_ffn_up", "ffn_conv_w", "ffn_conv_b", "w_ffn_down")


def _params(sem=None):
    return pltpu.CompilerParams(dimension_semantics=sem, vmem_limit_bytes=VMEM_LIMIT_BYTES)


def _pick(n, cands):
    for c in cands:
        if n % c == 0:
            return c
    return n


_DIMS = {"nn": (((1,), (0,)), ((), ())), "nt": (((1,), (1,)), ((), ())), "tn": (((0,), (0,)), ((), ()))}


def _mm(a, b, mode, name, out_dtype=F32, res=None):
    if mode == "nn":
        (m, k), (k2, n) = a.shape, b.shape
    elif mode == "nt":
        (m, k), (n, k2) = a.shape, b.shape
    else:
        (k, m), (k2, n) = a.shape, b.shape
    assert k == k2, (name, a.shape, b.shape)

    has_res = res is not None
    a_size, b_size = a.dtype.itemsize, b.dtype.itemsize
    o_size = jnp.dtype(out_dtype).itemsize + (res.dtype.itemsize if has_res else 0)

    def tiles(dim):
        return [c for c in range(MM_TILE_MAX, 0, -128) if dim % c == 0] or [dim]

    best = None
    for tm in tiles(m):
        for tn in tiles(n):
            a_blk, b_blk = tm * k * a_size, tn * k * b_size
            if max(a_blk, b_blk) > MM_BLOCK_BYTES or 2 * (a_blk + b_blk + tm * tn * o_size) > MM_VMEM_BYTES:
                continue
            for rows_outer in (True, False):
                moved = (m * k * a_size + (m // tm) * n * k * b_size) if rows_outer else \
                        (n * k * b_size + (n // tn) * m * k * a_size)
                key = (moved, -(tm * tn))
                if best is None or key < best[0]:
                    best = (key, tm, tn, rows_outer)
    assert best is not None, (name, a.shape, b.shape)
    _, tm, tn, rows_outer = best
    ij = (lambda g0, g1: (g0, g1)) if rows_outer else (lambda g0, g1: (g1, g0))
    if mode == "tn":
        a_spec = pl.BlockSpec((k, tm), lambda g0, g1: (0, ij(g0, g1)[0]))
    else:
        a_spec = pl.BlockSpec((tm, k), lambda g0, g1: (ij(g0, g1)[0], 0))
    if mode == "nt":
        b_spec = pl.BlockSpec((tn, k), lambda g0, g1: (ij(g0, g1)[1], 0))
    else:
        b_spec = pl.BlockSpec((k, tn), lambda g0, g1: (0, ij(g0, g1)[1]))
    o_spec = pl.BlockSpec((tm, tn), lambda g0, g1: ij(g0, g1))
    grid = (m // tm, n // tn) if rows_outer else (n // tn, m // tm)
    dims = _DIMS[mode]

    def body(*refs):
        a_ref, b_ref = refs[0], refs[1]
        o_ref = refs[-1]
        acc = lax.dot_general(a_ref[...].astype(BF16), b_ref[...].astype(BF16), dims, preferred_element_type=F32)
        if has_res:
            acc = acc + refs[2][...].astype(F32)
        o_ref[...] = acc.astype(o_ref.dtype)

    return pl.pallas_call(
        body, name=name, grid=grid,
        in_specs=[a_spec, b_spec] + ([o_spec] if has_res else []),
        out_specs=o_spec, out_shape=jax.ShapeDtypeStruct((m, n), out_dtype),
        compiler_params=_params(("parallel", "parallel")),
    )(*((a, b, res) if has_res else (a, b)))


def _row_spec(tm, bc, off, step):
    return pl.BlockSpec((tm, bc), lambda i, h: (i, off + step * h))


ROW_TILE_ELEMS = 512 * 1024


def _row_tile(t, rows):
    widest = max(bc for (_, bc, _, _) in rows)
    return _pick(t, (min(t, ROW_TILE_ELEMS // widest), 512, 256, 128, 64, 8))


def _rowwise(fn, rows, pars, outs, name, heads=1):
    t = rows[0][0].shape[0]
    tm = _row_tile(t, rows)
    nr, npar = len(rows), len(pars)

    def body(*refs):
        vals = [r[...].astype(F32) for r in refs[:nr + npar]]
        res = fn(*vals)
        if not isinstance(res, (tuple, list)):
            res = (res,)
        for o_ref, v in zip(refs[nr + npar:], res):
            o_ref[...] = v.astype(o_ref.dtype)

    in_specs = [_row_spec(tm, bc, off, st) for (_, bc, off, st) in rows]
    in_specs += [pl.BlockSpec(p.shape, lambda i, h: (0, 0)) for p in pars]
    out_specs = [_row_spec(tm, bc, 0, st) for (_, bc, st, _) in outs]
    out_shape = [jax.ShapeDtypeStruct((t, c), dt) for (c, _, _, dt) in outs]
    res = pl.pallas_call(
        body, name=name, grid=(t // tm, heads), in_specs=in_specs, out_specs=out_specs, out_shape=out_shape,
        compiler_params=_params(("parallel", "parallel")),
    )(*[r[0] for r in rows], *pars)
    return res[0] if len(res) == 1 else res


def _rowwise_vjp(fn, rows, pars, cts, name, heads=1, adds=None, row_dtypes=None):
    t = rows[0][0].shape[0]
    tm = _row_tile(t, rows)
    nr, npar, nct = len(rows), len(pars), len(cts)
    adds = adds or [None] * nr
    add_list = [a for a in adds if a is not None]
    row_dtypes = row_dtypes or [F32] * nr

    def body(*refs):
        i, h = pl.program_id(0), pl.program_id(1)
        p = 0
        row_v = [r[...].astype(F32) for r in refs[p:p + nr]]; p += nr
        par_v = [r[...].astype(F32) for r in refs[p:p + npar]]; p += npar
        ct_v = [r[...].astype(F32) for r in refs[p:p + nct]]; p += nct
        add_refs = refs[p:p + len(add_list)]; p += len(add_list)
        drow_refs = refs[p:p + nr]; p += nr
        dpar_refs = refs[p:p + npar]

        def wrapped(*a):
            r = fn(*a)
            return tuple(r) if isinstance(r, (tuple, list)) else (r,)

        _, pull = jax.vjp(wrapped, *row_v, *par_v)
        grads = pull(tuple(ct_v))
        ai = 0
        for k in range(nr):
            g = grads[k]
            if adds[k] is not None:
                g = g + add_refs[ai][...].astype(F32)
                ai += 1
            drow_refs[k][...] = g.astype(drow_refs[k].dtype)

        @pl.when((i == 0) & (h == 0))
        def _():
            for r in dpar_refs:
                r[...] = jnp.zeros(r.shape, r.dtype)

        for k in range(npar):
            dpar_refs[k][...] += grads[nr + k]

    in_specs = [_row_spec(tm, bc, off, st) for (_, bc, off, st) in rows]
    in_specs += [pl.BlockSpec(q.shape, lambda i, h: (0, 0)) for q in pars]
    in_specs += [_row_spec(tm, bc, off, st) for (_, bc, off, st) in cts]
    in_specs += [_row_spec(tm, bc, off, st) for (_, bc, off, st) in add_list]
    out_specs = [_row_spec(tm, bc, 0, st) for (_, bc, _, st) in rows]
    out_specs += [pl.BlockSpec(q.shape, lambda i, h: (0, 0)) for q in pars]
    out_shape = [jax.ShapeDtypeStruct((t, bc * (heads if st else 1)), dt) for (_, bc, _, st), dt in zip(rows, row_dtypes)]
    out_shape += [jax.ShapeDtypeStruct(q.shape, F32) for q in pars]
    res = pl.pallas_call(
        body, name=name, grid=(t // tm, heads), in_specs=in_specs, out_specs=out_specs, out_shape=out_shape,
        compiler_params=_params(("arbitrary", "arbitrary")),
    )(*[r[0] for r in rows], *pars, *[c[0] for c in cts], *[a[0] for a in add_list])
    return list(res[:nr]), list(res[nr:])


def _rms(x, g):
    return x * lax.rsqrt(jnp.mean(x * x, axis=-1, keepdims=True) + EPS) * g


def _rms_pair(x, g):
    left = lax.broadcasted_iota(jnp.int32, x.shape, 1) < HEAD_DIM
    x2 = x * x
    ms_a = jnp.sum(jnp.where(left, x2, 0.0), axis=-1, keepdims=True) * (1.0 / HEAD_DIM)
    ms_b = jnp.sum(jnp.where(left, 0.0, x2), axis=-1, keepdims=True) * (1.0 / HEAD_DIM)
    return x * lax.rsqrt(jnp.where(left, ms_a, ms_b) + EPS) * g


def _gelu(x):
    return 0.5 * x * (1.0 + jnp.tanh(math.sqrt(2.0 / math.pi) * (x + 0.044715 * (x * x * x))))


def _s5_act(ys, u, d):
    return _gelu(ys + d * u)


def _s5_gate(yg, z, b, g):
    return _rms(yg * jax.nn.sigmoid(z + b), g)


def _lane_cumsum(x, reverse):
    n = x.shape[-1]
    lane = lax.broadcasted_iota(jnp.int32, x.shape, 1)
    k = 1
    while k < n:
        if reverse:
            x = x + jnp.where(lane < n - k, pltpu.roll(x, n - k, 1), 0.0)
        else:
            x = x + jnp.where(lane >= k, pltpu.roll(x, k, 1), 0.0)
        k *= 2
    return x


def _log_sigmoid(z):
    return jnp.minimum(z, 0.0) - jnp.log(1.0 + jnp.exp(-jnp.abs(z)))


def _forget_fwd(f, bias):
    def body(f_ref, b_ref, c_ref):
        c_ref[...] = _lane_cumsum(_log_sigmoid(f_ref[...] + b_ref[...]), False)

    return pl.pallas_call(body, name="forget_fwd", out_shape=jax.ShapeDtypeStruct(f.shape, F32),
                          compiler_params=_params())(f, bias)


def _forget_bwd(f, bias, dc):
    def body(f_ref, b_ref, dc_ref, df_ref, db_ref):
        dlog = _lane_cumsum(dc_ref[...], True)
        df = dlog * jax.nn.sigmoid(-(f_ref[...] + b_ref[...]))
        df_ref[...] = df
        db_ref[...] = jnp.sum(df, axis=1, keepdims=True)

    return pl.pallas_call(body, name="forget_bwd",
                          out_shape=(jax.ShapeDtypeStruct(f.shape, F32), jax.ShapeDtypeStruct(bias.shape, F32)),
                          compiler_params=_params())(f, bias, dc)


FOX_BLOCK = 256
FOX_KEYS = 256
FOX_BWD_BLOCK = 512
_NT = _DIMS["nt"]
_TN = _DIMS["tn"]


N_PAIRS = N_FOX_HEADS // 2
V_BLOCK0 = 2 * N_PAIRS


def _left_lanes(shape):
    return lax.broadcasted_iota(jnp.int32, shape, 1) < HEAD_DIM


def _top_rows(shape):
    return lax.broadcasted_iota(jnp.int32, shape, 0) < HEAD_DIM


def _wide(c_tile, n):
    return c_tile if n == 128 else jnp.concatenate([c_tile] * (n // 128), axis=1)


def _fox_fwd(qn, kn, qkv, c_wide, seqs):
    t = qn.shape[0]
    l = t // seqs
    tb = min(FOX_BLOCK, l)
    tk = min(FOX_KEYS, tb)
    ratio = tb // tk
    nb = l // tb
    scale = HEAD_DIM ** -0.5

    def body(q_ref, k_ref, v_ref, ca_ref, cb_ref, o_ref, lse_ref, vt_ref):
        i = pl.program_id(2)
        top = _top_rows((128, tb))

        @pl.when(i == 0)
        def _():
            vt_ref[...] = v_ref[...].T.astype(BF16)

        qt = (q_ref[...].astype(F32) * scale).T.astype(BF16)
        zero = jnp.zeros_like(qt)
        qts = (jnp.where(top, qt, zero), jnp.where(top, zero, qt))
        top_k = _top_rows((128, tk))
        zero_k = jnp.zeros((128, tk), BF16)
        key_pos = lax.broadcasted_iota(jnp.int32, (tk, tb), 0)
        query_pos = lax.broadcasted_iota(jnp.int32, (tk, tb), 1)
        c_refs = (ca_ref, cb_ref)

        def scores(j):
            off = pl.multiple_of(j * tk, tk)
            k2 = k_ref[pl.ds(off, tk), :]
            return tuple(jnp.dot(k2, qts[h], preferred_element_type=F32) - _wide(c_refs[h][pl.ds(off, tk), :], tb)
                         for h in (0, 1))

        def values_times(ps, j):
            vt = vt_ref[:, pl.ds(pl.multiple_of(j * tk, tk), tk)]
            return (jnp.dot(jnp.where(top_k, vt, zero_k), ps[0], preferred_element_type=F32)
                    + jnp.dot(jnp.where(top_k, zero_k, vt), ps[1], preferred_element_type=F32))

        def softmax_step(sts, stats, first_key):
            ps, new, alphas = [], [], []
            for st, (m, s_sum) in zip(sts, stats):
                if first_key is not None:
                    st = jnp.where(key_pos + first_key <= query_pos, st, -jnp.inf)
                m_new = jnp.maximum(m, jnp.max(st, axis=0, keepdims=True))
                alpha = jnp.exp(m - m_new)
                p = jnp.exp(st - m_new)
                new.append((m_new, alpha * s_sum + jnp.sum(p, axis=0, keepdims=True)))
                alphas.append(alpha)
                ps.append(p.astype(BF16))
            return tuple(ps), tuple(new), jnp.where(top, alphas[0], alphas[1])

        def step(j, carry):
            sts, ps_prev, stats, acc = carry
            sts_next = scores(j + 1)
            acc = acc + values_times(ps_prev, jnp.maximum(j - 1, 0))
            ps, stats, alpha = softmax_step(sts, stats, None)
            return sts_next, ps, stats, alpha * acc

        stat = (jnp.full((1, tb), -jnp.inf, F32), jnp.zeros((1, tb), F32))
        no_p = jnp.zeros((tk, tb), BF16)
        below = i * ratio
        sts, ps_prev, stats, acc = lax.fori_loop(
            0, below, step, (scores(0), (no_p, no_p), (stat, stat), jnp.zeros((128, tb), F32)))
        for r in range(ratio):
            sts_next = scores(below + r + 1) if r + 1 < ratio else None
            acc = acc + values_times(ps_prev, jnp.maximum(below + r - 1, 0))
            ps_prev, stats, alpha = softmax_step(sts, stats, r * tk)
            acc = alpha * acc
            sts = sts_next
        acc = acc + values_times(ps_prev, below + ratio - 1)
        (ma, sa), (mb, sb) = stats
        o_ref[...] = (acc / jnp.where(top, sa, sb)).T
        lse_ref[0:1, :] = ma + jnp.log(sa)
        lse_ref[1:2, :] = mb + jnp.log(sb)

    qblk = pl.BlockSpec((tb, 128), lambda b, hp, i: (b * nb + i, hp))
    return pl.pallas_call(
        body, name="fox_fwd", grid=(seqs, N_PAIRS, nb),
        in_specs=[qblk, pl.BlockSpec((l, 128), lambda b, hp, i: (b, hp)),
                  pl.BlockSpec((l, 128), lambda b, hp, i: (b, V_BLOCK0 + hp)),
                  pl.BlockSpec((None, l, 128), lambda b, hp, i: (b * N_FOX_HEADS + 2 * hp, 0, 0)),
                  pl.BlockSpec((None, l, 128), lambda b, hp, i: (b * N_FOX_HEADS + 2 * hp + 1, 0, 0))],
        out_specs=[qblk, pl.BlockSpec((None, 2, tb), lambda b, hp, i: (b * N_PAIRS + hp, 0, i))],
        out_shape=[jax.ShapeDtypeStruct((t, FOX_WIDTH), F32), jax.ShapeDtypeStruct((seqs * N_PAIRS, 2, l), F32)],
        scratch_shapes=[pltpu.VMEM((128, l), BF16)],
        compiler_params=_params(("parallel", "parallel", "arbitrary")),
    )(qn, kn, qkv, c_wide, c_wide)


def _fox_bwd(qn, kn, qkv, c_wide, o, do, lse, seqs):
    t = qn.shape[0]
    l = t // seqs
    tb = min(FOX_BWD_BLOCK, l)
    nb = l // tb
    scale = HEAD_DIM ** -0.5
    one_at = (HEAD_DIM, 0)

    def body(q_ref, k_ref, v_ref, ca_ref, cb_ref, o_ref, do_ref, lse_ref, dq_ref, dk_ref, dv_ref, dc_ref, dcq_ref,
             qt_ref, kt_ref, dot_ref, delta_ref, dqa_ref, dqb_ref):
        top_l = _top_rows((128, l))
        top = _top_rows((128, tb))
        left = _left_lanes((tb, 128))
        row_id = lax.broadcasted_iota(jnp.int32, (128, tb), 0)
        lane_id = lax.broadcasted_iota(jnp.int32, (tb, 128), 1)
        zero_t = jnp.zeros((128, tb), BF16)
        zero_l = jnp.zeros((tb, 128), BF16)
        rows = lambda a: (jnp.where(top, a, zero_t), jnp.where(top, zero_t, a))
        lanes = lambda a: (jnp.where(left, a, zero_l), jnp.where(left, zero_l, a))
        with_one_row = lambda pair: tuple(jnp.where(row_id == one_at[h], 1.0, pair[h]).astype(BF16) for h in (0, 1))
        with_one_lane = lambda pair: tuple(jnp.where(lane_id == one_at[h], 1.0, pair[h]).astype(BF16) for h in (0, 1))
        causal = lax.broadcasted_iota(jnp.int32, (tb, tb), 0) <= lax.broadcasted_iota(jnp.int32, (tb, tb), 1)
        c_refs = (ca_ref, cb_ref)
        dq_refs = (dqa_ref, dqb_ref)

        qt_ref[...] = (q_ref[...].astype(F32) * scale).T.astype(BF16)
        kt_ref[...] = k_ref[...].astype(F32).T.astype(BF16)
        do_t = do_ref[...].T
        dot_ref[...] = do_t.astype(BF16)
        prod_t = do_t * o_ref[...].T
        delta_ref[0:1, :] = jnp.sum(jnp.where(top_l, prod_t, 0.0), axis=0, keepdims=True)
        delta_ref[1:2, :] = jnp.sum(jnp.where(top_l, 0.0, prod_t), axis=0, keepdims=True)
        dqa_ref[...] = jnp.zeros(dqa_ref.shape, F32)
        dqb_ref[...] = jnp.zeros(dqb_ref.shape, F32)

        def kv_block(j, _):
            koff = pl.multiple_of(j * tb, tb)
            k2 = k_ref[pl.ds(koff, tb), :]
            v2 = v_ref[pl.ds(koff, tb), :].astype(BF16)
            kts = with_one_row(rows(kt_ref[:, pl.ds(koff, tb)]))
            cw = tuple(_wide(c_refs[h][pl.ds(koff, tb), :], tb) for h in (0, 1))

            def q_block(i, carry, masked):
                dks, dv = list(carry[:2]), carry[2]
                qoff = pl.multiple_of(i * tb, tb)
                qs = lanes((q_ref[pl.ds(qoff, tb), :].astype(F32) * scale).astype(BF16))
                qs_one = with_one_lane(qs)
                dos = lanes(do_ref[pl.ds(qoff, tb), :].astype(BF16))
                qts = rows(qt_ref[:, pl.ds(qoff, tb)])
                dots = rows(dot_ref[:, pl.ds(qoff, tb)])
                for h in (0, 1):
                    st = jnp.dot(k2, qts[h], preferred_element_type=F32) - cw[h]
                    p = jnp.exp(st - lse_ref[h:h + 1, pl.ds(qoff, tb)])
                    if masked:
                        p = jnp.where(causal, p, 0.0)
                    dp = jnp.dot(v2, dots[h], preferred_element_type=F32)
                    dsb = (p * (dp - delta_ref[h:h + 1, pl.ds(qoff, tb)])).astype(BF16)
                    dv = dv + jnp.dot(p.astype(BF16), dos[h], preferred_element_type=F32)
                    dks[h] = dks[h] + jnp.dot(dsb, qs_one[h], preferred_element_type=F32)
                    dq_refs[h][:, pl.ds(qoff, tb)] += jnp.dot(kts[h], dsb, preferred_element_type=F32)
                return dks[0], dks[1], dv

            z = jnp.zeros((tb, 128), F32)
            carry = q_block(j, (z, z, z), True)
            rest = nb - 1 - j
            carry = lax.fori_loop(
                0, rest // 2, lambda n, c: q_block(j + 2 + 2 * n, q_block(j + 1 + 2 * n, c, False), False), carry)
            dka, dkb, dv = lax.cond(rest % 2 == 1, lambda c: q_block(nb - 1, c, False), lambda c: c, carry)
            dk_ref[pl.ds(koff, tb), :] = jnp.where(left, dka, dkb)
            dv_ref[pl.ds(koff, tb), :] = dv
            dc_ref[0:1, pl.ds(koff, tb)] = -dka.T[one_at[0]:one_at[0] + 1, :]
            dc_ref[1:2, pl.ds(koff, tb)] = -dkb.T[one_at[1]:one_at[1] + 1, :]
            return 0

        lax.fori_loop(0, nb, kv_block, 0)
        dq_ref[...] = (jnp.where(top_l, dqa_ref[...], dqb_ref[...]) * scale).T
        dcq_ref[0:1, :] = dqa_ref[one_at[0]:one_at[0] + 1, :]
        dcq_ref[1:2, :] = dqb_ref[one_at[1]:one_at[1] + 1, :]

    blk = pl.BlockSpec((l, 128), lambda b, hp: (b, hp))
    cspec = lambda k: pl.BlockSpec((None, l, 128), lambda b, hp: (b * N_FOX_HEADS + 2 * hp + k, 0, 0))
    rows2 = pl.BlockSpec((None, 2, l), lambda b, hp: (b * N_PAIRS + hp, 0, 0))
    wide = jax.ShapeDtypeStruct((t, FOX_WIDTH), F32)
    pair_rows = jax.ShapeDtypeStruct((seqs * N_PAIRS, 2, l), F32)
    return pl.pallas_call(
        body, name="fox_bwd", grid=(seqs, N_PAIRS),
        in_specs=[blk, blk, pl.BlockSpec((l, 128), lambda b, hp: (b, V_BLOCK0 + hp)), cspec(0), cspec(1), blk, blk, rows2],
        out_specs=[blk, blk, blk, rows2, rows2],
        out_shape=[wide, wide, wide, pair_rows, pair_rows],
        scratch_shapes=[pltpu.VMEM((128, l), BF16), pltpu.VMEM((128, l), BF16), pltpu.VMEM((128, l), BF16),
                        pltpu.VMEM((2, l), F32), pltpu.VMEM((128, l), F32), pltpu.VMEM((128, l), F32)],
        compiler_params=_params(("parallel", "parallel")),
    )(qn, kn, qkv, c_wide, c_wide, o, do, lse)


SCAN_ROWS = 256
SCAN_COLS = 1024


S5_IN = 128
S5_ST = 512
SCAN_CHUNKS = SCAN_COLS // S5_ST
SCAN_SEGS = 8
LANES = 128


def _cmul(ar, ai, br, bi):
    return ar * br - ai * bi, ar * bi + ai * br


def _powers_into(pw_r, pw_i, a_r, a_i, seg):
    pw_r[0:1, :] = a_r
    pw_i[0:1, :] = a_i
    for k in range(1, seg):
        pr, pi = _cmul(pw_r[k - 1:k, :], pw_i[k - 1:k, :], a_r, a_i)
        pw_r[k:k + 1, :] = pr
        pw_i[k:k + 1, :] = pi


def _interleave(dst, src, seg):
    for h in range(src.shape[0]):
        for j in range(seg):
            dst[h, j * SCAN_SEGS:(j + 1) * SCAN_SEGS, :] = src[h, pl.ds(j, SCAN_SEGS, stride=seg), :]


def _deinterleave(dst, src, seg):
    for h in range(src.shape[0]):
        for j in range(seg):
            dst[h, pl.ds(j, SCAN_SEGS, stride=seg), :] = src[h, j * SCAN_SEGS:(j + 1) * SCAN_SEGS, :]


def _interleaved(ref, tmp_a, tmp_b, seg):
    n = ref.shape[1] // LANES
    for h in range(n):
        tmp_a[h] = ref[:, h * LANES:(h + 1) * LANES].astype(F32)
    _interleave(tmp_b, tmp_a, seg)
    return jnp.concatenate([tmp_b[h] for h in range(n)], axis=1)


def _store_deinterleaved(ref, val, tmp_a, tmp_b, seg):
    n = ref.shape[1] // LANES
    for h in range(n):
        tmp_a[h] = val[:, h * LANES:(h + 1) * LANES]
    _deinterleave(tmp_b, tmp_a, seg)
    for h in range(n):
        ref[:, h * LANES:(h + 1) * LANES] = tmp_b[h]


def _segment_scan(b_r, b_i, x_r, x_i, pw_r, pw_i, car_r, car_i, seg, sign, reverse, visit=None):
    nc = b_r.shape[0]
    sub = lax.broadcasted_iota(jnp.int32, (SCAN_SEGS, LANES), 0)
    lanes = lambda c: slice(c * LANES, (c + 1) * LANES)
    rows = lambda j: pl.ds(pl.multiple_of(((seg - 1 - j) if reverse else j) * SCAN_SEGS, SCAN_SEGS), SCAN_SEGS)
    a1 = [(pw_r[0:1, lanes(c)], sign * pw_i[0:1, lanes(c)]) for c in range(nc)]

    def local(j, xs):
        out = []
        for c in range(nc):
            xr, xi = xs[2 * c], xs[2 * c + 1]
            nr = a1[c][0] * xr - a1[c][1] * xi + b_r[c, rows(j), :]
            ni = a1[c][0] * xi + a1[c][1] * xr + b_i[c, rows(j), :]
            x_r[c, rows(j), :] = nr
            x_i[c, rows(j), :] = ni
            out += [nr, ni]
        return tuple(out)

    zero = jnp.zeros((SCAN_SEGS, LANES), F32)
    ends = lax.fori_loop(0, seg, local, (zero,) * (2 * nc))

    if reverse:
        first = sub == SCAN_SEGS - 1
        neighbour = lambda v: pltpu.roll(v, SCAN_SEGS - 1, 0)
        shift = lambda v, d: jnp.where(sub < SCAN_SEGS - d, pltpu.roll(v, SCAN_SEGS - d, 0), 0.0)
    else:
        first = sub == 0
        neighbour = lambda v: pltpu.roll(v, 1, 0)
        shift = lambda v, d: jnp.where(sub >= d, pltpu.roll(v, d, 0), 0.0)
    last = 0 if reverse else SCAN_SEGS - 1
    entries = []
    for c in range(nc):
        er, ei = ends[2 * c], ends[2 * c + 1]
        pr, pi = pw_r[seg - 1:seg, lanes(c)], sign * pw_i[seg - 1:seg, lanes(c)]
        yr = jnp.where(first, car_r[:, lanes(c)], neighbour(er))
        yi = jnp.where(first, car_i[:, lanes(c)], neighbour(ei))
        qr, qi = pr, pi
        for d in (1, 2, 4):
            mr, mi = _cmul(qr, qi, shift(yr, d), shift(yi, d))
            yr, yi = yr + mr, yi + mi
            qr, qi = _cmul(qr, qi, qr, qi)
        lr, li = _cmul(pr, pi, yr, yi)
        car_r[:, lanes(c)] = (er + lr)[last:last + 1, :]
        car_i[:, lanes(c)] = (ei + li)[last:last + 1, :]
        entries += [yr, yi]

    def correct(j, prev):
        out = []
        row_r, row_i = pw_r[pl.ds(j, 1), :], sign * pw_i[pl.ds(j, 1), :]
        for c in range(nc):
            mr, mi = _cmul(row_r[:, lanes(c)], row_i[:, lanes(c)], entries[2 * c], entries[2 * c + 1])
            nr = x_r[c, rows(j), :] + mr
            ni = x_i[c, rows(j), :] + mi
            x_r[c, rows(j), :] = nr
            x_i[c, rows(j), :] = ni
            if visit is not None:
                visit(c, rows(j), prev[2 * c], prev[2 * c + 1])
            out += [nr, ni]
        return tuple(out)

    lax.fori_loop(0, seg, correct, tuple(entries))


def _s5_fwd(uf, bbr, bbi, cr, ci, ar, ai, seqs):
    t = uf.shape[0]
    l = t // seqs
    tl = min(SCAN_ROWS, l)
    nl = l // tl
    seg = tl // SCAN_SEGS
    cb, nq = SCAN_COLS, SCAN_CHUNKS
    nc = cb // LANES
    per = S5_ST // LANES

    def body(u_ref, bbr_ref, bbi_ref, cr_ref, ci_ref, ar_ref, ai_ref, xr_ref, xi_ref, ys_ref,
             car_r, car_i, pw_r, pw_i, b_r, b_i, x_r, x_i, tmp_a, tmp_b):
        @pl.when(pl.program_id(2) == 0)
        def _():
            car_r[...] = jnp.zeros(car_r.shape, F32)
            car_i[...] = jnp.zeros(car_i.shape, F32)
            _powers_into(pw_r, pw_i, ar_ref[...], ai_ref[...], seg)

        u = _interleaved(u_ref, tmp_a, tmp_b, seg).astype(BF16)
        for q in range(nq):
            uq = u[:, q * S5_IN:(q + 1) * S5_IN]
            br = jnp.dot(uq, bbr_ref[q], preferred_element_type=F32)
            bi = jnp.dot(uq, bbi_ref[q], preferred_element_type=F32)
            for s in range(per):
                b_r[q * per + s] = br[:, s * LANES:(s + 1) * LANES]
                b_i[q * per + s] = bi[:, s * LANES:(s + 1) * LANES]
        _segment_scan(b_r, b_i, x_r, x_i, pw_r, pw_i, car_r, car_i, seg, 1.0, False)
        for c in range(nc):
            xr_ref[:, c * LANES:(c + 1) * LANES] = x_r[c]
            xi_ref[:, c * LANES:(c + 1) * LANES] = x_i[c]
        ys = []
        for q in range(nq):
            xq_r = xr_ref[:, q * S5_ST:(q + 1) * S5_ST].astype(BF16)
            xq_i = xi_ref[:, q * S5_ST:(q + 1) * S5_ST].astype(BF16)
            ys.append(jnp.dot(xq_r, cr_ref[q], preferred_element_type=F32)
                      + jnp.dot(xq_i, ci_ref[q], preferred_element_type=F32))
        _store_deinterleaved(ys_ref, jnp.concatenate(ys, axis=1), tmp_a, tmp_b, seg)

    rows = lambda w: pl.BlockSpec((tl, w), lambda s, j, r: (s * nl + r, j))
    chunk = lambda a: pl.BlockSpec((nq,) + a.shape[1:], lambda s, j, r: (j, 0, 0))
    par = pl.BlockSpec((1, cb), lambda s, j, r: (0, j))
    return pl.pallas_call(
        body, name="s5_fwd", grid=(seqs, S5_CH // cb, nl),
        in_specs=[rows(nq * S5_IN), chunk(bbr), chunk(bbi), chunk(cr), chunk(ci), par, par],
        out_specs=[rows(cb), rows(cb), rows(nq * S5_IN)],
        out_shape=[jax.ShapeDtypeStruct((t, S5_CH), F32)] * 2 + [jax.ShapeDtypeStruct((t, S5_WIDTH), F32)],
        scratch_shapes=[pltpu.VMEM((1, cb), F32), pltpu.VMEM((1, cb), F32), pltpu.VMEM((seg, cb), F32),
                        pltpu.VMEM((seg, cb), F32)] + [pltpu.VMEM((nc, tl, LANES), F32)] * 4
        + [pltpu.VMEM((nq * S5_IN // LANES, tl, LANES), F32)] * 2,
        compiler_params=_params(("parallel", "parallel", "arbitrary")),
    )(uf, bbr, bbi, cr, ci, ar, ai)


def _s5_bwd(dys, uf, xr, xi, bbr, bbi, cr, ci, ar, ai, seqs):
    t = dys.shape[0]
    l = t // seqs
    tl = min(SCAN_ROWS, l)
    nl = l // tl
    seg = tl // SCAN_SEGS
    cb, nq = SCAN_COLS, SCAN_CHUNKS
    nc = cb // LANES
    per = S5_ST // LANES

    def body(dy_ref, u_ref, xr_ref, xi_ref, bbr_ref, bbi_ref, cr_ref, ci_ref, ar_ref, ai_ref,
             du_ref, dbbr_ref, dbbi_ref, dcr_ref, dci_ref, dar_ref, dai_ref,
             car_r, car_i, pw_r, pw_i, g_r, g_i, lam_r, lam_i, x_r, x_i, acc_r, acc_i, tmp_a, tmp_b):
        @pl.when(pl.program_id(2) == 0)
        def _():
            car_r[...] = jnp.zeros(car_r.shape, F32)
            car_i[...] = jnp.zeros(car_i.shape, F32)
            _powers_into(pw_r, pw_i, ar_ref[...], ai_ref[...], seg)
            for acc_ref in (dbbr_ref, dbbi_ref, dcr_ref, dci_ref, dar_ref, dai_ref):
                acc_ref[...] = jnp.zeros(acc_ref.shape, F32)

        dy = _interleaved(dy_ref, tmp_a, tmp_b, seg).astype(BF16)
        for q in range(nq):
            dyq = dy[:, q * S5_IN:(q + 1) * S5_IN]
            gr = lax.dot_general(dyq, cr_ref[q], _NT, preferred_element_type=F32)
            gi = lax.dot_general(dyq, ci_ref[q], _NT, preferred_element_type=F32)
            for s in range(per):
                g_r[q * per + s] = gr[:, s * LANES:(s + 1) * LANES]
                g_i[q * per + s] = gi[:, s * LANES:(s + 1) * LANES]
        for c in range(nc):
            x_r[c] = xr_ref[:, c * LANES:(c + 1) * LANES]
            x_i[c] = xi_ref[:, c * LANES:(c + 1) * LANES]
        acc_r[...] = jnp.zeros(acc_r.shape, F32)
        acc_i[...] = jnp.zeros(acc_i.shape, F32)

        def visit(c, rws, lr, li):
            xr_t, xi_t = x_r[c, rws, :], x_i[c, rws, :]
            acc_r[c] += lr * xr_t + li * xi_t
            acc_i[c] += li * xr_t - lr * xi_t

        _segment_scan(g_r, g_i, lam_r, lam_i, pw_r, pw_i, car_r, car_i, seg, -1.0, True, visit)
        for c in range(nc):
            dar_ref[:, c * LANES:(c + 1) * LANES] += jnp.sum(acc_r[c], axis=0, keepdims=True)
            dai_ref[:, c * LANES:(c + 1) * LANES] += jnp.sum(acc_i[c], axis=0, keepdims=True)
        u = _interleaved(u_ref, tmp_a, tmp_b, seg).astype(BF16)
        du = []
        for q in range(nq):
            st = slice(q * S5_ST, (q + 1) * S5_ST)
            io = slice(q * S5_IN, (q + 1) * S5_IN)
            lq_r = jnp.concatenate([lam_r[q * per + s] for s in range(per)], axis=1).astype(BF16)
            lq_i = jnp.concatenate([lam_i[q * per + s] for s in range(per)], axis=1).astype(BF16)
            du.append(lax.dot_general(lq_r, bbr_ref[q], _NT, preferred_element_type=F32)
                      + lax.dot_general(lq_i, bbi_ref[q], _NT, preferred_element_type=F32))
            dbbr_ref[q] += lax.dot_general(u[:, io], lq_r, _TN, preferred_element_type=F32)
            dbbi_ref[q] += lax.dot_general(u[:, io], lq_i, _TN, preferred_element_type=F32)
            dcr_ref[q] += lax.dot_general(xr_ref[:, st].astype(BF16), dy[:, io], _TN, preferred_element_type=F32)
            dci_ref[q] += lax.dot_general(xi_ref[:, st].astype(BF16), dy[:, io], _TN, preferred_element_type=F32)
        _store_deinterleaved(du_ref, jnp.concatenate(du, axis=1), tmp_a, tmp_b, seg)

    rows = lambda w: pl.BlockSpec((tl, w), lambda s, j, r: (s * nl + nl - 1 - r, j))
    chunk = lambda a: pl.BlockSpec((nq,) + a.shape[1:], lambda s, j, r: (j, 0, 0))
    acc = lambda a: pl.BlockSpec((None, nq) + a.shape[1:], lambda s, j, r: (s, j, 0, 0))
    par = pl.BlockSpec((1, cb), lambda s, j, r: (0, j))
    par_acc = pl.BlockSpec((None, 1, cb), lambda s, j, r: (s, 0, j))
    per_seq = lambda a: jax.ShapeDtypeStruct((seqs,) + a.shape, F32)
    return pl.pallas_call(
        body, name="s5_bwd", grid=(seqs, S5_CH // cb, nl),
        in_specs=[rows(nq * S5_IN), rows(nq * S5_IN), rows(cb), rows(cb), chunk(bbr), chunk(bbi), chunk(cr), chunk(ci),
                  par, par],
        out_specs=[rows(nq * S5_IN), acc(bbr), acc(bbi), acc(cr), acc(ci), par_acc, par_acc],
        out_shape=[jax.ShapeDtypeStruct((t, S5_WIDTH), F32), per_seq(bbr), per_seq(bbi), per_seq(cr), per_seq(ci),
                   jax.ShapeDtypeStruct((seqs, 1, S5_CH), F32), jax.ShapeDtypeStruct((seqs, 1, S5_CH), F32)],
        scratch_shapes=[pltpu.VMEM((1, cb), F32), pltpu.VMEM((1, cb), F32), pltpu.VMEM((seg, cb), F32),
                        pltpu.VMEM((seg, cb), F32)] + [pltpu.VMEM((nc, tl, LANES), F32)] * 6
        + [pltpu.VMEM((nc, SCAN_SEGS, LANES), F32)] * 2 + [pltpu.VMEM((nq * S5_IN // LANES, tl, LANES), F32)] * 2,
        compiler_params=_params(("parallel", "parallel", "arbitrary")),
    )(dys, uf, xr, xi, bbr, bbi, cr, ci, ar, ai)


XATT_BLOCK = 2048


def _xatt_probs(qv, kv):
    s = lax.dot_general(qv, kv, _NT, preferred_element_type=F32) * (X_HEAD_DIM ** -0.5)
    e = jnp.exp(s - jnp.max(s, axis=-1, keepdims=True))
    return e / jnp.sum(e, axis=-1, keepdims=True)


def _xatt_fwd(q, k, kv, seqs):
    t = q.shape[0]
    tq = min(XATT_BLOCK, t // seqs)
    nq = t // seqs // tq

    def body(q_ref, k_ref, v_ref, o_ref):
        p = _xatt_probs(q_ref[...], k_ref[...])
        o_ref[...] = jnp.dot(p.astype(BF16), v_ref[...].astype(BF16), preferred_element_type=F32).astype(o_ref.dtype)

    qs = pl.BlockSpec((tq, X_HEAD_DIM), lambda b, h, i: (b * nq + i, h))
    return pl.pallas_call(
        body, name="xatt_fwd", grid=(seqs, N_X_HEADS, nq),
        in_specs=[qs, pl.BlockSpec((N_MEM, X_HEAD_DIM), lambda b, h, i: (b, h)),
                  pl.BlockSpec((N_MEM, X_HEAD_DIM), lambda b, h, i: (b, N_X_HEADS + h))],
        out_specs=qs, out_shape=jax.ShapeDtypeStruct(q.shape, BF16),
        compiler_params=_params(("parallel", "parallel", "parallel")),
    )(q, k, kv)


def _xatt_bwd(q, k, kv, do, seqs):
    t = q.shape[0]
    tq = min(XATT_BLOCK, t // seqs)
    nq = t // seqs // tq
    scale = X_HEAD_DIM ** -0.5

    def body(q_ref, k_ref, v_ref, do_ref, dq_ref, dk_ref, dv_ref):
        @pl.when(pl.program_id(2) == 0)
        def _():
            dk_ref[...] = jnp.zeros(dk_ref.shape, F32)
            dv_ref[...] = jnp.zeros(dv_ref.shape, F32)

        qv, kk = q_ref[...], k_ref[...]
        p = _xatt_probs(qv, kk)
        dob = do_ref[...].astype(BF16)
        dp = lax.dot_general(dob, v_ref[...].astype(BF16), _NT, preferred_element_type=F32)
        ds = p * (dp - jnp.sum(dp * p, axis=-1, keepdims=True))
        dsb = ds.astype(BF16)
        dq_ref[...] = jnp.dot(dsb, kk, preferred_element_type=F32) * scale
        dk_ref[...] += lax.dot_general(dsb, qv, _TN, preferred_element_type=F32) * scale
        dv_ref[...] += lax.dot_general(p.astype(BF16), dob, _TN, preferred_element_type=F32)

    qs = pl.BlockSpec((tq, X_HEAD_DIM), lambda b, h, i: (b * nq + i, h))
    ks = pl.BlockSpec((N_MEM, X_HEAD_DIM), lambda b, h, i: (b, h))
    return pl.pallas_call(
        body, name="xatt_bwd", grid=(seqs, N_X_HEADS, nq),
        in_specs=[qs, ks, pl.BlockSpec((N_MEM, X_HEAD_DIM), lambda b, h, i: (b, N_X_HEADS + h)), qs],
        out_specs=[qs, ks, ks],
        out_shape=[jax.ShapeDtypeStruct(q.shape, F32), jax.ShapeDtypeStruct(k.shape, F32),
                   jax.ShapeDtypeStruct(k.shape, F32)],
        compiler_params=_params(("parallel", "parallel", "arbitrary")),
    )(q, k, kv, do)


CONV_COLS = 256


def _shift_down(x, k, row):
    return jnp.where(row >= k, pltpu.roll(x, k, 0), 0.0)


def _shift_up(x, k, row):
    n = x.shape[0]
    return jnp.where(row < n - k, pltpu.roll(x, n - k, 0), 0.0)


def _conv_pre(g, w, b, row):
    return b + w[0:1, :] * _shift_down(g, 2, row) + w[1:2, :] * _shift_down(g, 1, row) + w[2:3, :] * g


def _convgate_fwd(gu, w, b, seqs):
    t = gu.shape[0]
    l = t // seqs
    nc = D_FF // CONV_COLS

    def body(g_ref, u_ref, w_ref, b_ref, o_ref):
        g = g_ref[...].astype(F32)
        row = lax.broadcasted_iota(jnp.int32, g.shape, 0)
        pre = _conv_pre(g, w_ref[...], b_ref[...], row)
        o_ref[...] = (pre * jax.nn.sigmoid(pre) * u_ref[...].astype(F32)).astype(o_ref.dtype)

    return pl.pallas_call(
        body, name="convgate_fwd", grid=(seqs, nc),
        in_specs=[pl.BlockSpec((l, CONV_COLS), lambda s, j: (s, j)), pl.BlockSpec((l, CONV_COLS), lambda s, j: (s, nc + j)),
                  pl.BlockSpec((3, CONV_COLS), lambda s, j: (0, j)), pl.BlockSpec((1, CONV_COLS), lambda s, j: (0, j))],
        out_specs=pl.BlockSpec((l, CONV_COLS), lambda s, j: (s, j)),
        out_shape=jax.ShapeDtypeStruct((t, D_FF), BF16),
        compiler_params=_params(("parallel", "parallel")),
    )(gu, gu, w, b)


def _convgate_bwd(gu, w, b, dact, seqs):
    t = gu.shape[0]
    l = t // seqs
    nc = D_FF // CONV_COLS
    steps = nc * seqs

    def body(g_ref, u_ref, w_ref, b_ref, da_ref, dgu_ref, dw_ref, db_ref, stage, sems):
        j, s = pl.program_id(0), pl.program_id(1)
        n = j * seqs + s
        slot = n % 2

        def copies(slot_, j_, s_):
            rows = pl.ds(pl.multiple_of(s_ * l, 16), l)
            return [pltpu.make_async_copy(
                stage.at[slot_, half],
                dgu_ref.at[rows, pl.ds(pl.multiple_of((half * nc + j_) * CONV_COLS, 128), CONV_COLS)],
                sems.at[slot_, half]) for half in (0, 1)]

        @pl.when(s == 0)
        def _():
            dw_ref[...] = jnp.zeros(dw_ref.shape, F32)
            db_ref[...] = jnp.zeros(db_ref.shape, F32)

        @pl.when(n >= 2)
        def _():
            for cp in copies(slot, j, s):
                cp.wait()

        g, wv, da = g_ref[...].astype(F32), w_ref[...], da_ref[...].astype(F32)
        row = lax.broadcasted_iota(jnp.int32, g.shape, 0)
        g1, g2 = _shift_down(g, 1, row), _shift_down(g, 2, row)
        pre = b_ref[...] + wv[0:1, :] * g2 + wv[1:2, :] * g1 + wv[2:3, :] * g
        sg = jax.nn.sigmoid(pre)
        silu = pre * sg
        stage[slot, 1] = (da * silu).astype(stage.dtype)
        dpre = da * u_ref[...].astype(F32) * (sg * (1.0 + pre * (1.0 - sg)))
        dg = wv[2:3, :] * dpre + wv[1:2, :] * _shift_up(dpre, 1, row) + wv[0:1, :] * _shift_up(dpre, 2, row)
        stage[slot, 0] = dg.astype(stage.dtype)
        for cp in copies(slot, j, s):
            cp.start()
        dw_ref[0:1, :] += jnp.sum(dpre * g2, axis=0, keepdims=True)
        dw_ref[1:2, :] += jnp.sum(dpre * g1, axis=0, keepdims=True)
        dw_ref[2:3, :] += jnp.sum(dpre * g, axis=0, keepdims=True)
        db_ref[...] += jnp.sum(dpre, axis=0, keepdims=True)

        @pl.when(n == steps - 1)
        def _():
            for cp in copies(slot, j, s) + (copies(1 - slot, j, s) if steps > 1 else []):
                cp.wait()

    blk = lambda off: pl.BlockSpec((l, CONV_COLS), lambda j, s: (s, off + j))
    return pl.pallas_call(
        body, name="convgate_bwd", grid=(nc, seqs),
        in_specs=[blk(0), blk(nc), pl.BlockSpec((3, CONV_COLS), lambda j, s: (0, j)),
                  pl.BlockSpec((1, CONV_COLS), lambda j, s: (0, j)), blk(0)],
        out_specs=[ANY, pl.BlockSpec((3, CONV_COLS), lambda j, s: (0, j)),
                   pl.BlockSpec((1, CONV_COLS), lambda j, s: (0, j))],
        out_shape=[jax.ShapeDtypeStruct((t, 2 * D_FF), BF16), jax.ShapeDtypeStruct((3, D_FF), F32),
                   jax.ShapeDtypeStruct((1, D_FF), F32)],
        scratch_shapes=[pltpu.VMEM((2, 2, l, CONV_COLS), BF16), pltpu.SemaphoreType.DMA((2, 2))],
        compiler_params=_params(("arbitrary", "arbitrary")),
    )(gu, gu, w, b, dact)


def _loss_head(h, target):
    t, d = h.shape
    tm = _pick(t, (256, 128, 8))

    def body(h_ref, t_ref, dh_ref, dhb_ref, loss_ref):
        @pl.when(pl.program_id(0) == 0)
        def _():
            loss_ref[...] = jnp.zeros(loss_ref.shape, F32)

        e = h_ref[...] - t_ref[...]
        dh = e * (1.0 / d)
        dh_ref[...] = dh
        dhb_ref[...] = dh.astype(BF16)
        loss_ref[...] += (0.5 / d) * jnp.sum(jnp.sum(e * e, axis=1, keepdims=True), axis=0, keepdims=True)

    blk = pl.BlockSpec((tm, d), lambda i: (i, 0))
    return pl.pallas_call(
        body, name="loss_head", grid=(t // tm,), in_specs=[blk, blk],
        out_specs=[blk, blk, pl.BlockSpec((1, 1), lambda i: (0, 0))],
        out_shape=[jax.ShapeDtypeStruct((t, d), F32), jax.ShapeDtypeStruct((t, d), BF16),
                   jax.ShapeDtypeStruct((1, 1), F32)],
        compiler_params=_params(("arbitrary",)),
    )(h, target)


def _s5_discretise(a_re, a_im, log_dt, b_re, b_im):
    dt = jnp.exp(log_dt)[:, None]
    mag = jnp.exp(a_re * dt)
    lb_r = mag * jnp.cos(a_im * dt)
    lb_i = mag * jnp.sin(a_im * dt)
    den = a_re * a_re + a_im * a_im
    nr = lb_r - 1.0
    coef_r = (nr * a_re + lb_i * a_im) / den
    coef_i = (lb_i * a_re - nr * a_im) / den
    bb_r = coef_r[:, :, None] * b_re - coef_i[:, :, None] * b_im
    bb_i = coef_r[:, :, None] * b_im + coef_i[:, :, None] * b_re
    return lb_r, lb_i, bb_r, bb_i


S5_CHUNKS = 4
S5_PER = S5_GROUPS // S5_CHUNKS


def _blockdiag_in(bb):
    eye = jnp.eye(S5_PER, dtype=bb.dtype)
    return jnp.einsum("jgpc,gh->jgchp", bb.reshape(S5_CHUNKS, S5_PER, S5_STATE, S5_GROUP_CH), eye).reshape(
        S5_CHUNKS, S5_PER * S5_GROUP_CH, S5_PER * S5_STATE)


def _blockdiag_in_grad(d):
    eye = jnp.eye(S5_PER, dtype=d.dtype)
    return jnp.einsum("jgchp,gh->jgpc", d.reshape(S5_CHUNKS, S5_PER, S5_GROUP_CH, S5_PER, S5_STATE), eye).reshape(
        S5_GROUPS, S5_STATE, S5_GROUP_CH)


def _blockdiag_out(c):
    eye = jnp.eye(S5_PER, dtype=c.dtype)
    return jnp.einsum("jgcp,gh->jgphc", c.reshape(S5_CHUNKS, S5_PER, S5_GROUP_CH, S5_STATE), eye).reshape(
        S5_CHUNKS, S5_PER * S5_STATE, S5_PER * S5_GROUP_CH)


def _blockdiag_out_grad(d):
    eye = jnp.eye(S5_PER, dtype=d.dtype)
    return jnp.einsum("jgphc,gh->jgcp", d.reshape(S5_CHUNKS, S5_PER, S5_STATE, S5_PER, S5_GROUP_CH), eye).reshape(
        S5_GROUPS, S5_GROUP_CH, S5_STATE)


def _local_step(x3, mem3, target3, p, wb, late_weights=None, early_grads=None):
    seqs, l, d = x3.shape
    t = seqs * l
    x = x3.reshape(t, d)
    mem = mem3.reshape(seqs * N_MEM, d)
    target = target3.reshape(t, d)
    full = lambda a: (a, a.shape[1], 0, 0)

    s5_in = (p["s5_a_re"], p["s5_a_im"], p["s5_log_dt"], p["s5_b_re"], p["s5_b_im"])
    (lb_r, lb_i, bb_r, bb_i), s5_pull = jax.vjp(_s5_discretise, *s5_in)
    ar, ai = lb_r.reshape(1, S5_CH), lb_i.reshape(1, S5_CH)
    bbr_d, bbi_d = _blockdiag_in(bb_r).astype(BF16), _blockdiag_in(bb_i).astype(BF16)
    cr_d, ci_d = _blockdiag_out(p["s5_c_re"]).astype(BF16), (-_blockdiag_out(p["s5_c_im"])).astype(BF16)
    d_row = p["s5_d"].reshape(1, S5_WIDTH)

    w_in = wb["w_in"]
    w_qkv = w_in[:, :3 * FOX_WIDTH]
    w_uf = jnp.concatenate(
        [w_in[:, 3 * FOX_WIDTH + N_FOX_HEADS:], w_in[:, 3 * FOX_WIDTH:3 * FOX_WIDTH + N_FOX_HEADS],
         jnp.zeros((d, UF_COLS - S5_WIDTH - N_FOX_HEADS), w_in.dtype)], axis=1)

    hn1 = _rowwise(_rms, [full(x)], [p["norm_mix"]], [(d, d, 0, BF16)], "norm_mix_fwd")
    qkv = _mm(hn1, w_qkv, "nn", "in_qkv")
    uf = _mm(hn1, w_uf, "nn", "in_uf")

    bh = seqs * N_FOX_HEADS
    q_pair = (qkv, 128, 0, 1)
    k_pair = (qkv, 128, N_PAIRS, 1)
    gq2, gk2 = jnp.tile(p["fox_q_norm"], (1, 2)), jnp.tile(p["fox_k_norm"], (1, 2))
    pair_out = [(FOX_WIDTH, 128, 1, BF16)]
    qn = _rowwise(_rms_pair, [q_pair], [gq2], pair_out, "fox_qnorm_fwd", heads=N_PAIRS)
    kn = _rowwise(_rms_pair, [k_pair], [gk2], pair_out, "fox_knorm_fwd", heads=N_PAIRS)

    f_rows = uf[:, S5_WIDTH:S5_WIDTH + N_FOX_HEADS].reshape(seqs, l, N_FOX_HEADS).transpose(0, 2, 1).reshape(bh, l)
    f_bias = jnp.tile(p["fox_f_bias"].reshape(N_FOX_HEADS, 1), (seqs, 1))
    c_wide = jnp.broadcast_to(_forget_fwd(f_rows, f_bias)[:, :, None], (bh, l, 128))
    fox, lse = _fox_fwd(qn, kn, qkv, c_wide, seqs)

    xr, xi, ys = _s5_fwd(uf, bbr_d, bbi_d, cr_d, ci_d, ar, ai, seqs)
    u_blk = (uf, S5_WIDTH, 0, 0)
    yg = _rowwise(_s5_act, [full(ys), u_blk], [d_row], [(S5_WIDTH, S5_WIDTH, 0, F32)], "s5_act_fwd")
    if late_weights is not None:
        wb = dict(wb, **late_weights("mid", yg))
    z = _mm(yg, wb["s5_w_glu"], "nn", "s5_glu")
    y2n = _rowwise(_s5_gate, [full(yg), full(z)], [p["s5_b_glu"], p["out_norm_s5"]],
                   [(S5_WIDTH, S5_WIDTH, 0, BF16)], "s5_gate_fwd")
    foxn = _rowwise(_rms, [full(fox)], [p["out_norm_fox"]], [(FOX_WIDTH, FOX_WIDTH, 0, BF16)], "fox_outnorm_fwd")
    mixed = jnp.concatenate([foxn, y2n], axis=1)
    h1 = _mm(mixed, wb["w_out"], "nn", "mix_out", res=x)
    if late_weights is not None:
        wb = dict(wb, **late_weights("late", h1))

    hn2 = _rowwise(_rms, [full(h1)], [p["norm_cross"]], [(d, d, 0, BF16)], "norm_cross_fwd")
    mn = _rowwise(_rms, [full(mem)], [p["norm_mem"]], [(d, d, 0, BF16)], "norm_mem_fwd")
    xq_raw = _mm(hn2, wb["w_xq"], "nn", "x_q")
    kv = _mm(mn, wb["w_xkv"], "nn", "x_kv")
    xh = lambda a: (a, X_HEAD_DIM, 0, 1)
    xqn = _rowwise(_rms, [xh(xq_raw)], [p["xq_norm"]], [(d, X_HEAD_DIM, 1, BF16)], "x_qnorm_fwd", heads=N_X_HEADS)
    xkn = _rowwise(_rms, [xh(kv)], [p["xk_norm"]], [(d, X_HEAD_DIM, 1, BF16)], "x_knorm_fwd", heads=N_X_HEADS)
    xo = _xatt_fwd(xqn, xkn, kv, seqs)
    h2 = _mm(xo, wb["w_xo"], "nn", "x_out", res=h1)

    hn3 = _rowwise(_rms, [full(h2)], [p["norm_ffn"]], [(d, d, 0, BF16)], "norm_ffn_fwd")
    gu = _mm(hn3, wb["w_ffn_up"], "nn", "ffn_up", out_dtype=BF16)
    act = _convgate_fwd(gu, p["ffn_conv_w"], p["ffn_conv_b"], seqs)
    h3 = _mm(act, wb["w_ffn_down"], "nn", "ffn_down", res=h2)
    dh3, dh3_b, loss = _loss_head(h3, target)

    g = {}
    dact = _mm(dh3_b, wb["w_ffn_down"], "nt", "ffn_down_dx", out_dtype=BF16)
    late_dt = BF16 if early_grads is not None else F32
    g["w_ffn_down"] = _mm(act, dh3_b, "tn", "ffn_down_dw", out_dtype=late_dt)
    dgu, g["ffn_conv_w"], g["ffn_conv_b"] = _convgate_bwd(gu, p["ffn_conv_w"], p["ffn_conv_b"], dact, seqs)
    dhn3 = _mm(dgu, wb["w_ffn_up"], "nt", "ffn_up_dx")
    g["w_ffn_up"] = _mm(hn3, dgu, "tn", "ffn_up_dw", out_dtype=late_dt)
    (dh2,), (g["norm_ffn"],) = _rowwise_vjp(_rms, [full(h2)], [p["norm_ffn"]], [full(dhn3)], "norm_ffn_bwd",
                                            adds=[full(dh3)])

    dxo = _mm(dh2, wb["w_xo"], "nt", "x_out_dx")
    g["w_xo"] = _mm(xo, dh2, "tn", "x_out_dw", out_dtype=late_dt)
    dxqn, dxkn, dxv = _xatt_bwd(xqn, xkn, kv, dxo, seqs)
    (dxq_raw,), (g["xq_norm"],) = _rowwise_vjp(_rms, [xh(xq_raw)], [p["xq_norm"]], [xh(dxqn)], "x_qnorm_bwd",
                                               heads=N_X_HEADS, row_dtypes=[BF16])
    (dxk_raw,), (g["xk_norm"],) = _rowwise_vjp(_rms, [xh(kv)], [p["xk_norm"]], [xh(dxkn)], "x_knorm_bwd",
                                               heads=N_X_HEADS, row_dtypes=[BF16])
    dkv = jnp.concatenate([dxk_raw, dxv.astype(BF16)], axis=1)
    dhn2 = _mm(dxq_raw, wb["w_xq"], "nt", "x_q_dx")
    g["w_xq"] = _mm(hn2, dxq_raw, "tn", "x_q_dw", out_dtype=late_dt)
    dmn = _mm(dkv, wb["w_xkv"], "nt", "x_kv_dx")
    g["w_xkv"] = _mm(mn, dkv, "tn", "x_kv_dw", out_dtype=late_dt)
    norm_cross = p["norm_cross"]
    if early_grads is not None:
        norm_cross = norm_cross + early_grads({n: g[n] for n in LATE_WEIGHTS})
    (dh1,), (g["norm_cross"],) = _rowwise_vjp(_rms, [full(h1)], [norm_cross], [full(dhn2)], "norm_cross_bwd",
                                              adds=[full(dh2)])
    _, (g["norm_mem"],) = _rowwise_vjp(_rms, [full(mem)], [p["norm_mem"]], [full(dmn)], "norm_mem_bwd",
                                       row_dtypes=[BF16])

    dmixed = _mm(dh1, wb["w_out"], "nt", "mix_out_dx")
    g["w_out"] = _mm(mixed, dh1, "tn", "mix_out_dw", out_dtype=late_dt)
    (dfox,), (g["out_norm_fox"],) = _rowwise_vjp(_rms, [full(fox)], [p["out_norm_fox"]],
                                                 [(dmixed, FOX_WIDTH, 0, 0)], "fox_outnorm_bwd")
    (dyg_a, dz), (g["s5_b_glu"], g["out_norm_s5"]) = _rowwise_vjp(
        _s5_gate, [full(yg), full(z)], [p["s5_b_glu"], p["out_norm_s5"]], [(dmixed, S5_WIDTH, 1, 0)], "s5_gate_bwd",
        row_dtypes=[F32, BF16])
    dyg = _mm(dz, wb["s5_w_glu"], "nt", "s5_glu_dx", res=dyg_a)
    g["s5_w_glu"] = _mm(yg, dz, "tn", "s5_glu_dw", out_dtype=late_dt)
    (dys, du_a), (dd_row,) = _rowwise_vjp(_s5_act, [full(ys), u_blk], [d_row], [full(dyg)], "s5_act_bwd",
                                          row_dtypes=[BF16, F32])
    g["s5_d"] = dd_row
    du_b, dbbr_d, dbbi_d, dcr_d, dci_d, dar, dai = _s5_bwd(dys, uf, xr, xi, bbr_d, bbi_d, cr_d, ci_d, ar, ai, seqs)
    dbbr_d, dbbi_d, dcr_d, dci_d = (jnp.sum(a, axis=0) for a in (dbbr_d, dbbi_d, dcr_d, dci_d))
    d_lb_r = jnp.sum(dar, axis=0).reshape(S5_GROUPS, S5_STATE)
    d_lb_i = jnp.sum(dai, axis=0).reshape(S5_GROUPS, S5_STATE)
    g["s5_a_re"], g["s5_a_im"], g["s5_log_dt"], g["s5_b_re"], g["s5_b_im"] = s5_pull(
        (d_lb_r, d_lb_i, _blockdiag_in_grad(dbbr_d), _blockdiag_in_grad(dbbi_d)))
    g["s5_c_re"] = _blockdiag_out_grad(dcr_d)
    g["s5_c_im"] = -_blockdiag_out_grad(dci_d)

    dqn, dkn, dv, dc, dcq = _fox_bwd(qn, kn, qkv, c_wide, fox, dfox, lse, seqs)
    pair = lambda a: (a, 128, 0, 1)
    (dq_raw,), (dgq2,) = _rowwise_vjp(_rms_pair, [q_pair], [gq2], [pair(dqn)], "fox_qnorm_bwd", heads=N_PAIRS,
                                      row_dtypes=[BF16])
    (dk_raw,), (dgk2,) = _rowwise_vjp(_rms_pair, [k_pair], [gk2], [pair(dkn)], "fox_knorm_bwd", heads=N_PAIRS,
                                      row_dtypes=[BF16])
    g["fox_q_norm"] = dgq2[:, :HEAD_DIM] + dgq2[:, HEAD_DIM:]
    g["fox_k_norm"] = dgk2[:, :HEAD_DIM] + dgk2[:, HEAD_DIM:]
    df_rows, dfb = _forget_bwd(f_rows, f_bias, (dc + dcq).reshape(bh, l))
    g["fox_f_bias"] = jnp.sum(dfb.reshape(seqs, N_FOX_HEADS), axis=0)
    df = df_rows.reshape(seqs, N_FOX_HEADS, l).transpose(0, 2, 1).reshape(t, N_FOX_HEADS)
    dqkv = jnp.concatenate([dq_raw, dk_raw, dv.astype(BF16)], axis=1)
    duf = jnp.concatenate([du_a + du_b, df, jnp.zeros((t, UF_COLS - S5_WIDTH - N_FOX_HEADS), F32)],
                          axis=1).astype(BF16)
    dhn1 = _mm(duf, w_uf, "nt", "in_uf_dx", res=_mm(dqkv, w_qkv, "nt", "in_qkv_dx"))
    dw_qkv = _mm(hn1, dqkv, "tn", "in_qkv_dw")
    dw_uf = _mm(hn1, duf, "tn", "in_uf_dw")
    g["w_in"] = jnp.concatenate([dw_qkv, dw_uf[:, S5_WIDTH:S5_WIDTH + N_FOX_HEADS], dw_uf[:, :S5_WIDTH]], axis=1)
    (dx,), (g["norm_mix"],) = _rowwise_vjp(_rms, [full(x)], [p["norm_mix"]], [full(dhn1)], "norm_mix_bwd",
                                           adds=[full(dh1)])
    return loss, dx.reshape(seqs, l, d), g


def _place():
    return lax.axis_index("x"), lax.axis_index("y"), lax.axis_index("c")


def _other_chips(x, y):
    return [(1 - x, y), (x, 1 - y), (1 - x, 1 - y)]


ANY = pl.BlockSpec(memory_space=pl.ANY)


def _gather_weights(shards, col_kind, taps):
    n = len(shards)

    def body(*refs):
        ins, tap_in, outs, tap_out = refs[:n], refs[n], refs[n + 1:2 * n + 1], refs[2 * n + 1]
        ici_send, ici_recv, d2d_send, d2d_recv, own_send, own_recv = refs[2 * n + 2:]
        x, y, c = _place()
        mine = 2 * x + y
        chips = _other_chips(x, y)
        sibling = (x, y, 1 - c)

        def piece(a, s, h):
            r, cs = ins[a].shape
            hr = r // 2
            if col_kind[a]:
                return outs[a].at[pl.ds(pl.multiple_of(h * hr, 16), hr), pl.ds(pl.multiple_of(s * cs, 128), cs)]
            return outs[a].at[pl.ds(pl.multiple_of(s * r + h * hr, 16), hr), :]

        def slab(a, s):
            r, cs = ins[a].shape
            if col_kind[a]:
                return outs[a].at[:, pl.ds(pl.multiple_of(s * cs, 128), cs)]
            return outs[a].at[pl.ds(pl.multiple_of(s * r, 16), r), :]

        def own_half(a, h):
            hr = ins[a].shape[0] // 2
            return ins[a].at[pl.ds(pl.multiple_of(h * hr, 16), hr), :]

        sends = []
        for a in range(n):
            cp = pltpu.make_async_remote_copy(
                src_ref=ins[a], dst_ref=slab(a, mine), send_sem=own_send.at[a], recv_sem=own_recv.at[a],
                device_id=sibling, device_id_type=MESH)
            cp.start()
            sends.append(cp)
        cp = pltpu.make_async_remote_copy(
            src_ref=tap_in, dst_ref=tap_out.at[mine], send_sem=own_send.at[n], recv_sem=own_recv.at[n],
            device_id=sibling, device_id_type=MESH)
        cp.start()
        sends.append(cp)
        for a in range(n):
            for j, (px, py) in enumerate(chips):
                cp = pltpu.make_async_remote_copy(
                    src_ref=own_half(a, c), dst_ref=piece(a, mine, c), send_sem=ici_send.at[3 * a + j],
                    recv_sem=ici_recv.at[3 * a + j], device_id=(px, py, c), device_id_type=MESH)
                cp.start()
                sends.append(cp)
        for j, (px, py) in enumerate(chips):
            cp = pltpu.make_async_remote_copy(
                src_ref=tap_in, dst_ref=tap_out.at[mine], send_sem=ici_send.at[3 * n + j],
                recv_sem=ici_recv.at[3 * n + j], device_id=(px, py, c), device_id_type=MESH)
            cp.start()
            sends.append(cp)
        for a in range(n):
            for j, (px, py) in enumerate(chips):
                got = piece(a, 2 * px + py, c)
                pltpu.make_async_remote_copy(
                    src_ref=got, dst_ref=got, send_sem=ici_send.at[3 * a + j], recv_sem=ici_recv.at[3 * a + j],
                    device_id=(px, py, c), device_id_type=MESH).wait_recv()
                fwd = pltpu.make_async_remote_copy(
                    src_ref=got, dst_ref=got, send_sem=d2d_send.at[3 * a + j], recv_sem=d2d_recv.at[3 * a + j],
                    device_id=(x, y, 1 - c), device_id_type=MESH)
                fwd.start()
                sends.append(fwd)
        for a in range(n):
            for j, (px, py) in enumerate(chips):
                other = piece(a, 2 * px + py, 1 - c)
                pltpu.make_async_remote_copy(
                    src_ref=other, dst_ref=other, send_sem=d2d_send.at[3 * a + j], recv_sem=d2d_recv.at[3 * a + j],
                    device_id=(x, y, 1 - c), device_id_type=MESH).wait_recv()
        for j, (px, py) in enumerate(chips):
            pltpu.make_async_remote_copy(
                src_ref=tap_in, dst_ref=tap_out.at[2 * px + py], send_sem=ici_send.at[3 * n + j],
                recv_sem=ici_recv.at[3 * n + j], device_id=(px, py, c), device_id_type=MESH).wait_recv()
        for a in range(n):
            pltpu.make_async_remote_copy(
                src_ref=ins[a], dst_ref=slab(a, mine), send_sem=own_send.at[a], recv_sem=own_recv.at[a],
                device_id=sibling, device_id_type=MESH).wait_recv()
        pltpu.make_async_remote_copy(
            src_ref=tap_in, dst_ref=tap_out.at[mine], send_sem=own_send.at[n], recv_sem=own_recv.at[n],
            device_id=sibling, device_id_type=MESH).wait_recv()
        for cp in sends:
            cp.wait_send()

    def full_shape(a):
        r, cs = shards[a].shape
        return (r, 4 * cs) if col_kind[a] else (4 * r, cs)

    res = pl.pallas_call(
        body, name="gather_weights", in_specs=[ANY] * (n + 1), out_specs=[ANY] * (n + 1),
        out_shape=[jax.ShapeDtypeStruct(full_shape(a), shards[a].dtype) for a in range(n)]
        + [jax.ShapeDtypeStruct((4,) + taps.shape, taps.dtype)],
        scratch_shapes=[pltpu.SemaphoreType.DMA((3 * n + 3,)), pltpu.SemaphoreType.DMA((3 * n + 3,)),
                        pltpu.SemaphoreType.DMA((3 * n,)), pltpu.SemaphoreType.DMA((3 * n,)),
                        pltpu.SemaphoreType.DMA((n + 1,)), pltpu.SemaphoreType.DMA((n + 1,))],
        compiler_params=pltpu.CompilerParams(has_side_effects=True),
    )(*shards, taps)
    return res[:n], res[n]


HBM = pl.BlockSpec(memory_space=pltpu.HBM)
SEM = pl.BlockSpec(memory_space=pltpu.SEMAPHORE)
DATAFLOW = pltpu.SideEffectType.DATAFLOW_SIDE_EFFECTING


def _in_hbm(a):
    return pltpu.with_memory_space_constraint(a, pltpu.HBM)


def _split_start(name, srcs, lands, n_copies, plan):
    n = len(srcs)

    def body(*refs):
        src_refs, land_refs = refs[:n], refs[n:2 * n]
        send_sems, recv_sems = refs[2 * n], refs[2 * n + 1]
        for i, (src, dst, dev) in enumerate(plan(src_refs, land_refs)):
            pltpu.make_async_remote_copy(src_ref=src, dst_ref=dst, send_sem=send_sems.at[i], recv_sem=recv_sems.at[i],
                                         device_id=dev, device_id_type=MESH).start()
        refs[-1][...] = jnp.zeros((8, 128), F32)

    res = pl.pallas_call(
        body, name=name, in_specs=[HBM] * (2 * n),
        out_specs=[SEM, SEM] + [HBM] * (2 * n) + [pl.BlockSpec(memory_space=pltpu.VMEM)],
        out_shape=[pltpu.SemaphoreType.DMA((n_copies,)), pltpu.SemaphoreType.DMA((n_copies,))]
        + [pltpu.HBM(a.shape, a.dtype) for a in list(srcs) + list(lands)] + [jax.ShapeDtypeStruct((8, 128), F32)],
        input_output_aliases={i: 2 + i for i in range(2 * n)},
        compiler_params=pltpu.CompilerParams(has_side_effects=DATAFLOW),
    )(*[_in_hbm(a) for a in list(srcs) + list(lands)])
    return res[0], res[1], list(res[2:2 + n]), list(res[2 + n:2 + 2 * n]), res[-1]


def _split_wait(name, send_sems, recv_sems, srcs, lands, after, plan):
    n = len(srcs)

    def body(*refs):
        src_refs, land_refs = refs[:n], refs[n:2 * n]
        send_ref, recv_ref = refs[2 * n], refs[2 * n + 1]
        for i, (src, dst, dev) in enumerate(plan(src_refs, land_refs)):
            cp = pltpu.make_async_remote_copy(src_ref=src, dst_ref=dst, send_sem=send_ref.at[i], recv_sem=recv_ref.at[i],
                                              device_id=dev, device_id_type=MESH)
            cp.wait_send()
            cp.wait_recv()

    res = pl.pallas_call(
        body, name=name, in_specs=[HBM] * (2 * n) + [SEM, SEM, ANY], out_specs=[HBM] * (2 * n),
        out_shape=[pltpu.HBM(a.shape, a.dtype) for a in list(srcs) + list(lands)],
        input_output_aliases={i: i for i in range(2 * n)},
        compiler_params=pltpu.CompilerParams(has_side_effects=DATAFLOW),
    )(*srcs, *lands, send_sems, recv_sems, after)
    return list(res[:n]), list(res[n:])


def _late_gather_plan(col_kind):
    def plan(src_refs, land_refs):
        x, y, c = _place()
        mine = 2 * x + y
        copies = []
        for a, (src, land) in enumerate(zip(src_refs, land_refs)):
            r, cs = src.shape
            if col_kind[a]:
                dst = land.at[:, pl.ds(pl.multiple_of(mine * cs, 128), cs)]
            else:
                dst = land.at[pl.ds(pl.multiple_of(mine * r, 16), r), :]
            copies.append((src, dst, (x, y, 1 - c)))
            copies += [(src, dst, (px, py, c)) for (px, py) in _other_chips(x, y)]
        return copies
    return plan


def _late_reduce_plan(col_kind):
    def plan(src_refs, land_refs):
        x, y, c = _place()
        copies = []
        for a, (src, land) in enumerate(zip(src_refs, land_refs)):
            for j, (px, py) in enumerate(_other_chips(x, y)):
                if col_kind[a]:
                    cs = land.shape[2]
                    piece = src.at[:, pl.ds(pl.multiple_of((2 * px + py) * cs, 128), cs)]
                else:
                    piece = src.at[2 * px + py]
                copies.append((piece, land.at[j], (px, py, c)))
        return copies
    return plan


def _pair_swap(name, halves):
    n = len(halves)

    def body(*refs):
        ins, outs = refs[:n], refs[n:2 * n]
        send_sems, recv_sems = refs[2 * n:]
        x, y, c = _place()
        copies = []
        for a in range(n):
            cp = pltpu.make_async_remote_copy(
                src_ref=ins[a], dst_ref=outs[a], send_sem=send_sems.at[a], recv_sem=recv_sems.at[a],
                device_id=(x, y, 1 - c), device_id_type=MESH)
            cp.start()
            copies.append(cp)
        for cp in copies:
            cp.wait()

    return pl.pallas_call(
        body, name=name, in_specs=[ANY] * n, out_specs=[ANY] * n,
        out_shape=[jax.ShapeDtypeStruct(s.shape, s.dtype) for s in halves],
        scratch_shapes=[pltpu.SemaphoreType.DMA((n,)), pltpu.SemaphoreType.DMA((n,))],
        compiler_params=pltpu.CompilerParams(has_side_effects=True),
    )(*halves)


def _chip_sum(name, chip_sel, own, col, others):
    _, r, c = others.shape
    tr = _pick(r, (256, 128, 64, 32, 16))
    if col:
        own_spec = pl.BlockSpec((tr, c), lambda i, s: (i, s[0]))
    else:
        own_spec = pl.BlockSpec((None, tr, c), lambda i, s: (s[0], i, 0))
    specs = [own_spec] + [pl.BlockSpec((None, tr, c), lambda i, s, k=k: (k, i, 0)) for k in range(3)]

    def body(s_ref, own_ref, r0, r1, r2, o_ref):
        o_ref[...] = ((own_ref[...].astype(F32) + r0[...].astype(F32)) + r1[...].astype(F32)) + r2[...].astype(F32)

    return pl.pallas_call(
        body, name=name,
        grid_spec=pltpu.PrefetchScalarGridSpec(
            num_scalar_prefetch=1, grid=(r // tr,), in_specs=specs,
            out_specs=pl.BlockSpec((tr, c), lambda i, s: (i, 0))),
        out_shape=jax.ShapeDtypeStruct((r, c), F32),
        compiler_params=_params(("parallel",)),
    )(chip_sel, own, others, others, others)


def _allreduce_small(vals):
    sizes = [int(math.prod(v.shape)) for v in vals]
    padded = [-(-s // 128) * 128 for s in sizes]
    total = -(-sum(padded) // 1024) * 1024
    flat = [jnp.pad(v.reshape(-1), (0, p - s)) for v, s, p in zip(vals, sizes, padded)]
    flat.append(jnp.zeros((total - sum(padded),), F32))
    packed = jnp.concatenate(flat).reshape(total // 128, 128)

    def body(in_ref, out_ref, r0, r1, r2, send_sems, recv_sems):
        x, y, c = _place()
        out_ref[...] = in_ref[...]
        for k, (peer, land) in enumerate(zip([(x, y, 1 - c), (1 - x, y, c), (x, 1 - y, c)], (r0, r1, r2))):
            cp = pltpu.make_async_remote_copy(
                src_ref=out_ref, dst_ref=land, send_sem=send_sems.at[k], recv_sem=recv_sems.at[k],
                device_id=peer, device_id_type=MESH)
            cp.start()
            cp.wait()
            out_ref[...] = out_ref[...] + land[...]

    vm = pl.BlockSpec(memory_space=pltpu.VMEM)
    summed = pl.pallas_call(
        body, name="allreduce_small", in_specs=[vm], out_specs=vm,
        out_shape=jax.ShapeDtypeStruct(packed.shape, F32),
        scratch_shapes=[pltpu.VMEM(packed.shape, F32)] * 3
        + [pltpu.SemaphoreType.DMA((3,)), pltpu.SemaphoreType.DMA((3,))],
        compiler_params=pltpu.CompilerParams(has_side_effects=True, vmem_limit_bytes=VMEM_LIMIT_BYTES),
    )(packed).reshape(-1)
    outs, off = [], 0
    for v, s, p in zip(vals, sizes, padded):
        outs.append(summed[off:off + s].reshape(v.shape))
        off += p
    return outs


def _adamw_math(w, g, m, v):
    m2 = ADAM_B1 * m + (1.0 - ADAM_B1) * g
    v2 = ADAM_B2 * v + (1.0 - ADAM_B2) * (g * g)
    m_hat = m2 / (1.0 - ADAM_B1 ** ADAM_STEP)
    v_hat = v2 / (1.0 - ADAM_B2 ** ADAM_STEP)
    delta = -ADAM_LR * (m_hat / (jnp.sqrt(v_hat) + ADAM_EPS) + ADAM_WD * w)
    return delta, m2, v2


def _adamw_big(name, w, g_mine, g_sibling, m, v):
    _, r, c = w.shape

    def body(w_ref, ga_ref, gb_ref, m_ref, v_ref, go_ref, d_ref, mo_ref, vo_ref):
        gv = ga_ref[...] + gb_ref[...]
        d, m2, v2 = _adamw_math(w_ref[...], gv, m_ref[...], v_ref[...])
        go_ref[...] = gv
        d_ref[...] = d
        mo_ref[...] = m2
        vo_ref[...] = v2

    tr = _pick(r, (256, 128, 64, 32, 16, 8))
    if r % tr == 0 and tr % 8 == 0:
        grid = (r // tr,)
        blk = pl.BlockSpec((None, tr, c), lambda i: (0, i, 0))
        part = pl.BlockSpec((tr, c), lambda i: (i, 0))
    else:
        grid = (c // 512,)
        blk = pl.BlockSpec((None, r, 512), lambda i: (0, 0, i))
        part = pl.BlockSpec((r, 512), lambda i: (0, i))
    return pl.pallas_call(
        body, name=name, grid=grid, in_specs=[blk, part, part, blk, blk], out_specs=[blk] * 4,
        out_shape=[jax.ShapeDtypeStruct((1, r, c), F32)] * 4, compiler_params=_params(("parallel",)),
    )(w, g_mine, g_sibling, m, v)


def _adamw_small(ws, gs, ms, vs):
    n = len(ws)

    def body(*refs):
        w_r, g_r, m_r, v_r = refs[:n], refs[n:2 * n], refs[2 * n:3 * n], refs[3 * n:4 * n]
        o = refs[4 * n:]
        for a in range(n):
            gv = g_r[a][...]
            d, m2, v2 = _adamw_math(w_r[a][...], gv, m_r[a][...], v_r[a][...])
            o[a][...] = gv
            o[n + a][...] = d
            o[2 * n + a][...] = m2
            o[3 * n + a][...] = v2

    res = pl.pallas_call(
        body, name="adamw_small", out_shape=[jax.ShapeDtypeStruct(w.shape, F32) for _ in range(4) for w in ws],
        compiler_params=_params(),
    )(*ws, *gs, *ms, *vs)
    return res[:n], res[n:2 * n], res[2 * n:3 * n], res[3 * n:]


def _full_from_gathered(name, gathered):
    if name == "w_in":
        rows = gathered.shape[0] // 4
        return gathered.reshape(4, rows, gathered.shape[1]).transpose(1, 0, 2).reshape(rows, 4 * gathered.shape[1])
    return gathered


def _reduce_layout(name, full):
    if name in COL_KIND:
        return full
    if name == "w_in":
        rows, cols = full.shape
        return full.reshape(rows, 4, cols // 4).transpose(1, 0, 2)
    return full.reshape(4, full.shape[0] // 4, full.shape[1])


def kernel(x, mem, norm_mix, w_in, fox_q_norm, fox_k_norm, fox_f_bias, s5_a_re, s5_a_im, s5_log_dt, s5_b_re, s5_b_im, s5_c_re, s5_c_im, s5_d, s5_w_glu, s5_b_glu, out_norm_fox, out_norm_s5, w_out, norm_cross, norm_mem, w_xq, w_xkv, xq_norm, xk_norm, w_xo, norm_ffn, w_ffn_up, ffn_conv_w, ffn_conv_b, w_ffn_down, loss_target, m_norm_mix, m_w_in, m_fox_q_norm, m_fox_k_norm, m_fox_f_bias, m_s5_a_re, m_s5_a_im, m_s5_log_dt, m_s5_b_re, m_s5_b_im, m_s5_c_re, m_s5_c_im, m_s5_d, m_s5_w_glu, m_s5_b_glu, m_out_norm_fox, m_out_norm_s5, m_w_out, m_norm_cross, m_norm_mem, m_w_xq, m_w_xkv, m_xq_norm, m_xk_norm, m_w_xo, m_norm_ffn, m_w_ffn_up, m_ffn_conv_w, m_ffn_conv_b, m_w_ffn_down, v_norm_mix, v_w_in, v_fox_q_norm, v_fox_k_norm, v_fox_f_bias, v_s5_a_re, v_s5_a_im, v_s5_log_dt, v_s5_b_re, v_s5_b_im, v_s5_c_re, v_s5_c_im, v_s5_d, v_s5_w_glu, v_s5_b_glu, v_out_norm_fox, v_out_norm_s5, v_w_out, v_norm_cross, v_norm_mem, v_w_xq, v_w_xkv, v_xq_norm, v_xk_norm, v_w_xo, v_norm_ffn, v_w_ffn_up, v_ffn_conv_w, v_ffn_conv_b, v_w_ffn_down):
    given = dict(locals())
    w = {n: given[n] for n in WEIGHTS}
    m = {n: given["m_" + n] for n in WEIGHTS}
    v = {n: given["v_" + n] for n in WEIGHTS}
    xi, yi, _ = _place()
    chip = (2 * xi + yi).astype(jnp.int32)
    chip_sel = chip.reshape(1)
    early_kind = [n in COL_KIND for n in EARLY_WEIGHTS]
    late_kind = [n in COL_KIND for n in LATE_WEIGHTS]

    gathered, taps = _gather_weights([w[FIRST_WEIGHT][0].astype(BF16)], [False], w["ffn_conv_w"][0])
    first_full = gathered[0]
    conv_w = taps.transpose(1, 0, 2).reshape(3, D_FF)
    pending = {}
    g_started = None
    for stage, names in (("mid", MID_WEIGHTS), ("late", LATE_WEIGHTS)):
        kinds = [n in COL_KIND for n in names]
        shards = [w[n][0].astype(BF16) for n in names]
        if g_started is None:
            first_full, shards[0] = lax.optimization_barrier((first_full, shards[0]))
        else:
            shards[0] = shards[0] + g_started[0:1, 0:1].astype(BF16)
        full = [lax.empty((s.shape[0], 4 * s.shape[1]) if ck else (4 * s.shape[0], s.shape[1]), BF16)
                for s, ck in zip(shards, kinds)]
        plan = _late_gather_plan(kinds)
        send, recv, srcs, lands, g_started = _split_start(
            "gather_" + stage + "_start", shards, full, 4 * len(names), plan)
        pending[stage] = (names, plan, send, recv, srcs, lands)
    wb = {FIRST_WEIGHT: _full_from_gathered(FIRST_WEIGHT, first_full)}

    def late_weights(stage, after):
        names, plan, send, recv, srcs, lands = pending[stage]
        _, full = _split_wait("gather_" + stage + "_wait", send, recv, srcs, lands, after, plan)
        return dict(zip(names, full))

    reduce_plan = _late_reduce_plan(late_kind)
    late_reduce = {}

    def early_grads(late_g):
        grads = [_reduce_layout(n, late_g[n]) for n in LATE_WEIGHTS]
        lands = [lax.empty((3, s.shape[0], s.shape[1] // 4) if ck else (3,) + s.shape[1:], BF16)
                 for s, ck in zip(grads, late_kind)]
        late_reduce["sems"] = _split_start("reduce_late_start", grads, lands, 3 * len(LATE_WEIGHTS), reduce_plan)
        return late_reduce["sems"][4][0:1, 0:1]

    p = {n: w[n][0] for n in SMALL}
    p["ffn_conv_w"] = conv_w
    for n in ("norm_mix", "fox_q_norm", "fox_k_norm", "fox_f_bias", "s5_b_glu", "out_norm_fox", "out_norm_s5",
              "norm_cross", "norm_mem", "xq_norm", "xk_norm", "norm_ffn", "ffn_conv_b"):
        p[n] = p[n].reshape(1, -1)
    p["norm_mix"] = p["norm_mix"] + g_started[0:1, 0:1]
    loss, grad_x, g = _local_step(x, mem, loss_target, p, wb, late_weights, early_grads)

    grads = [_reduce_layout(n, g[n].astype(BF16)) for n in EARLY_WEIGHTS]
    early_lands = [lax.empty((3, s.shape[0], s.shape[1] // 4) if ck else (3,) + s.shape[1:], BF16)
                   for s, ck in zip(grads, early_kind)]
    early_plan = _late_reduce_plan(early_kind)
    e_send, e_recv, e_srcs, e_lands, e_started = _split_start(
        "reduce_early_start", grads, early_lands, 3 * len(EARLY_WEIGHTS), early_plan)

    out_g, out_d, out_m, out_v = {}, {}, {}, {}

    def finish(names, kinds, sums, from_chips, tag):
        mine = [_chip_sum("reduce_chip_sum_" + n, chip_sel, ps, ck, fc)
                for n, ps, fc, ck in zip(names, sums, from_chips, kinds)]
        theirs = _pair_swap("reduce_pair_swap_" + tag, mine)
        for n, a, b in zip(names, mine, theirs):
            if n == "w_in":
                flip = lambda t: jnp.swapaxes(t, -1, -2)
                res = _adamw_big("adamw_" + n, flip(w[n]), flip(a), flip(b), flip(m[n]), flip(v[n]))
                out_g[n], out_d[n], out_m[n], out_v[n] = (flip(t) for t in res)
                continue
            out_g[n], out_d[n], out_m[n], out_v[n] = _adamw_big("adamw_" + n, w[n], a, b, m[n], v[n])

    r_send, r_recv, r_srcs, r_lands, _ = late_reduce["sems"]
    late_sums, late_from_chips = _split_wait("reduce_late_wait", r_send, r_recv, r_srcs, r_lands, e_started,
                                             reduce_plan)
    finish(LATE_WEIGHTS, late_kind, late_sums, late_from_chips, "late")

    small_names = list(SMALL) + ["ffn_conv_w"]
    small_vals = [g[n].reshape(w[n].shape if n != "ffn_conv_w" else (1, 3, D_FF)) for n in small_names]
    last = LATE_WEIGHTS[-1]
    loss, out_v[last] = lax.optimization_barrier((loss, out_v[last]))
    reduced = _allreduce_small(small_vals + [loss])
    loss_all = reduced[-1].reshape(())
    conv_w_grad = lax.dynamic_slice_in_dim(reduced[-2], chip * (D_FF // 4), D_FF // 4, axis=2)
    sg, sd, sm, sv = _adamw_small(
        [w[n] for n in small_names], list(reduced[:len(SMALL)]) + [conv_w_grad],
        [m[n] for n in small_names], [v[n] for n in small_names])
    out_g.update(zip(small_names, sg))
    out_d.update(zip(small_names, sd))
    out_m.update(zip(small_names, sm))
    out_v.update(zip(small_names, sv))

    early_sums, early_from_chips = _split_wait("reduce_early_wait", e_send, e_recv, e_srcs, e_lands, reduced[0],
                                               early_plan)
    finish(EARLY_WEIGHTS, early_kind, early_sums, early_from_chips, "early")

    return (loss_all, grad_x, *[out_g[n] for n in WEIGHTS], *[out_d[n] for n in WEIGHTS],
            *[out_m[n] for n in WEIGHTS], *[out_v[n] for n in WEIGHTS])
```

```python
import math

import jax
import jax.numpy as jnp
from jax import lax
from jax.experimental import pallas as pl
from jax.experimental.pallas import tpu as pltpu

F32 = jnp.float32
BF16 = jnp.bfloat16

D_MODEL = 1024
FOX_WIDTH = 512
HEAD_DIM = 64
N_FOX_HEADS = 8
S5_WIDTH = 512
S5_GROUP_CH = 16
S5_GROUPS = 32
S5_STATE = 64
S5_CH = S5_GROUPS * S5_STATE
N_X_HEADS = 4
X_HEAD_DIM = 256
N_MEM = 256
D_FF = 2816
UF_COLS = 640
EPS = 1e-6
ADAM_LR = 0.001
ADAM_B1 = 0.9
ADAM_B2 = 0.999
ADAM_EPS = 1e-08
ADAM_WD = 0.01
ADAM_STEP = 10

VMEM_LIMIT_BYTES = 56 * 1024 * 1024
MM_BLOCK_BYTES = 6 * 1024 * 1024
MM_VMEM_BYTES = 40 * 1024 * 1024
MM_TILE_MAX = 1536
MESH = pl.DeviceIdType.MESH

FIRST_WEIGHT = "w_in"
MID_WEIGHTS = ("s5_w_glu", "w_out")
EARLY_WEIGHTS = (FIRST_WEIGHT,) + MID_WEIGHTS
LATE_WEIGHTS = ("w_xq", "w_xkv", "w_xo", "w_ffn_up", "w_ffn_down")
BIG = EARLY_WEIGHTS + LATE_WEIGHTS
COL_KIND = ("w_xkv", "w_ffn_up")
SMALL = ("norm_mix", "fox_q_norm", "fox_k_norm", "fox_f_bias", "s5_a_re", "s5_a_im", "s5_log_dt",
         "s5_b_re", "s5_b_im", "s5_c_re", "s5_c_im", "s5_d", "s5_b_glu", "out_norm_fox", "out_norm_s5",
         "norm_cross", "norm_mem", "xq_norm", "xk_norm", "norm_ffn", "ffn_conv_b")
WEIGHTS = ("norm_mix", "w_in", "fox_q_norm", "fox_k_norm", "fox_f_bias", "s5_a_re", "s5_a_im", "s5_log_dt",
           "s5_b_re", "s5_b_im", "s5_c_re", "s5_c_im", "s5_d", "s5_w_glu", "s5_b_glu", "out_norm_fox",
           "out_norm_s5", "w_out", "norm_cross", "norm_mem", "w_xq", "w_xkv", "xq_norm", "xk_norm", "w_xo",
           "norm_ffn", "w_ffn_up", "ffn_conv_w", "ffn_conv_b", "w_ffn_down")


def _params(sem=None):
    return pltpu.CompilerParams(dimension_semantics=sem, vmem_limit_bytes=VMEM_LIMIT_BYTES)


def _pick(n, cands):
    for c in cands:
        if n % c == 0:
            return c
    return n


_DIMS = {"nn": (((1,), (0,)), ((), ())), "nt": (((1,), (1,)), ((), ())), "tn": (((0,), (0,)), ((), ()))}


def _mm(a, b, mode, name, out_dtype=F32, res=None):
    if mode == "nn":
        (m, k), (k2, n) = a.shape, b.shape
    elif mode == "nt":
        (m, k), (n, k2) = a.shape, b.shape
    else:
        (k, m), (k2, n) = a.shape, b.shape
    assert k == k2, (name, a.shape, b.shape)

    has_res = res is not None
    a_size, b_size = a.dtype.itemsize, b.dtype.itemsize
    o_size = jnp.dtype(out_dtype).itemsize + (res.dtype.itemsize if has_res else 0)

    def tiles(dim):
        return [c for c in range(MM_TILE_MAX, 0, -128) if dim % c == 0] or [dim]

    best = None
    for tm in tiles(m):
        for tn in tiles(n):
            a_blk, b_blk = tm * k * a_size, tn * k * b_size
            if max(a_blk, b_blk) > MM_BLOCK_BYTES or 2 * (a_blk + b_blk + tm * tn * o_size) > MM_VMEM_BYTES:
                continue
            for rows_outer in (True, False):
                moved = (m * k * a_size + (m // tm) * n * k * b_size) if rows_outer else \
                        (n * k * b_size + (n // tn) * m * k * a_size)
                key = (moved, -(tm * tn))
                if best is None or key < best[0]:
                    best = (key, tm, tn, rows_outer)
    assert best is not None, (name, a.shape, b.shape)
    _, tm, tn, rows_outer = best
    ij = (lambda g0, g1: (g0, g1)) if rows_outer else (lambda g0, g1: (g1, g0))
    if mode == "tn":
        a_spec = pl.BlockSpec((k, tm), lambda g0, g1: (0, ij(g0, g1)[0]))
    else:
        a_spec = pl.BlockSpec((tm, k), lambda g0, g1: (ij(g0, g1)[0], 0))
    if mode == "nt":
        b_spec = pl.BlockSpec((tn, k), lambda g0, g1: (ij(g0, g1)[1], 0))
    else:
        b_spec = pl.BlockSpec((k, tn), lambda g0, g1: (0, ij(g0, g1)[1]))
    o_spec = pl.BlockSpec((tm, tn), lambda g0, g1: ij(g0, g1))
    grid = (m // tm, n // tn) if rows_outer else (n // tn, m // tm)
    dims = _DIMS[mode]

    def body(*refs):
        a_ref, b_ref = refs[0], refs[1]
        o_ref = refs[-1]
        acc = lax.dot_general(a_ref[...].astype(BF16), b_ref[...].astype(BF16), dims, preferred_element_type=F32)
        if has_res:
            acc = acc + refs[2][...].astype(F32)
        o_ref[...] = acc.astype(o_ref.dtype)

    return pl.pallas_call(
        body, name=name, grid=grid,
        in_specs=[a_spec, b_spec] + ([o_spec] if has_res else []),
        out_specs=o_spec, out_shape=jax.ShapeDtypeStruct((m, n), out_dtype),
        compiler_params=_params(("parallel", "parallel")),
    )(*((a, b, res) if has_res else (a, b)))


def _row_spec(tm, bc, off, step):
    return pl.BlockSpec((tm, bc), lambda i, h: (i, off + step * h))


ROW_TILE_ELEMS = 512 * 1024


def _row_tile(t, rows):
    widest = max(bc for (_, bc, _, _) in rows)
    return _pick(t, (min(t, ROW_TILE_ELEMS // widest), 512, 256, 128, 64, 8))


def _rowwise(fn, rows, pars, outs, name, heads=1):
    t = rows[0][0].shape[0]
    tm = _row_tile(t, rows)
    nr, npar = len(rows), len(pars)

    def body(*refs):
        vals = [r[...].astype(F32) for r in refs[:nr + npar]]
        res = fn(*vals)
        if not isinstance(res, (tuple, list)):
            res = (res,)
        for o_ref, v in zip(refs[nr + npar:], res):
            o_ref[...] = v.astype(o_ref.dtype)

    in_specs = [_row_spec(tm, bc, off, st) for (_, bc, off, st) in rows]
    in_specs += [pl.BlockSpec(p.shape, lambda i, h: (0, 0)) for p in pars]
    out_specs = [_row_spec(tm, bc, 0, st) for (_, bc, st, _) in outs]
    out_shape = [jax.ShapeDtypeStruct((t, c), dt) for (c, _, _, dt) in outs]
    res = pl.pallas_call(
        body, name=name, grid=(t // tm, heads), in_specs=in_specs, out_specs=out_specs, out_shape=out_shape,
        compiler_params=_params(("parallel", "parallel")),
    )(*[r[0] for r in rows], *pars)
    return res[0] if len(res) == 1 else res


def _rowwise_vjp(fn, rows, pars, cts, name, heads=1, adds=None, row_dtypes=None):
    t = rows[0][0].shape[0]
    tm = _row_tile(t, rows)
    nr, npar, nct = len(rows), len(pars), len(cts)
    adds = adds or [None] * nr
    add_list = [a for a in adds if a is not None]
    row_dtypes = row_dtypes or [F32] * nr

    def body(*refs):
        i, h = pl.program_id(0), pl.program_id(1)
        p = 0
        row_v = [r[...].astype(F32) for r in refs[p:p + nr]]; p += nr
        par_v = [r[...].astype(F32) for r in refs[p:p + npar]]; p += npar
        ct_v = [r[...].astype(F32) for r in refs[p:p + nct]]; p += nct
        add_refs = refs[p:p + len(add_list)]; p += len(add_list)
        drow_refs = refs[p:p + nr]; p += nr
        dpar_refs = refs[p:p + npar]

        def wrapped(*a):
            r = fn(*a)
            return tuple(r) if isinstance(r, (tuple, list)) else (r,)

        _, pull = jax.vjp(wrapped, *row_v, *par_v)
        grads = pull(tuple(ct_v))
        ai = 0
        for k in range(nr):
            g = grads[k]
            if adds[k] is not None:
                g = g + add_refs[ai][...].astype(F32)
                ai += 1
            drow_refs[k][...] = g.astype(drow_refs[k].dtype)

        @pl.when((i == 0) & (h == 0))
        def _():
            for r in dpar_refs:
                r[...] = jnp.zeros(r.shape, r.dtype)

        for k in range(npar):
            dpar_refs[k][...] += grads[nr + k]

    in_specs = [_row_spec(tm, bc, off, st) for (_, bc, off, st) in rows]
    in_specs += [pl.BlockSpec(q.shape, lambda i, h: (0, 0)) for q in pars]
    in_specs += [_row_spec(tm, bc, off, st) for (_, bc, off, st) in cts]
    in_specs += [_row_spec(tm, bc, off, st) for (_, bc, off, st) in add_list]
    out_specs = [_row_spec(tm, bc, 0, st) for (_, bc, _, st) in rows]
    out_specs += [pl.BlockSpec(q.shape, lambda i, h: (0, 0)) for q in pars]
    out_shape = [jax.ShapeDtypeStruct((t, bc * (heads if st else 1)), dt) for (_, bc, _, st), dt in zip(rows, row_dtypes)]
    out_shape += [jax.ShapeDtypeStruct(q.shape, F32) for q in pars]
    res = pl.pallas_call(
        body, name=name, grid=(t // tm, heads), in_specs=in_specs, out_specs=out_specs, out_shape=out_shape,
        compiler_params=_params(("arbitrary", "arbitrary")),
    )(*[r[0] for r in rows], *pars, *[c[0] for c in cts], *[a[0] for a in add_list])
    return list(res[:nr]), list(res[nr:])


def _rms(x, g):
    return x * lax.rsqrt(jnp.mean(x * x, axis=-1, keepdims=True) + EPS) * g


def _rms_pair(x, g):
    left = lax.broadcasted_iota(jnp.int32, x.shape, 1) < HEAD_DIM
    x2 = x * x
    ms_a = jnp.sum(jnp.where(left, x2, 0.0), axis=-1, keepdims=True) * (1.0 / HEAD_DIM)
    ms_b = jnp.sum(jnp.where(left, 0.0, x2), axis=-1, keepdims=True) * (1.0 / HEAD_DIM)
    return x * lax.rsqrt(jnp.where(left, ms_a, ms_b) + EPS) * g


def _gelu(x):
    return 0.5 * x * (1.0 + jnp.tanh(math.sqrt(2.0 / math.pi) * (x + 0.044715 * (x * x * x))))


def _s5_act(ys, u, d):
    return _gelu(ys + d * u)


def _s5_gate(yg, z, b, g):
    return _rms(yg * jax.nn.sigmoid(z + b), g)


def _lane_cumsum(x, reverse):
    n = x.shape[-1]
    lane = lax.broadcasted_iota(jnp.int32, x.shape, 1)
    k = 1
    while k < n:
        if reverse:
            x = x + jnp.where(lane < n - k, pltpu.roll(x, n - k, 1), 0.0)
        else:
            x = x + jnp.where(lane >= k, pltpu.roll(x, k, 1), 0.0)
        k *= 2
    return x


def _log_sigmoid(z):
    return jnp.minimum(z, 0.0) - jnp.log(1.0 + jnp.exp(-jnp.abs(z)))


def _forget_fwd(f, bias):
    def body(f_ref, b_ref, c_ref):
        c_ref[...] = _lane_cumsum(_log_sigmoid(f_ref[...] + b_ref[...]), False)

    return pl.pallas_call(body, name="forget_fwd", out_shape=jax.ShapeDtypeStruct(f.shape, F32),
                          compiler_params=_params())(f, bias)


def _forget_bwd(f, bias, dc):
    def body(f_ref, b_ref, dc_ref, df_ref, db_ref):
        dlog = _lane_cumsum(dc_ref[...], True)
        df = dlog * jax.nn.sigmoid(-(f_ref[...] + b_ref[...]))
        df_ref[...] = df
        db_ref[...] = jnp.sum(df, axis=1, keepdims=True)

    return pl.pallas_call(body, name="forget_bwd",
                          out_shape=(jax.ShapeDtypeStruct(f.shape, F32), jax.ShapeDtypeStruct(bias.shape, F32)),
                          compiler_params=_params())(f, bias, dc)


FOX_BLOCK = 256
FOX_KEYS = 256
FOX_BWD_BLOCK = 512
_NT = _DIMS["nt"]
_TN = _DIMS["tn"]


N_PAIRS = N_FOX_HEADS // 2
V_BLOCK0 = 2 * N_PAIRS


def _left_lanes(shape):
    return lax.broadcasted_iota(jnp.int32, shape, 1) < HEAD_DIM


def _top_rows(shape):
    return lax.broadcasted_iota(jnp.int32, shape, 0) < HEAD_DIM


def _wide(c_tile, n):
    return c_tile if n == 128 else jnp.concatenate([c_tile] * (n // 128), axis=1)


def _fox_fwd(qn, kn, qkv, c_wide, seqs):
    t = qn.shape[0]
    l = t // seqs
    tb = min(FOX_BLOCK, l)
    tk = min(FOX_KEYS, tb)
    ratio = tb // tk
    nb = l // tb
    scale = HEAD_DIM ** -0.5

    def body(q_ref, k_ref, v_ref, ca_ref, cb_ref, o_ref, lse_ref, vt_ref):
        i = pl.program_id(2)
        top = _top_rows((128, tb))

        @pl.when(i == 0)
        def _():
            vt_ref[...] = v_ref[...].T.astype(BF16)

        qt = (q_ref[...].astype(F32) * scale).T.astype(BF16)
        zero = jnp.zeros_like(qt)
        qts = (jnp.where(top, qt, zero), jnp.where(top, zero, qt))
        top_k = _top_rows((128, tk))
        zero_k = jnp.zeros((128, tk), BF16)
        key_pos = lax.broadcasted_iota(jnp.int32, (tk, tb), 0)
        query_pos = lax.broadcasted_iota(jnp.int32, (tk, tb), 1)
        c_refs = (ca_ref, cb_ref)

        def scores(j):
            off = pl.multiple_of(j * tk, tk)
            k2 = k_ref[pl.ds(off, tk), :]
            return tuple(jnp.dot(k2, qts[h], preferred_element_type=F32) - _wide(c_refs[h][pl.ds(off, tk), :], tb)
                         for h in (0, 1))

        def values_times(ps, j):
            vt = vt_ref[:, pl.ds(pl.multiple_of(j * tk, tk), tk)]
            return (jnp.dot(jnp.where(top_k, vt, zero_k), ps[0], preferred_element_type=F32)
                    + jnp.dot(jnp.where(top_k, zero_k, vt), ps[1], preferred_element_type=F32))

        def softmax_step(sts, stats, first_key):
            ps, new, alphas = [], [], []
            for st, (m, s_sum) in zip(sts, stats):
                if first_key is not None:
                    st = jnp.where(key_pos + first_key <= query_pos, st, -jnp.inf)
                m_new = jnp.maximum(m, jnp.max(st, axis=0, keepdims=True))
                alpha = jnp.exp(m - m_new)
                p = jnp.exp(st - m_new)
                new.append((m_new, alpha * s_sum + jnp.sum(p, axis=0, keepdims=True)))
                alphas.append(alpha)
                ps.append(p.astype(BF16))
            return tuple(ps), tuple(new), jnp.where(top, alphas[0], alphas[1])

        def step(j, carry):
            sts, ps_prev, stats, acc = carry
            sts_next = scores(j + 1)
            acc = acc + values_times(ps_prev, jnp.maximum(j - 1, 0))
            ps, stats, alpha = softmax_step(sts, stats, None)
            return sts_next, ps, stats, alpha * acc

        stat = (jnp.full((1, tb), -jnp.inf, F32), jnp.zeros((1, tb), F32))
        no_p = jnp.zeros((tk, tb), BF16)
        below = i * ratio
        sts, ps_prev, stats, acc = lax.fori_loop(
            0, below, step, (scores(0), (no_p, no_p), (stat, stat), jnp.zeros((128, tb), F32)))
        for r in range(ratio):
            sts_next = scores(below + r + 1) if r + 1 < ratio else None
            acc = acc + values_times(ps_prev, jnp.maximum(below + r - 1, 0))
            ps_prev, stats, alpha = softmax_step(sts, stats, r * tk)
            acc = alpha * acc
            sts = sts_next
        acc = acc + values_times(ps_prev, below + ratio - 1)
        (ma, sa), (mb, sb) = stats
        o_ref[...] = (acc / jnp.where(top, sa, sb)).T
        lse_ref[0:1, :] = ma + jnp.log(sa)
        lse_ref[1:2, :] = mb + jnp.log(sb)

    qblk = pl.BlockSpec((tb, 128), lambda b, hp, i: (b * nb + i, hp))
    return pl.pallas_call(
        body, name="fox_fwd", grid=(seqs, N_PAIRS, nb),
        in_specs=[qblk, pl.BlockSpec((l, 128), lambda b, hp, i: (b, hp)),
                  pl.BlockSpec((l, 128), lambda b, hp, i: (b, V_BLOCK0 + hp)),
                  pl.BlockSpec((None, l, 128), lambda b, hp, i: (b * N_FOX_HEADS + 2 * hp, 0, 0)),
                  pl.BlockSpec((None, l, 128), lambda b, hp, i: (b * N_FOX_HEADS + 2 * hp + 1, 0, 0))],
        out_specs=[qblk, pl.BlockSpec((None, 2, tb), lambda b, hp, i: (b * N_PAIRS + hp, 0, i))],
        out_shape=[jax.ShapeDtypeStruct((t, FOX_WIDTH), F32), jax.ShapeDtypeStruct((seqs * N_PAIRS, 2, l), F32)],
        scratch_shapes=[pltpu.VMEM((128, l), BF16)],
        compiler_params=_params(("parallel", "parallel", "arbitrary")),
    )(qn, kn, qkv, c_wide, c_wide)


def _fox_bwd(qn, kn, qkv, c_wide, o, do, lse, seqs):
    t = qn.shape[0]
    l = t // seqs
    tb = min(FOX_BWD_BLOCK, l)
    nb = l // tb
    scale = HEAD_DIM ** -0.5
    one_at = (HEAD_DIM, 0)

    def body(q_ref, k_ref, v_ref, ca_ref, cb_ref, o_ref, do_ref, lse_ref, dq_ref, dk_ref, dv_ref, dc_ref, dcq_ref,
             qt_ref, kt_ref, dot_ref, delta_ref, dqa_ref, dqb_ref):
        top_l = _top_rows((128, l))
        top = _top_rows((128, tb))
        left = _left_lanes((tb, 128))
        row_id = lax.broadcasted_iota(jnp.int32, (128, tb), 0)
        lane_id = lax.broadcasted_iota(jnp.int32, (tb, 128), 1)
        zero_t = jnp.zeros((128, tb), BF16)
        zero_l = jnp.zeros((tb, 128), BF16)
        rows = lambda a: (jnp.where(top, a, zero_t), jnp.where(top, zero_t, a))
        lanes = lambda a: (jnp.where(left, a, zero_l), jnp.where(left, zero_l, a))
        with_one_row = lambda pair: tuple(jnp.where(row_id == one_at[h], 1.0, pair[h]).astype(BF16) for h in (0, 1))
        with_one_lane = lambda pair: tuple(jnp.where(lane_id == one_at[h], 1.0, pair[h]).astype(BF16) for h in (0, 1))
        causal = lax.broadcasted_iota(jnp.int32, (tb, tb), 0) <= lax.broadcasted_iota(jnp.int32, (tb, tb), 1)
        c_refs = (ca_ref, cb_ref)
        dq_refs = (dqa_ref, dqb_ref)

        qt_ref[...] = (q_ref[...].astype(F32) * scale).T.astype(BF16)
        kt_ref[...] = k_ref[...].astype(F32).T.astype(BF16)
        do_t = do_ref[...].T
        dot_ref[...] = do_t.astype(BF16)
        prod_t = do_t * o_ref[...].T
        delta_ref[0:1, :] = jnp.sum(jnp.where(top_l, prod_t, 0.0), axis=0, keepdims=True)
        delta_ref[1:2, :] = jnp.sum(jnp.where(top_l, 0.0, prod_t), axis=0, keepdims=True)
        dqa_ref[...] = jnp.zeros(dqa_ref.shape, F32)
        dqb_ref[...] = jnp.zeros(dqb_ref.shape, F32)

        def kv_block(j, _):
            koff = pl.multiple_of(j * tb, tb)
            k2 = k_ref[pl.ds(koff, tb), :]
            v2 = v_ref[pl.ds(koff, tb), :].astype(BF16)
            kts = with_one_row(rows(kt_ref[:, pl.ds(koff, tb)]))
            cw = tuple(_wide(c_refs[h][pl.ds(koff, tb), :], tb) for h in (0, 1))

            def q_block(i, carry, masked):
                dks, dv = list(carry[:2]), carry[2]
                qoff = pl.multiple_of(i * tb, tb)
                qs = lanes((q_ref[pl.ds(qoff, tb), :].astype(F32) * scale).astype(BF16))
                qs_one = with_one_lane(qs)
                dos = lanes(do_ref[pl.ds(qoff, tb), :].astype(BF16))
                qts = rows(qt_ref[:, pl.ds(qoff, tb)])
                dots = rows(dot_ref[:, pl.ds(qoff, tb)])
                for h in (0, 1):
                    st = jnp.dot(k2, qts[h], preferred_element_type=F32) - cw[h]
                    p = jnp.exp(st - lse_ref[h:h + 1, pl.ds(qoff, tb)])
                    if masked:
                        p = jnp.where(causal, p, 0.0)
                    dp = jnp.dot(v2, dots[h], preferred_element_type=F32)
                    dsb = (p * (dp - delta_ref[h:h + 1, pl.ds(qoff, tb)])).astype(BF16)
                    dv = dv + jnp.dot(p.astype(BF16), dos[h], preferred_element_type=F32)
                    dks[h] = dks[h] + jnp.dot(dsb, qs_one[h], preferred_element_type=F32)
                    dq_refs[h][:, pl.ds(qoff, tb)] += jnp.dot(kts[h], dsb, preferred_element_type=F32)
                return dks[0], dks[1], dv

            z = jnp.zeros((tb, 128), F32)
            carry = q_block(j, (z, z, z), True)
            rest = nb - 1 - j
            carry = lax.fori_loop(
                0, rest // 2, lambda n, c: q_block(j + 2 + 2 * n, q_block(j + 1 + 2 * n, c, False), False), carry)
            dka, dkb, dv = lax.cond(rest % 2 == 1, lambda c: q_block(nb - 1, c, False), lambda c: c, carry)
            dk_ref[pl.ds(koff, tb), :] = jnp.where(left, dka, dkb)
            dv_ref[pl.ds(koff, tb), :] = dv
            dc_ref[0:1, pl.ds(koff, tb)] = -dka.T[one_at[0]:one_at[0] + 1, :]
            dc_ref[1:2, pl.ds(koff, tb)] = -dkb.T[one_at[1]:one_at[1] + 1, :]
            return 0

        lax.fori_loop(0, nb, kv_block, 0)
        dq_ref[...] = (jnp.where(top_l, dqa_ref[...], dqb_ref[...]) * scale).T
        dcq_ref[0:1, :] = dqa_ref[one_at[0]:one_at[0] + 1, :]
        dcq_ref[1:2, :] = dqb_ref[one_at[1]:one_at[1] + 1, :]

    blk = pl.BlockSpec((l, 128), lambda b, hp: (b, hp))
    cspec = lambda k: pl.BlockSpec((None, l, 128), lambda b, hp: (b * N_FOX_HEADS + 2 * hp + k, 0, 0))
    rows2 = pl.BlockSpec((None, 2, l), lambda b, hp: (b * N_PAIRS + hp, 0, 0))
    wide = jax.ShapeDtypeStruct((t, FOX_WIDTH), F32)
    pair_rows = jax.ShapeDtypeStruct((seqs * N_PAIRS, 2, l), F32)
    return pl.pallas_call(
        body, name="fox_bwd", grid=(seqs, N_PAIRS),
        in_specs=[blk, blk, pl.BlockSpec((l, 128), lambda b, hp: (b, V_BLOCK0 + hp)), cspec(0), cspec(1), blk, blk, rows2],
        out_specs=[blk, blk, blk, rows2, rows2],
        out_shape=[wide, wide, wide, pair_rows, pair_rows],
        scratch_shapes=[pltpu.VMEM((128, l), BF16), pltpu.VMEM((128, l), BF16), pltpu.VMEM((128, l), BF16),
                        pltpu.VMEM((2, l), F32), pltpu.VMEM((128, l), F32), pltpu.VMEM((128, l), F32)],
        compiler_params=_params(("parallel", "parallel")),
    )(qn, kn, qkv, c_wide, c_wide, o, do, lse)


SCAN_ROWS = 512
SCAN_COLS = 1024


S5_IN = 128
S5_ST = 512
SCAN_CHUNKS = SCAN_COLS // S5_ST
SCAN_SEGS = 8
LANES = 128


def _cmul(ar, ai, br, bi):
    return ar * br - ai * bi, ar * bi + ai * br


def _powers_into(pw_r, pw_i, a_r, a_i, seg):
    pw_r[0:1, :] = a_r
    pw_i[0:1, :] = a_i
    for k in range(1, seg):
        pr, pi = _cmul(pw_r[k - 1:k, :], pw_i[k - 1:k, :], a_r, a_i)
        pw_r[k:k + 1, :] = pr
        pw_i[k:k + 1, :] = pi


def _interleave(dst, src, seg):
    for h in range(src.shape[0]):
        for j in range(seg):
            dst[h, j * SCAN_SEGS:(j + 1) * SCAN_SEGS, :] = src[h, pl.ds(j, SCAN_SEGS, stride=seg), :]


def _deinterleave(dst, src, seg):
    for h in range(src.shape[0]):
        for j in range(seg):
            dst[h, pl.ds(j, SCAN_SEGS, stride=seg), :] = src[h, j * SCAN_SEGS:(j + 1) * SCAN_SEGS, :]


def _interleaved(ref, tmp_a, tmp_b, seg):
    n = ref.shape[1] // LANES
    for h in range(n):
        tmp_a[h] = ref[:, h * LANES:(h + 1) * LANES].astype(F32)
    _interleave(tmp_b, tmp_a, seg)
    return jnp.concatenate([tmp_b[h] for h in range(n)], axis=1)


def _store_deinterleaved(ref, val, tmp_a, tmp_b, seg):
    n = ref.shape[1] // LANES
    for h in range(n):
        tmp_a[h] = val[:, h * LANES:(h + 1) * LANES]
    _deinterleave(tmp_b, tmp_a, seg)
    for h in range(n):
        ref[:, h * LANES:(h + 1) * LANES] = tmp_b[h]


def _segment_scan(b_r, b_i, x_r, x_i, pw_r, pw_i, car_r, car_i, seg, sign, reverse, visit=None):
    nc = b_r.shape[0]
    sub = lax.broadcasted_iota(jnp.int32, (SCAN_SEGS, LANES), 0)
    lanes = lambda c: slice(c * LANES, (c + 1) * LANES)
    rows = lambda j: pl.ds(pl.multiple_of(((seg - 1 - j) if reverse else j) * SCAN_SEGS, SCAN_SEGS), SCAN_SEGS)
    a1 = [(pw_r[0:1, lanes(c)], sign * pw_i[0:1, lanes(c)]) for c in range(nc)]

    def local(j, xs):
        out = []
        for c in range(nc):
            xr, xi = xs[2 * c], xs[2 * c + 1]
            nr = a1[c][0] * xr - a1[c][1] * xi + b_r[c, rows(j), :]
            ni = a1[c][0] * xi + a1[c][1] * xr + b_i[c, rows(j), :]
            x_r[c, rows(j), :] = nr
            x_i[c, rows(j), :] = ni
            out += [nr, ni]
        return tuple(out)

    zero = jnp.zeros((SCAN_SEGS, LANES), F32)
    ends = lax.fori_loop(0, seg, local, (zero,) * (2 * nc))

    if reverse:
        first = sub == SCAN_SEGS - 1
        neighbour = lambda v: pltpu.roll(v, SCAN_SEGS - 1, 0)
        shift = lambda v, d: jnp.where(sub < SCAN_SEGS - d, pltpu.roll(v, SCAN_SEGS - d, 0), 0.0)
    else:
        first = sub == 0
        neighbour = lambda v: pltpu.roll(v, 1, 0)
        shift = lambda v, d: jnp.where(sub >= d, pltpu.roll(v, d, 0), 0.0)
    last = 0 if reverse else SCAN_SEGS - 1
    entries = []
    for c in range(nc):
        er, ei = ends[2 * c], ends[2 * c + 1]
        pr, pi = pw_r[seg - 1:seg, lanes(c)], sign * pw_i[seg - 1:seg, lanes(c)]
        yr = jnp.where(first, car_r[:, lanes(c)], neighbour(er))
        yi = jnp.where(first, car_i[:, lanes(c)], neighbour(ei))
        qr, qi = pr, pi
        for d in (1, 2, 4):
            mr, mi = _cmul(qr, qi, shift(yr, d), shift(yi, d))
            yr, yi = yr + mr, yi + mi
            qr, qi = _cmul(qr, qi, qr, qi)
        lr, li = _cmul(pr, pi, yr, yi)
        car_r[:, lanes(c)] = (er + lr)[last:last + 1, :]
        car_i[:, lanes(c)] = (ei + li)[last:last + 1, :]
        entries += [yr, yi]

    def correct(j, prev):
        out = []
        row_r, row_i = pw_r[pl.ds(j, 1), :], sign * pw_i[pl.ds(j, 1), :]
        for c in range(nc):
            mr, mi = _cmul(row_r[:, lanes(c)], row_i[:, lanes(c)], entries[2 * c], entries[2 * c + 1])
            nr = x_r[c, rows(j), :] + mr
            ni = x_i[c, rows(j), :] + mi
            x_r[c, rows(j), :] = nr
            x_i[c, rows(j), :] = ni
            if visit is not None:
                visit(c, rows(j), prev[2 * c], prev[2 * c + 1])
            out += [nr, ni]
        return tuple(out)

    lax.fori_loop(0, seg, correct, tuple(entries))


def _s5_fwd(uf, bbr, bbi, cr, ci, ar, ai, seqs):
    t = uf.shape[0]
    l = t // seqs
    tl = min(SCAN_ROWS, l)
    nl = l // tl
    seg = tl // SCAN_SEGS
    cb, nq = SCAN_COLS, SCAN_CHUNKS
    nc = cb // LANES
    per = S5_ST // LANES

    def body(u_ref, bbr_ref, bbi_ref, cr_ref, ci_ref, ar_ref, ai_ref, x_r, x_i, ys_ref,
             car_r, car_i, pw_r, pw_i, b_r, b_i, tmp_a, tmp_b):
        @pl.when(pl.program_id(2) == 0)
        def _():
            car_r[...] = jnp.zeros(car_r.shape, F32)
            car_i[...] = jnp.zeros(car_i.shape, F32)
            _powers_into(pw_r, pw_i, ar_ref[...], ai_ref[...], seg)

        u = _interleaved(u_ref, tmp_a, tmp_b, seg).astype(BF16)
        for q in range(nq):
            uq = u[:, q * S5_IN:(q + 1) * S5_IN]
            br = jnp.dot(uq, bbr_ref[q], preferred_element_type=F32)
            bi = jnp.dot(uq, bbi_ref[q], preferred_element_type=F32)
            for s in range(per):
                b_r[q * per + s] = br[:, s * LANES:(s + 1) * LANES]
                b_i[q * per + s] = bi[:, s * LANES:(s + 1) * LANES]
        _segment_scan(b_r, b_i, x_r, x_i, pw_r, pw_i, car_r, car_i, seg, 1.0, False)
        wide = lambda buf, q: jnp.concatenate([buf[q * per + s] for s in range(per)], axis=1).astype(BF16)
        ys = [jnp.dot(wide(x_r, q), cr_ref[q], preferred_element_type=F32)
              + jnp.dot(wide(x_i, q), ci_ref[q], preferred_element_type=F32) for q in range(nq)]
        _store_deinterleaved(ys_ref, jnp.concatenate(ys, axis=1), tmp_a, tmp_b, seg)

    rows = lambda w: pl.BlockSpec((tl, w), lambda s, j, r: (s * nl + r, j))
    state = pl.BlockSpec((nc, tl, LANES), lambda s, j, r: (j, s * nl + r, 0))
    chunk = lambda a: pl.BlockSpec((nq,) + a.shape[1:], lambda s, j, r: (j, 0, 0))
    par = pl.BlockSpec((1, cb), lambda s, j, r: (0, j))
    return pl.pallas_call(
        body, name="s5_fwd", grid=(seqs, S5_CH // cb, nl),
        in_specs=[rows(nq * S5_IN), chunk(bbr), chunk(bbi), chunk(cr), chunk(ci), par, par],
        out_specs=[state, state, rows(nq * S5_IN)],
        out_shape=[jax.ShapeDtypeStruct((S5_CH // LANES, t, LANES), F32)] * 2
        + [jax.ShapeDtypeStruct((t, S5_WIDTH), F32)],
        scratch_shapes=[pltpu.VMEM((1, cb), F32), pltpu.VMEM((1, cb), F32), pltpu.VMEM((seg, cb), F32),
                        pltpu.VMEM((seg, cb), F32)] + [pltpu.VMEM((nc, tl, LANES), F32)] * 2
        + [pltpu.VMEM((nq * S5_IN // LANES, tl, LANES), F32)] * 2,
        compiler_params=_params(("parallel", "parallel", "arbitrary")),
    )(uf, bbr, bbi, cr, ci, ar, ai)


def _s5_bwd(dys, uf, xr, xi, bbr, bbi, cr, ci, ar, ai, seqs):
    t = dys.shape[0]
    l = t // seqs
    tl = min(SCAN_ROWS, l)
    nl = l // tl
    seg = tl // SCAN_SEGS
    cb, nq = SCAN_COLS, SCAN_CHUNKS
    nc = cb // LANES
    per = S5_ST // LANES

    def body(dy_ref, u_ref, x_r, x_i, bbr_ref, bbi_ref, cr_ref, ci_ref, ar_ref, ai_ref,
             du_ref, dbbr_ref, dbbi_ref, dcr_ref, dci_ref, dar_ref, dai_ref,
             car_r, car_i, pw_r, pw_i, g_r, g_i, lam_r, lam_i, acc_r, acc_i, tmp_a, tmp_b):
        @pl.when(pl.program_id(2) == 0)
        def _():
            car_r[...] = jnp.zeros(car_r.shape, F32)
            car_i[...] = jnp.zeros(car_i.shape, F32)
            _powers_into(pw_r, pw_i, ar_ref[...], ai_ref[...], seg)
            for acc_ref in (dbbr_ref, dbbi_ref, dcr_ref, dci_ref, dar_ref, dai_ref):
                acc_ref[...] = jnp.zeros(acc_ref.shape, F32)

        dy = _interleaved(dy_ref, tmp_a, tmp_b, seg).astype(BF16)
        for q in range(nq):
            dyq = dy[:, q * S5_IN:(q + 1) * S5_IN]
            gr = lax.dot_general(dyq, cr_ref[q], _NT, preferred_element_type=F32)
            gi = lax.dot_general(dyq, ci_ref[q], _NT, preferred_element_type=F32)
            for s in range(per):
                g_r[q * per + s] = gr[:, s * LANES:(s + 1) * LANES]
                g_i[q * per + s] = gi[:, s * LANES:(s + 1) * LANES]
        acc_r[...] = jnp.zeros(acc_r.shape, F32)
        acc_i[...] = jnp.zeros(acc_i.shape, F32)

        def visit(c, rws, lr, li):
            xr_t, xi_t = x_r[c, rws, :], x_i[c, rws, :]
            acc_r[c] += lr * xr_t + li * xi_t
            acc_i[c] += li * xr_t - lr * xi_t

        _segment_scan(g_r, g_i, lam_r, lam_i, pw_r, pw_i, car_r, car_i, seg, -1.0, True, visit)
        for c in range(nc):
            dar_ref[:, c * LANES:(c + 1) * LANES] += jnp.sum(acc_r[c], axis=0, keepdims=True)
            dai_ref[:, c * LANES:(c + 1) * LANES] += jnp.sum(acc_i[c], axis=0, keepdims=True)
        u = _interleaved(u_ref, tmp_a, tmp_b, seg).astype(BF16)
        wide = lambda buf, q: jnp.concatenate([buf[q * per + s] for s in range(per)], axis=1).astype(BF16)
        du = []
        for q in range(nq):
            io = slice(q * S5_IN, (q + 1) * S5_IN)
            lq_r, lq_i = wide(lam_r, q), wide(lam_i, q)
            du.append(lax.dot_general(lq_r, bbr_ref[q], _NT, preferred_element_type=F32)
                      + lax.dot_general(lq_i, bbi_ref[q], _NT, preferred_element_type=F32))
            dbbr_ref[q] += lax.dot_general(u[:, io], lq_r, _TN, preferred_element_type=F32)
            dbbi_ref[q] += lax.dot_general(u[:, io], lq_i, _TN, preferred_element_type=F32)
            dcr_ref[q] += lax.dot_general(wide(x_r, q), dy[:, io], _TN, preferred_element_type=F32)
            dci_ref[q] += lax.dot_general(wide(x_i, q), dy[:, io], _TN, preferred_element_type=F32)
        _store_deinterleaved(du_ref, jnp.concatenate(du, axis=1), tmp_a, tmp_b, seg)

    rows = lambda w: pl.BlockSpec((tl, w), lambda s, j, r: (s * nl + nl - 1 - r, j))
    state = pl.BlockSpec((nc, tl, LANES), lambda s, j, r: (j, s * nl + nl - 1 - r, 0))
    chunk = lambda a: pl.BlockSpec((nq,) + a.shape[1:], lambda s, j, r: (j, 0, 0))
    acc = lambda a: pl.BlockSpec((None, nq) + a.shape[1:], lambda s, j, r: (s, j, 0, 0))
    par = pl.BlockSpec((1, cb), lambda s, j, r: (0, j))
    par_acc = pl.BlockSpec((None, 1, cb), lambda s, j, r: (s, 0, j))
    per_seq = lambda a: jax.ShapeDtypeStruct((seqs,) + a.shape, F32)
    return pl.pallas_call(
        body, name="s5_bwd", grid=(seqs, S5_CH // cb, nl),
        in_specs=[rows(nq * S5_IN), rows(nq * S5_IN), state, state, chunk(bbr), chunk(bbi), chunk(cr), chunk(ci),
                  par, par],
        out_specs=[rows(nq * S5_IN), acc(bbr), acc(bbi), acc(cr), acc(ci), par_acc, par_acc],
        out_shape=[jax.ShapeDtypeStruct((t, S5_WIDTH), F32), per_seq(bbr), per_seq(bbi), per_seq(cr), per_seq(ci),
                   jax.ShapeDtypeStruct((seqs, 1, S5_CH), F32), jax.ShapeDtypeStruct((seqs, 1, S5_CH), F32)],
        scratch_shapes=[pltpu.VMEM((1, cb), F32), pltpu.VMEM((1, cb), F32), pltpu.VMEM((seg, cb), F32),
                        pltpu.VMEM((seg, cb), F32)] + [pltpu.VMEM((nc, tl, LANES), F32)] * 4
        + [pltpu.VMEM((nc, SCAN_SEGS, LANES), F32)] * 2 + [pltpu.VMEM((nq * S5_IN // LANES, tl, LANES), F32)] * 2,
        compiler_params=_params(("parallel", "parallel", "arbitrary")),
    )(dys, uf, xr, xi, bbr, bbi, cr, ci, ar, ai)


XATT_BLOCK = 2048


def _xatt_probs(qv, kv):
    s = lax.dot_general(qv, kv, _NT, preferred_element_type=F32) * (X_HEAD_DIM ** -0.5)
    e = jnp.exp(s - jnp.max(s, axis=-1, keepdims=True))
    return e / jnp.sum(e, axis=-1, keepdims=True)


def _xatt_fwd(q, k, kv, seqs):
    t = q.shape[0]
    tq = min(XATT_BLOCK, t // seqs)
    nq = t // seqs // tq

    def body(q_ref, k_ref, v_ref, o_ref):
        p = _xatt_probs(q_ref[...], k_ref[...])
        o_ref[...] = jnp.dot(p.astype(BF16), v_ref[...].astype(BF16), preferred_element_type=F32).astype(o_ref.dtype)

    qs = pl.BlockSpec((tq, X_HEAD_DIM), lambda b, h, i: (b * nq + i, h))
    return pl.pallas_call(
        body, name="xatt_fwd", grid=(seqs, N_X_HEADS, nq),
        in_specs=[qs, pl.BlockSpec((N_MEM, X_HEAD_DIM), lambda b, h, i: (b, h)),
                  pl.BlockSpec((N_MEM, X_HEAD_DIM), lambda b, h, i: (b, N_X_HEADS + h))],
        out_specs=qs, out_shape=jax.ShapeDtypeStruct(q.shape, BF16),
        compiler_params=_params(("parallel", "parallel", "parallel")),
    )(q, k, kv)


def _xatt_bwd(q, k, kv, do, seqs):
    t = q.shape[0]
    tq = min(XATT_BLOCK, t // seqs)
    nq = t // seqs // tq
    scale = X_HEAD_DIM ** -0.5

    def body(q_ref, k_ref, v_ref, do_ref, dq_ref, dk_ref, dv_ref):
        @pl.when(pl.program_id(2) == 0)
        def _():
            dk_ref[...] = jnp.zeros(dk_ref.shape, F32)
            dv_ref[...] = jnp.zeros(dv_ref.shape, F32)

        qv, kk = q_ref[...], k_ref[...]
        p = _xatt_probs(qv, kk)
        dob = do_ref[...].astype(BF16)
        dp = lax.dot_general(dob, v_ref[...].astype(BF16), _NT, preferred_element_type=F32)
        ds = p * (dp - jnp.sum(dp * p, axis=-1, keepdims=True))
        dsb = ds.astype(BF16)
        dq_ref[...] = jnp.dot(dsb, kk, preferred_element_type=F32) * scale
        dk_ref[...] += lax.dot_general(dsb, qv, _TN, preferred_element_type=F32) * scale
        dv_ref[...] += lax.dot_general(p.astype(BF16), dob, _TN, preferred_element_type=F32)

    qs = pl.BlockSpec((tq, X_HEAD_DIM), lambda b, h, i: (b * nq + i, h))
    ks = pl.BlockSpec((N_MEM, X_HEAD_DIM), lambda b, h, i: (b, h))
    return pl.pallas_call(
        body, name="xatt_bwd", grid=(seqs, N_X_HEADS, nq),
        in_specs=[qs, ks, pl.BlockSpec((N_MEM, X_HEAD_DIM), lambda b, h, i: (b, N_X_HEADS + h)), qs],
        out_specs=[qs, ks, ks],
        out_shape=[jax.ShapeDtypeStruct(q.shape, F32), jax.ShapeDtypeStruct(k.shape, F32),
                   jax.ShapeDtypeStruct(k.shape, F32)],
        compiler_params=_params(("parallel", "parallel", "arbitrary")),
    )(q, k, kv, do)


CONV_COLS = 256


def _shift_down(x, k, row):
    return jnp.where(row >= k, pltpu.roll(x, k, 0), 0.0)


def _shift_up(x, k, row):
    n = x.shape[0]
    return jnp.where(row < n - k, pltpu.roll(x, n - k, 0), 0.0)


def _conv_pre(g, w, b, row):
    return b + w[0:1, :] * _shift_down(g, 2, row) + w[1:2, :] * _shift_down(g, 1, row) + w[2:3, :] * g


def _convgate_fwd(gu, w, b, seqs):
    t = gu.shape[0]
    l = t // seqs
    nc = D_FF // CONV_COLS

    def body(g_ref, u_ref, w_ref, b_ref, o_ref):
        g = g_ref[...].astype(F32)
        row = lax.broadcasted_iota(jnp.int32, g.shape, 0)
        pre = _conv_pre(g, w_ref[...], b_ref[...], row)
        o_ref[...] = (pre * jax.nn.sigmoid(pre) * u_ref[...].astype(F32)).astype(o_ref.dtype)

    return pl.pallas_call(
        body, name="convgate_fwd", grid=(seqs, nc),
        in_specs=[pl.BlockSpec((l, CONV_COLS), lambda s, j: (s, j)), pl.BlockSpec((l, CONV_COLS), lambda s, j: (s, nc + j)),
                  pl.BlockSpec((3, CONV_COLS), lambda s, j: (0, j)), pl.BlockSpec((1, CONV_COLS), lambda s, j: (0, j))],
        out_specs=pl.BlockSpec((l, CONV_COLS), lambda s, j: (s, j)),
        out_shape=jax.ShapeDtypeStruct((t, D_FF), BF16),
        compiler_params=_params(("parallel", "parallel")),
    )(gu, gu, w, b)


def _convgate_bwd(gu, w, b, dact, seqs):
    t = gu.shape[0]
    l = t // seqs
    nc = D_FF // CONV_COLS
    steps = nc * seqs

    def body(g_ref, u_ref, w_ref, b_ref, da_ref, dgu_ref, dw_ref, db_ref, stage, sems):
        j, s = pl.program_id(0), pl.program_id(1)
        n = j * seqs + s
        slot = n % 2

        def copies(slot_, j_, s_):
            rows = pl.ds(pl.multiple_of(s_ * l, 16), l)
            return [pltpu.make_async_copy(
                stage.at[slot_, half],
                dgu_ref.at[rows, pl.ds(pl.multiple_of((half * nc + j_) * CONV_COLS, 128), CONV_COLS)],
                sems.at[slot_, half]) for half in (0, 1)]

        @pl.when(s == 0)
        def _():
            dw_ref[...] = jnp.zeros(dw_ref.shape, F32)
            db_ref[...] = jnp.zeros(db_ref.shape, F32)

        @pl.when(n >= 2)
        def _():
            for cp in copies(slot, j, s):
                cp.wait()

        g, wv, da = g_ref[...].astype(F32), w_ref[...], da_ref[...].astype(F32)
        row = lax.broadcasted_iota(jnp.int32, g.shape, 0)
        g1, g2 = _shift_down(g, 1, row), _shift_down(g, 2, row)
        pre = b_ref[...] + wv[0:1, :] * g2 + wv[1:2, :] * g1 + wv[2:3, :] * g
        sg = jax.nn.sigmoid(pre)
        silu = pre * sg
        stage[slot, 1] = (da * silu).astype(stage.dtype)
        dpre = da * u_ref[...].astype(F32) * (sg * (1.0 + pre * (1.0 - sg)))
        dg = wv[2:3, :] * dpre + wv[1:2, :] * _shift_up(dpre, 1, row) + wv[0:1, :] * _shift_up(dpre, 2, row)
        stage[slot, 0] = dg.astype(stage.dtype)
        for cp in copies(slot, j, s):
            cp.start()
        dw_ref[0:1, :] += jnp.sum(dpre * g2, axis=0, keepdims=True)
        dw_ref[1:2, :] += jnp.sum(dpre * g1, axis=0, keepdims=True)
        dw_ref[2:3, :] += jnp.sum(dpre * g, axis=0, keepdims=True)
        db_ref[...] += jnp.sum(dpre, axis=0, keepdims=True)

        @pl.when(n == steps - 1)
        def _():
            for cp in copies(slot, j, s) + (copies(1 - slot, j, s) if steps > 1 else []):
                cp.wait()

    blk = lambda off: pl.BlockSpec((l, CONV_COLS), lambda j, s: (s, off + j))
    return pl.pallas_call(
        body, name="convgate_bwd", grid=(nc, seqs),
        in_specs=[blk(0), blk(nc), pl.BlockSpec((3, CONV_COLS), lambda j, s: (0, j)),
                  pl.BlockSpec((1, CONV_COLS), lambda j, s: (0, j)), blk(0)],
        out_specs=[ANY, pl.BlockSpec((3, CONV_COLS), lambda j, s: (0, j)),
                   pl.BlockSpec((1, CONV_COLS), lambda j, s: (0, j))],
        out_shape=[jax.ShapeDtypeStruct((t, 2 * D_FF), BF16), jax.ShapeDtypeStruct((3, D_FF), F32),
                   jax.ShapeDtypeStruct((1, D_FF), F32)],
        scratch_shapes=[pltpu.VMEM((2, 2, l, CONV_COLS), BF16), pltpu.SemaphoreType.DMA((2, 2))],
        compiler_params=_params(("arbitrary", "arbitrary")),
    )(gu, gu, w, b, dact)


def _loss_head(h, target):
    t, d = h.shape
    tm = _pick(t, (256, 128, 8))

    def body(h_ref, t_ref, dh_ref, dhb_ref, loss_ref):
        @pl.when(pl.program_id(0) == 0)
        def _():
            loss_ref[...] = jnp.zeros(loss_ref.shape, F32)

        e = h_ref[...] - t_ref[...]
        dh = e * (1.0 / d)
        dh_ref[...] = dh
        dhb_ref[...] = dh.astype(BF16)
        loss_ref[...] += (0.5 / d) * jnp.sum(jnp.sum(e * e, axis=1, keepdims=True), axis=0, keepdims=True)

    blk = pl.BlockSpec((tm, d), lambda i: (i, 0))
    return pl.pallas_call(
        body, name="loss_head", grid=(t // tm,), in_specs=[blk, blk],
        out_specs=[blk, blk, pl.BlockSpec((1, 1), lambda i: (0, 0))],
        out_shape=[jax.ShapeDtypeStruct((t, d), F32), jax.ShapeDtypeStruct((t, d), BF16),
                   jax.ShapeDtypeStruct((1, 1), F32)],
        compiler_params=_params(("arbitrary",)),
    )(h, target)


def _s5_discretise(a_re, a_im, log_dt, b_re, b_im):
    dt = jnp.exp(log_dt)[:, None]
    mag = jnp.exp(a_re * dt)
    lb_r = mag * jnp.cos(a_im * dt)
    lb_i = mag * jnp.sin(a_im * dt)
    den = a_re * a_re + a_im * a_im
    nr = lb_r - 1.0
    coef_r = (nr * a_re + lb_i * a_im) / den
    coef_i = (lb_i * a_re - nr * a_im) / den
    bb_r = coef_r[:, :, None] * b_re - coef_i[:, :, None] * b_im
    bb_i = coef_r[:, :, None] * b_im + coef_i[:, :, None] * b_re
    return lb_r, lb_i, bb_r, bb_i


S5_CHUNKS = 4
S5_PER = S5_GROUPS // S5_CHUNKS


def _blockdiag_in(bb):
    eye = jnp.eye(S5_PER, dtype=bb.dtype)
    return jnp.einsum("jgpc,gh->jgchp", bb.reshape(S5_CHUNKS, S5_PER, S5_STATE, S5_GROUP_CH), eye).reshape(
        S5_CHUNKS, S5_PER * S5_GROUP_CH, S5_PER * S5_STATE)


def _blockdiag_in_grad(d):
    eye = jnp.eye(S5_PER, dtype=d.dtype)
    return jnp.einsum("jgchp,gh->jgpc", d.reshape(S5_CHUNKS, S5_PER, S5_GROUP_CH, S5_PER, S5_STATE), eye).reshape(
        S5_GROUPS, S5_STATE, S5_GROUP_CH)


def _blockdiag_out(c):
    eye = jnp.eye(S5_PER, dtype=c.dtype)
    return jnp.einsum("jgcp,gh->jgphc", c.reshape(S5_CHUNKS, S5_PER, S5_GROUP_CH, S5_STATE), eye).reshape(
        S5_CHUNKS, S5_PER * S5_STATE, S5_PER * S5_GROUP_CH)


def _blockdiag_out_grad(d):
    eye = jnp.eye(S5_PER, dtype=d.dtype)
    return jnp.einsum("jgphc,gh->jgcp", d.reshape(S5_CHUNKS, S5_PER, S5_STATE, S5_PER, S5_GROUP_CH), eye).reshape(
        S5_GROUPS, S5_GROUP_CH, S5_STATE)


def _local_step(x3, mem3, target3, p, wb, late_weights=None, early_grads=None):
    seqs, l, d = x3.shape
    t = seqs * l
    x = x3.reshape(t, d)
    mem = mem3.reshape(seqs * N_MEM, d)
    target = target3.reshape(t, d)
    full = lambda a: (a, a.shape[1], 0, 0)

    s5_in = (p["s5_a_re"], p["s5_a_im"], p["s5_log_dt"], p["s5_b_re"], p["s5_b_im"])
    (lb_r, lb_i, bb_r, bb_i), s5_pull = jax.vjp(_s5_discretise, *s5_in)
    ar, ai = lb_r.reshape(1, S5_CH), lb_i.reshape(1, S5_CH)
    bbr_d, bbi_d = _blockdiag_in(bb_r).astype(BF16), _blockdiag_in(bb_i).astype(BF16)
    cr_d, ci_d = _blockdiag_out(p["s5_c_re"]).astype(BF16), (-_blockdiag_out(p["s5_c_im"])).astype(BF16)
    d_row = p["s5_d"].reshape(1, S5_WIDTH)

    w_in = wb["w_in"]
    w_qkv = w_in[:, :3 * FOX_WIDTH]
    w_uf = jnp.concatenate(
        [w_in[:, 3 * FOX_WIDTH + N_FOX_HEADS:], w_in[:, 3 * FOX_WIDTH:3 * FOX_WIDTH + N_FOX_HEADS],
         jnp.zeros((d, UF_COLS - S5_WIDTH - N_FOX_HEADS), w_in.dtype)], axis=1)

    hn1 = _rowwise(_rms, [full(x)], [p["norm_mix"]], [(d, d, 0, BF16)], "norm_mix_fwd")
    qkv = _mm(hn1, w_qkv, "nn", "in_qkv")
    uf = _mm(hn1, w_uf, "nn", "in_uf")

    bh = seqs * N_FOX_HEADS
    q_pair = (qkv, 128, 0, 1)
    k_pair = (qkv, 128, N_PAIRS, 1)
    gq2, gk2 = jnp.tile(p["fox_q_norm"], (1, 2)), jnp.tile(p["fox_k_norm"], (1, 2))
    pair_out = [(FOX_WIDTH, 128, 1, BF16)]
    qn = _rowwise(_rms_pair, [q_pair], [gq2], pair_out, "fox_qnorm_fwd", heads=N_PAIRS)
    kn = _rowwise(_rms_pair, [k_pair], [gk2], pair_out, "fox_knorm_fwd", heads=N_PAIRS)

    f_rows = uf[:, S5_WIDTH:S5_WIDTH + N_FOX_HEADS].reshape(seqs, l, N_FOX_HEADS).transpose(0, 2, 1).reshape(bh, l)
    f_bias = jnp.tile(p["fox_f_bias"].reshape(N_FOX_HEADS, 1), (seqs, 1))
    c_wide = jnp.broadcast_to(_forget_fwd(f_rows, f_bias)[:, :, None], (bh, l, 128))
    fox, lse = _fox_fwd(qn, kn, qkv, c_wide, seqs)

    xr, xi, ys = _s5_fwd(uf, bbr_d, bbi_d, cr_d, ci_d, ar, ai, seqs)
    u_blk = (uf, S5_WIDTH, 0, 0)
    yg = _rowwise(_s5_act, [full(ys), u_blk], [d_row], [(S5_WIDTH, S5_WIDTH, 0, F32)], "s5_act_fwd")
    if late_weights is not None:
        wb = dict(wb, **late_weights("mid", yg))
    z = _mm(yg, wb["s5_w_glu"], "nn", "s5_glu")
    y2n = _rowwise(_s5_gate, [full(yg), full(z)], [p["s5_b_glu"], p["out_norm_s5"]],
                   [(S5_WIDTH, S5_WIDTH, 0, BF16)], "s5_gate_fwd")
    foxn = _rowwise(_rms, [full(fox)], [p["out_norm_fox"]], [(FOX_WIDTH, FOX_WIDTH, 0, BF16)], "fox_outnorm_fwd")
    mixed = jnp.concatenate([foxn, y2n], axis=1)
    h1 = _mm(mixed, wb["w_out"], "nn", "mix_out", res=x)
    if late_weights is not None:
        wb = dict(wb, **late_weights("late", h1))

    hn2 = _rowwise(_rms, [full(h1)], [p["norm_cross"]], [(d, d, 0, BF16)], "norm_cross_fwd")
    mn = _rowwise(_rms, [full(mem)], [p["norm_mem"]], [(d, d, 0, BF16)], "norm_mem_fwd")
    xq_raw = _mm(hn2, wb["w_xq"], "nn", "x_q")
    kv = _mm(mn, wb["w_xkv"], "nn", "x_kv")
    xh = lambda a: (a, X_HEAD_DIM, 0, 1)
    xqn = _rowwise(_rms, [xh(xq_raw)], [p["xq_norm"]], [(d, X_HEAD_DIM, 1, BF16)], "x_qnorm_fwd", heads=N_X_HEADS)
    xkn = _rowwise(_rms, [xh(kv)], [p["xk_norm"]], [(d, X_HEAD_DIM, 1, BF16)], "x_knorm_fwd", heads=N_X_HEADS)
    xo = _xatt_fwd(xqn, xkn, kv, seqs)
    h2 = _mm(xo, wb["w_xo"], "nn", "x_out", res=h1)

    hn3 = _rowwise(_rms, [full(h2)], [p["norm_ffn"]], [(d, d, 0, BF16)], "norm_ffn_fwd")
    gu = _mm(hn3, wb["w_ffn_up"], "nn", "ffn_up", out_dtype=BF16)
    act = _convgate_fwd(gu, p["ffn_conv_w"], p["ffn_conv_b"], seqs)
    h3 = _mm(act, wb["w_ffn_down"], "nn", "ffn_down", res=h2)
    dh3, dh3_b, loss = _loss_head(h3, target)

    g = {}
    dact = _mm(dh3_b, wb["w_ffn_down"], "nt", "ffn_down_dx", out_dtype=BF16)
    late_dt = BF16 if early_grads is not None else F32
    g["w_ffn_down"] = _mm(act, dh3_b, "tn", "ffn_down_dw", out_dtype=late_dt)
    dgu, g["ffn_conv_w"], g["ffn_conv_b"] = _convgate_bwd(gu, p["ffn_conv_w"], p["ffn_conv_b"], dact, seqs)
    dhn3 = _mm(dgu, wb["w_ffn_up"], "nt", "ffn_up_dx")
    g["w_ffn_up"] = _mm(hn3, dgu, "tn", "ffn_up_dw", out_dtype=late_dt)
    (dh2,), (g["norm_ffn"],) = _rowwise_vjp(_rms, [full(h2)], [p["norm_ffn"]], [full(dhn3)], "norm_ffn_bwd",
                                            adds=[full(dh3)])

    dxo = _mm(dh2, wb["w_xo"], "nt", "x_out_dx")
    g["w_xo"] = _mm(xo, dh2, "tn", "x_out_dw", out_dtype=late_dt)
    dxqn, dxkn, dxv = _xatt_bwd(xqn, xkn, kv, dxo, seqs)
    (dxq_raw,), (g["xq_norm"],) = _rowwise_vjp(_rms, [xh(xq_raw)], [p["xq_norm"]], [xh(dxqn)], "x_qnorm_bwd",
                                               heads=N_X_HEADS, row_dtypes=[BF16])
    (dxk_raw,), (g["xk_norm"],) = _rowwise_vjp(_rms, [xh(kv)], [p["xk_norm"]], [xh(dxkn)], "x_knorm_bwd",
                                               heads=N_X_HEADS, row_dtypes=[BF16])
    dkv = jnp.concatenate([dxk_raw, dxv.astype(BF16)], axis=1)
    dhn2 = _mm(dxq_raw, wb["w_xq"], "nt", "x_q_dx")
    g["w_xq"] = _mm(hn2, dxq_raw, "tn", "x_q_dw", out_dtype=late_dt)
    dmn = _mm(dkv, wb["w_xkv"], "nt", "x_kv_dx")
    g["w_xkv"] = _mm(mn, dkv, "tn", "x_kv_dw", out_dtype=late_dt)
    norm_cross = p["norm_cross"]
    if early_grads is not None:
        norm_cross = norm_cross + early_grads({n: g[n] for n in LATE_WEIGHTS})
    (dh1,), (g["norm_cross"],) = _rowwise_vjp(_rms, [full(h1)], [norm_cross], [full(dhn2)], "norm_cross_bwd",
                                              adds=[full(dh2)])
    _, (g["norm_mem"],) = _rowwise_vjp(_rms, [full(mem)], [p["norm_mem"]], [full(dmn)], "norm_mem_bwd",
                                       row_dtypes=[BF16])

    dmixed = _mm(dh1, wb["w_out"], "nt", "mix_out_dx")
    g["w_out"] = _mm(mixed, dh1, "tn", "mix_out_dw", out_dtype=late_dt)
    (dfox,), (g["out_norm_fox"],) = _rowwise_vjp(_rms, [full(fox)], [p["out_norm_fox"]],
                                                 [(dmixed, FOX_WIDTH, 0, 0)], "fox_outnorm_bwd")
    (dyg_a, dz), (g["s5_b_glu"], g["out_norm_s5"]) = _rowwise_vjp(
        _s5_gate, [full(yg), full(z)], [p["s5_b_glu"], p["out_norm_s5"]], [(dmixed, S5_WIDTH, 1, 0)], "s5_gate_bwd",
        row_dtypes=[F32, BF16])
    dyg = _mm(dz, wb["s5_w_glu"], "nt", "s5_glu_dx", res=dyg_a)
    g["s5_w_glu"] = _mm(yg, dz, "tn", "s5_glu_dw", out_dtype=late_dt)
    (dys, du_a), (dd_row,) = _rowwise_vjp(_s5_act, [full(ys), u_blk], [d_row], [full(dyg)], "s5_act_bwd",
                                          row_dtypes=[BF16, F32])
    g["s5_d"] = dd_row
    du_b, dbbr_d, dbbi_d, dcr_d, dci_d, dar, dai = _s5_bwd(dys, uf, xr, xi, bbr_d, bbi_d, cr_d, ci_d, ar, ai, seqs)
    dbbr_d, dbbi_d, dcr_d, dci_d = (jnp.sum(a, axis=0) for a in (dbbr_d, dbbi_d, dcr_d, dci_d))
    d_lb_r = jnp.sum(dar, axis=0).reshape(S5_GROUPS, S5_STATE)
    d_lb_i = jnp.sum(dai, axis=0).reshape(S5_GROUPS, S5_STATE)
    g["s5_a_re"], g["s5_a_im"], g["s5_log_dt"], g["s5_b_re"], g["s5_b_im"] = s5_pull(
        (d_lb_r, d_lb_i, _blockdiag_in_grad(dbbr_d), _blockdiag_in_grad(dbbi_d)))
    g["s5_c_re"] = _blockdiag_out_grad(dcr_d)
    g["s5_c_im"] = -_blockdiag_out_grad(dci_d)

    dqn, dkn, dv, dc, dcq = _fox_bwd(qn, kn, qkv, c_wide, fox, dfox, lse, seqs)
    pair = lambda a: (a, 128, 0, 1)
    (dq_raw,), (dgq2,) = _rowwise_vjp(_rms_pair, [q_pair], [gq2], [pair(dqn)], "fox_qnorm_bwd", heads=N_PAIRS,
                                      row_dtypes=[BF16])
    (dk_raw,), (dgk2,) = _rowwise_vjp(_rms_pair, [k_pair], [gk2], [pair(dkn)], "fox_knorm_bwd", heads=N_PAIRS,
                                      row_dtypes=[BF16])
    g["fox_q_norm"] = dgq2[:, :HEAD_DIM] + dgq2[:, HEAD_DIM:]
    g["fox_k_norm"] = dgk2[:, :HEAD_DIM] + dgk2[:, HEAD_DIM:]
    df_rows, dfb = _forget_bwd(f_rows, f_bias, (dc + dcq).reshape(bh, l))
    g["fox_f_bias"] = jnp.sum(dfb.reshape(seqs, N_FOX_HEADS), axis=0)
    df = df_rows.reshape(seqs, N_FOX_HEADS, l).transpose(0, 2, 1).reshape(t, N_FOX_HEADS)
    dqkv = jnp.concatenate([dq_raw, dk_raw, dv.astype(BF16)], axis=1)
    duf = jnp.concatenate([du_a + du_b, df, jnp.zeros((t, UF_COLS - S5_WIDTH - N_FOX_HEADS), F32)],
                          axis=1).astype(BF16)
    dhn1 = _mm(duf, w_uf, "nt", "in_uf_dx", res=_mm(dqkv, w_qkv, "nt", "in_qkv_dx"))
    dw_qkv = _mm(hn1, dqkv, "tn", "in_qkv_dw")
    dw_uf = _mm(hn1, duf, "tn", "in_uf_dw")
    g["w_in"] = jnp.concatenate([dw_qkv, dw_uf[:, S5_WIDTH:S5_WIDTH + N_FOX_HEADS], dw_uf[:, :S5_WIDTH]], axis=1)
    (dx,), (g["norm_mix"],) = _rowwise_vjp(_rms, [full(x)], [p["norm_mix"]], [full(dhn1)], "norm_mix_bwd",
                                           adds=[full(dh1)])
    return loss, dx.reshape(seqs, l, d), g


def _place():
    return lax.axis_index("x"), lax.axis_index("y"), lax.axis_index("c")


def _other_chips(x, y):
    return [(1 - x, y), (x, 1 - y), (1 - x, 1 - y)]


ANY = pl.BlockSpec(memory_space=pl.ANY)


def _gather_weights(shards, col_kind, taps):
    n = len(shards)

    def body(*refs):
        ins, tap_in, outs, tap_out = refs[:n], refs[n], refs[n + 1:2 * n + 1], refs[2 * n + 1]
        ici_send, ici_recv, d2d_send, d2d_recv, own_send, own_recv = refs[2 * n + 2:]
        x, y, c = _place()
        mine = 2 * x + y
        chips = _other_chips(x, y)
        sibling = (x, y, 1 - c)

        def piece(a, s, h):
            r, cs = ins[a].shape
            hr = r // 2
            if col_kind[a]:
                return outs[a].at[pl.ds(pl.multiple_of(h * hr, 16), hr), pl.ds(pl.multiple_of(s * cs, 128), cs)]
            return outs[a].at[pl.ds(pl.multiple_of(s * r + h * hr, 16), hr), :]

        def slab(a, s):
            r, cs = ins[a].shape
            if col_kind[a]:
                return outs[a].at[:, pl.ds(pl.multiple_of(s * cs, 128), cs)]
            return outs[a].at[pl.ds(pl.multiple_of(s * r, 16), r), :]

        def own_half(a, h):
            hr = ins[a].shape[0] // 2
            return ins[a].at[pl.ds(pl.multiple_of(h * hr, 16), hr), :]

        sends = []
        for a in range(n):
            cp = pltpu.make_async_remote_copy(
                src_ref=ins[a], dst_ref=slab(a, mine), send_sem=own_send.at[a], recv_sem=own_recv.at[a],
                device_id=sibling, device_id_type=MESH)
            cp.start()
            sends.append(cp)
        cp = pltpu.make_async_remote_copy(
            src_ref=tap_in, dst_ref=tap_out.at[mine], send_sem=own_send.at[n], recv_sem=own_recv.at[n],
            device_id=sibling, device_id_type=MESH)
        cp.start()
        sends.append(cp)
        for a in range(n):
            for j, (px, py) in enumerate(chips):
                cp = pltpu.make_async_remote_copy(
                    src_ref=own_half(a, c), dst_ref=piece(a, mine, c), send_sem=ici_send.at[3 * a + j],
                    recv_sem=ici_recv.at[3 * a + j], device_id=(px, py, c), device_id_type=MESH)
                cp.start()
                sends.append(cp)
        for j, (px, py) in enumerate(chips):
            cp = pltpu.make_async_remote_copy(
                src_ref=tap_in, dst_ref=tap_out.at[mine], send_sem=ici_send.at[3 * n + j],
                recv_sem=ici_recv.at[3 * n + j], device_id=(px, py, c), device_id_type=MESH)
            cp.start()
            sends.append(cp)
        for a in range(n):
            for j, (px, py) in enumerate(chips):
                got = piece(a, 2 * px + py, c)
                pltpu.make_async_remote_copy(
                    src_ref=got, dst_ref=got, send_sem=ici_send.at[3 * a + j], recv_sem=ici_recv.at[3 * a + j],
                    device_id=(px, py, c), device_id_type=MESH).wait_recv()
                fwd = pltpu.make_async_remote_copy(
                    src_ref=got, dst_ref=got, send_sem=d2d_send.at[3 * a + j], recv_sem=d2d_recv.at[3 * a + j],
                    device_id=(x, y, 1 - c), device_id_type=MESH)
                fwd.start()
                sends.append(fwd)
        for a in range(n):
            for j, (px, py) in enumerate(chips):
                other = piece(a, 2 * px + py, 1 - c)
                pltpu.make_async_remote_copy(
                    src_ref=other, dst_ref=other, send_sem=d2d_send.at[3 * a + j], recv_sem=d2d_recv.at[3 * a + j],
                    device_id=(x, y, 1 - c), device_id_type=MESH).wait_recv()
        for j, (px, py) in enumerate(chips):
            pltpu.make_async_remote_copy(
                src_ref=tap_in, dst_ref=tap_out.at[2 * px + py], send_sem=ici_send.at[3 * n + j],
                recv_sem=ici_recv.at[3 * n + j], device_id=(px, py, c), device_id_type=MESH).wait_recv()
        for a in range(n):
            pltpu.make_async_remote_copy(
                src_ref=ins[a], dst_ref=slab(a, mine), send_sem=own_send.at[a], recv_sem=own_recv.at[a],
                device_id=sibling, device_id_type=MESH).wait_recv()
        pltpu.make_async_remote_copy(
            src_ref=tap_in, dst_ref=tap_out.at[mine], send_sem=own_send.at[n], recv_sem=own_recv.at[n],
            device_id=sibling, device_id_type=MESH).wait_recv()
        for cp in sends:
            cp.wait_send()

    def full_shape(a):
        r, cs = shards[a].shape
        return (r, 4 * cs) if col_kind[a] else (4 * r, cs)

    res = pl.pallas_call(
        body, name="gather_weights", in_specs=[ANY] * (n + 1), out_specs=[ANY] * (n + 1),
        out_shape=[jax.ShapeDtypeStruct(full_shape(a), shards[a].dtype) for a in range(n)]
        + [jax.ShapeDtypeStruct((4,) + taps.shape, taps.dtype)],
        scratch_shapes=[pltpu.SemaphoreType.DMA((3 * n + 3,)), pltpu.SemaphoreType.DMA((3 * n + 3,)),
                        pltpu.SemaphoreType.DMA((3 * n,)), pltpu.SemaphoreType.DMA((3 * n,)),
                        pltpu.SemaphoreType.DMA((n + 1,)), pltpu.SemaphoreType.DMA((n + 1,))],
        compiler_params=pltpu.CompilerParams(has_side_effects=True),
    )(*shards, taps)
    return res[:n], res[n]


HBM = pl.BlockSpec(memory_space=pltpu.HBM)
SEM = pl.BlockSpec(memory_space=pltpu.SEMAPHORE)
DATAFLOW = pltpu.SideEffectType.DATAFLOW_SIDE_EFFECTING


def _in_hbm(a):
    return pltpu.with_memory_space_constraint(a, pltpu.HBM)


def _split_start(name, srcs, lands, n_copies, plan):
    n = len(srcs)

    def body(*refs):
        src_refs, land_refs = refs[:n], refs[n:2 * n]
        send_sems, recv_sems = refs[2 * n], refs[2 * n + 1]
        for i, (src, dst, dev) in enumerate(plan(src_refs, land_refs)):
            pltpu.make_async_remote_copy(src_ref=src, dst_ref=dst, send_sem=send_sems.at[i], recv_sem=recv_sems.at[i],
                                         device_id=dev, device_id_type=MESH).start()
        refs[-1][...] = jnp.zeros((8, 128), F32)

    res = pl.pallas_call(
        body, name=name, in_specs=[HBM] * (2 * n),
        out_specs=[SEM, SEM] + [HBM] * (2 * n) + [pl.BlockSpec(memory_space=pltpu.VMEM)],
        out_shape=[pltpu.SemaphoreType.DMA((n_copies,)), pltpu.SemaphoreType.DMA((n_copies,))]
        + [pltpu.HBM(a.shape, a.dtype) for a in list(srcs) + list(lands)] + [jax.ShapeDtypeStruct((8, 128), F32)],
        input_output_aliases={i: 2 + i for i in range(2 * n)},
        compiler_params=pltpu.CompilerParams(has_side_effects=DATAFLOW),
    )(*[_in_hbm(a) for a in list(srcs) + list(lands)])
    return res[0], res[1], list(res[2:2 + n]), list(res[2 + n:2 + 2 * n]), res[-1]


def _split_wait(name, send_sems, recv_sems, srcs, lands, after, plan):
    n = len(srcs)

    def body(*refs):
        src_refs, land_refs = refs[:n], refs[n:2 * n]
        send_ref, recv_ref = refs[2 * n], refs[2 * n + 1]
        for i, (src, dst, dev) in enumerate(plan(src_refs, land_refs)):
            cp = pltpu.make_async_remote_copy(src_ref=src, dst_ref=dst, send_sem=send_ref.at[i], recv_sem=recv_ref.at[i],
                                              device_id=dev, device_id_type=MESH)
            cp.wait_send()
            cp.wait_recv()

    res = pl.pallas_call(
        body, name=name, in_specs=[HBM] * (2 * n) + [SEM, SEM, ANY], out_specs=[HBM] * (2 * n),
        out_shape=[pltpu.HBM(a.shape, a.dtype) for a in list(srcs) + list(lands)],
        input_output_aliases={i: i for i in range(2 * n)},
        compiler_params=pltpu.CompilerParams(has_side_effects=DATAFLOW),
    )(*srcs, *lands, send_sems, recv_sems, after)
    return list(res[:n]), list(res[n:])


def _late_gather_plan(col_kind):
    def plan(src_refs, land_refs):
        x, y, c = _place()
        mine = 2 * x + y
        copies = []
        for a, (src, land) in enumerate(zip(src_refs, land_refs)):
            r, cs = src.shape
            if col_kind[a]:
                dst = land.at[:, pl.ds(pl.multiple_of(mine * cs, 128), cs)]
            else:
                dst = land.at[pl.ds(pl.multiple_of(mine * r, 16), r), :]
            copies.append((src, dst, (x, y, 1 - c)))
            copies += [(src, dst, (px, py, c)) for (px, py) in _other_chips(x, y)]
        return copies
    return plan


def _late_reduce_plan(col_kind):
    def plan(src_refs, land_refs):
        x, y, c = _place()
        copies = []
        for a, (src, land) in enumerate(zip(src_refs, land_refs)):
            for j, (px, py) in enumerate(_other_chips(x, y)):
                if col_kind[a]:
                    cs = land.shape[2]
                    piece = src.at[:, pl.ds(pl.multiple_of((2 * px + py) * cs, 128), cs)]
                else:
                    piece = src.at[2 * px + py]
                copies.append((piece, land.at[j], (px, py, c)))
        return copies
    return plan


def _pair_swap(name, halves):
    n = len(halves)

    def body(*refs):
        ins, outs = refs[:n], refs[n:2 * n]
        send_sems, recv_sems = refs[2 * n:]
        x, y, c = _place()
        copies = []
        for a in range(n):
            cp = pltpu.make_async_remote_copy(
                src_ref=ins[a], dst_ref=outs[a], send_sem=send_sems.at[a], recv_sem=recv_sems.at[a],
                device_id=(x, y, 1 - c), device_id_type=MESH)
            cp.start()
            copies.append(cp)
        for cp in copies:
            cp.wait()

    return pl.pallas_call(
        body, name=name, in_specs=[ANY] * n, out_specs=[ANY] * n,
        out_shape=[jax.ShapeDtypeStruct(s.shape, s.dtype) for s in halves],
        scratch_shapes=[pltpu.SemaphoreType.DMA((n,)), pltpu.SemaphoreType.DMA((n,))],
        compiler_params=pltpu.CompilerParams(has_side_effects=True),
    )(*halves)


def _chip_sum(name, chip_sel, own, col, others):
    _, r, c = others.shape
    tr = _pick(r, (256, 128, 64, 32, 16))
    if col:
        own_spec = pl.BlockSpec((tr, c), lambda i, s: (i, s[0]))
    else:
        own_spec = pl.BlockSpec((None, tr, c), lambda i, s: (s[0], i, 0))
    specs = [own_spec] + [pl.BlockSpec((None, tr, c), lambda i, s, k=k: (k, i, 0)) for k in range(3)]

    def body(s_ref, own_ref, r0, r1, r2, o_ref):
        o_ref[...] = ((own_ref[...].astype(F32) + r0[...].astype(F32)) + r1[...].astype(F32)) + r2[...].astype(F32)

    return pl.pallas_call(
        body, name=name,
        grid_spec=pltpu.PrefetchScalarGridSpec(
            num_scalar_prefetch=1, grid=(r // tr,), in_specs=specs,
            out_specs=pl.BlockSpec((tr, c), lambda i, s: (i, 0))),
        out_shape=jax.ShapeDtypeStruct((r, c), F32),
        compiler_params=_params(("parallel",)),
    )(chip_sel, own, others, others, others)


def _allreduce_small(vals):
    sizes = [int(math.prod(v.shape)) for v in vals]
    padded = [-(-s // 128) * 128 for s in sizes]
    total = -(-sum(padded) // 1024) * 1024
    flat = [jnp.pad(v.reshape(-1), (0, p - s)) for v, s, p in zip(vals, sizes, padded)]
    flat.append(jnp.zeros((total - sum(padded),), F32))
    packed = jnp.concatenate(flat).reshape(total // 128, 128)

    def body(in_ref, out_ref, r0, r1, r2, send_sems, recv_sems):
        x, y, c = _place()
        out_ref[...] = in_ref[...]
        for k, (peer, land) in enumerate(zip([(x, y, 1 - c), (1 - x, y, c), (x, 1 - y, c)], (r0, r1, r2))):
            cp = pltpu.make_async_remote_copy(
                src_ref=out_ref, dst_ref=land, send_sem=send_sems.at[k], recv_sem=recv_sems.at[k],
                device_id=peer, device_id_type=MESH)
            cp.start()
            cp.wait()
            out_ref[...] = out_ref[...] + land[...]

    vm = pl.BlockSpec(memory_space=pltpu.VMEM)
    summed = pl.pallas_call(
        body, name="allreduce_small", in_specs=[vm], out_specs=vm,
        out_shape=jax.ShapeDtypeStruct(packed.shape, F32),
        scratch_shapes=[pltpu.VMEM(packed.shape, F32)] * 3
        + [pltpu.SemaphoreType.DMA((3,)), pltpu.SemaphoreType.DMA((3,))],
        compiler_params=pltpu.CompilerParams(has_side_effects=True, vmem_limit_bytes=VMEM_LIMIT_BYTES),
    )(packed).reshape(-1)
    outs, off = [], 0
    for v, s, p in zip(vals, sizes, padded):
        outs.append(summed[off:off + s].reshape(v.shape))
        off += p
    return outs


def _adamw_math(w, g, m, v):
    m2 = ADAM_B1 * m + (1.0 - ADAM_B1) * g
    v2 = ADAM_B2 * v + (1.0 - ADAM_B2) * (g * g)
    m_hat = m2 / (1.0 - ADAM_B1 ** ADAM_STEP)
    v_hat = v2 / (1.0 - ADAM_B2 ** ADAM_STEP)
    delta = -ADAM_LR * (m_hat / (jnp.sqrt(v_hat) + ADAM_EPS) + ADAM_WD * w)
    return delta, m2, v2


def _adamw_big(name, w, g_mine, g_sibling, m, v):
    _, r, c = w.shape

    def body(w_ref, ga_ref, gb_ref, m_ref, v_ref, go_ref, d_ref, mo_ref, vo_ref):
        gv = ga_ref[...] + gb_ref[...]
        d, m2, v2 = _adamw_math(w_ref[...], gv, m_ref[...], v_ref[...])
        go_ref[...] = gv
        d_ref[...] = d
        mo_ref[...] = m2
        vo_ref[...] = v2

    tr = _pick(r, (256, 128, 64, 32, 16, 8))
    if r % tr == 0 and tr % 8 == 0:
        grid = (r // tr,)
        blk = pl.BlockSpec((None, tr, c), lambda i: (0, i, 0))
        part = pl.BlockSpec((tr, c), lambda i: (i, 0))
    else:
        grid = (c // 512,)
        blk = pl.BlockSpec((None, r, 512), lambda i: (0, 0, i))
        part = pl.BlockSpec((r, 512), lambda i: (0, i))
    return pl.pallas_call(
        body, name=name, grid=grid, in_specs=[blk, part, part, blk, blk], out_specs=[blk] * 4,
        out_shape=[jax.ShapeDtypeStruct((1, r, c), F32)] * 4, compiler_params=_params(("parallel",)),
    )(w, g_mine, g_sibling, m, v)


def _adamw_small(ws, gs, ms, vs):
    n = len(ws)

    def body(*refs):
        w_r, g_r, m_r, v_r = refs[:n], refs[n:2 * n], refs[2 * n:3 * n], refs[3 * n:4 * n]
        o = refs[4 * n:]
        for a in range(n):
            gv = g_r[a][...]
            d, m2, v2 = _adamw_math(w_r[a][...], gv, m_r[a][...], v_r[a][...])
            o[a][...] = gv
            o[n + a][...] = d
            o[2 * n + a][...] = m2
            o[3 * n + a][...] = v2

    res = pl.pallas_call(
        body, name="adamw_small", out_shape=[jax.ShapeDtypeStruct(w.shape, F32) for _ in range(4) for w in ws],
        compiler_params=_params(),
    )(*ws, *gs, *ms, *vs)
    return res[:n], res[n:2 * n], res[2 * n:3 * n], res[3 * n:]


def _full_from_gathered(name, gathered):
    if name == "w_in":
        rows = gathered.shape[0] // 4
        return gathered.reshape(4, rows, gathered.shape[1]).transpose(1, 0, 2).reshape(rows, 4 * gathered.shape[1])
    return gathered


def _reduce_layout(name, full):
    if name in COL_KIND:
        return full
    if name == "w_in":
        rows, cols = full.shape
        return full.reshape(rows, 4, cols // 4).transpose(1, 0, 2)
    return full.reshape(4, full.shape[0] // 4, full.shape[1])


def kernel(x, mem, norm_mix, w_in, fox_q_norm, fox_k_norm, fox_f_bias, s5_a_re, s5_a_im, s5_log_dt, s5_b_re, s5_b_im, s5_c_re, s5_c_im, s5_d, s5_w_glu, s5_b_glu, out_norm_fox, out_norm_s5, w_out, norm_cross, norm_mem, w_xq, w_xkv, xq_norm, xk_norm, w_xo, norm_ffn, w_ffn_up, ffn_conv_w, ffn_conv_b, w_ffn_down, loss_target, m_norm_mix, m_w_in, m_fox_q_norm, m_fox_k_norm, m_fox_f_bias, m_s5_a_re, m_s5_a_im, m_s5_log_dt, m_s5_b_re, m_s5_b_im, m_s5_c_re, m_s5_c_im, m_s5_d, m_s5_w_glu, m_s5_b_glu, m_out_norm_fox, m_out_norm_s5, m_w_out, m_norm_cross, m_norm_mem, m_w_xq, m_w_xkv, m_xq_norm, m_xk_norm, m_w_xo, m_norm_ffn, m_w_ffn_up, m_ffn_conv_w, m_ffn_conv_b, m_w_ffn_down, v_norm_mix, v_w_in, v_fox_q_norm, v_fox_k_norm, v_fox_f_bias, v_s5_a_re, v_s5_a_im, v_s5_log_dt, v_s5_b_re, v_s5_b_im, v_s5_c_re, v_s5_c_im, v_s5_d, v_s5_w_glu, v_s5_b_glu, v_out_norm_fox, v_out_norm_s5, v_w_out, v_norm_cross, v_norm_mem, v_w_xq, v_w_xkv, v_xq_norm, v_xk_norm, v_w_xo, v_norm_ffn, v_w_ffn_up, v_ffn_conv_w, v_ffn_conv_b, v_w_ffn_down):
    given = dict(locals())
    w = {n: given[n] for n in WEIGHTS}
    m = {n: given["m_" + n] for n in WEIGHTS}
    v = {n: given["v_" + n] for n in WEIGHTS}
    xi, yi, _ = _place()
    chip = (2 * xi + yi).astype(jnp.int32)
    chip_sel = chip.reshape(1)
    early_kind = [n in COL_KIND for n in EARLY_WEIGHTS]
    late_kind = [n in COL_KIND for n in LATE_WEIGHTS]

    gathered, taps = _gather_weights([w[FIRST_WEIGHT][0].astype(BF16)], [False], w["ffn_conv_w"][0])
    first_full = gathered[0]
    conv_w = taps.transpose(1, 0, 2).reshape(3, D_FF)
    pending = {}
    g_started = None
    for stage, names in (("mid", MID_WEIGHTS), ("late", LATE_WEIGHTS)):
        kinds = [n in COL_KIND for n in names]
        shards = [w[n][0].astype(BF16) for n in names]
        if g_started is None:
            first_full, shards[0] = lax.optimization_barrier((first_full, shards[0]))
        else:
            shards[0] = shards[0] + g_started[0:1, 0:1].astype(BF16)
        full = [lax.empty((s.shape[0], 4 * s.shape[1]) if ck else (4 * s.shape[0], s.shape[1]), BF16)
                for s, ck in zip(shards, kinds)]
        plan = _late_gather_plan(kinds)
        send, recv, srcs, lands, g_started = _split_start(
            "gather_" + stage + "_start", shards, full, 4 * len(names), plan)
        pending[stage] = (names, plan, send, recv, srcs, lands)
    wb = {FIRST_WEIGHT: _full_from_gathered(FIRST_WEIGHT, first_full)}

    def late_weights(stage, after):
        names, plan, send, recv, srcs, lands = pending[stage]
        _, full = _split_wait("gather_" + stage + "_wait", send, recv, srcs, lands, after, plan)
        return dict(zip(names, full))

    reduce_plan = _late_reduce_plan(late_kind)
    late_reduce = {}

    def early_grads(late_g):
        grads = [_reduce_layout(n, late_g[n]) for n in LATE_WEIGHTS]
        lands = [lax.empty((3, s.shape[0], s.shape[1] // 4) if ck else (3,) + s.shape[1:], BF16)
                 for s, ck in zip(grads, late_kind)]
        late_reduce["sems"] = _split_start("reduce_late_start", grads, lands, 3 * len(LATE_WEIGHTS), reduce_plan)
        return late_reduce["sems"][4][0:1, 0:1]

    p = {n: w[n][0] for n in SMALL}
    p["ffn_conv_w"] = conv_w
    for n in ("norm_mix", "fox_q_norm", "fox_k_norm", "fox_f_bias", "s5_b_glu", "out_norm_fox", "out_norm_s5",
              "norm_cross", "norm_mem", "xq_norm", "xk_norm", "norm_ffn", "ffn_conv_b"):
        p[n] = p[n].reshape(1, -1)
    p["norm_mix"] = p["norm_mix"] + g_started[0:1, 0:1]
    loss, grad_x, g = _local_step(x, mem, loss_target, p, wb, late_weights, early_grads)

    grads = [_reduce_layout(n, g[n].astype(BF16)) for n in EARLY_WEIGHTS]
    early_lands = [lax.empty((3, s.shape[0], s.shape[1] // 4) if ck else (3,) + s.shape[1:], BF16)
                   for s, ck in zip(grads, early_kind)]
    early_plan = _late_reduce_plan(early_kind)
    e_send, e_recv, e_srcs, e_lands, e_started = _split_start(
        "reduce_early_start", grads, early_lands, 3 * len(EARLY_WEIGHTS), early_plan)

    out_g, out_d, out_m, out_v = {}, {}, {}, {}

    def finish(names, kinds, sums, from_chips, tag):
        mine = [_chip_sum("reduce_chip_sum_" + n, chip_sel, ps, ck, fc)
                for n, ps, fc, ck in zip(names, sums, from_chips, kinds)]
        theirs = _pair_swap("reduce_pair_swap_" + tag, mine)
        for n, a, b in zip(names, mine, theirs):
            if n == "w_in":
                flip = lambda t: jnp.swapaxes(t, -1, -2)
                res = _adamw_big("adamw_" + n, flip(w[n]), flip(a), flip(b), flip(m[n]), flip(v[n]))
                out_g[n], out_d[n], out_m[n], out_v[n] = (flip(t) for t in res)
                continue
            out_g[n], out_d[n], out_m[n], out_v[n] = _adamw_big("adamw_" + n, w[n], a, b, m[n], v[n])

    r_send, r_recv, r_srcs, r_lands, _ = late_reduce["sems"]
    late_sums, late_from_chips = _split_wait("reduce_late_wait", r_send, r_recv, r_srcs, r_lands, e_started,
                                             reduce_plan)
    finish(LATE_WEIGHTS, late_kind, late_sums, late_from_chips, "late")

    small_names = list(SMALL) + ["ffn_conv_w"]
    small_vals = [g[n].reshape(w[n].shape if n != "ffn_conv_w" else (1, 3, D_FF)) for n in small_names]
    last = LATE_WEIGHTS[-1]
    loss, out_v[last] = lax.optimization_barrier((loss, out_v[last]))
    reduced = _allreduce_small(small_vals + [loss])
    loss_all = reduced[-1].reshape(())
    conv_w_grad = lax.dynamic_slice_in_dim(reduced[-2], chip * (D_FF // 4), D_FF // 4, axis=2)
    sg, sd, sm, sv = _adamw_small(
        [w[n] for n in small_names], list(reduced[:len(SMALL)]) + [conv_w_grad],
        [m[n] for n in small_names], [v[n] for n in small_names])
    out_g.update(zip(small_names, sg))
    out_d.update(zip(small_names, sd))
    out_m.update(zip(small_names, sm))
    out_v.update(zip(small_names, sv))

    early_sums, early_from_chips = _split_wait("reduce_early_wait", e_send, e_recv, e_srcs, e_lands, reduced[0],
                                               early_plan)
    finish(EARLY_WEIGHTS, early_kind, early_sums, early_from_chips, "early")

    return (loss_all, grad_x, *[out_g[n] for n in WEIGHTS], *[out_d[n] for n in WEIGHTS],
            *[out_m[n] for n in WEIGHTS], *[out_v[n] for n in WEIGHTS])
```

```python
import math

import jax
import jax.numpy as jnp
from jax import lax
from jax.experimental import pallas as pl
from jax.experimental.pallas import tpu as pltpu

F32 = jnp.float32
BF16 = jnp.bfloat16

D_MODEL = 1024
FOX_WIDTH = 512
HEAD_DIM = 64
N_FOX_HEADS = 8
S5_WIDTH = 512
S5_GROUP_CH = 16
S5_GROUPS = 32
S5_STATE = 64
S5_CH = S5_GROUPS * S5_STATE
N_X_HEADS = 4
X_HEAD_DIM = 256
N_MEM = 256
D_FF = 2816
UF_COLS = 640
EPS = 1e-6
ADAM_LR = 0.001
ADAM_B1 = 0.9
ADAM_B2 = 0.999
ADAM_EPS = 1e-08
ADAM_WD = 0.01
ADAM_STEP = 10

VMEM_LIMIT_BYTES = 56 * 1024 * 1024
MM_BLOCK_BYTES = 6 * 1024 * 1024
MM_VMEM_BYTES = 40 * 1024 * 1024
MM_TILE_MAX = 1536
MESH = pl.DeviceIdType.MESH

FIRST_WEIGHT = "w_in"
MID_WEIGHTS = ("s5_w_glu", "w_out")
EARLY_WEIGHTS = (FIRST_WEIGHT,) + MID_WEIGHTS
LATE_WEIGHTS = ("w_xq", "w_xkv", "w_xo", "w_ffn_up", "w_ffn_down")
BIG = EARLY_WEIGHTS + LATE_WEIGHTS
COL_KIND = ("w_xkv", "w_ffn_up")
SMALL = ("norm_mix", "fox_q_norm", "fox_k_norm", "fox_f_bias", "s5_a_re", "s5_a_im", "s5_log_dt",
         "s5_b_re", "s5_b_im", "s5_c_re", "s5_c_im", "s5_d", "s5_b_glu", "out_norm_fox", "out_norm_s5",
         "norm_cross", "norm_mem", "xq_norm", "xk_norm", "norm_ffn", "ffn_conv_b")
WEIGHTS = ("norm_mix", "w_in", "fox_q_norm", "fox_k_norm", "fox_f_bias", "s5_a_re", "s5_a_im", "s5_log_dt",
           "s5_b_re", "s5_b_im", "s5_c_re", "s5_c_im", "s5_d", "s5_w_glu", "s5_b_glu", "out_norm_fox",
           "out_norm_s5", "w_out", "norm_cross", "norm_mem", "w_xq", "w_xkv", "xq_norm", "xk_norm", "w_xo",
           "norm_ffn", "w_ffn_up", "ffn_conv_w", "ffn_conv_b", "w_ffn_down")


def _params(sem=None):
    return pltpu.CompilerParams(dimension_semantics=sem, vmem_limit_bytes=VMEM_LIMIT_BYTES)


def _pick(n, cands):
    for c in cands:
        if n % c == 0:
            return c
    return n


_DIMS = {"nn": (((1,), (0,)), ((), ())), "nt": (((1,), (1,)), ((), ())), "tn": (((0,), (0,)), ((), ()))}


def _mm(a, b, mode, name, out_dtype=F32, res=None):
    if mode == "nn":
        (m, k), (k2, n) = a.shape, b.shape
    elif mode == "nt":
        (m, k), (n, k2) = a.shape, b.shape
    else:
        (k, m), (k2, n) = a.shape, b.shape
    assert k == k2, (name, a.shape, b.shape)

    has_res = res is not None
    a_size, b_size = a.dtype.itemsize, b.dtype.itemsize
    o_size = jnp.dtype(out_dtype).itemsize + (res.dtype.itemsize if has_res else 0)

    def tiles(dim):
        return [c for c in range(MM_TILE_MAX, 0, -128) if dim % c == 0] or [dim]

    best = None
    for tm in tiles(m):
        for tn in tiles(n):
            a_blk, b_blk = tm * k * a_size, tn * k * b_size
            if max(a_blk, b_blk) > MM_BLOCK_BYTES or 2 * (a_blk + b_blk + tm * tn * o_size) > MM_VMEM_BYTES:
                continue
            for rows_outer in (True, False):
                moved = (m * k * a_size + (m // tm) * n * k * b_size) if rows_outer else \
                        (n * k * b_size + (n // tn) * m * k * a_size)
                key = (moved, -(tm * tn))
                if best is None or key < best[0]:
                    best = (key, tm, tn, rows_outer)
    assert best is not None, (name, a.shape, b.shape)
    _, tm, tn, rows_outer = best
    ij = (lambda g0, g1: (g0, g1)) if rows_outer else (lambda g0, g1: (g1, g0))
    if mode == "tn":
        a_spec = pl.BlockSpec((k, tm), lambda g0, g1: (0, ij(g0, g1)[0]))
    else:
        a_spec = pl.BlockSpec((tm, k), lambda g0, g1: (ij(g0, g1)[0], 0))
    if mode == "nt":
        b_spec = pl.BlockSpec((tn, k), lambda g0, g1: (ij(g0, g1)[1], 0))
    else:
        b_spec = pl.BlockSpec((k, tn), lambda g0, g1: (0, ij(g0, g1)[1]))
    o_spec = pl.BlockSpec((tm, tn), lambda g0, g1: ij(g0, g1))
    grid = (m // tm, n // tn) if rows_outer else (n // tn, m // tm)
    dims = _DIMS[mode]

    def body(*refs):
        a_ref, b_ref = refs[0], refs[1]
        o_ref = refs[-1]
        acc = lax.dot_general(a_ref[...].astype(BF16), b_ref[...].astype(BF16), dims, preferred_element_type=F32)
        if has_res:
            acc = acc + refs[2][...].astype(F32)
        o_ref[...] = acc.astype(o_ref.dtype)

    return pl.pallas_call(
        body, name=name, grid=grid,
        in_specs=[a_spec, b_spec] + ([o_spec] if has_res else []),
        out_specs=o_spec, out_shape=jax.ShapeDtypeStruct((m, n), out_dtype),
        compiler_params=_params(("parallel", "parallel")),
    )(*((a, b, res) if has_res else (a, b)))


def _row_spec(tm, bc, off, step):
    return pl.BlockSpec((tm, bc), lambda i, h: (i, off + step * h))


ROW_TILE_ELEMS = 512 * 1024


def _row_tile(t, rows):
    widest = max(bc for (_, bc, _, _) in rows)
    return _pick(t, (min(t, ROW_TILE_ELEMS // widest), 512, 256, 128, 64, 8))


def _rowwise(fn, rows, pars, outs, name, heads=1):
    t = rows[0][0].shape[0]
    tm = _row_tile(t, rows)
    nr, npar = len(rows), len(pars)

    def body(*refs):
        vals = [r[...].astype(F32) for r in refs[:nr + npar]]
        res = fn(*vals)
        if not isinstance(res, (tuple, list)):
            res = (res,)
        for o_ref, v in zip(refs[nr + npar:], res):
            o_ref[...] = v.astype(o_ref.dtype)

    in_specs = [_row_spec(tm, bc, off, st) for (_, bc, off, st) in rows]
    in_specs += [pl.BlockSpec(p.shape, lambda i, h: (0, 0)) for p in pars]
    out_specs = [_row_spec(tm, bc, 0, st) for (_, bc, st, _) in outs]
    out_shape = [jax.ShapeDtypeStruct((t, c), dt) for (c, _, _, dt) in outs]
    res = pl.pallas_call(
        body, name=name, grid=(t // tm, heads), in_specs=in_specs, out_specs=out_specs, out_shape=out_shape,
        compiler_params=_params(("parallel", "parallel")),
    )(*[r[0] for r in rows], *pars)
    return res[0] if len(res) == 1 else res


def _rowwise_vjp(fn, rows, pars, cts, name, heads=1, adds=None, row_dtypes=None):
    t = rows[0][0].shape[0]
    tm = _row_tile(t, rows)
    nr, npar, nct = len(rows), len(pars), len(cts)
    adds = adds or [None] * nr
    add_list = [a for a in adds if a is not None]
    row_dtypes = row_dtypes or [F32] * nr

    def body(*refs):
        i, h = pl.program_id(0), pl.program_id(1)
        p = 0
        row_v = [r[...].astype(F32) for r in refs[p:p + nr]]; p += nr
        par_v = [r[...].astype(F32) for r in refs[p:p + npar]]; p += npar
        ct_v = [r[...].astype(F32) for r in refs[p:p + nct]]; p += nct
        add_refs = refs[p:p + len(add_list)]; p += len(add_list)
        drow_refs = refs[p:p + nr]; p += nr
        dpar_refs = refs[p:p + npar]

        def wrapped(*a):
            r = fn(*a)
            return tuple(r) if isinstance(r, (tuple, list)) else (r,)

        _, pull = jax.vjp(wrapped, *row_v, *par_v)
        grads = pull(tuple(ct_v))
        ai = 0
        for k in range(nr):
            g = grads[k]
            if adds[k] is not None:
                g = g + add_refs[ai][...].astype(F32)
                ai += 1
            drow_refs[k][...] = g.astype(drow_refs[k].dtype)

        @pl.when((i == 0) & (h == 0))
        def _():
            for r in dpar_refs:
                r[...] = jnp.zeros(r.shape, r.dtype)

        for k in range(npar):
            dpar_refs[k][...] += grads[nr + k]

    in_specs = [_row_spec(tm, bc, off, st) for (_, bc, off, st) in rows]
    in_specs += [pl.BlockSpec(q.shape, lambda i, h: (0, 0)) for q in pars]
    in_specs += [_row_spec(tm, bc, off, st) for (_, bc, off, st) in cts]
    in_specs += [_row_spec(tm, bc, off, st) for (_, bc, off, st) in add_list]
    out_specs = [_row_spec(tm, bc, 0, st) for (_, bc, _, st) in rows]
    out_specs += [pl.BlockSpec(q.shape, lambda i, h: (0, 0)) for q in pars]
    out_shape = [jax.ShapeDtypeStruct((t, bc * (heads if st else 1)), dt) for (_, bc, _, st), dt in zip(rows, row_dtypes)]
    out_shape += [jax.ShapeDtypeStruct(q.shape, F32) for q in pars]
    res = pl.pallas_call(
        body, name=name, grid=(t // tm, heads), in_specs=in_specs, out_specs=out_specs, out_shape=out_shape,
        compiler_params=_params(("arbitrary", "arbitrary")),
    )(*[r[0] for r in rows], *pars, *[c[0] for c in cts], *[a[0] for a in add_list])
    return list(res[:nr]), list(res[nr:])


def _rms(x, g):
    return x * lax.rsqrt(jnp.mean(x * x, axis=-1, keepdims=True) + EPS) * g


def _rms_pair(x, g):
    left = lax.broadcasted_iota(jnp.int32, x.shape, 1) < HEAD_DIM
    x2 = x * x
    ms_a = jnp.sum(jnp.where(left, x2, 0.0), axis=-1, keepdims=True) * (1.0 / HEAD_DIM)
    ms_b = jnp.sum(jnp.where(left, 0.0, x2), axis=-1, keepdims=True) * (1.0 / HEAD_DIM)
    return x * lax.rsqrt(jnp.where(left, ms_a, ms_b) + EPS) * g


def _gelu(x):
    return 0.5 * x * (1.0 + jnp.tanh(math.sqrt(2.0 / math.pi) * (x + 0.044715 * (x * x * x))))


def _s5_act(ys, u, d):
    return _gelu(ys + d * u)


def _s5_gate(yg, z, b, g):
    return _rms(yg * jax.nn.sigmoid(z + b), g)


def _lane_cumsum(x, reverse):
    n = x.shape[-1]
    lane = lax.broadcasted_iota(jnp.int32, x.shape, 1)
    k = 1
    while k < n:
        if reverse:
            x = x + jnp.where(lane < n - k, pltpu.roll(x, n - k, 1), 0.0)
        else:
            x = x + jnp.where(lane >= k, pltpu.roll(x, k, 1), 0.0)
        k *= 2
    return x


def _log_sigmoid(z):
    return jnp.minimum(z, 0.0) - jnp.log(1.0 + jnp.exp(-jnp.abs(z)))


def _forget_fwd(f, bias):
    def body(f_ref, b_ref, c_ref):
        c_ref[...] = _lane_cumsum(_log_sigmoid(f_ref[...] + b_ref[...]), False)

    return pl.pallas_call(body, name="forget_fwd", out_shape=jax.ShapeDtypeStruct(f.shape, F32),
                          compiler_params=_params())(f, bias)


def _forget_bwd(f, bias, dc):
    def body(f_ref, b_ref, dc_ref, df_ref, db_ref):
        dlog = _lane_cumsum(dc_ref[...], True)
        df = dlog * jax.nn.sigmoid(-(f_ref[...] + b_ref[...]))
        df_ref[...] = df
        db_ref[...] = jnp.sum(df, axis=1, keepdims=True)

    return pl.pallas_call(body, name="forget_bwd",
                          out_shape=(jax.ShapeDtypeStruct(f.shape, F32), jax.ShapeDtypeStruct(bias.shape, F32)),
                          compiler_params=_params())(f, bias, dc)


FOX_BLOCK = 512
FOX_KEYS = 512
FOX_BWD_BLOCK = 512
_NT = _DIMS["nt"]
_TN = _DIMS["tn"]


N_PAIRS = N_FOX_HEADS // 2
V_BLOCK0 = 2 * N_PAIRS


def _left_lanes(shape):
    return lax.broadcasted_iota(jnp.int32, shape, 1) < HEAD_DIM


def _top_rows(shape):
    return lax.broadcasted_iota(jnp.int32, shape, 0) < HEAD_DIM


def _wide(c_tile, n):
    return c_tile if n == 128 else jnp.concatenate([c_tile] * (n // 128), axis=1)


def _fox_fwd(qn, kn, qkv, c_wide, seqs):
    t = qn.shape[0]
    l = t // seqs
    tb = min(FOX_BLOCK, l)
    tk = min(FOX_KEYS, tb)
    ratio = tb // tk
    nb = l // tb
    scale = HEAD_DIM ** -0.5

    def body(q_ref, k_ref, v_ref, ca_ref, cb_ref, o_ref, lse_ref, vt_ref):
        i = pl.program_id(2)
        top = _top_rows((128, tb))

        @pl.when(i == 0)
        def _():
            vt_ref[...] = v_ref[...].T.astype(BF16)

        qt = (q_ref[...].astype(F32) * scale).T.astype(BF16)
        zero = jnp.zeros_like(qt)
        qts = (jnp.where(top, qt, zero), jnp.where(top, zero, qt))
        top_k = _top_rows((128, tk))
        zero_k = jnp.zeros((128, tk), BF16)
        key_pos = lax.broadcasted_iota(jnp.int32, (tk, tb), 0)
        query_pos = lax.broadcasted_iota(jnp.int32, (tk, tb), 1)
        c_refs = (ca_ref, cb_ref)

        def scores(j):
            off = pl.multiple_of(j * tk, tk)
            k2 = k_ref[pl.ds(off, tk), :]
            return tuple(jnp.dot(k2, qts[h], preferred_element_type=F32) - _wide(c_refs[h][pl.ds(off, tk), :], tb)
                         for h in (0, 1))

        def values_times(ps, j):
            vt = vt_ref[:, pl.ds(pl.multiple_of(j * tk, tk), tk)]
            return (jnp.dot(jnp.where(top_k, vt, zero_k), ps[0], preferred_element_type=F32)
                    + jnp.dot(jnp.where(top_k, zero_k, vt), ps[1], preferred_element_type=F32))

        def softmax_step(sts, stats, first_key):
            ps, new, alphas = [], [], []
            for st, (m, s_sum) in zip(sts, stats):
                if first_key is not None:
                    st = jnp.where(key_pos + first_key <= query_pos, st, -jnp.inf)
                m_new = jnp.maximum(m, jnp.max(st, axis=0, keepdims=True))
                alpha = jnp.exp(m - m_new)
                p = jnp.exp(st - m_new)
                new.append((m_new, alpha * s_sum + jnp.sum(p, axis=0, keepdims=True)))
                alphas.append(alpha)
                ps.append(p.astype(BF16))
            return tuple(ps), tuple(new), jnp.where(top, alphas[0], alphas[1])

        def tile(j, carry, first_key):
            stats, acc = carry
            ps, stats, alpha = softmax_step(scores(j), stats, first_key)
            return stats, alpha * acc + values_times(ps, j)

        stat = (jnp.full((1, tb), -jnp.inf, F32), jnp.zeros((1, tb), F32))
        below = i * ratio
        carry = lax.fori_loop(0, below, lambda j, c: tile(j, c, None), ((stat, stat), jnp.zeros((128, tb), F32)))
        for r in range(ratio):
            carry = tile(below + r, carry, r * tk)
        ((ma, sa), (mb, sb)), acc = carry
        o_ref[...] = (acc / jnp.where(top, sa, sb)).T
        lse_ref[0:1, :] = ma + jnp.log(sa)
        lse_ref[1:2, :] = mb + jnp.log(sb)

    qblk = pl.BlockSpec((tb, 128), lambda b, hp, i: (b * nb + i, hp))
    return pl.pallas_call(
        body, name="fox_fwd", grid=(seqs, N_PAIRS, nb),
        in_specs=[qblk, pl.BlockSpec((l, 128), lambda b, hp, i: (b, hp)),
                  pl.BlockSpec((l, 128), lambda b, hp, i: (b, V_BLOCK0 + hp)),
                  pl.BlockSpec((None, l, 128), lambda b, hp, i: (b * N_FOX_HEADS + 2 * hp, 0, 0)),
                  pl.BlockSpec((None, l, 128), lambda b, hp, i: (b * N_FOX_HEADS + 2 * hp + 1, 0, 0))],
        out_specs=[qblk, pl.BlockSpec((None, 2, tb), lambda b, hp, i: (b * N_PAIRS + hp, 0, i))],
        out_shape=[jax.ShapeDtypeStruct((t, FOX_WIDTH), F32), jax.ShapeDtypeStruct((seqs * N_PAIRS, 2, l), F32)],
        scratch_shapes=[pltpu.VMEM((128, l), BF16)],
        compiler_params=_params(("parallel", "parallel", "arbitrary")),
    )(qn, kn, qkv, c_wide, c_wide)


def _fox_bwd(qn, kn, qkv, c_wide, o, do, lse, seqs):
    t = qn.shape[0]
    l = t // seqs
    tb = min(FOX_BWD_BLOCK, l)
    nb = l // tb
    scale = HEAD_DIM ** -0.5
    one_at = (HEAD_DIM, 0)

    def body(q_ref, k_ref, v_ref, ca_ref, cb_ref, o_ref, do_ref, lse_ref, dq_ref, dk_ref, dv_ref, dc_ref, dcq_ref,
             qt_ref, kt_ref, dot_ref, delta_ref, dqa_ref, dqb_ref):
        top_l = _top_rows((128, l))
        top = _top_rows((128, tb))
        left = _left_lanes((tb, 128))
        row_id = lax.broadcasted_iota(jnp.int32, (128, tb), 0)
        lane_id = lax.broadcasted_iota(jnp.int32, (tb, 128), 1)
        zero_t = jnp.zeros((128, tb), BF16)
        zero_l = jnp.zeros((tb, 128), BF16)
        rows = lambda a: (jnp.where(top, a, zero_t), jnp.where(top, zero_t, a))
        lanes = lambda a: (jnp.where(left, a, zero_l), jnp.where(left, zero_l, a))
        with_one_row = lambda pair: tuple(jnp.where(row_id == one_at[h], 1.0, pair[h]).astype(BF16) for h in (0, 1))
        with_one_lane = lambda pair: tuple(jnp.where(lane_id == one_at[h], 1.0, pair[h]).astype(BF16) for h in (0, 1))
        causal = lax.broadcasted_iota(jnp.int32, (tb, tb), 0) <= lax.broadcasted_iota(jnp.int32, (tb, tb), 1)
        c_refs = (ca_ref, cb_ref)
        dq_refs = (dqa_ref, dqb_ref)

        qt_ref[...] = (q_ref[...].astype(F32) * scale).T.astype(BF16)
        kt_ref[...] = k_ref[...].astype(F32).T.astype(BF16)
        do_t = do_ref[...].T
        dot_ref[...] = do_t.astype(BF16)
        prod_t = do_t * o_ref[...].T
        delta_ref[0:1, :] = jnp.sum(jnp.where(top_l, prod_t, 0.0), axis=0, keepdims=True)
        delta_ref[1:2, :] = jnp.sum(jnp.where(top_l, 0.0, prod_t), axis=0, keepdims=True)
        dqa_ref[...] = jnp.zeros(dqa_ref.shape, F32)
        dqb_ref[...] = jnp.zeros(dqb_ref.shape, F32)

        def kv_block(j, _):
            koff = pl.multiple_of(j * tb, tb)
            k2 = k_ref[pl.ds(koff, tb), :]
            v2 = v_ref[pl.ds(koff, tb), :].astype(BF16)
            kts = with_one_row(rows(kt_ref[:, pl.ds(koff, tb)]))
            cw = tuple(_wide(c_refs[h][pl.ds(koff, tb), :], tb) for h in (0, 1))

            def q_block(i, carry, masked):
                dks, dv = list(carry[:2]), carry[2]
                qoff = pl.multiple_of(i * tb, tb)
                qs = lanes((q_ref[pl.ds(qoff, tb), :].astype(F32) * scale).astype(BF16))
                qs_one = with_one_lane(qs)
                dos = lanes(do_ref[pl.ds(qoff, tb), :].astype(BF16))
                qts = rows(qt_ref[:, pl.ds(qoff, tb)])
                dots = rows(dot_ref[:, pl.ds(qoff, tb)])
                for h in (0, 1):
                    st = jnp.dot(k2, qts[h], preferred_element_type=F32) - cw[h]
                    p = jnp.exp(st - lse_ref[h:h + 1, pl.ds(qoff, tb)])
                    if masked:
                        p = jnp.where(causal, p, 0.0)
                    dp = jnp.dot(v2, dots[h], preferred_element_type=F32)
                    dsb = (p * (dp - delta_ref[h:h + 1, pl.ds(qoff, tb)])).astype(BF16)
                    dv = dv + jnp.dot(p.astype(BF16), dos[h], preferred_element_type=F32)
                    dks[h] = dks[h] + jnp.dot(dsb, qs_one[h], preferred_element_type=F32)
                    dq_refs[h][:, pl.ds(qoff, tb)] += jnp.dot(kts[h], dsb, preferred_element_type=F32)
                return dks[0], dks[1], dv

            z = jnp.zeros((tb, 128), F32)
            carry = q_block(j, (z, z, z), True)
            rest = nb - 1 - j
            carry = lax.fori_loop(
                0, rest // 2, lambda n, c: q_block(j + 2 + 2 * n, q_block(j + 1 + 2 * n, c, False), False), carry)
            dka, dkb, dv = lax.cond(rest % 2 == 1, lambda c: q_block(nb - 1, c, False), lambda c: c, carry)
            dk_ref[pl.ds(koff, tb), :] = jnp.where(left, dka, dkb)
            dv_ref[pl.ds(koff, tb), :] = dv
            dc_ref[0:1, pl.ds(koff, tb)] = -dka.T[one_at[0]:one_at[0] + 1, :]
            dc_ref[1:2, pl.ds(koff, tb)] = -dkb.T[one_at[1]:one_at[1] + 1, :]
            return 0

        lax.fori_loop(0, nb, kv_block, 0)
        dq_ref[...] = (jnp.where(top_l, dqa_ref[...], dqb_ref[...]) * scale).T
        dcq_ref[0:1, :] = dqa_ref[one_at[0]:one_at[0] + 1, :]
        dcq_ref[1:2, :] = dqb_ref[one_at[1]:one_at[1] + 1, :]

    blk = pl.BlockSpec((l, 128), lambda b, hp: (b, hp))
    cspec = lambda k: pl.BlockSpec((None, l, 128), lambda b, hp: (b * N_FOX_HEADS + 2 * hp + k, 0, 0))
    rows2 = pl.BlockSpec((None, 2, l), lambda b, hp: (b * N_PAIRS + hp, 0, 0))
    wide = jax.ShapeDtypeStruct((t, FOX_WIDTH), F32)
    pair_rows = jax.ShapeDtypeStruct((seqs * N_PAIRS, 2, l), F32)
    return pl.pallas_call(
        body, name="fox_bwd", grid=(seqs, N_PAIRS),
        in_specs=[blk, blk, pl.BlockSpec((l, 128), lambda b, hp: (b, V_BLOCK0 + hp)), cspec(0), cspec(1), blk, blk, rows2],
        out_specs=[blk, blk, blk, rows2, rows2],
        out_shape=[wide, wide, wide, pair_rows, pair_rows],
        scratch_shapes=[pltpu.VMEM((128, l), BF16), pltpu.VMEM((128, l), BF16), pltpu.VMEM((128, l), BF16),
                        pltpu.VMEM((2, l), F32), pltpu.VMEM((128, l), F32), pltpu.VMEM((128, l), F32)],
        compiler_params=_params(("parallel", "parallel")),
    )(qn, kn, qkv, c_wide, c_wide, o, do, lse)


SCAN_ROWS = 512
SCAN_COLS = 1024


S5_IN = 128
S5_ST = 512
SCAN_CHUNKS = SCAN_COLS // S5_ST
SCAN_SEGS = 8
LANES = 128


def _cmul(ar, ai, br, bi):
    return ar * br - ai * bi, ar * bi + ai * br


def _powers_into(pw_r, pw_i, a_r, a_i, seg):
    pw_r[0:1, :] = a_r
    pw_i[0:1, :] = a_i
    for k in range(1, seg):
        pr, pi = _cmul(pw_r[k - 1:k, :], pw_i[k - 1:k, :], a_r, a_i)
        pw_r[k:k + 1, :] = pr
        pw_i[k:k + 1, :] = pi


def _interleave(dst, src, seg):
    for h in range(src.shape[0]):
        for j in range(seg):
            dst[h, j * SCAN_SEGS:(j + 1) * SCAN_SEGS, :] = src[h, pl.ds(j, SCAN_SEGS, stride=seg), :]


def _deinterleave(dst, src, seg):
    for h in range(src.shape[0]):
        for j in range(seg):
            dst[h, pl.ds(j, SCAN_SEGS, stride=seg), :] = src[h, j * SCAN_SEGS:(j + 1) * SCAN_SEGS, :]


def _interleaved(ref, tmp_a, tmp_b, seg):
    n = ref.shape[1] // LANES
    for h in range(n):
        tmp_a[h] = ref[:, h * LANES:(h + 1) * LANES].astype(F32)
    _interleave(tmp_b, tmp_a, seg)
    return jnp.concatenate([tmp_b[h] for h in range(n)], axis=1)


def _store_deinterleaved(ref, val, tmp_a, tmp_b, seg):
    n = ref.shape[1] // LANES
    for h in range(n):
        tmp_a[h] = val[:, h * LANES:(h + 1) * LANES]
    _deinterleave(tmp_b, tmp_a, seg)
    for h in range(n):
        ref[:, h * LANES:(h + 1) * LANES] = tmp_b[h]


def _segment_scan(b_r, b_i, x_r, x_i, pw_r, pw_i, car_r, car_i, seg, sign, reverse, visit=None):
    nc = b_r.shape[0]
    sub = lax.broadcasted_iota(jnp.int32, (SCAN_SEGS, LANES), 0)
    lanes = lambda c: slice(c * LANES, (c + 1) * LANES)
    rows = lambda j: pl.ds(pl.multiple_of(((seg - 1 - j) if reverse else j) * SCAN_SEGS, SCAN_SEGS), SCAN_SEGS)
    a1 = [(pw_r[0:1, lanes(c)], sign * pw_i[0:1, lanes(c)]) for c in range(nc)]

    def local(j, xs):
        out = []
        for c in range(nc):
            xr, xi = xs[2 * c], xs[2 * c + 1]
            nr = a1[c][0] * xr - a1[c][1] * xi + b_r[c, rows(j), :]
            ni = a1[c][0] * xi + a1[c][1] * xr + b_i[c, rows(j), :]
            x_r[c, rows(j), :] = nr
            x_i[c, rows(j), :] = ni
            out += [nr, ni]
        return tuple(out)

    zero = jnp.zeros((SCAN_SEGS, LANES), F32)
    ends = lax.fori_loop(0, seg, local, (zero,) * (2 * nc))

    if reverse:
        first = sub == SCAN_SEGS - 1
        neighbour = lambda v: pltpu.roll(v, SCAN_SEGS - 1, 0)
        shift = lambda v, d: jnp.where(sub < SCAN_SEGS - d, pltpu.roll(v, SCAN_SEGS - d, 0), 0.0)
    else:
        first = sub == 0
        neighbour = lambda v: pltpu.roll(v, 1, 0)
        shift = lambda v, d: jnp.where(sub >= d, pltpu.roll(v, d, 0), 0.0)
    last = 0 if reverse else SCAN_SEGS - 1
    entries = []
    for c in range(nc):
        er, ei = ends[2 * c], ends[2 * c + 1]
        pr, pi = pw_r[seg - 1:seg, lanes(c)], sign * pw_i[seg - 1:seg, lanes(c)]
        yr = jnp.where(first, car_r[:, lanes(c)], neighbour(er))
        yi = jnp.where(first, car_i[:, lanes(c)], neighbour(ei))
        qr, qi = pr, pi
        for d in (1, 2, 4):
            mr, mi = _cmul(qr, qi, shift(yr, d), shift(yi, d))
            yr, yi = yr + mr, yi + mi
            qr, qi = _cmul(qr, qi, qr, qi)
        lr, li = _cmul(pr, pi, yr, yi)
        car_r[:, lanes(c)] = (er + lr)[last:last + 1, :]
        car_i[:, lanes(c)] = (ei + li)[last:last + 1, :]
        entries += [yr, yi]

    def correct(j, prev):
        out = []
        row_r, row_i = pw_r[pl.ds(j, 1), :], sign * pw_i[pl.ds(j, 1), :]
        for c in range(nc):
            mr, mi = _cmul(row_r[:, lanes(c)], row_i[:, lanes(c)], entries[2 * c], entries[2 * c + 1])
            nr = x_r[c, rows(j), :] + mr
            ni = x_i[c, rows(j), :] + mi
            x_r[c, rows(j), :] = nr
            x_i[c, rows(j), :] = ni
            if visit is not None:
                visit(c, rows(j), prev[2 * c], prev[2 * c + 1])
            out += [nr, ni]
        return tuple(out)

    lax.fori_loop(0, seg, correct, tuple(entries))


def _s5_fwd(uf, bbr, bbi, cr, ci, ar, ai, seqs):
    t = uf.shape[0]
    l = t // seqs
    tl = min(SCAN_ROWS, l)
    nl = l // tl
    seg = tl // SCAN_SEGS
    cb, nq = SCAN_COLS, SCAN_CHUNKS
    nc = cb // LANES
    per = S5_ST // LANES

    def body(u_ref, bbr_ref, bbi_ref, cr_ref, ci_ref, ar_ref, ai_ref, x_r, x_i, ys_ref,
             car_r, car_i, pw_r, pw_i, b_r, b_i, tmp_a, tmp_b):
        @pl.when(pl.program_id(2) == 0)
        def _():
            car_r[...] = jnp.zeros(car_r.shape, F32)
            car_i[...] = jnp.zeros(car_i.shape, F32)
            _powers_into(pw_r, pw_i, ar_ref[...], ai_ref[...], seg)

        u = _interleaved(u_ref, tmp_a, tmp_b, seg).astype(BF16)
        for q in range(nq):
            uq = u[:, q * S5_IN:(q + 1) * S5_IN]
            br = jnp.dot(uq, bbr_ref[q], preferred_element_type=F32)
            bi = jnp.dot(uq, bbi_ref[q], preferred_element_type=F32)
            for s in range(per):
                b_r[q * per + s] = br[:, s * LANES:(s + 1) * LANES]
                b_i[q * per + s] = bi[:, s * LANES:(s + 1) * LANES]
        _segment_scan(b_r, b_i, x_r, x_i, pw_r, pw_i, car_r, car_i, seg, 1.0, False)
        wide = lambda buf, q: jnp.concatenate([buf[q * per + s] for s in range(per)], axis=1).astype(BF16)
        ys = [jnp.dot(wide(x_r, q), cr_ref[q], preferred_element_type=F32)
              + jnp.dot(wide(x_i, q), ci_ref[q], preferred_element_type=F32) for q in range(nq)]
        _store_deinterleaved(ys_ref, jnp.concatenate(ys, axis=1), tmp_a, tmp_b, seg)

    rows = lambda w: pl.BlockSpec((tl, w), lambda s, j, r: (s * nl + r, j))
    state = pl.BlockSpec((nc, tl, LANES), lambda s, j, r: (j, s * nl + r, 0))
    chunk = lambda a: pl.BlockSpec((nq,) + a.shape[1:], lambda s, j, r: (j, 0, 0))
    par = pl.BlockSpec((1, cb), lambda s, j, r: (0, j))
    return pl.pallas_call(
        body, name="s5_fwd", grid=(seqs, S5_CH // cb, nl),
        in_specs=[rows(nq * S5_IN), chunk(bbr), chunk(bbi), chunk(cr), chunk(ci), par, par],
        out_specs=[state, state, rows(nq * S5_IN)],
        out_shape=[jax.ShapeDtypeStruct((S5_CH // LANES, t, LANES), F32)] * 2
        + [jax.ShapeDtypeStruct((t, S5_WIDTH), F32)],
        scratch_shapes=[pltpu.VMEM((1, cb), F32), pltpu.VMEM((1, cb), F32), pltpu.VMEM((seg, cb), F32),
                        pltpu.VMEM((seg, cb), F32)] + [pltpu.VMEM((nc, tl, LANES), F32)] * 2
        + [pltpu.VMEM((nq * S5_IN // LANES, tl, LANES), F32)] * 2,
        compiler_params=_params(("parallel", "parallel", "arbitrary")),
    )(uf, bbr, bbi, cr, ci, ar, ai)


def _s5_bwd(dys, uf, xr, xi, bbr, bbi, cr, ci, ar, ai, seqs):
    t = dys.shape[0]
    l = t // seqs
    tl = min(SCAN_ROWS, l)
    nl = l // tl
    seg = tl // SCAN_SEGS
    cb, nq = SCAN_COLS, SCAN_CHUNKS
    nc = cb // LANES
    per = S5_ST // LANES

    def body(dy_ref, u_ref, x_r, x_i, bbr_ref, bbi_ref, cr_ref, ci_ref, ar_ref, ai_ref,
             du_ref, dbbr_ref, dbbi_ref, dcr_ref, dci_ref, dar_ref, dai_ref,
             car_r, car_i, pw_r, pw_i, g_r, g_i, lam_r, lam_i, acc_r, acc_i, tmp_a, tmp_b):
        @pl.when(pl.program_id(2) == 0)
        def _():
            car_r[...] = jnp.zeros(car_r.shape, F32)
            car_i[...] = jnp.zeros(car_i.shape, F32)
            _powers_into(pw_r, pw_i, ar_ref[...], ai_ref[...], seg)
            for acc_ref in (dbbr_ref, dbbi_ref, dcr_ref, dci_ref, dar_ref, dai_ref):
                acc_ref[...] = jnp.zeros(acc_ref.shape, F32)

        dy = _interleaved(dy_ref, tmp_a, tmp_b, seg).astype(BF16)
        for q in range(nq):
            dyq = dy[:, q * S5_IN:(q + 1) * S5_IN]
            gr = lax.dot_general(dyq, cr_ref[q], _NT, preferred_element_type=F32)
            gi = lax.dot_general(dyq, ci_ref[q], _NT, preferred_element_type=F32)
            for s in range(per):
                g_r[q * per + s] = gr[:, s * LANES:(s + 1) * LANES]
                g_i[q * per + s] = gi[:, s * LANES:(s + 1) * LANES]
        acc_r[...] = jnp.zeros(acc_r.shape, F32)
        acc_i[...] = jnp.zeros(acc_i.shape, F32)

        def visit(c, rws, lr, li):
            xr_t, xi_t = x_r[c, rws, :], x_i[c, rws, :]
            acc_r[c] += lr * xr_t + li * xi_t
            acc_i[c] += li * xr_t - lr * xi_t

        _segment_scan(g_r, g_i, lam_r, lam_i, pw_r, pw_i, car_r, car_i, seg, -1.0, True, visit)
        for c in range(nc):
            dar_ref[:, c * LANES:(c + 1) * LANES] += jnp.sum(acc_r[c], axis=0, keepdims=True)
            dai_ref[:, c * LANES:(c + 1) * LANES] += jnp.sum(acc_i[c], axis=0, keepdims=True)
        u = _interleaved(u_ref, tmp_a, tmp_b, seg).astype(BF16)
        wide = lambda buf, q: jnp.concatenate([buf[q * per + s] for s in range(per)], axis=1).astype(BF16)
        du = []
        for q in range(nq):
            io = slice(q * S5_IN, (q + 1) * S5_IN)
            lq_r, lq_i = wide(lam_r, q), wide(lam_i, q)
            du.append(lax.dot_general(lq_r, bbr_ref[q], _NT, preferred_element_type=F32)
                      + lax.dot_general(lq_i, bbi_ref[q], _NT, preferred_element_type=F32))
            dbbr_ref[q] += lax.dot_general(u[:, io], lq_r, _TN, preferred_element_type=F32)
            dbbi_ref[q] += lax.dot_general(u[:, io], lq_i, _TN, preferred_element_type=F32)
            dcr_ref[q] += lax.dot_general(wide(x_r, q), dy[:, io], _TN, preferred_element_type=F32)
            dci_ref[q] += lax.dot_general(wide(x_i, q), dy[:, io], _TN, preferred_element_type=F32)
        _store_deinterleaved(du_ref, jnp.concatenate(du, axis=1), tmp_a, tmp_b, seg)

    rows = lambda w: pl.BlockSpec((tl, w), lambda s, j, r: (s * nl + nl - 1 - r, j))
    state = pl.BlockSpec((nc, tl, LANES), lambda s, j, r: (j, s * nl + nl - 1 - r, 0))
    chunk = lambda a: pl.BlockSpec((nq,) + a.shape[1:], lambda s, j, r: (j, 0, 0))
    acc = lambda a: pl.BlockSpec((None, nq) + a.shape[1:], lambda s, j, r: (s, j, 0, 0))
    par = pl.BlockSpec((1, cb), lambda s, j, r: (0, j))
    par_acc = pl.BlockSpec((None, 1, cb), lambda s, j, r: (s, 0, j))
    per_seq = lambda a: jax.ShapeDtypeStruct((seqs,) + a.shape, F32)
    return pl.pallas_call(
        body, name="s5_bwd", grid=(seqs, S5_CH // cb, nl),
        in_specs=[rows(nq * S5_IN), rows(nq * S5_IN), state, state, chunk(bbr), chunk(bbi), chunk(cr), chunk(ci),
                  par, par],
        out_specs=[rows(nq * S5_IN), acc(bbr), acc(bbi), acc(cr), acc(ci), par_acc, par_acc],
        out_shape=[jax.ShapeDtypeStruct((t, S5_WIDTH), F32), per_seq(bbr), per_seq(bbi), per_seq(cr), per_seq(ci),
                   jax.ShapeDtypeStruct((seqs, 1, S5_CH), F32), jax.ShapeDtypeStruct((seqs, 1, S5_CH), F32)],
        scratch_shapes=[pltpu.VMEM((1, cb), F32), pltpu.VMEM((1, cb), F32), pltpu.VMEM((seg, cb), F32),
                        pltpu.VMEM((seg, cb), F32)] + [pltpu.VMEM((nc, tl, LANES), F32)] * 4
        + [pltpu.VMEM((nc, SCAN_SEGS, LANES), F32)] * 2 + [pltpu.VMEM((nq * S5_IN // LANES, tl, LANES), F32)] * 2,
        compiler_params=_params(("parallel", "parallel", "arbitrary")),
    )(dys, uf, xr, xi, bbr, bbi, cr, ci, ar, ai)


XATT_BLOCK = 2048


def _xatt_probs(qv, kv):
    s = lax.dot_general(qv, kv, _NT, preferred_element_type=F32) * (X_HEAD_DIM ** -0.5)
    e = jnp.exp(s - jnp.max(s, axis=-1, keepdims=True))
    return e / jnp.sum(e, axis=-1, keepdims=True)


def _xatt_fwd(q, k, kv, seqs):
    t = q.shape[0]
    tq = min(XATT_BLOCK, t // seqs)
    nq = t // seqs // tq

    def body(q_ref, k_ref, v_ref, o_ref):
        p = _xatt_probs(q_ref[...], k_ref[...])
        o_ref[...] = jnp.dot(p.astype(BF16), v_ref[...].astype(BF16), preferred_element_type=F32).astype(o_ref.dtype)

    qs = pl.BlockSpec((tq, X_HEAD_DIM), lambda b, h, i: (b * nq + i, h))
    return pl.pallas_call(
        body, name="xatt_fwd", grid=(seqs, N_X_HEADS, nq),
        in_specs=[qs, pl.BlockSpec((N_MEM, X_HEAD_DIM), lambda b, h, i: (b, h)),
                  pl.BlockSpec((N_MEM, X_HEAD_DIM), lambda b, h, i: (b, N_X_HEADS + h))],
        out_specs=qs, out_shape=jax.ShapeDtypeStruct(q.shape, BF16),
        compiler_params=_params(("parallel", "parallel", "parallel")),
    )(q, k, kv)


def _xatt_bwd(q, k, kv, do, seqs):
    t = q.shape[0]
    tq = min(XATT_BLOCK, t // seqs)
    nq = t // seqs // tq
    scale = X_HEAD_DIM ** -0.5

    def body(q_ref, k_ref, v_ref, do_ref, dq_ref, dk_ref, dv_ref):
        @pl.when(pl.program_id(2) == 0)
        def _():
            dk_ref[...] = jnp.zeros(dk_ref.shape, F32)
            dv_ref[...] = jnp.zeros(dv_ref.shape, F32)

        qv, kk = q_ref[...], k_ref[...]
        p = _xatt_probs(qv, kk)
        dob = do_ref[...].astype(BF16)
        dp = lax.dot_general(dob, v_ref[...].astype(BF16), _NT, preferred_element_type=F32)
        ds = p * (dp - jnp.sum(dp * p, axis=-1, keepdims=True))
        dsb = ds.astype(BF16)
        dq_ref[...] = jnp.dot(dsb, kk, preferred_element_type=F32) * scale
        dk_ref[...] += lax.dot_general(dsb, qv, _TN, preferred_element_type=F32) * scale
        dv_ref[...] += lax.dot_general(p.astype(BF16), dob, _TN, preferred_element_type=F32)

    qs = pl.BlockSpec((tq, X_HEAD_DIM), lambda b, h, i: (b * nq + i, h))
    ks = pl.BlockSpec((N_MEM, X_HEAD_DIM), lambda b, h, i: (b, h))
    return pl.pallas_call(
        body, name="xatt_bwd", grid=(seqs, N_X_HEADS, nq),
        in_specs=[qs, ks, pl.BlockSpec((N_MEM, X_HEAD_DIM), lambda b, h, i: (b, N_X_HEADS + h)), qs],
        out_specs=[qs, ks, ks],
        out_shape=[jax.ShapeDtypeStruct(q.shape, F32), jax.ShapeDtypeStruct(k.shape, F32),
                   jax.ShapeDtypeStruct(k.shape, F32)],
        compiler_params=_params(("parallel", "parallel", "arbitrary")),
    )(q, k, kv, do)


CONV_COLS = 256


def _shift_down(x, k, row):
    return jnp.where(row >= k, pltpu.roll(x, k, 0), 0.0)


def _shift_up(x, k, row):
    n = x.shape[0]
    return jnp.where(row < n - k, pltpu.roll(x, n - k, 0), 0.0)


def _conv_pre(g, w, b, row):
    return b + w[0:1, :] * _shift_down(g, 2, row) + w[1:2, :] * _shift_down(g, 1, row) + w[2:3, :] * g


def _convgate_fwd(gu, w, b, seqs):
    t = gu.shape[0]
    l = t // seqs
    nc = D_FF // CONV_COLS

    def body(g_ref, u_ref, w_ref, b_ref, o_ref):
        g = g_ref[...].astype(F32)
        row = lax.broadcasted_iota(jnp.int32, g.shape, 0)
        pre = _conv_pre(g, w_ref[...], b_ref[...], row)
        o_ref[...] = (pre * jax.nn.sigmoid(pre) * u_ref[...].astype(F32)).astype(o_ref.dtype)

    return pl.pallas_call(
        body, name="convgate_fwd", grid=(seqs, nc),
        in_specs=[pl.BlockSpec((l, CONV_COLS), lambda s, j: (s, j)), pl.BlockSpec((l, CONV_COLS), lambda s, j: (s, nc + j)),
                  pl.BlockSpec((3, CONV_COLS), lambda s, j: (0, j)), pl.BlockSpec((1, CONV_COLS), lambda s, j: (0, j))],
        out_specs=pl.BlockSpec((l, CONV_COLS), lambda s, j: (s, j)),
        out_shape=jax.ShapeDtypeStruct((t, D_FF), BF16),
        compiler_params=_params(("parallel", "parallel")),
    )(gu, gu, w, b)


def _convgate_bwd(gu, w, b, dact, seqs):
    t = gu.shape[0]
    l = t // seqs
    nc = D_FF // CONV_COLS
    steps = nc * seqs

    def body(g_ref, u_ref, w_ref, b_ref, da_ref, dgu_ref, dw_ref, db_ref, stage, sems):
        j, s = pl.program_id(0), pl.program_id(1)
        n = j * seqs + s
        slot = n % 2

        def copies(slot_, j_, s_):
            rows = pl.ds(pl.multiple_of(s_ * l, 16), l)
            return [pltpu.make_async_copy(
                stage.at[slot_, half],
                dgu_ref.at[rows, pl.ds(pl.multiple_of((half * nc + j_) * CONV_COLS, 128), CONV_COLS)],
                sems.at[slot_, half]) for half in (0, 1)]

        @pl.when(s == 0)
        def _():
            dw_ref[...] = jnp.zeros(dw_ref.shape, F32)
            db_ref[...] = jnp.zeros(db_ref.shape, F32)

        @pl.when(n >= 2)
        def _():
            for cp in copies(slot, j, s):
                cp.wait()

        g, wv, da = g_ref[...].astype(F32), w_ref[...], da_ref[...].astype(F32)
        row = lax.broadcasted_iota(jnp.int32, g.shape, 0)
        g1, g2 = _shift_down(g, 1, row), _shift_down(g, 2, row)
        pre = b_ref[...] + wv[0:1, :] * g2 + wv[1:2, :] * g1 + wv[2:3, :] * g
        sg = jax.nn.sigmoid(pre)
        silu = pre * sg
        stage[slot, 1] = (da * silu).astype(stage.dtype)
        dpre = da * u_ref[...].astype(F32) * (sg * (1.0 + pre * (1.0 - sg)))
        dg = wv[2:3, :] * dpre + wv[1:2, :] * _shift_up(dpre, 1, row) + wv[0:1, :] * _shift_up(dpre, 2, row)
        stage[slot, 0] = dg.astype(stage.dtype)
        for cp in copies(slot, j, s):
            cp.start()
        dw_ref[0:1, :] += jnp.sum(dpre * g2, axis=0, keepdims=True)
        dw_ref[1:2, :] += jnp.sum(dpre * g1, axis=0, keepdims=True)
        dw_ref[2:3, :] += jnp.sum(dpre * g, axis=0, keepdims=True)
        db_ref[...] += jnp.sum(dpre, axis=0, keepdims=True)

        @pl.when(n == steps - 1)
        def _():
            for cp in copies(slot, j, s) + (copies(1 - slot, j, s) if steps > 1 else []):
                cp.wait()

    blk = lambda off: pl.BlockSpec((l, CONV_COLS), lambda j, s: (s, off + j))
    return pl.pallas_call(
        body, name="convgate_bwd", grid=(nc, seqs),
        in_specs=[blk(0), blk(nc), pl.BlockSpec((3, CONV_COLS), lambda j, s: (0, j)),
                  pl.BlockSpec((1, CONV_COLS), lambda j, s: (0, j)), blk(0)],
        out_specs=[ANY, pl.BlockSpec((3, CONV_COLS), lambda j, s: (0, j)),
                   pl.BlockSpec((1, CONV_COLS), lambda j, s: (0, j))],
        out_shape=[jax.ShapeDtypeStruct((t, 2 * D_FF), BF16), jax.ShapeDtypeStruct((3, D_FF), F32),
                   jax.ShapeDtypeStruct((1, D_FF), F32)],
        scratch_shapes=[pltpu.VMEM((2, 2, l, CONV_COLS), BF16), pltpu.SemaphoreType.DMA((2, 2))],
        compiler_params=_params(("arbitrary", "arbitrary")),
    )(gu, gu, w, b, dact)


def _loss_head(h, target):
    t, d = h.shape
    tm = _pick(t, (256, 128, 8))

    def body(h_ref, t_ref, dh_ref, dhb_ref, loss_ref):
        @pl.when(pl.program_id(0) == 0)
        def _():
            loss_ref[...] = jnp.zeros(loss_ref.shape, F32)

        e = h_ref[...] - t_ref[...]
        dh = e * (1.0 / d)
        dh_ref[...] = dh
        dhb_ref[...] = dh.astype(BF16)
        loss_ref[...] += (0.5 / d) * jnp.sum(jnp.sum(e * e, axis=1, keepdims=True), axis=0, keepdims=True)

    blk = pl.BlockSpec((tm, d), lambda i: (i, 0))
    return pl.pallas_call(
        body, name="loss_head", grid=(t // tm,), in_specs=[blk, blk],
        out_specs=[blk, blk, pl.BlockSpec((1, 1), lambda i: (0, 0))],
        out_shape=[jax.ShapeDtypeStruct((t, d), F32), jax.ShapeDtypeStruct((t, d), BF16),
                   jax.ShapeDtypeStruct((1, 1), F32)],
        compiler_params=_params(("arbitrary",)),
    )(h, target)


def _s5_discretise(a_re, a_im, log_dt, b_re, b_im):
    dt = jnp.exp(log_dt)[:, None]
    mag = jnp.exp(a_re * dt)
    lb_r = mag * jnp.cos(a_im * dt)
    lb_i = mag * jnp.sin(a_im * dt)
    den = a_re * a_re + a_im * a_im
    nr = lb_r - 1.0
    coef_r = (nr * a_re + lb_i * a_im) / den
    coef_i = (lb_i * a_re - nr * a_im) / den
    bb_r = coef_r[:, :, None] * b_re - coef_i[:, :, None] * b_im
    bb_i = coef_r[:, :, None] * b_im + coef_i[:, :, None] * b_re
    return lb_r, lb_i, bb_r, bb_i


S5_CHUNKS = 4
S5_PER = S5_GROUPS // S5_CHUNKS


def _blockdiag_in(bb):
    eye = jnp.eye(S5_PER, dtype=bb.dtype)
    return jnp.einsum("jgpc,gh->jgchp", bb.reshape(S5_CHUNKS, S5_PER, S5_STATE, S5_GROUP_CH), eye).reshape(
        S5_CHUNKS, S5_PER * S5_GROUP_CH, S5_PER * S5_STATE)


def _blockdiag_in_grad(d):
    eye = jnp.eye(S5_PER, dtype=d.dtype)
    return jnp.einsum("jgchp,gh->jgpc", d.reshape(S5_CHUNKS, S5_PER, S5_GROUP_CH, S5_PER, S5_STATE), eye).reshape(
        S5_GROUPS, S5_STATE, S5_GROUP_CH)


def _blockdiag_out(c):
    eye = jnp.eye(S5_PER, dtype=c.dtype)
    return jnp.einsum("jgcp,gh->jgphc", c.reshape(S5_CHUNKS, S5_PER, S5_GROUP_CH, S5_STATE), eye).reshape(
        S5_CHUNKS, S5_PER * S5_STATE, S5_PER * S5_GROUP_CH)


def _blockdiag_out_grad(d):
    eye = jnp.eye(S5_PER, dtype=d.dtype)
    return jnp.einsum("jgphc,gh->jgcp", d.reshape(S5_CHUNKS, S5_PER, S5_STATE, S5_PER, S5_GROUP_CH), eye).reshape(
        S5_GROUPS, S5_GROUP_CH, S5_STATE)


def _local_step(x3, mem3, target3, p, wb, late_weights=None, early_grads=None):
    seqs, l, d = x3.shape
    t = seqs * l
    x = x3.reshape(t, d)
    mem = mem3.reshape(seqs * N_MEM, d)
    target = target3.reshape(t, d)
    full = lambda a: (a, a.shape[1], 0, 0)

    s5_in = (p["s5_a_re"], p["s5_a_im"], p["s5_log_dt"], p["s5_b_re"], p["s5_b_im"])
    (lb_r, lb_i, bb_r, bb_i), s5_pull = jax.vjp(_s5_discretise, *s5_in)
    ar, ai = lb_r.reshape(1, S5_CH), lb_i.reshape(1, S5_CH)
    bbr_d, bbi_d = _blockdiag_in(bb_r).astype(BF16), _blockdiag_in(bb_i).astype(BF16)
    cr_d, ci_d = _blockdiag_out(p["s5_c_re"]).astype(BF16), (-_blockdiag_out(p["s5_c_im"])).astype(BF16)
    d_row = p["s5_d"].reshape(1, S5_WIDTH)

    w_in = wb["w_in"]
    w_qkv = w_in[:, :3 * FOX_WIDTH]
    w_uf = jnp.concatenate(
        [w_in[:, 3 * FOX_WIDTH + N_FOX_HEADS:], w_in[:, 3 * FOX_WIDTH:3 * FOX_WIDTH + N_FOX_HEADS],
         jnp.zeros((d, UF_COLS - S5_WIDTH - N_FOX_HEADS), w_in.dtype)], axis=1)

    hn1 = _rowwise(_rms, [full(x)], [p["norm_mix"]], [(d, d, 0, BF16)], "norm_mix_fwd")
    qkv = _mm(hn1, w_qkv, "nn", "in_qkv")
    uf = _mm(hn1, w_uf, "nn", "in_uf")

    bh = seqs * N_FOX_HEADS
    q_pair = (qkv, 128, 0, 1)
    k_pair = (qkv, 128, N_PAIRS, 1)
    gq2, gk2 = jnp.tile(p["fox_q_norm"], (1, 2)), jnp.tile(p["fox_k_norm"], (1, 2))
    pair_out = [(FOX_WIDTH, 128, 1, BF16)]
    qn = _rowwise(_rms_pair, [q_pair], [gq2], pair_out, "fox_qnorm_fwd", heads=N_PAIRS)
    kn = _rowwise(_rms_pair, [k_pair], [gk2], pair_out, "fox_knorm_fwd", heads=N_PAIRS)

    f_rows = uf[:, S5_WIDTH:S5_WIDTH + N_FOX_HEADS].reshape(seqs, l, N_FOX_HEADS).transpose(0, 2, 1).reshape(bh, l)
    f_bias = jnp.tile(p["fox_f_bias"].reshape(N_FOX_HEADS, 1), (seqs, 1))
    c_wide = jnp.broadcast_to(_forget_fwd(f_rows, f_bias)[:, :, None], (bh, l, 128))
    fox, lse = _fox_fwd(qn, kn, qkv, c_wide, seqs)

    xr, xi, ys = _s5_fwd(uf, bbr_d, bbi_d, cr_d, ci_d, ar, ai, seqs)
    u_blk = (uf, S5_WIDTH, 0, 0)
    yg = _rowwise(_s5_act, [full(ys), u_blk], [d_row], [(S5_WIDTH, S5_WIDTH, 0, F32)], "s5_act_fwd")
    if late_weights is not None:
        wb = dict(wb, **late_weights("mid", yg))
    z = _mm(yg, wb["s5_w_glu"], "nn", "s5_glu")
    y2n = _rowwise(_s5_gate, [full(yg), full(z)], [p["s5_b_glu"], p["out_norm_s5"]],
                   [(S5_WIDTH, S5_WIDTH, 0, BF16)], "s5_gate_fwd")
    foxn = _rowwise(_rms, [full(fox)], [p["out_norm_fox"]], [(FOX_WIDTH, FOX_WIDTH, 0, BF16)], "fox_outnorm_fwd")
    mixed = jnp.concatenate([foxn, y2n], axis=1)
    h1 = _mm(mixed, wb["w_out"], "nn", "mix_out", res=x)
    if late_weights is not None:
        wb = dict(wb, **late_weights("late", h1))

    hn2 = _rowwise(_rms, [full(h1)], [p["norm_cross"]], [(d, d, 0, BF16)], "norm_cross_fwd")
    mn = _rowwise(_rms, [full(mem)], [p["norm_mem"]], [(d, d, 0, BF16)], "norm_mem_fwd")
    xq_raw = _mm(hn2, wb["w_xq"], "nn", "x_q")
    kv = _mm(mn, wb["w_xkv"], "nn", "x_kv")
    xh = lambda a: (a, X_HEAD_DIM, 0, 1)
    xqn = _rowwise(_rms, [xh(xq_raw)], [p["xq_norm"]], [(d, X_HEAD_DIM, 1, BF16)], "x_qnorm_fwd", heads=N_X_HEADS)
    xkn = _rowwise(_rms, [xh(kv)], [p["xk_norm"]], [(d, X_HEAD_DIM, 1, BF16)], "x_knorm_fwd", heads=N_X_HEADS)
    xo = _xatt_fwd(xqn, xkn, kv, seqs)
    h2 = _mm(xo, wb["w_xo"], "nn", "x_out", res=h1)

    hn3 = _rowwise(_rms, [full(h2)], [p["norm_ffn"]], [(d, d, 0, BF16)], "norm_ffn_fwd")
    gu = _mm(hn3, wb["w_ffn_up"], "nn", "ffn_up", out_dtype=BF16)
    act = _convgate_fwd(gu, p["ffn_conv_w"], p["ffn_conv_b"], seqs)
    h3 = _mm(act, wb["w_ffn_down"], "nn", "ffn_down", res=h2)
    dh3, dh3_b, loss = _loss_head(h3, target)

    g = {}
    dact = _mm(dh3_b, wb["w_ffn_down"], "nt", "ffn_down_dx", out_dtype=BF16)
    late_dt = BF16 if early_grads is not None else F32
    g["w_ffn_down"] = _mm(act, dh3_b, "tn", "ffn_down_dw", out_dtype=late_dt)
    dgu, g["ffn_conv_w"], g["ffn_conv_b"] = _convgate_bwd(gu, p["ffn_conv_w"], p["ffn_conv_b"], dact, seqs)
    dhn3 = _mm(dgu, wb["w_ffn_up"], "nt", "ffn_up_dx")
    g["w_ffn_up"] = _mm(hn3, dgu, "tn", "ffn_up_dw", out_dtype=late_dt)
    (dh2,), (g["norm_ffn"],) = _rowwise_vjp(_rms, [full(h2)], [p["norm_ffn"]], [full(dhn3)], "norm_ffn_bwd",
                                            adds=[full(dh3)])

    dxo = _mm(dh2, wb["w_xo"], "nt", "x_out_dx")
    g["w_xo"] = _mm(xo, dh2, "tn", "x_out_dw", out_dtype=late_dt)
    dxqn, dxkn, dxv = _xatt_bwd(xqn, xkn, kv, dxo, seqs)
    (dxq_raw,), (g["xq_norm"],) = _rowwise_vjp(_rms, [xh(xq_raw)], [p["xq_norm"]], [xh(dxqn)], "x_qnorm_bwd",
                                               heads=N_X_HEADS, row_dtypes=[BF16])
    (dxk_raw,), (g["xk_norm"],) = _rowwise_vjp(_rms, [xh(kv)], [p["xk_norm"]], [xh(dxkn)], "x_knorm_bwd",
                                               heads=N_X_HEADS, row_dtypes=[BF16])
    dkv = jnp.concatenate([dxk_raw, dxv.astype(BF16)], axis=1)
    dhn2 = _mm(dxq_raw, wb["w_xq"], "nt", "x_q_dx")
    g["w_xq"] = _mm(hn2, dxq_raw, "tn", "x_q_dw", out_dtype=late_dt)
    dmn = _mm(dkv, wb["w_xkv"], "nt", "x_kv_dx")
    g["w_xkv"] = _mm(mn, dkv, "tn", "x_kv_dw", out_dtype=late_dt)
    norm_cross = p["norm_cross"]
    if early_grads is not None:
        norm_cross = norm_cross + early_grads({n: g[n] for n in LATE_WEIGHTS})
    (dh1,), (g["norm_cross"],) = _rowwise_vjp(_rms, [full(h1)], [norm_cross], [full(dhn2)], "norm_cross_bwd",
                                              adds=[full(dh2)])
    _, (g["norm_mem"],) = _rowwise_vjp(_rms, [full(mem)], [p["norm_mem"]], [full(dmn)], "norm_mem_bwd",
                                       row_dtypes=[BF16])

    dmixed = _mm(dh1, wb["w_out"], "nt", "mix_out_dx")
    g["w_out"] = _mm(mixed, dh1, "tn", "mix_out_dw", out_dtype=late_dt)
    (dfox,), (g["out_norm_fox"],) = _rowwise_vjp(_rms, [full(fox)], [p["out_norm_fox"]],
                                                 [(dmixed, FOX_WIDTH, 0, 0)], "fox_outnorm_bwd")
    (dyg_a, dz), (g["s5_b_glu"], g["out_norm_s5"]) = _rowwise_vjp(
        _s5_gate, [full(yg), full(z)], [p["s5_b_glu"], p["out_norm_s5"]], [(dmixed, S5_WIDTH, 1, 0)], "s5_gate_bwd",
        row_dtypes=[F32, BF16])
    dyg = _mm(dz, wb["s5_w_glu"], "nt", "s5_glu_dx", res=dyg_a)
    g["s5_w_glu"] = _mm(yg, dz, "tn", "s5_glu_dw", out_dtype=late_dt)
    (dys, du_a), (dd_row,) = _rowwise_vjp(_s5_act, [full(ys), u_blk], [d_row], [full(dyg)], "s5_act_bwd",
                                          row_dtypes=[BF16, F32])
    g["s5_d"] = dd_row
    du_b, dbbr_d, dbbi_d, dcr_d, dci_d, dar, dai = _s5_bwd(dys, uf, xr, xi, bbr_d, bbi_d, cr_d, ci_d, ar, ai, seqs)
    dbbr_d, dbbi_d, dcr_d, dci_d = (jnp.sum(a, axis=0) for a in (dbbr_d, dbbi_d, dcr_d, dci_d))
    d_lb_r = jnp.sum(dar, axis=0).reshape(S5_GROUPS, S5_STATE)
    d_lb_i = jnp.sum(dai, axis=0).reshape(S5_GROUPS, S5_STATE)
    g["s5_a_re"], g["s5_a_im"], g["s5_log_dt"], g["s5_b_re"], g["s5_b_im"] = s5_pull(
        (d_lb_r, d_lb_i, _blockdiag_in_grad(dbbr_d), _blockdiag_in_grad(dbbi_d)))
    g["s5_c_re"] = _blockdiag_out_grad(dcr_d)
    g["s5_c_im"] = -_blockdiag_out_grad(dci_d)

    dqn, dkn, dv, dc, dcq = _fox_bwd(qn, kn, qkv, c_wide, fox, dfox, lse, seqs)
    pair = lambda a: (a, 128, 0, 1)
    (dq_raw,), (dgq2,) = _rowwise_vjp(_rms_pair, [q_pair], [gq2], [pair(dqn)], "fox_qnorm_bwd", heads=N_PAIRS,
                                      row_dtypes=[BF16])
    (dk_raw,), (dgk2,) = _rowwise_vjp(_rms_pair, [k_pair], [gk2], [pair(dkn)], "fox_knorm_bwd", heads=N_PAIRS,
                                      row_dtypes=[BF16])
    g["fox_q_norm"] = dgq2[:, :HEAD_DIM] + dgq2[:, HEAD_DIM:]
    g["fox_k_norm"] = dgk2[:, :HEAD_DIM] + dgk2[:, HEAD_DIM:]
    df_rows, dfb = _forget_bwd(f_rows, f_bias, (dc + dcq).reshape(bh, l))
    g["fox_f_bias"] = jnp.sum(dfb.reshape(seqs, N_FOX_HEADS), axis=0)
    df = df_rows.reshape(seqs, N_FOX_HEADS, l).transpose(0, 2, 1).reshape(t, N_FOX_HEADS)
    dqkv = jnp.concatenate([dq_raw, dk_raw, dv.astype(BF16)], axis=1)
    duf = jnp.concatenate([du_a + du_b, df, jnp.zeros((t, UF_COLS - S5_WIDTH - N_FOX_HEADS), F32)],
                          axis=1).astype(BF16)
    dhn1 = _mm(duf, w_uf, "nt", "in_uf_dx", res=_mm(dqkv, w_qkv, "nt", "in_qkv_dx"))
    dw_qkv = _mm(hn1, dqkv, "tn", "in_qkv_dw")
    dw_uf = _mm(hn1, duf, "tn", "in_uf_dw")
    g["w_in"] = jnp.concatenate([dw_qkv, dw_uf[:, S5_WIDTH:S5_WIDTH + N_FOX_HEADS], dw_uf[:, :S5_WIDTH]], axis=1)
    (dx,), (g["norm_mix"],) = _rowwise_vjp(_rms, [full(x)], [p["norm_mix"]], [full(dhn1)], "norm_mix_bwd",
                                           adds=[full(dh1)])
    return loss, dx.reshape(seqs, l, d), g


def _place():
    return lax.axis_index("x"), lax.axis_index("y"), lax.axis_index("c")


def _other_chips(x, y):
    return [(1 - x, y), (x, 1 - y), (1 - x, 1 - y)]


ANY = pl.BlockSpec(memory_space=pl.ANY)


def _gather_weights(shards, col_kind, taps):
    n = len(shards)

    def body(*refs):
        ins, tap_in, outs, tap_out = refs[:n], refs[n], refs[n + 1:2 * n + 1], refs[2 * n + 1]
        ici_send, ici_recv, d2d_send, d2d_recv, own_send, own_recv = refs[2 * n + 2:]
        x, y, c = _place()
        mine = 2 * x + y
        chips = _other_chips(x, y)
        sibling = (x, y, 1 - c)

        def piece(a, s, h):
            r, cs = ins[a].shape
            hr = r // 2
            if col_kind[a]:
                return outs[a].at[pl.ds(pl.multiple_of(h * hr, 16), hr), pl.ds(pl.multiple_of(s * cs, 128), cs)]
            return outs[a].at[pl.ds(pl.multiple_of(s * r + h * hr, 16), hr), :]

        def slab(a, s):
            r, cs = ins[a].shape
            if col_kind[a]:
                return outs[a].at[:, pl.ds(pl.multiple_of(s * cs, 128), cs)]
            return outs[a].at[pl.ds(pl.multiple_of(s * r, 16), r), :]

        def own_half(a, h):
            hr = ins[a].shape[0] // 2
            return ins[a].at[pl.ds(pl.multiple_of(h * hr, 16), hr), :]

        sends = []
        for a in range(n):
            cp = pltpu.make_async_remote_copy(
                src_ref=ins[a], dst_ref=slab(a, mine), send_sem=own_send.at[a], recv_sem=own_recv.at[a],
                device_id=sibling, device_id_type=MESH)
            cp.start()
            sends.append(cp)
        cp = pltpu.make_async_remote_copy(
            src_ref=tap_in, dst_ref=tap_out.at[mine], send_sem=own_send.at[n], recv_sem=own_recv.at[n],
            device_id=sibling, device_id_type=MESH)
        cp.start()
        sends.append(cp)
        for a in range(n):
            for j, (px, py) in enumerate(chips):
                cp = pltpu.make_async_remote_copy(
                    src_ref=own_half(a, c), dst_ref=piece(a, mine, c), send_sem=ici_send.at[3 * a + j],
                    recv_sem=ici_recv.at[3 * a + j], device_id=(px, py, c), device_id_type=MESH)
                cp.start()
                sends.append(cp)
        for j, (px, py) in enumerate(chips):
            cp = pltpu.make_async_remote_copy(
                src_ref=tap_in, dst_ref=tap_out.at[mine], send_sem=ici_send.at[3 * n + j],
                recv_sem=ici_recv.at[3 * n + j], device_id=(px, py, c), device_id_type=MESH)
            cp.start()
            sends.append(cp)
        for a in range(n):
            for j, (px, py) in enumerate(chips):
                got = piece(a, 2 * px + py, c)
                pltpu.make_async_remote_copy(
                    src_ref=got, dst_ref=got, send_sem=ici_send.at[3 * a + j], recv_sem=ici_recv.at[3 * a + j],
                    device_id=(px, py, c), device_id_type=MESH).wait_recv()
                fwd = pltpu.make_async_remote_copy(
                    src_ref=got, dst_ref=got, send_sem=d2d_send.at[3 * a + j], recv_sem=d2d_recv.at[3 * a + j],
                    device_id=(x, y, 1 - c), device_id_type=MESH)
                fwd.start()
                sends.append(fwd)
        for a in range(n):
            for j, (px, py) in enumerate(chips):
                other = piece(a, 2 * px + py, 1 - c)
                pltpu.make_async_remote_copy(
                    src_ref=other, dst_ref=other, send_sem=d2d_send.at[3 * a + j], recv_sem=d2d_recv.at[3 * a + j],
                    device_id=(x, y, 1 - c), device_id_type=MESH).wait_recv()
        for j, (px, py) in enumerate(chips):
            pltpu.make_async_remote_copy(
                src_ref=tap_in, dst_ref=tap_out.at[2 * px + py], send_sem=ici_send.at[3 * n + j],
                recv_sem=ici_recv.at[3 * n + j], device_id=(px, py, c), device_id_type=MESH).wait_recv()
        for a in range(n):
            pltpu.make_async_remote_copy(
                src_ref=ins[a], dst_ref=slab(a, mine), send_sem=own_send.at[a], recv_sem=own_recv.at[a],
                device_id=sibling, device_id_type=MESH).wait_recv()
        pltpu.make_async_remote_copy(
            src_ref=tap_in, dst_ref=tap_out.at[mine], send_sem=own_send.at[n], recv_sem=own_recv.at[n],
            device_id=sibling, device_id_type=MESH).wait_recv()
        for cp in sends:
            cp.wait_send()

    def full_shape(a):
        r, cs = shards[a].shape
        return (r, 4 * cs) if col_kind[a] else (4 * r, cs)

    res = pl.pallas_call(
        body, name="gather_weights", in_specs=[ANY] * (n + 1), out_specs=[ANY] * (n + 1),
        out_shape=[jax.ShapeDtypeStruct(full_shape(a), shards[a].dtype) for a in range(n)]
        + [jax.ShapeDtypeStruct((4,) + taps.shape, taps.dtype)],
        scratch_shapes=[pltpu.SemaphoreType.DMA((3 * n + 3,)), pltpu.SemaphoreType.DMA((3 * n + 3,)),
                        pltpu.SemaphoreType.DMA((3 * n,)), pltpu.SemaphoreType.DMA((3 * n,)),
                        pltpu.SemaphoreType.DMA((n + 1,)), pltpu.SemaphoreType.DMA((n + 1,))],
        compiler_params=pltpu.CompilerParams(has_side_effects=True),
    )(*shards, taps)
    return res[:n], res[n]


HBM = pl.BlockSpec(memory_space=pltpu.HBM)
SEM = pl.BlockSpec(memory_space=pltpu.SEMAPHORE)
DATAFLOW = pltpu.SideEffectType.DATAFLOW_SIDE_EFFECTING


def _in_hbm(a):
    return pltpu.with_memory_space_constraint(a, pltpu.HBM)


def _split_start(name, srcs, lands, n_copies, plan):
    n = len(srcs)

    def body(*refs):
        src_refs, land_refs = refs[:n], refs[n:2 * n]
        send_sems, recv_sems = refs[2 * n], refs[2 * n + 1]
        for i, (src, dst, dev) in enumerate(plan(src_refs, land_refs)):
            pltpu.make_async_remote_copy(src_ref=src, dst_ref=dst, send_sem=send_sems.at[i], recv_sem=recv_sems.at[i],
                                         device_id=dev, device_id_type=MESH).start()
        refs[-1][...] = jnp.zeros((8, 128), F32)

    res = pl.pallas_call(
        body, name=name, in_specs=[HBM] * (2 * n),
        out_specs=[SEM, SEM] + [HBM] * (2 * n) + [pl.BlockSpec(memory_space=pltpu.VMEM)],
        out_shape=[pltpu.SemaphoreType.DMA((n_copies,)), pltpu.SemaphoreType.DMA((n_copies,))]
        + [pltpu.HBM(a.shape, a.dtype) for a in list(srcs) + list(lands)] + [jax.ShapeDtypeStruct((8, 128), F32)],
        input_output_aliases={i: 2 + i for i in range(2 * n)},
        compiler_params=pltpu.CompilerParams(has_side_effects=DATAFLOW),
    )(*[_in_hbm(a) for a in list(srcs) + list(lands)])
    return res[0], res[1], list(res[2:2 + n]), list(res[2 + n:2 + 2 * n]), res[-1]


def _split_wait(name, send_sems, recv_sems, srcs, lands, after, plan):
    n = len(srcs)

    def body(*refs):
        src_refs, land_refs = refs[:n], refs[n:2 * n]
        send_ref, recv_ref = refs[2 * n], refs[2 * n + 1]
        for i, (src, dst, dev) in enumerate(plan(src_refs, land_refs)):
            cp = pltpu.make_async_remote_copy(src_ref=src, dst_ref=dst, send_sem=send_ref.at[i], recv_sem=recv_ref.at[i],
                                              device_id=dev, device_id_type=MESH)
            cp.wait_send()
            cp.wait_recv()

    res = pl.pallas_call(
        body, name=name, in_specs=[HBM] * (2 * n) + [SEM, SEM, ANY], out_specs=[HBM] * (2 * n),
        out_shape=[pltpu.HBM(a.shape, a.dtype) for a in list(srcs) + list(lands)],
        input_output_aliases={i: i for i in range(2 * n)},
        compiler_params=pltpu.CompilerParams(has_side_effects=DATAFLOW),
    )(*srcs, *lands, send_sems, recv_sems, after)
    return list(res[:n]), list(res[n:])


def _late_gather_plan(col_kind):
    def plan(src_refs, land_refs):
        x, y, c = _place()
        mine = 2 * x + y
        copies = []
        for a, (src, land) in enumerate(zip(src_refs, land_refs)):
            r, cs = src.shape
            if col_kind[a]:
                dst = land.at[:, pl.ds(pl.multiple_of(mine * cs, 128), cs)]
            else:
                dst = land.at[pl.ds(pl.multiple_of(mine * r, 16), r), :]
            copies.append((src, dst, (x, y, 1 - c)))
            copies += [(src, dst, (px, py, c)) for (px, py) in _other_chips(x, y)]
        return copies
    return plan


def _late_reduce_plan(col_kind):
    def plan(src_refs, land_refs):
        x, y, c = _place()
        copies = []
        for a, (src, land) in enumerate(zip(src_refs, land_refs)):
            for j, (px, py) in enumerate(_other_chips(x, y)):
                if col_kind[a]:
                    cs = land.shape[2]
                    piece = src.at[:, pl.ds(pl.multiple_of((2 * px + py) * cs, 128), cs)]
                else:
                    piece = src.at[2 * px + py]
                copies.append((piece, land.at[j], (px, py, c)))
        return copies
    return plan


def _pair_swap(name, halves):
    n = len(halves)

    def body(*refs):
        ins, outs = refs[:n], refs[n:2 * n]
        send_sems, recv_sems = refs[2 * n:]
        x, y, c = _place()
        copies = []
        for a in range(n):
            cp = pltpu.make_async_remote_copy(
                src_ref=ins[a], dst_ref=outs[a], send_sem=send_sems.at[a], recv_sem=recv_sems.at[a],
                device_id=(x, y, 1 - c), device_id_type=MESH)
            cp.start()
            copies.append(cp)
        for cp in copies:
            cp.wait()

    return pl.pallas_call(
        body, name=name, in_specs=[ANY] * n, out_specs=[ANY] * n,
        out_shape=[jax.ShapeDtypeStruct(s.shape, s.dtype) for s in halves],
        scratch_shapes=[pltpu.SemaphoreType.DMA((n,)), pltpu.SemaphoreType.DMA((n,))],
        compiler_params=pltpu.CompilerParams(has_side_effects=True),
    )(*halves)


def _chip_sum(name, chip_sel, own, col, others):
    _, r, c = others.shape
    tr = _pick(r, (256, 128, 64, 32, 16))
    if col:
        own_spec = pl.BlockSpec((tr, c), lambda i, s: (i, s[0]))
    else:
        own_spec = pl.BlockSpec((None, tr, c), lambda i, s: (s[0], i, 0))
    specs = [own_spec] + [pl.BlockSpec((None, tr, c), lambda i, s, k=k: (k, i, 0)) for k in range(3)]

    def body(s_ref, own_ref, r0, r1, r2, o_ref):
        o_ref[...] = ((own_ref[...].astype(F32) + r0[...].astype(F32)) + r1[...].astype(F32)) + r2[...].astype(F32)

    return pl.pallas_call(
        body, name=name,
        grid_spec=pltpu.PrefetchScalarGridSpec(
            num_scalar_prefetch=1, grid=(r // tr,), in_specs=specs,
            out_specs=pl.BlockSpec((tr, c), lambda i, s: (i, 0))),
        out_shape=jax.ShapeDtypeStruct((r, c), F32),
        compiler_params=_params(("parallel",)),
    )(chip_sel, own, others, others, others)


def _allreduce_small(vals):
    sizes = [int(math.prod(v.shape)) for v in vals]
    padded = [-(-s // 128) * 128 for s in sizes]
    total = -(-sum(padded) // 1024) * 1024
    flat = [jnp.pad(v.reshape(-1), (0, p - s)) for v, s, p in zip(vals, sizes, padded)]
    flat.append(jnp.zeros((total - sum(padded),), F32))
    packed = jnp.concatenate(flat).reshape(total // 128, 128)

    def body(in_ref, out_ref, r0, r1, r2, send_sems, recv_sems):
        x, y, c = _place()
        out_ref[...] = in_ref[...]
        for k, (peer, land) in enumerate(zip([(x, y, 1 - c), (1 - x, y, c), (x, 1 - y, c)], (r0, r1, r2))):
            cp = pltpu.make_async_remote_copy(
                src_ref=out_ref, dst_ref=land, send_sem=send_sems.at[k], recv_sem=recv_sems.at[k],
                device_id=peer, device_id_type=MESH)
            cp.start()
            cp.wait()
            out_ref[...] = out_ref[...] + land[...]

    vm = pl.BlockSpec(memory_space=pltpu.VMEM)
    summed = pl.pallas_call(
        body, name="allreduce_small", in_specs=[vm], out_specs=vm,
        out_shape=jax.ShapeDtypeStruct(packed.shape, F32),
        scratch_shapes=[pltpu.VMEM(packed.shape, F32)] * 3
        + [pltpu.SemaphoreType.DMA((3,)), pltpu.SemaphoreType.DMA((3,))],
        compiler_params=pltpu.CompilerParams(has_side_effects=True, vmem_limit_bytes=VMEM_LIMIT_BYTES),
    )(packed).reshape(-1)
    outs, off = [], 0
    for v, s, p in zip(vals, sizes, padded):
        outs.append(summed[off:off + s].reshape(v.shape))
        off += p
    return outs


def _adamw_math(w, g, m, v):
    m2 = ADAM_B1 * m + (1.0 - ADAM_B1) * g
    v2 = ADAM_B2 * v + (1.0 - ADAM_B2) * (g * g)
    m_hat = m2 / (1.0 - ADAM_B1 ** ADAM_STEP)
    v_hat = v2 / (1.0 - ADAM_B2 ** ADAM_STEP)
    delta = -ADAM_LR * (m_hat / (jnp.sqrt(v_hat) + ADAM_EPS) + ADAM_WD * w)
    return delta, m2, v2


def _adamw_big(name, w, g_mine, g_sibling, m, v):
    _, r, c = w.shape

    def body(w_ref, ga_ref, gb_ref, m_ref, v_ref, go_ref, d_ref, mo_ref, vo_ref):
        gv = ga_ref[...] + gb_ref[...]
        d, m2, v2 = _adamw_math(w_ref[...], gv, m_ref[...], v_ref[...])
        go_ref[...] = gv
        d_ref[...] = d
        mo_ref[...] = m2
        vo_ref[...] = v2

    tr = _pick(r, (256, 128, 64, 32, 16, 8))
    if r % tr == 0 and tr % 8 == 0:
        grid = (r // tr,)
        blk = pl.BlockSpec((None, tr, c), lambda i: (0, i, 0))
        part = pl.BlockSpec((tr, c), lambda i: (i, 0))
    else:
        grid = (c // 512,)
        blk = pl.BlockSpec((None, r, 512), lambda i: (0, 0, i))
        part = pl.BlockSpec((r, 512), lambda i: (0, i))
    return pl.pallas_call(
        body, name=name, grid=grid, in_specs=[blk, part, part, blk, blk], out_specs=[blk] * 4,
        out_shape=[jax.ShapeDtypeStruct((1, r, c), F32)] * 4, compiler_params=_params(("parallel",)),
    )(w, g_mine, g_sibling, m, v)


def _adamw_small(ws, gs, ms, vs):
    n = len(ws)

    def body(*refs):
        w_r, g_r, m_r, v_r = refs[:n], refs[n:2 * n], refs[2 * n:3 * n], refs[3 * n:4 * n]
        o = refs[4 * n:]
        for a in range(n):
            gv = g_r[a][...]
            d, m2, v2 = _adamw_math(w_r[a][...], gv, m_r[a][...], v_r[a][...])
            o[a][...] = gv
            o[n + a][...] = d
            o[2 * n + a][...] = m2
            o[3 * n + a][...] = v2

    res = pl.pallas_call(
        body, name="adamw_small", out_shape=[jax.ShapeDtypeStruct(w.shape, F32) for _ in range(4) for w in ws],
        compiler_params=_params(),
    )(*ws, *gs, *ms, *vs)
    return res[:n], res[n:2 * n], res[2 * n:3 * n], res[3 * n:]


def _full_from_gathered(name, gathered):
    if name == "w_in":
        rows = gathered.shape[0] // 4
        return gathered.reshape(4, rows, gathered.shape[1]).transpose(1, 0, 2).reshape(rows, 4 * gathered.shape[1])
    return gathered


def _reduce_layout(name, full):
    if name in COL_KIND:
        return full
    if name == "w_in":
        rows, cols = full.shape
        return full.reshape(rows, 4, cols // 4).transpose(1, 0, 2)
    return full.reshape(4, full.shape[0] // 4, full.shape[1])


def kernel(x, mem, norm_mix, w_in, fox_q_norm, fox_k_norm, fox_f_bias, s5_a_re, s5_a_im, s5_log_dt, s5_b_re, s5_b_im, s5_c_re, s5_c_im, s5_d, s5_w_glu, s5_b_glu, out_norm_fox, out_norm_s5, w_out, norm_cross, norm_mem, w_xq, w_xkv, xq_norm, xk_norm, w_xo, norm_ffn, w_ffn_up, ffn_conv_w, ffn_conv_b, w_ffn_down, loss_target, m_norm_mix, m_w_in, m_fox_q_norm, m_fox_k_norm, m_fox_f_bias, m_s5_a_re, m_s5_a_im, m_s5_log_dt, m_s5_b_re, m_s5_b_im, m_s5_c_re, m_s5_c_im, m_s5_d, m_s5_w_glu, m_s5_b_glu, m_out_norm_fox, m_out_norm_s5, m_w_out, m_norm_cross, m_norm_mem, m_w_xq, m_w_xkv, m_xq_norm, m_xk_norm, m_w_xo, m_norm_ffn, m_w_ffn_up, m_ffn_conv_w, m_ffn_conv_b, m_w_ffn_down, v_norm_mix, v_w_in, v_fox_q_norm, v_fox_k_norm, v_fox_f_bias, v_s5_a_re, v_s5_a_im, v_s5_log_dt, v_s5_b_re, v_s5_b_im, v_s5_c_re, v_s5_c_im, v_s5_d, v_s5_w_glu, v_s5_b_glu, v_out_norm_fox, v_out_norm_s5, v_w_out, v_norm_cross, v_norm_mem, v_w_xq, v_w_xkv, v_xq_norm, v_xk_norm, v_w_xo, v_norm_ffn, v_w_ffn_up, v_ffn_conv_w, v_ffn_conv_b, v_w_ffn_down):
    given = dict(locals())
    w = {n: given[n] for n in WEIGHTS}
    m = {n: given["m_" + n] for n in WEIGHTS}
    v = {n: given["v_" + n] for n in WEIGHTS}
    xi, yi, _ = _place()
    chip = (2 * xi + yi).astype(jnp.int32)
    chip_sel = chip.reshape(1)
    early_kind = [n in COL_KIND for n in EARLY_WEIGHTS]
    late_kind = [n in COL_KIND for n in LATE_WEIGHTS]

    gathered, taps = _gather_weights([w[FIRST_WEIGHT][0].astype(BF16)], [False], w["ffn_conv_w"][0])
    first_full = gathered[0]
    conv_w = taps.transpose(1, 0, 2).reshape(3, D_FF)
    pending = {}
    g_started = None
    for stage, names in (("mid", MID_WEIGHTS), ("late", LATE_WEIGHTS)):
        kinds = [n in COL_KIND for n in names]
        shards = [w[n][0].astype(BF16) for n in names]
        if g_started is None:
            first_full, shards[0] = lax.optimization_barrier((first_full, shards[0]))
        else:
            shards[0] = shards[0] + g_started[0:1, 0:1].astype(BF16)
        full = [lax.empty((s.shape[0], 4 * s.shape[1]) if ck else (4 * s.shape[0], s.shape[1]), BF16)
                for s, ck in zip(shards, kinds)]
        plan = _late_gather_plan(kinds)
        send, recv, srcs, lands, g_started = _split_start(
            "gather_" + stage + "_start", shards, full, 4 * len(names), plan)
        pending[stage] = (names, plan, send, recv, srcs, lands)
    wb = {FIRST_WEIGHT: _full_from_gathered(FIRST_WEIGHT, first_full)}

    def late_weights(stage, after):
        names, plan, send, recv, srcs, lands = pending[stage]
        _, full = _split_wait("gather_" + stage + "_wait", send, recv, srcs, lands, after, plan)
        return dict(zip(names, full))

    reduce_plan = _late_reduce_plan(late_kind)
    late_reduce = {}

    def early_grads(late_g):
        grads = [_reduce_layout(n, late_g[n]) for n in LATE_WEIGHTS]
        lands = [lax.empty((3, s.shape[0], s.shape[1] // 4) if ck else (3,) + s.shape[1:], BF16)
                 for s, ck in zip(grads, late_kind)]
        late_reduce["sems"] = _split_start("reduce_late_start", grads, lands, 3 * len(LATE_WEIGHTS), reduce_plan)
        return late_reduce["sems"][4][0:1, 0:1]

    p = {n: w[n][0] for n in SMALL}
    p["ffn_conv_w"] = conv_w
    for n in ("norm_mix", "fox_q_norm", "fox_k_norm", "fox_f_bias", "s5_b_glu", "out_norm_fox", "out_norm_s5",
              "norm_cross", "norm_mem", "xq_norm", "xk_norm", "norm_ffn", "ffn_conv_b"):
        p[n] = p[n].reshape(1, -1)
    p["norm_mix"] = p["norm_mix"] + g_started[0:1, 0:1]
    loss, grad_x, g = _local_step(x, mem, loss_target, p, wb, late_weights, early_grads)

    grads = [_reduce_layout(n, g[n].astype(BF16)) for n in EARLY_WEIGHTS]
    early_lands = [lax.empty((3, s.shape[0], s.shape[1] // 4) if ck else (3,) + s.shape[1:], BF16)
                   for s, ck in zip(grads, early_kind)]
    early_plan = _late_reduce_plan(early_kind)
    e_send, e_recv, e_srcs, e_lands, e_started = _split_start(
        "reduce_early_start", grads, early_lands, 3 * len(EARLY_WEIGHTS), early_plan)

    out_g, out_d, out_m, out_v = {}, {}, {}, {}

    def finish(names, kinds, sums, from_chips, tag):
        mine = [_chip_sum("reduce_chip_sum_" + n, chip_sel, ps, ck, fc)
                for n, ps, fc, ck in zip(names, sums, from_chips, kinds)]
        theirs = _pair_swap("reduce_pair_swap_" + tag, mine)
        for n, a, b in zip(names, mine, theirs):
            if n == "w_in":
                flip = lambda t: jnp.swapaxes(t, -1, -2)
                res = _adamw_big("adamw_" + n, flip(w[n]), flip(a), flip(b), flip(m[n]), flip(v[n]))
                out_g[n], out_d[n], out_m[n], out_v[n] = (flip(t) for t in res)
                continue
            out_g[n], out_d[n], out_m[n], out_v[n] = _adamw_big("adamw_" + n, w[n], a, b, m[n], v[n])

    r_send, r_recv, r_srcs, r_lands, _ = late_reduce["sems"]
    late_sums, late_from_chips = _split_wait("reduce_late_wait", r_send, r_recv, r_srcs, r_lands, e_started,
                                             reduce_plan)
    finish(LATE_WEIGHTS, late_kind, late_sums, late_from_chips, "late")

    small_names = list(SMALL) + ["ffn_conv_w"]
    small_vals = [g[n].reshape(w[n].shape if n != "ffn_conv_w" else (1, 3, D_FF)) for n in small_names]
    last = LATE_WEIGHTS[-1]
    loss, out_v[last] = lax.optimization_barrier((loss, out_v[last]))
    reduced = _allreduce_small(small_vals + [loss])
    loss_all = reduced[-1].reshape(())
    conv_w_grad = lax.dynamic_slice_in_dim(reduced[-2], chip * (D_FF // 4), D_FF // 4, axis=2)
    sg, sd, sm, sv = _adamw_small(
        [w[n] for n in small_names], list(reduced[:len(SMALL)]) + [conv_w_grad],
        [m[n] for n in small_names], [v[n] for n in small_names])
    out_g.update(zip(small_names, sg))
    out_d.update(zip(small_names, sd))
    out_m.update(zip(small_names, sm))
    out_v.update(zip(small_names, sv))

    early_sums, early_from_chips = _split_wait("reduce_early_wait", e_send, e_recv, e_srcs, e_lands, reduced[0],
                                               early_plan)
    finish(EARLY_WEIGHTS, early_kind, early_sums, early_from_chips, "early")

    return (loss_all, grad_x, *[out_g[n] for n in WEIGHTS], *[out_d[n] for n in WEIGHTS],
            *[out_m[n] for n in WEIGHTS], *[out_v[n] for n in WEIGHTS])
```

```python
import math

import jax
import jax.numpy as jnp
from jax import lax
from jax.experimental import pallas as pl
from jax.experimental.pallas import tpu as pltpu

F32 = jnp.float32
BF16 = jnp.bfloat16

D_MODEL = 1024
FOX_WIDTH = 512
HEAD_DIM = 64
N_FOX_HEADS = 8
S5_WIDTH = 512
S5_GROUP_CH = 16
S5_GROUPS = 32
S5_STATE = 64
S5_CH = S5_GROUPS * S5_STATE
N_X_HEADS = 4
X_HEAD_DIM = 256
N_MEM = 256
D_FF = 2816
UF_COLS = 640
EPS = 1e-6
ADAM_LR = 0.001
ADAM_B1 = 0.9
ADAM_B2 = 0.999
ADAM_EPS = 1e-08
ADAM_WD = 0.01
ADAM_STEP = 10

VMEM_LIMIT_BYTES = 56 * 1024 * 1024
MM_BLOCK_BYTES = 6 * 1024 * 1024
MM_VMEM_BYTES = 40 * 1024 * 1024
MM_TILE_MAX = 1536
MESH = pl.DeviceIdType.MESH

FIRST_WEIGHT = "w_in"
MID_WEIGHTS = ("s5_w_glu", "w_out")
EARLY_WEIGHTS = (FIRST_WEIGHT,) + MID_WEIGHTS
LATE_WEIGHTS = ("w_xq", "w_xkv", "w_xo", "w_ffn_up", "w_ffn_down")
BIG = EARLY_WEIGHTS + LATE_WEIGHTS
COL_KIND = ("w_xkv", "w_ffn_up")
SMALL = ("norm_mix", "fox_q_norm", "fox_k_norm", "fox_f_bias", "s5_a_re", "s5_a_im", "s5_log_dt",
         "s5_b_re", "s5_b_im", "s5_c_re", "s5_c_im", "s5_d", "s5_b_glu", "out_norm_fox", "out_norm_s5",
         "norm_cross", "norm_mem", "xq_norm", "xk_norm", "norm_ffn", "ffn_conv_b")
WEIGHTS = ("norm_mix", "w_in", "fox_q_norm", "fox_k_norm", "fox_f_bias", "s5_a_re", "s5_a_im", "s5_log_dt",
           "s5_b_re", "s5_b_im", "s5_c_re", "s5_c_im", "s5_d", "s5_w_glu", "s5_b_glu", "out_norm_fox",
           "out_norm_s5", "w_out", "norm_cross", "norm_mem", "w_xq", "w_xkv", "xq_norm", "xk_norm", "w_xo",
           "norm_ffn", "w_ffn_up", "ffn_conv_w", "ffn_conv_b", "w_ffn_down")


def _params(sem=None):
    return pltpu.CompilerParams(dimension_semantics=sem, vmem_limit_bytes=VMEM_LIMIT_BYTES)


def _pick(n, cands):
    for c in cands:
        if n % c == 0:
            return c
    return n


_DIMS = {"nn": (((1,), (0,)), ((), ())), "nt": (((1,), (1,)), ((), ())), "tn": (((0,), (0,)), ((), ()))}


def _mm(a, b, mode, name, out_dtype=F32, res=None):
    if mode == "nn":
        (m, k), (k2, n) = a.shape, b.shape
    elif mode == "nt":
        (m, k), (n, k2) = a.shape, b.shape
    else:
        (k, m), (k2, n) = a.shape, b.shape
    assert k == k2, (name, a.shape, b.shape)

    has_res = res is not None
    a_size, b_size = a.dtype.itemsize, b.dtype.itemsize
    o_size = jnp.dtype(out_dtype).itemsize + (res.dtype.itemsize if has_res else 0)

    def tiles(dim):
        return [c for c in range(MM_TILE_MAX, 0, -128) if dim % c == 0] or [dim]

    best = None
    for tm in tiles(m):
        for tn in tiles(n):
            a_blk, b_blk = tm * k * a_size, tn * k * b_size
            if max(a_blk, b_blk) > MM_BLOCK_BYTES or 2 * (a_blk + b_blk + tm * tn * o_size) > MM_VMEM_BYTES:
                continue
            for rows_outer in (True, False):
                moved = (m * k * a_size + (m // tm) * n * k * b_size) if rows_outer else \
                        (n * k * b_size + (n // tn) * m * k * a_size)
                key = (moved, -(tm * tn))
                if best is None or key < best[0]:
                    best = (key, tm, tn, rows_outer)
    assert best is not None, (name, a.shape, b.shape)
    _, tm, tn, rows_outer = best
    ij = (lambda g0, g1: (g0, g1)) if rows_outer else (lambda g0, g1: (g1, g0))
    if mode == "tn":
        a_spec = pl.BlockSpec((k, tm), lambda g0, g1: (0, ij(g0, g1)[0]))
    else:
        a_spec = pl.BlockSpec((tm, k), lambda g0, g1: (ij(g0, g1)[0], 0))
    if mode == "nt":
        b_spec = pl.BlockSpec((tn, k), lambda g0, g1: (ij(g0, g1)[1], 0))
    else:
        b_spec = pl.BlockSpec((k, tn), lambda g0, g1: (0, ij(g0, g1)[1]))
    o_spec = pl.BlockSpec((tm, tn), lambda g0, g1: ij(g0, g1))
    grid = (m // tm, n // tn) if rows_outer else (n // tn, m // tm)
    dims = _DIMS[mode]

    def body(*refs):
        a_ref, b_ref = refs[0], refs[1]
        o_ref = refs[-1]
        acc = lax.dot_general(a_ref[...].astype(BF16), b_ref[...].astype(BF16), dims, preferred_element_type=F32)
        if has_res:
            acc = acc + refs[2][...].astype(F32)
        o_ref[...] = acc.astype(o_ref.dtype)

    return pl.pallas_call(
        body, name=name, grid=grid,
        in_specs=[a_spec, b_spec] + ([o_spec] if has_res else []),
        out_specs=o_spec, out_shape=jax.ShapeDtypeStruct((m, n), out_dtype),
        compiler_params=_params(("parallel", "parallel")),
    )(*((a, b, res) if has_res else (a, b)))


def _row_spec(tm, bc, off, step):
    return pl.BlockSpec((tm, bc), lambda i, h: (i, off + step * h))


ROW_TILE_ELEMS = 512 * 1024


def _row_tile(t, rows):
    widest = max(bc for (_, bc, _, _) in rows)
    return _pick(t, (min(t, ROW_TILE_ELEMS // widest), 512, 256, 128, 64, 8))


def _rowwise(fn, rows, pars, outs, name, heads=1):
    t = rows[0][0].shape[0]
    tm = _row_tile(t, rows)
    nr, npar = len(rows), len(pars)

    def body(*refs):
        vals = [r[...].astype(F32) for r in refs[:nr + npar]]
        res = fn(*vals)
        if not isinstance(res, (tuple, list)):
            res = (res,)
        for o_ref, v in zip(refs[nr + npar:], res):
            o_ref[...] = v.astype(o_ref.dtype)

    in_specs = [_row_spec(tm, bc, off, st) for (_, bc, off, st) in rows]
    in_specs += [pl.BlockSpec(p.shape, lambda i, h: (0, 0)) for p in pars]
    out_specs = [_row_spec(tm, bc, 0, st) for (_, bc, st, _) in outs]
    out_shape = [jax.ShapeDtypeStruct((t, c), dt) for (c, _, _, dt) in outs]
    res = pl.pallas_call(
        body, name=name, grid=(t // tm, heads), in_specs=in_specs, out_specs=out_specs, out_shape=out_shape,
        compiler_params=_params(("parallel", "parallel")),
    )(*[r[0] for r in rows], *pars)
    return res[0] if len(res) == 1 else res


def _rowwise_vjp(fn, rows, pars, cts, name, heads=1, adds=None, row_dtypes=None):
    t = rows[0][0].shape[0]
    tm = _row_tile(t, rows)
    nr, npar, nct = len(rows), len(pars), len(cts)
    adds = adds or [None] * nr
    add_list = [a for a in adds if a is not None]
    row_dtypes = row_dtypes or [F32] * nr

    def body(*refs):
        i, h = pl.program_id(0), pl.program_id(1)
        p = 0
        row_v = [r[...].astype(F32) for r in refs[p:p + nr]]; p += nr
        par_v = [r[...].astype(F32) for r in refs[p:p + npar]]; p += npar
        ct_v = [r[...].astype(F32) for r in refs[p:p + nct]]; p += nct
        add_refs = refs[p:p + len(add_list)]; p += len(add_list)
        drow_refs = refs[p:p + nr]; p += nr
        dpar_refs = refs[p:p + npar]

        def wrapped(*a):
            r = fn(*a)
            return tuple(r) if isinstance(r, (tuple, list)) else (r,)

        _, pull = jax.vjp(wrapped, *row_v, *par_v)
        grads = pull(tuple(ct_v))
        ai = 0
        for k in range(nr):
            g = grads[k]
            if adds[k] is not None:
                g = g + add_refs[ai][...].astype(F32)
                ai += 1
            drow_refs[k][...] = g.astype(drow_refs[k].dtype)

        @pl.when((i == 0) & (h == 0))
        def _():
            for r in dpar_refs:
                r[...] = jnp.zeros(r.shape, r.dtype)

        for k in range(npar):
            dpar_refs[k][...] += grads[nr + k]

    in_specs = [_row_spec(tm, bc, off, st) for (_, bc, off, st) in rows]
    in_specs += [pl.BlockSpec(q.shape, lambda i, h: (0, 0)) for q in pars]
    in_specs += [_row_spec(tm, bc, off, st) for (_, bc, off, st) in cts]
    in_specs += [_row_spec(tm, bc, off, st) for (_, bc, off, st) in add_list]
    out_specs = [_row_spec(tm, bc, 0, st) for (_, bc, _, st) in rows]
    out_specs += [pl.BlockSpec(q.shape, lambda i, h: (0, 0)) for q in pars]
    out_shape = [jax.ShapeDtypeStruct((t, bc * (heads if st else 1)), dt) for (_, bc, _, st), dt in zip(rows, row_dtypes)]
    out_shape += [jax.ShapeDtypeStruct(q.shape, F32) for q in pars]
    res = pl.pallas_call(
        body, name=name, grid=(t // tm, heads), in_specs=in_specs, out_specs=out_specs, out_shape=out_shape,
        compiler_params=_params(("arbitrary", "arbitrary")),
    )(*[r[0] for r in rows], *pars, *[c[0] for c in cts], *[a[0] for a in add_list])
    return list(res[:nr]), list(res[nr:])


def _rms(x, g):
    return x * lax.rsqrt(jnp.mean(x * x, axis=-1, keepdims=True) + EPS) * g


def _rms_pair(x, g):
    left = lax.broadcasted_iota(jnp.int32, x.shape, 1) < HEAD_DIM
    x2 = x * x
    ms_a = jnp.sum(jnp.where(left, x2, 0.0), axis=-1, keepdims=True) * (1.0 / HEAD_DIM)
    ms_b = jnp.sum(jnp.where(left, 0.0, x2), axis=-1, keepdims=True) * (1.0 / HEAD_DIM)
    return x * lax.rsqrt(jnp.where(left, ms_a, ms_b) + EPS) * g


def _gelu(x):
    return 0.5 * x * (1.0 + jnp.tanh(math.sqrt(2.0 / math.pi) * (x + 0.044715 * (x * x * x))))


def _s5_act(ys, u, d):
    return _gelu(ys + d * u)


def _s5_gate(yg, z, b, g):
    return _rms(yg * jax.nn.sigmoid(z + b), g)


def _lane_cumsum(x, reverse):
    n = x.shape[-1]
    lane = lax.broadcasted_iota(jnp.int32, x.shape, 1)
    k = 1
    while k < n:
        if reverse:
            x = x + jnp.where(lane < n - k, pltpu.roll(x, n - k, 1), 0.0)
        else:
            x = x + jnp.where(lane >= k, pltpu.roll(x, k, 1), 0.0)
        k *= 2
    return x


def _log_sigmoid(z):
    return jnp.minimum(z, 0.0) - jnp.log(1.0 + jnp.exp(-jnp.abs(z)))


def _forget_fwd(f, bias):
    def body(f_ref, b_ref, c_ref):
        c_ref[...] = _lane_cumsum(_log_sigmoid(f_ref[...] + b_ref[...]), False)

    return pl.pallas_call(body, name="forget_fwd", out_shape=jax.ShapeDtypeStruct(f.shape, F32),
                          compiler_params=_params())(f, bias)


def _forget_bwd(f, bias, dc):
    def body(f_ref, b_ref, dc_ref, df_ref, db_ref):
        dlog = _lane_cumsum(dc_ref[...], True)
        df = dlog * jax.nn.sigmoid(-(f_ref[...] + b_ref[...]))
        df_ref[...] = df
        db_ref[...] = jnp.sum(df, axis=1, keepdims=True)

    return pl.pallas_call(body, name="forget_bwd",
                          out_shape=(jax.ShapeDtypeStruct(f.shape, F32), jax.ShapeDtypeStruct(bias.shape, F32)),
                          compiler_params=_params())(f, bias, dc)


FOX_BLOCK = 1024
FOX_KEYS = 512
FOX_BWD_BLOCK = 512
_NT = _DIMS["nt"]
_TN = _DIMS["tn"]


N_PAIRS = N_FOX_HEADS // 2
V_BLOCK0 = 2 * N_PAIRS


def _left_lanes(shape):
    return lax.broadcasted_iota(jnp.int32, shape, 1) < HEAD_DIM


def _top_rows(shape):
    return lax.broadcasted_iota(jnp.int32, shape, 0) < HEAD_DIM


def _wide(c_tile, n):
    return c_tile if n == 128 else jnp.concatenate([c_tile] * (n // 128), axis=1)


def _fox_fwd(qn, kn, qkv, c_wide, seqs):
    t = qn.shape[0]
    l = t // seqs
    tb = min(FOX_BLOCK, l)
    tk = min(FOX_KEYS, tb)
    ratio = tb // tk
    nb = l // tb
    scale = HEAD_DIM ** -0.5

    def body(q_ref, k_ref, v_ref, ca_ref, cb_ref, o_ref, lse_ref, vt_ref):
        i = pl.program_id(2)
        top = _top_rows((128, tb))

        @pl.when(i == 0)
        def _():
            vt_ref[...] = v_ref[...].T.astype(BF16)

        qt = (q_ref[...].astype(F32) * scale).T.astype(BF16)
        zero = jnp.zeros_like(qt)
        qts = (jnp.where(top, qt, zero), jnp.where(top, zero, qt))
        top_k = _top_rows((128, tk))
        zero_k = jnp.zeros((128, tk), BF16)
        key_pos = lax.broadcasted_iota(jnp.int32, (tk, tb), 0)
        query_pos = lax.broadcasted_iota(jnp.int32, (tk, tb), 1)
        c_refs = (ca_ref, cb_ref)

        def scores(j):
            off = pl.multiple_of(j * tk, tk)
            k2 = k_ref[pl.ds(off, tk), :]
            return tuple(jnp.dot(k2, qts[h], preferred_element_type=F32) - _wide(c_refs[h][pl.ds(off, tk), :], tb)
                         for h in (0, 1))

        def values_times(ps, j):
            vt = vt_ref[:, pl.ds(pl.multiple_of(j * tk, tk), tk)]
            return (jnp.dot(jnp.where(top_k, vt, zero_k), ps[0], preferred_element_type=F32)
                    + jnp.dot(jnp.where(top_k, zero_k, vt), ps[1], preferred_element_type=F32))

        def softmax_step(sts, stats, first_key):
            ps, new, alphas = [], [], []
            for st, (m, s_sum) in zip(sts, stats):
                if first_key is not None:
                    st = jnp.where(key_pos + first_key <= query_pos, st, -jnp.inf)
                m_new = jnp.maximum(m, jnp.max(st, axis=0, keepdims=True))
                alpha = jnp.exp(m - m_new)
                p = jnp.exp(st - m_new)
                new.append((m_new, alpha * s_sum + jnp.sum(p, axis=0, keepdims=True)))
                alphas.append(alpha)
                ps.append(p.astype(BF16))
            return tuple(ps), tuple(new), jnp.where(top, alphas[0], alphas[1])

        def tile(j, carry, first_key):
            stats, acc = carry
            ps, stats, alpha = softmax_step(scores(j), stats, first_key)
            return stats, alpha * acc + values_times(ps, j)

        stat = (jnp.full((1, tb), -jnp.inf, F32), jnp.zeros((1, tb), F32))
        below = i * ratio
        carry = lax.fori_loop(0, below, lambda j, c: tile(j, c, None), ((stat, stat), jnp.zeros((128, tb), F32)))
        for r in range(ratio):
            carry = tile(below + r, carry, r * tk)
        ((ma, sa), (mb, sb)), acc = carry
        o_ref[...] = (acc / jnp.where(top, sa, sb)).T
        lse_ref[0:1, :] = ma + jnp.log(sa)
        lse_ref[1:2, :] = mb + jnp.log(sb)

    qblk = pl.BlockSpec((tb, 128), lambda b, hp, i: (b * nb + i, hp))
    return pl.pallas_call(
        body, name="fox_fwd", grid=(seqs, N_PAIRS, nb),
        in_specs=[qblk, pl.BlockSpec((l, 128), lambda b, hp, i: (b, hp)),
                  pl.BlockSpec((l, 128), lambda b, hp, i: (b, V_BLOCK0 + hp)),
                  pl.BlockSpec((None, l, 128), lambda b, hp, i: (b * N_FOX_HEADS + 2 * hp, 0, 0)),
                  pl.BlockSpec((None, l, 128), lambda b, hp, i: (b * N_FOX_HEADS + 2 * hp + 1, 0, 0))],
        out_specs=[qblk, pl.BlockSpec((None, 2, tb), lambda b, hp, i: (b * N_PAIRS + hp, 0, i))],
        out_shape=[jax.ShapeDtypeStruct((t, FOX_WIDTH), F32), jax.ShapeDtypeStruct((seqs * N_PAIRS, 2, l), F32)],
        scratch_shapes=[pltpu.VMEM((128, l), BF16)],
        compiler_params=_params(("parallel", "parallel", "arbitrary")),
    )(qn, kn, qkv, c_wide, c_wide)


def _fox_bwd(qn, kn, qkv, c_wide, o, do, lse, seqs):
    t = qn.shape[0]
    l = t // seqs
    tb = min(FOX_BWD_BLOCK, l)
    nb = l // tb
    scale = HEAD_DIM ** -0.5
    one_at = (HEAD_DIM, 0)

    def body(q_ref, k_ref, v_ref, ca_ref, cb_ref, o_ref, do_ref, lse_ref, dq_ref, dk_ref, dv_ref, dc_ref, dcq_ref,
             qt_ref, kt_ref, dot_ref, delta_ref, dqa_ref, dqb_ref):
        top_l = _top_rows((128, l))
        top = _top_rows((128, tb))
        left = _left_lanes((tb, 128))
        row_id = lax.broadcasted_iota(jnp.int32, (128, tb), 0)
        lane_id = lax.broadcasted_iota(jnp.int32, (tb, 128), 1)
        zero_t = jnp.zeros((128, tb), BF16)
        zero_l = jnp.zeros((tb, 128), BF16)
        rows = lambda a: (jnp.where(top, a, zero_t), jnp.where(top, zero_t, a))
        lanes = lambda a: (jnp.where(left, a, zero_l), jnp.where(left, zero_l, a))
        with_one_row = lambda pair: tuple(jnp.where(row_id == one_at[h], 1.0, pair[h]).astype(BF16) for h in (0, 1))
        with_one_lane = lambda pair: tuple(jnp.where(lane_id == one_at[h], 1.0, pair[h]).astype(BF16) for h in (0, 1))
        causal = lax.broadcasted_iota(jnp.int32, (tb, tb), 0) <= lax.broadcasted_iota(jnp.int32, (tb, tb), 1)
        c_refs = (ca_ref, cb_ref)
        dq_refs = (dqa_ref, dqb_ref)

        qt_ref[...] = (q_ref[...].astype(F32) * scale).T.astype(BF16)
        kt_ref[...] = k_ref[...].astype(F32).T.astype(BF16)
        do_t = do_ref[...].T
        dot_ref[...] = do_t.astype(BF16)
        prod_t = do_t * o_ref[...].T
        delta_ref[0:1, :] = jnp.sum(jnp.where(top_l, prod_t, 0.0), axis=0, keepdims=True)
        delta_ref[1:2, :] = jnp.sum(jnp.where(top_l, 0.0, prod_t), axis=0, keepdims=True)
        dqa_ref[...] = jnp.zeros(dqa_ref.shape, F32)
        dqb_ref[...] = jnp.zeros(dqb_ref.shape, F32)

        def kv_block(j, _):
            koff = pl.multiple_of(j * tb, tb)
            k2 = k_ref[pl.ds(koff, tb), :]
            v2 = v_ref[pl.ds(koff, tb), :].astype(BF16)
            kts = with_one_row(rows(kt_ref[:, pl.ds(koff, tb)]))
            cw = tuple(_wide(c_refs[h][pl.ds(koff, tb), :], tb) for h in (0, 1))

            def q_block(i, carry, masked):
                dks, dv = list(carry[:2]), carry[2]
                qoff = pl.multiple_of(i * tb, tb)
                qs = lanes((q_ref[pl.ds(qoff, tb), :].astype(F32) * scale).astype(BF16))
                qs_one = with_one_lane(qs)
                dos = lanes(do_ref[pl.ds(qoff, tb), :].astype(BF16))
                qts = rows(qt_ref[:, pl.ds(qoff, tb)])
                dots = rows(dot_ref[:, pl.ds(qoff, tb)])
                for h in (0, 1):
                    st = jnp.dot(k2, qts[h], preferred_element_type=F32) - cw[h]
                    p = jnp.exp(st - lse_ref[h:h + 1, pl.ds(qoff, tb)])
                    if masked:
                        p = jnp.where(causal, p, 0.0)
                    dp = jnp.dot(v2, dots[h], preferred_element_type=F32)
                    dsb = (p * (dp - delta_ref[h:h + 1, pl.ds(qoff, tb)])).astype(BF16)
                    dv = dv + jnp.dot(p.astype(BF16), dos[h], preferred_element_type=F32)
                    dks[h] = dks[h] + jnp.dot(dsb, qs_one[h], preferred_element_type=F32)
                    dq_refs[h][:, pl.ds(qoff, tb)] += jnp.dot(kts[h], dsb, preferred_element_type=F32)
                return dks[0], dks[1], dv

            z = jnp.zeros((tb, 128), F32)
            carry = q_block(j, (z, z, z), True)
            rest = nb - 1 - j
            carry = lax.fori_loop(
                0, rest // 2, lambda n, c: q_block(j + 2 + 2 * n, q_block(j + 1 + 2 * n, c, False), False), carry)
            dka, dkb, dv = lax.cond(rest % 2 == 1, lambda c: q_block(nb - 1, c, False), lambda c: c, carry)
            dk_ref[pl.ds(koff, tb), :] = jnp.where(left, dka, dkb)
            dv_ref[pl.ds(koff, tb), :] = dv
            dc_ref[0:1, pl.ds(koff, tb)] = -dka.T[one_at[0]:one_at[0] + 1, :]
            dc_ref[1:2, pl.ds(koff, tb)] = -dkb.T[one_at[1]:one_at[1] + 1, :]
            return 0

        lax.fori_loop(0, nb, kv_block, 0)
        dq_ref[...] = (jnp.where(top_l, dqa_ref[...], dqb_ref[...]) * scale).T
        dcq_ref[0:1, :] = dqa_ref[one_at[0]:one_at[0] + 1, :]
        dcq_ref[1:2, :] = dqb_ref[one_at[1]:one_at[1] + 1, :]

    blk = pl.BlockSpec((l, 128), lambda b, hp: (b, hp))
    cspec = lambda k: pl.BlockSpec((None, l, 128), lambda b, hp: (b * N_FOX_HEADS + 2 * hp + k, 0, 0))
    rows2 = pl.BlockSpec((None, 2, l), lambda b, hp: (b * N_PAIRS + hp, 0, 0))
    wide = jax.ShapeDtypeStruct((t, FOX_WIDTH), F32)
    pair_rows = jax.ShapeDtypeStruct((seqs * N_PAIRS, 2, l), F32)
    return pl.pallas_call(
        body, name="fox_bwd", grid=(seqs, N_PAIRS),
        in_specs=[blk, blk, pl.BlockSpec((l, 128), lambda b, hp: (b, V_BLOCK0 + hp)), cspec(0), cspec(1), blk, blk, rows2],
        out_specs=[blk, blk, blk, rows2, rows2],
        out_shape=[wide, wide, wide, pair_rows, pair_rows],
        scratch_shapes=[pltpu.VMEM((128, l), BF16), pltpu.VMEM((128, l), BF16), pltpu.VMEM((128, l), BF16),
                        pltpu.VMEM((2, l), F32), pltpu.VMEM((128, l), F32), pltpu.VMEM((128, l), F32)],
        compiler_params=_params(("parallel", "parallel")),
    )(qn, kn, qkv, c_wide, c_wide, o, do, lse)


SCAN_ROWS = 512
SCAN_COLS = 1024


S5_IN = 128
S5_ST = 512
SCAN_CHUNKS = SCAN_COLS // S5_ST
SCAN_SEGS = 8
LANES = 128


def _cmul(ar, ai, br, bi):
    return ar * br - ai * bi, ar * bi + ai * br


def _powers_into(pw_r, pw_i, a_r, a_i, seg):
    pw_r[0:1, :] = a_r
    pw_i[0:1, :] = a_i
    for k in range(1, seg):
        pr, pi = _cmul(pw_r[k - 1:k, :], pw_i[k - 1:k, :], a_r, a_i)
        pw_r[k:k + 1, :] = pr
        pw_i[k:k + 1, :] = pi


def _interleave(dst, src, seg):
    for h in range(src.shape[0]):
        for j in range(seg):
            dst[h, j * SCAN_SEGS:(j + 1) * SCAN_SEGS, :] = src[h, pl.ds(j, SCAN_SEGS, stride=seg), :]


def _deinterleave(dst, src, seg):
    for h in range(src.shape[0]):
        for j in range(seg):
            dst[h, pl.ds(j, SCAN_SEGS, stride=seg), :] = src[h, j * SCAN_SEGS:(j + 1) * SCAN_SEGS, :]


def _interleaved(ref, tmp_a, tmp_b, seg):
    n = ref.shape[1] // LANES
    for h in range(n):
        tmp_a[h] = ref[:, h * LANES:(h + 1) * LANES].astype(F32)
    _interleave(tmp_b, tmp_a, seg)
    return jnp.concatenate([tmp_b[h] for h in range(n)], axis=1)


def _store_deinterleaved(ref, val, tmp_a, tmp_b, seg):
    n = ref.shape[1] // LANES
    for h in range(n):
        tmp_a[h] = val[:, h * LANES:(h + 1) * LANES]
    _deinterleave(tmp_b, tmp_a, seg)
    for h in range(n):
        ref[:, h * LANES:(h + 1) * LANES] = tmp_b[h]


def _segment_scan(b_r, b_i, x_r, x_i, pw_r, pw_i, car_r, car_i, seg, sign, reverse, visit=None):
    nc = b_r.shape[0]
    sub = lax.broadcasted_iota(jnp.int32, (SCAN_SEGS, LANES), 0)
    lanes = lambda c: slice(c * LANES, (c + 1) * LANES)
    rows = lambda j: pl.ds(pl.multiple_of(((seg - 1 - j) if reverse else j) * SCAN_SEGS, SCAN_SEGS), SCAN_SEGS)
    a1 = [(pw_r[0:1, lanes(c)], sign * pw_i[0:1, lanes(c)]) for c in range(nc)]

    def local(j, xs):
        out = []
        for c in range(nc):
            xr, xi = xs[2 * c], xs[2 * c + 1]
            nr = a1[c][0] * xr - a1[c][1] * xi + b_r[c, rows(j), :]
            ni = a1[c][0] * xi + a1[c][1] * xr + b_i[c, rows(j), :]
            x_r[c, rows(j), :] = nr
            x_i[c, rows(j), :] = ni
            out += [nr, ni]
        return tuple(out)

    zero = jnp.zeros((SCAN_SEGS, LANES), F32)
    ends = lax.fori_loop(0, seg, local, (zero,) * (2 * nc))

    if reverse:
        first = sub == SCAN_SEGS - 1
        neighbour = lambda v: pltpu.roll(v, SCAN_SEGS - 1, 0)
        shift = lambda v, d: jnp.where(sub < SCAN_SEGS - d, pltpu.roll(v, SCAN_SEGS - d, 0), 0.0)
    else:
        first = sub == 0
        neighbour = lambda v: pltpu.roll(v, 1, 0)
        shift = lambda v, d: jnp.where(sub >= d, pltpu.roll(v, d, 0), 0.0)
    last = 0 if reverse else SCAN_SEGS - 1
    entries = []
    for c in range(nc):
        er, ei = ends[2 * c], ends[2 * c + 1]
        pr, pi = pw_r[seg - 1:seg, lanes(c)], sign * pw_i[seg - 1:seg, lanes(c)]
        yr = jnp.where(first, car_r[:, lanes(c)], neighbour(er))
        yi = jnp.where(first, car_i[:, lanes(c)], neighbour(ei))
        qr, qi = pr, pi
        for d in (1, 2, 4):
            mr, mi = _cmul(qr, qi, shift(yr, d), shift(yi, d))
            yr, yi = yr + mr, yi + mi
            qr, qi = _cmul(qr, qi, qr, qi)
        lr, li = _cmul(pr, pi, yr, yi)
        car_r[:, lanes(c)] = (er + lr)[last:last + 1, :]
        car_i[:, lanes(c)] = (ei + li)[last:last + 1, :]
        entries += [yr, yi]

    def correct(j, prev):
        out = []
        row_r, row_i = pw_r[pl.ds(j, 1), :], sign * pw_i[pl.ds(j, 1), :]
        for c in range(nc):
            mr, mi = _cmul(row_r[:, lanes(c)], row_i[:, lanes(c)], entries[2 * c], entries[2 * c + 1])
            nr = x_r[c, rows(j), :] + mr
            ni = x_i[c, rows(j), :] + mi
            x_r[c, rows(j), :] = nr
            x_i[c, rows(j), :] = ni
            if visit is not None:
                visit(c, rows(j), prev[2 * c], prev[2 * c + 1])
            out += [nr, ni]
        return tuple(out)

    lax.fori_loop(0, seg, correct, tuple(entries))


def _s5_fwd(uf, bbr, bbi, cr, ci, ar, ai, seqs):
    t = uf.shape[0]
    l = t // seqs
    tl = min(SCAN_ROWS, l)
    nl = l // tl
    seg = tl // SCAN_SEGS
    cb, nq = SCAN_COLS, SCAN_CHUNKS
    nc = cb // LANES
    per = S5_ST // LANES

    def body(u_ref, bbr_ref, bbi_ref, cr_ref, ci_ref, ar_ref, ai_ref, x_r, x_i, ys_ref,
             car_r, car_i, pw_r, pw_i, b_r, b_i, tmp_a, tmp_b):
        @pl.when(pl.program_id(2) == 0)
        def _():
            car_r[...] = jnp.zeros(car_r.shape, F32)
            car_i[...] = jnp.zeros(car_i.shape, F32)
            _powers_into(pw_r, pw_i, ar_ref[...], ai_ref[...], seg)

        u = _interleaved(u_ref, tmp_a, tmp_b, seg).astype(BF16)
        for q in range(nq):
            uq = u[:, q * S5_IN:(q + 1) * S5_IN]
            br = jnp.dot(uq, bbr_ref[q], preferred_element_type=F32)
            bi = jnp.dot(uq, bbi_ref[q], preferred_element_type=F32)
            for s in range(per):
                b_r[q * per + s] = br[:, s * LANES:(s + 1) * LANES]
                b_i[q * per + s] = bi[:, s * LANES:(s + 1) * LANES]
        _segment_scan(b_r, b_i, x_r, x_i, pw_r, pw_i, car_r, car_i, seg, 1.0, False)
        wide = lambda buf, q: jnp.concatenate([buf[q * per + s] for s in range(per)], axis=1).astype(BF16)
        ys = [jnp.dot(wide(x_r, q), cr_ref[q], preferred_element_type=F32)
              + jnp.dot(wide(x_i, q), ci_ref[q], preferred_element_type=F32) for q in range(nq)]
        _store_deinterleaved(ys_ref, jnp.concatenate(ys, axis=1), tmp_a, tmp_b, seg)

    rows = lambda w: pl.BlockSpec((tl, w), lambda s, j, r: (s * nl + r, j))
    state = pl.BlockSpec((nc, tl, LANES), lambda s, j, r: (j, s * nl + r, 0))
    chunk = lambda a: pl.BlockSpec((nq,) + a.shape[1:], lambda s, j, r: (j, 0, 0))
    par = pl.BlockSpec((1, cb), lambda s, j, r: (0, j))
    return pl.pallas_call(
        body, name="s5_fwd", grid=(seqs, S5_CH // cb, nl),
        in_specs=[rows(nq * S5_IN), chunk(bbr), chunk(bbi), chunk(cr), chunk(ci), par, par],
        out_specs=[state, state, rows(nq * S5_IN)],
        out_shape=[jax.ShapeDtypeStruct((S5_CH // LANES, t, LANES), F32)] * 2
        + [jax.ShapeDtypeStruct((t, S5_WIDTH), F32)],
        scratch_shapes=[pltpu.VMEM((1, cb), F32), pltpu.VMEM((1, cb), F32), pltpu.VMEM((seg, cb), F32),
                        pltpu.VMEM((seg, cb), F32)] + [pltpu.VMEM((nc, tl, LANES), F32)] * 2
        + [pltpu.VMEM((nq * S5_IN // LANES, tl, LANES), F32)] * 2,
        compiler_params=_params(("parallel", "parallel", "arbitrary")),
    )(uf, bbr, bbi, cr, ci, ar, ai)


def _s5_bwd(dys, uf, xr, xi, bbr, bbi, cr, ci, ar, ai, seqs):
    t = dys.shape[0]
    l = t // seqs
    tl = min(SCAN_ROWS, l)
    nl = l // tl
    seg = tl // SCAN_SEGS
    cb, nq = SCAN_COLS, SCAN_CHUNKS
    nc = cb // LANES
    per = S5_ST // LANES

    def body(dy_ref, u_ref, x_r, x_i, bbr_ref, bbi_ref, cr_ref, ci_ref, ar_ref, ai_ref,
             du_ref, dbbr_ref, dbbi_ref, dcr_ref, dci_ref, dar_ref, dai_ref,
             car_r, car_i, pw_r, pw_i, g_r, g_i, lam_r, lam_i, acc_r, acc_i, tmp_a, tmp_b):
        @pl.when(pl.program_id(2) == 0)
        def _():
            car_r[...] = jnp.zeros(car_r.shape, F32)
            car_i[...] = jnp.zeros(car_i.shape, F32)
            _powers_into(pw_r, pw_i, ar_ref[...], ai_ref[...], seg)
            for acc_ref in (dbbr_ref, dbbi_ref, dcr_ref, dci_ref, dar_ref, dai_ref):
                acc_ref[...] = jnp.zeros(acc_ref.shape, F32)

        dy = _interleaved(dy_ref, tmp_a, tmp_b, seg).astype(BF16)
        for q in range(nq):
            dyq = dy[:, q * S5_IN:(q + 1) * S5_IN]
            gr = lax.dot_general(dyq, cr_ref[q], _NT, preferred_element_type=F32)
            gi = lax.dot_general(dyq, ci_ref[q], _NT, preferred_element_type=F32)
            for s in range(per):
                g_r[q * per + s] = gr[:, s * LANES:(s + 1) * LANES]
                g_i[q * per + s] = gi[:, s * LANES:(s + 1) * LANES]
        acc_r[...] = jnp.zeros(acc_r.shape, F32)
        acc_i[...] = jnp.zeros(acc_i.shape, F32)

        def visit(c, rws, lr, li):
            xr_t, xi_t = x_r[c, rws, :], x_i[c, rws, :]
            acc_r[c] += lr * xr_t + li * xi_t
            acc_i[c] += li * xr_t - lr * xi_t

        _segment_scan(g_r, g_i, lam_r, lam_i, pw_r, pw_i, car_r, car_i, seg, -1.0, True, visit)
        for c in range(nc):
            dar_ref[:, c * LANES:(c + 1) * LANES] += jnp.sum(acc_r[c], axis=0, keepdims=True)
            dai_ref[:, c * LANES:(c + 1) * LANES] += jnp.sum(acc_i[c], axis=0, keepdims=True)
        u = _interleaved(u_ref, tmp_a, tmp_b, seg).astype(BF16)
        wide = lambda buf, q: jnp.concatenate([buf[q * per + s] for s in range(per)], axis=1).astype(BF16)
        du = []
        for q in range(nq):
            io = slice(q * S5_IN, (q + 1) * S5_IN)
            lq_r, lq_i = wide(lam_r, q), wide(lam_i, q)
            du.append(lax.dot_general(lq_r, bbr_ref[q], _NT, preferred_element_type=F32)
                      + lax.dot_general(lq_i, bbi_ref[q], _NT, preferred_element_type=F32))
            dbbr_ref[q] += lax.dot_general(u[:, io], lq_r, _TN, preferred_element_type=F32)
            dbbi_ref[q] += lax.dot_general(u[:, io], lq_i, _TN, preferred_element_type=F32)
            dcr_ref[q] += lax.dot_general(wide(x_r, q), dy[:, io], _TN, preferred_element_type=F32)
            dci_ref[q] += lax.dot_general(wide(x_i, q), dy[:, io], _TN, preferred_element_type=F32)
        _store_deinterleaved(du_ref, jnp.concatenate(du, axis=1), tmp_a, tmp_b, seg)

    rows = lambda w: pl.BlockSpec((tl, w), lambda s, j, r: (s * nl + nl - 1 - r, j))
    state = pl.BlockSpec((nc, tl, LANES), lambda s, j, r: (j, s * nl + nl - 1 - r, 0))
    chunk = lambda a: pl.BlockSpec((nq,) + a.shape[1:], lambda s, j, r: (j, 0, 0))
    acc = lambda a: pl.BlockSpec((None, nq) + a.shape[1:], lambda s, j, r: (s, j, 0, 0))
    par = pl.BlockSpec((1, cb), lambda s, j, r: (0, j))
    par_acc = pl.BlockSpec((None, 1, cb), lambda s, j, r: (s, 0, j))
    per_seq = lambda a: jax.ShapeDtypeStruct((seqs,) + a.shape, F32)
    return pl.pallas_call(
        body, name="s5_bwd", grid=(seqs, S5_CH // cb, nl),
        in_specs=[rows(nq * S5_IN), rows(nq * S5_IN), state, state, chunk(bbr), chunk(bbi), chunk(cr), chunk(ci),
                  par, par],
        out_specs=[rows(nq * S5_IN), acc(bbr), acc(bbi), acc(cr), acc(ci), par_acc, par_acc],
        out_shape=[jax.ShapeDtypeStruct((t, S5_WIDTH), F32), per_seq(bbr), per_seq(bbi), per_seq(cr), per_seq(ci),
                   jax.ShapeDtypeStruct((seqs, 1, S5_CH), F32), jax.ShapeDtypeStruct((seqs, 1, S5_CH), F32)],
        scratch_shapes=[pltpu.VMEM((1, cb), F32), pltpu.VMEM((1, cb), F32), pltpu.VMEM((seg, cb), F32),
                        pltpu.VMEM((seg, cb), F32)] + [pltpu.VMEM((nc, tl, LANES), F32)] * 4
        + [pltpu.VMEM((nc, SCAN_SEGS, LANES), F32)] * 2 + [pltpu.VMEM((nq * S5_IN // LANES, tl, LANES), F32)] * 2,
        compiler_params=_params(("parallel", "parallel", "arbitrary")),
    )(dys, uf, xr, xi, bbr, bbi, cr, ci, ar, ai)


XATT_BLOCK = 2048


def _xatt_probs(qv, kv):
    s = lax.dot_general(qv, kv, _NT, preferred_element_type=F32) * (X_HEAD_DIM ** -0.5)
    e = jnp.exp(s - jnp.max(s, axis=-1, keepdims=True))
    return e / jnp.sum(e, axis=-1, keepdims=True)


def _xatt_fwd(q, k, kv, seqs):
    t = q.shape[0]
    tq = min(XATT_BLOCK, t // seqs)
    nq = t // seqs // tq

    def body(q_ref, k_ref, v_ref, o_ref):
        p = _xatt_probs(q_ref[...], k_ref[...])
        o_ref[...] = jnp.dot(p.astype(BF16), v_ref[...].astype(BF16), preferred_element_type=F32).astype(o_ref.dtype)

    qs = pl.BlockSpec((tq, X_HEAD_DIM), lambda b, h, i: (b * nq + i, h))
    return pl.pallas_call(
        body, name="xatt_fwd", grid=(seqs, N_X_HEADS, nq),
        in_specs=[qs, pl.BlockSpec((N_MEM, X_HEAD_DIM), lambda b, h, i: (b, h)),
                  pl.BlockSpec((N_MEM, X_HEAD_DIM), lambda b, h, i: (b, N_X_HEADS + h))],
        out_specs=qs, out_shape=jax.ShapeDtypeStruct(q.shape, BF16),
        compiler_params=_params(("parallel", "parallel", "parallel")),
    )(q, k, kv)


def _xatt_bwd(q, k, kv, do, seqs):
    t = q.shape[0]
    tq = min(XATT_BLOCK, t // seqs)
    nq = t // seqs // tq
    scale = X_HEAD_DIM ** -0.5

    def body(q_ref, k_ref, v_ref, do_ref, dq_ref, dk_ref, dv_ref):
        @pl.when(pl.program_id(2) == 0)
        def _():
            dk_ref[...] = jnp.zeros(dk_ref.shape, F32)
            dv_ref[...] = jnp.zeros(dv_ref.shape, F32)

        qv, kk = q_ref[...], k_ref[...]
        p = _xatt_probs(qv, kk)
        dob = do_ref[...].astype(BF16)
        dp = lax.dot_general(dob, v_ref[...].astype(BF16), _NT, preferred_element_type=F32)
        ds = p * (dp - jnp.sum(dp * p, axis=-1, keepdims=True))
        dsb = ds.astype(BF16)
        dq_ref[...] = jnp.dot(dsb, kk, preferred_element_type=F32) * scale
        dk_ref[...] += lax.dot_general(dsb, qv, _TN, preferred_element_type=F32) * scale
        dv_ref[...] += lax.dot_general(p.astype(BF16), dob, _TN, preferred_element_type=F32)

    qs = pl.BlockSpec((tq, X_HEAD_DIM), lambda b, h, i: (b * nq + i, h))
    ks = pl.BlockSpec((N_MEM, X_HEAD_DIM), lambda b, h, i: (b, h))
    return pl.pallas_call(
        body, name="xatt_bwd", grid=(seqs, N_X_HEADS, nq),
        in_specs=[qs, ks, pl.BlockSpec((N_MEM, X_HEAD_DIM), lambda b, h, i: (b, N_X_HEADS + h)), qs],
        out_specs=[qs, ks, ks],
        out_shape=[jax.ShapeDtypeStruct(q.shape, F32), jax.ShapeDtypeStruct(k.shape, F32),
                   jax.ShapeDtypeStruct(k.shape, F32)],
        compiler_params=_params(("parallel", "parallel", "arbitrary")),
    )(q, k, kv, do)


CONV_COLS = 256


def _shift_down(x, k, row):
    return jnp.where(row >= k, pltpu.roll(x, k, 0), 0.0)


def _shift_up(x, k, row):
    n = x.shape[0]
    return jnp.where(row < n - k, pltpu.roll(x, n - k, 0), 0.0)


def _conv_pre(g, w, b, row):
    return b + w[0:1, :] * _shift_down(g, 2, row) + w[1:2, :] * _shift_down(g, 1, row) + w[2:3, :] * g


def _ffn_up_gate(hn, w_up, w, b, seqs):
    t = hn.shape[0]
    l = t // seqs
    nc = D_FF // CONV_COLS

    def body(a_ref, wg_ref, wu_ref, w_ref, b_ref, g_ref, u_ref, o_ref):
        a = a_ref[...]
        gb = jnp.dot(a, wg_ref[...], preferred_element_type=F32).astype(BF16)
        ub = jnp.dot(a, wu_ref[...], preferred_element_type=F32).astype(BF16)
        g_ref[...] = gb
        u_ref[...] = ub
        g = gb.astype(F32)
        row = lax.broadcasted_iota(jnp.int32, g.shape, 0)
        pre = _conv_pre(g, w_ref[...], b_ref[...], row)
        o_ref[...] = (pre * jax.nn.sigmoid(pre) * ub.astype(F32)).astype(o_ref.dtype)

    cols = pl.BlockSpec((l, CONV_COLS), lambda s, j: (s, j))
    half = jax.ShapeDtypeStruct((t, D_FF), BF16)
    return pl.pallas_call(
        body, name="ffn_up_gate", grid=(seqs, nc),
        in_specs=[pl.BlockSpec((l, hn.shape[1]), lambda s, j: (s, 0)),
                  pl.BlockSpec((hn.shape[1], CONV_COLS), lambda s, j: (0, j)),
                  pl.BlockSpec((hn.shape[1], CONV_COLS), lambda s, j: (0, nc + j)),
                  pl.BlockSpec((3, CONV_COLS), lambda s, j: (0, j)), pl.BlockSpec((1, CONV_COLS), lambda s, j: (0, j))],
        out_specs=[cols, cols, cols], out_shape=[half, half, half],
        compiler_params=_params(("parallel", "parallel")),
    )(hn, w_up, w_up, w, b)


def _ffn_down_dx_gate(dh, w_down, gate, up, w, b, seqs):
    t = dh.shape[0]
    l = t // seqs
    nc = D_FF // CONV_COLS
    steps = nc * seqs

    def body(dh_ref, wd_ref, g_ref, u_ref, w_ref, b_ref, dgu_ref, dw_ref, db_ref, stage, sems):
        s, j = pl.program_id(0), pl.program_id(1)
        n = s * nc + j
        slot = n % 2

        def copies(slot_, j_, s_):
            rows = pl.ds(pl.multiple_of(s_ * l, 16), l)
            return [pltpu.make_async_copy(
                stage.at[slot_, half],
                dgu_ref.at[rows, pl.ds(pl.multiple_of((half * nc + j_) * CONV_COLS, 128), CONV_COLS)],
                sems.at[slot_, half]) for half in (0, 1)]

        @pl.when(n >= 2)
        def _():
            for cp in copies(slot, j, s):
                cp.wait()

        da = lax.dot_general(dh_ref[...], wd_ref[...], _NT, preferred_element_type=F32)
        g, wv = g_ref[...].astype(F32), w_ref[...]
        row = lax.broadcasted_iota(jnp.int32, g.shape, 0)
        g1, g2 = _shift_down(g, 1, row), _shift_down(g, 2, row)
        pre = b_ref[...] + wv[0:1, :] * g2 + wv[1:2, :] * g1 + wv[2:3, :] * g
        sg = jax.nn.sigmoid(pre)
        silu = pre * sg
        stage[slot, 1] = (da * silu).astype(stage.dtype)
        dpre = da * u_ref[...].astype(F32) * (sg * (1.0 + pre * (1.0 - sg)))
        dg = wv[2:3, :] * dpre + wv[1:2, :] * _shift_up(dpre, 1, row) + wv[0:1, :] * _shift_up(dpre, 2, row)
        stage[slot, 0] = dg.astype(stage.dtype)
        for cp in copies(slot, j, s):
            cp.start()
        dw_ref[0:1, :] = jnp.sum(dpre * g2, axis=0, keepdims=True)
        dw_ref[1:2, :] = jnp.sum(dpre * g1, axis=0, keepdims=True)
        dw_ref[2:3, :] = jnp.sum(dpre * g, axis=0, keepdims=True)
        db_ref[...] = jnp.sum(dpre, axis=0, keepdims=True)

        @pl.when(n == steps - 1)
        def _():
            for cp in copies(slot, j, s) + (copies(1 - slot, j, s) if steps > 1 else []):
                cp.wait()

    cols = pl.BlockSpec((l, CONV_COLS), lambda s, j: (s, j))
    return pl.pallas_call(
        body, name="ffn_down_dx_gate", grid=(seqs, nc),
        in_specs=[pl.BlockSpec((l, dh.shape[1]), lambda s, j: (s, 0)),
                  pl.BlockSpec((CONV_COLS, dh.shape[1]), lambda s, j: (j, 0)), cols, cols,
                  pl.BlockSpec((3, CONV_COLS), lambda s, j: (0, j)), pl.BlockSpec((1, CONV_COLS), lambda s, j: (0, j))],
        out_specs=[ANY, pl.BlockSpec((None, 3, CONV_COLS), lambda s, j: (s, 0, j)),
                   pl.BlockSpec((None, 1, CONV_COLS), lambda s, j: (s, 0, j))],
        out_shape=[jax.ShapeDtypeStruct((t, 2 * D_FF), BF16), jax.ShapeDtypeStruct((seqs, 3, D_FF), F32),
                   jax.ShapeDtypeStruct((seqs, 1, D_FF), F32)],
        scratch_shapes=[pltpu.VMEM((2, 2, l, CONV_COLS), BF16), pltpu.SemaphoreType.DMA((2, 2))],
        compiler_params=_params(("arbitrary", "arbitrary")),
    )(dh, w_down, gate, up, w, b)


def _loss_head(h, target):
    t, d = h.shape
    tm = _pick(t, (256, 128, 8))

    def body(h_ref, t_ref, dh_ref, dhb_ref, loss_ref):
        @pl.when(pl.program_id(0) == 0)
        def _():
            loss_ref[...] = jnp.zeros(loss_ref.shape, F32)

        e = h_ref[...] - t_ref[...]
        dh = e * (1.0 / d)
        dh_ref[...] = dh
        dhb_ref[...] = dh.astype(BF16)
        loss_ref[...] += (0.5 / d) * jnp.sum(jnp.sum(e * e, axis=1, keepdims=True), axis=0, keepdims=True)

    blk = pl.BlockSpec((tm, d), lambda i: (i, 0))
    return pl.pallas_call(
        body, name="loss_head", grid=(t // tm,), in_specs=[blk, blk],
        out_specs=[blk, blk, pl.BlockSpec((1, 1), lambda i: (0, 0))],
        out_shape=[jax.ShapeDtypeStruct((t, d), F32), jax.ShapeDtypeStruct((t, d), BF16),
                   jax.ShapeDtypeStruct((1, 1), F32)],
        compiler_params=_params(("arbitrary",)),
    )(h, target)


def _s5_discretise(a_re, a_im, log_dt, b_re, b_im):
    dt = jnp.exp(log_dt)[:, None]
    mag = jnp.exp(a_re * dt)
    lb_r = mag * jnp.cos(a_im * dt)
    lb_i = mag * jnp.sin(a_im * dt)
    den = a_re * a_re + a_im * a_im
    nr = lb_r - 1.0
    coef_r = (nr * a_re + lb_i * a_im) / den
    coef_i = (lb_i * a_re - nr * a_im) / den
    bb_r = coef_r[:, :, None] * b_re - coef_i[:, :, None] * b_im
    bb_i = coef_r[:, :, None] * b_im + coef_i[:, :, None] * b_re
    return lb_r, lb_i, bb_r, bb_i


S5_CHUNKS = 4
S5_PER = S5_GROUPS // S5_CHUNKS


def _blockdiag_in(bb):
    eye = jnp.eye(S5_PER, dtype=bb.dtype)
    return jnp.einsum("jgpc,gh->jgchp", bb.reshape(S5_CHUNKS, S5_PER, S5_STATE, S5_GROUP_CH), eye).reshape(
        S5_CHUNKS, S5_PER * S5_GROUP_CH, S5_PER * S5_STATE)


def _blockdiag_in_grad(d):
    eye = jnp.eye(S5_PER, dtype=d.dtype)
    return jnp.einsum("jgchp,gh->jgpc", d.reshape(S5_CHUNKS, S5_PER, S5_GROUP_CH, S5_PER, S5_STATE), eye).reshape(
        S5_GROUPS, S5_STATE, S5_GROUP_CH)


def _blockdiag_out(c):
    eye = jnp.eye(S5_PER, dtype=c.dtype)
    return jnp.einsum("jgcp,gh->jgphc", c.reshape(S5_CHUNKS, S5_PER, S5_GROUP_CH, S5_STATE), eye).reshape(
        S5_CHUNKS, S5_PER * S5_STATE, S5_PER * S5_GROUP_CH)


def _blockdiag_out_grad(d):
    eye = jnp.eye(S5_PER, dtype=d.dtype)
    return jnp.einsum("jgphc,gh->jgcp", d.reshape(S5_CHUNKS, S5_PER, S5_STATE, S5_PER, S5_GROUP_CH), eye).reshape(
        S5_GROUPS, S5_GROUP_CH, S5_STATE)


def _local_step(x3, mem3, target3, p, wb, late_weights=None, early_grads=None):
    seqs, l, d = x3.shape
    t = seqs * l
    x = x3.reshape(t, d)
    mem = mem3.reshape(seqs * N_MEM, d)
    target = target3.reshape(t, d)
    full = lambda a: (a, a.shape[1], 0, 0)

    s5_in = (p["s5_a_re"], p["s5_a_im"], p["s5_log_dt"], p["s5_b_re"], p["s5_b_im"])
    (lb_r, lb_i, bb_r, bb_i), s5_pull = jax.vjp(_s5_discretise, *s5_in)
    ar, ai = lb_r.reshape(1, S5_CH), lb_i.reshape(1, S5_CH)
    bbr_d, bbi_d = _blockdiag_in(bb_r).astype(BF16), _blockdiag_in(bb_i).astype(BF16)
    cr_d, ci_d = _blockdiag_out(p["s5_c_re"]).astype(BF16), (-_blockdiag_out(p["s5_c_im"])).astype(BF16)
    d_row = p["s5_d"].reshape(1, S5_WIDTH)

    w_in = wb["w_in"]
    w_qkv = w_in[:, :3 * FOX_WIDTH]
    w_uf = jnp.concatenate(
        [w_in[:, 3 * FOX_WIDTH + N_FOX_HEADS:], w_in[:, 3 * FOX_WIDTH:3 * FOX_WIDTH + N_FOX_HEADS],
         jnp.zeros((d, UF_COLS - S5_WIDTH - N_FOX_HEADS), w_in.dtype)], axis=1)

    hn1 = _rowwise(_rms, [full(x)], [p["norm_mix"]], [(d, d, 0, BF16)], "norm_mix_fwd")
    qkv = _mm(hn1, w_qkv, "nn", "in_qkv")
    uf = _mm(hn1, w_uf, "nn", "in_uf")

    bh = seqs * N_FOX_HEADS
    q_pair = (qkv, 128, 0, 1)
    k_pair = (qkv, 128, N_PAIRS, 1)
    gq2, gk2 = jnp.tile(p["fox_q_norm"], (1, 2)), jnp.tile(p["fox_k_norm"], (1, 2))
    pair_out = [(FOX_WIDTH, 128, 1, BF16)]
    qn = _rowwise(_rms_pair, [q_pair], [gq2], pair_out, "fox_qnorm_fwd", heads=N_PAIRS)
    kn = _rowwise(_rms_pair, [k_pair], [gk2], pair_out, "fox_knorm_fwd", heads=N_PAIRS)

    f_rows = uf[:, S5_WIDTH:S5_WIDTH + N_FOX_HEADS].reshape(seqs, l, N_FOX_HEADS).transpose(0, 2, 1).reshape(bh, l)
    f_bias = jnp.tile(p["fox_f_bias"].reshape(N_FOX_HEADS, 1), (seqs, 1))
    c_wide = jnp.broadcast_to(_forget_fwd(f_rows, f_bias)[:, :, None], (bh, l, 128))
    fox, lse = _fox_fwd(qn, kn, qkv, c_wide, seqs)

    xr, xi, ys = _s5_fwd(uf, bbr_d, bbi_d, cr_d, ci_d, ar, ai, seqs)
    u_blk = (uf, S5_WIDTH, 0, 0)
    yg = _rowwise(_s5_act, [full(ys), u_blk], [d_row], [(S5_WIDTH, S5_WIDTH, 0, F32)], "s5_act_fwd")
    if late_weights is not None:
        wb = dict(wb, **late_weights("mid", yg))
    z = _mm(yg, wb["s5_w_glu"], "nn", "s5_glu")
    y2n = _rowwise(_s5_gate, [full(yg), full(z)], [p["s5_b_glu"], p["out_norm_s5"]],
                   [(S5_WIDTH, S5_WIDTH, 0, BF16)], "s5_gate_fwd")
    foxn = _rowwise(_rms, [full(fox)], [p["out_norm_fox"]], [(FOX_WIDTH, FOX_WIDTH, 0, BF16)], "fox_outnorm_fwd")
    mixed = jnp.concatenate([foxn, y2n], axis=1)
    h1 = _mm(mixed, wb["w_out"], "nn", "mix_out", res=x)
    if late_weights is not None:
        wb = dict(wb, **late_weights("late", h1))

    hn2 = _rowwise(_rms, [full(h1)], [p["norm_cross"]], [(d, d, 0, BF16)], "norm_cross_fwd")
    mn = _rowwise(_rms, [full(mem)], [p["norm_mem"]], [(d, d, 0, BF16)], "norm_mem_fwd")
    xq_raw = _mm(hn2, wb["w_xq"], "nn", "x_q")
    kv = _mm(mn, wb["w_xkv"], "nn", "x_kv")
    xh = lambda a: (a, X_HEAD_DIM, 0, 1)
    xqn = _rowwise(_rms, [xh(xq_raw)], [p["xq_norm"]], [(d, X_HEAD_DIM, 1, BF16)], "x_qnorm_fwd", heads=N_X_HEADS)
    xkn = _rowwise(_rms, [xh(kv)], [p["xk_norm"]], [(d, X_HEAD_DIM, 1, BF16)], "x_knorm_fwd", heads=N_X_HEADS)
    xo = _xatt_fwd(xqn, xkn, kv, seqs)
    h2 = _mm(xo, wb["w_xo"], "nn", "x_out", res=h1)

    hn3 = _rowwise(_rms, [full(h2)], [p["norm_ffn"]], [(d, d, 0, BF16)], "norm_ffn_fwd")
    gate, up, act = _ffn_up_gate(hn3, wb["w_ffn_up"], p["ffn_conv_w"], p["ffn_conv_b"], seqs)
    h3 = _mm(act, wb["w_ffn_down"], "nn", "ffn_down", res=h2)
    dh3, dh3_b, loss = _loss_head(h3, target)

    g = {}
    late_dt = BF16 if early_grads is not None else F32
    g["w_ffn_down"] = _mm(act, dh3_b, "tn", "ffn_down_dw", out_dtype=late_dt)
    dgu, dconv_w, dconv_b = _ffn_down_dx_gate(dh3_b, wb["w_ffn_down"], gate, up, p["ffn_conv_w"], p["ffn_conv_b"], seqs)
    g["ffn_conv_w"], g["ffn_conv_b"] = jnp.sum(dconv_w, axis=0), jnp.sum(dconv_b, axis=0)
    dhn3 = _mm(dgu, wb["w_ffn_up"], "nt", "ffn_up_dx")
    g["w_ffn_up"] = _mm(hn3, dgu, "tn", "ffn_up_dw", out_dtype=late_dt)
    (dh2,), (g["norm_ffn"],) = _rowwise_vjp(_rms, [full(h2)], [p["norm_ffn"]], [full(dhn3)], "norm_ffn_bwd",
                                            adds=[full(dh3)])

    dxo = _mm(dh2, wb["w_xo"], "nt", "x_out_dx")
    g["w_xo"] = _mm(xo, dh2, "tn", "x_out_dw", out_dtype=late_dt)
    dxqn, dxkn, dxv = _xatt_bwd(xqn, xkn, kv, dxo, seqs)
    (dxq_raw,), (g["xq_norm"],) = _rowwise_vjp(_rms, [xh(xq_raw)], [p["xq_norm"]], [xh(dxqn)], "x_qnorm_bwd",
                                               heads=N_X_HEADS, row_dtypes=[BF16])
    (dxk_raw,), (g["xk_norm"],) = _rowwise_vjp(_rms, [xh(kv)], [p["xk_norm"]], [xh(dxkn)], "x_knorm_bwd",
                                               heads=N_X_HEADS, row_dtypes=[BF16])
    dkv = jnp.concatenate([dxk_raw, dxv.astype(BF16)], axis=1)
    dhn2 = _mm(dxq_raw, wb["w_xq"], "nt", "x_q_dx")
    g["w_xq"] = _mm(hn2, dxq_raw, "tn", "x_q_dw", out_dtype=late_dt)
    dmn = _mm(dkv, wb["w_xkv"], "nt", "x_kv_dx")
    g["w_xkv"] = _mm(mn, dkv, "tn", "x_kv_dw", out_dtype=late_dt)
    norm_cross = p["norm_cross"]
    if early_grads is not None:
        norm_cross = norm_cross + early_grads({n: g[n] for n in LATE_WEIGHTS})
    (dh1,), (g["norm_cross"],) = _rowwise_vjp(_rms, [full(h1)], [norm_cross], [full(dhn2)], "norm_cross_bwd",
                                              adds=[full(dh2)])
    _, (g["norm_mem"],) = _rowwise_vjp(_rms, [full(mem)], [p["norm_mem"]], [full(dmn)], "norm_mem_bwd",
                                       row_dtypes=[BF16])

    dmixed = _mm(dh1, wb["w_out"], "nt", "mix_out_dx")
    g["w_out"] = _mm(mixed, dh1, "tn", "mix_out_dw", out_dtype=late_dt)
    (dfox,), (g["out_norm_fox"],) = _rowwise_vjp(_rms, [full(fox)], [p["out_norm_fox"]],
                                                 [(dmixed, FOX_WIDTH, 0, 0)], "fox_outnorm_bwd")
    (dyg_a, dz), (g["s5_b_glu"], g["out_norm_s5"]) = _rowwise_vjp(
        _s5_gate, [full(yg), full(z)], [p["s5_b_glu"], p["out_norm_s5"]], [(dmixed, S5_WIDTH, 1, 0)], "s5_gate_bwd",
        row_dtypes=[F32, BF16])
    dyg = _mm(dz, wb["s5_w_glu"], "nt", "s5_glu_dx", res=dyg_a)
    g["s5_w_glu"] = _mm(yg, dz, "tn", "s5_glu_dw", out_dtype=late_dt)
    (dys, du_a), (dd_row,) = _rowwise_vjp(_s5_act, [full(ys), u_blk], [d_row], [full(dyg)], "s5_act_bwd",
                                          row_dtypes=[BF16, F32])
    g["s5_d"] = dd_row
    du_b, dbbr_d, dbbi_d, dcr_d, dci_d, dar, dai = _s5_bwd(dys, uf, xr, xi, bbr_d, bbi_d, cr_d, ci_d, ar, ai, seqs)
    dbbr_d, dbbi_d, dcr_d, dci_d = (jnp.sum(a, axis=0) for a in (dbbr_d, dbbi_d, dcr_d, dci_d))
    d_lb_r = jnp.sum(dar, axis=0).reshape(S5_GROUPS, S5_STATE)
    d_lb_i = jnp.sum(dai, axis=0).reshape(S5_GROUPS, S5_STATE)
    g["s5_a_re"], g["s5_a_im"], g["s5_log_dt"], g["s5_b_re"], g["s5_b_im"] = s5_pull(
        (d_lb_r, d_lb_i, _blockdiag_in_grad(dbbr_d), _blockdiag_in_grad(dbbi_d)))
    g["s5_c_re"] = _blockdiag_out_grad(dcr_d)
    g["s5_c_im"] = -_blockdiag_out_grad(dci_d)

    dqn, dkn, dv, dc, dcq = _fox_bwd(qn, kn, qkv, c_wide, fox, dfox, lse, seqs)
    pair = lambda a: (a, 128, 0, 1)
    (dq_raw,), (dgq2,) = _rowwise_vjp(_rms_pair, [q_pair], [gq2], [pair(dqn)], "fox_qnorm_bwd", heads=N_PAIRS,
                                      row_dtypes=[BF16])
    (dk_raw,), (dgk2,) = _rowwise_vjp(_rms_pair, [k_pair], [gk2], [pair(dkn)], "fox_knorm_bwd", heads=N_PAIRS,
                                      row_dtypes=[BF16])
    g["fox_q_norm"] = dgq2[:, :HEAD_DIM] + dgq2[:, HEAD_DIM:]
    g["fox_k_norm"] = dgk2[:, :HEAD_DIM] + dgk2[:, HEAD_DIM:]
    df_rows, dfb = _forget_bwd(f_rows, f_bias, (dc + dcq).reshape(bh, l))
    g["fox_f_bias"] = jnp.sum(dfb.reshape(seqs, N_FOX_HEADS), axis=0)
    df = df_rows.reshape(seqs, N_FOX_HEADS, l).transpose(0, 2, 1).reshape(t, N_FOX_HEADS)
    dqkv = jnp.concatenate([dq_raw, dk_raw, dv.astype(BF16)], axis=1)
    duf = jnp.concatenate([du_a + du_b, df, jnp.zeros((t, UF_COLS - S5_WIDTH - N_FOX_HEADS), F32)],
                          axis=1).astype(BF16)
    dhn1 = _mm(duf, w_uf, "nt", "in_uf_dx", res=_mm(dqkv, w_qkv, "nt", "in_qkv_dx"))
    dw_qkv = _mm(hn1, dqkv, "tn", "in_qkv_dw")
    dw_uf = _mm(hn1, duf, "tn", "in_uf_dw")
    g["w_in"] = jnp.concatenate([dw_qkv, dw_uf[:, S5_WIDTH:S5_WIDTH + N_FOX_HEADS], dw_uf[:, :S5_WIDTH]], axis=1)
    (dx,), (g["norm_mix"],) = _rowwise_vjp(_rms, [full(x)], [p["norm_mix"]], [full(dhn1)], "norm_mix_bwd",
                                           adds=[full(dh1)])
    return loss, dx.reshape(seqs, l, d), g


def _place():
    return lax.axis_index("x"), lax.axis_index("y"), lax.axis_index("c")


def _other_chips(x, y):
    return [(1 - x, y), (x, 1 - y), (1 - x, 1 - y)]


ANY = pl.BlockSpec(memory_space=pl.ANY)


def _gather_weights(shards, col_kind, taps):
    n = len(shards)

    def body(*refs):
        ins, tap_in, outs, tap_out = refs[:n], refs[n], refs[n + 1:2 * n + 1], refs[2 * n + 1]
        ici_send, ici_recv, d2d_send, d2d_recv, own_send, own_recv = refs[2 * n + 2:]
        x, y, c = _place()
        mine = 2 * x + y
        chips = _other_chips(x, y)
        sibling = (x, y, 1 - c)

        def piece(a, s, h):
            r, cs = ins[a].shape
            hr = r // 2
            if col_kind[a]:
                return outs[a].at[pl.ds(pl.multiple_of(h * hr, 16), hr), pl.ds(pl.multiple_of(s * cs, 128), cs)]
            return outs[a].at[pl.ds(pl.multiple_of(s * r + h * hr, 16), hr), :]

        def slab(a, s):
            r, cs = ins[a].shape
            if col_kind[a]:
                return outs[a].at[:, pl.ds(pl.multiple_of(s * cs, 128), cs)]
            return outs[a].at[pl.ds(pl.multiple_of(s * r, 16), r), :]

        def own_half(a, h):
            hr = ins[a].shape[0] // 2
            return ins[a].at[pl.ds(pl.multiple_of(h * hr, 16), hr), :]

        sends = []
        for a in range(n):
            cp = pltpu.make_async_remote_copy(
                src_ref=ins[a], dst_ref=slab(a, mine), send_sem=own_send.at[a], recv_sem=own_recv.at[a],
                device_id=sibling, device_id_type=MESH)
            cp.start()
            sends.append(cp)
        cp = pltpu.make_async_remote_copy(
            src_ref=tap_in, dst_ref=tap_out.at[mine], send_sem=own_send.at[n], recv_sem=own_recv.at[n],
            device_id=sibling, device_id_type=MESH)
        cp.start()
        sends.append(cp)
        for a in range(n):
            for j, (px, py) in enumerate(chips):
                cp = pltpu.make_async_remote_copy(
                    src_ref=own_half(a, c), dst_ref=piece(a, mine, c), send_sem=ici_send.at[3 * a + j],
                    recv_sem=ici_recv.at[3 * a + j], device_id=(px, py, c), device_id_type=MESH)
                cp.start()
                sends.append(cp)
        for j, (px, py) in enumerate(chips):
            cp = pltpu.make_async_remote_copy(
                src_ref=tap_in, dst_ref=tap_out.at[mine], send_sem=ici_send.at[3 * n + j],
                recv_sem=ici_recv.at[3 * n + j], device_id=(px, py, c), device_id_type=MESH)
            cp.start()
            sends.append(cp)
        for a in range(n):
            for j, (px, py) in enumerate(chips):
                got = piece(a, 2 * px + py, c)
                pltpu.make_async_remote_copy(
                    src_ref=got, dst_ref=got, send_sem=ici_send.at[3 * a + j], recv_sem=ici_recv.at[3 * a + j],
                    device_id=(px, py, c), device_id_type=MESH).wait_recv()
                fwd = pltpu.make_async_remote_copy(
                    src_ref=got, dst_ref=got, send_sem=d2d_send.at[3 * a + j], recv_sem=d2d_recv.at[3 * a + j],
                    device_id=(x, y, 1 - c), device_id_type=MESH)
                fwd.start()
                sends.append(fwd)
        for a in range(n):
            for j, (px, py) in enumerate(chips):
                other = piece(a, 2 * px + py, 1 - c)
                pltpu.make_async_remote_copy(
                    src_ref=other, dst_ref=other, send_sem=d2d_send.at[3 * a + j], recv_sem=d2d_recv.at[3 * a + j],
                    device_id=(x, y, 1 - c), device_id_type=MESH).wait_recv()
        for j, (px, py) in enumerate(chips):
            pltpu.make_async_remote_copy(
                src_ref=tap_in, dst_ref=tap_out.at[2 * px + py], send_sem=ici_send.at[3 * n + j],
                recv_sem=ici_recv.at[3 * n + j], device_id=(px, py, c), device_id_type=MESH).wait_recv()
        for a in range(n):
            pltpu.make_async_remote_copy(
                src_ref=ins[a], dst_ref=slab(a, mine), send_sem=own_send.at[a], recv_sem=own_recv.at[a],
                device_id=sibling, device_id_type=MESH).wait_recv()
        pltpu.make_async_remote_copy(
            src_ref=tap_in, dst_ref=tap_out.at[mine], send_sem=own_send.at[n], recv_sem=own_recv.at[n],
            device_id=sibling, device_id_type=MESH).wait_recv()
        for cp in sends:
            cp.wait_send()

    def full_shape(a):
        r, cs = shards[a].shape
        return (r, 4 * cs) if col_kind[a] else (4 * r, cs)

    res = pl.pallas_call(
        body, name="gather_weights", in_specs=[ANY] * (n + 1), out_specs=[ANY] * (n + 1),
        out_shape=[jax.ShapeDtypeStruct(full_shape(a), shards[a].dtype) for a in range(n)]
        + [jax.ShapeDtypeStruct((4,) + taps.shape, taps.dtype)],
        scratch_shapes=[pltpu.SemaphoreType.DMA((3 * n + 3,)), pltpu.SemaphoreType.DMA((3 * n + 3,)),
                        pltpu.SemaphoreType.DMA((3 * n,)), pltpu.SemaphoreType.DMA((3 * n,)),
                        pltpu.SemaphoreType.DMA((n + 1,)), pltpu.SemaphoreType.DMA((n + 1,))],
        compiler_params=pltpu.CompilerParams(has_side_effects=True),
    )(*shards, taps)
    return res[:n], res[n]


HBM = pl.BlockSpec(memory_space=pltpu.HBM)
SEM = pl.BlockSpec(memory_space=pltpu.SEMAPHORE)
DATAFLOW = pltpu.SideEffectType.DATAFLOW_SIDE_EFFECTING


def _in_hbm(a):
    return pltpu.with_memory_space_constraint(a, pltpu.HBM)


def _split_start(name, srcs, lands, n_copies, plan):
    n = len(srcs)

    def body(*refs):
        src_refs, land_refs = refs[:n], refs[n:2 * n]
        send_sems, recv_sems = refs[2 * n], refs[2 * n + 1]
        for i, (src, dst, dev) in enumerate(plan(src_refs, land_refs)):
            pltpu.make_async_remote_copy(src_ref=src, dst_ref=dst, send_sem=send_sems.at[i], recv_sem=recv_sems.at[i],
                                         device_id=dev, device_id_type=MESH).start()
        refs[-1][...] = jnp.zeros((8, 128), F32)

    res = pl.pallas_call(
        body, name=name, in_specs=[HBM] * (2 * n),
        out_specs=[SEM, SEM] + [HBM] * (2 * n) + [pl.BlockSpec(memory_space=pltpu.VMEM)],
        out_shape=[pltpu.SemaphoreType.DMA((n_copies,)), pltpu.SemaphoreType.DMA((n_copies,))]
        + [pltpu.HBM(a.shape, a.dtype) for a in list(srcs) + list(lands)] + [jax.ShapeDtypeStruct((8, 128), F32)],
        input_output_aliases={i: 2 + i for i in range(2 * n)},
        compiler_params=pltpu.CompilerParams(has_side_effects=DATAFLOW),
    )(*[_in_hbm(a) for a in list(srcs) + list(lands)])
    return res[0], res[1], list(res[2:2 + n]), list(res[2 + n:2 + 2 * n]), res[-1]


def _split_wait(name, send_sems, recv_sems, srcs, lands, after, plan):
    n = len(srcs)

    def body(*refs):
        src_refs, land_refs = refs[:n], refs[n:2 * n]
        send_ref, recv_ref = refs[2 * n], refs[2 * n + 1]
        for i, (src, dst, dev) in enumerate(plan(src_refs, land_refs)):
            cp = pltpu.make_async_remote_copy(src_ref=src, dst_ref=dst, send_sem=send_ref.at[i], recv_sem=recv_ref.at[i],
                                              device_id=dev, device_id_type=MESH)
            cp.wait_send()
            cp.wait_recv()

    res = pl.pallas_call(
        body, name=name, in_specs=[HBM] * (2 * n) + [SEM, SEM, ANY], out_specs=[HBM] * (2 * n),
        out_shape=[pltpu.HBM(a.shape, a.dtype) for a in list(srcs) + list(lands)],
        input_output_aliases={i: i for i in range(2 * n)},
        compiler_params=pltpu.CompilerParams(has_side_effects=DATAFLOW),
    )(*srcs, *lands, send_sems, recv_sems, after)
    return list(res[:n]), list(res[n:])


def _late_gather_plan(col_kind):
    def plan(src_refs, land_refs):
        x, y, c = _place()
        mine = 2 * x + y
        copies = []
        for a, (src, land) in enumerate(zip(src_refs, land_refs)):
            r, cs = src.shape
            if col_kind[a]:
                dst = land.at[:, pl.ds(pl.multiple_of(mine * cs, 128), cs)]
            else:
                dst = land.at[pl.ds(pl.multiple_of(mine * r, 16), r), :]
            copies.append((src, dst, (x, y, 1 - c)))
            copies += [(src, dst, (px, py, c)) for (px, py) in _other_chips(x, y)]
        return copies
    return plan


def _late_reduce_plan(col_kind):
    def plan(src_refs, land_refs):
        x, y, c = _place()
        copies = []
        for a, (src, land) in enumerate(zip(src_refs, land_refs)):
            for j, (px, py) in enumerate(_other_chips(x, y)):
                if col_kind[a]:
                    cs = land.shape[2]
                    piece = src.at[:, pl.ds(pl.multiple_of((2 * px + py) * cs, 128), cs)]
                else:
                    piece = src.at[2 * px + py]
                copies.append((piece, land.at[j], (px, py, c)))
        return copies
    return plan


def _pair_swap(name, halves):
    n = len(halves)

    def body(*refs):
        ins, outs = refs[:n], refs[n:2 * n]
        send_sems, recv_sems = refs[2 * n:]
        x, y, c = _place()
        copies = []
        for a in range(n):
            cp = pltpu.make_async_remote_copy(
                src_ref=ins[a], dst_ref=outs[a], send_sem=send_sems.at[a], recv_sem=recv_sems.at[a],
                device_id=(x, y, 1 - c), device_id_type=MESH)
            cp.start()
            copies.append(cp)
        for cp in copies:
            cp.wait()

    return pl.pallas_call(
        body, name=name, in_specs=[ANY] * n, out_specs=[ANY] * n,
        out_shape=[jax.ShapeDtypeStruct(s.shape, s.dtype) for s in halves],
        scratch_shapes=[pltpu.SemaphoreType.DMA((n,)), pltpu.SemaphoreType.DMA((n,))],
        compiler_params=pltpu.CompilerParams(has_side_effects=True),
    )(*halves)


def _chip_sum(name, chip_sel, own, col, others):
    _, r, c = others.shape
    tr = _pick(r, (256, 128, 64, 32, 16))
    if col:
        own_spec = pl.BlockSpec((tr, c), lambda i, s: (i, s[0]))
    else:
        own_spec = pl.BlockSpec((None, tr, c), lambda i, s: (s[0], i, 0))
    specs = [own_spec] + [pl.BlockSpec((None, tr, c), lambda i, s, k=k: (k, i, 0)) for k in range(3)]

    def body(s_ref, own_ref, r0, r1, r2, o_ref):
        o_ref[...] = ((own_ref[...].astype(F32) + r0[...].astype(F32)) + r1[...].astype(F32)) + r2[...].astype(F32)

    return pl.pallas_call(
        body, name=name,
        grid_spec=pltpu.PrefetchScalarGridSpec(
            num_scalar_prefetch=1, grid=(r // tr,), in_specs=specs,
            out_specs=pl.BlockSpec((tr, c), lambda i, s: (i, 0))),
        out_shape=jax.ShapeDtypeStruct((r, c), F32),
        compiler_params=_params(("parallel",)),
    )(chip_sel, own, others, others, others)


def _allreduce_small(vals):
    sizes = [int(math.prod(v.shape)) for v in vals]
    padded = [-(-s // 128) * 128 for s in sizes]
    total = -(-sum(padded) // 1024) * 1024
    flat = [jnp.pad(v.reshape(-1), (0, p - s)) for v, s, p in zip(vals, sizes, padded)]
    flat.append(jnp.zeros((total - sum(padded),), F32))
    packed = jnp.concatenate(flat).reshape(total // 128, 128)

    def body(in_ref, out_ref, r0, r1, r2, send_sems, recv_sems):
        x, y, c = _place()
        out_ref[...] = in_ref[...]
        for k, (peer, land) in enumerate(zip([(x, y, 1 - c), (1 - x, y, c), (x, 1 - y, c)], (r0, r1, r2))):
            cp = pltpu.make_async_remote_copy(
                src_ref=out_ref, dst_ref=land, send_sem=send_sems.at[k], recv_sem=recv_sems.at[k],
                device_id=peer, device_id_type=MESH)
            cp.start()
            cp.wait()
            out_ref[...] = out_ref[...] + land[...]

    vm = pl.BlockSpec(memory_space=pltpu.VMEM)
    summed = pl.pallas_call(
        body, name="allreduce_small", in_specs=[vm], out_specs=vm,
        out_shape=jax.ShapeDtypeStruct(packed.shape, F32),
        scratch_shapes=[pltpu.VMEM(packed.shape, F32)] * 3
        + [pltpu.SemaphoreType.DMA((3,)), pltpu.SemaphoreType.DMA((3,))],
        compiler_params=pltpu.CompilerParams(has_side_effects=True, vmem_limit_bytes=VMEM_LIMIT_BYTES),
    )(packed).reshape(-1)
    outs, off = [], 0
    for v, s, p in zip(vals, sizes, padded):
        outs.append(summed[off:off + s].reshape(v.shape))
        off += p
    return outs


def _adamw_math(w, g, m, v):
    m2 = ADAM_B1 * m + (1.0 - ADAM_B1) * g
    v2 = ADAM_B2 * v + (1.0 - ADAM_B2) * (g * g)
    m_hat = m2 / (1.0 - ADAM_B1 ** ADAM_STEP)
    v_hat = v2 / (1.0 - ADAM_B2 ** ADAM_STEP)
    delta = -ADAM_LR * (m_hat / (jnp.sqrt(v_hat) + ADAM_EPS) + ADAM_WD * w)
    return delta, m2, v2


def _adamw_big(name, w, g_mine, g_sibling, m, v):
    _, r, c = w.shape

    def body(w_ref, ga_ref, gb_ref, m_ref, v_ref, go_ref, d_ref, mo_ref, vo_ref):
        gv = ga_ref[...] + gb_ref[...]
        d, m2, v2 = _adamw_math(w_ref[...], gv, m_ref[...], v_ref[...])
        go_ref[...] = gv
        d_ref[...] = d
        mo_ref[...] = m2
        vo_ref[...] = v2

    tr = _pick(r, (256, 128, 64, 32, 16, 8))
    if r % tr == 0 and tr % 8 == 0:
        grid = (r // tr,)
        blk = pl.BlockSpec((None, tr, c), lambda i: (0, i, 0))
        part = pl.BlockSpec((tr, c), lambda i: (i, 0))
    else:
        grid = (c // 512,)
        blk = pl.BlockSpec((None, r, 512), lambda i: (0, 0, i))
        part = pl.BlockSpec((r, 512), lambda i: (0, i))
    return pl.pallas_call(
        body, name=name, grid=grid, in_specs=[blk, part, part, blk, blk], out_specs=[blk] * 4,
        out_shape=[jax.ShapeDtypeStruct((1, r, c), F32)] * 4, compiler_params=_params(("parallel",)),
    )(w, g_mine, g_sibling, m, v)


def _adamw_small(ws, gs, ms, vs):
    n = len(ws)

    def body(*refs):
        w_r, g_r, m_r, v_r = refs[:n], refs[n:2 * n], refs[2 * n:3 * n], refs[3 * n:4 * n]
        o = refs[4 * n:]
        for a in range(n):
            gv = g_r[a][...]
            d, m2, v2 = _adamw_math(w_r[a][...], gv, m_r[a][...], v_r[a][...])
            o[a][...] = gv
            o[n + a][...] = d
            o[2 * n + a][...] = m2
            o[3 * n + a][...] = v2

    res = pl.pallas_call(
        body, name="adamw_small", out_shape=[jax.ShapeDtypeStruct(w.shape, F32) for _ in range(4) for w in ws],
        compiler_params=_params(),
    )(*ws, *gs, *ms, *vs)
    return res[:n], res[n:2 * n], res[2 * n:3 * n], res[3 * n:]


def _full_from_gathered(name, gathered):
    if name == "w_in":
        rows = gathered.shape[0] // 4
        return gathered.reshape(4, rows, gathered.shape[1]).transpose(1, 0, 2).reshape(rows, 4 * gathered.shape[1])
    return gathered


def _reduce_layout(name, full):
    if name in COL_KIND:
        return full
    if name == "w_in":
        rows, cols = full.shape
        return full.reshape(rows, 4, cols // 4).transpose(1, 0, 2)
    return full.reshape(4, full.shape[0] // 4, full.shape[1])


def kernel(x, mem, norm_mix, w_in, fox_q_norm, fox_k_norm, fox_f_bias, s5_a_re, s5_a_im, s5_log_dt, s5_b_re, s5_b_im, s5_c_re, s5_c_im, s5_d, s5_w_glu, s5_b_glu, out_norm_fox, out_norm_s5, w_out, norm_cross, norm_mem, w_xq, w_xkv, xq_norm, xk_norm, w_xo, norm_ffn, w_ffn_up, ffn_conv_w, ffn_conv_b, w_ffn_down, loss_target, m_norm_mix, m_w_in, m_fox_q_norm, m_fox_k_norm, m_fox_f_bias, m_s5_a_re, m_s5_a_im, m_s5_log_dt, m_s5_b_re, m_s5_b_im, m_s5_c_re, m_s5_c_im, m_s5_d, m_s5_w_glu, m_s5_b_glu, m_out_norm_fox, m_out_norm_s5, m_w_out, m_norm_cross, m_norm_mem, m_w_xq, m_w_xkv, m_xq_norm, m_xk_norm, m_w_xo, m_norm_ffn, m_w_ffn_up, m_ffn_conv_w, m_ffn_conv_b, m_w_ffn_down, v_norm_mix, v_w_in, v_fox_q_norm, v_fox_k_norm, v_fox_f_bias, v_s5_a_re, v_s5_a_im, v_s5_log_dt, v_s5_b_re, v_s5_b_im, v_s5_c_re, v_s5_c_im, v_s5_d, v_s5_w_glu, v_s5_b_glu, v_out_norm_fox, v_out_norm_s5, v_w_out, v_norm_cross, v_norm_mem, v_w_xq, v_w_xkv, v_xq_norm, v_xk_norm, v_w_xo, v_norm_ffn, v_w_ffn_up, v_ffn_conv_w, v_ffn_conv_b, v_w_ffn_down):
    given = dict(locals())
    w = {n: given[n] for n in WEIGHTS}
    m = {n: given["m_" + n] for n in WEIGHTS}
    v = {n: given["v_" + n] for n in WEIGHTS}
    xi, yi, _ = _place()
    chip = (2 * xi + yi).astype(jnp.int32)
    chip_sel = chip.reshape(1)
    early_kind = [n in COL_KIND for n in EARLY_WEIGHTS]
    late_kind = [n in COL_KIND for n in LATE_WEIGHTS]

    gathered, taps = _gather_weights([w[FIRST_WEIGHT][0].astype(BF16)], [False], w["ffn_conv_w"][0])
    first_full = gathered[0]
    conv_w = taps.transpose(1, 0, 2).reshape(3, D_FF)
    pending = {}
    g_started = None
    for stage, names in (("mid", MID_WEIGHTS), ("late", LATE_WEIGHTS)):
        kinds = [n in COL_KIND for n in names]
        shards = [w[n][0].astype(BF16) for n in names]
        if g_started is None:
            first_full, shards[0] = lax.optimization_barrier((first_full, shards[0]))
        else:
            shards[0] = shards[0] + g_started[0:1, 0:1].astype(BF16)
        full = [lax.empty((s.shape[0], 4 * s.shape[1]) if ck else (4 * s.shape[0], s.shape[1]), BF16)
                for s, ck in zip(shards, kinds)]
        plan = _late_gather_plan(kinds)
        send, recv, srcs, lands, g_started = _split_start(
            "gather_" + stage + "_start", shards, full, 4 * len(names), plan)
        pending[stage] = (names, plan, send, recv, srcs, lands)
    wb = {FIRST_WEIGHT: _full_from_gathered(FIRST_WEIGHT, first_full)}

    def late_weights(stage, after):
        names, plan, send, recv, srcs, lands = pending[stage]
        _, full = _split_wait("gather_" + stage + "_wait", send, recv, srcs, lands, after, plan)
        return dict(zip(names, full))

    reduce_plan = _late_reduce_plan(late_kind)
    late_reduce = {}

    def early_grads(late_g):
        grads = [_reduce_layout(n, late_g[n]) for n in LATE_WEIGHTS]
        lands = [lax.empty((3, s.shape[0], s.shape[1] // 4) if ck else (3,) + s.shape[1:], BF16)
                 for s, ck in zip(grads, late_kind)]
        late_reduce["sems"] = _split_start("reduce_late_start", grads, lands, 3 * len(LATE_WEIGHTS), reduce_plan)
        return late_reduce["sems"][4][0:1, 0:1]

    p = {n: w[n][0] for n in SMALL}
    p["ffn_conv_w"] = conv_w
    for n in ("norm_mix", "fox_q_norm", "fox_k_norm", "fox_f_bias", "s5_b_glu", "out_norm_fox", "out_norm_s5",
              "norm_cross", "norm_mem", "xq_norm", "xk_norm", "norm_ffn", "ffn_conv_b"):
        p[n] = p[n].reshape(1, -1)
    p["norm_mix"] = p["norm_mix"] + g_started[0:1, 0:1]
    loss, grad_x, g = _local_step(x, mem, loss_target, p, wb, late_weights, early_grads)

    grads = [_reduce_layout(n, g[n].astype(BF16)) for n in EARLY_WEIGHTS]
    early_lands = [lax.empty((3, s.shape[0], s.shape[1] // 4) if ck else (3,) + s.shape[1:], BF16)
                   for s, ck in zip(grads, early_kind)]
    early_plan = _late_reduce_plan(early_kind)
    e_send, e_recv, e_srcs, e_lands, e_started = _split_start(
        "reduce_early_start", grads, early_lands, 3 * len(EARLY_WEIGHTS), early_plan)

    out_g, out_d, out_m, out_v = {}, {}, {}, {}

    def finish(names, kinds, sums, from_chips, tag):
        mine = [_chip_sum("reduce_chip_sum_" + n, chip_sel, ps, ck, fc)
                for n, ps, fc, ck in zip(names, sums, from_chips, kinds)]
        theirs = _pair_swap("reduce_pair_swap_" + tag, mine)
        for n, a, b in zip(names, mine, theirs):
            if n == "w_in":
                flip = lambda t: jnp.swapaxes(t, -1, -2)
                res = _adamw_big("adamw_" + n, flip(w[n]), flip(a), flip(b), flip(m[n]), flip(v[n]))
                out_g[n], out_d[n], out_m[n], out_v[n] = (flip(t) for t in res)
                continue
            out_g[n], out_d[n], out_m[n], out_v[n] = _adamw_big("adamw_" + n, w[n], a, b, m[n], v[n])

    r_send, r_recv, r_srcs, r_lands, _ = late_reduce["sems"]
    late_sums, late_from_chips = _split_wait("reduce_late_wait", r_send, r_recv, r_srcs, r_lands, e_started,
                                             reduce_plan)
    finish(LATE_WEIGHTS, late_kind, late_sums, late_from_chips, "late")

    small_names = list(SMALL) + ["ffn_conv_w"]
    small_vals = [g[n].reshape(w[n].shape if n != "ffn_conv_w" else (1, 3, D_FF)) for n in small_names]
    last = LATE_WEIGHTS[-1]
    loss, out_v[last] = lax.optimization_barrier((loss, out_v[last]))
    reduced = _allreduce_small(small_vals + [loss])
    loss_all = reduced[-1].reshape(())
    conv_w_grad = lax.dynamic_slice_in_dim(reduced[-2], chip * (D_FF // 4), D_FF // 4, axis=2)
    sg, sd, sm, sv = _adamw_small(
        [w[n] for n in small_names], list(reduced[:len(SMALL)]) + [conv_w_grad],
        [m[n] for n in small_names], [v[n] for n in small_names])
    out_g.update(zip(small_names, sg))
    out_d.update(zip(small_names, sd))
    out_m.update(zip(small_names, sm))
    out_v.update(zip(small_names, sv))

    early_sums, early_from_chips = _split_wait("reduce_early_wait", e_send, e_recv, e_srcs, e_lands, reduced[0],
                                               early_plan)
    finish(EARLY_WEIGHTS, early_kind, early_sums, early_from_chips, "early")

    return (loss_all, grad_x, *[out_g[n] for n in WEIGHTS], *[out_d[n] for n in WEIGHTS],
            *[out_m[n] for n in WEIGHTS], *[out_v[n] for n in WEIGHTS])
```

```python
import math

import jax
import jax.numpy as jnp
from jax import lax
from jax.experimental import pallas as pl
from jax.experimental.pallas import tpu as pltpu

F32 = jnp.float32
BF16 = jnp.bfloat16

D_MODEL = 1024
FOX_WIDTH = 512
HEAD_DIM = 64
N_FOX_HEADS = 8
S5_WIDTH = 512
S5_GROUP_CH = 16
S5_GROUPS = 32
S5_STATE = 64
S5_CH = S5_GROUPS * S5_STATE
N_X_HEADS = 4
X_HEAD_DIM = 256
N_MEM = 256
D_FF = 2816
UF_COLS = 640
EPS = 1e-6
ADAM_LR = 0.001
ADAM_B1 = 0.9
ADAM_B2 = 0.999
ADAM_EPS = 1e-08
ADAM_WD = 0.01
ADAM_STEP = 10

VMEM_LIMIT_BYTES = 56 * 1024 * 1024
MM_BLOCK_BYTES = 6 * 1024 * 1024
MM_VMEM_BYTES = 40 * 1024 * 1024
MM_TILE_MAX = 1536
MESH = pl.DeviceIdType.MESH

FIRST_WEIGHT = "w_in"
MID_WEIGHTS = ("s5_w_glu", "w_out")
EARLY_WEIGHTS = (FIRST_WEIGHT,) + MID_WEIGHTS
LATE_WEIGHTS = ("w_xq", "w_xkv", "w_xo", "w_ffn_up", "w_ffn_down")
BIG = EARLY_WEIGHTS + LATE_WEIGHTS
COL_KIND = ("w_xkv", "w_ffn_up")
SMALL = ("norm_mix", "fox_q_norm", "fox_k_norm", "fox_f_bias", "s5_a_re", "s5_a_im", "s5_log_dt",
         "s5_b_re", "s5_b_im", "s5_c_re", "s5_c_im", "s5_d", "s5_b_glu", "out_norm_fox", "out_norm_s5",
         "norm_cross", "norm_mem", "xq_norm", "xk_norm", "norm_ffn", "ffn_conv_b")
WEIGHTS = ("norm_mix", "w_in", "fox_q_norm", "fox_k_norm", "fox_f_bias", "s5_a_re", "s5_a_im", "s5_log_dt",
           "s5_b_re", "s5_b_im", "s5_c_re", "s5_c_im", "s5_d", "s5_w_glu", "s5_b_glu", "out_norm_fox",
           "out_norm_s5", "w_out", "norm_cross", "norm_mem", "w_xq", "w_xkv", "xq_norm", "xk_norm", "w_xo",
           "norm_ffn", "w_ffn_up", "ffn_conv_w", "ffn_conv_b", "w_ffn_down")


def _params(sem=None):
    return pltpu.CompilerParams(dimension_semantics=sem, vmem_limit_bytes=VMEM_LIMIT_BYTES)


def _pick(n, cands):
    for c in cands:
        if n % c == 0:
            return c
    return n


_DIMS = {"nn": (((1,), (0,)), ((), ())), "nt": (((1,), (1,)), ((), ())), "tn": (((0,), (0,)), ((), ()))}


def _mm(a, b, mode, name, out_dtype=F32, res=None):
    if mode == "nn":
        (m, k), (k2, n) = a.shape, b.shape
    elif mode == "nt":
        (m, k), (n, k2) = a.shape, b.shape
    else:
        (k, m), (k2, n) = a.shape, b.shape
    assert k == k2, (name, a.shape, b.shape)

    has_res = res is not None
    a_size, b_size = a.dtype.itemsize, b.dtype.itemsize
    o_size = jnp.dtype(out_dtype).itemsize + (res.dtype.itemsize if has_res else 0)

    def tiles(dim):
        return [c for c in range(MM_TILE_MAX, 0, -128) if dim % c == 0] or [dim]

    best = None
    for tm in tiles(m):
        for tn in tiles(n):
            a_blk, b_blk = tm * k * a_size, tn * k * b_size
            if max(a_blk, b_blk) > MM_BLOCK_BYTES or 2 * (a_blk + b_blk + tm * tn * o_size) > MM_VMEM_BYTES:
                continue
            for rows_outer in (True, False):
                moved = (m * k * a_size + (m // tm) * n * k * b_size) if rows_outer else \
                        (n * k * b_size + (n // tn) * m * k * a_size)
                key = (moved, -(tm * tn))
                if best is None or key < best[0]:
                    best = (key, tm, tn, rows_outer)
    assert best is not None, (name, a.shape, b.shape)
    _, tm, tn, rows_outer = best
    ij = (lambda g0, g1: (g0, g1)) if rows_outer else (lambda g0, g1: (g1, g0))
    if mode == "tn":
        a_spec = pl.BlockSpec((k, tm), lambda g0, g1: (0, ij(g0, g1)[0]))
    else:
        a_spec = pl.BlockSpec((tm, k), lambda g0, g1: (ij(g0, g1)[0], 0))
    if mode == "nt":
        b_spec = pl.BlockSpec((tn, k), lambda g0, g1: (ij(g0, g1)[1], 0))
    else:
        b_spec = pl.BlockSpec((k, tn), lambda g0, g1: (0, ij(g0, g1)[1]))
    o_spec = pl.BlockSpec((tm, tn), lambda g0, g1: ij(g0, g1))
    grid = (m // tm, n // tn) if rows_outer else (n // tn, m // tm)
    dims = _DIMS[mode]

    def body(*refs):
        a_ref, b_ref = refs[0], refs[1]
        o_ref = refs[-1]
        acc = lax.dot_general(a_ref[...].astype(BF16), b_ref[...].astype(BF16), dims, preferred_element_type=F32)
        if has_res:
            acc = acc + refs[2][...].astype(F32)
        o_ref[...] = acc.astype(o_ref.dtype)

    return pl.pallas_call(
        body, name=name, grid=grid,
        in_specs=[a_spec, b_spec] + ([o_spec] if has_res else []),
        out_specs=o_spec, out_shape=jax.ShapeDtypeStruct((m, n), out_dtype),
        compiler_params=_params(("parallel", "parallel")),
    )(*((a, b, res) if has_res else (a, b)))


def _row_spec(tm, bc, off, step):
    return pl.BlockSpec((tm, bc), lambda i, h: (i, off + step * h))


ROW_TILE_ELEMS = 512 * 1024


def _row_tile(t, rows):
    widest = max(bc for (_, bc, _, _) in rows)
    return _pick(t, (min(t, ROW_TILE_ELEMS // widest), 512, 256, 128, 64, 8))


def _rowwise(fn, rows, pars, outs, name, heads=1):
    t = rows[0][0].shape[0]
    tm = _row_tile(t, rows)
    nr, npar = len(rows), len(pars)

    def body(*refs):
        vals = [r[...].astype(F32) for r in refs[:nr + npar]]
        res = fn(*vals)
        if not isinstance(res, (tuple, list)):
            res = (res,)
        for o_ref, v in zip(refs[nr + npar:], res):
            o_ref[...] = v.astype(o_ref.dtype)

    in_specs = [_row_spec(tm, bc, off, st) for (_, bc, off, st) in rows]
    in_specs += [pl.BlockSpec(p.shape, lambda i, h: (0, 0)) for p in pars]
    out_specs = [_row_spec(tm, bc, 0, st) for (_, bc, st, _) in outs]
    out_shape = [jax.ShapeDtypeStruct((t, c), dt) for (c, _, _, dt) in outs]
    res = pl.pallas_call(
        body, name=name, grid=(t // tm, heads), in_specs=in_specs, out_specs=out_specs, out_shape=out_shape,
        compiler_params=_params(("parallel", "parallel")),
    )(*[r[0] for r in rows], *pars)
    return res[0] if len(res) == 1 else res


def _rowwise_vjp(fn, rows, pars, cts, name, heads=1, adds=None, row_dtypes=None):
    t = rows[0][0].shape[0]
    tm = _row_tile(t, rows)
    nr, npar, nct = len(rows), len(pars), len(cts)
    adds = adds or [None] * nr
    add_list = [a for a in adds if a is not None]
    row_dtypes = row_dtypes or [F32] * nr

    def body(*refs):
        i, h = pl.program_id(0), pl.program_id(1)
        p = 0
        row_v = [r[...].astype(F32) for r in refs[p:p + nr]]; p += nr
        par_v = [r[...].astype(F32) for r in refs[p:p + npar]]; p += npar
        ct_v = [r[...].astype(F32) for r in refs[p:p + nct]]; p += nct
        add_refs = refs[p:p + len(add_list)]; p += len(add_list)
        drow_refs = refs[p:p + nr]; p += nr
        dpar_refs = refs[p:p + npar]

        def wrapped(*a):
            r = fn(*a)
            return tuple(r) if isinstance(r, (tuple, list)) else (r,)

        _, pull = jax.vjp(wrapped, *row_v, *par_v)
        grads = pull(tuple(ct_v))
        ai = 0
        for k in range(nr):
            g = grads[k]
            if adds[k] is not None:
                g = g + add_refs[ai][...].astype(F32)
                ai += 1
            drow_refs[k][...] = g.astype(drow_refs[k].dtype)

        @pl.when((i == 0) & (h == 0))
        def _():
            for r in dpar_refs:
                r[...] = jnp.zeros(r.shape, r.dtype)

        for k in range(npar):
            dpar_refs[k][...] += grads[nr + k]

    in_specs = [_row_spec(tm, bc, off, st) for (_, bc, off, st) in rows]
    in_specs += [pl.BlockSpec(q.shape, lambda i, h: (0, 0)) for q in pars]
    in_specs += [_row_spec(tm, bc, off, st) for (_, bc, off, st) in cts]
    in_specs += [_row_spec(tm, bc, off, st) for (_, bc, off, st) in add_list]
    out_specs = [_row_spec(tm, bc, 0, st) for (_, bc, _, st) in rows]
    out_specs += [pl.BlockSpec(q.shape, lambda i, h: (0, 0)) for q in pars]
    out_shape = [jax.ShapeDtypeStruct((t, bc * (heads if st else 1)), dt) for (_, bc, _, st), dt in zip(rows, row_dtypes)]
    out_shape += [jax.ShapeDtypeStruct(q.shape, F32) for q in pars]
    res = pl.pallas_call(
        body, name=name, grid=(t // tm, heads), in_specs=in_specs, out_specs=out_specs, out_shape=out_shape,
        compiler_params=_params(("arbitrary", "arbitrary")),
    )(*[r[0] for r in rows], *pars, *[c[0] for c in cts], *[a[0] for a in add_list])
    return list(res[:nr]), list(res[nr:])


def _rms(x, g):
    return x * lax.rsqrt(jnp.mean(x * x, axis=-1, keepdims=True) + EPS) * g


def _rms_pair(x, g):
    left = lax.broadcasted_iota(jnp.int32, x.shape, 1) < HEAD_DIM
    x2 = x * x
    ms_a = jnp.sum(jnp.where(left, x2, 0.0), axis=-1, keepdims=True) * (1.0 / HEAD_DIM)
    ms_b = jnp.sum(jnp.where(left, 0.0, x2), axis=-1, keepdims=True) * (1.0 / HEAD_DIM)
    return x * lax.rsqrt(jnp.where(left, ms_a, ms_b) + EPS) * g


def _gelu(x):
    return 0.5 * x * (1.0 + jnp.tanh(math.sqrt(2.0 / math.pi) * (x + 0.044715 * (x * x * x))))


def _s5_act(ys, u, d):
    return _gelu(ys + d * u)


def _s5_gate(yg, z, b, g):
    return _rms(yg * jax.nn.sigmoid(z + b), g)


def _lane_cumsum(x, reverse):
    n = x.shape[-1]
    lane = lax.broadcasted_iota(jnp.int32, x.shape, 1)
    k = 1
    while k < n:
        if reverse:
            x = x + jnp.where(lane < n - k, pltpu.roll(x, n - k, 1), 0.0)
        else:
            x = x + jnp.where(lane >= k, pltpu.roll(x, k, 1), 0.0)
        k *= 2
    return x


def _log_sigmoid(z):
    return jnp.minimum(z, 0.0) - jnp.log(1.0 + jnp.exp(-jnp.abs(z)))


def _forget_fwd(f, bias):
    def body(f_ref, b_ref, c_ref):
        c_ref[...] = _lane_cumsum(_log_sigmoid(f_ref[...] + b_ref[...]), False)

    return pl.pallas_call(body, name="forget_fwd", out_shape=jax.ShapeDtypeStruct(f.shape, F32),
                          compiler_params=_params())(f, bias)


def _forget_bwd(f, bias, dc):
    def body(f_ref, b_ref, dc_ref, df_ref, db_ref):
        dlog = _lane_cumsum(dc_ref[...], True)
        df = dlog * jax.nn.sigmoid(-(f_ref[...] + b_ref[...]))
        df_ref[...] = df
        db_ref[...] = jnp.sum(df, axis=1, keepdims=True)

    return pl.pallas_call(body, name="forget_bwd",
                          out_shape=(jax.ShapeDtypeStruct(f.shape, F32), jax.ShapeDtypeStruct(bias.shape, F32)),
                          compiler_params=_params())(f, bias, dc)


FOX_BLOCK = 1024
FOX_KEYS = 512
FOX_BWD_BLOCK = 512
_NT = _DIMS["nt"]
_TN = _DIMS["tn"]


N_PAIRS = N_FOX_HEADS // 2
V_BLOCK0 = 2 * N_PAIRS


def _left_lanes(shape):
    return lax.broadcasted_iota(jnp.int32, shape, 1) < HEAD_DIM


def _top_rows(shape):
    return lax.broadcasted_iota(jnp.int32, shape, 0) < HEAD_DIM


def _wide(c_tile, n):
    return c_tile if n == 128 else jnp.concatenate([c_tile] * (n // 128), axis=1)


def _fox_fwd(qn, kn, qkv, c_wide, seqs):
    t = qn.shape[0]
    l = t // seqs
    tb = min(FOX_BLOCK, l)
    tk = min(FOX_KEYS, tb)
    ratio = tb // tk
    nb = l // tb
    scale = HEAD_DIM ** -0.5

    def body(q_ref, k_ref, v_ref, ca_ref, cb_ref, o_ref, lse_ref, vt_ref):
        i = pl.program_id(2)
        top = _top_rows((128, tb))

        @pl.when(i == 0)
        def _():
            vt_ref[...] = v_ref[...].T.astype(BF16)

        qt = (q_ref[...].astype(F32) * scale).T.astype(BF16)
        zero = jnp.zeros_like(qt)
        qts = (jnp.where(top, qt, zero), jnp.where(top, zero, qt))
        top_k = _top_rows((128, tk))
        zero_k = jnp.zeros((128, tk), BF16)
        key_pos = lax.broadcasted_iota(jnp.int32, (tk, tb), 0)
        query_pos = lax.broadcasted_iota(jnp.int32, (tk, tb), 1)
        c_refs = (ca_ref, cb_ref)

        def scores(j):
            off = pl.multiple_of(j * tk, tk)
            k2 = k_ref[pl.ds(off, tk), :]
            return tuple(jnp.dot(k2, qts[h], preferred_element_type=F32) - _wide(c_refs[h][pl.ds(off, tk), :], tb)
                         for h in (0, 1))

        def values_times(ps, j):
            vt = vt_ref[:, pl.ds(pl.multiple_of(j * tk, tk), tk)]
            return (jnp.dot(jnp.where(top_k, vt, zero_k), ps[0], preferred_element_type=F32)
                    + jnp.dot(jnp.where(top_k, zero_k, vt), ps[1], preferred_element_type=F32))

        def softmax_step(sts, stats, first_key):
            ps, new, alphas = [], [], []
            for st, (m, s_sum) in zip(sts, stats):
                if first_key is not None:
                    st = jnp.where(key_pos + first_key <= query_pos, st, -jnp.inf)
                m_new = jnp.maximum(m, jnp.max(st, axis=0, keepdims=True))
                alpha = jnp.exp(m - m_new)
                p = jnp.exp(st - m_new)
                new.append((m_new, alpha * s_sum + jnp.sum(p, axis=0, keepdims=True)))
                alphas.append(alpha)
                ps.append(p.astype(BF16))
            return tuple(ps), tuple(new), jnp.where(top, alphas[0], alphas[1])

        def tile(j, carry, first_key):
            stats, acc = carry
            ps, stats, alpha = softmax_step(scores(j), stats, first_key)
            return stats, alpha * acc + values_times(ps, j)

        stat = (jnp.full((1, tb), -jnp.inf, F32), jnp.zeros((1, tb), F32))
        below = i * ratio
        carry = lax.fori_loop(0, below, lambda j, c: tile(j, c, None), ((stat, stat), jnp.zeros((128, tb), F32)))
        for r in range(ratio):
            carry = tile(below + r, carry, r * tk)
        ((ma, sa), (mb, sb)), acc = carry
        o_ref[...] = (acc / jnp.where(top, sa, sb)).T
        lse_ref[0:1, :] = ma + jnp.log(sa)
        lse_ref[1:2, :] = mb + jnp.log(sb)

    qblk = pl.BlockSpec((tb, 128), lambda b, hp, i: (b * nb + i, hp))
    return pl.pallas_call(
        body, name="fox_fwd", grid=(seqs, N_PAIRS, nb),
        in_specs=[qblk, pl.BlockSpec((l, 128), lambda b, hp, i: (b, hp)),
                  pl.BlockSpec((l, 128), lambda b, hp, i: (b, V_BLOCK0 + hp)),
                  pl.BlockSpec((None, l, 128), lambda b, hp, i: (b * N_FOX_HEADS + 2 * hp, 0, 0)),
                  pl.BlockSpec((None, l, 128), lambda b, hp, i: (b * N_FOX_HEADS + 2 * hp + 1, 0, 0))],
        out_specs=[qblk, pl.BlockSpec((None, 2, tb), lambda b, hp, i: (b * N_PAIRS + hp, 0, i))],
        out_shape=[jax.ShapeDtypeStruct((t, FOX_WIDTH), F32), jax.ShapeDtypeStruct((seqs * N_PAIRS, 2, l), F32)],
        scratch_shapes=[pltpu.VMEM((128, l), BF16)],
        compiler_params=_params(("parallel", "parallel", "arbitrary")),
    )(qn, kn, qkv, c_wide, c_wide)


def _fox_bwd(qn, kn, qkv, c_wide, o, do, lse, seqs):
    t = qn.shape[0]
    l = t // seqs
    tb = min(FOX_BWD_BLOCK, l)
    nb = l // tb
    scale = HEAD_DIM ** -0.5
    one_at = (HEAD_DIM, 0)

    def body(q_ref, k_ref, v_ref, ca_ref, cb_ref, o_ref, do_ref, lse_ref, dq_ref, dk_ref, dv_ref, dc_ref, dcq_ref,
             qt_ref, kt_ref, dot_ref, delta_ref, dqa_ref, dqb_ref):
        top_l = _top_rows((128, l))
        top = _top_rows((128, tb))
        left = _left_lanes((tb, 128))
        row_id = lax.broadcasted_iota(jnp.int32, (128, tb), 0)
        lane_id = lax.broadcasted_iota(jnp.int32, (tb, 128), 1)
        zero_t = jnp.zeros((128, tb), BF16)
        zero_l = jnp.zeros((tb, 128), BF16)
        rows = lambda a: (jnp.where(top, a, zero_t), jnp.where(top, zero_t, a))
        lanes = lambda a: (jnp.where(left, a, zero_l), jnp.where(left, zero_l, a))
        with_one_row = lambda pair: tuple(jnp.where(row_id == one_at[h], 1.0, pair[h]).astype(BF16) for h in (0, 1))
        with_one_lane = lambda pair: tuple(jnp.where(lane_id == one_at[h], 1.0, pair[h]).astype(BF16) for h in (0, 1))
        causal = lax.broadcasted_iota(jnp.int32, (tb, tb), 0) <= lax.broadcasted_iota(jnp.int32, (tb, tb), 1)
        c_refs = (ca_ref, cb_ref)
        dq_refs = (dqa_ref, dqb_ref)

        qt_ref[...] = (q_ref[...].astype(F32) * scale).T.astype(BF16)
        kt_ref[...] = k_ref[...].astype(F32).T.astype(BF16)
        do_t = do_ref[...].T
        dot_ref[...] = do_t.astype(BF16)
        prod_t = do_t * o_ref[...].T
        delta_ref[0:1, :] = jnp.sum(jnp.where(top_l, prod_t, 0.0), axis=0, keepdims=True)
        delta_ref[1:2, :] = jnp.sum(jnp.where(top_l, 0.0, prod_t), axis=0, keepdims=True)
        dqa_ref[...] = jnp.zeros(dqa_ref.shape, F32)
        dqb_ref[...] = jnp.zeros(dqb_ref.shape, F32)

        def kv_block(j, _):
            koff = pl.multiple_of(j * tb, tb)
            k2 = k_ref[pl.ds(koff, tb), :]
            v2 = v_ref[pl.ds(koff, tb), :].astype(BF16)
            kts = with_one_row(rows(kt_ref[:, pl.ds(koff, tb)]))
            cw = tuple(_wide(c_refs[h][pl.ds(koff, tb), :], tb) for h in (0, 1))

            def q_block(i, carry, masked):
                dks, dv = list(carry[:2]), carry[2]
                qoff = pl.multiple_of(i * tb, tb)
                qs = lanes((q_ref[pl.ds(qoff, tb), :].astype(F32) * scale).astype(BF16))
                qs_one = with_one_lane(qs)
                dos = lanes(do_ref[pl.ds(qoff, tb), :].astype(BF16))
                qts = rows(qt_ref[:, pl.ds(qoff, tb)])
                dots = rows(dot_ref[:, pl.ds(qoff, tb)])
                for h in (0, 1):
                    st = jnp.dot(k2, qts[h], preferred_element_type=F32) - cw[h]
                    p = jnp.exp(st - lse_ref[h:h + 1, pl.ds(qoff, tb)])
                    if masked:
                        p = jnp.where(causal, p, 0.0)
                    dp = jnp.dot(v2, dots[h], preferred_element_type=F32)
                    dsb = (p * (dp - delta_ref[h:h + 1, pl.ds(qoff, tb)])).astype(BF16)
                    dv = dv + jnp.dot(p.astype(BF16), dos[h], preferred_element_type=F32)
                    dks[h] = dks[h] + jnp.dot(dsb, qs_one[h], preferred_element_type=F32)
                    dq_refs[h][:, pl.ds(qoff, tb)] += jnp.dot(kts[h], dsb, preferred_element_type=F32)
                return dks[0], dks[1], dv

            z = jnp.zeros((tb, 128), F32)
            carry = q_block(j, (z, z, z), True)
            rest = nb - 1 - j
            carry = lax.fori_loop(
                0, rest // 2, lambda n, c: q_block(j + 2 + 2 * n, q_block(j + 1 + 2 * n, c, False), False), carry)
            dka, dkb, dv = lax.cond(rest % 2 == 1, lambda c: q_block(nb - 1, c, False), lambda c: c, carry)
            dk_ref[pl.ds(koff, tb), :] = jnp.where(left, dka, dkb)
            dv_ref[pl.ds(koff, tb), :] = dv
            dc_ref[0:1, pl.ds(koff, tb)] = -dka.T[one_at[0]:one_at[0] + 1, :]
            dc_ref[1:2, pl.ds(koff, tb)] = -dkb.T[one_at[1]:one_at[1] + 1, :]
            return 0

        lax.fori_loop(0, nb, kv_block, 0)
        dq_ref[...] = (jnp.where(top_l, dqa_ref[...], dqb_ref[...]) * scale).T
        dcq_ref[0:1, :] = dqa_ref[one_at[0]:one_at[0] + 1, :]
        dcq_ref[1:2, :] = dqb_ref[one_at[1]:one_at[1] + 1, :]

    blk = pl.BlockSpec((l, 128), lambda b, hp: (b, hp))
    cspec = lambda k: pl.BlockSpec((None, l, 128), lambda b, hp: (b * N_FOX_HEADS + 2 * hp + k, 0, 0))
    rows2 = pl.BlockSpec((None, 2, l), lambda b, hp: (b * N_PAIRS + hp, 0, 0))
    wide = jax.ShapeDtypeStruct((t, FOX_WIDTH), F32)
    pair_rows = jax.ShapeDtypeStruct((seqs * N_PAIRS, 2, l), F32)
    return pl.pallas_call(
        body, name="fox_bwd", grid=(seqs, N_PAIRS),
        in_specs=[blk, blk, pl.BlockSpec((l, 128), lambda b, hp: (b, V_BLOCK0 + hp)), cspec(0), cspec(1), blk, blk, rows2],
        out_specs=[blk, blk, blk, rows2, rows2],
        out_shape=[wide, wide, wide, pair_rows, pair_rows],
        scratch_shapes=[pltpu.VMEM((128, l), BF16), pltpu.VMEM((128, l), BF16), pltpu.VMEM((128, l), BF16),
                        pltpu.VMEM((2, l), F32), pltpu.VMEM((128, l), F32), pltpu.VMEM((128, l), F32)],
        compiler_params=_params(("parallel", "parallel")),
    )(qn, kn, qkv, c_wide, c_wide, o, do, lse)


SCAN_ROWS = 512
SCAN_COLS = 1024


S5_IN = 128
S5_ST = 512
SCAN_CHUNKS = SCAN_COLS // S5_ST
SCAN_SEGS = 8
LANES = 128


def _cmul(ar, ai, br, bi):
    return ar * br - ai * bi, ar * bi + ai * br


def _powers_into(pw_r, pw_i, a_r, a_i, seg):
    pw_r[0:1, :] = a_r
    pw_i[0:1, :] = a_i
    for k in range(1, seg):
        pr, pi = _cmul(pw_r[k - 1:k, :], pw_i[k - 1:k, :], a_r, a_i)
        pw_r[k:k + 1, :] = pr
        pw_i[k:k + 1, :] = pi


def _interleave(dst, src, seg):
    for h in range(src.shape[0]):
        for j in range(seg):
            dst[h, j * SCAN_SEGS:(j + 1) * SCAN_SEGS, :] = src[h, pl.ds(j, SCAN_SEGS, stride=seg), :]


def _deinterleave(dst, src, seg):
    for h in range(src.shape[0]):
        for j in range(seg):
            dst[h, pl.ds(j, SCAN_SEGS, stride=seg), :] = src[h, j * SCAN_SEGS:(j + 1) * SCAN_SEGS, :]


def _interleaved(ref, tmp_a, tmp_b, seg):
    n = ref.shape[1] // LANES
    for h in range(n):
        tmp_a[h] = ref[:, h * LANES:(h + 1) * LANES].astype(F32)
    _interleave(tmp_b, tmp_a, seg)
    return jnp.concatenate([tmp_b[h] for h in range(n)], axis=1)


def _store_deinterleaved(ref, val, tmp_a, tmp_b, seg):
    n = ref.shape[1] // LANES
    for h in range(n):
        tmp_a[h] = val[:, h * LANES:(h + 1) * LANES]
    _deinterleave(tmp_b, tmp_a, seg)
    for h in range(n):
        ref[:, h * LANES:(h + 1) * LANES] = tmp_b[h]


def _segment_scan(b_r, b_i, x_r, x_i, pw_r, pw_i, car_r, car_i, seg, sign, reverse, visit=None):
    nc = b_r.shape[0]
    sub = lax.broadcasted_iota(jnp.int32, (SCAN_SEGS, LANES), 0)
    lanes = lambda c: slice(c * LANES, (c + 1) * LANES)
    rows = lambda j: pl.ds(pl.multiple_of(((seg - 1 - j) if reverse else j) * SCAN_SEGS, SCAN_SEGS), SCAN_SEGS)
    a1 = [(pw_r[0:1, lanes(c)], sign * pw_i[0:1, lanes(c)]) for c in range(nc)]

    def local(j, xs):
        out = []
        for c in range(nc):
            xr, xi = xs[2 * c], xs[2 * c + 1]
            nr = a1[c][0] * xr - a1[c][1] * xi + b_r[c, rows(j), :]
            ni = a1[c][0] * xi + a1[c][1] * xr + b_i[c, rows(j), :]
            x_r[c, rows(j), :] = nr
            x_i[c, rows(j), :] = ni
            out += [nr, ni]
        return tuple(out)

    zero = jnp.zeros((SCAN_SEGS, LANES), F32)
    ends = lax.fori_loop(0, seg, local, (zero,) * (2 * nc))

    if reverse:
        first = sub == SCAN_SEGS - 1
        neighbour = lambda v: pltpu.roll(v, SCAN_SEGS - 1, 0)
        shift = lambda v, d: jnp.where(sub < SCAN_SEGS - d, pltpu.roll(v, SCAN_SEGS - d, 0), 0.0)
    else:
        first = sub == 0
        neighbour = lambda v: pltpu.roll(v, 1, 0)
        shift = lambda v, d: jnp.where(sub >= d, pltpu.roll(v, d, 0), 0.0)
    last = 0 if reverse else SCAN_SEGS - 1
    entries = []
    for c in range(nc):
        er, ei = ends[2 * c], ends[2 * c + 1]
        pr, pi = pw_r[seg - 1:seg, lanes(c)], sign * pw_i[seg - 1:seg, lanes(c)]
        yr = jnp.where(first, car_r[:, lanes(c)], neighbour(er))
        yi = jnp.where(first, car_i[:, lanes(c)], neighbour(ei))
        qr, qi = pr, pi
        for d in (1, 2, 4):
            mr, mi = _cmul(qr, qi, shift(yr, d), shift(yi, d))
            yr, yi = yr + mr, yi + mi
            qr, qi = _cmul(qr, qi, qr, qi)
        lr, li = _cmul(pr, pi, yr, yi)
        car_r[:, lanes(c)] = (er + lr)[last:last + 1, :]
        car_i[:, lanes(c)] = (ei + li)[last:last + 1, :]
        entries += [yr, yi]

    def correct(j, prev):
        out = []
        row_r, row_i = pw_r[pl.ds(j, 1), :], sign * pw_i[pl.ds(j, 1), :]
        for c in range(nc):
            mr, mi = _cmul(row_r[:, lanes(c)], row_i[:, lanes(c)], entries[2 * c], entries[2 * c + 1])
            nr = x_r[c, rows(j), :] + mr
            ni = x_i[c, rows(j), :] + mi
            x_r[c, rows(j), :] = nr
            x_i[c, rows(j), :] = ni
            if visit is not None:
                visit(c, rows(j), prev[2 * c], prev[2 * c + 1])
            out += [nr, ni]
        return tuple(out)

    lax.fori_loop(0, seg, correct, tuple(entries))


def _s5_fwd(uf, bbr, bbi, cr, ci, ar, ai, seqs):
    t = uf.shape[0]
    l = t // seqs
    tl = min(SCAN_ROWS, l)
    nl = l // tl
    seg = tl // SCAN_SEGS
    cb, nq = SCAN_COLS, SCAN_CHUNKS
    nc = cb // LANES
    per = S5_ST // LANES

    def body(u_ref, bbr_ref, bbi_ref, cr_ref, ci_ref, ar_ref, ai_ref, x_r, x_i, ys_ref,
             car_r, car_i, pw_r, pw_i, b_r, b_i, tmp_a, tmp_b):
        @pl.when(pl.program_id(2) == 0)
        def _():
            car_r[...] = jnp.zeros(car_r.shape, F32)
            car_i[...] = jnp.zeros(car_i.shape, F32)
            _powers_into(pw_r, pw_i, ar_ref[...], ai_ref[...], seg)

        u = _interleaved(u_ref, tmp_a, tmp_b, seg).astype(BF16)
        for q in range(nq):
            uq = u[:, q * S5_IN:(q + 1) * S5_IN]
            br = jnp.dot(uq, bbr_ref[q], preferred_element_type=F32)
            bi = jnp.dot(uq, bbi_ref[q], preferred_element_type=F32)
            for s in range(per):
                b_r[q * per + s] = br[:, s * LANES:(s + 1) * LANES]
                b_i[q * per + s] = bi[:, s * LANES:(s + 1) * LANES]
        _segment_scan(b_r, b_i, x_r, x_i, pw_r, pw_i, car_r, car_i, seg, 1.0, False)
        wide = lambda buf, q: jnp.concatenate([buf[q * per + s] for s in range(per)], axis=1).astype(BF16)
        ys = [jnp.dot(wide(x_r, q), cr_ref[q], preferred_element_type=F32)
              + jnp.dot(wide(x_i, q), ci_ref[q], preferred_element_type=F32) for q in range(nq)]
        _store_deinterleaved(ys_ref, jnp.concatenate(ys, axis=1), tmp_a, tmp_b, seg)

    rows = lambda w: pl.BlockSpec((tl, w), lambda s, j, r: (s * nl + r, j))
    state = pl.BlockSpec((nc, tl, LANES), lambda s, j, r: (j, s * nl + r, 0))
    chunk = lambda a: pl.BlockSpec((nq,) + a.shape[1:], lambda s, j, r: (j, 0, 0))
    par = pl.BlockSpec((1, cb), lambda s, j, r: (0, j))
    return pl.pallas_call(
        body, name="s5_fwd", grid=(seqs, S5_CH // cb, nl),
        in_specs=[rows(nq * S5_IN), chunk(bbr), chunk(bbi), chunk(cr), chunk(ci), par, par],
        out_specs=[state, state, rows(nq * S5_IN)],
        out_shape=[jax.ShapeDtypeStruct((S5_CH // LANES, t, LANES), F32)] * 2
        + [jax.ShapeDtypeStruct((t, S5_WIDTH), F32)],
        scratch_shapes=[pltpu.VMEM((1, cb), F32), pltpu.VMEM((1, cb), F32), pltpu.VMEM((seg, cb), F32),
                        pltpu.VMEM((seg, cb), F32)] + [pltpu.VMEM((nc, tl, LANES), F32)] * 2
        + [pltpu.VMEM((nq * S5_IN // LANES, tl, LANES), F32)] * 2,
        compiler_params=_params(("parallel", "parallel", "arbitrary")),
    )(uf, bbr, bbi, cr, ci, ar, ai)


def _s5_bwd(dys, uf, xr, xi, bbr, bbi, cr, ci, ar, ai, seqs):
    t = dys.shape[0]
    l = t // seqs
    tl = min(SCAN_ROWS, l)
    nl = l // tl
    seg = tl // SCAN_SEGS
    cb, nq = SCAN_COLS, SCAN_CHUNKS
    nc = cb // LANES
    per = S5_ST // LANES

    def body(dy_ref, u_ref, x_r, x_i, bbr_ref, bbi_ref, cr_ref, ci_ref, ar_ref, ai_ref,
             du_ref, dbbr_ref, dbbi_ref, dcr_ref, dci_ref, dar_ref, dai_ref,
             car_r, car_i, pw_r, pw_i, g_r, g_i, lam_r, lam_i, acc_r, acc_i, tmp_a, tmp_b):
        @pl.when(pl.program_id(2) == 0)
        def _():
            car_r[...] = jnp.zeros(car_r.shape, F32)
            car_i[...] = jnp.zeros(car_i.shape, F32)
            _powers_into(pw_r, pw_i, ar_ref[...], ai_ref[...], seg)
            for acc_ref in (dbbr_ref, dbbi_ref, dcr_ref, dci_ref, dar_ref, dai_ref):
                acc_ref[...] = jnp.zeros(acc_ref.shape, F32)

        dy = _interleaved(dy_ref, tmp_a, tmp_b, seg).astype(BF16)
        for q in range(nq):
            dyq = dy[:, q * S5_IN:(q + 1) * S5_IN]
            gr = lax.dot_general(dyq, cr_ref[q], _NT, preferred_element_type=F32)
            gi = lax.dot_general(dyq, ci_ref[q], _NT, preferred_element_type=F32)
            for s in range(per):
                g_r[q * per + s] = gr[:, s * LANES:(s + 1) * LANES]
                g_i[q * per + s] = gi[:, s * LANES:(s + 1) * LANES]
        acc_r[...] = jnp.zeros(acc_r.shape, F32)
        acc_i[...] = jnp.zeros(acc_i.shape, F32)

        def visit(c, rws, lr, li):
            xr_t, xi_t = x_r[c, rws, :], x_i[c, rws, :]
            acc_r[c] += lr * xr_t + li * xi_t
            acc_i[c] += li * xr_t - lr * xi_t

        _segment_scan(g_r, g_i, lam_r, lam_i, pw_r, pw_i, car_r, car_i, seg, -1.0, True, visit)
        for c in range(nc):
            dar_ref[:, c * LANES:(c + 1) * LANES] += jnp.sum(acc_r[c], axis=0, keepdims=True)
            dai_ref[:, c * LANES:(c + 1) * LANES] += jnp.sum(acc_i[c], axis=0, keepdims=True)
        u = _interleaved(u_ref, tmp_a, tmp_b, seg).astype(BF16)
        wide = lambda buf, q: jnp.concatenate([buf[q * per + s] for s in range(per)], axis=1).astype(BF16)
        du = []
        for q in range(nq):
            io = slice(q * S5_IN, (q + 1) * S5_IN)
            lq_r, lq_i = wide(lam_r, q), wide(lam_i, q)
            du.append(lax.dot_general(lq_r, bbr_ref[q], _NT, preferred_element_type=F32)
                      + lax.dot_general(lq_i, bbi_ref[q], _NT, preferred_element_type=F32))
            dbbr_ref[q] += lax.dot_general(u[:, io], lq_r, _TN, preferred_element_type=F32)
            dbbi_ref[q] += lax.dot_general(u[:, io], lq_i, _TN, preferred_element_type=F32)
            dcr_ref[q] += lax.dot_general(wide(x_r, q), dy[:, io], _TN, preferred_element_type=F32)
            dci_ref[q] += lax.dot_general(wide(x_i, q), dy[:, io], _TN, preferred_element_type=F32)
        _store_deinterleaved(du_ref, jnp.concatenate(du, axis=1), tmp_a, tmp_b, seg)

    rows = lambda w: pl.BlockSpec((tl, w), lambda s, j, r: (s * nl + nl - 1 - r, j))
    state = pl.BlockSpec((nc, tl, LANES), lambda s, j, r: (j, s * nl + nl - 1 - r, 0))
    chunk = lambda a: pl.BlockSpec((nq,) + a.shape[1:], lambda s, j, r: (j, 0, 0))
    acc = lambda a: pl.BlockSpec((None, nq) + a.shape[1:], lambda s, j, r: (s, j, 0, 0))
    par = pl.BlockSpec((1, cb), lambda s, j, r: (0, j))
    par_acc = pl.BlockSpec((None, 1, cb), lambda s, j, r: (s, 0, j))
    per_seq = lambda a: jax.ShapeDtypeStruct((seqs,) + a.shape, F32)
    return pl.pallas_call(
        body, name="s5_bwd", grid=(seqs, S5_CH // cb, nl),
        in_specs=[rows(nq * S5_IN), rows(nq * S5_IN), state, state, chunk(bbr), chunk(bbi), chunk(cr), chunk(ci),
                  par, par],
        out_specs=[rows(nq * S5_IN), acc(bbr), acc(bbi), acc(cr), acc(ci), par_acc, par_acc],
        out_shape=[jax.ShapeDtypeStruct((t, S5_WIDTH), F32), per_seq(bbr), per_seq(bbi), per_seq(cr), per_seq(ci),
                   jax.ShapeDtypeStruct((seqs, 1, S5_CH), F32), jax.ShapeDtypeStruct((seqs, 1, S5_CH), F32)],
        scratch_shapes=[pltpu.VMEM((1, cb), F32), pltpu.VMEM((1, cb), F32), pltpu.VMEM((seg, cb), F32),
                        pltpu.VMEM((seg, cb), F32)] + [pltpu.VMEM((nc, tl, LANES), F32)] * 4
        + [pltpu.VMEM((nc, SCAN_SEGS, LANES), F32)] * 2 + [pltpu.VMEM((nq * S5_IN // LANES, tl, LANES), F32)] * 2,
        compiler_params=_params(("parallel", "parallel", "arbitrary")),
    )(dys, uf, xr, xi, bbr, bbi, cr, ci, ar, ai)


XATT_BLOCK = 2048


def _xatt_probs(qv, kv):
    s = lax.dot_general(qv, kv, _NT, preferred_element_type=F32) * (X_HEAD_DIM ** -0.5)
    e = jnp.exp(s - jnp.max(s, axis=-1, keepdims=True))
    return e / jnp.sum(e, axis=-1, keepdims=True)


def _xatt_fwd(q, k, kv, seqs):
    t = q.shape[0]
    tq = min(XATT_BLOCK, t // seqs)
    nq = t // seqs // tq

    def body(q_ref, k_ref, v_ref, o_ref):
        p = _xatt_probs(q_ref[...], k_ref[...])
        o_ref[...] = jnp.dot(p.astype(BF16), v_ref[...].astype(BF16), preferred_element_type=F32).astype(o_ref.dtype)

    qs = pl.BlockSpec((tq, X_HEAD_DIM), lambda b, h, i: (b * nq + i, h))
    return pl.pallas_call(
        body, name="xatt_fwd", grid=(seqs, N_X_HEADS, nq),
        in_specs=[qs, pl.BlockSpec((N_MEM, X_HEAD_DIM), lambda b, h, i: (b, h)),
                  pl.BlockSpec((N_MEM, X_HEAD_DIM), lambda b, h, i: (b, N_X_HEADS + h))],
        out_specs=qs, out_shape=jax.ShapeDtypeStruct(q.shape, BF16),
        compiler_params=_params(("parallel", "parallel", "parallel")),
    )(q, k, kv)


def _xatt_bwd(q, k, kv, do, seqs):
    t = q.shape[0]
    tq = min(XATT_BLOCK, t // seqs)
    nq = t // seqs // tq
    scale = X_HEAD_DIM ** -0.5

    def body(q_ref, k_ref, v_ref, do_ref, dq_ref, dk_ref, dv_ref):
        @pl.when(pl.program_id(2) == 0)
        def _():
            dk_ref[...] = jnp.zeros(dk_ref.shape, F32)
            dv_ref[...] = jnp.zeros(dv_ref.shape, F32)

        qv, kk = q_ref[...], k_ref[...]
        p = _xatt_probs(qv, kk)
        dob = do_ref[...].astype(BF16)
        dp = lax.dot_general(dob, v_ref[...].astype(BF16), _NT, preferred_element_type=F32)
        ds = p * (dp - jnp.sum(dp * p, axis=-1, keepdims=True))
        dsb = ds.astype(BF16)
        dq_ref[...] = jnp.dot(dsb, kk, preferred_element_type=F32) * scale
        dk_ref[...] += lax.dot_general(dsb, qv, _TN, preferred_element_type=F32) * scale
        dv_ref[...] += lax.dot_general(p.astype(BF16), dob, _TN, preferred_element_type=F32)

    qs = pl.BlockSpec((tq, X_HEAD_DIM), lambda b, h, i: (b * nq + i, h))
    ks = pl.BlockSpec((N_MEM, X_HEAD_DIM), lambda b, h, i: (b, h))
    return pl.pallas_call(
        body, name="xatt_bwd", grid=(seqs, N_X_HEADS, nq),
        in_specs=[qs, ks, pl.BlockSpec((N_MEM, X_HEAD_DIM), lambda b, h, i: (b, N_X_HEADS + h)), qs],
        out_specs=[qs, ks, ks],
        out_shape=[jax.ShapeDtypeStruct(q.shape, F32), jax.ShapeDtypeStruct(k.shape, F32),
                   jax.ShapeDtypeStruct(k.shape, F32)],
        compiler_params=_params(("parallel", "parallel", "arbitrary")),
    )(q, k, kv, do)


CONV_COLS = 256


def _shift_down(x, k, row):
    return jnp.where(row >= k, pltpu.roll(x, k, 0), 0.0)


def _shift_up(x, k, row):
    n = x.shape[0]
    return jnp.where(row < n - k, pltpu.roll(x, n - k, 0), 0.0)


def _down_from(x, prev, k, row):
    return jnp.where(row >= k, pltpu.roll(x, k, 0), pltpu.roll(prev, k, 0))


GATE_ROWS = 512


def _ffn_up_gate(hn, w_up, w, b, seqs):
    t = hn.shape[0]
    l = t // seqs
    nc = D_FF // CONV_COLS

    rc = min(GATE_ROWS, l)

    def body(a_ref, wg_ref, wu_ref, w_ref, b_ref, g_ref, u_ref, o_ref):
        wv, bias = w_ref[...], b_ref[...]
        row = lax.broadcasted_iota(jnp.int32, (rc, CONV_COLS), 0)
        prev = jnp.zeros((rc, CONV_COLS), F32)
        for k in range(l // rc):
            rows = slice(k * rc, (k + 1) * rc)
            a = a_ref[rows, :]
            gb = jnp.dot(a, wg_ref[...], preferred_element_type=F32).astype(BF16)
            ub = jnp.dot(a, wu_ref[...], preferred_element_type=F32).astype(BF16)
            g_ref[rows, :] = gb
            u_ref[rows, :] = ub
            g = gb.astype(F32)
            pre = bias + wv[0:1, :] * _down_from(g, prev, 2, row) + wv[1:2, :] * _down_from(g, prev, 1, row) \
                + wv[2:3, :] * g
            o_ref[rows, :] = (pre * jax.nn.sigmoid(pre) * ub.astype(F32)).astype(o_ref.dtype)
            prev = g

    cols = pl.BlockSpec((l, CONV_COLS), lambda s, j: (s, j))
    half = jax.ShapeDtypeStruct((t, D_FF), BF16)
    return pl.pallas_call(
        body, name="ffn_up_gate", grid=(seqs, nc),
        in_specs=[pl.BlockSpec((l, hn.shape[1]), lambda s, j: (s, 0)),
                  pl.BlockSpec((hn.shape[1], CONV_COLS), lambda s, j: (0, j)),
                  pl.BlockSpec((hn.shape[1], CONV_COLS), lambda s, j: (0, nc + j)),
                  pl.BlockSpec((3, CONV_COLS), lambda s, j: (0, j)), pl.BlockSpec((1, CONV_COLS), lambda s, j: (0, j))],
        out_specs=[cols, cols, cols], out_shape=[half, half, half],
        compiler_params=_params(("parallel", "parallel")),
    )(hn, w_up, w_up, w, b)


def _ffn_down_dx_gate(dh, w_down, gate, up, w, b, seqs):
    t = dh.shape[0]
    l = t // seqs
    nc = D_FF // CONV_COLS
    steps = nc * seqs

    def body(dh_ref, wd_ref, g_ref, u_ref, w_ref, b_ref, dgu_ref, dw_ref, db_ref, stage, sems):
        s, j = pl.program_id(0), pl.program_id(1)
        n = s * nc + j
        slot = n % 2

        def copies(slot_, j_, s_):
            rows = pl.ds(pl.multiple_of(s_ * l, 16), l)
            return [pltpu.make_async_copy(
                stage.at[slot_, half],
                dgu_ref.at[rows, pl.ds(pl.multiple_of((half * nc + j_) * CONV_COLS, 128), CONV_COLS)],
                sems.at[slot_, half]) for half in (0, 1)]

        @pl.when(n >= 2)
        def _():
            for cp in copies(slot, j, s):
                cp.wait()

        da = lax.dot_general(dh_ref[...], wd_ref[...], _NT, preferred_element_type=F32)
        g, wv = g_ref[...].astype(F32), w_ref[...]
        row = lax.broadcasted_iota(jnp.int32, g.shape, 0)
        g1, g2 = _shift_down(g, 1, row), _shift_down(g, 2, row)
        pre = b_ref[...] + wv[0:1, :] * g2 + wv[1:2, :] * g1 + wv[2:3, :] * g
        sg = jax.nn.sigmoid(pre)
        silu = pre * sg
        stage[slot, 1] = (da * silu).astype(stage.dtype)
        dpre = da * u_ref[...].astype(F32) * (sg * (1.0 + pre * (1.0 - sg)))
        dg = wv[2:3, :] * dpre + wv[1:2, :] * _shift_up(dpre, 1, row) + wv[0:1, :] * _shift_up(dpre, 2, row)
        stage[slot, 0] = dg.astype(stage.dtype)
        for cp in copies(slot, j, s):
            cp.start()
        dw_ref[0:1, :] = jnp.sum(dpre * g2, axis=0, keepdims=True)
        dw_ref[1:2, :] = jnp.sum(dpre * g1, axis=0, keepdims=True)
        dw_ref[2:3, :] = jnp.sum(dpre * g, axis=0, keepdims=True)
        db_ref[...] = jnp.sum(dpre, axis=0, keepdims=True)

        @pl.when(n == steps - 1)
        def _():
            for cp in copies(slot, j, s) + (copies(1 - slot, j, s) if steps > 1 else []):
                cp.wait()

    cols = pl.BlockSpec((l, CONV_COLS), lambda s, j: (s, j))
    return pl.pallas_call(
        body, name="ffn_down_dx_gate", grid=(seqs, nc),
        in_specs=[pl.BlockSpec((l, dh.shape[1]), lambda s, j: (s, 0)),
                  pl.BlockSpec((CONV_COLS, dh.shape[1]), lambda s, j: (j, 0)), cols, cols,
                  pl.BlockSpec((3, CONV_COLS), lambda s, j: (0, j)), pl.BlockSpec((1, CONV_COLS), lambda s, j: (0, j))],
        out_specs=[ANY, pl.BlockSpec((None, 3, CONV_COLS), lambda s, j: (s, 0, j)),
                   pl.BlockSpec((None, 1, CONV_COLS), lambda s, j: (s, 0, j))],
        out_shape=[jax.ShapeDtypeStruct((t, 2 * D_FF), BF16), jax.ShapeDtypeStruct((seqs, 3, D_FF), F32),
                   jax.ShapeDtypeStruct((seqs, 1, D_FF), F32)],
        scratch_shapes=[pltpu.VMEM((2, 2, l, CONV_COLS), BF16), pltpu.SemaphoreType.DMA((2, 2))],
        compiler_params=_params(("arbitrary", "arbitrary")),
    )(dh, w_down, gate, up, w, b)


def _loss_head(h, target):
    t, d = h.shape
    tm = _pick(t, (256, 128, 8))

    def body(h_ref, t_ref, dh_ref, dhb_ref, loss_ref):
        @pl.when(pl.program_id(0) == 0)
        def _():
            loss_ref[...] = jnp.zeros(loss_ref.shape, F32)

        e = h_ref[...] - t_ref[...]
        dh = e * (1.0 / d)
        dh_ref[...] = dh
        dhb_ref[...] = dh.astype(BF16)
        loss_ref[...] += (0.5 / d) * jnp.sum(jnp.sum(e * e, axis=1, keepdims=True), axis=0, keepdims=True)

    blk = pl.BlockSpec((tm, d), lambda i: (i, 0))
    return pl.pallas_call(
        body, name="loss_head", grid=(t // tm,), in_specs=[blk, blk],
        out_specs=[blk, blk, pl.BlockSpec((1, 1), lambda i: (0, 0))],
        out_shape=[jax.ShapeDtypeStruct((t, d), F32), jax.ShapeDtypeStruct((t, d), BF16),
                   jax.ShapeDtypeStruct((1, 1), F32)],
        compiler_params=_params(("arbitrary",)),
    )(h, target)


def _s5_discretise(a_re, a_im, log_dt, b_re, b_im):
    dt = jnp.exp(log_dt)[:, None]
    mag = jnp.exp(a_re * dt)
    lb_r = mag * jnp.cos(a_im * dt)
    lb_i = mag * jnp.sin(a_im * dt)
    den = a_re * a_re + a_im * a_im
    nr = lb_r - 1.0
    coef_r = (nr * a_re + lb_i * a_im) / den
    coef_i = (lb_i * a_re - nr * a_im) / den
    bb_r = coef_r[:, :, None] * b_re - coef_i[:, :, None] * b_im
    bb_i = coef_r[:, :, None] * b_im + coef_i[:, :, None] * b_re
    return lb_r, lb_i, bb_r, bb_i


S5_CHUNKS = 4
S5_PER = S5_GROUPS // S5_CHUNKS


def _blockdiag_in(bb):
    eye = jnp.eye(S5_PER, dtype=bb.dtype)
    return jnp.einsum("jgpc,gh->jgchp", bb.reshape(S5_CHUNKS, S5_PER, S5_STATE, S5_GROUP_CH), eye).reshape(
        S5_CHUNKS, S5_PER * S5_GROUP_CH, S5_PER * S5_STATE)


def _blockdiag_in_grad(d):
    eye = jnp.eye(S5_PER, dtype=d.dtype)
    return jnp.einsum("jgchp,gh->jgpc", d.reshape(S5_CHUNKS, S5_PER, S5_GROUP_CH, S5_PER, S5_STATE), eye).reshape(
        S5_GROUPS, S5_STATE, S5_GROUP_CH)


def _blockdiag_out(c):
    eye = jnp.eye(S5_PER, dtype=c.dtype)
    return jnp.einsum("jgcp,gh->jgphc", c.reshape(S5_CHUNKS, S5_PER, S5_GROUP_CH, S5_STATE), eye).reshape(
        S5_CHUNKS, S5_PER * S5_STATE, S5_PER * S5_GROUP_CH)


def _blockdiag_out_grad(d):
    eye = jnp.eye(S5_PER, dtype=d.dtype)
    return jnp.einsum("jgphc,gh->jgcp", d.reshape(S5_CHUNKS, S5_PER, S5_STATE, S5_PER, S5_GROUP_CH), eye).reshape(
        S5_GROUPS, S5_GROUP_CH, S5_STATE)


def _local_step(x3, mem3, target3, p, wb, late_weights=None, early_grads=None):
    seqs, l, d = x3.shape
    t = seqs * l
    x = x3.reshape(t, d)
    mem = mem3.reshape(seqs * N_MEM, d)
    target = target3.reshape(t, d)
    full = lambda a: (a, a.shape[1], 0, 0)

    s5_in = (p["s5_a_re"], p["s5_a_im"], p["s5_log_dt"], p["s5_b_re"], p["s5_b_im"])
    (lb_r, lb_i, bb_r, bb_i), s5_pull = jax.vjp(_s5_discretise, *s5_in)
    ar, ai = lb_r.reshape(1, S5_CH), lb_i.reshape(1, S5_CH)
    bbr_d, bbi_d = _blockdiag_in(bb_r).astype(BF16), _blockdiag_in(bb_i).astype(BF16)
    cr_d, ci_d = _blockdiag_out(p["s5_c_re"]).astype(BF16), (-_blockdiag_out(p["s5_c_im"])).astype(BF16)
    d_row = p["s5_d"].reshape(1, S5_WIDTH)

    w_in = wb["w_in"]
    w_qkv = w_in[:, :3 * FOX_WIDTH]
    w_uf = jnp.concatenate(
        [w_in[:, 3 * FOX_WIDTH + N_FOX_HEADS:], w_in[:, 3 * FOX_WIDTH:3 * FOX_WIDTH + N_FOX_HEADS],
         jnp.zeros((d, UF_COLS - S5_WIDTH - N_FOX_HEADS), w_in.dtype)], axis=1)

    hn1 = _rowwise(_rms, [full(x)], [p["norm_mix"]], [(d, d, 0, BF16)], "norm_mix_fwd")
    qkv = _mm(hn1, w_qkv, "nn", "in_qkv")
    uf = _mm(hn1, w_uf, "nn", "in_uf")

    bh = seqs * N_FOX_HEADS
    q_pair = (qkv, 128, 0, 1)
    k_pair = (qkv, 128, N_PAIRS, 1)
    gq2, gk2 = jnp.tile(p["fox_q_norm"], (1, 2)), jnp.tile(p["fox_k_norm"], (1, 2))
    pair_out = [(FOX_WIDTH, 128, 1, BF16)]
    qn = _rowwise(_rms_pair, [q_pair], [gq2], pair_out, "fox_qnorm_fwd", heads=N_PAIRS)
    kn = _rowwise(_rms_pair, [k_pair], [gk2], pair_out, "fox_knorm_fwd", heads=N_PAIRS)

    f_rows = uf[:, S5_WIDTH:S5_WIDTH + N_FOX_HEADS].reshape(seqs, l, N_FOX_HEADS).transpose(0, 2, 1).reshape(bh, l)
    f_bias = jnp.tile(p["fox_f_bias"].reshape(N_FOX_HEADS, 1), (seqs, 1))
    c_wide = jnp.broadcast_to(_forget_fwd(f_rows, f_bias)[:, :, None], (bh, l, 128))
    fox, lse = _fox_fwd(qn, kn, qkv, c_wide, seqs)

    xr, xi, ys = _s5_fwd(uf, bbr_d, bbi_d, cr_d, ci_d, ar, ai, seqs)
    u_blk = (uf, S5_WIDTH, 0, 0)
    yg = _rowwise(_s5_act, [full(ys), u_blk], [d_row], [(S5_WIDTH, S5_WIDTH, 0, F32)], "s5_act_fwd")
    if late_weights is not None:
        wb = dict(wb, **late_weights("mid", yg))
    z = _mm(yg, wb["s5_w_glu"], "nn", "s5_glu")
    y2n = _rowwise(_s5_gate, [full(yg), full(z)], [p["s5_b_glu"], p["out_norm_s5"]],
                   [(S5_WIDTH, S5_WIDTH, 0, BF16)], "s5_gate_fwd")
    foxn = _rowwise(_rms, [full(fox)], [p["out_norm_fox"]], [(FOX_WIDTH, FOX_WIDTH, 0, BF16)], "fox_outnorm_fwd")
    mixed = jnp.concatenate([foxn, y2n], axis=1)
    h1 = _mm(mixed, wb["w_out"], "nn", "mix_out", res=x)
    if late_weights is not None:
        wb = dict(wb, **late_weights("late", h1))

    hn2 = _rowwise(_rms, [full(h1)], [p["norm_cross"]], [(d, d, 0, BF16)], "norm_cross_fwd")
    mn = _rowwise(_rms, [full(mem)], [p["norm_mem"]], [(d, d, 0, BF16)], "norm_mem_fwd")
    xq_raw = _mm(hn2, wb["w_xq"], "nn", "x_q")
    kv = _mm(mn, wb["w_xkv"], "nn", "x_kv")
    xh = lambda a: (a, X_HEAD_DIM, 0, 1)
    xqn = _rowwise(_rms, [xh(xq_raw)], [p["xq_norm"]], [(d, X_HEAD_DIM, 1, BF16)], "x_qnorm_fwd", heads=N_X_HEADS)
    xkn = _rowwise(_rms, [xh(kv)], [p["xk_norm"]], [(d, X_HEAD_DIM, 1, BF16)], "x_knorm_fwd", heads=N_X_HEADS)
    xo = _xatt_fwd(xqn, xkn, kv, seqs)
    h2 = _mm(xo, wb["w_xo"], "nn", "x_out", res=h1)

    hn3 = _rowwise(_rms, [full(h2)], [p["norm_ffn"]], [(d, d, 0, BF16)], "norm_ffn_fwd")
    gate, up, act = _ffn_up_gate(hn3, wb["w_ffn_up"], p["ffn_conv_w"], p["ffn_conv_b"], seqs)
    h3 = _mm(act, wb["w_ffn_down"], "nn", "ffn_down", res=h2)
    dh3, dh3_b, loss = _loss_head(h3, target)

    g = {}
    late_dt = BF16 if early_grads is not None else F32
    g["w_ffn_down"] = _mm(act, dh3_b, "tn", "ffn_down_dw", out_dtype=late_dt)
    dgu, dconv_w, dconv_b = _ffn_down_dx_gate(dh3_b, wb["w_ffn_down"], gate, up, p["ffn_conv_w"], p["ffn_conv_b"], seqs)
    g["ffn_conv_w"], g["ffn_conv_b"] = jnp.sum(dconv_w, axis=0), jnp.sum(dconv_b, axis=0)
    dhn3 = _mm(dgu, wb["w_ffn_up"], "nt", "ffn_up_dx", out_dtype=BF16)
    g["w_ffn_up"] = _mm(hn3, dgu, "tn", "ffn_up_dw", out_dtype=late_dt)
    (dh2,), (g["norm_ffn"],) = _rowwise_vjp(_rms, [full(h2)], [p["norm_ffn"]], [full(dhn3)], "norm_ffn_bwd",
                                            adds=[full(dh3)])

    dxo = _mm(dh2, wb["w_xo"], "nt", "x_out_dx", out_dtype=BF16)
    g["w_xo"] = _mm(xo, dh2, "tn", "x_out_dw", out_dtype=late_dt)
    dxqn, dxkn, dxv = _xatt_bwd(xqn, xkn, kv, dxo, seqs)
    (dxq_raw,), (g["xq_norm"],) = _rowwise_vjp(_rms, [xh(xq_raw)], [p["xq_norm"]], [xh(dxqn)], "x_qnorm_bwd",
                                               heads=N_X_HEADS, row_dtypes=[BF16])
    (dxk_raw,), (g["xk_norm"],) = _rowwise_vjp(_rms, [xh(kv)], [p["xk_norm"]], [xh(dxkn)], "x_knorm_bwd",
                                               heads=N_X_HEADS, row_dtypes=[BF16])
    dkv = jnp.concatenate([dxk_raw, dxv.astype(BF16)], axis=1)
    dhn2 = _mm(dxq_raw, wb["w_xq"], "nt", "x_q_dx", out_dtype=BF16)
    g["w_xq"] = _mm(hn2, dxq_raw, "tn", "x_q_dw", out_dtype=late_dt)
    dmn = _mm(dkv, wb["w_xkv"], "nt", "x_kv_dx")
    g["w_xkv"] = _mm(mn, dkv, "tn", "x_kv_dw", out_dtype=late_dt)
    norm_cross = p["norm_cross"]
    if early_grads is not None:
        norm_cross = norm_cross + early_grads({n: g[n] for n in LATE_WEIGHTS})
    (dh1,), (g["norm_cross"],) = _rowwise_vjp(_rms, [full(h1)], [norm_cross], [full(dhn2)], "norm_cross_bwd",
                                              adds=[full(dh2)])
    _, (g["norm_mem"],) = _rowwise_vjp(_rms, [full(mem)], [p["norm_mem"]], [full(dmn)], "norm_mem_bwd",
                                       row_dtypes=[BF16])

    dmixed = _mm(dh1, wb["w_out"], "nt", "mix_out_dx", out_dtype=BF16)
    g["w_out"] = _mm(mixed, dh1, "tn", "mix_out_dw", out_dtype=late_dt)
    (dfox,), (g["out_norm_fox"],) = _rowwise_vjp(_rms, [full(fox)], [p["out_norm_fox"]],
                                                 [(dmixed, FOX_WIDTH, 0, 0)], "fox_outnorm_bwd")
    (dyg_a, dz), (g["s5_b_glu"], g["out_norm_s5"]) = _rowwise_vjp(
        _s5_gate, [full(yg), full(z)], [p["s5_b_glu"], p["out_norm_s5"]], [(dmixed, S5_WIDTH, 1, 0)], "s5_gate_bwd",
        row_dtypes=[F32, BF16])
    dyg = _mm(dz, wb["s5_w_glu"], "nt", "s5_glu_dx", res=dyg_a)
    g["s5_w_glu"] = _mm(yg, dz, "tn", "s5_glu_dw", out_dtype=late_dt)
    (dys, du_a), (dd_row,) = _rowwise_vjp(_s5_act, [full(ys), u_blk], [d_row], [full(dyg)], "s5_act_bwd",
                                          row_dtypes=[BF16, F32])
    g["s5_d"] = dd_row
    du_b, dbbr_d, dbbi_d, dcr_d, dci_d, dar, dai = _s5_bwd(dys, uf, xr, xi, bbr_d, bbi_d, cr_d, ci_d, ar, ai, seqs)
    dbbr_d, dbbi_d, dcr_d, dci_d = (jnp.sum(a, axis=0) for a in (dbbr_d, dbbi_d, dcr_d, dci_d))
    d_lb_r = jnp.sum(dar, axis=0).reshape(S5_GROUPS, S5_STATE)
    d_lb_i = jnp.sum(dai, axis=0).reshape(S5_GROUPS, S5_STATE)
    g["s5_a_re"], g["s5_a_im"], g["s5_log_dt"], g["s5_b_re"], g["s5_b_im"] = s5_pull(
        (d_lb_r, d_lb_i, _blockdiag_in_grad(dbbr_d), _blockdiag_in_grad(dbbi_d)))
    g["s5_c_re"] = _blockdiag_out_grad(dcr_d)
    g["s5_c_im"] = -_blockdiag_out_grad(dci_d)

    dqn, dkn, dv, dc, dcq = _fox_bwd(qn, kn, qkv, c_wide, fox, dfox, lse, seqs)
    pair = lambda a: (a, 128, 0, 1)
    (dq_raw,), (dgq2,) = _rowwise_vjp(_rms_pair, [q_pair], [gq2], [pair(dqn)], "fox_qnorm_bwd", heads=N_PAIRS,
                                      row_dtypes=[BF16])
    (dk_raw,), (dgk2,) = _rowwise_vjp(_rms_pair, [k_pair], [gk2], [pair(dkn)], "fox_knorm_bwd", heads=N_PAIRS,
                                      row_dtypes=[BF16])
    g["fox_q_norm"] = dgq2[:, :HEAD_DIM] + dgq2[:, HEAD_DIM:]
    g["fox_k_norm"] = dgk2[:, :HEAD_DIM] + dgk2[:, HEAD_DIM:]
    df_rows, dfb = _forget_bwd(f_rows, f_bias, (dc + dcq).reshape(bh, l))
    g["fox_f_bias"] = jnp.sum(dfb.reshape(seqs, N_FOX_HEADS), axis=0)
    df = df_rows.reshape(seqs, N_FOX_HEADS, l).transpose(0, 2, 1).reshape(t, N_FOX_HEADS)
    dqkv = jnp.concatenate([dq_raw, dk_raw, dv.astype(BF16)], axis=1)
    duf = jnp.concatenate([du_a + du_b, df, jnp.zeros((t, UF_COLS - S5_WIDTH - N_FOX_HEADS), F32)],
                          axis=1).astype(BF16)
    dhn1 = _mm(duf, w_uf, "nt", "in_uf_dx", res=_mm(dqkv, w_qkv, "nt", "in_qkv_dx"), out_dtype=BF16)
    dw_qkv = _mm(hn1, dqkv, "tn", "in_qkv_dw")
    dw_uf = _mm(hn1, duf, "tn", "in_uf_dw")
    g["w_in"] = jnp.concatenate([dw_qkv, dw_uf[:, S5_WIDTH:S5_WIDTH + N_FOX_HEADS], dw_uf[:, :S5_WIDTH]], axis=1)
    (dx,), (g["norm_mix"],) = _rowwise_vjp(_rms, [full(x)], [p["norm_mix"]], [full(dhn1)], "norm_mix_bwd",
                                           adds=[full(dh1)])
    return loss, dx.reshape(seqs, l, d), g


def _place():
    return lax.axis_index("x"), lax.axis_index("y"), lax.axis_index("c")


def _other_chips(x, y):
    return [(1 - x, y), (x, 1 - y), (1 - x, 1 - y)]


ANY = pl.BlockSpec(memory_space=pl.ANY)


def _gather_weights(shards, col_kind, taps):
    n = len(shards)

    def body(*refs):
        ins, tap_in, outs, tap_out = refs[:n], refs[n], refs[n + 1:2 * n + 1], refs[2 * n + 1]
        ici_send, ici_recv, d2d_send, d2d_recv, own_send, own_recv = refs[2 * n + 2:]
        x, y, c = _place()
        mine = 2 * x + y
        chips = _other_chips(x, y)
        sibling = (x, y, 1 - c)

        def piece(a, s, h):
            r, cs = ins[a].shape
            hr = r // 2
            if col_kind[a]:
                return outs[a].at[pl.ds(pl.multiple_of(h * hr, 16), hr), pl.ds(pl.multiple_of(s * cs, 128), cs)]
            return outs[a].at[pl.ds(pl.multiple_of(s * r + h * hr, 16), hr), :]

        def slab(a, s):
            r, cs = ins[a].shape
            if col_kind[a]:
                return outs[a].at[:, pl.ds(pl.multiple_of(s * cs, 128), cs)]
            return outs[a].at[pl.ds(pl.multiple_of(s * r, 16), r), :]

        def own_half(a, h):
            hr = ins[a].shape[0] // 2
            return ins[a].at[pl.ds(pl.multiple_of(h * hr, 16), hr), :]

        sends = []
        for a in range(n):
            cp = pltpu.make_async_remote_copy(
                src_ref=ins[a], dst_ref=slab(a, mine), send_sem=own_send.at[a], recv_sem=own_recv.at[a],
                device_id=sibling, device_id_type=MESH)
            cp.start()
            sends.append(cp)
        cp = pltpu.make_async_remote_copy(
            src_ref=tap_in, dst_ref=tap_out.at[mine], send_sem=own_send.at[n], recv_sem=own_recv.at[n],
            device_id=sibling, device_id_type=MESH)
        cp.start()
        sends.append(cp)
        for a in range(n):
            for j, (px, py) in enumerate(chips):
                cp = pltpu.make_async_remote_copy(
                    src_ref=own_half(a, c), dst_ref=piece(a, mine, c), send_sem=ici_send.at[3 * a + j],
                    recv_sem=ici_recv.at[3 * a + j], device_id=(px, py, c), device_id_type=MESH)
                cp.start()
                sends.append(cp)
        for j, (px, py) in enumerate(chips):
            cp = pltpu.make_async_remote_copy(
                src_ref=tap_in, dst_ref=tap_out.at[mine], send_sem=ici_send.at[3 * n + j],
                recv_sem=ici_recv.at[3 * n + j], device_id=(px, py, c), device_id_type=MESH)
            cp.start()
            sends.append(cp)
        for a in range(n):
            for j, (px, py) in enumerate(chips):
                got = piece(a, 2 * px + py, c)
                pltpu.make_async_remote_copy(
                    src_ref=got, dst_ref=got, send_sem=ici_send.at[3 * a + j], recv_sem=ici_recv.at[3 * a + j],
                    device_id=(px, py, c), device_id_type=MESH).wait_recv()
                fwd = pltpu.make_async_remote_copy(
                    src_ref=got, dst_ref=got, send_sem=d2d_send.at[3 * a + j], recv_sem=d2d_recv.at[3 * a + j],
                    device_id=(x, y, 1 - c), device_id_type=MESH)
                fwd.start()
                sends.append(fwd)
        for a in range(n):
            for j, (px, py) in enumerate(chips):
                other = piece(a, 2 * px + py, 1 - c)
                pltpu.make_async_remote_copy(
                    src_ref=other, dst_ref=other, send_sem=d2d_send.at[3 * a + j], recv_sem=d2d_recv.at[3 * a + j],
                    device_id=(x, y, 1 - c), device_id_type=MESH).wait_recv()
        for j, (px, py) in enumerate(chips):
            pltpu.make_async_remote_copy(
                src_ref=tap_in, dst_ref=tap_out.at[2 * px + py], send_sem=ici_send.at[3 * n + j],
                recv_sem=ici_recv.at[3 * n + j], device_id=(px, py, c), device_id_type=MESH).wait_recv()
        for a in range(n):
            pltpu.make_async_remote_copy(
                src_ref=ins[a], dst_ref=slab(a, mine), send_sem=own_send.at[a], recv_sem=own_recv.at[a],
                device_id=sibling, device_id_type=MESH).wait_recv()
        pltpu.make_async_remote_copy(
            src_ref=tap_in, dst_ref=tap_out.at[mine], send_sem=own_send.at[n], recv_sem=own_recv.at[n],
            device_id=sibling, device_id_type=MESH).wait_recv()
        for cp in sends:
            cp.wait_send()

    def full_shape(a):
        r, cs = shards[a].shape
        return (r, 4 * cs) if col_kind[a] else (4 * r, cs)

    res = pl.pallas_call(
        body, name="gather_weights", in_specs=[ANY] * (n + 1), out_specs=[ANY] * (n + 1),
        out_shape=[jax.ShapeDtypeStruct(full_shape(a), shards[a].dtype) for a in range(n)]
        + [jax.ShapeDtypeStruct((4,) + taps.shape, taps.dtype)],
        scratch_shapes=[pltpu.SemaphoreType.DMA((3 * n + 3,)), pltpu.SemaphoreType.DMA((3 * n + 3,)),
                        pltpu.SemaphoreType.DMA((3 * n,)), pltpu.SemaphoreType.DMA((3 * n,)),
                        pltpu.SemaphoreType.DMA((n + 1,)), pltpu.SemaphoreType.DMA((n + 1,))],
        compiler_params=pltpu.CompilerParams(has_side_effects=True),
    )(*shards, taps)
    return res[:n], res[n]


HBM = pl.BlockSpec(memory_space=pltpu.HBM)
SEM = pl.BlockSpec(memory_space=pltpu.SEMAPHORE)
DATAFLOW = pltpu.SideEffectType.DATAFLOW_SIDE_EFFECTING


def _in_hbm(a):
    return pltpu.with_memory_space_constraint(a, pltpu.HBM)


def _split_start(name, srcs, lands, n_copies, plan):
    n = len(srcs)

    def body(*refs):
        src_refs, land_refs = refs[:n], refs[n:2 * n]
        send_sems, recv_sems = refs[2 * n], refs[2 * n + 1]
        for i, (src, dst, dev) in enumerate(plan(src_refs, land_refs)):
            pltpu.make_async_remote_copy(src_ref=src, dst_ref=dst, send_sem=send_sems.at[i], recv_sem=recv_sems.at[i],
                                         device_id=dev, device_id_type=MESH).start()
        refs[-1][...] = jnp.zeros((8, 128), F32)

    res = pl.pallas_call(
        body, name=name, in_specs=[HBM] * (2 * n),
        out_specs=[SEM, SEM] + [HBM] * (2 * n) + [pl.BlockSpec(memory_space=pltpu.VMEM)],
        out_shape=[pltpu.SemaphoreType.DMA((n_copies,)), pltpu.SemaphoreType.DMA((n_copies,))]
        + [pltpu.HBM(a.shape, a.dtype) for a in list(srcs) + list(lands)] + [jax.ShapeDtypeStruct((8, 128), F32)],
        input_output_aliases={i: 2 + i for i in range(2 * n)},
        compiler_params=pltpu.CompilerParams(has_side_effects=DATAFLOW),
    )(*[_in_hbm(a) for a in list(srcs) + list(lands)])
    return res[0], res[1], list(res[2:2 + n]), list(res[2 + n:2 + 2 * n]), res[-1]


def _split_wait(name, send_sems, recv_sems, srcs, lands, after, plan):
    n = len(srcs)

    def body(*refs):
        src_refs, land_refs = refs[:n], refs[n:2 * n]
        send_ref, recv_ref = refs[2 * n], refs[2 * n + 1]
        for i, (src, dst, dev) in enumerate(plan(src_refs, land_refs)):
            cp = pltpu.make_async_remote_copy(src_ref=src, dst_ref=dst, send_sem=send_ref.at[i], recv_sem=recv_ref.at[i],
                                              device_id=dev, device_id_type=MESH)
            cp.wait_send()
            cp.wait_recv()

    res = pl.pallas_call(
        body, name=name, in_specs=[HBM] * (2 * n) + [SEM, SEM, ANY], out_specs=[HBM] * (2 * n),
        out_shape=[pltpu.HBM(a.shape, a.dtype) for a in list(srcs) + list(lands)],
        input_output_aliases={i: i for i in range(2 * n)},
        compiler_params=pltpu.CompilerParams(has_side_effects=DATAFLOW),
    )(*srcs, *lands, send_sems, recv_sems, after)
    return list(res[:n]), list(res[n:])


def _late_gather_plan(col_kind):
    def plan(src_refs, land_refs):
        x, y, c = _place()
        mine = 2 * x + y
        copies = []
        for a, (src, land) in enumerate(zip(src_refs, land_refs)):
            r, cs = src.shape
            if col_kind[a]:
                dst = land.at[:, pl.ds(pl.multiple_of(mine * cs, 128), cs)]
            else:
                dst = land.at[pl.ds(pl.multiple_of(mine * r, 16), r), :]
            copies.append((src, dst, (x, y, 1 - c)))
            copies += [(src, dst, (px, py, c)) for (px, py) in _other_chips(x, y)]
        return copies
    return plan


def _late_reduce_plan(col_kind):
    def plan(src_refs, land_refs):
        x, y, c = _place()
        copies = []
        for a, (src, land) in enumerate(zip(src_refs, land_refs)):
            for j, (px, py) in enumerate(_other_chips(x, y)):
                if col_kind[a]:
                    cs = land.shape[2]
                    piece = src.at[:, pl.ds(pl.multiple_of((2 * px + py) * cs, 128), cs)]
                else:
                    piece = src.at[2 * px + py]
                copies.append((piece, land.at[j], (px, py, c)))
        return copies
    return plan


def _pair_swap(name, halves):
    n = len(halves)

    def body(*refs):
        ins, outs = refs[:n], refs[n:2 * n]
        send_sems, recv_sems = refs[2 * n:]
        x, y, c = _place()
        copies = []
        for a in range(n):
            cp = pltpu.make_async_remote_copy(
                src_ref=ins[a], dst_ref=outs[a], send_sem=send_sems.at[a], recv_sem=recv_sems.at[a],
                device_id=(x, y, 1 - c), device_id_type=MESH)
            cp.start()
            copies.append(cp)
        for cp in copies:
            cp.wait()

    return pl.pallas_call(
        body, name=name, in_specs=[ANY] * n, out_specs=[ANY] * n,
        out_shape=[jax.ShapeDtypeStruct(s.shape, s.dtype) for s in halves],
        scratch_shapes=[pltpu.SemaphoreType.DMA((n,)), pltpu.SemaphoreType.DMA((n,))],
        compiler_params=pltpu.CompilerParams(has_side_effects=True),
    )(*halves)


def _chip_sum(name, chip_sel, own, col, others):
    _, r, c = others.shape
    tr = _pick(r, (256, 128, 64, 32, 16))
    if col:
        own_spec = pl.BlockSpec((tr, c), lambda i, s: (i, s[0]))
    else:
        own_spec = pl.BlockSpec((None, tr, c), lambda i, s: (s[0], i, 0))
    specs = [own_spec] + [pl.BlockSpec((None, tr, c), lambda i, s, k=k: (k, i, 0)) for k in range(3)]

    def body(s_ref, own_ref, r0, r1, r2, o_ref):
        o_ref[...] = ((own_ref[...].astype(F32) + r0[...].astype(F32)) + r1[...].astype(F32)) + r2[...].astype(F32)

    return pl.pallas_call(
        body, name=name,
        grid_spec=pltpu.PrefetchScalarGridSpec(
            num_scalar_prefetch=1, grid=(r // tr,), in_specs=specs,
            out_specs=pl.BlockSpec((tr, c), lambda i, s: (i, 0))),
        out_shape=jax.ShapeDtypeStruct((r, c), F32),
        compiler_params=_params(("parallel",)),
    )(chip_sel, own, others, others, others)


def _allreduce_small(vals):
    sizes = [int(math.prod(v.shape)) for v in vals]
    padded = [-(-s // 128) * 128 for s in sizes]
    total = -(-sum(padded) // 1024) * 1024
    flat = [jnp.pad(v.reshape(-1), (0, p - s)) for v, s, p in zip(vals, sizes, padded)]
    flat.append(jnp.zeros((total - sum(padded),), F32))
    packed = jnp.concatenate(flat).reshape(total // 128, 128)

    def body(in_ref, out_ref, r0, r1, r2, send_sems, recv_sems):
        x, y, c = _place()
        out_ref[...] = in_ref[...]
        for k, (peer, land) in enumerate(zip([(x, y, 1 - c), (1 - x, y, c), (x, 1 - y, c)], (r0, r1, r2))):
            cp = pltpu.make_async_remote_copy(
                src_ref=out_ref, dst_ref=land, send_sem=send_sems.at[k], recv_sem=recv_sems.at[k],
                device_id=peer, device_id_type=MESH)
            cp.start()
            cp.wait()
            out_ref[...] = out_ref[...] + land[...]

    vm = pl.BlockSpec(memory_space=pltpu.VMEM)
    summed = pl.pallas_call(
        body, name="allreduce_small", in_specs=[vm], out_specs=vm,
        out_shape=jax.ShapeDtypeStruct(packed.shape, F32),
        scratch_shapes=[pltpu.VMEM(packed.shape, F32)] * 3
        + [pltpu.SemaphoreType.DMA((3,)), pltpu.SemaphoreType.DMA((3,))],
        compiler_params=pltpu.CompilerParams(has_side_effects=True, vmem_limit_bytes=VMEM_LIMIT_BYTES),
    )(packed).reshape(-1)
    outs, off = [], 0
    for v, s, p in zip(vals, sizes, padded):
        outs.append(summed[off:off + s].reshape(v.shape))
        off += p
    return outs


def _adamw_math(w, g, m, v):
    m2 = ADAM_B1 * m + (1.0 - ADAM_B1) * g
    v2 = ADAM_B2 * v + (1.0 - ADAM_B2) * (g * g)
    m_hat = m2 / (1.0 - ADAM_B1 ** ADAM_STEP)
    v_hat = v2 / (1.0 - ADAM_B2 ** ADAM_STEP)
    delta = -ADAM_LR * (m_hat / (jnp.sqrt(v_hat) + ADAM_EPS) + ADAM_WD * w)
    return delta, m2, v2


def _adamw_big(name, w, g_mine, g_sibling, m, v):
    _, r, c = w.shape

    def body(w_ref, ga_ref, gb_ref, m_ref, v_ref, go_ref, d_ref, mo_ref, vo_ref):
        gv = ga_ref[...] + gb_ref[...]
        d, m2, v2 = _adamw_math(w_ref[...], gv, m_ref[...], v_ref[...])
        go_ref[...] = gv
        d_ref[...] = d
        mo_ref[...] = m2
        vo_ref[...] = v2

    tr = _pick(r, (256, 128, 64, 32, 16, 8))
    if r % tr == 0 and tr % 8 == 0:
        grid = (r // tr,)
        blk = pl.BlockSpec((None, tr, c), lambda i: (0, i, 0))
        part = pl.BlockSpec((tr, c), lambda i: (i, 0))
    else:
        grid = (c // 512,)
        blk = pl.BlockSpec((None, r, 512), lambda i: (0, 0, i))
        part = pl.BlockSpec((r, 512), lambda i: (0, i))
    return pl.pallas_call(
        body, name=name, grid=grid, in_specs=[blk, part, part, blk, blk], out_specs=[blk] * 4,
        out_shape=[jax.ShapeDtypeStruct((1, r, c), F32)] * 4, compiler_params=_params(("parallel",)),
    )(w, g_mine, g_sibling, m, v)


def _adamw_small(ws, gs, ms, vs):
    n = len(ws)

    def body(*refs):
        w_r, g_r, m_r, v_r = refs[:n], refs[n:2 * n], refs[2 * n:3 * n], refs[3 * n:4 * n]
        o = refs[4 * n:]
        for a in range(n):
            gv = g_r[a][...]
            d, m2, v2 = _adamw_math(w_r[a][...], gv, m_r[a][...], v_r[a][...])
            o[a][...] = gv
            o[n + a][...] = d
            o[2 * n + a][...] = m2
            o[3 * n + a][...] = v2

    res = pl.pallas_call(
        body, name="adamw_small", out_shape=[jax.ShapeDtypeStruct(w.shape, F32) for _ in range(4) for w in ws],
        compiler_params=_params(),
    )(*ws, *gs, *ms, *vs)
    return res[:n], res[n:2 * n], res[2 * n:3 * n], res[3 * n:]


def _full_from_gathered(name, gathered):
    if name == "w_in":
        rows = gathered.shape[0] // 4
        return gathered.reshape(4, rows, gathered.shape[1]).transpose(1, 0, 2).reshape(rows, 4 * gathered.shape[1])
    return gathered


def _reduce_layout(name, full):
    if name in COL_KIND:
        return full
    if name == "w_in":
        rows, cols = full.shape
        return full.reshape(rows, 4, cols // 4).transpose(1, 0, 2)
    return full.reshape(4, full.shape[0] // 4, full.shape[1])


def kernel(x, mem, norm_mix, w_in, fox_q_norm, fox_k_norm, fox_f_bias, s5_a_re, s5_a_im, s5_log_dt, s5_b_re, s5_b_im, s5_c_re, s5_c_im, s5_d, s5_w_glu, s5_b_glu, out_norm_fox, out_norm_s5, w_out, norm_cross, norm_mem, w_xq, w_xkv, xq_norm, xk_norm, w_xo, norm_ffn, w_ffn_up, ffn_conv_w, ffn_conv_b, w_ffn_down, loss_target, m_norm_mix, m_w_in, m_fox_q_norm, m_fox_k_norm, m_fox_f_bias, m_s5_a_re, m_s5_a_im, m_s5_log_dt, m_s5_b_re, m_s5_b_im, m_s5_c_re, m_s5_c_im, m_s5_d, m_s5_w_glu, m_s5_b_glu, m_out_norm_fox, m_out_norm_s5, m_w_out, m_norm_cross, m_norm_mem, m_w_xq, m_w_xkv, m_xq_norm, m_xk_norm, m_w_xo, m_norm_ffn, m_w_ffn_up, m_ffn_conv_w, m_ffn_conv_b, m_w_ffn_down, v_norm_mix, v_w_in, v_fox_q_norm, v_fox_k_norm, v_fox_f_bias, v_s5_a_re, v_s5_a_im, v_s5_log_dt, v_s5_b_re, v_s5_b_im, v_s5_c_re, v_s5_c_im, v_s5_d, v_s5_w_glu, v_s5_b_glu, v_out_norm_fox, v_out_norm_s5, v_w_out, v_norm_cross, v_norm_mem, v_w_xq, v_w_xkv, v_xq_norm, v_xk_norm, v_w_xo, v_norm_ffn, v_w_ffn_up, v_ffn_conv_w, v_ffn_conv_b, v_w_ffn_down):
    given = dict(locals())
    w = {n: given[n] for n in WEIGHTS}
    m = {n: given["m_" + n] for n in WEIGHTS}
    v = {n: given["v_" + n] for n in WEIGHTS}
    xi, yi, _ = _place()
    chip = (2 * xi + yi).astype(jnp.int32)
    chip_sel = chip.reshape(1)
    early_kind = [n in COL_KIND for n in EARLY_WEIGHTS]
    late_kind = [n in COL_KIND for n in LATE_WEIGHTS]

    gathered, taps = _gather_weights([w[FIRST_WEIGHT][0].astype(BF16)], [False], w["ffn_conv_w"][0])
    first_full = gathered[0]
    conv_w = taps.transpose(1, 0, 2).reshape(3, D_FF)
    pending = {}
    g_started = None
    for stage, names in (("mid", MID_WEIGHTS), ("late", LATE_WEIGHTS)):
        kinds = [n in COL_KIND for n in names]
        shards = [w[n][0].astype(BF16) for n in names]
        if g_started is None:
            first_full, shards[0] = lax.optimization_barrier((first_full, shards[0]))
        else:
            shards[0] = shards[0] + g_started[0:1, 0:1].astype(BF16)
        full = [lax.empty((s.shape[0], 4 * s.shape[1]) if ck else (4 * s.shape[0], s.shape[1]), BF16)
                for s, ck in zip(shards, kinds)]
        plan = _late_gather_plan(kinds)
        send, recv, srcs, lands, g_started = _split_start(
            "gather_" + stage + "_start", shards, full, 4 * len(names), plan)
        pending[stage] = (names, plan, send, recv, srcs, lands)
    wb = {FIRST_WEIGHT: _full_from_gathered(FIRST_WEIGHT, first_full)}

    def late_weights(stage, after):
        names, plan, send, recv, srcs, lands = pending[stage]
        _, full = _split_wait("gather_" + stage + "_wait", send, recv, srcs, lands, after, plan)
        return dict(zip(names, full))

    reduce_plan = _late_reduce_plan(late_kind)
    late_reduce = {}

    def early_grads(late_g):
        grads = [_reduce_layout(n, late_g[n]) for n in LATE_WEIGHTS]
        lands = [lax.empty((3, s.shape[0], s.shape[1] // 4) if ck else (3,) + s.shape[1:], BF16)
                 for s, ck in zip(grads, late_kind)]
        late_reduce["sems"] = _split_start("reduce_late_start", grads, lands, 3 * len(LATE_WEIGHTS), reduce_plan)
        return late_reduce["sems"][4][0:1, 0:1]

    p = {n: w[n][0] for n in SMALL}
    p["ffn_conv_w"] = conv_w
    for n in ("norm_mix", "fox_q_norm", "fox_k_norm", "fox_f_bias", "s5_b_glu", "out_norm_fox", "out_norm_s5",
              "norm_cross", "norm_mem", "xq_norm", "xk_norm", "norm_ffn", "ffn_conv_b"):
        p[n] = p[n].reshape(1, -1)
    p["norm_mix"] = p["norm_mix"] + g_started[0:1, 0:1]
    loss, grad_x, g = _local_step(x, mem, loss_target, p, wb, late_weights, early_grads)

    grads = [_reduce_layout(n, g[n].astype(BF16)) for n in EARLY_WEIGHTS]
    early_lands = [lax.empty((3, s.shape[0], s.shape[1] // 4) if ck else (3,) + s.shape[1:], BF16)
                   for s, ck in zip(grads, early_kind)]
    early_plan = _late_reduce_plan(early_kind)
    e_send, e_recv, e_srcs, e_lands, e_started = _split_start(
        "reduce_early_start", grads, early_lands, 3 * len(EARLY_WEIGHTS), early_plan)

    out_g, out_d, out_m, out_v = {}, {}, {}, {}

    def finish(names, kinds, sums, from_chips, tag):
        mine = [_chip_sum("reduce_chip_sum_" + n, chip_sel, ps, ck, fc)
                for n, ps, fc, ck in zip(names, sums, from_chips, kinds)]
        theirs = _pair_swap("reduce_pair_swap_" + tag, mine)
        for n, a, b in zip(names, mine, theirs):
            if n == "w_in":
                flip = lambda t: jnp.swapaxes(t, -1, -2)
                res = _adamw_big("adamw_" + n, flip(w[n]), flip(a), flip(b), flip(m[n]), flip(v[n]))
                out_g[n], out_d[n], out_m[n], out_v[n] = (flip(t) for t in res)
                continue
            out_g[n], out_d[n], out_m[n], out_v[n] = _adamw_big("adamw_" + n, w[n], a, b, m[n], v[n])

    r_send, r_recv, r_srcs, r_lands, _ = late_reduce["sems"]
    late_sums, late_from_chips = _split_wait("reduce_late_wait", r_send, r_recv, r_srcs, r_lands, e_started,
                                             reduce_plan)
    finish(LATE_WEIGHTS, late_kind, late_sums, late_from_chips, "late")

    small_names = list(SMALL) + ["ffn_conv_w"]
    small_vals = [g[n].reshape(w[n].shape if n != "ffn_conv_w" else (1, 3, D_FF)) for n in small_names]
    last = LATE_WEIGHTS[-1]
    loss, out_v[last] = lax.optimization_barrier((loss, out_v[last]))
    reduced = _allreduce_small(small_vals + [loss])
    loss_all = reduced[-1].reshape(())
    conv_w_grad = lax.dynamic_slice_in_dim(reduced[-2], chip * (D_FF // 4), D_FF // 4, axis=2)
    sg, sd, sm, sv = _adamw_small(
        [w[n] for n in small_names], list(reduced[:len(SMALL)]) + [conv_w_grad],
        [m[n] for n in small_names], [v[n] for n in small_names])
    out_g.update(zip(small_names, sg))
    out_d.update(zip(small_names, sd))
    out_m.update(zip(small_names, sm))
    out_v.update(zip(small_names, sv))

    early_sums, early_from_chips = _split_wait("reduce_early_wait", e_send, e_recv, e_srcs, e_lands, reduced[0],
                                               early_plan)
    finish(EARLY_WEIGHTS, early_kind, early_sums, early_from_chips, "early")

    return (loss_all, grad_x, *[out_g[n] for n in WEIGHTS], *[out_d[n] for n in WEIGHTS],
            *[out_m[n] for n in WEIGHTS], *[out_v[n] for n in WEIGHTS])
```

```python
import math

import jax
import jax.numpy as jnp
from jax import lax
from jax.experimental import pallas as pl
from jax.experimental.pallas import tpu as pltpu

F32 = jnp.float32
BF16 = jnp.bfloat16

D_MODEL = 1024
FOX_WIDTH = 512
HEAD_DIM = 64
N_FOX_HEADS = 8
S5_WIDTH = 512
S5_GROUP_CH = 16
S5_GROUPS = 32
S5_STATE = 64
S5_CH = S5_GROUPS * S5_STATE
N_X_HEADS = 4
X_HEAD_DIM = 256
N_MEM = 256
D_FF = 2816
UF_COLS = 640
EPS = 1e-6
ADAM_LR = 0.001
ADAM_B1 = 0.9
ADAM_B2 = 0.999
ADAM_EPS = 1e-08
ADAM_WD = 0.01
ADAM_STEP = 10

VMEM_LIMIT_BYTES = 56 * 1024 * 1024
MM_BLOCK_BYTES = 6 * 1024 * 1024
MM_VMEM_BYTES = 40 * 1024 * 1024
MM_TILE_MAX = 1536
MESH = pl.DeviceIdType.MESH

FIRST_WEIGHT = "w_in"
MID_WEIGHTS = ("s5_w_glu", "w_out")
EARLY_WEIGHTS = (FIRST_WEIGHT,) + MID_WEIGHTS
LATE_WEIGHTS = ("w_xq", "w_xkv", "w_xo", "w_ffn_up", "w_ffn_down")
BIG = EARLY_WEIGHTS + LATE_WEIGHTS
COL_KIND = ("w_xkv", "w_ffn_up")
SMALL = ("norm_mix", "fox_q_norm", "fox_k_norm", "fox_f_bias", "s5_a_re", "s5_a_im", "s5_log_dt",
         "s5_b_re", "s5_b_im", "s5_c_re", "s5_c_im", "s5_d", "s5_b_glu", "out_norm_fox", "out_norm_s5",
         "norm_cross", "norm_mem", "xq_norm", "xk_norm", "norm_ffn", "ffn_conv_b")
WEIGHTS = ("norm_mix", "w_in", "fox_q_norm", "fox_k_norm", "fox_f_bias", "s5_a_re", "s5_a_im", "s5_log_dt",
           "s5_b_re", "s5_b_im", "s5_c_re", "s5_c_im", "s5_d", "s5_w_glu", "s5_b_glu", "out_norm_fox",
           "out_norm_s5", "w_out", "norm_cross", "norm_mem", "w_xq", "w_xkv", "xq_norm", "xk_norm", "w_xo",
           "norm_ffn", "w_ffn_up", "ffn_conv_w", "ffn_conv_b", "w_ffn_down")


def _params(sem=None):
    return pltpu.CompilerParams(dimension_semantics=sem, vmem_limit_bytes=VMEM_LIMIT_BYTES)


def _pick(n, cands):
    for c in cands:
        if n % c == 0:
            return c
    return n


_DIMS = {"nn": (((1,), (0,)), ((), ())), "nt": (((1,), (1,)), ((), ())), "tn": (((0,), (0,)), ((), ()))}


def _mm(a, b, mode, name, out_dtype=F32, res=None):
    if mode == "nn":
        (m, k), (k2, n) = a.shape, b.shape
    elif mode == "nt":
        (m, k), (n, k2) = a.shape, b.shape
    else:
        (k, m), (k2, n) = a.shape, b.shape
    assert k == k2, (name, a.shape, b.shape)

    has_res = res is not None
    a_size, b_size = a.dtype.itemsize, b.dtype.itemsize
    o_size = jnp.dtype(out_dtype).itemsize + (res.dtype.itemsize if has_res else 0)

    def tiles(dim):
        return [c for c in range(MM_TILE_MAX, 0, -128) if dim % c == 0] or [dim]

    best = None
    for tm in tiles(m):
        for tn in tiles(n):
            a_blk, b_blk = tm * k * a_size, tn * k * b_size
            if max(a_blk, b_blk) > MM_BLOCK_BYTES or 2 * (a_blk + b_blk + tm * tn * o_size) > MM_VMEM_BYTES:
                continue
            for rows_outer in (True, False):
                moved = (m * k * a_size + (m // tm) * n * k * b_size) if rows_outer else \
                        (n * k * b_size + (n // tn) * m * k * a_size)
                key = (moved, -(tm * tn))
                if best is None or key < best[0]:
                    best = (key, tm, tn, rows_outer)
    assert best is not None, (name, a.shape, b.shape)
    _, tm, tn, rows_outer = best
    ij = (lambda g0, g1: (g0, g1)) if rows_outer else (lambda g0, g1: (g1, g0))
    if mode == "tn":
        a_spec = pl.BlockSpec((k, tm), lambda g0, g1: (0, ij(g0, g1)[0]))
    else:
        a_spec = pl.BlockSpec((tm, k), lambda g0, g1: (ij(g0, g1)[0], 0))
    if mode == "nt":
        b_spec = pl.BlockSpec((tn, k), lambda g0, g1: (ij(g0, g1)[1], 0))
    else:
        b_spec = pl.BlockSpec((k, tn), lambda g0, g1: (0, ij(g0, g1)[1]))
    o_spec = pl.BlockSpec((tm, tn), lambda g0, g1: ij(g0, g1))
    grid = (m // tm, n // tn) if rows_outer else (n // tn, m // tm)
    dims = _DIMS[mode]

    def body(*refs):
        a_ref, b_ref = refs[0], refs[1]
        o_ref = refs[-1]
        acc = lax.dot_general(a_ref[...].astype(BF16), b_ref[...].astype(BF16), dims, preferred_element_type=F32)
        if has_res:
            acc = acc + refs[2][...].astype(F32)
        o_ref[...] = acc.astype(o_ref.dtype)

    return pl.pallas_call(
        body, name=name, grid=grid,
        in_specs=[a_spec, b_spec] + ([o_spec] if has_res else []),
        out_specs=o_spec, out_shape=jax.ShapeDtypeStruct((m, n), out_dtype),
        compiler_params=_params(("parallel", "parallel")),
    )(*((a, b, res) if has_res else (a, b)))


def _row_spec(tm, bc, off, step):
    return pl.BlockSpec((tm, bc), lambda i, h: (i, off + step * h))


ROW_TILE_ELEMS = 512 * 1024


def _row_tile(t, rows):
    widest = max(bc for (_, bc, _, _) in rows)
    return _pick(t, (min(t, ROW_TILE_ELEMS // widest), 512, 256, 128, 64, 8))


def _rowwise(fn, rows, pars, outs, name, heads=1):
    t = rows[0][0].shape[0]
    tm = _row_tile(t, rows)
    nr, npar = len(rows), len(pars)

    def body(*refs):
        vals = [r[...].astype(F32) for r in refs[:nr + npar]]
        res = fn(*vals)
        if not isinstance(res, (tuple, list)):
            res = (res,)
        for o_ref, v in zip(refs[nr + npar:], res):
            o_ref[...] = v.astype(o_ref.dtype)

    in_specs = [_row_spec(tm, bc, off, st) for (_, bc, off, st) in rows]
    in_specs += [pl.BlockSpec(p.shape, lambda i, h: (0, 0)) for p in pars]
    out_specs = [_row_spec(tm, bc, 0, st) for (_, bc, st, _) in outs]
    out_shape = [jax.ShapeDtypeStruct((t, c), dt) for (c, _, _, dt) in outs]
    res = pl.pallas_call(
        body, name=name, grid=(t // tm, heads), in_specs=in_specs, out_specs=out_specs, out_shape=out_shape,
        compiler_params=_params(("parallel", "parallel")),
    )(*[r[0] for r in rows], *pars)
    return res[0] if len(res) == 1 else res


def _rowwise_vjp(fn, rows, pars, cts, name, heads=1, adds=None, row_dtypes=None):
    t = rows[0][0].shape[0]
    tm = _row_tile(t, rows)
    nr, npar, nct = len(rows), len(pars), len(cts)
    adds = adds or [None] * nr
    add_list = [a for a in adds if a is not None]
    row_dtypes = row_dtypes or [F32] * nr

    def body(*refs):
        i, h = pl.program_id(0), pl.program_id(1)
        p = 0
        row_v = [r[...].astype(F32) for r in refs[p:p + nr]]; p += nr
        par_v = [r[...].astype(F32) for r in refs[p:p + npar]]; p += npar
        ct_v = [r[...].astype(F32) for r in refs[p:p + nct]]; p += nct
        add_refs = refs[p:p + len(add_list)]; p += len(add_list)
        drow_refs = refs[p:p + nr]; p += nr
        dpar_refs = refs[p:p + npar]

        def wrapped(*a):
            r = fn(*a)
            return tuple(r) if isinstance(r, (tuple, list)) else (r,)

        _, pull = jax.vjp(wrapped, *row_v, *par_v)
        grads = pull(tuple(ct_v))
        ai = 0
        for k in range(nr):
            g = grads[k]
            if adds[k] is not None:
                g = g + add_refs[ai][...].astype(F32)
                ai += 1
            drow_refs[k][...] = g.astype(drow_refs[k].dtype)

        @pl.when((i == 0) & (h == 0))
        def _():
            for r in dpar_refs:
                r[...] = jnp.zeros(r.shape, r.dtype)

        for k in range(npar):
            dpar_refs[k][...] += grads[nr + k]

    in_specs = [_row_spec(tm, bc, off, st) for (_, bc, off, st) in rows]
    in_specs += [pl.BlockSpec(q.shape, lambda i, h: (0, 0)) for q in pars]
    in_specs += [_row_spec(tm, bc, off, st) for (_, bc, off, st) in cts]
    in_specs += [_row_spec(tm, bc, off, st) for (_, bc, off, st) in add_list]
    out_specs = [_row_spec(tm, bc, 0, st) for (_, bc, _, st) in rows]
    out_specs += [pl.BlockSpec(q.shape, lambda i, h: (0, 0)) for q in pars]
    out_shape = [jax.ShapeDtypeStruct((t, bc * (heads if st else 1)), dt) for (_, bc, _, st), dt in zip(rows, row_dtypes)]
    out_shape += [jax.ShapeDtypeStruct(q.shape, F32) for q in pars]
    res = pl.pallas_call(
        body, name=name, grid=(t // tm, heads), in_specs=in_specs, out_specs=out_specs, out_shape=out_shape,
        compiler_params=_params(("arbitrary", "arbitrary")),
    )(*[r[0] for r in rows], *pars, *[c[0] for c in cts], *[a[0] for a in add_list])
    return list(res[:nr]), list(res[nr:])


def _rms(x, g):
    return x * lax.rsqrt(jnp.mean(x * x, axis=-1, keepdims=True) + EPS) * g


def _rms_pair(x, g):
    left = lax.broadcasted_iota(jnp.int32, x.shape, 1) < HEAD_DIM
    x2 = x * x
    ms_a = jnp.sum(jnp.where(left, x2, 0.0), axis=-1, keepdims=True) * (1.0 / HEAD_DIM)
    ms_b = jnp.sum(jnp.where(left, 0.0, x2), axis=-1, keepdims=True) * (1.0 / HEAD_DIM)
    return x * lax.rsqrt(jnp.where(left, ms_a, ms_b) + EPS) * g


def _gelu(x):
    return 0.5 * x * (1.0 + jnp.tanh(math.sqrt(2.0 / math.pi) * (x + 0.044715 * (x * x * x))))


def _s5_act(ys, u, d):
    return _gelu(ys + d * u)


def _s5_gate(yg, z, b, g):
    return _rms(yg * jax.nn.sigmoid(z + b), g)


def _lane_cumsum(x, reverse):
    n = x.shape[-1]
    lane = lax.broadcasted_iota(jnp.int32, x.shape, 1)
    k = 1
    while k < n:
        if reverse:
            x = x + jnp.where(lane < n - k, pltpu.roll(x, n - k, 1), 0.0)
        else:
            x = x + jnp.where(lane >= k, pltpu.roll(x, k, 1), 0.0)
        k *= 2
    return x


def _log_sigmoid(z):
    return jnp.minimum(z, 0.0) - jnp.log(1.0 + jnp.exp(-jnp.abs(z)))


def _forget_fwd(f, bias):
    def body(f_ref, b_ref, c_ref):
        c_ref[...] = _lane_cumsum(_log_sigmoid(f_ref[...] + b_ref[...]), False)

    return pl.pallas_call(body, name="forget_fwd", out_shape=jax.ShapeDtypeStruct(f.shape, F32),
                          compiler_params=_params())(f, bias)


def _forget_bwd(f, bias, dc):
    def body(f_ref, b_ref, dc_ref, df_ref, db_ref):
        dlog = _lane_cumsum(dc_ref[...], True)
        df = dlog * jax.nn.sigmoid(-(f_ref[...] + b_ref[...]))
        df_ref[...] = df
        db_ref[...] = jnp.sum(df, axis=1, keepdims=True)

    return pl.pallas_call(body, name="forget_bwd",
                          out_shape=(jax.ShapeDtypeStruct(f.shape, F32), jax.ShapeDtypeStruct(bias.shape, F32)),
                          compiler_params=_params())(f, bias, dc)


FOX_BLOCK = 1024
FOX_KEYS = 1024
FOX_BWD_BLOCK = 512
_NT = _DIMS["nt"]
_TN = _DIMS["tn"]


N_PAIRS = N_FOX_HEADS // 2
V_BLOCK0 = 2 * N_PAIRS


def _left_lanes(shape):
    return lax.broadcasted_iota(jnp.int32, shape, 1) < HEAD_DIM


def _top_rows(shape):
    return lax.broadcasted_iota(jnp.int32, shape, 0) < HEAD_DIM


def _wide(c_tile, n):
    return c_tile if n == 128 else jnp.concatenate([c_tile] * (n // 128), axis=1)


def _fox_fwd(qn, kn, qkv, c_wide, seqs):
    t = qn.shape[0]
    l = t // seqs
    tb = min(FOX_BLOCK, l)
    tk = min(FOX_KEYS, tb)
    ratio = tb // tk
    nb = l // tb
    scale = HEAD_DIM ** -0.5

    def body(q_ref, k_ref, v_ref, ca_ref, cb_ref, o_ref, lse_ref, vt_ref):
        i = pl.program_id(2)
        top = _top_rows((128, tb))

        @pl.when(i == 0)
        def _():
            vt_ref[...] = v_ref[...].T.astype(BF16)

        qt = (q_ref[...].astype(F32) * scale).T.astype(BF16)
        zero = jnp.zeros_like(qt)
        qts = (jnp.where(top, qt, zero), jnp.where(top, zero, qt))
        top_k = _top_rows((128, tk))
        zero_k = jnp.zeros((128, tk), BF16)
        key_pos = lax.broadcasted_iota(jnp.int32, (tk, tb), 0)
        query_pos = lax.broadcasted_iota(jnp.int32, (tk, tb), 1)
        c_refs = (ca_ref, cb_ref)

        def scores(j):
            off = pl.multiple_of(j * tk, tk)
            k2 = k_ref[pl.ds(off, tk), :]
            return tuple(jnp.dot(k2, qts[h], preferred_element_type=F32) - _wide(c_refs[h][pl.ds(off, tk), :], tb)
                         for h in (0, 1))

        def values_times(ps, j):
            vt = vt_ref[:, pl.ds(pl.multiple_of(j * tk, tk), tk)]
            return (jnp.dot(jnp.where(top_k, vt, zero_k), ps[0], preferred_element_type=F32)
                    + jnp.dot(jnp.where(top_k, zero_k, vt), ps[1], preferred_element_type=F32))

        def softmax_step(sts, stats, first_key):
            ps, new, alphas = [], [], []
            for st, (m, s_sum) in zip(sts, stats):
                if first_key is not None:
                    st = jnp.where(key_pos + first_key <= query_pos, st, -jnp.inf)
                m_new = jnp.maximum(m, jnp.max(st, axis=0, keepdims=True))
                alpha = jnp.exp(m - m_new)
                p = jnp.exp(st - m_new)
                new.append((m_new, alpha * s_sum + jnp.sum(p, axis=0, keepdims=True)))
                alphas.append(alpha)
                ps.append(p.astype(BF16))
            return tuple(ps), tuple(new), jnp.where(top, alphas[0], alphas[1])

        def tile(j, carry, first_key):
            stats, acc = carry
            ps, stats, alpha = softmax_step(scores(j), stats, first_key)
            return stats, alpha * acc + values_times(ps, j)

        stat = (jnp.full((1, tb), -jnp.inf, F32), jnp.zeros((1, tb), F32))
        below = i * ratio
        carry = lax.fori_loop(0, below, lambda j, c: tile(j, c, None), ((stat, stat), jnp.zeros((128, tb), F32)))
        for r in range(ratio):
            carry = tile(below + r, carry, r * tk)
        ((ma, sa), (mb, sb)), acc = carry
        o_ref[...] = (acc / jnp.where(top, sa, sb)).T
        lse_ref[0:1, :] = ma + jnp.log(sa)
        lse_ref[1:2, :] = mb + jnp.log(sb)

    qblk = pl.BlockSpec((tb, 128), lambda b, hp, i: (b * nb + i, hp))
    return pl.pallas_call(
        body, name="fox_fwd", grid=(seqs, N_PAIRS, nb),
        in_specs=[qblk, pl.BlockSpec((l, 128), lambda b, hp, i: (b, hp)),
                  pl.BlockSpec((l, 128), lambda b, hp, i: (b, V_BLOCK0 + hp)),
                  pl.BlockSpec((None, l, 128), lambda b, hp, i: (b * N_FOX_HEADS + 2 * hp, 0, 0)),
                  pl.BlockSpec((None, l, 128), lambda b, hp, i: (b * N_FOX_HEADS + 2 * hp + 1, 0, 0))],
        out_specs=[qblk, pl.BlockSpec((None, 2, tb), lambda b, hp, i: (b * N_PAIRS + hp, 0, i))],
        out_shape=[jax.ShapeDtypeStruct((t, FOX_WIDTH), F32), jax.ShapeDtypeStruct((seqs * N_PAIRS, 2, l), F32)],
        scratch_shapes=[pltpu.VMEM((128, l), BF16)],
        compiler_params=_params(("parallel", "parallel", "arbitrary")),
    )(qn, kn, qkv, c_wide, c_wide)


def _fox_bwd(qn, kn, qkv, c_wide, o, do, lse, seqs):
    t = qn.shape[0]
    l = t // seqs
    tb = min(FOX_BWD_BLOCK, l)
    nb = l // tb
    scale = HEAD_DIM ** -0.5
    one_at = (HEAD_DIM, 0)

    def body(q_ref, k_ref, v_ref, ca_ref, cb_ref, o_ref, do_ref, lse_ref, dq_ref, dk_ref, dv_ref, dc_ref, dcq_ref,
             qt_ref, kt_ref, dot_ref, delta_ref, dqa_ref, dqb_ref):
        top_l = _top_rows((128, l))
        top = _top_rows((128, tb))
        left = _left_lanes((tb, 128))
        row_id = lax.broadcasted_iota(jnp.int32, (128, tb), 0)
        lane_id = lax.broadcasted_iota(jnp.int32, (tb, 128), 1)
        zero_t = jnp.zeros((128, tb), BF16)
        zero_l = jnp.zeros((tb, 128), BF16)
        rows = lambda a: (jnp.where(top, a, zero_t), jnp.where(top, zero_t, a))
        lanes = lambda a: (jnp.where(left, a, zero_l), jnp.where(left, zero_l, a))
        with_one_row = lambda pair: tuple(jnp.where(row_id == one_at[h], 1.0, pair[h]).astype(BF16) for h in (0, 1))
        with_one_lane = lambda pair: tuple(jnp.where(lane_id == one_at[h], 1.0, pair[h]).astype(BF16) for h in (0, 1))
        causal = lax.broadcasted_iota(jnp.int32, (tb, tb), 0) <= lax.broadcasted_iota(jnp.int32, (tb, tb), 1)
        c_refs = (ca_ref, cb_ref)
        dq_refs = (dqa_ref, dqb_ref)

        qt_ref[...] = (q_ref[...].astype(F32) * scale).T.astype(BF16)
        kt_ref[...] = k_ref[...].astype(F32).T.astype(BF16)
        do_t = do_ref[...].T
        dot_ref[...] = do_t.astype(BF16)
        prod_t = do_t * o_ref[...].T
        delta_ref[0:1, :] = jnp.sum(jnp.where(top_l, prod_t, 0.0), axis=0, keepdims=True)
        delta_ref[1:2, :] = jnp.sum(jnp.where(top_l, 0.0, prod_t), axis=0, keepdims=True)
        dqa_ref[...] = jnp.zeros(dqa_ref.shape, F32)
        dqb_ref[...] = jnp.zeros(dqb_ref.shape, F32)

        def kv_block(j, _):
            koff = pl.multiple_of(j * tb, tb)
            k2 = k_ref[pl.ds(koff, tb), :]
            v2 = v_ref[pl.ds(koff, tb), :].astype(BF16)
            kts = with_one_row(rows(kt_ref[:, pl.ds(koff, tb)]))
            cw = tuple(_wide(c_refs[h][pl.ds(koff, tb), :], tb) for h in (0, 1))

            def q_block(i, carry, masked):
                dks, dv = list(carry[:2]), carry[2]
                qoff = pl.multiple_of(i * tb, tb)
                qs = lanes((q_ref[pl.ds(qoff, tb), :].astype(F32) * scale).astype(BF16))
                qs_one = with_one_lane(qs)
                dos = lanes(do_ref[pl.ds(qoff, tb), :].astype(BF16))
                qts = rows(qt_ref[:, pl.ds(qoff, tb)])
                dots = rows(dot_ref[:, pl.ds(qoff, tb)])
                for h in (0, 1):
                    st = jnp.dot(k2, qts[h], preferred_element_type=F32) - cw[h]
                    p = jnp.exp(st - lse_ref[h:h + 1, pl.ds(qoff, tb)])
                    if masked:
                        p = jnp.where(causal, p, 0.0)
                    dp = jnp.dot(v2, dots[h], preferred_element_type=F32)
                    dsb = (p * (dp - delta_ref[h:h + 1, pl.ds(qoff, tb)])).astype(BF16)
                    dv = dv + jnp.dot(p.astype(BF16), dos[h], preferred_element_type=F32)
                    dks[h] = dks[h] + jnp.dot(dsb, qs_one[h], preferred_element_type=F32)
                    dq_refs[h][:, pl.ds(qoff, tb)] += jnp.dot(kts[h], dsb, preferred_element_type=F32)
                return dks[0], dks[1], dv

            z = jnp.zeros((tb, 128), F32)
            carry = q_block(j, (z, z, z), True)
            rest = nb - 1 - j
            carry = lax.fori_loop(
                0, rest // 2, lambda n, c: q_block(j + 2 + 2 * n, q_block(j + 1 + 2 * n, c, False), False), carry)
            dka, dkb, dv = lax.cond(rest % 2 == 1, lambda c: q_block(nb - 1, c, False), lambda c: c, carry)
            dk_ref[pl.ds(koff, tb), :] = jnp.where(left, dka, dkb)
            dv_ref[pl.ds(koff, tb), :] = dv
            dc_ref[0:1, pl.ds(koff, tb)] = -dka.T[one_at[0]:one_at[0] + 1, :]
            dc_ref[1:2, pl.ds(koff, tb)] = -dkb.T[one_at[1]:one_at[1] + 1, :]
            return 0

        lax.fori_loop(0, nb, kv_block, 0)
        dq_ref[...] = (jnp.where(top_l, dqa_ref[...], dqb_ref[...]) * scale).T
        dcq_ref[0:1, :] = dqa_ref[one_at[0]:one_at[0] + 1, :]
        dcq_ref[1:2, :] = dqb_ref[one_at[1]:one_at[1] + 1, :]

    blk = pl.BlockSpec((l, 128), lambda b, hp: (b, hp))
    cspec = lambda k: pl.BlockSpec((None, l, 128), lambda b, hp: (b * N_FOX_HEADS + 2 * hp + k, 0, 0))
    rows2 = pl.BlockSpec((None, 2, l), lambda b, hp: (b * N_PAIRS + hp, 0, 0))
    wide = jax.ShapeDtypeStruct((t, FOX_WIDTH), F32)
    pair_rows = jax.ShapeDtypeStruct((seqs * N_PAIRS, 2, l), F32)
    return pl.pallas_call(
        body, name="fox_bwd", grid=(seqs, N_PAIRS),
        in_specs=[blk, blk, pl.BlockSpec((l, 128), lambda b, hp: (b, V_BLOCK0 + hp)), cspec(0), cspec(1), blk, blk, rows2],
        out_specs=[blk, blk, blk, rows2, rows2],
        out_shape=[wide, wide, wide, pair_rows, pair_rows],
        scratch_shapes=[pltpu.VMEM((128, l), BF16), pltpu.VMEM((128, l), BF16), pltpu.VMEM((128, l), BF16),
                        pltpu.VMEM((2, l), F32), pltpu.VMEM((128, l), F32), pltpu.VMEM((128, l), F32)],
        compiler_params=_params(("parallel", "parallel")),
    )(qn, kn, qkv, c_wide, c_wide, o, do, lse)


SCAN_ROWS = 512
SCAN_COLS = 1024


S5_IN = 128
S5_ST = 512
SCAN_CHUNKS = SCAN_COLS // S5_ST
SCAN_SEGS = 8
LANES = 128


def _cmul(ar, ai, br, bi):
    return ar * br - ai * bi, ar * bi + ai * br


def _powers_into(pw_r, pw_i, a_r, a_i, seg):
    pw_r[0:1, :] = a_r
    pw_i[0:1, :] = a_i
    for k in range(1, seg):
        pr, pi = _cmul(pw_r[k - 1:k, :], pw_i[k - 1:k, :], a_r, a_i)
        pw_r[k:k + 1, :] = pr
        pw_i[k:k + 1, :] = pi


def _interleave(dst, src, seg):
    for h in range(src.shape[0]):
        for j in range(seg):
            dst[h, j * SCAN_SEGS:(j + 1) * SCAN_SEGS, :] = src[h, pl.ds(j, SCAN_SEGS, stride=seg), :]


def _deinterleave(dst, src, seg):
    for h in range(src.shape[0]):
        for j in range(seg):
            dst[h, pl.ds(j, SCAN_SEGS, stride=seg), :] = src[h, j * SCAN_SEGS:(j + 1) * SCAN_SEGS, :]


def _interleaved(ref, tmp_a, tmp_b, seg):
    n = ref.shape[1] // LANES
    for h in range(n):
        tmp_a[h] = ref[:, h * LANES:(h + 1) * LANES].astype(F32)
    _interleave(tmp_b, tmp_a, seg)
    return jnp.concatenate([tmp_b[h] for h in range(n)], axis=1)


def _store_deinterleaved(ref, val, tmp_a, tmp_b, seg):
    n = ref.shape[1] // LANES
    for h in range(n):
        tmp_a[h] = val[:, h * LANES:(h + 1) * LANES]
    _deinterleave(tmp_b, tmp_a, seg)
    for h in range(n):
        ref[:, h * LANES:(h + 1) * LANES] = tmp_b[h]


def _segment_scan(b_r, b_i, x_r, x_i, pw_r, pw_i, car_r, car_i, seg, sign, reverse, visit=None):
    nc = b_r.shape[0]
    sub = lax.broadcasted_iota(jnp.int32, (SCAN_SEGS, LANES), 0)
    lanes = lambda c: slice(c * LANES, (c + 1) * LANES)
    rows = lambda j: pl.ds(pl.multiple_of(((seg - 1 - j) if reverse else j) * SCAN_SEGS, SCAN_SEGS), SCAN_SEGS)
    a1 = [(pw_r[0:1, lanes(c)], sign * pw_i[0:1, lanes(c)]) for c in range(nc)]

    def local(j, xs):
        out = []
        for c in range(nc):
            xr, xi = xs[2 * c], xs[2 * c + 1]
            nr = a1[c][0] * xr - a1[c][1] * xi + b_r[c, rows(j), :]
            ni = a1[c][0] * xi + a1[c][1] * xr + b_i[c, rows(j), :]
            x_r[c, rows(j), :] = nr
            x_i[c, rows(j), :] = ni
            out += [nr, ni]
        return tuple(out)

    zero = jnp.zeros((SCAN_SEGS, LANES), F32)
    ends = lax.fori_loop(0, seg, local, (zero,) * (2 * nc))

    if reverse:
        first = sub == SCAN_SEGS - 1
        neighbour = lambda v: pltpu.roll(v, SCAN_SEGS - 1, 0)
        shift = lambda v, d: jnp.where(sub < SCAN_SEGS - d, pltpu.roll(v, SCAN_SEGS - d, 0), 0.0)
    else:
        first = sub == 0
        neighbour = lambda v: pltpu.roll(v, 1, 0)
        shift = lambda v, d: jnp.where(sub >= d, pltpu.roll(v, d, 0), 0.0)
    last = 0 if reverse else SCAN_SEGS - 1
    entries = []
    for c in range(nc):
        er, ei = ends[2 * c], ends[2 * c + 1]
        pr, pi = pw_r[seg - 1:seg, lanes(c)], sign * pw_i[seg - 1:seg, lanes(c)]
        yr = jnp.where(first, car_r[:, lanes(c)], neighbour(er))
        yi = jnp.where(first, car_i[:, lanes(c)], neighbour(ei))
        qr, qi = pr, pi
        for d in (1, 2, 4):
            mr, mi = _cmul(qr, qi, shift(yr, d), shift(yi, d))
            yr, yi = yr + mr, yi + mi
            qr, qi = _cmul(qr, qi, qr, qi)
        lr, li = _cmul(pr, pi, yr, yi)
        car_r[:, lanes(c)] = (er + lr)[last:last + 1, :]
        car_i[:, lanes(c)] = (ei + li)[last:last + 1, :]
        entries += [yr, yi]

    def correct(j, prev):
        out = []
        row_r, row_i = pw_r[pl.ds(j, 1), :], sign * pw_i[pl.ds(j, 1), :]
        for c in range(nc):
            mr, mi = _cmul(row_r[:, lanes(c)], row_i[:, lanes(c)], entries[2 * c], entries[2 * c + 1])
            nr = x_r[c, rows(j), :] + mr
            ni = x_i[c, rows(j), :] + mi
            x_r[c, rows(j), :] = nr
            x_i[c, rows(j), :] = ni
            if visit is not None:
                visit(c, rows(j), prev[2 * c], prev[2 * c + 1])
            out += [nr, ni]
        return tuple(out)

    lax.fori_loop(0, seg, correct, tuple(entries))


def _s5_fwd(uf, bbr, bbi, cr, ci, ar, ai, seqs):
    t = uf.shape[0]
    l = t // seqs
    tl = min(SCAN_ROWS, l)
    nl = l // tl
    seg = tl // SCAN_SEGS
    cb, nq = SCAN_COLS, SCAN_CHUNKS
    nc = cb // LANES
    per = S5_ST // LANES

    def body(u_ref, bbr_ref, bbi_ref, cr_ref, ci_ref, ar_ref, ai_ref, x_r, x_i, ys_ref,
             car_r, car_i, pw_r, pw_i, b_r, b_i, tmp_a, tmp_b):
        @pl.when(pl.program_id(2) == 0)
        def _():
            car_r[...] = jnp.zeros(car_r.shape, F32)
            car_i[...] = jnp.zeros(car_i.shape, F32)
            _powers_into(pw_r, pw_i, ar_ref[...], ai_ref[...], seg)

        u = _interleaved(u_ref, tmp_a, tmp_b, seg).astype(BF16)
        for q in range(nq):
            uq = u[:, q * S5_IN:(q + 1) * S5_IN]
            br = jnp.dot(uq, bbr_ref[q], preferred_element_type=F32)
            bi = jnp.dot(uq, bbi_ref[q], preferred_element_type=F32)
            for s in range(per):
                b_r[q * per + s] = br[:, s * LANES:(s + 1) * LANES]
                b_i[q * per + s] = bi[:, s * LANES:(s + 1) * LANES]
        _segment_scan(b_r, b_i, x_r, x_i, pw_r, pw_i, car_r, car_i, seg, 1.0, False)
        wide = lambda buf, q: jnp.concatenate([buf[q * per + s] for s in range(per)], axis=1).astype(BF16)
        ys = [jnp.dot(wide(x_r, q), cr_ref[q], preferred_element_type=F32)
              + jnp.dot(wide(x_i, q), ci_ref[q], preferred_element_type=F32) for q in range(nq)]
        _store_deinterleaved(ys_ref, jnp.concatenate(ys, axis=1), tmp_a, tmp_b, seg)

    rows = lambda w: pl.BlockSpec((tl, w), lambda s, j, r: (s * nl + r, j))
    state = pl.BlockSpec((nc, tl, LANES), lambda s, j, r: (j, s * nl + r, 0))
    chunk = lambda a: pl.BlockSpec((nq,) + a.shape[1:], lambda s, j, r: (j, 0, 0))
    par = pl.BlockSpec((1, cb), lambda s, j, r: (0, j))
    return pl.pallas_call(
        body, name="s5_fwd", grid=(seqs, S5_CH // cb, nl),
        in_specs=[rows(nq * S5_IN), chunk(bbr), chunk(bbi), chunk(cr), chunk(ci), par, par],
        out_specs=[state, state, rows(nq * S5_IN)],
        out_shape=[jax.ShapeDtypeStruct((S5_CH // LANES, t, LANES), F32)] * 2
        + [jax.ShapeDtypeStruct((t, S5_WIDTH), F32)],
        scratch_shapes=[pltpu.VMEM((1, cb), F32), pltpu.VMEM((1, cb), F32), pltpu.VMEM((seg, cb), F32),
                        pltpu.VMEM((seg, cb), F32)] + [pltpu.VMEM((nc, tl, LANES), F32)] * 2
        + [pltpu.VMEM((nq * S5_IN // LANES, tl, LANES), F32)] * 2,
        compiler_params=_params(("parallel", "parallel", "arbitrary")),
    )(uf, bbr, bbi, cr, ci, ar, ai)


def _s5_bwd(dys, uf, xr, xi, bbr, bbi, cr, ci, ar, ai, seqs):
    t = dys.shape[0]
    l = t // seqs
    tl = min(SCAN_ROWS, l)
    nl = l // tl
    seg = tl // SCAN_SEGS
    cb, nq = SCAN_COLS, SCAN_CHUNKS
    nc = cb // LANES
    per = S5_ST // LANES

    def body(dy_ref, u_ref, x_r, x_i, bbr_ref, bbi_ref, cr_ref, ci_ref, ar_ref, ai_ref,
             du_ref, dbbr_ref, dbbi_ref, dcr_ref, dci_ref, dar_ref, dai_ref,
             car_r, car_i, pw_r, pw_i, g_r, g_i, lam_r, lam_i, acc_r, acc_i, tmp_a, tmp_b):
        @pl.when(pl.program_id(2) == 0)
        def _():
            car_r[...] = jnp.zeros(car_r.shape, F32)
            car_i[...] = jnp.zeros(car_i.shape, F32)
            _powers_into(pw_r, pw_i, ar_ref[...], ai_ref[...], seg)
            for acc_ref in (dbbr_ref, dbbi_ref, dcr_ref, dci_ref, dar_ref, dai_ref):
                acc_ref[...] = jnp.zeros(acc_ref.shape, F32)

        dy = _interleaved(dy_ref, tmp_a, tmp_b, seg).astype(BF16)
        for q in range(nq):
            dyq = dy[:, q * S5_IN:(q + 1) * S5_IN]
            gr = lax.dot_general(dyq, cr_ref[q], _NT, preferred_element_type=F32)
            gi = lax.dot_general(dyq, ci_ref[q], _NT, preferred_element_type=F32)
            for s in range(per):
                g_r[q * per + s] = gr[:, s * LANES:(s + 1) * LANES]
                g_i[q * per + s] = gi[:, s * LANES:(s + 1) * LANES]
        acc_r[...] = jnp.zeros(acc_r.shape, F32)
        acc_i[...] = jnp.zeros(acc_i.shape, F32)

        def visit(c, rws, lr, li):
            xr_t, xi_t = x_r[c, rws, :], x_i[c, rws, :]
            acc_r[c] += lr * xr_t + li * xi_t
            acc_i[c] += li * xr_t - lr * xi_t

        _segment_scan(g_r, g_i, lam_r, lam_i, pw_r, pw_i, car_r, car_i, seg, -1.0, True, visit)
        for c in range(nc):
            dar_ref[:, c * LANES:(c + 1) * LANES] += jnp.sum(acc_r[c], axis=0, keepdims=True)
            dai_ref[:, c * LANES:(c + 1) * LANES] += jnp.sum(acc_i[c], axis=0, keepdims=True)
        u = _interleaved(u_ref, tmp_a, tmp_b, seg).astype(BF16)
        wide = lambda buf, q: jnp.concatenate([buf[q * per + s] for s in range(per)], axis=1).astype(BF16)
        du = []
        for q in range(nq):
            io = slice(q * S5_IN, (q + 1) * S5_IN)
            lq_r, lq_i = wide(lam_r, q), wide(lam_i, q)
            du.append(lax.dot_general(lq_r, bbr_ref[q], _NT, preferred_element_type=F32)
                      + lax.dot_general(lq_i, bbi_ref[q], _NT, preferred_element_type=F32))
            dbbr_ref[q] += lax.dot_general(u[:, io], lq_r, _TN, preferred_element_type=F32)
            dbbi_ref[q] += lax.dot_general(u[:, io], lq_i, _TN, preferred_element_type=F32)
            dcr_ref[q] += lax.dot_general(wide(x_r, q), dy[:, io], _TN, preferred_element_type=F32)
            dci_ref[q] += lax.dot_general(wide(x_i, q), dy[:, io], _TN, preferred_element_type=F32)
        _store_deinterleaved(du_ref, jnp.concatenate(du, axis=1), tmp_a, tmp_b, seg)

    rows = lambda w: pl.BlockSpec((tl, w), lambda s, j, r: (s * nl + nl - 1 - r, j))
    state = pl.BlockSpec((nc, tl, LANES), lambda s, j, r: (j, s * nl + nl - 1 - r, 0))
    chunk = lambda a: pl.BlockSpec((nq,) + a.shape[1:], lambda s, j, r: (j, 0, 0))
    acc = lambda a: pl.BlockSpec((None, nq) + a.shape[1:], lambda s, j, r: (s, j, 0, 0))
    par = pl.BlockSpec((1, cb), lambda s, j, r: (0, j))
    par_acc = pl.BlockSpec((None, 1, cb), lambda s, j, r: (s, 0, j))
    per_seq = lambda a: jax.ShapeDtypeStruct((seqs,) + a.shape, F32)
    return pl.pallas_call(
        body, name="s5_bwd", grid=(seqs, S5_CH // cb, nl),
        in_specs=[rows(nq * S5_IN), rows(nq * S5_IN), state, state, chunk(bbr), chunk(bbi), chunk(cr), chunk(ci),
                  par, par],
        out_specs=[rows(nq * S5_IN), acc(bbr), acc(bbi), acc(cr), acc(ci), par_acc, par_acc],
        out_shape=[jax.ShapeDtypeStruct((t, S5_WIDTH), F32), per_seq(bbr), per_seq(bbi), per_seq(cr), per_seq(ci),
                   jax.ShapeDtypeStruct((seqs, 1, S5_CH), F32), jax.ShapeDtypeStruct((seqs, 1, S5_CH), F32)],
        scratch_shapes=[pltpu.VMEM((1, cb), F32), pltpu.VMEM((1, cb), F32), pltpu.VMEM((seg, cb), F32),
                        pltpu.VMEM((seg, cb), F32)] + [pltpu.VMEM((nc, tl, LANES), F32)] * 4
        + [pltpu.VMEM((nc, SCAN_SEGS, LANES), F32)] * 2 + [pltpu.VMEM((nq * S5_IN // LANES, tl, LANES), F32)] * 2,
        compiler_params=_params(("parallel", "parallel", "arbitrary")),
    )(dys, uf, xr, xi, bbr, bbi, cr, ci, ar, ai)


XATT_BLOCK = 2048


def _xatt_probs(qv, kv):
    s = lax.dot_general(qv, kv, _NT, preferred_element_type=F32) * (X_HEAD_DIM ** -0.5)
    e = jnp.exp(s - jnp.max(s, axis=-1, keepdims=True))
    return e / jnp.sum(e, axis=-1, keepdims=True)


def _xatt_fwd(q, k, kv, seqs):
    t = q.shape[0]
    tq = min(XATT_BLOCK, t // seqs)
    nq = t // seqs // tq

    def body(q_ref, k_ref, v_ref, o_ref):
        p = _xatt_probs(q_ref[...], k_ref[...])
        o_ref[...] = jnp.dot(p.astype(BF16), v_ref[...].astype(BF16), preferred_element_type=F32).astype(o_ref.dtype)

    qs = pl.BlockSpec((tq, X_HEAD_DIM), lambda b, h, i: (b * nq + i, h))
    return pl.pallas_call(
        body, name="xatt_fwd", grid=(seqs, N_X_HEADS, nq),
        in_specs=[qs, pl.BlockSpec((N_MEM, X_HEAD_DIM), lambda b, h, i: (b, h)),
                  pl.BlockSpec((N_MEM, X_HEAD_DIM), lambda b, h, i: (b, N_X_HEADS + h))],
        out_specs=qs, out_shape=jax.ShapeDtypeStruct(q.shape, BF16),
        compiler_params=_params(("parallel", "parallel", "parallel")),
    )(q, k, kv)


def _xatt_bwd(q, k, kv, do, seqs):
    t = q.shape[0]
    tq = min(XATT_BLOCK, t // seqs)
    nq = t // seqs // tq
    scale = X_HEAD_DIM ** -0.5

    def body(q_ref, k_ref, v_ref, do_ref, dq_ref, dk_ref, dv_ref):
        @pl.when(pl.program_id(2) == 0)
        def _():
            dk_ref[...] = jnp.zeros(dk_ref.shape, F32)
            dv_ref[...] = jnp.zeros(dv_ref.shape, F32)

        qv, kk = q_ref[...], k_ref[...]
        p = _xatt_probs(qv, kk)
        dob = do_ref[...].astype(BF16)
        dp = lax.dot_general(dob, v_ref[...].astype(BF16), _NT, preferred_element_type=F32)
        ds = p * (dp - jnp.sum(dp * p, axis=-1, keepdims=True))
        dsb = ds.astype(BF16)
        dq_ref[...] = jnp.dot(dsb, kk, preferred_element_type=F32) * scale
        dk_ref[...] += lax.dot_general(dsb, qv, _TN, preferred_element_type=F32) * scale
        dv_ref[...] += lax.dot_general(p.astype(BF16), dob, _TN, preferred_element_type=F32)

    qs = pl.BlockSpec((tq, X_HEAD_DIM), lambda b, h, i: (b * nq + i, h))
    ks = pl.BlockSpec((N_MEM, X_HEAD_DIM), lambda b, h, i: (b, h))
    return pl.pallas_call(
        body, name="xatt_bwd", grid=(seqs, N_X_HEADS, nq),
        in_specs=[qs, ks, pl.BlockSpec((N_MEM, X_HEAD_DIM), lambda b, h, i: (b, N_X_HEADS + h)), qs],
        out_specs=[qs, ks, ks],
        out_shape=[jax.ShapeDtypeStruct(q.shape, F32), jax.ShapeDtypeStruct(k.shape, F32),
                   jax.ShapeDtypeStruct(k.shape, F32)],
        compiler_params=_params(("parallel", "parallel", "arbitrary")),
    )(q, k, kv, do)


CONV_COLS = 256


def _shift_down(x, k, row):
    return jnp.where(row >= k, pltpu.roll(x, k, 0), 0.0)


def _shift_up(x, k, row):
    n = x.shape[0]
    return jnp.where(row < n - k, pltpu.roll(x, n - k, 0), 0.0)


def _down_from(x, prev, k, row):
    return jnp.where(row >= k, pltpu.roll(x, k, 0), pltpu.roll(prev, k, 0))


GATE_ROWS = 512


def _ffn_up_gate(hn, w_up, w, b, seqs):
    t = hn.shape[0]
    l = t // seqs
    nc = D_FF // CONV_COLS

    rc = min(GATE_ROWS, l)

    def body(a_ref, wg_ref, wu_ref, w_ref, b_ref, g_ref, u_ref, o_ref):
        wv, bias = w_ref[...], b_ref[...]
        row = lax.broadcasted_iota(jnp.int32, (rc, CONV_COLS), 0)
        prev = jnp.zeros((rc, CONV_COLS), F32)
        for k in range(l // rc):
            rows = slice(k * rc, (k + 1) * rc)
            a = a_ref[rows, :]
            gb = jnp.dot(a, wg_ref[...], preferred_element_type=F32).astype(BF16)
            ub = jnp.dot(a, wu_ref[...], preferred_element_type=F32).astype(BF16)
            g_ref[rows, :] = gb
            u_ref[rows, :] = ub
            g = gb.astype(F32)
            pre = bias + wv[0:1, :] * _down_from(g, prev, 2, row) + wv[1:2, :] * _down_from(g, prev, 1, row) \
                + wv[2:3, :] * g
            o_ref[rows, :] = (pre * jax.nn.sigmoid(pre) * ub.astype(F32)).astype(o_ref.dtype)
            prev = g

    cols = pl.BlockSpec((l, CONV_COLS), lambda s, j: (s, j))
    half = jax.ShapeDtypeStruct((t, D_FF), BF16)
    return pl.pallas_call(
        body, name="ffn_up_gate", grid=(seqs, nc),
        in_specs=[pl.BlockSpec((l, hn.shape[1]), lambda s, j: (s, 0)),
                  pl.BlockSpec((hn.shape[1], CONV_COLS), lambda s, j: (0, j)),
                  pl.BlockSpec((hn.shape[1], CONV_COLS), lambda s, j: (0, nc + j)),
                  pl.BlockSpec((3, CONV_COLS), lambda s, j: (0, j)), pl.BlockSpec((1, CONV_COLS), lambda s, j: (0, j))],
        out_specs=[cols, cols, cols], out_shape=[half, half, half],
        compiler_params=_params(("parallel", "parallel")),
    )(hn, w_up, w_up, w, b)


def _ffn_down_dx_gate(dh, w_down, gate, up, w, b, seqs):
    t = dh.shape[0]
    l = t // seqs
    nc = D_FF // CONV_COLS
    steps = nc * seqs

    def body(dh_ref, wd_ref, g_ref, u_ref, w_ref, b_ref, dgu_ref, dw_ref, db_ref, stage, sems):
        s, j = pl.program_id(0), pl.program_id(1)
        n = s * nc + j
        slot = n % 2

        def copies(slot_, j_, s_):
            rows = pl.ds(pl.multiple_of(s_ * l, 16), l)
            return [pltpu.make_async_copy(
                stage.at[slot_, half],
                dgu_ref.at[rows, pl.ds(pl.multiple_of((half * nc + j_) * CONV_COLS, 128), CONV_COLS)],
                sems.at[slot_, half]) for half in (0, 1)]

        @pl.when(n >= 2)
        def _():
            for cp in copies(slot, j, s):
                cp.wait()

        da = lax.dot_general(dh_ref[...], wd_ref[...], _NT, preferred_element_type=F32)
        g, wv = g_ref[...].astype(F32), w_ref[...]
        row = lax.broadcasted_iota(jnp.int32, g.shape, 0)
        g1, g2 = _shift_down(g, 1, row), _shift_down(g, 2, row)
        pre = b_ref[...] + wv[0:1, :] * g2 + wv[1:2, :] * g1 + wv[2:3, :] * g
        sg = jax.nn.sigmoid(pre)
        silu = pre * sg
        stage[slot, 1] = (da * silu).astype(stage.dtype)
        dpre = da * u_ref[...].astype(F32) * (sg * (1.0 + pre * (1.0 - sg)))
        dg = wv[2:3, :] * dpre + wv[1:2, :] * _shift_up(dpre, 1, row) + wv[0:1, :] * _shift_up(dpre, 2, row)
        stage[slot, 0] = dg.astype(stage.dtype)
        for cp in copies(slot, j, s):
            cp.start()
        dw_ref[0:1, :] = jnp.sum(dpre * g2, axis=0, keepdims=True)
        dw_ref[1:2, :] = jnp.sum(dpre * g1, axis=0, keepdims=True)
        dw_ref[2:3, :] = jnp.sum(dpre * g, axis=0, keepdims=True)
        db_ref[...] = jnp.sum(dpre, axis=0, keepdims=True)

        @pl.when(n == steps - 1)
        def _():
            for cp in copies(slot, j, s) + (copies(1 - slot, j, s) if steps > 1 else []):
                cp.wait()

    cols = pl.BlockSpec((l, CONV_COLS), lambda s, j: (s, j))
    return pl.pallas_call(
        body, name="ffn_down_dx_gate", grid=(seqs, nc),
        in_specs=[pl.BlockSpec((l, dh.shape[1]), lambda s, j: (s, 0)),
                  pl.BlockSpec((CONV_COLS, dh.shape[1]), lambda s, j: (j, 0)), cols, cols,
                  pl.BlockSpec((3, CONV_COLS), lambda s, j: (0, j)), pl.BlockSpec((1, CONV_COLS), lambda s, j: (0, j))],
        out_specs=[ANY, pl.BlockSpec((None, 3, CONV_COLS), lambda s, j: (s, 0, j)),
                   pl.BlockSpec((None, 1, CONV_COLS), lambda s, j: (s, 0, j))],
        out_shape=[jax.ShapeDtypeStruct((t, 2 * D_FF), BF16), jax.ShapeDtypeStruct((seqs, 3, D_FF), F32),
                   jax.ShapeDtypeStruct((seqs, 1, D_FF), F32)],
        scratch_shapes=[pltpu.VMEM((2, 2, l, CONV_COLS), BF16), pltpu.SemaphoreType.DMA((2, 2))],
        compiler_params=_params(("arbitrary", "arbitrary")),
    )(dh, w_down, gate, up, w, b)


def _loss_head(h, target):
    t, d = h.shape
    tm = _pick(t, (256, 128, 8))

    def body(h_ref, t_ref, dh_ref, dhb_ref, loss_ref):
        @pl.when(pl.program_id(0) == 0)
        def _():
            loss_ref[...] = jnp.zeros(loss_ref.shape, F32)

        e = h_ref[...] - t_ref[...]
        dh = e * (1.0 / d)
        dh_ref[...] = dh
        dhb_ref[...] = dh.astype(BF16)
        loss_ref[...] += (0.5 / d) * jnp.sum(jnp.sum(e * e, axis=1, keepdims=True), axis=0, keepdims=True)

    blk = pl.BlockSpec((tm, d), lambda i: (i, 0))
    return pl.pallas_call(
        body, name="loss_head", grid=(t // tm,), in_specs=[blk, blk],
        out_specs=[blk, blk, pl.BlockSpec((1, 1), lambda i: (0, 0))],
        out_shape=[jax.ShapeDtypeStruct((t, d), F32), jax.ShapeDtypeStruct((t, d), BF16),
                   jax.ShapeDtypeStruct((1, 1), F32)],
        compiler_params=_params(("arbitrary",)),
    )(h, target)


def _s5_discretise(a_re, a_im, log_dt, b_re, b_im):
    dt = jnp.exp(log_dt)[:, None]
    mag = jnp.exp(a_re * dt)
    lb_r = mag * jnp.cos(a_im * dt)
    lb_i = mag * jnp.sin(a_im * dt)
    den = a_re * a_re + a_im * a_im
    nr = lb_r - 1.0
    coef_r = (nr * a_re + lb_i * a_im) / den
    coef_i = (lb_i * a_re - nr * a_im) / den
    bb_r = coef_r[:, :, None] * b_re - coef_i[:, :, None] * b_im
    bb_i = coef_r[:, :, None] * b_im + coef_i[:, :, None] * b_re
    return lb_r, lb_i, bb_r, bb_i


S5_CHUNKS = 4
S5_PER = S5_GROUPS // S5_CHUNKS


def _blockdiag_in(bb):
    eye = jnp.eye(S5_PER, dtype=bb.dtype)
    return jnp.einsum("jgpc,gh->jgchp", bb.reshape(S5_CHUNKS, S5_PER, S5_STATE, S5_GROUP_CH), eye).reshape(
        S5_CHUNKS, S5_PER * S5_GROUP_CH, S5_PER * S5_STATE)


def _blockdiag_in_grad(d):
    eye = jnp.eye(S5_PER, dtype=d.dtype)
    return jnp.einsum("jgchp,gh->jgpc", d.reshape(S5_CHUNKS, S5_PER, S5_GROUP_CH, S5_PER, S5_STATE), eye).reshape(
        S5_GROUPS, S5_STATE, S5_GROUP_CH)


def _blockdiag_out(c):
    eye = jnp.eye(S5_PER, dtype=c.dtype)
    return jnp.einsum("jgcp,gh->jgphc", c.reshape(S5_CHUNKS, S5_PER, S5_GROUP_CH, S5_STATE), eye).reshape(
        S5_CHUNKS, S5_PER * S5_STATE, S5_PER * S5_GROUP_CH)


def _blockdiag_out_grad(d):
    eye = jnp.eye(S5_PER, dtype=d.dtype)
    return jnp.einsum("jgphc,gh->jgcp", d.reshape(S5_CHUNKS, S5_PER, S5_STATE, S5_PER, S5_GROUP_CH), eye).reshape(
        S5_GROUPS, S5_GROUP_CH, S5_STATE)


def _local_step(x3, mem3, target3, p, wb, late_weights=None, early_grads=None):
    seqs, l, d = x3.shape
    t = seqs * l
    x = x3.reshape(t, d)
    mem = mem3.reshape(seqs * N_MEM, d)
    target = target3.reshape(t, d)
    full = lambda a: (a, a.shape[1], 0, 0)

    s5_in = (p["s5_a_re"], p["s5_a_im"], p["s5_log_dt"], p["s5_b_re"], p["s5_b_im"])
    (lb_r, lb_i, bb_r, bb_i), s5_pull = jax.vjp(_s5_discretise, *s5_in)
    ar, ai = lb_r.reshape(1, S5_CH), lb_i.reshape(1, S5_CH)
    bbr_d, bbi_d = _blockdiag_in(bb_r).astype(BF16), _blockdiag_in(bb_i).astype(BF16)
    cr_d, ci_d = _blockdiag_out(p["s5_c_re"]).astype(BF16), (-_blockdiag_out(p["s5_c_im"])).astype(BF16)
    d_row = p["s5_d"].reshape(1, S5_WIDTH)

    w_in = wb["w_in"]
    w_qkv = w_in[:, :3 * FOX_WIDTH]
    w_uf = jnp.concatenate(
        [w_in[:, 3 * FOX_WIDTH + N_FOX_HEADS:], w_in[:, 3 * FOX_WIDTH:3 * FOX_WIDTH + N_FOX_HEADS],
         jnp.zeros((d, UF_COLS - S5_WIDTH - N_FOX_HEADS), w_in.dtype)], axis=1)

    hn1 = _rowwise(_rms, [full(x)], [p["norm_mix"]], [(d, d, 0, BF16)], "norm_mix_fwd")
    qkv = _mm(hn1, w_qkv, "nn", "in_qkv")
    uf = _mm(hn1, w_uf, "nn", "in_uf")

    bh = seqs * N_FOX_HEADS
    q_pair = (qkv, 128, 0, 1)
    k_pair = (qkv, 128, N_PAIRS, 1)
    gq2, gk2 = jnp.tile(p["fox_q_norm"], (1, 2)), jnp.tile(p["fox_k_norm"], (1, 2))
    pair_out = [(FOX_WIDTH, 128, 1, BF16)]
    qn = _rowwise(_rms_pair, [q_pair], [gq2], pair_out, "fox_qnorm_fwd", heads=N_PAIRS)
    kn = _rowwise(_rms_pair, [k_pair], [gk2], pair_out, "fox_knorm_fwd", heads=N_PAIRS)

    f_rows = uf[:, S5_WIDTH:S5_WIDTH + N_FOX_HEADS].reshape(seqs, l, N_FOX_HEADS).transpose(0, 2, 1).reshape(bh, l)
    f_bias = jnp.tile(p["fox_f_bias"].reshape(N_FOX_HEADS, 1), (seqs, 1))
    c_wide = jnp.broadcast_to(_forget_fwd(f_rows, f_bias)[:, :, None], (bh, l, 128))
    fox, lse = _fox_fwd(qn, kn, qkv, c_wide, seqs)

    xr, xi, ys = _s5_fwd(uf, bbr_d, bbi_d, cr_d, ci_d, ar, ai, seqs)
    u_blk = (uf, S5_WIDTH, 0, 0)
    yg = _rowwise(_s5_act, [full(ys), u_blk], [d_row], [(S5_WIDTH, S5_WIDTH, 0, F32)], "s5_act_fwd")
    if late_weights is not None:
        wb = dict(wb, **late_weights("mid", yg))
    z = _mm(yg, wb["s5_w_glu"], "nn", "s5_glu")
    y2n = _rowwise(_s5_gate, [full(yg), full(z)], [p["s5_b_glu"], p["out_norm_s5"]],
                   [(S5_WIDTH, S5_WIDTH, 0, BF16)], "s5_gate_fwd")
    foxn = _rowwise(_rms, [full(fox)], [p["out_norm_fox"]], [(FOX_WIDTH, FOX_WIDTH, 0, BF16)], "fox_outnorm_fwd")
    mixed = jnp.concatenate([foxn, y2n], axis=1)
    h1 = _mm(mixed, wb["w_out"], "nn", "mix_out", res=x)
    if late_weights is not None:
        wb = dict(wb, **late_weights("late", h1))

    hn2 = _rowwise(_rms, [full(h1)], [p["norm_cross"]], [(d, d, 0, BF16)], "norm_cross_fwd")
    mn = _rowwise(_rms, [full(mem)], [p["norm_mem"]], [(d, d, 0, BF16)], "norm_mem_fwd")
    xq_raw = _mm(hn2, wb["w_xq"], "nn", "x_q")
    kv = _mm(mn, wb["w_xkv"], "nn", "x_kv")
    xh = lambda a: (a, X_HEAD_DIM, 0, 1)
    xqn = _rowwise(_rms, [xh(xq_raw)], [p["xq_norm"]], [(d, X_HEAD_DIM, 1, BF16)], "x_qnorm_fwd", heads=N_X_HEADS)
    xkn = _rowwise(_rms, [xh(kv)], [p["xk_norm"]], [(d, X_HEAD_DIM, 1, BF16)], "x_knorm_fwd", heads=N_X_HEADS)
    xo = _xatt_fwd(xqn, xkn, kv, seqs)
    h2 = _mm(xo, wb["w_xo"], "nn", "x_out", res=h1)

    hn3 = _rowwise(_rms, [full(h2)], [p["norm_ffn"]], [(d, d, 0, BF16)], "norm_ffn_fwd")
    gate, up, act = _ffn_up_gate(hn3, wb["w_ffn_up"], p["ffn_conv_w"], p["ffn_conv_b"], seqs)
    h3 = _mm(act, wb["w_ffn_down"], "nn", "ffn_down", res=h2)
    dh3, dh3_b, loss = _loss_head(h3, target)

    g = {}
    late_dt = BF16 if early_grads is not None else F32
    g["w_ffn_down"] = _mm(act, dh3_b, "tn", "ffn_down_dw", out_dtype=late_dt)
    dgu, dconv_w, dconv_b = _ffn_down_dx_gate(dh3_b, wb["w_ffn_down"], gate, up, p["ffn_conv_w"], p["ffn_conv_b"], seqs)
    g["ffn_conv_w"], g["ffn_conv_b"] = jnp.sum(dconv_w, axis=0), jnp.sum(dconv_b, axis=0)
    dhn3 = _mm(dgu, wb["w_ffn_up"], "nt", "ffn_up_dx", out_dtype=BF16)
    g["w_ffn_up"] = _mm(hn3, dgu, "tn", "ffn_up_dw", out_dtype=late_dt)
    (dh2,), (g["norm_ffn"],) = _rowwise_vjp(_rms, [full(h2)], [p["norm_ffn"]], [full(dhn3)], "norm_ffn_bwd",
                                            adds=[full(dh3)])

    dxo = _mm(dh2, wb["w_xo"], "nt", "x_out_dx", out_dtype=BF16)
    g["w_xo"] = _mm(xo, dh2, "tn", "x_out_dw", out_dtype=late_dt)
    dxqn, dxkn, dxv = _xatt_bwd(xqn, xkn, kv, dxo, seqs)
    (dxq_raw,), (g["xq_norm"],) = _rowwise_vjp(_rms, [xh(xq_raw)], [p["xq_norm"]], [xh(dxqn)], "x_qnorm_bwd",
                                               heads=N_X_HEADS, row_dtypes=[BF16])
    (dxk_raw,), (g["xk_norm"],) = _rowwise_vjp(_rms, [xh(kv)], [p["xk_norm"]], [xh(dxkn)], "x_knorm_bwd",
                                               heads=N_X_HEADS, row_dtypes=[BF16])
    dkv = jnp.concatenate([dxk_raw, dxv.astype(BF16)], axis=1)
    dhn2 = _mm(dxq_raw, wb["w_xq"], "nt", "x_q_dx", out_dtype=BF16)
    g["w_xq"] = _mm(hn2, dxq_raw, "tn", "x_q_dw", out_dtype=late_dt)
    dmn = _mm(dkv, wb["w_xkv"], "nt", "x_kv_dx")
    g["w_xkv"] = _mm(mn, dkv, "tn", "x_kv_dw", out_dtype=late_dt)
    norm_cross = p["norm_cross"]
    if early_grads is not None:
        norm_cross = norm_cross + early_grads({n: g[n] for n in LATE_WEIGHTS})
    (dh1,), (g["norm_cross"],) = _rowwise_vjp(_rms, [full(h1)], [norm_cross], [full(dhn2)], "norm_cross_bwd",
                                              adds=[full(dh2)])
    _, (g["norm_mem"],) = _rowwise_vjp(_rms, [full(mem)], [p["norm_mem"]], [full(dmn)], "norm_mem_bwd",
                                       row_dtypes=[BF16])

    dmixed = _mm(dh1, wb["w_out"], "nt", "mix_out_dx", out_dtype=BF16)
    g["w_out"] = _mm(mixed, dh1, "tn", "mix_out_dw", out_dtype=late_dt)
    (dfox,), (g["out_norm_fox"],) = _rowwise_vjp(_rms, [full(fox)], [p["out_norm_fox"]],
                                                 [(dmixed, FOX_WIDTH, 0, 0)], "fox_outnorm_bwd")
    (dyg_a, dz), (g["s5_b_glu"], g["out_norm_s5"]) = _rowwise_vjp(
        _s5_gate, [full(yg), full(z)], [p["s5_b_glu"], p["out_norm_s5"]], [(dmixed, S5_WIDTH, 1, 0)], "s5_gate_bwd",
        row_dtypes=[F32, BF16])
    dyg = _mm(dz, wb["s5_w_glu"], "nt", "s5_glu_dx", res=dyg_a)
    g["s5_w_glu"] = _mm(yg, dz, "tn", "s5_glu_dw", out_dtype=late_dt)
    (dys, du_a), (dd_row,) = _rowwise_vjp(_s5_act, [full(ys), u_blk], [d_row], [full(dyg)], "s5_act_bwd",
                                          row_dtypes=[BF16, F32])
    g["s5_d"] = dd_row
    du_b, dbbr_d, dbbi_d, dcr_d, dci_d, dar, dai = _s5_bwd(dys, uf, xr, xi, bbr_d, bbi_d, cr_d, ci_d, ar, ai, seqs)
    dbbr_d, dbbi_d, dcr_d, dci_d = (jnp.sum(a, axis=0) for a in (dbbr_d, dbbi_d, dcr_d, dci_d))
    d_lb_r = jnp.sum(dar, axis=0).reshape(S5_GROUPS, S5_STATE)
    d_lb_i = jnp.sum(dai, axis=0).reshape(S5_GROUPS, S5_STATE)
    g["s5_a_re"], g["s5_a_im"], g["s5_log_dt"], g["s5_b_re"], g["s5_b_im"] = s5_pull(
        (d_lb_r, d_lb_i, _blockdiag_in_grad(dbbr_d), _blockdiag_in_grad(dbbi_d)))
    g["s5_c_re"] = _blockdiag_out_grad(dcr_d)
    g["s5_c_im"] = -_blockdiag_out_grad(dci_d)

    dqn, dkn, dv, dc, dcq = _fox_bwd(qn, kn, qkv, c_wide, fox, dfox, lse, seqs)
    pair = lambda a: (a, 128, 0, 1)
    (dq_raw,), (dgq2,) = _rowwise_vjp(_rms_pair, [q_pair], [gq2], [pair(dqn)], "fox_qnorm_bwd", heads=N_PAIRS,
                                      row_dtypes=[BF16])
    (dk_raw,), (dgk2,) = _rowwise_vjp(_rms_pair, [k_pair], [gk2], [pair(dkn)], "fox_knorm_bwd", heads=N_PAIRS,
                                      row_dtypes=[BF16])
    g["fox_q_norm"] = dgq2[:, :HEAD_DIM] + dgq2[:, HEAD_DIM:]
    g["fox_k_norm"] = dgk2[:, :HEAD_DIM] + dgk2[:, HEAD_DIM:]
    df_rows, dfb = _forget_bwd(f_rows, f_bias, (dc + dcq).reshape(bh, l))
    g["fox_f_bias"] = jnp.sum(dfb.reshape(seqs, N_FOX_HEADS), axis=0)
    df = df_rows.reshape(seqs, N_FOX_HEADS, l).transpose(0, 2, 1).reshape(t, N_FOX_HEADS)
    dqkv = jnp.concatenate([dq_raw, dk_raw, dv.astype(BF16)], axis=1)
    duf = jnp.concatenate([du_a + du_b, df, jnp.zeros((t, UF_COLS - S5_WIDTH - N_FOX_HEADS), F32)],
                          axis=1).astype(BF16)
    dhn1 = _mm(duf, w_uf, "nt", "in_uf_dx", res=_mm(dqkv, w_qkv, "nt", "in_qkv_dx"), out_dtype=BF16)
    dw_qkv = _mm(hn1, dqkv, "tn", "in_qkv_dw")
    dw_uf = _mm(hn1, duf, "tn", "in_uf_dw")
    g["w_in"] = jnp.concatenate([dw_qkv, dw_uf[:, S5_WIDTH:S5_WIDTH + N_FOX_HEADS], dw_uf[:, :S5_WIDTH]], axis=1)
    (dx,), (g["norm_mix"],) = _rowwise_vjp(_rms, [full(x)], [p["norm_mix"]], [full(dhn1)], "norm_mix_bwd",
                                           adds=[full(dh1)])
    return loss, dx.reshape(seqs, l, d), g


def _place():
    return lax.axis_index("x"), lax.axis_index("y"), lax.axis_index("c")


def _other_chips(x, y):
    return [(1 - x, y), (x, 1 - y), (1 - x, 1 - y)]


ANY = pl.BlockSpec(memory_space=pl.ANY)


def _gather_weights(shards, col_kind, taps):
    n = len(shards)

    def body(*refs):
        ins, tap_in, outs, tap_out = refs[:n], refs[n], refs[n + 1:2 * n + 1], refs[2 * n + 1]
        ici_send, ici_recv, d2d_send, d2d_recv, own_send, own_recv = refs[2 * n + 2:]
        x, y, c = _place()
        mine = 2 * x + y
        chips = _other_chips(x, y)
        sibling = (x, y, 1 - c)

        def piece(a, s, h):
            r, cs = ins[a].shape
            hr = r // 2
            if col_kind[a]:
                return outs[a].at[pl.ds(pl.multiple_of(h * hr, 16), hr), pl.ds(pl.multiple_of(s * cs, 128), cs)]
            return outs[a].at[pl.ds(pl.multiple_of(s * r + h * hr, 16), hr), :]

        def slab(a, s):
            r, cs = ins[a].shape
            if col_kind[a]:
                return outs[a].at[:, pl.ds(pl.multiple_of(s * cs, 128), cs)]
            return outs[a].at[pl.ds(pl.multiple_of(s * r, 16), r), :]

        def own_half(a, h):
            hr = ins[a].shape[0] // 2
            return ins[a].at[pl.ds(pl.multiple_of(h * hr, 16), hr), :]

        sends = []
        for a in range(n):
            cp = pltpu.make_async_remote_copy(
                src_ref=ins[a], dst_ref=slab(a, mine), send_sem=own_send.at[a], recv_sem=own_recv.at[a],
                device_id=sibling, device_id_type=MESH)
            cp.start()
            sends.append(cp)
        cp = pltpu.make_async_remote_copy(
            src_ref=tap_in, dst_ref=tap_out.at[mine], send_sem=own_send.at[n], recv_sem=own_recv.at[n],
            device_id=sibling, device_id_type=MESH)
        cp.start()
        sends.append(cp)
        for a in range(n):
            for j, (px, py) in enumerate(chips):
                cp = pltpu.make_async_remote_copy(
                    src_ref=own_half(a, c), dst_ref=piece(a, mine, c), send_sem=ici_send.at[3 * a + j],
                    recv_sem=ici_recv.at[3 * a + j], device_id=(px, py, c), device_id_type=MESH)
                cp.start()
                sends.append(cp)
        for j, (px, py) in enumerate(chips):
            cp = pltpu.make_async_remote_copy(
                src_ref=tap_in, dst_ref=tap_out.at[mine], send_sem=ici_send.at[3 * n + j],
                recv_sem=ici_recv.at[3 * n + j], device_id=(px, py, c), device_id_type=MESH)
            cp.start()
            sends.append(cp)
        for a in range(n):
            for j, (px, py) in enumerate(chips):
                got = piece(a, 2 * px + py, c)
                pltpu.make_async_remote_copy(
                    src_ref=got, dst_ref=got, send_sem=ici_send.at[3 * a + j], recv_sem=ici_recv.at[3 * a + j],
                    device_id=(px, py, c), device_id_type=MESH).wait_recv()
                fwd = pltpu.make_async_remote_copy(
                    src_ref=got, dst_ref=got, send_sem=d2d_send.at[3 * a + j], recv_sem=d2d_recv.at[3 * a + j],
                    device_id=(x, y, 1 - c), device_id_type=MESH)
                fwd.start()
                sends.append(fwd)
        for a in range(n):
            for j, (px, py) in enumerate(chips):
                other = piece(a, 2 * px + py, 1 - c)
                pltpu.make_async_remote_copy(
                    src_ref=other, dst_ref=other, send_sem=d2d_send.at[3 * a + j], recv_sem=d2d_recv.at[3 * a + j],
                    device_id=(x, y, 1 - c), device_id_type=MESH).wait_recv()
        for j, (px, py) in enumerate(chips):
            pltpu.make_async_remote_copy(
                src_ref=tap_in, dst_ref=tap_out.at[2 * px + py], send_sem=ici_send.at[3 * n + j],
                recv_sem=ici_recv.at[3 * n + j], device_id=(px, py, c), device_id_type=MESH).wait_recv()
        for a in range(n):
            pltpu.make_async_remote_copy(
                src_ref=ins[a], dst_ref=slab(a, mine), send_sem=own_send.at[a], recv_sem=own_recv.at[a],
                device_id=sibling, device_id_type=MESH).wait_recv()
        pltpu.make_async_remote_copy(
            src_ref=tap_in, dst_ref=tap_out.at[mine], send_sem=own_send.at[n], recv_sem=own_recv.at[n],
            device_id=sibling, device_id_type=MESH).wait_recv()
        for cp in sends:
            cp.wait_send()

    def full_shape(a):
        r, cs = shards[a].shape
        return (r, 4 * cs) if col_kind[a] else (4 * r, cs)

    res = pl.pallas_call(
        body, name="gather_weights", in_specs=[ANY] * (n + 1), out_specs=[ANY] * (n + 1),
        out_shape=[jax.ShapeDtypeStruct(full_shape(a), shards[a].dtype) for a in range(n)]
        + [jax.ShapeDtypeStruct((4,) + taps.shape, taps.dtype)],
        scratch_shapes=[pltpu.SemaphoreType.DMA((3 * n + 3,)), pltpu.SemaphoreType.DMA((3 * n + 3,)),
                        pltpu.SemaphoreType.DMA((3 * n,)), pltpu.SemaphoreType.DMA((3 * n,)),
                        pltpu.SemaphoreType.DMA((n + 1,)), pltpu.SemaphoreType.DMA((n + 1,))],
        compiler_params=pltpu.CompilerParams(has_side_effects=True),
    )(*shards, taps)
    return res[:n], res[n]


HBM = pl.BlockSpec(memory_space=pltpu.HBM)
SEM = pl.BlockSpec(memory_space=pltpu.SEMAPHORE)
DATAFLOW = pltpu.SideEffectType.DATAFLOW_SIDE_EFFECTING


def _in_hbm(a):
    return pltpu.with_memory_space_constraint(a, pltpu.HBM)


def _split_start(name, srcs, lands, n_copies, plan):
    n = len(srcs)

    def body(*refs):
        src_refs, land_refs = refs[:n], refs[n:2 * n]
        send_sems, recv_sems = refs[2 * n], refs[2 * n + 1]
        for i, (src, dst, dev) in enumerate(plan(src_refs, land_refs)):
            pltpu.make_async_remote_copy(src_ref=src, dst_ref=dst, send_sem=send_sems.at[i], recv_sem=recv_sems.at[i],
                                         device_id=dev, device_id_type=MESH).start()
        refs[-1][...] = jnp.zeros((8, 128), F32)

    res = pl.pallas_call(
        body, name=name, in_specs=[HBM] * (2 * n),
        out_specs=[SEM, SEM] + [HBM] * (2 * n) + [pl.BlockSpec(memory_space=pltpu.VMEM)],
        out_shape=[pltpu.SemaphoreType.DMA((n_copies,)), pltpu.SemaphoreType.DMA((n_copies,))]
        + [pltpu.HBM(a.shape, a.dtype) for a in list(srcs) + list(lands)] + [jax.ShapeDtypeStruct((8, 128), F32)],
        input_output_aliases={i: 2 + i for i in range(2 * n)},
        compiler_params=pltpu.CompilerParams(has_side_effects=DATAFLOW),
    )(*[_in_hbm(a) for a in list(srcs) + list(lands)])
    return res[0], res[1], list(res[2:2 + n]), list(res[2 + n:2 + 2 * n]), res[-1]


def _split_wait(name, send_sems, recv_sems, srcs, lands, after, plan):
    n = len(srcs)

    def body(*refs):
        src_refs, land_refs = refs[:n], refs[n:2 * n]
        send_ref, recv_ref = refs[2 * n], refs[2 * n + 1]
        for i, (src, dst, dev) in enumerate(plan(src_refs, land_refs)):
            cp = pltpu.make_async_remote_copy(src_ref=src, dst_ref=dst, send_sem=send_ref.at[i], recv_sem=recv_ref.at[i],
                                              device_id=dev, device_id_type=MESH)
            cp.wait_send()
            cp.wait_recv()

    res = pl.pallas_call(
        body, name=name, in_specs=[HBM] * (2 * n) + [SEM, SEM, ANY], out_specs=[HBM] * (2 * n),
        out_shape=[pltpu.HBM(a.shape, a.dtype) for a in list(srcs) + list(lands)],
        input_output_aliases={i: i for i in range(2 * n)},
        compiler_params=pltpu.CompilerParams(has_side_effects=DATAFLOW),
    )(*srcs, *lands, send_sems, recv_sems, after)
    return list(res[:n]), list(res[n:])


def _late_gather_plan(col_kind):
    def plan(src_refs, land_refs):
        x, y, c = _place()
        mine = 2 * x + y
        copies = []
        for a, (src, land) in enumerate(zip(src_refs, land_refs)):
            r, cs = src.shape
            if col_kind[a]:
                dst = land.at[:, pl.ds(pl.multiple_of(mine * cs, 128), cs)]
            else:
                dst = land.at[pl.ds(pl.multiple_of(mine * r, 16), r), :]
            copies.append((src, dst, (x, y, 1 - c)))
            copies += [(src, dst, (px, py, c)) for (px, py) in _other_chips(x, y)]
        return copies
    return plan


def _late_reduce_plan(col_kind):
    def plan(src_refs, land_refs):
        x, y, c = _place()
        copies = []
        for a, (src, land) in enumerate(zip(src_refs, land_refs)):
            for j, (px, py) in enumerate(_other_chips(x, y)):
                if col_kind[a]:
                    cs = land.shape[2]
                    piece = src.at[:, pl.ds(pl.multiple_of((2 * px + py) * cs, 128), cs)]
                else:
                    piece = src.at[2 * px + py]
                copies.append((piece, land.at[j], (px, py, c)))
        return copies
    return plan


def _pair_swap(name, halves):
    n = len(halves)

    def body(*refs):
        ins, outs = refs[:n], refs[n:2 * n]
        send_sems, recv_sems = refs[2 * n:]
        x, y, c = _place()
        copies = []
        for a in range(n):
            cp = pltpu.make_async_remote_copy(
                src_ref=ins[a], dst_ref=outs[a], send_sem=send_sems.at[a], recv_sem=recv_sems.at[a],
                device_id=(x, y, 1 - c), device_id_type=MESH)
            cp.start()
            copies.append(cp)
        for cp in copies:
            cp.wait()

    return pl.pallas_call(
        body, name=name, in_specs=[ANY] * n, out_specs=[ANY] * n,
        out_shape=[jax.ShapeDtypeStruct(s.shape, s.dtype) for s in halves],
        scratch_shapes=[pltpu.SemaphoreType.DMA((n,)), pltpu.SemaphoreType.DMA((n,))],
        compiler_params=pltpu.CompilerParams(has_side_effects=True),
    )(*halves)


def _chip_sum(name, chip_sel, own, col, others):
    _, r, c = others.shape
    tr = _pick(r, (256, 128, 64, 32, 16))
    if col:
        own_spec = pl.BlockSpec((tr, c), lambda i, s: (i, s[0]))
    else:
        own_spec = pl.BlockSpec((None, tr, c), lambda i, s: (s[0], i, 0))
    specs = [own_spec] + [pl.BlockSpec((None, tr, c), lambda i, s, k=k: (k, i, 0)) for k in range(3)]

    def body(s_ref, own_ref, r0, r1, r2, o_ref):
        total = ((own_ref[...].astype(F32) + r0[...].astype(F32)) + r1[...].astype(F32)) + r2[...].astype(F32)
        o_ref[...] = total.astype(o_ref.dtype)

    return pl.pallas_call(
        body, name=name,
        grid_spec=pltpu.PrefetchScalarGridSpec(
            num_scalar_prefetch=1, grid=(r // tr,), in_specs=specs,
            out_specs=pl.BlockSpec((tr, c), lambda i, s: (i, 0))),
        out_shape=jax.ShapeDtypeStruct((r, c), BF16),
        compiler_params=_params(("parallel",)),
    )(chip_sel, own, others, others, others)


def _allreduce_small(vals):
    sizes = [int(math.prod(v.shape)) for v in vals]
    padded = [-(-s // 128) * 128 for s in sizes]
    total = -(-sum(padded) // 1024) * 1024
    flat = [jnp.pad(v.reshape(-1), (0, p - s)) for v, s, p in zip(vals, sizes, padded)]
    flat.append(jnp.zeros((total - sum(padded),), F32))
    packed = jnp.concatenate(flat).reshape(total // 128, 128)

    def body(in_ref, out_ref, r0, r1, r2, send_sems, recv_sems):
        x, y, c = _place()
        out_ref[...] = in_ref[...]
        for k, (peer, land) in enumerate(zip([(x, y, 1 - c), (1 - x, y, c), (x, 1 - y, c)], (r0, r1, r2))):
            cp = pltpu.make_async_remote_copy(
                src_ref=out_ref, dst_ref=land, send_sem=send_sems.at[k], recv_sem=recv_sems.at[k],
                device_id=peer, device_id_type=MESH)
            cp.start()
            cp.wait()
            out_ref[...] = out_ref[...] + land[...]

    vm = pl.BlockSpec(memory_space=pltpu.VMEM)
    summed = pl.pallas_call(
        body, name="allreduce_small", in_specs=[vm], out_specs=vm,
        out_shape=jax.ShapeDtypeStruct(packed.shape, F32),
        scratch_shapes=[pltpu.VMEM(packed.shape, F32)] * 3
        + [pltpu.SemaphoreType.DMA((3,)), pltpu.SemaphoreType.DMA((3,))],
        compiler_params=pltpu.CompilerParams(has_side_effects=True, vmem_limit_bytes=VMEM_LIMIT_BYTES),
    )(packed).reshape(-1)
    outs, off = [], 0
    for v, s, p in zip(vals, sizes, padded):
        outs.append(summed[off:off + s].reshape(v.shape))
        off += p
    return outs


def _adamw_math(w, g, m, v):
    m2 = ADAM_B1 * m + (1.0 - ADAM_B1) * g
    v2 = ADAM_B2 * v + (1.0 - ADAM_B2) * (g * g)
    m_hat = m2 / (1.0 - ADAM_B1 ** ADAM_STEP)
    v_hat = v2 / (1.0 - ADAM_B2 ** ADAM_STEP)
    delta = -ADAM_LR * (m_hat / (jnp.sqrt(v_hat) + ADAM_EPS) + ADAM_WD * w)
    return delta, m2, v2


def _adamw_big(name, w, g_mine, g_sibling, m, v):
    _, r, c = w.shape

    def body(w_ref, ga_ref, gb_ref, m_ref, v_ref, go_ref, d_ref, mo_ref, vo_ref):
        gv = ga_ref[...].astype(F32) + gb_ref[...].astype(F32)
        d, m2, v2 = _adamw_math(w_ref[...], gv, m_ref[...], v_ref[...])
        go_ref[...] = gv
        d_ref[...] = d
        mo_ref[...] = m2
        vo_ref[...] = v2

    tr = _pick(r, (256, 128, 64, 32, 16, 8))
    if r % tr == 0 and tr % 8 == 0:
        grid = (r // tr,)
        blk = pl.BlockSpec((None, tr, c), lambda i: (0, i, 0))
        part = pl.BlockSpec((tr, c), lambda i: (i, 0))
    else:
        grid = (c // 512,)
        blk = pl.BlockSpec((None, r, 512), lambda i: (0, 0, i))
        part = pl.BlockSpec((r, 512), lambda i: (0, i))
    return pl.pallas_call(
        body, name=name, grid=grid, in_specs=[blk, part, part, blk, blk], out_specs=[blk] * 4,
        out_shape=[jax.ShapeDtypeStruct((1, r, c), F32)] * 4, compiler_params=_params(("parallel",)),
    )(w, g_mine, g_sibling, m, v)


def _adamw_small(ws, gs, ms, vs):
    n = len(ws)

    def body(*refs):
        w_r, g_r, m_r, v_r = refs[:n], refs[n:2 * n], refs[2 * n:3 * n], refs[3 * n:4 * n]
        o = refs[4 * n:]
        for a in range(n):
            gv = g_r[a][...]
            d, m2, v2 = _adamw_math(w_r[a][...], gv, m_r[a][...], v_r[a][...])
            o[a][...] = gv
            o[n + a][...] = d
            o[2 * n + a][...] = m2
            o[3 * n + a][...] = v2

    res = pl.pallas_call(
        body, name="adamw_small", out_shape=[jax.ShapeDtypeStruct(w.shape, F32) for _ in range(4) for w in ws],
        compiler_params=_params(),
    )(*ws, *gs, *ms, *vs)
    return res[:n], res[n:2 * n], res[2 * n:3 * n], res[3 * n:]


def _full_from_gathered(name, gathered):
    if name == "w_in":
        rows = gathered.shape[0] // 4
        return gathered.reshape(4, rows, gathered.shape[1]).transpose(1, 0, 2).reshape(rows, 4 * gathered.shape[1])
    return gathered


def _reduce_layout(name, full):
    if name in COL_KIND:
        return full
    if name == "w_in":
        rows, cols = full.shape
        return full.reshape(rows, 4, cols // 4).transpose(1, 0, 2)
    return full.reshape(4, full.shape[0] // 4, full.shape[1])


def kernel(x, mem, norm_mix, w_in, fox_q_norm, fox_k_norm, fox_f_bias, s5_a_re, s5_a_im, s5_log_dt, s5_b_re, s5_b_im, s5_c_re, s5_c_im, s5_d, s5_w_glu, s5_b_glu, out_norm_fox, out_norm_s5, w_out, norm_cross, norm_mem, w_xq, w_xkv, xq_norm, xk_norm, w_xo, norm_ffn, w_ffn_up, ffn_conv_w, ffn_conv_b, w_ffn_down, loss_target, m_norm_mix, m_w_in, m_fox_q_norm, m_fox_k_norm, m_fox_f_bias, m_s5_a_re, m_s5_a_im, m_s5_log_dt, m_s5_b_re, m_s5_b_im, m_s5_c_re, m_s5_c_im, m_s5_d, m_s5_w_glu, m_s5_b_glu, m_out_norm_fox, m_out_norm_s5, m_w_out, m_norm_cross, m_norm_mem, m_w_xq, m_w_xkv, m_xq_norm, m_xk_norm, m_w_xo, m_norm_ffn, m_w_ffn_up, m_ffn_conv_w, m_ffn_conv_b, m_w_ffn_down, v_norm_mix, v_w_in, v_fox_q_norm, v_fox_k_norm, v_fox_f_bias, v_s5_a_re, v_s5_a_im, v_s5_log_dt, v_s5_b_re, v_s5_b_im, v_s5_c_re, v_s5_c_im, v_s5_d, v_s5_w_glu, v_s5_b_glu, v_out_norm_fox, v_out_norm_s5, v_w_out, v_norm_cross, v_norm_mem, v_w_xq, v_w_xkv, v_xq_norm, v_xk_norm, v_w_xo, v_norm_ffn, v_w_ffn_up, v_ffn_conv_w, v_ffn_conv_b, v_w_ffn_down):
    given = dict(locals())
    w = {n: given[n] for n in WEIGHTS}
    m = {n: given["m_" + n] for n in WEIGHTS}
    v = {n: given["v_" + n] for n in WEIGHTS}
    xi, yi, _ = _place()
    chip = (2 * xi + yi).astype(jnp.int32)
    chip_sel = chip.reshape(1)
    early_kind = [n in COL_KIND for n in EARLY_WEIGHTS]
    late_kind = [n in COL_KIND for n in LATE_WEIGHTS]

    gathered, taps = _gather_weights([w[FIRST_WEIGHT][0].astype(BF16)], [False], w["ffn_conv_w"][0])
    first_full = gathered[0]
    conv_w = taps.transpose(1, 0, 2).reshape(3, D_FF)
    pending = {}
    g_started = None
    for stage, names in (("mid", MID_WEIGHTS), ("late", LATE_WEIGHTS)):
        kinds = [n in COL_KIND for n in names]
        shards = [w[n][0].astype(BF16) for n in names]
        if g_started is None:
            first_full, shards[0] = lax.optimization_barrier((first_full, shards[0]))
        else:
            shards[0] = shards[0] + g_started[0:1, 0:1].astype(BF16)
        full = [lax.empty((s.shape[0], 4 * s.shape[1]) if ck else (4 * s.shape[0], s.shape[1]), BF16)
                for s, ck in zip(shards, kinds)]
        plan = _late_gather_plan(kinds)
        send, recv, srcs, lands, g_started = _split_start(
            "gather_" + stage + "_start", shards, full, 4 * len(names), plan)
        pending[stage] = (names, plan, send, recv, srcs, lands)
    wb = {FIRST_WEIGHT: _full_from_gathered(FIRST_WEIGHT, first_full)}

    def late_weights(stage, after):
        names, plan, send, recv, srcs, lands = pending[stage]
        _, full = _split_wait("gather_" + stage + "_wait", send, recv, srcs, lands, after, plan)
        return dict(zip(names, full))

    reduce_plan = _late_reduce_plan(late_kind)
    late_reduce = {}

    def early_grads(late_g):
        grads = [_reduce_layout(n, late_g[n]) for n in LATE_WEIGHTS]
        lands = [lax.empty((3, s.shape[0], s.shape[1] // 4) if ck else (3,) + s.shape[1:], BF16)
                 for s, ck in zip(grads, late_kind)]
        late_reduce["sems"] = _split_start("reduce_late_start", grads, lands, 3 * len(LATE_WEIGHTS), reduce_plan)
        return late_reduce["sems"][4][0:1, 0:1]

    p = {n: w[n][0] for n in SMALL}
    p["ffn_conv_w"] = conv_w
    for n in ("norm_mix", "fox_q_norm", "fox_k_norm", "fox_f_bias", "s5_b_glu", "out_norm_fox", "out_norm_s5",
              "norm_cross", "norm_mem", "xq_norm", "xk_norm", "norm_ffn", "ffn_conv_b"):
        p[n] = p[n].reshape(1, -1)
    p["norm_mix"] = p["norm_mix"] + g_started[0:1, 0:1]
    loss, grad_x, g = _local_step(x, mem, loss_target, p, wb, late_weights, early_grads)

    grads = [_reduce_layout(n, g[n].astype(BF16)) for n in EARLY_WEIGHTS]
    early_lands = [lax.empty((3, s.shape[0], s.shape[1] // 4) if ck else (3,) + s.shape[1:], BF16)
                   for s, ck in zip(grads, early_kind)]
    early_plan = _late_reduce_plan(early_kind)
    e_send, e_recv, e_srcs, e_lands, e_started = _split_start(
        "reduce_early_start", grads, early_lands, 3 * len(EARLY_WEIGHTS), early_plan)

    out_g, out_d, out_m, out_v = {}, {}, {}, {}

    def finish(names, kinds, sums, from_chips, tag):
        mine = [_chip_sum("reduce_chip_sum_" + n, chip_sel, ps, ck, fc)
                for n, ps, fc, ck in zip(names, sums, from_chips, kinds)]
        theirs = _pair_swap("reduce_pair_swap_" + tag, mine)
        for n, a, b in zip(names, mine, theirs):
            if n == "w_in":
                flip = lambda t: jnp.swapaxes(t, -1, -2)
                res = _adamw_big("adamw_" + n, flip(w[n]), flip(a), flip(b), flip(m[n]), flip(v[n]))
                out_g[n], out_d[n], out_m[n], out_v[n] = (flip(t) for t in res)
                continue
            out_g[n], out_d[n], out_m[n], out_v[n] = _adamw_big("adamw_" + n, w[n], a, b, m[n], v[n])

    r_send, r_recv, r_srcs, r_lands, _ = late_reduce["sems"]
    late_sums, late_from_chips = _split_wait("reduce_late_wait", r_send, r_recv, r_srcs, r_lands, e_started,
                                             reduce_plan)
    finish(LATE_WEIGHTS, late_kind, late_sums, late_from_chips, "late")

    small_names = list(SMALL) + ["ffn_conv_w"]
    small_vals = [g[n].reshape(w[n].shape if n != "ffn_conv_w" else (1, 3, D_FF)) for n in small_names]
    last = LATE_WEIGHTS[-1]
    loss, out_v[last] = lax.optimization_barrier((loss, out_v[last]))
    reduced = _allreduce_small(small_vals + [loss])
    loss_all = reduced[-1].reshape(())
    conv_w_grad = lax.dynamic_slice_in_dim(reduced[-2], chip * (D_FF // 4), D_FF // 4, axis=2)
    sg, sd, sm, sv = _adamw_small(
        [w[n] for n in small_names], list(reduced[:len(SMALL)]) + [conv_w_grad],
        [m[n] for n in small_names], [v[n] for n in small_names])
    out_g.update(zip(small_names, sg))
    out_d.update(zip(small_names, sd))
    out_m.update(zip(small_names, sm))
    out_v.update(zip(small_names, sv))

    early_sums, early_from_chips = _split_wait("reduce_early_wait", e_send, e_recv, e_srcs, e_lands, reduced[0],
                                               early_plan)
    finish(EARLY_WEIGHTS, early_kind, early_sums, early_from_chips, "early")

    return (loss_all, grad_x, *[out_g[n] for n in WEIGHTS], *[out_d[n] for n in WEIGHTS],
            *[out_m[n] for n in WEIGHTS], *[out_v[n] for n in WEIGHTS])
```

```python
import math

import jax
import jax.numpy as jnp
from jax import lax
from jax.experimental import pallas as pl
from jax.experimental.pallas import tpu as pltpu

F32 = jnp.float32
BF16 = jnp.bfloat16

D_MODEL = 1024
FOX_WIDTH = 512
HEAD_DIM = 64
N_FOX_HEADS = 8
S5_WIDTH = 512
S5_GROUP_CH = 16
S5_GROUPS = 32
S5_STATE = 64
S5_CH = S5_GROUPS * S5_STATE
N_X_HEADS = 4
X_HEAD_DIM = 256
N_MEM = 256
D_FF = 2816
UF_COLS = 640
EPS = 1e-6
ADAM_LR = 0.001
ADAM_B1 = 0.9
ADAM_B2 = 0.999
ADAM_EPS = 1e-08
ADAM_WD = 0.01
ADAM_STEP = 10

VMEM_LIMIT_BYTES = 56 * 1024 * 1024
MM_BLOCK_BYTES = 6 * 1024 * 1024
MM_VMEM_BYTES = 40 * 1024 * 1024
MM_TILE_MAX = 1536
MESH = pl.DeviceIdType.MESH

FIRST_WEIGHT = "w_in"
MID_WEIGHTS = ("s5_w_glu", "w_out")
EARLY_WEIGHTS = (FIRST_WEIGHT,) + MID_WEIGHTS
LATE_WEIGHTS = ("w_xq", "w_xkv", "w_xo", "w_ffn_up", "w_ffn_down")
BIG = EARLY_WEIGHTS + LATE_WEIGHTS
COL_KIND = ("w_xkv", "w_ffn_up")
SMALL = ("norm_mix", "fox_q_norm", "fox_k_norm", "fox_f_bias", "s5_a_re", "s5_a_im", "s5_log_dt",
         "s5_b_re", "s5_b_im", "s5_c_re", "s5_c_im", "s5_d", "s5_b_glu", "out_norm_fox", "out_norm_s5",
         "norm_cross", "norm_mem", "xq_norm", "xk_norm", "norm_ffn", "ffn_conv_b")
WEIGHTS = ("norm_mix", "w_in", "fox_q_norm", "fox_k_norm", "fox_f_bias", "s5_a_re", "s5_a_im", "s5_log_dt",
           "s5_b_re", "s5_b_im", "s5_c_re", "s5_c_im", "s5_d", "s5_w_glu", "s5_b_glu", "out_norm_fox",
           "out_norm_s5", "w_out", "norm_cross", "norm_mem", "w_xq", "w_xkv", "xq_norm", "xk_norm", "w_xo",
           "norm_ffn", "w_ffn_up", "ffn_conv_w", "ffn_conv_b", "w_ffn_down")


def _params(sem=None):
    return pltpu.CompilerParams(dimension_semantics=sem, vmem_limit_bytes=VMEM_LIMIT_BYTES)


def _pick(n, cands):
    for c in cands:
        if n % c == 0:
            return c
    return n


_DIMS = {"nn": (((1,), (0,)), ((), ())), "nt": (((1,), (1,)), ((), ())), "tn": (((0,), (0,)), ((), ()))}


def _mm(a, b, mode, name, out_dtype=F32, res=None):
    if mode == "nn":
        (m, k), (k2, n) = a.shape, b.shape
    elif mode == "nt":
        (m, k), (n, k2) = a.shape, b.shape
    else:
        (k, m), (k2, n) = a.shape, b.shape
    assert k == k2, (name, a.shape, b.shape)

    has_res = res is not None
    a_size, b_size = a.dtype.itemsize, b.dtype.itemsize
    o_size = jnp.dtype(out_dtype).itemsize + (res.dtype.itemsize if has_res else 0)

    def tiles(dim):
        return [c for c in range(MM_TILE_MAX, 0, -128) if dim % c == 0] or [dim]

    best = None
    for tm in tiles(m):
        for tn in tiles(n):
            a_blk, b_blk = tm * k * a_size, tn * k * b_size
            if max(a_blk, b_blk) > MM_BLOCK_BYTES or 2 * (a_blk + b_blk + tm * tn * o_size) > MM_VMEM_BYTES:
                continue
            for rows_outer in (True, False):
                moved = (m * k * a_size + (m // tm) * n * k * b_size) if rows_outer else \
                        (n * k * b_size + (n // tn) * m * k * a_size)
                key = (moved, -(tm * tn))
                if best is None or key < best[0]:
                    best = (key, tm, tn, rows_outer)
    assert best is not None, (name, a.shape, b.shape)
    _, tm, tn, rows_outer = best
    ij = (lambda g0, g1: (g0, g1)) if rows_outer else (lambda g0, g1: (g1, g0))
    if mode == "tn":
        a_spec = pl.BlockSpec((k, tm), lambda g0, g1: (0, ij(g0, g1)[0]))
    else:
        a_spec = pl.BlockSpec((tm, k), lambda g0, g1: (ij(g0, g1)[0], 0))
    if mode == "nt":
        b_spec = pl.BlockSpec((tn, k), lambda g0, g1: (ij(g0, g1)[1], 0))
    else:
        b_spec = pl.BlockSpec((k, tn), lambda g0, g1: (0, ij(g0, g1)[1]))
    o_spec = pl.BlockSpec((tm, tn), lambda g0, g1: ij(g0, g1))
    grid = (m // tm, n // tn) if rows_outer else (n // tn, m // tm)
    dims = _DIMS[mode]

    def body(*refs):
        a_ref, b_ref = refs[0], refs[1]
        o_ref = refs[-1]
        acc = lax.dot_general(a_ref[...].astype(BF16), b_ref[...].astype(BF16), dims, preferred_element_type=F32)
        if has_res:
            acc = acc + refs[2][...].astype(F32)
        o_ref[...] = acc.astype(o_ref.dtype)

    return pl.pallas_call(
        body, name=name, grid=grid,
        in_specs=[a_spec, b_spec] + ([o_spec] if has_res else []),
        out_specs=o_spec, out_shape=jax.ShapeDtypeStruct((m, n), out_dtype),
        compiler_params=_params(("parallel", "parallel")),
    )(*((a, b, res) if has_res else (a, b)))


def _row_spec(tm, bc, off, step):
    return pl.BlockSpec((tm, bc), lambda i, h: (i, off + step * h))


ROW_TILE_ELEMS = 512 * 1024


def _row_tile(t, rows):
    widest = max(bc for (_, bc, _, _) in rows)
    return _pick(t, (min(t, ROW_TILE_ELEMS // widest), 512, 256, 128, 64, 8))


def _rowwise(fn, rows, pars, outs, name, heads=1):
    t = rows[0][0].shape[0]
    tm = _row_tile(t, rows)
    nr, npar = len(rows), len(pars)

    def body(*refs):
        vals = [r[...].astype(F32) for r in refs[:nr + npar]]
        res = fn(*vals)
        if not isinstance(res, (tuple, list)):
            res = (res,)
        for o_ref, v in zip(refs[nr + npar:], res):
            o_ref[...] = v.astype(o_ref.dtype)

    in_specs = [_row_spec(tm, bc, off, st) for (_, bc, off, st) in rows]
    in_specs += [pl.BlockSpec(p.shape, lambda i, h: (0, 0)) for p in pars]
    out_specs = [_row_spec(tm, bc, 0, st) for (_, bc, st, _) in outs]
    out_shape = [jax.ShapeDtypeStruct((t, c), dt) for (c, _, _, dt) in outs]
    res = pl.pallas_call(
        body, name=name, grid=(t // tm, heads), in_specs=in_specs, out_specs=out_specs, out_shape=out_shape,
        compiler_params=_params(("parallel", "parallel")),
    )(*[r[0] for r in rows], *pars)
    return res[0] if len(res) == 1 else res


def _rowwise_vjp(fn, rows, pars, cts, name, heads=1, adds=None, row_dtypes=None):
    t = rows[0][0].shape[0]
    tm = _row_tile(t, rows)
    nr, npar, nct = len(rows), len(pars), len(cts)
    adds = adds or [None] * nr
    add_list = [a for a in adds if a is not None]
    row_dtypes = row_dtypes or [F32] * nr

    def body(*refs):
        i, h = pl.program_id(0), pl.program_id(1)
        p = 0
        row_v = [r[...].astype(F32) for r in refs[p:p + nr]]; p += nr
        par_v = [r[...].astype(F32) for r in refs[p:p + npar]]; p += npar
        ct_v = [r[...].astype(F32) for r in refs[p:p + nct]]; p += nct
        add_refs = refs[p:p + len(add_list)]; p += len(add_list)
        drow_refs = refs[p:p + nr]; p += nr
        dpar_refs = refs[p:p + npar]

        def wrapped(*a):
            r = fn(*a)
            return tuple(r) if isinstance(r, (tuple, list)) else (r,)

        _, pull = jax.vjp(wrapped, *row_v, *par_v)
        grads = pull(tuple(ct_v))
        ai = 0
        for k in range(nr):
            g = grads[k]
            if adds[k] is not None:
                g = g + add_refs[ai][...].astype(F32)
                ai += 1
            drow_refs[k][...] = g.astype(drow_refs[k].dtype)

        @pl.when((i == 0) & (h == 0))
        def _():
            for r in dpar_refs:
                r[...] = jnp.zeros(r.shape, r.dtype)

        for k in range(npar):
            dpar_refs[k][...] += grads[nr + k]

    in_specs = [_row_spec(tm, bc, off, st) for (_, bc, off, st) in rows]
    in_specs += [pl.BlockSpec(q.shape, lambda i, h: (0, 0)) for q in pars]
    in_specs += [_row_spec(tm, bc, off, st) for (_, bc, off, st) in cts]
    in_specs += [_row_spec(tm, bc, off, st) for (_, bc, off, st) in add_list]
    out_specs = [_row_spec(tm, bc, 0, st) for (_, bc, _, st) in rows]
    out_specs += [pl.BlockSpec(q.shape, lambda i, h: (0, 0)) for q in pars]
    out_shape = [jax.ShapeDtypeStruct((t, bc * (heads if st else 1)), dt) for (_, bc, _, st), dt in zip(rows, row_dtypes)]
    out_shape += [jax.ShapeDtypeStruct(q.shape, F32) for q in pars]
    res = pl.pallas_call(
        body, name=name, grid=(t // tm, heads), in_specs=in_specs, out_specs=out_specs, out_shape=out_shape,
        compiler_params=_params(("arbitrary", "arbitrary")),
    )(*[r[0] for r in rows], *pars, *[c[0] for c in cts], *[a[0] for a in add_list])
    return list(res[:nr]), list(res[nr:])


def _rms(x, g):
    return x * lax.rsqrt(jnp.mean(x * x, axis=-1, keepdims=True) + EPS) * g


def _rms_pair(x, g):
    left = lax.broadcasted_iota(jnp.int32, x.shape, 1) < HEAD_DIM
    x2 = x * x
    ms_a = jnp.sum(jnp.where(left, x2, 0.0), axis=-1, keepdims=True) * (1.0 / HEAD_DIM)
    ms_b = jnp.sum(jnp.where(left, 0.0, x2), axis=-1, keepdims=True) * (1.0 / HEAD_DIM)
    return x * lax.rsqrt(jnp.where(left, ms_a, ms_b) + EPS) * g


def _gelu(x):
    return 0.5 * x * (1.0 + jnp.tanh(math.sqrt(2.0 / math.pi) * (x + 0.044715 * (x * x * x))))


def _s5_act(ys, u, d):
    return _gelu(ys + d * u)


def _s5_gate(yg, z, b, g):
    return _rms(yg * jax.nn.sigmoid(z + b), g)


def _lane_cumsum(x, reverse):
    n = x.shape[-1]
    lane = lax.broadcasted_iota(jnp.int32, x.shape, 1)
    k = 1
    while k < n:
        if reverse:
            x = x + jnp.where(lane < n - k, pltpu.roll(x, n - k, 1), 0.0)
        else:
            x = x + jnp.where(lane >= k, pltpu.roll(x, k, 1), 0.0)
        k *= 2
    return x


def _log_sigmoid(z):
    return jnp.minimum(z, 0.0) - jnp.log(1.0 + jnp.exp(-jnp.abs(z)))


def _forget_fwd(f, bias):
    def body(f_ref, b_ref, c_ref):
        c_ref[...] = _lane_cumsum(_log_sigmoid(f_ref[...] + b_ref[...]), False)

    return pl.pallas_call(body, name="forget_fwd", out_shape=jax.ShapeDtypeStruct(f.shape, F32),
                          compiler_params=_params())(f, bias)


def _forget_bwd(f, bias, dc):
    def body(f_ref, b_ref, dc_ref, df_ref, db_ref):
        dlog = _lane_cumsum(dc_ref[...], True)
        df = dlog * jax.nn.sigmoid(-(f_ref[...] + b_ref[...]))
        df_ref[...] = df
        db_ref[...] = jnp.sum(df, axis=1, keepdims=True)

    return pl.pallas_call(body, name="forget_bwd",
                          out_shape=(jax.ShapeDtypeStruct(f.shape, F32), jax.ShapeDtypeStruct(bias.shape, F32)),
                          compiler_params=_params())(f, bias, dc)


FOX_BLOCK = 1024
FOX_KEYS = 1024
FOX_BWD_BLOCK = 512
_NT = _DIMS["nt"]
_TN = _DIMS["tn"]


N_PAIRS = N_FOX_HEADS // 2
V_BLOCK0 = 2 * N_PAIRS


def _left_lanes(shape):
    return lax.broadcasted_iota(jnp.int32, shape, 1) < HEAD_DIM


def _top_rows(shape):
    return lax.broadcasted_iota(jnp.int32, shape, 0) < HEAD_DIM


def _wide(c_tile, n):
    return c_tile if n == 128 else jnp.concatenate([c_tile] * (n // 128), axis=1)


def _fox_fwd(qn, kn, qkv, c_wide, seqs):
    t = qn.shape[0]
    l = t // seqs
    tb = min(FOX_BLOCK, l)
    tk = min(FOX_KEYS, tb)
    ratio = tb // tk
    nb = l // tb
    scale = HEAD_DIM ** -0.5

    def body(q_ref, k_ref, v_ref, ca_ref, cb_ref, o_ref, lse_ref, vt_ref):
        i = pl.program_id(2)
        top = _top_rows((128, tb))

        @pl.when(i == 0)
        def _():
            vt_ref[...] = v_ref[...].T.astype(BF16)

        qt = (q_ref[...].astype(F32) * scale).T.astype(BF16)
        zero = jnp.zeros_like(qt)
        qts = (jnp.where(top, qt, zero), jnp.where(top, zero, qt))
        top_k = _top_rows((128, tk))
        zero_k = jnp.zeros((128, tk), BF16)
        key_pos = lax.broadcasted_iota(jnp.int32, (tk, tb), 0)
        query_pos = lax.broadcasted_iota(jnp.int32, (tk, tb), 1)
        c_refs = (ca_ref, cb_ref)

        def scores(j):
            off = pl.multiple_of(j * tk, tk)
            k2 = k_ref[pl.ds(off, tk), :]
            return tuple(jnp.dot(k2, qts[h], preferred_element_type=F32) - _wide(c_refs[h][pl.ds(off, tk), :], tb)
                         for h in (0, 1))

        def values_times(ps, j):
            vt = vt_ref[:, pl.ds(pl.multiple_of(j * tk, tk), tk)]
            return (jnp.dot(jnp.where(top_k, vt, zero_k), ps[0], preferred_element_type=F32)
                    + jnp.dot(jnp.where(top_k, zero_k, vt), ps[1], preferred_element_type=F32))

        def softmax_step(sts, stats, first_key):
            ps, new, alphas = [], [], []
            for st, (m, s_sum) in zip(sts, stats):
                if first_key is not None:
                    st = jnp.where(key_pos + first_key <= query_pos, st, -jnp.inf)
                m_new = jnp.maximum(m, jnp.max(st, axis=0, keepdims=True))
                alpha = jnp.exp(m - m_new)
                p = jnp.exp(st - m_new)
                new.append((m_new, alpha * s_sum + jnp.sum(p, axis=0, keepdims=True)))
                alphas.append(alpha)
                ps.append(p.astype(BF16))
            return tuple(ps), tuple(new), jnp.where(top, alphas[0], alphas[1])

        def tile(j, carry, first_key):
            stats, acc = carry
            ps, stats, alpha = softmax_step(scores(j), stats, first_key)
            return stats, alpha * acc + values_times(ps, j)

        stat = (jnp.full((1, tb), -jnp.inf, F32), jnp.zeros((1, tb), F32))
        below = i * ratio
        carry = lax.fori_loop(0, below, lambda j, c: tile(j, c, None), ((stat, stat), jnp.zeros((128, tb), F32)))
        for r in range(ratio):
            carry = tile(below + r, carry, r * tk)
        ((ma, sa), (mb, sb)), acc = carry
        o_ref[...] = (acc / jnp.where(top, sa, sb)).T
        lse_ref[0:1, :] = ma + jnp.log(sa)
        lse_ref[1:2, :] = mb + jnp.log(sb)

    qblk = pl.BlockSpec((tb, 128), lambda b, hp, i: (b * nb + i, hp))
    return pl.pallas_call(
        body, name="fox_fwd", grid=(seqs, N_PAIRS, nb),
        in_specs=[qblk, pl.BlockSpec((l, 128), lambda b, hp, i: (b, hp)),
                  pl.BlockSpec((l, 128), lambda b, hp, i: (b, V_BLOCK0 + hp)),
                  pl.BlockSpec((None, l, 128), lambda b, hp, i: (b * N_FOX_HEADS + 2 * hp, 0, 0)),
                  pl.BlockSpec((None, l, 128), lambda b, hp, i: (b * N_FOX_HEADS + 2 * hp + 1, 0, 0))],
        out_specs=[qblk, pl.BlockSpec((None, 2, tb), lambda b, hp, i: (b * N_PAIRS + hp, 0, i))],
        out_shape=[jax.ShapeDtypeStruct((t, FOX_WIDTH), F32), jax.ShapeDtypeStruct((seqs * N_PAIRS, 2, l), F32)],
        scratch_shapes=[pltpu.VMEM((128, l), BF16)],
        compiler_params=_params(("parallel", "parallel", "arbitrary")),
    )(qn, kn, qkv, c_wide, c_wide)


def _fox_bwd(qn, kn, qkv, c_wide, o, do, lse, seqs):
    t = qn.shape[0]
    l = t // seqs
    tb = min(FOX_BWD_BLOCK, l)
    nb = l // tb
    scale = HEAD_DIM ** -0.5
    one_at = (HEAD_DIM, 0)

    def body(q_ref, k_ref, v_ref, ca_ref, cb_ref, o_ref, do_ref, lse_ref, dq_ref, dk_ref, dv_ref, dc_ref, dcq_ref,
             qt_ref, kt_ref, dot_ref, delta_ref, dqa_ref, dqb_ref):
        top_l = _top_rows((128, l))
        top = _top_rows((128, tb))
        left = _left_lanes((tb, 128))
        row_id = lax.broadcasted_iota(jnp.int32, (128, tb), 0)
        lane_id = lax.broadcasted_iota(jnp.int32, (tb, 128), 1)
        zero_t = jnp.zeros((128, tb), BF16)
        zero_l = jnp.zeros((tb, 128), BF16)
        rows = lambda a: (jnp.where(top, a, zero_t), jnp.where(top, zero_t, a))
        lanes = lambda a: (jnp.where(left, a, zero_l), jnp.where(left, zero_l, a))
        with_one_row = lambda pair: tuple(jnp.where(row_id == one_at[h], 1.0, pair[h]).astype(BF16) for h in (0, 1))
        with_one_lane = lambda pair: tuple(jnp.where(lane_id == one_at[h], 1.0, pair[h]).astype(BF16) for h in (0, 1))
        causal = lax.broadcasted_iota(jnp.int32, (tb, tb), 0) <= lax.broadcasted_iota(jnp.int32, (tb, tb), 1)
        c_refs = (ca_ref, cb_ref)
        dq_refs = (dqa_ref, dqb_ref)

        qt_ref[...] = (q_ref[...].astype(F32) * scale).T.astype(BF16)
        kt_ref[...] = k_ref[...].astype(F32).T.astype(BF16)
        do_t = do_ref[...].T
        dot_ref[...] = do_t.astype(BF16)
        prod_t = do_t * o_ref[...].T
        delta_ref[0:1, :] = jnp.sum(jnp.where(top_l, prod_t, 0.0), axis=0, keepdims=True)
        delta_ref[1:2, :] = jnp.sum(jnp.where(top_l, 0.0, prod_t), axis=0, keepdims=True)
        dqa_ref[...] = jnp.zeros(dqa_ref.shape, F32)
        dqb_ref[...] = jnp.zeros(dqb_ref.shape, F32)

        def kv_block(j, _):
            koff = pl.multiple_of(j * tb, tb)
            k2 = k_ref[pl.ds(koff, tb), :]
            v2 = v_ref[pl.ds(koff, tb), :].astype(BF16)
            kts = with_one_row(rows(kt_ref[:, pl.ds(koff, tb)]))
            cw = tuple(_wide(c_refs[h][pl.ds(koff, tb), :], tb) for h in (0, 1))

            def q_block(i, carry, masked):
                dks, dv = list(carry[:2]), carry[2]
                qoff = pl.multiple_of(i * tb, tb)
                qs = lanes((q_ref[pl.ds(qoff, tb), :].astype(F32) * scale).astype(BF16))
                qs_one = with_one_lane(qs)
                dos = lanes(do_ref[pl.ds(qoff, tb), :].astype(BF16))
                qts = rows(qt_ref[:, pl.ds(qoff, tb)])
                dots = rows(dot_ref[:, pl.ds(qoff, tb)])
                for h in (0, 1):
                    st = jnp.dot(k2, qts[h], preferred_element_type=F32) - cw[h]
                    p = jnp.exp(st - lse_ref[h:h + 1, pl.ds(qoff, tb)])
                    if masked:
                        p = jnp.where(causal, p, 0.0)
                    dp = jnp.dot(v2, dots[h], preferred_element_type=F32)
                    dsb = (p * (dp - delta_ref[h:h + 1, pl.ds(qoff, tb)])).astype(BF16)
                    dv = dv + jnp.dot(p.astype(BF16), dos[h], preferred_element_type=F32)
                    dks[h] = dks[h] + jnp.dot(dsb, qs_one[h], preferred_element_type=F32)
                    dq_refs[h][:, pl.ds(qoff, tb)] += jnp.dot(kts[h], dsb, preferred_element_type=F32)
                return dks[0], dks[1], dv

            z = jnp.zeros((tb, 128), F32)
            carry = q_block(j, (z, z, z), True)
            rest = nb - 1 - j
            carry = lax.fori_loop(
                0, rest // 2, lambda n, c: q_block(j + 2 + 2 * n, q_block(j + 1 + 2 * n, c, False), False), carry)
            dka, dkb, dv = lax.cond(rest % 2 == 1, lambda c: q_block(nb - 1, c, False), lambda c: c, carry)
            dk_ref[pl.ds(koff, tb), :] = jnp.where(left, dka, dkb)
            dv_ref[pl.ds(koff, tb), :] = dv
            dc_ref[0:1, pl.ds(koff, tb)] = -dka.T[one_at[0]:one_at[0] + 1, :]
            dc_ref[1:2, pl.ds(koff, tb)] = -dkb.T[one_at[1]:one_at[1] + 1, :]
            return 0

        lax.fori_loop(0, nb, kv_block, 0)
        dq_ref[...] = (jnp.where(top_l, dqa_ref[...], dqb_ref[...]) * scale).T
        dcq_ref[0:1, :] = dqa_ref[one_at[0]:one_at[0] + 1, :]
        dcq_ref[1:2, :] = dqb_ref[one_at[1]:one_at[1] + 1, :]

    blk = pl.BlockSpec((l, 128), lambda b, hp: (b, hp))
    cspec = lambda k: pl.BlockSpec((None, l, 128), lambda b, hp: (b * N_FOX_HEADS + 2 * hp + k, 0, 0))
    rows2 = pl.BlockSpec((None, 2, l), lambda b, hp: (b * N_PAIRS + hp, 0, 0))
    wide = jax.ShapeDtypeStruct((t, FOX_WIDTH), F32)
    pair_rows = jax.ShapeDtypeStruct((seqs * N_PAIRS, 2, l), F32)
    return pl.pallas_call(
        body, name="fox_bwd", grid=(seqs, N_PAIRS),
        in_specs=[blk, blk, pl.BlockSpec((l, 128), lambda b, hp: (b, V_BLOCK0 + hp)), cspec(0), cspec(1), blk, blk, rows2],
        out_specs=[blk, blk, blk, rows2, rows2],
        out_shape=[wide, wide, wide, pair_rows, pair_rows],
        scratch_shapes=[pltpu.VMEM((128, l), BF16), pltpu.VMEM((128, l), BF16), pltpu.VMEM((128, l), BF16),
                        pltpu.VMEM((2, l), F32), pltpu.VMEM((128, l), F32), pltpu.VMEM((128, l), F32)],
        compiler_params=_params(("parallel", "parallel")),
    )(qn, kn, qkv, c_wide, c_wide, o, do, lse)


SCAN_ROWS = 512
SCAN_COLS = 1024


S5_IN = 128
S5_ST = 512
SCAN_CHUNKS = SCAN_COLS // S5_ST
SCAN_SEGS = 8
LANES = 128


def _cmul(ar, ai, br, bi):
    return ar * br - ai * bi, ar * bi + ai * br


def _powers_into(pw_r, pw_i, a_r, a_i, seg):
    pw_r[0:1, :] = a_r
    pw_i[0:1, :] = a_i
    for k in range(1, seg):
        pr, pi = _cmul(pw_r[k - 1:k, :], pw_i[k - 1:k, :], a_r, a_i)
        pw_r[k:k + 1, :] = pr
        pw_i[k:k + 1, :] = pi


def _interleave(dst, src, seg):
    for h in range(src.shape[0]):
        for j in range(seg):
            dst[h, j * SCAN_SEGS:(j + 1) * SCAN_SEGS, :] = src[h, pl.ds(j, SCAN_SEGS, stride=seg), :]


def _deinterleave(dst, src, seg):
    for h in range(src.shape[0]):
        for j in range(seg):
            dst[h, pl.ds(j, SCAN_SEGS, stride=seg), :] = src[h, j * SCAN_SEGS:(j + 1) * SCAN_SEGS, :]


def _interleaved(ref, tmp_a, tmp_b, seg):
    n = ref.shape[1] // LANES
    for h in range(n):
        tmp_a[h] = ref[:, h * LANES:(h + 1) * LANES].astype(F32)
    _interleave(tmp_b, tmp_a, seg)
    return jnp.concatenate([tmp_b[h] for h in range(n)], axis=1)


def _store_deinterleaved(ref, val, tmp_a, tmp_b, seg):
    n = ref.shape[1] // LANES
    for h in range(n):
        tmp_a[h] = val[:, h * LANES:(h + 1) * LANES]
    _deinterleave(tmp_b, tmp_a, seg)
    for h in range(n):
        ref[:, h * LANES:(h + 1) * LANES] = tmp_b[h]


def _segment_scan(b_r, b_i, x_r, x_i, pw_r, pw_i, car_r, car_i, seg, sign, reverse, visit=None):
    nc = b_r.shape[0]
    sub = lax.broadcasted_iota(jnp.int32, (SCAN_SEGS, LANES), 0)
    lanes = lambda c: slice(c * LANES, (c + 1) * LANES)
    rows = lambda j: pl.ds(pl.multiple_of(((seg - 1 - j) if reverse else j) * SCAN_SEGS, SCAN_SEGS), SCAN_SEGS)
    a1 = [(pw_r[0:1, lanes(c)], sign * pw_i[0:1, lanes(c)]) for c in range(nc)]

    def local(j, xs):
        out = []
        for c in range(nc):
            xr, xi = xs[2 * c], xs[2 * c + 1]
            nr = a1[c][0] * xr - a1[c][1] * xi + b_r[c, rows(j), :]
            ni = a1[c][0] * xi + a1[c][1] * xr + b_i[c, rows(j), :]
            x_r[c, rows(j), :] = nr
            x_i[c, rows(j), :] = ni
            out += [nr, ni]
        return tuple(out)

    zero = jnp.zeros((SCAN_SEGS, LANES), F32)
    ends = lax.fori_loop(0, seg, local, (zero,) * (2 * nc))

    if reverse:
        first = sub == SCAN_SEGS - 1
        neighbour = lambda v: pltpu.roll(v, SCAN_SEGS - 1, 0)
        shift = lambda v, d: jnp.where(sub < SCAN_SEGS - d, pltpu.roll(v, SCAN_SEGS - d, 0), 0.0)
    else:
        first = sub == 0
        neighbour = lambda v: pltpu.roll(v, 1, 0)
        shift = lambda v, d: jnp.where(sub >= d, pltpu.roll(v, d, 0), 0.0)
    last = 0 if reverse else SCAN_SEGS - 1
    entries = []
    for c in range(nc):
        er, ei = ends[2 * c], ends[2 * c + 1]
        pr, pi = pw_r[seg - 1:seg, lanes(c)], sign * pw_i[seg - 1:seg, lanes(c)]
        yr = jnp.where(first, car_r[:, lanes(c)], neighbour(er))
        yi = jnp.where(first, car_i[:, lanes(c)], neighbour(ei))
        qr, qi = pr, pi
        for d in (1, 2, 4):
            mr, mi = _cmul(qr, qi, shift(yr, d), shift(yi, d))
            yr, yi = yr + mr, yi + mi
            qr, qi = _cmul(qr, qi, qr, qi)
        lr, li = _cmul(pr, pi, yr, yi)
        car_r[:, lanes(c)] = (er + lr)[last:last + 1, :]
        car_i[:, lanes(c)] = (ei + li)[last:last + 1, :]
        entries += [yr, yi]

    def correct(j, prev):
        out = []
        row_r, row_i = pw_r[pl.ds(j, 1), :], sign * pw_i[pl.ds(j, 1), :]
        for c in range(nc):
            mr, mi = _cmul(row_r[:, lanes(c)], row_i[:, lanes(c)], entries[2 * c], entries[2 * c + 1])
            nr = x_r[c, rows(j), :] + mr
            ni = x_i[c, rows(j), :] + mi
            x_r[c, rows(j), :] = nr
            x_i[c, rows(j), :] = ni
            if visit is not None:
                visit(c, rows(j), prev[2 * c], prev[2 * c + 1])
            out += [nr, ni]
        return tuple(out)

    lax.fori_loop(0, seg, correct, tuple(entries))


def _s5_fwd(uf, bbr, bbi, cr, ci, ar, ai, seqs):
    t = uf.shape[0]
    l = t // seqs
    tl = min(SCAN_ROWS, l)
    nl = l // tl
    seg = tl // SCAN_SEGS
    cb, nq = SCAN_COLS, SCAN_CHUNKS
    nc = cb // LANES
    per = S5_ST // LANES

    def body(u_ref, bbr_ref, bbi_ref, cr_ref, ci_ref, ar_ref, ai_ref, x_r, x_i, ys_ref,
             car_r, car_i, pw_r, pw_i, b_r, b_i, tmp_a, tmp_b):
        @pl.when(pl.program_id(2) == 0)
        def _():
            car_r[...] = jnp.zeros(car_r.shape, F32)
            car_i[...] = jnp.zeros(car_i.shape, F32)
            _powers_into(pw_r, pw_i, ar_ref[...], ai_ref[...], seg)

        u = _interleaved(u_ref, tmp_a, tmp_b, seg).astype(BF16)
        for q in range(nq):
            uq = u[:, q * S5_IN:(q + 1) * S5_IN]
            br = jnp.dot(uq, bbr_ref[q], preferred_element_type=F32)
            bi = jnp.dot(uq, bbi_ref[q], preferred_element_type=F32)
            for s in range(per):
                b_r[q * per + s] = br[:, s * LANES:(s + 1) * LANES]
                b_i[q * per + s] = bi[:, s * LANES:(s + 1) * LANES]
        _segment_scan(b_r, b_i, x_r, x_i, pw_r, pw_i, car_r, car_i, seg, 1.0, False)
        wide = lambda buf, q: jnp.concatenate([buf[q * per + s] for s in range(per)], axis=1).astype(BF16)
        ys = [jnp.dot(wide(x_r, q), cr_ref[q], preferred_element_type=F32)
              + jnp.dot(wide(x_i, q), ci_ref[q], preferred_element_type=F32) for q in range(nq)]
        _store_deinterleaved(ys_ref, jnp.concatenate(ys, axis=1), tmp_a, tmp_b, seg)

    rows = lambda w: pl.BlockSpec((tl, w), lambda s, j, r: (s * nl + r, j))
    state = pl.BlockSpec((nc, tl, LANES), lambda s, j, r: (j, s * nl + r, 0))
    chunk = lambda a: pl.BlockSpec((nq,) + a.shape[1:], lambda s, j, r: (j, 0, 0))
    par = pl.BlockSpec((1, cb), lambda s, j, r: (0, j))
    return pl.pallas_call(
        body, name="s5_fwd", grid=(seqs, S5_CH // cb, nl),
        in_specs=[rows(nq * S5_IN), chunk(bbr), chunk(bbi), chunk(cr), chunk(ci), par, par],
        out_specs=[state, state, rows(nq * S5_IN)],
        out_shape=[jax.ShapeDtypeStruct((S5_CH // LANES, t, LANES), F32)] * 2
        + [jax.ShapeDtypeStruct((t, S5_WIDTH), F32)],
        scratch_shapes=[pltpu.VMEM((1, cb), F32), pltpu.VMEM((1, cb), F32), pltpu.VMEM((seg, cb), F32),
                        pltpu.VMEM((seg, cb), F32)] + [pltpu.VMEM((nc, tl, LANES), F32)] * 2
        + [pltpu.VMEM((nq * S5_IN // LANES, tl, LANES), F32)] * 2,
        compiler_params=_params(("parallel", "parallel", "arbitrary")),
    )(uf, bbr, bbi, cr, ci, ar, ai)


def _s5_bwd(dys, uf, xr, xi, bbr, bbi, cr, ci, ar, ai, seqs):
    t = dys.shape[0]
    l = t // seqs
    tl = min(SCAN_ROWS, l)
    nl = l // tl
    seg = tl // SCAN_SEGS
    cb, nq = SCAN_COLS, SCAN_CHUNKS
    nc = cb // LANES
    per = S5_ST // LANES

    def body(dy_ref, u_ref, x_r, x_i, bbr_ref, bbi_ref, cr_ref, ci_ref, ar_ref, ai_ref,
             du_ref, dbbr_ref, dbbi_ref, dcr_ref, dci_ref, dar_ref, dai_ref,
             car_r, car_i, pw_r, pw_i, g_r, g_i, lam_r, lam_i, acc_r, acc_i, tmp_a, tmp_b):
        @pl.when(pl.program_id(2) == 0)
        def _():
            car_r[...] = jnp.zeros(car_r.shape, F32)
            car_i[...] = jnp.zeros(car_i.shape, F32)
            _powers_into(pw_r, pw_i, ar_ref[...], ai_ref[...], seg)
            for acc_ref in (dbbr_ref, dbbi_ref, dcr_ref, dci_ref, dar_ref, dai_ref):
                acc_ref[...] = jnp.zeros(acc_ref.shape, F32)

        dy = _interleaved(dy_ref, tmp_a, tmp_b, seg).astype(BF16)
        for q in range(nq):
            dyq = dy[:, q * S5_IN:(q + 1) * S5_IN]
            gr = lax.dot_general(dyq, cr_ref[q], _NT, preferred_element_type=F32)
            gi = lax.dot_general(dyq, ci_ref[q], _NT, preferred_element_type=F32)
            for s in range(per):
                g_r[q * per + s] = gr[:, s * LANES:(s + 1) * LANES]
                g_i[q * per + s] = gi[:, s * LANES:(s + 1) * LANES]
        acc_r[...] = jnp.zeros(acc_r.shape, F32)
        acc_i[...] = jnp.zeros(acc_i.shape, F32)

        def visit(c, rws, lr, li):
            xr_t, xi_t = x_r[c, rws, :], x_i[c, rws, :]
            acc_r[c] += lr * xr_t + li * xi_t
            acc_i[c] += li * xr_t - lr * xi_t

        _segment_scan(g_r, g_i, lam_r, lam_i, pw_r, pw_i, car_r, car_i, seg, -1.0, True, visit)
        for c in range(nc):
            dar_ref[:, c * LANES:(c + 1) * LANES] += jnp.sum(acc_r[c], axis=0, keepdims=True)
            dai_ref[:, c * LANES:(c + 1) * LANES] += jnp.sum(acc_i[c], axis=0, keepdims=True)
        u = _interleaved(u_ref, tmp_a, tmp_b, seg).astype(BF16)
        wide = lambda buf, q: jnp.concatenate([buf[q * per + s] for s in range(per)], axis=1).astype(BF16)
        du = []
        for q in range(nq):
            io = slice(q * S5_IN, (q + 1) * S5_IN)
            lq_r, lq_i = wide(lam_r, q), wide(lam_i, q)
            du.append(lax.dot_general(lq_r, bbr_ref[q], _NT, preferred_element_type=F32)
                      + lax.dot_general(lq_i, bbi_ref[q], _NT, preferred_element_type=F32))
            dbbr_ref[q] += lax.dot_general(u[:, io], lq_r, _TN, preferred_element_type=F32)
            dbbi_ref[q] += lax.dot_general(u[:, io], lq_i, _TN, preferred_element_type=F32)
            dcr_ref[q] += lax.dot_general(wide(x_r, q), dy[:, io], _TN, preferred_element_type=F32)
            dci_ref[q] += lax.dot_general(wide(x_i, q), dy[:, io], _TN, preferred_element_type=F32)
        _store_deinterleaved(du_ref, jnp.concatenate(du, axis=1), tmp_a, tmp_b, seg)

    rows = lambda w: pl.BlockSpec((tl, w), lambda s, j, r: (s * nl + nl - 1 - r, j))
    state = pl.BlockSpec((nc, tl, LANES), lambda s, j, r: (j, s * nl + nl - 1 - r, 0))
    chunk = lambda a: pl.BlockSpec((nq,) + a.shape[1:], lambda s, j, r: (j, 0, 0))
    acc = lambda a: pl.BlockSpec((None, nq) + a.shape[1:], lambda s, j, r: (s, j, 0, 0))
    par = pl.BlockSpec((1, cb), lambda s, j, r: (0, j))
    par_acc = pl.BlockSpec((None, 1, cb), lambda s, j, r: (s, 0, j))
    per_seq = lambda a: jax.ShapeDtypeStruct((seqs,) + a.shape, F32)
    return pl.pallas_call(
        body, name="s5_bwd", grid=(seqs, S5_CH // cb, nl),
        in_specs=[rows(nq * S5_IN), rows(nq * S5_IN), state, state, chunk(bbr), chunk(bbi), chunk(cr), chunk(ci),
                  par, par],
        out_specs=[rows(nq * S5_IN), acc(bbr), acc(bbi), acc(cr), acc(ci), par_acc, par_acc],
        out_shape=[jax.ShapeDtypeStruct((t, S5_WIDTH), F32), per_seq(bbr), per_seq(bbi), per_seq(cr), per_seq(ci),
                   jax.ShapeDtypeStruct((seqs, 1, S5_CH), F32), jax.ShapeDtypeStruct((seqs, 1, S5_CH), F32)],
        scratch_shapes=[pltpu.VMEM((1, cb), F32), pltpu.VMEM((1, cb), F32), pltpu.VMEM((seg, cb), F32),
                        pltpu.VMEM((seg, cb), F32)] + [pltpu.VMEM((nc, tl, LANES), F32)] * 4
        + [pltpu.VMEM((nc, SCAN_SEGS, LANES), F32)] * 2 + [pltpu.VMEM((nq * S5_IN // LANES, tl, LANES), F32)] * 2,
        compiler_params=_params(("parallel", "parallel", "arbitrary")),
    )(dys, uf, xr, xi, bbr, bbi, cr, ci, ar, ai)


XATT_BLOCK = 2048


def _xatt_probs(qv, kv):
    s = lax.dot_general(qv, kv, _NT, preferred_element_type=F32) * (X_HEAD_DIM ** -0.5)
    e = jnp.exp(s - jnp.max(s, axis=-1, keepdims=True))
    return e / jnp.sum(e, axis=-1, keepdims=True)


def _xatt_fwd(q, k, kv, seqs):
    t = q.shape[0]
    tq = min(XATT_BLOCK, t // seqs)
    nq = t // seqs // tq

    def body(q_ref, k_ref, v_ref, o_ref):
        p = _xatt_probs(q_ref[...], k_ref[...])
        o_ref[...] = jnp.dot(p.astype(BF16), v_ref[...].astype(BF16), preferred_element_type=F32).astype(o_ref.dtype)

    qs = pl.BlockSpec((tq, X_HEAD_DIM), lambda b, h, i: (b * nq + i, h))
    return pl.pallas_call(
        body, name="xatt_fwd", grid=(seqs, N_X_HEADS, nq),
        in_specs=[qs, pl.BlockSpec((N_MEM, X_HEAD_DIM), lambda b, h, i: (b, h)),
                  pl.BlockSpec((N_MEM, X_HEAD_DIM), lambda b, h, i: (b, N_X_HEADS + h))],
        out_specs=qs, out_shape=jax.ShapeDtypeStruct(q.shape, BF16),
        compiler_params=_params(("parallel", "parallel", "parallel")),
    )(q, k, kv)


def _xatt_bwd(q, k, kv, do, seqs):
    t = q.shape[0]
    tq = min(XATT_BLOCK, t // seqs)
    nq = t // seqs // tq
    scale = X_HEAD_DIM ** -0.5

    def body(q_ref, k_ref, v_ref, do_ref, dq_ref, dk_ref, dv_ref):
        @pl.when(pl.program_id(2) == 0)
        def _():
            dk_ref[...] = jnp.zeros(dk_ref.shape, F32)
            dv_ref[...] = jnp.zeros(dv_ref.shape, F32)

        qv, kk = q_ref[...], k_ref[...]
        p = _xatt_probs(qv, kk)
        dob = do_ref[...].astype(BF16)
        dp = lax.dot_general(dob, v_ref[...].astype(BF16), _NT, preferred_element_type=F32)
        ds = p * (dp - jnp.sum(dp * p, axis=-1, keepdims=True))
        dsb = ds.astype(BF16)
        dq_ref[...] = jnp.dot(dsb, kk, preferred_element_type=F32) * scale
        dk_ref[...] += lax.dot_general(dsb, qv, _TN, preferred_element_type=F32) * scale
        dv_ref[...] += lax.dot_general(p.astype(BF16), dob, _TN, preferred_element_type=F32)

    qs = pl.BlockSpec((tq, X_HEAD_DIM), lambda b, h, i: (b * nq + i, h))
    ks = pl.BlockSpec((N_MEM, X_HEAD_DIM), lambda b, h, i: (b, h))
    return pl.pallas_call(
        body, name="xatt_bwd", grid=(seqs, N_X_HEADS, nq),
        in_specs=[qs, ks, pl.BlockSpec((N_MEM, X_HEAD_DIM), lambda b, h, i: (b, N_X_HEADS + h)), qs],
        out_specs=[qs, ks, ks],
        out_shape=[jax.ShapeDtypeStruct(q.shape, F32), jax.ShapeDtypeStruct(k.shape, F32),
                   jax.ShapeDtypeStruct(k.shape, F32)],
        compiler_params=_params(("parallel", "parallel", "arbitrary")),
    )(q, k, kv, do)


CONV_COLS = 256


def _shift_down(x, k, row):
    return jnp.where(row >= k, pltpu.roll(x, k, 0), 0.0)


def _shift_up(x, k, row):
    n = x.shape[0]
    return jnp.where(row < n - k, pltpu.roll(x, n - k, 0), 0.0)


def _down_from(x, prev, k, row):
    return jnp.where(row >= k, pltpu.roll(x, k, 0), pltpu.roll(prev, k, 0))


GATE_ROWS = 512


def _ffn_up_gate(hn, w_up, w, b, seqs):
    t = hn.shape[0]
    l = t // seqs
    nc = D_FF // CONV_COLS

    rc = min(GATE_ROWS, l)

    def body(a_ref, wg_ref, wu_ref, w_ref, b_ref, g_ref, u_ref, o_ref):
        wv, bias = w_ref[...], b_ref[...]
        row = lax.broadcasted_iota(jnp.int32, (rc, CONV_COLS), 0)
        prev = jnp.zeros((rc, CONV_COLS), F32)
        for k in range(l // rc):
            rows = slice(k * rc, (k + 1) * rc)
            a = a_ref[rows, :]
            gb = jnp.dot(a, wg_ref[...], preferred_element_type=F32).astype(BF16)
            ub = jnp.dot(a, wu_ref[...], preferred_element_type=F32).astype(BF16)
            g_ref[rows, :] = gb
            u_ref[rows, :] = ub
            g = gb.astype(F32)
            pre = bias + wv[0:1, :] * _down_from(g, prev, 2, row) + wv[1:2, :] * _down_from(g, prev, 1, row) \
                + wv[2:3, :] * g
            o_ref[rows, :] = (pre * jax.nn.sigmoid(pre) * ub.astype(F32)).astype(o_ref.dtype)
            prev = g

    cols = pl.BlockSpec((l, CONV_COLS), lambda s, j: (s, j))
    half = jax.ShapeDtypeStruct((t, D_FF), BF16)
    return pl.pallas_call(
        body, name="ffn_up_gate", grid=(seqs, nc),
        in_specs=[pl.BlockSpec((l, hn.shape[1]), lambda s, j: (s, 0)),
                  pl.BlockSpec((hn.shape[1], CONV_COLS), lambda s, j: (0, j)),
                  pl.BlockSpec((hn.shape[1], CONV_COLS), lambda s, j: (0, nc + j)),
                  pl.BlockSpec((3, CONV_COLS), lambda s, j: (0, j)), pl.BlockSpec((1, CONV_COLS), lambda s, j: (0, j))],
        out_specs=[cols, cols, cols], out_shape=[half, half, half],
        compiler_params=_params(("parallel", "parallel")),
    )(hn, w_up, w_up, w, b)


def _ffn_down_dx_gate(dh, w_down, gate, up, w, b, seqs):
    t = dh.shape[0]
    l = t // seqs
    nc = D_FF // CONV_COLS
    steps = nc * seqs

    def body(dh_ref, wd_ref, g_ref, u_ref, w_ref, b_ref, dgu_ref, dw_ref, db_ref, stage, sems):
        s, j = pl.program_id(0), pl.program_id(1)
        n = s * nc + j
        slot = n % 2

        def copies(slot_, j_, s_):
            rows = pl.ds(pl.multiple_of(s_ * l, 16), l)
            return [pltpu.make_async_copy(
                stage.at[slot_, half],
                dgu_ref.at[rows, pl.ds(pl.multiple_of((half * nc + j_) * CONV_COLS, 128), CONV_COLS)],
                sems.at[slot_, half]) for half in (0, 1)]

        @pl.when(n >= 2)
        def _():
            for cp in copies(slot, j, s):
                cp.wait()

        da = lax.dot_general(dh_ref[...], wd_ref[...], _NT, preferred_element_type=F32)
        g, wv = g_ref[...].astype(F32), w_ref[...]
        row = lax.broadcasted_iota(jnp.int32, g.shape, 0)
        g1, g2 = _shift_down(g, 1, row), _shift_down(g, 2, row)
        pre = b_ref[...] + wv[0:1, :] * g2 + wv[1:2, :] * g1 + wv[2:3, :] * g
        sg = jax.nn.sigmoid(pre)
        silu = pre * sg
        stage[slot, 1] = (da * silu).astype(stage.dtype)
        dpre = da * u_ref[...].astype(F32) * (sg * (1.0 + pre * (1.0 - sg)))
        dg = wv[2:3, :] * dpre + wv[1:2, :] * _shift_up(dpre, 1, row) + wv[0:1, :] * _shift_up(dpre, 2, row)
        stage[slot, 0] = dg.astype(stage.dtype)
        for cp in copies(slot, j, s):
            cp.start()
        dw_ref[0:1, :] = jnp.sum(dpre * g2, axis=0, keepdims=True)
        dw_ref[1:2, :] = jnp.sum(dpre * g1, axis=0, keepdims=True)
        dw_ref[2:3, :] = jnp.sum(dpre * g, axis=0, keepdims=True)
        db_ref[...] = jnp.sum(dpre, axis=0, keepdims=True)

        @pl.when(n == steps - 1)
        def _():
            for cp in copies(slot, j, s) + (copies(1 - slot, j, s) if steps > 1 else []):
                cp.wait()

    cols = pl.BlockSpec((l, CONV_COLS), lambda s, j: (s, j))
    return pl.pallas_call(
        body, name="ffn_down_dx_gate", grid=(seqs, nc),
        in_specs=[pl.BlockSpec((l, dh.shape[1]), lambda s, j: (s, 0)),
                  pl.BlockSpec((CONV_COLS, dh.shape[1]), lambda s, j: (j, 0)), cols, cols,
                  pl.BlockSpec((3, CONV_COLS), lambda s, j: (0, j)), pl.BlockSpec((1, CONV_COLS), lambda s, j: (0, j))],
        out_specs=[ANY, pl.BlockSpec((None, 3, CONV_COLS), lambda s, j: (s, 0, j)),
                   pl.BlockSpec((None, 1, CONV_COLS), lambda s, j: (s, 0, j))],
        out_shape=[jax.ShapeDtypeStruct((t, 2 * D_FF), BF16), jax.ShapeDtypeStruct((seqs, 3, D_FF), F32),
                   jax.ShapeDtypeStruct((seqs, 1, D_FF), F32)],
        scratch_shapes=[pltpu.VMEM((2, 2, l, CONV_COLS), BF16), pltpu.SemaphoreType.DMA((2, 2))],
        compiler_params=_params(("arbitrary", "arbitrary")),
    )(dh, w_down, gate, up, w, b)


def _loss_head(h, target):
    t, d = h.shape
    tm = _pick(t, (256, 128, 8))

    def body(h_ref, t_ref, dh_ref, dhb_ref, loss_ref):
        @pl.when(pl.program_id(0) == 0)
        def _():
            loss_ref[...] = jnp.zeros(loss_ref.shape, F32)

        e = h_ref[...] - t_ref[...]
        dh = e * (1.0 / d)
        dh_ref[...] = dh
        dhb_ref[...] = dh.astype(BF16)
        loss_ref[...] += (0.5 / d) * jnp.sum(jnp.sum(e * e, axis=1, keepdims=True), axis=0, keepdims=True)

    blk = pl.BlockSpec((tm, d), lambda i: (i, 0))
    return pl.pallas_call(
        body, name="loss_head", grid=(t // tm,), in_specs=[blk, blk],
        out_specs=[blk, blk, pl.BlockSpec((1, 1), lambda i: (0, 0))],
        out_shape=[jax.ShapeDtypeStruct((t, d), F32), jax.ShapeDtypeStruct((t, d), BF16),
                   jax.ShapeDtypeStruct((1, 1), F32)],
        compiler_params=_params(("arbitrary",)),
    )(h, target)


def _s5_discretise(a_re, a_im, log_dt, b_re, b_im):
    dt = jnp.exp(log_dt)[:, None]
    mag = jnp.exp(a_re * dt)
    lb_r = mag * jnp.cos(a_im * dt)
    lb_i = mag * jnp.sin(a_im * dt)
    den = a_re * a_re + a_im * a_im
    nr = lb_r - 1.0
    coef_r = (nr * a_re + lb_i * a_im) / den
    coef_i = (lb_i * a_re - nr * a_im) / den
    bb_r = coef_r[:, :, None] * b_re - coef_i[:, :, None] * b_im
    bb_i = coef_r[:, :, None] * b_im + coef_i[:, :, None] * b_re
    return lb_r, lb_i, bb_r, bb_i


S5_CHUNKS = 4
S5_PER = S5_GROUPS // S5_CHUNKS


def _blockdiag_in(bb):
    eye = jnp.eye(S5_PER, dtype=bb.dtype)
    return jnp.einsum("jgpc,gh->jgchp", bb.reshape(S5_CHUNKS, S5_PER, S5_STATE, S5_GROUP_CH), eye).reshape(
        S5_CHUNKS, S5_PER * S5_GROUP_CH, S5_PER * S5_STATE)


def _blockdiag_in_grad(d):
    eye = jnp.eye(S5_PER, dtype=d.dtype)
    return jnp.einsum("jgchp,gh->jgpc", d.reshape(S5_CHUNKS, S5_PER, S5_GROUP_CH, S5_PER, S5_STATE), eye).reshape(
        S5_GROUPS, S5_STATE, S5_GROUP_CH)


def _blockdiag_out(c):
    eye = jnp.eye(S5_PER, dtype=c.dtype)
    return jnp.einsum("jgcp,gh->jgphc", c.reshape(S5_CHUNKS, S5_PER, S5_GROUP_CH, S5_STATE), eye).reshape(
        S5_CHUNKS, S5_PER * S5_STATE, S5_PER * S5_GROUP_CH)


def _blockdiag_out_grad(d):
    eye = jnp.eye(S5_PER, dtype=d.dtype)
    return jnp.einsum("jgphc,gh->jgcp", d.reshape(S5_CHUNKS, S5_PER, S5_STATE, S5_PER, S5_GROUP_CH), eye).reshape(
        S5_GROUPS, S5_GROUP_CH, S5_STATE)


def _local_step(x3, mem3, target3, p, wb, late_weights=None, early_grads=None):
    seqs, l, d = x3.shape
    t = seqs * l
    x = x3.reshape(t, d)
    mem = mem3.reshape(seqs * N_MEM, d)
    target = target3.reshape(t, d)
    full = lambda a: (a, a.shape[1], 0, 0)

    s5_in = (p["s5_a_re"], p["s5_a_im"], p["s5_log_dt"], p["s5_b_re"], p["s5_b_im"])
    (lb_r, lb_i, bb_r, bb_i), s5_pull = jax.vjp(_s5_discretise, *s5_in)
    ar, ai = lb_r.reshape(1, S5_CH), lb_i.reshape(1, S5_CH)
    bbr_d, bbi_d = _blockdiag_in(bb_r).astype(BF16), _blockdiag_in(bb_i).astype(BF16)
    cr_d, ci_d = _blockdiag_out(p["s5_c_re"]).astype(BF16), (-_blockdiag_out(p["s5_c_im"])).astype(BF16)
    d_row = p["s5_d"].reshape(1, S5_WIDTH)

    w_in = wb["w_in"]
    w_qkv = w_in[:, :3 * FOX_WIDTH]
    w_uf = jnp.concatenate(
        [w_in[:, 3 * FOX_WIDTH + N_FOX_HEADS:], w_in[:, 3 * FOX_WIDTH:3 * FOX_WIDTH + N_FOX_HEADS],
         jnp.zeros((d, UF_COLS - S5_WIDTH - N_FOX_HEADS), w_in.dtype)], axis=1)

    hn1 = _rowwise(_rms, [full(x)], [p["norm_mix"]], [(d, d, 0, BF16)], "norm_mix_fwd")
    qkv = _mm(hn1, w_qkv, "nn", "in_qkv")
    uf = _mm(hn1, w_uf, "nn", "in_uf")

    bh = seqs * N_FOX_HEADS
    q_pair = (qkv, 128, 0, 1)
    k_pair = (qkv, 128, N_PAIRS, 1)
    gq2, gk2 = jnp.tile(p["fox_q_norm"], (1, 2)), jnp.tile(p["fox_k_norm"], (1, 2))
    pair_out = [(FOX_WIDTH, 128, 1, BF16)]
    qn = _rowwise(_rms_pair, [q_pair], [gq2], pair_out, "fox_qnorm_fwd", heads=N_PAIRS)
    kn = _rowwise(_rms_pair, [k_pair], [gk2], pair_out, "fox_knorm_fwd", heads=N_PAIRS)

    f_rows = uf[:, S5_WIDTH:S5_WIDTH + N_FOX_HEADS].reshape(seqs, l, N_FOX_HEADS).transpose(0, 2, 1).reshape(bh, l)
    f_bias = jnp.tile(p["fox_f_bias"].reshape(N_FOX_HEADS, 1), (seqs, 1))
    c_wide = jnp.broadcast_to(_forget_fwd(f_rows, f_bias)[:, :, None], (bh, l, 128))
    fox, lse = _fox_fwd(qn, kn, qkv, c_wide, seqs)

    xr, xi, ys = _s5_fwd(uf, bbr_d, bbi_d, cr_d, ci_d, ar, ai, seqs)
    u_blk = (uf, S5_WIDTH, 0, 0)
    yg = _rowwise(_s5_act, [full(ys), u_blk], [d_row], [(S5_WIDTH, S5_WIDTH, 0, F32)], "s5_act_fwd")
    if late_weights is not None:
        wb = dict(wb, **late_weights("mid", yg))
    z = _mm(yg, wb["s5_w_glu"], "nn", "s5_glu")
    y2n = _rowwise(_s5_gate, [full(yg), full(z)], [p["s5_b_glu"], p["out_norm_s5"]],
                   [(S5_WIDTH, S5_WIDTH, 0, BF16)], "s5_gate_fwd")
    foxn = _rowwise(_rms, [full(fox)], [p["out_norm_fox"]], [(FOX_WIDTH, FOX_WIDTH, 0, BF16)], "fox_outnorm_fwd")
    mixed = jnp.concatenate([foxn, y2n], axis=1)
    h1 = _mm(mixed, wb["w_out"], "nn", "mix_out", res=x)
    if late_weights is not None:
        wb = dict(wb, **late_weights("late", h1))

    hn2 = _rowwise(_rms, [full(h1)], [p["norm_cross"]], [(d, d, 0, BF16)], "norm_cross_fwd")
    mn = _rowwise(_rms, [full(mem)], [p["norm_mem"]], [(d, d, 0, BF16)], "norm_mem_fwd")
    xq_raw = _mm(hn2, wb["w_xq"], "nn", "x_q")
    kv = _mm(mn, wb["w_xkv"], "nn", "x_kv")
    xh = lambda a: (a, X_HEAD_DIM, 0, 1)
    xqn = _rowwise(_rms, [xh(xq_raw)], [p["xq_norm"]], [(d, X_HEAD_DIM, 1, BF16)], "x_qnorm_fwd", heads=N_X_HEADS)
    xkn = _rowwise(_rms, [xh(kv)], [p["xk_norm"]], [(d, X_HEAD_DIM, 1, BF16)], "x_knorm_fwd", heads=N_X_HEADS)
    xo = _xatt_fwd(xqn, xkn, kv, seqs)
    h2 = _mm(xo, wb["w_xo"], "nn", "x_out", res=h1)

    hn3 = _rowwise(_rms, [full(h2)], [p["norm_ffn"]], [(d, d, 0, BF16)], "norm_ffn_fwd")
    gate, up, act = _ffn_up_gate(hn3, wb["w_ffn_up"], p["ffn_conv_w"], p["ffn_conv_b"], seqs)
    h3 = _mm(act, wb["w_ffn_down"], "nn", "ffn_down", res=h2)
    dh3, dh3_b, loss = _loss_head(h3, target)

    g = {}
    late_dt = BF16 if early_grads is not None else F32
    g["w_ffn_down"] = _mm(act, dh3_b, "tn", "ffn_down_dw", out_dtype=late_dt)
    dgu, dconv_w, dconv_b = _ffn_down_dx_gate(dh3_b, wb["w_ffn_down"], gate, up, p["ffn_conv_w"], p["ffn_conv_b"], seqs)
    g["ffn_conv_w"], g["ffn_conv_b"] = jnp.sum(dconv_w, axis=0), jnp.sum(dconv_b, axis=0)
    dhn3 = _mm(dgu, wb["w_ffn_up"], "nt", "ffn_up_dx", out_dtype=BF16)
    g["w_ffn_up"] = _mm(hn3, dgu, "tn", "ffn_up_dw", out_dtype=late_dt)
    (dh2,), (g["norm_ffn"],) = _rowwise_vjp(_rms, [full(h2)], [p["norm_ffn"]], [full(dhn3)], "norm_ffn_bwd",
                                            adds=[full(dh3)])

    dxo = _mm(dh2, wb["w_xo"], "nt", "x_out_dx", out_dtype=BF16)
    g["w_xo"] = _mm(xo, dh2, "tn", "x_out_dw", out_dtype=late_dt)
    dxqn, dxkn, dxv = _xatt_bwd(xqn, xkn, kv, dxo, seqs)
    (dxq_raw,), (g["xq_norm"],) = _rowwise_vjp(_rms, [xh(xq_raw)], [p["xq_norm"]], [xh(dxqn)], "x_qnorm_bwd",
                                               heads=N_X_HEADS, row_dtypes=[BF16])
    (dxk_raw,), (g["xk_norm"],) = _rowwise_vjp(_rms, [xh(kv)], [p["xk_norm"]], [xh(dxkn)], "x_knorm_bwd",
                                               heads=N_X_HEADS, row_dtypes=[BF16])
    dkv = jnp.concatenate([dxk_raw, dxv.astype(BF16)], axis=1)
    dhn2 = _mm(dxq_raw, wb["w_xq"], "nt", "x_q_dx", out_dtype=BF16)
    g["w_xq"] = _mm(hn2, dxq_raw, "tn", "x_q_dw", out_dtype=late_dt)
    dmn = _mm(dkv, wb["w_xkv"], "nt", "x_kv_dx")
    g["w_xkv"] = _mm(mn, dkv, "tn", "x_kv_dw", out_dtype=late_dt)
    norm_cross = p["norm_cross"]
    if early_grads is not None:
        norm_cross = norm_cross + early_grads({n: g[n] for n in LATE_WEIGHTS})
    (dh1,), (g["norm_cross"],) = _rowwise_vjp(_rms, [full(h1)], [norm_cross], [full(dhn2)], "norm_cross_bwd",
                                              adds=[full(dh2)])
    _, (g["norm_mem"],) = _rowwise_vjp(_rms, [full(mem)], [p["norm_mem"]], [full(dmn)], "norm_mem_bwd",
                                       row_dtypes=[BF16])

    dmixed = _mm(dh1, wb["w_out"], "nt", "mix_out_dx", out_dtype=BF16)
    g["w_out"] = _mm(mixed, dh1, "tn", "mix_out_dw", out_dtype=late_dt)
    (dfox,), (g["out_norm_fox"],) = _rowwise_vjp(_rms, [full(fox)], [p["out_norm_fox"]],
                                                 [(dmixed, FOX_WIDTH, 0, 0)], "fox_outnorm_bwd")
    (dyg_a, dz), (g["s5_b_glu"], g["out_norm_s5"]) = _rowwise_vjp(
        _s5_gate, [full(yg), full(z)], [p["s5_b_glu"], p["out_norm_s5"]], [(dmixed, S5_WIDTH, 1, 0)], "s5_gate_bwd",
        row_dtypes=[F32, BF16])
    dyg = _mm(dz, wb["s5_w_glu"], "nt", "s5_glu_dx", res=dyg_a)
    g["s5_w_glu"] = _mm(yg, dz, "tn", "s5_glu_dw", out_dtype=late_dt)
    (dys, du_a), (dd_row,) = _rowwise_vjp(_s5_act, [full(ys), u_blk], [d_row], [full(dyg)], "s5_act_bwd",
                                          row_dtypes=[BF16, F32])
    g["s5_d"] = dd_row
    du_b, dbbr_d, dbbi_d, dcr_d, dci_d, dar, dai = _s5_bwd(dys, uf, xr, xi, bbr_d, bbi_d, cr_d, ci_d, ar, ai, seqs)
    dbbr_d, dbbi_d, dcr_d, dci_d = (jnp.sum(a, axis=0) for a in (dbbr_d, dbbi_d, dcr_d, dci_d))
    d_lb_r = jnp.sum(dar, axis=0).reshape(S5_GROUPS, S5_STATE)
    d_lb_i = jnp.sum(dai, axis=0).reshape(S5_GROUPS, S5_STATE)
    g["s5_a_re"], g["s5_a_im"], g["s5_log_dt"], g["s5_b_re"], g["s5_b_im"] = s5_pull(
        (d_lb_r, d_lb_i, _blockdiag_in_grad(dbbr_d), _blockdiag_in_grad(dbbi_d)))
    g["s5_c_re"] = _blockdiag_out_grad(dcr_d)
    g["s5_c_im"] = -_blockdiag_out_grad(dci_d)

    dqn, dkn, dv, dc, dcq = _fox_bwd(qn, kn, qkv, c_wide, fox, dfox, lse, seqs)
    pair = lambda a: (a, 128, 0, 1)
    (dq_raw,), (dgq2,) = _rowwise_vjp(_rms_pair, [q_pair], [gq2], [pair(dqn)], "fox_qnorm_bwd", heads=N_PAIRS,
                                      row_dtypes=[BF16])
    (dk_raw,), (dgk2,) = _rowwise_vjp(_rms_pair, [k_pair], [gk2], [pair(dkn)], "fox_knorm_bwd", heads=N_PAIRS,
                                      row_dtypes=[BF16])
    g["fox_q_norm"] = dgq2[:, :HEAD_DIM] + dgq2[:, HEAD_DIM:]
    g["fox_k_norm"] = dgk2[:, :HEAD_DIM] + dgk2[:, HEAD_DIM:]
    df_rows, dfb = _forget_bwd(f_rows, f_bias, (dc + dcq).reshape(bh, l))
    g["fox_f_bias"] = jnp.sum(dfb.reshape(seqs, N_FOX_HEADS), axis=0)
    df = df_rows.reshape(seqs, N_FOX_HEADS, l).transpose(0, 2, 1).reshape(t, N_FOX_HEADS)
    dqkv = jnp.concatenate([dq_raw, dk_raw, dv.astype(BF16)], axis=1)
    duf = jnp.concatenate([du_a + du_b, df, jnp.zeros((t, UF_COLS - S5_WIDTH - N_FOX_HEADS), F32)],
                          axis=1).astype(BF16)
    dhn1 = _mm(duf, w_uf, "nt", "in_uf_dx", res=_mm(dqkv, w_qkv, "nt", "in_qkv_dx"), out_dtype=BF16)
    dw_qkv = _mm(hn1, dqkv, "tn", "in_qkv_dw")
    dw_uf = _mm(hn1, duf, "tn", "in_uf_dw")
    g["w_in"] = jnp.concatenate([dw_qkv, dw_uf[:, S5_WIDTH:S5_WIDTH + N_FOX_HEADS], dw_uf[:, :S5_WIDTH]], axis=1)
    (dx,), (g["norm_mix"],) = _rowwise_vjp(_rms, [full(x)], [p["norm_mix"]], [full(dhn1)], "norm_mix_bwd",
                                           adds=[full(dh1)])
    return loss, dx.reshape(seqs, l, d), g


def _place():
    return lax.axis_index("x"), lax.axis_index("y"), lax.axis_index("c")


def _other_chips(x, y):
    return [(1 - x, y), (x, 1 - y), (1 - x, 1 - y)]


ANY = pl.BlockSpec(memory_space=pl.ANY)


def _gather_weights(shards, col_kind, taps):
    n = len(shards)

    def body(*refs):
        ins, tap_in, outs, tap_out = refs[:n], refs[n], refs[n + 1:2 * n + 1], refs[2 * n + 1]
        ici_send, ici_recv, d2d_send, d2d_recv, own_send, own_recv = refs[2 * n + 2:]
        x, y, c = _place()
        mine = 2 * x + y
        chips = _other_chips(x, y)
        sibling = (x, y, 1 - c)

        def piece(a, s, h):
            r, cs = ins[a].shape
            hr = r // 2
            if col_kind[a]:
                return outs[a].at[pl.ds(pl.multiple_of(h * hr, 16), hr), pl.ds(pl.multiple_of(s * cs, 128), cs)]
            return outs[a].at[pl.ds(pl.multiple_of(s * r + h * hr, 16), hr), :]

        def slab(a, s):
            r, cs = ins[a].shape
            if col_kind[a]:
                return outs[a].at[:, pl.ds(pl.multiple_of(s * cs, 128), cs)]
            return outs[a].at[pl.ds(pl.multiple_of(s * r, 16), r), :]

        def own_half(a, h):
            hr = ins[a].shape[0] // 2
            return ins[a].at[pl.ds(pl.multiple_of(h * hr, 16), hr), :]

        sends = []
        for a in range(n):
            cp = pltpu.make_async_remote_copy(
                src_ref=ins[a], dst_ref=slab(a, mine), send_sem=own_send.at[a], recv_sem=own_recv.at[a],
                device_id=sibling, device_id_type=MESH)
            cp.start()
            sends.append(cp)
        cp = pltpu.make_async_remote_copy(
            src_ref=tap_in, dst_ref=tap_out.at[mine], send_sem=own_send.at[n], recv_sem=own_recv.at[n],
            device_id=sibling, device_id_type=MESH)
        cp.start()
        sends.append(cp)
        for a in range(n):
            for j, (px, py) in enumerate(chips):
                cp = pltpu.make_async_remote_copy(
                    src_ref=own_half(a, c), dst_ref=piece(a, mine, c), send_sem=ici_send.at[3 * a + j],
                    recv_sem=ici_recv.at[3 * a + j], device_id=(px, py, c), device_id_type=MESH)
                cp.start()
                sends.append(cp)
        for j, (px, py) in enumerate(chips):
            cp = pltpu.make_async_remote_copy(
                src_ref=tap_in, dst_ref=tap_out.at[mine], send_sem=ici_send.at[3 * n + j],
                recv_sem=ici_recv.at[3 * n + j], device_id=(px, py, c), device_id_type=MESH)
            cp.start()
            sends.append(cp)
        for a in range(n):
            for j, (px, py) in enumerate(chips):
                got = piece(a, 2 * px + py, c)
                pltpu.make_async_remote_copy(
                    src_ref=got, dst_ref=got, send_sem=ici_send.at[3 * a + j], recv_sem=ici_recv.at[3 * a + j],
                    device_id=(px, py, c), device_id_type=MESH).wait_recv()
                fwd = pltpu.make_async_remote_copy(
                    src_ref=got, dst_ref=got, send_sem=d2d_send.at[3 * a + j], recv_sem=d2d_recv.at[3 * a + j],
                    device_id=(x, y, 1 - c), device_id_type=MESH)
                fwd.start()
                sends.append(fwd)
        for a in range(n):
            for j, (px, py) in enumerate(chips):
                other = piece(a, 2 * px + py, 1 - c)
                pltpu.make_async_remote_copy(
                    src_ref=other, dst_ref=other, send_sem=d2d_send.at[3 * a + j], recv_sem=d2d_recv.at[3 * a + j],
                    device_id=(x, y, 1 - c), device_id_type=MESH).wait_recv()
        for j, (px, py) in enumerate(chips):
            pltpu.make_async_remote_copy(
                src_ref=tap_in, dst_ref=tap_out.at[2 * px + py], send_sem=ici_send.at[3 * n + j],
                recv_sem=ici_recv.at[3 * n + j], device_id=(px, py, c), device_id_type=MESH).wait_recv()
        for a in range(n):
            pltpu.make_async_remote_copy(
                src_ref=ins[a], dst_ref=slab(a, mine), send_sem=own_send.at[a], recv_sem=own_recv.at[a],
                device_id=sibling, device_id_type=MESH).wait_recv()
        pltpu.make_async_remote_copy(
            src_ref=tap_in, dst_ref=tap_out.at[mine], send_sem=own_send.at[n], recv_sem=own_recv.at[n],
            device_id=sibling, device_id_type=MESH).wait_recv()
        for cp in sends:
            cp.wait_send()

    def full_shape(a):
        r, cs = shards[a].shape
        return (r, 4 * cs) if col_kind[a] else (4 * r, cs)

    res = pl.pallas_call(
        body, name="gather_weights", in_specs=[ANY] * (n + 1), out_specs=[ANY] * (n + 1),
        out_shape=[jax.ShapeDtypeStruct(full_shape(a), shards[a].dtype) for a in range(n)]
        + [jax.ShapeDtypeStruct((4,) + taps.shape, taps.dtype)],
        scratch_shapes=[pltpu.SemaphoreType.DMA((3 * n + 3,)), pltpu.SemaphoreType.DMA((3 * n + 3,)),
                        pltpu.SemaphoreType.DMA((3 * n,)), pltpu.SemaphoreType.DMA((3 * n,)),
                        pltpu.SemaphoreType.DMA((n + 1,)), pltpu.SemaphoreType.DMA((n + 1,))],
        compiler_params=pltpu.CompilerParams(has_side_effects=True),
    )(*shards, taps)
    return res[:n], res[n]


HBM = pl.BlockSpec(memory_space=pltpu.HBM)
SEM = pl.BlockSpec(memory_space=pltpu.SEMAPHORE)
DATAFLOW = pltpu.SideEffectType.DATAFLOW_SIDE_EFFECTING


def _in_hbm(a):
    return pltpu.with_memory_space_constraint(a, pltpu.HBM)


def _split_start(name, srcs, lands, n_copies, plan):
    n = len(srcs)

    def body(*refs):
        src_refs, land_refs = refs[:n], refs[n:2 * n]
        send_sems, recv_sems = refs[2 * n], refs[2 * n + 1]
        for i, (src, dst, dev) in enumerate(plan(src_refs, land_refs)):
            pltpu.make_async_remote_copy(src_ref=src, dst_ref=dst, send_sem=send_sems.at[i], recv_sem=recv_sems.at[i],
                                         device_id=dev, device_id_type=MESH).start()
        refs[-1][...] = jnp.zeros((8, 128), F32)

    res = pl.pallas_call(
        body, name=name, in_specs=[HBM] * (2 * n),
        out_specs=[SEM, SEM] + [HBM] * (2 * n) + [pl.BlockSpec(memory_space=pltpu.VMEM)],
        out_shape=[pltpu.SemaphoreType.DMA((n_copies,)), pltpu.SemaphoreType.DMA((n_copies,))]
        + [pltpu.HBM(a.shape, a.dtype) for a in list(srcs) + list(lands)] + [jax.ShapeDtypeStruct((8, 128), F32)],
        input_output_aliases={i: 2 + i for i in range(2 * n)},
        compiler_params=pltpu.CompilerParams(has_side_effects=DATAFLOW),
    )(*[_in_hbm(a) for a in list(srcs) + list(lands)])
    return res[0], res[1], list(res[2:2 + n]), list(res[2 + n:2 + 2 * n]), res[-1]


def _split_wait(name, send_sems, recv_sems, srcs, lands, after, plan):
    n = len(srcs)

    def body(*refs):
        src_refs, land_refs = refs[:n], refs[n:2 * n]
        send_ref, recv_ref = refs[2 * n], refs[2 * n + 1]
        for i, (src, dst, dev) in enumerate(plan(src_refs, land_refs)):
            cp = pltpu.make_async_remote_copy(src_ref=src, dst_ref=dst, send_sem=send_ref.at[i], recv_sem=recv_ref.at[i],
                                              device_id=dev, device_id_type=MESH)
            cp.wait_send()
            cp.wait_recv()

    res = pl.pallas_call(
        body, name=name, in_specs=[HBM] * (2 * n) + [SEM, SEM, ANY], out_specs=[HBM] * (2 * n),
        out_shape=[pltpu.HBM(a.shape, a.dtype) for a in list(srcs) + list(lands)],
        input_output_aliases={i: i for i in range(2 * n)},
        compiler_params=pltpu.CompilerParams(has_side_effects=DATAFLOW),
    )(*srcs, *lands, send_sems, recv_sems, after)
    return list(res[:n]), list(res[n:])


def _late_gather_plan(col_kind):
    def plan(src_refs, land_refs):
        x, y, c = _place()
        mine = 2 * x + y
        copies = []
        for a, (src, land) in enumerate(zip(src_refs, land_refs)):
            r, cs = src.shape
            if col_kind[a]:
                dst = land.at[:, pl.ds(pl.multiple_of(mine * cs, 128), cs)]
            else:
                dst = land.at[pl.ds(pl.multiple_of(mine * r, 16), r), :]
            copies.append((src, dst, (x, y, 1 - c)))
            copies += [(src, dst, (px, py, c)) for (px, py) in _other_chips(x, y)]
        return copies
    return plan


def _late_reduce_plan(col_kind):
    def plan(src_refs, land_refs):
        x, y, c = _place()
        copies = []
        for a, (src, land) in enumerate(zip(src_refs, land_refs)):
            for j, (px, py) in enumerate(_other_chips(x, y)):
                if col_kind[a] is None:
                    piece = src
                elif col_kind[a]:
                    cs = land.shape[2]
                    piece = src.at[:, pl.ds(pl.multiple_of((2 * px + py) * cs, 128), cs)]
                else:
                    piece = src.at[2 * px + py]
                copies.append((piece, land.at[j], (px, py, c)))
        return copies
    return plan


def _pair_swap(name, halves):
    n = len(halves)

    def body(*refs):
        ins, outs = refs[:n], refs[n:2 * n]
        send_sems, recv_sems = refs[2 * n:]
        x, y, c = _place()
        copies = []
        for a in range(n):
            cp = pltpu.make_async_remote_copy(
                src_ref=ins[a], dst_ref=outs[a], send_sem=send_sems.at[a], recv_sem=recv_sems.at[a],
                device_id=(x, y, 1 - c), device_id_type=MESH)
            cp.start()
            copies.append(cp)
        for cp in copies:
            cp.wait()

    return pl.pallas_call(
        body, name=name, in_specs=[ANY] * n, out_specs=[ANY] * n,
        out_shape=[jax.ShapeDtypeStruct(s.shape, s.dtype) for s in halves],
        scratch_shapes=[pltpu.SemaphoreType.DMA((n,)), pltpu.SemaphoreType.DMA((n,))],
        compiler_params=pltpu.CompilerParams(has_side_effects=True),
    )(*halves)


def _chip_sum(name, chip_sel, own, col, others):
    _, r, c = others.shape
    tr = _pick(r, (256, 128, 64, 32, 16))
    if col:
        own_spec = pl.BlockSpec((tr, c), lambda i, s: (i, s[0]))
    else:
        own_spec = pl.BlockSpec((None, tr, c), lambda i, s: (s[0], i, 0))
    specs = [own_spec] + [pl.BlockSpec((None, tr, c), lambda i, s, k=k: (k, i, 0)) for k in range(3)]

    def body(s_ref, own_ref, r0, r1, r2, o_ref):
        total = ((own_ref[...].astype(F32) + r0[...].astype(F32)) + r1[...].astype(F32)) + r2[...].astype(F32)
        o_ref[...] = total.astype(o_ref.dtype)

    return pl.pallas_call(
        body, name=name,
        grid_spec=pltpu.PrefetchScalarGridSpec(
            num_scalar_prefetch=1, grid=(r // tr,), in_specs=specs,
            out_specs=pl.BlockSpec((tr, c), lambda i, s: (i, 0))),
        out_shape=jax.ShapeDtypeStruct((r, c), BF16),
        compiler_params=_params(("parallel",)),
    )(chip_sel, own, others, others, others)


def _small_layout(vals):
    sizes = [int(math.prod(v.shape)) for v in vals]
    padded = [-(-s // 128) * 128 for s in sizes]
    return sizes, padded, -(-sum(padded) // 1024) * 1024


def _pack_small(vals):
    sizes, padded, total = _small_layout(vals)
    flat = [jnp.pad(v.reshape(-1), (0, p - s)) for v, s, p in zip(vals, sizes, padded)]
    flat.append(jnp.zeros((total - sum(padded),), F32))
    return jnp.concatenate(flat).reshape(total // 128, 128)


def _allreduce_small(own, others, vals):
    def body(own_ref, oth_ref, out_ref, land, send_sem, recv_sem):
        x, y, c = _place()
        out_ref[...] = (own_ref[...] + oth_ref[0]) + (oth_ref[1] + oth_ref[2])
        cp = pltpu.make_async_remote_copy(
            src_ref=out_ref, dst_ref=land, send_sem=send_sem.at[0], recv_sem=recv_sem.at[0],
            device_id=(x, y, 1 - c), device_id_type=MESH)
        cp.start()
        cp.wait()
        out_ref[...] = out_ref[...] + land[...]

    vm = pl.BlockSpec(memory_space=pltpu.VMEM)
    summed = pl.pallas_call(
        body, name="allreduce_small", in_specs=[vm, vm], out_specs=vm,
        out_shape=jax.ShapeDtypeStruct(own.shape, F32),
        scratch_shapes=[pltpu.VMEM(own.shape, F32), pltpu.SemaphoreType.DMA((1,)), pltpu.SemaphoreType.DMA((1,))],
        compiler_params=pltpu.CompilerParams(has_side_effects=True, vmem_limit_bytes=VMEM_LIMIT_BYTES),
    )(own, others).reshape(-1)
    sizes, padded, _ = _small_layout(vals)
    outs, off = [], 0
    for v, s, p in zip(vals, sizes, padded):
        outs.append(summed[off:off + s].reshape(v.shape))
        off += p
    return outs


def _adamw_math(w, g, m, v):
    m2 = ADAM_B1 * m + (1.0 - ADAM_B1) * g
    v2 = ADAM_B2 * v + (1.0 - ADAM_B2) * (g * g)
    m_hat = m2 / (1.0 - ADAM_B1 ** ADAM_STEP)
    v_hat = v2 / (1.0 - ADAM_B2 ** ADAM_STEP)
    delta = -ADAM_LR * (m_hat / (jnp.sqrt(v_hat) + ADAM_EPS) + ADAM_WD * w)
    return delta, m2, v2


def _adamw_big(name, w, g_mine, g_sibling, m, v):
    _, r, c = w.shape

    def body(w_ref, ga_ref, gb_ref, m_ref, v_ref, go_ref, d_ref, mo_ref, vo_ref):
        gv = ga_ref[...].astype(F32) + gb_ref[...].astype(F32)
        d, m2, v2 = _adamw_math(w_ref[...], gv, m_ref[...], v_ref[...])
        go_ref[...] = gv
        d_ref[...] = d
        mo_ref[...] = m2
        vo_ref[...] = v2

    tr = _pick(r, (256, 128, 64, 32, 16, 8))
    if r % tr == 0 and tr % 8 == 0:
        grid = (r // tr,)
        blk = pl.BlockSpec((None, tr, c), lambda i: (0, i, 0))
        part = pl.BlockSpec((tr, c), lambda i: (i, 0))
    else:
        grid = (c // 512,)
        blk = pl.BlockSpec((None, r, 512), lambda i: (0, 0, i))
        part = pl.BlockSpec((r, 512), lambda i: (0, i))
    return pl.pallas_call(
        body, name=name, grid=grid, in_specs=[blk, part, part, blk, blk], out_specs=[blk] * 4,
        out_shape=[jax.ShapeDtypeStruct((1, r, c), F32)] * 4, compiler_params=_params(("parallel",)),
    )(w, g_mine, g_sibling, m, v)


def _adamw_small(ws, gs, ms, vs):
    n = len(ws)

    def body(*refs):
        w_r, g_r, m_r, v_r = refs[:n], refs[n:2 * n], refs[2 * n:3 * n], refs[3 * n:4 * n]
        o = refs[4 * n:]
        for a in range(n):
            gv = g_r[a][...]
            d, m2, v2 = _adamw_math(w_r[a][...], gv, m_r[a][...], v_r[a][...])
            o[a][...] = gv
            o[n + a][...] = d
            o[2 * n + a][...] = m2
            o[3 * n + a][...] = v2

    res = pl.pallas_call(
        body, name="adamw_small", out_shape=[jax.ShapeDtypeStruct(w.shape, F32) for _ in range(4) for w in ws],
        compiler_params=_params(),
    )(*ws, *gs, *ms, *vs)
    return res[:n], res[n:2 * n], res[2 * n:3 * n], res[3 * n:]


def _full_from_gathered(name, gathered):
    if name == "w_in":
        rows = gathered.shape[0] // 4
        return gathered.reshape(4, rows, gathered.shape[1]).transpose(1, 0, 2).reshape(rows, 4 * gathered.shape[1])
    return gathered


def _reduce_layout(name, full):
    if name in COL_KIND:
        return full
    if name == "w_in":
        rows, cols = full.shape
        return full.reshape(rows, 4, cols // 4).transpose(1, 0, 2)
    return full.reshape(4, full.shape[0] // 4, full.shape[1])


def kernel(x, mem, norm_mix, w_in, fox_q_norm, fox_k_norm, fox_f_bias, s5_a_re, s5_a_im, s5_log_dt, s5_b_re, s5_b_im, s5_c_re, s5_c_im, s5_d, s5_w_glu, s5_b_glu, out_norm_fox, out_norm_s5, w_out, norm_cross, norm_mem, w_xq, w_xkv, xq_norm, xk_norm, w_xo, norm_ffn, w_ffn_up, ffn_conv_w, ffn_conv_b, w_ffn_down, loss_target, m_norm_mix, m_w_in, m_fox_q_norm, m_fox_k_norm, m_fox_f_bias, m_s5_a_re, m_s5_a_im, m_s5_log_dt, m_s5_b_re, m_s5_b_im, m_s5_c_re, m_s5_c_im, m_s5_d, m_s5_w_glu, m_s5_b_glu, m_out_norm_fox, m_out_norm_s5, m_w_out, m_norm_cross, m_norm_mem, m_w_xq, m_w_xkv, m_xq_norm, m_xk_norm, m_w_xo, m_norm_ffn, m_w_ffn_up, m_ffn_conv_w, m_ffn_conv_b, m_w_ffn_down, v_norm_mix, v_w_in, v_fox_q_norm, v_fox_k_norm, v_fox_f_bias, v_s5_a_re, v_s5_a_im, v_s5_log_dt, v_s5_b_re, v_s5_b_im, v_s5_c_re, v_s5_c_im, v_s5_d, v_s5_w_glu, v_s5_b_glu, v_out_norm_fox, v_out_norm_s5, v_w_out, v_norm_cross, v_norm_mem, v_w_xq, v_w_xkv, v_xq_norm, v_xk_norm, v_w_xo, v_norm_ffn, v_w_ffn_up, v_ffn_conv_w, v_ffn_conv_b, v_w_ffn_down):
    given = dict(locals())
    w = {n: given[n] for n in WEIGHTS}
    m = {n: given["m_" + n] for n in WEIGHTS}
    v = {n: given["v_" + n] for n in WEIGHTS}
    xi, yi, _ = _place()
    chip = (2 * xi + yi).astype(jnp.int32)
    chip_sel = chip.reshape(1)
    early_kind = [n in COL_KIND for n in EARLY_WEIGHTS]
    late_kind = [n in COL_KIND for n in LATE_WEIGHTS]

    gathered, taps = _gather_weights([w[FIRST_WEIGHT][0].astype(BF16)], [False], w["ffn_conv_w"][0])
    first_full = gathered[0]
    conv_w = taps.transpose(1, 0, 2).reshape(3, D_FF)
    pending = {}
    g_started = None
    for stage, names in (("mid", MID_WEIGHTS), ("late", LATE_WEIGHTS)):
        kinds = [n in COL_KIND for n in names]
        shards = [w[n][0].astype(BF16) for n in names]
        if g_started is None:
            first_full, shards[0] = lax.optimization_barrier((first_full, shards[0]))
        else:
            shards[0] = shards[0] + g_started[0:1, 0:1].astype(BF16)
        full = [lax.empty((s.shape[0], 4 * s.shape[1]) if ck else (4 * s.shape[0], s.shape[1]), BF16)
                for s, ck in zip(shards, kinds)]
        plan = _late_gather_plan(kinds)
        send, recv, srcs, lands, g_started = _split_start(
            "gather_" + stage + "_start", shards, full, 4 * len(names), plan)
        pending[stage] = (names, plan, send, recv, srcs, lands)
    wb = {FIRST_WEIGHT: _full_from_gathered(FIRST_WEIGHT, first_full)}

    def late_weights(stage, after):
        names, plan, send, recv, srcs, lands = pending[stage]
        _, full = _split_wait("gather_" + stage + "_wait", send, recv, srcs, lands, after, plan)
        return dict(zip(names, full))

    reduce_plan = _late_reduce_plan(late_kind)
    late_reduce = {}

    def early_grads(late_g):
        grads = [_reduce_layout(n, late_g[n]) for n in LATE_WEIGHTS]
        lands = [lax.empty((3, s.shape[0], s.shape[1] // 4) if ck else (3,) + s.shape[1:], BF16)
                 for s, ck in zip(grads, late_kind)]
        late_reduce["sems"] = _split_start("reduce_late_start", grads, lands, 3 * len(LATE_WEIGHTS), reduce_plan)
        return late_reduce["sems"][4][0:1, 0:1]

    p = {n: w[n][0] for n in SMALL}
    p["ffn_conv_w"] = conv_w
    for n in ("norm_mix", "fox_q_norm", "fox_k_norm", "fox_f_bias", "s5_b_glu", "out_norm_fox", "out_norm_s5",
              "norm_cross", "norm_mem", "xq_norm", "xk_norm", "norm_ffn", "ffn_conv_b"):
        p[n] = p[n].reshape(1, -1)
    p["norm_mix"] = p["norm_mix"] + g_started[0:1, 0:1]
    loss, grad_x, g = _local_step(x, mem, loss_target, p, wb, late_weights, early_grads)

    grads = [_reduce_layout(n, g[n].astype(BF16)) for n in EARLY_WEIGHTS]
    early_lands = [lax.empty((3, s.shape[0], s.shape[1] // 4) if ck else (3,) + s.shape[1:], BF16)
                   for s, ck in zip(grads, early_kind)]
    small_names = list(SMALL) + ["ffn_conv_w"]
    small_vals = [g[n].reshape(w[n].shape if n != "ffn_conv_w" else (1, 3, D_FF)) for n in small_names] + [loss]
    packed = _pack_small(small_vals)
    early_plan = _late_reduce_plan(early_kind + [None])
    e_send, e_recv, e_srcs, e_lands, e_started = _split_start(
        "reduce_early_start", grads + [packed], early_lands + [lax.empty((3,) + packed.shape, F32)],
        3 * (len(EARLY_WEIGHTS) + 1), early_plan)

    out_g, out_d, out_m, out_v = {}, {}, {}, {}

    def finish(names, kinds, sums, from_chips, tag):
        mine = [_chip_sum("reduce_chip_sum_" + n, chip_sel, ps, ck, fc)
                for n, ps, fc, ck in zip(names, sums, from_chips, kinds)]
        theirs = _pair_swap("reduce_pair_swap_" + tag, mine)
        for n, a, b in zip(names, mine, theirs):
            if n == "w_in":
                flip = lambda t: jnp.swapaxes(t, -1, -2)
                res = _adamw_big("adamw_" + n, flip(w[n]), flip(a), flip(b), flip(m[n]), flip(v[n]))
                out_g[n], out_d[n], out_m[n], out_v[n] = (flip(t) for t in res)
                continue
            out_g[n], out_d[n], out_m[n], out_v[n] = _adamw_big("adamw_" + n, w[n], a, b, m[n], v[n])

    r_send, r_recv, r_srcs, r_lands, _ = late_reduce["sems"]
    late_sums, late_from_chips = _split_wait("reduce_late_wait", r_send, r_recv, r_srcs, r_lands, e_started,
                                             reduce_plan)
    finish(LATE_WEIGHTS, late_kind, late_sums, late_from_chips, "late")

    early_sums, early_from_chips = _split_wait("reduce_early_wait", e_send, e_recv, e_srcs, e_lands,
                                               out_v[LATE_WEIGHTS[-1]], early_plan)
    finish(EARLY_WEIGHTS, early_kind, early_sums[:-1], early_from_chips[:-1], "early")

    reduced = _allreduce_small(early_sums[-1], early_from_chips[-1], small_vals)
    loss_all = reduced[-1].reshape(())
    conv_w_grad = lax.dynamic_slice_in_dim(reduced[-2], chip * (D_FF // 4), D_FF // 4, axis=2)
    sg, sd, sm, sv = _adamw_small(
        [w[n] for n in small_names], list(reduced[:len(SMALL)]) + [conv_w_grad],
        [m[n] for n in small_names], [v[n] for n in small_names])
    out_g.update(zip(small_names, sg))
    out_d.update(zip(small_names, sd))
    out_m.update(zip(small_names, sm))
    out_v.update(zip(small_names, sv))

    return (loss_all, grad_x, *[out_g[n] for n in WEIGHTS], *[out_d[n] for n in WEIGHTS],
            *[out_m[n] for n in WEIGHTS], *[out_v[n] for n in WEIGHTS])
```

```python
import math

import jax
import jax.numpy as jnp
from jax import lax
from jax.experimental import pallas as pl
from jax.experimental.pallas import tpu as pltpu

F32 = jnp.float32
BF16 = jnp.bfloat16

D_MODEL = 1024
FOX_WIDTH = 512
HEAD_DIM = 64
N_FOX_HEADS = 8
S5_WIDTH = 512
S5_GROUP_CH = 16
S5_GROUPS = 32
S5_STATE = 64
S5_CH = S5_GROUPS * S5_STATE
N_X_HEADS = 4
X_HEAD_DIM = 256
N_MEM = 256
D_FF = 2816
UF_COLS = 640
EPS = 1e-6
ADAM_LR = 0.001
ADAM_B1 = 0.9
ADAM_B2 = 0.999
ADAM_EPS = 1e-08
ADAM_WD = 0.01
ADAM_STEP = 10

VMEM_LIMIT_BYTES = 56 * 1024 * 1024
MM_BLOCK_BYTES = 6 * 1024 * 1024
MM_VMEM_BYTES = 40 * 1024 * 1024
MM_TILE_MAX = 1536
MESH = pl.DeviceIdType.MESH

FIRST_WEIGHT = "w_in"
MID_WEIGHTS = ("s5_w_glu", "w_out")
EARLY_WEIGHTS = (FIRST_WEIGHT,) + MID_WEIGHTS
LATE_WEIGHTS = ("w_xq", "w_xkv", "w_xo", "w_ffn_up", "w_ffn_down")
BIG = EARLY_WEIGHTS + LATE_WEIGHTS
COL_KIND = ("w_xkv", "w_ffn_up")
SMALL = ("norm_mix", "fox_q_norm", "fox_k_norm", "fox_f_bias", "s5_a_re", "s5_a_im", "s5_log_dt",
         "s5_b_re", "s5_b_im", "s5_c_re", "s5_c_im", "s5_d", "s5_b_glu", "out_norm_fox", "out_norm_s5",
         "norm_cross", "norm_mem", "xq_norm", "xk_norm", "norm_ffn", "ffn_conv_b")
WEIGHTS = ("norm_mix", "w_in", "fox_q_norm", "fox_k_norm", "fox_f_bias", "s5_a_re", "s5_a_im", "s5_log_dt",
           "s5_b_re", "s5_b_im", "s5_c_re", "s5_c_im", "s5_d", "s5_w_glu", "s5_b_glu", "out_norm_fox",
           "out_norm_s5", "w_out", "norm_cross", "norm_mem", "w_xq", "w_xkv", "xq_norm", "xk_norm", "w_xo",
           "norm_ffn", "w_ffn_up", "ffn_conv_w", "ffn_conv_b", "w_ffn_down")


def _params(sem=None):
    return pltpu.CompilerParams(dimension_semantics=sem, vmem_limit_bytes=VMEM_LIMIT_BYTES)


def _pick(n, cands):
    for c in cands:
        if n % c == 0:
            return c
    return n


_DIMS = {"nn": (((1,), (0,)), ((), ())), "nt": (((1,), (1,)), ((), ())), "tn": (((0,), (0,)), ((), ()))}


def _mm(a, b, mode, name, out_dtype=F32, res=None):
    if mode == "nn":
        (m, k), (k2, n) = a.shape, b.shape
    elif mode == "nt":
        (m, k), (n, k2) = a.shape, b.shape
    else:
        (k, m), (k2, n) = a.shape, b.shape
    assert k == k2, (name, a.shape, b.shape)

    has_res = res is not None
    a_size, b_size = a.dtype.itemsize, b.dtype.itemsize
    o_size = jnp.dtype(out_dtype).itemsize + (res.dtype.itemsize if has_res else 0)

    def tiles(dim):
        return [c for c in range(MM_TILE_MAX, 0, -128) if dim % c == 0] or [dim]

    best = None
    for tm in tiles(m):
        for tn in tiles(n):
            a_blk, b_blk = tm * k * a_size, tn * k * b_size
            if max(a_blk, b_blk) > MM_BLOCK_BYTES or 2 * (a_blk + b_blk + tm * tn * o_size) > MM_VMEM_BYTES:
                continue
            for rows_outer in (True, False):
                moved = (m * k * a_size + (m // tm) * n * k * b_size) if rows_outer else \
                        (n * k * b_size + (n // tn) * m * k * a_size)
                key = (moved, -(tm * tn))
                if best is None or key < best[0]:
                    best = (key, tm, tn, rows_outer)
    assert best is not None, (name, a.shape, b.shape)
    _, tm, tn, rows_outer = best
    ij = (lambda g0, g1: (g0, g1)) if rows_outer else (lambda g0, g1: (g1, g0))
    if mode == "tn":
        a_spec = pl.BlockSpec((k, tm), lambda g0, g1: (0, ij(g0, g1)[0]))
    else:
        a_spec = pl.BlockSpec((tm, k), lambda g0, g1: (ij(g0, g1)[0], 0))
    if mode == "nt":
        b_spec = pl.BlockSpec((tn, k), lambda g0, g1: (ij(g0, g1)[1], 0))
    else:
        b_spec = pl.BlockSpec((k, tn), lambda g0, g1: (0, ij(g0, g1)[1]))
    o_spec = pl.BlockSpec((tm, tn), lambda g0, g1: ij(g0, g1))
    grid = (m // tm, n // tn) if rows_outer else (n // tn, m // tm)
    dims = _DIMS[mode]

    def body(*refs):
        a_ref, b_ref = refs[0], refs[1]
        o_ref = refs[-1]
        acc = lax.dot_general(a_ref[...].astype(BF16), b_ref[...].astype(BF16), dims, preferred_element_type=F32)
        if has_res:
            acc = acc + refs[2][...].astype(F32)
        o_ref[...] = acc.astype(o_ref.dtype)

    return pl.pallas_call(
        body, name=name, grid=grid,
        in_specs=[a_spec, b_spec] + ([o_spec] if has_res else []),
        out_specs=o_spec, out_shape=jax.ShapeDtypeStruct((m, n), out_dtype),
        compiler_params=_params(("parallel", "parallel")),
    )(*((a, b, res) if has_res else (a, b)))


def _row_spec(tm, bc, off, step):
    return pl.BlockSpec((tm, bc), lambda i, h: (i, off + step * h))


ROW_TILE_ELEMS = 512 * 1024


def _row_tile(t, rows):
    widest = max(bc for (_, bc, _, _) in rows)
    return _pick(t, (min(t, ROW_TILE_ELEMS // widest), 512, 256, 128, 64, 8))


def _rowwise(fn, rows, pars, outs, name, heads=1):
    t = rows[0][0].shape[0]
    tm = _row_tile(t, rows)
    nr, npar = len(rows), len(pars)

    def body(*refs):
        vals = [r[...].astype(F32) for r in refs[:nr + npar]]
        res = fn(*vals)
        if not isinstance(res, (tuple, list)):
            res = (res,)
        for o_ref, v in zip(refs[nr + npar:], res):
            o_ref[...] = v.astype(o_ref.dtype)

    in_specs = [_row_spec(tm, bc, off, st) for (_, bc, off, st) in rows]
    in_specs += [pl.BlockSpec(p.shape, lambda i, h: (0, 0)) for p in pars]
    out_specs = [_row_spec(tm, bc, 0, st) for (_, bc, st, _) in outs]
    out_shape = [jax.ShapeDtypeStruct((t, c), dt) for (c, _, _, dt) in outs]
    res = pl.pallas_call(
        body, name=name, grid=(t // tm, heads), in_specs=in_specs, out_specs=out_specs, out_shape=out_shape,
        compiler_params=_params(("parallel", "parallel")),
    )(*[r[0] for r in rows], *pars)
    return res[0] if len(res) == 1 else res


def _rowwise_vjp(fn, rows, pars, cts, name, heads=1, adds=None, row_dtypes=None):
    t = rows[0][0].shape[0]
    tm = _row_tile(t, rows)
    nr, npar, nct = len(rows), len(pars), len(cts)
    adds = adds or [None] * nr
    add_list = [a for a in adds if a is not None]
    row_dtypes = row_dtypes or [F32] * nr

    def body(*refs):
        i, h = pl.program_id(0), pl.program_id(1)
        p = 0
        row_v = [r[...].astype(F32) for r in refs[p:p + nr]]; p += nr
        par_v = [r[...].astype(F32) for r in refs[p:p + npar]]; p += npar
        ct_v = [r[...].astype(F32) for r in refs[p:p + nct]]; p += nct
        add_refs = refs[p:p + len(add_list)]; p += len(add_list)
        drow_refs = refs[p:p + nr]; p += nr
        dpar_refs = refs[p:p + npar]

        def wrapped(*a):
            r = fn(*a)
            return tuple(r) if isinstance(r, (tuple, list)) else (r,)

        _, pull = jax.vjp(wrapped, *row_v, *par_v)
        grads = pull(tuple(ct_v))
        ai = 0
        for k in range(nr):
            g = grads[k]
            if adds[k] is not None:
                g = g + add_refs[ai][...].astype(F32)
                ai += 1
            drow_refs[k][...] = g.astype(drow_refs[k].dtype)

        @pl.when((i == 0) & (h == 0))
        def _():
            for r in dpar_refs:
                r[...] = jnp.zeros(r.shape, r.dtype)

        for k in range(npar):
            dpar_refs[k][...] += grads[nr + k]

    in_specs = [_row_spec(tm, bc, off, st) for (_, bc, off, st) in rows]
    in_specs += [pl.BlockSpec(q.shape, lambda i, h: (0, 0)) for q in pars]
    in_specs += [_row_spec(tm, bc, off, st) for (_, bc, off, st) in cts]
    in_specs += [_row_spec(tm, bc, off, st) for (_, bc, off, st) in add_list]
    out_specs = [_row_spec(tm, bc, 0, st) for (_, bc, _, st) in rows]
    out_specs += [pl.BlockSpec(q.shape, lambda i, h: (0, 0)) for q in pars]
    out_shape = [jax.ShapeDtypeStruct((t, bc * (heads if st else 1)), dt) for (_, bc, _, st), dt in zip(rows, row_dtypes)]
    out_shape += [jax.ShapeDtypeStruct(q.shape, F32) for q in pars]
    res = pl.pallas_call(
        body, name=name, grid=(t // tm, heads), in_specs=in_specs, out_specs=out_specs, out_shape=out_shape,
        compiler_params=_params(("arbitrary", "arbitrary")),
    )(*[r[0] for r in rows], *pars, *[c[0] for c in cts], *[a[0] for a in add_list])
    return list(res[:nr]), list(res[nr:])


def _rms(x, g):
    return x * lax.rsqrt(jnp.mean(x * x, axis=-1, keepdims=True) + EPS) * g


def _rms_pair(x, g):
    left = lax.broadcasted_iota(jnp.int32, x.shape, 1) < HEAD_DIM
    x2 = x * x
    ms_a = jnp.sum(jnp.where(left, x2, 0.0), axis=-1, keepdims=True) * (1.0 / HEAD_DIM)
    ms_b = jnp.sum(jnp.where(left, 0.0, x2), axis=-1, keepdims=True) * (1.0 / HEAD_DIM)
    return x * lax.rsqrt(jnp.where(left, ms_a, ms_b) + EPS) * g


def _gelu(x):
    return 0.5 * x * (1.0 + jnp.tanh(math.sqrt(2.0 / math.pi) * (x + 0.044715 * (x * x * x))))


def _s5_act(ys, u, d):
    return _gelu(ys + d * u)


def _s5_gate(yg, z, b, g):
    return _rms(yg * jax.nn.sigmoid(z + b), g)


def _lane_cumsum(x, reverse):
    n = x.shape[-1]
    lane = lax.broadcasted_iota(jnp.int32, x.shape, 1)
    k = 1
    while k < n:
        if reverse:
            x = x + jnp.where(lane < n - k, pltpu.roll(x, n - k, 1), 0.0)
        else:
            x = x + jnp.where(lane >= k, pltpu.roll(x, k, 1), 0.0)
        k *= 2
    return x


def _log_sigmoid(z):
    return jnp.minimum(z, 0.0) - jnp.log(1.0 + jnp.exp(-jnp.abs(z)))


def _forget_fwd(f, bias):
    def body(f_ref, b_ref, c_ref):
        c_ref[...] = _lane_cumsum(_log_sigmoid(f_ref[...] + b_ref[...]), False)

    return pl.pallas_call(body, name="forget_fwd", out_shape=jax.ShapeDtypeStruct(f.shape, F32),
                          compiler_params=_params())(f, bias)


def _forget_bwd(f, bias, dc):
    def body(f_ref, b_ref, dc_ref, df_ref, db_ref):
        dlog = _lane_cumsum(dc_ref[...], True)
        df = dlog * jax.nn.sigmoid(-(f_ref[...] + b_ref[...]))
        df_ref[...] = df
        db_ref[...] = jnp.sum(df, axis=1, keepdims=True)

    return pl.pallas_call(body, name="forget_bwd",
                          out_shape=(jax.ShapeDtypeStruct(f.shape, F32), jax.ShapeDtypeStruct(bias.shape, F32)),
                          compiler_params=_params())(f, bias, dc)


FOX_BLOCK = 1024
FOX_KEYS = 1024
FOX_BWD_BLOCK = 512
_NT = _DIMS["nt"]
_TN = _DIMS["tn"]


N_PAIRS = N_FOX_HEADS // 2
V_BLOCK0 = 2 * N_PAIRS


def _left_lanes(shape):
    return lax.broadcasted_iota(jnp.int32, shape, 1) < HEAD_DIM


def _top_rows(shape):
    return lax.broadcasted_iota(jnp.int32, shape, 0) < HEAD_DIM


def _wide(c_tile, n):
    return c_tile if n == 128 else jnp.concatenate([c_tile] * (n // 128), axis=1)


def _fox_fwd(qn, kn, qkv, c_wide, seqs):
    t = qn.shape[0]
    l = t // seqs
    tb = min(FOX_BLOCK, l)
    tk = min(FOX_KEYS, tb)
    ratio = tb // tk
    nb = l // tb
    scale = HEAD_DIM ** -0.5

    def body(q_ref, k_ref, v_ref, ca_ref, cb_ref, o_ref, lse_ref, vt_ref):
        i = pl.program_id(2)
        top = _top_rows((128, tb))

        @pl.when(i == 0)
        def _():
            vt_ref[...] = v_ref[...].T.astype(BF16)

        qt = (q_ref[...].astype(F32) * scale).T.astype(BF16)
        zero = jnp.zeros_like(qt)
        qts = (jnp.where(top, qt, zero), jnp.where(top, zero, qt))
        top_k = _top_rows((128, tk))
        zero_k = jnp.zeros((128, tk), BF16)
        key_pos = lax.broadcasted_iota(jnp.int32, (tk, tb), 0)
        query_pos = lax.broadcasted_iota(jnp.int32, (tk, tb), 1)
        c_refs = (ca_ref, cb_ref)

        def scores(j):
            off = pl.multiple_of(j * tk, tk)
            k2 = k_ref[pl.ds(off, tk), :]
            return tuple(jnp.dot(k2, qts[h], preferred_element_type=F32) - _wide(c_refs[h][pl.ds(off, tk), :], tb)
                         for h in (0, 1))

        def values_times(ps, j):
            vt = vt_ref[:, pl.ds(pl.multiple_of(j * tk, tk), tk)]
            return (jnp.dot(jnp.where(top_k, vt, zero_k), ps[0], preferred_element_type=F32)
                    + jnp.dot(jnp.where(top_k, zero_k, vt), ps[1], preferred_element_type=F32))

        def softmax_step(sts, stats, first_key):
            ps, new, alphas = [], [], []
            for st, (m, s_sum) in zip(sts, stats):
                if first_key is not None:
                    st = jnp.where(key_pos + first_key <= query_pos, st, -jnp.inf)
                m_new = jnp.maximum(m, jnp.max(st, axis=0, keepdims=True))
                alpha = jnp.exp(m - m_new)
                p = jnp.exp(st - m_new)
                new.append((m_new, alpha * s_sum + jnp.sum(p, axis=0, keepdims=True)))
                alphas.append(alpha)
                ps.append(p.astype(BF16))
            return tuple(ps), tuple(new), jnp.where(top, alphas[0], alphas[1])

        def tile(j, carry, first_key):
            stats, acc = carry
            ps, stats, alpha = softmax_step(scores(j), stats, first_key)
            return stats, alpha * acc + values_times(ps, j)

        stat = (jnp.full((1, tb), -jnp.inf, F32), jnp.zeros((1, tb), F32))
        below = i * ratio
        carry = lax.fori_loop(0, below, lambda j, c: tile(j, c, None), ((stat, stat), jnp.zeros((128, tb), F32)))
        for r in range(ratio):
            carry = tile(below + r, carry, r * tk)
        ((ma, sa), (mb, sb)), acc = carry
        o_ref[...] = (acc / jnp.where(top, sa, sb)).T
        lse_ref[0:1, :] = ma + jnp.log(sa)
        lse_ref[1:2, :] = mb + jnp.log(sb)

    qblk = pl.BlockSpec((tb, 128), lambda b, hp, i: (b * nb + i, hp))
    return pl.pallas_call(
        body, name="fox_fwd", grid=(seqs, N_PAIRS, nb),
        in_specs=[qblk, pl.BlockSpec((l, 128), lambda b, hp, i: (b, hp)),
                  pl.BlockSpec((l, 128), lambda b, hp, i: (b, V_BLOCK0 + hp)),
                  pl.BlockSpec((None, l, 128), lambda b, hp, i: (b * N_FOX_HEADS + 2 * hp, 0, 0)),
                  pl.BlockSpec((None, l, 128), lambda b, hp, i: (b * N_FOX_HEADS + 2 * hp + 1, 0, 0))],
        out_specs=[qblk, pl.BlockSpec((None, 2, tb), lambda b, hp, i: (b * N_PAIRS + hp, 0, i))],
        out_shape=[jax.ShapeDtypeStruct((t, FOX_WIDTH), F32), jax.ShapeDtypeStruct((seqs * N_PAIRS, 2, l), F32)],
        scratch_shapes=[pltpu.VMEM((128, l), BF16)],
        compiler_params=_params(("parallel", "parallel", "arbitrary")),
    )(qn, kn, qkv, c_wide, c_wide)


def _fox_bwd(qn, kn, qkv, c_wide, o, do, lse, seqs):
    t = qn.shape[0]
    l = t // seqs
    tb = min(FOX_BWD_BLOCK, l)
    nb = l // tb
    scale = HEAD_DIM ** -0.5
    one_at = (HEAD_DIM, 0)

    def body(q_ref, k_ref, v_ref, ca_ref, cb_ref, o_ref, do_ref, lse_ref, dq_ref, dk_ref, dv_ref, dc_ref, dcq_ref,
             qt_ref, kt_ref, dot_ref, delta_ref, dqa_ref, dqb_ref):
        top_l = _top_rows((128, l))
        top = _top_rows((128, tb))
        left = _left_lanes((tb, 128))
        row_id = lax.broadcasted_iota(jnp.int32, (128, tb), 0)
        lane_id = lax.broadcasted_iota(jnp.int32, (tb, 128), 1)
        zero_t = jnp.zeros((128, tb), BF16)
        zero_l = jnp.zeros((tb, 128), BF16)
        rows = lambda a: (jnp.where(top, a, zero_t), jnp.where(top, zero_t, a))
        lanes = lambda a: (jnp.where(left, a, zero_l), jnp.where(left, zero_l, a))
        with_one_row = lambda pair: tuple(jnp.where(row_id == one_at[h], 1.0, pair[h]).astype(BF16) for h in (0, 1))
        with_one_lane = lambda pair: tuple(jnp.where(lane_id == one_at[h], 1.0, pair[h]).astype(BF16) for h in (0, 1))
        causal = lax.broadcasted_iota(jnp.int32, (tb, tb), 0) <= lax.broadcasted_iota(jnp.int32, (tb, tb), 1)
        c_refs = (ca_ref, cb_ref)
        dq_refs = (dqa_ref, dqb_ref)

        qt_ref[...] = (q_ref[...].astype(F32) * scale).T.astype(BF16)
        kt_ref[...] = k_ref[...].astype(F32).T.astype(BF16)
        do_t = do_ref[...].T
        dot_ref[...] = do_t.astype(BF16)
        prod_t = do_t * o_ref[...].T
        delta_ref[0:1, :] = jnp.sum(jnp.where(top_l, prod_t, 0.0), axis=0, keepdims=True)
        delta_ref[1:2, :] = jnp.sum(jnp.where(top_l, 0.0, prod_t), axis=0, keepdims=True)
        dqa_ref[...] = jnp.zeros(dqa_ref.shape, F32)
        dqb_ref[...] = jnp.zeros(dqb_ref.shape, F32)

        def kv_block(j, _):
            koff = pl.multiple_of(j * tb, tb)
            k2 = k_ref[pl.ds(koff, tb), :]
            v2 = v_ref[pl.ds(koff, tb), :].astype(BF16)
            kts = with_one_row(rows(kt_ref[:, pl.ds(koff, tb)]))
            cw = tuple(_wide(c_refs[h][pl.ds(koff, tb), :], tb) for h in (0, 1))

            def q_block(i, carry, masked):
                dks, dv = list(carry[:2]), carry[2]
                qoff = pl.multiple_of(i * tb, tb)
                qs = lanes((q_ref[pl.ds(qoff, tb), :].astype(F32) * scale).astype(BF16))
                qs_one = with_one_lane(qs)
                dos = lanes(do_ref[pl.ds(qoff, tb), :].astype(BF16))
                qts = rows(qt_ref[:, pl.ds(qoff, tb)])
                dots = rows(dot_ref[:, pl.ds(qoff, tb)])
                for h in (0, 1):
                    st = jnp.dot(k2, qts[h], preferred_element_type=F32) - cw[h]
                    p = jnp.exp(st - lse_ref[h:h + 1, pl.ds(qoff, tb)])
                    if masked:
                        p = jnp.where(causal, p, 0.0)
                    dp = jnp.dot(v2, dots[h], preferred_element_type=F32)
                    dsb = (p * (dp - delta_ref[h:h + 1, pl.ds(qoff, tb)])).astype(BF16)
                    dv = dv + jnp.dot(p.astype(BF16), dos[h], preferred_element_type=F32)
                    dks[h] = dks[h] + jnp.dot(dsb, qs_one[h], preferred_element_type=F32)
                    dq_refs[h][:, pl.ds(qoff, tb)] += jnp.dot(kts[h], dsb, preferred_element_type=F32)
                return dks[0], dks[1], dv

            z = jnp.zeros((tb, 128), F32)
            carry = q_block(j, (z, z, z), True)
            rest = nb - 1 - j
            carry = lax.fori_loop(
                0, rest // 2, lambda n, c: q_block(j + 2 + 2 * n, q_block(j + 1 + 2 * n, c, False), False), carry)
            dka, dkb, dv = lax.cond(rest % 2 == 1, lambda c: q_block(nb - 1, c, False), lambda c: c, carry)
            dk_ref[pl.ds(koff, tb), :] = jnp.where(left, dka, dkb)
            dv_ref[pl.ds(koff, tb), :] = dv
            dc_ref[0:1, pl.ds(koff, tb)] = -dka.T[one_at[0]:one_at[0] + 1, :]
            dc_ref[1:2, pl.ds(koff, tb)] = -dkb.T[one_at[1]:one_at[1] + 1, :]
            return 0

        lax.fori_loop(0, nb, kv_block, 0)
        dq_ref[...] = (jnp.where(top_l, dqa_ref[...], dqb_ref[...]) * scale).T
        dcq_ref[0:1, :] = dqa_ref[one_at[0]:one_at[0] + 1, :]
        dcq_ref[1:2, :] = dqb_ref[one_at[1]:one_at[1] + 1, :]

    blk = pl.BlockSpec((l, 128), lambda b, hp: (b, hp))
    cspec = lambda k: pl.BlockSpec((None, l, 128), lambda b, hp: (b * N_FOX_HEADS + 2 * hp + k, 0, 0))
    rows2 = pl.BlockSpec((None, 2, l), lambda b, hp: (b * N_PAIRS + hp, 0, 0))
    wide = jax.ShapeDtypeStruct((t, FOX_WIDTH), F32)
    pair_rows = jax.ShapeDtypeStruct((seqs * N_PAIRS, 2, l), F32)
    return pl.pallas_call(
        body, name="fox_bwd", grid=(seqs, N_PAIRS),
        in_specs=[blk, blk, pl.BlockSpec((l, 128), lambda b, hp: (b, V_BLOCK0 + hp)), cspec(0), cspec(1), blk, blk, rows2],
        out_specs=[blk, blk, blk, rows2, rows2],
        out_shape=[wide, wide, wide, pair_rows, pair_rows],
        scratch_shapes=[pltpu.VMEM((128, l), BF16), pltpu.VMEM((128, l), BF16), pltpu.VMEM((128, l), BF16),
                        pltpu.VMEM((2, l), F32), pltpu.VMEM((128, l), F32), pltpu.VMEM((128, l), F32)],
        compiler_params=_params(("parallel", "parallel")),
    )(qn, kn, qkv, c_wide, c_wide, o, do, lse)


SCAN_ROWS = 512
SCAN_COLS = 1024


S5_IN = 128
S5_ST = 512
SCAN_CHUNKS = SCAN_COLS // S5_ST
SCAN_SEGS = 8
LANES = 128


def _cmul(ar, ai, br, bi):
    return ar * br - ai * bi, ar * bi + ai * br


def _powers_into(pw_r, pw_i, a_r, a_i, seg):
    pw_r[0:1, :] = a_r
    pw_i[0:1, :] = a_i
    for k in range(1, seg):
        pr, pi = _cmul(pw_r[k - 1:k, :], pw_i[k - 1:k, :], a_r, a_i)
        pw_r[k:k + 1, :] = pr
        pw_i[k:k + 1, :] = pi


def _interleave(dst, src, seg):
    for h in range(src.shape[0]):
        for j in range(seg):
            dst[h, j * SCAN_SEGS:(j + 1) * SCAN_SEGS, :] = src[h, pl.ds(j, SCAN_SEGS, stride=seg), :]


def _deinterleave(dst, src, seg):
    for h in range(src.shape[0]):
        for j in range(seg):
            dst[h, pl.ds(j, SCAN_SEGS, stride=seg), :] = src[h, j * SCAN_SEGS:(j + 1) * SCAN_SEGS, :]


def _interleaved(ref, tmp_a, tmp_b, seg):
    n = ref.shape[1] // LANES
    for h in range(n):
        tmp_a[h] = ref[:, h * LANES:(h + 1) * LANES].astype(F32)
    _interleave(tmp_b, tmp_a, seg)
    return jnp.concatenate([tmp_b[h] for h in range(n)], axis=1)


def _store_deinterleaved(ref, val, tmp_a, tmp_b, seg):
    n = ref.shape[1] // LANES
    for h in range(n):
        tmp_a[h] = val[:, h * LANES:(h + 1) * LANES]
    _deinterleave(tmp_b, tmp_a, seg)
    for h in range(n):
        ref[:, h * LANES:(h + 1) * LANES] = tmp_b[h]


def _segment_scan(b_r, b_i, x_r, x_i, pw_r, pw_i, car_r, car_i, seg, sign, reverse, visit=None):
    nc = b_r.shape[0]
    sub = lax.broadcasted_iota(jnp.int32, (SCAN_SEGS, LANES), 0)
    lanes = lambda c: slice(c * LANES, (c + 1) * LANES)
    rows = lambda j: pl.ds(pl.multiple_of(((seg - 1 - j) if reverse else j) * SCAN_SEGS, SCAN_SEGS), SCAN_SEGS)
    a1 = [(pw_r[0:1, lanes(c)], sign * pw_i[0:1, lanes(c)]) for c in range(nc)]

    def local(j, xs):
        out = []
        for c in range(nc):
            xr, xi = xs[2 * c], xs[2 * c + 1]
            nr = a1[c][0] * xr - a1[c][1] * xi + b_r[c, rows(j), :]
            ni = a1[c][0] * xi + a1[c][1] * xr + b_i[c, rows(j), :]
            x_r[c, rows(j), :] = nr
            x_i[c, rows(j), :] = ni
            out += [nr, ni]
        return tuple(out)

    zero = jnp.zeros((SCAN_SEGS, LANES), F32)
    ends = lax.fori_loop(0, seg, local, (zero,) * (2 * nc))

    if reverse:
        first = sub == SCAN_SEGS - 1
        neighbour = lambda v: pltpu.roll(v, SCAN_SEGS - 1, 0)
        shift = lambda v, d: jnp.where(sub < SCAN_SEGS - d, pltpu.roll(v, SCAN_SEGS - d, 0), 0.0)
    else:
        first = sub == 0
        neighbour = lambda v: pltpu.roll(v, 1, 0)
        shift = lambda v, d: jnp.where(sub >= d, pltpu.roll(v, d, 0), 0.0)
    last = 0 if reverse else SCAN_SEGS - 1
    entries = []
    for c in range(nc):
        er, ei = ends[2 * c], ends[2 * c + 1]
        pr, pi = pw_r[seg - 1:seg, lanes(c)], sign * pw_i[seg - 1:seg, lanes(c)]
        yr = jnp.where(first, car_r[:, lanes(c)], neighbour(er))
        yi = jnp.where(first, car_i[:, lanes(c)], neighbour(ei))
        qr, qi = pr, pi
        for d in (1, 2, 4):
            mr, mi = _cmul(qr, qi, shift(yr, d), shift(yi, d))
            yr, yi = yr + mr, yi + mi
            qr, qi = _cmul(qr, qi, qr, qi)
        lr, li = _cmul(pr, pi, yr, yi)
        car_r[:, lanes(c)] = (er + lr)[last:last + 1, :]
        car_i[:, lanes(c)] = (ei + li)[last:last + 1, :]
        entries += [yr, yi]

    def correct(j, prev):
        out = []
        row_r, row_i = pw_r[pl.ds(j, 1), :], sign * pw_i[pl.ds(j, 1), :]
        for c in range(nc):
            mr, mi = _cmul(row_r[:, lanes(c)], row_i[:, lanes(c)], entries[2 * c], entries[2 * c + 1])
            nr = x_r[c, rows(j), :] + mr
            ni = x_i[c, rows(j), :] + mi
            x_r[c, rows(j), :] = nr
            x_i[c, rows(j), :] = ni
            if visit is not None:
                visit(c, rows(j), prev[2 * c], prev[2 * c + 1])
            out += [nr, ni]
        return tuple(out)

    lax.fori_loop(0, seg, correct, tuple(entries))


def _s5_fwd(uf, bbr, bbi, cr, ci, ar, ai, seqs):
    t = uf.shape[0]
    l = t // seqs
    tl = min(SCAN_ROWS, l)
    nl = l // tl
    seg = tl // SCAN_SEGS
    cb, nq = SCAN_COLS, SCAN_CHUNKS
    nc = cb // LANES
    per = S5_ST // LANES

    def body(u_ref, bbr_ref, bbi_ref, cr_ref, ci_ref, ar_ref, ai_ref, x_r, x_i, ys_ref,
             car_r, car_i, pw_r, pw_i, b_r, b_i, tmp_a, tmp_b):
        @pl.when(pl.program_id(2) == 0)
        def _():
            car_r[...] = jnp.zeros(car_r.shape, F32)
            car_i[...] = jnp.zeros(car_i.shape, F32)
            _powers_into(pw_r, pw_i, ar_ref[...], ai_ref[...], seg)

        u = _interleaved(u_ref, tmp_a, tmp_b, seg).astype(BF16)
        for q in range(nq):
            uq = u[:, q * S5_IN:(q + 1) * S5_IN]
            br = jnp.dot(uq, bbr_ref[q], preferred_element_type=F32)
            bi = jnp.dot(uq, bbi_ref[q], preferred_element_type=F32)
            for s in range(per):
                b_r[q * per + s] = br[:, s * LANES:(s + 1) * LANES]
                b_i[q * per + s] = bi[:, s * LANES:(s + 1) * LANES]
        _segment_scan(b_r, b_i, x_r, x_i, pw_r, pw_i, car_r, car_i, seg, 1.0, False)
        wide = lambda buf, q: jnp.concatenate([buf[q * per + s] for s in range(per)], axis=1).astype(BF16)
        ys = [jnp.dot(wide(x_r, q), cr_ref[q], preferred_element_type=F32)
              + jnp.dot(wide(x_i, q), ci_ref[q], preferred_element_type=F32) for q in range(nq)]
        _store_deinterleaved(ys_ref, jnp.concatenate(ys, axis=1), tmp_a, tmp_b, seg)

    rows = lambda w: pl.BlockSpec((tl, w), lambda s, j, r: (s * nl + r, j))
    state = pl.BlockSpec((nc, tl, LANES), lambda s, j, r: (j, s * nl + r, 0))
    chunk = lambda a: pl.BlockSpec((nq,) + a.shape[1:], lambda s, j, r: (j, 0, 0))
    par = pl.BlockSpec((1, cb), lambda s, j, r: (0, j))
    return pl.pallas_call(
        body, name="s5_fwd", grid=(seqs, S5_CH // cb, nl),
        in_specs=[rows(nq * S5_IN), chunk(bbr), chunk(bbi), chunk(cr), chunk(ci), par, par],
        out_specs=[state, state, rows(nq * S5_IN)],
        out_shape=[jax.ShapeDtypeStruct((S5_CH // LANES, t, LANES), F32)] * 2
        + [jax.ShapeDtypeStruct((t, S5_WIDTH), F32)],
        scratch_shapes=[pltpu.VMEM((1, cb), F32), pltpu.VMEM((1, cb), F32), pltpu.VMEM((seg, cb), F32),
                        pltpu.VMEM((seg, cb), F32)] + [pltpu.VMEM((nc, tl, LANES), F32)] * 2
        + [pltpu.VMEM((nq * S5_IN // LANES, tl, LANES), F32)] * 2,
        compiler_params=_params(("parallel", "parallel", "arbitrary")),
    )(uf, bbr, bbi, cr, ci, ar, ai)


def _s5_bwd(dys, uf, xr, xi, bbr, bbi, cr, ci, ar, ai, seqs):
    t = dys.shape[0]
    l = t // seqs
    tl = min(SCAN_ROWS, l)
    nl = l // tl
    seg = tl // SCAN_SEGS
    cb, nq = SCAN_COLS, SCAN_CHUNKS
    nc = cb // LANES
    per = S5_ST // LANES

    def body(dy_ref, u_ref, x_r, x_i, bbr_ref, bbi_ref, cr_ref, ci_ref, ar_ref, ai_ref,
             du_ref, dbbr_ref, dbbi_ref, dcr_ref, dci_ref, dar_ref, dai_ref,
             car_r, car_i, pw_r, pw_i, g_r, g_i, lam_r, lam_i, acc_r, acc_i, tmp_a, tmp_b):
        @pl.when(pl.program_id(2) == 0)
        def _():
            car_r[...] = jnp.zeros(car_r.shape, F32)
            car_i[...] = jnp.zeros(car_i.shape, F32)
            _powers_into(pw_r, pw_i, ar_ref[...], ai_ref[...], seg)
            for acc_ref in (dbbr_ref, dbbi_ref, dcr_ref, dci_ref, dar_ref, dai_ref):
                acc_ref[...] = jnp.zeros(acc_ref.shape, F32)

        dy = _interleaved(dy_ref, tmp_a, tmp_b, seg).astype(BF16)
        for q in range(nq):
            dyq = dy[:, q * S5_IN:(q + 1) * S5_IN]
            gr = lax.dot_general(dyq, cr_ref[q], _NT, preferred_element_type=F32)
            gi = lax.dot_general(dyq, ci_ref[q], _NT, preferred_element_type=F32)
            for s in range(per):
                g_r[q * per + s] = gr[:, s * LANES:(s + 1) * LANES]
                g_i[q * per + s] = gi[:, s * LANES:(s + 1) * LANES]
        acc_r[...] = jnp.zeros(acc_r.shape, F32)
        acc_i[...] = jnp.zeros(acc_i.shape, F32)

        def visit(c, rws, lr, li):
            xr_t, xi_t = x_r[c, rws, :], x_i[c, rws, :]
            acc_r[c] += lr * xr_t + li * xi_t
            acc_i[c] += li * xr_t - lr * xi_t

        _segment_scan(g_r, g_i, lam_r, lam_i, pw_r, pw_i, car_r, car_i, seg, -1.0, True, visit)
        for c in range(nc):
            dar_ref[:, c * LANES:(c + 1) * LANES] += jnp.sum(acc_r[c], axis=0, keepdims=True)
            dai_ref[:, c * LANES:(c + 1) * LANES] += jnp.sum(acc_i[c], axis=0, keepdims=True)
        u = _interleaved(u_ref, tmp_a, tmp_b, seg).astype(BF16)
        wide = lambda buf, q: jnp.concatenate([buf[q * per + s] for s in range(per)], axis=1).astype(BF16)
        du = []
        for q in range(nq):
            io = slice(q * S5_IN, (q + 1) * S5_IN)
            lq_r, lq_i = wide(lam_r, q), wide(lam_i, q)
            du.append(lax.dot_general(lq_r, bbr_ref[q], _NT, preferred_element_type=F32)
                      + lax.dot_general(lq_i, bbi_ref[q], _NT, preferred_element_type=F32))
            dbbr_ref[q] += lax.dot_general(u[:, io], lq_r, _TN, preferred_element_type=F32)
            dbbi_ref[q] += lax.dot_general(u[:, io], lq_i, _TN, preferred_element_type=F32)
            dcr_ref[q] += lax.dot_general(wide(x_r, q), dy[:, io], _TN, preferred_element_type=F32)
            dci_ref[q] += lax.dot_general(wide(x_i, q), dy[:, io], _TN, preferred_element_type=F32)
        _store_deinterleaved(du_ref, jnp.concatenate(du, axis=1), tmp_a, tmp_b, seg)

    rows = lambda w: pl.BlockSpec((tl, w), lambda s, j, r: (s * nl + nl - 1 - r, j))
    state = pl.BlockSpec((nc, tl, LANES), lambda s, j, r: (j, s * nl + nl - 1 - r, 0))
    chunk = lambda a: pl.BlockSpec((nq,) + a.shape[1:], lambda s, j, r: (j, 0, 0))
    acc = lambda a: pl.BlockSpec((None, nq) + a.shape[1:], lambda s, j, r: (s, j, 0, 0))
    par = pl.BlockSpec((1, cb), lambda s, j, r: (0, j))
    par_acc = pl.BlockSpec((None, 1, cb), lambda s, j, r: (s, 0, j))
    per_seq = lambda a: jax.ShapeDtypeStruct((seqs,) + a.shape, F32)
    return pl.pallas_call(
        body, name="s5_bwd", grid=(seqs, S5_CH // cb, nl),
        in_specs=[rows(nq * S5_IN), rows(nq * S5_IN), state, state, chunk(bbr), chunk(bbi), chunk(cr), chunk(ci),
                  par, par],
        out_specs=[rows(nq * S5_IN), acc(bbr), acc(bbi), acc(cr), acc(ci), par_acc, par_acc],
        out_shape=[jax.ShapeDtypeStruct((t, S5_WIDTH), F32), per_seq(bbr), per_seq(bbi), per_seq(cr), per_seq(ci),
                   jax.ShapeDtypeStruct((seqs, 1, S5_CH), F32), jax.ShapeDtypeStruct((seqs, 1, S5_CH), F32)],
        scratch_shapes=[pltpu.VMEM((1, cb), F32), pltpu.VMEM((1, cb), F32), pltpu.VMEM((seg, cb), F32),
                        pltpu.VMEM((seg, cb), F32)] + [pltpu.VMEM((nc, tl, LANES), F32)] * 4
        + [pltpu.VMEM((nc, SCAN_SEGS, LANES), F32)] * 2 + [pltpu.VMEM((nq * S5_IN // LANES, tl, LANES), F32)] * 2,
        compiler_params=_params(("parallel", "parallel", "arbitrary")),
    )(dys, uf, xr, xi, bbr, bbi, cr, ci, ar, ai)


XATT_BLOCK = 2048


def _xatt_probs(qv, kv):
    s = lax.dot_general(qv, kv, _NT, preferred_element_type=F32) * (X_HEAD_DIM ** -0.5)
    e = jnp.exp(s - jnp.max(s, axis=-1, keepdims=True))
    return e / jnp.sum(e, axis=-1, keepdims=True)


def _xatt_fwd(q, k, kv, seqs):
    t = q.shape[0]
    tq = min(XATT_BLOCK, t // seqs)
    nq = t // seqs // tq

    def body(q_ref, k_ref, v_ref, o_ref):
        p = _xatt_probs(q_ref[...], k_ref[...])
        o_ref[...] = jnp.dot(p.astype(BF16), v_ref[...].astype(BF16), preferred_element_type=F32).astype(o_ref.dtype)

    qs = pl.BlockSpec((tq, X_HEAD_DIM), lambda b, h, i: (b * nq + i, h))
    return pl.pallas_call(
        body, name="xatt_fwd", grid=(seqs, N_X_HEADS, nq),
        in_specs=[qs, pl.BlockSpec((N_MEM, X_HEAD_DIM), lambda b, h, i: (b, h)),
                  pl.BlockSpec((N_MEM, X_HEAD_DIM), lambda b, h, i: (b, N_X_HEADS + h))],
        out_specs=qs, out_shape=jax.ShapeDtypeStruct(q.shape, BF16),
        compiler_params=_params(("parallel", "parallel", "parallel")),
    )(q, k, kv)


def _xatt_bwd(q, k, kv, do, seqs):
    t = q.shape[0]
    tq = min(XATT_BLOCK, t // seqs)
    nq = t // seqs // tq
    scale = X_HEAD_DIM ** -0.5

    def body(q_ref, k_ref, v_ref, do_ref, dq_ref, dk_ref, dv_ref):
        @pl.when(pl.program_id(2) == 0)
        def _():
            dk_ref[...] = jnp.zeros(dk_ref.shape, F32)
            dv_ref[...] = jnp.zeros(dv_ref.shape, F32)

        qv, kk = q_ref[...], k_ref[...]
        p = _xatt_probs(qv, kk)
        dob = do_ref[...].astype(BF16)
        dp = lax.dot_general(dob, v_ref[...].astype(BF16), _NT, preferred_element_type=F32)
        ds = p * (dp - jnp.sum(dp * p, axis=-1, keepdims=True))
        dsb = ds.astype(BF16)
        dq_ref[...] = jnp.dot(dsb, kk, preferred_element_type=F32) * scale
        dk_ref[...] += lax.dot_general(dsb, qv, _TN, preferred_element_type=F32) * scale
        dv_ref[...] += lax.dot_general(p.astype(BF16), dob, _TN, preferred_element_type=F32)

    qs = pl.BlockSpec((tq, X_HEAD_DIM), lambda b, h, i: (b * nq + i, h))
    ks = pl.BlockSpec((N_MEM, X_HEAD_DIM), lambda b, h, i: (b, h))
    return pl.pallas_call(
        body, name="xatt_bwd", grid=(seqs, N_X_HEADS, nq),
        in_specs=[qs, ks, pl.BlockSpec((N_MEM, X_HEAD_DIM), lambda b, h, i: (b, N_X_HEADS + h)), qs],
        out_specs=[qs, ks, ks],
        out_shape=[jax.ShapeDtypeStruct(q.shape, F32), jax.ShapeDtypeStruct(k.shape, F32),
                   jax.ShapeDtypeStruct(k.shape, F32)],
        compiler_params=_params(("parallel", "parallel", "arbitrary")),
    )(q, k, kv, do)


CONV_COLS = 256


def _shift_down(x, k, row):
    return jnp.where(row >= k, pltpu.roll(x, k, 0), 0.0)


def _shift_up(x, k, row):
    n = x.shape[0]
    return jnp.where(row < n - k, pltpu.roll(x, n - k, 0), 0.0)


def _down_from(x, prev, k, row):
    return jnp.where(row >= k, pltpu.roll(x, k, 0), pltpu.roll(prev, k, 0))


GATE_ROWS = 512


def _ffn_up_gate(hn, w_up, w, b, seqs):
    t = hn.shape[0]
    l = t // seqs
    nc = D_FF // CONV_COLS

    rc = min(GATE_ROWS, l)

    def body(a_ref, wg_ref, wu_ref, w_ref, b_ref, g_ref, u_ref, o_ref):
        wv, bias = w_ref[...], b_ref[...]
        row = lax.broadcasted_iota(jnp.int32, (rc, CONV_COLS), 0)
        prev = jnp.zeros((rc, CONV_COLS), F32)
        for k in range(l // rc):
            rows = slice(k * rc, (k + 1) * rc)
            a = a_ref[rows, :]
            gb = jnp.dot(a, wg_ref[...], preferred_element_type=F32).astype(BF16)
            ub = jnp.dot(a, wu_ref[...], preferred_element_type=F32).astype(BF16)
            g_ref[rows, :] = gb
            u_ref[rows, :] = ub
            g = gb.astype(F32)
            pre = bias + wv[0:1, :] * _down_from(g, prev, 2, row) + wv[1:2, :] * _down_from(g, prev, 1, row) \
                + wv[2:3, :] * g
            o_ref[rows, :] = (pre * jax.nn.sigmoid(pre) * ub.astype(F32)).astype(o_ref.dtype)
            prev = g

    cols = pl.BlockSpec((l, CONV_COLS), lambda s, j: (s, j))
    half = jax.ShapeDtypeStruct((t, D_FF), BF16)
    return pl.pallas_call(
        body, name="ffn_up_gate", grid=(seqs, nc),
        in_specs=[pl.BlockSpec((l, hn.shape[1]), lambda s, j: (s, 0)),
                  pl.BlockSpec((hn.shape[1], CONV_COLS), lambda s, j: (0, j)),
                  pl.BlockSpec((hn.shape[1], CONV_COLS), lambda s, j: (0, nc + j)),
                  pl.BlockSpec((3, CONV_COLS), lambda s, j: (0, j)), pl.BlockSpec((1, CONV_COLS), lambda s, j: (0, j))],
        out_specs=[cols, cols, cols], out_shape=[half, half, half],
        compiler_params=_params(("parallel", "parallel")),
    )(hn, w_up, w_up, w, b)


def _ffn_down_dx_gate(dh, w_down, gate, up, w, b, seqs):
    t = dh.shape[0]
    l = t // seqs
    nc = D_FF // CONV_COLS
    steps = nc * seqs

    def body(dh_ref, wd_ref, g_ref, u_ref, w_ref, b_ref, dgu_ref, dw_ref, db_ref, stage, sems):
        s, j = pl.program_id(0), pl.program_id(1)
        n = s * nc + j
        slot = n % 2

        def copies(slot_, j_, s_):
            rows = pl.ds(pl.multiple_of(s_ * l, 16), l)
            return [pltpu.make_async_copy(
                stage.at[slot_, half],
                dgu_ref.at[rows, pl.ds(pl.multiple_of((half * nc + j_) * CONV_COLS, 128), CONV_COLS)],
                sems.at[slot_, half]) for half in (0, 1)]

        @pl.when(n >= 2)
        def _():
            for cp in copies(slot, j, s):
                cp.wait()

        da = lax.dot_general(dh_ref[...], wd_ref[...], _NT, preferred_element_type=F32)
        g, wv = g_ref[...].astype(F32), w_ref[...]
        row = lax.broadcasted_iota(jnp.int32, g.shape, 0)
        g1, g2 = _shift_down(g, 1, row), _shift_down(g, 2, row)
        pre = b_ref[...] + wv[0:1, :] * g2 + wv[1:2, :] * g1 + wv[2:3, :] * g
        sg = jax.nn.sigmoid(pre)
        silu = pre * sg
        stage[slot, 1] = (da * silu).astype(stage.dtype)
        dpre = da * u_ref[...].astype(F32) * (sg * (1.0 + pre * (1.0 - sg)))
        dg = wv[2:3, :] * dpre + wv[1:2, :] * _shift_up(dpre, 1, row) + wv[0:1, :] * _shift_up(dpre, 2, row)
        stage[slot, 0] = dg.astype(stage.dtype)
        for cp in copies(slot, j, s):
            cp.start()
        dw_ref[0:1, :] = jnp.sum(dpre * g2, axis=0, keepdims=True)
        dw_ref[1:2, :] = jnp.sum(dpre * g1, axis=0, keepdims=True)
        dw_ref[2:3, :] = jnp.sum(dpre * g, axis=0, keepdims=True)
        db_ref[...] = jnp.sum(dpre, axis=0, keepdims=True)

        @pl.when(n == steps - 1)
        def _():
            for cp in copies(slot, j, s) + (copies(1 - slot, j, s) if steps > 1 else []):
                cp.wait()

    cols = pl.BlockSpec((l, CONV_COLS), lambda s, j: (s, j))
    return pl.pallas_call(
        body, name="ffn_down_dx_gate", grid=(seqs, nc),
        in_specs=[pl.BlockSpec((l, dh.shape[1]), lambda s, j: (s, 0)),
                  pl.BlockSpec((CONV_COLS, dh.shape[1]), lambda s, j: (j, 0)), cols, cols,
                  pl.BlockSpec((3, CONV_COLS), lambda s, j: (0, j)), pl.BlockSpec((1, CONV_COLS), lambda s, j: (0, j))],
        out_specs=[ANY, pl.BlockSpec((None, 3, CONV_COLS), lambda s, j: (s, 0, j)),
                   pl.BlockSpec((None, 1, CONV_COLS), lambda s, j: (s, 0, j))],
        out_shape=[jax.ShapeDtypeStruct((t, 2 * D_FF), BF16), jax.ShapeDtypeStruct((seqs, 3, D_FF), F32),
                   jax.ShapeDtypeStruct((seqs, 1, D_FF), F32)],
        scratch_shapes=[pltpu.VMEM((2, 2, l, CONV_COLS), BF16), pltpu.SemaphoreType.DMA((2, 2))],
        compiler_params=_params(("arbitrary", "arbitrary")),
    )(dh, w_down, gate, up, w, b)


def _loss_head(h, target):
    t, d = h.shape
    tm = _pick(t, (256, 128, 8))

    def body(h_ref, t_ref, dh_ref, dhb_ref, loss_ref):
        @pl.when(pl.program_id(0) == 0)
        def _():
            loss_ref[...] = jnp.zeros(loss_ref.shape, F32)

        e = h_ref[...] - t_ref[...]
        dh = e * (1.0 / d)
        dh_ref[...] = dh
        dhb_ref[...] = dh.astype(BF16)
        loss_ref[...] += (0.5 / d) * jnp.sum(jnp.sum(e * e, axis=1, keepdims=True), axis=0, keepdims=True)

    blk = pl.BlockSpec((tm, d), lambda i: (i, 0))
    return pl.pallas_call(
        body, name="loss_head", grid=(t // tm,), in_specs=[blk, blk],
        out_specs=[blk, blk, pl.BlockSpec((1, 1), lambda i: (0, 0))],
        out_shape=[jax.ShapeDtypeStruct((t, d), F32), jax.ShapeDtypeStruct((t, d), BF16),
                   jax.ShapeDtypeStruct((1, 1), F32)],
        compiler_params=_params(("arbitrary",)),
    )(h, target)


def _s5_discretise(a_re, a_im, log_dt, b_re, b_im):
    dt = jnp.exp(log_dt)[:, None]
    mag = jnp.exp(a_re * dt)
    lb_r = mag * jnp.cos(a_im * dt)
    lb_i = mag * jnp.sin(a_im * dt)
    den = a_re * a_re + a_im * a_im
    nr = lb_r - 1.0
    coef_r = (nr * a_re + lb_i * a_im) / den
    coef_i = (lb_i * a_re - nr * a_im) / den
    bb_r = coef_r[:, :, None] * b_re - coef_i[:, :, None] * b_im
    bb_i = coef_r[:, :, None] * b_im + coef_i[:, :, None] * b_re
    return lb_r, lb_i, bb_r, bb_i


S5_CHUNKS = 4
S5_PER = S5_GROUPS // S5_CHUNKS


def _blockdiag_in(bb):
    eye = jnp.eye(S5_PER, dtype=bb.dtype)
    return jnp.einsum("jgpc,gh->jgchp", bb.reshape(S5_CHUNKS, S5_PER, S5_STATE, S5_GROUP_CH), eye).reshape(
        S5_CHUNKS, S5_PER * S5_GROUP_CH, S5_PER * S5_STATE)


def _blockdiag_in_grad(d):
    eye = jnp.eye(S5_PER, dtype=d.dtype)
    return jnp.einsum("jgchp,gh->jgpc", d.reshape(S5_CHUNKS, S5_PER, S5_GROUP_CH, S5_PER, S5_STATE), eye).reshape(
        S5_GROUPS, S5_STATE, S5_GROUP_CH)


def _blockdiag_out(c):
    eye = jnp.eye(S5_PER, dtype=c.dtype)
    return jnp.einsum("jgcp,gh->jgphc", c.reshape(S5_CHUNKS, S5_PER, S5_GROUP_CH, S5_STATE), eye).reshape(
        S5_CHUNKS, S5_PER * S5_STATE, S5_PER * S5_GROUP_CH)


def _blockdiag_out_grad(d):
    eye = jnp.eye(S5_PER, dtype=d.dtype)
    return jnp.einsum("jgphc,gh->jgcp", d.reshape(S5_CHUNKS, S5_PER, S5_STATE, S5_PER, S5_GROUP_CH), eye).reshape(
        S5_GROUPS, S5_GROUP_CH, S5_STATE)


def _local_step(x3, mem3, target3, p, wb, late_weights=None, early_grads=None):
    seqs, l, d = x3.shape
    t = seqs * l
    x = x3.reshape(t, d)
    mem = mem3.reshape(seqs * N_MEM, d)
    target = target3.reshape(t, d)
    full = lambda a: (a, a.shape[1], 0, 0)

    s5_in = (p["s5_a_re"], p["s5_a_im"], p["s5_log_dt"], p["s5_b_re"], p["s5_b_im"])
    (lb_r, lb_i, bb_r, bb_i), s5_pull = jax.vjp(_s5_discretise, *s5_in)
    ar, ai = lb_r.reshape(1, S5_CH), lb_i.reshape(1, S5_CH)
    bbr_d, bbi_d = _blockdiag_in(bb_r).astype(BF16), _blockdiag_in(bb_i).astype(BF16)
    cr_d, ci_d = _blockdiag_out(p["s5_c_re"]).astype(BF16), (-_blockdiag_out(p["s5_c_im"])).astype(BF16)
    d_row = p["s5_d"].reshape(1, S5_WIDTH)

    w_in = wb["w_in"]
    w_qkv = w_in[:, :3 * FOX_WIDTH]
    w_uf = jnp.concatenate(
        [w_in[:, 3 * FOX_WIDTH + N_FOX_HEADS:], w_in[:, 3 * FOX_WIDTH:3 * FOX_WIDTH + N_FOX_HEADS],
         jnp.zeros((d, UF_COLS - S5_WIDTH - N_FOX_HEADS), w_in.dtype)], axis=1)

    hn1 = _rowwise(_rms, [full(x)], [p["norm_mix"]], [(d, d, 0, BF16)], "norm_mix_fwd")
    qkv = _mm(hn1, w_qkv, "nn", "in_qkv")
    uf = _mm(hn1, w_uf, "nn", "in_uf")

    bh = seqs * N_FOX_HEADS
    q_pair = (qkv, 128, 0, 1)
    k_pair = (qkv, 128, N_PAIRS, 1)
    gq2, gk2 = jnp.tile(p["fox_q_norm"], (1, 2)), jnp.tile(p["fox_k_norm"], (1, 2))
    pair_out = [(FOX_WIDTH, 128, 1, BF16)]
    qn = _rowwise(_rms_pair, [q_pair], [gq2], pair_out, "fox_qnorm_fwd", heads=N_PAIRS)
    kn = _rowwise(_rms_pair, [k_pair], [gk2], pair_out, "fox_knorm_fwd", heads=N_PAIRS)

    f_rows = uf[:, S5_WIDTH:S5_WIDTH + N_FOX_HEADS].reshape(seqs, l, N_FOX_HEADS).transpose(0, 2, 1).reshape(bh, l)
    f_bias = jnp.tile(p["fox_f_bias"].reshape(N_FOX_HEADS, 1), (seqs, 1))
    c_wide = jnp.broadcast_to(_forget_fwd(f_rows, f_bias)[:, :, None], (bh, l, 128))
    fox, lse = _fox_fwd(qn, kn, qkv, c_wide, seqs)

    xr, xi, ys = _s5_fwd(uf, bbr_d, bbi_d, cr_d, ci_d, ar, ai, seqs)
    u_blk = (uf, S5_WIDTH, 0, 0)
    yg = _rowwise(_s5_act, [full(ys), u_blk], [d_row], [(S5_WIDTH, S5_WIDTH, 0, F32)], "s5_act_fwd")
    if late_weights is not None:
        wb = dict(wb, **late_weights("mid", yg))
    z = _mm(yg, wb["s5_w_glu"], "nn", "s5_glu")
    y2n = _rowwise(_s5_gate, [full(yg), full(z)], [p["s5_b_glu"], p["out_norm_s5"]],
                   [(S5_WIDTH, S5_WIDTH, 0, BF16)], "s5_gate_fwd")
    foxn = _rowwise(_rms, [full(fox)], [p["out_norm_fox"]], [(FOX_WIDTH, FOX_WIDTH, 0, BF16)], "fox_outnorm_fwd")
    mixed = jnp.concatenate([foxn, y2n], axis=1)
    h1 = _mm(mixed, wb["w_out"], "nn", "mix_out", res=x)
    if late_weights is not None:
        wb = dict(wb, **late_weights("late", h1))

    hn2 = _rowwise(_rms, [full(h1)], [p["norm_cross"]], [(d, d, 0, BF16)], "norm_cross_fwd")
    mn = _rowwise(_rms, [full(mem)], [p["norm_mem"]], [(d, d, 0, BF16)], "norm_mem_fwd")
    xq_raw = _mm(hn2, wb["w_xq"], "nn", "x_q")
    kv = _mm(mn, wb["w_xkv"], "nn", "x_kv")
    xh = lambda a: (a, X_HEAD_DIM, 0, 1)
    xqn = _rowwise(_rms, [xh(xq_raw)], [p["xq_norm"]], [(d, X_HEAD_DIM, 1, BF16)], "x_qnorm_fwd", heads=N_X_HEADS)
    xkn = _rowwise(_rms, [xh(kv)], [p["xk_norm"]], [(d, X_HEAD_DIM, 1, BF16)], "x_knorm_fwd", heads=N_X_HEADS)
    xo = _xatt_fwd(xqn, xkn, kv, seqs)
    h2 = _mm(xo, wb["w_xo"], "nn", "x_out", res=h1)

    hn3 = _rowwise(_rms, [full(h2)], [p["norm_ffn"]], [(d, d, 0, BF16)], "norm_ffn_fwd")
    gate, up, act = _ffn_up_gate(hn3, wb["w_ffn_up"], p["ffn_conv_w"], p["ffn_conv_b"], seqs)
    h3 = _mm(act, wb["w_ffn_down"], "nn", "ffn_down", res=h2)
    dh3, dh3_b, loss = _loss_head(h3, target)

    g = {}
    late_dt = BF16 if early_grads is not None else F32
    g["w_ffn_down"] = _mm(act, dh3_b, "tn", "ffn_down_dw", out_dtype=late_dt)
    dgu, dconv_w, dconv_b = _ffn_down_dx_gate(dh3_b, wb["w_ffn_down"], gate, up, p["ffn_conv_w"], p["ffn_conv_b"], seqs)
    g["ffn_conv_w"], g["ffn_conv_b"] = jnp.sum(dconv_w, axis=0), jnp.sum(dconv_b, axis=0)
    dhn3 = _mm(dgu, wb["w_ffn_up"], "nt", "ffn_up_dx", out_dtype=BF16)
    g["w_ffn_up"] = _mm(hn3, dgu, "tn", "ffn_up_dw", out_dtype=late_dt)
    (dh2,), (g["norm_ffn"],) = _rowwise_vjp(_rms, [full(h2)], [p["norm_ffn"]], [full(dhn3)], "norm_ffn_bwd",
                                            adds=[full(dh3)])

    dxo = _mm(dh2, wb["w_xo"], "nt", "x_out_dx", out_dtype=BF16)
    g["w_xo"] = _mm(xo, dh2, "tn", "x_out_dw", out_dtype=late_dt)
    dxqn, dxkn, dxv = _xatt_bwd(xqn, xkn, kv, dxo, seqs)
    (dxq_raw,), (g["xq_norm"],) = _rowwise_vjp(_rms, [xh(xq_raw)], [p["xq_norm"]], [xh(dxqn)], "x_qnorm_bwd",
                                               heads=N_X_HEADS, row_dtypes=[BF16])
    (dxk_raw,), (g["xk_norm"],) = _rowwise_vjp(_rms, [xh(kv)], [p["xk_norm"]], [xh(dxkn)], "x_knorm_bwd",
                                               heads=N_X_HEADS, row_dtypes=[BF16])
    dkv = jnp.concatenate([dxk_raw, dxv.astype(BF16)], axis=1)
    dhn2 = _mm(dxq_raw, wb["w_xq"], "nt", "x_q_dx", out_dtype=BF16)
    g["w_xq"] = _mm(hn2, dxq_raw, "tn", "x_q_dw", out_dtype=late_dt)
    dmn = _mm(dkv, wb["w_xkv"], "nt", "x_kv_dx")
    g["w_xkv"] = _mm(mn, dkv, "tn", "x_kv_dw", out_dtype=late_dt)
    norm_cross = p["norm_cross"]
    if early_grads is not None:
        norm_cross = norm_cross + early_grads("late", {n: g[n] for n in LATE_WEIGHTS})[0:1, 0:1]
    (dh1,), (g["norm_cross"],) = _rowwise_vjp(_rms, [full(h1)], [norm_cross], [full(dhn2)], "norm_cross_bwd",
                                              adds=[full(dh2)])
    _, (g["norm_mem"],) = _rowwise_vjp(_rms, [full(mem)], [p["norm_mem"]], [full(dmn)], "norm_mem_bwd",
                                       row_dtypes=[BF16])

    dmixed = _mm(dh1, wb["w_out"], "nt", "mix_out_dx", out_dtype=BF16)
    g["w_out"] = _mm(mixed, dh1, "tn", "mix_out_dw", out_dtype=late_dt)
    (dfox,), (g["out_norm_fox"],) = _rowwise_vjp(_rms, [full(fox)], [p["out_norm_fox"]],
                                                 [(dmixed, FOX_WIDTH, 0, 0)], "fox_outnorm_bwd")
    (dyg_a, dz), (g["s5_b_glu"], g["out_norm_s5"]) = _rowwise_vjp(
        _s5_gate, [full(yg), full(z)], [p["s5_b_glu"], p["out_norm_s5"]], [(dmixed, S5_WIDTH, 1, 0)], "s5_gate_bwd",
        row_dtypes=[F32, BF16])
    dyg = _mm(dz, wb["s5_w_glu"], "nt", "s5_glu_dx", res=dyg_a)
    g["s5_w_glu"] = _mm(yg, dz, "tn", "s5_glu_dw", out_dtype=late_dt)
    if early_grads is not None:
        d_row = d_row + early_grads("mid", {n: g[n] for n in MID_WEIGHTS})[0:1, 0:1]
    (dys, du_a), (dd_row,) = _rowwise_vjp(_s5_act, [full(ys), u_blk], [d_row], [full(dyg)], "s5_act_bwd",
                                          row_dtypes=[BF16, F32])
    g["s5_d"] = dd_row
    du_b, dbbr_d, dbbi_d, dcr_d, dci_d, dar, dai = _s5_bwd(dys, uf, xr, xi, bbr_d, bbi_d, cr_d, ci_d, ar, ai, seqs)
    dbbr_d, dbbi_d, dcr_d, dci_d = (jnp.sum(a, axis=0) for a in (dbbr_d, dbbi_d, dcr_d, dci_d))
    d_lb_r = jnp.sum(dar, axis=0).reshape(S5_GROUPS, S5_STATE)
    d_lb_i = jnp.sum(dai, axis=0).reshape(S5_GROUPS, S5_STATE)
    g["s5_a_re"], g["s5_a_im"], g["s5_log_dt"], g["s5_b_re"], g["s5_b_im"] = s5_pull(
        (d_lb_r, d_lb_i, _blockdiag_in_grad(dbbr_d), _blockdiag_in_grad(dbbi_d)))
    g["s5_c_re"] = _blockdiag_out_grad(dcr_d)
    g["s5_c_im"] = -_blockdiag_out_grad(dci_d)

    dqn, dkn, dv, dc, dcq = _fox_bwd(qn, kn, qkv, c_wide, fox, dfox, lse, seqs)
    pair = lambda a: (a, 128, 0, 1)
    (dq_raw,), (dgq2,) = _rowwise_vjp(_rms_pair, [q_pair], [gq2], [pair(dqn)], "fox_qnorm_bwd", heads=N_PAIRS,
                                      row_dtypes=[BF16])
    (dk_raw,), (dgk2,) = _rowwise_vjp(_rms_pair, [k_pair], [gk2], [pair(dkn)], "fox_knorm_bwd", heads=N_PAIRS,
                                      row_dtypes=[BF16])
    g["fox_q_norm"] = dgq2[:, :HEAD_DIM] + dgq2[:, HEAD_DIM:]
    g["fox_k_norm"] = dgk2[:, :HEAD_DIM] + dgk2[:, HEAD_DIM:]
    df_rows, dfb = _forget_bwd(f_rows, f_bias, (dc + dcq).reshape(bh, l))
    g["fox_f_bias"] = jnp.sum(dfb.reshape(seqs, N_FOX_HEADS), axis=0)
    df = df_rows.reshape(seqs, N_FOX_HEADS, l).transpose(0, 2, 1).reshape(t, N_FOX_HEADS)
    dqkv = jnp.concatenate([dq_raw, dk_raw, dv.astype(BF16)], axis=1)
    duf = jnp.concatenate([du_a + du_b, df, jnp.zeros((t, UF_COLS - S5_WIDTH - N_FOX_HEADS), F32)],
                          axis=1).astype(BF16)
    dhn1 = _mm(duf, w_uf, "nt", "in_uf_dx", res=_mm(dqkv, w_qkv, "nt", "in_qkv_dx"), out_dtype=BF16)
    dw_qkv = _mm(hn1, dqkv, "tn", "in_qkv_dw")
    dw_uf = _mm(hn1, duf, "tn", "in_uf_dw")
    g["w_in"] = jnp.concatenate([dw_qkv, dw_uf[:, S5_WIDTH:S5_WIDTH + N_FOX_HEADS], dw_uf[:, :S5_WIDTH]], axis=1)
    (dx,), (g["norm_mix"],) = _rowwise_vjp(_rms, [full(x)], [p["norm_mix"]], [full(dhn1)], "norm_mix_bwd",
                                           adds=[full(dh1)])
    return loss, dx.reshape(seqs, l, d), g


def _place():
    return lax.axis_index("x"), lax.axis_index("y"), lax.axis_index("c")


def _other_chips(x, y):
    return [(1 - x, y), (x, 1 - y), (1 - x, 1 - y)]


ANY = pl.BlockSpec(memory_space=pl.ANY)


def _gather_weights(shards, col_kind, taps):
    n = len(shards)

    def body(*refs):
        ins, tap_in, outs, tap_out = refs[:n], refs[n], refs[n + 1:2 * n + 1], refs[2 * n + 1]
        ici_send, ici_recv, d2d_send, d2d_recv, own_send, own_recv = refs[2 * n + 2:]
        x, y, c = _place()
        mine = 2 * x + y
        chips = _other_chips(x, y)
        sibling = (x, y, 1 - c)

        def piece(a, s, h):
            r, cs = ins[a].shape
            hr = r // 2
            if col_kind[a]:
                return outs[a].at[pl.ds(pl.multiple_of(h * hr, 16), hr), pl.ds(pl.multiple_of(s * cs, 128), cs)]
            return outs[a].at[pl.ds(pl.multiple_of(s * r + h * hr, 16), hr), :]

        def slab(a, s):
            r, cs = ins[a].shape
            if col_kind[a]:
                return outs[a].at[:, pl.ds(pl.multiple_of(s * cs, 128), cs)]
            return outs[a].at[pl.ds(pl.multiple_of(s * r, 16), r), :]

        def own_half(a, h):
            hr = ins[a].shape[0] // 2
            return ins[a].at[pl.ds(pl.multiple_of(h * hr, 16), hr), :]

        sends = []
        for a in range(n):
            cp = pltpu.make_async_remote_copy(
                src_ref=ins[a], dst_ref=slab(a, mine), send_sem=own_send.at[a], recv_sem=own_recv.at[a],
                device_id=sibling, device_id_type=MESH)
            cp.start()
            sends.append(cp)
        cp = pltpu.make_async_remote_copy(
            src_ref=tap_in, dst_ref=tap_out.at[mine], send_sem=own_send.at[n], recv_sem=own_recv.at[n],
            device_id=sibling, device_id_type=MESH)
        cp.start()
        sends.append(cp)
        for a in range(n):
            for j, (px, py) in enumerate(chips):
                cp = pltpu.make_async_remote_copy(
                    src_ref=own_half(a, c), dst_ref=piece(a, mine, c), send_sem=ici_send.at[3 * a + j],
                    recv_sem=ici_recv.at[3 * a + j], device_id=(px, py, c), device_id_type=MESH)
                cp.start()
                sends.append(cp)
        for j, (px, py) in enumerate(chips):
            cp = pltpu.make_async_remote_copy(
                src_ref=tap_in, dst_ref=tap_out.at[mine], send_sem=ici_send.at[3 * n + j],
                recv_sem=ici_recv.at[3 * n + j], device_id=(px, py, c), device_id_type=MESH)
            cp.start()
            sends.append(cp)
        for a in range(n):
            for j, (px, py) in enumerate(chips):
                got = piece(a, 2 * px + py, c)
                pltpu.make_async_remote_copy(
                    src_ref=got, dst_ref=got, send_sem=ici_send.at[3 * a + j], recv_sem=ici_recv.at[3 * a + j],
                    device_id=(px, py, c), device_id_type=MESH).wait_recv()
                fwd = pltpu.make_async_remote_copy(
                    src_ref=got, dst_ref=got, send_sem=d2d_send.at[3 * a + j], recv_sem=d2d_recv.at[3 * a + j],
                    device_id=(x, y, 1 - c), device_id_type=MESH)
                fwd.start()
                sends.append(fwd)
        for a in range(n):
            for j, (px, py) in enumerate(chips):
                other = piece(a, 2 * px + py, 1 - c)
                pltpu.make_async_remote_copy(
                    src_ref=other, dst_ref=other, send_sem=d2d_send.at[3 * a + j], recv_sem=d2d_recv.at[3 * a + j],
                    device_id=(x, y, 1 - c), device_id_type=MESH).wait_recv()
        for j, (px, py) in enumerate(chips):
            pltpu.make_async_remote_copy(
                src_ref=tap_in, dst_ref=tap_out.at[2 * px + py], send_sem=ici_send.at[3 * n + j],
                recv_sem=ici_recv.at[3 * n + j], device_id=(px, py, c), device_id_type=MESH).wait_recv()
        for a in range(n):
            pltpu.make_async_remote_copy(
                src_ref=ins[a], dst_ref=slab(a, mine), send_sem=own_send.at[a], recv_sem=own_recv.at[a],
                device_id=sibling, device_id_type=MESH).wait_recv()
        pltpu.make_async_remote_copy(
            src_ref=tap_in, dst_ref=tap_out.at[mine], send_sem=own_send.at[n], recv_sem=own_recv.at[n],
            device_id=sibling, device_id_type=MESH).wait_recv()
        for cp in sends:
            cp.wait_send()

    def full_shape(a):
        r, cs = shards[a].shape
        return (r, 4 * cs) if col_kind[a] else (4 * r, cs)

    res = pl.pallas_call(
        body, name="gather_weights", in_specs=[ANY] * (n + 1), out_specs=[ANY] * (n + 1),
        out_shape=[jax.ShapeDtypeStruct(full_shape(a), shards[a].dtype) for a in range(n)]
        + [jax.ShapeDtypeStruct((4,) + taps.shape, taps.dtype)],
        scratch_shapes=[pltpu.SemaphoreType.DMA((3 * n + 3,)), pltpu.SemaphoreType.DMA((3 * n + 3,)),
                        pltpu.SemaphoreType.DMA((3 * n,)), pltpu.SemaphoreType.DMA((3 * n,)),
                        pltpu.SemaphoreType.DMA((n + 1,)), pltpu.SemaphoreType.DMA((n + 1,))],
        compiler_params=pltpu.CompilerParams(has_side_effects=True),
    )(*shards, taps)
    return res[:n], res[n]


HBM = pl.BlockSpec(memory_space=pltpu.HBM)
SEM = pl.BlockSpec(memory_space=pltpu.SEMAPHORE)
DATAFLOW = pltpu.SideEffectType.DATAFLOW_SIDE_EFFECTING


def _in_hbm(a):
    return pltpu.with_memory_space_constraint(a, pltpu.HBM)


def _split_start(name, srcs, lands, n_copies, plan):
    n = len(srcs)

    def body(*refs):
        src_refs, land_refs = refs[:n], refs[n:2 * n]
        send_sems, recv_sems = refs[2 * n], refs[2 * n + 1]
        for i, (src, dst, dev) in enumerate(plan(src_refs, land_refs)):
            pltpu.make_async_remote_copy(src_ref=src, dst_ref=dst, send_sem=send_sems.at[i], recv_sem=recv_sems.at[i],
                                         device_id=dev, device_id_type=MESH).start()
        refs[-1][...] = jnp.zeros((8, 128), F32)

    res = pl.pallas_call(
        body, name=name, in_specs=[HBM] * (2 * n),
        out_specs=[SEM, SEM] + [HBM] * (2 * n) + [pl.BlockSpec(memory_space=pltpu.VMEM)],
        out_shape=[pltpu.SemaphoreType.DMA((n_copies,)), pltpu.SemaphoreType.DMA((n_copies,))]
        + [pltpu.HBM(a.shape, a.dtype) for a in list(srcs) + list(lands)] + [jax.ShapeDtypeStruct((8, 128), F32)],
        input_output_aliases={i: 2 + i for i in range(2 * n)},
        compiler_params=pltpu.CompilerParams(has_side_effects=DATAFLOW),
    )(*[_in_hbm(a) for a in list(srcs) + list(lands)])
    return res[0], res[1], list(res[2:2 + n]), list(res[2 + n:2 + 2 * n]), res[-1]


def _split_wait(name, send_sems, recv_sems, srcs, lands, after, plan):
    n = len(srcs)

    def body(*refs):
        src_refs, land_refs = refs[:n], refs[n:2 * n]
        send_ref, recv_ref = refs[2 * n], refs[2 * n + 1]
        for i, (src, dst, dev) in enumerate(plan(src_refs, land_refs)):
            cp = pltpu.make_async_remote_copy(src_ref=src, dst_ref=dst, send_sem=send_ref.at[i], recv_sem=recv_ref.at[i],
                                              device_id=dev, device_id_type=MESH)
            cp.wait_send()
            cp.wait_recv()

    res = pl.pallas_call(
        body, name=name, in_specs=[HBM] * (2 * n) + [SEM, SEM, ANY], out_specs=[HBM] * (2 * n),
        out_shape=[pltpu.HBM(a.shape, a.dtype) for a in list(srcs) + list(lands)],
        input_output_aliases={i: i for i in range(2 * n)},
        compiler_params=pltpu.CompilerParams(has_side_effects=DATAFLOW),
    )(*srcs, *lands, send_sems, recv_sems, after)
    return list(res[:n]), list(res[n:])


def _late_gather_plan(col_kind):
    def plan(src_refs, land_refs):
        x, y, c = _place()
        mine = 2 * x + y
        copies = []
        for a, (src, land) in enumerate(zip(src_refs, land_refs)):
            r, cs = src.shape
            if col_kind[a]:
                dst = land.at[:, pl.ds(pl.multiple_of(mine * cs, 128), cs)]
            else:
                dst = land.at[pl.ds(pl.multiple_of(mine * r, 16), r), :]
            copies.append((src, dst, (x, y, 1 - c)))
            copies += [(src, dst, (px, py, c)) for (px, py) in _other_chips(x, y)]
        return copies
    return plan


def _late_reduce_plan(col_kind):
    def plan(src_refs, land_refs):
        x, y, c = _place()
        copies = []
        for a, (src, land) in enumerate(zip(src_refs, land_refs)):
            for j, (px, py) in enumerate(_other_chips(x, y)):
                if col_kind[a] is None:
                    piece = src
                elif col_kind[a]:
                    cs = land.shape[2]
                    piece = src.at[:, pl.ds(pl.multiple_of((2 * px + py) * cs, 128), cs)]
                else:
                    piece = src.at[2 * px + py]
                copies.append((piece, land.at[j], (px, py, c)))
        return copies
    return plan


def _pair_swap(name, halves):
    n = len(halves)

    def body(*refs):
        ins, outs = refs[:n], refs[n:2 * n]
        send_sems, recv_sems = refs[2 * n:]
        x, y, c = _place()
        copies = []
        for a in range(n):
            cp = pltpu.make_async_remote_copy(
                src_ref=ins[a], dst_ref=outs[a], send_sem=send_sems.at[a], recv_sem=recv_sems.at[a],
                device_id=(x, y, 1 - c), device_id_type=MESH)
            cp.start()
            copies.append(cp)
        for cp in copies:
            cp.wait()

    return pl.pallas_call(
        body, name=name, in_specs=[ANY] * n, out_specs=[ANY] * n,
        out_shape=[jax.ShapeDtypeStruct(s.shape, s.dtype) for s in halves],
        scratch_shapes=[pltpu.SemaphoreType.DMA((n,)), pltpu.SemaphoreType.DMA((n,))],
        compiler_params=pltpu.CompilerParams(has_side_effects=True),
    )(*halves)


def _chip_sum(name, chip_sel, own, col, others):
    _, r, c = others.shape
    tr = _pick(r, (256, 128, 64, 32, 16))
    if col:
        own_spec = pl.BlockSpec((tr, c), lambda i, s: (i, s[0]))
    else:
        own_spec = pl.BlockSpec((None, tr, c), lambda i, s: (s[0], i, 0))
    specs = [own_spec] + [pl.BlockSpec((None, tr, c), lambda i, s, k=k: (k, i, 0)) for k in range(3)]

    def body(s_ref, own_ref, r0, r1, r2, o_ref):
        total = ((own_ref[...].astype(F32) + r0[...].astype(F32)) + r1[...].astype(F32)) + r2[...].astype(F32)
        o_ref[...] = total.astype(o_ref.dtype)

    return pl.pallas_call(
        body, name=name,
        grid_spec=pltpu.PrefetchScalarGridSpec(
            num_scalar_prefetch=1, grid=(r // tr,), in_specs=specs,
            out_specs=pl.BlockSpec((tr, c), lambda i, s: (i, 0))),
        out_shape=jax.ShapeDtypeStruct((r, c), BF16),
        compiler_params=_params(("parallel",)),
    )(chip_sel, own, others, others, others)


def _small_layout(vals):
    sizes = [int(math.prod(v.shape)) for v in vals]
    padded = [-(-s // 128) * 128 for s in sizes]
    return sizes, padded, -(-sum(padded) // 1024) * 1024


def _pack_small(vals):
    sizes, padded, total = _small_layout(vals)
    flat = [jnp.pad(v.reshape(-1), (0, p - s)) for v, s, p in zip(vals, sizes, padded)]
    flat.append(jnp.zeros((total - sum(padded),), F32))
    return jnp.concatenate(flat).reshape(total // 128, 128)


def _allreduce_small(own, others, vals):
    def body(own_ref, oth_ref, out_ref, land, send_sem, recv_sem):
        x, y, c = _place()
        out_ref[...] = (own_ref[...] + oth_ref[0]) + (oth_ref[1] + oth_ref[2])
        cp = pltpu.make_async_remote_copy(
            src_ref=out_ref, dst_ref=land, send_sem=send_sem.at[0], recv_sem=recv_sem.at[0],
            device_id=(x, y, 1 - c), device_id_type=MESH)
        cp.start()
        cp.wait()
        out_ref[...] = out_ref[...] + land[...]

    vm = pl.BlockSpec(memory_space=pltpu.VMEM)
    summed = pl.pallas_call(
        body, name="allreduce_small", in_specs=[vm, vm], out_specs=vm,
        out_shape=jax.ShapeDtypeStruct(own.shape, F32),
        scratch_shapes=[pltpu.VMEM(own.shape, F32), pltpu.SemaphoreType.DMA((1,)), pltpu.SemaphoreType.DMA((1,))],
        compiler_params=pltpu.CompilerParams(has_side_effects=True, vmem_limit_bytes=VMEM_LIMIT_BYTES),
    )(own, others).reshape(-1)
    sizes, padded, _ = _small_layout(vals)
    outs, off = [], 0
    for v, s, p in zip(vals, sizes, padded):
        outs.append(summed[off:off + s].reshape(v.shape))
        off += p
    return outs


def _adamw_math(w, g, m, v):
    m2 = ADAM_B1 * m + (1.0 - ADAM_B1) * g
    v2 = ADAM_B2 * v + (1.0 - ADAM_B2) * (g * g)
    m_hat = m2 / (1.0 - ADAM_B1 ** ADAM_STEP)
    v_hat = v2 / (1.0 - ADAM_B2 ** ADAM_STEP)
    delta = -ADAM_LR * (m_hat / (jnp.sqrt(v_hat) + ADAM_EPS) + ADAM_WD * w)
    return delta, m2, v2


def _adamw_big(name, w, g_mine, g_sibling, m, v):
    _, r, c = w.shape

    def body(w_ref, ga_ref, gb_ref, m_ref, v_ref, go_ref, d_ref, mo_ref, vo_ref):
        gv = ga_ref[...].astype(F32) + gb_ref[...].astype(F32)
        d, m2, v2 = _adamw_math(w_ref[...], gv, m_ref[...], v_ref[...])
        go_ref[...] = gv
        d_ref[...] = d
        mo_ref[...] = m2
        vo_ref[...] = v2

    tr = _pick(r, (256, 128, 64, 32, 16, 8))
    if r % tr == 0 and tr % 8 == 0:
        grid = (r // tr,)
        blk = pl.BlockSpec((None, tr, c), lambda i: (0, i, 0))
        part = pl.BlockSpec((tr, c), lambda i: (i, 0))
    else:
        grid = (c // 512,)
        blk = pl.BlockSpec((None, r, 512), lambda i: (0, 0, i))
        part = pl.BlockSpec((r, 512), lambda i: (0, i))
    return pl.pallas_call(
        body, name=name, grid=grid, in_specs=[blk, part, part, blk, blk], out_specs=[blk] * 4,
        out_shape=[jax.ShapeDtypeStruct((1, r, c), F32)] * 4, compiler_params=_params(("parallel",)),
    )(w, g_mine, g_sibling, m, v)


def _adamw_small(ws, gs, ms, vs):
    n = len(ws)

    def body(*refs):
        w_r, g_r, m_r, v_r = refs[:n], refs[n:2 * n], refs[2 * n:3 * n], refs[3 * n:4 * n]
        o = refs[4 * n:]
        for a in range(n):
            gv = g_r[a][...]
            d, m2, v2 = _adamw_math(w_r[a][...], gv, m_r[a][...], v_r[a][...])
            o[a][...] = gv
            o[n + a][...] = d
            o[2 * n + a][...] = m2
            o[3 * n + a][...] = v2

    res = pl.pallas_call(
        body, name="adamw_small", out_shape=[jax.ShapeDtypeStruct(w.shape, F32) for _ in range(4) for w in ws],
        compiler_params=_params(),
    )(*ws, *gs, *ms, *vs)
    return res[:n], res[n:2 * n], res[2 * n:3 * n], res[3 * n:]


def _full_from_gathered(name, gathered):
    if name == "w_in":
        rows = gathered.shape[0] // 4
        return gathered.reshape(4, rows, gathered.shape[1]).transpose(1, 0, 2).reshape(rows, 4 * gathered.shape[1])
    return gathered


def _reduce_layout(name, full):
    if name in COL_KIND:
        return full
    if name == "w_in":
        rows, cols = full.shape
        return full.reshape(rows, 4, cols // 4).transpose(1, 0, 2)
    return full.reshape(4, full.shape[0] // 4, full.shape[1])


def kernel(x, mem, norm_mix, w_in, fox_q_norm, fox_k_norm, fox_f_bias, s5_a_re, s5_a_im, s5_log_dt, s5_b_re, s5_b_im, s5_c_re, s5_c_im, s5_d, s5_w_glu, s5_b_glu, out_norm_fox, out_norm_s5, w_out, norm_cross, norm_mem, w_xq, w_xkv, xq_norm, xk_norm, w_xo, norm_ffn, w_ffn_up, ffn_conv_w, ffn_conv_b, w_ffn_down, loss_target, m_norm_mix, m_w_in, m_fox_q_norm, m_fox_k_norm, m_fox_f_bias, m_s5_a_re, m_s5_a_im, m_s5_log_dt, m_s5_b_re, m_s5_b_im, m_s5_c_re, m_s5_c_im, m_s5_d, m_s5_w_glu, m_s5_b_glu, m_out_norm_fox, m_out_norm_s5, m_w_out, m_norm_cross, m_norm_mem, m_w_xq, m_w_xkv, m_xq_norm, m_xk_norm, m_w_xo, m_norm_ffn, m_w_ffn_up, m_ffn_conv_w, m_ffn_conv_b, m_w_ffn_down, v_norm_mix, v_w_in, v_fox_q_norm, v_fox_k_norm, v_fox_f_bias, v_s5_a_re, v_s5_a_im, v_s5_log_dt, v_s5_b_re, v_s5_b_im, v_s5_c_re, v_s5_c_im, v_s5_d, v_s5_w_glu, v_s5_b_glu, v_out_norm_fox, v_out_norm_s5, v_w_out, v_norm_cross, v_norm_mem, v_w_xq, v_w_xkv, v_xq_norm, v_xk_norm, v_w_xo, v_norm_ffn, v_w_ffn_up, v_ffn_conv_w, v_ffn_conv_b, v_w_ffn_down):
    given = dict(locals())
    w = {n: given[n] for n in WEIGHTS}
    m = {n: given["m_" + n] for n in WEIGHTS}
    v = {n: given["v_" + n] for n in WEIGHTS}
    xi, yi, _ = _place()
    chip = (2 * xi + yi).astype(jnp.int32)
    chip_sel = chip.reshape(1)

    gathered, taps = _gather_weights([w[FIRST_WEIGHT][0].astype(BF16)], [False], w["ffn_conv_w"][0])
    first_full = gathered[0]
    conv_w = taps.transpose(1, 0, 2).reshape(3, D_FF)
    pending = {}
    g_started = None
    for stage, names in (("mid", MID_WEIGHTS), ("late", LATE_WEIGHTS)):
        kinds = [n in COL_KIND for n in names]
        shards = [w[n][0].astype(BF16) for n in names]
        if g_started is None:
            first_full, shards[0] = lax.optimization_barrier((first_full, shards[0]))
        else:
            shards[0] = shards[0] + g_started[0:1, 0:1].astype(BF16)
        full = [lax.empty((s.shape[0], 4 * s.shape[1]) if ck else (4 * s.shape[0], s.shape[1]), BF16)
                for s, ck in zip(shards, kinds)]
        plan = _late_gather_plan(kinds)
        send, recv, srcs, lands, g_started = _split_start(
            "gather_" + stage + "_start", shards, full, 4 * len(names), plan)
        pending[stage] = (names, plan, send, recv, srcs, lands)
    wb = {FIRST_WEIGHT: _full_from_gathered(FIRST_WEIGHT, first_full)}

    def late_weights(stage, after):
        names, plan, send, recv, srcs, lands = pending[stage]
        _, full = _split_wait("gather_" + stage + "_wait", send, recv, srcs, lands, after, plan)
        return dict(zip(names, full))

    reducing = {}

    def start_reduce(stage, grads_by_name, whole=()):
        names = list(grads_by_name)
        kinds = [n in COL_KIND for n in names]
        grads = [_reduce_layout(n, grads_by_name[n].astype(BF16)) for n in names]
        lands = [lax.empty((3, s.shape[0], s.shape[1] // 4) if ck else (3,) + s.shape[1:], BF16)
                 for s, ck in zip(grads, kinds)]
        plan = _late_reduce_plan(kinds + [None] * len(whole))
        send, recv, srcs, lands, started = _split_start(
            "reduce_" + stage + "_start", grads + list(whole),
            lands + [lax.empty((3,) + a.shape, a.dtype) for a in whole], 3 * (len(names) + len(whole)), plan)
        reducing[stage] = (names, kinds, plan, send, recv, srcs, lands)
        return started

    p = {n: w[n][0] for n in SMALL}
    p["ffn_conv_w"] = conv_w
    for n in ("norm_mix", "fox_q_norm", "fox_k_norm", "fox_f_bias", "s5_b_glu", "out_norm_fox", "out_norm_s5",
              "norm_cross", "norm_mem", "xq_norm", "xk_norm", "norm_ffn", "ffn_conv_b"):
        p[n] = p[n].reshape(1, -1)
    p["norm_mix"] = p["norm_mix"] + g_started[0:1, 0:1]
    loss, grad_x, g = _local_step(x, mem, loss_target, p, wb, late_weights, start_reduce)

    small_names = list(SMALL) + ["ffn_conv_w"]
    small_vals = [g[n].reshape(w[n].shape if n != "ffn_conv_w" else (1, 3, D_FF)) for n in small_names] + [loss]
    after = start_reduce("first", {FIRST_WEIGHT: g[FIRST_WEIGHT]}, whole=[_pack_small(small_vals)])

    out_g, out_d, out_m, out_v = {}, {}, {}, {}

    def finish(stage, after):
        names, kinds, plan, send, recv, srcs, lands = reducing[stage]
        sums, from_chips = _split_wait("reduce_" + stage + "_wait", send, recv, srcs, lands, after, plan)
        mine = [_chip_sum("reduce_chip_sum_" + n, chip_sel, ps, ck, fc)
                for n, ps, fc, ck in zip(names, sums, from_chips, kinds)]
        theirs = _pair_swap("reduce_pair_swap_" + stage, mine)
        for n, a, b in zip(names, mine, theirs):
            if n == "w_in":
                flip = lambda t: jnp.swapaxes(t, -1, -2)
                res = _adamw_big("adamw_" + n, flip(w[n]), flip(a), flip(b), flip(m[n]), flip(v[n]))
                out_g[n], out_d[n], out_m[n], out_v[n] = (flip(t) for t in res)
                continue
            out_g[n], out_d[n], out_m[n], out_v[n] = _adamw_big("adamw_" + n, w[n], a, b, m[n], v[n])
        return sums[len(names):], from_chips[len(names):], out_v[names[-1]]

    _, _, after = finish("late", after)
    _, _, after = finish("mid", after)
    (small_own,), (small_others,), _ = finish("first", after)

    reduced = _allreduce_small(small_own, small_others, small_vals)
    loss_all = reduced[-1].reshape(())
    conv_w_grad = lax.dynamic_slice_in_dim(reduced[-2], chip * (D_FF // 4), D_FF // 4, axis=2)
    sg, sd, sm, sv = _adamw_small(
        [w[n] for n in small_names], list(reduced[:len(SMALL)]) + [conv_w_grad],
        [m[n] for n in small_names], [v[n] for n in small_names])
    out_g.update(zip(small_names, sg))
    out_d.update(zip(small_names, sd))
    out_m.update(zip(small_names, sm))
    out_v.update(zip(small_names, sv))

    return (loss_all, grad_x, *[out_g[n] for n in WEIGHTS], *[out_d[n] for n in WEIGHTS],
            *[out_m[n] for n in WEIGHTS], *[out_v[n] for n in WEIGHTS])
```

```python
import math

import jax
import jax.numpy as jnp
from jax import lax
from jax.experimental import pallas as pl
from jax.experimental.pallas import tpu as pltpu

F32 = jnp.float32
BF16 = jnp.bfloat16

D_MODEL = 1024
FOX_WIDTH = 512
HEAD_DIM = 64
N_FOX_HEADS = 8
S5_WIDTH = 512
S5_GROUP_CH = 16
S5_GROUPS = 32
S5_STATE = 64
S5_CH = S5_GROUPS * S5_STATE
N_X_HEADS = 4
X_HEAD_DIM = 256
N_MEM = 256
D_FF = 2816
UF_COLS = 640
EPS = 1e-6
ADAM_LR = 0.001
ADAM_B1 = 0.9
ADAM_B2 = 0.999
ADAM_EPS = 1e-08
ADAM_WD = 0.01
ADAM_STEP = 10

VMEM_LIMIT_BYTES = 56 * 1024 * 1024
MM_BLOCK_BYTES = 6 * 1024 * 1024
MM_VMEM_BYTES = 40 * 1024 * 1024
MM_TILE_MAX = 1536
MESH = pl.DeviceIdType.MESH

FIRST_WEIGHT = "w_in"
MID_WEIGHTS = ("s5_w_glu", "w_out")
EARLY_WEIGHTS = (FIRST_WEIGHT,) + MID_WEIGHTS
LATE_WEIGHTS = ("w_xq", "w_xkv", "w_xo", "w_ffn_up", "w_ffn_down")
BIG = EARLY_WEIGHTS + LATE_WEIGHTS
COL_KIND = ("w_xkv", "w_ffn_up")
SMALL = ("norm_mix", "fox_q_norm", "fox_k_norm", "fox_f_bias", "s5_a_re", "s5_a_im", "s5_log_dt",
         "s5_b_re", "s5_b_im", "s5_c_re", "s5_c_im", "s5_d", "s5_b_glu", "out_norm_fox", "out_norm_s5",
         "norm_cross", "norm_mem", "xq_norm", "xk_norm", "norm_ffn", "ffn_conv_b")
WEIGHTS = ("norm_mix", "w_in", "fox_q_norm", "fox_k_norm", "fox_f_bias", "s5_a_re", "s5_a_im", "s5_log_dt",
           "s5_b_re", "s5_b_im", "s5_c_re", "s5_c_im", "s5_d", "s5_w_glu", "s5_b_glu", "out_norm_fox",
           "out_norm_s5", "w_out", "norm_cross", "norm_mem", "w_xq", "w_xkv", "xq_norm", "xk_norm", "w_xo",
           "norm_ffn", "w_ffn_up", "ffn_conv_w", "ffn_conv_b", "w_ffn_down")


def _params(sem=None):
    return pltpu.CompilerParams(dimension_semantics=sem, vmem_limit_bytes=VMEM_LIMIT_BYTES)


def _pick(n, cands):
    for c in cands:
        if n % c == 0:
            return c
    return n


_DIMS = {"nn": (((1,), (0,)), ((), ())), "nt": (((1,), (1,)), ((), ())), "tn": (((0,), (0,)), ((), ()))}


def _mm(a, b, mode, name, out_dtype=F32, res=None):
    if mode == "nn":
        (m, k), (k2, n) = a.shape, b.shape
    elif mode == "nt":
        (m, k), (n, k2) = a.shape, b.shape
    else:
        (k, m), (k2, n) = a.shape, b.shape
    assert k == k2, (name, a.shape, b.shape)

    has_res = res is not None
    a_size, b_size = a.dtype.itemsize, b.dtype.itemsize
    o_size = jnp.dtype(out_dtype).itemsize + (res.dtype.itemsize if has_res else 0)

    def tiles(dim):
        return [c for c in range(MM_TILE_MAX, 0, -128) if dim % c == 0] or [dim]

    best = None
    for tm in tiles(m):
        for tn in tiles(n):
            a_blk, b_blk = tm * k * a_size, tn * k * b_size
            if max(a_blk, b_blk) > MM_BLOCK_BYTES or 2 * (a_blk + b_blk + tm * tn * o_size) > MM_VMEM_BYTES:
                continue
            for rows_outer in (True, False):
                moved = (m * k * a_size + (m // tm) * n * k * b_size) if rows_outer else \
                        (n * k * b_size + (n // tn) * m * k * a_size)
                key = (moved, -(tm * tn))
                if best is None or key < best[0]:
                    best = (key, tm, tn, rows_outer)
    assert best is not None, (name, a.shape, b.shape)
    _, tm, tn, rows_outer = best
    ij = (lambda g0, g1: (g0, g1)) if rows_outer else (lambda g0, g1: (g1, g0))
    if mode == "tn":
        a_spec = pl.BlockSpec((k, tm), lambda g0, g1: (0, ij(g0, g1)[0]))
    else:
        a_spec = pl.BlockSpec((tm, k), lambda g0, g1: (ij(g0, g1)[0], 0))
    if mode == "nt":
        b_spec = pl.BlockSpec((tn, k), lambda g0, g1: (ij(g0, g1)[1], 0))
    else:
        b_spec = pl.BlockSpec((k, tn), lambda g0, g1: (0, ij(g0, g1)[1]))
    o_spec = pl.BlockSpec((tm, tn), lambda g0, g1: ij(g0, g1))
    grid = (m // tm, n // tn) if rows_outer else (n // tn, m // tm)
    dims = _DIMS[mode]

    def body(*refs):
        a_ref, b_ref = refs[0], refs[1]
        o_ref = refs[-1]
        acc = lax.dot_general(a_ref[...].astype(BF16), b_ref[...].astype(BF16), dims, preferred_element_type=F32)
        if has_res:
            acc = acc + refs[2][...].astype(F32)
        o_ref[...] = acc.astype(o_ref.dtype)

    return pl.pallas_call(
        body, name=name, grid=grid,
        in_specs=[a_spec, b_spec] + ([o_spec] if has_res else []),
        out_specs=o_spec, out_shape=jax.ShapeDtypeStruct((m, n), out_dtype),
        compiler_params=_params(("parallel", "parallel")),
    )(*((a, b, res) if has_res else (a, b)))


def _row_spec(tm, bc, off, step):
    return pl.BlockSpec((tm, bc), lambda i, h: (i, off + step * h))


ROW_TILE_ELEMS = 512 * 1024


def _row_tile(t, rows):
    widest = max(bc for (_, bc, _, _) in rows)
    return _pick(t, (min(t, ROW_TILE_ELEMS // widest), 512, 256, 128, 64, 8))


def _rowwise(fn, rows, pars, outs, name, heads=1):
    t = rows[0][0].shape[0]
    tm = _row_tile(t, rows)
    nr, npar = len(rows), len(pars)

    def body(*refs):
        vals = [r[...].astype(F32) for r in refs[:nr + npar]]
        res = fn(*vals)
        if not isinstance(res, (tuple, list)):
            res = (res,)
        for o_ref, v in zip(refs[nr + npar:], res):
            o_ref[...] = v.astype(o_ref.dtype)

    in_specs = [_row_spec(tm, bc, off, st) for (_, bc, off, st) in rows]
    in_specs += [pl.BlockSpec(p.shape, lambda i, h: (0, 0)) for p in pars]
    out_specs = [_row_spec(tm, bc, 0, st) for (_, bc, st, _) in outs]
    out_shape = [jax.ShapeDtypeStruct((t, c), dt) for (c, _, _, dt) in outs]
    res = pl.pallas_call(
        body, name=name, grid=(t // tm, heads), in_specs=in_specs, out_specs=out_specs, out_shape=out_shape,
        compiler_params=_params(("parallel", "parallel")),
    )(*[r[0] for r in rows], *pars)
    return res[0] if len(res) == 1 else res


def _rowwise_vjp(fn, rows, pars, cts, name, heads=1, adds=None, row_dtypes=None):
    t = rows[0][0].shape[0]
    tm = _row_tile(t, rows)
    nr, npar, nct = len(rows), len(pars), len(cts)
    adds = adds or [None] * nr
    add_list = [a for a in adds if a is not None]
    row_dtypes = row_dtypes or [F32] * nr

    def body(*refs):
        i, h = pl.program_id(0), pl.program_id(1)
        p = 0
        row_v = [r[...].astype(F32) for r in refs[p:p + nr]]; p += nr
        par_v = [r[...].astype(F32) for r in refs[p:p + npar]]; p += npar
        ct_v = [r[...].astype(F32) for r in refs[p:p + nct]]; p += nct
        add_refs = refs[p:p + len(add_list)]; p += len(add_list)
        drow_refs = refs[p:p + nr]; p += nr
        dpar_refs = refs[p:p + npar]

        def wrapped(*a):
            r = fn(*a)
            return tuple(r) if isinstance(r, (tuple, list)) else (r,)

        _, pull = jax.vjp(wrapped, *row_v, *par_v)
        grads = pull(tuple(ct_v))
        ai = 0
        for k in range(nr):
            g = grads[k]
            if adds[k] is not None:
                g = g + add_refs[ai][...].astype(F32)
                ai += 1
            drow_refs[k][...] = g.astype(drow_refs[k].dtype)

        @pl.when((i == 0) & (h == 0))
        def _():
            for r in dpar_refs:
                r[...] = jnp.zeros(r.shape, r.dtype)

        for k in range(npar):
            dpar_refs[k][...] += grads[nr + k]

    in_specs = [_row_spec(tm, bc, off, st) for (_, bc, off, st) in rows]
    in_specs += [pl.BlockSpec(q.shape, lambda i, h: (0, 0)) for q in pars]
    in_specs += [_row_spec(tm, bc, off, st) for (_, bc, off, st) in cts]
    in_specs += [_row_spec(tm, bc, off, st) for (_, bc, off, st) in add_list]
    out_specs = [_row_spec(tm, bc, 0, st) for (_, bc, _, st) in rows]
    out_specs += [pl.BlockSpec(q.shape, lambda i, h: (0, 0)) for q in pars]
    out_shape = [jax.ShapeDtypeStruct((t, bc * (heads if st else 1)), dt) for (_, bc, _, st), dt in zip(rows, row_dtypes)]
    out_shape += [jax.ShapeDtypeStruct(q.shape, F32) for q in pars]
    res = pl.pallas_call(
        body, name=name, grid=(t // tm, heads), in_specs=in_specs, out_specs=out_specs, out_shape=out_shape,
        compiler_params=_params(("arbitrary", "arbitrary")),
    )(*[r[0] for r in rows], *pars, *[c[0] for c in cts], *[a[0] for a in add_list])
    return list(res[:nr]), list(res[nr:])


def _rms(x, g):
    return x * lax.rsqrt(jnp.mean(x * x, axis=-1, keepdims=True) + EPS) * g


def _rms_pair(x, g):
    left = lax.broadcasted_iota(jnp.int32, x.shape, 1) < HEAD_DIM
    x2 = x * x
    ms_a = jnp.sum(jnp.where(left, x2, 0.0), axis=-1, keepdims=True) * (1.0 / HEAD_DIM)
    ms_b = jnp.sum(jnp.where(left, 0.0, x2), axis=-1, keepdims=True) * (1.0 / HEAD_DIM)
    return x * lax.rsqrt(jnp.where(left, ms_a, ms_b) + EPS) * g


def _gelu(x):
    return 0.5 * x * (1.0 + jnp.tanh(math.sqrt(2.0 / math.pi) * (x + 0.044715 * (x * x * x))))


def _s5_act(ys, u, d):
    return _gelu(ys + d * u)


def _s5_gate(yg, z, b, g):
    return _rms(yg * jax.nn.sigmoid(z + b), g)


def _lane_cumsum(x, reverse):
    n = x.shape[-1]
    lane = lax.broadcasted_iota(jnp.int32, x.shape, 1)
    k = 1
    while k < n:
        if reverse:
            x = x + jnp.where(lane < n - k, pltpu.roll(x, n - k, 1), 0.0)
        else:
            x = x + jnp.where(lane >= k, pltpu.roll(x, k, 1), 0.0)
        k *= 2
    return x


def _log_sigmoid(z):
    return jnp.minimum(z, 0.0) - jnp.log(1.0 + jnp.exp(-jnp.abs(z)))


def _forget_fwd(f, bias):
    def body(f_ref, b_ref, c_ref):
        c_ref[...] = _lane_cumsum(_log_sigmoid(f_ref[...] + b_ref[...]), False)

    return pl.pallas_call(body, name="forget_fwd", out_shape=jax.ShapeDtypeStruct(f.shape, F32),
                          compiler_params=_params())(f, bias)


def _forget_bwd(f, bias, dc):
    def body(f_ref, b_ref, dc_ref, df_ref, db_ref):
        dlog = _lane_cumsum(dc_ref[...], True)
        df = dlog * jax.nn.sigmoid(-(f_ref[...] + b_ref[...]))
        df_ref[...] = df
        db_ref[...] = jnp.sum(df, axis=1, keepdims=True)

    return pl.pallas_call(body, name="forget_bwd",
                          out_shape=(jax.ShapeDtypeStruct(f.shape, F32), jax.ShapeDtypeStruct(bias.shape, F32)),
                          compiler_params=_params())(f, bias, dc)


FOX_BLOCK = 1024
FOX_KEYS = 1024
FOX_BWD_BLOCK = 512
_NT = _DIMS["nt"]
_TN = _DIMS["tn"]


N_PAIRS = N_FOX_HEADS // 2
V_BLOCK0 = 2 * N_PAIRS


def _left_lanes(shape):
    return lax.broadcasted_iota(jnp.int32, shape, 1) < HEAD_DIM


def _top_rows(shape):
    return lax.broadcasted_iota(jnp.int32, shape, 0) < HEAD_DIM


def _wide(c_tile, n):
    return c_tile if n == 128 else jnp.concatenate([c_tile] * (n // 128), axis=1)


def _fox_fwd(qn, kn, qkv, c_wide, seqs):
    t = qn.shape[0]
    l = t // seqs
    tb = min(FOX_BLOCK, l)
    tk = min(FOX_KEYS, tb)
    ratio = tb // tk
    nb = l // tb
    scale = HEAD_DIM ** -0.5

    def body(q_ref, k_ref, v_ref, ca_ref, cb_ref, o_ref, lse_ref, vt_ref):
        i = pl.program_id(2)
        top = _top_rows((128, tb))

        @pl.when(i == 0)
        def _():
            vt_ref[...] = v_ref[...].T.astype(BF16)

        qt = (q_ref[...].astype(F32) * scale).T.astype(BF16)
        zero = jnp.zeros_like(qt)
        qts = (jnp.where(top, qt, zero), jnp.where(top, zero, qt))
        top_k = _top_rows((128, tk))
        zero_k = jnp.zeros((128, tk), BF16)
        key_pos = lax.broadcasted_iota(jnp.int32, (tk, tb), 0)
        query_pos = lax.broadcasted_iota(jnp.int32, (tk, tb), 1)
        c_refs = (ca_ref, cb_ref)

        def scores(j):
            off = pl.multiple_of(j * tk, tk)
            k2 = k_ref[pl.ds(off, tk), :]
            return tuple(jnp.dot(k2, qts[h], preferred_element_type=F32) - _wide(c_refs[h][pl.ds(off, tk), :], tb)
                         for h in (0, 1))

        def values_times(ps, j):
            vt = vt_ref[:, pl.ds(pl.multiple_of(j * tk, tk), tk)]
            return (jnp.dot(jnp.where(top_k, vt, zero_k), ps[0], preferred_element_type=F32)
                    + jnp.dot(jnp.where(top_k, zero_k, vt), ps[1], preferred_element_type=F32))

        def softmax_step(sts, stats, first_key):
            ps, new, alphas = [], [], []
            for st, (m, s_sum) in zip(sts, stats):
                if first_key is not None:
                    st = jnp.where(key_pos + first_key <= query_pos, st, -jnp.inf)
                m_new = jnp.maximum(m, jnp.max(st, axis=0, keepdims=True))
                alpha = jnp.exp(m - m_new)
                p = jnp.exp(st - m_new)
                new.append((m_new, alpha * s_sum + jnp.sum(p, axis=0, keepdims=True)))
                alphas.append(alpha)
                ps.append(p.astype(BF16))
            return tuple(ps), tuple(new), jnp.where(top, alphas[0], alphas[1])

        def tile(j, carry, first_key):
            stats, acc = carry
            ps, stats, alpha = softmax_step(scores(j), stats, first_key)
            return stats, alpha * acc + values_times(ps, j)

        stat = (jnp.full((1, tb), -jnp.inf, F32), jnp.zeros((1, tb), F32))
        below = i * ratio
        carry = lax.fori_loop(0, below, lambda j, c: tile(j, c, None), ((stat, stat), jnp.zeros((128, tb), F32)))
        for r in range(ratio):
            carry = tile(below + r, carry, r * tk)
        ((ma, sa), (mb, sb)), acc = carry
        o_ref[...] = (acc / jnp.where(top, sa, sb)).T
        lse_ref[0:1, :] = ma + jnp.log(sa)
        lse_ref[1:2, :] = mb + jnp.log(sb)

    qblk = pl.BlockSpec((tb, 128), lambda b, hp, i: (b * nb + i, hp))
    return pl.pallas_call(
        body, name="fox_fwd", grid=(seqs, N_PAIRS, nb),
        in_specs=[qblk, pl.BlockSpec((l, 128), lambda b, hp, i: (b, hp)),
                  pl.BlockSpec((l, 128), lambda b, hp, i: (b, V_BLOCK0 + hp)),
                  pl.BlockSpec((None, l, 128), lambda b, hp, i: (b * N_FOX_HEADS + 2 * hp, 0, 0)),
                  pl.BlockSpec((None, l, 128), lambda b, hp, i: (b * N_FOX_HEADS + 2 * hp + 1, 0, 0))],
        out_specs=[qblk, pl.BlockSpec((None, 2, tb), lambda b, hp, i: (b * N_PAIRS + hp, 0, i))],
        out_shape=[jax.ShapeDtypeStruct((t, FOX_WIDTH), F32), jax.ShapeDtypeStruct((seqs * N_PAIRS, 2, l), F32)],
        scratch_shapes=[pltpu.VMEM((128, l), BF16)],
        compiler_params=_params(("parallel", "parallel", "arbitrary")),
    )(qn, kn, qkv, c_wide, c_wide)


def _fox_bwd(qn, kn, qkv, c_wide, o, do, lse, seqs):
    t = qn.shape[0]
    l = t // seqs
    tb = min(FOX_BWD_BLOCK, l)
    nb = l // tb
    scale = HEAD_DIM ** -0.5
    one_at = (HEAD_DIM, 0)

    def body(q_ref, k_ref, v_ref, ca_ref, cb_ref, o_ref, do_ref, lse_ref, dq_ref, dk_ref, dv_ref, dc_ref, dcq_ref,
             qt_ref, kt_ref, dot_ref, delta_ref, dqa_ref, dqb_ref):
        top_l = _top_rows((128, l))
        top = _top_rows((128, tb))
        left = _left_lanes((tb, 128))
        row_id = lax.broadcasted_iota(jnp.int32, (128, tb), 0)
        lane_id = lax.broadcasted_iota(jnp.int32, (tb, 128), 1)
        zero_t = jnp.zeros((128, tb), BF16)
        zero_l = jnp.zeros((tb, 128), BF16)
        rows = lambda a: (jnp.where(top, a, zero_t), jnp.where(top, zero_t, a))
        lanes = lambda a: (jnp.where(left, a, zero_l), jnp.where(left, zero_l, a))
        with_one_row = lambda pair: tuple(jnp.where(row_id == one_at[h], 1.0, pair[h]).astype(BF16) for h in (0, 1))
        with_one_lane = lambda pair: tuple(jnp.where(lane_id == one_at[h], 1.0, pair[h]).astype(BF16) for h in (0, 1))
        causal = lax.broadcasted_iota(jnp.int32, (tb, tb), 0) <= lax.broadcasted_iota(jnp.int32, (tb, tb), 1)
        c_refs = (ca_ref, cb_ref)
        dq_refs = (dqa_ref, dqb_ref)

        qt_ref[...] = (q_ref[...].astype(F32) * scale).T.astype(BF16)
        kt_ref[...] = k_ref[...].astype(F32).T.astype(BF16)
        do_t = do_ref[...].T
        dot_ref[...] = do_t.astype(BF16)
        prod_t = do_t * o_ref[...].T
        delta_ref[0:1, :] = jnp.sum(jnp.where(top_l, prod_t, 0.0), axis=0, keepdims=True)
        delta_ref[1:2, :] = jnp.sum(jnp.where(top_l, 0.0, prod_t), axis=0, keepdims=True)
        dqa_ref[...] = jnp.zeros(dqa_ref.shape, F32)
        dqb_ref[...] = jnp.zeros(dqb_ref.shape, F32)

        def kv_block(j, _):
            koff = pl.multiple_of(j * tb, tb)
            k2 = k_ref[pl.ds(koff, tb), :]
            v2 = v_ref[pl.ds(koff, tb), :].astype(BF16)
            kts = with_one_row(rows(kt_ref[:, pl.ds(koff, tb)]))
            cw = tuple(_wide(c_refs[h][pl.ds(koff, tb), :], tb) for h in (0, 1))

            def q_block(i, carry, masked):
                dks, dv = list(carry[:2]), carry[2]
                qoff = pl.multiple_of(i * tb, tb)
                qs = lanes((q_ref[pl.ds(qoff, tb), :].astype(F32) * scale).astype(BF16))
                qs_one = with_one_lane(qs)
                dos = lanes(do_ref[pl.ds(qoff, tb), :].astype(BF16))
                qts = rows(qt_ref[:, pl.ds(qoff, tb)])
                dots = rows(dot_ref[:, pl.ds(qoff, tb)])
                for h in (0, 1):
                    st = jnp.dot(k2, qts[h], preferred_element_type=F32) - cw[h]
                    p = jnp.exp(st - lse_ref[h:h + 1, pl.ds(qoff, tb)])
                    if masked:
                        p = jnp.where(causal, p, 0.0)
                    dp = jnp.dot(v2, dots[h], preferred_element_type=F32)
                    dsb = (p * (dp - delta_ref[h:h + 1, pl.ds(qoff, tb)])).astype(BF16)
                    dv = dv + jnp.dot(p.astype(BF16), dos[h], preferred_element_type=F32)
                    dks[h] = dks[h] + jnp.dot(dsb, qs_one[h], preferred_element_type=F32)
                    dq_refs[h][:, pl.ds(qoff, tb)] += jnp.dot(kts[h], dsb, preferred_element_type=F32)
                return dks[0], dks[1], dv

            z = jnp.zeros((tb, 128), F32)
            carry = q_block(j, (z, z, z), True)
            rest = nb - 1 - j
            carry = lax.fori_loop(
                0, rest // 2, lambda n, c: q_block(j + 2 + 2 * n, q_block(j + 1 + 2 * n, c, False), False), carry)
            dka, dkb, dv = lax.cond(rest % 2 == 1, lambda c: q_block(nb - 1, c, False), lambda c: c, carry)
            dk_ref[pl.ds(koff, tb), :] = jnp.where(left, dka, dkb)
            dv_ref[pl.ds(koff, tb), :] = dv
            dc_ref[0:1, pl.ds(koff, tb)] = -dka.T[one_at[0]:one_at[0] + 1, :]
            dc_ref[1:2, pl.ds(koff, tb)] = -dkb.T[one_at[1]:one_at[1] + 1, :]
            return 0

        lax.fori_loop(0, nb, kv_block, 0)
        dq_ref[...] = (jnp.where(top_l, dqa_ref[...], dqb_ref[...]) * scale).T
        dcq_ref[0:1, :] = dqa_ref[one_at[0]:one_at[0] + 1, :]
        dcq_ref[1:2, :] = dqb_ref[one_at[1]:one_at[1] + 1, :]

    blk = pl.BlockSpec((l, 128), lambda b, hp: (b, hp))
    cspec = lambda k: pl.BlockSpec((None, l, 128), lambda b, hp: (b * N_FOX_HEADS + 2 * hp + k, 0, 0))
    rows2 = pl.BlockSpec((None, 2, l), lambda b, hp: (b * N_PAIRS + hp, 0, 0))
    wide = jax.ShapeDtypeStruct((t, FOX_WIDTH), F32)
    pair_rows = jax.ShapeDtypeStruct((seqs * N_PAIRS, 2, l), F32)
    return pl.pallas_call(
        body, name="fox_bwd", grid=(seqs, N_PAIRS),
        in_specs=[blk, blk, pl.BlockSpec((l, 128), lambda b, hp: (b, V_BLOCK0 + hp)), cspec(0), cspec(1), blk, blk, rows2],
        out_specs=[blk, blk, blk, rows2, rows2],
        out_shape=[wide, wide, wide, pair_rows, pair_rows],
        scratch_shapes=[pltpu.VMEM((128, l), BF16), pltpu.VMEM((128, l), BF16), pltpu.VMEM((128, l), BF16),
                        pltpu.VMEM((2, l), F32), pltpu.VMEM((128, l), F32), pltpu.VMEM((128, l), F32)],
        compiler_params=_params(("parallel", "parallel")),
    )(qn, kn, qkv, c_wide, c_wide, o, do, lse)


SCAN_ROWS = 512
SCAN_COLS = 1024


S5_IN = 128
S5_ST = 512
SCAN_CHUNKS = SCAN_COLS // S5_ST
SCAN_SEGS = 8
LANES = 128


def _cmul(ar, ai, br, bi):
    return ar * br - ai * bi, ar * bi + ai * br


def _powers_into(pw_r, pw_i, a_r, a_i, seg):
    pw_r[0:1, :] = a_r
    pw_i[0:1, :] = a_i
    for k in range(1, seg):
        pr, pi = _cmul(pw_r[k - 1:k, :], pw_i[k - 1:k, :], a_r, a_i)
        pw_r[k:k + 1, :] = pr
        pw_i[k:k + 1, :] = pi


def _interleave(dst, src, seg):
    for h in range(src.shape[0]):
        for j in range(seg):
            dst[h, j * SCAN_SEGS:(j + 1) * SCAN_SEGS, :] = src[h, pl.ds(j, SCAN_SEGS, stride=seg), :]


def _deinterleave(dst, src, seg):
    for h in range(src.shape[0]):
        for j in range(seg):
            dst[h, pl.ds(j, SCAN_SEGS, stride=seg), :] = src[h, j * SCAN_SEGS:(j + 1) * SCAN_SEGS, :]


def _interleaved(ref, tmp_a, tmp_b, seg):
    n = ref.shape[1] // LANES
    for h in range(n):
        tmp_a[h] = ref[:, h * LANES:(h + 1) * LANES].astype(F32)
    _interleave(tmp_b, tmp_a, seg)
    return jnp.concatenate([tmp_b[h] for h in range(n)], axis=1)


def _store_deinterleaved(ref, val, tmp_a, tmp_b, seg):
    n = ref.shape[1] // LANES
    for h in range(n):
        tmp_a[h] = val[:, h * LANES:(h + 1) * LANES]
    _deinterleave(tmp_b, tmp_a, seg)
    for h in range(n):
        ref[:, h * LANES:(h + 1) * LANES] = tmp_b[h]


def _segment_scan(b_r, b_i, x_r, x_i, pw_r, pw_i, car_r, car_i, seg, sign, reverse, visit=None):
    nc = b_r.shape[0]
    sub = lax.broadcasted_iota(jnp.int32, (SCAN_SEGS, LANES), 0)
    lanes = lambda c: slice(c * LANES, (c + 1) * LANES)
    rows = lambda j: pl.ds(pl.multiple_of(((seg - 1 - j) if reverse else j) * SCAN_SEGS, SCAN_SEGS), SCAN_SEGS)
    a1 = [(pw_r[0:1, lanes(c)], sign * pw_i[0:1, lanes(c)]) for c in range(nc)]

    def local(j, xs):
        out = []
        for c in range(nc):
            xr, xi = xs[2 * c], xs[2 * c + 1]
            nr = a1[c][0] * xr - a1[c][1] * xi + b_r[c, rows(j), :]
            ni = a1[c][0] * xi + a1[c][1] * xr + b_i[c, rows(j), :]
            x_r[c, rows(j), :] = nr
            x_i[c, rows(j), :] = ni
            out += [nr, ni]
        return tuple(out)

    zero = jnp.zeros((SCAN_SEGS, LANES), F32)
    ends = lax.fori_loop(0, seg, local, (zero,) * (2 * nc))

    if reverse:
        first = sub == SCAN_SEGS - 1
        neighbour = lambda v: pltpu.roll(v, SCAN_SEGS - 1, 0)
        shift = lambda v, d: jnp.where(sub < SCAN_SEGS - d, pltpu.roll(v, SCAN_SEGS - d, 0), 0.0)
    else:
        first = sub == 0
        neighbour = lambda v: pltpu.roll(v, 1, 0)
        shift = lambda v, d: jnp.where(sub >= d, pltpu.roll(v, d, 0), 0.0)
    last = 0 if reverse else SCAN_SEGS - 1
    entries = []
    for c in range(nc):
        er, ei = ends[2 * c], ends[2 * c + 1]
        pr, pi = pw_r[seg - 1:seg, lanes(c)], sign * pw_i[seg - 1:seg, lanes(c)]
        yr = jnp.where(first, car_r[:, lanes(c)], neighbour(er))
        yi = jnp.where(first, car_i[:, lanes(c)], neighbour(ei))
        qr, qi = pr, pi
        for d in (1, 2, 4):
            mr, mi = _cmul(qr, qi, shift(yr, d), shift(yi, d))
            yr, yi = yr + mr, yi + mi
            qr, qi = _cmul(qr, qi, qr, qi)
        lr, li = _cmul(pr, pi, yr, yi)
        car_r[:, lanes(c)] = (er + lr)[last:last + 1, :]
        car_i[:, lanes(c)] = (ei + li)[last:last + 1, :]
        entries += [yr, yi]

    def correct(j, prev):
        out = []
        row_r, row_i = pw_r[pl.ds(j, 1), :], sign * pw_i[pl.ds(j, 1), :]
        for c in range(nc):
            mr, mi = _cmul(row_r[:, lanes(c)], row_i[:, lanes(c)], entries[2 * c], entries[2 * c + 1])
            nr = x_r[c, rows(j), :] + mr
            ni = x_i[c, rows(j), :] + mi
            x_r[c, rows(j), :] = nr
            x_i[c, rows(j), :] = ni
            if visit is not None:
                visit(c, rows(j), prev[2 * c], prev[2 * c + 1])
            out += [nr, ni]
        return tuple(out)

    lax.fori_loop(0, seg, correct, tuple(entries))


def _s5_fwd(uf, bbr, bbi, cr, ci, ar, ai, seqs):
    t = uf.shape[0]
    l = t // seqs
    tl = min(SCAN_ROWS, l)
    nl = l // tl
    seg = tl // SCAN_SEGS
    cb, nq = SCAN_COLS, SCAN_CHUNKS
    nc = cb // LANES
    per = S5_ST // LANES

    def body(u_ref, bbr_ref, bbi_ref, cr_ref, ci_ref, ar_ref, ai_ref, x_r, x_i, ys_ref,
             car_r, car_i, pw_r, pw_i, b_r, b_i, tmp_a, tmp_b):
        @pl.when(pl.program_id(2) == 0)
        def _():
            car_r[...] = jnp.zeros(car_r.shape, F32)
            car_i[...] = jnp.zeros(car_i.shape, F32)
            _powers_into(pw_r, pw_i, ar_ref[...], ai_ref[...], seg)

        u = _interleaved(u_ref, tmp_a, tmp_b, seg).astype(BF16)
        for q in range(nq):
            uq = u[:, q * S5_IN:(q + 1) * S5_IN]
            br = jnp.dot(uq, bbr_ref[q], preferred_element_type=F32)
            bi = jnp.dot(uq, bbi_ref[q], preferred_element_type=F32)
            for s in range(per):
                b_r[q * per + s] = br[:, s * LANES:(s + 1) * LANES]
                b_i[q * per + s] = bi[:, s * LANES:(s + 1) * LANES]
        _segment_scan(b_r, b_i, x_r, x_i, pw_r, pw_i, car_r, car_i, seg, 1.0, False)
        wide = lambda buf, q: jnp.concatenate([buf[q * per + s] for s in range(per)], axis=1).astype(BF16)
        ys = [jnp.dot(wide(x_r, q), cr_ref[q], preferred_element_type=F32)
              + jnp.dot(wide(x_i, q), ci_ref[q], preferred_element_type=F32) for q in range(nq)]
        _store_deinterleaved(ys_ref, jnp.concatenate(ys, axis=1), tmp_a, tmp_b, seg)

    rows = lambda w: pl.BlockSpec((tl, w), lambda s, j, r: (s * nl + r, j))
    state = pl.BlockSpec((nc, tl, LANES), lambda s, j, r: (j, s * nl + r, 0))
    chunk = lambda a: pl.BlockSpec((nq,) + a.shape[1:], lambda s, j, r: (j, 0, 0))
    par = pl.BlockSpec((1, cb), lambda s, j, r: (0, j))
    return pl.pallas_call(
        body, name="s5_fwd", grid=(seqs, S5_CH // cb, nl),
        in_specs=[rows(nq * S5_IN), chunk(bbr), chunk(bbi), chunk(cr), chunk(ci), par, par],
        out_specs=[state, state, rows(nq * S5_IN)],
        out_shape=[jax.ShapeDtypeStruct((S5_CH // LANES, t, LANES), F32)] * 2
        + [jax.ShapeDtypeStruct((t, S5_WIDTH), F32)],
        scratch_shapes=[pltpu.VMEM((1, cb), F32), pltpu.VMEM((1, cb), F32), pltpu.VMEM((seg, cb), F32),
                        pltpu.VMEM((seg, cb), F32)] + [pltpu.VMEM((nc, tl, LANES), F32)] * 2
        + [pltpu.VMEM((nq * S5_IN // LANES, tl, LANES), F32)] * 2,
        compiler_params=_params(("parallel", "parallel", "arbitrary")),
    )(uf, bbr, bbi, cr, ci, ar, ai)


def _s5_bwd(dys, uf, xr, xi, bbr, bbi, cr, ci, ar, ai, seqs):
    t = dys.shape[0]
    l = t // seqs
    tl = min(SCAN_ROWS, l)
    nl = l // tl
    seg = tl // SCAN_SEGS
    cb, nq = SCAN_COLS, SCAN_CHUNKS
    nc = cb // LANES
    per = S5_ST // LANES

    def body(dy_ref, u_ref, x_r, x_i, bbr_ref, bbi_ref, cr_ref, ci_ref, ar_ref, ai_ref,
             du_ref, dbbr_ref, dbbi_ref, dcr_ref, dci_ref, dar_ref, dai_ref,
             car_r, car_i, pw_r, pw_i, g_r, g_i, lam_r, lam_i, acc_r, acc_i, tmp_a, tmp_b):
        @pl.when(pl.program_id(2) == 0)
        def _():
            car_r[...] = jnp.zeros(car_r.shape, F32)
            car_i[...] = jnp.zeros(car_i.shape, F32)
            _powers_into(pw_r, pw_i, ar_ref[...], ai_ref[...], seg)
            for acc_ref in (dbbr_ref, dbbi_ref, dcr_ref, dci_ref, dar_ref, dai_ref):
                acc_ref[...] = jnp.zeros(acc_ref.shape, F32)

        dy = _interleaved(dy_ref, tmp_a, tmp_b, seg).astype(BF16)
        for q in range(nq):
            dyq = dy[:, q * S5_IN:(q + 1) * S5_IN]
            gr = lax.dot_general(dyq, cr_ref[q], _NT, preferred_element_type=F32)
            gi = lax.dot_general(dyq, ci_ref[q], _NT, preferred_element_type=F32)
            for s in range(per):
                g_r[q * per + s] = gr[:, s * LANES:(s + 1) * LANES]
                g_i[q * per + s] = gi[:, s * LANES:(s + 1) * LANES]
        acc_r[...] = jnp.zeros(acc_r.shape, F32)
        acc_i[...] = jnp.zeros(acc_i.shape, F32)

        def visit(c, rws, lr, li):
            xr_t, xi_t = x_r[c, rws, :], x_i[c, rws, :]
            acc_r[c] += lr * xr_t + li * xi_t
            acc_i[c] += li * xr_t - lr * xi_t

        _segment_scan(g_r, g_i, lam_r, lam_i, pw_r, pw_i, car_r, car_i, seg, -1.0, True, visit)
        for c in range(nc):
            dar_ref[:, c * LANES:(c + 1) * LANES] += jnp.sum(acc_r[c], axis=0, keepdims=True)
            dai_ref[:, c * LANES:(c + 1) * LANES] += jnp.sum(acc_i[c], axis=0, keepdims=True)
        u = _interleaved(u_ref, tmp_a, tmp_b, seg).astype(BF16)
        wide = lambda buf, q: jnp.concatenate([buf[q * per + s] for s in range(per)], axis=1).astype(BF16)
        du = []
        for q in range(nq):
            io = slice(q * S5_IN, (q + 1) * S5_IN)
            lq_r, lq_i = wide(lam_r, q), wide(lam_i, q)
            du.append(lax.dot_general(lq_r, bbr_ref[q], _NT, preferred_element_type=F32)
                      + lax.dot_general(lq_i, bbi_ref[q], _NT, preferred_element_type=F32))
            dbbr_ref[q] += lax.dot_general(u[:, io], lq_r, _TN, preferred_element_type=F32)
            dbbi_ref[q] += lax.dot_general(u[:, io], lq_i, _TN, preferred_element_type=F32)
            dcr_ref[q] += lax.dot_general(wide(x_r, q), dy[:, io], _TN, preferred_element_type=F32)
            dci_ref[q] += lax.dot_general(wide(x_i, q), dy[:, io], _TN, preferred_element_type=F32)
        _store_deinterleaved(du_ref, jnp.concatenate(du, axis=1), tmp_a, tmp_b, seg)

    rows = lambda w: pl.BlockSpec((tl, w), lambda s, j, r: (s * nl + nl - 1 - r, j))
    state = pl.BlockSpec((nc, tl, LANES), lambda s, j, r: (j, s * nl + nl - 1 - r, 0))
    chunk = lambda a: pl.BlockSpec((nq,) + a.shape[1:], lambda s, j, r: (j, 0, 0))
    acc = lambda a: pl.BlockSpec((None, nq) + a.shape[1:], lambda s, j, r: (s, j, 0, 0))
    par = pl.BlockSpec((1, cb), lambda s, j, r: (0, j))
    par_acc = pl.BlockSpec((None, 1, cb), lambda s, j, r: (s, 0, j))
    per_seq = lambda a: jax.ShapeDtypeStruct((seqs,) + a.shape, F32)
    return pl.pallas_call(
        body, name="s5_bwd", grid=(seqs, S5_CH // cb, nl),
        in_specs=[rows(nq * S5_IN), rows(nq * S5_IN), state, state, chunk(bbr), chunk(bbi), chunk(cr), chunk(ci),
                  par, par],
        out_specs=[rows(nq * S5_IN), acc(bbr), acc(bbi), acc(cr), acc(ci), par_acc, par_acc],
        out_shape=[jax.ShapeDtypeStruct((t, S5_WIDTH), F32), per_seq(bbr), per_seq(bbi), per_seq(cr), per_seq(ci),
                   jax.ShapeDtypeStruct((seqs, 1, S5_CH), F32), jax.ShapeDtypeStruct((seqs, 1, S5_CH), F32)],
        scratch_shapes=[pltpu.VMEM((1, cb), F32), pltpu.VMEM((1, cb), F32), pltpu.VMEM((seg, cb), F32),
                        pltpu.VMEM((seg, cb), F32)] + [pltpu.VMEM((nc, tl, LANES), F32)] * 4
        + [pltpu.VMEM((nc, SCAN_SEGS, LANES), F32)] * 2 + [pltpu.VMEM((nq * S5_IN // LANES, tl, LANES), F32)] * 2,
        compiler_params=_params(("parallel", "parallel", "arbitrary")),
    )(dys, uf, xr, xi, bbr, bbi, cr, ci, ar, ai)


XATT_BLOCK = 2048


def _xatt_probs(qv, kv):
    s = lax.dot_general(qv, kv, _NT, preferred_element_type=F32) * (X_HEAD_DIM ** -0.5)
    e = jnp.exp(s - jnp.max(s, axis=-1, keepdims=True))
    return e / jnp.sum(e, axis=-1, keepdims=True)


def _xatt_fwd(q, k, kv, seqs):
    t = q.shape[0]
    tq = min(XATT_BLOCK, t // seqs)
    nq = t // seqs // tq

    def body(q_ref, k_ref, v_ref, o_ref):
        p = _xatt_probs(q_ref[...], k_ref[...])
        o_ref[...] = jnp.dot(p.astype(BF16), v_ref[...].astype(BF16), preferred_element_type=F32).astype(o_ref.dtype)

    qs = pl.BlockSpec((tq, X_HEAD_DIM), lambda b, h, i: (b * nq + i, h))
    return pl.pallas_call(
        body, name="xatt_fwd", grid=(seqs, N_X_HEADS, nq),
        in_specs=[qs, pl.BlockSpec((N_MEM, X_HEAD_DIM), lambda b, h, i: (b, h)),
                  pl.BlockSpec((N_MEM, X_HEAD_DIM), lambda b, h, i: (b, N_X_HEADS + h))],
        out_specs=qs, out_shape=jax.ShapeDtypeStruct(q.shape, BF16),
        compiler_params=_params(("parallel", "parallel", "parallel")),
    )(q, k, kv)


def _xatt_bwd(q, k, kv, do, seqs):
    t = q.shape[0]
    tq = min(XATT_BLOCK, t // seqs)
    nq = t // seqs // tq
    scale = X_HEAD_DIM ** -0.5

    def body(q_ref, k_ref, v_ref, do_ref, dq_ref, dk_ref, dv_ref):
        @pl.when(pl.program_id(2) == 0)
        def _():
            dk_ref[...] = jnp.zeros(dk_ref.shape, F32)
            dv_ref[...] = jnp.zeros(dv_ref.shape, F32)

        qv, kk = q_ref[...], k_ref[...]
        p = _xatt_probs(qv, kk)
        dob = do_ref[...].astype(BF16)
        dp = lax.dot_general(dob, v_ref[...].astype(BF16), _NT, preferred_element_type=F32)
        ds = p * (dp - jnp.sum(dp * p, axis=-1, keepdims=True))
        dsb = ds.astype(BF16)
        dq_ref[...] = jnp.dot(dsb, kk, preferred_element_type=F32) * scale
        dk_ref[...] += lax.dot_general(dsb, qv, _TN, preferred_element_type=F32) * scale
        dv_ref[...] += lax.dot_general(p.astype(BF16), dob, _TN, preferred_element_type=F32)

    qs = pl.BlockSpec((tq, X_HEAD_DIM), lambda b, h, i: (b * nq + i, h))
    ks = pl.BlockSpec((N_MEM, X_HEAD_DIM), lambda b, h, i: (b, h))
    return pl.pallas_call(
        body, name="xatt_bwd", grid=(seqs, N_X_HEADS, nq),
        in_specs=[qs, ks, pl.BlockSpec((N_MEM, X_HEAD_DIM), lambda b, h, i: (b, N_X_HEADS + h)), qs],
        out_specs=[qs, ks, ks],
        out_shape=[jax.ShapeDtypeStruct(q.shape, F32), jax.ShapeDtypeStruct(k.shape, F32),
                   jax.ShapeDtypeStruct(k.shape, F32)],
        compiler_params=_params(("parallel", "parallel", "arbitrary")),
    )(q, k, kv, do)


CONV_COLS = 256


def _shift_down(x, k, row):
    return jnp.where(row >= k, pltpu.roll(x, k, 0), 0.0)


def _shift_up(x, k, row):
    n = x.shape[0]
    return jnp.where(row < n - k, pltpu.roll(x, n - k, 0), 0.0)


def _down_from(x, prev, k, row):
    return jnp.where(row >= k, pltpu.roll(x, k, 0), pltpu.roll(prev, k, 0))


GATE_ROWS = 512


def _ffn_up_gate(hn, w_up, w, b, seqs):
    t = hn.shape[0]
    l = t // seqs
    nc = D_FF // CONV_COLS

    rc = min(GATE_ROWS, l)

    def body(a_ref, wg_ref, wu_ref, w_ref, b_ref, g_ref, u_ref, o_ref):
        wv, bias = w_ref[...], b_ref[...]
        row = lax.broadcasted_iota(jnp.int32, (rc, CONV_COLS), 0)
        prev = jnp.zeros((rc, CONV_COLS), F32)
        for k in range(l // rc):
            rows = slice(k * rc, (k + 1) * rc)
            a = a_ref[rows, :]
            gb = jnp.dot(a, wg_ref[...], preferred_element_type=F32).astype(BF16)
            ub = jnp.dot(a, wu_ref[...], preferred_element_type=F32).astype(BF16)
            g_ref[rows, :] = gb
            u_ref[rows, :] = ub
            g = gb.astype(F32)
            pre = bias + wv[0:1, :] * _down_from(g, prev, 2, row) + wv[1:2, :] * _down_from(g, prev, 1, row) \
                + wv[2:3, :] * g
            o_ref[rows, :] = (pre * jax.nn.sigmoid(pre) * ub.astype(F32)).astype(o_ref.dtype)
            prev = g

    cols = pl.BlockSpec((l, CONV_COLS), lambda s, j: (s, j))
    half = jax.ShapeDtypeStruct((t, D_FF), BF16)
    return pl.pallas_call(
        body, name="ffn_up_gate", grid=(seqs, nc),
        in_specs=[pl.BlockSpec((l, hn.shape[1]), lambda s, j: (s, 0)),
                  pl.BlockSpec((hn.shape[1], CONV_COLS), lambda s, j: (0, j)),
                  pl.BlockSpec((hn.shape[1], CONV_COLS), lambda s, j: (0, nc + j)),
                  pl.BlockSpec((3, CONV_COLS), lambda s, j: (0, j)), pl.BlockSpec((1, CONV_COLS), lambda s, j: (0, j))],
        out_specs=[cols, cols, cols], out_shape=[half, half, half],
        compiler_params=_params(("parallel", "parallel")),
    )(hn, w_up, w_up, w, b)


def _ffn_down_dx_gate(dh, w_down, gate, up, w, b, seqs):
    t = dh.shape[0]
    l = t // seqs
    nc = D_FF // CONV_COLS
    steps = nc * seqs

    def body(dh_ref, wd_ref, g_ref, u_ref, w_ref, b_ref, dgu_ref, dw_ref, db_ref, stage, sems):
        s, j = pl.program_id(0), pl.program_id(1)
        n = s * nc + j
        slot = n % 2

        def copies(slot_, j_, s_):
            rows = pl.ds(pl.multiple_of(s_ * l, 16), l)
            return [pltpu.make_async_copy(
                stage.at[slot_, half],
                dgu_ref.at[rows, pl.ds(pl.multiple_of((half * nc + j_) * CONV_COLS, 128), CONV_COLS)],
                sems.at[slot_, half]) for half in (0, 1)]

        @pl.when(n >= 2)
        def _():
            for cp in copies(slot, j, s):
                cp.wait()

        da = lax.dot_general(dh_ref[...], wd_ref[...], _NT, preferred_element_type=F32)
        g, wv = g_ref[...].astype(F32), w_ref[...]
        row = lax.broadcasted_iota(jnp.int32, g.shape, 0)
        g1, g2 = _shift_down(g, 1, row), _shift_down(g, 2, row)
        pre = b_ref[...] + wv[0:1, :] * g2 + wv[1:2, :] * g1 + wv[2:3, :] * g
        sg = jax.nn.sigmoid(pre)
        silu = pre * sg
        stage[slot, 1] = (da * silu).astype(stage.dtype)
        dpre = da * u_ref[...].astype(F32) * (sg * (1.0 + pre * (1.0 - sg)))
        dg = wv[2:3, :] * dpre + wv[1:2, :] * _shift_up(dpre, 1, row) + wv[0:1, :] * _shift_up(dpre, 2, row)
        stage[slot, 0] = dg.astype(stage.dtype)
        for cp in copies(slot, j, s):
            cp.start()
        dw_ref[0:1, :] = jnp.sum(dpre * g2, axis=0, keepdims=True)
        dw_ref[1:2, :] = jnp.sum(dpre * g1, axis=0, keepdims=True)
        dw_ref[2:3, :] = jnp.sum(dpre * g, axis=0, keepdims=True)
        db_ref[...] = jnp.sum(dpre, axis=0, keepdims=True)

        @pl.when(n == steps - 1)
        def _():
            for cp in copies(slot, j, s) + (copies(1 - slot, j, s) if steps > 1 else []):
                cp.wait()

    cols = pl.BlockSpec((l, CONV_COLS), lambda s, j: (s, j))
    return pl.pallas_call(
        body, name="ffn_down_dx_gate", grid=(seqs, nc),
        in_specs=[pl.BlockSpec((l, dh.shape[1]), lambda s, j: (s, 0)),
                  pl.BlockSpec((CONV_COLS, dh.shape[1]), lambda s, j: (j, 0)), cols, cols,
                  pl.BlockSpec((3, CONV_COLS), lambda s, j: (0, j)), pl.BlockSpec((1, CONV_COLS), lambda s, j: (0, j))],
        out_specs=[ANY, pl.BlockSpec((None, 3, CONV_COLS), lambda s, j: (s, 0, j)),
                   pl.BlockSpec((None, 1, CONV_COLS), lambda s, j: (s, 0, j))],
        out_shape=[jax.ShapeDtypeStruct((t, 2 * D_FF), BF16), jax.ShapeDtypeStruct((seqs, 3, D_FF), F32),
                   jax.ShapeDtypeStruct((seqs, 1, D_FF), F32)],
        scratch_shapes=[pltpu.VMEM((2, 2, l, CONV_COLS), BF16), pltpu.SemaphoreType.DMA((2, 2))],
        compiler_params=_params(("arbitrary", "arbitrary")),
    )(dh, w_down, gate, up, w, b)


def _loss_head(h, target):
    t, d = h.shape
    tm = _pick(t, (256, 128, 8))

    def body(h_ref, t_ref, dh_ref, dhb_ref, loss_ref):
        @pl.when(pl.program_id(0) == 0)
        def _():
            loss_ref[...] = jnp.zeros(loss_ref.shape, F32)

        e = h_ref[...] - t_ref[...]
        dh = e * (1.0 / d)
        dh_ref[...] = dh
        dhb_ref[...] = dh.astype(BF16)
        loss_ref[...] += (0.5 / d) * jnp.sum(jnp.sum(e * e, axis=1, keepdims=True), axis=0, keepdims=True)

    blk = pl.BlockSpec((tm, d), lambda i: (i, 0))
    return pl.pallas_call(
        body, name="loss_head", grid=(t // tm,), in_specs=[blk, blk],
        out_specs=[blk, blk, pl.BlockSpec((1, 1), lambda i: (0, 0))],
        out_shape=[jax.ShapeDtypeStruct((t, d), F32), jax.ShapeDtypeStruct((t, d), BF16),
                   jax.ShapeDtypeStruct((1, 1), F32)],
        compiler_params=_params(("arbitrary",)),
    )(h, target)


def _s5_discretise(a_re, a_im, log_dt, b_re, b_im):
    dt = jnp.exp(log_dt)[:, None]
    mag = jnp.exp(a_re * dt)
    lb_r = mag * jnp.cos(a_im * dt)
    lb_i = mag * jnp.sin(a_im * dt)
    den = a_re * a_re + a_im * a_im
    nr = lb_r - 1.0
    coef_r = (nr * a_re + lb_i * a_im) / den
    coef_i = (lb_i * a_re - nr * a_im) / den
    bb_r = coef_r[:, :, None] * b_re - coef_i[:, :, None] * b_im
    bb_i = coef_r[:, :, None] * b_im + coef_i[:, :, None] * b_re
    return lb_r, lb_i, bb_r, bb_i


S5_CHUNKS = 4
S5_PER = S5_GROUPS // S5_CHUNKS


def _blockdiag_in(bb):
    eye = jnp.eye(S5_PER, dtype=bb.dtype)
    return jnp.einsum("jgpc,gh->jgchp", bb.reshape(S5_CHUNKS, S5_PER, S5_STATE, S5_GROUP_CH), eye).reshape(
        S5_CHUNKS, S5_PER * S5_GROUP_CH, S5_PER * S5_STATE)


def _blockdiag_in_grad(d):
    eye = jnp.eye(S5_PER, dtype=d.dtype)
    return jnp.einsum("jgchp,gh->jgpc", d.reshape(S5_CHUNKS, S5_PER, S5_GROUP_CH, S5_PER, S5_STATE), eye).reshape(
        S5_GROUPS, S5_STATE, S5_GROUP_CH)


def _blockdiag_out(c):
    eye = jnp.eye(S5_PER, dtype=c.dtype)
    return jnp.einsum("jgcp,gh->jgphc", c.reshape(S5_CHUNKS, S5_PER, S5_GROUP_CH, S5_STATE), eye).reshape(
        S5_CHUNKS, S5_PER * S5_STATE, S5_PER * S5_GROUP_CH)


def _blockdiag_out_grad(d):
    eye = jnp.eye(S5_PER, dtype=d.dtype)
    return jnp.einsum("jgphc,gh->jgcp", d.reshape(S5_CHUNKS, S5_PER, S5_STATE, S5_PER, S5_GROUP_CH), eye).reshape(
        S5_GROUPS, S5_GROUP_CH, S5_STATE)


def _local_step(x3, mem3, target3, p, wb, late_weights=None, early_grads=None):
    seqs, l, d = x3.shape
    t = seqs * l
    x = x3.reshape(t, d)
    mem = mem3.reshape(seqs * N_MEM, d)
    target = target3.reshape(t, d)
    full = lambda a: (a, a.shape[1], 0, 0)

    s5_in = (p["s5_a_re"], p["s5_a_im"], p["s5_log_dt"], p["s5_b_re"], p["s5_b_im"])
    (lb_r, lb_i, bb_r, bb_i), s5_pull = jax.vjp(_s5_discretise, *s5_in)
    ar, ai = lb_r.reshape(1, S5_CH), lb_i.reshape(1, S5_CH)
    bbr_d, bbi_d = _blockdiag_in(bb_r).astype(BF16), _blockdiag_in(bb_i).astype(BF16)
    cr_d, ci_d = _blockdiag_out(p["s5_c_re"]).astype(BF16), (-_blockdiag_out(p["s5_c_im"])).astype(BF16)
    d_row = p["s5_d"].reshape(1, S5_WIDTH)

    hn1 = _rowwise(_rms, [full(x)], [p["norm_mix"]], [(d, d, 0, BF16)], "norm_mix_fwd")
    if late_weights is not None:
        wb = dict(wb, **late_weights("first", hn1))
    conv_w = wb["ffn_conv_w"] if "ffn_conv_w" in wb else p["ffn_conv_w"]
    w_in = wb["w_in"]
    w_qkv = w_in[:, :3 * FOX_WIDTH]
    w_uf = jnp.concatenate(
        [w_in[:, 3 * FOX_WIDTH + N_FOX_HEADS:], w_in[:, 3 * FOX_WIDTH:3 * FOX_WIDTH + N_FOX_HEADS],
         jnp.zeros((d, UF_COLS - S5_WIDTH - N_FOX_HEADS), w_in.dtype)], axis=1)
    qkv = _mm(hn1, w_qkv, "nn", "in_qkv")
    uf = _mm(hn1, w_uf, "nn", "in_uf")

    bh = seqs * N_FOX_HEADS
    q_pair = (qkv, 128, 0, 1)
    k_pair = (qkv, 128, N_PAIRS, 1)
    gq2, gk2 = jnp.tile(p["fox_q_norm"], (1, 2)), jnp.tile(p["fox_k_norm"], (1, 2))
    pair_out = [(FOX_WIDTH, 128, 1, BF16)]
    qn = _rowwise(_rms_pair, [q_pair], [gq2], pair_out, "fox_qnorm_fwd", heads=N_PAIRS)
    kn = _rowwise(_rms_pair, [k_pair], [gk2], pair_out, "fox_knorm_fwd", heads=N_PAIRS)

    f_rows = uf[:, S5_WIDTH:S5_WIDTH + N_FOX_HEADS].reshape(seqs, l, N_FOX_HEADS).transpose(0, 2, 1).reshape(bh, l)
    f_bias = jnp.tile(p["fox_f_bias"].reshape(N_FOX_HEADS, 1), (seqs, 1))
    c_wide = jnp.broadcast_to(_forget_fwd(f_rows, f_bias)[:, :, None], (bh, l, 128))
    fox, lse = _fox_fwd(qn, kn, qkv, c_wide, seqs)

    xr, xi, ys = _s5_fwd(uf, bbr_d, bbi_d, cr_d, ci_d, ar, ai, seqs)
    u_blk = (uf, S5_WIDTH, 0, 0)
    yg = _rowwise(_s5_act, [full(ys), u_blk], [d_row], [(S5_WIDTH, S5_WIDTH, 0, F32)], "s5_act_fwd")
    if late_weights is not None:
        wb = dict(wb, **late_weights("mid", yg))
    z = _mm(yg, wb["s5_w_glu"], "nn", "s5_glu")
    y2n = _rowwise(_s5_gate, [full(yg), full(z)], [p["s5_b_glu"], p["out_norm_s5"]],
                   [(S5_WIDTH, S5_WIDTH, 0, BF16)], "s5_gate_fwd")
    foxn = _rowwise(_rms, [full(fox)], [p["out_norm_fox"]], [(FOX_WIDTH, FOX_WIDTH, 0, BF16)], "fox_outnorm_fwd")
    mixed = jnp.concatenate([foxn, y2n], axis=1)
    h1 = _mm(mixed, wb["w_out"], "nn", "mix_out", res=x)
    if late_weights is not None:
        wb = dict(wb, **late_weights("late", h1))

    hn2 = _rowwise(_rms, [full(h1)], [p["norm_cross"]], [(d, d, 0, BF16)], "norm_cross_fwd")
    mn = _rowwise(_rms, [full(mem)], [p["norm_mem"]], [(d, d, 0, BF16)], "norm_mem_fwd")
    xq_raw = _mm(hn2, wb["w_xq"], "nn", "x_q")
    kv = _mm(mn, wb["w_xkv"], "nn", "x_kv")
    xh = lambda a: (a, X_HEAD_DIM, 0, 1)
    xqn = _rowwise(_rms, [xh(xq_raw)], [p["xq_norm"]], [(d, X_HEAD_DIM, 1, BF16)], "x_qnorm_fwd", heads=N_X_HEADS)
    xkn = _rowwise(_rms, [xh(kv)], [p["xk_norm"]], [(d, X_HEAD_DIM, 1, BF16)], "x_knorm_fwd", heads=N_X_HEADS)
    xo = _xatt_fwd(xqn, xkn, kv, seqs)
    h2 = _mm(xo, wb["w_xo"], "nn", "x_out", res=h1)

    hn3 = _rowwise(_rms, [full(h2)], [p["norm_ffn"]], [(d, d, 0, BF16)], "norm_ffn_fwd")
    gate, up, act = _ffn_up_gate(hn3, wb["w_ffn_up"], conv_w, p["ffn_conv_b"], seqs)
    h3 = _mm(act, wb["w_ffn_down"], "nn", "ffn_down", res=h2)
    dh3, dh3_b, loss = _loss_head(h3, target)

    g = {}
    late_dt = BF16 if early_grads is not None else F32
    g["w_ffn_down"] = _mm(act, dh3_b, "tn", "ffn_down_dw", out_dtype=late_dt)
    dgu, dconv_w, dconv_b = _ffn_down_dx_gate(dh3_b, wb["w_ffn_down"], gate, up, conv_w, p["ffn_conv_b"], seqs)
    g["ffn_conv_w"], g["ffn_conv_b"] = jnp.sum(dconv_w, axis=0), jnp.sum(dconv_b, axis=0)
    dhn3 = _mm(dgu, wb["w_ffn_up"], "nt", "ffn_up_dx", out_dtype=BF16)
    g["w_ffn_up"] = _mm(hn3, dgu, "tn", "ffn_up_dw", out_dtype=late_dt)
    (dh2,), (g["norm_ffn"],) = _rowwise_vjp(_rms, [full(h2)], [p["norm_ffn"]], [full(dhn3)], "norm_ffn_bwd",
                                            adds=[full(dh3)])

    dxo = _mm(dh2, wb["w_xo"], "nt", "x_out_dx", out_dtype=BF16)
    g["w_xo"] = _mm(xo, dh2, "tn", "x_out_dw", out_dtype=late_dt)
    dxqn, dxkn, dxv = _xatt_bwd(xqn, xkn, kv, dxo, seqs)
    (dxq_raw,), (g["xq_norm"],) = _rowwise_vjp(_rms, [xh(xq_raw)], [p["xq_norm"]], [xh(dxqn)], "x_qnorm_bwd",
                                               heads=N_X_HEADS, row_dtypes=[BF16])
    (dxk_raw,), (g["xk_norm"],) = _rowwise_vjp(_rms, [xh(kv)], [p["xk_norm"]], [xh(dxkn)], "x_knorm_bwd",
                                               heads=N_X_HEADS, row_dtypes=[BF16])
    dkv = jnp.concatenate([dxk_raw, dxv.astype(BF16)], axis=1)
    dhn2 = _mm(dxq_raw, wb["w_xq"], "nt", "x_q_dx", out_dtype=BF16)
    g["w_xq"] = _mm(hn2, dxq_raw, "tn", "x_q_dw", out_dtype=late_dt)
    dmn = _mm(dkv, wb["w_xkv"], "nt", "x_kv_dx")
    g["w_xkv"] = _mm(mn, dkv, "tn", "x_kv_dw", out_dtype=late_dt)
    norm_cross = p["norm_cross"]
    if early_grads is not None:
        norm_cross = norm_cross + early_grads("late", {n: g[n] for n in LATE_WEIGHTS})[0:1, 0:1]
    (dh1,), (g["norm_cross"],) = _rowwise_vjp(_rms, [full(h1)], [norm_cross], [full(dhn2)], "norm_cross_bwd",
                                              adds=[full(dh2)])
    _, (g["norm_mem"],) = _rowwise_vjp(_rms, [full(mem)], [p["norm_mem"]], [full(dmn)], "norm_mem_bwd",
                                       row_dtypes=[BF16])

    dmixed = _mm(dh1, wb["w_out"], "nt", "mix_out_dx", out_dtype=BF16)
    g["w_out"] = _mm(mixed, dh1, "tn", "mix_out_dw", out_dtype=late_dt)
    (dfox,), (g["out_norm_fox"],) = _rowwise_vjp(_rms, [full(fox)], [p["out_norm_fox"]],
                                                 [(dmixed, FOX_WIDTH, 0, 0)], "fox_outnorm_bwd")
    (dyg_a, dz), (g["s5_b_glu"], g["out_norm_s5"]) = _rowwise_vjp(
        _s5_gate, [full(yg), full(z)], [p["s5_b_glu"], p["out_norm_s5"]], [(dmixed, S5_WIDTH, 1, 0)], "s5_gate_bwd",
        row_dtypes=[F32, BF16])
    dyg = _mm(dz, wb["s5_w_glu"], "nt", "s5_glu_dx", res=dyg_a)
    g["s5_w_glu"] = _mm(yg, dz, "tn", "s5_glu_dw", out_dtype=late_dt)
    if early_grads is not None:
        d_row = d_row + early_grads("mid", {n: g[n] for n in MID_WEIGHTS})[0:1, 0:1]
    (dys, du_a), (dd_row,) = _rowwise_vjp(_s5_act, [full(ys), u_blk], [d_row], [full(dyg)], "s5_act_bwd",
                                          row_dtypes=[BF16, F32])
    g["s5_d"] = dd_row
    du_b, dbbr_d, dbbi_d, dcr_d, dci_d, dar, dai = _s5_bwd(dys, uf, xr, xi, bbr_d, bbi_d, cr_d, ci_d, ar, ai, seqs)
    dbbr_d, dbbi_d, dcr_d, dci_d = (jnp.sum(a, axis=0) for a in (dbbr_d, dbbi_d, dcr_d, dci_d))
    d_lb_r = jnp.sum(dar, axis=0).reshape(S5_GROUPS, S5_STATE)
    d_lb_i = jnp.sum(dai, axis=0).reshape(S5_GROUPS, S5_STATE)
    g["s5_a_re"], g["s5_a_im"], g["s5_log_dt"], g["s5_b_re"], g["s5_b_im"] = s5_pull(
        (d_lb_r, d_lb_i, _blockdiag_in_grad(dbbr_d), _blockdiag_in_grad(dbbi_d)))
    g["s5_c_re"] = _blockdiag_out_grad(dcr_d)
    g["s5_c_im"] = -_blockdiag_out_grad(dci_d)

    dqn, dkn, dv, dc, dcq = _fox_bwd(qn, kn, qkv, c_wide, fox, dfox, lse, seqs)
    pair = lambda a: (a, 128, 0, 1)
    (dq_raw,), (dgq2,) = _rowwise_vjp(_rms_pair, [q_pair], [gq2], [pair(dqn)], "fox_qnorm_bwd", heads=N_PAIRS,
                                      row_dtypes=[BF16])
    (dk_raw,), (dgk2,) = _rowwise_vjp(_rms_pair, [k_pair], [gk2], [pair(dkn)], "fox_knorm_bwd", heads=N_PAIRS,
                                      row_dtypes=[BF16])
    g["fox_q_norm"] = dgq2[:, :HEAD_DIM] + dgq2[:, HEAD_DIM:]
    g["fox_k_norm"] = dgk2[:, :HEAD_DIM] + dgk2[:, HEAD_DIM:]
    df_rows, dfb = _forget_bwd(f_rows, f_bias, (dc + dcq).reshape(bh, l))
    g["fox_f_bias"] = jnp.sum(dfb.reshape(seqs, N_FOX_HEADS), axis=0)
    df = df_rows.reshape(seqs, N_FOX_HEADS, l).transpose(0, 2, 1).reshape(t, N_FOX_HEADS)
    dqkv = jnp.concatenate([dq_raw, dk_raw, dv.astype(BF16)], axis=1)
    duf = jnp.concatenate([du_a + du_b, df, jnp.zeros((t, UF_COLS - S5_WIDTH - N_FOX_HEADS), F32)],
                          axis=1).astype(BF16)
    dhn1 = _mm(duf, w_uf, "nt", "in_uf_dx", res=_mm(dqkv, w_qkv, "nt", "in_qkv_dx"), out_dtype=BF16)
    dw_qkv = _mm(hn1, dqkv, "tn", "in_qkv_dw")
    dw_uf = _mm(hn1, duf, "tn", "in_uf_dw")
    g["w_in"] = jnp.concatenate([dw_qkv, dw_uf[:, S5_WIDTH:S5_WIDTH + N_FOX_HEADS], dw_uf[:, :S5_WIDTH]], axis=1)
    (dx,), (g["norm_mix"],) = _rowwise_vjp(_rms, [full(x)], [p["norm_mix"]], [full(dhn1)], "norm_mix_bwd",
                                           adds=[full(dh1)])
    return loss, dx.reshape(seqs, l, d), g


def _place():
    return lax.axis_index("x"), lax.axis_index("y"), lax.axis_index("c")


def _other_chips(x, y):
    return [(1 - x, y), (x, 1 - y), (1 - x, 1 - y)]


ANY = pl.BlockSpec(memory_space=pl.ANY)


def _gather_weights(shards, col_kind, taps):
    n = len(shards)

    def body(*refs):
        ins, tap_in, outs, tap_out = refs[:n], refs[n], refs[n + 1:2 * n + 1], refs[2 * n + 1]
        ici_send, ici_recv, d2d_send, d2d_recv, own_send, own_recv = refs[2 * n + 2:]
        x, y, c = _place()
        mine = 2 * x + y
        chips = _other_chips(x, y)
        sibling = (x, y, 1 - c)

        def piece(a, s, h):
            r, cs = ins[a].shape
            hr = r // 2
            if col_kind[a]:
                return outs[a].at[pl.ds(pl.multiple_of(h * hr, 16), hr), pl.ds(pl.multiple_of(s * cs, 128), cs)]
            return outs[a].at[pl.ds(pl.multiple_of(s * r + h * hr, 16), hr), :]

        def slab(a, s):
            r, cs = ins[a].shape
            if col_kind[a]:
                return outs[a].at[:, pl.ds(pl.multiple_of(s * cs, 128), cs)]
            return outs[a].at[pl.ds(pl.multiple_of(s * r, 16), r), :]

        def own_half(a, h):
            hr = ins[a].shape[0] // 2
            return ins[a].at[pl.ds(pl.multiple_of(h * hr, 16), hr), :]

        sends = []
        for a in range(n):
            cp = pltpu.make_async_remote_copy(
                src_ref=ins[a], dst_ref=slab(a, mine), send_sem=own_send.at[a], recv_sem=own_recv.at[a],
                device_id=sibling, device_id_type=MESH)
            cp.start()
            sends.append(cp)
        cp = pltpu.make_async_remote_copy(
            src_ref=tap_in, dst_ref=tap_out.at[mine], send_sem=own_send.at[n], recv_sem=own_recv.at[n],
            device_id=sibling, device_id_type=MESH)
        cp.start()
        sends.append(cp)
        for a in range(n):
            for j, (px, py) in enumerate(chips):
                cp = pltpu.make_async_remote_copy(
                    src_ref=own_half(a, c), dst_ref=piece(a, mine, c), send_sem=ici_send.at[3 * a + j],
                    recv_sem=ici_recv.at[3 * a + j], device_id=(px, py, c), device_id_type=MESH)
                cp.start()
                sends.append(cp)
        for j, (px, py) in enumerate(chips):
            cp = pltpu.make_async_remote_copy(
                src_ref=tap_in, dst_ref=tap_out.at[mine], send_sem=ici_send.at[3 * n + j],
                recv_sem=ici_recv.at[3 * n + j], device_id=(px, py, c), device_id_type=MESH)
            cp.start()
            sends.append(cp)
        for a in range(n):
            for j, (px, py) in enumerate(chips):
                got = piece(a, 2 * px + py, c)
                pltpu.make_async_remote_copy(
                    src_ref=got, dst_ref=got, send_sem=ici_send.at[3 * a + j], recv_sem=ici_recv.at[3 * a + j],
                    device_id=(px, py, c), device_id_type=MESH).wait_recv()
                fwd = pltpu.make_async_remote_copy(
                    src_ref=got, dst_ref=got, send_sem=d2d_send.at[3 * a + j], recv_sem=d2d_recv.at[3 * a + j],
                    device_id=(x, y, 1 - c), device_id_type=MESH)
                fwd.start()
                sends.append(fwd)
        for a in range(n):
            for j, (px, py) in enumerate(chips):
                other = piece(a, 2 * px + py, 1 - c)
                pltpu.make_async_remote_copy(
                    src_ref=other, dst_ref=other, send_sem=d2d_send.at[3 * a + j], recv_sem=d2d_recv.at[3 * a + j],
                    device_id=(x, y, 1 - c), device_id_type=MESH).wait_recv()
        for j, (px, py) in enumerate(chips):
            pltpu.make_async_remote_copy(
                src_ref=tap_in, dst_ref=tap_out.at[2 * px + py], send_sem=ici_send.at[3 * n + j],
                recv_sem=ici_recv.at[3 * n + j], device_id=(px, py, c), device_id_type=MESH).wait_recv()
        for a in range(n):
            pltpu.make_async_remote_copy(
                src_ref=ins[a], dst_ref=slab(a, mine), send_sem=own_send.at[a], recv_sem=own_recv.at[a],
                device_id=sibling, device_id_type=MESH).wait_recv()
        pltpu.make_async_remote_copy(
            src_ref=tap_in, dst_ref=tap_out.at[mine], send_sem=own_send.at[n], recv_sem=own_recv.at[n],
            device_id=sibling, device_id_type=MESH).wait_recv()
        for cp in sends:
            cp.wait_send()

    def full_shape(a):
        r, cs = shards[a].shape
        return (r, 4 * cs) if col_kind[a] else (4 * r, cs)

    res = pl.pallas_call(
        body, name="gather_weights", in_specs=[ANY] * (n + 1), out_specs=[ANY] * (n + 1),
        out_shape=[jax.ShapeDtypeStruct(full_shape(a), shards[a].dtype) for a in range(n)]
        + [jax.ShapeDtypeStruct((4,) + taps.shape, taps.dtype)],
        scratch_shapes=[pltpu.SemaphoreType.DMA((3 * n + 3,)), pltpu.SemaphoreType.DMA((3 * n + 3,)),
                        pltpu.SemaphoreType.DMA((3 * n,)), pltpu.SemaphoreType.DMA((3 * n,)),
                        pltpu.SemaphoreType.DMA((n + 1,)), pltpu.SemaphoreType.DMA((n + 1,))],
        compiler_params=pltpu.CompilerParams(has_side_effects=True),
    )(*shards, taps)
    return res[:n], res[n]


HBM = pl.BlockSpec(memory_space=pltpu.HBM)
SEM = pl.BlockSpec(memory_space=pltpu.SEMAPHORE)
DATAFLOW = pltpu.SideEffectType.DATAFLOW_SIDE_EFFECTING


def _in_hbm(a):
    return pltpu.with_memory_space_constraint(a, pltpu.HBM)


def _split_start(name, srcs, lands, n_copies, plan):
    n = len(srcs)

    def body(*refs):
        src_refs, land_refs = refs[:n], refs[n:2 * n]
        send_sems, recv_sems = refs[2 * n], refs[2 * n + 1]
        for i, (src, dst, dev) in enumerate(plan(src_refs, land_refs)):
            pltpu.make_async_remote_copy(src_ref=src, dst_ref=dst, send_sem=send_sems.at[i], recv_sem=recv_sems.at[i],
                                         device_id=dev, device_id_type=MESH).start()
        refs[-1][...] = jnp.zeros((8, 128), F32)

    res = pl.pallas_call(
        body, name=name, in_specs=[HBM] * (2 * n),
        out_specs=[SEM, SEM] + [HBM] * (2 * n) + [pl.BlockSpec(memory_space=pltpu.VMEM)],
        out_shape=[pltpu.SemaphoreType.DMA((n_copies,)), pltpu.SemaphoreType.DMA((n_copies,))]
        + [pltpu.HBM(a.shape, a.dtype) for a in list(srcs) + list(lands)] + [jax.ShapeDtypeStruct((8, 128), F32)],
        input_output_aliases={i: 2 + i for i in range(2 * n)},
        compiler_params=pltpu.CompilerParams(has_side_effects=DATAFLOW),
    )(*[_in_hbm(a) for a in list(srcs) + list(lands)])
    return res[0], res[1], list(res[2:2 + n]), list(res[2 + n:2 + 2 * n]), res[-1]


def _split_wait(name, send_sems, recv_sems, srcs, lands, after, plan):
    n = len(srcs)

    def body(*refs):
        src_refs, land_refs = refs[:n], refs[n:2 * n]
        send_ref, recv_ref = refs[2 * n], refs[2 * n + 1]
        for i, (src, dst, dev) in enumerate(plan(src_refs, land_refs)):
            cp = pltpu.make_async_remote_copy(src_ref=src, dst_ref=dst, send_sem=send_ref.at[i], recv_sem=recv_ref.at[i],
                                              device_id=dev, device_id_type=MESH)
            cp.wait_send()
            cp.wait_recv()

    res = pl.pallas_call(
        body, name=name, in_specs=[HBM] * (2 * n) + [SEM, SEM, ANY], out_specs=[HBM] * (2 * n),
        out_shape=[pltpu.HBM(a.shape, a.dtype) for a in list(srcs) + list(lands)],
        input_output_aliases={i: i for i in range(2 * n)},
        compiler_params=pltpu.CompilerParams(has_side_effects=DATAFLOW),
    )(*srcs, *lands, send_sems, recv_sems, after)
    return list(res[:n]), list(res[n:])


def _first_gather_plan(src_refs, land_refs):
    x, y, c = _place()
    mine = 2 * x + y
    (src, small), (land, small_land) = src_refs, land_refs
    r = src.shape[0]
    hr = r // 2
    half = src.at[pl.ds(pl.multiple_of(c * hr, 16), hr), :]
    half_dst = land.at[pl.ds(pl.multiple_of(mine * r + c * hr, 16), hr), :]
    copies = [(src, land.at[pl.ds(pl.multiple_of(mine * r, 16), r), :], (x, y, 1 - c)),
              (small, small_land.at[mine], (x, y, 1 - c))]
    for px, py in _other_chips(x, y):
        copies += [(half, half_dst, (px, py, c)), (small, small_land.at[mine], (px, py, c))]
    return copies


def _forward_to_sibling(full):
    def body(full_in, full_ref, send_sems, recv_sems):
        x, y, c = _place()
        r = full_ref.shape[0] // 4
        hr = r // 2
        copies = []
        for j, (px, py) in enumerate(_other_chips(x, y)):
            got = full_ref.at[pl.ds(pl.multiple_of((2 * px + py) * r + c * hr, 16), hr), :]
            cp = pltpu.make_async_remote_copy(
                src_ref=got, dst_ref=got, send_sem=send_sems.at[j], recv_sem=recv_sems.at[j],
                device_id=(x, y, 1 - c), device_id_type=MESH)
            cp.start()
            copies.append(cp)
        for cp in copies:
            cp.wait()

    return pl.pallas_call(
        body, name="gather_first_forward", in_specs=[ANY], out_specs=ANY,
        out_shape=jax.ShapeDtypeStruct(full.shape, full.dtype), input_output_aliases={0: 0},
        scratch_shapes=[pltpu.SemaphoreType.DMA((3,)), pltpu.SemaphoreType.DMA((3,))],
        compiler_params=pltpu.CompilerParams(has_side_effects=True),
    )(full)


def _late_gather_plan(col_kind):
    def plan(src_refs, land_refs):
        x, y, c = _place()
        mine = 2 * x + y
        copies = []
        for a, (src, land) in enumerate(zip(src_refs, land_refs)):
            r, cs = src.shape
            if col_kind[a]:
                dst = land.at[:, pl.ds(pl.multiple_of(mine * cs, 128), cs)]
            else:
                dst = land.at[pl.ds(pl.multiple_of(mine * r, 16), r), :]
            copies.append((src, dst, (x, y, 1 - c)))
            copies += [(src, dst, (px, py, c)) for (px, py) in _other_chips(x, y)]
        return copies
    return plan


def _late_reduce_plan(col_kind):
    def plan(src_refs, land_refs):
        x, y, c = _place()
        copies = []
        for a, (src, land) in enumerate(zip(src_refs, land_refs)):
            for j, (px, py) in enumerate(_other_chips(x, y)):
                if col_kind[a] is None:
                    piece = src
                elif col_kind[a]:
                    cs = land.shape[2]
                    piece = src.at[:, pl.ds(pl.multiple_of((2 * px + py) * cs, 128), cs)]
                else:
                    piece = src.at[2 * px + py]
                copies.append((piece, land.at[j], (px, py, c)))
        return copies
    return plan


def _pair_swap(name, halves):
    n = len(halves)

    def body(*refs):
        ins, outs = refs[:n], refs[n:2 * n]
        send_sems, recv_sems = refs[2 * n:]
        x, y, c = _place()
        copies = []
        for a in range(n):
            cp = pltpu.make_async_remote_copy(
                src_ref=ins[a], dst_ref=outs[a], send_sem=send_sems.at[a], recv_sem=recv_sems.at[a],
                device_id=(x, y, 1 - c), device_id_type=MESH)
            cp.start()
            copies.append(cp)
        for cp in copies:
            cp.wait()

    return pl.pallas_call(
        body, name=name, in_specs=[ANY] * n, out_specs=[ANY] * n,
        out_shape=[jax.ShapeDtypeStruct(s.shape, s.dtype) for s in halves],
        scratch_shapes=[pltpu.SemaphoreType.DMA((n,)), pltpu.SemaphoreType.DMA((n,))],
        compiler_params=pltpu.CompilerParams(has_side_effects=True),
    )(*halves)


def _chip_sum(name, chip_sel, own, col, others):
    _, r, c = others.shape
    tr = _pick(r, (256, 128, 64, 32, 16))
    if col:
        own_spec = pl.BlockSpec((tr, c), lambda i, s: (i, s[0]))
    else:
        own_spec = pl.BlockSpec((None, tr, c), lambda i, s: (s[0], i, 0))
    specs = [own_spec] + [pl.BlockSpec((None, tr, c), lambda i, s, k=k: (k, i, 0)) for k in range(3)]

    def body(s_ref, own_ref, r0, r1, r2, o_ref):
        total = ((own_ref[...].astype(F32) + r0[...].astype(F32)) + r1[...].astype(F32)) + r2[...].astype(F32)
        o_ref[...] = total.astype(o_ref.dtype)

    return pl.pallas_call(
        body, name=name,
        grid_spec=pltpu.PrefetchScalarGridSpec(
            num_scalar_prefetch=1, grid=(r // tr,), in_specs=specs,
            out_specs=pl.BlockSpec((tr, c), lambda i, s: (i, 0))),
        out_shape=jax.ShapeDtypeStruct((r, c), BF16),
        compiler_params=_params(("parallel",)),
    )(chip_sel, own, others, others, others)


def _small_layout(vals):
    sizes = [int(math.prod(v.shape)) for v in vals]
    padded = [-(-s // 128) * 128 for s in sizes]
    return sizes, padded, -(-sum(padded) // 1024) * 1024


def _pack_small(vals):
    sizes, padded, total = _small_layout(vals)
    flat = [jnp.pad(v.reshape(-1), (0, p - s)) for v, s, p in zip(vals, sizes, padded)]
    flat.append(jnp.zeros((total - sum(padded),), F32))
    return jnp.concatenate(flat).reshape(total // 128, 128)


def _allreduce_small(own, others, vals):
    def body(own_ref, oth_ref, out_ref, land, send_sem, recv_sem):
        x, y, c = _place()
        out_ref[...] = (own_ref[...] + oth_ref[0]) + (oth_ref[1] + oth_ref[2])
        cp = pltpu.make_async_remote_copy(
            src_ref=out_ref, dst_ref=land, send_sem=send_sem.at[0], recv_sem=recv_sem.at[0],
            device_id=(x, y, 1 - c), device_id_type=MESH)
        cp.start()
        cp.wait()
        out_ref[...] = out_ref[...] + land[...]

    vm = pl.BlockSpec(memory_space=pltpu.VMEM)
    summed = pl.pallas_call(
        body, name="allreduce_small", in_specs=[vm, vm], out_specs=vm,
        out_shape=jax.ShapeDtypeStruct(own.shape, F32),
        scratch_shapes=[pltpu.VMEM(own.shape, F32), pltpu.SemaphoreType.DMA((1,)), pltpu.SemaphoreType.DMA((1,))],
        compiler_params=pltpu.CompilerParams(has_side_effects=True, vmem_limit_bytes=VMEM_LIMIT_BYTES),
    )(own, others).reshape(-1)
    sizes, padded, _ = _small_layout(vals)
    outs, off = [], 0
    for v, s, p in zip(vals, sizes, padded):
        outs.append(summed[off:off + s].reshape(v.shape))
        off += p
    return outs


def _adamw_math(w, g, m, v):
    m2 = ADAM_B1 * m + (1.0 - ADAM_B1) * g
    v2 = ADAM_B2 * v + (1.0 - ADAM_B2) * (g * g)
    m_hat = m2 / (1.0 - ADAM_B1 ** ADAM_STEP)
    v_hat = v2 / (1.0 - ADAM_B2 ** ADAM_STEP)
    delta = -ADAM_LR * (m_hat / (jnp.sqrt(v_hat) + ADAM_EPS) + ADAM_WD * w)
    return delta, m2, v2


def _adamw_big(name, w, g_mine, g_sibling, m, v):
    _, r, c = w.shape

    def body(w_ref, ga_ref, gb_ref, m_ref, v_ref, go_ref, d_ref, mo_ref, vo_ref):
        gv = ga_ref[...].astype(F32) + gb_ref[...].astype(F32)
        d, m2, v2 = _adamw_math(w_ref[...], gv, m_ref[...], v_ref[...])
        go_ref[...] = gv
        d_ref[...] = d
        mo_ref[...] = m2
        vo_ref[...] = v2

    tr = _pick(r, (256, 128, 64, 32, 16, 8))
    if r % tr == 0 and tr % 8 == 0:
        grid = (r // tr,)
        blk = pl.BlockSpec((None, tr, c), lambda i: (0, i, 0))
        part = pl.BlockSpec((tr, c), lambda i: (i, 0))
    else:
        grid = (c // 512,)
        blk = pl.BlockSpec((None, r, 512), lambda i: (0, 0, i))
        part = pl.BlockSpec((r, 512), lambda i: (0, i))
    return pl.pallas_call(
        body, name=name, grid=grid, in_specs=[blk, part, part, blk, blk], out_specs=[blk] * 4,
        out_shape=[jax.ShapeDtypeStruct((1, r, c), F32)] * 4, compiler_params=_params(("parallel",)),
    )(w, g_mine, g_sibling, m, v)


def _adamw_small(ws, gs, ms, vs):
    n = len(ws)

    def body(*refs):
        w_r, g_r, m_r, v_r = refs[:n], refs[n:2 * n], refs[2 * n:3 * n], refs[3 * n:4 * n]
        o = refs[4 * n:]
        for a in range(n):
            gv = g_r[a][...]
            d, m2, v2 = _adamw_math(w_r[a][...], gv, m_r[a][...], v_r[a][...])
            o[a][...] = gv
            o[n + a][...] = d
            o[2 * n + a][...] = m2
            o[3 * n + a][...] = v2

    res = pl.pallas_call(
        body, name="adamw_small", out_shape=[jax.ShapeDtypeStruct(w.shape, F32) for _ in range(4) for w in ws],
        compiler_params=_params(),
    )(*ws, *gs, *ms, *vs)
    return res[:n], res[n:2 * n], res[2 * n:3 * n], res[3 * n:]


def _full_from_gathered(name, gathered):
    if name == "w_in":
        rows = gathered.shape[0] // 4
        return gathered.reshape(4, rows, gathered.shape[1]).transpose(1, 0, 2).reshape(rows, 4 * gathered.shape[1])
    return gathered


def _reduce_layout(name, full):
    if name in COL_KIND:
        return full
    if name == "w_in":
        rows, cols = full.shape
        return full.reshape(rows, 4, cols // 4).transpose(1, 0, 2)
    return full.reshape(4, full.shape[0] // 4, full.shape[1])


def kernel(x, mem, norm_mix, w_in, fox_q_norm, fox_k_norm, fox_f_bias, s5_a_re, s5_a_im, s5_log_dt, s5_b_re, s5_b_im, s5_c_re, s5_c_im, s5_d, s5_w_glu, s5_b_glu, out_norm_fox, out_norm_s5, w_out, norm_cross, norm_mem, w_xq, w_xkv, xq_norm, xk_norm, w_xo, norm_ffn, w_ffn_up, ffn_conv_w, ffn_conv_b, w_ffn_down, loss_target, m_norm_mix, m_w_in, m_fox_q_norm, m_fox_k_norm, m_fox_f_bias, m_s5_a_re, m_s5_a_im, m_s5_log_dt, m_s5_b_re, m_s5_b_im, m_s5_c_re, m_s5_c_im, m_s5_d, m_s5_w_glu, m_s5_b_glu, m_out_norm_fox, m_out_norm_s5, m_w_out, m_norm_cross, m_norm_mem, m_w_xq, m_w_xkv, m_xq_norm, m_xk_norm, m_w_xo, m_norm_ffn, m_w_ffn_up, m_ffn_conv_w, m_ffn_conv_b, m_w_ffn_down, v_norm_mix, v_w_in, v_fox_q_norm, v_fox_k_norm, v_fox_f_bias, v_s5_a_re, v_s5_a_im, v_s5_log_dt, v_s5_b_re, v_s5_b_im, v_s5_c_re, v_s5_c_im, v_s5_d, v_s5_w_glu, v_s5_b_glu, v_out_norm_fox, v_out_norm_s5, v_w_out, v_norm_cross, v_norm_mem, v_w_xq, v_w_xkv, v_xq_norm, v_xk_norm, v_w_xo, v_norm_ffn, v_w_ffn_up, v_ffn_conv_w, v_ffn_conv_b, v_w_ffn_down):
    given = dict(locals())
    w = {n: given[n] for n in WEIGHTS}
    m = {n: given["m_" + n] for n in WEIGHTS}
    v = {n: given["v_" + n] for n in WEIGHTS}
    xi, yi, _ = _place()
    chip = (2 * xi + yi).astype(jnp.int32)
    chip_sel = chip.reshape(1)

    first_shard, taps = w[FIRST_WEIGHT][0].astype(BF16), w["ffn_conv_w"][0]
    send, recv, srcs, lands, g_started = _split_start(
        "gather_first_start", [first_shard, taps],
        [lax.empty((4 * first_shard.shape[0], first_shard.shape[1]), BF16), lax.empty((4,) + taps.shape, F32)],
        8, _first_gather_plan)
    pending = {"first": ((FIRST_WEIGHT, "ffn_conv_w"), _first_gather_plan, send, recv, srcs, lands)}
    for stage, names in (("mid", MID_WEIGHTS), ("late", LATE_WEIGHTS)):
        kinds = [n in COL_KIND for n in names]
        shards = [w[n][0].astype(BF16) for n in names]
        shards[0] = shards[0] + g_started[0:1, 0:1].astype(BF16)
        full = [lax.empty((s.shape[0], 4 * s.shape[1]) if ck else (4 * s.shape[0], s.shape[1]), BF16)
                for s, ck in zip(shards, kinds)]
        plan = _late_gather_plan(kinds)
        send, recv, srcs, lands, g_started = _split_start(
            "gather_" + stage + "_start", shards, full, 4 * len(names), plan)
        pending[stage] = (names, plan, send, recv, srcs, lands)

    def late_weights(stage, after):
        names, plan, send, recv, srcs, lands = pending[stage]
        _, full = _split_wait("gather_" + stage + "_wait", send, recv, srcs, lands, after, plan)
        if stage == "first":
            full = [_full_from_gathered(FIRST_WEIGHT, _forward_to_sibling(full[0])),
                    full[1].transpose(1, 0, 2).reshape(3, D_FF)]
        return dict(zip(names, full))

    reducing = {}

    def start_reduce(stage, grads_by_name, whole=()):
        names = list(grads_by_name)
        kinds = [n in COL_KIND for n in names]
        grads = [_reduce_layout(n, grads_by_name[n].astype(BF16)) for n in names]
        lands = [lax.empty((3, s.shape[0], s.shape[1] // 4) if ck else (3,) + s.shape[1:], BF16)
                 for s, ck in zip(grads, kinds)]
        plan = _late_reduce_plan(kinds + [None] * len(whole))
        send, recv, srcs, lands, started = _split_start(
            "reduce_" + stage + "_start", grads + list(whole),
            lands + [lax.empty((3,) + a.shape, a.dtype) for a in whole], 3 * (len(names) + len(whole)), plan)
        reducing[stage] = (names, kinds, plan, send, recv, srcs, lands)
        return started

    p = {n: w[n][0] for n in SMALL}
    for n in ("norm_mix", "fox_q_norm", "fox_k_norm", "fox_f_bias", "s5_b_glu", "out_norm_fox", "out_norm_s5",
              "norm_cross", "norm_mem", "xq_norm", "xk_norm", "norm_ffn", "ffn_conv_b"):
        p[n] = p[n].reshape(1, -1)
    p["norm_mix"] = p["norm_mix"] + g_started[0:1, 0:1]
    loss, grad_x, g = _local_step(x, mem, loss_target, p, {}, late_weights, start_reduce)

    small_names = list(SMALL) + ["ffn_conv_w"]
    small_vals = [g[n].reshape(w[n].shape if n != "ffn_conv_w" else (1, 3, D_FF)) for n in small_names] + [loss]
    after = start_reduce("first", {FIRST_WEIGHT: g[FIRST_WEIGHT]}, whole=[_pack_small(small_vals)])

    out_g, out_d, out_m, out_v = {}, {}, {}, {}

    def finish(stage, after):
        names, kinds, plan, send, recv, srcs, lands = reducing[stage]
        sums, from_chips = _split_wait("reduce_" + stage + "_wait", send, recv, srcs, lands, after, plan)
        mine = [_chip_sum("reduce_chip_sum_" + n, chip_sel, ps, ck, fc)
                for n, ps, fc, ck in zip(names, sums, from_chips, kinds)]
        theirs = _pair_swap("reduce_pair_swap_" + stage, mine)
        for n, a, b in zip(names, mine, theirs):
            if n == "w_in":
                flip = lambda t: jnp.swapaxes(t, -1, -2)
                res = _adamw_big("adamw_" + n, flip(w[n]), flip(a), flip(b), flip(m[n]), flip(v[n]))
                out_g[n], out_d[n], out_m[n], out_v[n] = (flip(t) for t in res)
                continue
            out_g[n], out_d[n], out_m[n], out_v[n] = _adamw_big("adamw_" + n, w[n], a, b, m[n], v[n])
        return sums[len(names):], from_chips[len(names):], out_v[names[-1]]

    _, _, after = finish("late", after)
    _, _, after = finish("mid", after)
    (small_own,), (small_others,), _ = finish("first", after)

    reduced = _allreduce_small(small_own, small_others, small_vals)
    loss_all = reduced[-1].reshape(())
    conv_w_grad = lax.dynamic_slice_in_dim(reduced[-2], chip * (D_FF // 4), D_FF // 4, axis=2)
    sg, sd, sm, sv = _adamw_small(
        [w[n] for n in small_names], list(reduced[:len(SMALL)]) + [conv_w_grad],
        [m[n] for n in small_names], [v[n] for n in small_names])
    out_g.update(zip(small_names, sg))
    out_d.update(zip(small_names, sd))
    out_m.update(zip(small_names, sm))
    out_v.update(zip(small_names, sv))

    return (loss_all, grad_x, *[out_g[n] for n in WEIGHTS], *[out_d[n] for n in WEIGHTS],
            *[out_m[n] for n in WEIGHTS], *[out_v[n] for n in WEIGHTS])
```

```python
import math

import jax
import jax.numpy as jnp
from jax import lax
from jax.experimental import pallas as pl
from jax.experimental.pallas import tpu as pltpu

F32 = jnp.float32
BF16 = jnp.bfloat16

D_MODEL = 1024
FOX_WIDTH = 512
HEAD_DIM = 64
N_FOX_HEADS = 8
S5_WIDTH = 512
S5_GROUP_CH = 16
S5_GROUPS = 32
S5_STATE = 64
S5_CH = S5_GROUPS * S5_STATE
N_X_HEADS = 4
X_HEAD_DIM = 256
N_MEM = 256
D_FF = 2816
UF_COLS = 640
EPS = 1e-6
ADAM_LR = 0.001
ADAM_B1 = 0.9
ADAM_B2 = 0.999
ADAM_EPS = 1e-08
ADAM_WD = 0.01
ADAM_STEP = 10

VMEM_LIMIT_BYTES = 56 * 1024 * 1024
MM_BLOCK_BYTES = 6 * 1024 * 1024
MM_VMEM_BYTES = 40 * 1024 * 1024
MM_TILE_MAX = 1536
MESH = pl.DeviceIdType.MESH

FIRST_WEIGHT = "w_in"
MID_WEIGHTS = ("s5_w_glu", "w_out")
EARLY_WEIGHTS = (FIRST_WEIGHT,) + MID_WEIGHTS
LATE_WEIGHTS = ("w_xq", "w_xkv", "w_xo", "w_ffn_up", "w_ffn_down")
BIG = EARLY_WEIGHTS + LATE_WEIGHTS
COL_KIND = ("w_xkv", "w_ffn_up")
SMALL = ("norm_mix", "fox_q_norm", "fox_k_norm", "fox_f_bias", "s5_a_re", "s5_a_im", "s5_log_dt",
         "s5_b_re", "s5_b_im", "s5_c_re", "s5_c_im", "s5_d", "s5_b_glu", "out_norm_fox", "out_norm_s5",
         "norm_cross", "norm_mem", "xq_norm", "xk_norm", "norm_ffn", "ffn_conv_b")
WEIGHTS = ("norm_mix", "w_in", "fox_q_norm", "fox_k_norm", "fox_f_bias", "s5_a_re", "s5_a_im", "s5_log_dt",
           "s5_b_re", "s5_b_im", "s5_c_re", "s5_c_im", "s5_d", "s5_w_glu", "s5_b_glu", "out_norm_fox",
           "out_norm_s5", "w_out", "norm_cross", "norm_mem", "w_xq", "w_xkv", "xq_norm", "xk_norm", "w_xo",
           "norm_ffn", "w_ffn_up", "ffn_conv_w", "ffn_conv_b", "w_ffn_down")


def _params(sem=None):
    return pltpu.CompilerParams(dimension_semantics=sem, vmem_limit_bytes=VMEM_LIMIT_BYTES)


def _pick(n, cands):
    for c in cands:
        if n % c == 0:
            return c
    return n


_DIMS = {"nn": (((1,), (0,)), ((), ())), "nt": (((1,), (1,)), ((), ())), "tn": (((0,), (0,)), ((), ()))}


def _mm(a, b, mode, name, out_dtype=F32, res=None):
    if mode == "nn":
        (m, k), (k2, n) = a.shape, b.shape
    elif mode == "nt":
        (m, k), (n, k2) = a.shape, b.shape
    else:
        (k, m), (k2, n) = a.shape, b.shape
    assert k == k2, (name, a.shape, b.shape)

    has_res = res is not None
    a_size, b_size = a.dtype.itemsize, b.dtype.itemsize
    o_size = jnp.dtype(out_dtype).itemsize + (res.dtype.itemsize if has_res else 0)

    def tiles(dim):
        return [c for c in range(MM_TILE_MAX, 0, -128) if dim % c == 0] or [dim]

    best = None
    for tm in tiles(m):
        for tn in tiles(n):
            a_blk, b_blk = tm * k * a_size, tn * k * b_size
            if max(a_blk, b_blk) > MM_BLOCK_BYTES or 2 * (a_blk + b_blk + tm * tn * o_size) > MM_VMEM_BYTES:
                continue
            for rows_outer in (True, False):
                moved = (m * k * a_size + (m // tm) * n * k * b_size) if rows_outer else \
                        (n * k * b_size + (n // tn) * m * k * a_size)
                key = (moved, -(tm * tn))
                if best is None or key < best[0]:
                    best = (key, tm, tn, rows_outer)
    assert best is not None, (name, a.shape, b.shape)
    _, tm, tn, rows_outer = best
    ij = (lambda g0, g1: (g0, g1)) if rows_outer else (lambda g0, g1: (g1, g0))
    if mode == "tn":
        a_spec = pl.BlockSpec((k, tm), lambda g0, g1: (0, ij(g0, g1)[0]))
    else:
        a_spec = pl.BlockSpec((tm, k), lambda g0, g1: (ij(g0, g1)[0], 0))
    if mode == "nt":
        b_spec = pl.BlockSpec((tn, k), lambda g0, g1: (ij(g0, g1)[1], 0))
    else:
        b_spec = pl.BlockSpec((k, tn), lambda g0, g1: (0, ij(g0, g1)[1]))
    o_spec = pl.BlockSpec((tm, tn), lambda g0, g1: ij(g0, g1))
    grid = (m // tm, n // tn) if rows_outer else (n // tn, m // tm)
    dims = _DIMS[mode]

    def body(*refs):
        a_ref, b_ref = refs[0], refs[1]
        o_ref = refs[-1]
        acc = lax.dot_general(a_ref[...].astype(BF16), b_ref[...].astype(BF16), dims, preferred_element_type=F32)
        if has_res:
            acc = acc + refs[2][...].astype(F32)
        o_ref[...] = acc.astype(o_ref.dtype)

    return pl.pallas_call(
        body, name=name, grid=grid,
        in_specs=[a_spec, b_spec] + ([o_spec] if has_res else []),
        out_specs=o_spec, out_shape=jax.ShapeDtypeStruct((m, n), out_dtype),
        compiler_params=_params(("parallel", "parallel")),
    )(*((a, b, res) if has_res else (a, b)))


def _row_spec(tm, bc, off, step):
    return pl.BlockSpec((tm, bc), lambda i, h: (i, off + step * h))


ROW_TILE_ELEMS = 512 * 1024


def _row_tile(t, rows):
    widest = max(bc for (_, bc, _, _) in rows)
    return _pick(t, (min(t, ROW_TILE_ELEMS // widest), 512, 256, 128, 64, 8))


def _rowwise(fn, rows, pars, outs, name, heads=1):
    t = rows[0][0].shape[0]
    tm = _row_tile(t, rows)
    nr, npar = len(rows), len(pars)

    def body(*refs):
        vals = [r[...].astype(F32) for r in refs[:nr + npar]]
        res = fn(*vals)
        if not isinstance(res, (tuple, list)):
            res = (res,)
        for o_ref, v in zip(refs[nr + npar:], res):
            o_ref[...] = v.astype(o_ref.dtype)

    in_specs = [_row_spec(tm, bc, off, st) for (_, bc, off, st) in rows]
    in_specs += [pl.BlockSpec(p.shape, lambda i, h: (0, 0)) for p in pars]
    out_specs = [_row_spec(tm, bc, 0, st) for (_, bc, st, _) in outs]
    out_shape = [jax.ShapeDtypeStruct((t, c), dt) for (c, _, _, dt) in outs]
    res = pl.pallas_call(
        body, name=name, grid=(t // tm, heads), in_specs=in_specs, out_specs=out_specs, out_shape=out_shape,
        compiler_params=_params(("parallel", "parallel")),
    )(*[r[0] for r in rows], *pars)
    return res[0] if len(res) == 1 else res


def _rowwise_vjp(fn, rows, pars, cts, name, heads=1, adds=None, row_dtypes=None):
    t = rows[0][0].shape[0]
    tm = _row_tile(t, rows)
    nr, npar, nct = len(rows), len(pars), len(cts)
    adds = adds or [None] * nr
    add_list = [a for a in adds if a is not None]
    row_dtypes = row_dtypes or [F32] * nr

    def body(*refs):
        i, h = pl.program_id(0), pl.program_id(1)
        p = 0
        row_v = [r[...].astype(F32) for r in refs[p:p + nr]]; p += nr
        par_v = [r[...].astype(F32) for r in refs[p:p + npar]]; p += npar
        ct_v = [r[...].astype(F32) for r in refs[p:p + nct]]; p += nct
        add_refs = refs[p:p + len(add_list)]; p += len(add_list)
        drow_refs = refs[p:p + nr]; p += nr
        dpar_refs = refs[p:p + npar]

        def wrapped(*a):
            r = fn(*a)
            return tuple(r) if isinstance(r, (tuple, list)) else (r,)

        _, pull = jax.vjp(wrapped, *row_v, *par_v)
        grads = pull(tuple(ct_v))
        ai = 0
        for k in range(nr):
            g = grads[k]
            if adds[k] is not None:
                g = g + add_refs[ai][...].astype(F32)
                ai += 1
            drow_refs[k][...] = g.astype(drow_refs[k].dtype)

        @pl.when((i == 0) & (h == 0))
        def _():
            for r in dpar_refs:
                r[...] = jnp.zeros(r.shape, r.dtype)

        for k in range(npar):
            dpar_refs[k][...] += grads[nr + k]

    in_specs = [_row_spec(tm, bc, off, st) for (_, bc, off, st) in rows]
    in_specs += [pl.BlockSpec(q.shape, lambda i, h: (0, 0)) for q in pars]
    in_specs += [_row_spec(tm, bc, off, st) for (_, bc, off, st) in cts]
    in_specs += [_row_spec(tm, bc, off, st) for (_, bc, off, st) in add_list]
    out_specs = [_row_spec(tm, bc, 0, st) for (_, bc, _, st) in rows]
    out_specs += [pl.BlockSpec(q.shape, lambda i, h: (0, 0)) for q in pars]
    out_shape = [jax.ShapeDtypeStruct((t, bc * (heads if st else 1)), dt) for (_, bc, _, st), dt in zip(rows, row_dtypes)]
    out_shape += [jax.ShapeDtypeStruct(q.shape, F32) for q in pars]
    res = pl.pallas_call(
        body, name=name, grid=(t // tm, heads), in_specs=in_specs, out_specs=out_specs, out_shape=out_shape,
        compiler_params=_params(("arbitrary", "arbitrary")),
    )(*[r[0] for r in rows], *pars, *[c[0] for c in cts], *[a[0] for a in add_list])
    return list(res[:nr]), list(res[nr:])


def _rms(x, g):
    return x * lax.rsqrt(jnp.mean(x * x, axis=-1, keepdims=True) + EPS) * g


def _rms_pair(x, g):
    left = lax.broadcasted_iota(jnp.int32, x.shape, 1) < HEAD_DIM
    x2 = x * x
    ms_a = jnp.sum(jnp.where(left, x2, 0.0), axis=-1, keepdims=True) * (1.0 / HEAD_DIM)
    ms_b = jnp.sum(jnp.where(left, 0.0, x2), axis=-1, keepdims=True) * (1.0 / HEAD_DIM)
    return x * lax.rsqrt(jnp.where(left, ms_a, ms_b) + EPS) * g


def _gelu(x):
    return 0.5 * x * (1.0 + jnp.tanh(math.sqrt(2.0 / math.pi) * (x + 0.044715 * (x * x * x))))


def _s5_act(ys, u, d):
    return _gelu(ys + d * u)


def _s5_gate(yg, z, b, g):
    return _rms(yg * jax.nn.sigmoid(z + b), g)


def _lane_cumsum(x, reverse):
    n = x.shape[-1]
    lane = lax.broadcasted_iota(jnp.int32, x.shape, 1)
    k = 1
    while k < n:
        if reverse:
            x = x + jnp.where(lane < n - k, pltpu.roll(x, n - k, 1), 0.0)
        else:
            x = x + jnp.where(lane >= k, pltpu.roll(x, k, 1), 0.0)
        k *= 2
    return x


def _log_sigmoid(z):
    return jnp.minimum(z, 0.0) - jnp.log(1.0 + jnp.exp(-jnp.abs(z)))


def _forget_fwd(f, bias):
    def body(f_ref, b_ref, c_ref):
        c_ref[...] = _lane_cumsum(_log_sigmoid(f_ref[...] + b_ref[...]), False)

    return pl.pallas_call(body, name="forget_fwd", out_shape=jax.ShapeDtypeStruct(f.shape, F32),
                          compiler_params=_params())(f, bias)


def _forget_bwd(f, bias, dc):
    def body(f_ref, b_ref, dc_ref, df_ref, db_ref):
        dlog = _lane_cumsum(dc_ref[...], True)
        df = dlog * jax.nn.sigmoid(-(f_ref[...] + b_ref[...]))
        df_ref[...] = df
        db_ref[...] = jnp.sum(df, axis=1, keepdims=True)

    return pl.pallas_call(body, name="forget_bwd",
                          out_shape=(jax.ShapeDtypeStruct(f.shape, F32), jax.ShapeDtypeStruct(bias.shape, F32)),
                          compiler_params=_params())(f, bias, dc)


FOX_BLOCK = 1024
FOX_KEYS = 1024
FOX_BWD_BLOCK = 512
_NT = _DIMS["nt"]
_TN = _DIMS["tn"]


N_PAIRS = N_FOX_HEADS // 2
V_BLOCK0 = 2 * N_PAIRS


def _left_lanes(shape):
    return lax.broadcasted_iota(jnp.int32, shape, 1) < HEAD_DIM


def _top_rows(shape):
    return lax.broadcasted_iota(jnp.int32, shape, 0) < HEAD_DIM


def _wide(c_tile, n):
    return c_tile if n == 128 else jnp.concatenate([c_tile] * (n // 128), axis=1)


def _fox_fwd(qn, kn, qkv, c_wide, seqs):
    t = qn.shape[0]
    l = t // seqs
    tb = min(FOX_BLOCK, l)
    tk = min(FOX_KEYS, tb)
    ratio = tb // tk
    nb = l // tb
    scale = HEAD_DIM ** -0.5

    def body(q_ref, k_ref, v_ref, ca_ref, cb_ref, o_ref, lse_ref, vt_ref):
        i = pl.program_id(2)
        top = _top_rows((128, tb))

        @pl.when(i == 0)
        def _():
            vt_ref[...] = v_ref[...].T.astype(BF16)

        qt = (q_ref[...].astype(F32) * scale).T.astype(BF16)
        zero = jnp.zeros_like(qt)
        qts = (jnp.where(top, qt, zero), jnp.where(top, zero, qt))
        top_k = _top_rows((128, tk))
        zero_k = jnp.zeros((128, tk), BF16)
        key_pos = lax.broadcasted_iota(jnp.int32, (tk, tb), 0)
        query_pos = lax.broadcasted_iota(jnp.int32, (tk, tb), 1)
        c_refs = (ca_ref, cb_ref)

        def scores(j):
            off = pl.multiple_of(j * tk, tk)
            k2 = k_ref[pl.ds(off, tk), :]
            return tuple(jnp.dot(k2, qts[h], preferred_element_type=F32) - _wide(c_refs[h][pl.ds(off, tk), :], tb)
                         for h in (0, 1))

        def values_times(ps, j):
            vt = vt_ref[:, pl.ds(pl.multiple_of(j * tk, tk), tk)]
            return (jnp.dot(jnp.where(top_k, vt, zero_k), ps[0], preferred_element_type=F32)
                    + jnp.dot(jnp.where(top_k, zero_k, vt), ps[1], preferred_element_type=F32))

        def softmax_step(sts, stats, first_key):
            ps, new, alphas = [], [], []
            for st, (m, s_sum) in zip(sts, stats):
                if first_key is not None:
                    st = jnp.where(key_pos + first_key <= query_pos, st, -jnp.inf)
                m_new = jnp.maximum(m, jnp.max(st, axis=0, keepdims=True))
                alpha = jnp.exp(m - m_new)
                p = jnp.exp(st - m_new)
                new.append((m_new, alpha * s_sum + jnp.sum(p, axis=0, keepdims=True)))
                alphas.append(alpha)
                ps.append(p.astype(BF16))
            return tuple(ps), tuple(new), jnp.where(top, alphas[0], alphas[1])

        def tile(j, carry, first_key):
            stats, acc = carry
            ps, stats, alpha = softmax_step(scores(j), stats, first_key)
            return stats, alpha * acc + values_times(ps, j)

        stat = (jnp.full((1, tb), -jnp.inf, F32), jnp.zeros((1, tb), F32))
        below = i * ratio
        carry = lax.fori_loop(0, below, lambda j, c: tile(j, c, None), ((stat, stat), jnp.zeros((128, tb), F32)))
        for r in range(ratio):
            carry = tile(below + r, carry, r * tk)
        ((ma, sa), (mb, sb)), acc = carry
        o_ref[...] = (acc / jnp.where(top, sa, sb)).T
        lse_ref[0:1, :] = ma + jnp.log(sa)
        lse_ref[1:2, :] = mb + jnp.log(sb)

    qblk = pl.BlockSpec((tb, 128), lambda b, hp, i: (b * nb + i, hp))
    return pl.pallas_call(
        body, name="fox_fwd", grid=(seqs, N_PAIRS, nb),
        in_specs=[qblk, pl.BlockSpec((l, 128), lambda b, hp, i: (b, hp)),
                  pl.BlockSpec((l, 128), lambda b, hp, i: (b, V_BLOCK0 + hp)),
                  pl.BlockSpec((None, l, 128), lambda b, hp, i: (b * N_FOX_HEADS + 2 * hp, 0, 0)),
                  pl.BlockSpec((None, l, 128), lambda b, hp, i: (b * N_FOX_HEADS + 2 * hp + 1, 0, 0))],
        out_specs=[qblk, pl.BlockSpec((None, 2, tb), lambda b, hp, i: (b * N_PAIRS + hp, 0, i))],
        out_shape=[jax.ShapeDtypeStruct((t, FOX_WIDTH), F32), jax.ShapeDtypeStruct((seqs * N_PAIRS, 2, l), F32)],
        scratch_shapes=[pltpu.VMEM((128, l), BF16)],
        compiler_params=_params(("parallel", "parallel", "arbitrary")),
    )(qn, kn, qkv, c_wide, c_wide)


def _fox_bwd(qn, kn, qkv, c_wide, o, do, lse, seqs):
    t = qn.shape[0]
    l = t // seqs
    tb = min(FOX_BWD_BLOCK, l)
    nb = l // tb
    scale = HEAD_DIM ** -0.5
    one_at = (HEAD_DIM, 0)

    def body(q_ref, k_ref, v_ref, ca_ref, cb_ref, o_ref, do_ref, lse_ref, dq_ref, dk_ref, dv_ref, dc_ref, dcq_ref,
             qt_ref, kt_ref, dot_ref, delta_ref, dqa_ref, dqb_ref):
        top_l = _top_rows((128, l))
        top = _top_rows((128, tb))
        left = _left_lanes((tb, 128))
        row_id = lax.broadcasted_iota(jnp.int32, (128, tb), 0)
        lane_id = lax.broadcasted_iota(jnp.int32, (tb, 128), 1)
        zero_t = jnp.zeros((128, tb), BF16)
        zero_l = jnp.zeros((tb, 128), BF16)
        rows = lambda a: (jnp.where(top, a, zero_t), jnp.where(top, zero_t, a))
        lanes = lambda a: (jnp.where(left, a, zero_l), jnp.where(left, zero_l, a))
        with_one_row = lambda pair: tuple(jnp.where(row_id == one_at[h], 1.0, pair[h]).astype(BF16) for h in (0, 1))
        with_one_lane = lambda pair: tuple(jnp.where(lane_id == one_at[h], 1.0, pair[h]).astype(BF16) for h in (0, 1))
        causal = lax.broadcasted_iota(jnp.int32, (tb, tb), 0) <= lax.broadcasted_iota(jnp.int32, (tb, tb), 1)
        c_refs = (ca_ref, cb_ref)
        dq_refs = (dqa_ref, dqb_ref)

        qt_ref[...] = (q_ref[...].astype(F32) * scale).T.astype(BF16)
        kt_ref[...] = k_ref[...].astype(F32).T.astype(BF16)
        do_t = do_ref[...].T
        dot_ref[...] = do_t.astype(BF16)
        prod_t = do_t * o_ref[...].T
        delta_ref[0:1, :] = jnp.sum(jnp.where(top_l, prod_t, 0.0), axis=0, keepdims=True)
        delta_ref[1:2, :] = jnp.sum(jnp.where(top_l, 0.0, prod_t), axis=0, keepdims=True)
        dqa_ref[...] = jnp.zeros(dqa_ref.shape, F32)
        dqb_ref[...] = jnp.zeros(dqb_ref.shape, F32)

        def kv_block(j, _):
            koff = pl.multiple_of(j * tb, tb)
            k2 = k_ref[pl.ds(koff, tb), :]
            v2 = v_ref[pl.ds(koff, tb), :].astype(BF16)
            kts = with_one_row(rows(kt_ref[:, pl.ds(koff, tb)]))
            cw = tuple(_wide(c_refs[h][pl.ds(koff, tb), :], tb) for h in (0, 1))

            def q_block(i, carry, masked):
                dks, dv = list(carry[:2]), carry[2]
                qoff = pl.multiple_of(i * tb, tb)
                qs = lanes((q_ref[pl.ds(qoff, tb), :].astype(F32) * scale).astype(BF16))
                qs_one = with_one_lane(qs)
                dos = lanes(do_ref[pl.ds(qoff, tb), :].astype(BF16))
                qts = rows(qt_ref[:, pl.ds(qoff, tb)])
                dots = rows(dot_ref[:, pl.ds(qoff, tb)])
                for h in (0, 1):
                    st = jnp.dot(k2, qts[h], preferred_element_type=F32) - cw[h]
                    p = jnp.exp(st - lse_ref[h:h + 1, pl.ds(qoff, tb)])
                    if masked:
                        p = jnp.where(causal, p, 0.0)
                    dp = jnp.dot(v2, dots[h], preferred_element_type=F32)
                    dsb = (p * (dp - delta_ref[h:h + 1, pl.ds(qoff, tb)])).astype(BF16)
                    dv = dv + jnp.dot(p.astype(BF16), dos[h], preferred_element_type=F32)
                    dks[h] = dks[h] + jnp.dot(dsb, qs_one[h], preferred_element_type=F32)
                    dq_refs[h][:, pl.ds(qoff, tb)] += jnp.dot(kts[h], dsb, preferred_element_type=F32)
                return dks[0], dks[1], dv

            z = jnp.zeros((tb, 128), F32)
            carry = q_block(j, (z, z, z), True)
            rest = nb - 1 - j
            carry = lax.fori_loop(
                0, rest // 2, lambda n, c: q_block(j + 2 + 2 * n, q_block(j + 1 + 2 * n, c, False), False), carry)
            dka, dkb, dv = lax.cond(rest % 2 == 1, lambda c: q_block(nb - 1, c, False), lambda c: c, carry)
            dk_ref[pl.ds(koff, tb), :] = jnp.where(left, dka, dkb)
            dv_ref[pl.ds(koff, tb), :] = dv
            dc_ref[0:1, pl.ds(koff, tb)] = -dka.T[one_at[0]:one_at[0] + 1, :]
            dc_ref[1:2, pl.ds(koff, tb)] = -dkb.T[one_at[1]:one_at[1] + 1, :]
            return 0

        lax.fori_loop(0, nb, kv_block, 0)
        dq_ref[...] = (jnp.where(top_l, dqa_ref[...], dqb_ref[...]) * scale).T
        dcq_ref[0:1, :] = dqa_ref[one_at[0]:one_at[0] + 1, :]
        dcq_ref[1:2, :] = dqb_ref[one_at[1]:one_at[1] + 1, :]

    blk = pl.BlockSpec((l, 128), lambda b, hp: (b, hp))
    cspec = lambda k: pl.BlockSpec((None, l, 128), lambda b, hp: (b * N_FOX_HEADS + 2 * hp + k, 0, 0))
    rows2 = pl.BlockSpec((None, 2, l), lambda b, hp: (b * N_PAIRS + hp, 0, 0))
    wide = jax.ShapeDtypeStruct((t, FOX_WIDTH), F32)
    pair_rows = jax.ShapeDtypeStruct((seqs * N_PAIRS, 2, l), F32)
    return pl.pallas_call(
        body, name="fox_bwd", grid=(seqs, N_PAIRS),
        in_specs=[blk, blk, pl.BlockSpec((l, 128), lambda b, hp: (b, V_BLOCK0 + hp)), cspec(0), cspec(1), blk, blk, rows2],
        out_specs=[blk, blk, blk, rows2, rows2],
        out_shape=[wide, wide, wide, pair_rows, pair_rows],
        scratch_shapes=[pltpu.VMEM((128, l), BF16), pltpu.VMEM((128, l), BF16), pltpu.VMEM((128, l), BF16),
                        pltpu.VMEM((2, l), F32), pltpu.VMEM((128, l), F32), pltpu.VMEM((128, l), F32)],
        compiler_params=_params(("parallel", "parallel")),
    )(qn, kn, qkv, c_wide, c_wide, o, do, lse)


SCAN_ROWS = 512
SCAN_COLS = 1024


S5_IN = 128
S5_ST = 512
SCAN_CHUNKS = SCAN_COLS // S5_ST
SCAN_SEGS = 8
LANES = 128


def _cmul(ar, ai, br, bi):
    return ar * br - ai * bi, ar * bi + ai * br


def _powers_into(pw_r, pw_i, a_r, a_i, seg):
    pw_r[0:1, :] = a_r
    pw_i[0:1, :] = a_i
    for k in range(1, seg):
        pr, pi = _cmul(pw_r[k - 1:k, :], pw_i[k - 1:k, :], a_r, a_i)
        pw_r[k:k + 1, :] = pr
        pw_i[k:k + 1, :] = pi


def _interleave(dst, src, seg):
    for h in range(src.shape[0]):
        for j in range(seg):
            dst[h, j * SCAN_SEGS:(j + 1) * SCAN_SEGS, :] = src[h, pl.ds(j, SCAN_SEGS, stride=seg), :]


def _deinterleave(dst, src, seg):
    for h in range(src.shape[0]):
        for j in range(seg):
            dst[h, pl.ds(j, SCAN_SEGS, stride=seg), :] = src[h, j * SCAN_SEGS:(j + 1) * SCAN_SEGS, :]


def _interleaved(ref, tmp_a, tmp_b, seg):
    n = ref.shape[1] // LANES
    for h in range(n):
        tmp_a[h] = ref[:, h * LANES:(h + 1) * LANES].astype(F32)
    _interleave(tmp_b, tmp_a, seg)
    return jnp.concatenate([tmp_b[h] for h in range(n)], axis=1)


def _store_deinterleaved(ref, val, tmp_a, tmp_b, seg):
    n = ref.shape[1] // LANES
    for h in range(n):
        tmp_a[h] = val[:, h * LANES:(h + 1) * LANES]
    _deinterleave(tmp_b, tmp_a, seg)
    for h in range(n):
        ref[:, h * LANES:(h + 1) * LANES] = tmp_b[h]


def _segment_scan(b_r, b_i, x_r, x_i, pw_r, pw_i, car_r, car_i, seg, sign, reverse, visit=None):
    nc = b_r.shape[0]
    sub = lax.broadcasted_iota(jnp.int32, (SCAN_SEGS, LANES), 0)
    lanes = lambda c: slice(c * LANES, (c + 1) * LANES)
    rows = lambda j: pl.ds(pl.multiple_of(((seg - 1 - j) if reverse else j) * SCAN_SEGS, SCAN_SEGS), SCAN_SEGS)
    a1 = [(pw_r[0:1, lanes(c)], sign * pw_i[0:1, lanes(c)]) for c in range(nc)]

    def local(j, xs):
        out = []
        for c in range(nc):
            xr, xi = xs[2 * c], xs[2 * c + 1]
            nr = a1[c][0] * xr - a1[c][1] * xi + b_r[c, rows(j), :]
            ni = a1[c][0] * xi + a1[c][1] * xr + b_i[c, rows(j), :]
            x_r[c, rows(j), :] = nr
            x_i[c, rows(j), :] = ni
            out += [nr, ni]
        return tuple(out)

    zero = jnp.zeros((SCAN_SEGS, LANES), F32)
    ends = lax.fori_loop(0, seg, local, (zero,) * (2 * nc))

    if reverse:
        first = sub == SCAN_SEGS - 1
        neighbour = lambda v: pltpu.roll(v, SCAN_SEGS - 1, 0)
        shift = lambda v, d: jnp.where(sub < SCAN_SEGS - d, pltpu.roll(v, SCAN_SEGS - d, 0), 0.0)
    else:
        first = sub == 0
        neighbour = lambda v: pltpu.roll(v, 1, 0)
        shift = lambda v, d: jnp.where(sub >= d, pltpu.roll(v, d, 0), 0.0)
    last = 0 if reverse else SCAN_SEGS - 1
    entries = []
    for c in range(nc):
        er, ei = ends[2 * c], ends[2 * c + 1]
        pr, pi = pw_r[seg - 1:seg, lanes(c)], sign * pw_i[seg - 1:seg, lanes(c)]
        yr = jnp.where(first, car_r[:, lanes(c)], neighbour(er))
        yi = jnp.where(first, car_i[:, lanes(c)], neighbour(ei))
        qr, qi = pr, pi
        for d in (1, 2, 4):
            mr, mi = _cmul(qr, qi, shift(yr, d), shift(yi, d))
            yr, yi = yr + mr, yi + mi
            qr, qi = _cmul(qr, qi, qr, qi)
        lr, li = _cmul(pr, pi, yr, yi)
        car_r[:, lanes(c)] = (er + lr)[last:last + 1, :]
        car_i[:, lanes(c)] = (ei + li)[last:last + 1, :]
        entries += [yr, yi]

    def correct(j, prev):
        out = []
        row_r, row_i = pw_r[pl.ds(j, 1), :], sign * pw_i[pl.ds(j, 1), :]
        for c in range(nc):
            mr, mi = _cmul(row_r[:, lanes(c)], row_i[:, lanes(c)], entries[2 * c], entries[2 * c + 1])
            nr = x_r[c, rows(j), :] + mr
            ni = x_i[c, rows(j), :] + mi
            x_r[c, rows(j), :] = nr
            x_i[c, rows(j), :] = ni
            if visit is not None:
                visit(c, rows(j), prev[2 * c], prev[2 * c + 1])
            out += [nr, ni]
        return tuple(out)

    lax.fori_loop(0, seg, correct, tuple(entries))


def _s5_fwd(uf, bbr, bbi, cr, ci, ar, ai, seqs):
    t = uf.shape[0]
    l = t // seqs
    tl = min(SCAN_ROWS, l)
    nl = l // tl
    seg = tl // SCAN_SEGS
    cb, nq = SCAN_COLS, SCAN_CHUNKS
    nc = cb // LANES
    per = S5_ST // LANES

    def body(u_ref, bbr_ref, bbi_ref, cr_ref, ci_ref, ar_ref, ai_ref, x_r, x_i, ys_ref,
             car_r, car_i, pw_r, pw_i, b_r, b_i, tmp_a, tmp_b):
        @pl.when(pl.program_id(2) == 0)
        def _():
            car_r[...] = jnp.zeros(car_r.shape, F32)
            car_i[...] = jnp.zeros(car_i.shape, F32)
            _powers_into(pw_r, pw_i, ar_ref[...], ai_ref[...], seg)

        u = _interleaved(u_ref, tmp_a, tmp_b, seg).astype(BF16)
        for q in range(nq):
            uq = u[:, q * S5_IN:(q + 1) * S5_IN]
            br = jnp.dot(uq, bbr_ref[q], preferred_element_type=F32)
            bi = jnp.dot(uq, bbi_ref[q], preferred_element_type=F32)
            for s in range(per):
                b_r[q * per + s] = br[:, s * LANES:(s + 1) * LANES]
                b_i[q * per + s] = bi[:, s * LANES:(s + 1) * LANES]
        _segment_scan(b_r, b_i, x_r, x_i, pw_r, pw_i, car_r, car_i, seg, 1.0, False)
        wide = lambda buf, q: jnp.concatenate([buf[q * per + s] for s in range(per)], axis=1).astype(BF16)
        ys = [jnp.dot(wide(x_r, q), cr_ref[q], preferred_element_type=F32)
              + jnp.dot(wide(x_i, q), ci_ref[q], preferred_element_type=F32) for q in range(nq)]
        _store_deinterleaved(ys_ref, jnp.concatenate(ys, axis=1), tmp_a, tmp_b, seg)

    rows = lambda w: pl.BlockSpec((tl, w), lambda s, j, r: (s * nl + r, j))
    state = pl.BlockSpec((nc, tl, LANES), lambda s, j, r: (j, s * nl + r, 0))
    chunk = lambda a: pl.BlockSpec((nq,) + a.shape[1:], lambda s, j, r: (j, 0, 0))
    par = pl.BlockSpec((1, cb), lambda s, j, r: (0, j))
    return pl.pallas_call(
        body, name="s5_fwd", grid=(seqs, S5_CH // cb, nl),
        in_specs=[rows(nq * S5_IN), chunk(bbr), chunk(bbi), chunk(cr), chunk(ci), par, par],
        out_specs=[state, state, rows(nq * S5_IN)],
        out_shape=[jax.ShapeDtypeStruct((S5_CH // LANES, t, LANES), F32)] * 2
        + [jax.ShapeDtypeStruct((t, S5_WIDTH), F32)],
        scratch_shapes=[pltpu.VMEM((1, cb), F32), pltpu.VMEM((1, cb), F32), pltpu.VMEM((seg, cb), F32),
                        pltpu.VMEM((seg, cb), F32)] + [pltpu.VMEM((nc, tl, LANES), F32)] * 2
        + [pltpu.VMEM((nq * S5_IN // LANES, tl, LANES), F32)] * 2,
        compiler_params=_params(("parallel", "parallel", "arbitrary")),
    )(uf, bbr, bbi, cr, ci, ar, ai)


def _s5_bwd(dys, uf, xr, xi, bbr, bbi, cr, ci, ar, ai, seqs):
    t = dys.shape[0]
    l = t // seqs
    tl = min(SCAN_ROWS, l)
    nl = l // tl
    seg = tl // SCAN_SEGS
    cb, nq = SCAN_COLS, SCAN_CHUNKS
    nc = cb // LANES
    per = S5_ST // LANES

    def body(dy_ref, u_ref, x_r, x_i, bbr_ref, bbi_ref, cr_ref, ci_ref, ar_ref, ai_ref,
             du_ref, dbbr_ref, dbbi_ref, dcr_ref, dci_ref, dar_ref, dai_ref,
             car_r, car_i, pw_r, pw_i, g_r, g_i, lam_r, lam_i, acc_r, acc_i, tmp_a, tmp_b):
        @pl.when(pl.program_id(2) == 0)
        def _():
            car_r[...] = jnp.zeros(car_r.shape, F32)
            car_i[...] = jnp.zeros(car_i.shape, F32)
            _powers_into(pw_r, pw_i, ar_ref[...], ai_ref[...], seg)
            for acc_ref in (dbbr_ref, dbbi_ref, dcr_ref, dci_ref, dar_ref, dai_ref):
                acc_ref[...] = jnp.zeros(acc_ref.shape, F32)

        dy = _interleaved(dy_ref, tmp_a, tmp_b, seg).astype(BF16)
        for q in range(nq):
            dyq = dy[:, q * S5_IN:(q + 1) * S5_IN]
            gr = lax.dot_general(dyq, cr_ref[q], _NT, preferred_element_type=F32)
            gi = lax.dot_general(dyq, ci_ref[q], _NT, preferred_element_type=F32)
            for s in range(per):
                g_r[q * per + s] = gr[:, s * LANES:(s + 1) * LANES]
                g_i[q * per + s] = gi[:, s * LANES:(s + 1) * LANES]
        acc_r[...] = jnp.zeros(acc_r.shape, F32)
        acc_i[...] = jnp.zeros(acc_i.shape, F32)

        def visit(c, rws, lr, li):
            xr_t, xi_t = x_r[c, rws, :], x_i[c, rws, :]
            acc_r[c] += lr * xr_t + li * xi_t
            acc_i[c] += li * xr_t - lr * xi_t

        _segment_scan(g_r, g_i, lam_r, lam_i, pw_r, pw_i, car_r, car_i, seg, -1.0, True, visit)
        for c in range(nc):
            dar_ref[:, c * LANES:(c + 1) * LANES] += jnp.sum(acc_r[c], axis=0, keepdims=True)
            dai_ref[:, c * LANES:(c + 1) * LANES] += jnp.sum(acc_i[c], axis=0, keepdims=True)
        u = _interleaved(u_ref, tmp_a, tmp_b, seg).astype(BF16)
        wide = lambda buf, q: jnp.concatenate([buf[q * per + s] for s in range(per)], axis=1).astype(BF16)
        du = []
        for q in range(nq):
            io = slice(q * S5_IN, (q + 1) * S5_IN)
            lq_r, lq_i = wide(lam_r, q), wide(lam_i, q)
            du.append(lax.dot_general(lq_r, bbr_ref[q], _NT, preferred_element_type=F32)
                      + lax.dot_general(lq_i, bbi_ref[q], _NT, preferred_element_type=F32))
            dbbr_ref[q] += lax.dot_general(u[:, io], lq_r, _TN, preferred_element_type=F32)
            dbbi_ref[q] += lax.dot_general(u[:, io], lq_i, _TN, preferred_element_type=F32)
            dcr_ref[q] += lax.dot_general(wide(x_r, q), dy[:, io], _TN, preferred_element_type=F32)
            dci_ref[q] += lax.dot_general(wide(x_i, q), dy[:, io], _TN, preferred_element_type=F32)
        _store_deinterleaved(du_ref, jnp.concatenate(du, axis=1), tmp_a, tmp_b, seg)

    rows = lambda w: pl.BlockSpec((tl, w), lambda s, j, r: (s * nl + nl - 1 - r, j))
    state = pl.BlockSpec((nc, tl, LANES), lambda s, j, r: (j, s * nl + nl - 1 - r, 0))
    chunk = lambda a: pl.BlockSpec((nq,) + a.shape[1:], lambda s, j, r: (j, 0, 0))
    acc = lambda a: pl.BlockSpec((None, nq) + a.shape[1:], lambda s, j, r: (s, j, 0, 0))
    par = pl.BlockSpec((1, cb), lambda s, j, r: (0, j))
    par_acc = pl.BlockSpec((None, 1, cb), lambda s, j, r: (s, 0, j))
    per_seq = lambda a: jax.ShapeDtypeStruct((seqs,) + a.shape, F32)
    return pl.pallas_call(
        body, name="s5_bwd", grid=(seqs, S5_CH // cb, nl),
        in_specs=[rows(nq * S5_IN), rows(nq * S5_IN), state, state, chunk(bbr), chunk(bbi), chunk(cr), chunk(ci),
                  par, par],
        out_specs=[rows(nq * S5_IN), acc(bbr), acc(bbi), acc(cr), acc(ci), par_acc, par_acc],
        out_shape=[jax.ShapeDtypeStruct((t, S5_WIDTH), F32), per_seq(bbr), per_seq(bbi), per_seq(cr), per_seq(ci),
                   jax.ShapeDtypeStruct((seqs, 1, S5_CH), F32), jax.ShapeDtypeStruct((seqs, 1, S5_CH), F32)],
        scratch_shapes=[pltpu.VMEM((1, cb), F32), pltpu.VMEM((1, cb), F32), pltpu.VMEM((seg, cb), F32),
                        pltpu.VMEM((seg, cb), F32)] + [pltpu.VMEM((nc, tl, LANES), F32)] * 4
        + [pltpu.VMEM((nc, SCAN_SEGS, LANES), F32)] * 2 + [pltpu.VMEM((nq * S5_IN // LANES, tl, LANES), F32)] * 2,
        compiler_params=_params(("parallel", "parallel", "arbitrary")),
    )(dys, uf, xr, xi, bbr, bbi, cr, ci, ar, ai)


XATT_BLOCK = 2048


def _xatt_probs(qv, kv):
    s = lax.dot_general(qv, kv, _NT, preferred_element_type=F32) * (X_HEAD_DIM ** -0.5)
    e = jnp.exp(s - jnp.max(s, axis=-1, keepdims=True))
    return e / jnp.sum(e, axis=-1, keepdims=True)


def _xatt_fwd(q, k, kv, seqs):
    t = q.shape[0]
    tq = min(XATT_BLOCK, t // seqs)
    nq = t // seqs // tq

    def body(q_ref, k_ref, v_ref, o_ref):
        p = _xatt_probs(q_ref[...], k_ref[...])
        o_ref[...] = jnp.dot(p.astype(BF16), v_ref[...].astype(BF16), preferred_element_type=F32).astype(o_ref.dtype)

    qs = pl.BlockSpec((tq, X_HEAD_DIM), lambda b, h, i: (b * nq + i, h))
    return pl.pallas_call(
        body, name="xatt_fwd", grid=(seqs, N_X_HEADS, nq),
        in_specs=[qs, pl.BlockSpec((N_MEM, X_HEAD_DIM), lambda b, h, i: (b, h)),
                  pl.BlockSpec((N_MEM, X_HEAD_DIM), lambda b, h, i: (b, N_X_HEADS + h))],
        out_specs=qs, out_shape=jax.ShapeDtypeStruct(q.shape, BF16),
        compiler_params=_params(("parallel", "parallel", "parallel")),
    )(q, k, kv)


def _xatt_bwd(q, k, kv, do, seqs):
    t = q.shape[0]
    tq = min(XATT_BLOCK, t // seqs)
    nq = t // seqs // tq
    scale = X_HEAD_DIM ** -0.5

    def body(q_ref, k_ref, v_ref, do_ref, dq_ref, dk_ref, dv_ref):
        @pl.when(pl.program_id(2) == 0)
        def _():
            dk_ref[...] = jnp.zeros(dk_ref.shape, F32)
            dv_ref[...] = jnp.zeros(dv_ref.shape, F32)

        qv, kk = q_ref[...], k_ref[...]
        p = _xatt_probs(qv, kk)
        dob = do_ref[...].astype(BF16)
        dp = lax.dot_general(dob, v_ref[...].astype(BF16), _NT, preferred_element_type=F32)
        ds = p * (dp - jnp.sum(dp * p, axis=-1, keepdims=True))
        dsb = ds.astype(BF16)
        dq_ref[...] = jnp.dot(dsb, kk, preferred_element_type=F32) * scale
        dk_ref[...] += lax.dot_general(dsb, qv, _TN, preferred_element_type=F32) * scale
        dv_ref[...] += lax.dot_general(p.astype(BF16), dob, _TN, preferred_element_type=F32)

    qs = pl.BlockSpec((tq, X_HEAD_DIM), lambda b, h, i: (b * nq + i, h))
    ks = pl.BlockSpec((N_MEM, X_HEAD_DIM), lambda b, h, i: (b, h))
    return pl.pallas_call(
        body, name="xatt_bwd", grid=(seqs, N_X_HEADS, nq),
        in_specs=[qs, ks, pl.BlockSpec((N_MEM, X_HEAD_DIM), lambda b, h, i: (b, N_X_HEADS + h)), qs],
        out_specs=[qs, ks, ks],
        out_shape=[jax.ShapeDtypeStruct(q.shape, F32), jax.ShapeDtypeStruct(k.shape, F32),
                   jax.ShapeDtypeStruct(k.shape, F32)],
        compiler_params=_params(("parallel", "parallel", "arbitrary")),
    )(q, k, kv, do)


CONV_COLS = 256


def _shift_down(x, k, row):
    return jnp.where(row >= k, pltpu.roll(x, k, 0), 0.0)


def _shift_up(x, k, row):
    n = x.shape[0]
    return jnp.where(row < n - k, pltpu.roll(x, n - k, 0), 0.0)


def _down_from(x, prev, k, row):
    return jnp.where(row >= k, pltpu.roll(x, k, 0), pltpu.roll(prev, k, 0))


GATE_ROWS = 512


def _ffn_up_gate(hn, w_up, w, b, seqs):
    t = hn.shape[0]
    l = t // seqs
    nc = D_FF // CONV_COLS

    rc = min(GATE_ROWS, l)

    def body(a_ref, wg_ref, wu_ref, w_ref, b_ref, g_ref, u_ref, o_ref):
        wv, bias = w_ref[...], b_ref[...]
        row = lax.broadcasted_iota(jnp.int32, (rc, CONV_COLS), 0)
        prev = jnp.zeros((rc, CONV_COLS), F32)
        for k in range(l // rc):
            rows = slice(k * rc, (k + 1) * rc)
            a = a_ref[rows, :]
            gb = jnp.dot(a, wg_ref[...], preferred_element_type=F32).astype(BF16)
            ub = jnp.dot(a, wu_ref[...], preferred_element_type=F32).astype(BF16)
            g_ref[rows, :] = gb
            u_ref[rows, :] = ub
            g = gb.astype(F32)
            pre = bias + wv[0:1, :] * _down_from(g, prev, 2, row) + wv[1:2, :] * _down_from(g, prev, 1, row) \
                + wv[2:3, :] * g
            o_ref[rows, :] = (pre * jax.nn.sigmoid(pre) * ub.astype(F32)).astype(o_ref.dtype)
            prev = g

    cols = pl.BlockSpec((l, CONV_COLS), lambda s, j: (s, j))
    half = jax.ShapeDtypeStruct((t, D_FF), BF16)
    return pl.pallas_call(
        body, name="ffn_up_gate", grid=(seqs, nc),
        in_specs=[pl.BlockSpec((l, hn.shape[1]), lambda s, j: (s, 0)),
                  pl.BlockSpec((hn.shape[1], CONV_COLS), lambda s, j: (0, j)),
                  pl.BlockSpec((hn.shape[1], CONV_COLS), lambda s, j: (0, nc + j)),
                  pl.BlockSpec((3, CONV_COLS), lambda s, j: (0, j)), pl.BlockSpec((1, CONV_COLS), lambda s, j: (0, j))],
        out_specs=[cols, cols, cols], out_shape=[half, half, half],
        compiler_params=_params(("parallel", "parallel")),
    )(hn, w_up, w_up, w, b)


def _ffn_down_dx_gate(dh, w_down, gate, up, w, b, seqs):
    t = dh.shape[0]
    l = t // seqs
    nc = D_FF // CONV_COLS
    steps = nc * seqs

    def body(dh_ref, wd_ref, g_ref, u_ref, w_ref, b_ref, dgu_ref, dw_ref, db_ref, stage, sems):
        s, j = pl.program_id(0), pl.program_id(1)
        n = s * nc + j
        slot = n % 2

        def copies(slot_, j_, s_):
            rows = pl.ds(pl.multiple_of(s_ * l, 16), l)
            return [pltpu.make_async_copy(
                stage.at[slot_, half],
                dgu_ref.at[rows, pl.ds(pl.multiple_of((half * nc + j_) * CONV_COLS, 128), CONV_COLS)],
                sems.at[slot_, half]) for half in (0, 1)]

        @pl.when(n >= 2)
        def _():
            for cp in copies(slot, j, s):
                cp.wait()

        da = lax.dot_general(dh_ref[...], wd_ref[...], _NT, preferred_element_type=F32)
        g, wv = g_ref[...].astype(F32), w_ref[...]
        row = lax.broadcasted_iota(jnp.int32, g.shape, 0)
        g1, g2 = _shift_down(g, 1, row), _shift_down(g, 2, row)
        pre = b_ref[...] + wv[0:1, :] * g2 + wv[1:2, :] * g1 + wv[2:3, :] * g
        sg = jax.nn.sigmoid(pre)
        silu = pre * sg
        stage[slot, 1] = (da * silu).astype(stage.dtype)
        dpre = da * u_ref[...].astype(F32) * (sg * (1.0 + pre * (1.0 - sg)))
        dg = wv[2:3, :] * dpre + wv[1:2, :] * _shift_up(dpre, 1, row) + wv[0:1, :] * _shift_up(dpre, 2, row)
        stage[slot, 0] = dg.astype(stage.dtype)
        for cp in copies(slot, j, s):
            cp.start()
        dw_ref[0:1, :] = jnp.sum(dpre * g2, axis=0, keepdims=True)
        dw_ref[1:2, :] = jnp.sum(dpre * g1, axis=0, keepdims=True)
        dw_ref[2:3, :] = jnp.sum(dpre * g, axis=0, keepdims=True)
        db_ref[...] = jnp.sum(dpre, axis=0, keepdims=True)

        @pl.when(n == steps - 1)
        def _():
            for cp in copies(slot, j, s) + (copies(1 - slot, j, s) if steps > 1 else []):
                cp.wait()

    cols = pl.BlockSpec((l, CONV_COLS), lambda s, j: (s, j))
    return pl.pallas_call(
        body, name="ffn_down_dx_gate", grid=(seqs, nc),
        in_specs=[pl.BlockSpec((l, dh.shape[1]), lambda s, j: (s, 0)),
                  pl.BlockSpec((CONV_COLS, dh.shape[1]), lambda s, j: (j, 0)), cols, cols,
                  pl.BlockSpec((3, CONV_COLS), lambda s, j: (0, j)), pl.BlockSpec((1, CONV_COLS), lambda s, j: (0, j))],
        out_specs=[ANY, pl.BlockSpec((None, 3, CONV_COLS), lambda s, j: (s, 0, j)),
                   pl.BlockSpec((None, 1, CONV_COLS), lambda s, j: (s, 0, j))],
        out_shape=[jax.ShapeDtypeStruct((t, 2 * D_FF), BF16), jax.ShapeDtypeStruct((seqs, 3, D_FF), F32),
                   jax.ShapeDtypeStruct((seqs, 1, D_FF), F32)],
        scratch_shapes=[pltpu.VMEM((2, 2, l, CONV_COLS), BF16), pltpu.SemaphoreType.DMA((2, 2))],
        compiler_params=_params(("arbitrary", "arbitrary")),
    )(dh, w_down, gate, up, w, b)


def _loss_head(h, target):
    t, d = h.shape
    tm = _pick(t, (256, 128, 8))

    def body(h_ref, t_ref, dh_ref, dhb_ref, loss_ref):
        @pl.when(pl.program_id(0) == 0)
        def _():
            loss_ref[...] = jnp.zeros(loss_ref.shape, F32)

        e = h_ref[...] - t_ref[...]
        dh = e * (1.0 / d)
        dh_ref[...] = dh
        dhb_ref[...] = dh.astype(BF16)
        loss_ref[...] += (0.5 / d) * jnp.sum(jnp.sum(e * e, axis=1, keepdims=True), axis=0, keepdims=True)

    blk = pl.BlockSpec((tm, d), lambda i: (i, 0))
    return pl.pallas_call(
        body, name="loss_head", grid=(t // tm,), in_specs=[blk, blk],
        out_specs=[blk, blk, pl.BlockSpec((1, 1), lambda i: (0, 0))],
        out_shape=[jax.ShapeDtypeStruct((t, d), F32), jax.ShapeDtypeStruct((t, d), BF16),
                   jax.ShapeDtypeStruct((1, 1), F32)],
        compiler_params=_params(("arbitrary",)),
    )(h, target)


def _s5_discretise(a_re, a_im, log_dt, b_re, b_im):
    dt = jnp.exp(log_dt)[:, None]
    mag = jnp.exp(a_re * dt)
    lb_r = mag * jnp.cos(a_im * dt)
    lb_i = mag * jnp.sin(a_im * dt)
    den = a_re * a_re + a_im * a_im
    nr = lb_r - 1.0
    coef_r = (nr * a_re + lb_i * a_im) / den
    coef_i = (lb_i * a_re - nr * a_im) / den
    bb_r = coef_r[:, :, None] * b_re - coef_i[:, :, None] * b_im
    bb_i = coef_r[:, :, None] * b_im + coef_i[:, :, None] * b_re
    return lb_r, lb_i, bb_r, bb_i


S5_CHUNKS = 4
S5_PER = S5_GROUPS // S5_CHUNKS


def _blockdiag_in(bb):
    eye = jnp.eye(S5_PER, dtype=bb.dtype)
    return jnp.einsum("jgpc,gh->jgchp", bb.reshape(S5_CHUNKS, S5_PER, S5_STATE, S5_GROUP_CH), eye).reshape(
        S5_CHUNKS, S5_PER * S5_GROUP_CH, S5_PER * S5_STATE)


def _blockdiag_in_grad(d):
    eye = jnp.eye(S5_PER, dtype=d.dtype)
    return jnp.einsum("jgchp,gh->jgpc", d.reshape(S5_CHUNKS, S5_PER, S5_GROUP_CH, S5_PER, S5_STATE), eye).reshape(
        S5_GROUPS, S5_STATE, S5_GROUP_CH)


def _blockdiag_out(c):
    eye = jnp.eye(S5_PER, dtype=c.dtype)
    return jnp.einsum("jgcp,gh->jgphc", c.reshape(S5_CHUNKS, S5_PER, S5_GROUP_CH, S5_STATE), eye).reshape(
        S5_CHUNKS, S5_PER * S5_STATE, S5_PER * S5_GROUP_CH)


def _blockdiag_out_grad(d):
    eye = jnp.eye(S5_PER, dtype=d.dtype)
    return jnp.einsum("jgphc,gh->jgcp", d.reshape(S5_CHUNKS, S5_PER, S5_STATE, S5_PER, S5_GROUP_CH), eye).reshape(
        S5_GROUPS, S5_GROUP_CH, S5_STATE)


def _local_step(x3, mem3, target3, p, wb, late_weights=None, early_grads=None):
    seqs, l, d = x3.shape
    t = seqs * l
    x = x3.reshape(t, d)
    mem = mem3.reshape(seqs * N_MEM, d)
    target = target3.reshape(t, d)
    full = lambda a: (a, a.shape[1], 0, 0)

    s5_in = (p["s5_a_re"], p["s5_a_im"], p["s5_log_dt"], p["s5_b_re"], p["s5_b_im"])
    (lb_r, lb_i, bb_r, bb_i), s5_pull = jax.vjp(_s5_discretise, *s5_in)
    ar, ai = lb_r.reshape(1, S5_CH), lb_i.reshape(1, S5_CH)
    bbr_d, bbi_d = _blockdiag_in(bb_r).astype(BF16), _blockdiag_in(bb_i).astype(BF16)
    cr_d, ci_d = _blockdiag_out(p["s5_c_re"]).astype(BF16), (-_blockdiag_out(p["s5_c_im"])).astype(BF16)
    d_row = p["s5_d"].reshape(1, S5_WIDTH)

    hn1 = _rowwise(_rms, [full(x)], [p["norm_mix"]], [(d, d, 0, BF16)], "norm_mix_fwd")
    if late_weights is not None:
        wb = dict(wb, **late_weights("first", hn1))
    conv_w = wb["ffn_conv_w"] if "ffn_conv_w" in wb else p["ffn_conv_w"]
    w_in = wb["w_in"]
    w_qkv = w_in[:, :3 * FOX_WIDTH]
    w_uf = jnp.concatenate(
        [w_in[:, 3 * FOX_WIDTH + N_FOX_HEADS:], w_in[:, 3 * FOX_WIDTH:3 * FOX_WIDTH + N_FOX_HEADS],
         jnp.zeros((d, UF_COLS - S5_WIDTH - N_FOX_HEADS), w_in.dtype)], axis=1)
    qkv = _mm(hn1, w_qkv, "nn", "in_qkv")
    uf = _mm(hn1, w_uf, "nn", "in_uf")

    bh = seqs * N_FOX_HEADS
    q_pair = (qkv, 128, 0, 1)
    k_pair = (qkv, 128, N_PAIRS, 1)
    gq2, gk2 = jnp.tile(p["fox_q_norm"], (1, 2)), jnp.tile(p["fox_k_norm"], (1, 2))
    pair_out = [(FOX_WIDTH, 128, 1, BF16)]
    qn = _rowwise(_rms_pair, [q_pair], [gq2], pair_out, "fox_qnorm_fwd", heads=N_PAIRS)
    kn = _rowwise(_rms_pair, [k_pair], [gk2], pair_out, "fox_knorm_fwd", heads=N_PAIRS)

    f_rows = uf[:, S5_WIDTH:S5_WIDTH + N_FOX_HEADS].reshape(seqs, l, N_FOX_HEADS).transpose(0, 2, 1).reshape(bh, l)
    f_bias = jnp.tile(p["fox_f_bias"].reshape(N_FOX_HEADS, 1), (seqs, 1))
    c_wide = jnp.broadcast_to(_forget_fwd(f_rows, f_bias)[:, :, None], (bh, l, 128))
    fox, lse = _fox_fwd(qn, kn, qkv, c_wide, seqs)

    xr, xi, ys = _s5_fwd(uf, bbr_d, bbi_d, cr_d, ci_d, ar, ai, seqs)
    u_blk = (uf, S5_WIDTH, 0, 0)
    yg = _rowwise(_s5_act, [full(ys), u_blk], [d_row], [(S5_WIDTH, S5_WIDTH, 0, F32)], "s5_act_fwd")
    if late_weights is not None:
        wb = dict(wb, **late_weights("mid", yg))
    z = _mm(yg, wb["s5_w_glu"], "nn", "s5_glu")
    y2n = _rowwise(_s5_gate, [full(yg), full(z)], [p["s5_b_glu"], p["out_norm_s5"]],
                   [(S5_WIDTH, S5_WIDTH, 0, BF16)], "s5_gate_fwd")
    foxn = _rowwise(_rms, [full(fox)], [p["out_norm_fox"]], [(FOX_WIDTH, FOX_WIDTH, 0, BF16)], "fox_outnorm_fwd")
    mixed = jnp.concatenate([foxn, y2n], axis=1)
    h1 = _mm(mixed, wb["w_out"], "nn", "mix_out", res=x)
    if late_weights is not None:
        wb = dict(wb, **late_weights("late", h1))

    hn2 = _rowwise(_rms, [full(h1)], [p["norm_cross"]], [(d, d, 0, BF16)], "norm_cross_fwd")
    mn = _rowwise(_rms, [full(mem)], [p["norm_mem"]], [(d, d, 0, BF16)], "norm_mem_fwd")
    xq_raw = _mm(hn2, wb["w_xq"], "nn", "x_q")
    kv = _mm(mn, wb["w_xkv"], "nn", "x_kv")
    xh = lambda a: (a, X_HEAD_DIM, 0, 1)
    xqn = _rowwise(_rms, [xh(xq_raw)], [p["xq_norm"]], [(d, X_HEAD_DIM, 1, BF16)], "x_qnorm_fwd", heads=N_X_HEADS)
    xkn = _rowwise(_rms, [xh(kv)], [p["xk_norm"]], [(d, X_HEAD_DIM, 1, BF16)], "x_knorm_fwd", heads=N_X_HEADS)
    xo = _xatt_fwd(xqn, xkn, kv, seqs)
    h2 = _mm(xo, wb["w_xo"], "nn", "x_out", res=h1)

    hn3 = _rowwise(_rms, [full(h2)], [p["norm_ffn"]], [(d, d, 0, BF16)], "norm_ffn_fwd")
    gate, up, act = _ffn_up_gate(hn3, wb["w_ffn_up"], conv_w, p["ffn_conv_b"], seqs)
    h3 = _mm(act, wb["w_ffn_down"], "nn", "ffn_down", res=h2)
    dh3, dh3_b, loss = _loss_head(h3, target)

    g = {}
    late_dt = BF16 if early_grads is not None else F32
    g["w_ffn_down"] = _mm(act, dh3_b, "tn", "ffn_down_dw", out_dtype=late_dt)
    dgu, dconv_w, dconv_b = _ffn_down_dx_gate(dh3_b, wb["w_ffn_down"], gate, up, conv_w, p["ffn_conv_b"], seqs)
    g["ffn_conv_w"], g["ffn_conv_b"] = jnp.sum(dconv_w, axis=0), jnp.sum(dconv_b, axis=0)
    dhn3 = _mm(dgu, wb["w_ffn_up"], "nt", "ffn_up_dx", out_dtype=BF16)
    g["w_ffn_up"] = _mm(hn3, dgu, "tn", "ffn_up_dw", out_dtype=late_dt)
    (dh2,), (g["norm_ffn"],) = _rowwise_vjp(_rms, [full(h2)], [p["norm_ffn"]], [full(dhn3)], "norm_ffn_bwd",
                                            adds=[full(dh3)])

    dxo = _mm(dh2, wb["w_xo"], "nt", "x_out_dx", out_dtype=BF16)
    g["w_xo"] = _mm(xo, dh2, "tn", "x_out_dw", out_dtype=late_dt)
    dxqn, dxkn, dxv = _xatt_bwd(xqn, xkn, kv, dxo, seqs)
    (dxq_raw,), (g["xq_norm"],) = _rowwise_vjp(_rms, [xh(xq_raw)], [p["xq_norm"]], [xh(dxqn)], "x_qnorm_bwd",
                                               heads=N_X_HEADS, row_dtypes=[BF16])
    (dxk_raw,), (g["xk_norm"],) = _rowwise_vjp(_rms, [xh(kv)], [p["xk_norm"]], [xh(dxkn)], "x_knorm_bwd",
                                               heads=N_X_HEADS, row_dtypes=[BF16])
    dkv = jnp.concatenate([dxk_raw, dxv.astype(BF16)], axis=1)
    dhn2 = _mm(dxq_raw, wb["w_xq"], "nt", "x_q_dx", out_dtype=BF16)
    g["w_xq"] = _mm(hn2, dxq_raw, "tn", "x_q_dw", out_dtype=late_dt)
    dmn = _mm(dkv, wb["w_xkv"], "nt", "x_kv_dx")
    g["w_xkv"] = _mm(mn, dkv, "tn", "x_kv_dw", out_dtype=late_dt)
    norm_cross = p["norm_cross"]
    if early_grads is not None:
        norm_cross = norm_cross + early_grads("late", {n: g[n] for n in LATE_WEIGHTS})[0:1, 0:1]
    (dh1,), (g["norm_cross"],) = _rowwise_vjp(_rms, [full(h1)], [norm_cross], [full(dhn2)], "norm_cross_bwd",
                                              adds=[full(dh2)])
    _, (g["norm_mem"],) = _rowwise_vjp(_rms, [full(mem)], [p["norm_mem"]], [full(dmn)], "norm_mem_bwd",
                                       row_dtypes=[BF16])

    dmixed = _mm(dh1, wb["w_out"], "nt", "mix_out_dx", out_dtype=BF16)
    g["w_out"] = _mm(mixed, dh1, "tn", "mix_out_dw", out_dtype=late_dt)
    (dfox,), (g["out_norm_fox"],) = _rowwise_vjp(_rms, [full(fox)], [p["out_norm_fox"]],
                                                 [(dmixed, FOX_WIDTH, 0, 0)], "fox_outnorm_bwd")
    (dyg_a, dz), (g["s5_b_glu"], g["out_norm_s5"]) = _rowwise_vjp(
        _s5_gate, [full(yg), full(z)], [p["s5_b_glu"], p["out_norm_s5"]], [(dmixed, S5_WIDTH, 1, 0)], "s5_gate_bwd",
        row_dtypes=[F32, BF16])
    dyg = _mm(dz, wb["s5_w_glu"], "nt", "s5_glu_dx", res=dyg_a)
    g["s5_w_glu"] = _mm(yg, dz, "tn", "s5_glu_dw", out_dtype=late_dt)
    if early_grads is not None:
        d_row = d_row + early_grads("mid", {n: g[n] for n in MID_WEIGHTS})[0:1, 0:1]
    (dys, du_a), (dd_row,) = _rowwise_vjp(_s5_act, [full(ys), u_blk], [d_row], [full(dyg)], "s5_act_bwd",
                                          row_dtypes=[BF16, F32])
    g["s5_d"] = dd_row
    du_b, dbbr_d, dbbi_d, dcr_d, dci_d, dar, dai = _s5_bwd(dys, uf, xr, xi, bbr_d, bbi_d, cr_d, ci_d, ar, ai, seqs)
    dbbr_d, dbbi_d, dcr_d, dci_d = (jnp.sum(a, axis=0) for a in (dbbr_d, dbbi_d, dcr_d, dci_d))
    d_lb_r = jnp.sum(dar, axis=0).reshape(S5_GROUPS, S5_STATE)
    d_lb_i = jnp.sum(dai, axis=0).reshape(S5_GROUPS, S5_STATE)
    g["s5_a_re"], g["s5_a_im"], g["s5_log_dt"], g["s5_b_re"], g["s5_b_im"] = s5_pull(
        (d_lb_r, d_lb_i, _blockdiag_in_grad(dbbr_d), _blockdiag_in_grad(dbbi_d)))
    g["s5_c_re"] = _blockdiag_out_grad(dcr_d)
    g["s5_c_im"] = -_blockdiag_out_grad(dci_d)

    dqn, dkn, dv, dc, dcq = _fox_bwd(qn, kn, qkv, c_wide, fox, dfox, lse, seqs)
    pair = lambda a: (a, 128, 0, 1)
    (dq_raw,), (dgq2,) = _rowwise_vjp(_rms_pair, [q_pair], [gq2], [pair(dqn)], "fox_qnorm_bwd", heads=N_PAIRS,
                                      row_dtypes=[BF16])
    (dk_raw,), (dgk2,) = _rowwise_vjp(_rms_pair, [k_pair], [gk2], [pair(dkn)], "fox_knorm_bwd", heads=N_PAIRS,
                                      row_dtypes=[BF16])
    g["fox_q_norm"] = dgq2[:, :HEAD_DIM] + dgq2[:, HEAD_DIM:]
    g["fox_k_norm"] = dgk2[:, :HEAD_DIM] + dgk2[:, HEAD_DIM:]
    df_rows, dfb = _forget_bwd(f_rows, f_bias, (dc + dcq).reshape(bh, l))
    g["fox_f_bias"] = jnp.sum(dfb.reshape(seqs, N_FOX_HEADS), axis=0)
    df = df_rows.reshape(seqs, N_FOX_HEADS, l).transpose(0, 2, 1).reshape(t, N_FOX_HEADS)
    dqkv = jnp.concatenate([dq_raw, dk_raw, dv.astype(BF16)], axis=1)
    duf = jnp.concatenate([du_a + du_b, df, jnp.zeros((t, UF_COLS - S5_WIDTH - N_FOX_HEADS), F32)],
                          axis=1).astype(BF16)
    dhn1 = _mm(duf, w_uf, "nt", "in_uf_dx", res=_mm(dqkv, w_qkv, "nt", "in_qkv_dx"), out_dtype=BF16)
    dw_qkv = _mm(hn1, dqkv, "tn", "in_qkv_dw")
    dw_uf = _mm(hn1, duf, "tn", "in_uf_dw")
    g["w_in"] = jnp.concatenate([dw_qkv, dw_uf[:, S5_WIDTH:S5_WIDTH + N_FOX_HEADS], dw_uf[:, :S5_WIDTH]], axis=1)
    (dx,), (g["norm_mix"],) = _rowwise_vjp(_rms, [full(x)], [p["norm_mix"]], [full(dhn1)], "norm_mix_bwd",
                                           adds=[full(dh1)])
    return loss, dx.reshape(seqs, l, d), g


def _place():
    return lax.axis_index("x"), lax.axis_index("y"), lax.axis_index("c")


def _other_chips(x, y):
    return [(1 - x, y), (x, 1 - y), (1 - x, 1 - y)]


ANY = pl.BlockSpec(memory_space=pl.ANY)


HBM = pl.BlockSpec(memory_space=pltpu.HBM)
SEM = pl.BlockSpec(memory_space=pltpu.SEMAPHORE)
DATAFLOW = pltpu.SideEffectType.DATAFLOW_SIDE_EFFECTING


def _in_hbm(a):
    return pltpu.with_memory_space_constraint(a, pltpu.HBM)


def _split_start(name, srcs, lands, n_copies, plan):
    n = len(srcs)

    def body(*refs):
        src_refs, land_refs = refs[:n], refs[n:2 * n]
        send_sems, recv_sems = refs[2 * n], refs[2 * n + 1]
        for i, (src, dst, dev) in enumerate(plan(src_refs, land_refs)):
            pltpu.make_async_remote_copy(src_ref=src, dst_ref=dst, send_sem=send_sems.at[i], recv_sem=recv_sems.at[i],
                                         device_id=dev, device_id_type=MESH).start()
        refs[-1][...] = jnp.zeros((8, 128), F32)

    res = pl.pallas_call(
        body, name=name, in_specs=[HBM] * (2 * n),
        out_specs=[SEM, SEM] + [HBM] * (2 * n) + [pl.BlockSpec(memory_space=pltpu.VMEM)],
        out_shape=[pltpu.SemaphoreType.DMA((n_copies,)), pltpu.SemaphoreType.DMA((n_copies,))]
        + [pltpu.HBM(a.shape, a.dtype) for a in list(srcs) + list(lands)] + [jax.ShapeDtypeStruct((8, 128), F32)],
        input_output_aliases={i: 2 + i for i in range(2 * n)},
        compiler_params=pltpu.CompilerParams(has_side_effects=DATAFLOW),
    )(*[_in_hbm(a) for a in list(srcs) + list(lands)])
    return res[0], res[1], list(res[2:2 + n]), list(res[2 + n:2 + 2 * n]), res[-1]


def _split_wait(name, send_sems, recv_sems, srcs, lands, after, plan):
    n = len(srcs)

    def body(*refs):
        src_refs, land_refs = refs[:n], refs[n:2 * n]
        send_ref, recv_ref = refs[2 * n], refs[2 * n + 1]
        for i, (src, dst, dev) in enumerate(plan(src_refs, land_refs)):
            cp = pltpu.make_async_remote_copy(src_ref=src, dst_ref=dst, send_sem=send_ref.at[i], recv_sem=recv_ref.at[i],
                                              device_id=dev, device_id_type=MESH)
            cp.wait_send()
            cp.wait_recv()

    res = pl.pallas_call(
        body, name=name, in_specs=[HBM] * (2 * n) + [SEM, SEM, ANY], out_specs=[HBM] * (2 * n),
        out_shape=[pltpu.HBM(a.shape, a.dtype) for a in list(srcs) + list(lands)],
        input_output_aliases={i: i for i in range(2 * n)},
        compiler_params=pltpu.CompilerParams(has_side_effects=DATAFLOW),
    )(*srcs, *lands, send_sems, recv_sems, after)
    return list(res[:n]), list(res[n:])


def _first_gather_plan(src_refs, land_refs):
    x, y, c = _place()
    mine = 2 * x + y
    (src, small), (land, small_land) = src_refs, land_refs
    r = src.shape[0]
    hr = r // 2
    half = src.at[pl.ds(pl.multiple_of(c * hr, 16), hr), :]
    half_dst = land.at[pl.ds(pl.multiple_of(mine * r + c * hr, 16), hr), :]
    copies = [(src, land.at[pl.ds(pl.multiple_of(mine * r, 16), r), :], (x, y, 1 - c)),
              (small, small_land.at[mine], (x, y, 1 - c))]
    for px, py in _other_chips(x, y):
        copies += [(half, half_dst, (px, py, c)), (small, small_land.at[mine], (px, py, c))]
    return copies


def _forward_to_sibling(full):
    def body(full_in, full_ref, send_sems, recv_sems):
        x, y, c = _place()
        r = full_ref.shape[0] // 4
        hr = r // 2
        copies = []
        for j, (px, py) in enumerate(_other_chips(x, y)):
            got = full_ref.at[pl.ds(pl.multiple_of((2 * px + py) * r + c * hr, 16), hr), :]
            cp = pltpu.make_async_remote_copy(
                src_ref=got, dst_ref=got, send_sem=send_sems.at[j], recv_sem=recv_sems.at[j],
                device_id=(x, y, 1 - c), device_id_type=MESH)
            cp.start()
            copies.append(cp)
        for cp in copies:
            cp.wait()

    return pl.pallas_call(
        body, name="gather_first_forward", in_specs=[ANY], out_specs=ANY,
        out_shape=jax.ShapeDtypeStruct(full.shape, full.dtype), input_output_aliases={0: 0},
        scratch_shapes=[pltpu.SemaphoreType.DMA((3,)), pltpu.SemaphoreType.DMA((3,))],
        compiler_params=pltpu.CompilerParams(has_side_effects=True),
    )(full)


def _late_gather_plan(col_kind):
    def plan(src_refs, land_refs):
        x, y, c = _place()
        mine = 2 * x + y
        copies = []
        for a, (src, land) in enumerate(zip(src_refs, land_refs)):
            r, cs = src.shape
            if col_kind[a]:
                dst = land.at[:, pl.ds(pl.multiple_of(mine * cs, 128), cs)]
            else:
                dst = land.at[pl.ds(pl.multiple_of(mine * r, 16), r), :]
            copies.append((src, dst, (x, y, 1 - c)))
            copies += [(src, dst, (px, py, c)) for (px, py) in _other_chips(x, y)]
        return copies
    return plan


def _late_reduce_plan(col_kind):
    def plan(src_refs, land_refs):
        x, y, c = _place()
        copies = []
        for a, (src, land) in enumerate(zip(src_refs, land_refs)):
            for j, (px, py) in enumerate(_other_chips(x, y)):
                if col_kind[a] is None:
                    piece = src
                elif col_kind[a]:
                    cs = land.shape[2]
                    piece = src.at[:, pl.ds(pl.multiple_of((2 * px + py) * cs, 128), cs)]
                else:
                    piece = src.at[2 * px + py]
                copies.append((piece, land.at[j], (px, py, c)))
        return copies
    return plan


def _pair_swap(name, halves):
    n = len(halves)

    def body(*refs):
        ins, outs = refs[:n], refs[n:2 * n]
        send_sems, recv_sems = refs[2 * n:]
        x, y, c = _place()
        copies = []
        for a in range(n):
            cp = pltpu.make_async_remote_copy(
                src_ref=ins[a], dst_ref=outs[a], send_sem=send_sems.at[a], recv_sem=recv_sems.at[a],
                device_id=(x, y, 1 - c), device_id_type=MESH)
            cp.start()
            copies.append(cp)
        for cp in copies:
            cp.wait()

    return pl.pallas_call(
        body, name=name, in_specs=[ANY] * n, out_specs=[ANY] * n,
        out_shape=[jax.ShapeDtypeStruct(s.shape, s.dtype) for s in halves],
        scratch_shapes=[pltpu.SemaphoreType.DMA((n,)), pltpu.SemaphoreType.DMA((n,))],
        compiler_params=pltpu.CompilerParams(has_side_effects=True),
    )(*halves)


def _chip_sum(name, chip_sel, own, col, others, swap=None):
    _, r, c = others.shape
    tr = _pick(r, (256, 128, 64, 32, 16))
    steps = r // tr
    if col:
        own_spec = pl.BlockSpec((tr, c), lambda i, s: (i, s[0]))
    else:
        own_spec = pl.BlockSpec((None, tr, c), lambda i, s: (s[0], i, 0))
    specs = [own_spec] + [pl.BlockSpec((None, tr, c), lambda i, s, k=k: (k, i, 0)) for k in range(3)]
    o_spec = pl.BlockSpec((tr, c), lambda i, s: (i, 0))

    def add(own_ref, r0, r1, r2, o_ref):
        total = ((own_ref[...].astype(F32) + r0[...].astype(F32)) + r1[...].astype(F32)) + r2[...].astype(F32)
        o_ref[...] = total.astype(o_ref.dtype)

    if swap is None:
        def body(s_ref, own_ref, r0, r1, r2, o_ref):
            add(own_ref, r0, r1, r2, o_ref)

        return pl.pallas_call(
            body, name=name,
            grid_spec=pltpu.PrefetchScalarGridSpec(
                num_scalar_prefetch=1, grid=(steps,), in_specs=specs, out_specs=o_spec),
            out_shape=jax.ShapeDtypeStruct((r, c), BF16),
            compiler_params=_params(("parallel",)),
        )(chip_sel, own, others, others, others)

    def body(s_ref, own_ref, r0, r1, r2, mine_ref, o_ref, theirs_ref, send_sem, recv_sem):
        x, y, core = _place()
        cp = pltpu.make_async_remote_copy(
            src_ref=mine_ref, dst_ref=theirs_ref, send_sem=send_sem.at[0], recv_sem=recv_sem.at[0],
            device_id=(x, y, 1 - core), device_id_type=MESH)

        @pl.when(pl.program_id(0) == 0)
        def _():
            cp.start()

        add(own_ref, r0, r1, r2, o_ref)

        @pl.when(pl.program_id(0) == steps - 1)
        def _():
            cp.wait()

    return pl.pallas_call(
        body, name=name,
        grid_spec=pltpu.PrefetchScalarGridSpec(
            num_scalar_prefetch=1, grid=(steps,), in_specs=specs + [ANY], out_specs=[o_spec, ANY],
            scratch_shapes=[pltpu.SemaphoreType.DMA((1,)), pltpu.SemaphoreType.DMA((1,))]),
        out_shape=[jax.ShapeDtypeStruct((r, c), BF16), jax.ShapeDtypeStruct(swap.shape, swap.dtype)],
        compiler_params=pltpu.CompilerParams(dimension_semantics=("arbitrary",), has_side_effects=True,
                                             vmem_limit_bytes=VMEM_LIMIT_BYTES),
    )(chip_sel, own, others, others, others, swap)


def _small_layout(vals):
    sizes = [int(math.prod(v.shape)) for v in vals]
    padded = [-(-s // 128) * 128 for s in sizes]
    return sizes, padded, -(-sum(padded) // 1024) * 1024


def _pack_small(vals):
    sizes, padded, total = _small_layout(vals)
    flat = [jnp.pad(v.reshape(-1), (0, p - s)) for v, s, p in zip(vals, sizes, padded)]
    flat.append(jnp.zeros((total - sum(padded),), F32))
    return jnp.concatenate(flat).reshape(total // 128, 128)


def _allreduce_small(own, others, vals):
    def body(own_ref, oth_ref, out_ref, land, send_sem, recv_sem):
        x, y, c = _place()
        out_ref[...] = (own_ref[...] + oth_ref[0]) + (oth_ref[1] + oth_ref[2])
        cp = pltpu.make_async_remote_copy(
            src_ref=out_ref, dst_ref=land, send_sem=send_sem.at[0], recv_sem=recv_sem.at[0],
            device_id=(x, y, 1 - c), device_id_type=MESH)
        cp.start()
        cp.wait()
        out_ref[...] = out_ref[...] + land[...]

    vm = pl.BlockSpec(memory_space=pltpu.VMEM)
    summed = pl.pallas_call(
        body, name="allreduce_small", in_specs=[vm, vm], out_specs=vm,
        out_shape=jax.ShapeDtypeStruct(own.shape, F32),
        scratch_shapes=[pltpu.VMEM(own.shape, F32), pltpu.SemaphoreType.DMA((1,)), pltpu.SemaphoreType.DMA((1,))],
        compiler_params=pltpu.CompilerParams(has_side_effects=True, vmem_limit_bytes=VMEM_LIMIT_BYTES),
    )(own, others).reshape(-1)
    sizes, padded, _ = _small_layout(vals)
    outs, off = [], 0
    for v, s, p in zip(vals, sizes, padded):
        outs.append(summed[off:off + s].reshape(v.shape))
        off += p
    return outs


def _adamw_math(w, g, m, v):
    m2 = ADAM_B1 * m + (1.0 - ADAM_B1) * g
    v2 = ADAM_B2 * v + (1.0 - ADAM_B2) * (g * g)
    m_hat = m2 / (1.0 - ADAM_B1 ** ADAM_STEP)
    v_hat = v2 / (1.0 - ADAM_B2 ** ADAM_STEP)
    delta = -ADAM_LR * (m_hat / (jnp.sqrt(v_hat) + ADAM_EPS) + ADAM_WD * w)
    return delta, m2, v2


def _adamw_big(name, w, g_mine, g_sibling, m, v):
    _, r, c = w.shape

    def body(w_ref, ga_ref, gb_ref, m_ref, v_ref, go_ref, d_ref, mo_ref, vo_ref):
        gv = ga_ref[...].astype(F32) + gb_ref[...].astype(F32)
        d, m2, v2 = _adamw_math(w_ref[...], gv, m_ref[...], v_ref[...])
        go_ref[...] = gv
        d_ref[...] = d
        mo_ref[...] = m2
        vo_ref[...] = v2

    tr = _pick(r, (256, 128, 64, 32, 16, 8))
    if r % tr == 0 and tr % 8 == 0:
        grid = (r // tr,)
        blk = pl.BlockSpec((None, tr, c), lambda i: (0, i, 0))
        part = pl.BlockSpec((tr, c), lambda i: (i, 0))
    else:
        grid = (c // 512,)
        blk = pl.BlockSpec((None, r, 512), lambda i: (0, 0, i))
        part = pl.BlockSpec((r, 512), lambda i: (0, i))
    return pl.pallas_call(
        body, name=name, grid=grid, in_specs=[blk, part, part, blk, blk], out_specs=[blk] * 4,
        out_shape=[jax.ShapeDtypeStruct((1, r, c), F32)] * 4, compiler_params=_params(("parallel",)),
    )(w, g_mine, g_sibling, m, v)


def _adamw_small(ws, gs, ms, vs):
    n = len(ws)

    def body(*refs):
        w_r, g_r, m_r, v_r = refs[:n], refs[n:2 * n], refs[2 * n:3 * n], refs[3 * n:4 * n]
        o = refs[4 * n:]
        for a in range(n):
            gv = g_r[a][...]
            d, m2, v2 = _adamw_math(w_r[a][...], gv, m_r[a][...], v_r[a][...])
            o[a][...] = gv
            o[n + a][...] = d
            o[2 * n + a][...] = m2
            o[3 * n + a][...] = v2

    res = pl.pallas_call(
        body, name="adamw_small", out_shape=[jax.ShapeDtypeStruct(w.shape, F32) for _ in range(4) for w in ws],
        compiler_params=_params(),
    )(*ws, *gs, *ms, *vs)
    return res[:n], res[n:2 * n], res[2 * n:3 * n], res[3 * n:]


def _full_from_gathered(name, gathered):
    if name == "w_in":
        rows = gathered.shape[0] // 4
        return gathered.reshape(4, rows, gathered.shape[1]).transpose(1, 0, 2).reshape(rows, 4 * gathered.shape[1])
    return gathered


def _reduce_layout(name, full):
    if name in COL_KIND:
        return full
    if name == "w_in":
        rows, cols = full.shape
        return full.reshape(rows, 4, cols // 4).transpose(1, 0, 2)
    return full.reshape(4, full.shape[0] // 4, full.shape[1])


def kernel(x, mem, norm_mix, w_in, fox_q_norm, fox_k_norm, fox_f_bias, s5_a_re, s5_a_im, s5_log_dt, s5_b_re, s5_b_im, s5_c_re, s5_c_im, s5_d, s5_w_glu, s5_b_glu, out_norm_fox, out_norm_s5, w_out, norm_cross, norm_mem, w_xq, w_xkv, xq_norm, xk_norm, w_xo, norm_ffn, w_ffn_up, ffn_conv_w, ffn_conv_b, w_ffn_down, loss_target, m_norm_mix, m_w_in, m_fox_q_norm, m_fox_k_norm, m_fox_f_bias, m_s5_a_re, m_s5_a_im, m_s5_log_dt, m_s5_b_re, m_s5_b_im, m_s5_c_re, m_s5_c_im, m_s5_d, m_s5_w_glu, m_s5_b_glu, m_out_norm_fox, m_out_norm_s5, m_w_out, m_norm_cross, m_norm_mem, m_w_xq, m_w_xkv, m_xq_norm, m_xk_norm, m_w_xo, m_norm_ffn, m_w_ffn_up, m_ffn_conv_w, m_ffn_conv_b, m_w_ffn_down, v_norm_mix, v_w_in, v_fox_q_norm, v_fox_k_norm, v_fox_f_bias, v_s5_a_re, v_s5_a_im, v_s5_log_dt, v_s5_b_re, v_s5_b_im, v_s5_c_re, v_s5_c_im, v_s5_d, v_s5_w_glu, v_s5_b_glu, v_out_norm_fox, v_out_norm_s5, v_w_out, v_norm_cross, v_norm_mem, v_w_xq, v_w_xkv, v_xq_norm, v_xk_norm, v_w_xo, v_norm_ffn, v_w_ffn_up, v_ffn_conv_w, v_ffn_conv_b, v_w_ffn_down):
    given = dict(locals())
    w = {n: given[n] for n in WEIGHTS}
    m = {n: given["m_" + n] for n in WEIGHTS}
    v = {n: given["v_" + n] for n in WEIGHTS}
    xi, yi, _ = _place()
    chip = (2 * xi + yi).astype(jnp.int32)
    chip_sel = chip.reshape(1)

    first_shard, taps = w[FIRST_WEIGHT][0].astype(BF16), w["ffn_conv_w"][0]
    send, recv, srcs, lands, g_started = _split_start(
        "gather_first_start", [first_shard, taps],
        [lax.empty((4 * first_shard.shape[0], first_shard.shape[1]), BF16), lax.empty((4,) + taps.shape, F32)],
        8, _first_gather_plan)
    pending = {"first": ((FIRST_WEIGHT, "ffn_conv_w"), _first_gather_plan, send, recv, srcs, lands)}
    for stage, names in (("mid", MID_WEIGHTS), ("late", LATE_WEIGHTS)):
        kinds = [n in COL_KIND for n in names]
        shards = [w[n][0].astype(BF16) for n in names]
        shards[0] = shards[0] + g_started[0:1, 0:1].astype(BF16)
        full = [lax.empty((s.shape[0], 4 * s.shape[1]) if ck else (4 * s.shape[0], s.shape[1]), BF16)
                for s, ck in zip(shards, kinds)]
        plan = _late_gather_plan(kinds)
        send, recv, srcs, lands, g_started = _split_start(
            "gather_" + stage + "_start", shards, full, 4 * len(names), plan)
        pending[stage] = (names, plan, send, recv, srcs, lands)

    def late_weights(stage, after):
        names, plan, send, recv, srcs, lands = pending[stage]
        _, full = _split_wait("gather_" + stage + "_wait", send, recv, srcs, lands, after, plan)
        if stage == "first":
            full = [_full_from_gathered(FIRST_WEIGHT, _forward_to_sibling(full[0])),
                    full[1].transpose(1, 0, 2).reshape(3, D_FF)]
        return dict(zip(names, full))

    reducing = {}

    def start_reduce(stage, grads_by_name, whole=()):
        names = list(grads_by_name)
        kinds = [n in COL_KIND for n in names]
        grads = [_reduce_layout(n, grads_by_name[n].astype(BF16)) for n in names]
        lands = [lax.empty((3, s.shape[0], s.shape[1] // 4) if ck else (3,) + s.shape[1:], BF16)
                 for s, ck in zip(grads, kinds)]
        plan = _late_reduce_plan(kinds + [None] * len(whole))
        send, recv, srcs, lands, started = _split_start(
            "reduce_" + stage + "_start", grads + list(whole),
            lands + [lax.empty((3,) + a.shape, a.dtype) for a in whole], 3 * (len(names) + len(whole)), plan)
        reducing[stage] = (names, kinds, plan, send, recv, srcs, lands)
        return started

    p = {n: w[n][0] for n in SMALL}
    for n in ("norm_mix", "fox_q_norm", "fox_k_norm", "fox_f_bias", "s5_b_glu", "out_norm_fox", "out_norm_s5",
              "norm_cross", "norm_mem", "xq_norm", "xk_norm", "norm_ffn", "ffn_conv_b"):
        p[n] = p[n].reshape(1, -1)
    p["norm_mix"] = p["norm_mix"] + g_started[0:1, 0:1]
    loss, grad_x, g = _local_step(x, mem, loss_target, p, {}, late_weights, start_reduce)

    small_names = list(SMALL) + ["ffn_conv_w"]
    small_vals = [g[n].reshape(w[n].shape if n != "ffn_conv_w" else (1, 3, D_FF)) for n in small_names] + [loss]
    after = start_reduce("first", {FIRST_WEIGHT: g[FIRST_WEIGHT]}, whole=[_pack_small(small_vals)])

    out_g, out_d, out_m, out_v = {}, {}, {}, {}

    def finish(stage, after):
        names, kinds, plan, send, recv, srcs, lands = reducing[stage]
        sums, from_chips = _split_wait("reduce_" + stage + "_wait", send, recv, srcs, lands, after, plan)
        order = sorted(range(len(names)), key=lambda i: -math.prod(from_chips[i].shape))
        mine, theirs, prev = {}, {}, None
        for i in order:
            res = _chip_sum("reduce_chip_sum_" + names[i], chip_sel, sums[i], kinds[i], from_chips[i],
                            swap=None if prev is None else mine[prev])
            if prev is None:
                mine[i] = res
            else:
                mine[i], theirs[prev] = res
            prev = i
        (theirs[prev],) = _pair_swap("reduce_pair_swap_" + stage, [mine[prev]])
        for i, n in enumerate(names):
            a, b = mine[i], theirs[i]
            if n == "w_in":
                flip = lambda t: jnp.swapaxes(t, -1, -2)
                res = _adamw_big("adamw_" + n, flip(w[n]), flip(a), flip(b), flip(m[n]), flip(v[n]))
                out_g[n], out_d[n], out_m[n], out_v[n] = (flip(t) for t in res)
                continue
            out_g[n], out_d[n], out_m[n], out_v[n] = _adamw_big("adamw_" + n, w[n], a, b, m[n], v[n])
        return sums[len(names):], from_chips[len(names):], out_v[names[-1]]

    _, _, after = finish("late", after)
    _, _, after = finish("mid", after)
    (small_own,), (small_others,), _ = finish("first", after)

    reduced = _allreduce_small(small_own, small_others, small_vals)
    loss_all = reduced[-1].reshape(())
    conv_w_grad = lax.dynamic_slice_in_dim(reduced[-2], chip * (D_FF // 4), D_FF // 4, axis=2)
    sg, sd, sm, sv = _adamw_small(
        [w[n] for n in small_names], list(reduced[:len(SMALL)]) + [conv_w_grad],
        [m[n] for n in small_names], [v[n] for n in small_names])
    out_g.update(zip(small_names, sg))
    out_d.update(zip(small_names, sd))
    out_m.update(zip(small_names, sm))
    out_v.update(zip(small_names, sv))

    return (loss_all, grad_x, *[out_g[n] for n in WEIGHTS], *[out_d[n] for n in WEIGHTS],
            *[out_m[n] for n in WEIGHTS], *[out_v[n] for n in WEIGHTS])
```

```python
import math

import jax
import jax.numpy as jnp
from jax import lax
from jax.experimental import pallas as pl
from jax.experimental.pallas import tpu as pltpu

F32 = jnp.float32
BF16 = jnp.bfloat16

D_MODEL = 1024
FOX_WIDTH = 512
HEAD_DIM = 64
N_FOX_HEADS = 8
S5_WIDTH = 512
S5_GROUP_CH = 16
S5_GROUPS = 32
S5_STATE = 64
S5_CH = S5_GROUPS * S5_STATE
N_X_HEADS = 4
X_HEAD_DIM = 256
N_MEM = 256
D_FF = 2816
UF_COLS = 640
EPS = 1e-6
ADAM_LR = 0.001
ADAM_B1 = 0.9
ADAM_B2 = 0.999
ADAM_EPS = 1e-08
ADAM_WD = 0.01
ADAM_STEP = 10

VMEM_LIMIT_BYTES = 56 * 1024 * 1024
MM_BLOCK_BYTES = 6 * 1024 * 1024
MM_VMEM_BYTES = 40 * 1024 * 1024
MM_TILE_MAX = 1536
MESH = pl.DeviceIdType.MESH

FIRST_WEIGHT = "w_in"
MID_WEIGHTS = ("s5_w_glu", "w_out")
EARLY_WEIGHTS = (FIRST_WEIGHT,) + MID_WEIGHTS
LATE_WEIGHTS = ("w_xq", "w_xkv", "w_xo", "w_ffn_up", "w_ffn_down")
BIG = EARLY_WEIGHTS + LATE_WEIGHTS
COL_KIND = ("w_xkv", "w_ffn_up")
SMALL = ("norm_mix", "fox_q_norm", "fox_k_norm", "fox_f_bias", "s5_a_re", "s5_a_im", "s5_log_dt",
         "s5_b_re", "s5_b_im", "s5_c_re", "s5_c_im", "s5_d", "s5_b_glu", "out_norm_fox", "out_norm_s5",
         "norm_cross", "norm_mem", "xq_norm", "xk_norm", "norm_ffn", "ffn_conv_b")
WEIGHTS = ("norm_mix", "w_in", "fox_q_norm", "fox_k_norm", "fox_f_bias", "s5_a_re", "s5_a_im", "s5_log_dt",
           "s5_b_re", "s5_b_im", "s5_c_re", "s5_c_im", "s5_d", "s5_w_glu", "s5_b_glu", "out_norm_fox",
           "out_norm_s5", "w_out", "norm_cross", "norm_mem", "w_xq", "w_xkv", "xq_norm", "xk_norm", "w_xo",
           "norm_ffn", "w_ffn_up", "ffn_conv_w", "ffn_conv_b", "w_ffn_down")


def _params(sem=None):
    return pltpu.CompilerParams(dimension_semantics=sem, vmem_limit_bytes=VMEM_LIMIT_BYTES)


def _pick(n, cands):
    for c in cands:
        if n % c == 0:
            return c
    return n


_DIMS = {"nn": (((1,), (0,)), ((), ())), "nt": (((1,), (1,)), ((), ())), "tn": (((0,), (0,)), ((), ()))}


def _mm(a, b, mode, name, out_dtype=F32, res=None):
    if mode == "nn":
        (m, k), (k2, n) = a.shape, b.shape
    elif mode == "nt":
        (m, k), (n, k2) = a.shape, b.shape
    else:
        (k, m), (k2, n) = a.shape, b.shape
    assert k == k2, (name, a.shape, b.shape)

    has_res = res is not None
    a_size, b_size = a.dtype.itemsize, b.dtype.itemsize
    o_size = jnp.dtype(out_dtype).itemsize + (res.dtype.itemsize if has_res else 0)

    def tiles(dim):
        return [c for c in range(MM_TILE_MAX, 0, -128) if dim % c == 0] or [dim]

    best = None
    for tm in tiles(m):
        for tn in tiles(n):
            a_blk, b_blk = tm * k * a_size, tn * k * b_size
            if max(a_blk, b_blk) > MM_BLOCK_BYTES or 2 * (a_blk + b_blk + tm * tn * o_size) > MM_VMEM_BYTES:
                continue
            for rows_outer in (True, False):
                moved = (m * k * a_size + (m // tm) * n * k * b_size) if rows_outer else \
                        (n * k * b_size + (n // tn) * m * k * a_size)
                key = (moved, -(tm * tn))
                if best is None or key < best[0]:
                    best = (key, tm, tn, rows_outer)
    assert best is not None, (name, a.shape, b.shape)
    _, tm, tn, rows_outer = best
    ij = (lambda g0, g1: (g0, g1)) if rows_outer else (lambda g0, g1: (g1, g0))
    if mode == "tn":
        a_spec = pl.BlockSpec((k, tm), lambda g0, g1: (0, ij(g0, g1)[0]))
    else:
        a_spec = pl.BlockSpec((tm, k), lambda g0, g1: (ij(g0, g1)[0], 0))
    if mode == "nt":
        b_spec = pl.BlockSpec((tn, k), lambda g0, g1: (ij(g0, g1)[1], 0))
    else:
        b_spec = pl.BlockSpec((k, tn), lambda g0, g1: (0, ij(g0, g1)[1]))
    o_spec = pl.BlockSpec((tm, tn), lambda g0, g1: ij(g0, g1))
    grid = (m // tm, n // tn) if rows_outer else (n // tn, m // tm)
    dims = _DIMS[mode]

    def body(*refs):
        a_ref, b_ref = refs[0], refs[1]
        o_ref = refs[-1]
        acc = lax.dot_general(a_ref[...].astype(BF16), b_ref[...].astype(BF16), dims, preferred_element_type=F32)
        if has_res:
            acc = acc + refs[2][...].astype(F32)
        o_ref[...] = acc.astype(o_ref.dtype)

    return pl.pallas_call(
        body, name=name, grid=grid,
        in_specs=[a_spec, b_spec] + ([o_spec] if has_res else []),
        out_specs=o_spec, out_shape=jax.ShapeDtypeStruct((m, n), out_dtype),
        compiler_params=_params(("parallel", "parallel")),
    )(*((a, b, res) if has_res else (a, b)))


def _row_spec(tm, bc, off, step):
    return pl.BlockSpec((tm, bc), lambda i, h: (i, off + step * h))


ROW_TILE_ELEMS = 512 * 1024


def _row_tile(t, rows):
    widest = max(bc for (_, bc, _, _) in rows)
    return _pick(t, (min(t, ROW_TILE_ELEMS // widest), 512, 256, 128, 64, 8))


def _rowwise(fn, rows, pars, outs, name, heads=1):
    t = rows[0][0].shape[0]
    tm = _row_tile(t, rows)
    nr, npar = len(rows), len(pars)

    def body(*refs):
        vals = [r[...].astype(F32) for r in refs[:nr + npar]]
        res = fn(*vals)
        if not isinstance(res, (tuple, list)):
            res = (res,)
        for o_ref, v in zip(refs[nr + npar:], res):
            o_ref[...] = v.astype(o_ref.dtype)

    in_specs = [_row_spec(tm, bc, off, st) for (_, bc, off, st) in rows]
    in_specs += [pl.BlockSpec(p.shape, lambda i, h: (0, 0)) for p in pars]
    out_specs = [_row_spec(tm, bc, 0, st) for (_, bc, st, _) in outs]
    out_shape = [jax.ShapeDtypeStruct((t, c), dt) for (c, _, _, dt) in outs]
    res = pl.pallas_call(
        body, name=name, grid=(t // tm, heads), in_specs=in_specs, out_specs=out_specs, out_shape=out_shape,
        compiler_params=_params(("parallel", "parallel")),
    )(*[r[0] for r in rows], *pars)
    return res[0] if len(res) == 1 else res


def _rowwise_vjp(fn, rows, pars, cts, name, heads=1, adds=None, row_dtypes=None):
    t = rows[0][0].shape[0]
    tm = _row_tile(t, rows)
    nr, npar, nct = len(rows), len(pars), len(cts)
    adds = adds or [None] * nr
    add_list = [a for a in adds if a is not None]
    row_dtypes = row_dtypes or [F32] * nr

    def body(*refs):
        i, h = pl.program_id(0), pl.program_id(1)
        p = 0
        row_v = [r[...].astype(F32) for r in refs[p:p + nr]]; p += nr
        par_v = [r[...].astype(F32) for r in refs[p:p + npar]]; p += npar
        ct_v = [r[...].astype(F32) for r in refs[p:p + nct]]; p += nct
        add_refs = refs[p:p + len(add_list)]; p += len(add_list)
        drow_refs = refs[p:p + nr]; p += nr
        dpar_refs = refs[p:p + npar]

        def wrapped(*a):
            r = fn(*a)
            return tuple(r) if isinstance(r, (tuple, list)) else (r,)

        _, pull = jax.vjp(wrapped, *row_v, *par_v)
        grads = pull(tuple(ct_v))
        ai = 0
        for k in range(nr):
            g = grads[k]
            if adds[k] is not None:
                g = g + add_refs[ai][...].astype(F32)
                ai += 1
            drow_refs[k][...] = g.astype(drow_refs[k].dtype)

        @pl.when((i == 0) & (h == 0))
        def _():
            for r in dpar_refs:
                r[...] = jnp.zeros(r.shape, r.dtype)

        for k in range(npar):
            dpar_refs[k][...] += grads[nr + k]

    in_specs = [_row_spec(tm, bc, off, st) for (_, bc, off, st) in rows]
    in_specs += [pl.BlockSpec(q.shape, lambda i, h: (0, 0)) for q in pars]
    in_specs += [_row_spec(tm, bc, off, st) for (_, bc, off, st) in cts]
    in_specs += [_row_spec(tm, bc, off, st) for (_, bc, off, st) in add_list]
    out_specs = [_row_spec(tm, bc, 0, st) for (_, bc, _, st) in rows]
    out_specs += [pl.BlockSpec(q.shape, lambda i, h: (0, 0)) for q in pars]
    out_shape = [jax.ShapeDtypeStruct((t, bc * (heads if st else 1)), dt) for (_, bc, _, st), dt in zip(rows, row_dtypes)]
    out_shape += [jax.ShapeDtypeStruct(q.shape, F32) for q in pars]
    res = pl.pallas_call(
        body, name=name, grid=(t // tm, heads), in_specs=in_specs, out_specs=out_specs, out_shape=out_shape,
        compiler_params=_params(("arbitrary", "arbitrary")),
    )(*[r[0] for r in rows], *pars, *[c[0] for c in cts], *[a[0] for a in add_list])
    return list(res[:nr]), list(res[nr:])


def _rms(x, g):
    return x * lax.rsqrt(jnp.mean(x * x, axis=-1, keepdims=True) + EPS) * g


def _rms_pair(x, g):
    left = lax.broadcasted_iota(jnp.int32, x.shape, 1) < HEAD_DIM
    x2 = x * x
    ms_a = jnp.sum(jnp.where(left, x2, 0.0), axis=-1, keepdims=True) * (1.0 / HEAD_DIM)
    ms_b = jnp.sum(jnp.where(left, 0.0, x2), axis=-1, keepdims=True) * (1.0 / HEAD_DIM)
    return x * lax.rsqrt(jnp.where(left, ms_a, ms_b) + EPS) * g


def _gelu(x):
    return 0.5 * x * (1.0 + jnp.tanh(math.sqrt(2.0 / math.pi) * (x + 0.044715 * (x * x * x))))


def _s5_act(ys, u, d):
    return _gelu(ys + d * u)


def _s5_gate(yg, z, b, g):
    return _rms(yg * jax.nn.sigmoid(z + b), g)


def _lane_cumsum(x, reverse):
    n = x.shape[-1]
    lane = lax.broadcasted_iota(jnp.int32, x.shape, 1)
    k = 1
    while k < n:
        if reverse:
            x = x + jnp.where(lane < n - k, pltpu.roll(x, n - k, 1), 0.0)
        else:
            x = x + jnp.where(lane >= k, pltpu.roll(x, k, 1), 0.0)
        k *= 2
    return x


def _log_sigmoid(z):
    return jnp.minimum(z, 0.0) - jnp.log(1.0 + jnp.exp(-jnp.abs(z)))


def _forget_fwd(f, bias):
    def body(f_ref, b_ref, c_ref):
        c_ref[...] = _lane_cumsum(_log_sigmoid(f_ref[...] + b_ref[...]), False)

    return pl.pallas_call(body, name="forget_fwd", out_shape=jax.ShapeDtypeStruct(f.shape, F32),
                          compiler_params=_params())(f, bias)


def _forget_bwd(f, bias, dc):
    def body(f_ref, b_ref, dc_ref, df_ref, db_ref):
        dlog = _lane_cumsum(dc_ref[...], True)
        df = dlog * jax.nn.sigmoid(-(f_ref[...] + b_ref[...]))
        df_ref[...] = df
        db_ref[...] = jnp.sum(df, axis=1, keepdims=True)

    return pl.pallas_call(body, name="forget_bwd",
                          out_shape=(jax.ShapeDtypeStruct(f.shape, F32), jax.ShapeDtypeStruct(bias.shape, F32)),
                          compiler_params=_params())(f, bias, dc)


FOX_BLOCK = 1024
FOX_KEYS = 1024
FOX_BWD_BLOCK = 512
_NT = _DIMS["nt"]
_TN = _DIMS["tn"]


N_PAIRS = N_FOX_HEADS // 2
V_BLOCK0 = 2 * N_PAIRS


def _left_lanes(shape):
    return lax.broadcasted_iota(jnp.int32, shape, 1) < HEAD_DIM


def _top_rows(shape):
    return lax.broadcasted_iota(jnp.int32, shape, 0) < HEAD_DIM


def _wide(c_tile, n):
    return c_tile if n == 128 else jnp.concatenate([c_tile] * (n // 128), axis=1)


def _fox_fwd(qn, kn, qkv, c_wide, seqs):
    t = qn.shape[0]
    l = t // seqs
    tb = min(FOX_BLOCK, l)
    tk = min(FOX_KEYS, tb)
    ratio = tb // tk
    nb = l // tb
    scale = HEAD_DIM ** -0.5

    def body(q_ref, k_ref, v_ref, ca_ref, cb_ref, o_ref, lse_ref, vt_ref):
        i = pl.program_id(2)
        top = _top_rows((128, tb))

        @pl.when(i == 0)
        def _():
            vt_ref[...] = v_ref[...].T.astype(BF16)

        qt = (q_ref[...].astype(F32) * scale).T.astype(BF16)
        zero = jnp.zeros_like(qt)
        qts = (jnp.where(top, qt, zero), jnp.where(top, zero, qt))
        top_k = _top_rows((128, tk))
        zero_k = jnp.zeros((128, tk), BF16)
        key_pos = lax.broadcasted_iota(jnp.int32, (tk, tb), 0)
        query_pos = lax.broadcasted_iota(jnp.int32, (tk, tb), 1)
        c_refs = (ca_ref, cb_ref)

        def scores(j):
            off = pl.multiple_of(j * tk, tk)
            k2 = k_ref[pl.ds(off, tk), :]
            return tuple(jnp.dot(k2, qts[h], preferred_element_type=F32) - _wide(c_refs[h][pl.ds(off, tk), :], tb)
                         for h in (0, 1))

        def values_times(ps, j):
            vt = vt_ref[:, pl.ds(pl.multiple_of(j * tk, tk), tk)]
            return (jnp.dot(jnp.where(top_k, vt, zero_k), ps[0], preferred_element_type=F32)
                    + jnp.dot(jnp.where(top_k, zero_k, vt), ps[1], preferred_element_type=F32))

        def softmax_step(sts, stats, first_key):
            ps, new, alphas = [], [], []
            for st, (m, s_sum) in zip(sts, stats):
                if first_key is not None:
                    st = jnp.where(key_pos + first_key <= query_pos, st, -jnp.inf)
                m_new = jnp.maximum(m, jnp.max(st, axis=0, keepdims=True))
                alpha = jnp.exp(m - m_new)
                p = jnp.exp(st - m_new)
                new.append((m_new, alpha * s_sum + jnp.sum(p, axis=0, keepdims=True)))
                alphas.append(alpha)
                ps.append(p.astype(BF16))
            return tuple(ps), tuple(new), jnp.where(top, alphas[0], alphas[1])

        def tile(j, carry, first_key):
            stats, acc = carry
            ps, stats, alpha = softmax_step(scores(j), stats, first_key)
            return stats, alpha * acc + values_times(ps, j)

        stat = (jnp.full((1, tb), -jnp.inf, F32), jnp.zeros((1, tb), F32))
        below = i * ratio
        carry = lax.fori_loop(0, below, lambda j, c: tile(j, c, None), ((stat, stat), jnp.zeros((128, tb), F32)))
        for r in range(ratio):
            carry = tile(below + r, carry, r * tk)
        ((ma, sa), (mb, sb)), acc = carry
        o_ref[...] = (acc / jnp.where(top, sa, sb)).T
        lse_ref[0:1, :] = ma + jnp.log(sa)
        lse_ref[1:2, :] = mb + jnp.log(sb)

    qblk = pl.BlockSpec((tb, 128), lambda b, hp, i: (b * nb + i, hp))
    return pl.pallas_call(
        body, name="fox_fwd", grid=(seqs, N_PAIRS, nb),
        in_specs=[qblk, pl.BlockSpec((l, 128), lambda b, hp, i: (b, hp)),
                  pl.BlockSpec((l, 128), lambda b, hp, i: (b, V_BLOCK0 + hp)),
                  pl.BlockSpec((None, l, 128), lambda b, hp, i: (b * N_FOX_HEADS + 2 * hp, 0, 0)),
                  pl.BlockSpec((None, l, 128), lambda b, hp, i: (b * N_FOX_HEADS + 2 * hp + 1, 0, 0))],
        out_specs=[qblk, pl.BlockSpec((None, 2, tb), lambda b, hp, i: (b * N_PAIRS + hp, 0, i))],
        out_shape=[jax.ShapeDtypeStruct((t, FOX_WIDTH), F32), jax.ShapeDtypeStruct((seqs * N_PAIRS, 2, l), F32)],
        scratch_shapes=[pltpu.VMEM((128, l), BF16)],
        compiler_params=_params(("parallel", "parallel", "arbitrary")),
    )(qn, kn, qkv, c_wide, c_wide)


def _fox_bwd(qn, kn, qkv, c_wide, o, do, lse, seqs):
    t = qn.shape[0]
    l = t // seqs
    tb = min(FOX_BWD_BLOCK, l)
    nb = l // tb
    scale = HEAD_DIM ** -0.5
    one_at = (HEAD_DIM, 0)

    def body(q_ref, k_ref, v_ref, ca_ref, cb_ref, o_ref, do_ref, lse_ref, dq_ref, dk_ref, dv_ref, dc_ref, dcq_ref,
             qt_ref, kt_ref, dot_ref, delta_ref, dqa_ref, dqb_ref):
        top_l = _top_rows((128, l))
        top = _top_rows((128, tb))
        left = _left_lanes((tb, 128))
        row_id = lax.broadcasted_iota(jnp.int32, (128, tb), 0)
        lane_id = lax.broadcasted_iota(jnp.int32, (tb, 128), 1)
        zero_t = jnp.zeros((128, tb), BF16)
        zero_l = jnp.zeros((tb, 128), BF16)
        rows = lambda a: (jnp.where(top, a, zero_t), jnp.where(top, zero_t, a))
        lanes = lambda a: (jnp.where(left, a, zero_l), jnp.where(left, zero_l, a))
        with_one_row = lambda pair: tuple(jnp.where(row_id == one_at[h], 1.0, pair[h]).astype(BF16) for h in (0, 1))
        with_one_lane = lambda pair: tuple(jnp.where(lane_id == one_at[h], 1.0, pair[h]).astype(BF16) for h in (0, 1))
        causal = lax.broadcasted_iota(jnp.int32, (tb, tb), 0) <= lax.broadcasted_iota(jnp.int32, (tb, tb), 1)
        c_refs = (ca_ref, cb_ref)
        dq_refs = (dqa_ref, dqb_ref)

        qt_ref[...] = (q_ref[...].astype(F32) * scale).T.astype(BF16)
        kt_ref[...] = k_ref[...].astype(F32).T.astype(BF16)
        do_t = do_ref[...].T
        dot_ref[...] = do_t.astype(BF16)
        prod_t = do_t * o_ref[...].T
        delta_ref[0:1, :] = jnp.sum(jnp.where(top_l, prod_t, 0.0), axis=0, keepdims=True)
        delta_ref[1:2, :] = jnp.sum(jnp.where(top_l, 0.0, prod_t), axis=0, keepdims=True)
        dqa_ref[...] = jnp.zeros(dqa_ref.shape, F32)
        dqb_ref[...] = jnp.zeros(dqb_ref.shape, F32)

        def kv_block(j, _):
            koff = pl.multiple_of(j * tb, tb)
            k2 = k_ref[pl.ds(koff, tb), :]
            v2 = v_ref[pl.ds(koff, tb), :].astype(BF16)
            kts = with_one_row(rows(kt_ref[:, pl.ds(koff, tb)]))
            cw = tuple(_wide(c_refs[h][pl.ds(koff, tb), :], tb) for h in (0, 1))

            def q_block(i, carry, masked):
                dks, dv = list(carry[:2]), carry[2]
                qoff = pl.multiple_of(i * tb, tb)
                qs = lanes((q_ref[pl.ds(qoff, tb), :].astype(F32) * scale).astype(BF16))
                qs_one = with_one_lane(qs)
                dos = lanes(do_ref[pl.ds(qoff, tb), :].astype(BF16))
                qts = rows(qt_ref[:, pl.ds(qoff, tb)])
                dots = rows(dot_ref[:, pl.ds(qoff, tb)])
                for h in (0, 1):
                    st = jnp.dot(k2, qts[h], preferred_element_type=F32) - cw[h]
                    p = jnp.exp(st - lse_ref[h:h + 1, pl.ds(qoff, tb)])
                    if masked:
                        p = jnp.where(causal, p, 0.0)
                    dp = jnp.dot(v2, dots[h], preferred_element_type=F32)
                    dsb = (p * (dp - delta_ref[h:h + 1, pl.ds(qoff, tb)])).astype(BF16)
                    dv = dv + jnp.dot(p.astype(BF16), dos[h], preferred_element_type=F32)
                    dks[h] = dks[h] + jnp.dot(dsb, qs_one[h], preferred_element_type=F32)
                    dq_refs[h][:, pl.ds(qoff, tb)] += jnp.dot(kts[h], dsb, preferred_element_type=F32)
                return dks[0], dks[1], dv

            z = jnp.zeros((tb, 128), F32)
            carry = q_block(j, (z, z, z), True)
            rest = nb - 1 - j
            carry = lax.fori_loop(
                0, rest // 2, lambda n, c: q_block(j + 2 + 2 * n, q_block(j + 1 + 2 * n, c, False), False), carry)
            dka, dkb, dv = lax.cond(rest % 2 == 1, lambda c: q_block(nb - 1, c, False), lambda c: c, carry)
            dk_ref[pl.ds(koff, tb), :] = jnp.where(left, dka, dkb)
            dv_ref[pl.ds(koff, tb), :] = dv
            dc_ref[0:1, pl.ds(koff, tb)] = -dka.T[one_at[0]:one_at[0] + 1, :]
            dc_ref[1:2, pl.ds(koff, tb)] = -dkb.T[one_at[1]:one_at[1] + 1, :]
            return 0

        lax.fori_loop(0, nb, kv_block, 0)
        dq_ref[...] = (jnp.where(top_l, dqa_ref[...], dqb_ref[...]) * scale).T
        dcq_ref[0:1, :] = dqa_ref[one_at[0]:one_at[0] + 1, :]
        dcq_ref[1:2, :] = dqb_ref[one_at[1]:one_at[1] + 1, :]

    blk = pl.BlockSpec((l, 128), lambda b, hp: (b, hp))
    cspec = lambda k: pl.BlockSpec((None, l, 128), lambda b, hp: (b * N_FOX_HEADS + 2 * hp + k, 0, 0))
    rows2 = pl.BlockSpec((None, 2, l), lambda b, hp: (b * N_PAIRS + hp, 0, 0))
    wide = jax.ShapeDtypeStruct((t, FOX_WIDTH), F32)
    pair_rows = jax.ShapeDtypeStruct((seqs * N_PAIRS, 2, l), F32)
    return pl.pallas_call(
        body, name="fox_bwd", grid=(seqs, N_PAIRS),
        in_specs=[blk, blk, pl.BlockSpec((l, 128), lambda b, hp: (b, V_BLOCK0 + hp)), cspec(0), cspec(1), blk, blk, rows2],
        out_specs=[blk, blk, blk, rows2, rows2],
        out_shape=[wide, wide, wide, pair_rows, pair_rows],
        scratch_shapes=[pltpu.VMEM((128, l), BF16), pltpu.VMEM((128, l), BF16), pltpu.VMEM((128, l), BF16),
                        pltpu.VMEM((2, l), F32), pltpu.VMEM((128, l), F32), pltpu.VMEM((128, l), F32)],
        compiler_params=_params(("parallel", "parallel")),
    )(qn, kn, qkv, c_wide, c_wide, o, do, lse)


SCAN_ROWS = 512
SCAN_COLS = 1024


S5_IN = 128
S5_ST = 512
SCAN_CHUNKS = SCAN_COLS // S5_ST
SCAN_SEGS = 8
LANES = 128


def _cmul(ar, ai, br, bi):
    return ar * br - ai * bi, ar * bi + ai * br


def _powers_into(pw_r, pw_i, a_r, a_i, seg):
    pw_r[0:1, :] = a_r
    pw_i[0:1, :] = a_i
    for k in range(1, seg):
        pr, pi = _cmul(pw_r[k - 1:k, :], pw_i[k - 1:k, :], a_r, a_i)
        pw_r[k:k + 1, :] = pr
        pw_i[k:k + 1, :] = pi


def _interleave(dst, src, seg):
    for h in range(src.shape[0]):
        for j in range(seg):
            dst[h, j * SCAN_SEGS:(j + 1) * SCAN_SEGS, :] = src[h, pl.ds(j, SCAN_SEGS, stride=seg), :]


def _deinterleave(dst, src, seg):
    for h in range(src.shape[0]):
        for j in range(seg):
            dst[h, pl.ds(j, SCAN_SEGS, stride=seg), :] = src[h, j * SCAN_SEGS:(j + 1) * SCAN_SEGS, :]


def _interleaved(ref, tmp_a, tmp_b, seg):
    n = ref.shape[1] // LANES
    for h in range(n):
        tmp_a[h] = ref[:, h * LANES:(h + 1) * LANES].astype(F32)
    _interleave(tmp_b, tmp_a, seg)
    return jnp.concatenate([tmp_b[h] for h in range(n)], axis=1)


def _store_deinterleaved(ref, val, tmp_a, tmp_b, seg):
    n = ref.shape[1] // LANES
    for h in range(n):
        tmp_a[h] = val[:, h * LANES:(h + 1) * LANES]
    _deinterleave(tmp_b, tmp_a, seg)
    for h in range(n):
        ref[:, h * LANES:(h + 1) * LANES] = tmp_b[h]


def _segment_scan(b_r, b_i, x_r, x_i, pw_r, pw_i, car_r, car_i, seg, sign, reverse, visit=None):
    nc = b_r.shape[0]
    sub = lax.broadcasted_iota(jnp.int32, (SCAN_SEGS, LANES), 0)
    lanes = lambda c: slice(c * LANES, (c + 1) * LANES)
    rows = lambda j: pl.ds(pl.multiple_of(((seg - 1 - j) if reverse else j) * SCAN_SEGS, SCAN_SEGS), SCAN_SEGS)
    a1 = [(pw_r[0:1, lanes(c)], sign * pw_i[0:1, lanes(c)]) for c in range(nc)]

    def local(j, xs):
        out = []
        for c in range(nc):
            xr, xi = xs[2 * c], xs[2 * c + 1]
            nr = a1[c][0] * xr - a1[c][1] * xi + b_r[c, rows(j), :]
            ni = a1[c][0] * xi + a1[c][1] * xr + b_i[c, rows(j), :]
            x_r[c, rows(j), :] = nr
            x_i[c, rows(j), :] = ni
            out += [nr, ni]
        return tuple(out)

    zero = jnp.zeros((SCAN_SEGS, LANES), F32)
    ends = lax.fori_loop(0, seg, local, (zero,) * (2 * nc))

    if reverse:
        first = sub == SCAN_SEGS - 1
        neighbour = lambda v: pltpu.roll(v, SCAN_SEGS - 1, 0)
        shift = lambda v, d: jnp.where(sub < SCAN_SEGS - d, pltpu.roll(v, SCAN_SEGS - d, 0), 0.0)
    else:
        first = sub == 0
        neighbour = lambda v: pltpu.roll(v, 1, 0)
        shift = lambda v, d: jnp.where(sub >= d, pltpu.roll(v, d, 0), 0.0)
    last = 0 if reverse else SCAN_SEGS - 1
    entries = []
    for c in range(nc):
        er, ei = ends[2 * c], ends[2 * c + 1]
        pr, pi = pw_r[seg - 1:seg, lanes(c)], sign * pw_i[seg - 1:seg, lanes(c)]
        yr = jnp.where(first, car_r[:, lanes(c)], neighbour(er))
        yi = jnp.where(first, car_i[:, lanes(c)], neighbour(ei))
        qr, qi = pr, pi
        for d in (1, 2, 4):
            mr, mi = _cmul(qr, qi, shift(yr, d), shift(yi, d))
            yr, yi = yr + mr, yi + mi
            qr, qi = _cmul(qr, qi, qr, qi)
        lr, li = _cmul(pr, pi, yr, yi)
        car_r[:, lanes(c)] = (er + lr)[last:last + 1, :]
        car_i[:, lanes(c)] = (ei + li)[last:last + 1, :]
        entries += [yr, yi]

    def correct(j, prev):
        out = []
        row_r, row_i = pw_r[pl.ds(j, 1), :], sign * pw_i[pl.ds(j, 1), :]
        for c in range(nc):
            mr, mi = _cmul(row_r[:, lanes(c)], row_i[:, lanes(c)], entries[2 * c], entries[2 * c + 1])
            nr = x_r[c, rows(j), :] + mr
            ni = x_i[c, rows(j), :] + mi
            x_r[c, rows(j), :] = nr
            x_i[c, rows(j), :] = ni
            if visit is not None:
                visit(c, rows(j), prev[2 * c], prev[2 * c + 1])
            out += [nr, ni]
        return tuple(out)

    lax.fori_loop(0, seg, correct, tuple(entries))


def _s5_fwd(uf, bbr, bbi, cr, ci, ar, ai, seqs):
    t = uf.shape[0]
    l = t // seqs
    tl = min(SCAN_ROWS, l)
    nl = l // tl
    seg = tl // SCAN_SEGS
    cb, nq = SCAN_COLS, SCAN_CHUNKS
    nc = cb // LANES
    per = S5_ST // LANES

    def body(u_ref, bbr_ref, bbi_ref, cr_ref, ci_ref, ar_ref, ai_ref, x_r, x_i, ys_ref,
             car_r, car_i, pw_r, pw_i, b_r, b_i, tmp_a, tmp_b):
        @pl.when(pl.program_id(2) == 0)
        def _():
            car_r[...] = jnp.zeros(car_r.shape, F32)
            car_i[...] = jnp.zeros(car_i.shape, F32)
            _powers_into(pw_r, pw_i, ar_ref[...], ai_ref[...], seg)

        u = _interleaved(u_ref, tmp_a, tmp_b, seg).astype(BF16)
        for q in range(nq):
            uq = u[:, q * S5_IN:(q + 1) * S5_IN]
            br = jnp.dot(uq, bbr_ref[q], preferred_element_type=F32)
            bi = jnp.dot(uq, bbi_ref[q], preferred_element_type=F32)
            for s in range(per):
                b_r[q * per + s] = br[:, s * LANES:(s + 1) * LANES]
                b_i[q * per + s] = bi[:, s * LANES:(s + 1) * LANES]
        _segment_scan(b_r, b_i, x_r, x_i, pw_r, pw_i, car_r, car_i, seg, 1.0, False)
        wide = lambda buf, q: jnp.concatenate([buf[q * per + s] for s in range(per)], axis=1).astype(BF16)
        ys = [jnp.dot(wide(x_r, q), cr_ref[q], preferred_element_type=F32)
              + jnp.dot(wide(x_i, q), ci_ref[q], preferred_element_type=F32) for q in range(nq)]
        _store_deinterleaved(ys_ref, jnp.concatenate(ys, axis=1), tmp_a, tmp_b, seg)

    rows = lambda w: pl.BlockSpec((tl, w), lambda s, j, r: (s * nl + r, j))
    state = pl.BlockSpec((nc, tl, LANES), lambda s, j, r: (j, s * nl + r, 0))
    chunk = lambda a: pl.BlockSpec((nq,) + a.shape[1:], lambda s, j, r: (j, 0, 0))
    par = pl.BlockSpec((1, cb), lambda s, j, r: (0, j))
    return pl.pallas_call(
        body, name="s5_fwd", grid=(seqs, S5_CH // cb, nl),
        in_specs=[rows(nq * S5_IN), chunk(bbr), chunk(bbi), chunk(cr), chunk(ci), par, par],
        out_specs=[state, state, rows(nq * S5_IN)],
        out_shape=[jax.ShapeDtypeStruct((S5_CH // LANES, t, LANES), F32)] * 2
        + [jax.ShapeDtypeStruct((t, S5_WIDTH), F32)],
        scratch_shapes=[pltpu.VMEM((1, cb), F32), pltpu.VMEM((1, cb), F32), pltpu.VMEM((seg, cb), F32),
                        pltpu.VMEM((seg, cb), F32)] + [pltpu.VMEM((nc, tl, LANES), F32)] * 2
        + [pltpu.VMEM((nq * S5_IN // LANES, tl, LANES), F32)] * 2,
        compiler_params=_params(("parallel", "parallel", "arbitrary")),
    )(uf, bbr, bbi, cr, ci, ar, ai)


def _s5_bwd(dys, uf, xr, xi, bbr, bbi, cr, ci, ar, ai, seqs):
    t = dys.shape[0]
    l = t // seqs
    tl = min(SCAN_ROWS, l)
    nl = l // tl
    seg = tl // SCAN_SEGS
    cb, nq = SCAN_COLS, SCAN_CHUNKS
    nc = cb // LANES
    per = S5_ST // LANES

    def body(dy_ref, u_ref, x_r, x_i, bbr_ref, bbi_ref, cr_ref, ci_ref, ar_ref, ai_ref,
             du_ref, dbbr_ref, dbbi_ref, dcr_ref, dci_ref, dar_ref, dai_ref,
             car_r, car_i, pw_r, pw_i, g_r, g_i, lam_r, lam_i, acc_r, acc_i, tmp_a, tmp_b):
        @pl.when(pl.program_id(2) == 0)
        def _():
            car_r[...] = jnp.zeros(car_r.shape, F32)
            car_i[...] = jnp.zeros(car_i.shape, F32)
            _powers_into(pw_r, pw_i, ar_ref[...], ai_ref[...], seg)
            for acc_ref in (dbbr_ref, dbbi_ref, dcr_ref, dci_ref, dar_ref, dai_ref):
                acc_ref[...] = jnp.zeros(acc_ref.shape, F32)

        dy = _interleaved(dy_ref, tmp_a, tmp_b, seg).astype(BF16)
        for q in range(nq):
            dyq = dy[:, q * S5_IN:(q + 1) * S5_IN]
            gr = lax.dot_general(dyq, cr_ref[q], _NT, preferred_element_type=F32)
            gi = lax.dot_general(dyq, ci_ref[q], _NT, preferred_element_type=F32)
            for s in range(per):
                g_r[q * per + s] = gr[:, s * LANES:(s + 1) * LANES]
                g_i[q * per + s] = gi[:, s * LANES:(s + 1) * LANES]
        acc_r[...] = jnp.zeros(acc_r.shape, F32)
        acc_i[...] = jnp.zeros(acc_i.shape, F32)

        def visit(c, rws, lr, li):
            xr_t, xi_t = x_r[c, rws, :], x_i[c, rws, :]
            acc_r[c] += lr * xr_t + li * xi_t
            acc_i[c] += li * xr_t - lr * xi_t

        _segment_scan(g_r, g_i, lam_r, lam_i, pw_r, pw_i, car_r, car_i, seg, -1.0, True, visit)
        for c in range(nc):
            dar_ref[:, c * LANES:(c + 1) * LANES] += jnp.sum(acc_r[c], axis=0, keepdims=True)
            dai_ref[:, c * LANES:(c + 1) * LANES] += jnp.sum(acc_i[c], axis=0, keepdims=True)
        u = _interleaved(u_ref, tmp_a, tmp_b, seg).astype(BF16)
        wide = lambda buf, q: jnp.concatenate([buf[q * per + s] for s in range(per)], axis=1).astype(BF16)
        du = []
        for q in range(nq):
            io = slice(q * S5_IN, (q + 1) * S5_IN)
            lq_r, lq_i = wide(lam_r, q), wide(lam_i, q)
            du.append(lax.dot_general(lq_r, bbr_ref[q], _NT, preferred_element_type=F32)
                      + lax.dot_general(lq_i, bbi_ref[q], _NT, preferred_element_type=F32))
            dbbr_ref[q] += lax.dot_general(u[:, io], lq_r, _TN, preferred_element_type=F32)
            dbbi_ref[q] += lax.dot_general(u[:, io], lq_i, _TN, preferred_element_type=F32)
            dcr_ref[q] += lax.dot_general(wide(x_r, q), dy[:, io], _TN, preferred_element_type=F32)
            dci_ref[q] += lax.dot_general(wide(x_i, q), dy[:, io], _TN, preferred_element_type=F32)
        _store_deinterleaved(du_ref, jnp.concatenate(du, axis=1), tmp_a, tmp_b, seg)

    rows = lambda w: pl.BlockSpec((tl, w), lambda s, j, r: (s * nl + nl - 1 - r, j))
    state = pl.BlockSpec((nc, tl, LANES), lambda s, j, r: (j, s * nl + nl - 1 - r, 0))
    chunk = lambda a: pl.BlockSpec((nq,) + a.shape[1:], lambda s, j, r: (j, 0, 0))
    acc = lambda a: pl.BlockSpec((None, nq) + a.shape[1:], lambda s, j, r: (s, j, 0, 0))
    par = pl.BlockSpec((1, cb), lambda s, j, r: (0, j))
    par_acc = pl.BlockSpec((None, 1, cb), lambda s, j, r: (s, 0, j))
    per_seq = lambda a: jax.ShapeDtypeStruct((seqs,) + a.shape, F32)
    return pl.pallas_call(
        body, name="s5_bwd", grid=(seqs, S5_CH // cb, nl),
        in_specs=[rows(nq * S5_IN), rows(nq * S5_IN), state, state, chunk(bbr), chunk(bbi), chunk(cr), chunk(ci),
                  par, par],
        out_specs=[rows(nq * S5_IN), acc(bbr), acc(bbi), acc(cr), acc(ci), par_acc, par_acc],
        out_shape=[jax.ShapeDtypeStruct((t, S5_WIDTH), F32), per_seq(bbr), per_seq(bbi), per_seq(cr), per_seq(ci),
                   jax.ShapeDtypeStruct((seqs, 1, S5_CH), F32), jax.ShapeDtypeStruct((seqs, 1, S5_CH), F32)],
        scratch_shapes=[pltpu.VMEM((1, cb), F32), pltpu.VMEM((1, cb), F32), pltpu.VMEM((seg, cb), F32),
                        pltpu.VMEM((seg, cb), F32)] + [pltpu.VMEM((nc, tl, LANES), F32)] * 4
        + [pltpu.VMEM((nc, SCAN_SEGS, LANES), F32)] * 2 + [pltpu.VMEM((nq * S5_IN // LANES, tl, LANES), F32)] * 2,
        compiler_params=_params(("parallel", "parallel", "arbitrary")),
    )(dys, uf, xr, xi, bbr, bbi, cr, ci, ar, ai)


XATT_BLOCK = 2048


def _xatt_probs(qv, kv):
    s = lax.dot_general(qv, kv, _NT, preferred_element_type=F32) * (X_HEAD_DIM ** -0.5)
    e = jnp.exp(s - jnp.max(s, axis=-1, keepdims=True))
    return e / jnp.sum(e, axis=-1, keepdims=True)


def _xatt_fwd(q, k, kv, seqs):
    t = q.shape[0]
    tq = min(XATT_BLOCK, t // seqs)
    nq = t // seqs // tq

    def body(q_ref, k_ref, v_ref, o_ref):
        p = _xatt_probs(q_ref[...], k_ref[...])
        o_ref[...] = jnp.dot(p.astype(BF16), v_ref[...].astype(BF16), preferred_element_type=F32).astype(o_ref.dtype)

    qs = pl.BlockSpec((tq, X_HEAD_DIM), lambda b, h, i: (b * nq + i, h))
    return pl.pallas_call(
        body, name="xatt_fwd", grid=(seqs, N_X_HEADS, nq),
        in_specs=[qs, pl.BlockSpec((N_MEM, X_HEAD_DIM), lambda b, h, i: (b, h)),
                  pl.BlockSpec((N_MEM, X_HEAD_DIM), lambda b, h, i: (b, N_X_HEADS + h))],
        out_specs=qs, out_shape=jax.ShapeDtypeStruct(q.shape, BF16),
        compiler_params=_params(("parallel", "parallel", "parallel")),
    )(q, k, kv)


def _xatt_bwd(q, k, kv, do, seqs):
    t = q.shape[0]
    tq = min(XATT_BLOCK, t // seqs)
    nq = t // seqs // tq
    scale = X_HEAD_DIM ** -0.5

    def body(q_ref, k_ref, v_ref, do_ref, dq_ref, dk_ref, dv_ref):
        @pl.when(pl.program_id(2) == 0)
        def _():
            dk_ref[...] = jnp.zeros(dk_ref.shape, F32)
            dv_ref[...] = jnp.zeros(dv_ref.shape, F32)

        qv, kk = q_ref[...], k_ref[...]
        p = _xatt_probs(qv, kk)
        dob = do_ref[...].astype(BF16)
        dp = lax.dot_general(dob, v_ref[...].astype(BF16), _NT, preferred_element_type=F32)
        ds = p * (dp - jnp.sum(dp * p, axis=-1, keepdims=True))
        dsb = ds.astype(BF16)
        dq_ref[...] = jnp.dot(dsb, kk, preferred_element_type=F32) * scale
        dk_ref[...] += lax.dot_general(dsb, qv, _TN, preferred_element_type=F32) * scale
        dv_ref[...] += lax.dot_general(p.astype(BF16), dob, _TN, preferred_element_type=F32)

    qs = pl.BlockSpec((tq, X_HEAD_DIM), lambda b, h, i: (b * nq + i, h))
    ks = pl.BlockSpec((N_MEM, X_HEAD_DIM), lambda b, h, i: (b, h))
    return pl.pallas_call(
        body, name="xatt_bwd", grid=(seqs, N_X_HEADS, nq),
        in_specs=[qs, ks, pl.BlockSpec((N_MEM, X_HEAD_DIM), lambda b, h, i: (b, N_X_HEADS + h)), qs],
        out_specs=[qs, ks, ks],
        out_shape=[jax.ShapeDtypeStruct(q.shape, F32), jax.ShapeDtypeStruct(k.shape, F32),
                   jax.ShapeDtypeStruct(k.shape, F32)],
        compiler_params=_params(("parallel", "parallel", "arbitrary")),
    )(q, k, kv, do)


CONV_COLS = 256


def _shift_down(x, k, row):
    return jnp.where(row >= k, pltpu.roll(x, k, 0), 0.0)


def _shift_up(x, k, row):
    n = x.shape[0]
    return jnp.where(row < n - k, pltpu.roll(x, n - k, 0), 0.0)


def _down_from(x, prev, k, row):
    return jnp.where(row >= k, pltpu.roll(x, k, 0), pltpu.roll(prev, k, 0))


GATE_ROWS = 512


def _ffn_up_gate(hn, w_up, w, b, seqs):
    t = hn.shape[0]
    l = t // seqs
    nc = D_FF // CONV_COLS

    rc = min(GATE_ROWS, l)

    def body(a_ref, wg_ref, wu_ref, w_ref, b_ref, g_ref, u_ref, p_ref, o_ref):
        wv, bias = w_ref[...], b_ref[...]
        row = lax.broadcasted_iota(jnp.int32, (rc, CONV_COLS), 0)
        prev = jnp.zeros((rc, CONV_COLS), F32)
        for k in range(l // rc):
            rows = slice(k * rc, (k + 1) * rc)
            a = a_ref[rows, :]
            gb = jnp.dot(a, wg_ref[...], preferred_element_type=F32).astype(BF16)
            ub = jnp.dot(a, wu_ref[...], preferred_element_type=F32).astype(BF16)
            g_ref[rows, :] = gb
            u_ref[rows, :] = ub
            g = gb.astype(F32)
            pre = bias + wv[0:1, :] * _down_from(g, prev, 2, row) + wv[1:2, :] * _down_from(g, prev, 1, row) \
                + wv[2:3, :] * g
            p_ref[rows, :] = pre.astype(p_ref.dtype)
            o_ref[rows, :] = (pre * jax.nn.sigmoid(pre) * ub.astype(F32)).astype(o_ref.dtype)
            prev = g

    cols = pl.BlockSpec((l, CONV_COLS), lambda s, j: (s, j))
    half = jax.ShapeDtypeStruct((t, D_FF), BF16)
    return pl.pallas_call(
        body, name="ffn_up_gate", grid=(seqs, nc),
        in_specs=[pl.BlockSpec((l, hn.shape[1]), lambda s, j: (s, 0)),
                  pl.BlockSpec((hn.shape[1], CONV_COLS), lambda s, j: (0, j)),
                  pl.BlockSpec((hn.shape[1], CONV_COLS), lambda s, j: (0, nc + j)),
                  pl.BlockSpec((3, CONV_COLS), lambda s, j: (0, j)), pl.BlockSpec((1, CONV_COLS), lambda s, j: (0, j))],
        out_specs=[cols] * 4, out_shape=[half] * 4,
        compiler_params=_params(("parallel", "parallel")),
    )(hn, w_up, w_up, w, b)


def _ffn_down_dx_gate(dh, w_down, gate, up, pre, w, seqs):
    t = dh.shape[0]
    l = t // seqs
    nc = D_FF // CONV_COLS
    steps = nc * seqs

    def body(dh_ref, wd_ref, g_ref, u_ref, p_ref, w_ref, dgu_ref, dw_ref, db_ref, stage, sems):
        s, j = pl.program_id(0), pl.program_id(1)
        n = s * nc + j
        slot = n % 2

        def copies(slot_, j_, s_):
            rows = pl.ds(pl.multiple_of(s_ * l, 16), l)
            return [pltpu.make_async_copy(
                stage.at[slot_, half],
                dgu_ref.at[rows, pl.ds(pl.multiple_of((half * nc + j_) * CONV_COLS, 128), CONV_COLS)],
                sems.at[slot_, half]) for half in (0, 1)]

        @pl.when(n >= 2)
        def _():
            for cp in copies(slot, j, s):
                cp.wait()

        da = lax.dot_general(dh_ref[...], wd_ref[...], _NT, preferred_element_type=F32)
        g, pre, wv = g_ref[...].astype(F32), p_ref[...].astype(F32), w_ref[...]
        row = lax.broadcasted_iota(jnp.int32, g.shape, 0)
        sg = jax.nn.sigmoid(pre)
        silu = pre * sg
        stage[slot, 1] = (da * silu).astype(stage.dtype)
        dpre = da * u_ref[...].astype(F32) * (sg * (1.0 + pre * (1.0 - sg)))
        dpre1, dpre2 = _shift_up(dpre, 1, row), _shift_up(dpre, 2, row)
        dg = wv[2:3, :] * dpre + wv[1:2, :] * dpre1 + wv[0:1, :] * dpre2
        stage[slot, 0] = dg.astype(stage.dtype)
        for cp in copies(slot, j, s):
            cp.start()
        dw_ref[0:1, :] = jnp.sum(dpre2 * g, axis=0, keepdims=True)
        dw_ref[1:2, :] = jnp.sum(dpre1 * g, axis=0, keepdims=True)
        dw_ref[2:3, :] = jnp.sum(dpre * g, axis=0, keepdims=True)
        db_ref[...] = jnp.sum(dpre, axis=0, keepdims=True)

        @pl.when(n == steps - 1)
        def _():
            for cp in copies(slot, j, s) + (copies(1 - slot, j, s) if steps > 1 else []):
                cp.wait()

    cols = pl.BlockSpec((l, CONV_COLS), lambda s, j: (s, j))
    return pl.pallas_call(
        body, name="ffn_down_dx_gate", grid=(seqs, nc),
        in_specs=[pl.BlockSpec((l, dh.shape[1]), lambda s, j: (s, 0)),
                  pl.BlockSpec((CONV_COLS, dh.shape[1]), lambda s, j: (j, 0)), cols, cols, cols,
                  pl.BlockSpec((3, CONV_COLS), lambda s, j: (0, j))],
        out_specs=[ANY, pl.BlockSpec((None, 3, CONV_COLS), lambda s, j: (s, 0, j)),
                   pl.BlockSpec((None, 1, CONV_COLS), lambda s, j: (s, 0, j))],
        out_shape=[jax.ShapeDtypeStruct((t, 2 * D_FF), BF16), jax.ShapeDtypeStruct((seqs, 3, D_FF), F32),
                   jax.ShapeDtypeStruct((seqs, 1, D_FF), F32)],
        scratch_shapes=[pltpu.VMEM((2, 2, l, CONV_COLS), BF16), pltpu.SemaphoreType.DMA((2, 2))],
        compiler_params=_params(("arbitrary", "arbitrary")),
    )(dh, w_down, gate, up, pre, w)


def _loss_head(h, target):
    t, d = h.shape
    tm = _pick(t, (256, 128, 8))

    def body(h_ref, t_ref, dh_ref, dhb_ref, loss_ref):
        @pl.when(pl.program_id(0) == 0)
        def _():
            loss_ref[...] = jnp.zeros(loss_ref.shape, F32)

        e = h_ref[...] - t_ref[...]
        dh = e * (1.0 / d)
        dh_ref[...] = dh
        dhb_ref[...] = dh.astype(BF16)
        loss_ref[...] += (0.5 / d) * jnp.sum(jnp.sum(e * e, axis=1, keepdims=True), axis=0, keepdims=True)

    blk = pl.BlockSpec((tm, d), lambda i: (i, 0))
    return pl.pallas_call(
        body, name="loss_head", grid=(t // tm,), in_specs=[blk, blk],
        out_specs=[blk, blk, pl.BlockSpec((1, 1), lambda i: (0, 0))],
        out_shape=[jax.ShapeDtypeStruct((t, d), F32), jax.ShapeDtypeStruct((t, d), BF16),
                   jax.ShapeDtypeStruct((1, 1), F32)],
        compiler_params=_params(("arbitrary",)),
    )(h, target)


def _s5_discretise(a_re, a_im, log_dt, b_re, b_im):
    dt = jnp.exp(log_dt)[:, None]
    mag = jnp.exp(a_re * dt)
    lb_r = mag * jnp.cos(a_im * dt)
    lb_i = mag * jnp.sin(a_im * dt)
    den = a_re * a_re + a_im * a_im
    nr = lb_r - 1.0
    coef_r = (nr * a_re + lb_i * a_im) / den
    coef_i = (lb_i * a_re - nr * a_im) / den
    bb_r = coef_r[:, :, None] * b_re - coef_i[:, :, None] * b_im
    bb_i = coef_r[:, :, None] * b_im + coef_i[:, :, None] * b_re
    return lb_r, lb_i, bb_r, bb_i


S5_CHUNKS = 4
S5_PER = S5_GROUPS // S5_CHUNKS


def _blockdiag_in(bb):
    eye = jnp.eye(S5_PER, dtype=bb.dtype)
    return jnp.einsum("jgpc,gh->jgchp", bb.reshape(S5_CHUNKS, S5_PER, S5_STATE, S5_GROUP_CH), eye).reshape(
        S5_CHUNKS, S5_PER * S5_GROUP_CH, S5_PER * S5_STATE)


def _blockdiag_in_grad(d):
    eye = jnp.eye(S5_PER, dtype=d.dtype)
    return jnp.einsum("jgchp,gh->jgpc", d.reshape(S5_CHUNKS, S5_PER, S5_GROUP_CH, S5_PER, S5_STATE), eye).reshape(
        S5_GROUPS, S5_STATE, S5_GROUP_CH)


def _blockdiag_out(c):
    eye = jnp.eye(S5_PER, dtype=c.dtype)
    return jnp.einsum("jgcp,gh->jgphc", c.reshape(S5_CHUNKS, S5_PER, S5_GROUP_CH, S5_STATE), eye).reshape(
        S5_CHUNKS, S5_PER * S5_STATE, S5_PER * S5_GROUP_CH)


def _blockdiag_out_grad(d):
    eye = jnp.eye(S5_PER, dtype=d.dtype)
    return jnp.einsum("jgphc,gh->jgcp", d.reshape(S5_CHUNKS, S5_PER, S5_STATE, S5_PER, S5_GROUP_CH), eye).reshape(
        S5_GROUPS, S5_GROUP_CH, S5_STATE)


def _local_step(x3, mem3, target3, p, wb, late_weights=None, early_grads=None):
    seqs, l, d = x3.shape
    t = seqs * l
    x = x3.reshape(t, d)
    mem = mem3.reshape(seqs * N_MEM, d)
    target = target3.reshape(t, d)
    full = lambda a: (a, a.shape[1], 0, 0)

    s5_in = (p["s5_a_re"], p["s5_a_im"], p["s5_log_dt"], p["s5_b_re"], p["s5_b_im"])
    (lb_r, lb_i, bb_r, bb_i), s5_pull = jax.vjp(_s5_discretise, *s5_in)
    ar, ai = lb_r.reshape(1, S5_CH), lb_i.reshape(1, S5_CH)
    bbr_d, bbi_d = _blockdiag_in(bb_r).astype(BF16), _blockdiag_in(bb_i).astype(BF16)
    cr_d, ci_d = _blockdiag_out(p["s5_c_re"]).astype(BF16), (-_blockdiag_out(p["s5_c_im"])).astype(BF16)
    d_row = p["s5_d"].reshape(1, S5_WIDTH)

    hn1 = _rowwise(_rms, [full(x)], [p["norm_mix"]], [(d, d, 0, BF16)], "norm_mix_fwd")
    if late_weights is not None:
        wb = dict(wb, **late_weights("first", hn1))
    conv_w = wb["ffn_conv_w"] if "ffn_conv_w" in wb else p["ffn_conv_w"]
    w_in = wb["w_in"]
    w_qkv = w_in[:, :3 * FOX_WIDTH]
    w_uf = jnp.concatenate(
        [w_in[:, 3 * FOX_WIDTH + N_FOX_HEADS:], w_in[:, 3 * FOX_WIDTH:3 * FOX_WIDTH + N_FOX_HEADS],
         jnp.zeros((d, UF_COLS - S5_WIDTH - N_FOX_HEADS), w_in.dtype)], axis=1)
    qkv = _mm(hn1, w_qkv, "nn", "in_qkv")
    uf = _mm(hn1, w_uf, "nn", "in_uf")

    bh = seqs * N_FOX_HEADS
    q_pair = (qkv, 128, 0, 1)
    k_pair = (qkv, 128, N_PAIRS, 1)
    gq2, gk2 = jnp.tile(p["fox_q_norm"], (1, 2)), jnp.tile(p["fox_k_norm"], (1, 2))
    pair_out = [(FOX_WIDTH, 128, 1, BF16)]
    qn = _rowwise(_rms_pair, [q_pair], [gq2], pair_out, "fox_qnorm_fwd", heads=N_PAIRS)
    kn = _rowwise(_rms_pair, [k_pair], [gk2], pair_out, "fox_knorm_fwd", heads=N_PAIRS)

    f_rows = uf[:, S5_WIDTH:S5_WIDTH + N_FOX_HEADS].reshape(seqs, l, N_FOX_HEADS).transpose(0, 2, 1).reshape(bh, l)
    f_bias = jnp.tile(p["fox_f_bias"].reshape(N_FOX_HEADS, 1), (seqs, 1))
    c_wide = jnp.broadcast_to(_forget_fwd(f_rows, f_bias)[:, :, None], (bh, l, 128))
    fox, lse = _fox_fwd(qn, kn, qkv, c_wide, seqs)

    xr, xi, ys = _s5_fwd(uf, bbr_d, bbi_d, cr_d, ci_d, ar, ai, seqs)
    u_blk = (uf, S5_WIDTH, 0, 0)
    yg = _rowwise(_s5_act, [full(ys), u_blk], [d_row], [(S5_WIDTH, S5_WIDTH, 0, F32)], "s5_act_fwd")
    if late_weights is not None:
        wb = dict(wb, **late_weights("mid", yg))
    z = _mm(yg, wb["s5_w_glu"], "nn", "s5_glu")
    y2n = _rowwise(_s5_gate, [full(yg), full(z)], [p["s5_b_glu"], p["out_norm_s5"]],
                   [(S5_WIDTH, S5_WIDTH, 0, BF16)], "s5_gate_fwd")
    foxn = _rowwise(_rms, [full(fox)], [p["out_norm_fox"]], [(FOX_WIDTH, FOX_WIDTH, 0, BF16)], "fox_outnorm_fwd")
    mixed = jnp.concatenate([foxn, y2n], axis=1)
    h1 = _mm(mixed, wb["w_out"], "nn", "mix_out", res=x)
    if late_weights is not None:
        wb = dict(wb, **late_weights("late", h1))

    hn2 = _rowwise(_rms, [full(h1)], [p["norm_cross"]], [(d, d, 0, BF16)], "norm_cross_fwd")
    mn = _rowwise(_rms, [full(mem)], [p["norm_mem"]], [(d, d, 0, BF16)], "norm_mem_fwd")
    xq_raw = _mm(hn2, wb["w_xq"], "nn", "x_q")
    kv = _mm(mn, wb["w_xkv"], "nn", "x_kv")
    xh = lambda a: (a, X_HEAD_DIM, 0, 1)
    xqn = _rowwise(_rms, [xh(xq_raw)], [p["xq_norm"]], [(d, X_HEAD_DIM, 1, BF16)], "x_qnorm_fwd", heads=N_X_HEADS)
    xkn = _rowwise(_rms, [xh(kv)], [p["xk_norm"]], [(d, X_HEAD_DIM, 1, BF16)], "x_knorm_fwd", heads=N_X_HEADS)
    xo = _xatt_fwd(xqn, xkn, kv, seqs)
    h2 = _mm(xo, wb["w_xo"], "nn", "x_out", res=h1)

    hn3 = _rowwise(_rms, [full(h2)], [p["norm_ffn"]], [(d, d, 0, BF16)], "norm_ffn_fwd")
    gate, up, pre, act = _ffn_up_gate(hn3, wb["w_ffn_up"], conv_w, p["ffn_conv_b"], seqs)
    h3 = _mm(act, wb["w_ffn_down"], "nn", "ffn_down", res=h2)
    dh3, dh3_b, loss = _loss_head(h3, target)

    g = {}
    late_dt = BF16 if early_grads is not None else F32
    g["w_ffn_down"] = _mm(act, dh3_b, "tn", "ffn_down_dw", out_dtype=late_dt)
    dgu, dconv_w, dconv_b = _ffn_down_dx_gate(dh3_b, wb["w_ffn_down"], gate, up, pre, conv_w, seqs)
    g["ffn_conv_w"], g["ffn_conv_b"] = jnp.sum(dconv_w, axis=0), jnp.sum(dconv_b, axis=0)
    dhn3 = _mm(dgu, wb["w_ffn_up"], "nt", "ffn_up_dx", out_dtype=BF16)
    g["w_ffn_up"] = _mm(hn3, dgu, "tn", "ffn_up_dw", out_dtype=late_dt)
    (dh2,), (g["norm_ffn"],) = _rowwise_vjp(_rms, [full(h2)], [p["norm_ffn"]], [full(dhn3)], "norm_ffn_bwd",
                                            adds=[full(dh3)])

    dxo = _mm(dh2, wb["w_xo"], "nt", "x_out_dx", out_dtype=BF16)
    g["w_xo"] = _mm(xo, dh2, "tn", "x_out_dw", out_dtype=late_dt)
    dxqn, dxkn, dxv = _xatt_bwd(xqn, xkn, kv, dxo, seqs)
    (dxq_raw,), (g["xq_norm"],) = _rowwise_vjp(_rms, [xh(xq_raw)], [p["xq_norm"]], [xh(dxqn)], "x_qnorm_bwd",
                                               heads=N_X_HEADS, row_dtypes=[BF16])
    (dxk_raw,), (g["xk_norm"],) = _rowwise_vjp(_rms, [xh(kv)], [p["xk_norm"]], [xh(dxkn)], "x_knorm_bwd",
                                               heads=N_X_HEADS, row_dtypes=[BF16])
    dkv = jnp.concatenate([dxk_raw, dxv.astype(BF16)], axis=1)
    dhn2 = _mm(dxq_raw, wb["w_xq"], "nt", "x_q_dx", out_dtype=BF16)
    g["w_xq"] = _mm(hn2, dxq_raw, "tn", "x_q_dw", out_dtype=late_dt)
    dmn = _mm(dkv, wb["w_xkv"], "nt", "x_kv_dx")
    g["w_xkv"] = _mm(mn, dkv, "tn", "x_kv_dw", out_dtype=late_dt)
    norm_cross = p["norm_cross"]
    if early_grads is not None:
        norm_cross = norm_cross + early_grads("late", {n: g[n] for n in LATE_WEIGHTS})[0:1, 0:1]
    (dh1,), (g["norm_cross"],) = _rowwise_vjp(_rms, [full(h1)], [norm_cross], [full(dhn2)], "norm_cross_bwd",
                                              adds=[full(dh2)])
    _, (g["norm_mem"],) = _rowwise_vjp(_rms, [full(mem)], [p["norm_mem"]], [full(dmn)], "norm_mem_bwd",
                                       row_dtypes=[BF16])

    dmixed = _mm(dh1, wb["w_out"], "nt", "mix_out_dx", out_dtype=BF16)
    g["w_out"] = _mm(mixed, dh1, "tn", "mix_out_dw", out_dtype=late_dt)
    (dfox,), (g["out_norm_fox"],) = _rowwise_vjp(_rms, [full(fox)], [p["out_norm_fox"]],
                                                 [(dmixed, FOX_WIDTH, 0, 0)], "fox_outnorm_bwd")
    (dyg_a, dz), (g["s5_b_glu"], g["out_norm_s5"]) = _rowwise_vjp(
        _s5_gate, [full(yg), full(z)], [p["s5_b_glu"], p["out_norm_s5"]], [(dmixed, S5_WIDTH, 1, 0)], "s5_gate_bwd",
        row_dtypes=[F32, BF16])
    dyg = _mm(dz, wb["s5_w_glu"], "nt", "s5_glu_dx", res=dyg_a)
    g["s5_w_glu"] = _mm(yg, dz, "tn", "s5_glu_dw", out_dtype=late_dt)
    if early_grads is not None:
        d_row = d_row + early_grads("mid", {n: g[n] for n in MID_WEIGHTS})[0:1, 0:1]
    (dys, du_a), (dd_row,) = _rowwise_vjp(_s5_act, [full(ys), u_blk], [d_row], [full(dyg)], "s5_act_bwd",
                                          row_dtypes=[BF16, F32])
    g["s5_d"] = dd_row
    du_b, dbbr_d, dbbi_d, dcr_d, dci_d, dar, dai = _s5_bwd(dys, uf, xr, xi, bbr_d, bbi_d, cr_d, ci_d, ar, ai, seqs)
    dbbr_d, dbbi_d, dcr_d, dci_d = (jnp.sum(a, axis=0) for a in (dbbr_d, dbbi_d, dcr_d, dci_d))
    d_lb_r = jnp.sum(dar, axis=0).reshape(S5_GROUPS, S5_STATE)
    d_lb_i = jnp.sum(dai, axis=0).reshape(S5_GROUPS, S5_STATE)
    g["s5_a_re"], g["s5_a_im"], g["s5_log_dt"], g["s5_b_re"], g["s5_b_im"] = s5_pull(
        (d_lb_r, d_lb_i, _blockdiag_in_grad(dbbr_d), _blockdiag_in_grad(dbbi_d)))
    g["s5_c_re"] = _blockdiag_out_grad(dcr_d)
    g["s5_c_im"] = -_blockdiag_out_grad(dci_d)

    dqn, dkn, dv, dc, dcq = _fox_bwd(qn, kn, qkv, c_wide, fox, dfox, lse, seqs)
    pair = lambda a: (a, 128, 0, 1)
    (dq_raw,), (dgq2,) = _rowwise_vjp(_rms_pair, [q_pair], [gq2], [pair(dqn)], "fox_qnorm_bwd", heads=N_PAIRS,
                                      row_dtypes=[BF16])
    (dk_raw,), (dgk2,) = _rowwise_vjp(_rms_pair, [k_pair], [gk2], [pair(dkn)], "fox_knorm_bwd", heads=N_PAIRS,
                                      row_dtypes=[BF16])
    g["fox_q_norm"] = dgq2[:, :HEAD_DIM] + dgq2[:, HEAD_DIM:]
    g["fox_k_norm"] = dgk2[:, :HEAD_DIM] + dgk2[:, HEAD_DIM:]
    df_rows, dfb = _forget_bwd(f_rows, f_bias, (dc + dcq).reshape(bh, l))
    g["fox_f_bias"] = jnp.sum(dfb.reshape(seqs, N_FOX_HEADS), axis=0)
    df = df_rows.reshape(seqs, N_FOX_HEADS, l).transpose(0, 2, 1).reshape(t, N_FOX_HEADS)
    dqkv = jnp.concatenate([dq_raw, dk_raw, dv.astype(BF16)], axis=1)
    duf = jnp.concatenate([du_a + du_b, df, jnp.zeros((t, UF_COLS - S5_WIDTH - N_FOX_HEADS), F32)],
                          axis=1).astype(BF16)
    dhn1 = _mm(duf, w_uf, "nt", "in_uf_dx", res=_mm(dqkv, w_qkv, "nt", "in_qkv_dx"), out_dtype=BF16)
    dw_qkv = _mm(hn1, dqkv, "tn", "in_qkv_dw")
    dw_uf = _mm(hn1, duf, "tn", "in_uf_dw")
    g["w_in"] = jnp.concatenate([dw_qkv, dw_uf[:, S5_WIDTH:S5_WIDTH + N_FOX_HEADS], dw_uf[:, :S5_WIDTH]], axis=1)
    (dx,), (g["norm_mix"],) = _rowwise_vjp(_rms, [full(x)], [p["norm_mix"]], [full(dhn1)], "norm_mix_bwd",
                                           adds=[full(dh1)])
    return loss, dx.reshape(seqs, l, d), g


def _place():
    return lax.axis_index("x"), lax.axis_index("y"), lax.axis_index("c")


def _other_chips(x, y):
    return [(1 - x, y), (x, 1 - y), (1 - x, 1 - y)]


ANY = pl.BlockSpec(memory_space=pl.ANY)


HBM = pl.BlockSpec(memory_space=pltpu.HBM)
SEM = pl.BlockSpec(memory_space=pltpu.SEMAPHORE)
DATAFLOW = pltpu.SideEffectType.DATAFLOW_SIDE_EFFECTING


def _in_hbm(a):
    return pltpu.with_memory_space_constraint(a, pltpu.HBM)


def _split_start(name, srcs, lands, n_copies, plan):
    n = len(srcs)

    def body(*refs):
        src_refs, land_refs = refs[:n], refs[n:2 * n]
        send_sems, recv_sems = refs[2 * n], refs[2 * n + 1]
        for i, (src, dst, dev) in enumerate(plan(src_refs, land_refs)):
            pltpu.make_async_remote_copy(src_ref=src, dst_ref=dst, send_sem=send_sems.at[i], recv_sem=recv_sems.at[i],
                                         device_id=dev, device_id_type=MESH).start()
        refs[-1][...] = jnp.zeros((8, 128), F32)

    res = pl.pallas_call(
        body, name=name, in_specs=[HBM] * (2 * n),
        out_specs=[SEM, SEM] + [HBM] * (2 * n) + [pl.BlockSpec(memory_space=pltpu.VMEM)],
        out_shape=[pltpu.SemaphoreType.DMA((n_copies,)), pltpu.SemaphoreType.DMA((n_copies,))]
        + [pltpu.HBM(a.shape, a.dtype) for a in list(srcs) + list(lands)] + [jax.ShapeDtypeStruct((8, 128), F32)],
        input_output_aliases={i: 2 + i for i in range(2 * n)},
        compiler_params=pltpu.CompilerParams(has_side_effects=DATAFLOW),
    )(*[_in_hbm(a) for a in list(srcs) + list(lands)])
    return res[0], res[1], list(res[2:2 + n]), list(res[2 + n:2 + 2 * n]), res[-1]


def _split_wait(name, send_sems, recv_sems, srcs, lands, after, plan):
    n = len(srcs)

    def body(*refs):
        src_refs, land_refs = refs[:n], refs[n:2 * n]
        send_ref, recv_ref = refs[2 * n], refs[2 * n + 1]
        for i, (src, dst, dev) in enumerate(plan(src_refs, land_refs)):
            cp = pltpu.make_async_remote_copy(src_ref=src, dst_ref=dst, send_sem=send_ref.at[i], recv_sem=recv_ref.at[i],
                                              device_id=dev, device_id_type=MESH)
            cp.wait_send()
            cp.wait_recv()

    res = pl.pallas_call(
        body, name=name, in_specs=[HBM] * (2 * n) + [SEM, SEM, ANY], out_specs=[HBM] * (2 * n),
        out_shape=[pltpu.HBM(a.shape, a.dtype) for a in list(srcs) + list(lands)],
        input_output_aliases={i: i for i in range(2 * n)},
        compiler_params=pltpu.CompilerParams(has_side_effects=DATAFLOW),
    )(*srcs, *lands, send_sems, recv_sems, after)
    return list(res[:n]), list(res[n:])


def _first_gather_plan(src_refs, land_refs):
    x, y, c = _place()
    mine = 2 * x + y
    (src, small), (land, small_land) = src_refs, land_refs
    r = src.shape[0]
    hr = r // 2
    half = src.at[pl.ds(pl.multiple_of(c * hr, 16), hr), :]
    half_dst = land.at[pl.ds(pl.multiple_of(mine * r + c * hr, 16), hr), :]
    copies = [(src, land.at[pl.ds(pl.multiple_of(mine * r, 16), r), :], (x, y, 1 - c)),
              (small, small_land.at[mine], (x, y, 1 - c))]
    for px, py in _other_chips(x, y):
        copies += [(half, half_dst, (px, py, c)), (small, small_land.at[mine], (px, py, c))]
    return copies


def _forward_to_sibling(full):
    def body(full_in, full_ref, send_sems, recv_sems):
        x, y, c = _place()
        r = full_ref.shape[0] // 4
        hr = r // 2
        copies = []
        for j, (px, py) in enumerate(_other_chips(x, y)):
            got = full_ref.at[pl.ds(pl.multiple_of((2 * px + py) * r + c * hr, 16), hr), :]
            cp = pltpu.make_async_remote_copy(
                src_ref=got, dst_ref=got, send_sem=send_sems.at[j], recv_sem=recv_sems.at[j],
                device_id=(x, y, 1 - c), device_id_type=MESH)
            cp.start()
            copies.append(cp)
        for cp in copies:
            cp.wait()

    return pl.pallas_call(
        body, name="gather_first_forward", in_specs=[ANY], out_specs=ANY,
        out_shape=jax.ShapeDtypeStruct(full.shape, full.dtype), input_output_aliases={0: 0},
        scratch_shapes=[pltpu.SemaphoreType.DMA((3,)), pltpu.SemaphoreType.DMA((3,))],
        compiler_params=pltpu.CompilerParams(has_side_effects=True),
    )(full)


def _late_gather_plan(col_kind):
    def plan(src_refs, land_refs):
        x, y, c = _place()
        mine = 2 * x + y
        copies = []
        for a, (src, land) in enumerate(zip(src_refs, land_refs)):
            r, cs = src.shape
            if col_kind[a]:
                dst = land.at[:, pl.ds(pl.multiple_of(mine * cs, 128), cs)]
            else:
                dst = land.at[pl.ds(pl.multiple_of(mine * r, 16), r), :]
            copies.append((src, dst, (x, y, 1 - c)))
            copies += [(src, dst, (px, py, c)) for (px, py) in _other_chips(x, y)]
        return copies
    return plan


def _late_reduce_plan(col_kind):
    def plan(src_refs, land_refs):
        x, y, c = _place()
        copies = []
        for a, (src, land) in enumerate(zip(src_refs, land_refs)):
            for j, (px, py) in enumerate(_other_chips(x, y)):
                if col_kind[a] is None:
                    piece = src
                elif col_kind[a]:
                    cs = land.shape[2]
                    piece = src.at[:, pl.ds(pl.multiple_of((2 * px + py) * cs, 128), cs)]
                else:
                    piece = src.at[2 * px + py]
                copies.append((piece, land.at[j], (px, py, c)))
        return copies
    return plan


def _pair_swap(name, halves):
    n = len(halves)

    def body(*refs):
        ins, outs = refs[:n], refs[n:2 * n]
        send_sems, recv_sems = refs[2 * n:]
        x, y, c = _place()
        copies = []
        for a in range(n):
            cp = pltpu.make_async_remote_copy(
                src_ref=ins[a], dst_ref=outs[a], send_sem=send_sems.at[a], recv_sem=recv_sems.at[a],
                device_id=(x, y, 1 - c), device_id_type=MESH)
            cp.start()
            copies.append(cp)
        for cp in copies:
            cp.wait()

    return pl.pallas_call(
        body, name=name, in_specs=[ANY] * n, out_specs=[ANY] * n,
        out_shape=[jax.ShapeDtypeStruct(s.shape, s.dtype) for s in halves],
        scratch_shapes=[pltpu.SemaphoreType.DMA((n,)), pltpu.SemaphoreType.DMA((n,))],
        compiler_params=pltpu.CompilerParams(has_side_effects=True),
    )(*halves)


def _chip_sum(name, chip_sel, own, col, others):
    _, r, c = others.shape
    tr = _pick(r, (256, 128, 64, 32, 16))
    if col:
        own_spec = pl.BlockSpec((tr, c), lambda i, s: (i, s[0]))
    else:
        own_spec = pl.BlockSpec((None, tr, c), lambda i, s: (s[0], i, 0))
    specs = [own_spec] + [pl.BlockSpec((None, tr, c), lambda i, s, k=k: (k, i, 0)) for k in range(3)]

    def body(s_ref, own_ref, r0, r1, r2, o_ref):
        total = ((own_ref[...].astype(F32) + r0[...].astype(F32)) + r1[...].astype(F32)) + r2[...].astype(F32)
        o_ref[...] = total.astype(o_ref.dtype)

    return pl.pallas_call(
        body, name=name,
        grid_spec=pltpu.PrefetchScalarGridSpec(
            num_scalar_prefetch=1, grid=(r // tr,), in_specs=specs,
            out_specs=pl.BlockSpec((tr, c), lambda i, s: (i, 0))),
        out_shape=jax.ShapeDtypeStruct((r, c), BF16),
        compiler_params=_params(("parallel",)),
    )(chip_sel, own, others, others, others)


def _small_layout(vals):
    sizes = [int(math.prod(v.shape)) for v in vals]
    padded = [-(-s // 128) * 128 for s in sizes]
    return sizes, padded, -(-sum(padded) // 1024) * 1024


def _pack_small(vals):
    sizes, padded, total = _small_layout(vals)
    flat = [jnp.pad(v.reshape(-1), (0, p - s)) for v, s, p in zip(vals, sizes, padded)]
    flat.append(jnp.zeros((total - sum(padded),), F32))
    return jnp.concatenate(flat).reshape(total // 128, 128)


def _allreduce_small(own, others, vals):
    def body(own_ref, oth_ref, out_ref, land, send_sem, recv_sem):
        x, y, c = _place()
        out_ref[...] = (own_ref[...] + oth_ref[0]) + (oth_ref[1] + oth_ref[2])
        cp = pltpu.make_async_remote_copy(
            src_ref=out_ref, dst_ref=land, send_sem=send_sem.at[0], recv_sem=recv_sem.at[0],
            device_id=(x, y, 1 - c), device_id_type=MESH)
        cp.start()
        cp.wait()
        out_ref[...] = out_ref[...] + land[...]

    vm = pl.BlockSpec(memory_space=pltpu.VMEM)
    summed = pl.pallas_call(
        body, name="allreduce_small", in_specs=[vm, vm], out_specs=vm,
        out_shape=jax.ShapeDtypeStruct(own.shape, F32),
        scratch_shapes=[pltpu.VMEM(own.shape, F32), pltpu.SemaphoreType.DMA((1,)), pltpu.SemaphoreType.DMA((1,))],
        compiler_params=pltpu.CompilerParams(has_side_effects=True, vmem_limit_bytes=VMEM_LIMIT_BYTES),
    )(own, others).reshape(-1)
    sizes, padded, _ = _small_layout(vals)
    outs, off = [], 0
    for v, s, p in zip(vals, sizes, padded):
        outs.append(summed[off:off + s].reshape(v.shape))
        off += p
    return outs


def _adamw_math(w, g, m, v):
    m2 = ADAM_B1 * m + (1.0 - ADAM_B1) * g
    v2 = ADAM_B2 * v + (1.0 - ADAM_B2) * (g * g)
    m_hat = m2 / (1.0 - ADAM_B1 ** ADAM_STEP)
    v_hat = v2 / (1.0 - ADAM_B2 ** ADAM_STEP)
    delta = -ADAM_LR * (m_hat / (jnp.sqrt(v_hat) + ADAM_EPS) + ADAM_WD * w)
    return delta, m2, v2


def _adamw_big(name, w, g_mine, g_sibling, m, v):
    _, r, c = w.shape

    def body(w_ref, ga_ref, gb_ref, m_ref, v_ref, go_ref, d_ref, mo_ref, vo_ref):
        gv = ga_ref[...].astype(F32) + gb_ref[...].astype(F32)
        d, m2, v2 = _adamw_math(w_ref[...], gv, m_ref[...], v_ref[...])
        go_ref[...] = gv
        d_ref[...] = d
        mo_ref[...] = m2
        vo_ref[...] = v2

    tr = _pick(r, (256, 128, 64, 32, 16, 8))
    if r % tr == 0 and tr % 8 == 0:
        grid = (r // tr,)
        blk = pl.BlockSpec((None, tr, c), lambda i: (0, i, 0))
        part = pl.BlockSpec((tr, c), lambda i: (i, 0))
    else:
        grid = (c // 512,)
        blk = pl.BlockSpec((None, r, 512), lambda i: (0, 0, i))
        part = pl.BlockSpec((r, 512), lambda i: (0, i))
    return pl.pallas_call(
        body, name=name, grid=grid, in_specs=[blk, part, part, blk, blk], out_specs=[blk] * 4,
        out_shape=[jax.ShapeDtypeStruct((1, r, c), F32)] * 4, compiler_params=_params(("parallel",)),
    )(w, g_mine, g_sibling, m, v)


def _adamw_small(ws, gs, ms, vs):
    n = len(ws)

    def body(*refs):
        w_r, g_r, m_r, v_r = refs[:n], refs[n:2 * n], refs[2 * n:3 * n], refs[3 * n:4 * n]
        o = refs[4 * n:]
        for a in range(n):
            gv = g_r[a][...]
            d, m2, v2 = _adamw_math(w_r[a][...], gv, m_r[a][...], v_r[a][...])
            o[a][...] = gv
            o[n + a][...] = d
            o[2 * n + a][...] = m2
            o[3 * n + a][...] = v2

    res = pl.pallas_call(
        body, name="adamw_small", out_shape=[jax.ShapeDtypeStruct(w.shape, F32) for _ in range(4) for w in ws],
        compiler_params=_params(),
    )(*ws, *gs, *ms, *vs)
    return res[:n], res[n:2 * n], res[2 * n:3 * n], res[3 * n:]


def _full_from_gathered(name, gathered):
    if name == "w_in":
        rows = gathered.shape[0] // 4
        return gathered.reshape(4, rows, gathered.shape[1]).transpose(1, 0, 2).reshape(rows, 4 * gathered.shape[1])
    return gathered


def _reduce_layout(name, full):
    if name in COL_KIND:
        return full
    if name == "w_in":
        rows, cols = full.shape
        return full.reshape(rows, 4, cols // 4).transpose(1, 0, 2)
    return full.reshape(4, full.shape[0] // 4, full.shape[1])


def kernel(x, mem, norm_mix, w_in, fox_q_norm, fox_k_norm, fox_f_bias, s5_a_re, s5_a_im, s5_log_dt, s5_b_re, s5_b_im, s5_c_re, s5_c_im, s5_d, s5_w_glu, s5_b_glu, out_norm_fox, out_norm_s5, w_out, norm_cross, norm_mem, w_xq, w_xkv, xq_norm, xk_norm, w_xo, norm_ffn, w_ffn_up, ffn_conv_w, ffn_conv_b, w_ffn_down, loss_target, m_norm_mix, m_w_in, m_fox_q_norm, m_fox_k_norm, m_fox_f_bias, m_s5_a_re, m_s5_a_im, m_s5_log_dt, m_s5_b_re, m_s5_b_im, m_s5_c_re, m_s5_c_im, m_s5_d, m_s5_w_glu, m_s5_b_glu, m_out_norm_fox, m_out_norm_s5, m_w_out, m_norm_cross, m_norm_mem, m_w_xq, m_w_xkv, m_xq_norm, m_xk_norm, m_w_xo, m_norm_ffn, m_w_ffn_up, m_ffn_conv_w, m_ffn_conv_b, m_w_ffn_down, v_norm_mix, v_w_in, v_fox_q_norm, v_fox_k_norm, v_fox_f_bias, v_s5_a_re, v_s5_a_im, v_s5_log_dt, v_s5_b_re, v_s5_b_im, v_s5_c_re, v_s5_c_im, v_s5_d, v_s5_w_glu, v_s5_b_glu, v_out_norm_fox, v_out_norm_s5, v_w_out, v_norm_cross, v_norm_mem, v_w_xq, v_w_xkv, v_xq_norm, v_xk_norm, v_w_xo, v_norm_ffn, v_w_ffn_up, v_ffn_conv_w, v_ffn_conv_b, v_w_ffn_down):
    given = dict(locals())
    w = {n: given[n] for n in WEIGHTS}
    m = {n: given["m_" + n] for n in WEIGHTS}
    v = {n: given["v_" + n] for n in WEIGHTS}
    xi, yi, _ = _place()
    chip = (2 * xi + yi).astype(jnp.int32)
    chip_sel = chip.reshape(1)

    first_shard, taps = w[FIRST_WEIGHT][0].astype(BF16), w["ffn_conv_w"][0]
    send, recv, srcs, lands, g_started = _split_start(
        "gather_first_start", [first_shard, taps],
        [lax.empty((4 * first_shard.shape[0], first_shard.shape[1]), BF16), lax.empty((4,) + taps.shape, F32)],
        8, _first_gather_plan)
    pending = {"first": ((FIRST_WEIGHT, "ffn_conv_w"), _first_gather_plan, send, recv, srcs, lands)}
    for stage, names in (("mid", MID_WEIGHTS), ("late", LATE_WEIGHTS)):
        kinds = [n in COL_KIND for n in names]
        shards = [w[n][0].astype(BF16) for n in names]
        shards[0] = shards[0] + g_started[0:1, 0:1].astype(BF16)
        full = [lax.empty((s.shape[0], 4 * s.shape[1]) if ck else (4 * s.shape[0], s.shape[1]), BF16)
                for s, ck in zip(shards, kinds)]
        plan = _late_gather_plan(kinds)
        send, recv, srcs, lands, g_started = _split_start(
            "gather_" + stage + "_start", shards, full, 4 * len(names), plan)
        pending[stage] = (names, plan, send, recv, srcs, lands)

    def late_weights(stage, after):
        names, plan, send, recv, srcs, lands = pending[stage]
        _, full = _split_wait("gather_" + stage + "_wait", send, recv, srcs, lands, after, plan)
        if stage == "first":
            full = [_full_from_gathered(FIRST_WEIGHT, _forward_to_sibling(full[0])),
                    full[1].transpose(1, 0, 2).reshape(3, D_FF)]
        return dict(zip(names, full))

    reducing = {}

    def start_reduce(stage, grads_by_name, whole=()):
        names = list(grads_by_name)
        kinds = [n in COL_KIND for n in names]
        grads = [_reduce_layout(n, grads_by_name[n].astype(BF16)) for n in names]
        lands = [lax.empty((3, s.shape[0], s.shape[1] // 4) if ck else (3,) + s.shape[1:], BF16)
                 for s, ck in zip(grads, kinds)]
        plan = _late_reduce_plan(kinds + [None] * len(whole))
        send, recv, srcs, lands, started = _split_start(
            "reduce_" + stage + "_start", grads + list(whole),
            lands + [lax.empty((3,) + a.shape, a.dtype) for a in whole], 3 * (len(names) + len(whole)), plan)
        reducing[stage] = (names, kinds, plan, send, recv, srcs, lands)
        return started

    p = {n: w[n][0] for n in SMALL}
    for n in ("norm_mix", "fox_q_norm", "fox_k_norm", "fox_f_bias", "s5_b_glu", "out_norm_fox", "out_norm_s5",
              "norm_cross", "norm_mem", "xq_norm", "xk_norm", "norm_ffn", "ffn_conv_b"):
        p[n] = p[n].reshape(1, -1)
    p["norm_mix"] = p["norm_mix"] + g_started[0:1, 0:1]
    loss, grad_x, g = _local_step(x, mem, loss_target, p, {}, late_weights, start_reduce)

    small_names = list(SMALL) + ["ffn_conv_w"]
    small_vals = [g[n].reshape(w[n].shape if n != "ffn_conv_w" else (1, 3, D_FF)) for n in small_names] + [loss]
    after = start_reduce("first", {FIRST_WEIGHT: g[FIRST_WEIGHT]}, whole=[_pack_small(small_vals)])

    out_g, out_d, out_m, out_v = {}, {}, {}, {}

    def finish(stage, after):
        names, kinds, plan, send, recv, srcs, lands = reducing[stage]
        sums, from_chips = _split_wait("reduce_" + stage + "_wait", send, recv, srcs, lands, after, plan)
        mine = [_chip_sum("reduce_chip_sum_" + n, chip_sel, ps, ck, fc)
                for n, ps, fc, ck in zip(names, sums, from_chips, kinds)]
        theirs = _pair_swap("reduce_pair_swap_" + stage, mine)
        for n, a, b in zip(names, mine, theirs):
            if n == "w_in":
                flip = lambda t: jnp.swapaxes(t, -1, -2)
                res = _adamw_big("adamw_" + n, flip(w[n]), flip(a), flip(b), flip(m[n]), flip(v[n]))
                out_g[n], out_d[n], out_m[n], out_v[n] = (flip(t) for t in res)
                continue
            out_g[n], out_d[n], out_m[n], out_v[n] = _adamw_big("adamw_" + n, w[n], a, b, m[n], v[n])
        return sums[len(names):], from_chips[len(names):], out_v[names[-1]]

    _, _, after = finish("late", after)
    _, _, after = finish("mid", after)
    (small_own,), (small_others,), _ = finish("first", after)

    reduced = _allreduce_small(small_own, small_others, small_vals)
    loss_all = reduced[-1].reshape(())
    conv_w_grad = lax.dynamic_slice_in_dim(reduced[-2], chip * (D_FF // 4), D_FF // 4, axis=2)
    sg, sd, sm, sv = _adamw_small(
        [w[n] for n in small_names], list(reduced[:len(SMALL)]) + [conv_w_grad],
        [m[n] for n in small_names], [v[n] for n in small_names])
    out_g.update(zip(small_names, sg))
    out_d.update(zip(small_names, sd))
    out_m.update(zip(small_names, sm))
    out_v.update(zip(small_names, sv))

    return (loss_all, grad_x, *[out_g[n] for n in WEIGHTS], *[out_d[n] for n in WEIGHTS],
            *[out_m[n] for n in WEIGHTS], *[out_v[n] for n in WEIGHTS])
```

```python
import math

import jax
import jax.numpy as jnp
from jax import lax
from jax.experimental import pallas as pl
from jax.experimental.pallas import tpu as pltpu

F32 = jnp.float32
BF16 = jnp.bfloat16

D_MODEL = 1024
FOX_WIDTH = 512
HEAD_DIM = 64
N_FOX_HEADS = 8
S5_WIDTH = 512
S5_GROUP_CH = 16
S5_GROUPS = 32
S5_STATE = 64
S5_CH = S5_GROUPS * S5_STATE
N_X_HEADS = 4
X_HEAD_DIM = 256
N_MEM = 256
D_FF = 2816
UF_COLS = 640
EPS = 1e-6
ADAM_LR = 0.001
ADAM_B1 = 0.9
ADAM_B2 = 0.999
ADAM_EPS = 1e-08
ADAM_WD = 0.01
ADAM_STEP = 10

VMEM_LIMIT_BYTES = 56 * 1024 * 1024
MM_BLOCK_BYTES = 6 * 1024 * 1024
MM_VMEM_BYTES = 40 * 1024 * 1024
MM_TILE_MAX = 1536
MESH = pl.DeviceIdType.MESH

FIRST_WEIGHT = "w_in"
MID_WEIGHTS = ("s5_w_glu", "w_out")
EARLY_WEIGHTS = (FIRST_WEIGHT,) + MID_WEIGHTS
LATE_WEIGHTS = ("w_xq", "w_xkv", "w_xo", "w_ffn_up", "w_ffn_down")
BIG = EARLY_WEIGHTS + LATE_WEIGHTS
COL_KIND = ("w_xkv", "w_ffn_up")
SMALL = ("norm_mix", "fox_q_norm", "fox_k_norm", "fox_f_bias", "s5_a_re", "s5_a_im", "s5_log_dt",
         "s5_b_re", "s5_b_im", "s5_c_re", "s5_c_im", "s5_d", "s5_b_glu", "out_norm_fox", "out_norm_s5",
         "norm_cross", "norm_mem", "xq_norm", "xk_norm", "norm_ffn", "ffn_conv_b")
WEIGHTS = ("norm_mix", "w_in", "fox_q_norm", "fox_k_norm", "fox_f_bias", "s5_a_re", "s5_a_im", "s5_log_dt",
           "s5_b_re", "s5_b_im", "s5_c_re", "s5_c_im", "s5_d", "s5_w_glu", "s5_b_glu", "out_norm_fox",
           "out_norm_s5", "w_out", "norm_cross", "norm_mem", "w_xq", "w_xkv", "xq_norm", "xk_norm", "w_xo",
           "norm_ffn", "w_ffn_up", "ffn_conv_w", "ffn_conv_b", "w_ffn_down")


def _params(sem=None):
    return pltpu.CompilerParams(dimension_semantics=sem, vmem_limit_bytes=VMEM_LIMIT_BYTES)


def _pick(n, cands):
    for c in cands:
        if n % c == 0:
            return c
    return n


_DIMS = {"nn": (((1,), (0,)), ((), ())), "nt": (((1,), (1,)), ((), ())), "tn": (((0,), (0,)), ((), ()))}


def _mm(a, b, mode, name, out_dtype=F32, res=None):
    if mode == "nn":
        (m, k), (k2, n) = a.shape, b.shape
    elif mode == "nt":
        (m, k), (n, k2) = a.shape, b.shape
    else:
        (k, m), (k2, n) = a.shape, b.shape
    assert k == k2, (name, a.shape, b.shape)

    has_res = res is not None
    a_size, b_size = a.dtype.itemsize, b.dtype.itemsize
    o_size = jnp.dtype(out_dtype).itemsize + (res.dtype.itemsize if has_res else 0)

    def tiles(dim):
        return [c for c in range(MM_TILE_MAX, 0, -128) if dim % c == 0] or [dim]

    best = None
    for tm in tiles(m):
        for tn in tiles(n):
            a_blk, b_blk = tm * k * a_size, tn * k * b_size
            if max(a_blk, b_blk) > MM_BLOCK_BYTES or 2 * (a_blk + b_blk + tm * tn * o_size) > MM_VMEM_BYTES:
                continue
            for rows_outer in (True, False):
                moved = (m * k * a_size + (m // tm) * n * k * b_size) if rows_outer else \
                        (n * k * b_size + (n // tn) * m * k * a_size)
                key = (moved, -(tm * tn))
                if best is None or key < best[0]:
                    best = (key, tm, tn, rows_outer)
    assert best is not None, (name, a.shape, b.shape)
    _, tm, tn, rows_outer = best
    ij = (lambda g0, g1: (g0, g1)) if rows_outer else (lambda g0, g1: (g1, g0))
    if mode == "tn":
        a_spec = pl.BlockSpec((k, tm), lambda g0, g1: (0, ij(g0, g1)[0]))
    else:
        a_spec = pl.BlockSpec((tm, k), lambda g0, g1: (ij(g0, g1)[0], 0))
    if mode == "nt":
        b_spec = pl.BlockSpec((tn, k), lambda g0, g1: (ij(g0, g1)[1], 0))
    else:
        b_spec = pl.BlockSpec((k, tn), lambda g0, g1: (0, ij(g0, g1)[1]))
    o_spec = pl.BlockSpec((tm, tn), lambda g0, g1: ij(g0, g1))
    grid = (m // tm, n // tn) if rows_outer else (n // tn, m // tm)
    dims = _DIMS[mode]

    def body(*refs):
        a_ref, b_ref = refs[0], refs[1]
        o_ref = refs[-1]
        acc = lax.dot_general(a_ref[...].astype(BF16), b_ref[...].astype(BF16), dims, preferred_element_type=F32)
        if has_res:
            acc = acc + refs[2][...].astype(F32)
        o_ref[...] = acc.astype(o_ref.dtype)

    return pl.pallas_call(
        body, name=name, grid=grid,
        in_specs=[a_spec, b_spec] + ([o_spec] if has_res else []),
        out_specs=o_spec, out_shape=jax.ShapeDtypeStruct((m, n), out_dtype),
        compiler_params=_params(("parallel", "parallel")),
    )(*((a, b, res) if has_res else (a, b)))


def _row_spec(tm, bc, off, step):
    return pl.BlockSpec((tm, bc), lambda i, h: (i, off + step * h))


ROW_TILE_ELEMS = 512 * 1024


def _row_tile(t, rows):
    widest = max(bc for (_, bc, _, _) in rows)
    return _pick(t, (min(t, ROW_TILE_ELEMS // widest), 512, 256, 128, 64, 8))


def _rowwise(fn, rows, pars, outs, name, heads=1):
    t = rows[0][0].shape[0]
    tm = _row_tile(t, rows)
    nr, npar = len(rows), len(pars)

    def body(*refs):
        vals = [r[...].astype(F32) for r in refs[:nr + npar]]
        res = fn(*vals)
        if not isinstance(res, (tuple, list)):
            res = (res,)
        for o_ref, v in zip(refs[nr + npar:], res):
            o_ref[...] = v.astype(o_ref.dtype)

    in_specs = [_row_spec(tm, bc, off, st) for (_, bc, off, st) in rows]
    in_specs += [pl.BlockSpec(p.shape, lambda i, h: (0, 0)) for p in pars]
    out_specs = [_row_spec(tm, bc, 0, st) for (_, bc, st, _) in outs]
    out_shape = [jax.ShapeDtypeStruct((t, c), dt) for (c, _, _, dt) in outs]
    res = pl.pallas_call(
        body, name=name, grid=(t // tm, heads), in_specs=in_specs, out_specs=out_specs, out_shape=out_shape,
        compiler_params=_params(("parallel", "parallel")),
    )(*[r[0] for r in rows], *pars)
    return res[0] if len(res) == 1 else res


def _rowwise_vjp(fn, rows, pars, cts, name, heads=1, adds=None, row_dtypes=None):
    t = rows[0][0].shape[0]
    tm = _row_tile(t, rows)
    nr, npar, nct = len(rows), len(pars), len(cts)
    adds = adds or [None] * nr
    add_list = [a for a in adds if a is not None]
    row_dtypes = row_dtypes or [F32] * nr

    def body(*refs):
        i, h = pl.program_id(0), pl.program_id(1)
        p = 0
        row_v = [r[...].astype(F32) for r in refs[p:p + nr]]; p += nr
        par_v = [r[...].astype(F32) for r in refs[p:p + npar]]; p += npar
        ct_v = [r[...].astype(F32) for r in refs[p:p + nct]]; p += nct
        add_refs = refs[p:p + len(add_list)]; p += len(add_list)
        drow_refs = refs[p:p + nr]; p += nr
        dpar_refs = refs[p:p + npar]

        def wrapped(*a):
            r = fn(*a)
            return tuple(r) if isinstance(r, (tuple, list)) else (r,)

        _, pull = jax.vjp(wrapped, *row_v, *par_v)
        grads = pull(tuple(ct_v))
        ai = 0
        for k in range(nr):
            g = grads[k]
            if adds[k] is not None:
                g = g + add_refs[ai][...].astype(F32)
                ai += 1
            drow_refs[k][...] = g.astype(drow_refs[k].dtype)

        @pl.when((i == 0) & (h == 0))
        def _():
            for r in dpar_refs:
                r[...] = jnp.zeros(r.shape, r.dtype)

        for k in range(npar):
            dpar_refs[k][...] += grads[nr + k]

    in_specs = [_row_spec(tm, bc, off, st) for (_, bc, off, st) in rows]
    in_specs += [pl.BlockSpec(q.shape, lambda i, h: (0, 0)) for q in pars]
    in_specs += [_row_spec(tm, bc, off, st) for (_, bc, off, st) in cts]
    in_specs += [_row_spec(tm, bc, off, st) for (_, bc, off, st) in add_list]
    out_specs = [_row_spec(tm, bc, 0, st) for (_, bc, _, st) in rows]
    out_specs += [pl.BlockSpec(q.shape, lambda i, h: (0, 0)) for q in pars]
    out_shape = [jax.ShapeDtypeStruct((t, bc * (heads if st else 1)), dt) for (_, bc, _, st), dt in zip(rows, row_dtypes)]
    out_shape += [jax.ShapeDtypeStruct(q.shape, F32) for q in pars]
    res = pl.pallas_call(
        body, name=name, grid=(t // tm, heads), in_specs=in_specs, out_specs=out_specs, out_shape=out_shape,
        compiler_params=_params(("arbitrary", "arbitrary")),
    )(*[r[0] for r in rows], *pars, *[c[0] for c in cts], *[a[0] for a in add_list])
    return list(res[:nr]), list(res[nr:])


def _rms(x, g):
    return x * lax.rsqrt(jnp.mean(x * x, axis=-1, keepdims=True) + EPS) * g


def _rms_pair(x, g):
    left = lax.broadcasted_iota(jnp.int32, x.shape, 1) < HEAD_DIM
    x2 = x * x
    ms_a = jnp.sum(jnp.where(left, x2, 0.0), axis=-1, keepdims=True) * (1.0 / HEAD_DIM)
    ms_b = jnp.sum(jnp.where(left, 0.0, x2), axis=-1, keepdims=True) * (1.0 / HEAD_DIM)
    return x * lax.rsqrt(jnp.where(left, ms_a, ms_b) + EPS) * g


def _gelu(x):
    return 0.5 * x * (1.0 + jnp.tanh(math.sqrt(2.0 / math.pi) * (x + 0.044715 * (x * x * x))))


def _s5_act(ys, u, d):
    return _gelu(ys + d * u)


def _s5_gate(yg, z, b, g):
    return _rms(yg * jax.nn.sigmoid(z + b), g)


def _lane_cumsum(x, reverse):
    n = x.shape[-1]
    lane = lax.broadcasted_iota(jnp.int32, x.shape, 1)
    k = 1
    while k < n:
        if reverse:
            x = x + jnp.where(lane < n - k, pltpu.roll(x, n - k, 1), 0.0)
        else:
            x = x + jnp.where(lane >= k, pltpu.roll(x, k, 1), 0.0)
        k *= 2
    return x


def _log_sigmoid(z):
    return jnp.minimum(z, 0.0) - jnp.log(1.0 + jnp.exp(-jnp.abs(z)))


def _forget_fwd(f, bias):
    def body(f_ref, b_ref, c_ref):
        c_ref[...] = _lane_cumsum(_log_sigmoid(f_ref[...] + b_ref[...]), False)

    return pl.pallas_call(body, name="forget_fwd", out_shape=jax.ShapeDtypeStruct(f.shape, F32),
                          compiler_params=_params())(f, bias)


def _forget_bwd(f, bias, dc):
    def body(f_ref, b_ref, dc_ref, df_ref, db_ref):
        dlog = _lane_cumsum(dc_ref[...], True)
        df = dlog * jax.nn.sigmoid(-(f_ref[...] + b_ref[...]))
        df_ref[...] = df
        db_ref[...] = jnp.sum(df, axis=1, keepdims=True)

    return pl.pallas_call(body, name="forget_bwd",
                          out_shape=(jax.ShapeDtypeStruct(f.shape, F32), jax.ShapeDtypeStruct(bias.shape, F32)),
                          compiler_params=_params())(f, bias, dc)


FOX_BLOCK = 1024
FOX_KEYS = 1024
FOX_BWD_BLOCK = 512
_NT = _DIMS["nt"]
_TN = _DIMS["tn"]


N_PAIRS = N_FOX_HEADS // 2
V_BLOCK0 = 2 * N_PAIRS


def _left_lanes(shape):
    return lax.broadcasted_iota(jnp.int32, shape, 1) < HEAD_DIM


def _top_rows(shape):
    return lax.broadcasted_iota(jnp.int32, shape, 0) < HEAD_DIM


def _wide(c_tile, n):
    return c_tile if n == 128 else jnp.concatenate([c_tile] * (n // 128), axis=1)


def _fox_fwd(qn, kn, qkv, c_wide, seqs):
    t = qn.shape[0]
    l = t // seqs
    tb = min(FOX_BLOCK, l)
    tk = min(FOX_KEYS, tb)
    ratio = tb // tk
    nb = l // tb
    scale = HEAD_DIM ** -0.5

    def body(q_ref, k_ref, v_ref, ca_ref, cb_ref, o_ref, lse_ref, vt_ref):
        i = pl.program_id(2)
        top = _top_rows((128, tb))

        @pl.when(i == 0)
        def _():
            vt_ref[...] = v_ref[...].T.astype(BF16)

        qt = (q_ref[...].astype(F32) * scale).T.astype(BF16)
        zero = jnp.zeros_like(qt)
        qts = (jnp.where(top, qt, zero), jnp.where(top, zero, qt))
        top_k = _top_rows((128, tk))
        zero_k = jnp.zeros((128, tk), BF16)
        key_pos = lax.broadcasted_iota(jnp.int32, (tk, tb), 0)
        query_pos = lax.broadcasted_iota(jnp.int32, (tk, tb), 1)
        c_refs = (ca_ref, cb_ref)

        def scores(j):
            off = pl.multiple_of(j * tk, tk)
            k2 = k_ref[pl.ds(off, tk), :]
            return tuple(jnp.dot(k2, qts[h], preferred_element_type=F32) - _wide(c_refs[h][pl.ds(off, tk), :], tb)
                         for h in (0, 1))

        def values_times(ps, j):
            vt = vt_ref[:, pl.ds(pl.multiple_of(j * tk, tk), tk)]
            return (jnp.dot(jnp.where(top_k, vt, zero_k), ps[0], preferred_element_type=F32)
                    + jnp.dot(jnp.where(top_k, zero_k, vt), ps[1], preferred_element_type=F32))

        def softmax_step(sts, stats, first_key):
            ps, new, alphas = [], [], []
            for st, (m, s_sum) in zip(sts, stats):
                if first_key is not None:
                    st = jnp.where(key_pos + first_key <= query_pos, st, -jnp.inf)
                m_new = jnp.maximum(m, jnp.max(st, axis=0, keepdims=True))
                alpha = jnp.exp(m - m_new)
                p = jnp.exp(st - m_new)
                new.append((m_new, alpha * s_sum + jnp.sum(p, axis=0, keepdims=True)))
                alphas.append(alpha)
                ps.append(p.astype(BF16))
            return tuple(ps), tuple(new), jnp.where(top, alphas[0], alphas[1])

        def tile(j, carry, first_key):
            stats, acc = carry
            ps, stats, alpha = softmax_step(scores(j), stats, first_key)
            return stats, alpha * acc + values_times(ps, j)

        stat = (jnp.full((1, tb), -jnp.inf, F32), jnp.zeros((1, tb), F32))
        below = i * ratio
        carry = lax.fori_loop(0, below, lambda j, c: tile(j, c, None), ((stat, stat), jnp.zeros((128, tb), F32)))
        for r in range(ratio):
            carry = tile(below + r, carry, r * tk)
        ((ma, sa), (mb, sb)), acc = carry
        o_ref[...] = (acc / jnp.where(top, sa, sb)).T
        lse_ref[0:1, :] = ma + jnp.log(sa)
        lse_ref[1:2, :] = mb + jnp.log(sb)

    qblk = pl.BlockSpec((tb, 128), lambda b, hp, i: (b * nb + i, hp))
    return pl.pallas_call(
        body, name="fox_fwd", grid=(seqs, N_PAIRS, nb),
        in_specs=[qblk, pl.BlockSpec((l, 128), lambda b, hp, i: (b, hp)),
                  pl.BlockSpec((l, 128), lambda b, hp, i: (b, V_BLOCK0 + hp)),
                  pl.BlockSpec((None, l, 128), lambda b, hp, i: (b * N_FOX_HEADS + 2 * hp, 0, 0)),
                  pl.BlockSpec((None, l, 128), lambda b, hp, i: (b * N_FOX_HEADS + 2 * hp + 1, 0, 0))],
        out_specs=[qblk, pl.BlockSpec((None, 2, tb), lambda b, hp, i: (b * N_PAIRS + hp, 0, i))],
        out_shape=[jax.ShapeDtypeStruct((t, FOX_WIDTH), F32), jax.ShapeDtypeStruct((seqs * N_PAIRS, 2, l), F32)],
        scratch_shapes=[pltpu.VMEM((128, l), BF16)],
        compiler_params=_params(("parallel", "parallel", "arbitrary")),
    )(qn, kn, qkv, c_wide, c_wide)


def _fox_bwd(qn, kn, qkv, c_wide, o, do, lse, seqs):
    t = qn.shape[0]
    l = t // seqs
    tb = min(FOX_BWD_BLOCK, l)
    nb = l // tb
    scale = HEAD_DIM ** -0.5
    one_at = (HEAD_DIM, 0)

    def body(q_ref, k_ref, v_ref, ca_ref, cb_ref, o_ref, do_ref, lse_ref, dq_ref, dk_ref, dv_ref, dc_ref, dcq_ref,
             qt_ref, kt_ref, dot_ref, delta_ref, dqa_ref, dqb_ref):
        top_l = _top_rows((128, l))
        top = _top_rows((128, tb))
        left = _left_lanes((tb, 128))
        row_id = lax.broadcasted_iota(jnp.int32, (128, tb), 0)
        lane_id = lax.broadcasted_iota(jnp.int32, (tb, 128), 1)
        zero_t = jnp.zeros((128, tb), BF16)
        zero_l = jnp.zeros((tb, 128), BF16)
        rows = lambda a: (jnp.where(top, a, zero_t), jnp.where(top, zero_t, a))
        lanes = lambda a: (jnp.where(left, a, zero_l), jnp.where(left, zero_l, a))
        with_one_row = lambda pair: tuple(jnp.where(row_id == one_at[h], 1.0, pair[h]).astype(BF16) for h in (0, 1))
        with_one_lane = lambda pair: tuple(jnp.where(lane_id == one_at[h], 1.0, pair[h]).astype(BF16) for h in (0, 1))
        causal = lax.broadcasted_iota(jnp.int32, (tb, tb), 0) <= lax.broadcasted_iota(jnp.int32, (tb, tb), 1)
        c_refs = (ca_ref, cb_ref)
        dq_refs = (dqa_ref, dqb_ref)

        qt_ref[...] = (q_ref[...].astype(F32) * scale).T.astype(BF16)
        kt_ref[...] = k_ref[...].astype(F32).T.astype(BF16)
        do_t = do_ref[...].T
        dot_ref[...] = do_t.astype(BF16)
        prod_t = do_t * o_ref[...].T
        delta_ref[0:1, :] = jnp.sum(jnp.where(top_l, prod_t, 0.0), axis=0, keepdims=True)
        delta_ref[1:2, :] = jnp.sum(jnp.where(top_l, 0.0, prod_t), axis=0, keepdims=True)
        dqa_ref[...] = jnp.zeros(dqa_ref.shape, F32)
        dqb_ref[...] = jnp.zeros(dqb_ref.shape, F32)

        def kv_block(j, _):
            koff = pl.multiple_of(j * tb, tb)
            k2 = k_ref[pl.ds(koff, tb), :]
            v2 = v_ref[pl.ds(koff, tb), :].astype(BF16)
            kts = with_one_row(rows(kt_ref[:, pl.ds(koff, tb)]))
            cw = tuple(_wide(c_refs[h][pl.ds(koff, tb), :], tb) for h in (0, 1))

            def q_block(i, carry, masked):
                dks, dv = list(carry[:2]), carry[2]
                qoff = pl.multiple_of(i * tb, tb)
                qs = lanes((q_ref[pl.ds(qoff, tb), :].astype(F32) * scale).astype(BF16))
                qs_one = with_one_lane(qs)
                dos = lanes(do_ref[pl.ds(qoff, tb), :].astype(BF16))
                qts = rows(qt_ref[:, pl.ds(qoff, tb)])
                dots = rows(dot_ref[:, pl.ds(qoff, tb)])
                for h in (0, 1):
                    st = jnp.dot(k2, qts[h], preferred_element_type=F32) - cw[h]
                    p = jnp.exp(st - lse_ref[h:h + 1, pl.ds(qoff, tb)])
                    if masked:
                        p = jnp.where(causal, p, 0.0)
                    dp = jnp.dot(v2, dots[h], preferred_element_type=F32)
                    dsb = (p * (dp - delta_ref[h:h + 1, pl.ds(qoff, tb)])).astype(BF16)
                    dv = dv + jnp.dot(p.astype(BF16), dos[h], preferred_element_type=F32)
                    dks[h] = dks[h] + jnp.dot(dsb, qs_one[h], preferred_element_type=F32)
                    dq_refs[h][:, pl.ds(qoff, tb)] += jnp.dot(kts[h], dsb, preferred_element_type=F32)
                return dks[0], dks[1], dv

            z = jnp.zeros((tb, 128), F32)
            carry = q_block(j, (z, z, z), True)
            rest = nb - 1 - j
            carry = lax.fori_loop(
                0, rest // 2, lambda n, c: q_block(j + 2 + 2 * n, q_block(j + 1 + 2 * n, c, False), False), carry)
            dka, dkb, dv = lax.cond(rest % 2 == 1, lambda c: q_block(nb - 1, c, False), lambda c: c, carry)
            dk_ref[pl.ds(koff, tb), :] = jnp.where(left, dka, dkb)
            dv_ref[pl.ds(koff, tb), :] = dv
            dc_ref[0:1, pl.ds(koff, tb)] = -dka.T[one_at[0]:one_at[0] + 1, :]
            dc_ref[1:2, pl.ds(koff, tb)] = -dkb.T[one_at[1]:one_at[1] + 1, :]
            return 0

        lax.fori_loop(0, nb, kv_block, 0)
        dq_ref[...] = (jnp.where(top_l, dqa_ref[...], dqb_ref[...]) * scale).T
        dcq_ref[0:1, :] = dqa_ref[one_at[0]:one_at[0] + 1, :]
        dcq_ref[1:2, :] = dqb_ref[one_at[1]:one_at[1] + 1, :]

    blk = pl.BlockSpec((l, 128), lambda b, hp: (b, hp))
    cspec = lambda k: pl.BlockSpec((None, l, 128), lambda b, hp: (b * N_FOX_HEADS + 2 * hp + k, 0, 0))
    rows2 = pl.BlockSpec((None, 2, l), lambda b, hp: (b * N_PAIRS + hp, 0, 0))
    wide = jax.ShapeDtypeStruct((t, FOX_WIDTH), F32)
    pair_rows = jax.ShapeDtypeStruct((seqs * N_PAIRS, 2, l), F32)
    return pl.pallas_call(
        body, name="fox_bwd", grid=(seqs, N_PAIRS),
        in_specs=[blk, blk, pl.BlockSpec((l, 128), lambda b, hp: (b, V_BLOCK0 + hp)), cspec(0), cspec(1), blk, blk, rows2],
        out_specs=[blk, blk, blk, rows2, rows2],
        out_shape=[wide, wide, wide, pair_rows, pair_rows],
        scratch_shapes=[pltpu.VMEM((128, l), BF16), pltpu.VMEM((128, l), BF16), pltpu.VMEM((128, l), BF16),
                        pltpu.VMEM((2, l), F32), pltpu.VMEM((128, l), F32), pltpu.VMEM((128, l), F32)],
        compiler_params=_params(("parallel", "parallel")),
    )(qn, kn, qkv, c_wide, c_wide, o, do, lse)


SCAN_ROWS = 512
SCAN_COLS = 1024


S5_IN = 128
S5_ST = 512
SCAN_CHUNKS = SCAN_COLS // S5_ST
SCAN_SEGS = 8
LANES = 128


def _cmul(ar, ai, br, bi):
    return ar * br - ai * bi, ar * bi + ai * br


def _powers_into(pw_r, pw_i, a_r, a_i, seg):
    pw_r[0:1, :] = a_r
    pw_i[0:1, :] = a_i
    for k in range(1, seg):
        pr, pi = _cmul(pw_r[k - 1:k, :], pw_i[k - 1:k, :], a_r, a_i)
        pw_r[k:k + 1, :] = pr
        pw_i[k:k + 1, :] = pi


def _interleave(dst, src, seg):
    for h in range(src.shape[0]):
        for j in range(seg):
            dst[h, j * SCAN_SEGS:(j + 1) * SCAN_SEGS, :] = src[h, pl.ds(j, SCAN_SEGS, stride=seg), :]


def _deinterleave(dst, src, seg):
    for h in range(src.shape[0]):
        for j in range(seg):
            dst[h, pl.ds(j, SCAN_SEGS, stride=seg), :] = src[h, j * SCAN_SEGS:(j + 1) * SCAN_SEGS, :]


def _interleaved(ref, tmp_a, tmp_b, seg):
    n = ref.shape[1] // LANES
    for h in range(n):
        tmp_a[h] = ref[:, h * LANES:(h + 1) * LANES].astype(F32)
    _interleave(tmp_b, tmp_a, seg)
    return jnp.concatenate([tmp_b[h] for h in range(n)], axis=1)


def _store_deinterleaved(ref, val, tmp_a, tmp_b, seg):
    n = ref.shape[1] // LANES
    for h in range(n):
        tmp_a[h] = val[:, h * LANES:(h + 1) * LANES]
    _deinterleave(tmp_b, tmp_a, seg)
    for h in range(n):
        ref[:, h * LANES:(h + 1) * LANES] = tmp_b[h]


def _segment_scan(b_r, b_i, x_r, x_i, pw_r, pw_i, car_r, car_i, seg, sign, reverse, visit=None):
    nc = b_r.shape[0]
    sub = lax.broadcasted_iota(jnp.int32, (SCAN_SEGS, LANES), 0)
    lanes = lambda c: slice(c * LANES, (c + 1) * LANES)
    rows = lambda j: pl.ds(pl.multiple_of(((seg - 1 - j) if reverse else j) * SCAN_SEGS, SCAN_SEGS), SCAN_SEGS)
    a1 = [(pw_r[0:1, lanes(c)], sign * pw_i[0:1, lanes(c)]) for c in range(nc)]

    def local(j, xs):
        out = []
        for c in range(nc):
            xr, xi = xs[2 * c], xs[2 * c + 1]
            nr = a1[c][0] * xr - a1[c][1] * xi + b_r[c, rows(j), :]
            ni = a1[c][0] * xi + a1[c][1] * xr + b_i[c, rows(j), :]
            x_r[c, rows(j), :] = nr
            x_i[c, rows(j), :] = ni
            out += [nr, ni]
        return tuple(out)

    zero = jnp.zeros((SCAN_SEGS, LANES), F32)
    ends = lax.fori_loop(0, seg, local, (zero,) * (2 * nc))

    if reverse:
        first = sub == SCAN_SEGS - 1
        neighbour = lambda v: pltpu.roll(v, SCAN_SEGS - 1, 0)
        shift = lambda v, d: jnp.where(sub < SCAN_SEGS - d, pltpu.roll(v, SCAN_SEGS - d, 0), 0.0)
    else:
        first = sub == 0
        neighbour = lambda v: pltpu.roll(v, 1, 0)
        shift = lambda v, d: jnp.where(sub >= d, pltpu.roll(v, d, 0), 0.0)
    last = 0 if reverse else SCAN_SEGS - 1
    entries = []
    for c in range(nc):
        er, ei = ends[2 * c], ends[2 * c + 1]
        pr, pi = pw_r[seg - 1:seg, lanes(c)], sign * pw_i[seg - 1:seg, lanes(c)]
        yr = jnp.where(first, car_r[:, lanes(c)], neighbour(er))
        yi = jnp.where(first, car_i[:, lanes(c)], neighbour(ei))
        qr, qi = pr, pi
        for d in (1, 2, 4):
            mr, mi = _cmul(qr, qi, shift(yr, d), shift(yi, d))
            yr, yi = yr + mr, yi + mi
            qr, qi = _cmul(qr, qi, qr, qi)
        lr, li = _cmul(pr, pi, yr, yi)
        car_r[:, lanes(c)] = (er + lr)[last:last + 1, :]
        car_i[:, lanes(c)] = (ei + li)[last:last + 1, :]
        entries += [yr, yi]

    def correct(j, prev):
        out = []
        row_r, row_i = pw_r[pl.ds(j, 1), :], sign * pw_i[pl.ds(j, 1), :]
        for c in range(nc):
            mr, mi = _cmul(row_r[:, lanes(c)], row_i[:, lanes(c)], entries[2 * c], entries[2 * c + 1])
            nr = x_r[c, rows(j), :] + mr
            ni = x_i[c, rows(j), :] + mi
            x_r[c, rows(j), :] = nr
            x_i[c, rows(j), :] = ni
            if visit is not None:
                visit(c, rows(j), prev[2 * c], prev[2 * c + 1])
            out += [nr, ni]
        return tuple(out)

    lax.fori_loop(0, seg, correct, tuple(entries))


def _s5_fwd(uf, bbr, bbi, cr, ci, ar, ai, seqs):
    t = uf.shape[0]
    l = t // seqs
    tl = min(SCAN_ROWS, l)
    nl = l // tl
    seg = tl // SCAN_SEGS
    cb, nq = SCAN_COLS, SCAN_CHUNKS
    nc = cb // LANES
    per = S5_ST // LANES

    def body(u_ref, bbr_ref, bbi_ref, cr_ref, ci_ref, ar_ref, ai_ref, x_r, x_i, ys_ref,
             car_r, car_i, pw_r, pw_i, b_r, b_i, tmp_a, tmp_b):
        @pl.when(pl.program_id(2) == 0)
        def _():
            car_r[...] = jnp.zeros(car_r.shape, F32)
            car_i[...] = jnp.zeros(car_i.shape, F32)
            _powers_into(pw_r, pw_i, ar_ref[...], ai_ref[...], seg)

        u = _interleaved(u_ref, tmp_a, tmp_b, seg).astype(BF16)
        for q in range(nq):
            uq = u[:, q * S5_IN:(q + 1) * S5_IN]
            br = jnp.dot(uq, bbr_ref[q], preferred_element_type=F32)
            bi = jnp.dot(uq, bbi_ref[q], preferred_element_type=F32)
            for s in range(per):
                b_r[q * per + s] = br[:, s * LANES:(s + 1) * LANES]
                b_i[q * per + s] = bi[:, s * LANES:(s + 1) * LANES]
        _segment_scan(b_r, b_i, x_r, x_i, pw_r, pw_i, car_r, car_i, seg, 1.0, False)
        wide = lambda buf, q: jnp.concatenate([buf[q * per + s] for s in range(per)], axis=1).astype(BF16)
        ys = [jnp.dot(wide(x_r, q), cr_ref[q], preferred_element_type=F32)
              + jnp.dot(wide(x_i, q), ci_ref[q], preferred_element_type=F32) for q in range(nq)]
        _store_deinterleaved(ys_ref, jnp.concatenate(ys, axis=1), tmp_a, tmp_b, seg)

    rows = lambda w: pl.BlockSpec((tl, w), lambda s, j, r: (s * nl + r, j))
    state = pl.BlockSpec((nc, tl, LANES), lambda s, j, r: (j, s * nl + r, 0))
    chunk = lambda a: pl.BlockSpec((nq,) + a.shape[1:], lambda s, j, r: (j, 0, 0))
    par = pl.BlockSpec((1, cb), lambda s, j, r: (0, j))
    return pl.pallas_call(
        body, name="s5_fwd", grid=(seqs, S5_CH // cb, nl),
        in_specs=[rows(nq * S5_IN), chunk(bbr), chunk(bbi), chunk(cr), chunk(ci), par, par],
        out_specs=[state, state, rows(nq * S5_IN)],
        out_shape=[jax.ShapeDtypeStruct((S5_CH // LANES, t, LANES), F32)] * 2
        + [jax.ShapeDtypeStruct((t, S5_WIDTH), F32)],
        scratch_shapes=[pltpu.VMEM((1, cb), F32), pltpu.VMEM((1, cb), F32), pltpu.VMEM((seg, cb), F32),
                        pltpu.VMEM((seg, cb), F32)] + [pltpu.VMEM((nc, tl, LANES), F32)] * 2
        + [pltpu.VMEM((nq * S5_IN // LANES, tl, LANES), F32)] * 2,
        compiler_params=_params(("parallel", "parallel", "arbitrary")),
    )(uf, bbr, bbi, cr, ci, ar, ai)


def _s5_bwd(dys, uf, xr, xi, bbr, bbi, cr, ci, ar, ai, seqs):
    t = dys.shape[0]
    l = t // seqs
    tl = min(SCAN_ROWS, l)
    nl = l // tl
    seg = tl // SCAN_SEGS
    cb, nq = SCAN_COLS, SCAN_CHUNKS
    nc = cb // LANES
    per = S5_ST // LANES

    def body(dy_ref, u_ref, x_r, x_i, bbr_ref, bbi_ref, cr_ref, ci_ref, ar_ref, ai_ref,
             du_ref, dbbr_ref, dbbi_ref, dcr_ref, dci_ref, dar_ref, dai_ref,
             car_r, car_i, pw_r, pw_i, g_r, g_i, lam_r, lam_i, acc_r, acc_i, tmp_a, tmp_b):
        @pl.when(pl.program_id(2) == 0)
        def _():
            car_r[...] = jnp.zeros(car_r.shape, F32)
            car_i[...] = jnp.zeros(car_i.shape, F32)
            _powers_into(pw_r, pw_i, ar_ref[...], ai_ref[...], seg)
            for acc_ref in (dbbr_ref, dbbi_ref, dcr_ref, dci_ref, dar_ref, dai_ref):
                acc_ref[...] = jnp.zeros(acc_ref.shape, F32)

        dy = _interleaved(dy_ref, tmp_a, tmp_b, seg).astype(BF16)
        for q in range(nq):
            dyq = dy[:, q * S5_IN:(q + 1) * S5_IN]
            gr = lax.dot_general(dyq, cr_ref[q], _NT, preferred_element_type=F32)
            gi = lax.dot_general(dyq, ci_ref[q], _NT, preferred_element_type=F32)
            for s in range(per):
                g_r[q * per + s] = gr[:, s * LANES:(s + 1) * LANES]
                g_i[q * per + s] = gi[:, s * LANES:(s + 1) * LANES]
        acc_r[...] = jnp.zeros(acc_r.shape, F32)
        acc_i[...] = jnp.zeros(acc_i.shape, F32)

        def visit(c, rws, lr, li):
            xr_t, xi_t = x_r[c, rws, :], x_i[c, rws, :]
            acc_r[c] += lr * xr_t + li * xi_t
            acc_i[c] += li * xr_t - lr * xi_t

        _segment_scan(g_r, g_i, lam_r, lam_i, pw_r, pw_i, car_r, car_i, seg, -1.0, True, visit)
        for c in range(nc):
            dar_ref[:, c * LANES:(c + 1) * LANES] += jnp.sum(acc_r[c], axis=0, keepdims=True)
            dai_ref[:, c * LANES:(c + 1) * LANES] += jnp.sum(acc_i[c], axis=0, keepdims=True)
        u = _interleaved(u_ref, tmp_a, tmp_b, seg).astype(BF16)
        wide = lambda buf, q: jnp.concatenate([buf[q * per + s] for s in range(per)], axis=1).astype(BF16)
        du = []
        for q in range(nq):
            io = slice(q * S5_IN, (q + 1) * S5_IN)
            lq_r, lq_i = wide(lam_r, q), wide(lam_i, q)
            du.append(lax.dot_general(lq_r, bbr_ref[q], _NT, preferred_element_type=F32)
                      + lax.dot_general(lq_i, bbi_ref[q], _NT, preferred_element_type=F32))
            dbbr_ref[q] += lax.dot_general(u[:, io], lq_r, _TN, preferred_element_type=F32)
            dbbi_ref[q] += lax.dot_general(u[:, io], lq_i, _TN, preferred_element_type=F32)
            dcr_ref[q] += lax.dot_general(wide(x_r, q), dy[:, io], _TN, preferred_element_type=F32)
            dci_ref[q] += lax.dot_general(wide(x_i, q), dy[:, io], _TN, preferred_element_type=F32)
        _store_deinterleaved(du_ref, jnp.concatenate(du, axis=1), tmp_a, tmp_b, seg)

    rows = lambda w: pl.BlockSpec((tl, w), lambda s, j, r: (s * nl + nl - 1 - r, j))
    state = pl.BlockSpec((nc, tl, LANES), lambda s, j, r: (j, s * nl + nl - 1 - r, 0))
    chunk = lambda a: pl.BlockSpec((nq,) + a.shape[1:], lambda s, j, r: (j, 0, 0))
    acc = lambda a: pl.BlockSpec((None, nq) + a.shape[1:], lambda s, j, r: (s, j, 0, 0))
    par = pl.BlockSpec((1, cb), lambda s, j, r: (0, j))
    par_acc = pl.BlockSpec((None, 1, cb), lambda s, j, r: (s, 0, j))
    per_seq = lambda a: jax.ShapeDtypeStruct((seqs,) + a.shape, F32)
    return pl.pallas_call(
        body, name="s5_bwd", grid=(seqs, S5_CH // cb, nl),
        in_specs=[rows(nq * S5_IN), rows(nq * S5_IN), state, state, chunk(bbr), chunk(bbi), chunk(cr), chunk(ci),
                  par, par],
        out_specs=[rows(nq * S5_IN), acc(bbr), acc(bbi), acc(cr), acc(ci), par_acc, par_acc],
        out_shape=[jax.ShapeDtypeStruct((t, S5_WIDTH), F32), per_seq(bbr), per_seq(bbi), per_seq(cr), per_seq(ci),
                   jax.ShapeDtypeStruct((seqs, 1, S5_CH), F32), jax.ShapeDtypeStruct((seqs, 1, S5_CH), F32)],
        scratch_shapes=[pltpu.VMEM((1, cb), F32), pltpu.VMEM((1, cb), F32), pltpu.VMEM((seg, cb), F32),
                        pltpu.VMEM((seg, cb), F32)] + [pltpu.VMEM((nc, tl, LANES), F32)] * 4
        + [pltpu.VMEM((nc, SCAN_SEGS, LANES), F32)] * 2 + [pltpu.VMEM((nq * S5_IN // LANES, tl, LANES), F32)] * 2,
        compiler_params=_params(("parallel", "parallel", "arbitrary")),
    )(dys, uf, xr, xi, bbr, bbi, cr, ci, ar, ai)


XATT_BLOCK = 2048


def _xatt_probs(qv, kv):
    s = lax.dot_general(qv, kv, _NT, preferred_element_type=F32) * (X_HEAD_DIM ** -0.5)
    e = jnp.exp(s - jnp.max(s, axis=-1, keepdims=True))
    return e / jnp.sum(e, axis=-1, keepdims=True)


def _xatt_fwd(q, k, kv, seqs):
    t = q.shape[0]
    tq = min(XATT_BLOCK, t // seqs)
    nq = t // seqs // tq

    def body(q_ref, k_ref, v_ref, o_ref):
        p = _xatt_probs(q_ref[...], k_ref[...])
        o_ref[...] = jnp.dot(p.astype(BF16), v_ref[...].astype(BF16), preferred_element_type=F32).astype(o_ref.dtype)

    qs = pl.BlockSpec((tq, X_HEAD_DIM), lambda b, h, i: (b * nq + i, h))
    return pl.pallas_call(
        body, name="xatt_fwd", grid=(seqs, N_X_HEADS, nq),
        in_specs=[qs, pl.BlockSpec((N_MEM, X_HEAD_DIM), lambda b, h, i: (b, h)),
                  pl.BlockSpec((N_MEM, X_HEAD_DIM), lambda b, h, i: (b, N_X_HEADS + h))],
        out_specs=qs, out_shape=jax.ShapeDtypeStruct(q.shape, BF16),
        compiler_params=_params(("parallel", "parallel", "parallel")),
    )(q, k, kv)


def _xatt_bwd(q, k, kv, do, seqs):
    t = q.shape[0]
    tq = min(XATT_BLOCK, t // seqs)
    nq = t // seqs // tq
    scale = X_HEAD_DIM ** -0.5

    def body(q_ref, k_ref, v_ref, do_ref, dq_ref, dk_ref, dv_ref):
        @pl.when(pl.program_id(2) == 0)
        def _():
            dk_ref[...] = jnp.zeros(dk_ref.shape, F32)
            dv_ref[...] = jnp.zeros(dv_ref.shape, F32)

        qv, kk = q_ref[...], k_ref[...]
        p = _xatt_probs(qv, kk)
        dob = do_ref[...].astype(BF16)
        dp = lax.dot_general(dob, v_ref[...].astype(BF16), _NT, preferred_element_type=F32)
        ds = p * (dp - jnp.sum(dp * p, axis=-1, keepdims=True))
        dsb = ds.astype(BF16)
        dq_ref[...] = jnp.dot(dsb, kk, preferred_element_type=F32) * scale
        dk_ref[...] += lax.dot_general(dsb, qv, _TN, preferred_element_type=F32) * scale
        dv_ref[...] += lax.dot_general(p.astype(BF16), dob, _TN, preferred_element_type=F32)

    qs = pl.BlockSpec((tq, X_HEAD_DIM), lambda b, h, i: (b * nq + i, h))
    ks = pl.BlockSpec((N_MEM, X_HEAD_DIM), lambda b, h, i: (b, h))
    return pl.pallas_call(
        body, name="xatt_bwd", grid=(seqs, N_X_HEADS, nq),
        in_specs=[qs, ks, pl.BlockSpec((N_MEM, X_HEAD_DIM), lambda b, h, i: (b, N_X_HEADS + h)), qs],
        out_specs=[qs, ks, ks],
        out_shape=[jax.ShapeDtypeStruct(q.shape, F32), jax.ShapeDtypeStruct(k.shape, F32),
                   jax.ShapeDtypeStruct(k.shape, F32)],
        compiler_params=_params(("parallel", "parallel", "arbitrary")),
    )(q, k, kv, do)


CONV_COLS = 256


def _shift_down(x, k, row):
    return jnp.where(row >= k, pltpu.roll(x, k, 0), 0.0)


def _shift_up(x, k, row):
    n = x.shape[0]
    return jnp.where(row < n - k, pltpu.roll(x, n - k, 0), 0.0)


def _down_from(x, prev, k, row):
    n = x.shape[0]
    y = pltpu.roll(x, k, 0)
    head = jnp.where(row[0:8] >= k, y[0:8], pltpu.roll(prev[n - 8:n], k, 0))
    return jnp.concatenate([head, y[8:]], axis=0)


GATE_ROWS = 512


def _ffn_up_gate(hn, w_up, w, b, seqs):
    t = hn.shape[0]
    l = t // seqs
    nc = D_FF // CONV_COLS

    rc = min(GATE_ROWS, l)

    def body(a_ref, wg_ref, wu_ref, w_ref, b_ref, g_ref, u_ref, p_ref, o_ref):
        wv, bias = w_ref[...], b_ref[...]
        row = lax.broadcasted_iota(jnp.int32, (rc, CONV_COLS), 0)
        prev = jnp.zeros((rc, CONV_COLS), F32)
        for k in range(l // rc):
            rows = slice(k * rc, (k + 1) * rc)
            a = a_ref[rows, :]
            gb = jnp.dot(a, wg_ref[...], preferred_element_type=F32).astype(BF16)
            ub = jnp.dot(a, wu_ref[...], preferred_element_type=F32).astype(BF16)
            g_ref[rows, :] = gb
            u_ref[rows, :] = ub
            g = gb.astype(F32)
            pre = bias + wv[0:1, :] * _down_from(g, prev, 2, row) + wv[1:2, :] * _down_from(g, prev, 1, row) \
                + wv[2:3, :] * g
            p_ref[rows, :] = pre.astype(p_ref.dtype)
            o_ref[rows, :] = (pre * jax.nn.sigmoid(pre) * ub.astype(F32)).astype(o_ref.dtype)
            prev = g

    cols = pl.BlockSpec((l, CONV_COLS), lambda s, j: (s, j))
    half = jax.ShapeDtypeStruct((t, D_FF), BF16)
    return pl.pallas_call(
        body, name="ffn_up_gate", grid=(seqs, nc),
        in_specs=[pl.BlockSpec((l, hn.shape[1]), lambda s, j: (s, 0)),
                  pl.BlockSpec((hn.shape[1], CONV_COLS), lambda s, j: (0, j)),
                  pl.BlockSpec((hn.shape[1], CONV_COLS), lambda s, j: (0, nc + j)),
                  pl.BlockSpec((3, CONV_COLS), lambda s, j: (0, j)), pl.BlockSpec((1, CONV_COLS), lambda s, j: (0, j))],
        out_specs=[cols] * 4, out_shape=[half] * 4,
        compiler_params=_params(("parallel", "parallel")),
    )(hn, w_up, w_up, w, b)


def _ffn_down_dx_gate(dh, w_down, gate, up, pre, w, seqs):
    t = dh.shape[0]
    l = t // seqs
    nc = D_FF // CONV_COLS
    steps = nc * seqs

    def body(dh_ref, wd_ref, g_ref, u_ref, p_ref, w_ref, dgu_ref, dw_ref, db_ref, stage, sems):
        s, j = pl.program_id(0), pl.program_id(1)
        n = s * nc + j
        slot = n % 2

        def copies(slot_, j_, s_):
            rows = pl.ds(pl.multiple_of(s_ * l, 16), l)
            return [pltpu.make_async_copy(
                stage.at[slot_, half],
                dgu_ref.at[rows, pl.ds(pl.multiple_of((half * nc + j_) * CONV_COLS, 128), CONV_COLS)],
                sems.at[slot_, half]) for half in (0, 1)]

        @pl.when(n >= 2)
        def _():
            for cp in copies(slot, j, s):
                cp.wait()

        da = lax.dot_general(dh_ref[...], wd_ref[...], _NT, preferred_element_type=F32)
        g, pre, wv = g_ref[...].astype(F32), p_ref[...].astype(F32), w_ref[...]
        row = lax.broadcasted_iota(jnp.int32, g.shape, 0)
        sg = jax.nn.sigmoid(pre)
        silu = pre * sg
        stage[slot, 1] = (da * silu).astype(stage.dtype)
        dpre = da * u_ref[...].astype(F32) * (sg * (1.0 + pre * (1.0 - sg)))
        dpre1, dpre2 = _shift_up(dpre, 1, row), _shift_up(dpre, 2, row)
        dg = wv[2:3, :] * dpre + wv[1:2, :] * dpre1 + wv[0:1, :] * dpre2
        stage[slot, 0] = dg.astype(stage.dtype)
        for cp in copies(slot, j, s):
            cp.start()
        dw_ref[0:1, :] = jnp.sum(dpre2 * g, axis=0, keepdims=True)
        dw_ref[1:2, :] = jnp.sum(dpre1 * g, axis=0, keepdims=True)
        dw_ref[2:3, :] = jnp.sum(dpre * g, axis=0, keepdims=True)
        db_ref[...] = jnp.sum(dpre, axis=0, keepdims=True)

        @pl.when(n == steps - 1)
        def _():
            for cp in copies(slot, j, s) + (copies(1 - slot, j, s) if steps > 1 else []):
                cp.wait()

    cols = pl.BlockSpec((l, CONV_COLS), lambda s, j: (s, j))
    return pl.pallas_call(
        body, name="ffn_down_dx_gate", grid=(seqs, nc),
        in_specs=[pl.BlockSpec((l, dh.shape[1]), lambda s, j: (s, 0)),
                  pl.BlockSpec((CONV_COLS, dh.shape[1]), lambda s, j: (j, 0)), cols, cols, cols,
                  pl.BlockSpec((3, CONV_COLS), lambda s, j: (0, j))],
        out_specs=[ANY, pl.BlockSpec((None, 3, CONV_COLS), lambda s, j: (s, 0, j)),
                   pl.BlockSpec((None, 1, CONV_COLS), lambda s, j: (s, 0, j))],
        out_shape=[jax.ShapeDtypeStruct((t, 2 * D_FF), BF16), jax.ShapeDtypeStruct((seqs, 3, D_FF), F32),
                   jax.ShapeDtypeStruct((seqs, 1, D_FF), F32)],
        scratch_shapes=[pltpu.VMEM((2, 2, l, CONV_COLS), BF16), pltpu.SemaphoreType.DMA((2, 2))],
        compiler_params=_params(("arbitrary", "arbitrary")),
    )(dh, w_down, gate, up, pre, w)


def _loss_head(h, target):
    t, d = h.shape
    tm = _pick(t, (256, 128, 8))

    def body(h_ref, t_ref, dh_ref, dhb_ref, loss_ref):
        @pl.when(pl.program_id(0) == 0)
        def _():
            loss_ref[...] = jnp.zeros(loss_ref.shape, F32)

        e = h_ref[...] - t_ref[...]
        dh = e * (1.0 / d)
        dh_ref[...] = dh
        dhb_ref[...] = dh.astype(BF16)
        loss_ref[...] += (0.5 / d) * jnp.sum(jnp.sum(e * e, axis=1, keepdims=True), axis=0, keepdims=True)

    blk = pl.BlockSpec((tm, d), lambda i: (i, 0))
    return pl.pallas_call(
        body, name="loss_head", grid=(t // tm,), in_specs=[blk, blk],
        out_specs=[blk, blk, pl.BlockSpec((1, 1), lambda i: (0, 0))],
        out_shape=[jax.ShapeDtypeStruct((t, d), F32), jax.ShapeDtypeStruct((t, d), BF16),
                   jax.ShapeDtypeStruct((1, 1), F32)],
        compiler_params=_params(("arbitrary",)),
    )(h, target)


def _s5_discretise(a_re, a_im, log_dt, b_re, b_im):
    dt = jnp.exp(log_dt)[:, None]
    mag = jnp.exp(a_re * dt)
    lb_r = mag * jnp.cos(a_im * dt)
    lb_i = mag * jnp.sin(a_im * dt)
    den = a_re * a_re + a_im * a_im
    nr = lb_r - 1.0
    coef_r = (nr * a_re + lb_i * a_im) / den
    coef_i = (lb_i * a_re - nr * a_im) / den
    bb_r = coef_r[:, :, None] * b_re - coef_i[:, :, None] * b_im
    bb_i = coef_r[:, :, None] * b_im + coef_i[:, :, None] * b_re
    return lb_r, lb_i, bb_r, bb_i


S5_CHUNKS = 4
S5_PER = S5_GROUPS // S5_CHUNKS


def _blockdiag_in(bb):
    eye = jnp.eye(S5_PER, dtype=bb.dtype)
    return jnp.einsum("jgpc,gh->jgchp", bb.reshape(S5_CHUNKS, S5_PER, S5_STATE, S5_GROUP_CH), eye).reshape(
        S5_CHUNKS, S5_PER * S5_GROUP_CH, S5_PER * S5_STATE)


def _blockdiag_in_grad(d):
    eye = jnp.eye(S5_PER, dtype=d.dtype)
    return jnp.einsum("jgchp,gh->jgpc", d.reshape(S5_CHUNKS, S5_PER, S5_GROUP_CH, S5_PER, S5_STATE), eye).reshape(
        S5_GROUPS, S5_STATE, S5_GROUP_CH)


def _blockdiag_out(c):
    eye = jnp.eye(S5_PER, dtype=c.dtype)
    return jnp.einsum("jgcp,gh->jgphc", c.reshape(S5_CHUNKS, S5_PER, S5_GROUP_CH, S5_STATE), eye).reshape(
        S5_CHUNKS, S5_PER * S5_STATE, S5_PER * S5_GROUP_CH)


def _blockdiag_out_grad(d):
    eye = jnp.eye(S5_PER, dtype=d.dtype)
    return jnp.einsum("jgphc,gh->jgcp", d.reshape(S5_CHUNKS, S5_PER, S5_STATE, S5_PER, S5_GROUP_CH), eye).reshape(
        S5_GROUPS, S5_GROUP_CH, S5_STATE)


def _local_step(x3, mem3, target3, p, wb, late_weights=None, early_grads=None):
    seqs, l, d = x3.shape
    t = seqs * l
    x = x3.reshape(t, d)
    mem = mem3.reshape(seqs * N_MEM, d)
    target = target3.reshape(t, d)
    full = lambda a: (a, a.shape[1], 0, 0)

    s5_in = (p["s5_a_re"], p["s5_a_im"], p["s5_log_dt"], p["s5_b_re"], p["s5_b_im"])
    (lb_r, lb_i, bb_r, bb_i), s5_pull = jax.vjp(_s5_discretise, *s5_in)
    ar, ai = lb_r.reshape(1, S5_CH), lb_i.reshape(1, S5_CH)
    bbr_d, bbi_d = _blockdiag_in(bb_r).astype(BF16), _blockdiag_in(bb_i).astype(BF16)
    cr_d, ci_d = _blockdiag_out(p["s5_c_re"]).astype(BF16), (-_blockdiag_out(p["s5_c_im"])).astype(BF16)
    d_row = p["s5_d"].reshape(1, S5_WIDTH)

    hn1 = _rowwise(_rms, [full(x)], [p["norm_mix"]], [(d, d, 0, BF16)], "norm_mix_fwd")
    if late_weights is not None:
        wb = dict(wb, **late_weights("first", hn1))
    conv_w = wb["ffn_conv_w"] if "ffn_conv_w" in wb else p["ffn_conv_w"]
    w_in = wb["w_in"]
    w_qkv = w_in[:, :3 * FOX_WIDTH]
    w_uf = jnp.concatenate(
        [w_in[:, 3 * FOX_WIDTH + N_FOX_HEADS:], w_in[:, 3 * FOX_WIDTH:3 * FOX_WIDTH + N_FOX_HEADS],
         jnp.zeros((d, UF_COLS - S5_WIDTH - N_FOX_HEADS), w_in.dtype)], axis=1)
    qkv = _mm(hn1, w_qkv, "nn", "in_qkv")
    uf = _mm(hn1, w_uf, "nn", "in_uf")

    bh = seqs * N_FOX_HEADS
    q_pair = (qkv, 128, 0, 1)
    k_pair = (qkv, 128, N_PAIRS, 1)
    gq2, gk2 = jnp.tile(p["fox_q_norm"], (1, 2)), jnp.tile(p["fox_k_norm"], (1, 2))
    pair_out = [(FOX_WIDTH, 128, 1, BF16)]
    qn = _rowwise(_rms_pair, [q_pair], [gq2], pair_out, "fox_qnorm_fwd", heads=N_PAIRS)
    kn = _rowwise(_rms_pair, [k_pair], [gk2], pair_out, "fox_knorm_fwd", heads=N_PAIRS)

    f_rows = uf[:, S5_WIDTH:S5_WIDTH + N_FOX_HEADS].reshape(seqs, l, N_FOX_HEADS).transpose(0, 2, 1).reshape(bh, l)
    f_bias = jnp.tile(p["fox_f_bias"].reshape(N_FOX_HEADS, 1), (seqs, 1))
    c_wide = jnp.broadcast_to(_forget_fwd(f_rows, f_bias)[:, :, None], (bh, l, 128))
    fox, lse = _fox_fwd(qn, kn, qkv, c_wide, seqs)

    xr, xi, ys = _s5_fwd(uf, bbr_d, bbi_d, cr_d, ci_d, ar, ai, seqs)
    u_blk = (uf, S5_WIDTH, 0, 0)
    yg = _rowwise(_s5_act, [full(ys), u_blk], [d_row], [(S5_WIDTH, S5_WIDTH, 0, F32)], "s5_act_fwd")
    if late_weights is not None:
        wb = dict(wb, **late_weights("mid", yg))
    z = _mm(yg, wb["s5_w_glu"], "nn", "s5_glu")
    y2n = _rowwise(_s5_gate, [full(yg), full(z)], [p["s5_b_glu"], p["out_norm_s5"]],
                   [(S5_WIDTH, S5_WIDTH, 0, BF16)], "s5_gate_fwd")
    foxn = _rowwise(_rms, [full(fox)], [p["out_norm_fox"]], [(FOX_WIDTH, FOX_WIDTH, 0, BF16)], "fox_outnorm_fwd")
    mixed = jnp.concatenate([foxn, y2n], axis=1)
    h1 = _mm(mixed, wb["w_out"], "nn", "mix_out", res=x)
    if late_weights is not None:
        wb = dict(wb, **late_weights("late", h1))

    hn2 = _rowwise(_rms, [full(h1)], [p["norm_cross"]], [(d, d, 0, BF16)], "norm_cross_fwd")
    mn = _rowwise(_rms, [full(mem)], [p["norm_mem"]], [(d, d, 0, BF16)], "norm_mem_fwd")
    xq_raw = _mm(hn2, wb["w_xq"], "nn", "x_q")
    kv = _mm(mn, wb["w_xkv"], "nn", "x_kv")
    xh = lambda a: (a, X_HEAD_DIM, 0, 1)
    xqn = _rowwise(_rms, [xh(xq_raw)], [p["xq_norm"]], [(d, X_HEAD_DIM, 1, BF16)], "x_qnorm_fwd", heads=N_X_HEADS)
    xkn = _rowwise(_rms, [xh(kv)], [p["xk_norm"]], [(d, X_HEAD_DIM, 1, BF16)], "x_knorm_fwd", heads=N_X_HEADS)
    xo = _xatt_fwd(xqn, xkn, kv, seqs)
    h2 = _mm(xo, wb["w_xo"], "nn", "x_out", res=h1)

    hn3 = _rowwise(_rms, [full(h2)], [p["norm_ffn"]], [(d, d, 0, BF16)], "norm_ffn_fwd")
    gate, up, pre, act = _ffn_up_gate(hn3, wb["w_ffn_up"], conv_w, p["ffn_conv_b"], seqs)
    h3 = _mm(act, wb["w_ffn_down"], "nn", "ffn_down", res=h2)
    dh3, dh3_b, loss = _loss_head(h3, target)

    g = {}
    late_dt = BF16 if early_grads is not None else F32
    g["w_ffn_down"] = _mm(act, dh3_b, "tn", "ffn_down_dw", out_dtype=late_dt)
    dgu, dconv_w, dconv_b = _ffn_down_dx_gate(dh3_b, wb["w_ffn_down"], gate, up, pre, conv_w, seqs)
    g["ffn_conv_w"], g["ffn_conv_b"] = jnp.sum(dconv_w, axis=0), jnp.sum(dconv_b, axis=0)
    dhn3 = _mm(dgu, wb["w_ffn_up"], "nt", "ffn_up_dx", out_dtype=BF16)
    g["w_ffn_up"] = _mm(hn3, dgu, "tn", "ffn_up_dw", out_dtype=late_dt)
    (dh2,), (g["norm_ffn"],) = _rowwise_vjp(_rms, [full(h2)], [p["norm_ffn"]], [full(dhn3)], "norm_ffn_bwd",
                                            adds=[full(dh3)])

    dxo = _mm(dh2, wb["w_xo"], "nt", "x_out_dx", out_dtype=BF16)
    g["w_xo"] = _mm(xo, dh2, "tn", "x_out_dw", out_dtype=late_dt)
    dxqn, dxkn, dxv = _xatt_bwd(xqn, xkn, kv, dxo, seqs)
    (dxq_raw,), (g["xq_norm"],) = _rowwise_vjp(_rms, [xh(xq_raw)], [p["xq_norm"]], [xh(dxqn)], "x_qnorm_bwd",
                                               heads=N_X_HEADS, row_dtypes=[BF16])
    (dxk_raw,), (g["xk_norm"],) = _rowwise_vjp(_rms, [xh(kv)], [p["xk_norm"]], [xh(dxkn)], "x_knorm_bwd",
                                               heads=N_X_HEADS, row_dtypes=[BF16])
    dkv = jnp.concatenate([dxk_raw, dxv.astype(BF16)], axis=1)
    dhn2 = _mm(dxq_raw, wb["w_xq"], "nt", "x_q_dx", out_dtype=BF16)
    g["w_xq"] = _mm(hn2, dxq_raw, "tn", "x_q_dw", out_dtype=late_dt)
    dmn = _mm(dkv, wb["w_xkv"], "nt", "x_kv_dx")
    g["w_xkv"] = _mm(mn, dkv, "tn", "x_kv_dw", out_dtype=late_dt)
    norm_cross = p["norm_cross"]
    if early_grads is not None:
        norm_cross = norm_cross + early_grads("late", {n: g[n] for n in LATE_WEIGHTS})[0:1, 0:1]
    (dh1,), (g["norm_cross"],) = _rowwise_vjp(_rms, [full(h1)], [norm_cross], [full(dhn2)], "norm_cross_bwd",
                                              adds=[full(dh2)])
    _, (g["norm_mem"],) = _rowwise_vjp(_rms, [full(mem)], [p["norm_mem"]], [full(dmn)], "norm_mem_bwd",
                                       row_dtypes=[BF16])

    dmixed = _mm(dh1, wb["w_out"], "nt", "mix_out_dx", out_dtype=BF16)
    g["w_out"] = _mm(mixed, dh1, "tn", "mix_out_dw", out_dtype=late_dt)
    (dfox,), (g["out_norm_fox"],) = _rowwise_vjp(_rms, [full(fox)], [p["out_norm_fox"]],
                                                 [(dmixed, FOX_WIDTH, 0, 0)], "fox_outnorm_bwd")
    (dyg_a, dz), (g["s5_b_glu"], g["out_norm_s5"]) = _rowwise_vjp(
        _s5_gate, [full(yg), full(z)], [p["s5_b_glu"], p["out_norm_s5"]], [(dmixed, S5_WIDTH, 1, 0)], "s5_gate_bwd",
        row_dtypes=[F32, BF16])
    dyg = _mm(dz, wb["s5_w_glu"], "nt", "s5_glu_dx", res=dyg_a)
    g["s5_w_glu"] = _mm(yg, dz, "tn", "s5_glu_dw", out_dtype=late_dt)
    if early_grads is not None:
        d_row = d_row + early_grads("mid", {n: g[n] for n in MID_WEIGHTS})[0:1, 0:1]
    (dys, du_a), (dd_row,) = _rowwise_vjp(_s5_act, [full(ys), u_blk], [d_row], [full(dyg)], "s5_act_bwd",
                                          row_dtypes=[BF16, F32])
    g["s5_d"] = dd_row
    du_b, dbbr_d, dbbi_d, dcr_d, dci_d, dar, dai = _s5_bwd(dys, uf, xr, xi, bbr_d, bbi_d, cr_d, ci_d, ar, ai, seqs)
    dbbr_d, dbbi_d, dcr_d, dci_d = (jnp.sum(a, axis=0) for a in (dbbr_d, dbbi_d, dcr_d, dci_d))
    d_lb_r = jnp.sum(dar, axis=0).reshape(S5_GROUPS, S5_STATE)
    d_lb_i = jnp.sum(dai, axis=0).reshape(S5_GROUPS, S5_STATE)
    g["s5_a_re"], g["s5_a_im"], g["s5_log_dt"], g["s5_b_re"], g["s5_b_im"] = s5_pull(
        (d_lb_r, d_lb_i, _blockdiag_in_grad(dbbr_d), _blockdiag_in_grad(dbbi_d)))
    g["s5_c_re"] = _blockdiag_out_grad(dcr_d)
    g["s5_c_im"] = -_blockdiag_out_grad(dci_d)

    dqn, dkn, dv, dc, dcq = _fox_bwd(qn, kn, qkv, c_wide, fox, dfox, lse, seqs)
    pair = lambda a: (a, 128, 0, 1)
    (dq_raw,), (dgq2,) = _rowwise_vjp(_rms_pair, [q_pair], [gq2], [pair(dqn)], "fox_qnorm_bwd", heads=N_PAIRS,
                                      row_dtypes=[BF16])
    (dk_raw,), (dgk2,) = _rowwise_vjp(_rms_pair, [k_pair], [gk2], [pair(dkn)], "fox_knorm_bwd", heads=N_PAIRS,
                                      row_dtypes=[BF16])
    g["fox_q_norm"] = dgq2[:, :HEAD_DIM] + dgq2[:, HEAD_DIM:]
    g["fox_k_norm"] = dgk2[:, :HEAD_DIM] + dgk2[:, HEAD_DIM:]
    df_rows, dfb = _forget_bwd(f_rows, f_bias, (dc + dcq).reshape(bh, l))
    g["fox_f_bias"] = jnp.sum(dfb.reshape(seqs, N_FOX_HEADS), axis=0)
    df = df_rows.reshape(seqs, N_FOX_HEADS, l).transpose(0, 2, 1).reshape(t, N_FOX_HEADS)
    dqkv = jnp.concatenate([dq_raw, dk_raw, dv.astype(BF16)], axis=1)
    duf = jnp.concatenate([du_a + du_b, df, jnp.zeros((t, UF_COLS - S5_WIDTH - N_FOX_HEADS), F32)],
                          axis=1).astype(BF16)
    dhn1 = _mm(duf, w_uf, "nt", "in_uf_dx", res=_mm(dqkv, w_qkv, "nt", "in_qkv_dx"), out_dtype=BF16)
    dw_qkv = _mm(hn1, dqkv, "tn", "in_qkv_dw")
    dw_uf = _mm(hn1, duf, "tn", "in_uf_dw")
    g["w_in"] = jnp.concatenate([dw_qkv, dw_uf[:, S5_WIDTH:S5_WIDTH + N_FOX_HEADS], dw_uf[:, :S5_WIDTH]], axis=1)
    (dx,), (g["norm_mix"],) = _rowwise_vjp(_rms, [full(x)], [p["norm_mix"]], [full(dhn1)], "norm_mix_bwd",
                                           adds=[full(dh1)])
    return loss, dx.reshape(seqs, l, d), g


def _place():
    return lax.axis_index("x"), lax.axis_index("y"), lax.axis_index("c")


def _other_chips(x, y):
    return [(1 - x, y), (x, 1 - y), (1 - x, 1 - y)]


ANY = pl.BlockSpec(memory_space=pl.ANY)


HBM = pl.BlockSpec(memory_space=pltpu.HBM)
SEM = pl.BlockSpec(memory_space=pltpu.SEMAPHORE)
DATAFLOW = pltpu.SideEffectType.DATAFLOW_SIDE_EFFECTING


def _in_hbm(a):
    return pltpu.with_memory_space_constraint(a, pltpu.HBM)


def _split_start(name, srcs, lands, n_copies, plan):
    n = len(srcs)

    def body(*refs):
        src_refs, land_refs = refs[:n], refs[n:2 * n]
        send_sems, recv_sems = refs[2 * n], refs[2 * n + 1]
        for i, (src, dst, dev) in enumerate(plan(src_refs, land_refs)):
            pltpu.make_async_remote_copy(src_ref=src, dst_ref=dst, send_sem=send_sems.at[i], recv_sem=recv_sems.at[i],
                                         device_id=dev, device_id_type=MESH).start()
        refs[-1][...] = jnp.zeros((8, 128), F32)

    res = pl.pallas_call(
        body, name=name, in_specs=[HBM] * (2 * n),
        out_specs=[SEM, SEM] + [HBM] * (2 * n) + [pl.BlockSpec(memory_space=pltpu.VMEM)],
        out_shape=[pltpu.SemaphoreType.DMA((n_copies,)), pltpu.SemaphoreType.DMA((n_copies,))]
        + [pltpu.HBM(a.shape, a.dtype) for a in list(srcs) + list(lands)] + [jax.ShapeDtypeStruct((8, 128), F32)],
        input_output_aliases={i: 2 + i for i in range(2 * n)},
        compiler_params=pltpu.CompilerParams(has_side_effects=DATAFLOW),
    )(*[_in_hbm(a) for a in list(srcs) + list(lands)])
    return res[0], res[1], list(res[2:2 + n]), list(res[2 + n:2 + 2 * n]), res[-1]


def _split_wait(name, send_sems, recv_sems, srcs, lands, after, plan):
    n = len(srcs)

    def body(*refs):
        src_refs, land_refs = refs[:n], refs[n:2 * n]
        send_ref, recv_ref = refs[2 * n], refs[2 * n + 1]
        for i, (src, dst, dev) in enumerate(plan(src_refs, land_refs)):
            cp = pltpu.make_async_remote_copy(src_ref=src, dst_ref=dst, send_sem=send_ref.at[i], recv_sem=recv_ref.at[i],
                                              device_id=dev, device_id_type=MESH)
            cp.wait_send()
            cp.wait_recv()

    res = pl.pallas_call(
        body, name=name, in_specs=[HBM] * (2 * n) + [SEM, SEM, ANY], out_specs=[HBM] * (2 * n),
        out_shape=[pltpu.HBM(a.shape, a.dtype) for a in list(srcs) + list(lands)],
        input_output_aliases={i: i for i in range(2 * n)},
        compiler_params=pltpu.CompilerParams(has_side_effects=DATAFLOW),
    )(*srcs, *lands, send_sems, recv_sems, after)
    return list(res[:n]), list(res[n:])


def _first_gather_plan(src_refs, land_refs):
    x, y, c = _place()
    mine = 2 * x + y
    (src, small), (land, small_land) = src_refs, land_refs
    r = src.shape[0]
    hr = r // 2
    half = src.at[pl.ds(pl.multiple_of(c * hr, 16), hr), :]
    half_dst = land.at[pl.ds(pl.multiple_of(mine * r + c * hr, 16), hr), :]
    copies = [(src, land.at[pl.ds(pl.multiple_of(mine * r, 16), r), :], (x, y, 1 - c)),
              (small, small_land.at[mine], (x, y, 1 - c))]
    for px, py in _other_chips(x, y):
        copies += [(half, half_dst, (px, py, c)), (small, small_land.at[mine], (px, py, c))]
    return copies


def _forward_to_sibling(full):
    def body(full_in, full_ref, send_sems, recv_sems):
        x, y, c = _place()
        r = full_ref.shape[0] // 4
        hr = r // 2
        copies = []
        for j, (px, py) in enumerate(_other_chips(x, y)):
            got = full_ref.at[pl.ds(pl.multiple_of((2 * px + py) * r + c * hr, 16), hr), :]
            cp = pltpu.make_async_remote_copy(
                src_ref=got, dst_ref=got, send_sem=send_sems.at[j], recv_sem=recv_sems.at[j],
                device_id=(x, y, 1 - c), device_id_type=MESH)
            cp.start()
            copies.append(cp)
        for cp in copies:
            cp.wait()

    return pl.pallas_call(
        body, name="gather_first_forward", in_specs=[ANY], out_specs=ANY,
        out_shape=jax.ShapeDtypeStruct(full.shape, full.dtype), input_output_aliases={0: 0},
        scratch_shapes=[pltpu.SemaphoreType.DMA((3,)), pltpu.SemaphoreType.DMA((3,))],
        compiler_params=pltpu.CompilerParams(has_side_effects=True),
    )(full)


def _late_gather_plan(col_kind):
    def plan(src_refs, land_refs):
        x, y, c = _place()
        mine = 2 * x + y
        copies = []
        for a, (src, land) in enumerate(zip(src_refs, land_refs)):
            r, cs = src.shape
            if col_kind[a]:
                dst = land.at[:, pl.ds(pl.multiple_of(mine * cs, 128), cs)]
            else:
                dst = land.at[pl.ds(pl.multiple_of(mine * r, 16), r), :]
            copies.append((src, dst, (x, y, 1 - c)))
            copies += [(src, dst, (px, py, c)) for (px, py) in _other_chips(x, y)]
        return copies
    return plan


def _late_reduce_plan(col_kind):
    def plan(src_refs, land_refs):
        x, y, c = _place()
        copies = []
        for a, (src, land) in enumerate(zip(src_refs, land_refs)):
            for j, (px, py) in enumerate(_other_chips(x, y)):
                if col_kind[a] is None:
                    piece = src
                elif col_kind[a]:
                    cs = land.shape[2]
                    piece = src.at[:, pl.ds(pl.multiple_of((2 * px + py) * cs, 128), cs)]
                else:
                    piece = src.at[2 * px + py]
                copies.append((piece, land.at[j], (px, py, c)))
        return copies
    return plan


def _pair_swap(name, halves):
    n = len(halves)

    def body(*refs):
        ins, outs = refs[:n], refs[n:2 * n]
        send_sems, recv_sems = refs[2 * n:]
        x, y, c = _place()
        copies = []
        for a in range(n):
            cp = pltpu.make_async_remote_copy(
                src_ref=ins[a], dst_ref=outs[a], send_sem=send_sems.at[a], recv_sem=recv_sems.at[a],
                device_id=(x, y, 1 - c), device_id_type=MESH)
            cp.start()
            copies.append(cp)
        for cp in copies:
            cp.wait()

    return pl.pallas_call(
        body, name=name, in_specs=[ANY] * n, out_specs=[ANY] * n,
        out_shape=[jax.ShapeDtypeStruct(s.shape, s.dtype) for s in halves],
        scratch_shapes=[pltpu.SemaphoreType.DMA((n,)), pltpu.SemaphoreType.DMA((n,))],
        compiler_params=pltpu.CompilerParams(has_side_effects=True),
    )(*halves)


def _chip_sum(name, chip_sel, own, col, others):
    _, r, c = others.shape
    tr = _pick(r, (256, 128, 64, 32, 16))
    if col:
        own_spec = pl.BlockSpec((tr, c), lambda i, s: (i, s[0]))
    else:
        own_spec = pl.BlockSpec((None, tr, c), lambda i, s: (s[0], i, 0))
    specs = [own_spec] + [pl.BlockSpec((None, tr, c), lambda i, s, k=k: (k, i, 0)) for k in range(3)]

    def body(s_ref, own_ref, r0, r1, r2, o_ref):
        total = ((own_ref[...].astype(F32) + r0[...].astype(F32)) + r1[...].astype(F32)) + r2[...].astype(F32)
        o_ref[...] = total.astype(o_ref.dtype)

    return pl.pallas_call(
        body, name=name,
        grid_spec=pltpu.PrefetchScalarGridSpec(
            num_scalar_prefetch=1, grid=(r // tr,), in_specs=specs,
            out_specs=pl.BlockSpec((tr, c), lambda i, s: (i, 0))),
        out_shape=jax.ShapeDtypeStruct((r, c), BF16),
        compiler_params=_params(("parallel",)),
    )(chip_sel, own, others, others, others)


def _small_layout(vals):
    sizes = [int(math.prod(v.shape)) for v in vals]
    padded = [-(-s // 128) * 128 for s in sizes]
    return sizes, padded, -(-sum(padded) // 1024) * 1024


def _pack_small(vals):
    sizes, padded, total = _small_layout(vals)
    flat = [jnp.pad(v.reshape(-1), (0, p - s)) for v, s, p in zip(vals, sizes, padded)]
    flat.append(jnp.zeros((total - sum(padded),), F32))
    return jnp.concatenate(flat).reshape(total // 128, 128)


def _allreduce_small(own, others, vals):
    def body(own_ref, oth_ref, out_ref, land, send_sem, recv_sem):
        x, y, c = _place()
        out_ref[...] = (own_ref[...] + oth_ref[0]) + (oth_ref[1] + oth_ref[2])
        cp = pltpu.make_async_remote_copy(
            src_ref=out_ref, dst_ref=land, send_sem=send_sem.at[0], recv_sem=recv_sem.at[0],
            device_id=(x, y, 1 - c), device_id_type=MESH)
        cp.start()
        cp.wait()
        out_ref[...] = out_ref[...] + land[...]

    vm = pl.BlockSpec(memory_space=pltpu.VMEM)
    summed = pl.pallas_call(
        body, name="allreduce_small", in_specs=[vm, vm], out_specs=vm,
        out_shape=jax.ShapeDtypeStruct(own.shape, F32),
        scratch_shapes=[pltpu.VMEM(own.shape, F32), pltpu.SemaphoreType.DMA((1,)), pltpu.SemaphoreType.DMA((1,))],
        compiler_params=pltpu.CompilerParams(has_side_effects=True, vmem_limit_bytes=VMEM_LIMIT_BYTES),
    )(own, others).reshape(-1)
    sizes, padded, _ = _small_layout(vals)
    outs, off = [], 0
    for v, s, p in zip(vals, sizes, padded):
        outs.append(summed[off:off + s].reshape(v.shape))
        off += p
    return outs


def _adamw_math(w, g, m, v):
    m2 = ADAM_B1 * m + (1.0 - ADAM_B1) * g
    v2 = ADAM_B2 * v + (1.0 - ADAM_B2) * (g * g)
    m_hat = m2 / (1.0 - ADAM_B1 ** ADAM_STEP)
    v_hat = v2 / (1.0 - ADAM_B2 ** ADAM_STEP)
    delta = -ADAM_LR * (m_hat / (jnp.sqrt(v_hat) + ADAM_EPS) + ADAM_WD * w)
    return delta, m2, v2


def _adamw_big(name, w, g_mine, g_sibling, m, v):
    _, r, c = w.shape

    def body(w_ref, ga_ref, gb_ref, m_ref, v_ref, go_ref, d_ref, mo_ref, vo_ref):
        gv = ga_ref[...].astype(F32) + gb_ref[...].astype(F32)
        d, m2, v2 = _adamw_math(w_ref[...], gv, m_ref[...], v_ref[...])
        go_ref[...] = gv
        d_ref[...] = d
        mo_ref[...] = m2
        vo_ref[...] = v2

    tr = _pick(r, (256, 128, 64, 32, 16, 8))
    if r % tr == 0 and tr % 8 == 0:
        grid = (r // tr,)
        blk = pl.BlockSpec((None, tr, c), lambda i: (0, i, 0))
        part = pl.BlockSpec((tr, c), lambda i: (i, 0))
    else:
        grid = (c // 512,)
        blk = pl.BlockSpec((None, r, 512), lambda i: (0, 0, i))
        part = pl.BlockSpec((r, 512), lambda i: (0, i))
    return pl.pallas_call(
        body, name=name, grid=grid, in_specs=[blk, part, part, blk, blk], out_specs=[blk] * 4,
        out_shape=[jax.ShapeDtypeStruct((1, r, c), F32)] * 4, compiler_params=_params(("parallel",)),
    )(w, g_mine, g_sibling, m, v)


def _adamw_small(ws, gs, ms, vs):
    n = len(ws)

    def body(*refs):
        w_r, g_r, m_r, v_r = refs[:n], refs[n:2 * n], refs[2 * n:3 * n], refs[3 * n:4 * n]
        o = refs[4 * n:]
        for a in range(n):
            gv = g_r[a][...]
            d, m2, v2 = _adamw_math(w_r[a][...], gv, m_r[a][...], v_r[a][...])
            o[a][...] = gv
            o[n + a][...] = d
            o[2 * n + a][...] = m2
            o[3 * n + a][...] = v2

    res = pl.pallas_call(
        body, name="adamw_small", out_shape=[jax.ShapeDtypeStruct(w.shape, F32) for _ in range(4) for w in ws],
        compiler_params=_params(),
    )(*ws, *gs, *ms, *vs)
    return res[:n], res[n:2 * n], res[2 * n:3 * n], res[3 * n:]


def _full_from_gathered(name, gathered):
    if name == "w_in":
        rows = gathered.shape[0] // 4
        return gathered.reshape(4, rows, gathered.shape[1]).transpose(1, 0, 2).reshape(rows, 4 * gathered.shape[1])
    return gathered


def _reduce_layout(name, full):
    if name in COL_KIND:
        return full
    if name == "w_in":
        rows, cols = full.shape
        return full.reshape(rows, 4, cols // 4).transpose(1, 0, 2)
    return full.reshape(4, full.shape[0] // 4, full.shape[1])


def kernel(x, mem, norm_mix, w_in, fox_q_norm, fox_k_norm, fox_f_bias, s5_a_re, s5_a_im, s5_log_dt, s5_b_re, s5_b_im, s5_c_re, s5_c_im, s5_d, s5_w_glu, s5_b_glu, out_norm_fox, out_norm_s5, w_out, norm_cross, norm_mem, w_xq, w_xkv, xq_norm, xk_norm, w_xo, norm_ffn, w_ffn_up, ffn_conv_w, ffn_conv_b, w_ffn_down, loss_target, m_norm_mix, m_w_in, m_fox_q_norm, m_fox_k_norm, m_fox_f_bias, m_s5_a_re, m_s5_a_im, m_s5_log_dt, m_s5_b_re, m_s5_b_im, m_s5_c_re, m_s5_c_im, m_s5_d, m_s5_w_glu, m_s5_b_glu, m_out_norm_fox, m_out_norm_s5, m_w_out, m_norm_cross, m_norm_mem, m_w_xq, m_w_xkv, m_xq_norm, m_xk_norm, m_w_xo, m_norm_ffn, m_w_ffn_up, m_ffn_conv_w, m_ffn_conv_b, m_w_ffn_down, v_norm_mix, v_w_in, v_fox_q_norm, v_fox_k_norm, v_fox_f_bias, v_s5_a_re, v_s5_a_im, v_s5_log_dt, v_s5_b_re, v_s5_b_im, v_s5_c_re, v_s5_c_im, v_s5_d, v_s5_w_glu, v_s5_b_glu, v_out_norm_fox, v_out_norm_s5, v_w_out, v_norm_cross, v_norm_mem, v_w_xq, v_w_xkv, v_xq_norm, v_xk_norm, v_w_xo, v_norm_ffn, v_w_ffn_up, v_ffn_conv_w, v_ffn_conv_b, v_w_ffn_down):
    given = dict(locals())
    w = {n: given[n] for n in WEIGHTS}
    m = {n: given["m_" + n] for n in WEIGHTS}
    v = {n: given["v_" + n] for n in WEIGHTS}
    xi, yi, _ = _place()
    chip = (2 * xi + yi).astype(jnp.int32)
    chip_sel = chip.reshape(1)

    first_shard, taps = w[FIRST_WEIGHT][0].astype(BF16), w["ffn_conv_w"][0]
    send, recv, srcs, lands, g_started = _split_start(
        "gather_first_start", [first_shard, taps],
        [lax.empty((4 * first_shard.shape[0], first_shard.shape[1]), BF16), lax.empty((4,) + taps.shape, F32)],
        8, _first_gather_plan)
    pending = {"first": ((FIRST_WEIGHT, "ffn_conv_w"), _first_gather_plan, send, recv, srcs, lands)}
    for stage, names in (("mid", MID_WEIGHTS), ("late", LATE_WEIGHTS)):
        kinds = [n in COL_KIND for n in names]
        shards = [w[n][0].astype(BF16) for n in names]
        shards[0] = shards[0] + g_started[0:1, 0:1].astype(BF16)
        full = [lax.empty((s.shape[0], 4 * s.shape[1]) if ck else (4 * s.shape[0], s.shape[1]), BF16)
                for s, ck in zip(shards, kinds)]
        plan = _late_gather_plan(kinds)
        send, recv, srcs, lands, g_started = _split_start(
            "gather_" + stage + "_start", shards, full, 4 * len(names), plan)
        pending[stage] = (names, plan, send, recv, srcs, lands)

    def late_weights(stage, after):
        names, plan, send, recv, srcs, lands = pending[stage]
        _, full = _split_wait("gather_" + stage + "_wait", send, recv, srcs, lands, after, plan)
        if stage == "first":
            full = [_full_from_gathered(FIRST_WEIGHT, _forward_to_sibling(full[0])),
                    full[1].transpose(1, 0, 2).reshape(3, D_FF)]
        return dict(zip(names, full))

    reducing = {}

    def start_reduce(stage, grads_by_name, whole=()):
        names = list(grads_by_name)
        kinds = [n in COL_KIND for n in names]
        grads = [_reduce_layout(n, grads_by_name[n].astype(BF16)) for n in names]
        lands = [lax.empty((3, s.shape[0], s.shape[1] // 4) if ck else (3,) + s.shape[1:], BF16)
                 for s, ck in zip(grads, kinds)]
        plan = _late_reduce_plan(kinds + [None] * len(whole))
        send, recv, srcs, lands, started = _split_start(
            "reduce_" + stage + "_start", grads + list(whole),
            lands + [lax.empty((3,) + a.shape, a.dtype) for a in whole], 3 * (len(names) + len(whole)), plan)
        reducing[stage] = (names, kinds, plan, send, recv, srcs, lands)
        return started

    p = {n: w[n][0] for n in SMALL}
    for n in ("norm_mix", "fox_q_norm", "fox_k_norm", "fox_f_bias", "s5_b_glu", "out_norm_fox", "out_norm_s5",
              "norm_cross", "norm_mem", "xq_norm", "xk_norm", "norm_ffn", "ffn_conv_b"):
        p[n] = p[n].reshape(1, -1)
    p["norm_mix"] = p["norm_mix"] + g_started[0:1, 0:1]
    loss, grad_x, g = _local_step(x, mem, loss_target, p, {}, late_weights, start_reduce)

    small_names = list(SMALL) + ["ffn_conv_w"]
    small_vals = [g[n].reshape(w[n].shape if n != "ffn_conv_w" else (1, 3, D_FF)) for n in small_names] + [loss]
    after = start_reduce("first", {FIRST_WEIGHT: g[FIRST_WEIGHT]}, whole=[_pack_small(small_vals)])

    out_g, out_d, out_m, out_v = {}, {}, {}, {}

    def finish(stage, after):
        names, kinds, plan, send, recv, srcs, lands = reducing[stage]
        sums, from_chips = _split_wait("reduce_" + stage + "_wait", send, recv, srcs, lands, after, plan)
        mine = [_chip_sum("reduce_chip_sum_" + n, chip_sel, ps, ck, fc)
                for n, ps, fc, ck in zip(names, sums, from_chips, kinds)]
        theirs = _pair_swap("reduce_pair_swap_" + stage, mine)
        for n, a, b in zip(names, mine, theirs):
            if n == "w_in":
                flip = lambda t: jnp.swapaxes(t, -1, -2)
                res = _adamw_big("adamw_" + n, flip(w[n]), flip(a), flip(b), flip(m[n]), flip(v[n]))
                out_g[n], out_d[n], out_m[n], out_v[n] = (flip(t) for t in res)
                continue
            out_g[n], out_d[n], out_m[n], out_v[n] = _adamw_big("adamw_" + n, w[n], a, b, m[n], v[n])
        return sums[len(names):], from_chips[len(names):], out_v[names[-1]]

    _, _, after = finish("late", after)
    _, _, after = finish("mid", after)
    (small_own,), (small_others,), _ = finish("first", after)

    reduced = _allreduce_small(small_own, small_others, small_vals)
    loss_all = reduced[-1].reshape(())
    conv_w_grad = lax.dynamic_slice_in_dim(reduced[-2], chip * (D_FF // 4), D_FF // 4, axis=2)
    sg, sd, sm, sv = _adamw_small(
        [w[n] for n in small_names], list(reduced[:len(SMALL)]) + [conv_w_grad],
        [m[n] for n in small_names], [v[n] for n in small_names])
    out_g.update(zip(small_names, sg))
    out_d.update(zip(small_names, sd))
    out_m.update(zip(small_names, sm))
    out_v.update(zip(small_names, sv))

    return (loss_all, grad_x, *[out_g[n] for n in WEIGHTS], *[out_d[n] for n in WEIGHTS],
            *[out_m[n] for n in WEIGHTS], *[out_v[n] for n in WEIGHTS])
```

```python
import math

import jax
import jax.numpy as jnp
from jax import lax
from jax.experimental import pallas as pl
from jax.experimental.pallas import tpu as pltpu

F32 = jnp.float32
BF16 = jnp.bfloat16

D_MODEL = 1024
FOX_WIDTH = 512
HEAD_DIM = 64
N_FOX_HEADS = 8
S5_WIDTH = 512
S5_GROUP_CH = 16
S5_GROUPS = 32
S5_STATE = 64
S5_CH = S5_GROUPS * S5_STATE
N_X_HEADS = 4
X_HEAD_DIM = 256
N_MEM = 256
D_FF = 2816
UF_COLS = 640
EPS = 1e-6
ADAM_LR = 0.001
ADAM_B1 = 0.9
ADAM_B2 = 0.999
ADAM_EPS = 1e-08
ADAM_WD = 0.01
ADAM_STEP = 10

VMEM_LIMIT_BYTES = 56 * 1024 * 1024
MM_BLOCK_BYTES = 6 * 1024 * 1024
MM_VMEM_BYTES = 40 * 1024 * 1024
MM_TILE_MAX = 1536
MESH = pl.DeviceIdType.MESH

FIRST_WEIGHT = "w_in"
MID_WEIGHTS = ("s5_w_glu", "w_out")
EARLY_WEIGHTS = (FIRST_WEIGHT,) + MID_WEIGHTS
LATE_WEIGHTS = ("w_xq", "w_xkv", "w_xo", "w_ffn_up", "w_ffn_down")
BIG = EARLY_WEIGHTS + LATE_WEIGHTS
COL_KIND = ("w_xkv", "w_ffn_up")
SMALL = ("norm_mix", "fox_q_norm", "fox_k_norm", "fox_f_bias", "s5_a_re", "s5_a_im", "s5_log_dt",
         "s5_b_re", "s5_b_im", "s5_c_re", "s5_c_im", "s5_d", "s5_b_glu", "out_norm_fox", "out_norm_s5",
         "norm_cross", "norm_mem", "xq_norm", "xk_norm", "norm_ffn", "ffn_conv_b")
WEIGHTS = ("norm_mix", "w_in", "fox_q_norm", "fox_k_norm", "fox_f_bias", "s5_a_re", "s5_a_im", "s5_log_dt",
           "s5_b_re", "s5_b_im", "s5_c_re", "s5_c_im", "s5_d", "s5_w_glu", "s5_b_glu", "out_norm_fox",
           "out_norm_s5", "w_out", "norm_cross", "norm_mem", "w_xq", "w_xkv", "xq_norm", "xk_norm", "w_xo",
           "norm_ffn", "w_ffn_up", "ffn_conv_w", "ffn_conv_b", "w_ffn_down")


def _params(sem=None):
    return pltpu.CompilerParams(dimension_semantics=sem, vmem_limit_bytes=VMEM_LIMIT_BYTES)


def _pick(n, cands):
    for c in cands:
        if n % c == 0:
            return c
    return n


_DIMS = {"nn": (((1,), (0,)), ((), ())), "nt": (((1,), (1,)), ((), ())), "tn": (((0,), (0,)), ((), ()))}


def _mm(a, b, mode, name, out_dtype=F32, res=None):
    if mode == "nn":
        (m, k), (k2, n) = a.shape, b.shape
    elif mode == "nt":
        (m, k), (n, k2) = a.shape, b.shape
    else:
        (k, m), (k2, n) = a.shape, b.shape
    assert k == k2, (name, a.shape, b.shape)

    has_res = res is not None
    a_size, b_size = a.dtype.itemsize, b.dtype.itemsize
    o_size = jnp.dtype(out_dtype).itemsize + (res.dtype.itemsize if has_res else 0)

    def tiles(dim):
        return [c for c in range(MM_TILE_MAX, 0, -128) if dim % c == 0] or [dim]

    best = None
    for tm in tiles(m):
        for tn in tiles(n):
            a_blk, b_blk = tm * k * a_size, tn * k * b_size
            if max(a_blk, b_blk) > MM_BLOCK_BYTES or 2 * (a_blk + b_blk + tm * tn * o_size) > MM_VMEM_BYTES:
                continue
            for rows_outer in (True, False):
                moved = (m * k * a_size + (m // tm) * n * k * b_size) if rows_outer else \
                        (n * k * b_size + (n // tn) * m * k * a_size)
                key = (moved, -(tm * tn))
                if best is None or key < best[0]:
                    best = (key, tm, tn, rows_outer)
    assert best is not None, (name, a.shape, b.shape)
    _, tm, tn, rows_outer = best
    ij = (lambda g0, g1: (g0, g1)) if rows_outer else (lambda g0, g1: (g1, g0))
    if mode == "tn":
        a_spec = pl.BlockSpec((k, tm), lambda g0, g1: (0, ij(g0, g1)[0]))
    else:
        a_spec = pl.BlockSpec((tm, k), lambda g0, g1: (ij(g0, g1)[0], 0))
    if mode == "nt":
        b_spec = pl.BlockSpec((tn, k), lambda g0, g1: (ij(g0, g1)[1], 0))
    else:
        b_spec = pl.BlockSpec((k, tn), lambda g0, g1: (0, ij(g0, g1)[1]))
    o_spec = pl.BlockSpec((tm, tn), lambda g0, g1: ij(g0, g1))
    grid = (m // tm, n // tn) if rows_outer else (n // tn, m // tm)
    dims = _DIMS[mode]

    def body(*refs):
        a_ref, b_ref = refs[0], refs[1]
        o_ref = refs[-1]
        acc = lax.dot_general(a_ref[...].astype(BF16), b_ref[...].astype(BF16), dims, preferred_element_type=F32)
        if has_res:
            acc = acc + refs[2][...].astype(F32)
        o_ref[...] = acc.astype(o_ref.dtype)

    return pl.pallas_call(
        body, name=name, grid=grid,
        in_specs=[a_spec, b_spec] + ([o_spec] if has_res else []),
        out_specs=o_spec, out_shape=jax.ShapeDtypeStruct((m, n), out_dtype),
        compiler_params=_params(("parallel", "parallel")),
    )(*((a, b, res) if has_res else (a, b)))


def _row_spec(tm, bc, off, step):
    return pl.BlockSpec((tm, bc), lambda i, h: (i, off + step * h))


ROW_TILE_ELEMS = 512 * 1024


def _row_tile(t, rows):
    widest = max(bc for (_, bc, _, _) in rows)
    return _pick(t, (min(t, ROW_TILE_ELEMS // widest), 512, 256, 128, 64, 8))


def _rowwise(fn, rows, pars, outs, name, heads=1):
    t = rows[0][0].shape[0]
    tm = _row_tile(t, rows)
    nr, npar = len(rows), len(pars)

    def body(*refs):
        vals = [r[...].astype(F32) for r in refs[:nr + npar]]
        res = fn(*vals)
        if not isinstance(res, (tuple, list)):
            res = (res,)
        for o_ref, v in zip(refs[nr + npar:], res):
            o_ref[...] = v.astype(o_ref.dtype)

    in_specs = [_row_spec(tm, bc, off, st) for (_, bc, off, st) in rows]
    in_specs += [pl.BlockSpec(p.shape, lambda i, h: (0, 0)) for p in pars]
    out_specs = [_row_spec(tm, bc, 0, st) for (_, bc, st, _) in outs]
    out_shape = [jax.ShapeDtypeStruct((t, c), dt) for (c, _, _, dt) in outs]
    res = pl.pallas_call(
        body, name=name, grid=(t // tm, heads), in_specs=in_specs, out_specs=out_specs, out_shape=out_shape,
        compiler_params=_params(("parallel", "parallel")),
    )(*[r[0] for r in rows], *pars)
    return res[0] if len(res) == 1 else res


def _rowwise_vjp(fn, rows, pars, cts, name, heads=1, adds=None, row_dtypes=None):
    t = rows[0][0].shape[0]
    tm = _row_tile(t, rows)
    nr, npar, nct = len(rows), len(pars), len(cts)
    adds = adds or [None] * nr
    add_list = [a for a in adds if a is not None]
    row_dtypes = row_dtypes or [F32] * nr

    def body(*refs):
        i, h = pl.program_id(0), pl.program_id(1)
        p = 0
        row_v = [r[...].astype(F32) for r in refs[p:p + nr]]; p += nr
        par_v = [r[...].astype(F32) for r in refs[p:p + npar]]; p += npar
        ct_v = [r[...].astype(F32) for r in refs[p:p + nct]]; p += nct
        add_refs = refs[p:p + len(add_list)]; p += len(add_list)
        drow_refs = refs[p:p + nr]; p += nr
        dpar_refs = refs[p:p + npar]

        def wrapped(*a):
            r = fn(*a)
            return tuple(r) if isinstance(r, (tuple, list)) else (r,)

        _, pull = jax.vjp(wrapped, *row_v, *par_v)
        grads = pull(tuple(ct_v))
        ai = 0
        for k in range(nr):
            g = grads[k]
            if adds[k] is not None:
                g = g + add_refs[ai][...].astype(F32)
                ai += 1
            drow_refs[k][...] = g.astype(drow_refs[k].dtype)

        @pl.when((i == 0) & (h == 0))
        def _():
            for r in dpar_refs:
                r[...] = jnp.zeros(r.shape, r.dtype)

        for k in range(npar):
            dpar_refs[k][...] += grads[nr + k]

    in_specs = [_row_spec(tm, bc, off, st) for (_, bc, off, st) in rows]
    in_specs += [pl.BlockSpec(q.shape, lambda i, h: (0, 0)) for q in pars]
    in_specs += [_row_spec(tm, bc, off, st) for (_, bc, off, st) in cts]
    in_specs += [_row_spec(tm, bc, off, st) for (_, bc, off, st) in add_list]
    out_specs = [_row_spec(tm, bc, 0, st) for (_, bc, _, st) in rows]
    out_specs += [pl.BlockSpec(q.shape, lambda i, h: (0, 0)) for q in pars]
    out_shape = [jax.ShapeDtypeStruct((t, bc * (heads if st else 1)), dt) for (_, bc, _, st), dt in zip(rows, row_dtypes)]
    out_shape += [jax.ShapeDtypeStruct(q.shape, F32) for q in pars]
    res = pl.pallas_call(
        body, name=name, grid=(t // tm, heads), in_specs=in_specs, out_specs=out_specs, out_shape=out_shape,
        compiler_params=_params(("arbitrary", "arbitrary")),
    )(*[r[0] for r in rows], *pars, *[c[0] for c in cts], *[a[0] for a in add_list])
    return list(res[:nr]), list(res[nr:])


def _rms(x, g):
    return x * lax.rsqrt(jnp.mean(x * x, axis=-1, keepdims=True) + EPS) * g


def _rms_pair(x, g):
    left = lax.broadcasted_iota(jnp.int32, x.shape, 1) < HEAD_DIM
    x2 = x * x
    ms_a = jnp.sum(jnp.where(left, x2, 0.0), axis=-1, keepdims=True) * (1.0 / HEAD_DIM)
    ms_b = jnp.sum(jnp.where(left, 0.0, x2), axis=-1, keepdims=True) * (1.0 / HEAD_DIM)
    return x * lax.rsqrt(jnp.where(left, ms_a, ms_b) + EPS) * g


def _gelu(x):
    return 0.5 * x * (1.0 + jnp.tanh(math.sqrt(2.0 / math.pi) * (x + 0.044715 * (x * x * x))))


def _s5_act(ys, u, d):
    return _gelu(ys + d * u)


def _s5_gate(yg, z, b, g):
    return _rms(yg * jax.nn.sigmoid(z + b), g)


def _lane_cumsum(x, reverse):
    n = x.shape[-1]
    lane = lax.broadcasted_iota(jnp.int32, x.shape, 1)
    k = 1
    while k < n:
        if reverse:
            x = x + jnp.where(lane < n - k, pltpu.roll(x, n - k, 1), 0.0)
        else:
            x = x + jnp.where(lane >= k, pltpu.roll(x, k, 1), 0.0)
        k *= 2
    return x


def _log_sigmoid(z):
    return jnp.minimum(z, 0.0) - jnp.log(1.0 + jnp.exp(-jnp.abs(z)))


def _forget_fwd(f, bias):
    def body(f_ref, b_ref, c_ref):
        c_ref[...] = _lane_cumsum(_log_sigmoid(f_ref[...] + b_ref[...]), False)

    return pl.pallas_call(body, name="forget_fwd", out_shape=jax.ShapeDtypeStruct(f.shape, F32),
                          compiler_params=_params())(f, bias)


def _forget_bwd(f, bias, dc):
    def body(f_ref, b_ref, dc_ref, df_ref, db_ref):
        dlog = _lane_cumsum(dc_ref[...], True)
        df = dlog * jax.nn.sigmoid(-(f_ref[...] + b_ref[...]))
        df_ref[...] = df
        db_ref[...] = jnp.sum(df, axis=1, keepdims=True)

    return pl.pallas_call(body, name="forget_bwd",
                          out_shape=(jax.ShapeDtypeStruct(f.shape, F32), jax.ShapeDtypeStruct(bias.shape, F32)),
                          compiler_params=_params())(f, bias, dc)


FOX_BLOCK = 1024
FOX_KEYS = 1024
FOX_BWD_BLOCK = 512
_NT = _DIMS["nt"]
_TN = _DIMS["tn"]


N_PAIRS = N_FOX_HEADS // 2
V_BLOCK0 = 2 * N_PAIRS


def _left_lanes(shape):
    return lax.broadcasted_iota(jnp.int32, shape, 1) < HEAD_DIM


def _top_rows(shape):
    return lax.broadcasted_iota(jnp.int32, shape, 0) < HEAD_DIM


def _wide(c_tile, n):
    return c_tile if n == 128 else jnp.concatenate([c_tile] * (n // 128), axis=1)


def _fox_fwd(qn, kn, qkv, c_wide, seqs):
    t = qn.shape[0]
    l = t // seqs
    tb = min(FOX_BLOCK, l)
    tk = min(FOX_KEYS, tb)
    ratio = tb // tk
    nb = l // tb
    scale = HEAD_DIM ** -0.5

    def body(q_ref, k_ref, v_ref, ca_ref, cb_ref, o_ref, lse_ref, vt_ref):
        i = pl.program_id(2)
        top = _top_rows((128, tb))

        @pl.when(i == 0)
        def _():
            vt_ref[...] = v_ref[...].T.astype(BF16)

        qt = (q_ref[...].astype(F32) * scale).T.astype(BF16)
        zero = jnp.zeros_like(qt)
        qts = (jnp.where(top, qt, zero), jnp.where(top, zero, qt))
        top_k = _top_rows((128, tk))
        zero_k = jnp.zeros((128, tk), BF16)
        key_pos = lax.broadcasted_iota(jnp.int32, (tk, tb), 0)
        query_pos = lax.broadcasted_iota(jnp.int32, (tk, tb), 1)
        c_refs = (ca_ref, cb_ref)

        def scores(j):
            off = pl.multiple_of(j * tk, tk)
            k2 = k_ref[pl.ds(off, tk), :]
            return tuple(jnp.dot(k2, qts[h], preferred_element_type=F32) - _wide(c_refs[h][pl.ds(off, tk), :], tb)
                         for h in (0, 1))

        def values_times(ps, j):
            vt = vt_ref[:, pl.ds(pl.multiple_of(j * tk, tk), tk)]
            return (jnp.dot(jnp.where(top_k, vt, zero_k), ps[0], preferred_element_type=F32)
                    + jnp.dot(jnp.where(top_k, zero_k, vt), ps[1], preferred_element_type=F32))

        def softmax_step(sts, stats, first_key):
            ps, new, alphas = [], [], []
            for st, (m, s_sum) in zip(sts, stats):
                if first_key is not None:
                    st = jnp.where(key_pos + first_key <= query_pos, st, -jnp.inf)
                m_new = jnp.maximum(m, jnp.max(st, axis=0, keepdims=True))
                alpha = jnp.exp(m - m_new)
                p = jnp.exp(st - m_new)
                new.append((m_new, alpha * s_sum + jnp.sum(p, axis=0, keepdims=True)))
                alphas.append(alpha)
                ps.append(p.astype(BF16))
            return tuple(ps), tuple(new), jnp.where(top, alphas[0], alphas[1])

        def tile(j, carry, first_key):
            stats, acc = carry
            ps, stats, alpha = softmax_step(scores(j), stats, first_key)
            return stats, alpha * acc + values_times(ps, j)

        stat = (jnp.full((1, tb), -jnp.inf, F32), jnp.zeros((1, tb), F32))
        below = i * ratio
        carry = lax.fori_loop(0, below, lambda j, c: tile(j, c, None), ((stat, stat), jnp.zeros((128, tb), F32)))
        for r in range(ratio):
            carry = tile(below + r, carry, r * tk)
        ((ma, sa), (mb, sb)), acc = carry
        o_ref[...] = (acc / jnp.where(top, sa, sb)).T
        lse_ref[0:1, :] = ma + jnp.log(sa)
        lse_ref[1:2, :] = mb + jnp.log(sb)

    qblk = pl.BlockSpec((tb, 128), lambda b, hp, i: (b * nb + i, hp))
    return pl.pallas_call(
        body, name="fox_fwd", grid=(seqs, N_PAIRS, nb),
        in_specs=[qblk, pl.BlockSpec((l, 128), lambda b, hp, i: (b, hp)),
                  pl.BlockSpec((l, 128), lambda b, hp, i: (b, V_BLOCK0 + hp)),
                  pl.BlockSpec((None, l, 128), lambda b, hp, i: (b * N_FOX_HEADS + 2 * hp, 0, 0)),
                  pl.BlockSpec((None, l, 128), lambda b, hp, i: (b * N_FOX_HEADS + 2 * hp + 1, 0, 0))],
        out_specs=[qblk, pl.BlockSpec((None, 2, tb), lambda b, hp, i: (b * N_PAIRS + hp, 0, i))],
        out_shape=[jax.ShapeDtypeStruct((t, FOX_WIDTH), F32), jax.ShapeDtypeStruct((seqs * N_PAIRS, 2, l), F32)],
        scratch_shapes=[pltpu.VMEM((128, l), BF16)],
        compiler_params=_params(("parallel", "parallel", "arbitrary")),
    )(qn, kn, qkv, c_wide, c_wide)


def _fox_bwd(qn, kn, qkv, c_wide, o, do, lse, seqs):
    t = qn.shape[0]
    l = t // seqs
    tb = min(FOX_BWD_BLOCK, l)
    nb = l // tb
    scale = HEAD_DIM ** -0.5
    one_at = (HEAD_DIM, 0)

    def body(q_ref, k_ref, v_ref, ca_ref, cb_ref, o_ref, do_ref, lse_ref, dq_ref, dk_ref, dv_ref, dc_ref, dcq_ref,
             qt_ref, kt_ref, dot_ref, delta_ref, dqa_ref, dqb_ref):
        top_l = _top_rows((128, l))
        top = _top_rows((128, tb))
        left = _left_lanes((tb, 128))
        row_id = lax.broadcasted_iota(jnp.int32, (128, tb), 0)
        lane_id = lax.broadcasted_iota(jnp.int32, (tb, 128), 1)
        zero_t = jnp.zeros((128, tb), BF16)
        zero_l = jnp.zeros((tb, 128), BF16)
        rows = lambda a: (jnp.where(top, a, zero_t), jnp.where(top, zero_t, a))
        lanes = lambda a: (jnp.where(left, a, zero_l), jnp.where(left, zero_l, a))
        with_one_row = lambda pair: tuple(jnp.where(row_id == one_at[h], 1.0, pair[h]).astype(BF16) for h in (0, 1))
        with_one_lane = lambda pair: tuple(jnp.where(lane_id == one_at[h], 1.0, pair[h]).astype(BF16) for h in (0, 1))
        causal = lax.broadcasted_iota(jnp.int32, (tb, tb), 0) <= lax.broadcasted_iota(jnp.int32, (tb, tb), 1)
        c_refs = (ca_ref, cb_ref)
        dq_refs = (dqa_ref, dqb_ref)

        qt_ref[...] = (q_ref[...].astype(F32) * scale).T.astype(BF16)
        kt_ref[...] = k_ref[...].astype(F32).T.astype(BF16)
        do_t = do_ref[...].T
        dot_ref[...] = do_t.astype(BF16)
        prod_t = do_t * o_ref[...].T
        delta_ref[0:1, :] = jnp.sum(jnp.where(top_l, prod_t, 0.0), axis=0, keepdims=True)
        delta_ref[1:2, :] = jnp.sum(jnp.where(top_l, 0.0, prod_t), axis=0, keepdims=True)
        dqa_ref[...] = jnp.zeros(dqa_ref.shape, F32)
        dqb_ref[...] = jnp.zeros(dqb_ref.shape, F32)

        def kv_block(j, _):
            koff = pl.multiple_of(j * tb, tb)
            k2 = k_ref[pl.ds(koff, tb), :]
            v2 = v_ref[pl.ds(koff, tb), :].astype(BF16)
            kts = with_one_row(rows(kt_ref[:, pl.ds(koff, tb)]))
            cw = tuple(_wide(c_refs[h][pl.ds(koff, tb), :], tb) for h in (0, 1))

            def q_block(i, carry, masked):
                dks, dv = list(carry[:2]), carry[2]
                qoff = pl.multiple_of(i * tb, tb)
                qs = lanes((q_ref[pl.ds(qoff, tb), :].astype(F32) * scale).astype(BF16))
                qs_one = with_one_lane(qs)
                dos = lanes(do_ref[pl.ds(qoff, tb), :].astype(BF16))
                qts = rows(qt_ref[:, pl.ds(qoff, tb)])
                dots = rows(dot_ref[:, pl.ds(qoff, tb)])
                for h in (0, 1):
                    st = jnp.dot(k2, qts[h], preferred_element_type=F32) - cw[h]
                    p = jnp.exp(st - lse_ref[h:h + 1, pl.ds(qoff, tb)])
                    if masked:
                        p = jnp.where(causal, p, 0.0)
                    dp = jnp.dot(v2, dots[h], preferred_element_type=F32)
                    dsb = (p * (dp - delta_ref[h:h + 1, pl.ds(qoff, tb)])).astype(BF16)
                    dv = dv + jnp.dot(p.astype(BF16), dos[h], preferred_element_type=F32)
                    dks[h] = dks[h] + jnp.dot(dsb, qs_one[h], preferred_element_type=F32)
                    dq_refs[h][:, pl.ds(qoff, tb)] += jnp.dot(kts[h], dsb, preferred_element_type=F32)
                return dks[0], dks[1], dv

            z = jnp.zeros((tb, 128), F32)
            carry = q_block(j, (z, z, z), True)
            rest = nb - 1 - j
            carry = lax.fori_loop(
                0, rest // 2, lambda n, c: q_block(j + 2 + 2 * n, q_block(j + 1 + 2 * n, c, False), False), carry)
            dka, dkb, dv = lax.cond(rest % 2 == 1, lambda c: q_block(nb - 1, c, False), lambda c: c, carry)
            dk_ref[pl.ds(koff, tb), :] = jnp.where(left, dka, dkb)
            dv_ref[pl.ds(koff, tb), :] = dv
            dc_ref[0:1, pl.ds(koff, tb)] = -dka.T[one_at[0]:one_at[0] + 1, :]
            dc_ref[1:2, pl.ds(koff, tb)] = -dkb.T[one_at[1]:one_at[1] + 1, :]
            return 0

        lax.fori_loop(0, nb, kv_block, 0)
        dq_ref[...] = (jnp.where(top_l, dqa_ref[...], dqb_ref[...]) * scale).T
        dcq_ref[0:1, :] = dqa_ref[one_at[0]:one_at[0] + 1, :]
        dcq_ref[1:2, :] = dqb_ref[one_at[1]:one_at[1] + 1, :]

    blk = pl.BlockSpec((l, 128), lambda b, hp: (b, hp))
    cspec = lambda k: pl.BlockSpec((None, l, 128), lambda b, hp: (b * N_FOX_HEADS + 2 * hp + k, 0, 0))
    rows2 = pl.BlockSpec((None, 2, l), lambda b, hp: (b * N_PAIRS + hp, 0, 0))
    wide = jax.ShapeDtypeStruct((t, FOX_WIDTH), F32)
    pair_rows = jax.ShapeDtypeStruct((seqs * N_PAIRS, 2, l), F32)
    return pl.pallas_call(
        body, name="fox_bwd", grid=(seqs, N_PAIRS),
        in_specs=[blk, blk, pl.BlockSpec((l, 128), lambda b, hp: (b, V_BLOCK0 + hp)), cspec(0), cspec(1), blk, blk, rows2],
        out_specs=[blk, blk, blk, rows2, rows2],
        out_shape=[wide, wide, wide, pair_rows, pair_rows],
        scratch_shapes=[pltpu.VMEM((128, l), BF16), pltpu.VMEM((128, l), BF16), pltpu.VMEM((128, l), BF16),
                        pltpu.VMEM((2, l), F32), pltpu.VMEM((128, l), F32), pltpu.VMEM((128, l), F32)],
        compiler_params=_params(("parallel", "parallel")),
    )(qn, kn, qkv, c_wide, c_wide, o, do, lse)


SCAN_ROWS = 512
SCAN_COLS = 1024


S5_IN = 128
S5_ST = 512
SCAN_CHUNKS = SCAN_COLS // S5_ST
SCAN_SEGS = 8
LANES = 128


def _cmul(ar, ai, br, bi):
    return ar * br - ai * bi, ar * bi + ai * br


def _powers_into(pw_r, pw_i, a_r, a_i, seg):
    pw_r[0:1, :] = a_r
    pw_i[0:1, :] = a_i
    for k in range(1, seg):
        pr, pi = _cmul(pw_r[k - 1:k, :], pw_i[k - 1:k, :], a_r, a_i)
        pw_r[k:k + 1, :] = pr
        pw_i[k:k + 1, :] = pi


def _interleave(dst, src, seg):
    for h in range(src.shape[0]):
        for j in range(seg):
            dst[h, j * SCAN_SEGS:(j + 1) * SCAN_SEGS, :] = src[h, pl.ds(j, SCAN_SEGS, stride=seg), :]


def _deinterleave(dst, src, seg):
    for h in range(src.shape[0]):
        for j in range(seg):
            dst[h, pl.ds(j, SCAN_SEGS, stride=seg), :] = src[h, j * SCAN_SEGS:(j + 1) * SCAN_SEGS, :]


def _interleaved(ref, tmp_a, tmp_b, seg):
    n = ref.shape[1] // LANES
    for h in range(n):
        tmp_a[h] = ref[:, h * LANES:(h + 1) * LANES].astype(F32)
    _interleave(tmp_b, tmp_a, seg)
    return jnp.concatenate([tmp_b[h] for h in range(n)], axis=1)


def _store_deinterleaved(ref, val, tmp_a, tmp_b, seg):
    n = ref.shape[1] // LANES
    for h in range(n):
        tmp_a[h] = val[:, h * LANES:(h + 1) * LANES]
    _deinterleave(tmp_b, tmp_a, seg)
    for h in range(n):
        ref[:, h * LANES:(h + 1) * LANES] = tmp_b[h]


def _segment_scan(b_r, b_i, x_r, x_i, pw_r, pw_i, car_r, car_i, seg, sign, reverse, visit=None):
    nc = b_r.shape[0]
    sub = lax.broadcasted_iota(jnp.int32, (SCAN_SEGS, LANES), 0)
    lanes = lambda c: slice(c * LANES, (c + 1) * LANES)
    rows = lambda j: pl.ds(pl.multiple_of(((seg - 1 - j) if reverse else j) * SCAN_SEGS, SCAN_SEGS), SCAN_SEGS)
    a1 = [(pw_r[0:1, lanes(c)], sign * pw_i[0:1, lanes(c)]) for c in range(nc)]

    def local(j, xs):
        out = []
        for c in range(nc):
            xr, xi = xs[2 * c], xs[2 * c + 1]
            nr = a1[c][0] * xr - a1[c][1] * xi + b_r[c, rows(j), :]
            ni = a1[c][0] * xi + a1[c][1] * xr + b_i[c, rows(j), :]
            x_r[c, rows(j), :] = nr
            x_i[c, rows(j), :] = ni
            out += [nr, ni]
        return tuple(out)

    zero = jnp.zeros((SCAN_SEGS, LANES), F32)
    ends = lax.fori_loop(0, seg, local, (zero,) * (2 * nc))

    if reverse:
        first = sub == SCAN_SEGS - 1
        neighbour = lambda v: pltpu.roll(v, SCAN_SEGS - 1, 0)
        shift = lambda v, d: jnp.where(sub < SCAN_SEGS - d, pltpu.roll(v, SCAN_SEGS - d, 0), 0.0)
    else:
        first = sub == 0
        neighbour = lambda v: pltpu.roll(v, 1, 0)
        shift = lambda v, d: jnp.where(sub >= d, pltpu.roll(v, d, 0), 0.0)
    last = 0 if reverse else SCAN_SEGS - 1
    entries = []
    for c in range(nc):
        er, ei = ends[2 * c], ends[2 * c + 1]
        pr, pi = pw_r[seg - 1:seg, lanes(c)], sign * pw_i[seg - 1:seg, lanes(c)]
        yr = jnp.where(first, car_r[:, lanes(c)], neighbour(er))
        yi = jnp.where(first, car_i[:, lanes(c)], neighbour(ei))
        qr, qi = pr, pi
        for d in (1, 2, 4):
            mr, mi = _cmul(qr, qi, shift(yr, d), shift(yi, d))
            yr, yi = yr + mr, yi + mi
            qr, qi = _cmul(qr, qi, qr, qi)
        lr, li = _cmul(pr, pi, yr, yi)
        car_r[:, lanes(c)] = (er + lr)[last:last + 1, :]
        car_i[:, lanes(c)] = (ei + li)[last:last + 1, :]
        entries += [yr, yi]

    def correct(j, prev):
        out = []
        row_r, row_i = pw_r[pl.ds(j, 1), :], sign * pw_i[pl.ds(j, 1), :]
        for c in range(nc):
            mr, mi = _cmul(row_r[:, lanes(c)], row_i[:, lanes(c)], entries[2 * c], entries[2 * c + 1])
            nr = x_r[c, rows(j), :] + mr
            ni = x_i[c, rows(j), :] + mi
            x_r[c, rows(j), :] = nr
            x_i[c, rows(j), :] = ni
            if visit is not None:
                visit(c, rows(j), prev[2 * c], prev[2 * c + 1])
            out += [nr, ni]
        return tuple(out)

    lax.fori_loop(0, seg, correct, tuple(entries))


def _s5_fwd(uf, bbr, bbi, cr, ci, ar, ai, seqs):
    t = uf.shape[0]
    l = t // seqs
    tl = min(SCAN_ROWS, l)
    nl = l // tl
    seg = tl // SCAN_SEGS
    cb, nq = SCAN_COLS, SCAN_CHUNKS
    nc = cb // LANES
    per = S5_ST // LANES

    def body(u_ref, bbr_ref, bbi_ref, cr_ref, ci_ref, ar_ref, ai_ref, x_r, x_i, ys_ref,
             car_r, car_i, pw_r, pw_i, b_r, b_i, tmp_a, tmp_b):
        @pl.when(pl.program_id(2) == 0)
        def _():
            car_r[...] = jnp.zeros(car_r.shape, F32)
            car_i[...] = jnp.zeros(car_i.shape, F32)
            _powers_into(pw_r, pw_i, ar_ref[...], ai_ref[...], seg)

        u = _interleaved(u_ref, tmp_a, tmp_b, seg).astype(BF16)
        for q in range(nq):
            uq = u[:, q * S5_IN:(q + 1) * S5_IN]
            br = jnp.dot(uq, bbr_ref[q], preferred_element_type=F32)
            bi = jnp.dot(uq, bbi_ref[q], preferred_element_type=F32)
            for s in range(per):
                b_r[q * per + s] = br[:, s * LANES:(s + 1) * LANES]
                b_i[q * per + s] = bi[:, s * LANES:(s + 1) * LANES]
        _segment_scan(b_r, b_i, x_r, x_i, pw_r, pw_i, car_r, car_i, seg, 1.0, False)
        wide = lambda buf, q: jnp.concatenate([buf[q * per + s] for s in range(per)], axis=1).astype(BF16)
        ys = [jnp.dot(wide(x_r, q), cr_ref[q], preferred_element_type=F32)
              + jnp.dot(wide(x_i, q), ci_ref[q], preferred_element_type=F32) for q in range(nq)]
        _store_deinterleaved(ys_ref, jnp.concatenate(ys, axis=1), tmp_a, tmp_b, seg)

    rows = lambda w: pl.BlockSpec((tl, w), lambda s, j, r: (s * nl + r, j))
    state = pl.BlockSpec((nc, tl, LANES), lambda s, j, r: (j, s * nl + r, 0))
    chunk = lambda a: pl.BlockSpec((nq,) + a.shape[1:], lambda s, j, r: (j, 0, 0))
    par = pl.BlockSpec((1, cb), lambda s, j, r: (0, j))
    return pl.pallas_call(
        body, name="s5_fwd", grid=(seqs, S5_CH // cb, nl),
        in_specs=[rows(nq * S5_IN), chunk(bbr), chunk(bbi), chunk(cr), chunk(ci), par, par],
        out_specs=[state, state, rows(nq * S5_IN)],
        out_shape=[jax.ShapeDtypeStruct((S5_CH // LANES, t, LANES), F32)] * 2
        + [jax.ShapeDtypeStruct((t, S5_WIDTH), F32)],
        scratch_shapes=[pltpu.VMEM((1, cb), F32), pltpu.VMEM((1, cb), F32), pltpu.VMEM((seg, cb), F32),
                        pltpu.VMEM((seg, cb), F32)] + [pltpu.VMEM((nc, tl, LANES), F32)] * 2
        + [pltpu.VMEM((nq * S5_IN // LANES, tl, LANES), F32)] * 2,
        compiler_params=_params(("parallel", "parallel", "arbitrary")),
    )(uf, bbr, bbi, cr, ci, ar, ai)


def _s5_bwd(dys, uf, xr, xi, bbr, bbi, cr, ci, ar, ai, seqs):
    t = dys.shape[0]
    l = t // seqs
    tl = min(SCAN_ROWS, l)
    nl = l // tl
    seg = tl // SCAN_SEGS
    cb, nq = SCAN_COLS, SCAN_CHUNKS
    nc = cb // LANES
    per = S5_ST // LANES

    def body(dy_ref, u_ref, x_r, x_i, bbr_ref, bbi_ref, cr_ref, ci_ref, ar_ref, ai_ref,
             du_ref, dbbr_ref, dbbi_ref, dcr_ref, dci_ref, dar_ref, dai_ref,
             car_r, car_i, pw_r, pw_i, g_r, g_i, lam_r, lam_i, acc_r, acc_i, tmp_a, tmp_b):
        @pl.when(pl.program_id(2) == 0)
        def _():
            car_r[...] = jnp.zeros(car_r.shape, F32)
            car_i[...] = jnp.zeros(car_i.shape, F32)
            _powers_into(pw_r, pw_i, ar_ref[...], ai_ref[...], seg)
            for acc_ref in (dbbr_ref, dbbi_ref, dcr_ref, dci_ref, dar_ref, dai_ref):
                acc_ref[...] = jnp.zeros(acc_ref.shape, F32)

        dy = _interleaved(dy_ref, tmp_a, tmp_b, seg).astype(BF16)
        for q in range(nq):
            dyq = dy[:, q * S5_IN:(q + 1) * S5_IN]
            gr = lax.dot_general(dyq, cr_ref[q], _NT, preferred_element_type=F32)
            gi = lax.dot_general(dyq, ci_ref[q], _NT, preferred_element_type=F32)
            for s in range(per):
                g_r[q * per + s] = gr[:, s * LANES:(s + 1) * LANES]
                g_i[q * per + s] = gi[:, s * LANES:(s + 1) * LANES]
        acc_r[...] = jnp.zeros(acc_r.shape, F32)
        acc_i[...] = jnp.zeros(acc_i.shape, F32)

        def visit(c, rws, lr, li):
            xr_t, xi_t = x_r[c, rws, :], x_i[c, rws, :]
            acc_r[c] += lr * xr_t + li * xi_t
            acc_i[c] += li * xr_t - lr * xi_t

        _segment_scan(g_r, g_i, lam_r, lam_i, pw_r, pw_i, car_r, car_i, seg, -1.0, True, visit)
        for c in range(nc):
            dar_ref[:, c * LANES:(c + 1) * LANES] += jnp.sum(acc_r[c], axis=0, keepdims=True)
            dai_ref[:, c * LANES:(c + 1) * LANES] += jnp.sum(acc_i[c], axis=0, keepdims=True)
        u = _interleaved(u_ref, tmp_a, tmp_b, seg).astype(BF16)
        wide = lambda buf, q: jnp.concatenate([buf[q * per + s] for s in range(per)], axis=1).astype(BF16)
        du = []
        for q in range(nq):
            io = slice(q * S5_IN, (q + 1) * S5_IN)
            lq_r, lq_i = wide(lam_r, q), wide(lam_i, q)
            du.append(lax.dot_general(lq_r, bbr_ref[q], _NT, preferred_element_type=F32)
                      + lax.dot_general(lq_i, bbi_ref[q], _NT, preferred_element_type=F32))
            dbbr_ref[q] += lax.dot_general(u[:, io], lq_r, _TN, preferred_element_type=F32)
            dbbi_ref[q] += lax.dot_general(u[:, io], lq_i, _TN, preferred_element_type=F32)
            dcr_ref[q] += lax.dot_general(wide(x_r, q), dy[:, io], _TN, preferred_element_type=F32)
            dci_ref[q] += lax.dot_general(wide(x_i, q), dy[:, io], _TN, preferred_element_type=F32)
        _store_deinterleaved(du_ref, jnp.concatenate(du, axis=1), tmp_a, tmp_b, seg)

    rows = lambda w: pl.BlockSpec((tl, w), lambda s, j, r: (s * nl + nl - 1 - r, j))
    state = pl.BlockSpec((nc, tl, LANES), lambda s, j, r: (j, s * nl + nl - 1 - r, 0))
    chunk = lambda a: pl.BlockSpec((nq,) + a.shape[1:], lambda s, j, r: (j, 0, 0))
    acc = lambda a: pl.BlockSpec((None, nq) + a.shape[1:], lambda s, j, r: (s, j, 0, 0))
    par = pl.BlockSpec((1, cb), lambda s, j, r: (0, j))
    par_acc = pl.BlockSpec((None, 1, cb), lambda s, j, r: (s, 0, j))
    per_seq = lambda a: jax.ShapeDtypeStruct((seqs,) + a.shape, F32)
    return pl.pallas_call(
        body, name="s5_bwd", grid=(seqs, S5_CH // cb, nl),
        in_specs=[rows(nq * S5_IN), rows(nq * S5_IN), state, state, chunk(bbr), chunk(bbi), chunk(cr), chunk(ci),
                  par, par],
        out_specs=[rows(nq * S5_IN), acc(bbr), acc(bbi), acc(cr), acc(ci), par_acc, par_acc],
        out_shape=[jax.ShapeDtypeStruct((t, S5_WIDTH), F32), per_seq(bbr), per_seq(bbi), per_seq(cr), per_seq(ci),
                   jax.ShapeDtypeStruct((seqs, 1, S5_CH), F32), jax.ShapeDtypeStruct((seqs, 1, S5_CH), F32)],
        scratch_shapes=[pltpu.VMEM((1, cb), F32), pltpu.VMEM((1, cb), F32), pltpu.VMEM((seg, cb), F32),
                        pltpu.VMEM((seg, cb), F32)] + [pltpu.VMEM((nc, tl, LANES), F32)] * 4
        + [pltpu.VMEM((nc, SCAN_SEGS, LANES), F32)] * 2 + [pltpu.VMEM((nq * S5_IN // LANES, tl, LANES), F32)] * 2,
        compiler_params=_params(("parallel", "parallel", "arbitrary")),
    )(dys, uf, xr, xi, bbr, bbi, cr, ci, ar, ai)


XATT_BLOCK = 2048


def _xatt_probs(qv, kv):
    s = lax.dot_general(qv, kv, _NT, preferred_element_type=F32) * (X_HEAD_DIM ** -0.5)
    e = jnp.exp(s - jnp.max(s, axis=-1, keepdims=True))
    return e / jnp.sum(e, axis=-1, keepdims=True)


def _xatt_fwd(q, k, kv, seqs):
    t = q.shape[0]
    tq = min(XATT_BLOCK, t // seqs)
    nq = t // seqs // tq

    def body(q_ref, k_ref, v_ref, o_ref):
        p = _xatt_probs(q_ref[...], k_ref[...])
        o_ref[...] = jnp.dot(p.astype(BF16), v_ref[...].astype(BF16), preferred_element_type=F32).astype(o_ref.dtype)

    qs = pl.BlockSpec((tq, X_HEAD_DIM), lambda b, h, i: (b * nq + i, h))
    return pl.pallas_call(
        body, name="xatt_fwd", grid=(seqs, N_X_HEADS, nq),
        in_specs=[qs, pl.BlockSpec((N_MEM, X_HEAD_DIM), lambda b, h, i: (b, h)),
                  pl.BlockSpec((N_MEM, X_HEAD_DIM), lambda b, h, i: (b, N_X_HEADS + h))],
        out_specs=qs, out_shape=jax.ShapeDtypeStruct(q.shape, BF16),
        compiler_params=_params(("parallel", "parallel", "parallel")),
    )(q, k, kv)


def _xatt_bwd(q, k, kv, do, seqs):
    t = q.shape[0]
    tq = min(XATT_BLOCK, t // seqs)
    nq = t // seqs // tq
    scale = X_HEAD_DIM ** -0.5

    def body(q_ref, k_ref, v_ref, do_ref, dq_ref, dk_ref, dv_ref):
        @pl.when(pl.program_id(2) == 0)
        def _():
            dk_ref[...] = jnp.zeros(dk_ref.shape, F32)
            dv_ref[...] = jnp.zeros(dv_ref.shape, F32)

        qv, kk = q_ref[...], k_ref[...]
        p = _xatt_probs(qv, kk)
        dob = do_ref[...].astype(BF16)
        dp = lax.dot_general(dob, v_ref[...].astype(BF16), _NT, preferred_element_type=F32)
        ds = p * (dp - jnp.sum(dp * p, axis=-1, keepdims=True))
        dsb = ds.astype(BF16)
        dq_ref[...] = jnp.dot(dsb, kk, preferred_element_type=F32) * scale
        dk_ref[...] += lax.dot_general(dsb, qv, _TN, preferred_element_type=F32) * scale
        dv_ref[...] += lax.dot_general(p.astype(BF16), dob, _TN, preferred_element_type=F32)

    qs = pl.BlockSpec((tq, X_HEAD_DIM), lambda b, h, i: (b * nq + i, h))
    ks = pl.BlockSpec((N_MEM, X_HEAD_DIM), lambda b, h, i: (b, h))
    return pl.pallas_call(
        body, name="xatt_bwd", grid=(seqs, N_X_HEADS, nq),
        in_specs=[qs, ks, pl.BlockSpec((N_MEM, X_HEAD_DIM), lambda b, h, i: (b, N_X_HEADS + h)), qs],
        out_specs=[qs, ks, ks],
        out_shape=[jax.ShapeDtypeStruct(q.shape, F32), jax.ShapeDtypeStruct(k.shape, F32),
                   jax.ShapeDtypeStruct(k.shape, F32)],
        compiler_params=_params(("parallel", "parallel", "arbitrary")),
    )(q, k, kv, do)


CONV_COLS = 256


def _shift_down(x, k, row):
    return jnp.where(row >= k, pltpu.roll(x, k, 0), 0.0)


def _shift_up(x, k, row):
    n = x.shape[0]
    return jnp.where(row < n - k, pltpu.roll(x, n - k, 0), 0.0)


def _down_from(x, prev, k, row):
    return jnp.where(row >= k, pltpu.roll(x, k, 0), pltpu.roll(prev, k, 0))


GATE_ROWS = 256


def _ffn_up_gate(hn, w_up, w, b, seqs):
    t = hn.shape[0]
    l = t // seqs
    nc = D_FF // CONV_COLS

    rc = min(GATE_ROWS, l)

    def body(a_ref, wg_ref, wu_ref, w_ref, b_ref, g_ref, u_ref, p_ref, o_ref):
        wv, bias = w_ref[...], b_ref[...]
        row = lax.broadcasted_iota(jnp.int32, (rc, CONV_COLS), 0)
        prev = jnp.zeros((rc, CONV_COLS), F32)
        for k in range(l // rc):
            rows = slice(k * rc, (k + 1) * rc)
            a = a_ref[rows, :]
            gb = jnp.dot(a, wg_ref[...], preferred_element_type=F32).astype(BF16)
            ub = jnp.dot(a, wu_ref[...], preferred_element_type=F32).astype(BF16)
            g_ref[rows, :] = gb
            u_ref[rows, :] = ub
            g = gb.astype(F32)
            pre = bias + wv[0:1, :] * _down_from(g, prev, 2, row) + wv[1:2, :] * _down_from(g, prev, 1, row) \
                + wv[2:3, :] * g
            p_ref[rows, :] = pre.astype(p_ref.dtype)
            o_ref[rows, :] = (pre * jax.nn.sigmoid(pre) * ub.astype(F32)).astype(o_ref.dtype)
            prev = g

    cols = pl.BlockSpec((l, CONV_COLS), lambda s, j: (s, j))
    half = jax.ShapeDtypeStruct((t, D_FF), BF16)
    return pl.pallas_call(
        body, name="ffn_up_gate", grid=(seqs, nc),
        in_specs=[pl.BlockSpec((l, hn.shape[1]), lambda s, j: (s, 0)),
                  pl.BlockSpec((hn.shape[1], CONV_COLS), lambda s, j: (0, j)),
                  pl.BlockSpec((hn.shape[1], CONV_COLS), lambda s, j: (0, nc + j)),
                  pl.BlockSpec((3, CONV_COLS), lambda s, j: (0, j)), pl.BlockSpec((1, CONV_COLS), lambda s, j: (0, j))],
        out_specs=[cols] * 4, out_shape=[half] * 4,
        compiler_params=_params(("parallel", "parallel")),
    )(hn, w_up, w_up, w, b)


def _ffn_down_dx_gate(dh, w_down, gate, up, pre, w, seqs):
    t = dh.shape[0]
    l = t // seqs
    nc = D_FF // CONV_COLS
    steps = nc * seqs

    def body(dh_ref, wd_ref, g_ref, u_ref, p_ref, w_ref, dgu_ref, dw_ref, db_ref, stage, sems):
        s, j = pl.program_id(0), pl.program_id(1)
        n = s * nc + j
        slot = n % 2

        def copies(slot_, j_, s_):
            rows = pl.ds(pl.multiple_of(s_ * l, 16), l)
            return [pltpu.make_async_copy(
                stage.at[slot_, half],
                dgu_ref.at[rows, pl.ds(pl.multiple_of((half * nc + j_) * CONV_COLS, 128), CONV_COLS)],
                sems.at[slot_, half]) for half in (0, 1)]

        @pl.when(n >= 2)
        def _():
            for cp in copies(slot, j, s):
                cp.wait()

        da = lax.dot_general(dh_ref[...], wd_ref[...], _NT, preferred_element_type=F32)
        g, pre, wv = g_ref[...].astype(F32), p_ref[...].astype(F32), w_ref[...]
        row = lax.broadcasted_iota(jnp.int32, g.shape, 0)
        sg = jax.nn.sigmoid(pre)
        silu = pre * sg
        stage[slot, 1] = (da * silu).astype(stage.dtype)
        dpre = da * u_ref[...].astype(F32) * (sg * (1.0 + pre * (1.0 - sg)))
        dpre1, dpre2 = _shift_up(dpre, 1, row), _shift_up(dpre, 2, row)
        dg = wv[2:3, :] * dpre + wv[1:2, :] * dpre1 + wv[0:1, :] * dpre2
        stage[slot, 0] = dg.astype(stage.dtype)
        for cp in copies(slot, j, s):
            cp.start()
        dw_ref[0:1, :] = jnp.sum(dpre2 * g, axis=0, keepdims=True)
        dw_ref[1:2, :] = jnp.sum(dpre1 * g, axis=0, keepdims=True)
        dw_ref[2:3, :] = jnp.sum(dpre * g, axis=0, keepdims=True)
        db_ref[...] = jnp.sum(dpre, axis=0, keepdims=True)

        @pl.when(n == steps - 1)
        def _():
            for cp in copies(slot, j, s) + (copies(1 - slot, j, s) if steps > 1 else []):
                cp.wait()

    cols = pl.BlockSpec((l, CONV_COLS), lambda s, j: (s, j))
    return pl.pallas_call(
        body, name="ffn_down_dx_gate", grid=(seqs, nc),
        in_specs=[pl.BlockSpec((l, dh.shape[1]), lambda s, j: (s, 0)),
                  pl.BlockSpec((CONV_COLS, dh.shape[1]), lambda s, j: (j, 0)), cols, cols, cols,
                  pl.BlockSpec((3, CONV_COLS), lambda s, j: (0, j))],
        out_specs=[ANY, pl.BlockSpec((None, 3, CONV_COLS), lambda s, j: (s, 0, j)),
                   pl.BlockSpec((None, 1, CONV_COLS), lambda s, j: (s, 0, j))],
        out_shape=[jax.ShapeDtypeStruct((t, 2 * D_FF), BF16), jax.ShapeDtypeStruct((seqs, 3, D_FF), F32),
                   jax.ShapeDtypeStruct((seqs, 1, D_FF), F32)],
        scratch_shapes=[pltpu.VMEM((2, 2, l, CONV_COLS), BF16), pltpu.SemaphoreType.DMA((2, 2))],
        compiler_params=_params(("arbitrary", "arbitrary")),
    )(dh, w_down, gate, up, pre, w)


def _loss_head(h, target):
    t, d = h.shape
    tm = _pick(t, (256, 128, 8))

    def body(h_ref, t_ref, dh_ref, dhb_ref, loss_ref):
        @pl.when(pl.program_id(0) == 0)
        def _():
            loss_ref[...] = jnp.zeros(loss_ref.shape, F32)

        e = h_ref[...] - t_ref[...]
        dh = e * (1.0 / d)
        dh_ref[...] = dh
        dhb_ref[...] = dh.astype(BF16)
        loss_ref[...] += (0.5 / d) * jnp.sum(jnp.sum(e * e, axis=1, keepdims=True), axis=0, keepdims=True)

    blk = pl.BlockSpec((tm, d), lambda i: (i, 0))
    return pl.pallas_call(
        body, name="loss_head", grid=(t // tm,), in_specs=[blk, blk],
        out_specs=[blk, blk, pl.BlockSpec((1, 1), lambda i: (0, 0))],
        out_shape=[jax.ShapeDtypeStruct((t, d), F32), jax.ShapeDtypeStruct((t, d), BF16),
                   jax.ShapeDtypeStruct((1, 1), F32)],
        compiler_params=_params(("arbitrary",)),
    )(h, target)


def _s5_discretise(a_re, a_im, log_dt, b_re, b_im):
    dt = jnp.exp(log_dt)[:, None]
    mag = jnp.exp(a_re * dt)
    lb_r = mag * jnp.cos(a_im * dt)
    lb_i = mag * jnp.sin(a_im * dt)
    den = a_re * a_re + a_im * a_im
    nr = lb_r - 1.0
    coef_r = (nr * a_re + lb_i * a_im) / den
    coef_i = (lb_i * a_re - nr * a_im) / den
    bb_r = coef_r[:, :, None] * b_re - coef_i[:, :, None] * b_im
    bb_i = coef_r[:, :, None] * b_im + coef_i[:, :, None] * b_re
    return lb_r, lb_i, bb_r, bb_i


S5_CHUNKS = 4
S5_PER = S5_GROUPS // S5_CHUNKS


def _blockdiag_in(bb):
    eye = jnp.eye(S5_PER, dtype=bb.dtype)
    return jnp.einsum("jgpc,gh->jgchp", bb.reshape(S5_CHUNKS, S5_PER, S5_STATE, S5_GROUP_CH), eye).reshape(
        S5_CHUNKS, S5_PER * S5_GROUP_CH, S5_PER * S5_STATE)


def _blockdiag_in_grad(d):
    eye = jnp.eye(S5_PER, dtype=d.dtype)
    return jnp.einsum("jgchp,gh->jgpc", d.reshape(S5_CHUNKS, S5_PER, S5_GROUP_CH, S5_PER, S5_STATE), eye).reshape(
        S5_GROUPS, S5_STATE, S5_GROUP_CH)


def _blockdiag_out(c):
    eye = jnp.eye(S5_PER, dtype=c.dtype)
    return jnp.einsum("jgcp,gh->jgphc", c.reshape(S5_CHUNKS, S5_PER, S5_GROUP_CH, S5_STATE), eye).reshape(
        S5_CHUNKS, S5_PER * S5_STATE, S5_PER * S5_GROUP_CH)


def _blockdiag_out_grad(d):
    eye = jnp.eye(S5_PER, dtype=d.dtype)
    return jnp.einsum("jgphc,gh->jgcp", d.reshape(S5_CHUNKS, S5_PER, S5_STATE, S5_PER, S5_GROUP_CH), eye).reshape(
        S5_GROUPS, S5_GROUP_CH, S5_STATE)


def _local_step(x3, mem3, target3, p, wb, late_weights=None, early_grads=None):
    seqs, l, d = x3.shape
    t = seqs * l
    x = x3.reshape(t, d)
    mem = mem3.reshape(seqs * N_MEM, d)
    target = target3.reshape(t, d)
    full = lambda a: (a, a.shape[1], 0, 0)

    s5_in = (p["s5_a_re"], p["s5_a_im"], p["s5_log_dt"], p["s5_b_re"], p["s5_b_im"])
    (lb_r, lb_i, bb_r, bb_i), s5_pull = jax.vjp(_s5_discretise, *s5_in)
    ar, ai = lb_r.reshape(1, S5_CH), lb_i.reshape(1, S5_CH)
    bbr_d, bbi_d = _blockdiag_in(bb_r).astype(BF16), _blockdiag_in(bb_i).astype(BF16)
    cr_d, ci_d = _blockdiag_out(p["s5_c_re"]).astype(BF16), (-_blockdiag_out(p["s5_c_im"])).astype(BF16)
    d_row = p["s5_d"].reshape(1, S5_WIDTH)

    hn1 = _rowwise(_rms, [full(x)], [p["norm_mix"]], [(d, d, 0, BF16)], "norm_mix_fwd")
    if late_weights is not None:
        wb = dict(wb, **late_weights("first", hn1))
    conv_w = wb["ffn_conv_w"] if "ffn_conv_w" in wb else p["ffn_conv_w"]
    w_in = wb["w_in"]
    w_qkv = w_in[:, :3 * FOX_WIDTH]
    w_uf = jnp.concatenate(
        [w_in[:, 3 * FOX_WIDTH + N_FOX_HEADS:], w_in[:, 3 * FOX_WIDTH:3 * FOX_WIDTH + N_FOX_HEADS],
         jnp.zeros((d, UF_COLS - S5_WIDTH - N_FOX_HEADS), w_in.dtype)], axis=1)
    qkv = _mm(hn1, w_qkv, "nn", "in_qkv")
    uf = _mm(hn1, w_uf, "nn", "in_uf")

    bh = seqs * N_FOX_HEADS
    q_pair = (qkv, 128, 0, 1)
    k_pair = (qkv, 128, N_PAIRS, 1)
    gq2, gk2 = jnp.tile(p["fox_q_norm"], (1, 2)), jnp.tile(p["fox_k_norm"], (1, 2))
    pair_out = [(FOX_WIDTH, 128, 1, BF16)]
    qn = _rowwise(_rms_pair, [q_pair], [gq2], pair_out, "fox_qnorm_fwd", heads=N_PAIRS)
    kn = _rowwise(_rms_pair, [k_pair], [gk2], pair_out, "fox_knorm_fwd", heads=N_PAIRS)

    f_rows = uf[:, S5_WIDTH:S5_WIDTH + N_FOX_HEADS].reshape(seqs, l, N_FOX_HEADS).transpose(0, 2, 1).reshape(bh, l)
    f_bias = jnp.tile(p["fox_f_bias"].reshape(N_FOX_HEADS, 1), (seqs, 1))
    c_wide = jnp.broadcast_to(_forget_fwd(f_rows, f_bias)[:, :, None], (bh, l, 128))
    fox, lse = _fox_fwd(qn, kn, qkv, c_wide, seqs)

    xr, xi, ys = _s5_fwd(uf, bbr_d, bbi_d, cr_d, ci_d, ar, ai, seqs)
    u_blk = (uf, S5_WIDTH, 0, 0)
    yg = _rowwise(_s5_act, [full(ys), u_blk], [d_row], [(S5_WIDTH, S5_WIDTH, 0, F32)], "s5_act_fwd")
    if late_weights is not None:
        wb = dict(wb, **late_weights("mid", yg))
    z = _mm(yg, wb["s5_w_glu"], "nn", "s5_glu")
    y2n = _rowwise(_s5_gate, [full(yg), full(z)], [p["s5_b_glu"], p["out_norm_s5"]],
                   [(S5_WIDTH, S5_WIDTH, 0, BF16)], "s5_gate_fwd")
    foxn = _rowwise(_rms, [full(fox)], [p["out_norm_fox"]], [(FOX_WIDTH, FOX_WIDTH, 0, BF16)], "fox_outnorm_fwd")
    mixed = jnp.concatenate([foxn, y2n], axis=1)
    h1 = _mm(mixed, wb["w_out"], "nn", "mix_out", res=x)
    if late_weights is not None:
        wb = dict(wb, **late_weights("late", h1))

    hn2 = _rowwise(_rms, [full(h1)], [p["norm_cross"]], [(d, d, 0, BF16)], "norm_cross_fwd")
    mn = _rowwise(_rms, [full(mem)], [p["norm_mem"]], [(d, d, 0, BF16)], "norm_mem_fwd")
    xq_raw = _mm(hn2, wb["w_xq"], "nn", "x_q")
    kv = _mm(mn, wb["w_xkv"], "nn", "x_kv")
    xh = lambda a: (a, X_HEAD_DIM, 0, 1)
    xqn = _rowwise(_rms, [xh(xq_raw)], [p["xq_norm"]], [(d, X_HEAD_DIM, 1, BF16)], "x_qnorm_fwd", heads=N_X_HEADS)
    xkn = _rowwise(_rms, [xh(kv)], [p["xk_norm"]], [(d, X_HEAD_DIM, 1, BF16)], "x_knorm_fwd", heads=N_X_HEADS)
    xo = _xatt_fwd(xqn, xkn, kv, seqs)
    h2 = _mm(xo, wb["w_xo"], "nn", "x_out", res=h1)

    hn3 = _rowwise(_rms, [full(h2)], [p["norm_ffn"]], [(d, d, 0, BF16)], "norm_ffn_fwd")
    gate, up, pre, act = _ffn_up_gate(hn3, wb["w_ffn_up"], conv_w, p["ffn_conv_b"], seqs)
    h3 = _mm(act, wb["w_ffn_down"], "nn", "ffn_down", res=h2)
    dh3, dh3_b, loss = _loss_head(h3, target)

    g = {}
    late_dt = BF16 if early_grads is not None else F32
    g["w_ffn_down"] = _mm(act, dh3_b, "tn", "ffn_down_dw", out_dtype=late_dt)
    dgu, dconv_w, dconv_b = _ffn_down_dx_gate(dh3_b, wb["w_ffn_down"], gate, up, pre, conv_w, seqs)
    g["ffn_conv_w"], g["ffn_conv_b"] = jnp.sum(dconv_w, axis=0), jnp.sum(dconv_b, axis=0)
    dhn3 = _mm(dgu, wb["w_ffn_up"], "nt", "ffn_up_dx", out_dtype=BF16)
    g["w_ffn_up"] = _mm(hn3, dgu, "tn", "ffn_up_dw", out_dtype=late_dt)
    (dh2,), (g["norm_ffn"],) = _rowwise_vjp(_rms, [full(h2)], [p["norm_ffn"]], [full(dhn3)], "norm_ffn_bwd",
                                            adds=[full(dh3)])

    dxo = _mm(dh2, wb["w_xo"], "nt", "x_out_dx", out_dtype=BF16)
    g["w_xo"] = _mm(xo, dh2, "tn", "x_out_dw", out_dtype=late_dt)
    dxqn, dxkn, dxv = _xatt_bwd(xqn, xkn, kv, dxo, seqs)
    (dxq_raw,), (g["xq_norm"],) = _rowwise_vjp(_rms, [xh(xq_raw)], [p["xq_norm"]], [xh(dxqn)], "x_qnorm_bwd",
                                               heads=N_X_HEADS, row_dtypes=[BF16])
    (dxk_raw,), (g["xk_norm"],) = _rowwise_vjp(_rms, [xh(kv)], [p["xk_norm"]], [xh(dxkn)], "x_knorm_bwd",
                                               heads=N_X_HEADS, row_dtypes=[BF16])
    dkv = jnp.concatenate([dxk_raw, dxv.astype(BF16)], axis=1)
    dhn2 = _mm(dxq_raw, wb["w_xq"], "nt", "x_q_dx", out_dtype=BF16)
    g["w_xq"] = _mm(hn2, dxq_raw, "tn", "x_q_dw", out_dtype=late_dt)
    dmn = _mm(dkv, wb["w_xkv"], "nt", "x_kv_dx")
    g["w_xkv"] = _mm(mn, dkv, "tn", "x_kv_dw", out_dtype=late_dt)
    norm_cross = p["norm_cross"]
    if early_grads is not None:
        norm_cross = norm_cross + early_grads("late", {n: g[n] for n in LATE_WEIGHTS})[0:1, 0:1]
    (dh1,), (g["norm_cross"],) = _rowwise_vjp(_rms, [full(h1)], [norm_cross], [full(dhn2)], "norm_cross_bwd",
                                              adds=[full(dh2)])
    _, (g["norm_mem"],) = _rowwise_vjp(_rms, [full(mem)], [p["norm_mem"]], [full(dmn)], "norm_mem_bwd",
                                       row_dtypes=[BF16])

    dmixed = _mm(dh1, wb["w_out"], "nt", "mix_out_dx", out_dtype=BF16)
    g["w_out"] = _mm(mixed, dh1, "tn", "mix_out_dw", out_dtype=late_dt)
    (dfox,), (g["out_norm_fox"],) = _rowwise_vjp(_rms, [full(fox)], [p["out_norm_fox"]],
                                                 [(dmixed, FOX_WIDTH, 0, 0)], "fox_outnorm_bwd")
    (dyg_a, dz), (g["s5_b_glu"], g["out_norm_s5"]) = _rowwise_vjp(
        _s5_gate, [full(yg), full(z)], [p["s5_b_glu"], p["out_norm_s5"]], [(dmixed, S5_WIDTH, 1, 0)], "s5_gate_bwd",
        row_dtypes=[F32, BF16])
    dyg = _mm(dz, wb["s5_w_glu"], "nt", "s5_glu_dx", res=dyg_a)
    g["s5_w_glu"] = _mm(yg, dz, "tn", "s5_glu_dw", out_dtype=late_dt)
    if early_grads is not None:
        d_row = d_row + early_grads("mid", {n: g[n] for n in MID_WEIGHTS})[0:1, 0:1]
    (dys, du_a), (dd_row,) = _rowwise_vjp(_s5_act, [full(ys), u_blk], [d_row], [full(dyg)], "s5_act_bwd",
                                          row_dtypes=[BF16, F32])
    g["s5_d"] = dd_row
    du_b, dbbr_d, dbbi_d, dcr_d, dci_d, dar, dai = _s5_bwd(dys, uf, xr, xi, bbr_d, bbi_d, cr_d, ci_d, ar, ai, seqs)
    dbbr_d, dbbi_d, dcr_d, dci_d = (jnp.sum(a, axis=0) for a in (dbbr_d, dbbi_d, dcr_d, dci_d))
    d_lb_r = jnp.sum(dar, axis=0).reshape(S5_GROUPS, S5_STATE)
    d_lb_i = jnp.sum(dai, axis=0).reshape(S5_GROUPS, S5_STATE)
    g["s5_a_re"], g["s5_a_im"], g["s5_log_dt"], g["s5_b_re"], g["s5_b_im"] = s5_pull(
        (d_lb_r, d_lb_i, _blockdiag_in_grad(dbbr_d), _blockdiag_in_grad(dbbi_d)))
    g["s5_c_re"] = _blockdiag_out_grad(dcr_d)
    g["s5_c_im"] = -_blockdiag_out_grad(dci_d)

    dqn, dkn, dv, dc, dcq = _fox_bwd(qn, kn, qkv, c_wide, fox, dfox, lse, seqs)
    pair = lambda a: (a, 128, 0, 1)
    (dq_raw,), (dgq2,) = _rowwise_vjp(_rms_pair, [q_pair], [gq2], [pair(dqn)], "fox_qnorm_bwd", heads=N_PAIRS,
                                      row_dtypes=[BF16])
    (dk_raw,), (dgk2,) = _rowwise_vjp(_rms_pair, [k_pair], [gk2], [pair(dkn)], "fox_knorm_bwd", heads=N_PAIRS,
                                      row_dtypes=[BF16])
    g["fox_q_norm"] = dgq2[:, :HEAD_DIM] + dgq2[:, HEAD_DIM:]
    g["fox_k_norm"] = dgk2[:, :HEAD_DIM] + dgk2[:, HEAD_DIM:]
    df_rows, dfb = _forget_bwd(f_rows, f_bias, (dc + dcq).reshape(bh, l))
    g["fox_f_bias"] = jnp.sum(dfb.reshape(seqs, N_FOX_HEADS), axis=0)
    df = df_rows.reshape(seqs, N_FOX_HEADS, l).transpose(0, 2, 1).reshape(t, N_FOX_HEADS)
    dqkv = jnp.concatenate([dq_raw, dk_raw, dv.astype(BF16)], axis=1)
    duf = jnp.concatenate([du_a + du_b, df, jnp.zeros((t, UF_COLS - S5_WIDTH - N_FOX_HEADS), F32)],
                          axis=1).astype(BF16)
    dhn1 = _mm(duf, w_uf, "nt", "in_uf_dx", res=_mm(dqkv, w_qkv, "nt", "in_qkv_dx"), out_dtype=BF16)
    dw_qkv = _mm(hn1, dqkv, "tn", "in_qkv_dw")
    dw_uf = _mm(hn1, duf, "tn", "in_uf_dw")
    g["w_in"] = jnp.concatenate([dw_qkv, dw_uf[:, S5_WIDTH:S5_WIDTH + N_FOX_HEADS], dw_uf[:, :S5_WIDTH]], axis=1)
    (dx,), (g["norm_mix"],) = _rowwise_vjp(_rms, [full(x)], [p["norm_mix"]], [full(dhn1)], "norm_mix_bwd",
                                           adds=[full(dh1)])
    return loss, dx.reshape(seqs, l, d), g


def _place():
    return lax.axis_index("x"), lax.axis_index("y"), lax.axis_index("c")


def _other_chips(x, y):
    return [(1 - x, y), (x, 1 - y), (1 - x, 1 - y)]


ANY = pl.BlockSpec(memory_space=pl.ANY)


HBM = pl.BlockSpec(memory_space=pltpu.HBM)
SEM = pl.BlockSpec(memory_space=pltpu.SEMAPHORE)
DATAFLOW = pltpu.SideEffectType.DATAFLOW_SIDE_EFFECTING


def _in_hbm(a):
    return pltpu.with_memory_space_constraint(a, pltpu.HBM)


def _split_start(name, srcs, lands, n_copies, plan):
    n = len(srcs)

    def body(*refs):
        src_refs, land_refs = refs[:n], refs[n:2 * n]
        send_sems, recv_sems = refs[2 * n], refs[2 * n + 1]
        for i, (src, dst, dev) in enumerate(plan(src_refs, land_refs)):
            pltpu.make_async_remote_copy(src_ref=src, dst_ref=dst, send_sem=send_sems.at[i], recv_sem=recv_sems.at[i],
                                         device_id=dev, device_id_type=MESH).start()
        refs[-1][...] = jnp.zeros((8, 128), F32)

    res = pl.pallas_call(
        body, name=name, in_specs=[HBM] * (2 * n),
        out_specs=[SEM, SEM] + [HBM] * (2 * n) + [pl.BlockSpec(memory_space=pltpu.VMEM)],
        out_shape=[pltpu.SemaphoreType.DMA((n_copies,)), pltpu.SemaphoreType.DMA((n_copies,))]
        + [pltpu.HBM(a.shape, a.dtype) for a in list(srcs) + list(lands)] + [jax.ShapeDtypeStruct((8, 128), F32)],
        input_output_aliases={i: 2 + i for i in range(2 * n)},
        compiler_params=pltpu.CompilerParams(has_side_effects=DATAFLOW),
    )(*[_in_hbm(a) for a in list(srcs) + list(lands)])
    return res[0], res[1], list(res[2:2 + n]), list(res[2 + n:2 + 2 * n]), res[-1]


def _split_wait(name, send_sems, recv_sems, srcs, lands, after, plan):
    n = len(srcs)

    def body(*refs):
        src_refs, land_refs = refs[:n], refs[n:2 * n]
        send_ref, recv_ref = refs[2 * n], refs[2 * n + 1]
        for i, (src, dst, dev) in enumerate(plan(src_refs, land_refs)):
            cp = pltpu.make_async_remote_copy(src_ref=src, dst_ref=dst, send_sem=send_ref.at[i], recv_sem=recv_ref.at[i],
                                              device_id=dev, device_id_type=MESH)
            cp.wait_send()
            cp.wait_recv()

    res = pl.pallas_call(
        body, name=name, in_specs=[HBM] * (2 * n) + [SEM, SEM, ANY], out_specs=[HBM] * (2 * n),
        out_shape=[pltpu.HBM(a.shape, a.dtype) for a in list(srcs) + list(lands)],
        input_output_aliases={i: i for i in range(2 * n)},
        compiler_params=pltpu.CompilerParams(has_side_effects=DATAFLOW),
    )(*srcs, *lands, send_sems, recv_sems, after)
    return list(res[:n]), list(res[n:])


def _first_gather_plan(src_refs, land_refs):
    x, y, c = _place()
    mine = 2 * x + y
    (src, small), (land, small_land) = src_refs, land_refs
    r = src.shape[0]
    hr = r // 2
    half = src.at[pl.ds(pl.multiple_of(c * hr, 16), hr), :]
    half_dst = land.at[pl.ds(pl.multiple_of(mine * r + c * hr, 16), hr), :]
    copies = [(src, land.at[pl.ds(pl.multiple_of(mine * r, 16), r), :], (x, y, 1 - c)),
              (small, small_land.at[mine], (x, y, 1 - c))]
    for px, py in _other_chips(x, y):
        copies += [(half, half_dst, (px, py, c)), (small, small_land.at[mine], (px, py, c))]
    return copies


def _forward_to_sibling(full):
    def body(full_in, full_ref, send_sems, recv_sems):
        x, y, c = _place()
        r = full_ref.shape[0] // 4
        hr = r // 2
        copies = []
        for j, (px, py) in enumerate(_other_chips(x, y)):
            got = full_ref.at[pl.ds(pl.multiple_of((2 * px + py) * r + c * hr, 16), hr), :]
            cp = pltpu.make_async_remote_copy(
                src_ref=got, dst_ref=got, send_sem=send_sems.at[j], recv_sem=recv_sems.at[j],
                device_id=(x, y, 1 - c), device_id_type=MESH)
            cp.start()
            copies.append(cp)
        for cp in copies:
            cp.wait()

    return pl.pallas_call(
        body, name="gather_first_forward", in_specs=[ANY], out_specs=ANY,
        out_shape=jax.ShapeDtypeStruct(full.shape, full.dtype), input_output_aliases={0: 0},
        scratch_shapes=[pltpu.SemaphoreType.DMA((3,)), pltpu.SemaphoreType.DMA((3,))],
        compiler_params=pltpu.CompilerParams(has_side_effects=True),
    )(full)


def _late_gather_plan(col_kind):
    def plan(src_refs, land_refs):
        x, y, c = _place()
        mine = 2 * x + y
        copies = []
        for a, (src, land) in enumerate(zip(src_refs, land_refs)):
            r, cs = src.shape
            if col_kind[a]:
                dst = land.at[:, pl.ds(pl.multiple_of(mine * cs, 128), cs)]
            else:
                dst = land.at[pl.ds(pl.multiple_of(mine * r, 16), r), :]
            copies.append((src, dst, (x, y, 1 - c)))
            copies += [(src, dst, (px, py, c)) for (px, py) in _other_chips(x, y)]
        return copies
    return plan


def _late_reduce_plan(col_kind):
    def plan(src_refs, land_refs):
        x, y, c = _place()
        copies = []
        for a, (src, land) in enumerate(zip(src_refs, land_refs)):
            for j, (px, py) in enumerate(_other_chips(x, y)):
                if col_kind[a] is None:
                    piece = src
                elif col_kind[a]:
                    cs = land.shape[2]
                    piece = src.at[:, pl.ds(pl.multiple_of((2 * px + py) * cs, 128), cs)]
                else:
                    piece = src.at[2 * px + py]
                copies.append((piece, land.at[j], (px, py, c)))
        return copies
    return plan


def _pair_swap(name, halves):
    n = len(halves)

    def body(*refs):
        ins, outs = refs[:n], refs[n:2 * n]
        send_sems, recv_sems = refs[2 * n:]
        x, y, c = _place()
        copies = []
        for a in range(n):
            cp = pltpu.make_async_remote_copy(
                src_ref=ins[a], dst_ref=outs[a], send_sem=send_sems.at[a], recv_sem=recv_sems.at[a],
                device_id=(x, y, 1 - c), device_id_type=MESH)
            cp.start()
            copies.append(cp)
        for cp in copies:
            cp.wait()

    return pl.pallas_call(
        body, name=name, in_specs=[ANY] * n, out_specs=[ANY] * n,
        out_shape=[jax.ShapeDtypeStruct(s.shape, s.dtype) for s in halves],
        scratch_shapes=[pltpu.SemaphoreType.DMA((n,)), pltpu.SemaphoreType.DMA((n,))],
        compiler_params=pltpu.CompilerParams(has_side_effects=True),
    )(*halves)


def _chip_sum(name, chip_sel, own, col, others):
    _, r, c = others.shape
    tr = _pick(r, (256, 128, 64, 32, 16))
    if col:
        own_spec = pl.BlockSpec((tr, c), lambda i, s: (i, s[0]))
    else:
        own_spec = pl.BlockSpec((None, tr, c), lambda i, s: (s[0], i, 0))
    specs = [own_spec] + [pl.BlockSpec((None, tr, c), lambda i, s, k=k: (k, i, 0)) for k in range(3)]

    def body(s_ref, own_ref, r0, r1, r2, o_ref):
        total = ((own_ref[...].astype(F32) + r0[...].astype(F32)) + r1[...].astype(F32)) + r2[...].astype(F32)
        o_ref[...] = total.astype(o_ref.dtype)

    return pl.pallas_call(
        body, name=name,
        grid_spec=pltpu.PrefetchScalarGridSpec(
            num_scalar_prefetch=1, grid=(r // tr,), in_specs=specs,
            out_specs=pl.BlockSpec((tr, c), lambda i, s: (i, 0))),
        out_shape=jax.ShapeDtypeStruct((r, c), BF16),
        compiler_params=_params(("parallel",)),
    )(chip_sel, own, others, others, others)


def _small_layout(vals):
    sizes = [int(math.prod(v.shape)) for v in vals]
    padded = [-(-s // 128) * 128 for s in sizes]
    return sizes, padded, -(-sum(padded) // 1024) * 1024


def _pack_small(vals):
    sizes, padded, total = _small_layout(vals)
    flat = [jnp.pad(v.reshape(-1), (0, p - s)) for v, s, p in zip(vals, sizes, padded)]
    flat.append(jnp.zeros((total - sum(padded),), F32))
    return jnp.concatenate(flat).reshape(total // 128, 128)


def _allreduce_small(own, others, vals):
    def body(own_ref, oth_ref, out_ref, land, send_sem, recv_sem):
        x, y, c = _place()
        out_ref[...] = (own_ref[...] + oth_ref[0]) + (oth_ref[1] + oth_ref[2])
        cp = pltpu.make_async_remote_copy(
            src_ref=out_ref, dst_ref=land, send_sem=send_sem.at[0], recv_sem=recv_sem.at[0],
            device_id=(x, y, 1 - c), device_id_type=MESH)
        cp.start()
        cp.wait()
        out_ref[...] = out_ref[...] + land[...]

    vm = pl.BlockSpec(memory_space=pltpu.VMEM)
    summed = pl.pallas_call(
        body, name="allreduce_small", in_specs=[vm, vm], out_specs=vm,
        out_shape=jax.ShapeDtypeStruct(own.shape, F32),
        scratch_shapes=[pltpu.VMEM(own.shape, F32), pltpu.SemaphoreType.DMA((1,)), pltpu.SemaphoreType.DMA((1,))],
        compiler_params=pltpu.CompilerParams(has_side_effects=True, vmem_limit_bytes=VMEM_LIMIT_BYTES),
    )(own, others).reshape(-1)
    sizes, padded, _ = _small_layout(vals)
    outs, off = [], 0
    for v, s, p in zip(vals, sizes, padded):
        outs.append(summed[off:off + s].reshape(v.shape))
        off += p
    return outs


def _adamw_math(w, g, m, v):
    m2 = ADAM_B1 * m + (1.0 - ADAM_B1) * g
    v2 = ADAM_B2 * v + (1.0 - ADAM_B2) * (g * g)
    m_hat = m2 / (1.0 - ADAM_B1 ** ADAM_STEP)
    v_hat = v2 / (1.0 - ADAM_B2 ** ADAM_STEP)
    delta = -ADAM_LR * (m_hat / (jnp.sqrt(v_hat) + ADAM_EPS) + ADAM_WD * w)
    return delta, m2, v2


def _adamw_big(name, w, g_mine, g_sibling, m, v):
    _, r, c = w.shape

    def body(w_ref, ga_ref, gb_ref, m_ref, v_ref, go_ref, d_ref, mo_ref, vo_ref):
        gv = ga_ref[...].astype(F32) + gb_ref[...].astype(F32)
        d, m2, v2 = _adamw_math(w_ref[...], gv, m_ref[...], v_ref[...])
        go_ref[...] = gv
        d_ref[...] = d
        mo_ref[...] = m2
        vo_ref[...] = v2

    tr = _pick(r, (256, 128, 64, 32, 16, 8))
    if r % tr == 0 and tr % 8 == 0:
        grid = (r // tr,)
        blk = pl.BlockSpec((None, tr, c), lambda i: (0, i, 0))
        part = pl.BlockSpec((tr, c), lambda i: (i, 0))
    else:
        grid = (c // 512,)
        blk = pl.BlockSpec((None, r, 512), lambda i: (0, 0, i))
        part = pl.BlockSpec((r, 512), lambda i: (0, i))
    return pl.pallas_call(
        body, name=name, grid=grid, in_specs=[blk, part, part, blk, blk], out_specs=[blk] * 4,
        out_shape=[jax.ShapeDtypeStruct((1, r, c), F32)] * 4, compiler_params=_params(("parallel",)),
    )(w, g_mine, g_sibling, m, v)


def _adamw_small(ws, gs, ms, vs):
    n = len(ws)

    def body(*refs):
        w_r, g_r, m_r, v_r = refs[:n], refs[n:2 * n], refs[2 * n:3 * n], refs[3 * n:4 * n]
        o = refs[4 * n:]
        for a in range(n):
            gv = g_r[a][...]
            d, m2, v2 = _adamw_math(w_r[a][...], gv, m_r[a][...], v_r[a][...])
            o[a][...] = gv
            o[n + a][...] = d
            o[2 * n + a][...] = m2
            o[3 * n + a][...] = v2

    res = pl.pallas_call(
        body, name="adamw_small", out_shape=[jax.ShapeDtypeStruct(w.shape, F32) for _ in range(4) for w in ws],
        compiler_params=_params(),
    )(*ws, *gs, *ms, *vs)
    return res[:n], res[n:2 * n], res[2 * n:3 * n], res[3 * n:]


def _full_from_gathered(name, gathered):
    if name == "w_in":
        rows = gathered.shape[0] // 4
        return gathered.reshape(4, rows, gathered.shape[1]).transpose(1, 0, 2).reshape(rows, 4 * gathered.shape[1])
    return gathered


def _reduce_layout(name, full):
    if name in COL_KIND:
        return full
    if name == "w_in":
        rows, cols = full.shape
        return full.reshape(rows, 4, cols // 4).transpose(1, 0, 2)
    return full.reshape(4, full.shape[0] // 4, full.shape[1])


def kernel(x, mem, norm_mix, w_in, fox_q_norm, fox_k_norm, fox_f_bias, s5_a_re, s5_a_im, s5_log_dt, s5_b_re, s5_b_im, s5_c_re, s5_c_im, s5_d, s5_w_glu, s5_b_glu, out_norm_fox, out_norm_s5, w_out, norm_cross, norm_mem, w_xq, w_xkv, xq_norm, xk_norm, w_xo, norm_ffn, w_ffn_up, ffn_conv_w, ffn_conv_b, w_ffn_down, loss_target, m_norm_mix, m_w_in, m_fox_q_norm, m_fox_k_norm, m_fox_f_bias, m_s5_a_re, m_s5_a_im, m_s5_log_dt, m_s5_b_re, m_s5_b_im, m_s5_c_re, m_s5_c_im, m_s5_d, m_s5_w_glu, m_s5_b_glu, m_out_norm_fox, m_out_norm_s5, m_w_out, m_norm_cross, m_norm_mem, m_w_xq, m_w_xkv, m_xq_norm, m_xk_norm, m_w_xo, m_norm_ffn, m_w_ffn_up, m_ffn_conv_w, m_ffn_conv_b, m_w_ffn_down, v_norm_mix, v_w_in, v_fox_q_norm, v_fox_k_norm, v_fox_f_bias, v_s5_a_re, v_s5_a_im, v_s5_log_dt, v_s5_b_re, v_s5_b_im, v_s5_c_re, v_s5_c_im, v_s5_d, v_s5_w_glu, v_s5_b_glu, v_out_norm_fox, v_out_norm_s5, v_w_out, v_norm_cross, v_norm_mem, v_w_xq, v_w_xkv, v_xq_norm, v_xk_norm, v_w_xo, v_norm_ffn, v_w_ffn_up, v_ffn_conv_w, v_ffn_conv_b, v_w_ffn_down):
    given = dict(locals())
    w = {n: given[n] for n in WEIGHTS}
    m = {n: given["m_" + n] for n in WEIGHTS}
    v = {n: given["v_" + n] for n in WEIGHTS}
    xi, yi, _ = _place()
    chip = (2 * xi + yi).astype(jnp.int32)
    chip_sel = chip.reshape(1)

    first_shard, taps = w[FIRST_WEIGHT][0].astype(BF16), w["ffn_conv_w"][0]
    send, recv, srcs, lands, g_started = _split_start(
        "gather_first_start", [first_shard, taps],
        [lax.empty((4 * first_shard.shape[0], first_shard.shape[1]), BF16), lax.empty((4,) + taps.shape, F32)],
        8, _first_gather_plan)
    pending = {"first": ((FIRST_WEIGHT, "ffn_conv_w"), _first_gather_plan, send, recv, srcs, lands)}
    for stage, names in (("mid", MID_WEIGHTS), ("late", LATE_WEIGHTS)):
        kinds = [n in COL_KIND for n in names]
        shards = [w[n][0].astype(BF16) for n in names]
        shards[0] = shards[0] + g_started[0:1, 0:1].astype(BF16)
        full = [lax.empty((s.shape[0], 4 * s.shape[1]) if ck else (4 * s.shape[0], s.shape[1]), BF16)
                for s, ck in zip(shards, kinds)]
        plan = _late_gather_plan(kinds)
        send, recv, srcs, lands, g_started = _split_start(
            "gather_" + stage + "_start", shards, full, 4 * len(names), plan)
        pending[stage] = (names, plan, send, recv, srcs, lands)

    def late_weights(stage, after):
        names, plan, send, recv, srcs, lands = pending[stage]
        _, full = _split_wait("gather_" + stage + "_wait", send, recv, srcs, lands, after, plan)
        if stage == "first":
            full = [_full_from_gathered(FIRST_WEIGHT, _forward_to_sibling(full[0])),
                    full[1].transpose(1, 0, 2).reshape(3, D_FF)]
        return dict(zip(names, full))

    reducing = {}

    def start_reduce(stage, grads_by_name, whole=()):
        names = list(grads_by_name)
        kinds = [n in COL_KIND for n in names]
        grads = [_reduce_layout(n, grads_by_name[n].astype(BF16)) for n in names]
        lands = [lax.empty((3, s.shape[0], s.shape[1] // 4) if ck else (3,) + s.shape[1:], BF16)
                 for s, ck in zip(grads, kinds)]
        plan = _late_reduce_plan(kinds + [None] * len(whole))
        send, recv, srcs, lands, started = _split_start(
            "reduce_" + stage + "_start", grads + list(whole),
            lands + [lax.empty((3,) + a.shape, a.dtype) for a in whole], 3 * (len(names) + len(whole)), plan)
        reducing[stage] = (names, kinds, plan, send, recv, srcs, lands)
        return started

    p = {n: w[n][0] for n in SMALL}
    for n in ("norm_mix", "fox_q_norm", "fox_k_norm", "fox_f_bias", "s5_b_glu", "out_norm_fox", "out_norm_s5",
              "norm_cross", "norm_mem", "xq_norm", "xk_norm", "norm_ffn", "ffn_conv_b"):
        p[n] = p[n].reshape(1, -1)
    p["norm_mix"] = p["norm_mix"] + g_started[0:1, 0:1]
    loss, grad_x, g = _local_step(x, mem, loss_target, p, {}, late_weights, start_reduce)

    small_names = list(SMALL) + ["ffn_conv_w"]
    small_vals = [g[n].reshape(w[n].shape if n != "ffn_conv_w" else (1, 3, D_FF)) for n in small_names] + [loss]
    after = start_reduce("first", {FIRST_WEIGHT: g[FIRST_WEIGHT]}, whole=[_pack_small(small_vals)])

    out_g, out_d, out_m, out_v = {}, {}, {}, {}

    def finish(stage, after):
        names, kinds, plan, send, recv, srcs, lands = reducing[stage]
        sums, from_chips = _split_wait("reduce_" + stage + "_wait", send, recv, srcs, lands, after, plan)
        mine = [_chip_sum("reduce_chip_sum_" + n, chip_sel, ps, ck, fc)
                for n, ps, fc, ck in zip(names, sums, from_chips, kinds)]
        theirs = _pair_swap("reduce_pair_swap_" + stage, mine)
        for n, a, b in zip(names, mine, theirs):
            if n == "w_in":
                flip = lambda t: jnp.swapaxes(t, -1, -2)
                res = _adamw_big("adamw_" + n, flip(w[n]), flip(a), flip(b), flip(m[n]), flip(v[n]))
                out_g[n], out_d[n], out_m[n], out_v[n] = (flip(t) for t in res)
                continue
            out_g[n], out_d[n], out_m[n], out_v[n] = _adamw_big("adamw_" + n, w[n], a, b, m[n], v[n])
        return sums[len(names):], from_chips[len(names):], out_v[names[-1]]

    _, _, after = finish("late", after)
    _, _, after = finish("mid", after)
    (small_own,), (small_others,), _ = finish("first", after)

    reduced = _allreduce_small(small_own, small_others, small_vals)
    loss_all = reduced[-1].reshape(())
    conv_w_grad = lax.dynamic_slice_in_dim(reduced[-2], chip * (D_FF // 4), D_FF // 4, axis=2)
    sg, sd, sm, sv = _adamw_small(
        [w[n] for n in small_names], list(reduced[:len(SMALL)]) + [conv_w_grad],
        [m[n] for n in small_names], [v[n] for n in small_names])
    out_g.update(zip(small_names, sg))
    out_d.update(zip(small_names, sd))
    out_m.update(zip(small_names, sm))
    out_v.update(zip(small_names, sv))

    return (loss_all, grad_x, *[out_g[n] for n in WEIGHTS], *[out_d[n] for n in WEIGHTS],
            *[out_m[n] for n in WEIGHTS], *[out_v[n] for n in WEIGHTS])
```

```python
import math

import jax
import jax.numpy as jnp
from jax import lax
from jax.experimental import pallas as pl
from jax.experimental.pallas import tpu as pltpu

F32 = jnp.float32
BF16 = jnp.bfloat16

D_MODEL = 1024
FOX_WIDTH = 512
HEAD_DIM = 64
N_FOX_HEADS = 8
S5_WIDTH = 512
S5_GROUP_CH = 16
S5_GROUPS = 32
S5_STATE = 64
S5_CH = S5_GROUPS * S5_STATE
N_X_HEADS = 4
X_HEAD_DIM = 256
N_MEM = 256
D_FF = 2816
UF_COLS = 640
EPS = 1e-6
ADAM_LR = 0.001
ADAM_B1 = 0.9
ADAM_B2 = 0.999
ADAM_EPS = 1e-08
ADAM_WD = 0.01
ADAM_STEP = 10

VMEM_LIMIT_BYTES = 56 * 1024 * 1024
MM_BLOCK_BYTES = 6 * 1024 * 1024
MM_VMEM_BYTES = 40 * 1024 * 1024
MM_TILE_MAX = 1536
MESH = pl.DeviceIdType.MESH

FIRST_WEIGHT = "w_in"
MID_WEIGHTS = ("s5_w_glu", "w_out")
EARLY_WEIGHTS = (FIRST_WEIGHT,) + MID_WEIGHTS
LATE_WEIGHTS = ("w_xq", "w_xkv", "w_xo", "w_ffn_up", "w_ffn_down")
BIG = EARLY_WEIGHTS + LATE_WEIGHTS
COL_KIND = ("w_xkv", "w_ffn_up")
SMALL = ("norm_mix", "fox_q_norm", "fox_k_norm", "fox_f_bias", "s5_a_re", "s5_a_im", "s5_log_dt",
         "s5_b_re", "s5_b_im", "s5_c_re", "s5_c_im", "s5_d", "s5_b_glu", "out_norm_fox", "out_norm_s5",
         "norm_cross", "norm_mem", "xq_norm", "xk_norm", "norm_ffn", "ffn_conv_b")
WEIGHTS = ("norm_mix", "w_in", "fox_q_norm", "fox_k_norm", "fox_f_bias", "s5_a_re", "s5_a_im", "s5_log_dt",
           "s5_b_re", "s5_b_im", "s5_c_re", "s5_c_im", "s5_d", "s5_w_glu", "s5_b_glu", "out_norm_fox",
           "out_norm_s5", "w_out", "norm_cross", "norm_mem", "w_xq", "w_xkv", "xq_norm", "xk_norm", "w_xo",
           "norm_ffn", "w_ffn_up", "ffn_conv_w", "ffn_conv_b", "w_ffn_down")


def _params(sem=None):
    return pltpu.CompilerParams(dimension_semantics=sem, vmem_limit_bytes=VMEM_LIMIT_BYTES)


def _pick(n, cands):
    for c in cands:
        if n % c == 0:
            return c
    return n


_DIMS = {"nn": (((1,), (0,)), ((), ())), "nt": (((1,), (1,)), ((), ())), "tn": (((0,), (0,)), ((), ()))}


def _mm(a, b, mode, name, out_dtype=F32, res=None):
    if mode == "nn":
        (m, k), (k2, n) = a.shape, b.shape
    elif mode == "nt":
        (m, k), (n, k2) = a.shape, b.shape
    else:
        (k, m), (k2, n) = a.shape, b.shape
    assert k == k2, (name, a.shape, b.shape)

    has_res = res is not None
    a_size, b_size = a.dtype.itemsize, b.dtype.itemsize
    o_size = jnp.dtype(out_dtype).itemsize + (res.dtype.itemsize if has_res else 0)

    def tiles(dim):
        return [c for c in range(MM_TILE_MAX, 0, -128) if dim % c == 0] or [dim]

    best = None
    for tm in tiles(m):
        for tn in tiles(n):
            a_blk, b_blk = tm * k * a_size, tn * k * b_size
            if max(a_blk, b_blk) > MM_BLOCK_BYTES or 2 * (a_blk + b_blk + tm * tn * o_size) > MM_VMEM_BYTES:
                continue
            for rows_outer in (True, False):
                moved = (m * k * a_size + (m // tm) * n * k * b_size) if rows_outer else \
                        (n * k * b_size + (n // tn) * m * k * a_size)
                key = (moved, -(tm * tn))
                if best is None or key < best[0]:
                    best = (key, tm, tn, rows_outer)
    assert best is not None, (name, a.shape, b.shape)
    _, tm, tn, rows_outer = best
    ij = (lambda g0, g1: (g0, g1)) if rows_outer else (lambda g0, g1: (g1, g0))
    if mode == "tn":
        a_spec = pl.BlockSpec((k, tm), lambda g0, g1: (0, ij(g0, g1)[0]))
    else:
        a_spec = pl.BlockSpec((tm, k), lambda g0, g1: (ij(g0, g1)[0], 0))
    if mode == "nt":
        b_spec = pl.BlockSpec((tn, k), lambda g0, g1: (ij(g0, g1)[1], 0))
    else:
        b_spec = pl.BlockSpec((k, tn), lambda g0, g1: (0, ij(g0, g1)[1]))
    o_spec = pl.BlockSpec((tm, tn), lambda g0, g1: ij(g0, g1))
    grid = (m // tm, n // tn) if rows_outer else (n // tn, m // tm)
    dims = _DIMS[mode]

    def body(*refs):
        a_ref, b_ref = refs[0], refs[1]
        o_ref = refs[-1]
        acc = lax.dot_general(a_ref[...].astype(BF16), b_ref[...].astype(BF16), dims, preferred_element_type=F32)
        if has_res:
            acc = acc + refs[2][...].astype(F32)
        o_ref[...] = acc.astype(o_ref.dtype)

    return pl.pallas_call(
        body, name=name, grid=grid,
        in_specs=[a_spec, b_spec] + ([o_spec] if has_res else []),
        out_specs=o_spec, out_shape=jax.ShapeDtypeStruct((m, n), out_dtype),
        compiler_params=_params(("parallel", "parallel")),
    )(*((a, b, res) if has_res else (a, b)))


def _row_spec(tm, bc, off, step):
    return pl.BlockSpec((tm, bc), lambda i, h: (i, off + step * h))


ROW_TILE_ELEMS = 512 * 1024


def _row_tile(t, rows):
    widest = max(bc for (_, bc, _, _) in rows)
    return _pick(t, (min(t, ROW_TILE_ELEMS // widest), 512, 256, 128, 64, 8))


def _rowwise(fn, rows, pars, outs, name, heads=1):
    t = rows[0][0].shape[0]
    tm = _row_tile(t, rows)
    nr, npar = len(rows), len(pars)

    def body(*refs):
        vals = [r[...].astype(F32) for r in refs[:nr + npar]]
        res = fn(*vals)
        if not isinstance(res, (tuple, list)):
            res = (res,)
        for o_ref, v in zip(refs[nr + npar:], res):
            o_ref[...] = v.astype(o_ref.dtype)

    in_specs = [_row_spec(tm, bc, off, st) for (_, bc, off, st) in rows]
    in_specs += [pl.BlockSpec(p.shape, lambda i, h: (0, 0)) for p in pars]
    out_specs = [_row_spec(tm, bc, 0, st) for (_, bc, st, _) in outs]
    out_shape = [jax.ShapeDtypeStruct((t, c), dt) for (c, _, _, dt) in outs]
    res = pl.pallas_call(
        body, name=name, grid=(t // tm, heads), in_specs=in_specs, out_specs=out_specs, out_shape=out_shape,
        compiler_params=_params(("parallel", "parallel")),
    )(*[r[0] for r in rows], *pars)
    return res[0] if len(res) == 1 else res


def _rowwise_vjp(fn, rows, pars, cts, name, heads=1, adds=None, row_dtypes=None):
    t = rows[0][0].shape[0]
    tm = _row_tile(t, rows)
    nr, npar, nct = len(rows), len(pars), len(cts)
    adds = adds or [None] * nr
    add_list = [a for a in adds if a is not None]
    row_dtypes = row_dtypes or [F32] * nr

    def body(*refs):
        i, h = pl.program_id(0), pl.program_id(1)
        p = 0
        row_v = [r[...].astype(F32) for r in refs[p:p + nr]]; p += nr
        par_v = [r[...].astype(F32) for r in refs[p:p + npar]]; p += npar
        ct_v = [r[...].astype(F32) for r in refs[p:p + nct]]; p += nct
        add_refs = refs[p:p + len(add_list)]; p += len(add_list)
        drow_refs = refs[p:p + nr]; p += nr
        dpar_refs = refs[p:p + npar]

        def wrapped(*a):
            r = fn(*a)
            return tuple(r) if isinstance(r, (tuple, list)) else (r,)

        _, pull = jax.vjp(wrapped, *row_v, *par_v)
        grads = pull(tuple(ct_v))
        ai = 0
        for k in range(nr):
            g = grads[k]
            if adds[k] is not None:
                g = g + add_refs[ai][...].astype(F32)
                ai += 1
            drow_refs[k][...] = g.astype(drow_refs[k].dtype)

        @pl.when((i == 0) & (h == 0))
        def _():
            for r in dpar_refs:
                r[...] = jnp.zeros(r.shape, r.dtype)

        for k in range(npar):
            dpar_refs[k][...] += grads[nr + k]

    in_specs = [_row_spec(tm, bc, off, st) for (_, bc, off, st) in rows]
    in_specs += [pl.BlockSpec(q.shape, lambda i, h: (0, 0)) for q in pars]
    in_specs += [_row_spec(tm, bc, off, st) for (_, bc, off, st) in cts]
    in_specs += [_row_spec(tm, bc, off, st) for (_, bc, off, st) in add_list]
    out_specs = [_row_spec(tm, bc, 0, st) for (_, bc, _, st) in rows]
    out_specs += [pl.BlockSpec(q.shape, lambda i, h: (0, 0)) for q in pars]
    out_shape = [jax.ShapeDtypeStruct((t, bc * (heads if st else 1)), dt) for (_, bc, _, st), dt in zip(rows, row_dtypes)]
    out_shape += [jax.ShapeDtypeStruct(q.shape, F32) for q in pars]
    res = pl.pallas_call(
        body, name=name, grid=(t // tm, heads), in_specs=in_specs, out_specs=out_specs, out_shape=out_shape,
        compiler_params=_params(("arbitrary", "arbitrary")),
    )(*[r[0] for r in rows], *pars, *[c[0] for c in cts], *[a[0] for a in add_list])
    return list(res[:nr]), list(res[nr:])


def _rms(x, g):
    return x * lax.rsqrt(jnp.mean(x * x, axis=-1, keepdims=True) + EPS) * g


def _rms_pair(x, g):
    left = lax.broadcasted_iota(jnp.int32, x.shape, 1) < HEAD_DIM
    x2 = x * x
    ms_a = jnp.sum(jnp.where(left, x2, 0.0), axis=-1, keepdims=True) * (1.0 / HEAD_DIM)
    ms_b = jnp.sum(jnp.where(left, 0.0, x2), axis=-1, keepdims=True) * (1.0 / HEAD_DIM)
    return x * lax.rsqrt(jnp.where(left, ms_a, ms_b) + EPS) * g


def _gelu(x):
    return 0.5 * x * (1.0 + jnp.tanh(math.sqrt(2.0 / math.pi) * (x + 0.044715 * (x * x * x))))


def _s5_act(ys, u, d):
    return _gelu(ys + d * u)


def _s5_gate(yg, z, b, g):
    return _rms(yg * jax.nn.sigmoid(z + b), g)


def _lane_cumsum(x, reverse):
    n = x.shape[-1]
    lane = lax.broadcasted_iota(jnp.int32, x.shape, 1)
    k = 1
    while k < n:
        if reverse:
            x = x + jnp.where(lane < n - k, pltpu.roll(x, n - k, 1), 0.0)
        else:
            x = x + jnp.where(lane >= k, pltpu.roll(x, k, 1), 0.0)
        k *= 2
    return x


def _log_sigmoid(z):
    return jnp.minimum(z, 0.0) - jnp.log(1.0 + jnp.exp(-jnp.abs(z)))


def _forget_fwd(f, bias):
    def body(f_ref, b_ref, c_ref):
        c_ref[...] = _lane_cumsum(_log_sigmoid(f_ref[...] + b_ref[...]), False)

    return pl.pallas_call(body, name="forget_fwd", out_shape=jax.ShapeDtypeStruct(f.shape, F32),
                          compiler_params=_params())(f, bias)


def _forget_bwd(f, bias, dc):
    def body(f_ref, b_ref, dc_ref, df_ref, db_ref):
        dlog = _lane_cumsum(dc_ref[...], True)
        df = dlog * jax.nn.sigmoid(-(f_ref[...] + b_ref[...]))
        df_ref[...] = df
        db_ref[...] = jnp.sum(df, axis=1, keepdims=True)

    return pl.pallas_call(body, name="forget_bwd",
                          out_shape=(jax.ShapeDtypeStruct(f.shape, F32), jax.ShapeDtypeStruct(bias.shape, F32)),
                          compiler_params=_params())(f, bias, dc)


FOX_BLOCK = 1024
FOX_KEYS = 1024
FOX_BWD_BLOCK = 512
_NT = _DIMS["nt"]
_TN = _DIMS["tn"]


N_PAIRS = N_FOX_HEADS // 2
V_BLOCK0 = 2 * N_PAIRS


def _left_lanes(shape):
    return lax.broadcasted_iota(jnp.int32, shape, 1) < HEAD_DIM


def _top_rows(shape):
    return lax.broadcasted_iota(jnp.int32, shape, 0) < HEAD_DIM


def _wide(c_tile, n):
    return c_tile if n == 128 else jnp.concatenate([c_tile] * (n // 128), axis=1)


def _fox_fwd(qn, kn, qkv, c_wide, seqs):
    t = qn.shape[0]
    l = t // seqs
    tb = min(FOX_BLOCK, l)
    tk = min(FOX_KEYS, tb)
    ratio = tb // tk
    nb = l // tb
    scale = HEAD_DIM ** -0.5

    def body(q_ref, k_ref, v_ref, ca_ref, cb_ref, o_ref, lse_ref, vt_ref):
        i = pl.program_id(2)
        top = _top_rows((128, tb))

        @pl.when(i == 0)
        def _():
            vt_ref[...] = v_ref[...].T.astype(BF16)

        qt = (q_ref[...].astype(F32) * scale).T.astype(BF16)
        zero = jnp.zeros_like(qt)
        qts = (jnp.where(top, qt, zero), jnp.where(top, zero, qt))
        top_k = _top_rows((128, tk))
        zero_k = jnp.zeros((128, tk), BF16)
        key_pos = lax.broadcasted_iota(jnp.int32, (tk, tb), 0)
        query_pos = lax.broadcasted_iota(jnp.int32, (tk, tb), 1)
        c_refs = (ca_ref, cb_ref)

        def scores(j):
            off = pl.multiple_of(j * tk, tk)
            k2 = k_ref[pl.ds(off, tk), :]
            return tuple(jnp.dot(k2, qts[h], preferred_element_type=F32) - _wide(c_refs[h][pl.ds(off, tk), :], tb)
                         for h in (0, 1))

        def values_times(ps, j):
            vt = vt_ref[:, pl.ds(pl.multiple_of(j * tk, tk), tk)]
            return (jnp.dot(jnp.where(top_k, vt, zero_k), ps[0], preferred_element_type=F32)
                    + jnp.dot(jnp.where(top_k, zero_k, vt), ps[1], preferred_element_type=F32))

        def softmax_step(sts, stats, first_key):
            ps, new, alphas = [], [], []
            for st, (m, s_sum) in zip(sts, stats):
                if first_key is not None:
                    st = jnp.where(key_pos + first_key <= query_pos, st, -jnp.inf)
                m_new = jnp.maximum(m, jnp.max(st, axis=0, keepdims=True))
                alpha = jnp.exp(m - m_new)
                p = jnp.exp(st - m_new)
                new.append((m_new, alpha * s_sum + jnp.sum(p, axis=0, keepdims=True)))
                alphas.append(alpha)
                ps.append(p.astype(BF16))
            return tuple(ps), tuple(new), jnp.where(top, alphas[0], alphas[1])

        def tile(j, carry, first_key):
            stats, acc = carry
            ps, stats, alpha = softmax_step(scores(j), stats, first_key)
            return stats, alpha * acc + values_times(ps, j)

        stat = (jnp.full((1, tb), -jnp.inf, F32), jnp.zeros((1, tb), F32))
        below = i * ratio
        carry = lax.fori_loop(0, below, lambda j, c: tile(j, c, None), ((stat, stat), jnp.zeros((128, tb), F32)))
        for r in range(ratio):
            carry = tile(below + r, carry, r * tk)
        ((ma, sa), (mb, sb)), acc = carry
        o_ref[...] = (acc / jnp.where(top, sa, sb)).T
        lse_ref[0:1, :] = ma + jnp.log(sa)
        lse_ref[1:2, :] = mb + jnp.log(sb)

    qblk = pl.BlockSpec((tb, 128), lambda b, hp, i: (b * nb + i, hp))
    return pl.pallas_call(
        body, name="fox_fwd", grid=(seqs, N_PAIRS, nb),
        in_specs=[qblk, pl.BlockSpec((l, 128), lambda b, hp, i: (b, hp)),
                  pl.BlockSpec((l, 128), lambda b, hp, i: (b, V_BLOCK0 + hp)),
                  pl.BlockSpec((None, l, 128), lambda b, hp, i: (b * N_FOX_HEADS + 2 * hp, 0, 0)),
                  pl.BlockSpec((None, l, 128), lambda b, hp, i: (b * N_FOX_HEADS + 2 * hp + 1, 0, 0))],
        out_specs=[qblk, pl.BlockSpec((None, 2, tb), lambda b, hp, i: (b * N_PAIRS + hp, 0, i))],
        out_shape=[jax.ShapeDtypeStruct((t, FOX_WIDTH), F32), jax.ShapeDtypeStruct((seqs * N_PAIRS, 2, l), F32)],
        scratch_shapes=[pltpu.VMEM((128, l), BF16)],
        compiler_params=_params(("parallel", "parallel", "arbitrary")),
    )(qn, kn, qkv, c_wide, c_wide)


def _fox_bwd(qn, kn, qkv, c_wide, o, do, lse, seqs):
    t = qn.shape[0]
    l = t // seqs
    tb = min(FOX_BWD_BLOCK, l)
    nb = l // tb
    scale = HEAD_DIM ** -0.5
    one_at = (HEAD_DIM, 0)

    def body(q_ref, k_ref, v_ref, ca_ref, cb_ref, o_ref, do_ref, lse_ref, dq_ref, dk_ref, dv_ref, dc_ref, dcq_ref,
             qt_ref, kt_ref, dot_ref, delta_ref, dqa_ref, dqb_ref):
        top_l = _top_rows((128, l))
        top = _top_rows((128, tb))
        left = _left_lanes((tb, 128))
        row_id = lax.broadcasted_iota(jnp.int32, (128, tb), 0)
        lane_id = lax.broadcasted_iota(jnp.int32, (tb, 128), 1)
        zero_t = jnp.zeros((128, tb), BF16)
        zero_l = jnp.zeros((tb, 128), BF16)
        rows = lambda a: (jnp.where(top, a, zero_t), jnp.where(top, zero_t, a))
        lanes = lambda a: (jnp.where(left, a, zero_l), jnp.where(left, zero_l, a))
        with_one_row = lambda pair: tuple(jnp.where(row_id == one_at[h], 1.0, pair[h]).astype(BF16) for h in (0, 1))
        with_one_lane = lambda pair: tuple(jnp.where(lane_id == one_at[h], 1.0, pair[h]).astype(BF16) for h in (0, 1))
        causal = lax.broadcasted_iota(jnp.int32, (tb, tb), 0) <= lax.broadcasted_iota(jnp.int32, (tb, tb), 1)
        c_refs = (ca_ref, cb_ref)
        dq_refs = (dqa_ref, dqb_ref)

        qt_ref[...] = (q_ref[...].astype(F32) * scale).T.astype(BF16)
        kt_ref[...] = k_ref[...].astype(F32).T.astype(BF16)
        do_t = do_ref[...].T
        dot_ref[...] = do_t.astype(BF16)
        prod_t = do_t * o_ref[...].T
        delta_ref[0:1, :] = jnp.sum(jnp.where(top_l, prod_t, 0.0), axis=0, keepdims=True)
        delta_ref[1:2, :] = jnp.sum(jnp.where(top_l, 0.0, prod_t), axis=0, keepdims=True)
        dqa_ref[...] = jnp.zeros(dqa_ref.shape, F32)
        dqb_ref[...] = jnp.zeros(dqb_ref.shape, F32)

        def kv_block(j, _):
            koff = pl.multiple_of(j * tb, tb)
            k2 = k_ref[pl.ds(koff, tb), :]
            v2 = v_ref[pl.ds(koff, tb), :].astype(BF16)
            kts = with_one_row(rows(kt_ref[:, pl.ds(koff, tb)]))
            cw = tuple(_wide(c_refs[h][pl.ds(koff, tb), :], tb) for h in (0, 1))

            def q_block(i, carry, masked):
                dks, dv = list(carry[:2]), carry[2]
                qoff = pl.multiple_of(i * tb, tb)
                qs = lanes((q_ref[pl.ds(qoff, tb), :].astype(F32) * scale).astype(BF16))
                qs_one = with_one_lane(qs)
                dos = lanes(do_ref[pl.ds(qoff, tb), :].astype(BF16))
                qts = rows(qt_ref[:, pl.ds(qoff, tb)])
                dots = rows(dot_ref[:, pl.ds(qoff, tb)])
                for h in (0, 1):
                    st = jnp.dot(k2, qts[h], preferred_element_type=F32) - cw[h]
                    p = jnp.exp(st - lse_ref[h:h + 1, pl.ds(qoff, tb)])
                    if masked:
                        p = jnp.where(causal, p, 0.0)
                    dp = jnp.dot(v2, dots[h], preferred_element_type=F32)
                    dsb = (p * (dp - delta_ref[h:h + 1, pl.ds(qoff, tb)])).astype(BF16)
                    dv = dv + jnp.dot(p.astype(BF16), dos[h], preferred_element_type=F32)
                    dks[h] = dks[h] + jnp.dot(dsb, qs_one[h], preferred_element_type=F32)
                    dq_refs[h][:, pl.ds(qoff, tb)] += jnp.dot(kts[h], dsb, preferred_element_type=F32)
                return dks[0], dks[1], dv

            z = jnp.zeros((tb, 128), F32)
            carry = q_block(j, (z, z, z), True)
            rest = nb - 1 - j
            carry = lax.fori_loop(
                0, rest // 2, lambda n, c: q_block(j + 2 + 2 * n, q_block(j + 1 + 2 * n, c, False), False), carry)
            dka, dkb, dv = lax.cond(rest % 2 == 1, lambda c: q_block(nb - 1, c, False), lambda c: c, carry)
            dk_ref[pl.ds(koff, tb), :] = jnp.where(left, dka, dkb)
            dv_ref[pl.ds(koff, tb), :] = dv
            dc_ref[0:1, pl.ds(koff, tb)] = -dka.T[one_at[0]:one_at[0] + 1, :]
            dc_ref[1:2, pl.ds(koff, tb)] = -dkb.T[one_at[1]:one_at[1] + 1, :]
            return 0

        lax.fori_loop(0, nb, kv_block, 0)
        dq_ref[...] = (jnp.where(top_l, dqa_ref[...], dqb_ref[...]) * scale).T
        dcq_ref[0:1, :] = dqa_ref[one_at[0]:one_at[0] + 1, :]
        dcq_ref[1:2, :] = dqb_ref[one_at[1]:one_at[1] + 1, :]

    blk = pl.BlockSpec((l, 128), lambda b, hp: (b, hp))
    cspec = lambda k: pl.BlockSpec((None, l, 128), lambda b, hp: (b * N_FOX_HEADS + 2 * hp + k, 0, 0))
    rows2 = pl.BlockSpec((None, 2, l), lambda b, hp: (b * N_PAIRS + hp, 0, 0))
    wide = jax.ShapeDtypeStruct((t, FOX_WIDTH), F32)
    pair_rows = jax.ShapeDtypeStruct((seqs * N_PAIRS, 2, l), F32)
    return pl.pallas_call(
        body, name="fox_bwd", grid=(seqs, N_PAIRS),
        in_specs=[blk, blk, pl.BlockSpec((l, 128), lambda b, hp: (b, V_BLOCK0 + hp)), cspec(0), cspec(1), blk, blk, rows2],
        out_specs=[blk, blk, blk, rows2, rows2],
        out_shape=[wide, wide, wide, pair_rows, pair_rows],
        scratch_shapes=[pltpu.VMEM((128, l), BF16), pltpu.VMEM((128, l), BF16), pltpu.VMEM((128, l), BF16),
                        pltpu.VMEM((2, l), F32), pltpu.VMEM((128, l), F32), pltpu.VMEM((128, l), F32)],
        compiler_params=_params(("parallel", "parallel")),
    )(qn, kn, qkv, c_wide, c_wide, o, do, lse)


SCAN_ROWS = 512
SCAN_COLS = 1024


S5_IN = 128
S5_ST = 512
SCAN_CHUNKS = SCAN_COLS // S5_ST
SCAN_SEGS = 8
LANES = 128


def _cmul(ar, ai, br, bi):
    return ar * br - ai * bi, ar * bi + ai * br


def _powers_into(pw_r, pw_i, a_r, a_i, seg):
    pw_r[0:1, :] = a_r
    pw_i[0:1, :] = a_i
    for k in range(1, seg):
        pr, pi = _cmul(pw_r[k - 1:k, :], pw_i[k - 1:k, :], a_r, a_i)
        pw_r[k:k + 1, :] = pr
        pw_i[k:k + 1, :] = pi


def _interleave(dst, src, seg):
    for h in range(src.shape[0]):
        for j in range(seg):
            dst[h, j * SCAN_SEGS:(j + 1) * SCAN_SEGS, :] = src[h, pl.ds(j, SCAN_SEGS, stride=seg), :]


def _deinterleave(dst, src, seg):
    for h in range(src.shape[0]):
        for j in range(seg):
            dst[h, pl.ds(j, SCAN_SEGS, stride=seg), :] = src[h, j * SCAN_SEGS:(j + 1) * SCAN_SEGS, :]


def _interleaved(ref, tmp_a, tmp_b, seg):
    n = ref.shape[1] // LANES
    for h in range(n):
        tmp_a[h] = ref[:, h * LANES:(h + 1) * LANES].astype(F32)
    _interleave(tmp_b, tmp_a, seg)
    return jnp.concatenate([tmp_b[h] for h in range(n)], axis=1)


def _store_deinterleaved(ref, val, tmp_a, tmp_b, seg):
    n = ref.shape[1] // LANES
    for h in range(n):
        tmp_a[h] = val[:, h * LANES:(h + 1) * LANES]
    _deinterleave(tmp_b, tmp_a, seg)
    for h in range(n):
        ref[:, h * LANES:(h + 1) * LANES] = tmp_b[h]


def _segment_scan(b_r, b_i, x_r, x_i, pw_r, pw_i, car_r, car_i, seg, sign, reverse, visit=None):
    nc = b_r.shape[0]
    sub = lax.broadcasted_iota(jnp.int32, (SCAN_SEGS, LANES), 0)
    lanes = lambda c: slice(c * LANES, (c + 1) * LANES)
    rows = lambda j: pl.ds(pl.multiple_of(((seg - 1 - j) if reverse else j) * SCAN_SEGS, SCAN_SEGS), SCAN_SEGS)
    a1 = [(pw_r[0:1, lanes(c)], sign * pw_i[0:1, lanes(c)]) for c in range(nc)]

    def local(j, xs):
        out = []
        for c in range(nc):
            xr, xi = xs[2 * c], xs[2 * c + 1]
            nr = a1[c][0] * xr - a1[c][1] * xi + b_r[c, rows(j), :]
            ni = a1[c][0] * xi + a1[c][1] * xr + b_i[c, rows(j), :]
            x_r[c, rows(j), :] = nr
            x_i[c, rows(j), :] = ni
            out += [nr, ni]
        return tuple(out)

    zero = jnp.zeros((SCAN_SEGS, LANES), F32)
    ends = lax.fori_loop(0, seg, local, (zero,) * (2 * nc))

    if reverse:
        first = sub == SCAN_SEGS - 1
        neighbour = lambda v: pltpu.roll(v, SCAN_SEGS - 1, 0)
        shift = lambda v, d: jnp.where(sub < SCAN_SEGS - d, pltpu.roll(v, SCAN_SEGS - d, 0), 0.0)
    else:
        first = sub == 0
        neighbour = lambda v: pltpu.roll(v, 1, 0)
        shift = lambda v, d: jnp.where(sub >= d, pltpu.roll(v, d, 0), 0.0)
    last = 0 if reverse else SCAN_SEGS - 1
    entries = []
    for c in range(nc):
        er, ei = ends[2 * c], ends[2 * c + 1]
        pr, pi = pw_r[seg - 1:seg, lanes(c)], sign * pw_i[seg - 1:seg, lanes(c)]
        yr = jnp.where(first, car_r[:, lanes(c)], neighbour(er))
        yi = jnp.where(first, car_i[:, lanes(c)], neighbour(ei))
        qr, qi = pr, pi
        for d in (1, 2, 4):
            mr, mi = _cmul(qr, qi, shift(yr, d), shift(yi, d))
            yr, yi = yr + mr, yi + mi
            qr, qi = _cmul(qr, qi, qr, qi)
        lr, li = _cmul(pr, pi, yr, yi)
        car_r[:, lanes(c)] = (er + lr)[last:last + 1, :]
        car_i[:, lanes(c)] = (ei + li)[last:last + 1, :]
        entries += [yr, yi]

    def correct(j, prev):
        out = []
        row_r, row_i = pw_r[pl.ds(j, 1), :], sign * pw_i[pl.ds(j, 1), :]
        for c in range(nc):
            mr, mi = _cmul(row_r[:, lanes(c)], row_i[:, lanes(c)], entries[2 * c], entries[2 * c + 1])
            nr = x_r[c, rows(j), :] + mr
            ni = x_i[c, rows(j), :] + mi
            x_r[c, rows(j), :] = nr
            x_i[c, rows(j), :] = ni
            if visit is not None:
                visit(c, rows(j), prev[2 * c], prev[2 * c + 1])
            out += [nr, ni]
        return tuple(out)

    lax.fori_loop(0, seg, correct, tuple(entries))


def _s5_fwd(uf, bbr, bbi, cr, ci, ar, ai, seqs):
    t = uf.shape[0]
    l = t // seqs
    tl = min(SCAN_ROWS, l)
    nl = l // tl
    seg = tl // SCAN_SEGS
    cb, nq = SCAN_COLS, SCAN_CHUNKS
    nc = cb // LANES
    per = S5_ST // LANES

    def body(u_ref, bbr_ref, bbi_ref, cr_ref, ci_ref, ar_ref, ai_ref, x_r, x_i, ys_ref,
             car_r, car_i, pw_r, pw_i, b_r, b_i, tmp_a, tmp_b):
        @pl.when(pl.program_id(2) == 0)
        def _():
            car_r[...] = jnp.zeros(car_r.shape, F32)
            car_i[...] = jnp.zeros(car_i.shape, F32)
            _powers_into(pw_r, pw_i, ar_ref[...], ai_ref[...], seg)

        u = _interleaved(u_ref, tmp_a, tmp_b, seg).astype(BF16)
        for q in range(nq):
            uq = u[:, q * S5_IN:(q + 1) * S5_IN]
            br = jnp.dot(uq, bbr_ref[q], preferred_element_type=F32)
            bi = jnp.dot(uq, bbi_ref[q], preferred_element_type=F32)
            for s in range(per):
                b_r[q * per + s] = br[:, s * LANES:(s + 1) * LANES]
                b_i[q * per + s] = bi[:, s * LANES:(s + 1) * LANES]
        _segment_scan(b_r, b_i, x_r, x_i, pw_r, pw_i, car_r, car_i, seg, 1.0, False)
        wide = lambda buf, q: jnp.concatenate([buf[q * per + s] for s in range(per)], axis=1).astype(BF16)
        ys = [jnp.dot(wide(x_r, q), cr_ref[q], preferred_element_type=F32)
              + jnp.dot(wide(x_i, q), ci_ref[q], preferred_element_type=F32) for q in range(nq)]
        _store_deinterleaved(ys_ref, jnp.concatenate(ys, axis=1), tmp_a, tmp_b, seg)

    rows = lambda w: pl.BlockSpec((tl, w), lambda s, j, r: (s * nl + r, j))
    state = pl.BlockSpec((nc, tl, LANES), lambda s, j, r: (j, s * nl + r, 0))
    chunk = lambda a: pl.BlockSpec((nq,) + a.shape[1:], lambda s, j, r: (j, 0, 0))
    par = pl.BlockSpec((1, cb), lambda s, j, r: (0, j))
    return pl.pallas_call(
        body, name="s5_fwd", grid=(seqs, S5_CH // cb, nl),
        in_specs=[rows(nq * S5_IN), chunk(bbr), chunk(bbi), chunk(cr), chunk(ci), par, par],
        out_specs=[state, state, rows(nq * S5_IN)],
        out_shape=[jax.ShapeDtypeStruct((S5_CH // LANES, t, LANES), F32)] * 2
        + [jax.ShapeDtypeStruct((t, S5_WIDTH), F32)],
        scratch_shapes=[pltpu.VMEM((1, cb), F32), pltpu.VMEM((1, cb), F32), pltpu.VMEM((seg, cb), F32),
                        pltpu.VMEM((seg, cb), F32)] + [pltpu.VMEM((nc, tl, LANES), F32)] * 2
        + [pltpu.VMEM((nq * S5_IN // LANES, tl, LANES), F32)] * 2,
        compiler_params=_params(("parallel", "parallel", "arbitrary")),
    )(uf, bbr, bbi, cr, ci, ar, ai)


def _s5_bwd(dys, uf, xr, xi, bbr, bbi, cr, ci, ar, ai, seqs):
    t = dys.shape[0]
    l = t // seqs
    tl = min(SCAN_ROWS, l)
    nl = l // tl
    seg = tl // SCAN_SEGS
    cb, nq = SCAN_COLS, SCAN_CHUNKS
    nc = cb // LANES
    per = S5_ST // LANES

    def body(dy_ref, u_ref, x_r, x_i, bbr_ref, bbi_ref, cr_ref, ci_ref, ar_ref, ai_ref,
             du_ref, dbbr_ref, dbbi_ref, dcr_ref, dci_ref, dar_ref, dai_ref,
             car_r, car_i, pw_r, pw_i, g_r, g_i, lam_r, lam_i, acc_r, acc_i, tmp_a, tmp_b):
        @pl.when(pl.program_id(2) == 0)
        def _():
            car_r[...] = jnp.zeros(car_r.shape, F32)
            car_i[...] = jnp.zeros(car_i.shape, F32)
            _powers_into(pw_r, pw_i, ar_ref[...], ai_ref[...], seg)
            for acc_ref in (dbbr_ref, dbbi_ref, dcr_ref, dci_ref, dar_ref, dai_ref):
                acc_ref[...] = jnp.zeros(acc_ref.shape, F32)

        dy = _interleaved(dy_ref, tmp_a, tmp_b, seg).astype(BF16)
        for q in range(nq):
            dyq = dy[:, q * S5_IN:(q + 1) * S5_IN]
            gr = lax.dot_general(dyq, cr_ref[q], _NT, preferred_element_type=F32)
            gi = lax.dot_general(dyq, ci_ref[q], _NT, preferred_element_type=F32)
            for s in range(per):
                g_r[q * per + s] = gr[:, s * LANES:(s + 1) * LANES]
                g_i[q * per + s] = gi[:, s * LANES:(s + 1) * LANES]
        acc_r[...] = jnp.zeros(acc_r.shape, F32)
        acc_i[...] = jnp.zeros(acc_i.shape, F32)

        def visit(c, rws, lr, li):
            xr_t, xi_t = x_r[c, rws, :], x_i[c, rws, :]
            acc_r[c] += lr * xr_t + li * xi_t
            acc_i[c] += li * xr_t - lr * xi_t

        _segment_scan(g_r, g_i, lam_r, lam_i, pw_r, pw_i, car_r, car_i, seg, -1.0, True, visit)
        for c in range(nc):
            dar_ref[:, c * LANES:(c + 1) * LANES] += jnp.sum(acc_r[c], axis=0, keepdims=True)
            dai_ref[:, c * LANES:(c + 1) * LANES] += jnp.sum(acc_i[c], axis=0, keepdims=True)
        u = _interleaved(u_ref, tmp_a, tmp_b, seg).astype(BF16)
        wide = lambda buf, q: jnp.concatenate([buf[q * per + s] for s in range(per)], axis=1).astype(BF16)
        du = []
        for q in range(nq):
            io = slice(q * S5_IN, (q + 1) * S5_IN)
            lq_r, lq_i = wide(lam_r, q), wide(lam_i, q)
            du.append(lax.dot_general(lq_r, bbr_ref[q], _NT, preferred_element_type=F32)
                      + lax.dot_general(lq_i, bbi_ref[q], _NT, preferred_element_type=F32))
            dbbr_ref[q] += lax.dot_general(u[:, io], lq_r, _TN, preferred_element_type=F32)
            dbbi_ref[q] += lax.dot_general(u[:, io], lq_i, _TN, preferred_element_type=F32)
            dcr_ref[q] += lax.dot_general(wide(x_r, q), dy[:, io], _TN, preferred_element_type=F32)
            dci_ref[q] += lax.dot_general(wide(x_i, q), dy[:, io], _TN, preferred_element_type=F32)
        _store_deinterleaved(du_ref, jnp.concatenate(du, axis=1), tmp_a, tmp_b, seg)

    rows = lambda w: pl.BlockSpec((tl, w), lambda s, j, r: (s * nl + nl - 1 - r, j))
    state = pl.BlockSpec((nc, tl, LANES), lambda s, j, r: (j, s * nl + nl - 1 - r, 0))
    chunk = lambda a: pl.BlockSpec((nq,) + a.shape[1:], lambda s, j, r: (j, 0, 0))
    acc = lambda a: pl.BlockSpec((None, nq) + a.shape[1:], lambda s, j, r: (s, j, 0, 0))
    par = pl.BlockSpec((1, cb), lambda s, j, r: (0, j))
    par_acc = pl.BlockSpec((None, 1, cb), lambda s, j, r: (s, 0, j))
    per_seq = lambda a: jax.ShapeDtypeStruct((seqs,) + a.shape, F32)
    return pl.pallas_call(
        body, name="s5_bwd", grid=(seqs, S5_CH // cb, nl),
        in_specs=[rows(nq * S5_IN), rows(nq * S5_IN), state, state, chunk(bbr), chunk(bbi), chunk(cr), chunk(ci),
                  par, par],
        out_specs=[rows(nq * S5_IN), acc(bbr), acc(bbi), acc(cr), acc(ci), par_acc, par_acc],
        out_shape=[jax.ShapeDtypeStruct((t, S5_WIDTH), F32), per_seq(bbr), per_seq(bbi), per_seq(cr), per_seq(ci),
                   jax.ShapeDtypeStruct((seqs, 1, S5_CH), F32), jax.ShapeDtypeStruct((seqs, 1, S5_CH), F32)],
        scratch_shapes=[pltpu.VMEM((1, cb), F32), pltpu.VMEM((1, cb), F32), pltpu.VMEM((seg, cb), F32),
                        pltpu.VMEM((seg, cb), F32)] + [pltpu.VMEM((nc, tl, LANES), F32)] * 4
        + [pltpu.VMEM((nc, SCAN_SEGS, LANES), F32)] * 2 + [pltpu.VMEM((nq * S5_IN // LANES, tl, LANES), F32)] * 2,
        compiler_params=_params(("parallel", "parallel", "arbitrary")),
    )(dys, uf, xr, xi, bbr, bbi, cr, ci, ar, ai)


XATT_BLOCK = 2048


def _xatt_probs(qv, kv):
    s = lax.dot_general(qv, kv, _NT, preferred_element_type=F32) * (X_HEAD_DIM ** -0.5)
    e = jnp.exp(s - jnp.max(s, axis=-1, keepdims=True))
    return e / jnp.sum(e, axis=-1, keepdims=True)


def _xatt_fwd(q, k, kv, seqs):
    t = q.shape[0]
    tq = min(XATT_BLOCK, t // seqs)
    nq = t // seqs // tq

    def body(q_ref, k_ref, v_ref, o_ref):
        p = _xatt_probs(q_ref[...], k_ref[...])
        o_ref[...] = jnp.dot(p.astype(BF16), v_ref[...].astype(BF16), preferred_element_type=F32).astype(o_ref.dtype)

    qs = pl.BlockSpec((tq, X_HEAD_DIM), lambda b, h, i: (b * nq + i, h))
    return pl.pallas_call(
        body, name="xatt_fwd", grid=(seqs, N_X_HEADS, nq),
        in_specs=[qs, pl.BlockSpec((N_MEM, X_HEAD_DIM), lambda b, h, i: (b, h)),
                  pl.BlockSpec((N_MEM, X_HEAD_DIM), lambda b, h, i: (b, N_X_HEADS + h))],
        out_specs=qs, out_shape=jax.ShapeDtypeStruct(q.shape, BF16),
        compiler_params=_params(("parallel", "parallel", "parallel")),
    )(q, k, kv)


def _xatt_bwd(q, k, kv, do, seqs):
    t = q.shape[0]
    tq = min(XATT_BLOCK, t // seqs)
    nq = t // seqs // tq
    scale = X_HEAD_DIM ** -0.5

    def body(q_ref, k_ref, v_ref, do_ref, dq_ref, dk_ref, dv_ref):
        @pl.when(pl.program_id(2) == 0)
        def _():
            dk_ref[...] = jnp.zeros(dk_ref.shape, F32)
            dv_ref[...] = jnp.zeros(dv_ref.shape, F32)

        qv, kk = q_ref[...], k_ref[...]
        p = _xatt_probs(qv, kk)
        dob = do_ref[...].astype(BF16)
        dp = lax.dot_general(dob, v_ref[...].astype(BF16), _NT, preferred_element_type=F32)
        ds = p * (dp - jnp.sum(dp * p, axis=-1, keepdims=True))
        dsb = ds.astype(BF16)
        dq_ref[...] = jnp.dot(dsb, kk, preferred_element_type=F32) * scale
        dk_ref[...] += lax.dot_general(dsb, qv, _TN, preferred_element_type=F32) * scale
        dv_ref[...] += lax.dot_general(p.astype(BF16), dob, _TN, preferred_element_type=F32)

    qs = pl.BlockSpec((tq, X_HEAD_DIM), lambda b, h, i: (b * nq + i, h))
    ks = pl.BlockSpec((N_MEM, X_HEAD_DIM), lambda b, h, i: (b, h))
    return pl.pallas_call(
        body, name="xatt_bwd", grid=(seqs, N_X_HEADS, nq),
        in_specs=[qs, ks, pl.BlockSpec((N_MEM, X_HEAD_DIM), lambda b, h, i: (b, N_X_HEADS + h)), qs],
        out_specs=[qs, ks, ks],
        out_shape=[jax.ShapeDtypeStruct(q.shape, F32), jax.ShapeDtypeStruct(k.shape, F32),
                   jax.ShapeDtypeStruct(k.shape, F32)],
        compiler_params=_params(("parallel", "parallel", "arbitrary")),
    )(q, k, kv, do)


CONV_COLS = 256


def _shift_down(x, k, row):
    return jnp.where(row >= k, pltpu.roll(x, k, 0), 0.0)


def _shift_up(x, k, row):
    n = x.shape[0]
    return jnp.where(row < n - k, pltpu.roll(x, n - k, 0), 0.0)


def _down_from(x, prev, k, row):
    return jnp.where(row >= k, pltpu.roll(x, k, 0), pltpu.roll(prev, k, 0))


GATE_ROWS = 1024


def _ffn_up_gate(hn, w_up, w, b, seqs):
    t = hn.shape[0]
    l = t // seqs
    nc = D_FF // CONV_COLS

    rc = min(GATE_ROWS, l)

    def body(a_ref, wg_ref, wu_ref, w_ref, b_ref, g_ref, u_ref, p_ref, o_ref):
        wv, bias = w_ref[...], b_ref[...]
        row = lax.broadcasted_iota(jnp.int32, (rc, CONV_COLS), 0)
        prev = jnp.zeros((rc, CONV_COLS), F32)
        for k in range(l // rc):
            rows = slice(k * rc, (k + 1) * rc)
            a = a_ref[rows, :]
            gb = jnp.dot(a, wg_ref[...], preferred_element_type=F32).astype(BF16)
            ub = jnp.dot(a, wu_ref[...], preferred_element_type=F32).astype(BF16)
            g_ref[rows, :] = gb
            u_ref[rows, :] = ub
            g = gb.astype(F32)
            pre = bias + wv[0:1, :] * _down_from(g, prev, 2, row) + wv[1:2, :] * _down_from(g, prev, 1, row) \
                + wv[2:3, :] * g
            p_ref[rows, :] = pre.astype(p_ref.dtype)
            o_ref[rows, :] = (pre * jax.nn.sigmoid(pre) * ub.astype(F32)).astype(o_ref.dtype)
            prev = g

    cols = pl.BlockSpec((l, CONV_COLS), lambda s, j: (s, j))
    half = jax.ShapeDtypeStruct((t, D_FF), BF16)
    return pl.pallas_call(
        body, name="ffn_up_gate", grid=(seqs, nc),
        in_specs=[pl.BlockSpec((l, hn.shape[1]), lambda s, j: (s, 0)),
                  pl.BlockSpec((hn.shape[1], CONV_COLS), lambda s, j: (0, j)),
                  pl.BlockSpec((hn.shape[1], CONV_COLS), lambda s, j: (0, nc + j)),
                  pl.BlockSpec((3, CONV_COLS), lambda s, j: (0, j)), pl.BlockSpec((1, CONV_COLS), lambda s, j: (0, j))],
        out_specs=[cols] * 4, out_shape=[half] * 4,
        compiler_params=_params(("parallel", "parallel")),
    )(hn, w_up, w_up, w, b)


def _ffn_down_dx_gate(dh, w_down, gate, up, pre, w, seqs):
    t = dh.shape[0]
    l = t // seqs
    nc = D_FF // CONV_COLS
    steps = nc * seqs

    def body(dh_ref, wd_ref, g_ref, u_ref, p_ref, w_ref, dgu_ref, dw_ref, db_ref, stage, sems):
        s, j = pl.program_id(0), pl.program_id(1)
        n = s * nc + j
        slot = n % 2

        def copies(slot_, j_, s_):
            rows = pl.ds(pl.multiple_of(s_ * l, 16), l)
            return [pltpu.make_async_copy(
                stage.at[slot_, half],
                dgu_ref.at[rows, pl.ds(pl.multiple_of((half * nc + j_) * CONV_COLS, 128), CONV_COLS)],
                sems.at[slot_, half]) for half in (0, 1)]

        @pl.when(n >= 2)
        def _():
            for cp in copies(slot, j, s):
                cp.wait()

        da = lax.dot_general(dh_ref[...], wd_ref[...], _NT, preferred_element_type=F32)
        g, pre, wv = g_ref[...].astype(F32), p_ref[...].astype(F32), w_ref[...]
        row = lax.broadcasted_iota(jnp.int32, g.shape, 0)
        sg = jax.nn.sigmoid(pre)
        silu = pre * sg
        stage[slot, 1] = (da * silu).astype(stage.dtype)
        dpre = da * u_ref[...].astype(F32) * (sg * (1.0 + pre * (1.0 - sg)))
        dpre1, dpre2 = _shift_up(dpre, 1, row), _shift_up(dpre, 2, row)
        dg = wv[2:3, :] * dpre + wv[1:2, :] * dpre1 + wv[0:1, :] * dpre2
        stage[slot, 0] = dg.astype(stage.dtype)
        for cp in copies(slot, j, s):
            cp.start()
        dw_ref[0:1, :] = jnp.sum(dpre2 * g, axis=0, keepdims=True)
        dw_ref[1:2, :] = jnp.sum(dpre1 * g, axis=0, keepdims=True)
        dw_ref[2:3, :] = jnp.sum(dpre * g, axis=0, keepdims=True)
        db_ref[...] = jnp.sum(dpre, axis=0, keepdims=True)

        @pl.when(n == steps - 1)
        def _():
            for cp in copies(slot, j, s) + (copies(1 - slot, j, s) if steps > 1 else []):
                cp.wait()

    cols = pl.BlockSpec((l, CONV_COLS), lambda s, j: (s, j))
    return pl.pallas_call(
        body, name="ffn_down_dx_gate", grid=(seqs, nc),
        in_specs=[pl.BlockSpec((l, dh.shape[1]), lambda s, j: (s, 0)),
                  pl.BlockSpec((CONV_COLS, dh.shape[1]), lambda s, j: (j, 0)), cols, cols, cols,
                  pl.BlockSpec((3, CONV_COLS), lambda s, j: (0, j))],
        out_specs=[ANY, pl.BlockSpec((None, 3, CONV_COLS), lambda s, j: (s, 0, j)),
                   pl.BlockSpec((None, 1, CONV_COLS), lambda s, j: (s, 0, j))],
        out_shape=[jax.ShapeDtypeStruct((t, 2 * D_FF), BF16), jax.ShapeDtypeStruct((seqs, 3, D_FF), F32),
                   jax.ShapeDtypeStruct((seqs, 1, D_FF), F32)],
        scratch_shapes=[pltpu.VMEM((2, 2, l, CONV_COLS), BF16), pltpu.SemaphoreType.DMA((2, 2))],
        compiler_params=_params(("arbitrary", "arbitrary")),
    )(dh, w_down, gate, up, pre, w)


def _loss_head(h, target):
    t, d = h.shape
    tm = _pick(t, (256, 128, 8))

    def body(h_ref, t_ref, dh_ref, dhb_ref, loss_ref):
        @pl.when(pl.program_id(0) == 0)
        def _():
            loss_ref[...] = jnp.zeros(loss_ref.shape, F32)

        e = h_ref[...] - t_ref[...]
        dh = e * (1.0 / d)
        dh_ref[...] = dh
        dhb_ref[...] = dh.astype(BF16)
        loss_ref[...] += (0.5 / d) * jnp.sum(jnp.sum(e * e, axis=1, keepdims=True), axis=0, keepdims=True)

    blk = pl.BlockSpec((tm, d), lambda i: (i, 0))
    return pl.pallas_call(
        body, name="loss_head", grid=(t // tm,), in_specs=[blk, blk],
        out_specs=[blk, blk, pl.BlockSpec((1, 1), lambda i: (0, 0))],
        out_shape=[jax.ShapeDtypeStruct((t, d), F32), jax.ShapeDtypeStruct((t, d), BF16),
                   jax.ShapeDtypeStruct((1, 1), F32)],
        compiler_params=_params(("arbitrary",)),
    )(h, target)


def _s5_discretise(a_re, a_im, log_dt, b_re, b_im):
    dt = jnp.exp(log_dt)[:, None]
    mag = jnp.exp(a_re * dt)
    lb_r = mag * jnp.cos(a_im * dt)
    lb_i = mag * jnp.sin(a_im * dt)
    den = a_re * a_re + a_im * a_im
    nr = lb_r - 1.0
    coef_r = (nr * a_re + lb_i * a_im) / den
    coef_i = (lb_i * a_re - nr * a_im) / den
    bb_r = coef_r[:, :, None] * b_re - coef_i[:, :, None] * b_im
    bb_i = coef_r[:, :, None] * b_im + coef_i[:, :, None] * b_re
    return lb_r, lb_i, bb_r, bb_i


S5_CHUNKS = 4
S5_PER = S5_GROUPS // S5_CHUNKS


def _blockdiag_in(bb):
    eye = jnp.eye(S5_PER, dtype=bb.dtype)
    return jnp.einsum("jgpc,gh->jgchp", bb.reshape(S5_CHUNKS, S5_PER, S5_STATE, S5_GROUP_CH), eye).reshape(
        S5_CHUNKS, S5_PER * S5_GROUP_CH, S5_PER * S5_STATE)


def _blockdiag_in_grad(d):
    eye = jnp.eye(S5_PER, dtype=d.dtype)
    return jnp.einsum("jgchp,gh->jgpc", d.reshape(S5_CHUNKS, S5_PER, S5_GROUP_CH, S5_PER, S5_STATE), eye).reshape(
        S5_GROUPS, S5_STATE, S5_GROUP_CH)


def _blockdiag_out(c):
    eye = jnp.eye(S5_PER, dtype=c.dtype)
    return jnp.einsum("jgcp,gh->jgphc", c.reshape(S5_CHUNKS, S5_PER, S5_GROUP_CH, S5_STATE), eye).reshape(
        S5_CHUNKS, S5_PER * S5_STATE, S5_PER * S5_GROUP_CH)


def _blockdiag_out_grad(d):
    eye = jnp.eye(S5_PER, dtype=d.dtype)
    return jnp.einsum("jgphc,gh->jgcp", d.reshape(S5_CHUNKS, S5_PER, S5_STATE, S5_PER, S5_GROUP_CH), eye).reshape(
        S5_GROUPS, S5_GROUP_CH, S5_STATE)


def _local_step(x3, mem3, target3, p, wb, late_weights=None, early_grads=None):
    seqs, l, d = x3.shape
    t = seqs * l
    x = x3.reshape(t, d)
    mem = mem3.reshape(seqs * N_MEM, d)
    target = target3.reshape(t, d)
    full = lambda a: (a, a.shape[1], 0, 0)

    s5_in = (p["s5_a_re"], p["s5_a_im"], p["s5_log_dt"], p["s5_b_re"], p["s5_b_im"])
    (lb_r, lb_i, bb_r, bb_i), s5_pull = jax.vjp(_s5_discretise, *s5_in)
    ar, ai = lb_r.reshape(1, S5_CH), lb_i.reshape(1, S5_CH)
    bbr_d, bbi_d = _blockdiag_in(bb_r).astype(BF16), _blockdiag_in(bb_i).astype(BF16)
    cr_d, ci_d = _blockdiag_out(p["s5_c_re"]).astype(BF16), (-_blockdiag_out(p["s5_c_im"])).astype(BF16)
    d_row = p["s5_d"].reshape(1, S5_WIDTH)

    hn1 = _rowwise(_rms, [full(x)], [p["norm_mix"]], [(d, d, 0, BF16)], "norm_mix_fwd")
    if late_weights is not None:
        wb = dict(wb, **late_weights("first", hn1))
    conv_w = wb["ffn_conv_w"] if "ffn_conv_w" in wb else p["ffn_conv_w"]
    w_in = wb["w_in"]
    w_qkv = w_in[:, :3 * FOX_WIDTH]
    w_uf = jnp.concatenate(
        [w_in[:, 3 * FOX_WIDTH + N_FOX_HEADS:], w_in[:, 3 * FOX_WIDTH:3 * FOX_WIDTH + N_FOX_HEADS],
         jnp.zeros((d, UF_COLS - S5_WIDTH - N_FOX_HEADS), w_in.dtype)], axis=1)
    qkv = _mm(hn1, w_qkv, "nn", "in_qkv")
    uf = _mm(hn1, w_uf, "nn", "in_uf")

    bh = seqs * N_FOX_HEADS
    q_pair = (qkv, 128, 0, 1)
    k_pair = (qkv, 128, N_PAIRS, 1)
    gq2, gk2 = jnp.tile(p["fox_q_norm"], (1, 2)), jnp.tile(p["fox_k_norm"], (1, 2))
    pair_out = [(FOX_WIDTH, 128, 1, BF16)]
    qn = _rowwise(_rms_pair, [q_pair], [gq2], pair_out, "fox_qnorm_fwd", heads=N_PAIRS)
    kn = _rowwise(_rms_pair, [k_pair], [gk2], pair_out, "fox_knorm_fwd", heads=N_PAIRS)

    f_rows = uf[:, S5_WIDTH:S5_WIDTH + N_FOX_HEADS].reshape(seqs, l, N_FOX_HEADS).transpose(0, 2, 1).reshape(bh, l)
    f_bias = jnp.tile(p["fox_f_bias"].reshape(N_FOX_HEADS, 1), (seqs, 1))
    c_wide = jnp.broadcast_to(_forget_fwd(f_rows, f_bias)[:, :, None], (bh, l, 128))
    fox, lse = _fox_fwd(qn, kn, qkv, c_wide, seqs)

    xr, xi, ys = _s5_fwd(uf, bbr_d, bbi_d, cr_d, ci_d, ar, ai, seqs)
    u_blk = (uf, S5_WIDTH, 0, 0)
    yg = _rowwise(_s5_act, [full(ys), u_blk], [d_row], [(S5_WIDTH, S5_WIDTH, 0, F32)], "s5_act_fwd")
    if late_weights is not None:
        wb = dict(wb, **late_weights("mid", yg))
    z = _mm(yg, wb["s5_w_glu"], "nn", "s5_glu")
    y2n = _rowwise(_s5_gate, [full(yg), full(z)], [p["s5_b_glu"], p["out_norm_s5"]],
                   [(S5_WIDTH, S5_WIDTH, 0, BF16)], "s5_gate_fwd")
    foxn = _rowwise(_rms, [full(fox)], [p["out_norm_fox"]], [(FOX_WIDTH, FOX_WIDTH, 0, BF16)], "fox_outnorm_fwd")
    mixed = jnp.concatenate([foxn, y2n], axis=1)
    h1 = _mm(mixed, wb["w_out"], "nn", "mix_out", res=x)
    if late_weights is not None:
        wb = dict(wb, **late_weights("late", h1))

    hn2 = _rowwise(_rms, [full(h1)], [p["norm_cross"]], [(d, d, 0, BF16)], "norm_cross_fwd")
    mn = _rowwise(_rms, [full(mem)], [p["norm_mem"]], [(d, d, 0, BF16)], "norm_mem_fwd")
    xq_raw = _mm(hn2, wb["w_xq"], "nn", "x_q")
    kv = _mm(mn, wb["w_xkv"], "nn", "x_kv")
    xh = lambda a: (a, X_HEAD_DIM, 0, 1)
    xqn = _rowwise(_rms, [xh(xq_raw)], [p["xq_norm"]], [(d, X_HEAD_DIM, 1, BF16)], "x_qnorm_fwd", heads=N_X_HEADS)
    xkn = _rowwise(_rms, [xh(kv)], [p["xk_norm"]], [(d, X_HEAD_DIM, 1, BF16)], "x_knorm_fwd", heads=N_X_HEADS)
    xo = _xatt_fwd(xqn, xkn, kv, seqs)
    h2 = _mm(xo, wb["w_xo"], "nn", "x_out", res=h1)

    hn3 = _rowwise(_rms, [full(h2)], [p["norm_ffn"]], [(d, d, 0, BF16)], "norm_ffn_fwd")
    gate, up, pre, act = _ffn_up_gate(hn3, wb["w_ffn_up"], conv_w, p["ffn_conv_b"], seqs)
    h3 = _mm(act, wb["w_ffn_down"], "nn", "ffn_down", res=h2)
    dh3, dh3_b, loss = _loss_head(h3, target)

    g = {}
    late_dt = BF16 if early_grads is not None else F32
    g["w_ffn_down"] = _mm(act, dh3_b, "tn", "ffn_down_dw", out_dtype=late_dt)
    dgu, dconv_w, dconv_b = _ffn_down_dx_gate(dh3_b, wb["w_ffn_down"], gate, up, pre, conv_w, seqs)
    g["ffn_conv_w"], g["ffn_conv_b"] = jnp.sum(dconv_w, axis=0), jnp.sum(dconv_b, axis=0)
    dhn3 = _mm(dgu, wb["w_ffn_up"], "nt", "ffn_up_dx", out_dtype=BF16)
    g["w_ffn_up"] = _mm(hn3, dgu, "tn", "ffn_up_dw", out_dtype=late_dt)
    (dh2,), (g["norm_ffn"],) = _rowwise_vjp(_rms, [full(h2)], [p["norm_ffn"]], [full(dhn3)], "norm_ffn_bwd",
                                            adds=[full(dh3)])

    dxo = _mm(dh2, wb["w_xo"], "nt", "x_out_dx", out_dtype=BF16)
    g["w_xo"] = _mm(xo, dh2, "tn", "x_out_dw", out_dtype=late_dt)
    dxqn, dxkn, dxv = _xatt_bwd(xqn, xkn, kv, dxo, seqs)
    (dxq_raw,), (g["xq_norm"],) = _rowwise_vjp(_rms, [xh(xq_raw)], [p["xq_norm"]], [xh(dxqn)], "x_qnorm_bwd",
                                               heads=N_X_HEADS, row_dtypes=[BF16])
    (dxk_raw,), (g["xk_norm"],) = _rowwise_vjp(_rms, [xh(kv)], [p["xk_norm"]], [xh(dxkn)], "x_knorm_bwd",
                                               heads=N_X_HEADS, row_dtypes=[BF16])
    dkv = jnp.concatenate([dxk_raw, dxv.astype(BF16)], axis=1)
    dhn2 = _mm(dxq_raw, wb["w_xq"], "nt", "x_q_dx", out_dtype=BF16)
    g["w_xq"] = _mm(hn2, dxq_raw, "tn", "x_q_dw", out_dtype=late_dt)
    dmn = _mm(dkv, wb["w_xkv"], "nt", "x_kv_dx")
    g["w_xkv"] = _mm(mn, dkv, "tn", "x_kv_dw", out_dtype=late_dt)
    norm_cross = p["norm_cross"]
    if early_grads is not None:
        norm_cross = norm_cross + early_grads("late", {n: g[n] for n in LATE_WEIGHTS})[0:1, 0:1]
    (dh1,), (g["norm_cross"],) = _rowwise_vjp(_rms, [full(h1)], [norm_cross], [full(dhn2)], "norm_cross_bwd",
                                              adds=[full(dh2)])
    _, (g["norm_mem"],) = _rowwise_vjp(_rms, [full(mem)], [p["norm_mem"]], [full(dmn)], "norm_mem_bwd",
                                       row_dtypes=[BF16])

    dmixed = _mm(dh1, wb["w_out"], "nt", "mix_out_dx", out_dtype=BF16)
    g["w_out"] = _mm(mixed, dh1, "tn", "mix_out_dw", out_dtype=late_dt)
    (dfox,), (g["out_norm_fox"],) = _rowwise_vjp(_rms, [full(fox)], [p["out_norm_fox"]],
                                                 [(dmixed, FOX_WIDTH, 0, 0)], "fox_outnorm_bwd")
    (dyg_a, dz), (g["s5_b_glu"], g["out_norm_s5"]) = _rowwise_vjp(
        _s5_gate, [full(yg), full(z)], [p["s5_b_glu"], p["out_norm_s5"]], [(dmixed, S5_WIDTH, 1, 0)], "s5_gate_bwd",
        row_dtypes=[F32, BF16])
    dyg = _mm(dz, wb["s5_w_glu"], "nt", "s5_glu_dx", res=dyg_a)
    g["s5_w_glu"] = _mm(yg, dz, "tn", "s5_glu_dw", out_dtype=late_dt)
    if early_grads is not None:
        d_row = d_row + early_grads("mid", {n: g[n] for n in MID_WEIGHTS})[0:1, 0:1]
    (dys, du_a), (dd_row,) = _rowwise_vjp(_s5_act, [full(ys), u_blk], [d_row], [full(dyg)], "s5_act_bwd",
                                          row_dtypes=[BF16, F32])
    g["s5_d"] = dd_row
    du_b, dbbr_d, dbbi_d, dcr_d, dci_d, dar, dai = _s5_bwd(dys, uf, xr, xi, bbr_d, bbi_d, cr_d, ci_d, ar, ai, seqs)
    dbbr_d, dbbi_d, dcr_d, dci_d = (jnp.sum(a, axis=0) for a in (dbbr_d, dbbi_d, dcr_d, dci_d))
    d_lb_r = jnp.sum(dar, axis=0).reshape(S5_GROUPS, S5_STATE)
    d_lb_i = jnp.sum(dai, axis=0).reshape(S5_GROUPS, S5_STATE)
    g["s5_a_re"], g["s5_a_im"], g["s5_log_dt"], g["s5_b_re"], g["s5_b_im"] = s5_pull(
        (d_lb_r, d_lb_i, _blockdiag_in_grad(dbbr_d), _blockdiag_in_grad(dbbi_d)))
    g["s5_c_re"] = _blockdiag_out_grad(dcr_d)
    g["s5_c_im"] = -_blockdiag_out_grad(dci_d)

    dqn, dkn, dv, dc, dcq = _fox_bwd(qn, kn, qkv, c_wide, fox, dfox, lse, seqs)
    pair = lambda a: (a, 128, 0, 1)
    (dq_raw,), (dgq2,) = _rowwise_vjp(_rms_pair, [q_pair], [gq2], [pair(dqn)], "fox_qnorm_bwd", heads=N_PAIRS,
                                      row_dtypes=[BF16])
    (dk_raw,), (dgk2,) = _rowwise_vjp(_rms_pair, [k_pair], [gk2], [pair(dkn)], "fox_knorm_bwd", heads=N_PAIRS,
                                      row_dtypes=[BF16])
    g["fox_q_norm"] = dgq2[:, :HEAD_DIM] + dgq2[:, HEAD_DIM:]
    g["fox_k_norm"] = dgk2[:, :HEAD_DIM] + dgk2[:, HEAD_DIM:]
    df_rows, dfb = _forget_bwd(f_rows, f_bias, (dc + dcq).reshape(bh, l))
    g["fox_f_bias"] = jnp.sum(dfb.reshape(seqs, N_FOX_HEADS), axis=0)
    df = df_rows.reshape(seqs, N_FOX_HEADS, l).transpose(0, 2, 1).reshape(t, N_FOX_HEADS)
    dqkv = jnp.concatenate([dq_raw, dk_raw, dv.astype(BF16)], axis=1)
    duf = jnp.concatenate([du_a + du_b, df, jnp.zeros((t, UF_COLS - S5_WIDTH - N_FOX_HEADS), F32)],
                          axis=1).astype(BF16)
    dhn1 = _mm(duf, w_uf, "nt", "in_uf_dx", res=_mm(dqkv, w_qkv, "nt", "in_qkv_dx"), out_dtype=BF16)
    dw_qkv = _mm(hn1, dqkv, "tn", "in_qkv_dw")
    dw_uf = _mm(hn1, duf, "tn", "in_uf_dw")
    g["w_in"] = jnp.concatenate([dw_qkv, dw_uf[:, S5_WIDTH:S5_WIDTH + N_FOX_HEADS], dw_uf[:, :S5_WIDTH]], axis=1)
    (dx,), (g["norm_mix"],) = _rowwise_vjp(_rms, [full(x)], [p["norm_mix"]], [full(dhn1)], "norm_mix_bwd",
                                           adds=[full(dh1)])
    return loss, dx.reshape(seqs, l, d), g


def _place():
    return lax.axis_index("x"), lax.axis_index("y"), lax.axis_index("c")


def _other_chips(x, y):
    return [(1 - x, y), (x, 1 - y), (1 - x, 1 - y)]


ANY = pl.BlockSpec(memory_space=pl.ANY)


HBM = pl.BlockSpec(memory_space=pltpu.HBM)
SEM = pl.BlockSpec(memory_space=pltpu.SEMAPHORE)
DATAFLOW = pltpu.SideEffectType.DATAFLOW_SIDE_EFFECTING


def _in_hbm(a):
    return pltpu.with_memory_space_constraint(a, pltpu.HBM)


def _split_start(name, srcs, lands, n_copies, plan):
    n = len(srcs)

    def body(*refs):
        src_refs, land_refs = refs[:n], refs[n:2 * n]
        send_sems, recv_sems = refs[2 * n], refs[2 * n + 1]
        for i, (src, dst, dev) in enumerate(plan(src_refs, land_refs)):
            pltpu.make_async_remote_copy(src_ref=src, dst_ref=dst, send_sem=send_sems.at[i], recv_sem=recv_sems.at[i],
                                         device_id=dev, device_id_type=MESH).start()
        refs[-1][...] = jnp.zeros((8, 128), F32)

    res = pl.pallas_call(
        body, name=name, in_specs=[HBM] * (2 * n),
        out_specs=[SEM, SEM] + [HBM] * (2 * n) + [pl.BlockSpec(memory_space=pltpu.VMEM)],
        out_shape=[pltpu.SemaphoreType.DMA((n_copies,)), pltpu.SemaphoreType.DMA((n_copies,))]
        + [pltpu.HBM(a.shape, a.dtype) for a in list(srcs) + list(lands)] + [jax.ShapeDtypeStruct((8, 128), F32)],
        input_output_aliases={i: 2 + i for i in range(2 * n)},
        compiler_params=pltpu.CompilerParams(has_side_effects=DATAFLOW),
    )(*[_in_hbm(a) for a in list(srcs) + list(lands)])
    return res[0], res[1], list(res[2:2 + n]), list(res[2 + n:2 + 2 * n]), res[-1]


def _split_wait(name, send_sems, recv_sems, srcs, lands, after, plan):
    n = len(srcs)

    def body(*refs):
        src_refs, land_refs = refs[:n], refs[n:2 * n]
        send_ref, recv_ref = refs[2 * n], refs[2 * n + 1]
        for i, (src, dst, dev) in enumerate(plan(src_refs, land_refs)):
            cp = pltpu.make_async_remote_copy(src_ref=src, dst_ref=dst, send_sem=send_ref.at[i], recv_sem=recv_ref.at[i],
                                              device_id=dev, device_id_type=MESH)
            cp.wait_send()
            cp.wait_recv()

    res = pl.pallas_call(
        body, name=name, in_specs=[HBM] * (2 * n) + [SEM, SEM, ANY], out_specs=[HBM] * (2 * n),
        out_shape=[pltpu.HBM(a.shape, a.dtype) for a in list(srcs) + list(lands)],
        input_output_aliases={i: i for i in range(2 * n)},
        compiler_params=pltpu.CompilerParams(has_side_effects=DATAFLOW),
    )(*srcs, *lands, send_sems, recv_sems, after)
    return list(res[:n]), list(res[n:])


def _first_gather_plan(src_refs, land_refs):
    x, y, c = _place()
    mine = 2 * x + y
    (src, small), (land, small_land) = src_refs, land_refs
    r = src.shape[0]
    hr = r // 2
    half = src.at[pl.ds(pl.multiple_of(c * hr, 16), hr), :]
    half_dst = land.at[pl.ds(pl.multiple_of(mine * r + c * hr, 16), hr), :]
    copies = [(src, land.at[pl.ds(pl.multiple_of(mine * r, 16), r), :], (x, y, 1 - c)),
              (small, small_land.at[mine], (x, y, 1 - c))]
    for px, py in _other_chips(x, y):
        copies += [(half, half_dst, (px, py, c)), (small, small_land.at[mine], (px, py, c))]
    return copies


def _forward_to_sibling(full):
    def body(full_in, full_ref, send_sems, recv_sems):
        x, y, c = _place()
        r = full_ref.shape[0] // 4
        hr = r // 2
        copies = []
        for j, (px, py) in enumerate(_other_chips(x, y)):
            got = full_ref.at[pl.ds(pl.multiple_of((2 * px + py) * r + c * hr, 16), hr), :]
            cp = pltpu.make_async_remote_copy(
                src_ref=got, dst_ref=got, send_sem=send_sems.at[j], recv_sem=recv_sems.at[j],
                device_id=(x, y, 1 - c), device_id_type=MESH)
            cp.start()
            copies.append(cp)
        for cp in copies:
            cp.wait()

    return pl.pallas_call(
        body, name="gather_first_forward", in_specs=[ANY], out_specs=ANY,
        out_shape=jax.ShapeDtypeStruct(full.shape, full.dtype), input_output_aliases={0: 0},
        scratch_shapes=[pltpu.SemaphoreType.DMA((3,)), pltpu.SemaphoreType.DMA((3,))],
        compiler_params=pltpu.CompilerParams(has_side_effects=True),
    )(full)


def _late_gather_plan(col_kind):
    def plan(src_refs, land_refs):
        x, y, c = _place()
        mine = 2 * x + y
        copies = []
        for a, (src, land) in enumerate(zip(src_refs, land_refs)):
            r, cs = src.shape
            if col_kind[a]:
                dst = land.at[:, pl.ds(pl.multiple_of(mine * cs, 128), cs)]
            else:
                dst = land.at[pl.ds(pl.multiple_of(mine * r, 16), r), :]
            copies.append((src, dst, (x, y, 1 - c)))
            copies += [(src, dst, (px, py, c)) for (px, py) in _other_chips(x, y)]
        return copies
    return plan


def _late_reduce_plan(col_kind):
    def plan(src_refs, land_refs):
        x, y, c = _place()
        copies = []
        for a, (src, land) in enumerate(zip(src_refs, land_refs)):
            for j, (px, py) in enumerate(_other_chips(x, y)):
                if col_kind[a] is None:
                    piece = src
                elif col_kind[a]:
                    cs = land.shape[2]
                    piece = src.at[:, pl.ds(pl.multiple_of((2 * px + py) * cs, 128), cs)]
                else:
                    piece = src.at[2 * px + py]
                copies.append((piece, land.at[j], (px, py, c)))
        return copies
    return plan


def _pair_swap(name, halves):
    n = len(halves)

    def body(*refs):
        ins, outs = refs[:n], refs[n:2 * n]
        send_sems, recv_sems = refs[2 * n:]
        x, y, c = _place()
        copies = []
        for a in range(n):
            cp = pltpu.make_async_remote_copy(
                src_ref=ins[a], dst_ref=outs[a], send_sem=send_sems.at[a], recv_sem=recv_sems.at[a],
                device_id=(x, y, 1 - c), device_id_type=MESH)
            cp.start()
            copies.append(cp)
        for cp in copies:
            cp.wait()

    return pl.pallas_call(
        body, name=name, in_specs=[ANY] * n, out_specs=[ANY] * n,
        out_shape=[jax.ShapeDtypeStruct(s.shape, s.dtype) for s in halves],
        scratch_shapes=[pltpu.SemaphoreType.DMA((n,)), pltpu.SemaphoreType.DMA((n,))],
        compiler_params=pltpu.CompilerParams(has_side_effects=True),
    )(*halves)


def _chip_sum(name, chip_sel, own, col, others):
    _, r, c = others.shape
    tr = _pick(r, (256, 128, 64, 32, 16))
    if col:
        own_spec = pl.BlockSpec((tr, c), lambda i, s: (i, s[0]))
    else:
        own_spec = pl.BlockSpec((None, tr, c), lambda i, s: (s[0], i, 0))
    specs = [own_spec] + [pl.BlockSpec((None, tr, c), lambda i, s, k=k: (k, i, 0)) for k in range(3)]

    def body(s_ref, own_ref, r0, r1, r2, o_ref):
        total = ((own_ref[...].astype(F32) + r0[...].astype(F32)) + r1[...].astype(F32)) + r2[...].astype(F32)
        o_ref[...] = total.astype(o_ref.dtype)

    return pl.pallas_call(
        body, name=name,
        grid_spec=pltpu.PrefetchScalarGridSpec(
            num_scalar_prefetch=1, grid=(r // tr,), in_specs=specs,
            out_specs=pl.BlockSpec((tr, c), lambda i, s: (i, 0))),
        out_shape=jax.ShapeDtypeStruct((r, c), BF16),
        compiler_params=_params(("parallel",)),
    )(chip_sel, own, others, others, others)


def _small_layout(vals):
    sizes = [int(math.prod(v.shape)) for v in vals]
    padded = [-(-s // 128) * 128 for s in sizes]
    return sizes, padded, -(-sum(padded) // 1024) * 1024


def _pack_small(vals):
    sizes, padded, total = _small_layout(vals)
    flat = [jnp.pad(v.reshape(-1), (0, p - s)) for v, s, p in zip(vals, sizes, padded)]
    flat.append(jnp.zeros((total - sum(padded),), F32))
    return jnp.concatenate(flat).reshape(total // 128, 128)


def _allreduce_small(own, others, vals):
    def body(own_ref, oth_ref, out_ref, land, send_sem, recv_sem):
        x, y, c = _place()
        out_ref[...] = (own_ref[...] + oth_ref[0]) + (oth_ref[1] + oth_ref[2])
        cp = pltpu.make_async_remote_copy(
            src_ref=out_ref, dst_ref=land, send_sem=send_sem.at[0], recv_sem=recv_sem.at[0],
            device_id=(x, y, 1 - c), device_id_type=MESH)
        cp.start()
        cp.wait()
        out_ref[...] = out_ref[...] + land[...]

    vm = pl.BlockSpec(memory_space=pltpu.VMEM)
    summed = pl.pallas_call(
        body, name="allreduce_small", in_specs=[vm, vm], out_specs=vm,
        out_shape=jax.ShapeDtypeStruct(own.shape, F32),
        scratch_shapes=[pltpu.VMEM(own.shape, F32), pltpu.SemaphoreType.DMA((1,)), pltpu.SemaphoreType.DMA((1,))],
        compiler_params=pltpu.CompilerParams(has_side_effects=True, vmem_limit_bytes=VMEM_LIMIT_BYTES),
    )(own, others).reshape(-1)
    sizes, padded, _ = _small_layout(vals)
    outs, off = [], 0
    for v, s, p in zip(vals, sizes, padded):
        outs.append(summed[off:off + s].reshape(v.shape))
        off += p
    return outs


def _adamw_math(w, g, m, v):
    m2 = ADAM_B1 * m + (1.0 - ADAM_B1) * g
    v2 = ADAM_B2 * v + (1.0 - ADAM_B2) * (g * g)
    m_hat = m2 / (1.0 - ADAM_B1 ** ADAM_STEP)
    v_hat = v2 / (1.0 - ADAM_B2 ** ADAM_STEP)
    delta = -ADAM_LR * (m_hat / (jnp.sqrt(v_hat) + ADAM_EPS) + ADAM_WD * w)
    return delta, m2, v2


def _adamw_big(name, w, g_mine, g_sibling, m, v):
    _, r, c = w.shape

    def body(w_ref, ga_ref, gb_ref, m_ref, v_ref, go_ref, d_ref, mo_ref, vo_ref):
        gv = ga_ref[...].astype(F32) + gb_ref[...].astype(F32)
        d, m2, v2 = _adamw_math(w_ref[...], gv, m_ref[...], v_ref[...])
        go_ref[...] = gv
        d_ref[...] = d
        mo_ref[...] = m2
        vo_ref[...] = v2

    tr = _pick(r, (256, 128, 64, 32, 16, 8))
    if r % tr == 0 and tr % 8 == 0:
        grid = (r // tr,)
        blk = pl.BlockSpec((None, tr, c), lambda i: (0, i, 0))
        part = pl.BlockSpec((tr, c), lambda i: (i, 0))
    else:
        grid = (c // 512,)
        blk = pl.BlockSpec((None, r, 512), lambda i: (0, 0, i))
        part = pl.BlockSpec((r, 512), lambda i: (0, i))
    return pl.pallas_call(
        body, name=name, grid=grid, in_specs=[blk, part, part, blk, blk], out_specs=[blk] * 4,
        out_shape=[jax.ShapeDtypeStruct((1, r, c), F32)] * 4, compiler_params=_params(("parallel",)),
    )(w, g_mine, g_sibling, m, v)


def _adamw_small(ws, gs, ms, vs):
    n = len(ws)

    def body(*refs):
        w_r, g_r, m_r, v_r = refs[:n], refs[n:2 * n], refs[2 * n:3 * n], refs[3 * n:4 * n]
        o = refs[4 * n:]
        for a in range(n):
            gv = g_r[a][...]
            d, m2, v2 = _adamw_math(w_r[a][...], gv, m_r[a][...], v_r[a][...])
            o[a][...] = gv
            o[n + a][...] = d
            o[2 * n + a][...] = m2
            o[3 * n + a][...] = v2

    res = pl.pallas_call(
        body, name="adamw_small", out_shape=[jax.ShapeDtypeStruct(w.shape, F32) for _ in range(4) for w in ws],
        compiler_params=_params(),
    )(*ws, *gs, *ms, *vs)
    return res[:n], res[n:2 * n], res[2 * n:3 * n], res[3 * n:]


def _full_from_gathered(name, gathered):
    if name == "w_in":
        rows = gathered.shape[0] // 4
        return gathered.reshape(4, rows, gathered.shape[1]).transpose(1, 0, 2).reshape(rows, 4 * gathered.shape[1])
    return gathered


def _reduce_layout(name, full):
    if name in COL_KIND:
        return full
    if name == "w_in":
        rows, cols = full.shape
        return full.reshape(rows, 4, cols // 4).transpose(1, 0, 2)
    return full.reshape(4, full.shape[0] // 4, full.shape[1])


def kernel(x, mem, norm_mix, w_in, fox_q_norm, fox_k_norm, fox_f_bias, s5_a_re, s5_a_im, s5_log_dt, s5_b_re, s5_b_im, s5_c_re, s5_c_im, s5_d, s5_w_glu, s5_b_glu, out_norm_fox, out_norm_s5, w_out, norm_cross, norm_mem, w_xq, w_xkv, xq_norm, xk_norm, w_xo, norm_ffn, w_ffn_up, ffn_conv_w, ffn_conv_b, w_ffn_down, loss_target, m_norm_mix, m_w_in, m_fox_q_norm, m_fox_k_norm, m_fox_f_bias, m_s5_a_re, m_s5_a_im, m_s5_log_dt, m_s5_b_re, m_s5_b_im, m_s5_c_re, m_s5_c_im, m_s5_d, m_s5_w_glu, m_s5_b_glu, m_out_norm_fox, m_out_norm_s5, m_w_out, m_norm_cross, m_norm_mem, m_w_xq, m_w_xkv, m_xq_norm, m_xk_norm, m_w_xo, m_norm_ffn, m_w_ffn_up, m_ffn_conv_w, m_ffn_conv_b, m_w_ffn_down, v_norm_mix, v_w_in, v_fox_q_norm, v_fox_k_norm, v_fox_f_bias, v_s5_a_re, v_s5_a_im, v_s5_log_dt, v_s5_b_re, v_s5_b_im, v_s5_c_re, v_s5_c_im, v_s5_d, v_s5_w_glu, v_s5_b_glu, v_out_norm_fox, v_out_norm_s5, v_w_out, v_norm_cross, v_norm_mem, v_w_xq, v_w_xkv, v_xq_norm, v_xk_norm, v_w_xo, v_norm_ffn, v_w_ffn_up, v_ffn_conv_w, v_ffn_conv_b, v_w_ffn_down):
    given = dict(locals())
    w = {n: given[n] for n in WEIGHTS}
    m = {n: given["m_" + n] for n in WEIGHTS}
    v = {n: given["v_" + n] for n in WEIGHTS}
    xi, yi, _ = _place()
    chip = (2 * xi + yi).astype(jnp.int32)
    chip_sel = chip.reshape(1)

    first_shard, taps = w[FIRST_WEIGHT][0].astype(BF16), w["ffn_conv_w"][0]
    send, recv, srcs, lands, g_started = _split_start(
        "gather_first_start", [first_shard, taps],
        [lax.empty((4 * first_shard.shape[0], first_shard.shape[1]), BF16), lax.empty((4,) + taps.shape, F32)],
        8, _first_gather_plan)
    pending = {"first": ((FIRST_WEIGHT, "ffn_conv_w"), _first_gather_plan, send, recv, srcs, lands)}
    for stage, names in (("mid", MID_WEIGHTS), ("late", LATE_WEIGHTS)):
        kinds = [n in COL_KIND for n in names]
        shards = [w[n][0].astype(BF16) for n in names]
        shards[0] = shards[0] + g_started[0:1, 0:1].astype(BF16)
        full = [lax.empty((s.shape[0], 4 * s.shape[1]) if ck else (4 * s.shape[0], s.shape[1]), BF16)
                for s, ck in zip(shards, kinds)]
        plan = _late_gather_plan(kinds)
        send, recv, srcs, lands, g_started = _split_start(
            "gather_" + stage + "_start", shards, full, 4 * len(names), plan)
        pending[stage] = (names, plan, send, recv, srcs, lands)

    def late_weights(stage, after):
        names, plan, send, recv, srcs, lands = pending[stage]
        _, full = _split_wait("gather_" + stage + "_wait", send, recv, srcs, lands, after, plan)
        if stage == "first":
            full = [_full_from_gathered(FIRST_WEIGHT, _forward_to_sibling(full[0])),
                    full[1].transpose(1, 0, 2).reshape(3, D_FF)]
        return dict(zip(names, full))

    reducing = {}

    def start_reduce(stage, grads_by_name, whole=()):
        names = list(grads_by_name)
        kinds = [n in COL_KIND for n in names]
        grads = [_reduce_layout(n, grads_by_name[n].astype(BF16)) for n in names]
        lands = [lax.empty((3, s.shape[0], s.shape[1] // 4) if ck else (3,) + s.shape[1:], BF16)
                 for s, ck in zip(grads, kinds)]
        plan = _late_reduce_plan(kinds + [None] * len(whole))
        send, recv, srcs, lands, started = _split_start(
            "reduce_" + stage + "_start", grads + list(whole),
            lands + [lax.empty((3,) + a.shape, a.dtype) for a in whole], 3 * (len(names) + len(whole)), plan)
        reducing[stage] = (names, kinds, plan, send, recv, srcs, lands)
        return started

    p = {n: w[n][0] for n in SMALL}
    for n in ("norm_mix", "fox_q_norm", "fox_k_norm", "fox_f_bias", "s5_b_glu", "out_norm_fox", "out_norm_s5",
              "norm_cross", "norm_mem", "xq_norm", "xk_norm", "norm_ffn", "ffn_conv_b"):
        p[n] = p[n].reshape(1, -1)
    p["norm_mix"] = p["norm_mix"] + g_started[0:1, 0:1]
    loss, grad_x, g = _local_step(x, mem, loss_target, p, {}, late_weights, start_reduce)

    small_names = list(SMALL) + ["ffn_conv_w"]
    small_vals = [g[n].reshape(w[n].shape if n != "ffn_conv_w" else (1, 3, D_FF)) for n in small_names] + [loss]
    after = start_reduce("first", {FIRST_WEIGHT: g[FIRST_WEIGHT]}, whole=[_pack_small(small_vals)])

    out_g, out_d, out_m, out_v = {}, {}, {}, {}

    def finish(stage, after):
        names, kinds, plan, send, recv, srcs, lands = reducing[stage]
        sums, from_chips = _split_wait("reduce_" + stage + "_wait", send, recv, srcs, lands, after, plan)
        mine = [_chip_sum("reduce_chip_sum_" + n, chip_sel, ps, ck, fc)
                for n, ps, fc, ck in zip(names, sums, from_chips, kinds)]
        theirs = _pair_swap("reduce_pair_swap_" + stage, mine)
        for n, a, b in zip(names, mine, theirs):
            if n == "w_in":
                flip = lambda t: jnp.swapaxes(t, -1, -2)
                res = _adamw_big("adamw_" + n, flip(w[n]), flip(a), flip(b), flip(m[n]), flip(v[n]))
                out_g[n], out_d[n], out_m[n], out_v[n] = (flip(t) for t in res)
                continue
            out_g[n], out_d[n], out_m[n], out_v[n] = _adamw_big("adamw_" + n, w[n], a, b, m[n], v[n])
        return sums[len(names):], from_chips[len(names):], out_v[names[-1]]

    _, _, after = finish("late", after)
    _, _, after = finish("mid", after)
    (small_own,), (small_others,), _ = finish("first", after)

    reduced = _allreduce_small(small_own, small_others, small_vals)
    loss_all = reduced[-1].reshape(())
    conv_w_grad = lax.dynamic_slice_in_dim(reduced[-2], chip * (D_FF // 4), D_FF // 4, axis=2)
    sg, sd, sm, sv = _adamw_small(
        [w[n] for n in small_names], list(reduced[:len(SMALL)]) + [conv_w_grad],
        [m[n] for n in small_names], [v[n] for n in small_names])
    out_g.update(zip(small_names, sg))
    out_d.update(zip(small_names, sd))
    out_m.update(zip(small_names, sm))
    out_v.update(zip(small_names, sv))

    return (loss_all, grad_x, *[out_g[n] for n in WEIGHTS], *[out_d[n] for n in WEIGHTS],
            *[out_m[n] for n in WEIGHTS], *[out_v[n] for n in WEIGHTS])
```

```python
import math

import jax
import jax.numpy as jnp
from jax import lax
from jax.experimental import pallas as pl
from jax.experimental.pallas import tpu as pltpu

F32 = jnp.float32
BF16 = jnp.bfloat16

D_MODEL = 1024
FOX_WIDTH = 512
HEAD_DIM = 64
N_FOX_HEADS = 8
S5_WIDTH = 512
S5_GROUP_CH = 16
S5_GROUPS = 32
S5_STATE = 64
S5_CH = S5_GROUPS * S5_STATE
N_X_HEADS = 4
X_HEAD_DIM = 256
N_MEM = 256
D_FF = 2816
UF_COLS = 640
EPS = 1e-6
ADAM_LR = 0.001
ADAM_B1 = 0.9
ADAM_B2 = 0.999
ADAM_EPS = 1e-08
ADAM_WD = 0.01
ADAM_STEP = 10

VMEM_LIMIT_BYTES = 56 * 1024 * 1024
MM_BLOCK_BYTES = 6 * 1024 * 1024
MM_VMEM_BYTES = 40 * 1024 * 1024
MM_TILE_MAX = 1536
MESH = pl.DeviceIdType.MESH

FIRST_WEIGHT = "w_in"
MID_WEIGHTS = ("s5_w_glu", "w_out")
EARLY_WEIGHTS = (FIRST_WEIGHT,) + MID_WEIGHTS
LATE_WEIGHTS = ("w_xq", "w_xkv", "w_xo", "w_ffn_up", "w_ffn_down")
BIG = EARLY_WEIGHTS + LATE_WEIGHTS
COL_KIND = ("w_xkv", "w_ffn_up")
SMALL = ("norm_mix", "fox_q_norm", "fox_k_norm", "fox_f_bias", "s5_a_re", "s5_a_im", "s5_log_dt",
         "s5_b_re", "s5_b_im", "s5_c_re", "s5_c_im", "s5_d", "s5_b_glu", "out_norm_fox", "out_norm_s5",
         "norm_cross", "norm_mem", "xq_norm", "xk_norm", "norm_ffn", "ffn_conv_b")
WEIGHTS = ("norm_mix", "w_in", "fox_q_norm", "fox_k_norm", "fox_f_bias", "s5_a_re", "s5_a_im", "s5_log_dt",
           "s5_b_re", "s5_b_im", "s5_c_re", "s5_c_im", "s5_d", "s5_w_glu", "s5_b_glu", "out_norm_fox",
           "out_norm_s5", "w_out", "norm_cross", "norm_mem", "w_xq", "w_xkv", "xq_norm", "xk_norm", "w_xo",
           "norm_ffn", "w_ffn_up", "ffn_conv_w", "ffn_conv_b", "w_ffn_down")


def _params(sem=None):
    return pltpu.CompilerParams(dimension_semantics=sem, vmem_limit_bytes=VMEM_LIMIT_BYTES)


def _pick(n, cands):
    for c in cands:
        if n % c == 0:
            return c
    return n


_DIMS = {"nn": (((1,), (0,)), ((), ())), "nt": (((1,), (1,)), ((), ())), "tn": (((0,), (0,)), ((), ()))}


def _mm(a, b, mode, name, out_dtype=F32, res=None):
    if mode == "nn":
        (m, k), (k2, n) = a.shape, b.shape
    elif mode == "nt":
        (m, k), (n, k2) = a.shape, b.shape
    else:
        (k, m), (k2, n) = a.shape, b.shape
    assert k == k2, (name, a.shape, b.shape)

    has_res = res is not None
    a_size, b_size = a.dtype.itemsize, b.dtype.itemsize
    o_size = jnp.dtype(out_dtype).itemsize + (res.dtype.itemsize if has_res else 0)

    def tiles(dim):
        return [c for c in range(MM_TILE_MAX, 0, -128) if dim % c == 0] or [dim]

    best = None
    for tm in tiles(m):
        for tn in tiles(n):
            a_blk, b_blk = tm * k * a_size, tn * k * b_size
            if max(a_blk, b_blk) > MM_BLOCK_BYTES or 2 * (a_blk + b_blk + tm * tn * o_size) > MM_VMEM_BYTES:
                continue
            for rows_outer in (True, False):
                moved = (m * k * a_size + (m // tm) * n * k * b_size) if rows_outer else \
                        (n * k * b_size + (n // tn) * m * k * a_size)
                key = (moved, -(tm * tn))
                if best is None or key < best[0]:
                    best = (key, tm, tn, rows_outer)
    assert best is not None, (name, a.shape, b.shape)
    _, tm, tn, rows_outer = best
    ij = (lambda g0, g1: (g0, g1)) if rows_outer else (lambda g0, g1: (g1, g0))
    if mode == "tn":
        a_spec = pl.BlockSpec((k, tm), lambda g0, g1: (0, ij(g0, g1)[0]))
    else:
        a_spec = pl.BlockSpec((tm, k), lambda g0, g1: (ij(g0, g1)[0], 0))
    if mode == "nt":
        b_spec = pl.BlockSpec((tn, k), lambda g0, g1: (ij(g0, g1)[1], 0))
    else:
        b_spec = pl.BlockSpec((k, tn), lambda g0, g1: (0, ij(g0, g1)[1]))
    o_spec = pl.BlockSpec((tm, tn), lambda g0, g1: ij(g0, g1))
    grid = (m // tm, n // tn) if rows_outer else (n // tn, m // tm)
    dims = _DIMS[mode]

    def body(*refs):
        a_ref, b_ref = refs[0], refs[1]
        o_ref = refs[-1]
        acc = lax.dot_general(a_ref[...].astype(BF16), b_ref[...].astype(BF16), dims, preferred_element_type=F32)
        if has_res:
            acc = acc + refs[2][...].astype(F32)
        o_ref[...] = acc.astype(o_ref.dtype)

    return pl.pallas_call(
        body, name=name, grid=grid,
        in_specs=[a_spec, b_spec] + ([o_spec] if has_res else []),
        out_specs=o_spec, out_shape=jax.ShapeDtypeStruct((m, n), out_dtype),
        compiler_params=_params(("parallel", "parallel")),
    )(*((a, b, res) if has_res else (a, b)))


def _row_spec(tm, bc, off, step):
    return pl.BlockSpec((tm, bc), lambda i, h: (i, off + step * h))


ROW_TILE_ELEMS = 512 * 1024


def _row_tile(t, rows):
    widest = max(bc for (_, bc, _, _) in rows)
    return _pick(t, (min(t, ROW_TILE_ELEMS // widest), 512, 256, 128, 64, 8))


def _rowwise(fn, rows, pars, outs, name, heads=1):
    t = rows[0][0].shape[0]
    tm = _row_tile(t, rows)
    nr, npar = len(rows), len(pars)

    def body(*refs):
        vals = [r[...].astype(F32) for r in refs[:nr + npar]]
        res = fn(*vals)
        if not isinstance(res, (tuple, list)):
            res = (res,)
        for o_ref, v in zip(refs[nr + npar:], res):
            o_ref[...] = v.astype(o_ref.dtype)

    in_specs = [_row_spec(tm, bc, off, st) for (_, bc, off, st) in rows]
    in_specs += [pl.BlockSpec(p.shape, lambda i, h: (0, 0)) for p in pars]
    out_specs = [_row_spec(tm, bc, 0, st) for (_, bc, st, _) in outs]
    out_shape = [jax.ShapeDtypeStruct((t, c), dt) for (c, _, _, dt) in outs]
    res = pl.pallas_call(
        body, name=name, grid=(t // tm, heads), in_specs=in_specs, out_specs=out_specs, out_shape=out_shape,
        compiler_params=_params(("parallel", "parallel")),
    )(*[r[0] for r in rows], *pars)
    return res[0] if len(res) == 1 else res


def _rowwise_vjp(fn, rows, pars, cts, name, heads=1, adds=None, row_dtypes=None):
    t = rows[0][0].shape[0]
    tm = _row_tile(t, rows)
    nr, npar, nct = len(rows), len(pars), len(cts)
    adds = adds or [None] * nr
    add_list = [a for a in adds if a is not None]
    row_dtypes = row_dtypes or [F32] * nr

    def body(*refs):
        i, h = pl.program_id(0), pl.program_id(1)
        p = 0
        row_v = [r[...].astype(F32) for r in refs[p:p + nr]]; p += nr
        par_v = [r[...].astype(F32) for r in refs[p:p + npar]]; p += npar
        ct_v = [r[...].astype(F32) for r in refs[p:p + nct]]; p += nct
        add_refs = refs[p:p + len(add_list)]; p += len(add_list)
        drow_refs = refs[p:p + nr]; p += nr
        dpar_refs = refs[p:p + npar]

        def wrapped(*a):
            r = fn(*a)
            return tuple(r) if isinstance(r, (tuple, list)) else (r,)

        _, pull = jax.vjp(wrapped, *row_v, *par_v)
        grads = pull(tuple(ct_v))
        ai = 0
        for k in range(nr):
            g = grads[k]
            if adds[k] is not None:
                g = g + add_refs[ai][...].astype(F32)
                ai += 1
            drow_refs[k][...] = g.astype(drow_refs[k].dtype)

        @pl.when((i == 0) & (h == 0))
        def _():
            for r in dpar_refs:
                r[...] = jnp.zeros(r.shape, r.dtype)

        for k in range(npar):
            dpar_refs[k][...] += grads[nr + k]

    in_specs = [_row_spec(tm, bc, off, st) for (_, bc, off, st) in rows]
    in_specs += [pl.BlockSpec(q.shape, lambda i, h: (0, 0)) for q in pars]
    in_specs += [_row_spec(tm, bc, off, st) for (_, bc, off, st) in cts]
    in_specs += [_row_spec(tm, bc, off, st) for (_, bc, off, st) in add_list]
    out_specs = [_row_spec(tm, bc, 0, st) for (_, bc, _, st) in rows]
    out_specs += [pl.BlockSpec(q.shape, lambda i, h: (0, 0)) for q in pars]
    out_shape = [jax.ShapeDtypeStruct((t, bc * (heads if st else 1)), dt) for (_, bc, _, st), dt in zip(rows, row_dtypes)]
    out_shape += [jax.ShapeDtypeStruct(q.shape, F32) for q in pars]
    res = pl.pallas_call(
        body, name=name, grid=(t // tm, heads), in_specs=in_specs, out_specs=out_specs, out_shape=out_shape,
        compiler_params=_params(("arbitrary", "arbitrary")),
    )(*[r[0] for r in rows], *pars, *[c[0] for c in cts], *[a[0] for a in add_list])
    return list(res[:nr]), list(res[nr:])


def _rms(x, g):
    return x * lax.rsqrt(jnp.mean(x * x, axis=-1, keepdims=True) + EPS) * g


def _rms_pair(x, g):
    n = 2 * HEAD_DIM
    same_head = (lax.broadcasted_iota(jnp.int32, (n, n), 0) < HEAD_DIM) == \
                (lax.broadcasted_iota(jnp.int32, (n, n), 1) < HEAD_DIM)
    ms = jnp.dot(x * x, same_head.astype(F32), precision=lax.Precision.HIGHEST,
                 preferred_element_type=F32) * (1.0 / HEAD_DIM)
    return x * lax.rsqrt(ms + EPS) * g


def _gelu(x):
    return 0.5 * x * (1.0 + jnp.tanh(math.sqrt(2.0 / math.pi) * (x + 0.044715 * (x * x * x))))


def _s5_act(ys, u, d):
    return _gelu(ys + d * u)


def _s5_gate(yg, z, b, g):
    return _rms(yg * jax.nn.sigmoid(z + b), g)


def _lane_cumsum(x, reverse):
    n = x.shape[-1]
    lane = lax.broadcasted_iota(jnp.int32, x.shape, 1)
    k = 1
    while k < n:
        if reverse:
            x = x + jnp.where(lane < n - k, pltpu.roll(x, n - k, 1), 0.0)
        else:
            x = x + jnp.where(lane >= k, pltpu.roll(x, k, 1), 0.0)
        k *= 2
    return x


def _log_sigmoid(z):
    return jnp.minimum(z, 0.0) - jnp.log(1.0 + jnp.exp(-jnp.abs(z)))


def _forget_fwd(f, bias):
    def body(f_ref, b_ref, c_ref):
        c_ref[...] = _lane_cumsum(_log_sigmoid(f_ref[...] + b_ref[...]), False)

    return pl.pallas_call(body, name="forget_fwd", out_shape=jax.ShapeDtypeStruct(f.shape, F32),
                          compiler_params=_params())(f, bias)


def _forget_bwd(f, bias, dc):
    def body(f_ref, b_ref, dc_ref, df_ref, db_ref):
        dlog = _lane_cumsum(dc_ref[...], True)
        df = dlog * jax.nn.sigmoid(-(f_ref[...] + b_ref[...]))
        df_ref[...] = df
        db_ref[...] = jnp.sum(df, axis=1, keepdims=True)

    return pl.pallas_call(body, name="forget_bwd",
                          out_shape=(jax.ShapeDtypeStruct(f.shape, F32), jax.ShapeDtypeStruct(bias.shape, F32)),
                          compiler_params=_params())(f, bias, dc)


FOX_BLOCK = 1024
FOX_KEYS = 1024
FOX_BWD_BLOCK = 512
_NT = _DIMS["nt"]
_TN = _DIMS["tn"]


N_PAIRS = N_FOX_HEADS // 2
V_BLOCK0 = 2 * N_PAIRS


def _left_lanes(shape):
    return lax.broadcasted_iota(jnp.int32, shape, 1) < HEAD_DIM


def _top_rows(shape):
    return lax.broadcasted_iota(jnp.int32, shape, 0) < HEAD_DIM


def _wide(c_tile, n):
    return c_tile if n == 128 else jnp.concatenate([c_tile] * (n // 128), axis=1)


def _fox_fwd(qn, kn, qkv, c_wide, seqs):
    t = qn.shape[0]
    l = t // seqs
    tb = min(FOX_BLOCK, l)
    tk = min(FOX_KEYS, tb)
    ratio = tb // tk
    nb = l // tb
    scale = HEAD_DIM ** -0.5

    def body(q_ref, k_ref, v_ref, ca_ref, cb_ref, o_ref, lse_ref, vt_ref):
        i = pl.program_id(2)
        top = _top_rows((128, tb))

        @pl.when(i == 0)
        def _():
            vt_ref[...] = v_ref[...].T.astype(BF16)

        qt = (q_ref[...].astype(F32) * scale).T.astype(BF16)
        zero = jnp.zeros_like(qt)
        qts = (jnp.where(top, qt, zero), jnp.where(top, zero, qt))
        top_k = _top_rows((128, tk))
        zero_k = jnp.zeros((128, tk), BF16)
        key_pos = lax.broadcasted_iota(jnp.int32, (tk, tb), 0)
        query_pos = lax.broadcasted_iota(jnp.int32, (tk, tb), 1)
        c_refs = (ca_ref, cb_ref)

        def scores(j):
            off = pl.multiple_of(j * tk, tk)
            k2 = k_ref[pl.ds(off, tk), :]
            return tuple(jnp.dot(k2, qts[h], preferred_element_type=F32) - _wide(c_refs[h][pl.ds(off, tk), :], tb)
                         for h in (0, 1))

        def values_times(ps, j):
            vt = vt_ref[:, pl.ds(pl.multiple_of(j * tk, tk), tk)]
            return (jnp.dot(jnp.where(top_k, vt, zero_k), ps[0], preferred_element_type=F32)
                    + jnp.dot(jnp.where(top_k, zero_k, vt), ps[1], preferred_element_type=F32))

        def softmax_step(sts, stats, first_key):
            ps, new, alphas = [], [], []
            for st, (m, s_sum) in zip(sts, stats):
                if first_key is not None:
                    st = jnp.where(key_pos + first_key <= query_pos, st, -jnp.inf)
                m_new = jnp.maximum(m, jnp.max(st, axis=0, keepdims=True))
                alpha = jnp.exp(m - m_new)
                p = jnp.exp(st - m_new)
                new.append((m_new, alpha * s_sum + jnp.sum(p, axis=0, keepdims=True)))
                alphas.append(alpha)
                ps.append(p.astype(BF16))
            return tuple(ps), tuple(new), jnp.where(top, alphas[0], alphas[1])

        def tile(j, carry, first_key):
            stats, acc = carry
            ps, stats, alpha = softmax_step(scores(j), stats, first_key)
            return stats, alpha * acc + values_times(ps, j)

        stat = (jnp.full((1, tb), -jnp.inf, F32), jnp.zeros((1, tb), F32))
        below = i * ratio
        carry = lax.fori_loop(0, below, lambda j, c: tile(j, c, None), ((stat, stat), jnp.zeros((128, tb), F32)))
        for r in range(ratio):
            carry = tile(below + r, carry, r * tk)
        ((ma, sa), (mb, sb)), acc = carry
        o_ref[...] = (acc / jnp.where(top, sa, sb)).T
        lse_ref[0:1, :] = ma + jnp.log(sa)
        lse_ref[1:2, :] = mb + jnp.log(sb)

    qblk = pl.BlockSpec((tb, 128), lambda b, hp, i: (b * nb + i, hp))
    return pl.pallas_call(
        body, name="fox_fwd", grid=(seqs, N_PAIRS, nb),
        in_specs=[qblk, pl.BlockSpec((l, 128), lambda b, hp, i: (b, hp)),
                  pl.BlockSpec((l, 128), lambda b, hp, i: (b, V_BLOCK0 + hp)),
                  pl.BlockSpec((None, l, 128), lambda b, hp, i: (b * N_FOX_HEADS + 2 * hp, 0, 0)),
                  pl.BlockSpec((None, l, 128), lambda b, hp, i: (b * N_FOX_HEADS + 2 * hp + 1, 0, 0))],
        out_specs=[qblk, pl.BlockSpec((None, 2, tb), lambda b, hp, i: (b * N_PAIRS + hp, 0, i))],
        out_shape=[jax.ShapeDtypeStruct((t, FOX_WIDTH), F32), jax.ShapeDtypeStruct((seqs * N_PAIRS, 2, l), F32)],
        scratch_shapes=[pltpu.VMEM((128, l), BF16)],
        compiler_params=_params(("parallel", "parallel", "arbitrary")),
    )(qn, kn, qkv, c_wide, c_wide)


def _fox_bwd(qn, kn, qkv, c_wide, o, do, lse, seqs):
    t = qn.shape[0]
    l = t // seqs
    tb = min(FOX_BWD_BLOCK, l)
    nb = l // tb
    scale = HEAD_DIM ** -0.5
    one_at = (HEAD_DIM, 0)

    def body(q_ref, k_ref, v_ref, ca_ref, cb_ref, o_ref, do_ref, lse_ref, dq_ref, dk_ref, dv_ref, dc_ref, dcq_ref,
             qt_ref, kt_ref, dot_ref, delta_ref, dqa_ref, dqb_ref):
        top_l = _top_rows((128, l))
        top = _top_rows((128, tb))
        left = _left_lanes((tb, 128))
        row_id = lax.broadcasted_iota(jnp.int32, (128, tb), 0)
        lane_id = lax.broadcasted_iota(jnp.int32, (tb, 128), 1)
        zero_t = jnp.zeros((128, tb), BF16)
        zero_l = jnp.zeros((tb, 128), BF16)
        rows = lambda a: (jnp.where(top, a, zero_t), jnp.where(top, zero_t, a))
        lanes = lambda a: (jnp.where(left, a, zero_l), jnp.where(left, zero_l, a))
        with_one_row = lambda pair: tuple(jnp.where(row_id == one_at[h], 1.0, pair[h]).astype(BF16) for h in (0, 1))
        with_one_lane = lambda pair: tuple(jnp.where(lane_id == one_at[h], 1.0, pair[h]).astype(BF16) for h in (0, 1))
        causal = lax.broadcasted_iota(jnp.int32, (tb, tb), 0) <= lax.broadcasted_iota(jnp.int32, (tb, tb), 1)
        c_refs = (ca_ref, cb_ref)
        dq_refs = (dqa_ref, dqb_ref)

        qt_ref[...] = (q_ref[...].astype(F32) * scale).T.astype(BF16)
        kt_ref[...] = k_ref[...].astype(F32).T.astype(BF16)
        do_t = do_ref[...].T
        dot_ref[...] = do_t.astype(BF16)
        prod_t = do_t * o_ref[...].T
        delta_ref[0:1, :] = jnp.sum(jnp.where(top_l, prod_t, 0.0), axis=0, keepdims=True)
        delta_ref[1:2, :] = jnp.sum(jnp.where(top_l, 0.0, prod_t), axis=0, keepdims=True)
        dqa_ref[...] = jnp.zeros(dqa_ref.shape, F32)
        dqb_ref[...] = jnp.zeros(dqb_ref.shape, F32)

        def kv_block(j, _):
            koff = pl.multiple_of(j * tb, tb)
            k2 = k_ref[pl.ds(koff, tb), :]
            v2 = v_ref[pl.ds(koff, tb), :].astype(BF16)
            kts = with_one_row(rows(kt_ref[:, pl.ds(koff, tb)]))
            cw = tuple(_wide(c_refs[h][pl.ds(koff, tb), :], tb) for h in (0, 1))

            def q_block(i, carry, masked):
                dks, dv = list(carry[:2]), carry[2]
                qoff = pl.multiple_of(i * tb, tb)
                qs = lanes((q_ref[pl.ds(qoff, tb), :].astype(F32) * scale).astype(BF16))
                qs_one = with_one_lane(qs)
                dos = lanes(do_ref[pl.ds(qoff, tb), :].astype(BF16))
                qts = rows(qt_ref[:, pl.ds(qoff, tb)])
                dots = rows(dot_ref[:, pl.ds(qoff, tb)])
                for h in (0, 1):
                    st = jnp.dot(k2, qts[h], preferred_element_type=F32) - cw[h]
                    p = jnp.exp(st - lse_ref[h:h + 1, pl.ds(qoff, tb)])
                    if masked:
                        p = jnp.where(causal, p, 0.0)
                    dp = jnp.dot(v2, dots[h], preferred_element_type=F32)
                    dsb = (p * (dp - delta_ref[h:h + 1, pl.ds(qoff, tb)])).astype(BF16)
                    dv = dv + jnp.dot(p.astype(BF16), dos[h], preferred_element_type=F32)
                    dks[h] = dks[h] + jnp.dot(dsb, qs_one[h], preferred_element_type=F32)
                    dq_refs[h][:, pl.ds(qoff, tb)] += jnp.dot(kts[h], dsb, preferred_element_type=F32)
                return dks[0], dks[1], dv

            z = jnp.zeros((tb, 128), F32)
            carry = q_block(j, (z, z, z), True)
            rest = nb - 1 - j
            carry = lax.fori_loop(
                0, rest // 2, lambda n, c: q_block(j + 2 + 2 * n, q_block(j + 1 + 2 * n, c, False), False), carry)
            dka, dkb, dv = lax.cond(rest % 2 == 1, lambda c: q_block(nb - 1, c, False), lambda c: c, carry)
            dk_ref[pl.ds(koff, tb), :] = jnp.where(left, dka, dkb)
            dv_ref[pl.ds(koff, tb), :] = dv
            dc_ref[0:1, pl.ds(koff, tb)] = -dka.T[one_at[0]:one_at[0] + 1, :]
            dc_ref[1:2, pl.ds(koff, tb)] = -dkb.T[one_at[1]:one_at[1] + 1, :]
            return 0

        lax.fori_loop(0, nb, kv_block, 0)
        dq_ref[...] = (jnp.where(top_l, dqa_ref[...], dqb_ref[...]) * scale).T
        dcq_ref[0:1, :] = dqa_ref[one_at[0]:one_at[0] + 1, :]
        dcq_ref[1:2, :] = dqb_ref[one_at[1]:one_at[1] + 1, :]

    blk = pl.BlockSpec((l, 128), lambda b, hp: (b, hp))
    cspec = lambda k: pl.BlockSpec((None, l, 128), lambda b, hp: (b * N_FOX_HEADS + 2 * hp + k, 0, 0))
    rows2 = pl.BlockSpec((None, 2, l), lambda b, hp: (b * N_PAIRS + hp, 0, 0))
    wide = jax.ShapeDtypeStruct((t, FOX_WIDTH), F32)
    pair_rows = jax.ShapeDtypeStruct((seqs * N_PAIRS, 2, l), F32)
    return pl.pallas_call(
        body, name="fox_bwd", grid=(seqs, N_PAIRS),
        in_specs=[blk, blk, pl.BlockSpec((l, 128), lambda b, hp: (b, V_BLOCK0 + hp)), cspec(0), cspec(1), blk, blk, rows2],
        out_specs=[blk, blk, blk, rows2, rows2],
        out_shape=[wide, wide, wide, pair_rows, pair_rows],
        scratch_shapes=[pltpu.VMEM((128, l), BF16), pltpu.VMEM((128, l), BF16), pltpu.VMEM((128, l), BF16),
                        pltpu.VMEM((2, l), F32), pltpu.VMEM((128, l), F32), pltpu.VMEM((128, l), F32)],
        compiler_params=_params(("parallel", "parallel")),
    )(qn, kn, qkv, c_wide, c_wide, o, do, lse)


SCAN_ROWS = 512
SCAN_COLS = 1024


S5_IN = 128
S5_ST = 512
SCAN_CHUNKS = SCAN_COLS // S5_ST
SCAN_SEGS = 8
LANES = 128


def _cmul(ar, ai, br, bi):
    return ar * br - ai * bi, ar * bi + ai * br


def _powers_into(pw_r, pw_i, a_r, a_i, seg):
    pw_r[0:1, :] = a_r
    pw_i[0:1, :] = a_i
    for k in range(1, seg):
        pr, pi = _cmul(pw_r[k - 1:k, :], pw_i[k - 1:k, :], a_r, a_i)
        pw_r[k:k + 1, :] = pr
        pw_i[k:k + 1, :] = pi


def _interleave(dst, src, seg):
    for h in range(src.shape[0]):
        for j in range(seg):
            dst[h, j * SCAN_SEGS:(j + 1) * SCAN_SEGS, :] = src[h, pl.ds(j, SCAN_SEGS, stride=seg), :]


def _deinterleave(dst, src, seg):
    for h in range(src.shape[0]):
        for j in range(seg):
            dst[h, pl.ds(j, SCAN_SEGS, stride=seg), :] = src[h, j * SCAN_SEGS:(j + 1) * SCAN_SEGS, :]


def _interleaved(ref, tmp_a, tmp_b, seg):
    n = ref.shape[1] // LANES
    for h in range(n):
        tmp_a[h] = ref[:, h * LANES:(h + 1) * LANES].astype(F32)
    _interleave(tmp_b, tmp_a, seg)
    return jnp.concatenate([tmp_b[h] for h in range(n)], axis=1)


def _store_deinterleaved(ref, val, tmp_a, tmp_b, seg):
    n = ref.shape[1] // LANES
    for h in range(n):
        tmp_a[h] = val[:, h * LANES:(h + 1) * LANES]
    _deinterleave(tmp_b, tmp_a, seg)
    for h in range(n):
        ref[:, h * LANES:(h + 1) * LANES] = tmp_b[h]


def _segment_scan(b_r, b_i, x_r, x_i, pw_r, pw_i, car_r, car_i, seg, sign, reverse, visit=None):
    nc = b_r.shape[0]
    sub = lax.broadcasted_iota(jnp.int32, (SCAN_SEGS, LANES), 0)
    lanes = lambda c: slice(c * LANES, (c + 1) * LANES)
    rows = lambda j: pl.ds(pl.multiple_of(((seg - 1 - j) if reverse else j) * SCAN_SEGS, SCAN_SEGS), SCAN_SEGS)
    a1 = [(pw_r[0:1, lanes(c)], sign * pw_i[0:1, lanes(c)]) for c in range(nc)]

    def local(j, xs):
        out = []
        for c in range(nc):
            xr, xi = xs[2 * c], xs[2 * c + 1]
            nr = a1[c][0] * xr - a1[c][1] * xi + b_r[c, rows(j), :]
            ni = a1[c][0] * xi + a1[c][1] * xr + b_i[c, rows(j), :]
            x_r[c, rows(j), :] = nr
            x_i[c, rows(j), :] = ni
            out += [nr, ni]
        return tuple(out)

    zero = jnp.zeros((SCAN_SEGS, LANES), F32)
    ends = lax.fori_loop(0, seg, local, (zero,) * (2 * nc))

    if reverse:
        first = sub == SCAN_SEGS - 1
        neighbour = lambda v: pltpu.roll(v, SCAN_SEGS - 1, 0)
        shift = lambda v, d: jnp.where(sub < SCAN_SEGS - d, pltpu.roll(v, SCAN_SEGS - d, 0), 0.0)
    else:
        first = sub == 0
        neighbour = lambda v: pltpu.roll(v, 1, 0)
        shift = lambda v, d: jnp.where(sub >= d, pltpu.roll(v, d, 0), 0.0)
    last = 0 if reverse else SCAN_SEGS - 1
    entries = []
    for c in range(nc):
        er, ei = ends[2 * c], ends[2 * c + 1]
        pr, pi = pw_r[seg - 1:seg, lanes(c)], sign * pw_i[seg - 1:seg, lanes(c)]
        yr = jnp.where(first, car_r[:, lanes(c)], neighbour(er))
        yi = jnp.where(first, car_i[:, lanes(c)], neighbour(ei))
        qr, qi = pr, pi
        for d in (1, 2, 4):
            mr, mi = _cmul(qr, qi, shift(yr, d), shift(yi, d))
            yr, yi = yr + mr, yi + mi
            qr, qi = _cmul(qr, qi, qr, qi)
        lr, li = _cmul(pr, pi, yr, yi)
        car_r[:, lanes(c)] = (er + lr)[last:last + 1, :]
        car_i[:, lanes(c)] = (ei + li)[last:last + 1, :]
        entries += [yr, yi]

    def correct(j, prev):
        out = []
        row_r, row_i = pw_r[pl.ds(j, 1), :], sign * pw_i[pl.ds(j, 1), :]
        for c in range(nc):
            mr, mi = _cmul(row_r[:, lanes(c)], row_i[:, lanes(c)], entries[2 * c], entries[2 * c + 1])
            nr = x_r[c, rows(j), :] + mr
            ni = x_i[c, rows(j), :] + mi
            x_r[c, rows(j), :] = nr
            x_i[c, rows(j), :] = ni
            if visit is not None:
                visit(c, rows(j), prev[2 * c], prev[2 * c + 1])
            out += [nr, ni]
        return tuple(out)

    lax.fori_loop(0, seg, correct, tuple(entries))


def _s5_fwd(uf, bbr, bbi, cr, ci, ar, ai, seqs):
    t = uf.shape[0]
    l = t // seqs
    tl = min(SCAN_ROWS, l)
    nl = l // tl
    seg = tl // SCAN_SEGS
    cb, nq = SCAN_COLS, SCAN_CHUNKS
    nc = cb // LANES
    per = S5_ST // LANES

    def body(u_ref, bbr_ref, bbi_ref, cr_ref, ci_ref, ar_ref, ai_ref, x_r, x_i, ys_ref,
             car_r, car_i, pw_r, pw_i, b_r, b_i, tmp_a, tmp_b):
        @pl.when(pl.program_id(2) == 0)
        def _():
            car_r[...] = jnp.zeros(car_r.shape, F32)
            car_i[...] = jnp.zeros(car_i.shape, F32)
            _powers_into(pw_r, pw_i, ar_ref[...], ai_ref[...], seg)

        u = _interleaved(u_ref, tmp_a, tmp_b, seg).astype(BF16)
        for q in range(nq):
            uq = u[:, q * S5_IN:(q + 1) * S5_IN]
            br = jnp.dot(uq, bbr_ref[q], preferred_element_type=F32)
            bi = jnp.dot(uq, bbi_ref[q], preferred_element_type=F32)
            for s in range(per):
                b_r[q * per + s] = br[:, s * LANES:(s + 1) * LANES]
                b_i[q * per + s] = bi[:, s * LANES:(s + 1) * LANES]
        _segment_scan(b_r, b_i, x_r, x_i, pw_r, pw_i, car_r, car_i, seg, 1.0, False)
        wide = lambda buf, q: jnp.concatenate([buf[q * per + s] for s in range(per)], axis=1).astype(BF16)
        ys = [jnp.dot(wide(x_r, q), cr_ref[q], preferred_element_type=F32)
              + jnp.dot(wide(x_i, q), ci_ref[q], preferred_element_type=F32) for q in range(nq)]
        _store_deinterleaved(ys_ref, jnp.concatenate(ys, axis=1), tmp_a, tmp_b, seg)

    rows = lambda w: pl.BlockSpec((tl, w), lambda s, j, r: (s * nl + r, j))
    state = pl.BlockSpec((nc, tl, LANES), lambda s, j, r: (j, s * nl + r, 0))
    chunk = lambda a: pl.BlockSpec((nq,) + a.shape[1:], lambda s, j, r: (j, 0, 0))
    par = pl.BlockSpec((1, cb), lambda s, j, r: (0, j))
    return pl.pallas_call(
        body, name="s5_fwd", grid=(seqs, S5_CH // cb, nl),
        in_specs=[rows(nq * S5_IN), chunk(bbr), chunk(bbi), chunk(cr), chunk(ci), par, par],
        out_specs=[state, state, rows(nq * S5_IN)],
        out_shape=[jax.ShapeDtypeStruct((S5_CH // LANES, t, LANES), F32)] * 2
        + [jax.ShapeDtypeStruct((t, S5_WIDTH), F32)],
        scratch_shapes=[pltpu.VMEM((1, cb), F32), pltpu.VMEM((1, cb), F32), pltpu.VMEM((seg, cb), F32),
                        pltpu.VMEM((seg, cb), F32)] + [pltpu.VMEM((nc, tl, LANES), F32)] * 2
        + [pltpu.VMEM((nq * S5_IN // LANES, tl, LANES), F32)] * 2,
        compiler_params=_params(("parallel", "parallel", "arbitrary")),
    )(uf, bbr, bbi, cr, ci, ar, ai)


def _s5_bwd(dys, uf, xr, xi, bbr, bbi, cr, ci, ar, ai, seqs):
    t = dys.shape[0]
    l = t // seqs
    tl = min(SCAN_ROWS, l)
    nl = l // tl
    seg = tl // SCAN_SEGS
    cb, nq = SCAN_COLS, SCAN_CHUNKS
    nc = cb // LANES
    per = S5_ST // LANES

    def body(dy_ref, u_ref, x_r, x_i, bbr_ref, bbi_ref, cr_ref, ci_ref, ar_ref, ai_ref,
             du_ref, dbbr_ref, dbbi_ref, dcr_ref, dci_ref, dar_ref, dai_ref,
             car_r, car_i, pw_r, pw_i, g_r, g_i, lam_r, lam_i, acc_r, acc_i, tmp_a, tmp_b):
        @pl.when(pl.program_id(2) == 0)
        def _():
            car_r[...] = jnp.zeros(car_r.shape, F32)
            car_i[...] = jnp.zeros(car_i.shape, F32)
            _powers_into(pw_r, pw_i, ar_ref[...], ai_ref[...], seg)
            for acc_ref in (dbbr_ref, dbbi_ref, dcr_ref, dci_ref, dar_ref, dai_ref):
                acc_ref[...] = jnp.zeros(acc_ref.shape, F32)

        dy = _interleaved(dy_ref, tmp_a, tmp_b, seg).astype(BF16)
        for q in range(nq):
            dyq = dy[:, q * S5_IN:(q + 1) * S5_IN]
            gr = lax.dot_general(dyq, cr_ref[q], _NT, preferred_element_type=F32)
            gi = lax.dot_general(dyq, ci_ref[q], _NT, preferred_element_type=F32)
            for s in range(per):
                g_r[q * per + s] = gr[:, s * LANES:(s + 1) * LANES]
                g_i[q * per + s] = gi[:, s * LANES:(s + 1) * LANES]
        acc_r[...] = jnp.zeros(acc_r.shape, F32)
        acc_i[...] = jnp.zeros(acc_i.shape, F32)

        def visit(c, rws, lr, li):
            xr_t, xi_t = x_r[c, rws, :], x_i[c, rws, :]
            acc_r[c] += lr * xr_t + li * xi_t
            acc_i[c] += li * xr_t - lr * xi_t

        _segment_scan(g_r, g_i, lam_r, lam_i, pw_r, pw_i, car_r, car_i, seg, -1.0, True, visit)
        for c in range(nc):
            dar_ref[:, c * LANES:(c + 1) * LANES] += jnp.sum(acc_r[c], axis=0, keepdims=True)
            dai_ref[:, c * LANES:(c + 1) * LANES] += jnp.sum(acc_i[c], axis=0, keepdims=True)
        u = _interleaved(u_ref, tmp_a, tmp_b, seg).astype(BF16)
        wide = lambda buf, q: jnp.concatenate([buf[q * per + s] for s in range(per)], axis=1).astype(BF16)
        du = []
        for q in range(nq):
            io = slice(q * S5_IN, (q + 1) * S5_IN)
            lq_r, lq_i = wide(lam_r, q), wide(lam_i, q)
            du.append(lax.dot_general(lq_r, bbr_ref[q], _NT, preferred_element_type=F32)
                      + lax.dot_general(lq_i, bbi_ref[q], _NT, preferred_element_type=F32))
            dbbr_ref[q] += lax.dot_general(u[:, io], lq_r, _TN, preferred_element_type=F32)
            dbbi_ref[q] += lax.dot_general(u[:, io], lq_i, _TN, preferred_element_type=F32)
            dcr_ref[q] += lax.dot_general(wide(x_r, q), dy[:, io], _TN, preferred_element_type=F32)
            dci_ref[q] += lax.dot_general(wide(x_i, q), dy[:, io], _TN, preferred_element_type=F32)
        _store_deinterleaved(du_ref, jnp.concatenate(du, axis=1), tmp_a, tmp_b, seg)

    rows = lambda w: pl.BlockSpec((tl, w), lambda s, j, r: (s * nl + nl - 1 - r, j))
    state = pl.BlockSpec((nc, tl, LANES), lambda s, j, r: (j, s * nl + nl - 1 - r, 0))
    chunk = lambda a: pl.BlockSpec((nq,) + a.shape[1:], lambda s, j, r: (j, 0, 0))
    acc = lambda a: pl.BlockSpec((None, nq) + a.shape[1:], lambda s, j, r: (s, j, 0, 0))
    par = pl.BlockSpec((1, cb), lambda s, j, r: (0, j))
    par_acc = pl.BlockSpec((None, 1, cb), lambda s, j, r: (s, 0, j))
    per_seq = lambda a: jax.ShapeDtypeStruct((seqs,) + a.shape, F32)
    return pl.pallas_call(
        body, name="s5_bwd", grid=(seqs, S5_CH // cb, nl),
        in_specs=[rows(nq * S5_IN), rows(nq * S5_IN), state, state, chunk(bbr), chunk(bbi), chunk(cr), chunk(ci),
                  par, par],
        out_specs=[rows(nq * S5_IN), acc(bbr), acc(bbi), acc(cr), acc(ci), par_acc, par_acc],
        out_shape=[jax.ShapeDtypeStruct((t, S5_WIDTH), F32), per_seq(bbr), per_seq(bbi), per_seq(cr), per_seq(ci),
                   jax.ShapeDtypeStruct((seqs, 1, S5_CH), F32), jax.ShapeDtypeStruct((seqs, 1, S5_CH), F32)],
        scratch_shapes=[pltpu.VMEM((1, cb), F32), pltpu.VMEM((1, cb), F32), pltpu.VMEM((seg, cb), F32),
                        pltpu.VMEM((seg, cb), F32)] + [pltpu.VMEM((nc, tl, LANES), F32)] * 4
        + [pltpu.VMEM((nc, SCAN_SEGS, LANES), F32)] * 2 + [pltpu.VMEM((nq * S5_IN // LANES, tl, LANES), F32)] * 2,
        compiler_params=_params(("parallel", "parallel", "arbitrary")),
    )(dys, uf, xr, xi, bbr, bbi, cr, ci, ar, ai)


XATT_BLOCK = 2048


def _xatt_probs(qv, kv):
    s = lax.dot_general(qv, kv, _NT, preferred_element_type=F32) * (X_HEAD_DIM ** -0.5)
    e = jnp.exp(s - jnp.max(s, axis=-1, keepdims=True))
    return e / jnp.sum(e, axis=-1, keepdims=True)


def _xatt_fwd(q, k, kv, seqs):
    t = q.shape[0]
    tq = min(XATT_BLOCK, t // seqs)
    nq = t // seqs // tq

    def body(q_ref, k_ref, v_ref, o_ref):
        p = _xatt_probs(q_ref[...], k_ref[...])
        o_ref[...] = jnp.dot(p.astype(BF16), v_ref[...].astype(BF16), preferred_element_type=F32).astype(o_ref.dtype)

    qs = pl.BlockSpec((tq, X_HEAD_DIM), lambda b, h, i: (b * nq + i, h))
    return pl.pallas_call(
        body, name="xatt_fwd", grid=(seqs, N_X_HEADS, nq),
        in_specs=[qs, pl.BlockSpec((N_MEM, X_HEAD_DIM), lambda b, h, i: (b, h)),
                  pl.BlockSpec((N_MEM, X_HEAD_DIM), lambda b, h, i: (b, N_X_HEADS + h))],
        out_specs=qs, out_shape=jax.ShapeDtypeStruct(q.shape, BF16),
        compiler_params=_params(("parallel", "parallel", "parallel")),
    )(q, k, kv)


def _xatt_bwd(q, k, kv, do, seqs):
    t = q.shape[0]
    tq = min(XATT_BLOCK, t // seqs)
    nq = t // seqs // tq
    scale = X_HEAD_DIM ** -0.5

    def body(q_ref, k_ref, v_ref, do_ref, dq_ref, dk_ref, dv_ref):
        @pl.when(pl.program_id(2) == 0)
        def _():
            dk_ref[...] = jnp.zeros(dk_ref.shape, F32)
            dv_ref[...] = jnp.zeros(dv_ref.shape, F32)

        qv, kk = q_ref[...], k_ref[...]
        p = _xatt_probs(qv, kk)
        dob = do_ref[...].astype(BF16)
        dp = lax.dot_general(dob, v_ref[...].astype(BF16), _NT, preferred_element_type=F32)
        ds = p * (dp - jnp.sum(dp * p, axis=-1, keepdims=True))
        dsb = ds.astype(BF16)
        dq_ref[...] = jnp.dot(dsb, kk, preferred_element_type=F32) * scale
        dk_ref[...] += lax.dot_general(dsb, qv, _TN, preferred_element_type=F32) * scale
        dv_ref[...] += lax.dot_general(p.astype(BF16), dob, _TN, preferred_element_type=F32)

    qs = pl.BlockSpec((tq, X_HEAD_DIM), lambda b, h, i: (b * nq + i, h))
    ks = pl.BlockSpec((N_MEM, X_HEAD_DIM), lambda b, h, i: (b, h))
    return pl.pallas_call(
        body, name="xatt_bwd", grid=(seqs, N_X_HEADS, nq),
        in_specs=[qs, ks, pl.BlockSpec((N_MEM, X_HEAD_DIM), lambda b, h, i: (b, N_X_HEADS + h)), qs],
        out_specs=[qs, ks, ks],
        out_shape=[jax.ShapeDtypeStruct(q.shape, F32), jax.ShapeDtypeStruct(k.shape, F32),
                   jax.ShapeDtypeStruct(k.shape, F32)],
        compiler_params=_params(("parallel", "parallel", "arbitrary")),
    )(q, k, kv, do)


CONV_COLS = 256


def _shift_down(x, k, row):
    return jnp.where(row >= k, pltpu.roll(x, k, 0), 0.0)


def _shift_up(x, k, row):
    n = x.shape[0]
    return jnp.where(row < n - k, pltpu.roll(x, n - k, 0), 0.0)


def _down_from(x, prev, k, row):
    return jnp.where(row >= k, pltpu.roll(x, k, 0), pltpu.roll(prev, k, 0))


GATE_ROWS = 512


def _ffn_up_gate(hn, w_up, w, b, seqs):
    t = hn.shape[0]
    l = t // seqs
    nc = D_FF // CONV_COLS

    rc = min(GATE_ROWS, l)

    def body(a_ref, wg_ref, wu_ref, w_ref, b_ref, g_ref, u_ref, p_ref, o_ref):
        wv, bias = w_ref[...], b_ref[...]
        row = lax.broadcasted_iota(jnp.int32, (rc, CONV_COLS), 0)
        prev = jnp.zeros((rc, CONV_COLS), F32)
        for k in range(l // rc):
            rows = slice(k * rc, (k + 1) * rc)
            a = a_ref[rows, :]
            gb = jnp.dot(a, wg_ref[...], preferred_element_type=F32).astype(BF16)
            ub = jnp.dot(a, wu_ref[...], preferred_element_type=F32).astype(BF16)
            g_ref[rows, :] = gb
            u_ref[rows, :] = ub
            g = gb.astype(F32)
            pre = bias + wv[0:1, :] * _down_from(g, prev, 2, row) + wv[1:2, :] * _down_from(g, prev, 1, row) \
                + wv[2:3, :] * g
            p_ref[rows, :] = pre.astype(p_ref.dtype)
            o_ref[rows, :] = (pre * jax.nn.sigmoid(pre) * ub.astype(F32)).astype(o_ref.dtype)
            prev = g

    cols = pl.BlockSpec((l, CONV_COLS), lambda s, j: (s, j))
    half = jax.ShapeDtypeStruct((t, D_FF), BF16)
    return pl.pallas_call(
        body, name="ffn_up_gate", grid=(seqs, nc),
        in_specs=[pl.BlockSpec((l, hn.shape[1]), lambda s, j: (s, 0)),
                  pl.BlockSpec((hn.shape[1], CONV_COLS), lambda s, j: (0, j)),
                  pl.BlockSpec((hn.shape[1], CONV_COLS), lambda s, j: (0, nc + j)),
                  pl.BlockSpec((3, CONV_COLS), lambda s, j: (0, j)), pl.BlockSpec((1, CONV_COLS), lambda s, j: (0, j))],
        out_specs=[cols] * 4, out_shape=[half] * 4,
        compiler_params=_params(("parallel", "parallel")),
    )(hn, w_up, w_up, w, b)


def _ffn_down_dx_gate(dh, w_down, gate, up, pre, w, seqs):
    t = dh.shape[0]
    l = t // seqs
    nc = D_FF // CONV_COLS
    steps = nc * seqs

    def body(dh_ref, wd_ref, g_ref, u_ref, p_ref, w_ref, dgu_ref, dw_ref, db_ref, stage, sems):
        s, j = pl.program_id(0), pl.program_id(1)
        n = s * nc + j
        slot = n % 2

        def copies(slot_, j_, s_):
            rows = pl.ds(pl.multiple_of(s_ * l, 16), l)
            return [pltpu.make_async_copy(
                stage.at[slot_, half],
                dgu_ref.at[rows, pl.ds(pl.multiple_of((half * nc + j_) * CONV_COLS, 128), CONV_COLS)],
                sems.at[slot_, half]) for half in (0, 1)]

        @pl.when(n >= 2)
        def _():
            for cp in copies(slot, j, s):
                cp.wait()

        da = lax.dot_general(dh_ref[...], wd_ref[...], _NT, preferred_element_type=F32)
        g, pre, wv = g_ref[...].astype(F32), p_ref[...].astype(F32), w_ref[...]
        row = lax.broadcasted_iota(jnp.int32, g.shape, 0)
        sg = jax.nn.sigmoid(pre)
        silu = pre * sg
        stage[slot, 1] = (da * silu).astype(stage.dtype)
        dpre = da * u_ref[...].astype(F32) * (sg * (1.0 + pre * (1.0 - sg)))
        dpre1, dpre2 = _shift_up(dpre, 1, row), _shift_up(dpre, 2, row)
        dg = wv[2:3, :] * dpre + wv[1:2, :] * dpre1 + wv[0:1, :] * dpre2
        stage[slot, 0] = dg.astype(stage.dtype)
        for cp in copies(slot, j, s):
            cp.start()
        dw_ref[0:1, :] = jnp.sum(dpre2 * g, axis=0, keepdims=True)
        dw_ref[1:2, :] = jnp.sum(dpre1 * g, axis=0, keepdims=True)
        dw_ref[2:3, :] = jnp.sum(dpre * g, axis=0, keepdims=True)
        db_ref[...] = jnp.sum(dpre, axis=0, keepdims=True)

        @pl.when(n == steps - 1)
        def _():
            for cp in copies(slot, j, s) + (copies(1 - slot, j, s) if steps > 1 else []):
                cp.wait()

    cols = pl.BlockSpec((l, CONV_COLS), lambda s, j: (s, j))
    return pl.pallas_call(
        body, name="ffn_down_dx_gate", grid=(seqs, nc),
        in_specs=[pl.BlockSpec((l, dh.shape[1]), lambda s, j: (s, 0)),
                  pl.BlockSpec((CONV_COLS, dh.shape[1]), lambda s, j: (j, 0)), cols, cols, cols,
                  pl.BlockSpec((3, CONV_COLS), lambda s, j: (0, j))],
        out_specs=[ANY, pl.BlockSpec((None, 3, CONV_COLS), lambda s, j: (s, 0, j)),
                   pl.BlockSpec((None, 1, CONV_COLS), lambda s, j: (s, 0, j))],
        out_shape=[jax.ShapeDtypeStruct((t, 2 * D_FF), BF16), jax.ShapeDtypeStruct((seqs, 3, D_FF), F32),
                   jax.ShapeDtypeStruct((seqs, 1, D_FF), F32)],
        scratch_shapes=[pltpu.VMEM((2, 2, l, CONV_COLS), BF16), pltpu.SemaphoreType.DMA((2, 2))],
        compiler_params=_params(("arbitrary", "arbitrary")),
    )(dh, w_down, gate, up, pre, w)


def _loss_head(h, target):
    t, d = h.shape
    tm = _pick(t, (256, 128, 8))

    def body(h_ref, t_ref, dh_ref, dhb_ref, loss_ref):
        @pl.when(pl.program_id(0) == 0)
        def _():
            loss_ref[...] = jnp.zeros(loss_ref.shape, F32)

        e = h_ref[...] - t_ref[...]
        dh = e * (1.0 / d)
        dh_ref[...] = dh
        dhb_ref[...] = dh.astype(BF16)
        loss_ref[...] += (0.5 / d) * jnp.sum(jnp.sum(e * e, axis=1, keepdims=True), axis=0, keepdims=True)

    blk = pl.BlockSpec((tm, d), lambda i: (i, 0))
    return pl.pallas_call(
        body, name="loss_head", grid=(t // tm,), in_specs=[blk, blk],
        out_specs=[blk, blk, pl.BlockSpec((1, 1), lambda i: (0, 0))],
        out_shape=[jax.ShapeDtypeStruct((t, d), F32), jax.ShapeDtypeStruct((t, d), BF16),
                   jax.ShapeDtypeStruct((1, 1), F32)],
        compiler_params=_params(("arbitrary",)),
    )(h, target)


def _s5_discretise(a_re, a_im, log_dt, b_re, b_im):
    dt = jnp.exp(log_dt)[:, None]
    mag = jnp.exp(a_re * dt)
    lb_r = mag * jnp.cos(a_im * dt)
    lb_i = mag * jnp.sin(a_im * dt)
    den = a_re * a_re + a_im * a_im
    nr = lb_r - 1.0
    coef_r = (nr * a_re + lb_i * a_im) / den
    coef_i = (lb_i * a_re - nr * a_im) / den
    bb_r = coef_r[:, :, None] * b_re - coef_i[:, :, None] * b_im
    bb_i = coef_r[:, :, None] * b_im + coef_i[:, :, None] * b_re
    return lb_r, lb_i, bb_r, bb_i


S5_CHUNKS = 4
S5_PER = S5_GROUPS // S5_CHUNKS


def _blockdiag_in(bb):
    eye = jnp.eye(S5_PER, dtype=bb.dtype)
    return jnp.einsum("jgpc,gh->jgchp", bb.reshape(S5_CHUNKS, S5_PER, S5_STATE, S5_GROUP_CH), eye).reshape(
        S5_CHUNKS, S5_PER * S5_GROUP_CH, S5_PER * S5_STATE)


def _blockdiag_in_grad(d):
    eye = jnp.eye(S5_PER, dtype=d.dtype)
    return jnp.einsum("jgchp,gh->jgpc", d.reshape(S5_CHUNKS, S5_PER, S5_GROUP_CH, S5_PER, S5_STATE), eye).reshape(
        S5_GROUPS, S5_STATE, S5_GROUP_CH)


def _blockdiag_out(c):
    eye = jnp.eye(S5_PER, dtype=c.dtype)
    return jnp.einsum("jgcp,gh->jgphc", c.reshape(S5_CHUNKS, S5_PER, S5_GROUP_CH, S5_STATE), eye).reshape(
        S5_CHUNKS, S5_PER * S5_STATE, S5_PER * S5_GROUP_CH)


def _blockdiag_out_grad(d):
    eye = jnp.eye(S5_PER, dtype=d.dtype)
    return jnp.einsum("jgphc,gh->jgcp", d.reshape(S5_CHUNKS, S5_PER, S5_STATE, S5_PER, S5_GROUP_CH), eye).reshape(
        S5_GROUPS, S5_GROUP_CH, S5_STATE)


def _local_step(x3, mem3, target3, p, wb, late_weights=None, early_grads=None):
    seqs, l, d = x3.shape
    t = seqs * l
    x = x3.reshape(t, d)
    mem = mem3.reshape(seqs * N_MEM, d)
    target = target3.reshape(t, d)
    full = lambda a: (a, a.shape[1], 0, 0)

    s5_in = (p["s5_a_re"], p["s5_a_im"], p["s5_log_dt"], p["s5_b_re"], p["s5_b_im"])
    (lb_r, lb_i, bb_r, bb_i), s5_pull = jax.vjp(_s5_discretise, *s5_in)
    ar, ai = lb_r.reshape(1, S5_CH), lb_i.reshape(1, S5_CH)
    bbr_d, bbi_d = _blockdiag_in(bb_r).astype(BF16), _blockdiag_in(bb_i).astype(BF16)
    cr_d, ci_d = _blockdiag_out(p["s5_c_re"]).astype(BF16), (-_blockdiag_out(p["s5_c_im"])).astype(BF16)
    d_row = p["s5_d"].reshape(1, S5_WIDTH)

    hn1 = _rowwise(_rms, [full(x)], [p["norm_mix"]], [(d, d, 0, BF16)], "norm_mix_fwd")
    if late_weights is not None:
        wb = dict(wb, **late_weights("first", hn1))
    conv_w = wb["ffn_conv_w"] if "ffn_conv_w" in wb else p["ffn_conv_w"]
    w_in = wb["w_in"]
    w_qkv = w_in[:, :3 * FOX_WIDTH]
    w_uf = jnp.concatenate(
        [w_in[:, 3 * FOX_WIDTH + N_FOX_HEADS:], w_in[:, 3 * FOX_WIDTH:3 * FOX_WIDTH + N_FOX_HEADS],
         jnp.zeros((d, UF_COLS - S5_WIDTH - N_FOX_HEADS), w_in.dtype)], axis=1)
    qkv = _mm(hn1, w_qkv, "nn", "in_qkv")
    uf = _mm(hn1, w_uf, "nn", "in_uf")

    bh = seqs * N_FOX_HEADS
    q_pair = (qkv, 128, 0, 1)
    k_pair = (qkv, 128, N_PAIRS, 1)
    gq2, gk2 = jnp.tile(p["fox_q_norm"], (1, 2)), jnp.tile(p["fox_k_norm"], (1, 2))
    pair_out = [(FOX_WIDTH, 128, 1, BF16)]
    qn = _rowwise(_rms_pair, [q_pair], [gq2], pair_out, "fox_qnorm_fwd", heads=N_PAIRS)
    kn = _rowwise(_rms_pair, [k_pair], [gk2], pair_out, "fox_knorm_fwd", heads=N_PAIRS)

    f_rows = uf[:, S5_WIDTH:S5_WIDTH + N_FOX_HEADS].reshape(seqs, l, N_FOX_HEADS).transpose(0, 2, 1).reshape(bh, l)
    f_bias = jnp.tile(p["fox_f_bias"].reshape(N_FOX_HEADS, 1), (seqs, 1))
    c_wide = jnp.broadcast_to(_forget_fwd(f_rows, f_bias)[:, :, None], (bh, l, 128))
    fox, lse = _fox_fwd(qn, kn, qkv, c_wide, seqs)

    xr, xi, ys = _s5_fwd(uf, bbr_d, bbi_d, cr_d, ci_d, ar, ai, seqs)
    u_blk = (uf, S5_WIDTH, 0, 0)
    yg = _rowwise(_s5_act, [full(ys), u_blk], [d_row], [(S5_WIDTH, S5_WIDTH, 0, F32)], "s5_act_fwd")
    if late_weights is not None:
        wb = dict(wb, **late_weights("mid", yg))
    z = _mm(yg, wb["s5_w_glu"], "nn", "s5_glu")
    y2n = _rowwise(_s5_gate, [full(yg), full(z)], [p["s5_b_glu"], p["out_norm_s5"]],
                   [(S5_WIDTH, S5_WIDTH, 0, BF16)], "s5_gate_fwd")
    foxn = _rowwise(_rms, [full(fox)], [p["out_norm_fox"]], [(FOX_WIDTH, FOX_WIDTH, 0, BF16)], "fox_outnorm_fwd")
    mixed = jnp.concatenate([foxn, y2n], axis=1)
    h1 = _mm(mixed, wb["w_out"], "nn", "mix_out", res=x)
    if late_weights is not None:
        wb = dict(wb, **late_weights("late", h1))

    hn2 = _rowwise(_rms, [full(h1)], [p["norm_cross"]], [(d, d, 0, BF16)], "norm_cross_fwd")
    mn = _rowwise(_rms, [full(mem)], [p["norm_mem"]], [(d, d, 0, BF16)], "norm_mem_fwd")
    xq_raw = _mm(hn2, wb["w_xq"], "nn", "x_q")
    kv = _mm(mn, wb["w_xkv"], "nn", "x_kv")
    xh = lambda a: (a, X_HEAD_DIM, 0, 1)
    xqn = _rowwise(_rms, [xh(xq_raw)], [p["xq_norm"]], [(d, X_HEAD_DIM, 1, BF16)], "x_qnorm_fwd", heads=N_X_HEADS)
    xkn = _rowwise(_rms, [xh(kv)], [p["xk_norm"]], [(d, X_HEAD_DIM, 1, BF16)], "x_knorm_fwd", heads=N_X_HEADS)
    xo = _xatt_fwd(xqn, xkn, kv, seqs)
    h2 = _mm(xo, wb["w_xo"], "nn", "x_out", res=h1)

    hn3 = _rowwise(_rms, [full(h2)], [p["norm_ffn"]], [(d, d, 0, BF16)], "norm_ffn_fwd")
    gate, up, pre, act = _ffn_up_gate(hn3, wb["w_ffn_up"], conv_w, p["ffn_conv_b"], seqs)
    h3 = _mm(act, wb["w_ffn_down"], "nn", "ffn_down", res=h2)
    dh3, dh3_b, loss = _loss_head(h3, target)

    g = {}
    late_dt = BF16 if early_grads is not None else F32
    g["w_ffn_down"] = _mm(act, dh3_b, "tn", "ffn_down_dw", out_dtype=late_dt)
    dgu, dconv_w, dconv_b = _ffn_down_dx_gate(dh3_b, wb["w_ffn_down"], gate, up, pre, conv_w, seqs)
    g["ffn_conv_w"], g["ffn_conv_b"] = jnp.sum(dconv_w, axis=0), jnp.sum(dconv_b, axis=0)
    dhn3 = _mm(dgu, wb["w_ffn_up"], "nt", "ffn_up_dx", out_dtype=BF16)
    g["w_ffn_up"] = _mm(hn3, dgu, "tn", "ffn_up_dw", out_dtype=late_dt)
    (dh2,), (g["norm_ffn"],) = _rowwise_vjp(_rms, [full(h2)], [p["norm_ffn"]], [full(dhn3)], "norm_ffn_bwd",
                                            adds=[full(dh3)])

    dxo = _mm(dh2, wb["w_xo"], "nt", "x_out_dx", out_dtype=BF16)
    g["w_xo"] = _mm(xo, dh2, "tn", "x_out_dw", out_dtype=late_dt)
    dxqn, dxkn, dxv = _xatt_bwd(xqn, xkn, kv, dxo, seqs)
    (dxq_raw,), (g["xq_norm"],) = _rowwise_vjp(_rms, [xh(xq_raw)], [p["xq_norm"]], [xh(dxqn)], "x_qnorm_bwd",
                                               heads=N_X_HEADS, row_dtypes=[BF16])
    (dxk_raw,), (g["xk_norm"],) = _rowwise_vjp(_rms, [xh(kv)], [p["xk_norm"]], [xh(dxkn)], "x_knorm_bwd",
                                               heads=N_X_HEADS, row_dtypes=[BF16])
    dkv = jnp.concatenate([dxk_raw, dxv.astype(BF16)], axis=1)
    dhn2 = _mm(dxq_raw, wb["w_xq"], "nt", "x_q_dx", out_dtype=BF16)
    g["w_xq"] = _mm(hn2, dxq_raw, "tn", "x_q_dw", out_dtype=late_dt)
    dmn = _mm(dkv, wb["w_xkv"], "nt", "x_kv_dx")
    g["w_xkv"] = _mm(mn, dkv, "tn", "x_kv_dw", out_dtype=late_dt)
    norm_cross = p["norm_cross"]
    if early_grads is not None:
        norm_cross = norm_cross + early_grads("late", {n: g[n] for n in LATE_WEIGHTS})[0:1, 0:1]
    (dh1,), (g["norm_cross"],) = _rowwise_vjp(_rms, [full(h1)], [norm_cross], [full(dhn2)], "norm_cross_bwd",
                                              adds=[full(dh2)])
    _, (g["norm_mem"],) = _rowwise_vjp(_rms, [full(mem)], [p["norm_mem"]], [full(dmn)], "norm_mem_bwd",
                                       row_dtypes=[BF16])

    dmixed = _mm(dh1, wb["w_out"], "nt", "mix_out_dx", out_dtype=BF16)
    g["w_out"] = _mm(mixed, dh1, "tn", "mix_out_dw", out_dtype=late_dt)
    (dfox,), (g["out_norm_fox"],) = _rowwise_vjp(_rms, [full(fox)], [p["out_norm_fox"]],
                                                 [(dmixed, FOX_WIDTH, 0, 0)], "fox_outnorm_bwd")
    (dyg_a, dz), (g["s5_b_glu"], g["out_norm_s5"]) = _rowwise_vjp(
        _s5_gate, [full(yg), full(z)], [p["s5_b_glu"], p["out_norm_s5"]], [(dmixed, S5_WIDTH, 1, 0)], "s5_gate_bwd",
        row_dtypes=[F32, BF16])
    dyg = _mm(dz, wb["s5_w_glu"], "nt", "s5_glu_dx", res=dyg_a)
    g["s5_w_glu"] = _mm(yg, dz, "tn", "s5_glu_dw", out_dtype=late_dt)
    if early_grads is not None:
        d_row = d_row + early_grads("mid", {n: g[n] for n in MID_WEIGHTS})[0:1, 0:1]
    (dys, du_a), (dd_row,) = _rowwise_vjp(_s5_act, [full(ys), u_blk], [d_row], [full(dyg)], "s5_act_bwd",
                                          row_dtypes=[BF16, F32])
    g["s5_d"] = dd_row
    du_b, dbbr_d, dbbi_d, dcr_d, dci_d, dar, dai = _s5_bwd(dys, uf, xr, xi, bbr_d, bbi_d, cr_d, ci_d, ar, ai, seqs)
    dbbr_d, dbbi_d, dcr_d, dci_d = (jnp.sum(a, axis=0) for a in (dbbr_d, dbbi_d, dcr_d, dci_d))
    d_lb_r = jnp.sum(dar, axis=0).reshape(S5_GROUPS, S5_STATE)
    d_lb_i = jnp.sum(dai, axis=0).reshape(S5_GROUPS, S5_STATE)
    g["s5_a_re"], g["s5_a_im"], g["s5_log_dt"], g["s5_b_re"], g["s5_b_im"] = s5_pull(
        (d_lb_r, d_lb_i, _blockdiag_in_grad(dbbr_d), _blockdiag_in_grad(dbbi_d)))
    g["s5_c_re"] = _blockdiag_out_grad(dcr_d)
    g["s5_c_im"] = -_blockdiag_out_grad(dci_d)

    dqn, dkn, dv, dc, dcq = _fox_bwd(qn, kn, qkv, c_wide, fox, dfox, lse, seqs)
    pair = lambda a: (a, 128, 0, 1)
    (dq_raw,), (dgq2,) = _rowwise_vjp(_rms_pair, [q_pair], [gq2], [pair(dqn)], "fox_qnorm_bwd", heads=N_PAIRS,
                                      row_dtypes=[BF16])
    (dk_raw,), (dgk2,) = _rowwise_vjp(_rms_pair, [k_pair], [gk2], [pair(dkn)], "fox_knorm_bwd", heads=N_PAIRS,
                                      row_dtypes=[BF16])
    g["fox_q_norm"] = dgq2[:, :HEAD_DIM] + dgq2[:, HEAD_DIM:]
    g["fox_k_norm"] = dgk2[:, :HEAD_DIM] + dgk2[:, HEAD_DIM:]
    df_rows, dfb = _forget_bwd(f_rows, f_bias, (dc + dcq).reshape(bh, l))
    g["fox_f_bias"] = jnp.sum(dfb.reshape(seqs, N_FOX_HEADS), axis=0)
    df = df_rows.reshape(seqs, N_FOX_HEADS, l).transpose(0, 2, 1).reshape(t, N_FOX_HEADS)
    dqkv = jnp.concatenate([dq_raw, dk_raw, dv.astype(BF16)], axis=1)
    duf = jnp.concatenate([du_a + du_b, df, jnp.zeros((t, UF_COLS - S5_WIDTH - N_FOX_HEADS), F32)],
                          axis=1).astype(BF16)
    dhn1 = _mm(duf, w_uf, "nt", "in_uf_dx", res=_mm(dqkv, w_qkv, "nt", "in_qkv_dx"), out_dtype=BF16)
    dw_qkv = _mm(hn1, dqkv, "tn", "in_qkv_dw")
    dw_uf = _mm(hn1, duf, "tn", "in_uf_dw")
    g["w_in"] = jnp.concatenate([dw_qkv, dw_uf[:, S5_WIDTH:S5_WIDTH + N_FOX_HEADS], dw_uf[:, :S5_WIDTH]], axis=1)
    (dx,), (g["norm_mix"],) = _rowwise_vjp(_rms, [full(x)], [p["norm_mix"]], [full(dhn1)], "norm_mix_bwd",
                                           adds=[full(dh1)])
    return loss, dx.reshape(seqs, l, d), g


def _place():
    return lax.axis_index("x"), lax.axis_index("y"), lax.axis_index("c")


def _other_chips(x, y):
    return [(1 - x, y), (x, 1 - y), (1 - x, 1 - y)]


ANY = pl.BlockSpec(memory_space=pl.ANY)


HBM = pl.BlockSpec(memory_space=pltpu.HBM)
SEM = pl.BlockSpec(memory_space=pltpu.SEMAPHORE)
DATAFLOW = pltpu.SideEffectType.DATAFLOW_SIDE_EFFECTING


def _in_hbm(a):
    return pltpu.with_memory_space_constraint(a, pltpu.HBM)


def _split_start(name, srcs, lands, n_copies, plan):
    n = len(srcs)

    def body(*refs):
        src_refs, land_refs = refs[:n], refs[n:2 * n]
        send_sems, recv_sems = refs[2 * n], refs[2 * n + 1]
        for i, (src, dst, dev) in enumerate(plan(src_refs, land_refs)):
            pltpu.make_async_remote_copy(src_ref=src, dst_ref=dst, send_sem=send_sems.at[i], recv_sem=recv_sems.at[i],
                                         device_id=dev, device_id_type=MESH).start()
        refs[-1][...] = jnp.zeros((8, 128), F32)

    res = pl.pallas_call(
        body, name=name, in_specs=[HBM] * (2 * n),
        out_specs=[SEM, SEM] + [HBM] * (2 * n) + [pl.BlockSpec(memory_space=pltpu.VMEM)],
        out_shape=[pltpu.SemaphoreType.DMA((n_copies,)), pltpu.SemaphoreType.DMA((n_copies,))]
        + [pltpu.HBM(a.shape, a.dtype) for a in list(srcs) + list(lands)] + [jax.ShapeDtypeStruct((8, 128), F32)],
        input_output_aliases={i: 2 + i for i in range(2 * n)},
        compiler_params=pltpu.CompilerParams(has_side_effects=DATAFLOW),
    )(*[_in_hbm(a) for a in list(srcs) + list(lands)])
    return res[0], res[1], list(res[2:2 + n]), list(res[2 + n:2 + 2 * n]), res[-1]


def _split_wait(name, send_sems, recv_sems, srcs, lands, after, plan):
    n = len(srcs)

    def body(*refs):
        src_refs, land_refs = refs[:n], refs[n:2 * n]
        send_ref, recv_ref = refs[2 * n], refs[2 * n + 1]
        for i, (src, dst, dev) in enumerate(plan(src_refs, land_refs)):
            cp = pltpu.make_async_remote_copy(src_ref=src, dst_ref=dst, send_sem=send_ref.at[i], recv_sem=recv_ref.at[i],
                                              device_id=dev, device_id_type=MESH)
            cp.wait_send()
            cp.wait_recv()

    res = pl.pallas_call(
        body, name=name, in_specs=[HBM] * (2 * n) + [SEM, SEM, ANY], out_specs=[HBM] * (2 * n),
        out_shape=[pltpu.HBM(a.shape, a.dtype) for a in list(srcs) + list(lands)],
        input_output_aliases={i: i for i in range(2 * n)},
        compiler_params=pltpu.CompilerParams(has_side_effects=DATAFLOW),
    )(*srcs, *lands, send_sems, recv_sems, after)
    return list(res[:n]), list(res[n:])


def _first_gather_plan(src_refs, land_refs):
    x, y, c = _place()
    mine = 2 * x + y
    (src, small), (land, small_land) = src_refs, land_refs
    r = src.shape[0]
    hr = r // 2
    half = src.at[pl.ds(pl.multiple_of(c * hr, 16), hr), :]
    half_dst = land.at[pl.ds(pl.multiple_of(mine * r + c * hr, 16), hr), :]
    copies = [(src, land.at[pl.ds(pl.multiple_of(mine * r, 16), r), :], (x, y, 1 - c)),
              (small, small_land.at[mine], (x, y, 1 - c))]
    for px, py in _other_chips(x, y):
        copies += [(half, half_dst, (px, py, c)), (small, small_land.at[mine], (px, py, c))]
    return copies


def _forward_to_sibling(full):
    def body(full_in, full_ref, send_sems, recv_sems):
        x, y, c = _place()
        r = full_ref.shape[0] // 4
        hr = r // 2
        copies = []
        for j, (px, py) in enumerate(_other_chips(x, y)):
            got = full_ref.at[pl.ds(pl.multiple_of((2 * px + py) * r + c * hr, 16), hr), :]
            cp = pltpu.make_async_remote_copy(
                src_ref=got, dst_ref=got, send_sem=send_sems.at[j], recv_sem=recv_sems.at[j],
                device_id=(x, y, 1 - c), device_id_type=MESH)
            cp.start()
            copies.append(cp)
        for cp in copies:
            cp.wait()

    return pl.pallas_call(
        body, name="gather_first_forward", in_specs=[ANY], out_specs=ANY,
        out_shape=jax.ShapeDtypeStruct(full.shape, full.dtype), input_output_aliases={0: 0},
        scratch_shapes=[pltpu.SemaphoreType.DMA((3,)), pltpu.SemaphoreType.DMA((3,))],
        compiler_params=pltpu.CompilerParams(has_side_effects=True),
    )(full)


def _late_gather_plan(col_kind):
    def plan(src_refs, land_refs):
        x, y, c = _place()
        mine = 2 * x + y
        copies = []
        for a, (src, land) in enumerate(zip(src_refs, land_refs)):
            r, cs = src.shape
            if col_kind[a]:
                dst = land.at[:, pl.ds(pl.multiple_of(mine * cs, 128), cs)]
            else:
                dst = land.at[pl.ds(pl.multiple_of(mine * r, 16), r), :]
            copies.append((src, dst, (x, y, 1 - c)))
            copies += [(src, dst, (px, py, c)) for (px, py) in _other_chips(x, y)]
        return copies
    return plan


def _late_reduce_plan(col_kind):
    def plan(src_refs, land_refs):
        x, y, c = _place()
        copies = []
        for a, (src, land) in enumerate(zip(src_refs, land_refs)):
            for j, (px, py) in enumerate(_other_chips(x, y)):
                if col_kind[a] is None:
                    piece = src
                elif col_kind[a]:
                    cs = land.shape[2]
                    piece = src.at[:, pl.ds(pl.multiple_of((2 * px + py) * cs, 128), cs)]
                else:
                    piece = src.at[2 * px + py]
                copies.append((piece, land.at[j], (px, py, c)))
        return copies
    return plan


def _pair_swap(name, halves):
    n = len(halves)

    def body(*refs):
        ins, outs = refs[:n], refs[n:2 * n]
        send_sems, recv_sems = refs[2 * n:]
        x, y, c = _place()
        copies = []
        for a in range(n):
            cp = pltpu.make_async_remote_copy(
                src_ref=ins[a], dst_ref=outs[a], send_sem=send_sems.at[a], recv_sem=recv_sems.at[a],
                device_id=(x, y, 1 - c), device_id_type=MESH)
            cp.start()
            copies.append(cp)
        for cp in copies:
            cp.wait()

    return pl.pallas_call(
        body, name=name, in_specs=[ANY] * n, out_specs=[ANY] * n,
        out_shape=[jax.ShapeDtypeStruct(s.shape, s.dtype) for s in halves],
        scratch_shapes=[pltpu.SemaphoreType.DMA((n,)), pltpu.SemaphoreType.DMA((n,))],
        compiler_params=pltpu.CompilerParams(has_side_effects=True),
    )(*halves)


def _chip_sum(name, chip_sel, own, col, others):
    _, r, c = others.shape
    tr = _pick(r, (256, 128, 64, 32, 16))
    if col:
        own_spec = pl.BlockSpec((tr, c), lambda i, s: (i, s[0]))
    else:
        own_spec = pl.BlockSpec((None, tr, c), lambda i, s: (s[0], i, 0))
    specs = [own_spec] + [pl.BlockSpec((None, tr, c), lambda i, s, k=k: (k, i, 0)) for k in range(3)]

    def body(s_ref, own_ref, r0, r1, r2, o_ref):
        total = ((own_ref[...].astype(F32) + r0[...].astype(F32)) + r1[...].astype(F32)) + r2[...].astype(F32)
        o_ref[...] = total.astype(o_ref.dtype)

    return pl.pallas_call(
        body, name=name,
        grid_spec=pltpu.PrefetchScalarGridSpec(
            num_scalar_prefetch=1, grid=(r // tr,), in_specs=specs,
            out_specs=pl.BlockSpec((tr, c), lambda i, s: (i, 0))),
        out_shape=jax.ShapeDtypeStruct((r, c), BF16),
        compiler_params=_params(("parallel",)),
    )(chip_sel, own, others, others, others)


def _small_layout(vals):
    sizes = [int(math.prod(v.shape)) for v in vals]
    padded = [-(-s // 128) * 128 for s in sizes]
    return sizes, padded, -(-sum(padded) // 1024) * 1024


def _pack_small(vals):
    sizes, padded, total = _small_layout(vals)
    flat = [jnp.pad(v.reshape(-1), (0, p - s)) for v, s, p in zip(vals, sizes, padded)]
    flat.append(jnp.zeros((total - sum(padded),), F32))
    return jnp.concatenate(flat).reshape(total // 128, 128)


def _allreduce_small(own, others, vals):
    def body(own_ref, oth_ref, out_ref, land, send_sem, recv_sem):
        x, y, c = _place()
        out_ref[...] = (own_ref[...] + oth_ref[0]) + (oth_ref[1] + oth_ref[2])
        cp = pltpu.make_async_remote_copy(
            src_ref=out_ref, dst_ref=land, send_sem=send_sem.at[0], recv_sem=recv_sem.at[0],
            device_id=(x, y, 1 - c), device_id_type=MESH)
        cp.start()
        cp.wait()
        out_ref[...] = out_ref[...] + land[...]

    vm = pl.BlockSpec(memory_space=pltpu.VMEM)
    summed = pl.pallas_call(
        body, name="allreduce_small", in_specs=[vm, vm], out_specs=vm,
        out_shape=jax.ShapeDtypeStruct(own.shape, F32),
        scratch_shapes=[pltpu.VMEM(own.shape, F32), pltpu.SemaphoreType.DMA((1,)), pltpu.SemaphoreType.DMA((1,))],
        compiler_params=pltpu.CompilerParams(has_side_effects=True, vmem_limit_bytes=VMEM_LIMIT_BYTES),
    )(own, others).reshape(-1)
    sizes, padded, _ = _small_layout(vals)
    outs, off = [], 0
    for v, s, p in zip(vals, sizes, padded):
        outs.append(summed[off:off + s].reshape(v.shape))
        off += p
    return outs


def _adamw_math(w, g, m, v):
    m2 = ADAM_B1 * m + (1.0 - ADAM_B1) * g
    v2 = ADAM_B2 * v + (1.0 - ADAM_B2) * (g * g)
    m_hat = m2 / (1.0 - ADAM_B1 ** ADAM_STEP)
    v_hat = v2 / (1.0 - ADAM_B2 ** ADAM_STEP)
    delta = -ADAM_LR * (m_hat / (jnp.sqrt(v_hat) + ADAM_EPS) + ADAM_WD * w)
    return delta, m2, v2


def _adamw_big(name, w, g_mine, g_sibling, m, v):
    _, r, c = w.shape

    def body(w_ref, ga_ref, gb_ref, m_ref, v_ref, go_ref, d_ref, mo_ref, vo_ref):
        gv = ga_ref[...].astype(F32) + gb_ref[...].astype(F32)
        d, m2, v2 = _adamw_math(w_ref[...], gv, m_ref[...], v_ref[...])
        go_ref[...] = gv
        d_ref[...] = d
        mo_ref[...] = m2
        vo_ref[...] = v2

    tr = _pick(r, (256, 128, 64, 32, 16, 8))
    if r % tr == 0 and tr % 8 == 0:
        grid = (r // tr,)
        blk = pl.BlockSpec((None, tr, c), lambda i: (0, i, 0))
        part = pl.BlockSpec((tr, c), lambda i: (i, 0))
    else:
        grid = (c // 512,)
        blk = pl.BlockSpec((None, r, 512), lambda i: (0, 0, i))
        part = pl.BlockSpec((r, 512), lambda i: (0, i))
    return pl.pallas_call(
        body, name=name, grid=grid, in_specs=[blk, part, part, blk, blk], out_specs=[blk] * 4,
        out_shape=[jax.ShapeDtypeStruct((1, r, c), F32)] * 4, compiler_params=_params(("parallel",)),
    )(w, g_mine, g_sibling, m, v)


def _adamw_small(ws, gs, ms, vs):
    n = len(ws)

    def body(*refs):
        w_r, g_r, m_r, v_r = refs[:n], refs[n:2 * n], refs[2 * n:3 * n], refs[3 * n:4 * n]
        o = refs[4 * n:]
        for a in range(n):
            gv = g_r[a][...]
            d, m2, v2 = _adamw_math(w_r[a][...], gv, m_r[a][...], v_r[a][...])
            o[a][...] = gv
            o[n + a][...] = d
            o[2 * n + a][...] = m2
            o[3 * n + a][...] = v2

    res = pl.pallas_call(
        body, name="adamw_small", out_shape=[jax.ShapeDtypeStruct(w.shape, F32) for _ in range(4) for w in ws],
        compiler_params=_params(),
    )(*ws, *gs, *ms, *vs)
    return res[:n], res[n:2 * n], res[2 * n:3 * n], res[3 * n:]


def _full_from_gathered(name, gathered):
    if name == "w_in":
        rows = gathered.shape[0] // 4
        return gathered.reshape(4, rows, gathered.shape[1]).transpose(1, 0, 2).reshape(rows, 4 * gathered.shape[1])
    return gathered


def _reduce_layout(name, full):
    if name in COL_KIND:
        return full
    if name == "w_in":
        rows, cols = full.shape
        return full.reshape(rows, 4, cols // 4).transpose(1, 0, 2)
    return full.reshape(4, full.shape[0] // 4, full.shape[1])


def kernel(x, mem, norm_mix, w_in, fox_q_norm, fox_k_norm, fox_f_bias, s5_a_re, s5_a_im, s5_log_dt, s5_b_re, s5_b_im, s5_c_re, s5_c_im, s5_d, s5_w_glu, s5_b_glu, out_norm_fox, out_norm_s5, w_out, norm_cross, norm_mem, w_xq, w_xkv, xq_norm, xk_norm, w_xo, norm_ffn, w_ffn_up, ffn_conv_w, ffn_conv_b, w_ffn_down, loss_target, m_norm_mix, m_w_in, m_fox_q_norm, m_fox_k_norm, m_fox_f_bias, m_s5_a_re, m_s5_a_im, m_s5_log_dt, m_s5_b_re, m_s5_b_im, m_s5_c_re, m_s5_c_im, m_s5_d, m_s5_w_glu, m_s5_b_glu, m_out_norm_fox, m_out_norm_s5, m_w_out, m_norm_cross, m_norm_mem, m_w_xq, m_w_xkv, m_xq_norm, m_xk_norm, m_w_xo, m_norm_ffn, m_w_ffn_up, m_ffn_conv_w, m_ffn_conv_b, m_w_ffn_down, v_norm_mix, v_w_in, v_fox_q_norm, v_fox_k_norm, v_fox_f_bias, v_s5_a_re, v_s5_a_im, v_s5_log_dt, v_s5_b_re, v_s5_b_im, v_s5_c_re, v_s5_c_im, v_s5_d, v_s5_w_glu, v_s5_b_glu, v_out_norm_fox, v_out_norm_s5, v_w_out, v_norm_cross, v_norm_mem, v_w_xq, v_w_xkv, v_xq_norm, v_xk_norm, v_w_xo, v_norm_ffn, v_w_ffn_up, v_ffn_conv_w, v_ffn_conv_b, v_w_ffn_down):
    given = dict(locals())
    w = {n: given[n] for n in WEIGHTS}
    m = {n: given["m_" + n] for n in WEIGHTS}
    v = {n: given["v_" + n] for n in WEIGHTS}
    xi, yi, _ = _place()
    chip = (2 * xi + yi).astype(jnp.int32)
    chip_sel = chip.reshape(1)

    first_shard, taps = w[FIRST_WEIGHT][0].astype(BF16), w["ffn_conv_w"][0]
    send, recv, srcs, lands, g_started = _split_start(
        "gather_first_start", [first_shard, taps],
        [lax.empty((4 * first_shard.shape[0], first_shard.shape[1]), BF16), lax.empty((4,) + taps.shape, F32)],
        8, _first_gather_plan)
    pending = {"first": ((FIRST_WEIGHT, "ffn_conv_w"), _first_gather_plan, send, recv, srcs, lands)}
    for stage, names in (("mid", MID_WEIGHTS), ("late", LATE_WEIGHTS)):
        kinds = [n in COL_KIND for n in names]
        shards = [w[n][0].astype(BF16) for n in names]
        shards[0] = shards[0] + g_started[0:1, 0:1].astype(BF16)
        full = [lax.empty((s.shape[0], 4 * s.shape[1]) if ck else (4 * s.shape[0], s.shape[1]), BF16)
                for s, ck in zip(shards, kinds)]
        plan = _late_gather_plan(kinds)
        send, recv, srcs, lands, g_started = _split_start(
            "gather_" + stage + "_start", shards, full, 4 * len(names), plan)
        pending[stage] = (names, plan, send, recv, srcs, lands)

    def late_weights(stage, after):
        names, plan, send, recv, srcs, lands = pending[stage]
        _, full = _split_wait("gather_" + stage + "_wait", send, recv, srcs, lands, after, plan)
        if stage == "first":
            full = [_full_from_gathered(FIRST_WEIGHT, _forward_to_sibling(full[0])),
                    full[1].transpose(1, 0, 2).reshape(3, D_FF)]
        return dict(zip(names, full))

    reducing = {}

    def start_reduce(stage, grads_by_name, whole=()):
        names = list(grads_by_name)
        kinds = [n in COL_KIND for n in names]
        grads = [_reduce_layout(n, grads_by_name[n].astype(BF16)) for n in names]
        lands = [lax.empty((3, s.shape[0], s.shape[1] // 4) if ck else (3,) + s.shape[1:], BF16)
                 for s, ck in zip(grads, kinds)]
        plan = _late_reduce_plan(kinds + [None] * len(whole))
        send, recv, srcs, lands, started = _split_start(
            "reduce_" + stage + "_start", grads + list(whole),
            lands + [lax.empty((3,) + a.shape, a.dtype) for a in whole], 3 * (len(names) + len(whole)), plan)
        reducing[stage] = (names, kinds, plan, send, recv, srcs, lands)
        return started

    p = {n: w[n][0] for n in SMALL}
    for n in ("norm_mix", "fox_q_norm", "fox_k_norm", "fox_f_bias", "s5_b_glu", "out_norm_fox", "out_norm_s5",
              "norm_cross", "norm_mem", "xq_norm", "xk_norm", "norm_ffn", "ffn_conv_b"):
        p[n] = p[n].reshape(1, -1)
    p["norm_mix"] = p["norm_mix"] + g_started[0:1, 0:1]
    loss, grad_x, g = _local_step(x, mem, loss_target, p, {}, late_weights, start_reduce)

    small_names = list(SMALL) + ["ffn_conv_w"]
    small_vals = [g[n].reshape(w[n].shape if n != "ffn_conv_w" else (1, 3, D_FF)) for n in small_names] + [loss]
    after = start_reduce("first", {FIRST_WEIGHT: g[FIRST_WEIGHT]}, whole=[_pack_small(small_vals)])

    out_g, out_d, out_m, out_v = {}, {}, {}, {}

    def finish(stage, after):
        names, kinds, plan, send, recv, srcs, lands = reducing[stage]
        sums, from_chips = _split_wait("reduce_" + stage + "_wait", send, recv, srcs, lands, after, plan)
        mine = [_chip_sum("reduce_chip_sum_" + n, chip_sel, ps, ck, fc)
                for n, ps, fc, ck in zip(names, sums, from_chips, kinds)]
        theirs = _pair_swap("reduce_pair_swap_" + stage, mine)
        for n, a, b in zip(names, mine, theirs):
            if n == "w_in":
                flip = lambda t: jnp.swapaxes(t, -1, -2)
                res = _adamw_big("adamw_" + n, flip(w[n]), flip(a), flip(b), flip(m[n]), flip(v[n]))
                out_g[n], out_d[n], out_m[n], out_v[n] = (flip(t) for t in res)
                continue
            out_g[n], out_d[n], out_m[n], out_v[n] = _adamw_big("adamw_" + n, w[n], a, b, m[n], v[n])
        return sums[len(names):], from_chips[len(names):], out_v[names[-1]]

    _, _, after = finish("late", after)
    _, _, after = finish("mid", after)
    (small_own,), (small_others,), _ = finish("first", after)

    reduced = _allreduce_small(small_own, small_others, small_vals)
    loss_all = reduced[-1].reshape(())
    conv_w_grad = lax.dynamic_slice_in_dim(reduced[-2], chip * (D_FF // 4), D_FF // 4, axis=2)
    sg, sd, sm, sv = _adamw_small(
        [w[n] for n in small_names], list(reduced[:len(SMALL)]) + [conv_w_grad],
        [m[n] for n in small_names], [v[n] for n in small_names])
    out_g.update(zip(small_names, sg))
    out_d.update(zip(small_names, sd))
    out_m.update(zip(small_names, sm))
    out_v.update(zip(small_names, sv))

    return (loss_all, grad_x, *[out_g[n] for n in WEIGHTS], *[out_d[n] for n in WEIGHTS],
            *[out_m[n] for n in WEIGHTS], *[out_v[n] for n in WEIGHTS])
```

```python
import math

import jax
import jax.numpy as jnp
from jax import lax
from jax.experimental import pallas as pl
from jax.experimental.pallas import tpu as pltpu

F32 = jnp.float32
BF16 = jnp.bfloat16

D_MODEL = 1024
FOX_WIDTH = 512
HEAD_DIM = 64
N_FOX_HEADS = 8
S5_WIDTH = 512
S5_GROUP_CH = 16
S5_GROUPS = 32
S5_STATE = 64
S5_CH = S5_GROUPS * S5_STATE
N_X_HEADS = 4
X_HEAD_DIM = 256
N_MEM = 256
D_FF = 2816
UF_COLS = 640
EPS = 1e-6
ADAM_LR = 0.001
ADAM_B1 = 0.9
ADAM_B2 = 0.999
ADAM_EPS = 1e-08
ADAM_WD = 0.01
ADAM_STEP = 10

VMEM_LIMIT_BYTES = 56 * 1024 * 1024
MM_BLOCK_BYTES = 6 * 1024 * 1024
MM_VMEM_BYTES = 40 * 1024 * 1024
MM_TILE_MAX = 1536
MESH = pl.DeviceIdType.MESH

FIRST_WEIGHT = "w_in"
MID_WEIGHTS = ("s5_w_glu", "w_out")
EARLY_WEIGHTS = (FIRST_WEIGHT,) + MID_WEIGHTS
LATE_WEIGHTS = ("w_xq", "w_xkv", "w_xo", "w_ffn_up", "w_ffn_down")
BIG = EARLY_WEIGHTS + LATE_WEIGHTS
COL_KIND = ("w_xkv", "w_ffn_up")
SMALL = ("norm_mix", "fox_q_norm", "fox_k_norm", "fox_f_bias", "s5_a_re", "s5_a_im", "s5_log_dt",
         "s5_b_re", "s5_b_im", "s5_c_re", "s5_c_im", "s5_d", "s5_b_glu", "out_norm_fox", "out_norm_s5",
         "norm_cross", "norm_mem", "xq_norm", "xk_norm", "norm_ffn", "ffn_conv_b")
WEIGHTS = ("norm_mix", "w_in", "fox_q_norm", "fox_k_norm", "fox_f_bias", "s5_a_re", "s5_a_im", "s5_log_dt",
           "s5_b_re", "s5_b_im", "s5_c_re", "s5_c_im", "s5_d", "s5_w_glu", "s5_b_glu", "out_norm_fox",
           "out_norm_s5", "w_out", "norm_cross", "norm_mem", "w_xq", "w_xkv", "xq_norm", "xk_norm", "w_xo",
           "norm_ffn", "w_ffn_up", "ffn_conv_w", "ffn_conv_b", "w_ffn_down")


def _params(sem=None):
    return pltpu.CompilerParams(dimension_semantics=sem, vmem_limit_bytes=VMEM_LIMIT_BYTES)


def _pick(n, cands):
    for c in cands:
        if n % c == 0:
            return c
    return n


_DIMS = {"nn": (((1,), (0,)), ((), ())), "nt": (((1,), (1,)), ((), ())), "tn": (((0,), (0,)), ((), ()))}


def _mm(a, b, mode, name, out_dtype=F32, res=None):
    if mode == "nn":
        (m, k), (k2, n) = a.shape, b.shape
    elif mode == "nt":
        (m, k), (n, k2) = a.shape, b.shape
    else:
        (k, m), (k2, n) = a.shape, b.shape
    assert k == k2, (name, a.shape, b.shape)

    has_res = res is not None
    a_size, b_size = a.dtype.itemsize, b.dtype.itemsize
    o_size = jnp.dtype(out_dtype).itemsize + (res.dtype.itemsize if has_res else 0)

    def tiles(dim):
        return [c for c in range(MM_TILE_MAX, 0, -128) if dim % c == 0] or [dim]

    best = None
    for tm in tiles(m):
        for tn in tiles(n):
            a_blk, b_blk = tm * k * a_size, tn * k * b_size
            if max(a_blk, b_blk) > MM_BLOCK_BYTES or 2 * (a_blk + b_blk + tm * tn * o_size) > MM_VMEM_BYTES:
                continue
            for rows_outer in (True, False):
                moved = (m * k * a_size + (m // tm) * n * k * b_size) if rows_outer else \
                        (n * k * b_size + (n // tn) * m * k * a_size)
                key = (moved, -(tm * tn))
                if best is None or key < best[0]:
                    best = (key, tm, tn, rows_outer)
    assert best is not None, (name, a.shape, b.shape)
    _, tm, tn, rows_outer = best
    ij = (lambda g0, g1: (g0, g1)) if rows_outer else (lambda g0, g1: (g1, g0))
    if mode == "tn":
        a_spec = pl.BlockSpec((k, tm), lambda g0, g1: (0, ij(g0, g1)[0]))
    else:
        a_spec = pl.BlockSpec((tm, k), lambda g0, g1: (ij(g0, g1)[0], 0))
    if mode == "nt":
        b_spec = pl.BlockSpec((tn, k), lambda g0, g1: (ij(g0, g1)[1], 0))
    else:
        b_spec = pl.BlockSpec((k, tn), lambda g0, g1: (0, ij(g0, g1)[1]))
    o_spec = pl.BlockSpec((tm, tn), lambda g0, g1: ij(g0, g1))
    grid = (m // tm, n // tn) if rows_outer else (n // tn, m // tm)
    dims = _DIMS[mode]

    def body(*refs):
        a_ref, b_ref = refs[0], refs[1]
        o_ref = refs[-1]
        acc = lax.dot_general(a_ref[...].astype(BF16), b_ref[...].astype(BF16), dims, preferred_element_type=F32)
        if has_res:
            acc = acc + refs[2][...].astype(F32)
        o_ref[...] = acc.astype(o_ref.dtype)

    return pl.pallas_call(
        body, name=name, grid=grid,
        in_specs=[a_spec, b_spec] + ([o_spec] if has_res else []),
        out_specs=o_spec, out_shape=jax.ShapeDtypeStruct((m, n), out_dtype),
        compiler_params=_params(("parallel", "parallel")),
    )(*((a, b, res) if has_res else (a, b)))


def _row_spec(tm, bc, off, step):
    return pl.BlockSpec((tm, bc), lambda i, h: (i, off + step * h))


ROW_TILE_ELEMS = 512 * 1024


def _row_tile(t, rows):
    widest = max(bc for (_, bc, _, _) in rows)
    return _pick(t, (min(t, ROW_TILE_ELEMS // widest), 512, 256, 128, 64, 8))


def _rowwise(fn, rows, pars, outs, name, heads=1):
    t = rows[0][0].shape[0]
    tm = _row_tile(t, rows)
    nr, npar = len(rows), len(pars)

    def body(*refs):
        vals = [r[...].astype(F32) for r in refs[:nr + npar]]
        res = fn(*vals)
        if not isinstance(res, (tuple, list)):
            res = (res,)
        for o_ref, v in zip(refs[nr + npar:], res):
            o_ref[...] = v.astype(o_ref.dtype)

    in_specs = [_row_spec(tm, bc, off, st) for (_, bc, off, st) in rows]
    in_specs += [pl.BlockSpec(p.shape, lambda i, h: (0, 0)) for p in pars]
    out_specs = [_row_spec(tm, bc, 0, st) for (_, bc, st, _) in outs]
    out_shape = [jax.ShapeDtypeStruct((t, c), dt) for (c, _, _, dt) in outs]
    res = pl.pallas_call(
        body, name=name, grid=(t // tm, heads), in_specs=in_specs, out_specs=out_specs, out_shape=out_shape,
        compiler_params=_params(("parallel", "parallel")),
    )(*[r[0] for r in rows], *pars)
    return res[0] if len(res) == 1 else res


def _rowwise_vjp(fn, rows, pars, cts, name, heads=1, adds=None, row_dtypes=None):
    t = rows[0][0].shape[0]
    tm = _row_tile(t, rows)
    nr, npar, nct = len(rows), len(pars), len(cts)
    adds = adds or [None] * nr
    add_list = [a for a in adds if a is not None]
    row_dtypes = row_dtypes or [F32] * nr

    def body(*refs):
        i, h = pl.program_id(0), pl.program_id(1)
        p = 0
        row_v = [r[...].astype(F32) for r in refs[p:p + nr]]; p += nr
        par_v = [r[...].astype(F32) for r in refs[p:p + npar]]; p += npar
        ct_v = [r[...].astype(F32) for r in refs[p:p + nct]]; p += nct
        add_refs = refs[p:p + len(add_list)]; p += len(add_list)
        drow_refs = refs[p:p + nr]; p += nr
        dpar_refs = refs[p:p + npar]

        def wrapped(*a):
            r = fn(*a)
            return tuple(r) if isinstance(r, (tuple, list)) else (r,)

        _, pull = jax.vjp(wrapped, *row_v, *par_v)
        grads = pull(tuple(ct_v))
        ai = 0
        for k in range(nr):
            g = grads[k]
            if adds[k] is not None:
                g = g + add_refs[ai][...].astype(F32)
                ai += 1
            drow_refs[k][...] = g.astype(drow_refs[k].dtype)

        @pl.when((i == 0) & (h == 0))
        def _():
            for r in dpar_refs:
                r[...] = jnp.zeros(r.shape, r.dtype)

        for k in range(npar):
            dpar_refs[k][...] += grads[nr + k]

    in_specs = [_row_spec(tm, bc, off, st) for (_, bc, off, st) in rows]
    in_specs += [pl.BlockSpec(q.shape, lambda i, h: (0, 0)) for q in pars]
    in_specs += [_row_spec(tm, bc, off, st) for (_, bc, off, st) in cts]
    in_specs += [_row_spec(tm, bc, off, st) for (_, bc, off, st) in add_list]
    out_specs = [_row_spec(tm, bc, 0, st) for (_, bc, _, st) in rows]
    out_specs += [pl.BlockSpec(q.shape, lambda i, h: (0, 0)) for q in pars]
    out_shape = [jax.ShapeDtypeStruct((t, bc * (heads if st else 1)), dt) for (_, bc, _, st), dt in zip(rows, row_dtypes)]
    out_shape += [jax.ShapeDtypeStruct(q.shape, F32) for q in pars]
    res = pl.pallas_call(
        body, name=name, grid=(t // tm, heads), in_specs=in_specs, out_specs=out_specs, out_shape=out_shape,
        compiler_params=_params(("arbitrary", "arbitrary")),
    )(*[r[0] for r in rows], *pars, *[c[0] for c in cts], *[a[0] for a in add_list])
    return list(res[:nr]), list(res[nr:])


def _rms(x, g):
    return x * lax.rsqrt(jnp.mean(x * x, axis=-1, keepdims=True) + EPS) * g


def _rms_pair(x, g):
    left = lax.broadcasted_iota(jnp.int32, x.shape, 1) < HEAD_DIM
    x2 = x * x
    ms_a = jnp.sum(jnp.where(left, x2, 0.0), axis=-1, keepdims=True) * (1.0 / HEAD_DIM)
    ms_b = jnp.sum(jnp.where(left, 0.0, x2), axis=-1, keepdims=True) * (1.0 / HEAD_DIM)
    return x * lax.rsqrt(jnp.where(left, ms_a, ms_b) + EPS) * g


def _gelu(x):
    return 0.5 * x * (1.0 + jnp.tanh(math.sqrt(2.0 / math.pi) * (x + 0.044715 * (x * x * x))))


def _s5_act(ys, u, d):
    return _gelu(ys + d * u)


def _s5_gate(yg, z, b, g):
    return _rms(yg * jax.nn.sigmoid(z + b), g)


def _lane_cumsum(x, reverse):
    n = x.shape[-1]
    lane = lax.broadcasted_iota(jnp.int32, x.shape, 1)
    k = 1
    while k < n:
        if reverse:
            x = x + jnp.where(lane < n - k, pltpu.roll(x, n - k, 1), 0.0)
        else:
            x = x + jnp.where(lane >= k, pltpu.roll(x, k, 1), 0.0)
        k *= 2
    return x


def _log_sigmoid(z):
    return jnp.minimum(z, 0.0) - jnp.log(1.0 + jnp.exp(-jnp.abs(z)))


def _forget_fwd(f, bias):
    def body(f_ref, b_ref, c_ref):
        c_ref[...] = _lane_cumsum(_log_sigmoid(f_ref[...] + b_ref[...]), False)

    return pl.pallas_call(body, name="forget_fwd", out_shape=jax.ShapeDtypeStruct(f.shape, F32),
                          compiler_params=_params())(f, bias)


def _forget_bwd(f, bias, dc):
    def body(f_ref, b_ref, dc_ref, df_ref, db_ref):
        dlog = _lane_cumsum(dc_ref[...], True)
        df = dlog * jax.nn.sigmoid(-(f_ref[...] + b_ref[...]))
        df_ref[...] = df
        db_ref[...] = jnp.sum(df, axis=1, keepdims=True)

    return pl.pallas_call(body, name="forget_bwd",
                          out_shape=(jax.ShapeDtypeStruct(f.shape, F32), jax.ShapeDtypeStruct(bias.shape, F32)),
                          compiler_params=_params())(f, bias, dc)


FOX_BLOCK = 1024
FOX_KEYS = 1024
FOX_BWD_BLOCK = 512
_NT = _DIMS["nt"]
_TN = _DIMS["tn"]


N_PAIRS = N_FOX_HEADS // 2
V_BLOCK0 = 2 * N_PAIRS


def _left_lanes(shape):
    return lax.broadcasted_iota(jnp.int32, shape, 1) < HEAD_DIM


def _top_rows(shape):
    return lax.broadcasted_iota(jnp.int32, shape, 0) < HEAD_DIM


def _wide(c_tile, n):
    return c_tile if n == 128 else jnp.concatenate([c_tile] * (n // 128), axis=1)


def _fox_fwd(qn, kn, qkv, c_wide, seqs):
    t = qn.shape[0]
    l = t // seqs
    tb = min(FOX_BLOCK, l)
    tk = min(FOX_KEYS, tb)
    ratio = tb // tk
    nb = l // tb
    scale = HEAD_DIM ** -0.5

    def body(q_ref, k_ref, v_ref, ca_ref, cb_ref, o_ref, lse_ref, vt_ref):
        i = pl.program_id(2)
        top = _top_rows((128, tb))

        @pl.when(i == 0)
        def _():
            vt_ref[...] = v_ref[...].T.astype(BF16)

        qt = (q_ref[...].astype(F32) * scale).T.astype(BF16)
        zero = jnp.zeros_like(qt)
        qts = (jnp.where(top, qt, zero), jnp.where(top, zero, qt))
        top_k = _top_rows((128, tk))
        zero_k = jnp.zeros((128, tk), BF16)
        key_pos = lax.broadcasted_iota(jnp.int32, (tk, tb), 0)
        query_pos = lax.broadcasted_iota(jnp.int32, (tk, tb), 1)
        c_refs = (ca_ref, cb_ref)

        def scores(j):
            off = pl.multiple_of(j * tk, tk)
            k2 = k_ref[pl.ds(off, tk), :]
            return tuple(jnp.dot(k2, qts[h], preferred_element_type=F32) - _wide(c_refs[h][pl.ds(off, tk), :], tb)
                         for h in (0, 1))

        def values_times(ps, j):
            vt = vt_ref[:, pl.ds(pl.multiple_of(j * tk, tk), tk)]
            return (jnp.dot(jnp.where(top_k, vt, zero_k), ps[0], preferred_element_type=F32)
                    + jnp.dot(jnp.where(top_k, zero_k, vt), ps[1], preferred_element_type=F32))

        def softmax_step(sts, stats, first_key):
            ps, new, alphas = [], [], []
            for st, (m, s_sum) in zip(sts, stats):
                if first_key is not None:
                    st = jnp.where(key_pos + first_key <= query_pos, st, -jnp.inf)
                m_new = jnp.maximum(m, jnp.max(st, axis=0, keepdims=True))
                alpha = jnp.exp(m - m_new)
                p = jnp.exp(st - m_new)
                new.append((m_new, alpha * s_sum + jnp.sum(p, axis=0, keepdims=True)))
                alphas.append(alpha)
                ps.append(p.astype(BF16))
            return tuple(ps), tuple(new), jnp.where(top, alphas[0], alphas[1])

        def tile(j, carry, first_key):
            stats, acc = carry
            ps, stats, alpha = softmax_step(scores(j), stats, first_key)
            return stats, alpha * acc + values_times(ps, j)

        stat = (jnp.full((1, tb), -jnp.inf, F32), jnp.zeros((1, tb), F32))
        below = i * ratio
        carry = lax.fori_loop(0, below, lambda j, c: tile(j, c, None), ((stat, stat), jnp.zeros((128, tb), F32)))
        for r in range(ratio):
            carry = tile(below + r, carry, r * tk)
        ((ma, sa), (mb, sb)), acc = carry
        o_ref[...] = (acc / jnp.where(top, sa, sb)).T
        lse_ref[0:1, :] = ma + jnp.log(sa)
        lse_ref[1:2, :] = mb + jnp.log(sb)

    qblk = pl.BlockSpec((tb, 128), lambda b, hp, i: (b * nb + i, hp))
    return pl.pallas_call(
        body, name="fox_fwd", grid=(seqs, N_PAIRS, nb),
        in_specs=[qblk, pl.BlockSpec((l, 128), lambda b, hp, i: (b, hp)),
                  pl.BlockSpec((l, 128), lambda b, hp, i: (b, V_BLOCK0 + hp)),
                  pl.BlockSpec((None, l, 128), lambda b, hp, i: (b * N_FOX_HEADS + 2 * hp, 0, 0)),
                  pl.BlockSpec((None, l, 128), lambda b, hp, i: (b * N_FOX_HEADS + 2 * hp + 1, 0, 0))],
        out_specs=[qblk, pl.BlockSpec((None, 2, tb), lambda b, hp, i: (b * N_PAIRS + hp, 0, i))],
        out_shape=[jax.ShapeDtypeStruct((t, FOX_WIDTH), F32), jax.ShapeDtypeStruct((seqs * N_PAIRS, 2, l), F32)],
        scratch_shapes=[pltpu.VMEM((128, l), BF16)],
        compiler_params=_params(("parallel", "parallel", "arbitrary")),
    )(qn, kn, qkv, c_wide, c_wide)


def _fox_bwd(qn, kn, qkv, c_wide, o, do, lse, seqs):
    t = qn.shape[0]
    l = t // seqs
    tb = min(FOX_BWD_BLOCK, l)
    nb = l // tb
    scale = HEAD_DIM ** -0.5
    one_at = (HEAD_DIM, 0)

    def body(q_ref, k_ref, v_ref, ca_ref, cb_ref, o_ref, do_ref, lse_ref, dq_ref, dk_ref, dv_ref, dc_ref, dcq_ref,
             qt_ref, kt_ref, dot_ref, delta_ref, dqa_ref, dqb_ref):
        top_l = _top_rows((128, l))
        top = _top_rows((128, tb))
        left = _left_lanes((tb, 128))
        row_id = lax.broadcasted_iota(jnp.int32, (128, tb), 0)
        lane_id = lax.broadcasted_iota(jnp.int32, (tb, 128), 1)
        zero_t = jnp.zeros((128, tb), BF16)
        zero_l = jnp.zeros((tb, 128), BF16)
        rows = lambda a: (jnp.where(top, a, zero_t), jnp.where(top, zero_t, a))
        lanes = lambda a: (jnp.where(left, a, zero_l), jnp.where(left, zero_l, a))
        with_one_row = lambda pair: tuple(jnp.where(row_id == one_at[h], 1.0, pair[h]).astype(BF16) for h in (0, 1))
        with_one_lane = lambda pair: tuple(jnp.where(lane_id == one_at[h], 1.0, pair[h]).astype(BF16) for h in (0, 1))
        causal = lax.broadcasted_iota(jnp.int32, (tb, tb), 0) <= lax.broadcasted_iota(jnp.int32, (tb, tb), 1)
        c_refs = (ca_ref, cb_ref)
        dq_refs = (dqa_ref, dqb_ref)

        qt_ref[...] = (q_ref[...].astype(F32) * scale).T.astype(BF16)
        kt_ref[...] = k_ref[...].astype(F32).T.astype(BF16)
        do_t = do_ref[...].T
        dot_ref[...] = do_t.astype(BF16)
        prod_t = do_t * o_ref[...].T
        delta_ref[0:1, :] = jnp.sum(jnp.where(top_l, prod_t, 0.0), axis=0, keepdims=True)
        delta_ref[1:2, :] = jnp.sum(jnp.where(top_l, 0.0, prod_t), axis=0, keepdims=True)
        dqa_ref[...] = jnp.zeros(dqa_ref.shape, F32)
        dqb_ref[...] = jnp.zeros(dqb_ref.shape, F32)

        def kv_block(j, _):
            koff = pl.multiple_of(j * tb, tb)
            k2 = k_ref[pl.ds(koff, tb), :]
            v2 = v_ref[pl.ds(koff, tb), :].astype(BF16)
            kts = with_one_row(rows(kt_ref[:, pl.ds(koff, tb)]))
            cw = tuple(_wide(c_refs[h][pl.ds(koff, tb), :], tb) for h in (0, 1))

            def q_block(i, carry, masked):
                dks, dv = list(carry[:2]), carry[2]
                qoff = pl.multiple_of(i * tb, tb)
                qs = lanes((q_ref[pl.ds(qoff, tb), :].astype(F32) * scale).astype(BF16))
                qs_one = with_one_lane(qs)
                dos = lanes(do_ref[pl.ds(qoff, tb), :].astype(BF16))
                qts = rows(qt_ref[:, pl.ds(qoff, tb)])
                dots = rows(dot_ref[:, pl.ds(qoff, tb)])
                for h in (0, 1):
                    st = jnp.dot(k2, qts[h], preferred_element_type=F32) - cw[h]
                    p = jnp.exp(st - lse_ref[h:h + 1, pl.ds(qoff, tb)])
                    if masked:
                        p = jnp.where(causal, p, 0.0)
                    dp = jnp.dot(v2, dots[h], preferred_element_type=F32)
                    dsb = (p * (dp - delta_ref[h:h + 1, pl.ds(qoff, tb)])).astype(BF16)
                    dv = dv + jnp.dot(p.astype(BF16), dos[h], preferred_element_type=F32)
                    dks[h] = dks[h] + jnp.dot(dsb, qs_one[h], preferred_element_type=F32)
                    dq_refs[h][:, pl.ds(qoff, tb)] += jnp.dot(kts[h], dsb, preferred_element_type=F32)
                return dks[0], dks[1], dv

            z = jnp.zeros((tb, 128), F32)
            carry = q_block(j, (z, z, z), True)
            rest = nb - 1 - j
            carry = lax.fori_loop(
                0, rest // 2, lambda n, c: q_block(j + 2 + 2 * n, q_block(j + 1 + 2 * n, c, False), False), carry)
            dka, dkb, dv = lax.cond(rest % 2 == 1, lambda c: q_block(nb - 1, c, False), lambda c: c, carry)
            dk_ref[pl.ds(koff, tb), :] = jnp.where(left, dka, dkb)
            dv_ref[pl.ds(koff, tb), :] = dv
            dc_ref[0:1, pl.ds(koff, tb)] = -dka.T[one_at[0]:one_at[0] + 1, :]
            dc_ref[1:2, pl.ds(koff, tb)] = -dkb.T[one_at[1]:one_at[1] + 1, :]
            return 0

        lax.fori_loop(0, nb, kv_block, 0)
        dq_ref[...] = (jnp.where(top_l, dqa_ref[...], dqb_ref[...]) * scale).T
        dcq_ref[0:1, :] = dqa_ref[one_at[0]:one_at[0] + 1, :]
        dcq_ref[1:2, :] = dqb_ref[one_at[1]:one_at[1] + 1, :]

    blk = pl.BlockSpec((l, 128), lambda b, hp: (b, hp))
    cspec = lambda k: pl.BlockSpec((None, l, 128), lambda b, hp: (b * N_FOX_HEADS + 2 * hp + k, 0, 0))
    rows2 = pl.BlockSpec((None, 2, l), lambda b, hp: (b * N_PAIRS + hp, 0, 0))
    wide = jax.ShapeDtypeStruct((t, FOX_WIDTH), F32)
    pair_rows = jax.ShapeDtypeStruct((seqs * N_PAIRS, 2, l), F32)
    return pl.pallas_call(
        body, name="fox_bwd", grid=(seqs, N_PAIRS),
        in_specs=[blk, blk, pl.BlockSpec((l, 128), lambda b, hp: (b, V_BLOCK0 + hp)), cspec(0), cspec(1), blk, blk, rows2],
        out_specs=[blk, blk, blk, rows2, rows2],
        out_shape=[wide, wide, wide, pair_rows, pair_rows],
        scratch_shapes=[pltpu.VMEM((128, l), BF16), pltpu.VMEM((128, l), BF16), pltpu.VMEM((128, l), BF16),
                        pltpu.VMEM((2, l), F32), pltpu.VMEM((128, l), F32), pltpu.VMEM((128, l), F32)],
        compiler_params=_params(("parallel", "parallel")),
    )(qn, kn, qkv, c_wide, c_wide, o, do, lse)


SCAN_ROWS = 1024
SCAN_COLS = 1024


S5_IN = 128
S5_ST = 512
SCAN_CHUNKS = SCAN_COLS // S5_ST
SCAN_SEGS = 8
LANES = 128


def _cmul(ar, ai, br, bi):
    return ar * br - ai * bi, ar * bi + ai * br


def _powers_into(pw_r, pw_i, a_r, a_i, seg):
    pw_r[0:1, :] = a_r
    pw_i[0:1, :] = a_i
    for k in range(1, seg):
        pr, pi = _cmul(pw_r[k - 1:k, :], pw_i[k - 1:k, :], a_r, a_i)
        pw_r[k:k + 1, :] = pr
        pw_i[k:k + 1, :] = pi


def _interleave(dst, src, seg):
    for h in range(src.shape[0]):
        for j in range(seg):
            dst[h, j * SCAN_SEGS:(j + 1) * SCAN_SEGS, :] = src[h, pl.ds(j, SCAN_SEGS, stride=seg), :]


def _deinterleave(dst, src, seg):
    for h in range(src.shape[0]):
        for j in range(seg):
            dst[h, pl.ds(j, SCAN_SEGS, stride=seg), :] = src[h, j * SCAN_SEGS:(j + 1) * SCAN_SEGS, :]


def _interleaved(ref, tmp_a, tmp_b, seg):
    n = ref.shape[1] // LANES
    for h in range(n):
        tmp_a[h] = ref[:, h * LANES:(h + 1) * LANES].astype(F32)
    _interleave(tmp_b, tmp_a, seg)
    return jnp.concatenate([tmp_b[h] for h in range(n)], axis=1)


def _store_deinterleaved(ref, val, tmp_a, tmp_b, seg):
    n = ref.shape[1] // LANES
    for h in range(n):
        tmp_a[h] = val[:, h * LANES:(h + 1) * LANES]
    _deinterleave(tmp_b, tmp_a, seg)
    for h in range(n):
        ref[:, h * LANES:(h + 1) * LANES] = tmp_b[h]


def _segment_scan(b_r, b_i, x_r, x_i, pw_r, pw_i, car_r, car_i, seg, sign, reverse, visit=None):
    nc = b_r.shape[0]
    sub = lax.broadcasted_iota(jnp.int32, (SCAN_SEGS, LANES), 0)
    lanes = lambda c: slice(c * LANES, (c + 1) * LANES)
    rows = lambda j: pl.ds(pl.multiple_of(((seg - 1 - j) if reverse else j) * SCAN_SEGS, SCAN_SEGS), SCAN_SEGS)
    a1 = [(pw_r[0:1, lanes(c)], sign * pw_i[0:1, lanes(c)]) for c in range(nc)]

    def local(j, xs):
        out = []
        for c in range(nc):
            xr, xi = xs[2 * c], xs[2 * c + 1]
            nr = a1[c][0] * xr - a1[c][1] * xi + b_r[c, rows(j), :]
            ni = a1[c][0] * xi + a1[c][1] * xr + b_i[c, rows(j), :]
            x_r[c, rows(j), :] = nr
            x_i[c, rows(j), :] = ni
            out += [nr, ni]
        return tuple(out)

    zero = jnp.zeros((SCAN_SEGS, LANES), F32)
    ends = lax.fori_loop(0, seg, local, (zero,) * (2 * nc))

    if reverse:
        first = sub == SCAN_SEGS - 1
        neighbour = lambda v: pltpu.roll(v, SCAN_SEGS - 1, 0)
        shift = lambda v, d: jnp.where(sub < SCAN_SEGS - d, pltpu.roll(v, SCAN_SEGS - d, 0), 0.0)
    else:
        first = sub == 0
        neighbour = lambda v: pltpu.roll(v, 1, 0)
        shift = lambda v, d: jnp.where(sub >= d, pltpu.roll(v, d, 0), 0.0)
    last = 0 if reverse else SCAN_SEGS - 1
    entries = []
    for c in range(nc):
        er, ei = ends[2 * c], ends[2 * c + 1]
        pr, pi = pw_r[seg - 1:seg, lanes(c)], sign * pw_i[seg - 1:seg, lanes(c)]
        yr = jnp.where(first, car_r[:, lanes(c)], neighbour(er))
        yi = jnp.where(first, car_i[:, lanes(c)], neighbour(ei))
        qr, qi = pr, pi
        for d in (1, 2, 4):
            mr, mi = _cmul(qr, qi, shift(yr, d), shift(yi, d))
            yr, yi = yr + mr, yi + mi
            qr, qi = _cmul(qr, qi, qr, qi)
        lr, li = _cmul(pr, pi, yr, yi)
        car_r[:, lanes(c)] = (er + lr)[last:last + 1, :]
        car_i[:, lanes(c)] = (ei + li)[last:last + 1, :]
        entries += [yr, yi]

    def correct(j, prev):
        out = []
        row_r, row_i = pw_r[pl.ds(j, 1), :], sign * pw_i[pl.ds(j, 1), :]
        for c in range(nc):
            mr, mi = _cmul(row_r[:, lanes(c)], row_i[:, lanes(c)], entries[2 * c], entries[2 * c + 1])
            nr = x_r[c, rows(j), :] + mr
            ni = x_i[c, rows(j), :] + mi
            x_r[c, rows(j), :] = nr
            x_i[c, rows(j), :] = ni
            if visit is not None:
                visit(c, rows(j), prev[2 * c], prev[2 * c + 1])
            out += [nr, ni]
        return tuple(out)

    lax.fori_loop(0, seg, correct, tuple(entries))


def _s5_fwd(uf, bbr, bbi, cr, ci, ar, ai, seqs):
    t = uf.shape[0]
    l = t // seqs
    tl = min(SCAN_ROWS, l)
    nl = l // tl
    seg = tl // SCAN_SEGS
    cb, nq = SCAN_COLS, SCAN_CHUNKS
    nc = cb // LANES
    per = S5_ST // LANES

    def body(u_ref, bbr_ref, bbi_ref, cr_ref, ci_ref, ar_ref, ai_ref, x_r, x_i, ys_ref,
             car_r, car_i, pw_r, pw_i, b_r, b_i, tmp_a, tmp_b):
        @pl.when(pl.program_id(2) == 0)
        def _():
            car_r[...] = jnp.zeros(car_r.shape, F32)
            car_i[...] = jnp.zeros(car_i.shape, F32)
            _powers_into(pw_r, pw_i, ar_ref[...], ai_ref[...], seg)

        u = _interleaved(u_ref, tmp_a, tmp_b, seg).astype(BF16)
        for q in range(nq):
            uq = u[:, q * S5_IN:(q + 1) * S5_IN]
            br = jnp.dot(uq, bbr_ref[q], preferred_element_type=F32)
            bi = jnp.dot(uq, bbi_ref[q], preferred_element_type=F32)
            for s in range(per):
                b_r[q * per + s] = br[:, s * LANES:(s + 1) * LANES]
                b_i[q * per + s] = bi[:, s * LANES:(s + 1) * LANES]
        _segment_scan(b_r, b_i, x_r, x_i, pw_r, pw_i, car_r, car_i, seg, 1.0, False)
        wide = lambda buf, q: jnp.concatenate([buf[q * per + s] for s in range(per)], axis=1).astype(BF16)
        ys = [jnp.dot(wide(x_r, q), cr_ref[q], preferred_element_type=F32)
              + jnp.dot(wide(x_i, q), ci_ref[q], preferred_element_type=F32) for q in range(nq)]
        _store_deinterleaved(ys_ref, jnp.concatenate(ys, axis=1), tmp_a, tmp_b, seg)

    rows = lambda w: pl.BlockSpec((tl, w), lambda s, j, r: (s * nl + r, j))
    state = pl.BlockSpec((nc, tl, LANES), lambda s, j, r: (j, s * nl + r, 0))
    chunk = lambda a: pl.BlockSpec((nq,) + a.shape[1:], lambda s, j, r: (j, 0, 0))
    par = pl.BlockSpec((1, cb), lambda s, j, r: (0, j))
    return pl.pallas_call(
        body, name="s5_fwd", grid=(seqs, S5_CH // cb, nl),
        in_specs=[rows(nq * S5_IN), chunk(bbr), chunk(bbi), chunk(cr), chunk(ci), par, par],
        out_specs=[state, state, rows(nq * S5_IN)],
        out_shape=[jax.ShapeDtypeStruct((S5_CH // LANES, t, LANES), F32)] * 2
        + [jax.ShapeDtypeStruct((t, S5_WIDTH), F32)],
        scratch_shapes=[pltpu.VMEM((1, cb), F32), pltpu.VMEM((1, cb), F32), pltpu.VMEM((seg, cb), F32),
                        pltpu.VMEM((seg, cb), F32)] + [pltpu.VMEM((nc, tl, LANES), F32)] * 2
        + [pltpu.VMEM((nq * S5_IN // LANES, tl, LANES), F32)] * 2,
        compiler_params=_params(("parallel", "parallel", "arbitrary")),
    )(uf, bbr, bbi, cr, ci, ar, ai)


def _s5_bwd(dys, uf, xr, xi, bbr, bbi, cr, ci, ar, ai, seqs):
    t = dys.shape[0]
    l = t // seqs
    tl = min(SCAN_ROWS, l)
    nl = l // tl
    seg = tl // SCAN_SEGS
    cb, nq = SCAN_COLS, SCAN_CHUNKS
    nc = cb // LANES
    per = S5_ST // LANES

    def body(dy_ref, u_ref, x_r, x_i, bbr_ref, bbi_ref, cr_ref, ci_ref, ar_ref, ai_ref,
             du_ref, dbbr_ref, dbbi_ref, dcr_ref, dci_ref, dar_ref, dai_ref,
             car_r, car_i, pw_r, pw_i, g_r, g_i, lam_r, lam_i, acc_r, acc_i, tmp_a, tmp_b):
        @pl.when(pl.program_id(2) == 0)
        def _():
            car_r[...] = jnp.zeros(car_r.shape, F32)
            car_i[...] = jnp.zeros(car_i.shape, F32)
            _powers_into(pw_r, pw_i, ar_ref[...], ai_ref[...], seg)
            for acc_ref in (dbbr_ref, dbbi_ref, dcr_ref, dci_ref, dar_ref, dai_ref):
                acc_ref[...] = jnp.zeros(acc_ref.shape, F32)

        dy = _interleaved(dy_ref, tmp_a, tmp_b, seg).astype(BF16)
        for q in range(nq):
            dyq = dy[:, q * S5_IN:(q + 1) * S5_IN]
            gr = lax.dot_general(dyq, cr_ref[q], _NT, preferred_element_type=F32)
            gi = lax.dot_general(dyq, ci_ref[q], _NT, preferred_element_type=F32)
            for s in range(per):
                g_r[q * per + s] = gr[:, s * LANES:(s + 1) * LANES]
                g_i[q * per + s] = gi[:, s * LANES:(s + 1) * LANES]
        acc_r[...] = jnp.zeros(acc_r.shape, F32)
        acc_i[...] = jnp.zeros(acc_i.shape, F32)

        def visit(c, rws, lr, li):
            xr_t, xi_t = x_r[c, rws, :], x_i[c, rws, :]
            acc_r[c] += lr * xr_t + li * xi_t
            acc_i[c] += li * xr_t - lr * xi_t

        _segment_scan(g_r, g_i, lam_r, lam_i, pw_r, pw_i, car_r, car_i, seg, -1.0, True, visit)
        for c in range(nc):
            dar_ref[:, c * LANES:(c + 1) * LANES] += jnp.sum(acc_r[c], axis=0, keepdims=True)
            dai_ref[:, c * LANES:(c + 1) * LANES] += jnp.sum(acc_i[c], axis=0, keepdims=True)
        u = _interleaved(u_ref, tmp_a, tmp_b, seg).astype(BF16)
        wide = lambda buf, q: jnp.concatenate([buf[q * per + s] for s in range(per)], axis=1).astype(BF16)
        du = []
        for q in range(nq):
            io = slice(q * S5_IN, (q + 1) * S5_IN)
            lq_r, lq_i = wide(lam_r, q), wide(lam_i, q)
            du.append(lax.dot_general(lq_r, bbr_ref[q], _NT, preferred_element_type=F32)
                      + lax.dot_general(lq_i, bbi_ref[q], _NT, preferred_element_type=F32))
            dbbr_ref[q] += lax.dot_general(u[:, io], lq_r, _TN, preferred_element_type=F32)
            dbbi_ref[q] += lax.dot_general(u[:, io], lq_i, _TN, preferred_element_type=F32)
            dcr_ref[q] += lax.dot_general(wide(x_r, q), dy[:, io], _TN, preferred_element_type=F32)
            dci_ref[q] += lax.dot_general(wide(x_i, q), dy[:, io], _TN, preferred_element_type=F32)
        _store_deinterleaved(du_ref, jnp.concatenate(du, axis=1), tmp_a, tmp_b, seg)

    rows = lambda w: pl.BlockSpec((tl, w), lambda s, j, r: (s * nl + nl - 1 - r, j))
    state = pl.BlockSpec((nc, tl, LANES), lambda s, j, r: (j, s * nl + nl - 1 - r, 0))
    chunk = lambda a: pl.BlockSpec((nq,) + a.shape[1:], lambda s, j, r: (j, 0, 0))
    acc = lambda a: pl.BlockSpec((None, nq) + a.shape[1:], lambda s, j, r: (s, j, 0, 0))
    par = pl.BlockSpec((1, cb), lambda s, j, r: (0, j))
    par_acc = pl.BlockSpec((None, 1, cb), lambda s, j, r: (s, 0, j))
    per_seq = lambda a: jax.ShapeDtypeStruct((seqs,) + a.shape, F32)
    return pl.pallas_call(
        body, name="s5_bwd", grid=(seqs, S5_CH // cb, nl),
        in_specs=[rows(nq * S5_IN), rows(nq * S5_IN), state, state, chunk(bbr), chunk(bbi), chunk(cr), chunk(ci),
                  par, par],
        out_specs=[rows(nq * S5_IN), acc(bbr), acc(bbi), acc(cr), acc(ci), par_acc, par_acc],
        out_shape=[jax.ShapeDtypeStruct((t, S5_WIDTH), F32), per_seq(bbr), per_seq(bbi), per_seq(cr), per_seq(ci),
                   jax.ShapeDtypeStruct((seqs, 1, S5_CH), F32), jax.ShapeDtypeStruct((seqs, 1, S5_CH), F32)],
        scratch_shapes=[pltpu.VMEM((1, cb), F32), pltpu.VMEM((1, cb), F32), pltpu.VMEM((seg, cb), F32),
                        pltpu.VMEM((seg, cb), F32)] + [pltpu.VMEM((nc, tl, LANES), F32)] * 4
        + [pltpu.VMEM((nc, SCAN_SEGS, LANES), F32)] * 2 + [pltpu.VMEM((nq * S5_IN // LANES, tl, LANES), F32)] * 2,
        compiler_params=_params(("parallel", "parallel", "arbitrary")),
    )(dys, uf, xr, xi, bbr, bbi, cr, ci, ar, ai)


XATT_BLOCK = 2048


def _xatt_probs(qv, kv):
    s = lax.dot_general(qv, kv, _NT, preferred_element_type=F32) * (X_HEAD_DIM ** -0.5)
    e = jnp.exp(s - jnp.max(s, axis=-1, keepdims=True))
    return e / jnp.sum(e, axis=-1, keepdims=True)


def _xatt_fwd(q, k, kv, seqs):
    t = q.shape[0]
    tq = min(XATT_BLOCK, t // seqs)
    nq = t // seqs // tq

    def body(q_ref, k_ref, v_ref, o_ref):
        p = _xatt_probs(q_ref[...], k_ref[...])
        o_ref[...] = jnp.dot(p.astype(BF16), v_ref[...].astype(BF16), preferred_element_type=F32).astype(o_ref.dtype)

    qs = pl.BlockSpec((tq, X_HEAD_DIM), lambda b, h, i: (b * nq + i, h))
    return pl.pallas_call(
        body, name="xatt_fwd", grid=(seqs, N_X_HEADS, nq),
        in_specs=[qs, pl.BlockSpec((N_MEM, X_HEAD_DIM), lambda b, h, i: (b, h)),
                  pl.BlockSpec((N_MEM, X_HEAD_DIM), lambda b, h, i: (b, N_X_HEADS + h))],
        out_specs=qs, out_shape=jax.ShapeDtypeStruct(q.shape, BF16),
        compiler_params=_params(("parallel", "parallel", "parallel")),
    )(q, k, kv)


def _xatt_bwd(q, k, kv, do, seqs):
    t = q.shape[0]
    tq = min(XATT_BLOCK, t // seqs)
    nq = t // seqs // tq
    scale = X_HEAD_DIM ** -0.5

    def body(q_ref, k_ref, v_ref, do_ref, dq_ref, dk_ref, dv_ref):
        @pl.when(pl.program_id(2) == 0)
        def _():
            dk_ref[...] = jnp.zeros(dk_ref.shape, F32)
            dv_ref[...] = jnp.zeros(dv_ref.shape, F32)

        qv, kk = q_ref[...], k_ref[...]
        p = _xatt_probs(qv, kk)
        dob = do_ref[...].astype(BF16)
        dp = lax.dot_general(dob, v_ref[...].astype(BF16), _NT, preferred_element_type=F32)
        ds = p * (dp - jnp.sum(dp * p, axis=-1, keepdims=True))
        dsb = ds.astype(BF16)
        dq_ref[...] = jnp.dot(dsb, kk, preferred_element_type=F32) * scale
        dk_ref[...] += lax.dot_general(dsb, qv, _TN, preferred_element_type=F32) * scale
        dv_ref[...] += lax.dot_general(p.astype(BF16), dob, _TN, preferred_element_type=F32)

    qs = pl.BlockSpec((tq, X_HEAD_DIM), lambda b, h, i: (b * nq + i, h))
    ks = pl.BlockSpec((N_MEM, X_HEAD_DIM), lambda b, h, i: (b, h))
    return pl.pallas_call(
        body, name="xatt_bwd", grid=(seqs, N_X_HEADS, nq),
        in_specs=[qs, ks, pl.BlockSpec((N_MEM, X_HEAD_DIM), lambda b, h, i: (b, N_X_HEADS + h)), qs],
        out_specs=[qs, ks, ks],
        out_shape=[jax.ShapeDtypeStruct(q.shape, F32), jax.ShapeDtypeStruct(k.shape, F32),
                   jax.ShapeDtypeStruct(k.shape, F32)],
        compiler_params=_params(("parallel", "parallel", "arbitrary")),
    )(q, k, kv, do)


CONV_COLS = 256


def _shift_down(x, k, row):
    return jnp.where(row >= k, pltpu.roll(x, k, 0), 0.0)


def _shift_up(x, k, row):
    n = x.shape[0]
    return jnp.where(row < n - k, pltpu.roll(x, n - k, 0), 0.0)


def _down_from(x, prev, k, row):
    return jnp.where(row >= k, pltpu.roll(x, k, 0), pltpu.roll(prev, k, 0))


GATE_ROWS = 512


def _ffn_up_gate(hn, w_up, w, b, seqs):
    t = hn.shape[0]
    l = t // seqs
    nc = D_FF // CONV_COLS

    rc = min(GATE_ROWS, l)

    def body(a_ref, wg_ref, wu_ref, w_ref, b_ref, g_ref, u_ref, p_ref, o_ref):
        wv, bias = w_ref[...], b_ref[...]
        row = lax.broadcasted_iota(jnp.int32, (rc, CONV_COLS), 0)
        prev = jnp.zeros((rc, CONV_COLS), F32)
        for k in range(l // rc):
            rows = slice(k * rc, (k + 1) * rc)
            a = a_ref[rows, :]
            gb = jnp.dot(a, wg_ref[...], preferred_element_type=F32).astype(BF16)
            ub = jnp.dot(a, wu_ref[...], preferred_element_type=F32).astype(BF16)
            g_ref[rows, :] = gb
            u_ref[rows, :] = ub
            g = gb.astype(F32)
            pre = bias + wv[0:1, :] * _down_from(g, prev, 2, row) + wv[1:2, :] * _down_from(g, prev, 1, row) \
                + wv[2:3, :] * g
            p_ref[rows, :] = pre.astype(p_ref.dtype)
            o_ref[rows, :] = (pre * jax.nn.sigmoid(pre) * ub.astype(F32)).astype(o_ref.dtype)
            prev = g

    cols = pl.BlockSpec((l, CONV_COLS), lambda s, j: (s, j))
    half = jax.ShapeDtypeStruct((t, D_FF), BF16)
    return pl.pallas_call(
        body, name="ffn_up_gate", grid=(seqs, nc),
        in_specs=[pl.BlockSpec((l, hn.shape[1]), lambda s, j: (s, 0)),
                  pl.BlockSpec((hn.shape[1], CONV_COLS), lambda s, j: (0, j)),
                  pl.BlockSpec((hn.shape[1], CONV_COLS), lambda s, j: (0, nc + j)),
                  pl.BlockSpec((3, CONV_COLS), lambda s, j: (0, j)), pl.BlockSpec((1, CONV_COLS), lambda s, j: (0, j))],
        out_specs=[cols] * 4, out_shape=[half] * 4,
        compiler_params=_params(("parallel", "parallel")),
    )(hn, w_up, w_up, w, b)


def _ffn_down_dx_gate(dh, w_down, gate, up, pre, w, seqs):
    t = dh.shape[0]
    l = t // seqs
    nc = D_FF // CONV_COLS
    steps = nc * seqs

    def body(dh_ref, wd_ref, g_ref, u_ref, p_ref, w_ref, dgu_ref, dw_ref, db_ref, stage, sems):
        s, j = pl.program_id(0), pl.program_id(1)
        n = s * nc + j
        slot = n % 2

        def copies(slot_, j_, s_):
            rows = pl.ds(pl.multiple_of(s_ * l, 16), l)
            return [pltpu.make_async_copy(
                stage.at[slot_, half],
                dgu_ref.at[rows, pl.ds(pl.multiple_of((half * nc + j_) * CONV_COLS, 128), CONV_COLS)],
                sems.at[slot_, half]) for half in (0, 1)]

        @pl.when(n >= 2)
        def _():
            for cp in copies(slot, j, s):
                cp.wait()

        da = lax.dot_general(dh_ref[...], wd_ref[...], _NT, preferred_element_type=F32)
        g, pre, wv = g_ref[...].astype(F32), p_ref[...].astype(F32), w_ref[...]
        row = lax.broadcasted_iota(jnp.int32, g.shape, 0)
        sg = jax.nn.sigmoid(pre)
        silu = pre * sg
        stage[slot, 1] = (da * silu).astype(stage.dtype)
        dpre = da * u_ref[...].astype(F32) * (sg * (1.0 + pre * (1.0 - sg)))
        dpre1, dpre2 = _shift_up(dpre, 1, row), _shift_up(dpre, 2, row)
        dg = wv[2:3, :] * dpre + wv[1:2, :] * dpre1 + wv[0:1, :] * dpre2
        stage[slot, 0] = dg.astype(stage.dtype)
        for cp in copies(slot, j, s):
            cp.start()
        dw_ref[0:1, :] = jnp.sum(dpre2 * g, axis=0, keepdims=True)
        dw_ref[1:2, :] = jnp.sum(dpre1 * g, axis=0, keepdims=True)
        dw_ref[2:3, :] = jnp.sum(dpre * g, axis=0, keepdims=True)
        db_ref[...] = jnp.sum(dpre, axis=0, keepdims=True)

        @pl.when(n == steps - 1)
        def _():
            for cp in copies(slot, j, s) + (copies(1 - slot, j, s) if steps > 1 else []):
                cp.wait()

    cols = pl.BlockSpec((l, CONV_COLS), lambda s, j: (s, j))
    return pl.pallas_call(
        body, name="ffn_down_dx_gate", grid=(seqs, nc),
        in_specs=[pl.BlockSpec((l, dh.shape[1]), lambda s, j: (s, 0)),
                  pl.BlockSpec((CONV_COLS, dh.shape[1]), lambda s, j: (j, 0)), cols, cols, cols,
                  pl.BlockSpec((3, CONV_COLS), lambda s, j: (0, j))],
        out_specs=[ANY, pl.BlockSpec((None, 3, CONV_COLS), lambda s, j: (s, 0, j)),
                   pl.BlockSpec((None, 1, CONV_COLS), lambda s, j: (s, 0, j))],
        out_shape=[jax.ShapeDtypeStruct((t, 2 * D_FF), BF16), jax.ShapeDtypeStruct((seqs, 3, D_FF), F32),
                   jax.ShapeDtypeStruct((seqs, 1, D_FF), F32)],
        scratch_shapes=[pltpu.VMEM((2, 2, l, CONV_COLS), BF16), pltpu.SemaphoreType.DMA((2, 2))],
        compiler_params=_params(("arbitrary", "arbitrary")),
    )(dh, w_down, gate, up, pre, w)


def _loss_head(h, target):
    t, d = h.shape
    tm = _pick(t, (256, 128, 8))

    def body(h_ref, t_ref, dh_ref, dhb_ref, loss_ref):
        @pl.when(pl.program_id(0) == 0)
        def _():
            loss_ref[...] = jnp.zeros(loss_ref.shape, F32)

        e = h_ref[...] - t_ref[...]
        dh = e * (1.0 / d)
        dh_ref[...] = dh
        dhb_ref[...] = dh.astype(BF16)
        loss_ref[...] += (0.5 / d) * jnp.sum(jnp.sum(e * e, axis=1, keepdims=True), axis=0, keepdims=True)

    blk = pl.BlockSpec((tm, d), lambda i: (i, 0))
    return pl.pallas_call(
        body, name="loss_head", grid=(t // tm,), in_specs=[blk, blk],
        out_specs=[blk, blk, pl.BlockSpec((1, 1), lambda i: (0, 0))],
        out_shape=[jax.ShapeDtypeStruct((t, d), F32), jax.ShapeDtypeStruct((t, d), BF16),
                   jax.ShapeDtypeStruct((1, 1), F32)],
        compiler_params=_params(("arbitrary",)),
    )(h, target)


def _s5_discretise(a_re, a_im, log_dt, b_re, b_im):
    dt = jnp.exp(log_dt)[:, None]
    mag = jnp.exp(a_re * dt)
    lb_r = mag * jnp.cos(a_im * dt)
    lb_i = mag * jnp.sin(a_im * dt)
    den = a_re * a_re + a_im * a_im
    nr = lb_r - 1.0
    coef_r = (nr * a_re + lb_i * a_im) / den
    coef_i = (lb_i * a_re - nr * a_im) / den
    bb_r = coef_r[:, :, None] * b_re - coef_i[:, :, None] * b_im
    bb_i = coef_r[:, :, None] * b_im + coef_i[:, :, None] * b_re
    return lb_r, lb_i, bb_r, bb_i


S5_CHUNKS = 4
S5_PER = S5_GROUPS // S5_CHUNKS


def _blockdiag_in(bb):
    eye = jnp.eye(S5_PER, dtype=bb.dtype)
    return jnp.einsum("jgpc,gh->jgchp", bb.reshape(S5_CHUNKS, S5_PER, S5_STATE, S5_GROUP_CH), eye).reshape(
        S5_CHUNKS, S5_PER * S5_GROUP_CH, S5_PER * S5_STATE)


def _blockdiag_in_grad(d):
    eye = jnp.eye(S5_PER, dtype=d.dtype)
    return jnp.einsum("jgchp,gh->jgpc", d.reshape(S5_CHUNKS, S5_PER, S5_GROUP_CH, S5_PER, S5_STATE), eye).reshape(
        S5_GROUPS, S5_STATE, S5_GROUP_CH)


def _blockdiag_out(c):
    eye = jnp.eye(S5_PER, dtype=c.dtype)
    return jnp.einsum("jgcp,gh->jgphc", c.reshape(S5_CHUNKS, S5_PER, S5_GROUP_CH, S5_STATE), eye).reshape(
        S5_CHUNKS, S5_PER * S5_STATE, S5_PER * S5_GROUP_CH)


def _blockdiag_out_grad(d):
    eye = jnp.eye(S5_PER, dtype=d.dtype)
    return jnp.einsum("jgphc,gh->jgcp", d.reshape(S5_CHUNKS, S5_PER, S5_STATE, S5_PER, S5_GROUP_CH), eye).reshape(
        S5_GROUPS, S5_GROUP_CH, S5_STATE)


def _local_step(x3, mem3, target3, p, wb, late_weights=None, early_grads=None):
    seqs, l, d = x3.shape
    t = seqs * l
    x = x3.reshape(t, d)
    mem = mem3.reshape(seqs * N_MEM, d)
    target = target3.reshape(t, d)
    full = lambda a: (a, a.shape[1], 0, 0)

    s5_in = (p["s5_a_re"], p["s5_a_im"], p["s5_log_dt"], p["s5_b_re"], p["s5_b_im"])
    (lb_r, lb_i, bb_r, bb_i), s5_pull = jax.vjp(_s5_discretise, *s5_in)
    ar, ai = lb_r.reshape(1, S5_CH), lb_i.reshape(1, S5_CH)
    bbr_d, bbi_d = _blockdiag_in(bb_r).astype(BF16), _blockdiag_in(bb_i).astype(BF16)
    cr_d, ci_d = _blockdiag_out(p["s5_c_re"]).astype(BF16), (-_blockdiag_out(p["s5_c_im"])).astype(BF16)
    d_row = p["s5_d"].reshape(1, S5_WIDTH)

    hn1 = _rowwise(_rms, [full(x)], [p["norm_mix"]], [(d, d, 0, BF16)], "norm_mix_fwd")
    if late_weights is not None:
        wb = dict(wb, **late_weights("first", hn1))
    conv_w = wb["ffn_conv_w"] if "ffn_conv_w" in wb else p["ffn_conv_w"]
    w_in = wb["w_in"]
    w_qkv = w_in[:, :3 * FOX_WIDTH]
    w_uf = jnp.concatenate(
        [w_in[:, 3 * FOX_WIDTH + N_FOX_HEADS:], w_in[:, 3 * FOX_WIDTH:3 * FOX_WIDTH + N_FOX_HEADS],
         jnp.zeros((d, UF_COLS - S5_WIDTH - N_FOX_HEADS), w_in.dtype)], axis=1)
    qkv = _mm(hn1, w_qkv, "nn", "in_qkv")
    uf = _mm(hn1, w_uf, "nn", "in_uf")

    bh = seqs * N_FOX_HEADS
    q_pair = (qkv, 128, 0, 1)
    k_pair = (qkv, 128, N_PAIRS, 1)
    gq2, gk2 = jnp.tile(p["fox_q_norm"], (1, 2)), jnp.tile(p["fox_k_norm"], (1, 2))
    pair_out = [(FOX_WIDTH, 128, 1, BF16)]
    qn = _rowwise(_rms_pair, [q_pair], [gq2], pair_out, "fox_qnorm_fwd", heads=N_PAIRS)
    kn = _rowwise(_rms_pair, [k_pair], [gk2], pair_out, "fox_knorm_fwd", heads=N_PAIRS)

    f_rows = uf[:, S5_WIDTH:S5_WIDTH + N_FOX_HEADS].reshape(seqs, l, N_FOX_HEADS).transpose(0, 2, 1).reshape(bh, l)
    f_bias = jnp.tile(p["fox_f_bias"].reshape(N_FOX_HEADS, 1), (seqs, 1))
    c_wide = jnp.broadcast_to(_forget_fwd(f_rows, f_bias)[:, :, None], (bh, l, 128))
    fox, lse = _fox_fwd(qn, kn, qkv, c_wide, seqs)

    xr, xi, ys = _s5_fwd(uf, bbr_d, bbi_d, cr_d, ci_d, ar, ai, seqs)
    u_blk = (uf, S5_WIDTH, 0, 0)
    yg = _rowwise(_s5_act, [full(ys), u_blk], [d_row], [(S5_WIDTH, S5_WIDTH, 0, F32)], "s5_act_fwd")
    if late_weights is not None:
        wb = dict(wb, **late_weights("mid", yg))
    z = _mm(yg, wb["s5_w_glu"], "nn", "s5_glu")
    y2n = _rowwise(_s5_gate, [full(yg), full(z)], [p["s5_b_glu"], p["out_norm_s5"]],
                   [(S5_WIDTH, S5_WIDTH, 0, BF16)], "s5_gate_fwd")
    foxn = _rowwise(_rms, [full(fox)], [p["out_norm_fox"]], [(FOX_WIDTH, FOX_WIDTH, 0, BF16)], "fox_outnorm_fwd")
    mixed = jnp.concatenate([foxn, y2n], axis=1)
    h1 = _mm(mixed, wb["w_out"], "nn", "mix_out", res=x)
    if late_weights is not None:
        wb = dict(wb, **late_weights("late", h1))

    hn2 = _rowwise(_rms, [full(h1)], [p["norm_cross"]], [(d, d, 0, BF16)], "norm_cross_fwd")
    mn = _rowwise(_rms, [full(mem)], [p["norm_mem"]], [(d, d, 0, BF16)], "norm_mem_fwd")
    xq_raw = _mm(hn2, wb["w_xq"], "nn", "x_q")
    kv = _mm(mn, wb["w_xkv"], "nn", "x_kv")
    xh = lambda a: (a, X_HEAD_DIM, 0, 1)
    xqn = _rowwise(_rms, [xh(xq_raw)], [p["xq_norm"]], [(d, X_HEAD_DIM, 1, BF16)], "x_qnorm_fwd", heads=N_X_HEADS)
    xkn = _rowwise(_rms, [xh(kv)], [p["xk_norm"]], [(d, X_HEAD_DIM, 1, BF16)], "x_knorm_fwd", heads=N_X_HEADS)
    xo = _xatt_fwd(xqn, xkn, kv, seqs)
    h2 = _mm(xo, wb["w_xo"], "nn", "x_out", res=h1)

    hn3 = _rowwise(_rms, [full(h2)], [p["norm_ffn"]], [(d, d, 0, BF16)], "norm_ffn_fwd")
    gate, up, pre, act = _ffn_up_gate(hn3, wb["w_ffn_up"], conv_w, p["ffn_conv_b"], seqs)
    h3 = _mm(act, wb["w_ffn_down"], "nn", "ffn_down", res=h2)
    dh3, dh3_b, loss = _loss_head(h3, target)

    g = {}
    late_dt = BF16 if early_grads is not None else F32
    g["w_ffn_down"] = _mm(act, dh3_b, "tn", "ffn_down_dw", out_dtype=late_dt)
    dgu, dconv_w, dconv_b = _ffn_down_dx_gate(dh3_b, wb["w_ffn_down"], gate, up, pre, conv_w, seqs)
    g["ffn_conv_w"], g["ffn_conv_b"] = jnp.sum(dconv_w, axis=0), jnp.sum(dconv_b, axis=0)
    dhn3 = _mm(dgu, wb["w_ffn_up"], "nt", "ffn_up_dx", out_dtype=BF16)
    g["w_ffn_up"] = _mm(hn3, dgu, "tn", "ffn_up_dw", out_dtype=late_dt)
    (dh2,), (g["norm_ffn"],) = _rowwise_vjp(_rms, [full(h2)], [p["norm_ffn"]], [full(dhn3)], "norm_ffn_bwd",
                                            adds=[full(dh3)])

    dxo = _mm(dh2, wb["w_xo"], "nt", "x_out_dx", out_dtype=BF16)
    g["w_xo"] = _mm(xo, dh2, "tn", "x_out_dw", out_dtype=late_dt)
    dxqn, dxkn, dxv = _xatt_bwd(xqn, xkn, kv, dxo, seqs)
    (dxq_raw,), (g["xq_norm"],) = _rowwise_vjp(_rms, [xh(xq_raw)], [p["xq_norm"]], [xh(dxqn)], "x_qnorm_bwd",
                                               heads=N_X_HEADS, row_dtypes=[BF16])
    (dxk_raw,), (g["xk_norm"],) = _rowwise_vjp(_rms, [xh(kv)], [p["xk_norm"]], [xh(dxkn)], "x_knorm_bwd",
                                               heads=N_X_HEADS, row_dtypes=[BF16])
    dkv = jnp.concatenate([dxk_raw, dxv.astype(BF16)], axis=1)
    dhn2 = _mm(dxq_raw, wb["w_xq"], "nt", "x_q_dx", out_dtype=BF16)
    g["w_xq"] = _mm(hn2, dxq_raw, "tn", "x_q_dw", out_dtype=late_dt)
    dmn = _mm(dkv, wb["w_xkv"], "nt", "x_kv_dx")
    g["w_xkv"] = _mm(mn, dkv, "tn", "x_kv_dw", out_dtype=late_dt)
    norm_cross = p["norm_cross"]
    if early_grads is not None:
        norm_cross = norm_cross + early_grads("late", {n: g[n] for n in LATE_WEIGHTS})[0:1, 0:1]
    (dh1,), (g["norm_cross"],) = _rowwise_vjp(_rms, [full(h1)], [norm_cross], [full(dhn2)], "norm_cross_bwd",
                                              adds=[full(dh2)])
    _, (g["norm_mem"],) = _rowwise_vjp(_rms, [full(mem)], [p["norm_mem"]], [full(dmn)], "norm_mem_bwd",
                                       row_dtypes=[BF16])

    dmixed = _mm(dh1, wb["w_out"], "nt", "mix_out_dx", out_dtype=BF16)
    g["w_out"] = _mm(mixed, dh1, "tn", "mix_out_dw", out_dtype=late_dt)
    (dfox,), (g["out_norm_fox"],) = _rowwise_vjp(_rms, [full(fox)], [p["out_norm_fox"]],
                                                 [(dmixed, FOX_WIDTH, 0, 0)], "fox_outnorm_bwd")
    (dyg_a, dz), (g["s5_b_glu"], g["out_norm_s5"]) = _rowwise_vjp(
        _s5_gate, [full(yg), full(z)], [p["s5_b_glu"], p["out_norm_s5"]], [(dmixed, S5_WIDTH, 1, 0)], "s5_gate_bwd",
        row_dtypes=[F32, BF16])
    dyg = _mm(dz, wb["s5_w_glu"], "nt", "s5_glu_dx", res=dyg_a)
    g["s5_w_glu"] = _mm(yg, dz, "tn", "s5_glu_dw", out_dtype=late_dt)
    if early_grads is not None:
        d_row = d_row + early_grads("mid", {n: g[n] for n in MID_WEIGHTS})[0:1, 0:1]
    (dys, du_a), (dd_row,) = _rowwise_vjp(_s5_act, [full(ys), u_blk], [d_row], [full(dyg)], "s5_act_bwd",
                                          row_dtypes=[BF16, F32])
    g["s5_d"] = dd_row
    du_b, dbbr_d, dbbi_d, dcr_d, dci_d, dar, dai = _s5_bwd(dys, uf, xr, xi, bbr_d, bbi_d, cr_d, ci_d, ar, ai, seqs)
    dbbr_d, dbbi_d, dcr_d, dci_d = (jnp.sum(a, axis=0) for a in (dbbr_d, dbbi_d, dcr_d, dci_d))
    d_lb_r = jnp.sum(dar, axis=0).reshape(S5_GROUPS, S5_STATE)
    d_lb_i = jnp.sum(dai, axis=0).reshape(S5_GROUPS, S5_STATE)
    g["s5_a_re"], g["s5_a_im"], g["s5_log_dt"], g["s5_b_re"], g["s5_b_im"] = s5_pull(
        (d_lb_r, d_lb_i, _blockdiag_in_grad(dbbr_d), _blockdiag_in_grad(dbbi_d)))
    g["s5_c_re"] = _blockdiag_out_grad(dcr_d)
    g["s5_c_im"] = -_blockdiag_out_grad(dci_d)

    dqn, dkn, dv, dc, dcq = _fox_bwd(qn, kn, qkv, c_wide, fox, dfox, lse, seqs)
    pair = lambda a: (a, 128, 0, 1)
    (dq_raw,), (dgq2,) = _rowwise_vjp(_rms_pair, [q_pair], [gq2], [pair(dqn)], "fox_qnorm_bwd", heads=N_PAIRS,
                                      row_dtypes=[BF16])
    (dk_raw,), (dgk2,) = _rowwise_vjp(_rms_pair, [k_pair], [gk2], [pair(dkn)], "fox_knorm_bwd", heads=N_PAIRS,
                                      row_dtypes=[BF16])
    g["fox_q_norm"] = dgq2[:, :HEAD_DIM] + dgq2[:, HEAD_DIM:]
    g["fox_k_norm"] = dgk2[:, :HEAD_DIM] + dgk2[:, HEAD_DIM:]
    df_rows, dfb = _forget_bwd(f_rows, f_bias, (dc + dcq).reshape(bh, l))
    g["fox_f_bias"] = jnp.sum(dfb.reshape(seqs, N_FOX_HEADS), axis=0)
    df = df_rows.reshape(seqs, N_FOX_HEADS, l).transpose(0, 2, 1).reshape(t, N_FOX_HEADS)
    dqkv = jnp.concatenate([dq_raw, dk_raw, dv.astype(BF16)], axis=1)
    duf = jnp.concatenate([du_a + du_b, df, jnp.zeros((t, UF_COLS - S5_WIDTH - N_FOX_HEADS), F32)],
                          axis=1).astype(BF16)
    dhn1 = _mm(duf, w_uf, "nt", "in_uf_dx", res=_mm(dqkv, w_qkv, "nt", "in_qkv_dx"), out_dtype=BF16)
    dw_qkv = _mm(hn1, dqkv, "tn", "in_qkv_dw")
    dw_uf = _mm(hn1, duf, "tn", "in_uf_dw")
    g["w_in"] = jnp.concatenate([dw_qkv, dw_uf[:, S5_WIDTH:S5_WIDTH + N_FOX_HEADS], dw_uf[:, :S5_WIDTH]], axis=1)
    (dx,), (g["norm_mix"],) = _rowwise_vjp(_rms, [full(x)], [p["norm_mix"]], [full(dhn1)], "norm_mix_bwd",
                                           adds=[full(dh1)])
    return loss, dx.reshape(seqs, l, d), g


def _place():
    return lax.axis_index("x"), lax.axis_index("y"), lax.axis_index("c")


def _other_chips(x, y):
    return [(1 - x, y), (x, 1 - y), (1 - x, 1 - y)]


ANY = pl.BlockSpec(memory_space=pl.ANY)


HBM = pl.BlockSpec(memory_space=pltpu.HBM)
SEM = pl.BlockSpec(memory_space=pltpu.SEMAPHORE)
DATAFLOW = pltpu.SideEffectType.DATAFLOW_SIDE_EFFECTING


def _in_hbm(a):
    return pltpu.with_memory_space_constraint(a, pltpu.HBM)


def _split_start(name, srcs, lands, n_copies, plan):
    n = len(srcs)

    def body(*refs):
        src_refs, land_refs = refs[:n], refs[n:2 * n]
        send_sems, recv_sems = refs[2 * n], refs[2 * n + 1]
        for i, (src, dst, dev) in enumerate(plan(src_refs, land_refs)):
            pltpu.make_async_remote_copy(src_ref=src, dst_ref=dst, send_sem=send_sems.at[i], recv_sem=recv_sems.at[i],
                                         device_id=dev, device_id_type=MESH).start()
        refs[-1][...] = jnp.zeros((8, 128), F32)

    res = pl.pallas_call(
        body, name=name, in_specs=[HBM] * (2 * n),
        out_specs=[SEM, SEM] + [HBM] * (2 * n) + [pl.BlockSpec(memory_space=pltpu.VMEM)],
        out_shape=[pltpu.SemaphoreType.DMA((n_copies,)), pltpu.SemaphoreType.DMA((n_copies,))]
        + [pltpu.HBM(a.shape, a.dtype) for a in list(srcs) + list(lands)] + [jax.ShapeDtypeStruct((8, 128), F32)],
        input_output_aliases={i: 2 + i for i in range(2 * n)},
        compiler_params=pltpu.CompilerParams(has_side_effects=DATAFLOW),
    )(*[_in_hbm(a) for a in list(srcs) + list(lands)])
    return res[0], res[1], list(res[2:2 + n]), list(res[2 + n:2 + 2 * n]), res[-1]


def _split_wait(name, send_sems, recv_sems, srcs, lands, after, plan):
    n = len(srcs)

    def body(*refs):
        src_refs, land_refs = refs[:n], refs[n:2 * n]
        send_ref, recv_ref = refs[2 * n], refs[2 * n + 1]
        for i, (src, dst, dev) in enumerate(plan(src_refs, land_refs)):
            cp = pltpu.make_async_remote_copy(src_ref=src, dst_ref=dst, send_sem=send_ref.at[i], recv_sem=recv_ref.at[i],
                                              device_id=dev, device_id_type=MESH)
            cp.wait_send()
            cp.wait_recv()

    res = pl.pallas_call(
        body, name=name, in_specs=[HBM] * (2 * n) + [SEM, SEM, ANY], out_specs=[HBM] * (2 * n),
        out_shape=[pltpu.HBM(a.shape, a.dtype) for a in list(srcs) + list(lands)],
        input_output_aliases={i: i for i in range(2 * n)},
        compiler_params=pltpu.CompilerParams(has_side_effects=DATAFLOW),
    )(*srcs, *lands, send_sems, recv_sems, after)
    return list(res[:n]), list(res[n:])


def _first_gather_plan(src_refs, land_refs):
    x, y, c = _place()
    mine = 2 * x + y
    (src, small), (land, small_land) = src_refs, land_refs
    r = src.shape[0]
    hr = r // 2
    half = src.at[pl.ds(pl.multiple_of(c * hr, 16), hr), :]
    half_dst = land.at[pl.ds(pl.multiple_of(mine * r + c * hr, 16), hr), :]
    copies = [(src, land.at[pl.ds(pl.multiple_of(mine * r, 16), r), :], (x, y, 1 - c)),
              (small, small_land.at[mine], (x, y, 1 - c))]
    for px, py in _other_chips(x, y):
        copies += [(half, half_dst, (px, py, c)), (small, small_land.at[mine], (px, py, c))]
    return copies


def _forward_to_sibling(full):
    def body(full_in, full_ref, send_sems, recv_sems):
        x, y, c = _place()
        r = full_ref.shape[0] // 4
        hr = r // 2
        copies = []
        for j, (px, py) in enumerate(_other_chips(x, y)):
            got = full_ref.at[pl.ds(pl.multiple_of((2 * px + py) * r + c * hr, 16), hr), :]
            cp = pltpu.make_async_remote_copy(
                src_ref=got, dst_ref=got, send_sem=send_sems.at[j], recv_sem=recv_sems.at[j],
                device_id=(x, y, 1 - c), device_id_type=MESH)
            cp.start()
            copies.append(cp)
        for cp in copies:
            cp.wait()

    return pl.pallas_call(
        body, name="gather_first_forward", in_specs=[ANY], out_specs=ANY,
        out_shape=jax.ShapeDtypeStruct(full.shape, full.dtype), input_output_aliases={0: 0},
        scratch_shapes=[pltpu.SemaphoreType.DMA((3,)), pltpu.SemaphoreType.DMA((3,))],
        compiler_params=pltpu.CompilerParams(has_side_effects=True),
    )(full)


def _late_gather_plan(col_kind):
    def plan(src_refs, land_refs):
        x, y, c = _place()
        mine = 2 * x + y
        copies = []
        for a, (src, land) in enumerate(zip(src_refs, land_refs)):
            r, cs = src.shape
            if col_kind[a]:
                dst = land.at[:, pl.ds(pl.multiple_of(mine * cs, 128), cs)]
            else:
                dst = land.at[pl.ds(pl.multiple_of(mine * r, 16), r), :]
            copies.append((src, dst, (x, y, 1 - c)))
            copies += [(src, dst, (px, py, c)) for (px, py) in _other_chips(x, y)]
        return copies
    return plan


def _late_reduce_plan(col_kind):
    def plan(src_refs, land_refs):
        x, y, c = _place()
        copies = []
        for a, (src, land) in enumerate(zip(src_refs, land_refs)):
            for j, (px, py) in enumerate(_other_chips(x, y)):
                if col_kind[a] is None:
                    piece = src
                elif col_kind[a]:
                    cs = land.shape[2]
                    piece = src.at[:, pl.ds(pl.multiple_of((2 * px + py) * cs, 128), cs)]
                else:
                    piece = src.at[2 * px + py]
                copies.append((piece, land.at[j], (px, py, c)))
        return copies
    return plan


def _pair_swap(name, halves):
    n = len(halves)

    def body(*refs):
        ins, outs = refs[:n], refs[n:2 * n]
        send_sems, recv_sems = refs[2 * n:]
        x, y, c = _place()
        copies = []
        for a in range(n):
            cp = pltpu.make_async_remote_copy(
                src_ref=ins[a], dst_ref=outs[a], send_sem=send_sems.at[a], recv_sem=recv_sems.at[a],
                device_id=(x, y, 1 - c), device_id_type=MESH)
            cp.start()
            copies.append(cp)
        for cp in copies:
            cp.wait()

    return pl.pallas_call(
        body, name=name, in_specs=[ANY] * n, out_specs=[ANY] * n,
        out_shape=[jax.ShapeDtypeStruct(s.shape, s.dtype) for s in halves],
        scratch_shapes=[pltpu.SemaphoreType.DMA((n,)), pltpu.SemaphoreType.DMA((n,))],
        compiler_params=pltpu.CompilerParams(has_side_effects=True),
    )(*halves)


def _chip_sum(name, chip_sel, own, col, others):
    _, r, c = others.shape
    tr = _pick(r, (256, 128, 64, 32, 16))
    if col:
        own_spec = pl.BlockSpec((tr, c), lambda i, s: (i, s[0]))
    else:
        own_spec = pl.BlockSpec((None, tr, c), lambda i, s: (s[0], i, 0))
    specs = [own_spec] + [pl.BlockSpec((None, tr, c), lambda i, s, k=k: (k, i, 0)) for k in range(3)]

    def body(s_ref, own_ref, r0, r1, r2, o_ref):
        total = ((own_ref[...].astype(F32) + r0[...].astype(F32)) + r1[...].astype(F32)) + r2[...].astype(F32)
        o_ref[...] = total.astype(o_ref.dtype)

    return pl.pallas_call(
        body, name=name,
        grid_spec=pltpu.PrefetchScalarGridSpec(
            num_scalar_prefetch=1, grid=(r // tr,), in_specs=specs,
            out_specs=pl.BlockSpec((tr, c), lambda i, s: (i, 0))),
        out_shape=jax.ShapeDtypeStruct((r, c), BF16),
        compiler_params=_params(("parallel",)),
    )(chip_sel, own, others, others, others)


def _small_layout(vals):
    sizes = [int(math.prod(v.shape)) for v in vals]
    padded = [-(-s // 128) * 128 for s in sizes]
    return sizes, padded, -(-sum(padded) // 1024) * 1024


def _pack_small(vals):
    sizes, padded, total = _small_layout(vals)
    flat = [jnp.pad(v.reshape(-1), (0, p - s)) for v, s, p in zip(vals, sizes, padded)]
    flat.append(jnp.zeros((total - sum(padded),), F32))
    return jnp.concatenate(flat).reshape(total // 128, 128)


def _allreduce_small(own, others, vals):
    def body(own_ref, oth_ref, out_ref, land, send_sem, recv_sem):
        x, y, c = _place()
        out_ref[...] = (own_ref[...] + oth_ref[0]) + (oth_ref[1] + oth_ref[2])
        cp = pltpu.make_async_remote_copy(
            src_ref=out_ref, dst_ref=land, send_sem=send_sem.at[0], recv_sem=recv_sem.at[0],
            device_id=(x, y, 1 - c), device_id_type=MESH)
        cp.start()
        cp.wait()
        out_ref[...] = out_ref[...] + land[...]

    vm = pl.BlockSpec(memory_space=pltpu.VMEM)
    summed = pl.pallas_call(
        body, name="allreduce_small", in_specs=[vm, vm], out_specs=vm,
        out_shape=jax.ShapeDtypeStruct(own.shape, F32),
        scratch_shapes=[pltpu.VMEM(own.shape, F32), pltpu.SemaphoreType.DMA((1,)), pltpu.SemaphoreType.DMA((1,))],
        compiler_params=pltpu.CompilerParams(has_side_effects=True, vmem_limit_bytes=VMEM_LIMIT_BYTES),
    )(own, others).reshape(-1)
    sizes, padded, _ = _small_layout(vals)
    outs, off = [], 0
    for v, s, p in zip(vals, sizes, padded):
        outs.append(summed[off:off + s].reshape(v.shape))
        off += p
    return outs


def _adamw_math(w, g, m, v):
    m2 = ADAM_B1 * m + (1.0 - ADAM_B1) * g
    v2 = ADAM_B2 * v + (1.0 - ADAM_B2) * (g * g)
    m_hat = m2 / (1.0 - ADAM_B1 ** ADAM_STEP)
    v_hat = v2 / (1.0 - ADAM_B2 ** ADAM_STEP)
    delta = -ADAM_LR * (m_hat / (jnp.sqrt(v_hat) + ADAM_EPS) + ADAM_WD * w)
    return delta, m2, v2


def _adamw_big(name, w, g_mine, g_sibling, m, v):
    _, r, c = w.shape

    def body(w_ref, ga_ref, gb_ref, m_ref, v_ref, go_ref, d_ref, mo_ref, vo_ref):
        gv = ga_ref[...].astype(F32) + gb_ref[...].astype(F32)
        d, m2, v2 = _adamw_math(w_ref[...], gv, m_ref[...], v_ref[...])
        go_ref[...] = gv
        d_ref[...] = d
        mo_ref[...] = m2
        vo_ref[...] = v2

    tr = _pick(r, (256, 128, 64, 32, 16, 8))
    if r % tr == 0 and tr % 8 == 0:
        grid = (r // tr,)
        blk = pl.BlockSpec((None, tr, c), lambda i: (0, i, 0))
        part = pl.BlockSpec((tr, c), lambda i: (i, 0))
    else:
        grid = (c // 512,)
        blk = pl.BlockSpec((None, r, 512), lambda i: (0, 0, i))
        part = pl.BlockSpec((r, 512), lambda i: (0, i))
    return pl.pallas_call(
        body, name=name, grid=grid, in_specs=[blk, part, part, blk, blk], out_specs=[blk] * 4,
        out_shape=[jax.ShapeDtypeStruct((1, r, c), F32)] * 4, compiler_params=_params(("parallel",)),
    )(w, g_mine, g_sibling, m, v)


def _adamw_small(ws, gs, ms, vs):
    n = len(ws)

    def body(*refs):
        w_r, g_r, m_r, v_r = refs[:n], refs[n:2 * n], refs[2 * n:3 * n], refs[3 * n:4 * n]
        o = refs[4 * n:]
        for a in range(n):
            gv = g_r[a][...]
            d, m2, v2 = _adamw_math(w_r[a][...], gv, m_r[a][...], v_r[a][...])
            o[a][...] = gv
            o[n + a][...] = d
            o[2 * n + a][...] = m2
            o[3 * n + a][...] = v2

    res = pl.pallas_call(
        body, name="adamw_small", out_shape=[jax.ShapeDtypeStruct(w.shape, F32) for _ in range(4) for w in ws],
        compiler_params=_params(),
    )(*ws, *gs, *ms, *vs)
    return res[:n], res[n:2 * n], res[2 * n:3 * n], res[3 * n:]


def _full_from_gathered(name, gathered):
    if name == "w_in":
        rows = gathered.shape[0] // 4
        return gathered.reshape(4, rows, gathered.shape[1]).transpose(1, 0, 2).reshape(rows, 4 * gathered.shape[1])
    return gathered


def _reduce_layout(name, full):
    if name in COL_KIND:
        return full
    if name == "w_in":
        rows, cols = full.shape
        return full.reshape(rows, 4, cols // 4).transpose(1, 0, 2)
    return full.reshape(4, full.shape[0] // 4, full.shape[1])


def kernel(x, mem, norm_mix, w_in, fox_q_norm, fox_k_norm, fox_f_bias, s5_a_re, s5_a_im, s5_log_dt, s5_b_re, s5_b_im, s5_c_re, s5_c_im, s5_d, s5_w_glu, s5_b_glu, out_norm_fox, out_norm_s5, w_out, norm_cross, norm_mem, w_xq, w_xkv, xq_norm, xk_norm, w_xo, norm_ffn, w_ffn_up, ffn_conv_w, ffn_conv_b, w_ffn_down, loss_target, m_norm_mix, m_w_in, m_fox_q_norm, m_fox_k_norm, m_fox_f_bias, m_s5_a_re, m_s5_a_im, m_s5_log_dt, m_s5_b_re, m_s5_b_im, m_s5_c_re, m_s5_c_im, m_s5_d, m_s5_w_glu, m_s5_b_glu, m_out_norm_fox, m_out_norm_s5, m_w_out, m_norm_cross, m_norm_mem, m_w_xq, m_w_xkv, m_xq_norm, m_xk_norm, m_w_xo, m_norm_ffn, m_w_ffn_up, m_ffn_conv_w, m_ffn_conv_b, m_w_ffn_down, v_norm_mix, v_w_in, v_fox_q_norm, v_fox_k_norm, v_fox_f_bias, v_s5_a_re, v_s5_a_im, v_s5_log_dt, v_s5_b_re, v_s5_b_im, v_s5_c_re, v_s5_c_im, v_s5_d, v_s5_w_glu, v_s5_b_glu, v_out_norm_fox, v_out_norm_s5, v_w_out, v_norm_cross, v_norm_mem, v_w_xq, v_w_xkv, v_xq_norm, v_xk_norm, v_w_xo, v_norm_ffn, v_w_ffn_up, v_ffn_conv_w, v_ffn_conv_b, v_w_ffn_down):
    given = dict(locals())
    w = {n: given[n] for n in WEIGHTS}
    m = {n: given["m_" + n] for n in WEIGHTS}
    v = {n: given["v_" + n] for n in WEIGHTS}
    xi, yi, _ = _place()
    chip = (2 * xi + yi).astype(jnp.int32)
    chip_sel = chip.reshape(1)

    first_shard, taps = w[FIRST_WEIGHT][0].astype(BF16), w["ffn_conv_w"][0]
    send, recv, srcs, lands, g_started = _split_start(
        "gather_first_start", [first_shard, taps],
        [lax.empty((4 * first_shard.shape[0], first_shard.shape[1]), BF16), lax.empty((4,) + taps.shape, F32)],
        8, _first_gather_plan)
    pending = {"first": ((FIRST_WEIGHT, "ffn_conv_w"), _first_gather_plan, send, recv, srcs, lands)}
    for stage, names in (("mid", MID_WEIGHTS), ("late", LATE_WEIGHTS)):
        kinds = [n in COL_KIND for n in names]
        shards = [w[n][0].astype(BF16) for n in names]
        shards[0] = shards[0] + g_started[0:1, 0:1].astype(BF16)
        full = [lax.empty((s.shape[0], 4 * s.shape[1]) if ck else (4 * s.shape[0], s.shape[1]), BF16)
                for s, ck in zip(shards, kinds)]
        plan = _late_gather_plan(kinds)
        send, recv, srcs, lands, g_started = _split_start(
            "gather_" + stage + "_start", shards, full, 4 * len(names), plan)
        pending[stage] = (names, plan, send, recv, srcs, lands)

    def late_weights(stage, after):
        names, plan, send, recv, srcs, lands = pending[stage]
        _, full = _split_wait("gather_" + stage + "_wait", send, recv, srcs, lands, after, plan)
        if stage == "first":
            full = [_full_from_gathered(FIRST_WEIGHT, _forward_to_sibling(full[0])),
                    full[1].transpose(1, 0, 2).reshape(3, D_FF)]
        return dict(zip(names, full))

    reducing = {}

    def start_reduce(stage, grads_by_name, whole=()):
        names = list(grads_by_name)
        kinds = [n in COL_KIND for n in names]
        grads = [_reduce_layout(n, grads_by_name[n].astype(BF16)) for n in names]
        lands = [lax.empty((3, s.shape[0], s.shape[1] // 4) if ck else (3,) + s.shape[1:], BF16)
                 for s, ck in zip(grads, kinds)]
        plan = _late_reduce_plan(kinds + [None] * len(whole))
        send, recv, srcs, lands, started = _split_start(
            "reduce_" + stage + "_start", grads + list(whole),
            lands + [lax.empty((3,) + a.shape, a.dtype) for a in whole], 3 * (len(names) + len(whole)), plan)
        reducing[stage] = (names, kinds, plan, send, recv, srcs, lands)
        return started

    p = {n: w[n][0] for n in SMALL}
    for n in ("norm_mix", "fox_q_norm", "fox_k_norm", "fox_f_bias", "s5_b_glu", "out_norm_fox", "out_norm_s5",
              "norm_cross", "norm_mem", "xq_norm", "xk_norm", "norm_ffn", "ffn_conv_b"):
        p[n] = p[n].reshape(1, -1)
    p["norm_mix"] = p["norm_mix"] + g_started[0:1, 0:1]
    loss, grad_x, g = _local_step(x, mem, loss_target, p, {}, late_weights, start_reduce)

    small_names = list(SMALL) + ["ffn_conv_w"]
    small_vals = [g[n].reshape(w[n].shape if n != "ffn_conv_w" else (1, 3, D_FF)) for n in small_names] + [loss]
    after = start_reduce("first", {FIRST_WEIGHT: g[FIRST_WEIGHT]}, whole=[_pack_small(small_vals)])

    out_g, out_d, out_m, out_v = {}, {}, {}, {}

    def finish(stage, after):
        names, kinds, plan, send, recv, srcs, lands = reducing[stage]
        sums, from_chips = _split_wait("reduce_" + stage + "_wait", send, recv, srcs, lands, after, plan)
        mine = [_chip_sum("reduce_chip_sum_" + n, chip_sel, ps, ck, fc)
                for n, ps, fc, ck in zip(names, sums, from_chips, kinds)]
        theirs = _pair_swap("reduce_pair_swap_" + stage, mine)
        for n, a, b in zip(names, mine, theirs):
            if n == "w_in":
                flip = lambda t: jnp.swapaxes(t, -1, -2)
                res = _adamw_big("adamw_" + n, flip(w[n]), flip(a), flip(b), flip(m[n]), flip(v[n]))
                out_g[n], out_d[n], out_m[n], out_v[n] = (flip(t) for t in res)
                continue
            out_g[n], out_d[n], out_m[n], out_v[n] = _adamw_big("adamw_" + n, w[n], a, b, m[n], v[n])
        return sums[len(names):], from_chips[len(names):], out_v[names[-1]]

    _, _, after = finish("late", after)
    _, _, after = finish("mid", after)
    (small_own,), (small_others,), _ = finish("first", after)

    reduced = _allreduce_small(small_own, small_others, small_vals)
    loss_all = reduced[-1].reshape(())
    conv_w_grad = lax.dynamic_slice_in_dim(reduced[-2], chip * (D_FF // 4), D_FF // 4, axis=2)
    sg, sd, sm, sv = _adamw_small(
        [w[n] for n in small_names], list(reduced[:len(SMALL)]) + [conv_w_grad],
        [m[n] for n in small_names], [v[n] for n in small_names])
    out_g.update(zip(small_names, sg))
    out_d.update(zip(small_names, sd))
    out_m.update(zip(small_names, sm))
    out_v.update(zip(small_names, sv))

    return (loss_all, grad_x, *[out_g[n] for n in WEIGHTS], *[out_d[n] for n in WEIGHTS],
            *[out_m[n] for n in WEIGHTS], *[out_v[n] for n in WEIGHTS])
```
